```python
import jax, jax.numpy as jnp
from jax import lax
import numpy as np

D_MODEL = 1024
BATCH = 8
SEQ = 2048
DEPTH = 1

CHUNK = 64
N_MEM = 256
HEAD_DIM = 64
D_MIX = D_MODEL
FOX_HEADS = D_MIX // 2 // HEAD_DIM
CHK_HEADS = D_MIX // 2 // HEAD_DIM
D_FOX = FOX_HEADS * HEAD_DIM
D_CHK = CHK_HEADS * HEAD_DIM
LEFT_CHUNKS = 8
BAND = (LEFT_CHUNKS + 1) * CHUNK
MAX_REL = 128
N_REL = 2 * MAX_REL + 1
Q_BLOCK = 128
MEM_HEADS = 4
MEM_HEAD_DIM = D_MODEL // MEM_HEADS
D_FF = 4 * D_MODEL
EPS = 1e-6
D_IN = 3 * D_FOX + FOX_HEADS + 3 * D_CHK

kernel_name = 'hybrid_fox_chunkrel_memxattn_block'


def rmsnorm(x, g):
    xf = x.astype(jnp.float32)
    y = xf * lax.rsqrt(jnp.mean(xf * xf, axis=-1, keepdims=True) + EPS) * g.astype(jnp.float32)
    return y.astype(x.dtype)


def forgetting_attention(q, k, v, f_logit):
    S = q.shape[1]
    Dh = q.shape[-1]
    scale = Dh ** -0.5
    logf = jax.nn.log_sigmoid(f_logit.astype(jnp.float32))
    c = jnp.cumsum(logf, axis=1).transpose(0, 2, 1)
    pos = jnp.arange(S)
    outs = []
    for i in range(S // Q_BLOCK):
        q0, q1 = i * Q_BLOCK, (i + 1) * Q_BLOCK
        qb = q[:, q0:q1]
        kb = k[:, :q1]
        vb = v[:, :q1]
        logits = jnp.einsum('bqhd,bkhd->bhqk', qb, kb).astype(jnp.float32) * scale
        logits = logits + c[:, :, q0:q1, None] - c[:, :, None, :q1]
        causal = pos[q0:q1, None] >= pos[None, :q1]
        logits = jnp.where(causal[None, None], logits, -jnp.inf)
        p = jax.nn.softmax(logits, axis=-1).astype(v.dtype)
        outs.append(jnp.einsum('bhqk,bkhd->bqhd', p, vb))
    return jnp.concatenate(outs, axis=1)


def _rel_index():
    i = np.arange(CHUNK)[:, None]
    m = np.arange(BAND)[None, :]
    rel = i + LEFT_CHUNKS * CHUNK - m
    return np.clip(rel, -MAX_REL, MAX_REL) + MAX_REL


def chunked_relpos_attention(q, k, v, rel_table):
    B, S, H, Dh = q.shape
    NC = S // CHUNK
    scale = Dh ** -0.5
    qc = q.reshape(B, NC, CHUNK, H, Dh)
    pad = ((0, 0), (LEFT_CHUNKS * CHUNK, 0), (0, 0), (0, 0))
    kp = jnp.pad(k, pad).reshape(B, NC + LEFT_CHUNKS, CHUNK, H, Dh)
    vp = jnp.pad(v, pad).reshape(B, NC + LEFT_CHUNKS, CHUNK, H, Dh)
    kband = jnp.concatenate([kp[:, j:j + NC] for j in range(LEFT_CHUNKS + 1)], axis=2)
    vband = jnp.concatenate([vp[:, j:j + NC] for j in range(LEFT_CHUNKS + 1)], axis=2)
    bias = rel_table.astype(jnp.float32)[:, _rel_index()]
    key_pos = jnp.arange(NC)[:, None] * CHUNK + jnp.arange(BAND)[None, :] - LEFT_CHUNKS * CHUNK
    valid = key_pos >= 0
    logits = jnp.einsum('bcqhd,bckhd->bhcqk', qc, kband).astype(jnp.float32) * scale
    logits = logits + bias[None, :, None]
    logits = jnp.where(valid[None, None, :, None, :], logits, -jnp.inf)
    p = jax.nn.softmax(logits, axis=-1).astype(v.dtype)
    out = jnp.einsum('bhcqk,bckhd->bcqhd', p, vband)
    return out.reshape(B, S, H, Dh)


def memory_cross_attention(h, mem_n, w_mq, w_mk, w_mv, w_mo):
    B, S, _ = h.shape
    M = mem_n.shape[1]
    q = (h @ w_mq).reshape(B, S, MEM_HEADS, MEM_HEAD_DIM)
    k = (mem_n @ w_mk).reshape(B, M, MEM_HEADS, MEM_HEAD_DIM)
    v = (mem_n @ w_mv).reshape(B, M, MEM_HEADS, MEM_HEAD_DIM)
    logits = jnp.einsum('bshd,bmhd->bhsm', q, k).astype(jnp.float32) * (MEM_HEAD_DIM ** -0.5)
    p = jax.nn.softmax(logits, axis=-1).astype(v.dtype)
    o = jnp.einsum('bhsm,bmhd->bshd', p, v).reshape(B, S, D_MODEL)
    return o @ w_mo


def _fwd_setup_inputs(seed: int = 0) -> dict:
    key = jax.random.key(seed)
    ks = jax.random.split(key, 24)
    f32 = jnp.float32

    def w(k, shape, fan_in):
        return jax.random.normal(k, shape, f32) * fan_in ** -0.5

    def gain(k, n):
        return 1.0 + 0.05 * jax.random.normal(k, (DEPTH, n), f32)

    return {
        'x': jax.random.normal(ks[0], (BATCH, SEQ, D_MODEL), f32),
        'mem': jax.random.normal(ks[1], (BATCH, N_MEM, D_MODEL), f32),
        'w_in': w(ks[2], (DEPTH, D_MODEL, D_IN), D_MODEL),
        'b_fgt': 3.0 + 0.1 * jax.random.normal(ks[3], (DEPTH, FOX_HEADS), f32),
        'rel_bias': 0.2 * jax.random.normal(ks[4], (DEPTH, CHK_HEADS, N_REL), f32),
        'g_fox_out': gain(ks[5], D_FOX),
        'g_chk_out': gain(ks[6], D_CHK),
        'w_out': w(ks[7], (DEPTH, D_MIX, D_MODEL), D_MIX),
        'g_mix_pre': gain(ks[8], D_MODEL),
        'g_mix_post': gain(ks[9], D_MODEL),
        'g_mem_kv': gain(ks[10], D_MODEL),
        'w_mq': w(ks[11], (DEPTH, D_MODEL, D_MODEL), D_MODEL),
        'w_mk': w(ks[12], (DEPTH, D_MODEL, D_MODEL), D_MODEL),
        'w_mv': w(ks[13], (DEPTH, D_MODEL, D_MODEL), D_MODEL),
        'w_mo': w(ks[14], (DEPTH, D_MODEL, D_MODEL), D_MODEL),
        'g_mem_pre': gain(ks[15], D_MODEL),
        'g_mem_post': gain(ks[16], D_MODEL),
        'w_ff1': w(ks[17], (DEPTH, D_MODEL, D_FF), D_MODEL),
        'w_ff2': w(ks[18], (DEPTH, D_FF, D_MODEL), D_FF),
        'g_ff_pre': gain(ks[19], D_MODEL),
        'g_ff_post': gain(ks[20], D_MODEL),
    }


def _fwd_reference(x, mem, w_in, b_fgt, rel_bias, g_fox_out, g_chk_out, w_out, g_mix_pre, g_mix_post,
              g_mem_kv, w_mq, w_mk, w_mv, w_mo, g_mem_pre, g_mem_post,
              w_ff1, w_ff2, g_ff_pre, g_ff_post):
    B, S, _ = x.shape
    for l in range(DEPTH):
        h = rmsnorm(x, g_mix_pre[l])
        proj = h @ w_in[l]
        o0 = 0
        fq = proj[..., o0:o0 + D_FOX]; o0 += D_FOX
        fk = proj[..., o0:o0 + D_FOX]; o0 += D_FOX
        fv = proj[..., o0:o0 + D_FOX]; o0 += D_FOX
        f_logit = proj[..., o0:o0 + FOX_HEADS] + b_fgt[l]; o0 += FOX_HEADS
        cq = proj[..., o0:o0 + D_CHK]; o0 += D_CHK
        ck = proj[..., o0:o0 + D_CHK]; o0 += D_CHK
        cv = proj[..., o0:o0 + D_CHK]
        shp_f = (B, S, FOX_HEADS, HEAD_DIM)
        shp_c = (B, S, CHK_HEADS, HEAD_DIM)
        y_fox = forgetting_attention(fq.reshape(shp_f), fk.reshape(shp_f), fv.reshape(shp_f), f_logit)
        y_chk = chunked_relpos_attention(cq.reshape(shp_c), ck.reshape(shp_c), cv.reshape(shp_c), rel_bias[l])
        y = jnp.concatenate([rmsnorm(y_fox.reshape(B, S, D_FOX), g_fox_out[l]),
                             rmsnorm(y_chk.reshape(B, S, D_CHK), g_chk_out[l])], axis=-1)
        x = x + rmsnorm(y @ w_out[l], g_mix_post[l])
        h = rmsnorm(x, g_mem_pre[l])
        mem_n = rmsnorm(mem, g_mem_kv[l])
        y = memory_cross_attention(h, mem_n, w_mq[l], w_mk[l], w_mv[l], w_mo[l])
        x = x + rmsnorm(y, g_mem_post[l])
        h = rmsnorm(x, g_ff_pre[l])
        y = jnp.square(jax.nn.relu(h @ w_ff1[l])) @ w_ff2[l]
        x = x + rmsnorm(y, g_ff_post[l])
    return x


import jax as _jax
import jax.numpy as _jnp

TWIN_FORMAT = 'train_step'
FWD_PARAMS = ['x', 'mem', 'w_in', 'b_fgt', 'rel_bias', 'g_fox_out', 'g_chk_out', 'w_out', 'g_mix_pre', 'g_mix_post', 'g_mem_kv', 'w_mq', 'w_mk', 'w_mv', 'w_mo', 'g_mem_pre', 'g_mem_post', 'w_ff1', 'w_ff2', 'g_ff_pre', 'g_ff_post']
TWIN_WEIGHTS = ['w_in', 'b_fgt', 'rel_bias', 'g_fox_out', 'g_chk_out', 'w_out', 'g_mix_pre', 'g_mix_post', 'g_mem_kv', 'w_mq', 'w_mk', 'w_mv', 'w_mo', 'g_mem_pre', 'g_mem_post', 'w_ff1', 'w_ff2', 'g_ff_pre', 'g_ff_post']
TWIN_DIFF_INPUT = 'x'
TWIN_INPUTS = ['x', 'mem', 'w_in', 'b_fgt', 'rel_bias', 'g_fox_out', 'g_chk_out', 'w_out', 'g_mix_pre', 'g_mix_post', 'g_mem_kv', 'w_mq', 'w_mk', 'w_mv', 'w_mo', 'g_mem_pre', 'g_mem_post', 'w_ff1', 'w_ff2', 'g_ff_pre', 'g_ff_post', 'loss_target', 'm_w_in', 'm_b_fgt', 'm_rel_bias', 'm_g_fox_out', 'm_g_chk_out', 'm_w_out', 'm_g_mix_pre', 'm_g_mix_post', 'm_g_mem_kv', 'm_w_mq', 'm_w_mk', 'm_w_mv', 'm_w_mo', 'm_g_mem_pre', 'm_g_mem_post', 'm_w_ff1', 'm_w_ff2', 'm_g_ff_pre', 'm_g_ff_post', 'v_w_in', 'v_b_fgt', 'v_rel_bias', 'v_g_fox_out', 'v_g_chk_out', 'v_w_out', 'v_g_mix_pre', 'v_g_mix_post', 'v_g_mem_kv', 'v_w_mq', 'v_w_mk', 'v_w_mv', 'v_w_mo', 'v_g_mem_pre', 'v_g_mem_post', 'v_w_ff1', 'v_w_ff2', 'v_g_ff_pre', 'v_g_ff_post']
TWIN_OUTPUTS = ['loss', 'grad_x', 'grad_w_in', 'grad_b_fgt', 'grad_rel_bias', 'grad_g_fox_out', 'grad_g_chk_out', 'grad_w_out', 'grad_g_mix_pre', 'grad_g_mix_post', 'grad_g_mem_kv', 'grad_w_mq', 'grad_w_mk', 'grad_w_mv', 'grad_w_mo', 'grad_g_mem_pre', 'grad_g_mem_post', 'grad_w_ff1', 'grad_w_ff2', 'grad_g_ff_pre', 'grad_g_ff_post', 'delta_w_in', 'delta_b_fgt', 'delta_rel_bias', 'delta_g_fox_out', 'delta_g_chk_out', 'delta_w_out', 'delta_g_mix_pre', 'delta_g_mix_post', 'delta_g_mem_kv', 'delta_w_mq', 'delta_w_mk', 'delta_w_mv', 'delta_w_mo', 'delta_g_mem_pre', 'delta_g_mem_post', 'delta_w_ff1', 'delta_w_ff2', 'delta_g_ff_pre', 'delta_g_ff_post', 'new_m_w_in', 'new_m_b_fgt', 'new_m_rel_bias', 'new_m_g_fox_out', 'new_m_g_chk_out', 'new_m_w_out', 'new_m_g_mix_pre', 'new_m_g_mix_post', 'new_m_g_mem_kv', 'new_m_w_mq', 'new_m_w_mk', 'new_m_w_mv', 'new_m_w_mo', 'new_m_g_mem_pre', 'new_m_g_mem_post', 'new_m_w_ff1', 'new_m_w_ff2', 'new_m_g_ff_pre', 'new_m_g_ff_post', 'new_v_w_in', 'new_v_b_fgt', 'new_v_rel_bias', 'new_v_g_fox_out', 'new_v_g_chk_out', 'new_v_w_out', 'new_v_g_mix_pre', 'new_v_g_mix_post', 'new_v_g_mem_kv', 'new_v_w_mq', 'new_v_w_mk', 'new_v_w_mv', 'new_v_w_mo', 'new_v_g_mem_pre', 'new_v_g_mem_post', 'new_v_w_ff1', 'new_v_w_ff2', 'new_v_g_ff_pre', 'new_v_g_ff_post']
TWIN_LEAF_KINDS = {'loss': 'loss', 'grad_x': 'grad_x', 'grad_w_in': 'grad_w', 'grad_b_fgt': 'grad_w', 'grad_rel_bias': 'grad_w', 'grad_g_fox_out': 'grad_w', 'grad_g_chk_out': 'grad_w', 'grad_w_out': 'grad_w', 'grad_g_mix_pre': 'grad_w', 'grad_g_mix_post': 'grad_w', 'grad_g_mem_kv': 'grad_w', 'grad_w_mq': 'grad_w', 'grad_w_mk': 'grad_w', 'grad_w_mv': 'grad_w', 'grad_w_mo': 'grad_w', 'grad_g_mem_pre': 'grad_w', 'grad_g_mem_post': 'grad_w', 'grad_w_ff1': 'grad_w', 'grad_w_ff2': 'grad_w', 'grad_g_ff_pre': 'grad_w', 'grad_g_ff_post': 'grad_w', 'delta_w_in': 'delta_w', 'delta_b_fgt': 'delta_w', 'delta_rel_bias': 'delta_w', 'delta_g_fox_out': 'delta_w', 'delta_g_chk_out': 'delta_w', 'delta_w_out': 'delta_w', 'delta_g_mix_pre': 'delta_w', 'delta_g_mix_post': 'delta_w', 'delta_g_mem_kv': 'delta_w', 'delta_w_mq': 'delta_w', 'delta_w_mk': 'delta_w', 'delta_w_mv': 'delta_w', 'delta_w_mo': 'delta_w', 'delta_g_mem_pre': 'delta_w', 'delta_g_mem_post': 'delta_w', 'delta_w_ff1': 'delta_w', 'delta_w_ff2': 'delta_w', 'delta_g_ff_pre': 'delta_w', 'delta_g_ff_post': 'delta_w', 'new_m_w_in': 'new_m', 'new_m_b_fgt': 'new_m', 'new_m_rel_bias': 'new_m', 'new_m_g_fox_out': 'new_m', 'new_m_g_chk_out': 'new_m', 'new_m_w_out': 'new_m', 'new_m_g_mix_pre': 'new_m', 'new_m_g_mix_post': 'new_m', 'new_m_g_mem_kv': 'new_m', 'new_m_w_mq': 'new_m', 'new_m_w_mk': 'new_m', 'new_m_w_mv': 'new_m', 'new_m_w_mo': 'new_m', 'new_m_g_mem_pre': 'new_m', 'new_m_g_mem_post': 'new_m', 'new_m_w_ff1': 'new_m', 'new_m_w_ff2': 'new_m', 'new_m_g_ff_pre': 'new_m', 'new_m_g_ff_post': 'new_m', 'new_v_w_in': 'new_v', 'new_v_b_fgt': 'new_v', 'new_v_rel_bias': 'new_v', 'new_v_g_fox_out': 'new_v', 'new_v_g_chk_out': 'new_v', 'new_v_w_out': 'new_v', 'new_v_g_mix_pre': 'new_v', 'new_v_g_mix_post': 'new_v', 'new_v_g_mem_kv': 'new_v', 'new_v_w_mq': 'new_v', 'new_v_w_mk': 'new_v', 'new_v_w_mv': 'new_v', 'new_v_w_mo': 'new_v', 'new_v_g_mem_pre': 'new_v', 'new_v_g_mem_post': 'new_v', 'new_v_w_ff1': 'new_v', 'new_v_w_ff2': 'new_v', 'new_v_g_ff_pre': 'new_v', 'new_v_g_ff_post': 'new_v'}


def _forward(args):
    return _fwd_reference(*[args[k] for k in FWD_PARAMS])


def _output_shape():
    out = _jax.eval_shape(lambda: _forward(_fwd_setup_inputs(0)))
    return out.shape, out.dtype

N_MICROBATCH = 1
ADAM_LR = 0.001
ADAM_B1 = 0.9
ADAM_B2 = 0.999
ADAM_EPS = 1e-08
ADAM_WD = 0.01
ADAM_STEP = 10
PER_EXAMPLE_BATCH_AXIS = {'x': 0, 'mem': 0, 'loss_target': 0}
SHARED_INPUTS = []
_WEIGHT_DTYPES = {'w_in': _jnp.float32, 'b_fgt': _jnp.float32, 'rel_bias': _jnp.float32, 'g_fox_out': _jnp.float32, 'g_chk_out': _jnp.float32, 'w_out': _jnp.float32, 'g_mix_pre': _jnp.float32, 'g_mix_post': _jnp.float32, 'g_mem_kv': _jnp.float32, 'w_mq': _jnp.float32, 'w_mk': _jnp.float32, 'w_mv': _jnp.float32, 'w_mo': _jnp.float32, 'g_mem_pre': _jnp.float32, 'g_mem_post': _jnp.float32, 'w_ff1': _jnp.float32, 'w_ff2': _jnp.float32, 'g_ff_pre': _jnp.float32, 'g_ff_post': _jnp.float32}
MOMENT_SCALE = {'w_in': 8.894401e-01, 'b_fgt': 2.503900e+00, 'rel_bias': 3.212639e-01, 'g_fox_out': 7.015692e-01, 'g_chk_out': 2.146716e+00, 'w_out': 1.534494e+00, 'g_mix_pre': 1.506681e+00, 'g_mix_post': 1.591844e+01, 'g_mem_kv': 2.795155e+00, 'w_mq': 6.599846e-01, 'w_mk': 6.905991e-01, 'w_mv': 2.733929e+00, 'w_mo': 2.659897e+00, 'g_mem_pre': 6.572801e-01, 'g_mem_post': 1.673536e+01, 'w_ff1': 7.072265e-01, 'w_ff2': 2.282878e+00, 'g_ff_pre': 1.268360e+00, 'g_ff_post': 1.685674e+01}


def _to_microbatches(a, axis):
    t = _jnp.moveaxis(a, axis, 0)
    t = t.reshape((N_MICROBATCH, t.shape[0] // N_MICROBATCH) + t.shape[1:])
    return _jnp.moveaxis(t, 1, axis + 1)


def setup_inputs(seed: int = 0) -> dict:
    inp = _fwd_setup_inputs(seed)
    key = _jax.random.fold_in(_jax.random.key(seed), 7919)
    shape, _ = _output_shape()
    out = dict(inp)
    out["loss_target"] = _jax.random.normal(_jax.random.fold_in(key, 0), shape, _jnp.float32)
    for i, name in enumerate(TWIN_WEIGHTS):
        w = inp[name].astype(_jnp.float32)
        if MOMENT_SCALE is None:
            s = _jnp.sqrt(_jnp.mean(_jnp.square(w)) + 1e-30)
        else:
            s = MOMENT_SCALE[name]
        km, kv = _jax.random.split(_jax.random.fold_in(key, i + 1))
        out[name] = w
        out["m_" + name] = s * _jax.random.normal(km, w.shape, _jnp.float32)
        out["v_" + name] = (s * s) * _jax.random.uniform(kv, w.shape, _jnp.float32, 0.5, 1.5)
    if N_MICROBATCH > 1:
        for name, axis in PER_EXAMPLE_BATCH_AXIS.items():
            out[name] = _to_microbatches(out[name], axis)
    return {'x': out['x'], 'mem': out['mem'], 'w_in': out['w_in'], 'b_fgt': out['b_fgt'], 'rel_bias': out['rel_bias'], 'g_fox_out': out['g_fox_out'], 'g_chk_out': out['g_chk_out'], 'w_out': out['w_out'], 'g_mix_pre': out['g_mix_pre'], 'g_mix_post': out['g_mix_post'], 'g_mem_kv': out['g_mem_kv'], 'w_mq': out['w_mq'], 'w_mk': out['w_mk'], 'w_mv': out['w_mv'], 'w_mo': out['w_mo'], 'g_mem_pre': out['g_mem_pre'], 'g_mem_post': out['g_mem_post'], 'w_ff1': out['w_ff1'], 'w_ff2': out['w_ff2'], 'g_ff_pre': out['g_ff_pre'], 'g_ff_post': out['g_ff_post'], 'loss_target': out['loss_target'], 'm_w_in': out['m_w_in'], 'm_b_fgt': out['m_b_fgt'], 'm_rel_bias': out['m_rel_bias'], 'm_g_fox_out': out['m_g_fox_out'], 'm_g_chk_out': out['m_g_chk_out'], 'm_w_out': out['m_w_out'], 'm_g_mix_pre': out['m_g_mix_pre'], 'm_g_mix_post': out['m_g_mix_post'], 'm_g_mem_kv': out['m_g_mem_kv'], 'm_w_mq': out['m_w_mq'], 'm_w_mk': out['m_w_mk'], 'm_w_mv': out['m_w_mv'], 'm_w_mo': out['m_w_mo'], 'm_g_mem_pre': out['m_g_mem_pre'], 'm_g_mem_post': out['m_g_mem_post'], 'm_w_ff1': out['m_w_ff1'], 'm_w_ff2': out['m_w_ff2'], 'm_g_ff_pre': out['m_g_ff_pre'], 'm_g_ff_post': out['m_g_ff_post'], 'v_w_in': out['v_w_in'], 'v_b_fgt': out['v_b_fgt'], 'v_rel_bias': out['v_rel_bias'], 'v_g_fox_out': out['v_g_fox_out'], 'v_g_chk_out': out['v_g_chk_out'], 'v_w_out': out['v_w_out'], 'v_g_mix_pre': out['v_g_mix_pre'], 'v_g_mix_post': out['v_g_mix_post'], 'v_g_mem_kv': out['v_g_mem_kv'], 'v_w_mq': out['v_w_mq'], 'v_w_mk': out['v_w_mk'], 'v_w_mv': out['v_w_mv'], 'v_w_mo': out['v_w_mo'], 'v_g_mem_pre': out['v_g_mem_pre'], 'v_g_mem_post': out['v_g_mem_post'], 'v_w_ff1': out['v_w_ff1'], 'v_w_ff2': out['v_w_ff2'], 'v_g_ff_pre': out['v_g_ff_pre'], 'v_g_ff_post': out['v_g_ff_post']}


def _loss(weights, diff, rest, loss_target):
    with _jax.named_scope("forward"):
        args = {**rest, TWIN_DIFF_INPUT: diff, **{k: w.astype(_WEIGHT_DTYPES[k]) for k, w in weights.items()}}
        y = _forward(args)
    with _jax.named_scope("loss_head"):
        err = _jnp.square(y.astype(_jnp.float32) - loss_target)
        return 0.5 * _jnp.sum(_jnp.mean(err, axis=-1)) if err.ndim else 0.5 * err


def _adamw(w, g, m, v):
    m = ADAM_B1 * m + (1.0 - ADAM_B1) * g
    v = ADAM_B2 * v + (1.0 - ADAM_B2) * _jnp.square(g)
    m_hat = m / (1.0 - ADAM_B1 ** ADAM_STEP)
    v_hat = v / (1.0 - ADAM_B2 ** ADAM_STEP)
    delta = -ADAM_LR * (m_hat / (_jnp.sqrt(v_hat) + ADAM_EPS) + ADAM_WD * w)
    return delta, m, v


def reference(x, mem, w_in, b_fgt, rel_bias, g_fox_out, g_chk_out, w_out, g_mix_pre, g_mix_post, g_mem_kv, w_mq, w_mk, w_mv, w_mo, g_mem_pre, g_mem_post, w_ff1, w_ff2, g_ff_pre, g_ff_post, loss_target, m_w_in, m_b_fgt, m_rel_bias, m_g_fox_out, m_g_chk_out, m_w_out, m_g_mix_pre, m_g_mix_post, m_g_mem_kv, m_w_mq, m_w_mk, m_w_mv, m_w_mo, m_g_mem_pre, m_g_mem_post, m_w_ff1, m_w_ff2, m_g_ff_pre, m_g_ff_post, v_w_in, v_b_fgt, v_rel_bias, v_g_fox_out, v_g_chk_out, v_w_out, v_g_mix_pre, v_g_mix_post, v_g_mem_kv, v_w_mq, v_w_mk, v_w_mv, v_w_mo, v_g_mem_pre, v_g_mem_post, v_w_ff1, v_w_ff2, v_g_ff_pre, v_g_ff_post):
    given = dict(x=x, mem=mem, w_in=w_in, b_fgt=b_fgt, rel_bias=rel_bias, g_fox_out=g_fox_out, g_chk_out=g_chk_out, w_out=w_out, g_mix_pre=g_mix_pre, g_mix_post=g_mix_post, g_mem_kv=g_mem_kv, w_mq=w_mq, w_mk=w_mk, w_mv=w_mv, w_mo=w_mo, g_mem_pre=g_mem_pre, g_mem_post=g_mem_post, w_ff1=w_ff1, w_ff2=w_ff2, g_ff_pre=g_ff_pre, g_ff_post=g_ff_post, loss_target=loss_target, m_w_in=m_w_in, m_b_fgt=m_b_fgt, m_rel_bias=m_rel_bias, m_g_fox_out=m_g_fox_out, m_g_chk_out=m_g_chk_out, m_w_out=m_w_out, m_g_mix_pre=m_g_mix_pre, m_g_mix_post=m_g_mix_post, m_g_mem_kv=m_g_mem_kv, m_w_mq=m_w_mq, m_w_mk=m_w_mk, m_w_mv=m_w_mv, m_w_mo=m_w_mo, m_g_mem_pre=m_g_mem_pre, m_g_mem_post=m_g_mem_post, m_w_ff1=m_w_ff1, m_w_ff2=m_w_ff2, m_g_ff_pre=m_g_ff_pre, m_g_ff_post=m_g_ff_post, v_w_in=v_w_in, v_b_fgt=v_b_fgt, v_rel_bias=v_rel_bias, v_g_fox_out=v_g_fox_out, v_g_chk_out=v_g_chk_out, v_w_out=v_w_out, v_g_mix_pre=v_g_mix_pre, v_g_mix_post=v_g_mix_post, v_g_mem_kv=v_g_mem_kv, v_w_mq=v_w_mq, v_w_mk=v_w_mk, v_w_mv=v_w_mv, v_w_mo=v_w_mo, v_g_mem_pre=v_g_mem_pre, v_g_mem_post=v_g_mem_post, v_w_ff1=v_w_ff1, v_w_ff2=v_w_ff2, v_g_ff_pre=v_g_ff_pre, v_g_ff_post=v_g_ff_post)
    weights = {n: given[n] for n in TWIN_WEIGHTS}
    shared = {n: given[n] for n in SHARED_INPUTS}
    per_example = {n: given[n] for n in ['x', 'mem']}
    grad_fn = _jax.value_and_grad(_loss, argnums=(0, 1))

    def one_microbatch(ex, loss_target):
        ex = dict(ex)
        diff = ex.pop(TWIN_DIFF_INPUT)
        return grad_fn(weights, diff, {**shared, **ex}, loss_target)

    if N_MICROBATCH == 1:
        loss, (grad_w, grad_x) = one_microbatch(per_example, given["loss_target"])
    else:
        def body(carry, xs):
            loss_sum, grad_sum = carry
            l_k, (gw_k, gx_k) = one_microbatch(xs[0], xs[1])
            with _jax.named_scope("update"):
                return (loss_sum + l_k, _jax.tree.map(_jnp.add, grad_sum, gw_k)), gx_k

        init = (_jnp.zeros((), _jnp.float32), _jax.tree.map(_jnp.zeros_like, weights))
        (loss, grad_w), grad_x = _jax.lax.scan(body, init, (per_example, given["loss_target"]))
    with _jax.named_scope("update"):
        delta_w, new_m, new_v = {}, {}, {}
        for n in TWIN_WEIGHTS:
            delta_w[n], new_m[n], new_v[n] = _adamw(weights[n], grad_w[n], given["m_" + n], given["v_" + n])
    return (loss, grad_x, *[grad_w[n] for n in TWIN_WEIGHTS], *[delta_w[n] for n in TWIN_WEIGHTS],
            *[new_m[n] for n in TWIN_WEIGHTS], *[new_v[n] for n in TWIN_WEIGHTS])
```

```python
import functools

import jax
import jax.numpy as jnp
from jax import lax
from jax.experimental import pallas as pl
from jax.experimental.pallas import tpu as pltpu

F32 = jnp.float32
BF16 = jnp.bfloat16

D = 1024
HEAD = 64
N_PAIR = 4
D_GRP = 512
CHUNK = 64
LEFT = 8
MAX_REL = 128
N_REL = 2 * MAX_REL + 1
N_MEM = 256
MEM_HEADS = 4
MEM_HD = 256
D_FF = 4096
D_IN = 3080
D_ALL = 3200
EPS = 1e-6
TQ = 256
WIN = (LEFT + TQ // CHUNK) * CHUNK
PADK = LEFT * CHUNK
ROLL_W = 1024
NEG = -1e30
N_CHIP = 4
VMEM_LIMIT = 48 * 1024 * 1024

ADAM_LR = 0.001
ADAM_B1 = 0.9
ADAM_B2 = 0.999
ADAM_EPS = 1e-08
ADAM_WD = 0.01
ADAM_STEP = 10

MESH = pl.DeviceIdType.MESH


def _cparams():
    return pltpu.CompilerParams(vmem_limit_bytes=VMEM_LIMIT)


def _dot(a, b):
    return jnp.dot(a, b, preferred_element_type=F32)


def _dot_nt(a, b):
    return lax.dot_general(a, b, (((1,), (1,)), ((), ())), preferred_element_type=F32)


def _dot_tn(a, b):
    return lax.dot_general(a, b, (((0,), (0,)), ((), ())), preferred_element_type=F32)


def _split3(x):
    hi = x.astype(BF16)
    r1 = x - hi.astype(F32)
    mid = r1.astype(BF16)
    lo = (r1 - mid.astype(F32)).astype(BF16)
    return hi, mid, lo


def _dot3(x, m01):
    hi, mid, lo = _split3(x)
    return _dot(hi, m01) + _dot(mid, m01) + _dot(lo, m01)


def _dot3_l(m01, x):
    hi, mid, lo = _split3(x)
    return _dot(m01, hi) + _dot(m01, mid) + _dot(m01, lo)


def _mm_nn(a, b, kind, *, name, out_dtype=BF16, tm=512, tn=512, epi=None):
    M, K = a.shape
    if kind == "plain":
        N = b.shape[1]
        b_spec = pl.BlockSpec((K, tn), lambda m, n: (0, n))
    elif kind == "rows":
        N = b.shape[2]
        b_spec = pl.BlockSpec((N_CHIP, K // N_CHIP, tn), lambda m, n: (0, 0, n))
    else:
        nq = b.shape[2]
        N = N_CHIP * nq
        per = nq // tn
        b_spec = pl.BlockSpec((None, K, tn), lambda m, n: (n // per, 0, n % per))
    tm = min(tm, M)
    kq = K // N_CHIP

    def body(a_ref, b_ref, *o_refs):
        if kind == "rows":
            acc = _dot(a_ref[:, 0:kq], b_ref[0])
            for j in range(1, N_CHIP):
                acc += _dot(a_ref[:, j * kq:(j + 1) * kq], b_ref[j])
        else:
            acc = _dot(a_ref[...], b_ref[...])
        if epi == "relu2":
            r = jnp.maximum(acc, 0.0)
            o_refs[0][...] = (r * r).astype(BF16)
            o_refs[1][...] = r.astype(BF16)
        else:
            o_refs[0][...] = acc.astype(out_dtype)

    o_spec = pl.BlockSpec((tm, tn), lambda m, n: (m, n))
    if epi == "relu2":
        out_shape = (jax.ShapeDtypeStruct((M, N), BF16), jax.ShapeDtypeStruct((M, N), BF16))
        out_specs = (o_spec, o_spec)
    else:
        out_shape = jax.ShapeDtypeStruct((M, N), out_dtype)
        out_specs = o_spec
    return pl.pallas_call(
        body, grid=(M // tm, N // tn),
        in_specs=[pl.BlockSpec((tm, K), lambda m, n: (m, 0)), b_spec],
        out_specs=out_specs, out_shape=out_shape, name=name, compiler_params=_cparams(),
    )(a, b)


def _mm_nt(a, b, kind, *, name, out_dtype=BF16, tm=512, tn=512, mul2r=None):
    M, K = a.shape
    if kind == "plain":
        N = b.shape[0]
        b_spec = pl.BlockSpec((tn, K), lambda m, n: (n, 0))
    elif kind == "rows":
        nq = b.shape[1]
        N = N_CHIP * nq
        tn = min(tn, nq)
        per = nq // tn
        b_spec = pl.BlockSpec((None, tn, K), lambda m, n: (n // per, n % per, 0))
    else:
        N = b.shape[1]
        b_spec = pl.BlockSpec((N_CHIP, tn, K // N_CHIP), lambda m, n: (0, n, 0))
    tm = min(tm, M)
    kq = K // N_CHIP

    def body(a_ref, b_ref, *rest):
        o_ref = rest[-1]
        if kind == "cols":
            acc = _dot_nt(a_ref[:, 0:kq], b_ref[0])
            for j in range(1, N_CHIP):
                acc += _dot_nt(a_ref[:, j * kq:(j + 1) * kq], b_ref[j])
        else:
            acc = _dot_nt(a_ref[...], b_ref[...])
        if mul2r is not None:
            acc = acc * (2.0 * rest[0][...].astype(F32))
        o_ref[...] = acc.astype(out_dtype)

    in_specs = [pl.BlockSpec((tm, K), lambda m, n: (m, 0)), b_spec]
    args = [a, b]
    if mul2r is not None:
        in_specs.append(pl.BlockSpec((tm, tn), lambda m, n: (m, n)))
        args.append(mul2r)
    return pl.pallas_call(
        body, grid=(M // tm, N // tn), in_specs=in_specs,
        out_specs=pl.BlockSpec((tm, tn), lambda m, n: (m, n)),
        out_shape=jax.ShapeDtypeStruct((M, N), out_dtype), name=name, compiler_params=_cparams(),
    )(*args)


def _mm_tn(a, b, *, name, out_dtype=BF16, tk=512, tn=512, cols4=False):
    M, K1 = a.shape
    N = b.shape[1]
    tk = min(tk, K1)
    tn = min(tn, N)

    def body(a_ref, b_ref, o_ref):
        o_ref[...] = _dot_tn(a_ref[...], b_ref[...]).astype(out_dtype)

    if cols4:
        per = (N // N_CHIP) // tn
        out_shape = jax.ShapeDtypeStruct((N_CHIP, K1, N // N_CHIP), out_dtype)
        o_spec = pl.BlockSpec((None, tk, tn), lambda k, n: (n // per, k, n % per))
    else:
        out_shape = jax.ShapeDtypeStruct((K1, N), out_dtype)
        o_spec = pl.BlockSpec((tk, tn), lambda k, n: (k, n))
    return pl.pallas_call(
        body, grid=(K1 // tk, N // tn),
        in_specs=[pl.BlockSpec((M, tk), lambda k, n: (0, k)), pl.BlockSpec((M, tn), lambda k, n: (0, n))],
        out_specs=o_spec, out_shape=out_shape, name=name, compiler_params=_cparams(),
    )(a, b)


def _rms(x, g):
    r = lax.rsqrt(jnp.mean(x * x, axis=-1, keepdims=True) + EPS)
    return x * r * g


def _rms_bwd(x, g, dy):
    r = lax.rsqrt(jnp.mean(x * x, axis=-1, keepdims=True) + EPS)
    xh = x * r
    dg = jnp.sum(dy * xh, axis=0, keepdims=True)
    dxh = dy * g
    dx = r * (dxh - xh * jnp.mean(dxh * xh, axis=-1, keepdims=True))
    return dx, dg


def _row_spec(tm, n):
    return pl.BlockSpec((tm, n), lambda i: (i, 0))


def _vec_spec(n):
    return pl.BlockSpec((1, n), lambda i: (0, 0))


def _acc_spec(n):
    return pl.BlockSpec((8, n), lambda i: (0, 0))


def _acc_add(ref, row, i):
    @pl.when(i == 0)
    def _():
        ref[...] = jnp.zeros_like(ref)
    ref[0:1, :] += row


def _rms_fwd_call(x, g, *, name, tm=256):
    M, n = x.shape
    tm = min(tm, M)

    def body(x_ref, g_ref, h_ref):
        h_ref[...] = _rms(x_ref[...], g_ref[...]).astype(BF16)

    return pl.pallas_call(
        body, grid=(M // tm,), in_specs=[_row_spec(tm, n), _vec_spec(n)], out_specs=_row_spec(tm, n),
        out_shape=jax.ShapeDtypeStruct((M, n), BF16), name=name, compiler_params=_cparams(),
    )(x, g)


def _post_pre_call(xres, z, g_post, g_pre, *, name, tm=256):
    M, n = xres.shape

    def body(x_ref, z_ref, gp_ref, gn_ref, xo_ref, h_ref):
        xn = x_ref[...] + _rms(z_ref[...], gp_ref[...])
        xo_ref[...] = xn
        h_ref[...] = _rms(xn, gn_ref[...]).astype(BF16)

    return pl.pallas_call(
        body, grid=(M // tm,),
        in_specs=[_row_spec(tm, n), _row_spec(tm, n), _vec_spec(n), _vec_spec(n)],
        out_specs=(_row_spec(tm, n), _row_spec(tm, n)),
        out_shape=(jax.ShapeDtypeStruct((M, n), F32), jax.ShapeDtypeStruct((M, n), BF16)),
        name=name, compiler_params=_cparams(),
    )(xres, z, g_post, g_pre)


def _final_call(x2, y3, g_post, target, *, name, tm=256):
    M, n = x2.shape

    def body(x_ref, y_ref, g_ref, t_ref, loss_ref, dx_ref, dy_ref, dg_ref):
        i = pl.program_id(0)
        y = y_ref[...]
        g = g_ref[...]
        diff = x_ref[...] + _rms(y, g) - t_ref[...]
        part = 0.5 * jnp.sum(jnp.sum(diff * diff, axis=1, keepdims=True), axis=0, keepdims=True) / n

        @pl.when(i == 0)
        def _():
            loss_ref[...] = jnp.zeros_like(loss_ref)
        loss_ref[...] += jnp.broadcast_to(part, loss_ref.shape)
        dx = diff / n
        dx_ref[...] = dx
        dy, dg = _rms_bwd(y, g, dx)
        dy_ref[...] = dy.astype(BF16)
        _acc_add(dg_ref, dg, i)

    return pl.pallas_call(
        body, grid=(M // tm,),
        in_specs=[_row_spec(tm, n), _row_spec(tm, n), _vec_spec(n), _row_spec(tm, n)],
        out_specs=(pl.BlockSpec((8, 128), lambda i: (0, 0)), _row_spec(tm, n), _row_spec(tm, n), _acc_spec(n)),
        out_shape=(jax.ShapeDtypeStruct((8, 128), F32), jax.ShapeDtypeStruct((M, n), F32),
                   jax.ShapeDtypeStruct((M, n), BF16), jax.ShapeDtypeStruct((8, n), F32)),
        name=name, compiler_params=_cparams(),
    )(x2, y3, g_post, target)


def _bwd_mid_call(dx_in, x, dh, g_pre, y, g_post, *, name, tm=256):
    M, n = x.shape

    def body(dxi_ref, x_ref, dh_ref, gpre_ref, y_ref, gpost_ref, dx_ref, dy_ref, dgpre_ref, dgpost_ref):
        i = pl.program_id(0)
        d1, dg1 = _rms_bwd(x_ref[...], gpre_ref[...], dh_ref[...])
        dx = dxi_ref[...] + d1
        dx_ref[...] = dx
        dy, dg2 = _rms_bwd(y_ref[...], gpost_ref[...], dx)
        dy_ref[...] = dy.astype(BF16)
        _acc_add(dgpre_ref, dg1, i)
        _acc_add(dgpost_ref, dg2, i)

    return pl.pallas_call(
        body, grid=(M // tm,),
        in_specs=[_row_spec(tm, n), _row_spec(tm, n), _row_spec(tm, n), _vec_spec(n), _row_spec(tm, n), _vec_spec(n)],
        out_specs=(_row_spec(tm, n), _row_spec(tm, n), _acc_spec(n), _acc_spec(n)),
        out_shape=(jax.ShapeDtypeStruct((M, n), F32), jax.ShapeDtypeStruct((M, n), BF16),
                   jax.ShapeDtypeStruct((8, n), F32), jax.ShapeDtypeStruct((8, n), F32)),
        name=name, compiler_params=_cparams(),
    )(dx_in, x, dh, g_pre, y, g_post)


def _bwd_last_call(dx_in, x, dh, g_pre, *, name, tm=256):
    M, n = x.shape

    def body(dxi_ref, x_ref, dh_ref, g_ref, dx_ref, dg_ref):
        i = pl.program_id(0)
        d1, dg1 = _rms_bwd(x_ref[...], g_ref[...], dh_ref[...])
        dx_ref[...] = dxi_ref[...] + d1
        _acc_add(dg_ref, dg1, i)

    return pl.pallas_call(
        body, grid=(M // tm,),
        in_specs=[_row_spec(tm, n), _row_spec(tm, n), _row_spec(tm, n), _vec_spec(n)],
        out_specs=(_row_spec(tm, n), _acc_spec(n)),
        out_shape=(jax.ShapeDtypeStruct((M, n), F32), jax.ShapeDtypeStruct((8, n), F32)),
        name=name, compiler_params=_cparams(),
    )(dx_in, x, dh, g_pre)


def _gain_grad_call(x, g, dy_a, dy_b, *, name):
    M, n = x.shape

    def body(x_ref, g_ref, a_ref, b_ref, dg_ref):
        _, dg = _rms_bwd(x_ref[...], g_ref[...], a_ref[...] + b_ref[...])
        dg_ref[...] = jnp.zeros_like(dg_ref)
        dg_ref[0:1, :] = dg

    return pl.pallas_call(
        body, grid=(1,),
        in_specs=[_row_spec(M, n), _vec_spec(n), _row_spec(M, n), _row_spec(M, n)],
        out_specs=_acc_spec(n), out_shape=jax.ShapeDtypeStruct((8, n), F32),
        name=name, compiler_params=_cparams(),
    )(x, g, dy_a, dy_b)


def _head_group_matrix():
    a = lax.broadcasted_iota(jnp.int32, (D_GRP, D_GRP), 0) // HEAD
    b = lax.broadcasted_iota(jnp.int32, (D_GRP, D_GRP), 1) // HEAD
    return jnp.where(a == b, 1.0, 0.0).astype(BF16)


def _mix_norm_fwd_call(yf, yc, gf, gc, *, name, tm=256):
    M = yf.shape[0]

    def body(yf_ref, yc_ref, gf_ref, gc_ref, o_ref):
        o_ref[:, 0:D_GRP] = _rms(yf_ref[...], gf_ref[...]).astype(BF16)
        o_ref[:, D_GRP:D] = _rms(yc_ref[...], gc_ref[...]).astype(BF16)

    return pl.pallas_call(
        body, grid=(M // tm,),
        in_specs=[_row_spec(tm, D_GRP), _row_spec(tm, D_GRP), _vec_spec(D_GRP), _vec_spec(D_GRP)],
        out_specs=_row_spec(tm, D), out_shape=jax.ShapeDtypeStruct((M, D), BF16),
        name=name, compiler_params=_cparams(),
    )(yf, yc, gf, gc)


def _mix_norm_bwd_call(dyn, yf, yc, gf, gc, *, name, tm=256):
    M = yf.shape[0]

    def body(dyn_ref, yf_ref, yc_ref, gf_ref, gc_ref, dof_ref, doc_ref, delta_ref, dgf_ref, dgc_ref):
        i = pl.program_id(0)
        yf_ = yf_ref[...]
        dof, dgf = _rms_bwd(yf_, gf_ref[...], dyn_ref[:, 0:D_GRP])
        doc, dgc = _rms_bwd(yc_ref[...], gc_ref[...], dyn_ref[:, D_GRP:D])
        dof_b = dof.astype(BF16)
        dof_ref[...] = dof_b
        doc_ref[...] = doc.astype(BF16)
        prod = dof_b.astype(F32) * yf_
        hi = prod.astype(BF16)
        lo = (prod - hi.astype(F32)).astype(BF16)
        grp = _head_group_matrix()
        delta_ref[...] = _dot(hi, grp) + _dot(lo, grp)
        _acc_add(dgf_ref, dgf, i)
        _acc_add(dgc_ref, dgc, i)

    return pl.pallas_call(
        body, grid=(M // tm,),
        in_specs=[_row_spec(tm, D), _row_spec(tm, D_GRP), _row_spec(tm, D_GRP), _vec_spec(D_GRP), _vec_spec(D_GRP)],
        out_specs=(_row_spec(tm, D_GRP), _row_spec(tm, D_GRP), _row_spec(tm, D_GRP), _acc_spec(D_GRP), _acc_spec(D_GRP)),
        out_shape=(jax.ShapeDtypeStruct((M, D_GRP), BF16), jax.ShapeDtypeStruct((M, D_GRP), BF16),
                   jax.ShapeDtypeStruct((M, D_GRP), F32), jax.ShapeDtypeStruct((8, D_GRP), F32),
                   jax.ShapeDtypeStruct((8, D_GRP), F32)),
        name=name, compiler_params=_cparams(),
    )(dyn, yf, yc, gf, gc)


def _tri(n, lower_incl):
    a = lax.broadcasted_iota(jnp.int32, (n, n), 0)
    b = lax.broadcasted_iota(jnp.int32, (n, n), 1)
    return jnp.where(a >= b, 1.0, 0.0).astype(BF16) if lower_incl else jnp.where(a <= b, 1.0, 0.0).astype(BF16)


def _fox_prep_call(fl_raw, b_pad, *, name):
    S = fl_raw.shape[0]
    nb = S // TQ

    def body(fl_ref, b_ref, crep_ref, ct_ref, carry_ref):
        i = pl.program_id(0)

        @pl.when(i == 0)
        def _():
            carry_ref[...] = jnp.zeros_like(carry_ref)
        logf = jax.nn.log_sigmoid(fl_ref[...] + b_ref[...])
        cb = _dot3_l(_tri(TQ, True), logf) + carry_ref[0:1, :]
        carry_ref[0:1, :] = cb[TQ - 1:TQ, :]
        a = lax.broadcasted_iota(jnp.int32, (128, D_GRP), 0)
        b = lax.broadcasted_iota(jnp.int32, (128, D_GRP), 1) // HEAD
        expand = jnp.where(a == b, 1.0, 0.0).astype(BF16)
        crep = _dot3(cb, expand)
        crep_ref[...] = crep
        ct_ref[...] = crep.T

    return pl.pallas_call(
        body, grid=(nb,),
        in_specs=[_row_spec(TQ, 128), _vec_spec(128)],
        out_specs=(_row_spec(TQ, D_GRP), pl.BlockSpec((None, D_GRP, TQ), lambda i: (i, 0, 0))),
        out_shape=(jax.ShapeDtypeStruct((S, D_GRP), F32), jax.ShapeDtypeStruct((nb, D_GRP, TQ), F32)),
        scratch_shapes=[pltpu.VMEM((8, 128), F32)],
        name=name, compiler_params=_cparams(),
    )(fl_raw, b_pad)


def _lane_masks():
    lane = lax.broadcasted_iota(jnp.int32, (1, 128), 1)
    return lane < HEAD, lane >= HEAD


def _fox_fwd_call(proj, c_rep, c_t, *, name):
    S = proj.shape[0]
    nq = S // TQ
    scale = HEAD ** -0.5

    def body(q_ref, k_ref, v_ref, c_ref, ct_ref, o_ref, lse_ref):
        i = pl.program_id(1)
        m_lo, m_hi = _lane_masks()
        masks = (m_lo, m_hi)
        q = q_ref[...]
        qm = [jnp.where(mk, q, jnp.zeros_like(q)) for mk in masks]
        cq = c_ref[...]
        cqh = [cq[:, 0:1], cq[:, HEAD:HEAD + 1]]
        row = lax.broadcasted_iota(jnp.int32, (TQ, TQ), 0)
        col = lax.broadcasted_iota(jnp.int32, (TQ, TQ), 1)

        def step(j, carry, masked):
            ms, ls, acc = carry
            start = pl.multiple_of(j * TQ, TQ)
            k = k_ref[pl.ds(start, TQ), :]
            v = v_ref[pl.ds(start, TQ), :]
            ct = ct_ref[j]
            new_m, new_l, pv, alpha_l = [], [], [], []
            for h in range(2):
                s = _dot_nt(qm[h], k) * scale + (cqh[h] - ct[HEAD * h:HEAD * h + 1, :])
                if masked:
                    s = jnp.where(row >= col, s, NEG)
                mn = jnp.maximum(ms[h], jnp.max(s, axis=1, keepdims=True))
                alpha = jnp.exp(ms[h] - mn)
                p = jnp.exp(s - mn)
                new_l.append(alpha * ls[h] + jnp.sum(p, axis=1, keepdims=True))
                new_m.append(mn)
                alpha_l.append(alpha)
                pv.append(_dot(p.astype(BF16), jnp.where(masks[h], v, jnp.zeros_like(v))))
            alpha_lane = jnp.where(m_lo, alpha_l[0], alpha_l[1])
            acc = acc * alpha_lane + pv[0] + pv[1]
            return (tuple(new_m), tuple(new_l), acc)

        init = ((jnp.full((TQ, 1), NEG, F32),) * 2, (jnp.zeros((TQ, 1), F32),) * 2, jnp.zeros((TQ, 128), F32))
        carry = lax.fori_loop(0, i, lambda j, c: step(j, c, False), init)
        ms, ls, acc = step(i, carry, True)
        l_lane = jnp.where(m_lo, ls[0], ls[1])
        o_ref[...] = acc / l_lane
        lse_ref[...] = jnp.where(m_lo, ms[0] + jnp.log(ls[0]), ms[1] + jnp.log(ls[1]))

    return pl.pallas_call(
        body, grid=(N_PAIR, nq),
        in_specs=[pl.BlockSpec((TQ, 128), lambda p, i: (i, p)),
                  pl.BlockSpec((S, 128), lambda p, i: (0, N_PAIR + p)),
                  pl.BlockSpec((S, 128), lambda p, i: (0, 2 * N_PAIR + p)),
                  pl.BlockSpec((TQ, 128), lambda p, i: (i, p)),
                  pl.BlockSpec((nq, 128, TQ), lambda p, i: (0, p, 0))],
        out_specs=(pl.BlockSpec((TQ, 128), lambda p, i: (i, p)), pl.BlockSpec((TQ, 128), lambda p, i: (i, p))),
        out_shape=(jax.ShapeDtypeStruct((S, D_GRP), F32), jax.ShapeDtypeStruct((S, D_GRP), F32)),
        name=name, compiler_params=_cparams(),
    )(proj, proj, proj, c_rep, c_t)


def _fox_bwd_call(proj, do, lse_rep, delta_rep, c_rep, c_t, *, name):
    S = proj.shape[0]
    nq = S // TQ
    scale = HEAD ** -0.5

    def body(q_ref, k_ref, v_ref, do_ref, lse_ref, dl_ref, c_ref, ct_ref, dq_ref, dk_ref, dv_ref, dct_ref, dcq_ref, dqa_ref):
        j = pl.program_id(1)
        m_lo, m_hi = _lane_masks()
        masks = (m_lo, m_hi)

        @pl.when(j == 0)
        def _():
            dqa_ref[...] = jnp.zeros_like(dqa_ref)
            dcq_ref[...] = jnp.zeros_like(dcq_ref)
        k = k_ref[...]
        v = v_ref[...]
        km = [jnp.where(mk, k, jnp.zeros_like(k)) for mk in masks]
        ct = ct_ref[...]
        row = lax.broadcasted_iota(jnp.int32, (TQ, TQ), 0)
        col = lax.broadcasted_iota(jnp.int32, (TQ, TQ), 1)

        def step(i, carry, masked):
            dk, dv, dcs = carry
            start = pl.multiple_of(i * TQ, TQ)
            q = q_ref[pl.ds(start, TQ), :]
            do = do_ref[pl.ds(start, TQ), :]
            lse = lse_ref[pl.ds(start, TQ), :]
            dl = dl_ref[pl.ds(start, TQ), :]
            cq = c_ref[pl.ds(start, TQ), :]
            dq = jnp.zeros((TQ, 128), F32)
            new_dcs = []
            rows = []
            for h in range(2):
                lo = HEAD * h
                qm = jnp.where(masks[h], q, jnp.zeros_like(q))
                dom = jnp.where(masks[h], do, jnp.zeros_like(do))
                s = _dot_nt(qm, k) * scale + (cq[:, lo:lo + 1] - ct[lo:lo + 1, :])
                p = jnp.exp(s - lse[:, lo:lo + 1])
                if masked:
                    p = jnp.where(row >= col, p, 0.0)
                dp = _dot_nt(dom, v)
                ds = p * (dp - dl[:, lo:lo + 1])
                new_dcs.append(dcs[h] + jnp.sum(ds, axis=0, keepdims=True))
                rows.append(jnp.sum(ds, axis=1, keepdims=True))
                dsb = (ds * scale).astype(BF16)
                dv = dv + _dot_tn(p.astype(BF16), dom)
                dk = dk + _dot_tn(dsb, qm)
                dq = dq + _dot(dsb, km[h])
            dqa_ref[pl.ds(start, TQ), :] += dq
            dcq_ref[pl.ds(start, TQ), :] += jnp.where(m_lo, rows[0], rows[1])
            return (dk, dv, tuple(new_dcs))

        init = (jnp.zeros((TQ, 128), F32), jnp.zeros((TQ, 128), F32), (jnp.zeros((1, TQ), F32),) * 2)
        carry = step(j, init, True)
        dk, dv, dcs = lax.fori_loop(j + 1, nq, lambda i, c: step(i, c, False), carry)
        dk_ref[...] = dk.astype(BF16)
        dv_ref[...] = dv.astype(BF16)
        dct_ref[...] = jnp.zeros_like(dct_ref)
        dct_ref[0:1, :] = -dcs[0]
        dct_ref[1:2, :] = -dcs[1]

        @pl.when(j == nq - 1)
        def _():
            dq_ref[...] = dqa_ref[...].astype(BF16)

    res = lambda p, j: (0, p)
    return pl.pallas_call(
        body, grid=(N_PAIR, nq),
        in_specs=[pl.BlockSpec((S, 128), res),
                  pl.BlockSpec((TQ, 128), lambda p, j: (j, N_PAIR + p)),
                  pl.BlockSpec((TQ, 128), lambda p, j: (j, 2 * N_PAIR + p)),
                  pl.BlockSpec((S, 128), res), pl.BlockSpec((S, 128), res), pl.BlockSpec((S, 128), res),
                  pl.BlockSpec((S, 128), res),
                  pl.BlockSpec((None, 128, TQ), lambda p, j: (j, p, 0))],
        out_specs=(pl.BlockSpec((S, 128), res),
                   pl.BlockSpec((TQ, 128), lambda p, j: (j, p)), pl.BlockSpec((TQ, 128), lambda p, j: (j, p)),
                   pl.BlockSpec((None, None, 8, TQ), lambda p, j: (p, j, 0, 0)),
                   pl.BlockSpec((S, 128), res)),
        out_shape=(jax.ShapeDtypeStruct((S, D_GRP), BF16), jax.ShapeDtypeStruct((S, D_GRP), BF16),
                   jax.ShapeDtypeStruct((S, D_GRP), BF16), jax.ShapeDtypeStruct((N_PAIR, nq, 8, TQ), F32),
                   jax.ShapeDtypeStruct((S, D_GRP), F32)),
        scratch_shapes=[pltpu.VMEM((S, 128), F32)],
        name=name, compiler_params=_cparams(),
    )(proj, proj, proj, do, lse_rep, delta_rep, c_rep, c_t)


def _fox_gate_bwd_call(dc_rows, fl_raw, b_pad, *, name):
    S = fl_raw.shape[0]
    nb = S // TQ

    def body(dc_ref, fl_ref, b_ref, dfl_ref, db_ref, carry_ref):
        i = pl.program_id(0)

        @pl.when(i == 0)
        def _():
            carry_ref[...] = jnp.zeros_like(carry_ref)
        rc = _dot3(dc_ref[...], _tri(TQ, True)) + carry_ref[:, 0:1]
        carry_ref[...] = jnp.broadcast_to(rc[:, 0:1], carry_ref.shape)
        fl = fl_ref[...] + b_ref[...]
        dfl = rc.T * jax.nn.sigmoid(-fl)
        dfl_ref[...] = dfl.astype(BF16)
        _acc_add(db_ref, jnp.sum(dfl, axis=0, keepdims=True), i)

    rev = lambda i: (nb - 1 - i, 0)
    return pl.pallas_call(
        body, grid=(nb,),
        in_specs=[pl.BlockSpec((128, TQ), lambda i: (0, nb - 1 - i)), pl.BlockSpec((TQ, 128), rev), _vec_spec(128)],
        out_specs=(pl.BlockSpec((TQ, 128), rev), _acc_spec(128)),
        out_shape=(jax.ShapeDtypeStruct((S, 128), BF16), jax.ShapeDtypeStruct((8, 128), F32)),
        scratch_shapes=[pltpu.VMEM((128, 128), F32)],
        name=name, compiler_params=_cparams(),
    )(dc_rows, fl_raw, b_pad)


def _chk_bias_call(g_rev, *, name):
    def body(g_ref, o_ref):
        x = jnp.broadcast_to(g_ref[...], (TQ, ROLL_W))
        rolled = pltpu.roll(x, ROLL_W - (TQ - 1), 1, stride=1, stride_axis=0)
        qc = lax.broadcasted_iota(jnp.int32, (TQ, WIN), 0) // CHUNK
        kc = lax.broadcasted_iota(jnp.int32, (TQ, WIN), 1) // CHUNK
        band = (kc >= qc) & (kc <= qc + LEFT)
        o_ref[...] = jnp.where(band, rolled[:, 0:WIN], NEG)

    return pl.pallas_call(
        body, grid=(8,),
        in_specs=[pl.BlockSpec((None, 1, ROLL_W), lambda h: (h, 0, 0))],
        out_specs=pl.BlockSpec((None, TQ, WIN), lambda h: (h, 0, 0)),
        out_shape=jax.ShapeDtypeStruct((8, TQ, WIN), F32), name=name, compiler_params=_cparams(),
    )(g_rev.reshape(8, 1, ROLL_W))


def _chk_scores(i, qm, kwin, bias, scale):
    s = _dot_nt(qm, kwin) * scale + bias
    kc = lax.broadcasted_iota(jnp.int32, (TQ, WIN), 1) // CHUNK
    return jnp.where(kc + i * (TQ // CHUNK) >= LEFT, s, NEG)


def _chk_fwd_call(proj, bias, *, name):
    S = proj.shape[0]
    nq = S // TQ
    scale = HEAD ** -0.5

    def body(q_ref, k_ref, v_ref, b_ref, o_ref, kp_ref, vp_ref):
        i = pl.program_id(1)

        @pl.when(i == 0)
        def _():
            kp_ref[0:PADK, :] = jnp.zeros((PADK, 128), BF16)
            vp_ref[0:PADK, :] = jnp.zeros((PADK, 128), BF16)
            kp_ref[PADK:PADK + S, :] = k_ref[...]
            vp_ref[PADK:PADK + S, :] = v_ref[...]
        masks = _lane_masks()
        q = q_ref[...]
        start = pl.multiple_of(i * TQ, TQ)
        kwin = kp_ref[pl.ds(start, WIN), :]
        vwin = vp_ref[pl.ds(start, WIN), :]
        out = jnp.zeros((TQ, 128), F32)
        for h in range(2):
            qm = jnp.where(masks[h], q, jnp.zeros_like(q))
            s = _chk_scores(i, qm, kwin, b_ref[h], scale)
            p = jnp.exp(s - jnp.max(s, axis=1, keepdims=True))
            p = p / jnp.sum(p, axis=1, keepdims=True)
            out = out + _dot(p.astype(BF16), jnp.where(masks[h], vwin, jnp.zeros_like(vwin)))
        o_ref[...] = out

    c0 = 3 * N_PAIR
    return pl.pallas_call(
        body, grid=(N_PAIR, nq),
        in_specs=[pl.BlockSpec((TQ, 128), lambda p, i: (i, c0 + p)),
                  pl.BlockSpec((S, 128), lambda p, i: (0, c0 + N_PAIR + p)),
                  pl.BlockSpec((S, 128), lambda p, i: (0, c0 + 2 * N_PAIR + p)),
                  pl.BlockSpec((2, TQ, WIN), lambda p, i: (p, 0, 0))],
        out_specs=pl.BlockSpec((TQ, 128), lambda p, i: (i, p)),
        out_shape=jax.ShapeDtypeStruct((S, D_GRP), F32),
        scratch_shapes=[pltpu.VMEM((S + PADK, 128), BF16), pltpu.VMEM((S + PADK, 128), BF16)],
        name=name, compiler_params=_cparams(),
    )(proj, proj, proj, bias)


def _chk_bwd_call(proj, do, bias, *, name):
    S = proj.shape[0]
    nq = S // TQ
    scale = HEAD ** -0.5

    def body(q_ref, k_ref, v_ref, do_ref, b_ref, dq_ref, dk_ref, dv_ref, dg_ref, kp_ref, vp_ref, dkp_ref, dvp_ref, db_ref):
        i = pl.program_id(1)

        @pl.when(i == 0)
        def _():
            kp_ref[0:PADK, :] = jnp.zeros((PADK, 128), BF16)
            vp_ref[0:PADK, :] = jnp.zeros((PADK, 128), BF16)
            kp_ref[PADK:PADK + S, :] = k_ref[...]
            vp_ref[PADK:PADK + S, :] = v_ref[...]
            dkp_ref[...] = jnp.zeros_like(dkp_ref)
            dvp_ref[...] = jnp.zeros_like(dvp_ref)
            db_ref[...] = jnp.zeros_like(db_ref)
        masks = _lane_masks()
        q = q_ref[...]
        dout = do_ref[...]
        start = pl.multiple_of(i * TQ, TQ)
        kwin = kp_ref[pl.ds(start, WIN), :]
        vwin = vp_ref[pl.ds(start, WIN), :]
        dq = jnp.zeros((TQ, 128), F32)
        dkw = jnp.zeros((WIN, 128), F32)
        dvw = jnp.zeros((WIN, 128), F32)
        for h in range(2):
            qm = jnp.where(masks[h], q, jnp.zeros_like(q))
            dom = jnp.where(masks[h], dout, jnp.zeros_like(dout))
            s = _chk_scores(i, qm, kwin, b_ref[h], scale)
            p = jnp.exp(s - jnp.max(s, axis=1, keepdims=True))
            p = p / jnp.sum(p, axis=1, keepdims=True)
            dp = _dot_nt(dom, vwin)
            ds = p * (dp - jnp.sum(p * dp, axis=1, keepdims=True))
            db_ref[h] += ds
            dsb = (ds * scale).astype(BF16)
            dq = dq + _dot(dsb, jnp.where(masks[h], kwin, jnp.zeros_like(kwin)))
            dkw = dkw + _dot_tn(dsb, qm)
            dvw = dvw + _dot_tn(p.astype(BF16), dom)
        dq_ref[...] = dq.astype(BF16)
        dkp_ref[pl.ds(start, WIN), :] += dkw
        dvp_ref[pl.ds(start, WIN), :] += dvw

        @pl.when(i == nq - 1)
        def _():
            dk_ref[...] = dkp_ref[PADK:PADK + S, :].astype(BF16)
            dv_ref[...] = dvp_ref[PADK:PADK + S, :].astype(BF16)
            a = lax.broadcasted_iota(jnp.int32, (TQ, TQ), 0)
            b = lax.broadcasted_iota(jnp.int32, (TQ, TQ), 1)
            flip = jnp.where(a + b == TQ - 1, 1.0, 0.0).astype(BF16)
            e = lax.broadcasted_iota(jnp.int32, (1, ROLL_W), 1)
            dg_ref[...] = jnp.zeros_like(dg_ref)
            for h in range(2):
                rev = _dot3_l(flip, db_ref[h])
                wide = jnp.concatenate([rev, jnp.zeros((TQ, ROLL_W - WIN), F32)], axis=1)
                diag = pltpu.roll(wide, 0, 1, stride=1, stride_axis=0)
                dg = jnp.sum(diag, axis=0, keepdims=True)
                lo = jnp.sum(jnp.where(e <= 639, dg, 0.0), axis=1, keepdims=True)
                hi = jnp.sum(jnp.where(e >= 895, dg, 0.0), axis=1, keepdims=True)
                dg_ref[h:h + 1, :] = jnp.where(e == 639, lo, jnp.where(e == 895, hi, dg))

    c0 = 3 * N_PAIR
    res = lambda p, i: (0, p)
    return pl.pallas_call(
        body, grid=(N_PAIR, nq),
        in_specs=[pl.BlockSpec((TQ, 128), lambda p, i: (i, c0 + p)),
                  pl.BlockSpec((S, 128), lambda p, i: (0, c0 + N_PAIR + p)),
                  pl.BlockSpec((S, 128), lambda p, i: (0, c0 + 2 * N_PAIR + p)),
                  pl.BlockSpec((TQ, 128), lambda p, i: (i, p)),
                  pl.BlockSpec((2, TQ, WIN), lambda p, i: (p, 0, 0))],
        out_specs=(pl.BlockSpec((TQ, 128), lambda p, i: (i, p)), pl.BlockSpec((S, 128), res),
                   pl.BlockSpec((S, 128), res), pl.BlockSpec((None, 8, ROLL_W), lambda p, i: (p, 0, 0))),
        out_shape=(jax.ShapeDtypeStruct((S, D_GRP), BF16), jax.ShapeDtypeStruct((S, D_GRP), BF16),
                   jax.ShapeDtypeStruct((S, D_GRP), BF16), jax.ShapeDtypeStruct((N_PAIR, 8, ROLL_W), F32)),
        scratch_shapes=[pltpu.VMEM((S + PADK, 128), BF16), pltpu.VMEM((S + PADK, 128), BF16),
                        pltpu.VMEM((S + PADK, 128), F32), pltpu.VMEM((S + PADK, 128), F32),
                        pltpu.VMEM((2, TQ, WIN), F32)],
        name=name, compiler_params=_cparams(),
    )(proj, proj, proj, do, bias)


def _mem_fwd_call(q, k, v, *, name, tq=512):
    S = q.shape[0]
    scale = MEM_HD ** -0.5

    def body(q_ref, k_ref, v_ref, o_ref):
        s = _dot_nt(q_ref[...], k_ref[...]) * scale
        p = jnp.exp(s - jnp.max(s, axis=1, keepdims=True))
        p = p / jnp.sum(p, axis=1, keepdims=True)
        o_ref[...] = _dot(p.astype(BF16), v_ref[...]).astype(BF16)

    return pl.pallas_call(
        body, grid=(MEM_HEADS, S // tq),
        in_specs=[pl.BlockSpec((tq, MEM_HD), lambda h, i: (i, h)),
                  pl.BlockSpec((N_MEM, MEM_HD), lambda h, i: (0, h)),
                  pl.BlockSpec((N_MEM, MEM_HD), lambda h, i: (0, h))],
        out_specs=pl.BlockSpec((tq, MEM_HD), lambda h, i: (i, h)),
        out_shape=jax.ShapeDtypeStruct((S, D), BF16), name=name, compiler_params=_cparams(),
    )(q, k, v)


def _mem_bwd_call(q, k, v, do, *, name, tq=512):
    S = q.shape[0]
    n = S // tq
    scale = MEM_HD ** -0.5

    def body(q_ref, k_ref, v_ref, do_ref, dq_ref, dk_ref, dv_ref, dka_ref, dva_ref):
        i = pl.program_id(1)

        @pl.when(i == 0)
        def _():
            dka_ref[...] = jnp.zeros_like(dka_ref)
            dva_ref[...] = jnp.zeros_like(dva_ref)
        qb = q_ref[...]
        kb = k_ref[...]
        dob = do_ref[...]
        s = _dot_nt(qb, kb) * scale
        p = jnp.exp(s - jnp.max(s, axis=1, keepdims=True))
        p = p / jnp.sum(p, axis=1, keepdims=True)
        dp = _dot_nt(dob, v_ref[...])
        ds = p * (dp - jnp.sum(p * dp, axis=1, keepdims=True))
        dsb = (ds * scale).astype(BF16)
        dq_ref[...] = _dot(dsb, kb).astype(BF16)
        dka_ref[...] += _dot_tn(dsb, qb)
        dva_ref[...] += _dot_tn(p.astype(BF16), dob)

        @pl.when(i == n - 1)
        def _():
            dk_ref[...] = dka_ref[...].astype(BF16)
            dv_ref[...] = dva_ref[...].astype(BF16)

    kv = pl.BlockSpec((N_MEM, MEM_HD), lambda h, i: (0, h))
    qs = pl.BlockSpec((tq, MEM_HD), lambda h, i: (i, h))
    return pl.pallas_call(
        body, grid=(MEM_HEADS, n), in_specs=[qs, kv, kv, qs], out_specs=(qs, kv, kv),
        out_shape=(jax.ShapeDtypeStruct((S, D), BF16), jax.ShapeDtypeStruct((N_MEM, D), BF16),
                   jax.ShapeDtypeStruct((N_MEM, D), BF16)),
        scratch_shapes=[pltpu.VMEM((N_MEM, MEM_HD), F32), pltpu.VMEM((N_MEM, MEM_HD), F32)],
        name=name, compiler_params=_cparams(),
    )(q, k, v, do)


def _rel_table_to_g(rel):
    return jnp.concatenate([
        jnp.broadcast_to(rel[:, N_REL - 1:N_REL], (8, 640)),
        rel[:, 1:N_REL - 1][:, ::-1],
        jnp.broadcast_to(rel[:, 0:1], (8, 129)),
    ], axis=1)


def _g_to_rel_table(dg):
    return dg[:, 639:896][:, ::-1]


def _local_step(x, mem, target, sp, w_main, w_gate, w_all, g_out, g_mq, g_mk, g_mv, g_mo, g_ff1, g_ff2):
    S = x.shape[0]
    nq = S // TQ
    h1 = _rms_fwd_call(x, sp["g_mix_pre"], name="rms_mix_pre")
    proj = _mm_nn(h1, w_main, "plain", name="mm_proj")
    fl_raw = _mm_nn(h1, w_gate, "plain", name="mm_gate", out_dtype=F32, tn=128)
    c_rep, c_t = _fox_prep_call(fl_raw, sp["b_fgt"], name="fox_prep")
    bias = _chk_bias_call(_rel_table_to_g(sp["rel_bias"]), name="chk_bias")
    yf, lse = _fox_fwd_call(proj, c_rep, c_t, name="fox_fwd")
    yc = _chk_fwd_call(proj, bias, name="chk_fwd")
    yn = _mix_norm_fwd_call(yf, yc, sp["g_fox_out"], sp["g_chk_out"], name="mix_norm_fwd")
    z = _mm_nn(yn, g_out, "rows", name="mm_out", out_dtype=F32)
    x1, h2 = _post_pre_call(x, z, sp["g_mix_post"], sp["g_mem_pre"], name="post_mix")
    memn = _rms_fwd_call(mem, sp["g_mem_kv"], name="rms_mem_kv")
    q2 = _mm_nn(h2, g_mq, "rows", name="mm_mq")
    k2 = _mm_nn(memn, g_mk, "rows", name="mm_mk")
    v2 = _mm_nn(memn, g_mv, "rows", name="mm_mv")
    o2 = _mem_fwd_call(q2, k2, v2, name="mem_fwd")
    y2 = _mm_nn(o2, g_mo, "rows", name="mm_mo", out_dtype=F32)
    x2, h3 = _post_pre_call(x1, y2, sp["g_mem_post"], sp["g_ff_pre"], name="post_mem")
    act, relu = _mm_nn(h3, g_ff1, "cols", name="mm_ff1", epi="relu2")
    y3 = _mm_nn(act, g_ff2, "rows", name="mm_ff2", out_dtype=F32)
    loss_blk, dx3, dy3, dg_ff_post = _final_call(x2, y3, sp["g_ff_post"], target, name="final")
    d_ff2 = _mm_tn(act, dy3, name="mm_dff2").reshape(N_CHIP, D_FF // N_CHIP, D)
    du = _mm_nt(dy3, g_ff2, "rows", name="mm_du", mul2r=relu)
    d_ff1 = _mm_tn(h3, du, name="mm_dff1", cols4=True)
    dh3 = _mm_nt(du, g_ff1, "cols", name="mm_dh3", out_dtype=F32)
    dx2, dy2, dg_ff_pre, dg_mem_post = _bwd_mid_call(dx3, x2, dh3, sp["g_ff_pre"], y2, sp["g_mem_post"], name="bwd_ff")
    d_mo = _mm_tn(o2, dy2, name="mm_dmo").reshape(N_CHIP, D // N_CHIP, D)
    do2 = _mm_nt(dy2, g_mo, "rows", name="mm_do2")
    dq2, dk2, dv2 = _mem_bwd_call(q2, k2, v2, do2, name="mem_bwd")
    d_mq = _mm_tn(h2, dq2, name="mm_dmq").reshape(N_CHIP, D // N_CHIP, D)
    dh2 = _mm_nt(dq2, g_mq, "rows", name="mm_dh2", out_dtype=F32)
    d_mk = _mm_tn(memn, dk2, name="mm_dmk").reshape(N_CHIP, D // N_CHIP, D)
    d_mv = _mm_tn(memn, dv2, name="mm_dmv").reshape(N_CHIP, D // N_CHIP, D)
    dmn_k = _mm_nt(dk2, g_mk, "rows", name="mm_dmemk", out_dtype=F32)
    dmn_v = _mm_nt(dv2, g_mv, "rows", name="mm_dmemv", out_dtype=F32)
    dg_mem_kv = _gain_grad_call(mem, sp["g_mem_kv"], dmn_k, dmn_v, name="gain_mem_kv")
    dx1, dz, dg_mem_pre, dg_mix_post = _bwd_mid_call(dx2, x1, dh2, sp["g_mem_pre"], z, sp["g_mix_post"], name="bwd_mem")
    d_out = _mm_tn(yn, dz, name="mm_dout").reshape(N_CHIP, D // N_CHIP, D)
    dyn = _mm_nt(dz, g_out, "rows", name="mm_dyn", out_dtype=F32)
    dof, doc, delta, dg_fox, dg_chk = _mix_norm_bwd_call(dyn, yf, yc, sp["g_fox_out"], sp["g_chk_out"], name="mix_norm_bwd")
    dqf, dkf, dvf, dct, dcq = _fox_bwd_call(proj, dof, lse, delta, c_rep, c_t, name="fox_bwd")
    dqc, dkc, dvc, dgrev = _chk_bwd_call(proj, doc, bias, name="chk_bwd")
    dc8 = dct[:, :, 0:2, :].transpose(0, 2, 1, 3).reshape(8, S) + dcq[:, ::HEAD].T
    dc_rows = jnp.concatenate([dc8, jnp.zeros((120, S), F32)], axis=0)
    dfl, db_fgt = _fox_gate_bwd_call(dc_rows, fl_raw, sp["b_fgt"], name="fox_gate_bwd")
    dproj = jnp.concatenate([dqf, dkf, dvf, dqc, dkc, dvc, dfl], axis=1)
    d_all = _mm_tn(h1, dproj, name="mm_dwin", tn=640)
    dh1 = _mm_nt(dproj, w_all, "plain", name="mm_dh1", out_dtype=F32)
    grad_x, dg_mix_pre = _bwd_last_call(dx1, x, dh1, sp["g_mix_pre"], name="bwd_mix")
    d_rel = _g_to_rel_table(dgrev[:, 0:2, :].reshape(8, ROLL_W))
    small = {
        "b_fgt": db_fgt[0:1, 0:8], "rel_bias": d_rel, "g_fox_out": dg_fox[0:1], "g_chk_out": dg_chk[0:1],
        "g_mix_pre": dg_mix_pre[0:1], "g_mix_post": dg_mix_post[0:1], "g_mem_kv": dg_mem_kv[0:1],
        "g_mem_pre": dg_mem_pre[0:1], "g_mem_post": dg_mem_post[0:1], "g_ff_pre": dg_ff_pre[0:1],
        "g_ff_post": dg_ff_post[0:1],
    }
    big = {"w_all": d_all, "w_out": d_out, "w_mq": d_mq, "w_mk": d_mk, "w_mv": d_mv, "w_mo": d_mo,
           "w_ff1": d_ff1, "w_ff2": d_ff2}
    return loss_blk[0, 0], grad_x, big, small


ANY = pl.BlockSpec(memory_space=pl.ANY)


def _place():
    x, y, c = lax.axis_index("x"), lax.axis_index("y"), lax.axis_index("c")
    others = [(1 - x, y), (x, 1 - y), (1 - x, 1 - y)]
    return x, y, c, others


def _half(c, rows):
    hr = rows // 2
    return pl.ds(pl.multiple_of(c * hr, 16), hr)


def _all_gather_call(shards, *, name):
    n = len(shards)

    def body(*refs):
        ins, outs = refs[:n], refs[n:2 * n]
        send_sems, recv_sems, loc_sems = refs[2 * n:]
        x, y, c, others = _place()
        me = 2 * x + y
        sends, local = [], []
        for k in range(n):
            rows = shards[k].shape[0]
            mine = _half(c, rows)
            loc = pltpu.make_async_copy(ins[k], outs[k].at[me], loc_sems.at[k])
            loc.start()
            local.append(loc)
            for t, (ox, oy) in enumerate(others):
                cp = pltpu.make_async_remote_copy(
                    src_ref=ins[k].at[mine], dst_ref=outs[k].at[me, mine],
                    send_sem=send_sems.at[k, t], recv_sem=recv_sems.at[k, t],
                    device_id=(ox, oy, c), device_id_type=MESH)
                cp.start()
                sends.append(cp)
        for k in range(n):
            rows = shards[k].shape[0]
            mine = _half(c, rows)
            for t, (ox, oy) in enumerate(others):
                oj = 2 * ox + oy
                pltpu.make_async_remote_copy(
                    src_ref=ins[k].at[mine], dst_ref=outs[k].at[oj, mine],
                    send_sem=send_sems.at[k, t], recv_sem=recv_sems.at[k, t],
                    device_id=(ox, oy, c), device_id_type=MESH).wait_recv()
                fwd = pltpu.make_async_remote_copy(
                    src_ref=outs[k].at[oj, mine], dst_ref=outs[k].at[oj, mine],
                    send_sem=send_sems.at[k, 3 + t], recv_sem=recv_sems.at[k, 3 + t],
                    device_id=(x, y, 1 - c), device_id_type=MESH)
                fwd.start()
                sends.append(fwd)
        for k in range(n):
            rows = shards[k].shape[0]
            theirs = _half(1 - c, rows)
            for t, (ox, oy) in enumerate(others):
                oj = 2 * ox + oy
                pltpu.make_async_remote_copy(
                    src_ref=outs[k].at[oj, theirs], dst_ref=outs[k].at[oj, theirs],
                    send_sem=send_sems.at[k, 3 + t], recv_sem=recv_sems.at[k, 3 + t],
                    device_id=(x, y, 1 - c), device_id_type=MESH).wait_recv()
        for cp in sends:
            cp.wait_send()
        for cp in local:
            cp.wait()

    return pl.pallas_call(
        body, in_specs=[ANY] * n, out_specs=[ANY] * n,
        out_shape=[jax.ShapeDtypeStruct((N_CHIP,) + s.shape, s.dtype) for s in shards],
        scratch_shapes=[pltpu.SemaphoreType.DMA((n, 6)), pltpu.SemaphoreType.DMA((n, 6)),
                        pltpu.SemaphoreType.DMA((n,))],
        name=name,
    )(*shards)


def _pair_send_call(ds, *, name):
    n = len(ds)

    def body(*refs):
        ins, outs = refs[:n], refs[n:2 * n]
        send_sems, recv_sems = refs[2 * n:]
        x, y, c, _ = _place()
        cps = []
        for k in range(n):
            rows = ds[k].shape[1]
            cp = pltpu.make_async_remote_copy(
                src_ref=ins[k].at[:, _half(1 - c, rows)], dst_ref=outs[k],
                send_sem=send_sems.at[k], recv_sem=recv_sems.at[k],
                device_id=(x, y, 1 - c), device_id_type=MESH)
            cp.start()
            cps.append(cp)
        for cp in cps:
            cp.wait()

    return pl.pallas_call(
        body, in_specs=[ANY] * n, out_specs=[ANY] * n,
        out_shape=[jax.ShapeDtypeStruct((N_CHIP, d.shape[1] // 2, d.shape[2]), d.dtype) for d in ds],
        scratch_shapes=[pltpu.SemaphoreType.DMA((n,)), pltpu.SemaphoreType.DMA((n,))],
        name=name,
    )(*ds)


def _pair_add_call(d, r1, c_arr, *, name, tm=256):
    _, rows, cols = d.shape
    hr = rows // 2
    tm = min(tm, hr)
    nb = hr // tm

    def body(c_ref, d_ref, r_ref, o_ref):
        o_ref[...] = (d_ref[...].astype(F32) + r_ref[...].astype(F32)).astype(BF16)

    return pl.pallas_call(
        body,
        grid_spec=pltpu.PrefetchScalarGridSpec(
            num_scalar_prefetch=1, grid=(N_CHIP, nb),
            in_specs=[pl.BlockSpec((None, tm, cols), lambda j, i, c: (j, c[0] * nb + i, 0)),
                      pl.BlockSpec((None, tm, cols), lambda j, i, c: (j, i, 0))],
            out_specs=pl.BlockSpec((None, tm, cols), lambda j, i, c: (j, i, 0))),
        out_shape=jax.ShapeDtypeStruct((N_CHIP, hr, cols), BF16), name=name, compiler_params=_cparams(),
    )(c_arr, d, r1)


def _chip_send_call(ps, *, name):
    n = len(ps)

    def body(*refs):
        ins, outs = refs[:n], refs[n:2 * n]
        send_sems, recv_sems, loc_sems = refs[2 * n:]
        x, y, c, others = _place()
        me = 2 * x + y
        sends, local = [], []
        for k in range(n):
            loc = pltpu.make_async_copy(ins[k].at[me], outs[k].at[me], loc_sems.at[k])
            loc.start()
            local.append(loc)
            for t, (ox, oy) in enumerate(others):
                oj = 2 * ox + oy
                cp = pltpu.make_async_remote_copy(
                    src_ref=ins[k].at[oj], dst_ref=outs[k].at[me],
                    send_sem=send_sems.at[k, t], recv_sem=recv_sems.at[k, t],
                    device_id=(ox, oy, c), device_id_type=MESH)
                cp.start()
                sends.append(cp)
        for k in range(n):
            for t, (ox, oy) in enumerate(others):
                oj = 2 * ox + oy
                pltpu.make_async_remote_copy(
                    src_ref=ins[k].at[oj], dst_ref=outs[k].at[oj],
                    send_sem=send_sems.at[k, t], recv_sem=recv_sems.at[k, t],
                    device_id=(ox, oy, c), device_id_type=MESH).wait_recv()
        for cp in sends:
            cp.wait_send()
        for cp in local:
            cp.wait()

    return pl.pallas_call(
        body, in_specs=[ANY] * n, out_specs=[ANY] * n,
        out_shape=[jax.ShapeDtypeStruct(p.shape, p.dtype) for p in ps],
        scratch_shapes=[pltpu.SemaphoreType.DMA((n, 3)), pltpu.SemaphoreType.DMA((n, 3)),
                        pltpu.SemaphoreType.DMA((n,))],
        name=name,
    )(*ps)


def _chip_sum_call(r2, *, name, tm=256):
    _, hr, cols = r2.shape
    tm = min(tm, hr)

    def body(r_ref, o_ref):
        acc = r_ref[0].astype(F32)
        for j in range(1, N_CHIP):
            acc = acc + r_ref[j].astype(F32)
        o_ref[...] = acc

    return pl.pallas_call(
        body, grid=(hr // tm,),
        in_specs=[pl.BlockSpec((N_CHIP, tm, cols), lambda i: (0, i, 0))],
        out_specs=pl.BlockSpec((tm, cols), lambda i: (i, 0)),
        out_shape=jax.ShapeDtypeStruct((hr, cols), F32), name=name, compiler_params=_cparams(),
    )(r2)


def _pair_gather_call(fs, *, name):
    n = len(fs)

    def body(*refs):
        ins, outs = refs[:n], refs[n:2 * n]
        send_sems, recv_sems, loc_sems = refs[2 * n:]
        x, y, c, _ = _place()
        sends, local = [], []
        for k in range(n):
            rows = 2 * fs[k].shape[0]
            mine = _half(c, rows)
            loc = pltpu.make_async_copy(ins[k], outs[k].at[mine], loc_sems.at[k])
            loc.start()
            local.append(loc)
            cp = pltpu.make_async_remote_copy(
                src_ref=ins[k], dst_ref=outs[k].at[mine],
                send_sem=send_sems.at[k], recv_sem=recv_sems.at[k],
                device_id=(x, y, 1 - c), device_id_type=MESH)
            cp.start()
            sends.append(cp)
        for k in range(n):
            rows = 2 * fs[k].shape[0]
            pltpu.make_async_remote_copy(
                src_ref=ins[k], dst_ref=outs[k].at[_half(1 - c, rows)],
                send_sem=send_sems.at[k], recv_sem=recv_sems.at[k],
                device_id=(x, y, 1 - c), device_id_type=MESH).wait_recv()
        for cp in sends:
            cp.wait_send()
        for cp in local:
            cp.wait()

    return pl.pallas_call(
        body, in_specs=[ANY] * n, out_specs=[ANY] * n,
        out_shape=[jax.ShapeDtypeStruct((2 * f.shape[0], f.shape[1]), f.dtype) for f in fs],
        scratch_shapes=[pltpu.SemaphoreType.DMA((n,)), pltpu.SemaphoreType.DMA((n,)),
                        pltpu.SemaphoreType.DMA((n,))],
        name=name,
    )(*fs)


def _adamw(w, g, m, v):
    m = ADAM_B1 * m + (1.0 - ADAM_B1) * g
    v = ADAM_B2 * v + (1.0 - ADAM_B2) * jnp.square(g)
    m_hat = m / (1.0 - ADAM_B1 ** ADAM_STEP)
    v_hat = v / (1.0 - ADAM_B2 ** ADAM_STEP)
    delta = -ADAM_LR * (m_hat / (jnp.sqrt(v_hat) + ADAM_EPS) + ADAM_WD * w)
    return delta, m, v


def _adamw_call(w, g, m, v, *, name, tm=256):
    rows, cols = w.shape
    tm = min(tm, rows)

    def body(w_ref, g_ref, m_ref, v_ref, d_ref, mo_ref, vo_ref):
        d, mn, vn = _adamw(w_ref[...], g_ref[...], m_ref[...], v_ref[...])
        d_ref[...] = d
        mo_ref[...] = mn
        vo_ref[...] = vn

    spec = pl.BlockSpec((tm, cols), lambda i: (i, 0))
    shp = jax.ShapeDtypeStruct((rows, cols), F32)
    return pl.pallas_call(
        body, grid=(rows // tm,), in_specs=[spec] * 4, out_specs=(spec,) * 3, out_shape=(shp,) * 3,
        name=name, compiler_params=_cparams(),
    )(w, g, m, v)


SMALL_ROWS = 24
N_DEV = 8


def _small_call(gpack, wpack, mpack, vpack, *, name):
    def body(g_ref, w_ref, m_ref, v_ref, gs_ref, d_ref, mo_ref, vo_ref, slots, send_sems, recv_sems):
        x, y, c, _ = _place()
        me = 4 * x + 2 * y + c
        slots[me] = g_ref[...]
        peers = [(dx, dy, dc) for dx in (0, 1) for dy in (0, 1) for dc in (0, 1)][1:]
        cps = []
        for t, (dx, dy, dc) in enumerate(peers):
            px, py, pc = (x + dx) % 2, (y + dy) % 2, (c + dc) % 2
            cp = pltpu.make_async_remote_copy(
                src_ref=g_ref, dst_ref=slots.at[me],
                send_sem=send_sems.at[t], recv_sem=recv_sems.at[t],
                device_id=(px, py, pc), device_id_type=MESH)
            cp.start()
            cps.append(cp)
        for t, (dx, dy, dc) in enumerate(peers):
            px, py, pc = (x + dx) % 2, (y + dy) % 2, (c + dc) % 2
            pltpu.make_async_remote_copy(
                src_ref=g_ref, dst_ref=slots.at[4 * px + 2 * py + pc],
                send_sem=send_sems.at[t], recv_sem=recv_sems.at[t],
                device_id=(px, py, pc), device_id_type=MESH).wait_recv()
        for cp in cps:
            cp.wait_send()
        g = slots[0]
        for j in range(1, N_DEV):
            g = g + slots[j]
        gs_ref[...] = g
        d, mn, vn = _adamw(w_ref[...], g, m_ref[...], v_ref[...])
        d_ref[...] = d
        mo_ref[...] = mn
        vo_ref[...] = vn

    shp = jax.ShapeDtypeStruct((SMALL_ROWS, D), F32)
    vm = pl.BlockSpec(memory_space=pltpu.VMEM)
    return pl.pallas_call(
        body, in_specs=[vm] * 4, out_specs=(vm,) * 4, out_shape=(shp,) * 4,
        scratch_shapes=[pltpu.VMEM((N_DEV, SMALL_ROWS, D), F32),
                        pltpu.SemaphoreType.DMA((N_DEV - 1,)), pltpu.SemaphoreType.DMA((N_DEV - 1,))],
        name=name,
    )(gpack, wpack, mpack, vpack)


SMALL_NAMES = ["g_mix_pre", "g_mix_post", "g_mem_kv", "g_mem_pre", "g_mem_post", "g_ff_pre", "g_ff_post"]


def _pack_small(d):
    rows = [d[k] for k in SMALL_NAMES]
    rows.append(jnp.concatenate([d["g_fox_out"], d["g_chk_out"]], axis=1))
    rows.append(jnp.pad(d["b_fgt"], ((0, 0), (0, D - 8))))
    rows.append(jnp.pad(d["rel_bias"], ((0, 0), (0, D - N_REL))))
    rows.append(jnp.zeros((SMALL_ROWS - 17, D), F32))
    return jnp.concatenate(rows, axis=0)


def _unpack_small(p):
    out = {k: p[i:i + 1] for i, k in enumerate(SMALL_NAMES)}
    out["g_fox_out"] = p[7:8, 0:D_GRP]
    out["g_chk_out"] = p[7:8, D_GRP:D]
    out["b_fgt"] = p[8:9, 0:8]
    out["rel_bias"] = p[9:17, 0:N_REL].reshape(1, 8, N_REL)
    return out


WEIGHTS = ["w_in", "b_fgt", "rel_bias", "g_fox_out", "g_chk_out", "w_out", "g_mix_pre", "g_mix_post", "g_mem_kv",
           "w_mq", "w_mk", "w_mv", "w_mo", "g_mem_pre", "g_mem_post", "w_ff1", "w_ff2", "g_ff_pre", "g_ff_post"]
BIG = ["w_in", "w_out", "w_mq", "w_mk", "w_mv", "w_mo", "w_ff1", "w_ff2"]


def _in_cols_to_all(w):
    pad = jnp.zeros(w.shape[:-1] + (D_ALL - D_IN,), w.dtype)
    return jnp.concatenate([w[..., :1536], w[..., 1544:D_IN], w[..., 1536:1544], pad], axis=-1)


def _all_cols_to_in(w):
    return jnp.concatenate([w[..., :1536], w[..., 3072:3080], w[..., 1536:3072]], axis=-1)


def kernel(x, mem, w_in, b_fgt, rel_bias, g_fox_out, g_chk_out, w_out, g_mix_pre, g_mix_post, g_mem_kv, w_mq, w_mk, w_mv, w_mo, g_mem_pre, g_mem_post, w_ff1, w_ff2, g_ff_pre, g_ff_post, loss_target, m_w_in, m_b_fgt, m_rel_bias, m_g_fox_out, m_g_chk_out, m_w_out, m_g_mix_pre, m_g_mix_post, m_g_mem_kv, m_w_mq, m_w_mk, m_w_mv, m_w_mo, m_g_mem_pre, m_g_mem_post, m_w_ff1, m_w_ff2, m_g_ff_pre, m_g_ff_post, v_w_in, v_b_fgt, v_rel_bias, v_g_fox_out, v_g_chk_out, v_w_out, v_g_mix_pre, v_g_mix_post, v_g_mem_kv, v_w_mq, v_w_mk, v_w_mv, v_w_mo, v_g_mem_pre, v_g_mem_post, v_w_ff1, v_w_ff2, v_g_ff_pre, v_g_ff_post):
    w = dict(w_in=w_in, b_fgt=b_fgt, rel_bias=rel_bias, g_fox_out=g_fox_out, g_chk_out=g_chk_out, w_out=w_out,
             g_mix_pre=g_mix_pre, g_mix_post=g_mix_post, g_mem_kv=g_mem_kv, w_mq=w_mq, w_mk=w_mk, w_mv=w_mv,
             w_mo=w_mo, g_mem_pre=g_mem_pre, g_mem_post=g_mem_post, w_ff1=w_ff1, w_ff2=w_ff2, g_ff_pre=g_ff_pre,
             g_ff_post=g_ff_post)
    m = dict(w_in=m_w_in, b_fgt=m_b_fgt, rel_bias=m_rel_bias, g_fox_out=m_g_fox_out, g_chk_out=m_g_chk_out,
             w_out=m_w_out, g_mix_pre=m_g_mix_pre, g_mix_post=m_g_mix_post, g_mem_kv=m_g_mem_kv, w_mq=m_w_mq,
             w_mk=m_w_mk, w_mv=m_w_mv, w_mo=m_w_mo, g_mem_pre=m_g_mem_pre, g_mem_post=m_g_mem_post,
             w_ff1=m_w_ff1, w_ff2=m_w_ff2, g_ff_pre=m_g_ff_pre, g_ff_post=m_g_ff_post)
    v = dict(w_in=v_w_in, b_fgt=v_b_fgt, rel_bias=v_rel_bias, g_fox_out=v_g_fox_out, g_chk_out=v_g_chk_out,
             w_out=v_w_out, g_mix_pre=v_g_mix_pre, g_mix_post=v_g_mix_post, g_mem_kv=v_g_mem_kv, w_mq=v_w_mq,
             w_mk=v_w_mk, w_mv=v_w_mv, w_mo=v_w_mo, g_mem_pre=v_g_mem_pre, g_mem_post=v_g_mem_post,
             w_ff1=v_w_ff1, w_ff2=v_w_ff2, g_ff_pre=v_g_ff_pre, g_ff_post=v_g_ff_post)

    def small_rows(d):
        out = {k: d[k] for k in SMALL_NAMES + ["g_fox_out", "g_chk_out", "b_fgt"]}
        out["rel_bias"] = d["rel_bias"][0]
        return out

    gathered = _all_gather_call([w[k][0].astype(BF16) for k in BIG], name="all_gather")
    gw = dict(zip(BIG, gathered))
    w_in_full = gw["w_in"].transpose(1, 0, 2).reshape(D, D_IN)
    w_all = _in_cols_to_all(w_in_full)
    sp = small_rows(w)
    sp["b_fgt"] = jnp.pad(sp["b_fgt"], ((0, 0), (0, 120)))
    loss_part, grad_x, big, small = _local_step(
        x[0], mem[0], loss_target[0], sp, w_all[:, :3072], w_all[:, 3072:], w_all,
        gw["w_out"], gw["w_mq"], gw["w_mk"], gw["w_mv"], gw["w_mo"], gw["w_ff1"], gw["w_ff2"])
    loss = lax.psum(loss_part, ("x", "y", "c"))

    d_in = _all_cols_to_in(big["w_all"]).reshape(D, N_CHIP, D_IN // N_CHIP).transpose(1, 0, 2)
    ds = [d_in] + [big[k] for k in BIG[1:]]
    c_arr = jnp.reshape(lax.axis_index("c"), (1,)).astype(jnp.int32)
    r1 = _pair_send_call(ds, name="rs_pair_send")
    ps = [_pair_add_call(d, r, c_arr, name="rs_pair_add_" + k) for k, d, r in zip(BIG, ds, r1)]
    r2 = _chip_send_call(ps, name="rs_chip_send")
    fs = [_chip_sum_call(r, name="rs_chip_sum_" + k) for k, r in zip(BIG, r2)]
    grads = dict(zip(BIG, _pair_gather_call(fs, name="rs_pair_gather")))

    delta, new_m, new_v = {}, {}, {}
    for k in BIG:
        delta[k], new_m[k], new_v[k] = _adamw_call(w[k][0], grads[k], m[k][0], v[k][0], name="adamw_" + k)
    sg, sd, sm, sv = _small_call(_pack_small(small), _pack_small(small_rows(w)), _pack_small(small_rows(m)),
                                 _pack_small(small_rows(v)), name="small_allreduce_adamw")
    for dst, packed in ((grads, sg), (delta, sd), (new_m, sm), (new_v, sv)):
        dst.update(_unpack_small(packed))

    def out(d, k):
        return d[k][None] if k in BIG else d[k]

    return (loss, grad_x[None], *[out(grads, k) for k in WEIGHTS], *[out(delta, k) for k in WEIGHTS],
            *[out(new_m, k) for k in WEIGHTS], *[out(new_v, k) for k in WEIGHTS])
```

```python
import functools

import jax
import jax.numpy as jnp
from jax import lax
from jax.experimental import pallas as pl
from jax.experimental.pallas import tpu as pltpu

F32 = jnp.float32
BF16 = jnp.bfloat16

D = 1024
HEAD = 64
N_PAIR = 4
D_GRP = 512
CHUNK = 64
LEFT = 8
MAX_REL = 128
N_REL = 2 * MAX_REL + 1
N_MEM = 256
MEM_HEADS = 4
MEM_HD = 256
D_FF = 4096
D_IN = 3080
D_ALL = 3200
EPS = 1e-6
TQ = 256
WIN = (LEFT + TQ // CHUNK) * CHUNK
PADK = LEFT * CHUNK
ROLL_W = 1024
NEG = -1e30
N_CHIP = 4
VMEM_LIMIT = 48 * 1024 * 1024

ADAM_LR = 0.001
ADAM_B1 = 0.9
ADAM_B2 = 0.999
ADAM_EPS = 1e-08
ADAM_WD = 0.01
ADAM_STEP = 10

MESH = pl.DeviceIdType.MESH


def _cparams():
    return pltpu.CompilerParams(vmem_limit_bytes=VMEM_LIMIT)


ANY = pl.BlockSpec(memory_space=pl.ANY)


class _Task:
    def __init__(self, arrays, out_shapes, sems, issue, drain, aliases=None):
        self.arrays, self.out_shapes, self.sems = list(arrays), list(out_shapes), list(sems)
        self.issue, self.drain, self.aliases = issue, drain, dict(aliases or {})


def _merge_tasks(tasks):
    tasks = [t for t in tasks if t is not None]
    if len(tasks) == 1:
        return tasks[0]
    cuts, a, o, s = [], 0, 0, 0
    aliases = {}
    for t in tasks:
        cuts.append((a, o, s))
        aliases.update({a + i: o + j for i, j in t.aliases.items()})
        a, o, s = a + len(t.arrays), o + len(t.out_shapes), s + len(t.sems)

    def part(fn_name):
        def run(ins, outs, sems):
            for t, (a0, o0, s0) in zip(tasks, cuts):
                getattr(t, fn_name)(ins[a0:a0 + len(t.arrays)], outs[o0:o0 + len(t.out_shapes)],
                                    sems[s0:s0 + len(t.sems)])
        return run

    return _Task([x for t in tasks for x in t.arrays], [x for t in tasks for x in t.out_shapes],
                 [x for t in tasks for x in t.sems], part("issue"), part("drain"), aliases)


def _pallas(body, *, grid, in_specs, out_specs, out_shape, name, scratch_shapes=(), task=None):
    if task is None:
        return pl.pallas_call(body, grid=grid, in_specs=list(in_specs), out_specs=out_specs, out_shape=out_shape,
                              scratch_shapes=list(scratch_shapes), name=name, compiler_params=_cparams())
    single = not isinstance(out_shape, (tuple, list))
    o_shapes = [out_shape] if single else list(out_shape)
    o_specs = [out_specs] if single else list(out_specs)
    n_in, n_out, n_scr = len(in_specs), len(o_shapes), len(scratch_shapes)
    t_in, t_out = len(task.arrays), len(task.out_shapes)

    def carried(*refs):
        cut = [n_in, t_in, n_out, t_out, n_scr]
        parts, p = [], 0
        for c in cut:
            parts.append(refs[p:p + c])
            p += c
        ins, tins, outs, touts, scr = parts
        tsems = refs[p:]
        ids = [pl.program_id(a) for a in range(len(grid))]
        first = functools.reduce(jnp.logical_and, [i == 0 for i in ids])
        last = functools.reduce(jnp.logical_and, [i == g - 1 for i, g in zip(ids, grid)])

        @pl.when(first)
        def _():
            task.issue(tins, touts, tsems)
        body(*ins, *outs, *scr)

        @pl.when(last)
        def _():
            task.drain(tins, touts, tsems)

    call = pl.pallas_call(
        carried, grid=grid, in_specs=list(in_specs) + [ANY] * t_in, out_specs=o_specs + [ANY] * t_out,
        out_shape=o_shapes + list(task.out_shapes), scratch_shapes=list(scratch_shapes) + list(task.sems),
        input_output_aliases={n_in + i: n_out + j for i, j in task.aliases.items()},
        name=name, compiler_params=_cparams())

    def run(*args):
        res = call(*args, *task.arrays)
        outs = res[:n_out]
        return (outs[0] if single else tuple(outs)), list(res[n_out:])

    return run


def _comm_call(task, *, name):
    t_in, t_out = len(task.arrays), len(task.out_shapes)

    def body(*refs):
        tins, touts, tsems = refs[:t_in], refs[t_in:t_in + t_out], refs[t_in + t_out:]
        task.issue(tins, touts, tsems)
        task.drain(tins, touts, tsems)

    return pl.pallas_call(
        body, in_specs=[ANY] * t_in, out_specs=[ANY] * t_out, out_shape=list(task.out_shapes),
        scratch_shapes=list(task.sems), input_output_aliases=dict(task.aliases), name=name,
    )(*task.arrays)


def _dot(a, b):
    return jnp.dot(a, b, preferred_element_type=F32)


def _dot_nt(a, b):
    return lax.dot_general(a, b, (((1,), (1,)), ((), ())), preferred_element_type=F32)


def _dot_tn(a, b):
    return lax.dot_general(a, b, (((0,), (0,)), ((), ())), preferred_element_type=F32)


def _split3(x):
    hi = x.astype(BF16)
    r1 = x - hi.astype(F32)
    mid = r1.astype(BF16)
    lo = (r1 - mid.astype(F32)).astype(BF16)
    return hi, mid, lo


def _dot3(x, m01):
    hi, mid, lo = _split3(x)
    return _dot(hi, m01) + _dot(mid, m01) + _dot(lo, m01)


def _dot3_l(m01, x):
    hi, mid, lo = _split3(x)
    return _dot(m01, hi) + _dot(m01, mid) + _dot(m01, lo)


def _mm_nn(a, b, kind, *, name, out_dtype=BF16, tm=512, tn=512, epi=None, task=None):
    M, K = a.shape
    if kind == "plain":
        N = b.shape[1]
        b_spec = pl.BlockSpec((K, tn), lambda m, n: (0, n))
    elif kind == "rows":
        N = b.shape[2]
        b_spec = pl.BlockSpec((N_CHIP, K // N_CHIP, tn), lambda m, n: (0, 0, n))
    else:
        nq = b.shape[2]
        N = N_CHIP * nq
        per = nq // tn
        b_spec = pl.BlockSpec((None, K, tn), lambda m, n: (n // per, 0, n % per))
    tm = min(tm, M)
    kq = K // N_CHIP

    def body(a_ref, b_ref, *o_refs):
        if kind == "rows":
            acc = _dot(a_ref[:, 0:kq], b_ref[0])
            for j in range(1, N_CHIP):
                acc += _dot(a_ref[:, j * kq:(j + 1) * kq], b_ref[j])
        else:
            acc = _dot(a_ref[...], b_ref[...])
        if epi == "relu2":
            r = jnp.maximum(acc, 0.0)
            o_refs[0][...] = (r * r).astype(BF16)
            o_refs[1][...] = r.astype(BF16)
        else:
            o_refs[0][...] = acc.astype(out_dtype)

    o_spec = pl.BlockSpec((tm, tn), lambda m, n: (m, n))
    if epi == "relu2":
        out_shape = (jax.ShapeDtypeStruct((M, N), BF16), jax.ShapeDtypeStruct((M, N), BF16))
        out_specs = (o_spec, o_spec)
    else:
        out_shape = jax.ShapeDtypeStruct((M, N), out_dtype)
        out_specs = o_spec
    return _pallas(
        body, grid=(M // tm, N // tn),
        in_specs=[pl.BlockSpec((tm, K), lambda m, n: (m, 0)), b_spec],
        out_specs=out_specs, out_shape=out_shape, name=name, task=task,
    )(a, b)


def _mm_nt(a, b, kind, *, name, out_dtype=BF16, tm=512, tn=512, mul2r=None, task=None):
    M, K = a.shape
    if kind == "plain":
        N = b.shape[0]
        b_spec = pl.BlockSpec((tn, K), lambda m, n: (n, 0))
    elif kind == "rows":
        nq = b.shape[1]
        N = N_CHIP * nq
        tn = min(tn, nq)
        per = nq // tn
        b_spec = pl.BlockSpec((None, tn, K), lambda m, n: (n // per, n % per, 0))
    else:
        N = b.shape[1]
        b_spec = pl.BlockSpec((N_CHIP, tn, K // N_CHIP), lambda m, n: (0, n, 0))
    tm = min(tm, M)
    kq = K // N_CHIP

    def body(a_ref, b_ref, *rest):
        o_ref = rest[-1]
        if kind == "cols":
            acc = _dot_nt(a_ref[:, 0:kq], b_ref[0])
            for j in range(1, N_CHIP):
                acc += _dot_nt(a_ref[:, j * kq:(j + 1) * kq], b_ref[j])
        else:
            acc = _dot_nt(a_ref[...], b_ref[...])
        if mul2r is not None:
            acc = acc * (2.0 * rest[0][...].astype(F32))
        o_ref[...] = acc.astype(out_dtype)

    in_specs = [pl.BlockSpec((tm, K), lambda m, n: (m, 0)), b_spec]
    args = [a, b]
    if mul2r is not None:
        in_specs.append(pl.BlockSpec((tm, tn), lambda m, n: (m, n)))
        args.append(mul2r)
    return _pallas(
        body, grid=(M // tm, N // tn), in_specs=in_specs,
        out_specs=pl.BlockSpec((tm, tn), lambda m, n: (m, n)),
        out_shape=jax.ShapeDtypeStruct((M, N), out_dtype), name=name, task=task,
    )(*args)


def _mm_tn(a, b, *, name, out_dtype=BF16, tk=512, tn=512, cols4=False, task=None):
    M, K1 = a.shape
    N = b.shape[1]
    tk = min(tk, K1)
    tn = min(tn, N)

    def body(a_ref, b_ref, o_ref):
        o_ref[...] = _dot_tn(a_ref[...], b_ref[...]).astype(out_dtype)

    if cols4:
        per = (N // N_CHIP) // tn
        out_shape = jax.ShapeDtypeStruct((N_CHIP, K1, N // N_CHIP), out_dtype)
        o_spec = pl.BlockSpec((None, tk, tn), lambda k, n: (n // per, k, n % per))
    else:
        out_shape = jax.ShapeDtypeStruct((K1, N), out_dtype)
        o_spec = pl.BlockSpec((tk, tn), lambda k, n: (k, n))
    return _pallas(
        body, grid=(K1 // tk, N // tn),
        in_specs=[pl.BlockSpec((M, tk), lambda k, n: (0, k)), pl.BlockSpec((M, tn), lambda k, n: (0, n))],
        out_specs=o_spec, out_shape=out_shape, name=name, task=task,
    )(a, b)


def _rms(x, g):
    r = lax.rsqrt(jnp.mean(x * x, axis=-1, keepdims=True) + EPS)
    return x * r * g


def _rms_bwd(x, g, dy):
    r = lax.rsqrt(jnp.mean(x * x, axis=-1, keepdims=True) + EPS)
    xh = x * r
    dg = jnp.sum(dy * xh, axis=0, keepdims=True)
    dxh = dy * g
    dx = r * (dxh - xh * jnp.mean(dxh * xh, axis=-1, keepdims=True))
    return dx, dg


def _row_spec(tm, n):
    return pl.BlockSpec((tm, n), lambda i: (i, 0))


def _vec_spec(n):
    return pl.BlockSpec((1, n), lambda i: (0, 0))


def _acc_spec(n):
    return pl.BlockSpec((8, n), lambda i: (0, 0))


def _acc_add(ref, row, i):
    @pl.when(i == 0)
    def _():
        ref[...] = jnp.zeros_like(ref)
    ref[0:1, :] += row


def _rms_fwd_call(x, g, *, name, tm=256, task=None):
    M, n = x.shape
    tm = min(tm, M)

    def body(x_ref, g_ref, h_ref):
        h_ref[...] = _rms(x_ref[...], g_ref[...]).astype(BF16)

    return _pallas(
        body, grid=(M // tm,), in_specs=[_row_spec(tm, n), _vec_spec(n)], out_specs=_row_spec(tm, n),
        out_shape=jax.ShapeDtypeStruct((M, n), BF16), name=name, task=task,
    )(x, g)


def _post_pre_call(xres, z, g_post, g_pre, *, name, tm=256):
    M, n = xres.shape

    def body(x_ref, z_ref, gp_ref, gn_ref, xo_ref, h_ref):
        xn = x_ref[...] + _rms(z_ref[...], gp_ref[...])
        xo_ref[...] = xn
        h_ref[...] = _rms(xn, gn_ref[...]).astype(BF16)

    return pl.pallas_call(
        body, grid=(M // tm,),
        in_specs=[_row_spec(tm, n), _row_spec(tm, n), _vec_spec(n), _vec_spec(n)],
        out_specs=(_row_spec(tm, n), _row_spec(tm, n)),
        out_shape=(jax.ShapeDtypeStruct((M, n), F32), jax.ShapeDtypeStruct((M, n), BF16)),
        name=name, compiler_params=_cparams(),
    )(xres, z, g_post, g_pre)


def _final_call(x2, y3, g_post, target, *, name, tm=256):
    M, n = x2.shape

    def body(x_ref, y_ref, g_ref, t_ref, loss_ref, dx_ref, dy_ref, dg_ref):
        i = pl.program_id(0)
        y = y_ref[...]
        g = g_ref[...]
        diff = x_ref[...] + _rms(y, g) - t_ref[...]
        part = 0.5 * jnp.sum(jnp.sum(diff * diff, axis=1, keepdims=True), axis=0, keepdims=True) / n

        @pl.when(i == 0)
        def _():
            loss_ref[...] = jnp.zeros_like(loss_ref)
        loss_ref[...] += jnp.broadcast_to(part, loss_ref.shape)
        dx = diff / n
        dx_ref[...] = dx
        dy, dg = _rms_bwd(y, g, dx)
        dy_ref[...] = dy.astype(BF16)
        _acc_add(dg_ref, dg, i)

    return pl.pallas_call(
        body, grid=(M // tm,),
        in_specs=[_row_spec(tm, n), _row_spec(tm, n), _vec_spec(n), _row_spec(tm, n)],
        out_specs=(pl.BlockSpec((8, 128), lambda i: (0, 0)), _row_spec(tm, n), _row_spec(tm, n), _acc_spec(n)),
        out_shape=(jax.ShapeDtypeStruct((8, 128), F32), jax.ShapeDtypeStruct((M, n), F32),
                   jax.ShapeDtypeStruct((M, n), BF16), jax.ShapeDtypeStruct((8, n), F32)),
        name=name, compiler_params=_cparams(),
    )(x2, y3, g_post, target)


def _bwd_mid_call(dx_in, x, dh, g_pre, y, g_post, *, name, tm=256):
    M, n = x.shape

    def body(dxi_ref, x_ref, dh_ref, gpre_ref, y_ref, gpost_ref, dx_ref, dy_ref, dgpre_ref, dgpost_ref):
        i = pl.program_id(0)
        d1, dg1 = _rms_bwd(x_ref[...], gpre_ref[...], dh_ref[...])
        dx = dxi_ref[...] + d1
        dx_ref[...] = dx
        dy, dg2 = _rms_bwd(y_ref[...], gpost_ref[...], dx)
        dy_ref[...] = dy.astype(BF16)
        _acc_add(dgpre_ref, dg1, i)
        _acc_add(dgpost_ref, dg2, i)

    return pl.pallas_call(
        body, grid=(M // tm,),
        in_specs=[_row_spec(tm, n), _row_spec(tm, n), _row_spec(tm, n), _vec_spec(n), _row_spec(tm, n), _vec_spec(n)],
        out_specs=(_row_spec(tm, n), _row_spec(tm, n), _acc_spec(n), _acc_spec(n)),
        out_shape=(jax.ShapeDtypeStruct((M, n), F32), jax.ShapeDtypeStruct((M, n), BF16),
                   jax.ShapeDtypeStruct((8, n), F32), jax.ShapeDtypeStruct((8, n), F32)),
        name=name, compiler_params=_cparams(),
    )(dx_in, x, dh, g_pre, y, g_post)


def _bwd_last_call(dx_in, x, dh, g_pre, *, name, tm=256, task=None):
    M, n = x.shape

    def body(dxi_ref, x_ref, dh_ref, g_ref, dx_ref, dg_ref):
        i = pl.program_id(0)
        d1, dg1 = _rms_bwd(x_ref[...], g_ref[...], dh_ref[...])
        dx_ref[...] = dxi_ref[...] + d1
        _acc_add(dg_ref, dg1, i)

    return _pallas(
        body, grid=(M // tm,),
        in_specs=[_row_spec(tm, n), _row_spec(tm, n), _row_spec(tm, n), _vec_spec(n)],
        out_specs=(_row_spec(tm, n), _acc_spec(n)),
        out_shape=(jax.ShapeDtypeStruct((M, n), F32), jax.ShapeDtypeStruct((8, n), F32)),
        name=name, task=task,
    )(dx_in, x, dh, g_pre)


def _gain_grad_call(x, g, dy_a, dy_b, *, name):
    M, n = x.shape

    def body(x_ref, g_ref, a_ref, b_ref, dg_ref):
        _, dg = _rms_bwd(x_ref[...], g_ref[...], a_ref[...] + b_ref[...])
        dg_ref[...] = jnp.zeros_like(dg_ref)
        dg_ref[0:1, :] = dg

    return pl.pallas_call(
        body, grid=(1,),
        in_specs=[_row_spec(M, n), _vec_spec(n), _row_spec(M, n), _row_spec(M, n)],
        out_specs=_acc_spec(n), out_shape=jax.ShapeDtypeStruct((8, n), F32),
        name=name, compiler_params=_cparams(),
    )(x, g, dy_a, dy_b)


def _head_group_matrix():
    a = lax.broadcasted_iota(jnp.int32, (D_GRP, D_GRP), 0) // HEAD
    b = lax.broadcasted_iota(jnp.int32, (D_GRP, D_GRP), 1) // HEAD
    return jnp.where(a == b, 1.0, 0.0).astype(BF16)


def _mix_norm_fwd_call(yf, yc, gf, gc, *, name, tm=256):
    M = yf.shape[0]

    def body(yf_ref, yc_ref, gf_ref, gc_ref, o_ref):
        o_ref[:, 0:D_GRP] = _rms(yf_ref[...], gf_ref[...]).astype(BF16)
        o_ref[:, D_GRP:D] = _rms(yc_ref[...], gc_ref[...]).astype(BF16)

    return pl.pallas_call(
        body, grid=(M // tm,),
        in_specs=[_row_spec(tm, D_GRP), _row_spec(tm, D_GRP), _vec_spec(D_GRP), _vec_spec(D_GRP)],
        out_specs=_row_spec(tm, D), out_shape=jax.ShapeDtypeStruct((M, D), BF16),
        name=name, compiler_params=_cparams(),
    )(yf, yc, gf, gc)


def _mix_norm_bwd_call(dyn, yf, yc, gf, gc, *, name, tm=256):
    M = yf.shape[0]

    def body(dyn_ref, yf_ref, yc_ref, gf_ref, gc_ref, dof_ref, doc_ref, delta_ref, dgf_ref, dgc_ref):
        i = pl.program_id(0)
        yf_ = yf_ref[...]
        dof, dgf = _rms_bwd(yf_, gf_ref[...], dyn_ref[:, 0:D_GRP])
        doc, dgc = _rms_bwd(yc_ref[...], gc_ref[...], dyn_ref[:, D_GRP:D])
        dof_b = dof.astype(BF16)
        dof_ref[...] = dof_b
        doc_ref[...] = doc.astype(BF16)
        prod = dof_b.astype(F32) * yf_
        hi = prod.astype(BF16)
        lo = (prod - hi.astype(F32)).astype(BF16)
        grp = _head_group_matrix()
        delta_ref[...] = _dot(hi, grp) + _dot(lo, grp)
        _acc_add(dgf_ref, dgf, i)
        _acc_add(dgc_ref, dgc, i)

    return pl.pallas_call(
        body, grid=(M // tm,),
        in_specs=[_row_spec(tm, D), _row_spec(tm, D_GRP), _row_spec(tm, D_GRP), _vec_spec(D_GRP), _vec_spec(D_GRP)],
        out_specs=(_row_spec(tm, D_GRP), _row_spec(tm, D_GRP), _row_spec(tm, D_GRP), _acc_spec(D_GRP), _acc_spec(D_GRP)),
        out_shape=(jax.ShapeDtypeStruct((M, D_GRP), BF16), jax.ShapeDtypeStruct((M, D_GRP), BF16),
                   jax.ShapeDtypeStruct((M, D_GRP), F32), jax.ShapeDtypeStruct((8, D_GRP), F32),
                   jax.ShapeDtypeStruct((8, D_GRP), F32)),
        name=name, compiler_params=_cparams(),
    )(dyn, yf, yc, gf, gc)


def _tri(n, lower_incl):
    a = lax.broadcasted_iota(jnp.int32, (n, n), 0)
    b = lax.broadcasted_iota(jnp.int32, (n, n), 1)
    return jnp.where(a >= b, 1.0, 0.0).astype(BF16) if lower_incl else jnp.where(a <= b, 1.0, 0.0).astype(BF16)


def _fox_prep_call(fl_raw, b_pad, *, name):
    S = fl_raw.shape[0]
    nb = S // TQ

    def body(fl_ref, b_ref, crep_ref, ct_ref, carry_ref):
        i = pl.program_id(0)

        @pl.when(i == 0)
        def _():
            carry_ref[...] = jnp.zeros_like(carry_ref)
        logf = jax.nn.log_sigmoid(fl_ref[...] + b_ref[...])
        cb = _dot3_l(_tri(TQ, True), logf) + carry_ref[0:1, :]
        carry_ref[0:1, :] = cb[TQ - 1:TQ, :]
        a = lax.broadcasted_iota(jnp.int32, (128, D_GRP), 0)
        b = lax.broadcasted_iota(jnp.int32, (128, D_GRP), 1) // HEAD
        expand = jnp.where(a == b, 1.0, 0.0).astype(BF16)
        crep = _dot3(cb, expand)
        crep_ref[...] = crep
        ct_ref[...] = crep.T

    return pl.pallas_call(
        body, grid=(nb,),
        in_specs=[_row_spec(TQ, 128), _vec_spec(128)],
        out_specs=(_row_spec(TQ, D_GRP), pl.BlockSpec((None, D_GRP, TQ), lambda i: (i, 0, 0))),
        out_shape=(jax.ShapeDtypeStruct((S, D_GRP), F32), jax.ShapeDtypeStruct((nb, D_GRP, TQ), F32)),
        scratch_shapes=[pltpu.VMEM((8, 128), F32)],
        name=name, compiler_params=_cparams(),
    )(fl_raw, b_pad)


def _lane_masks():
    lane = lax.broadcasted_iota(jnp.int32, (1, 128), 1)
    return lane < HEAD, lane >= HEAD


def _fox_fwd_call(proj, c_rep, c_t, *, name, task=None):
    S = proj.shape[0]
    nq = S // TQ
    scale = HEAD ** -0.5

    def body(q_ref, k_ref, v_ref, c_ref, ct_ref, o_ref, lse_ref):
        i = pl.program_id(1)
        m_lo, m_hi = _lane_masks()
        masks = (m_lo, m_hi)
        q = q_ref[...]
        qm = [jnp.where(mk, q, jnp.zeros_like(q)) for mk in masks]
        cq = c_ref[...]
        cqh = [cq[:, 0:1], cq[:, HEAD:HEAD + 1]]
        row = lax.broadcasted_iota(jnp.int32, (TQ, TQ), 0)
        col = lax.broadcasted_iota(jnp.int32, (TQ, TQ), 1)

        def step(j, carry, masked):
            ms, ls, acc = carry
            start = pl.multiple_of(j * TQ, TQ)
            k = k_ref[pl.ds(start, TQ), :]
            v = v_ref[pl.ds(start, TQ), :]
            ct = ct_ref[j]
            new_m, new_l, pv, alpha_l = [], [], [], []
            for h in range(2):
                s = _dot_nt(qm[h], k) * scale + (cqh[h] - ct[HEAD * h:HEAD * h + 1, :])
                if masked:
                    s = jnp.where(row >= col, s, NEG)
                mn = jnp.maximum(ms[h], jnp.max(s, axis=1, keepdims=True))
                alpha = jnp.exp(ms[h] - mn)
                p = jnp.exp(s - mn)
                new_l.append(alpha * ls[h] + jnp.sum(p, axis=1, keepdims=True))
                new_m.append(mn)
                alpha_l.append(alpha)
                pv.append(_dot(p.astype(BF16), jnp.where(masks[h], v, jnp.zeros_like(v))))
            alpha_lane = jnp.where(m_lo, alpha_l[0], alpha_l[1])
            acc = acc * alpha_lane + pv[0] + pv[1]
            return (tuple(new_m), tuple(new_l), acc)

        init = ((jnp.full((TQ, 1), NEG, F32),) * 2, (jnp.zeros((TQ, 1), F32),) * 2, jnp.zeros((TQ, 128), F32))
        carry = lax.fori_loop(0, i, lambda j, c: step(j, c, False), init)
        ms, ls, acc = step(i, carry, True)
        l_lane = jnp.where(m_lo, ls[0], ls[1])
        o_ref[...] = acc / l_lane
        lse_ref[...] = jnp.where(m_lo, ms[0] + jnp.log(ls[0]), ms[1] + jnp.log(ls[1]))

    return _pallas(
        body, grid=(N_PAIR, nq),
        in_specs=[pl.BlockSpec((TQ, 128), lambda p, i: (i, p)),
                  pl.BlockSpec((S, 128), lambda p, i: (0, N_PAIR + p)),
                  pl.BlockSpec((S, 128), lambda p, i: (0, 2 * N_PAIR + p)),
                  pl.BlockSpec((TQ, 128), lambda p, i: (i, p)),
                  pl.BlockSpec((nq, 128, TQ), lambda p, i: (0, p, 0))],
        out_specs=(pl.BlockSpec((TQ, 128), lambda p, i: (i, p)), pl.BlockSpec((TQ, 128), lambda p, i: (i, p))),
        out_shape=(jax.ShapeDtypeStruct((S, D_GRP), F32), jax.ShapeDtypeStruct((S, D_GRP), F32)),
        name=name, task=task,
    )(proj, proj, proj, c_rep, c_t)


def _fox_bwd_call(proj, do, lse_rep, delta_rep, c_rep, c_t, *, name, task=None):
    S = proj.shape[0]
    nq = S // TQ
    scale = HEAD ** -0.5

    def body(q_ref, k_ref, v_ref, do_ref, lse_ref, dl_ref, c_ref, ct_ref, dq_ref, dk_ref, dv_ref, dct_ref, dcq_ref, dqa_ref):
        j = pl.program_id(1)
        m_lo, m_hi = _lane_masks()
        masks = (m_lo, m_hi)

        @pl.when(j == 0)
        def _():
            dqa_ref[...] = jnp.zeros_like(dqa_ref)
            dcq_ref[...] = jnp.zeros_like(dcq_ref)
        k = k_ref[...]
        v = v_ref[...]
        km = [jnp.where(mk, k, jnp.zeros_like(k)) for mk in masks]
        ct = ct_ref[...]
        row = lax.broadcasted_iota(jnp.int32, (TQ, TQ), 0)
        col = lax.broadcasted_iota(jnp.int32, (TQ, TQ), 1)

        def step(i, carry, masked):
            dk, dv, dcs = carry
            start = pl.multiple_of(i * TQ, TQ)
            q = q_ref[pl.ds(start, TQ), :]
            do = do_ref[pl.ds(start, TQ), :]
            lse = lse_ref[pl.ds(start, TQ), :]
            dl = dl_ref[pl.ds(start, TQ), :]
            cq = c_ref[pl.ds(start, TQ), :]
            dq = jnp.zeros((TQ, 128), F32)
            new_dcs = []
            rows = []
            for h in range(2):
                lo = HEAD * h
                qm = jnp.where(masks[h], q, jnp.zeros_like(q))
                dom = jnp.where(masks[h], do, jnp.zeros_like(do))
                s = _dot_nt(qm, k) * scale + (cq[:, lo:lo + 1] - ct[lo:lo + 1, :])
                p = jnp.exp(s - lse[:, lo:lo + 1])
                if masked:
                    p = jnp.where(row >= col, p, 0.0)
                dp = _dot_nt(dom, v)
                ds = p * (dp - dl[:, lo:lo + 1])
                new_dcs.append(dcs[h] + jnp.sum(ds, axis=0, keepdims=True))
                rows.append(jnp.sum(ds, axis=1, keepdims=True))
                dsb = (ds * scale).astype(BF16)
                dv = dv + _dot_tn(p.astype(BF16), dom)
                dk = dk + _dot_tn(dsb, qm)
                dq = dq + _dot(dsb, km[h])
            dqa_ref[pl.ds(start, TQ), :] += dq
            dcq_ref[pl.ds(start, TQ), :] += jnp.where(m_lo, rows[0], rows[1])
            return (dk, dv, tuple(new_dcs))

        init = (jnp.zeros((TQ, 128), F32), jnp.zeros((TQ, 128), F32), (jnp.zeros((1, TQ), F32),) * 2)
        carry = step(j, init, True)
        dk, dv, dcs = lax.fori_loop(j + 1, nq, lambda i, c: step(i, c, False), carry)
        dk_ref[...] = dk.astype(BF16)
        dv_ref[...] = dv.astype(BF16)
        dct_ref[...] = jnp.zeros_like(dct_ref)
        dct_ref[0:1, :] = -dcs[0]
        dct_ref[1:2, :] = -dcs[1]

        @pl.when(j == nq - 1)
        def _():
            dq_ref[...] = dqa_ref[...].astype(BF16)

    res = lambda p, j: (0, p)
    return _pallas(
        body, grid=(N_PAIR, nq), task=task,
        in_specs=[pl.BlockSpec((S, 128), res),
                  pl.BlockSpec((TQ, 128), lambda p, j: (j, N_PAIR + p)),
                  pl.BlockSpec((TQ, 128), lambda p, j: (j, 2 * N_PAIR + p)),
                  pl.BlockSpec((S, 128), res), pl.BlockSpec((S, 128), res), pl.BlockSpec((S, 128), res),
                  pl.BlockSpec((S, 128), res),
                  pl.BlockSpec((None, 128, TQ), lambda p, j: (j, p, 0))],
        out_specs=(pl.BlockSpec((S, 128), res),
                   pl.BlockSpec((TQ, 128), lambda p, j: (j, p)), pl.BlockSpec((TQ, 128), lambda p, j: (j, p)),
                   pl.BlockSpec((None, None, 8, TQ), lambda p, j: (p, j, 0, 0)),
                   pl.BlockSpec((S, 128), res)),
        out_shape=(jax.ShapeDtypeStruct((S, D_GRP), BF16), jax.ShapeDtypeStruct((S, D_GRP), BF16),
                   jax.ShapeDtypeStruct((S, D_GRP), BF16), jax.ShapeDtypeStruct((N_PAIR, nq, 8, TQ), F32),
                   jax.ShapeDtypeStruct((S, D_GRP), F32)),
        scratch_shapes=[pltpu.VMEM((S, 128), F32)],
        name=name,
    )(proj, proj, proj, do, lse_rep, delta_rep, c_rep, c_t)


def _fox_gate_bwd_call(dc_rows, fl_raw, b_pad, *, name):
    S = fl_raw.shape[0]
    nb = S // TQ

    def body(dc_ref, fl_ref, b_ref, dfl_ref, db_ref, carry_ref):
        i = pl.program_id(0)

        @pl.when(i == 0)
        def _():
            carry_ref[...] = jnp.zeros_like(carry_ref)
        rc = _dot3(dc_ref[...], _tri(TQ, True)) + carry_ref[:, 0:1]
        carry_ref[...] = jnp.broadcast_to(rc[:, 0:1], carry_ref.shape)
        fl = fl_ref[...] + b_ref[...]
        dfl = rc.T * jax.nn.sigmoid(-fl)
        dfl_ref[...] = dfl.astype(BF16)
        _acc_add(db_ref, jnp.sum(dfl, axis=0, keepdims=True), i)

    rev = lambda i: (nb - 1 - i, 0)
    return pl.pallas_call(
        body, grid=(nb,),
        in_specs=[pl.BlockSpec((128, TQ), lambda i: (0, nb - 1 - i)), pl.BlockSpec((TQ, 128), rev), _vec_spec(128)],
        out_specs=(pl.BlockSpec((TQ, 128), rev), _acc_spec(128)),
        out_shape=(jax.ShapeDtypeStruct((S, 128), BF16), jax.ShapeDtypeStruct((8, 128), F32)),
        scratch_shapes=[pltpu.VMEM((128, 128), F32)],
        name=name, compiler_params=_cparams(),
    )(dc_rows, fl_raw, b_pad)


def _chk_bias_call(g_rev, *, name):
    def body(g_ref, o_ref):
        x = jnp.broadcast_to(g_ref[...], (TQ, ROLL_W))
        rolled = pltpu.roll(x, ROLL_W - (TQ - 1), 1, stride=1, stride_axis=0)
        qc = lax.broadcasted_iota(jnp.int32, (TQ, WIN), 0) // CHUNK
        kc = lax.broadcasted_iota(jnp.int32, (TQ, WIN), 1) // CHUNK
        band = (kc >= qc) & (kc <= qc + LEFT)
        o_ref[...] = jnp.where(band, rolled[:, 0:WIN], NEG)

    return pl.pallas_call(
        body, grid=(8,),
        in_specs=[pl.BlockSpec((None, 1, ROLL_W), lambda h: (h, 0, 0))],
        out_specs=pl.BlockSpec((None, TQ, WIN), lambda h: (h, 0, 0)),
        out_shape=jax.ShapeDtypeStruct((8, TQ, WIN), F32), name=name, compiler_params=_cparams(),
    )(g_rev.reshape(8, 1, ROLL_W))


def _chk_scores(i, qm, kwin, bias, scale):
    s = _dot_nt(qm, kwin) * scale + bias
    kc = lax.broadcasted_iota(jnp.int32, (TQ, WIN), 1) // CHUNK
    return jnp.where(kc + i * (TQ // CHUNK) >= LEFT, s, NEG)


def _chk_fwd_call(proj, bias, *, name, task=None):
    S = proj.shape[0]
    nq = S // TQ
    scale = HEAD ** -0.5

    def body(q_ref, k_ref, v_ref, b_ref, o_ref, kp_ref, vp_ref):
        i = pl.program_id(1)

        @pl.when(i == 0)
        def _():
            kp_ref[0:PADK, :] = jnp.zeros((PADK, 128), BF16)
            vp_ref[0:PADK, :] = jnp.zeros((PADK, 128), BF16)
            kp_ref[PADK:PADK + S, :] = k_ref[...]
            vp_ref[PADK:PADK + S, :] = v_ref[...]
        masks = _lane_masks()
        q = q_ref[...]
        start = pl.multiple_of(i * TQ, TQ)
        kwin = kp_ref[pl.ds(start, WIN), :]
        vwin = vp_ref[pl.ds(start, WIN), :]
        out = jnp.zeros((TQ, 128), F32)
        for h in range(2):
            qm = jnp.where(masks[h], q, jnp.zeros_like(q))
            s = _chk_scores(i, qm, kwin, b_ref[h], scale)
            p = jnp.exp(s - jnp.max(s, axis=1, keepdims=True))
            p = p / jnp.sum(p, axis=1, keepdims=True)
            out = out + _dot(p.astype(BF16), jnp.where(masks[h], vwin, jnp.zeros_like(vwin)))
        o_ref[...] = out

    c0 = 3 * N_PAIR
    return _pallas(
        body, grid=(N_PAIR, nq), task=task,
        in_specs=[pl.BlockSpec((TQ, 128), lambda p, i: (i, c0 + p)),
                  pl.BlockSpec((S, 128), lambda p, i: (0, c0 + N_PAIR + p)),
                  pl.BlockSpec((S, 128), lambda p, i: (0, c0 + 2 * N_PAIR + p)),
                  pl.BlockSpec((2, TQ, WIN), lambda p, i: (p, 0, 0))],
        out_specs=pl.BlockSpec((TQ, 128), lambda p, i: (i, p)),
        out_shape=jax.ShapeDtypeStruct((S, D_GRP), F32),
        scratch_shapes=[pltpu.VMEM((S + PADK, 128), BF16), pltpu.VMEM((S + PADK, 128), BF16)],
        name=name,
    )(proj, proj, proj, bias)


def _chk_bwd_call(proj, do, bias, *, name, task=None):
    S = proj.shape[0]
    nq = S // TQ
    scale = HEAD ** -0.5

    def body(q_ref, k_ref, v_ref, do_ref, b_ref, dq_ref, dk_ref, dv_ref, dg_ref, kp_ref, vp_ref, dkp_ref, dvp_ref, db_ref):
        i = pl.program_id(1)

        @pl.when(i == 0)
        def _():
            kp_ref[0:PADK, :] = jnp.zeros((PADK, 128), BF16)
            vp_ref[0:PADK, :] = jnp.zeros((PADK, 128), BF16)
            kp_ref[PADK:PADK + S, :] = k_ref[...]
            vp_ref[PADK:PADK + S, :] = v_ref[...]
            dkp_ref[...] = jnp.zeros_like(dkp_ref)
            dvp_ref[...] = jnp.zeros_like(dvp_ref)
            db_ref[...] = jnp.zeros_like(db_ref)
        masks = _lane_masks()
        q = q_ref[...]
        dout = do_ref[...]
        start = pl.multiple_of(i * TQ, TQ)
        kwin = kp_ref[pl.ds(start, WIN), :]
        vwin = vp_ref[pl.ds(start, WIN), :]
        dq = jnp.zeros((TQ, 128), F32)
        dkw = jnp.zeros((WIN, 128), F32)
        dvw = jnp.zeros((WIN, 128), F32)
        for h in range(2):
            qm = jnp.where(masks[h], q, jnp.zeros_like(q))
            dom = jnp.where(masks[h], dout, jnp.zeros_like(dout))
            s = _chk_scores(i, qm, kwin, b_ref[h], scale)
            p = jnp.exp(s - jnp.max(s, axis=1, keepdims=True))
            p = p / jnp.sum(p, axis=1, keepdims=True)
            dp = _dot_nt(dom, vwin)
            ds = p * (dp - jnp.sum(p * dp, axis=1, keepdims=True))
            db_ref[h] += ds
            dsb = (ds * scale).astype(BF16)
            dq = dq + _dot(dsb, jnp.where(masks[h], kwin, jnp.zeros_like(kwin)))
            dkw = dkw + _dot_tn(dsb, qm)
            dvw = dvw + _dot_tn(p.astype(BF16), dom)
        dq_ref[...] = dq.astype(BF16)
        dkp_ref[pl.ds(start, WIN), :] += dkw
        dvp_ref[pl.ds(start, WIN), :] += dvw

        @pl.when(i == nq - 1)
        def _():
            dk_ref[...] = dkp_ref[PADK:PADK + S, :].astype(BF16)
            dv_ref[...] = dvp_ref[PADK:PADK + S, :].astype(BF16)
            a = lax.broadcasted_iota(jnp.int32, (TQ, TQ), 0)
            b = lax.broadcasted_iota(jnp.int32, (TQ, TQ), 1)
            flip = jnp.where(a + b == TQ - 1, 1.0, 0.0).astype(BF16)
            e = lax.broadcasted_iota(jnp.int32, (1, ROLL_W), 1)
            dg_ref[...] = jnp.zeros_like(dg_ref)
            for h in range(2):
                rev = _dot3_l(flip, db_ref[h])
                wide = jnp.concatenate([rev, jnp.zeros((TQ, ROLL_W - WIN), F32)], axis=1)
                diag = pltpu.roll(wide, 0, 1, stride=1, stride_axis=0)
                dg = jnp.sum(diag, axis=0, keepdims=True)
                lo = jnp.sum(jnp.where(e <= 639, dg, 0.0), axis=1, keepdims=True)
                hi = jnp.sum(jnp.where(e >= 895, dg, 0.0), axis=1, keepdims=True)
                dg_ref[h:h + 1, :] = jnp.where(e == 639, lo, jnp.where(e == 895, hi, dg))

    c0 = 3 * N_PAIR
    res = lambda p, i: (0, p)
    return _pallas(
        body, grid=(N_PAIR, nq), task=task,
        in_specs=[pl.BlockSpec((TQ, 128), lambda p, i: (i, c0 + p)),
                  pl.BlockSpec((S, 128), lambda p, i: (0, c0 + N_PAIR + p)),
                  pl.BlockSpec((S, 128), lambda p, i: (0, c0 + 2 * N_PAIR + p)),
                  pl.BlockSpec((TQ, 128), lambda p, i: (i, p)),
                  pl.BlockSpec((2, TQ, WIN), lambda p, i: (p, 0, 0))],
        out_specs=(pl.BlockSpec((TQ, 128), lambda p, i: (i, p)), pl.BlockSpec((S, 128), res),
                   pl.BlockSpec((S, 128), res), pl.BlockSpec((None, 8, ROLL_W), lambda p, i: (p, 0, 0))),
        out_shape=(jax.ShapeDtypeStruct((S, D_GRP), BF16), jax.ShapeDtypeStruct((S, D_GRP), BF16),
                   jax.ShapeDtypeStruct((S, D_GRP), BF16), jax.ShapeDtypeStruct((N_PAIR, 8, ROLL_W), F32)),
        scratch_shapes=[pltpu.VMEM((S + PADK, 128), BF16), pltpu.VMEM((S + PADK, 128), BF16),
                        pltpu.VMEM((S + PADK, 128), F32), pltpu.VMEM((S + PADK, 128), F32),
                        pltpu.VMEM((2, TQ, WIN), F32)],
        name=name,
    )(proj, proj, proj, do, bias)


def _mem_fwd_call(q, k, v, *, name, tq=512):
    S = q.shape[0]
    scale = MEM_HD ** -0.5

    def body(q_ref, k_ref, v_ref, o_ref):
        s = _dot_nt(q_ref[...], k_ref[...]) * scale
        p = jnp.exp(s - jnp.max(s, axis=1, keepdims=True))
        p = p / jnp.sum(p, axis=1, keepdims=True)
        o_ref[...] = _dot(p.astype(BF16), v_ref[...]).astype(BF16)

    return pl.pallas_call(
        body, grid=(MEM_HEADS, S // tq),
        in_specs=[pl.BlockSpec((tq, MEM_HD), lambda h, i: (i, h)),
                  pl.BlockSpec((N_MEM, MEM_HD), lambda h, i: (0, h)),
                  pl.BlockSpec((N_MEM, MEM_HD), lambda h, i: (0, h))],
        out_specs=pl.BlockSpec((tq, MEM_HD), lambda h, i: (i, h)),
        out_shape=jax.ShapeDtypeStruct((S, D), BF16), name=name, compiler_params=_cparams(),
    )(q, k, v)


def _mem_bwd_call(q, k, v, do, *, name, tq=512):
    S = q.shape[0]
    n = S // tq
    scale = MEM_HD ** -0.5

    def body(q_ref, k_ref, v_ref, do_ref, dq_ref, dk_ref, dv_ref, dka_ref, dva_ref):
        i = pl.program_id(1)

        @pl.when(i == 0)
        def _():
            dka_ref[...] = jnp.zeros_like(dka_ref)
            dva_ref[...] = jnp.zeros_like(dva_ref)
        qb = q_ref[...]
        kb = k_ref[...]
        dob = do_ref[...]
        s = _dot_nt(qb, kb) * scale
        p = jnp.exp(s - jnp.max(s, axis=1, keepdims=True))
        p = p / jnp.sum(p, axis=1, keepdims=True)
        dp = _dot_nt(dob, v_ref[...])
        ds = p * (dp - jnp.sum(p * dp, axis=1, keepdims=True))
        dsb = (ds * scale).astype(BF16)
        dq_ref[...] = _dot(dsb, kb).astype(BF16)
        dka_ref[...] += _dot_tn(dsb, qb)
        dva_ref[...] += _dot_tn(p.astype(BF16), dob)

        @pl.when(i == n - 1)
        def _():
            dk_ref[...] = dka_ref[...].astype(BF16)
            dv_ref[...] = dva_ref[...].astype(BF16)

    kv = pl.BlockSpec((N_MEM, MEM_HD), lambda h, i: (0, h))
    qs = pl.BlockSpec((tq, MEM_HD), lambda h, i: (i, h))
    return pl.pallas_call(
        body, grid=(MEM_HEADS, n), in_specs=[qs, kv, kv, qs], out_specs=(qs, kv, kv),
        out_shape=(jax.ShapeDtypeStruct((S, D), BF16), jax.ShapeDtypeStruct((N_MEM, D), BF16),
                   jax.ShapeDtypeStruct((N_MEM, D), BF16)),
        scratch_shapes=[pltpu.VMEM((N_MEM, MEM_HD), F32), pltpu.VMEM((N_MEM, MEM_HD), F32)],
        name=name, compiler_params=_cparams(),
    )(q, k, v, do)


def _rel_table_to_g(rel):
    return jnp.concatenate([
        jnp.broadcast_to(rel[:, N_REL - 1:N_REL], (8, 640)),
        rel[:, 1:N_REL - 1][:, ::-1],
        jnp.broadcast_to(rel[:, 0:1], (8, 129)),
    ], axis=1)


def _g_to_rel_table(dg):
    return dg[:, 639:896][:, ::-1]


def _place():
    x, y, c = lax.axis_index("x"), lax.axis_index("y"), lax.axis_index("c")
    others = [(1 - x, y), (x, 1 - y), (1 - x, 1 - y)]
    return x, y, c, others


def _half(c, rows):
    hr = rows // 2
    return pl.ds(pl.multiple_of(c * hr, 16), hr)


def _dma_sems(*shape):
    return pltpu.SemaphoreType.DMA(shape)


def _ag_ici_task(shards):
    n = len(shards)

    def copies(ins, outs, sems):
        send_sems, recv_sems, loc_sems = sems
        x, y, c, others = _place()
        me = 2 * x + y
        for k in range(n):
            mine = _half(c, shards[k].shape[0])
            local = pltpu.make_async_copy(ins[k], outs[k].at[me], loc_sems.at[k])
            outgoing, incoming = [], []
            for t, (ox, oy) in enumerate(others):
                for lst, slab in ((outgoing, me), (incoming, 2 * ox + oy)):
                    lst.append(pltpu.make_async_remote_copy(
                        src_ref=ins[k].at[mine], dst_ref=outs[k].at[slab, mine],
                        send_sem=send_sems.at[k, t], recv_sem=recv_sems.at[k, t],
                        device_id=(ox, oy, c), device_id_type=MESH))
            yield local, outgoing, incoming

    def issue(ins, outs, sems):
        for local, outgoing, _ in copies(ins, outs, sems):
            local.start()
            for cp in outgoing:
                cp.start()

    def drain(ins, outs, sems):
        for local, outgoing, incoming in copies(ins, outs, sems):
            for cp in incoming:
                cp.wait_recv()
            for cp in outgoing:
                cp.wait_send()
            local.wait()

    return _Task(shards, [jax.ShapeDtypeStruct((N_CHIP,) + s.shape, s.dtype) for s in shards],
                 [_dma_sems(n, 3), _dma_sems(n, 3), _dma_sems(n)], issue, drain)


def _ag_d2d_task(gathered):
    n = len(gathered)

    def copies(ins, outs, sems):
        send_sems, recv_sems = sems
        x, y, c, others = _place()
        for k in range(n):
            rows = gathered[k].shape[1]
            mine, theirs = _half(c, rows), _half(1 - c, rows)
            for t, (ox, oy) in enumerate(others):
                slab = 2 * ox + oy
                pair = [pltpu.make_async_remote_copy(
                    src_ref=ins[k].at[slab, half], dst_ref=outs[k].at[slab, half],
                    send_sem=send_sems.at[k, t], recv_sem=recv_sems.at[k, t],
                    device_id=(x, y, 1 - c), device_id_type=MESH) for half in (mine, theirs)]
                yield pair

    def issue(ins, outs, sems):
        for outgoing, _ in copies(ins, outs, sems):
            outgoing.start()

    def drain(ins, outs, sems):
        for outgoing, incoming in copies(ins, outs, sems):
            incoming.wait_recv()
            outgoing.wait_send()

    return _Task(gathered, [jax.ShapeDtypeStruct(g.shape, g.dtype) for g in gathered],
                 [_dma_sems(n, 3), _dma_sems(n, 3)], issue, drain, aliases={k: k for k in range(n)})


def _rs_pair_task(ds):
    n = len(ds)

    def copies(ins, outs, sems):
        send_sems, recv_sems = sems
        x, y, c, _ = _place()
        for k in range(n):
            yield pltpu.make_async_remote_copy(
                src_ref=ins[k].at[:, _half(1 - c, ds[k].shape[1])], dst_ref=outs[k],
                send_sem=send_sems.at[k], recv_sem=recv_sems.at[k],
                device_id=(x, y, 1 - c), device_id_type=MESH)

    def issue(ins, outs, sems):
        for cp in copies(ins, outs, sems):
            cp.start()

    def drain(ins, outs, sems):
        for cp in copies(ins, outs, sems):
            cp.wait()

    return _Task(ds, [jax.ShapeDtypeStruct((N_CHIP, d.shape[1] // 2, d.shape[2]), d.dtype) for d in ds],
                 [_dma_sems(n), _dma_sems(n)], issue, drain)


def _pair_add_call(d, r1, c_arr, *, name, tm=256):
    _, rows, cols = d.shape
    hr = rows // 2
    tm = min(tm, hr)
    nb = hr // tm

    def body(c_ref, d_ref, r_ref, o_ref):
        o_ref[...] = (d_ref[...].astype(F32) + r_ref[...].astype(F32)).astype(BF16)

    return pl.pallas_call(
        body,
        grid_spec=pltpu.PrefetchScalarGridSpec(
            num_scalar_prefetch=1, grid=(N_CHIP, nb),
            in_specs=[pl.BlockSpec((None, tm, cols), lambda j, i, c: (j, c[0] * nb + i, 0)),
                      pl.BlockSpec((None, tm, cols), lambda j, i, c: (j, i, 0))],
            out_specs=pl.BlockSpec((None, tm, cols), lambda j, i, c: (j, i, 0))),
        out_shape=jax.ShapeDtypeStruct((N_CHIP, hr, cols), BF16), name=name, compiler_params=_cparams(),
    )(c_arr, d, r1)


def _rs_chip_task(ps):
    n = len(ps)

    def copies(ins, outs, sems):
        send_sems, recv_sems, loc_sems = sems
        x, y, c, others = _place()
        me = 2 * x + y
        for k in range(n):
            local = pltpu.make_async_copy(ins[k].at[me], outs[k].at[me], loc_sems.at[k])
            outgoing, incoming = [], []
            for t, (ox, oy) in enumerate(others):
                oj = 2 * ox + oy
                for lst, slot in ((outgoing, me), (incoming, oj)):
                    lst.append(pltpu.make_async_remote_copy(
                        src_ref=ins[k].at[oj], dst_ref=outs[k].at[slot],
                        send_sem=send_sems.at[k, t], recv_sem=recv_sems.at[k, t],
                        device_id=(ox, oy, c), device_id_type=MESH))
            yield local, outgoing, incoming

    def issue(ins, outs, sems):
        for local, outgoing, _ in copies(ins, outs, sems):
            local.start()
            for cp in outgoing:
                cp.start()

    def drain(ins, outs, sems):
        for local, outgoing, incoming in copies(ins, outs, sems):
            for cp in incoming:
                cp.wait_recv()
            for cp in outgoing:
                cp.wait_send()
            local.wait()

    return _Task(ps, [jax.ShapeDtypeStruct(p.shape, p.dtype) for p in ps],
                 [_dma_sems(n, 3), _dma_sems(n, 3), _dma_sems(n)], issue, drain)


def _chip_sum_call(r2, *, name, tm=256):
    _, hr, cols = r2.shape
    tm = min(tm, hr)

    def body(r_ref, o_ref):
        acc = r_ref[0].astype(F32)
        for j in range(1, N_CHIP):
            acc = acc + r_ref[j].astype(F32)
        o_ref[...] = acc

    return pl.pallas_call(
        body, grid=(hr // tm,),
        in_specs=[pl.BlockSpec((N_CHIP, tm, cols), lambda i: (0, i, 0))],
        out_specs=pl.BlockSpec((tm, cols), lambda i: (i, 0)),
        out_shape=jax.ShapeDtypeStruct((hr, cols), F32), name=name, compiler_params=_cparams(),
    )(r2)


def _rs_gather_task(fs):
    n = len(fs)

    def copies(ins, outs, sems):
        send_sems, recv_sems, loc_sems = sems
        x, y, c, _ = _place()
        for k in range(n):
            rows = 2 * fs[k].shape[0]
            mine, theirs = _half(c, rows), _half(1 - c, rows)
            local = pltpu.make_async_copy(ins[k], outs[k].at[mine], loc_sems.at[k])
            outgoing, incoming = [pltpu.make_async_remote_copy(
                src_ref=ins[k], dst_ref=outs[k].at[half],
                send_sem=send_sems.at[k], recv_sem=recv_sems.at[k],
                device_id=(x, y, 1 - c), device_id_type=MESH) for half in (mine, theirs)]
            yield local, outgoing, incoming

    def issue(ins, outs, sems):
        for local, outgoing, _ in copies(ins, outs, sems):
            local.start()
            outgoing.start()

    def drain(ins, outs, sems):
        for local, outgoing, incoming in copies(ins, outs, sems):
            incoming.wait_recv()
            outgoing.wait_send()
            local.wait()

    return _Task(fs, [jax.ShapeDtypeStruct((2 * f.shape[0], f.shape[1]), f.dtype) for f in fs],
                 [_dma_sems(n), _dma_sems(n), _dma_sems(n)], issue, drain)


def _adamw(w, g, m, v):
    m = ADAM_B1 * m + (1.0 - ADAM_B1) * g
    v = ADAM_B2 * v + (1.0 - ADAM_B2) * jnp.square(g)
    m_hat = m / (1.0 - ADAM_B1 ** ADAM_STEP)
    v_hat = v / (1.0 - ADAM_B2 ** ADAM_STEP)
    delta = -ADAM_LR * (m_hat / (jnp.sqrt(v_hat) + ADAM_EPS) + ADAM_WD * w)
    return delta, m, v


def _adamw_call(items, *, name, tm=256, task=None):
    n = len(items)
    cols = items[0][0].shape[1]
    tiles = [it[0].shape[0] // tm for it in items]
    steps = max(tiles)

    def body(*refs):
        i = pl.program_id(0)
        ins, outs = refs[:4 * n], refs[4 * n:]
        for k in range(n):
            def update(k=k):
                res = _adamw(*(ins[4 * k + j][...] for j in range(4)))
                for j in range(3):
                    outs[3 * k + j][...] = res[j]
            if tiles[k] == steps:
                update()
            else:
                pl.when(i < tiles[k])(update)

    in_specs, out_specs, out_shape, args = [], [], [], []
    for it, t in zip(items, tiles):
        spec = pl.BlockSpec((tm, cols), lambda i, t=t: (jnp.minimum(i, t - 1), 0))
        in_specs += [spec] * 4
        out_specs += [spec] * 3
        out_shape += [jax.ShapeDtypeStruct(it[0].shape, F32)] * 3
        args += list(it)
    res = _pallas(body, grid=(steps,), in_specs=in_specs, out_specs=out_specs, out_shape=out_shape,
                  name=name, task=task)(*args)
    outs, extra = res if task is not None else (res, None)
    grouped = [tuple(outs[3 * k:3 * k + 3]) for k in range(n)]
    return (grouped, extra) if task is not None else grouped


N_DEV = 8
SMALL_ROWS = 24
SMALL_LAYOUT = {
    "g_mix_pre": (0, 0, 1, D), "g_mix_post": (1, 0, 1, D), "g_mem_kv": (2, 0, 1, D), "g_mem_pre": (3, 0, 1, D),
    "g_mem_post": (4, 0, 1, D), "g_ff_pre": (5, 0, 1, D), "g_ff_post": (6, 0, 1, D),
    "g_fox_out": (7, 0, 1, D_GRP), "g_chk_out": (7, D_GRP, 1, D_GRP), "b_fgt": (8, 0, 1, 8),
    "rel_bias": (16, 0, 8, N_REL),
}
SMALL = list(SMALL_LAYOUT)


def _small_call(grads, ws, ms, vs, *, name):
    n = len(SMALL)

    def body(*refs):
        g_refs, w_refs, m_refs, v_refs = (refs[j * n:(j + 1) * n] for j in range(4))
        outs = refs[4 * n:8 * n]
        mine, slots, send_sems, recv_sems = refs[8 * n:]
        x, y, c, _ = _place()
        me = 4 * x + 2 * y + c
        mine[...] = jnp.zeros_like(mine)
        for k, name_k in enumerate(SMALL):
            r, l, nr, nl = SMALL_LAYOUT[name_k]
            mine[r:r + nr, l:l + nl] = g_refs[k][0:nr, 0:nl]
        slots[me] = mine[...]
        peers = [(dx, dy, dc) for dx in (0, 1) for dy in (0, 1) for dc in (0, 1)][1:]
        cps = []
        for t, (dx, dy, dc) in enumerate(peers):
            px, py, pc = (x + dx) % 2, (y + dy) % 2, (c + dc) % 2
            cps.append(pltpu.make_async_remote_copy(
                src_ref=mine, dst_ref=slots.at[me], send_sem=send_sems.at[t], recv_sem=recv_sems.at[t],
                device_id=(px, py, pc), device_id_type=MESH))
            cps[-1].start()
        for t, (dx, dy, dc) in enumerate(peers):
            px, py, pc = (x + dx) % 2, (y + dy) % 2, (c + dc) % 2
            pltpu.make_async_remote_copy(
                src_ref=mine, dst_ref=slots.at[4 * px + 2 * py + pc], send_sem=send_sems.at[t],
                recv_sem=recv_sems.at[t], device_id=(px, py, pc), device_id_type=MESH).wait_recv()
        for cp in cps:
            cp.wait_send()
        total = slots[0]
        for j in range(1, N_DEV):
            total = total + slots[j]
        for k, name_k in enumerate(SMALL):
            r, l, nr, nl = SMALL_LAYOUT[name_k]
            g = total[r:r + nr, l:l + nl]
            d, mn, vn = _adamw(w_refs[k][...], g, m_refs[k][...], v_refs[k][...])
            for j, val in enumerate((g, d, mn, vn)):
                outs[4 * k + j][...] = val

    vm = pl.BlockSpec(memory_space=pltpu.VMEM)
    out_shape = [jax.ShapeDtypeStruct(ws[k].shape, F32) for k in SMALL for _ in range(4)]
    res = pl.pallas_call(
        body, in_specs=[vm] * (4 * n), out_specs=[vm] * (4 * n), out_shape=out_shape,
        scratch_shapes=[pltpu.VMEM((SMALL_ROWS, D), F32), pltpu.VMEM((N_DEV, SMALL_ROWS, D), F32),
                        _dma_sems(N_DEV - 1), _dma_sems(N_DEV - 1)],
        name=name,
    )(*[d[k] for d in (grads, ws, ms, vs) for k in SMALL])
    return {k: tuple(res[4 * i:4 * i + 4]) for i, k in enumerate(SMALL)}


WEIGHTS = ["w_in", "b_fgt", "rel_bias", "g_fox_out", "g_chk_out", "w_out", "g_mix_pre", "g_mix_post", "g_mem_kv",
           "w_mq", "w_mk", "w_mv", "w_mo", "g_mem_pre", "g_mem_post", "w_ff1", "w_ff2", "g_ff_pre", "g_ff_post"]
BIG = ["w_in", "w_out", "w_mq", "w_mk", "w_mv", "w_mo", "w_ff1", "w_ff2"]


def _in_cols_to_all(w):
    pad = jnp.zeros(w.shape[:-1] + (D_ALL - D_IN,), w.dtype)
    return jnp.concatenate([w[..., :1536], w[..., 1544:D_IN], w[..., 1536:1544], pad], axis=-1)


def _all_cols_to_in(w):
    return jnp.concatenate([w[..., :1536], w[..., 3072:3080], w[..., 1536:3072]], axis=-1)


def kernel(x, mem, w_in, b_fgt, rel_bias, g_fox_out, g_chk_out, w_out, g_mix_pre, g_mix_post, g_mem_kv, w_mq, w_mk, w_mv, w_mo, g_mem_pre, g_mem_post, w_ff1, w_ff2, g_ff_pre, g_ff_post, loss_target, m_w_in, m_b_fgt, m_rel_bias, m_g_fox_out, m_g_chk_out, m_w_out, m_g_mix_pre, m_g_mix_post, m_g_mem_kv, m_w_mq, m_w_mk, m_w_mv, m_w_mo, m_g_mem_pre, m_g_mem_post, m_w_ff1, m_w_ff2, m_g_ff_pre, m_g_ff_post, v_w_in, v_b_fgt, v_rel_bias, v_g_fox_out, v_g_chk_out, v_w_out, v_g_mix_pre, v_g_mix_post, v_g_mem_kv, v_w_mq, v_w_mk, v_w_mv, v_w_mo, v_g_mem_pre, v_g_mem_post, v_w_ff1, v_w_ff2, v_g_ff_pre, v_g_ff_post):
    w = dict(w_in=w_in, b_fgt=b_fgt, rel_bias=rel_bias, g_fox_out=g_fox_out, g_chk_out=g_chk_out, w_out=w_out,
             g_mix_pre=g_mix_pre, g_mix_post=g_mix_post, g_mem_kv=g_mem_kv, w_mq=w_mq, w_mk=w_mk, w_mv=w_mv,
             w_mo=w_mo, g_mem_pre=g_mem_pre, g_mem_post=g_mem_post, w_ff1=w_ff1, w_ff2=w_ff2, g_ff_pre=g_ff_pre,
             g_ff_post=g_ff_post)
    m = dict(w_in=m_w_in, b_fgt=m_b_fgt, rel_bias=m_rel_bias, g_fox_out=m_g_fox_out, g_chk_out=m_g_chk_out,
             w_out=m_w_out, g_mix_pre=m_g_mix_pre, g_mix_post=m_g_mix_post, g_mem_kv=m_g_mem_kv, w_mq=m_w_mq,
             w_mk=m_w_mk, w_mv=m_w_mv, w_mo=m_w_mo, g_mem_pre=m_g_mem_pre, g_mem_post=m_g_mem_post,
             w_ff1=m_w_ff1, w_ff2=m_w_ff2, g_ff_pre=m_g_ff_pre, g_ff_post=m_g_ff_post)
    v = dict(w_in=v_w_in, b_fgt=v_b_fgt, rel_bias=v_rel_bias, g_fox_out=v_g_fox_out, g_chk_out=v_g_chk_out,
             w_out=v_w_out, g_mix_pre=v_g_mix_pre, g_mix_post=v_g_mix_post, g_mem_kv=v_g_mem_kv, w_mq=v_w_mq,
             w_mk=v_w_mk, w_mv=v_w_mv, w_mo=v_w_mo, g_mem_pre=v_g_mem_pre, g_mem_post=v_g_mem_post,
             w_ff1=v_w_ff1, w_ff2=v_w_ff2, g_ff_pre=v_g_ff_pre, g_ff_post=v_g_ff_post)

    def rows(d, k):
        return d[k][0] if k == "rel_bias" else d[k]

    xs, mems, target = x[0], mem[0], loss_target[0]
    S = xs.shape[0]
    sp = {k: rows(w, k) for k in SMALL}
    b_pad = jnp.pad(sp["b_fgt"], ((0, 0), (0, 120)))
    bf = {k: w[k][0].astype(BF16) for k in BIG}
    c_arr = jnp.reshape(lax.axis_index("c"), (1,)).astype(jnp.int32)

    def gather_ici(names):
        return _ag_ici_task([bf[k] for k in names])

    def pair_add(k, d, r1):
        return _pair_add_call(d, r1, c_arr, name="rs_pair_add_" + k)

    g_in, = _comm_call(gather_ici(["w_in"]), name="ag_w_in")
    h1, (g_in,) = _rms_fwd_call(xs, sp["g_mix_pre"], name="rms_mix_pre", task=_ag_d2d_task([g_in]))
    w_all = _in_cols_to_all(g_in.transpose(1, 0, 2).reshape(D, D_IN))
    proj, (g_out,) = _mm_nn(h1, w_all[:, :3072], "plain", name="mm_proj", task=gather_ici(["w_out"]))
    fl_raw = _mm_nn(h1, w_all[:, 3072:], "plain", name="mm_gate", out_dtype=F32, tn=128)
    c_rep, c_t = _fox_prep_call(fl_raw, b_pad, name="fox_prep")
    bias = _chk_bias_call(_rel_table_to_g(sp["rel_bias"]), name="chk_bias")
    mid = ["w_mq", "w_mk", "w_mv", "w_mo", "w_ff1"]
    (yf, lse), got = _fox_fwd_call(proj, c_rep, c_t, name="fox_fwd",
                                   task=_merge_tasks([gather_ici(mid), _ag_d2d_task([g_out])]))
    g_mid, g_out = got[:5], got[5]
    yc, got = _chk_fwd_call(proj, bias, name="chk_fwd",
                            task=_merge_tasks([gather_ici(["w_ff2"]), _ag_d2d_task(g_mid)]))
    g_ff2, (g_mq, g_mk, g_mv, g_mo, g_ff1) = got[0], got[1:]
    yn = _mix_norm_fwd_call(yf, yc, sp["g_fox_out"], sp["g_chk_out"], name="mix_norm_fwd")
    z, (g_ff2,) = _mm_nn(yn, g_out, "rows", name="mm_out", out_dtype=F32, task=_ag_d2d_task([g_ff2]))
    x1, h2 = _post_pre_call(xs, z, sp["g_mix_post"], sp["g_mem_pre"], name="post_mix")
    memn = _rms_fwd_call(mems, sp["g_mem_kv"], name="rms_mem_kv")
    q2 = _mm_nn(h2, g_mq, "rows", name="mm_mq")
    k2 = _mm_nn(memn, g_mk, "rows", name="mm_mk")
    v2 = _mm_nn(memn, g_mv, "rows", name="mm_mv")
    o2 = _mem_fwd_call(q2, k2, v2, name="mem_fwd")
    y2 = _mm_nn(o2, g_mo, "rows", name="mm_mo", out_dtype=F32)
    x2, h3 = _post_pre_call(x1, y2, sp["g_mem_post"], sp["g_ff_pre"], name="post_mem")
    act, relu = _mm_nn(h3, g_ff1, "cols", name="mm_ff1", epi="relu2")
    y3 = _mm_nn(act, g_ff2, "rows", name="mm_ff2", out_dtype=F32)
    loss_blk, dx3, dy3, dg_ff_post = _final_call(x2, y3, sp["g_ff_post"], target, name="final")
    loss = lax.psum(loss_blk[0, 0], ("x", "y", "c"))

    d_ff2 = _mm_tn(act, dy3, name="mm_dff2").reshape(N_CHIP, D_FF // N_CHIP, D)
    du, (r1,) = _mm_nt(dy3, g_ff2, "rows", name="mm_du", mul2r=relu, task=_rs_pair_task([d_ff2]))
    p_ff2 = pair_add("w_ff2", d_ff2, r1)
    d_ff1 = _mm_tn(h3, du, name="mm_dff1", cols4=True)
    dh3, (r1,) = _mm_nt(du, g_ff1, "cols", name="mm_dh3", out_dtype=F32, task=_rs_pair_task([d_ff1]))
    p_ff1 = pair_add("w_ff1", d_ff1, r1)
    dx2, dy2, dg_ff_pre, dg_mem_post = _bwd_mid_call(dx3, x2, dh3, sp["g_ff_pre"], y2, sp["g_mem_post"], name="bwd_ff")
    d_mo = _mm_tn(o2, dy2, name="mm_dmo").reshape(N_CHIP, D // N_CHIP, D)
    do2 = _mm_nt(dy2, g_mo, "rows", name="mm_do2")
    dq2, dk2, dv2 = _mem_bwd_call(q2, k2, v2, do2, name="mem_bwd")
    d_mq = _mm_tn(h2, dq2, name="mm_dmq").reshape(N_CHIP, D // N_CHIP, D)
    dh2 = _mm_nt(dq2, g_mq, "rows", name="mm_dh2", out_dtype=F32)
    d_mk = _mm_tn(memn, dk2, name="mm_dmk").reshape(N_CHIP, D // N_CHIP, D)
    d_mv = _mm_tn(memn, dv2, name="mm_dmv").reshape(N_CHIP, D // N_CHIP, D)
    dmn_k = _mm_nt(dk2, g_mk, "rows", name="mm_dmemk", out_dtype=F32)
    dmn_v = _mm_nt(dv2, g_mv, "rows", name="mm_dmemv", out_dtype=F32)
    dg_mem_kv = _gain_grad_call(mems, sp["g_mem_kv"], dmn_k, dmn_v, name="gain_mem_kv")
    dx1, dz, dg_mem_pre, dg_mix_post = _bwd_mid_call(dx2, x1, dh2, sp["g_mem_pre"], z, sp["g_mix_post"], name="bwd_mem")
    d_out = _mm_tn(yn, dz, name="mm_dout").reshape(N_CHIP, D // N_CHIP, D)
    late = ["w_mo", "w_mq", "w_mk", "w_mv", "w_out"]
    d_late = [d_mo, d_mq, d_mk, d_mv, d_out]
    dyn, r1_late = _mm_nt(dz, g_out, "rows", name="mm_dyn", out_dtype=F32, task=_rs_pair_task(d_late))
    p_late = [pair_add(k, d, r1) for k, d, r1 in zip(late, d_late, r1_late)]
    dof, doc, delta, dg_fox, dg_chk = _mix_norm_bwd_call(dyn, yf, yc, sp["g_fox_out"], sp["g_chk_out"], name="mix_norm_bwd")
    (dqf, dkf, dvf, dct, dcq), r2_ff = _fox_bwd_call(proj, dof, lse, delta, c_rep, c_t, name="fox_bwd",
                                                      task=_rs_chip_task([p_ff2, p_ff1]))
    (dqc, dkc, dvc, dgrev), r2_late = _chk_bwd_call(proj, doc, bias, name="chk_bwd", task=_rs_chip_task(p_late))
    first = ["w_ff2", "w_ff1"] + late
    f_first = [_chip_sum_call(r, name="rs_chip_sum_" + k) for k, r in zip(first, r2_ff + r2_late)]
    dc8 = dct[:, :, 0:2, :].transpose(0, 2, 1, 3).reshape(8, S) + dcq[:, ::HEAD].T
    dc_rows = jnp.concatenate([dc8, jnp.zeros((120, S), F32)], axis=0)
    dfl, db_fgt = _fox_gate_bwd_call(dc_rows, fl_raw, b_pad, name="fox_gate_bwd")
    dproj = jnp.concatenate([dqf, dkf, dvf, dqc, dkc, dvc, dfl], axis=1)
    d_all, g_first = _mm_tn(h1, dproj, name="mm_dwin", tn=640, task=_rs_gather_task(f_first))
    grads = dict(zip(first, g_first))
    d_in = _all_cols_to_in(d_all).reshape(D, N_CHIP, D_IN // N_CHIP).transpose(1, 0, 2)
    dh1, (r1,) = _mm_nt(dproj, w_all, "plain", name="mm_dh1", out_dtype=F32, task=_rs_pair_task([d_in]))
    p_in = pair_add("w_in", d_in, r1)
    delta_w, new_m, new_v = {}, {}, {}
    upd, (r2_in,) = _adamw_call([(w[k][0], grads[k], m[k][0], v[k][0]) for k in first], name="adamw_7", tm=64,
                                task=_rs_chip_task([p_in]))
    for k, (dk_, mk_, vk_) in zip(first, upd):
        delta_w[k], new_m[k], new_v[k] = dk_, mk_, vk_
    f_in = _chip_sum_call(r2_in, name="rs_chip_sum_w_in")
    (grad_x, dg_mix_pre), (g_w_in,) = _bwd_last_call(dx1, xs, dh1, sp["g_mix_pre"], name="bwd_mix",
                                                     task=_rs_gather_task([f_in]))
    grads["w_in"] = g_w_in
    (delta_w["w_in"], new_m["w_in"], new_v["w_in"]), = _adamw_call(
        [(w["w_in"][0], g_w_in, m["w_in"][0], v["w_in"][0])], name="adamw_w_in")

    small_g = {"g_mix_pre": dg_mix_pre, "g_mix_post": dg_mix_post, "g_mem_kv": dg_mem_kv, "g_mem_pre": dg_mem_pre,
               "g_mem_post": dg_mem_post, "g_ff_pre": dg_ff_pre, "g_ff_post": dg_ff_post, "g_fox_out": dg_fox,
               "g_chk_out": dg_chk, "b_fgt": db_fgt,
               "rel_bias": _g_to_rel_table(dgrev[:, 0:2, :].reshape(8, ROLL_W))}
    small = _small_call(small_g, sp, {k: rows(m, k) for k in SMALL}, {k: rows(v, k) for k in SMALL},
                        name="small_allreduce_adamw")
    for k in SMALL:
        vals = small[k]
        if k == "rel_bias":
            vals = tuple(a[None] for a in vals)
        grads[k], delta_w[k], new_m[k], new_v[k] = vals

    def out(d, k):
        return d[k][None] if k in BIG else d[k]

    return (loss, grad_x[None], *[out(grads, k) for k in WEIGHTS], *[out(delta_w, k) for k in WEIGHTS],
            *[out(new_m, k) for k in WEIGHTS], *[out(new_v, k) for k in WEIGHTS])
```

```python
import functools

import jax
import jax.numpy as jnp
from jax import lax
from jax.experimental import pallas as pl
from jax.experimental.pallas import tpu as pltpu

F32 = jnp.float32
BF16 = jnp.bfloat16

D = 1024
HEAD = 64
N_PAIR = 4
D_GRP = 512
CHUNK = 64
LEFT = 8
MAX_REL = 128
N_REL = 2 * MAX_REL + 1
N_MEM = 256
MEM_HEADS = 4
MEM_HD = 256
D_FF = 4096
D_IN = 3080
D_ALL = 3200
EPS = 1e-6
TQ = 256
WIN = (LEFT + TQ // CHUNK) * CHUNK
PADK = LEFT * CHUNK
ROLL_W = 1024
NEG = -1e30
N_CHIP = 4
VMEM_LIMIT = 48 * 1024 * 1024

ADAM_LR = 0.001
ADAM_B1 = 0.9
ADAM_B2 = 0.999
ADAM_EPS = 1e-08
ADAM_WD = 0.01
ADAM_STEP = 10

MESH = pl.DeviceIdType.MESH


def _cparams():
    return pltpu.CompilerParams(vmem_limit_bytes=VMEM_LIMIT)


ANY = pl.BlockSpec(memory_space=pl.ANY)


class _Task:
    def __init__(self, arrays, out_shapes, sems, issue, drain, aliases=None):
        self.arrays, self.out_shapes, self.sems = list(arrays), list(out_shapes), list(sems)
        self.issue, self.drain, self.aliases = issue, drain, dict(aliases or {})


def _merge_tasks(tasks):
    tasks = [t for t in tasks if t is not None]
    if len(tasks) == 1:
        return tasks[0]
    cuts, a, o, s = [], 0, 0, 0
    aliases = {}
    for t in tasks:
        cuts.append((a, o, s))
        aliases.update({a + i: o + j for i, j in t.aliases.items()})
        a, o, s = a + len(t.arrays), o + len(t.out_shapes), s + len(t.sems)

    def part(fn_name):
        def run(ins, outs, sems):
            for t, (a0, o0, s0) in zip(tasks, cuts):
                getattr(t, fn_name)(ins[a0:a0 + len(t.arrays)], outs[o0:o0 + len(t.out_shapes)],
                                    sems[s0:s0 + len(t.sems)])
        return run

    return _Task([x for t in tasks for x in t.arrays], [x for t in tasks for x in t.out_shapes],
                 [x for t in tasks for x in t.sems], part("issue"), part("drain"), aliases)


def _pallas(body, *, grid, in_specs, out_specs, out_shape, name, scratch_shapes=(), task=None):
    if task is None:
        return pl.pallas_call(body, grid=grid, in_specs=list(in_specs), out_specs=out_specs, out_shape=out_shape,
                              scratch_shapes=list(scratch_shapes), name=name, compiler_params=_cparams())
    single = not isinstance(out_shape, (tuple, list))
    o_shapes = [out_shape] if single else list(out_shape)
    o_specs = [out_specs] if single else list(out_specs)
    n_in, n_out, n_scr = len(in_specs), len(o_shapes), len(scratch_shapes)
    t_in, t_out = len(task.arrays), len(task.out_shapes)

    def carried(*refs):
        cut = [n_in, t_in, n_out, t_out, n_scr]
        parts, p = [], 0
        for c in cut:
            parts.append(refs[p:p + c])
            p += c
        ins, tins, outs, touts, scr = parts
        tsems = refs[p:]
        ids = [pl.program_id(a) for a in range(len(grid))]
        first = functools.reduce(jnp.logical_and, [i == 0 for i in ids])
        last = functools.reduce(jnp.logical_and, [i == g - 1 for i, g in zip(ids, grid)])

        @pl.when(first)
        def _():
            task.issue(tins, touts, tsems)
        body(*ins, *outs, *scr)

        @pl.when(last)
        def _():
            task.drain(tins, touts, tsems)

    call = pl.pallas_call(
        carried, grid=grid, in_specs=list(in_specs) + [ANY] * t_in, out_specs=o_specs + [ANY] * t_out,
        out_shape=o_shapes + list(task.out_shapes), scratch_shapes=list(scratch_shapes) + list(task.sems),
        input_output_aliases={n_in + i: n_out + j for i, j in task.aliases.items()},
        name=name, compiler_params=_cparams())

    def run(*args):
        res = call(*args, *task.arrays)
        outs = res[:n_out]
        return (outs[0] if single else tuple(outs)), list(res[n_out:])

    return run


def _comm_call(task, *, name):
    t_in, t_out = len(task.arrays), len(task.out_shapes)

    def body(*refs):
        tins, touts, tsems = refs[:t_in], refs[t_in:t_in + t_out], refs[t_in + t_out:]
        task.issue(tins, touts, tsems)
        task.drain(tins, touts, tsems)

    return pl.pallas_call(
        body, in_specs=[ANY] * t_in, out_specs=[ANY] * t_out, out_shape=list(task.out_shapes),
        scratch_shapes=list(task.sems), input_output_aliases=dict(task.aliases), name=name,
    )(*task.arrays)


def _dot(a, b):
    return jnp.dot(a, b, preferred_element_type=F32)


def _dot_nt(a, b):
    return lax.dot_general(a, b, (((1,), (1,)), ((), ())), preferred_element_type=F32)


def _dot_tn(a, b):
    return lax.dot_general(a, b, (((0,), (0,)), ((), ())), preferred_element_type=F32)


def _split3(x):
    hi = x.astype(BF16)
    r1 = x - hi.astype(F32)
    mid = r1.astype(BF16)
    lo = (r1 - mid.astype(F32)).astype(BF16)
    return hi, mid, lo


def _dot3(x, m01):
    hi, mid, lo = _split3(x)
    return _dot(hi, m01) + _dot(mid, m01) + _dot(lo, m01)


def _dot3_l(m01, x):
    hi, mid, lo = _split3(x)
    return _dot(m01, hi) + _dot(m01, mid) + _dot(m01, lo)


def _mm_nn(a, b, kind, *, name, out_dtype=BF16, tm=512, tn=512, epi=None, task=None):
    M, K = a.shape
    if kind == "plain":
        N = b.shape[1]
        b_spec = pl.BlockSpec((K, tn), lambda m, n: (0, n))
    elif kind == "rows":
        N = b.shape[2]
        b_spec = pl.BlockSpec((N_CHIP, K // N_CHIP, tn), lambda m, n: (0, 0, n))
    else:
        nq = b.shape[2]
        N = N_CHIP * nq
        per = nq // tn
        b_spec = pl.BlockSpec((None, K, tn), lambda m, n: (n // per, 0, n % per))
    tm = min(tm, M)
    kq = K // N_CHIP

    def body(a_ref, b_ref, *o_refs):
        if kind == "rows":
            acc = _dot(a_ref[:, 0:kq], b_ref[0])
            for j in range(1, N_CHIP):
                acc += _dot(a_ref[:, j * kq:(j + 1) * kq], b_ref[j])
        else:
            acc = _dot(a_ref[...], b_ref[...])
        if epi == "relu2":
            r = jnp.maximum(acc, 0.0)
            o_refs[0][...] = (r * r).astype(BF16)
            o_refs[1][...] = r.astype(BF16)
        else:
            o_refs[0][...] = acc.astype(out_dtype)

    o_spec = pl.BlockSpec((tm, tn), lambda m, n: (m, n))
    if epi == "relu2":
        out_shape = (jax.ShapeDtypeStruct((M, N), BF16), jax.ShapeDtypeStruct((M, N), BF16))
        out_specs = (o_spec, o_spec)
    else:
        out_shape = jax.ShapeDtypeStruct((M, N), out_dtype)
        out_specs = o_spec
    return _pallas(
        body, grid=(M // tm, N // tn),
        in_specs=[pl.BlockSpec((tm, K), lambda m, n: (m, 0)), b_spec],
        out_specs=out_specs, out_shape=out_shape, name=name, task=task,
    )(a, b)


def _mm_nt(a, b, kind, *, name, out_dtype=BF16, tm=512, tn=512, mul2r=None, task=None):
    M, K = a.shape
    if kind == "plain":
        N = b.shape[0]
        b_spec = pl.BlockSpec((tn, K), lambda m, n: (n, 0))
    elif kind == "rows":
        nq = b.shape[1]
        N = N_CHIP * nq
        tn = min(tn, nq)
        per = nq // tn
        b_spec = pl.BlockSpec((None, tn, K), lambda m, n: (n // per, n % per, 0))
    else:
        N = b.shape[1]
        b_spec = pl.BlockSpec((N_CHIP, tn, K // N_CHIP), lambda m, n: (0, n, 0))
    tm = min(tm, M)
    kq = K // N_CHIP

    def body(a_ref, b_ref, *rest):
        o_ref = rest[-1]
        if kind == "cols":
            acc = _dot_nt(a_ref[:, 0:kq], b_ref[0])
            for j in range(1, N_CHIP):
                acc += _dot_nt(a_ref[:, j * kq:(j + 1) * kq], b_ref[j])
        else:
            acc = _dot_nt(a_ref[...], b_ref[...])
        if mul2r is not None:
            acc = acc * (2.0 * rest[0][...].astype(F32))
        o_ref[...] = acc.astype(out_dtype)

    in_specs = [pl.BlockSpec((tm, K), lambda m, n: (m, 0)), b_spec]
    args = [a, b]
    if mul2r is not None:
        in_specs.append(pl.BlockSpec((tm, tn), lambda m, n: (m, n)))
        args.append(mul2r)
    return _pallas(
        body, grid=(M // tm, N // tn), in_specs=in_specs,
        out_specs=pl.BlockSpec((tm, tn), lambda m, n: (m, n)),
        out_shape=jax.ShapeDtypeStruct((M, N), out_dtype), name=name, task=task,
    )(*args)


def _mm_tn(a, b, *, name, out_dtype=BF16, tk=512, tn=512, cols4=False, task=None):
    M, K1 = a.shape
    N = b.shape[1]
    tk = min(tk, K1)
    tn = min(tn, N)

    def body(a_ref, b_ref, o_ref):
        o_ref[...] = _dot_tn(a_ref[...], b_ref[...]).astype(out_dtype)

    if cols4:
        per = (N // N_CHIP) // tn
        out_shape = jax.ShapeDtypeStruct((N_CHIP, K1, N // N_CHIP), out_dtype)
        o_spec = pl.BlockSpec((None, tk, tn), lambda k, n: (n // per, k, n % per))
    else:
        out_shape = jax.ShapeDtypeStruct((K1, N), out_dtype)
        o_spec = pl.BlockSpec((tk, tn), lambda k, n: (k, n))
    return _pallas(
        body, grid=(K1 // tk, N // tn),
        in_specs=[pl.BlockSpec((M, tk), lambda k, n: (0, k)), pl.BlockSpec((M, tn), lambda k, n: (0, n))],
        out_specs=o_spec, out_shape=out_shape, name=name, task=task,
    )(a, b)


def _rms(x, g):
    r = lax.rsqrt(jnp.mean(x * x, axis=-1, keepdims=True) + EPS)
    return x * r * g


def _rms_bwd(x, g, dy):
    r = lax.rsqrt(jnp.mean(x * x, axis=-1, keepdims=True) + EPS)
    xh = x * r
    dg = jnp.sum(dy * xh, axis=0, keepdims=True)
    dxh = dy * g
    dx = r * (dxh - xh * jnp.mean(dxh * xh, axis=-1, keepdims=True))
    return dx, dg


def _row_spec(tm, n):
    return pl.BlockSpec((tm, n), lambda i: (i, 0))


def _vec_spec(n):
    return pl.BlockSpec((1, n), lambda i: (0, 0))


def _acc_spec(n):
    return pl.BlockSpec((8, n), lambda i: (0, 0))


def _acc_add(ref, row, i):
    @pl.when(i == 0)
    def _():
        ref[...] = jnp.zeros_like(ref)
    ref[0:1, :] += row


def _rms_fwd_call(x, g, *, name, tm=256, task=None):
    M, n = x.shape
    tm = min(tm, M)

    def body(x_ref, g_ref, h_ref):
        h_ref[...] = _rms(x_ref[...], g_ref[...]).astype(BF16)

    return _pallas(
        body, grid=(M // tm,), in_specs=[_row_spec(tm, n), _vec_spec(n)], out_specs=_row_spec(tm, n),
        out_shape=jax.ShapeDtypeStruct((M, n), BF16), name=name, task=task,
    )(x, g)


def _post_pre_call(xres, z, g_post, g_pre, *, name, tm=256):
    M, n = xres.shape

    def body(x_ref, z_ref, gp_ref, gn_ref, xo_ref, h_ref):
        xn = x_ref[...] + _rms(z_ref[...], gp_ref[...])
        xo_ref[...] = xn
        h_ref[...] = _rms(xn, gn_ref[...]).astype(BF16)

    return pl.pallas_call(
        body, grid=(M // tm,),
        in_specs=[_row_spec(tm, n), _row_spec(tm, n), _vec_spec(n), _vec_spec(n)],
        out_specs=(_row_spec(tm, n), _row_spec(tm, n)),
        out_shape=(jax.ShapeDtypeStruct((M, n), F32), jax.ShapeDtypeStruct((M, n), BF16)),
        name=name, compiler_params=_cparams(),
    )(xres, z, g_post, g_pre)


def _final_call(x2, y3, g_post, target, *, name, tm=256):
    M, n = x2.shape

    def body(x_ref, y_ref, g_ref, t_ref, loss_ref, dx_ref, dy_ref, dg_ref):
        i = pl.program_id(0)
        y = y_ref[...]
        g = g_ref[...]
        diff = x_ref[...] + _rms(y, g) - t_ref[...]
        part = 0.5 * jnp.sum(jnp.sum(diff * diff, axis=1, keepdims=True), axis=0, keepdims=True) / n

        @pl.when(i == 0)
        def _():
            loss_ref[...] = jnp.zeros_like(loss_ref)
        loss_ref[...] += jnp.broadcast_to(part, loss_ref.shape)
        dx = diff / n
        dx_ref[...] = dx
        dy, dg = _rms_bwd(y, g, dx)
        dy_ref[...] = dy.astype(BF16)
        _acc_add(dg_ref, dg, i)

    return pl.pallas_call(
        body, grid=(M // tm,),
        in_specs=[_row_spec(tm, n), _row_spec(tm, n), _vec_spec(n), _row_spec(tm, n)],
        out_specs=(pl.BlockSpec((8, 128), lambda i: (0, 0)), _row_spec(tm, n), _row_spec(tm, n), _acc_spec(n)),
        out_shape=(jax.ShapeDtypeStruct((8, 128), F32), jax.ShapeDtypeStruct((M, n), F32),
                   jax.ShapeDtypeStruct((M, n), BF16), jax.ShapeDtypeStruct((8, n), F32)),
        name=name, compiler_params=_cparams(),
    )(x2, y3, g_post, target)


def _bwd_mid_call(dx_in, x, dh, g_pre, y, g_post, *, name, tm=256):
    M, n = x.shape

    def body(dxi_ref, x_ref, dh_ref, gpre_ref, y_ref, gpost_ref, dx_ref, dy_ref, dgpre_ref, dgpost_ref):
        i = pl.program_id(0)
        d1, dg1 = _rms_bwd(x_ref[...], gpre_ref[...], dh_ref[...])
        dx = dxi_ref[...] + d1
        dx_ref[...] = dx
        dy, dg2 = _rms_bwd(y_ref[...], gpost_ref[...], dx)
        dy_ref[...] = dy.astype(BF16)
        _acc_add(dgpre_ref, dg1, i)
        _acc_add(dgpost_ref, dg2, i)

    return pl.pallas_call(
        body, grid=(M // tm,),
        in_specs=[_row_spec(tm, n), _row_spec(tm, n), _row_spec(tm, n), _vec_spec(n), _row_spec(tm, n), _vec_spec(n)],
        out_specs=(_row_spec(tm, n), _row_spec(tm, n), _acc_spec(n), _acc_spec(n)),
        out_shape=(jax.ShapeDtypeStruct((M, n), F32), jax.ShapeDtypeStruct((M, n), BF16),
                   jax.ShapeDtypeStruct((8, n), F32), jax.ShapeDtypeStruct((8, n), F32)),
        name=name, compiler_params=_cparams(),
    )(dx_in, x, dh, g_pre, y, g_post)


def _bwd_last_call(dx_in, x, dh, g_pre, *, name, tm=256, task=None):
    M, n = x.shape

    def body(dxi_ref, x_ref, dh_ref, g_ref, dx_ref, dg_ref):
        i = pl.program_id(0)
        d1, dg1 = _rms_bwd(x_ref[...], g_ref[...], dh_ref[...])
        dx_ref[...] = dxi_ref[...] + d1
        _acc_add(dg_ref, dg1, i)

    return _pallas(
        body, grid=(M // tm,),
        in_specs=[_row_spec(tm, n), _row_spec(tm, n), _row_spec(tm, n), _vec_spec(n)],
        out_specs=(_row_spec(tm, n), _acc_spec(n)),
        out_shape=(jax.ShapeDtypeStruct((M, n), F32), jax.ShapeDtypeStruct((8, n), F32)),
        name=name, task=task,
    )(dx_in, x, dh, g_pre)


def _gain_grad_call(x, g, dy_a, dy_b, *, name):
    M, n = x.shape

    def body(x_ref, g_ref, a_ref, b_ref, dg_ref):
        _, dg = _rms_bwd(x_ref[...], g_ref[...], a_ref[...] + b_ref[...])
        dg_ref[...] = jnp.zeros_like(dg_ref)
        dg_ref[0:1, :] = dg

    return pl.pallas_call(
        body, grid=(1,),
        in_specs=[_row_spec(M, n), _vec_spec(n), _row_spec(M, n), _row_spec(M, n)],
        out_specs=_acc_spec(n), out_shape=jax.ShapeDtypeStruct((8, n), F32),
        name=name, compiler_params=_cparams(),
    )(x, g, dy_a, dy_b)


def _head_group_matrix():
    a = lax.broadcasted_iota(jnp.int32, (D_GRP, D_GRP), 0) // HEAD
    b = lax.broadcasted_iota(jnp.int32, (D_GRP, D_GRP), 1) // HEAD
    return jnp.where(a == b, 1.0, 0.0).astype(BF16)


def _mix_norm_fwd_call(yf, yc, gf, gc, *, name, tm=256):
    M = yf.shape[0]

    def body(yf_ref, yc_ref, gf_ref, gc_ref, o_ref):
        o_ref[:, 0:D_GRP] = _rms(yf_ref[...], gf_ref[...]).astype(BF16)
        o_ref[:, D_GRP:D] = _rms(yc_ref[...], gc_ref[...]).astype(BF16)

    return pl.pallas_call(
        body, grid=(M // tm,),
        in_specs=[_row_spec(tm, D_GRP), _row_spec(tm, D_GRP), _vec_spec(D_GRP), _vec_spec(D_GRP)],
        out_specs=_row_spec(tm, D), out_shape=jax.ShapeDtypeStruct((M, D), BF16),
        name=name, compiler_params=_cparams(),
    )(yf, yc, gf, gc)


def _mix_norm_bwd_call(dyn, yf, yc, gf, gc, *, name, tm=256):
    M = yf.shape[0]

    def body(dyn_ref, yf_ref, yc_ref, gf_ref, gc_ref, dof_ref, doc_ref, delta_ref, dgf_ref, dgc_ref):
        i = pl.program_id(0)
        yf_ = yf_ref[...]
        dof, dgf = _rms_bwd(yf_, gf_ref[...], dyn_ref[:, 0:D_GRP])
        doc, dgc = _rms_bwd(yc_ref[...], gc_ref[...], dyn_ref[:, D_GRP:D])
        dof_b = dof.astype(BF16)
        dof_ref[...] = dof_b
        doc_ref[...] = doc.astype(BF16)
        prod = dof_b.astype(F32) * yf_
        hi = prod.astype(BF16)
        lo = (prod - hi.astype(F32)).astype(BF16)
        grp = _head_group_matrix()
        delta_ref[...] = _dot(hi, grp) + _dot(lo, grp)
        _acc_add(dgf_ref, dgf, i)
        _acc_add(dgc_ref, dgc, i)

    return pl.pallas_call(
        body, grid=(M // tm,),
        in_specs=[_row_spec(tm, D), _row_spec(tm, D_GRP), _row_spec(tm, D_GRP), _vec_spec(D_GRP), _vec_spec(D_GRP)],
        out_specs=(_row_spec(tm, D_GRP), _row_spec(tm, D_GRP), _row_spec(tm, D_GRP), _acc_spec(D_GRP), _acc_spec(D_GRP)),
        out_shape=(jax.ShapeDtypeStruct((M, D_GRP), BF16), jax.ShapeDtypeStruct((M, D_GRP), BF16),
                   jax.ShapeDtypeStruct((M, D_GRP), F32), jax.ShapeDtypeStruct((8, D_GRP), F32),
                   jax.ShapeDtypeStruct((8, D_GRP), F32)),
        name=name, compiler_params=_cparams(),
    )(dyn, yf, yc, gf, gc)


def _tri(n, lower_incl):
    a = lax.broadcasted_iota(jnp.int32, (n, n), 0)
    b = lax.broadcasted_iota(jnp.int32, (n, n), 1)
    return jnp.where(a >= b, 1.0, 0.0).astype(BF16) if lower_incl else jnp.where(a <= b, 1.0, 0.0).astype(BF16)


def _fox_prep_call(fl_raw, b_pad, *, name):
    S = fl_raw.shape[0]
    nb = S // TQ

    def body(fl_ref, b_ref, crep_ref, ct_ref, carry_ref):
        i = pl.program_id(0)

        @pl.when(i == 0)
        def _():
            carry_ref[...] = jnp.zeros_like(carry_ref)
        logf = jax.nn.log_sigmoid(fl_ref[...] + b_ref[...])
        cb = _dot3_l(_tri(TQ, True), logf) + carry_ref[0:1, :]
        carry_ref[0:1, :] = cb[TQ - 1:TQ, :]
        a = lax.broadcasted_iota(jnp.int32, (128, D_GRP), 0)
        b = lax.broadcasted_iota(jnp.int32, (128, D_GRP), 1) // HEAD
        expand = jnp.where(a == b, 1.0, 0.0).astype(BF16)
        crep = _dot3(cb, expand)
        crep_ref[...] = crep
        ct_ref[...] = crep.T

    return pl.pallas_call(
        body, grid=(nb,),
        in_specs=[_row_spec(TQ, 128), _vec_spec(128)],
        out_specs=(_row_spec(TQ, D_GRP), pl.BlockSpec((None, D_GRP, TQ), lambda i: (i, 0, 0))),
        out_shape=(jax.ShapeDtypeStruct((S, D_GRP), F32), jax.ShapeDtypeStruct((nb, D_GRP, TQ), F32)),
        scratch_shapes=[pltpu.VMEM((8, 128), F32)],
        name=name, compiler_params=_cparams(),
    )(fl_raw, b_pad)


def _lane_masks():
    lane = lax.broadcasted_iota(jnp.int32, (1, 128), 1)
    return lane < HEAD, lane >= HEAD


def _fox_fwd_call(proj, c_rep, c_t, *, name, task=None):
    S = proj.shape[0]
    nq = S // TQ
    scale = HEAD ** -0.5

    def body(q_ref, k_ref, v_ref, c_ref, ct_ref, o_ref, lse_ref):
        i = pl.program_id(1)
        m_lo, m_hi = _lane_masks()
        masks = (m_lo, m_hi)
        q = q_ref[...]
        qm = [jnp.where(mk, q, jnp.zeros_like(q)) for mk in masks]
        cq = c_ref[...]
        cqh = [cq[:, 0:1], cq[:, HEAD:HEAD + 1]]
        row = lax.broadcasted_iota(jnp.int32, (TQ, TQ), 0)
        col = lax.broadcasted_iota(jnp.int32, (TQ, TQ), 1)

        def step(j, carry, masked):
            ms, ls, acc = carry
            start = pl.multiple_of(j * TQ, TQ)
            k = k_ref[pl.ds(start, TQ), :]
            v = v_ref[pl.ds(start, TQ), :]
            ct = ct_ref[j]
            new_m, new_l, pv, alpha_l = [], [], [], []
            for h in range(2):
                s = _dot_nt(qm[h], k) * scale + (cqh[h] - ct[HEAD * h:HEAD * h + 1, :])
                if masked:
                    s = jnp.where(row >= col, s, NEG)
                mn = jnp.maximum(ms[h], jnp.max(s, axis=1, keepdims=True))
                alpha = jnp.exp(ms[h] - mn)
                p = jnp.exp(s - mn)
                new_l.append(alpha * ls[h] + jnp.sum(p, axis=1, keepdims=True))
                new_m.append(mn)
                alpha_l.append(alpha)
                pv.append(_dot(p.astype(BF16), jnp.where(masks[h], v, jnp.zeros_like(v))))
            alpha_lane = jnp.where(m_lo, alpha_l[0], alpha_l[1])
            acc = acc * alpha_lane + pv[0] + pv[1]
            return (tuple(new_m), tuple(new_l), acc)

        init = ((jnp.full((TQ, 1), NEG, F32),) * 2, (jnp.zeros((TQ, 1), F32),) * 2, jnp.zeros((TQ, 128), F32))
        carry = lax.fori_loop(0, i, lambda j, c: step(j, c, False), init)
        ms, ls, acc = step(i, carry, True)
        l_lane = jnp.where(m_lo, ls[0], ls[1])
        o_ref[...] = acc / l_lane
        lse_ref[...] = jnp.where(m_lo, ms[0] + jnp.log(ls[0]), ms[1] + jnp.log(ls[1]))

    return _pallas(
        body, grid=(N_PAIR, nq),
        in_specs=[pl.BlockSpec((TQ, 128), lambda p, i: (i, p)),
                  pl.BlockSpec((S, 128), lambda p, i: (0, N_PAIR + p)),
                  pl.BlockSpec((S, 128), lambda p, i: (0, 2 * N_PAIR + p)),
                  pl.BlockSpec((TQ, 128), lambda p, i: (i, p)),
                  pl.BlockSpec((nq, 128, TQ), lambda p, i: (0, p, 0))],
        out_specs=(pl.BlockSpec((TQ, 128), lambda p, i: (i, p)), pl.BlockSpec((TQ, 128), lambda p, i: (i, p))),
        out_shape=(jax.ShapeDtypeStruct((S, D_GRP), F32), jax.ShapeDtypeStruct((S, D_GRP), F32)),
        name=name, task=task,
    )(proj, proj, proj, c_rep, c_t)


def _fox_bwd_call(proj, do, lse_rep, delta_rep, c_rep, c_t, *, name, task=None):
    S = proj.shape[0]
    nq = S // TQ
    scale = HEAD ** -0.5

    def body(q_ref, k_ref, v_ref, do_ref, lse_ref, dl_ref, c_ref, ct_ref, dq_ref, dk_ref, dv_ref, dct_ref, dcq_ref, dqa_ref):
        j = pl.program_id(1)
        m_lo, m_hi = _lane_masks()
        masks = (m_lo, m_hi)

        @pl.when(j == 0)
        def _():
            dqa_ref[...] = jnp.zeros_like(dqa_ref)
            dcq_ref[...] = jnp.zeros_like(dcq_ref)
        k = k_ref[...]
        v = v_ref[...]
        km = [jnp.where(mk, k, jnp.zeros_like(k)) for mk in masks]
        ct = ct_ref[...]
        row = lax.broadcasted_iota(jnp.int32, (TQ, TQ), 0)
        col = lax.broadcasted_iota(jnp.int32, (TQ, TQ), 1)

        def step(i, carry, masked):
            dk, dv, dcs = carry
            start = pl.multiple_of(i * TQ, TQ)
            q = q_ref[pl.ds(start, TQ), :]
            do = do_ref[pl.ds(start, TQ), :]
            lse = lse_ref[pl.ds(start, TQ), :]
            dl = dl_ref[pl.ds(start, TQ), :]
            cq = c_ref[pl.ds(start, TQ), :]
            dq = jnp.zeros((TQ, 128), F32)
            new_dcs = []
            rows = []
            for h in range(2):
                lo = HEAD * h
                qm = jnp.where(masks[h], q, jnp.zeros_like(q))
                dom = jnp.where(masks[h], do, jnp.zeros_like(do))
                s = _dot_nt(qm, k) * scale + (cq[:, lo:lo + 1] - ct[lo:lo + 1, :])
                p = jnp.exp(s - lse[:, lo:lo + 1])
                if masked:
                    p = jnp.where(row >= col, p, 0.0)
                dp = _dot_nt(dom, v)
                ds = p * (dp - dl[:, lo:lo + 1])
                new_dcs.append(dcs[h] + jnp.sum(ds, axis=0, keepdims=True))
                rows.append(jnp.sum(ds, axis=1, keepdims=True))
                dsb = (ds * scale).astype(BF16)
                dv = dv + _dot_tn(p.astype(BF16), dom)
                dk = dk + _dot_tn(dsb, qm)
                dq = dq + _dot(dsb, km[h])
            dqa_ref[pl.ds(start, TQ), :] += dq
            dcq_ref[pl.ds(start, TQ), :] += jnp.where(m_lo, rows[0], rows[1])
            return (dk, dv, tuple(new_dcs))

        init = (jnp.zeros((TQ, 128), F32), jnp.zeros((TQ, 128), F32), (jnp.zeros((1, TQ), F32),) * 2)
        carry = step(j, init, True)
        dk, dv, dcs = lax.fori_loop(j + 1, nq, lambda i, c: step(i, c, False), carry)
        dk_ref[...] = dk.astype(BF16)
        dv_ref[...] = dv.astype(BF16)
        dct_ref[...] = jnp.zeros_like(dct_ref)
        dct_ref[0:1, :] = -dcs[0]
        dct_ref[1:2, :] = -dcs[1]

        @pl.when(j == nq - 1)
        def _():
            dq_ref[...] = dqa_ref[...].astype(BF16)

    res = lambda p, j: (0, p)
    return _pallas(
        body, grid=(N_PAIR, nq), task=task,
        in_specs=[pl.BlockSpec((S, 128), res),
                  pl.BlockSpec((TQ, 128), lambda p, j: (j, N_PAIR + p)),
                  pl.BlockSpec((TQ, 128), lambda p, j: (j, 2 * N_PAIR + p)),
                  pl.BlockSpec((S, 128), res), pl.BlockSpec((S, 128), res), pl.BlockSpec((S, 128), res),
                  pl.BlockSpec((S, 128), res),
                  pl.BlockSpec((None, 128, TQ), lambda p, j: (j, p, 0))],
        out_specs=(pl.BlockSpec((S, 128), res),
                   pl.BlockSpec((TQ, 128), lambda p, j: (j, p)), pl.BlockSpec((TQ, 128), lambda p, j: (j, p)),
                   pl.BlockSpec((None, None, 8, TQ), lambda p, j: (p, j, 0, 0)),
                   pl.BlockSpec((S, 128), res)),
        out_shape=(jax.ShapeDtypeStruct((S, D_GRP), BF16), jax.ShapeDtypeStruct((S, D_GRP), BF16),
                   jax.ShapeDtypeStruct((S, D_GRP), BF16), jax.ShapeDtypeStruct((N_PAIR, nq, 8, TQ), F32),
                   jax.ShapeDtypeStruct((S, D_GRP), F32)),
        scratch_shapes=[pltpu.VMEM((S, 128), F32)],
        name=name,
    )(proj, proj, proj, do, lse_rep, delta_rep, c_rep, c_t)


def _fox_gate_bwd_call(dc_rows, fl_raw, b_pad, *, name):
    S = fl_raw.shape[0]
    nb = S // TQ

    def body(dc_ref, fl_ref, b_ref, dfl_ref, db_ref, carry_ref):
        i = pl.program_id(0)

        @pl.when(i == 0)
        def _():
            carry_ref[...] = jnp.zeros_like(carry_ref)
        rc = _dot3(dc_ref[...], _tri(TQ, True)) + carry_ref[:, 0:1]
        carry_ref[...] = jnp.broadcast_to(rc[:, 0:1], carry_ref.shape)
        fl = fl_ref[...] + b_ref[...]
        dfl = rc.T * jax.nn.sigmoid(-fl)
        dfl_ref[...] = dfl.astype(BF16)
        _acc_add(db_ref, jnp.sum(dfl, axis=0, keepdims=True), i)

    rev = lambda i: (nb - 1 - i, 0)
    return pl.pallas_call(
        body, grid=(nb,),
        in_specs=[pl.BlockSpec((128, TQ), lambda i: (0, nb - 1 - i)), pl.BlockSpec((TQ, 128), rev), _vec_spec(128)],
        out_specs=(pl.BlockSpec((TQ, 128), rev), _acc_spec(128)),
        out_shape=(jax.ShapeDtypeStruct((S, 128), BF16), jax.ShapeDtypeStruct((8, 128), F32)),
        scratch_shapes=[pltpu.VMEM((128, 128), F32)],
        name=name, compiler_params=_cparams(),
    )(dc_rows, fl_raw, b_pad)


def _chk_bias_call(g_rev, *, name):
    def body(g_ref, o_ref):
        x = jnp.broadcast_to(g_ref[...], (TQ, ROLL_W))
        rolled = pltpu.roll(x, ROLL_W - (TQ - 1), 1, stride=1, stride_axis=0)
        qc = lax.broadcasted_iota(jnp.int32, (TQ, WIN), 0) // CHUNK
        kc = lax.broadcasted_iota(jnp.int32, (TQ, WIN), 1) // CHUNK
        band = (kc >= qc) & (kc <= qc + LEFT)
        o_ref[...] = jnp.where(band, rolled[:, 0:WIN], NEG)

    return pl.pallas_call(
        body, grid=(8,),
        in_specs=[pl.BlockSpec((None, 1, ROLL_W), lambda h: (h, 0, 0))],
        out_specs=pl.BlockSpec((None, TQ, WIN), lambda h: (h, 0, 0)),
        out_shape=jax.ShapeDtypeStruct((8, TQ, WIN), F32), name=name, compiler_params=_cparams(),
    )(g_rev.reshape(8, 1, ROLL_W))


def _chk_scores(i, qm, kwin, bias, scale):
    s = _dot_nt(qm, kwin) * scale + bias
    kc = lax.broadcasted_iota(jnp.int32, (TQ, WIN), 1) // CHUNK
    return jnp.where(kc + i * (TQ // CHUNK) >= LEFT, s, NEG)


def _chk_fwd_call(proj, bias, *, name, task=None):
    S = proj.shape[0]
    nq = S // TQ
    scale = HEAD ** -0.5

    def body(q_ref, k_ref, v_ref, b_ref, o_ref, kp_ref, vp_ref):
        i = pl.program_id(1)

        @pl.when(i == 0)
        def _():
            kp_ref[0:PADK, :] = jnp.zeros((PADK, 128), BF16)
            vp_ref[0:PADK, :] = jnp.zeros((PADK, 128), BF16)
            kp_ref[PADK:PADK + S, :] = k_ref[...]
            vp_ref[PADK:PADK + S, :] = v_ref[...]
        masks = _lane_masks()
        q = q_ref[...]
        start = pl.multiple_of(i * TQ, TQ)
        kwin = kp_ref[pl.ds(start, WIN), :]
        vwin = vp_ref[pl.ds(start, WIN), :]
        out = jnp.zeros((TQ, 128), F32)
        for h in range(2):
            qm = jnp.where(masks[h], q, jnp.zeros_like(q))
            s = _chk_scores(i, qm, kwin, b_ref[h], scale)
            p = jnp.exp(s - jnp.max(s, axis=1, keepdims=True))
            p = p / jnp.sum(p, axis=1, keepdims=True)
            out = out + _dot(p.astype(BF16), jnp.where(masks[h], vwin, jnp.zeros_like(vwin)))
        o_ref[...] = out

    c0 = 3 * N_PAIR
    return _pallas(
        body, grid=(N_PAIR, nq), task=task,
        in_specs=[pl.BlockSpec((TQ, 128), lambda p, i: (i, c0 + p)),
                  pl.BlockSpec((S, 128), lambda p, i: (0, c0 + N_PAIR + p)),
                  pl.BlockSpec((S, 128), lambda p, i: (0, c0 + 2 * N_PAIR + p)),
                  pl.BlockSpec((2, TQ, WIN), lambda p, i: (p, 0, 0))],
        out_specs=pl.BlockSpec((TQ, 128), lambda p, i: (i, p)),
        out_shape=jax.ShapeDtypeStruct((S, D_GRP), F32),
        scratch_shapes=[pltpu.VMEM((S + PADK, 128), BF16), pltpu.VMEM((S + PADK, 128), BF16)],
        name=name,
    )(proj, proj, proj, bias)


def _chk_bwd_call(proj, do, bias, *, name, task=None):
    S = proj.shape[0]
    nq = S // TQ
    scale = HEAD ** -0.5

    def body(q_ref, k_ref, v_ref, do_ref, b_ref, dq_ref, dk_ref, dv_ref, dg_ref, kp_ref, vp_ref, dkp_ref, dvp_ref, db_ref):
        i = pl.program_id(1)

        @pl.when(i == 0)
        def _():
            kp_ref[0:PADK, :] = jnp.zeros((PADK, 128), BF16)
            vp_ref[0:PADK, :] = jnp.zeros((PADK, 128), BF16)
            kp_ref[PADK:PADK + S, :] = k_ref[...]
            vp_ref[PADK:PADK + S, :] = v_ref[...]
            dkp_ref[...] = jnp.zeros_like(dkp_ref)
            dvp_ref[...] = jnp.zeros_like(dvp_ref)
            db_ref[...] = jnp.zeros_like(db_ref)
        masks = _lane_masks()
        q = q_ref[...]
        dout = do_ref[...]
        start = pl.multiple_of(i * TQ, TQ)
        kwin = kp_ref[pl.ds(start, WIN), :]
        vwin = vp_ref[pl.ds(start, WIN), :]
        dq = jnp.zeros((TQ, 128), F32)
        dkw = jnp.zeros((WIN, 128), F32)
        dvw = jnp.zeros((WIN, 128), F32)
        for h in range(2):
            qm = jnp.where(masks[h], q, jnp.zeros_like(q))
            dom = jnp.where(masks[h], dout, jnp.zeros_like(dout))
            s = _chk_scores(i, qm, kwin, b_ref[h], scale)
            p = jnp.exp(s - jnp.max(s, axis=1, keepdims=True))
            p = p / jnp.sum(p, axis=1, keepdims=True)
            dp = _dot_nt(dom, vwin)
            ds = p * (dp - jnp.sum(p * dp, axis=1, keepdims=True))
            db_ref[h] += ds
            dsb = (ds * scale).astype(BF16)
            dq = dq + _dot(dsb, jnp.where(masks[h], kwin, jnp.zeros_like(kwin)))
            dkw = dkw + _dot_tn(dsb, qm)
            dvw = dvw + _dot_tn(p.astype(BF16), dom)
        dq_ref[...] = dq.astype(BF16)
        dkp_ref[pl.ds(start, WIN), :] += dkw
        dvp_ref[pl.ds(start, WIN), :] += dvw

        @pl.when(i == nq - 1)
        def _():
            dk_ref[...] = dkp_ref[PADK:PADK + S, :].astype(BF16)
            dv_ref[...] = dvp_ref[PADK:PADK + S, :].astype(BF16)
            a = lax.broadcasted_iota(jnp.int32, (TQ, TQ), 0)
            b = lax.broadcasted_iota(jnp.int32, (TQ, TQ), 1)
            flip = jnp.where(a + b == TQ - 1, 1.0, 0.0).astype(BF16)
            e = lax.broadcasted_iota(jnp.int32, (1, ROLL_W), 1)
            dg_ref[...] = jnp.zeros_like(dg_ref)
            for h in range(2):
                rev = _dot3_l(flip, db_ref[h])
                wide = jnp.concatenate([rev, jnp.zeros((TQ, ROLL_W - WIN), F32)], axis=1)
                diag = pltpu.roll(wide, 0, 1, stride=1, stride_axis=0)
                dg = jnp.sum(diag, axis=0, keepdims=True)
                lo = jnp.sum(jnp.where(e <= 639, dg, 0.0), axis=1, keepdims=True)
                hi = jnp.sum(jnp.where(e >= 895, dg, 0.0), axis=1, keepdims=True)
                dg_ref[h:h + 1, :] = jnp.where(e == 639, lo, jnp.where(e == 895, hi, dg))

    c0 = 3 * N_PAIR
    res = lambda p, i: (0, p)
    return _pallas(
        body, grid=(N_PAIR, nq), task=task,
        in_specs=[pl.BlockSpec((TQ, 128), lambda p, i: (i, c0 + p)),
                  pl.BlockSpec((S, 128), lambda p, i: (0, c0 + N_PAIR + p)),
                  pl.BlockSpec((S, 128), lambda p, i: (0, c0 + 2 * N_PAIR + p)),
                  pl.BlockSpec((TQ, 128), lambda p, i: (i, p)),
                  pl.BlockSpec((2, TQ, WIN), lambda p, i: (p, 0, 0))],
        out_specs=(pl.BlockSpec((TQ, 128), lambda p, i: (i, p)), pl.BlockSpec((S, 128), res),
                   pl.BlockSpec((S, 128), res), pl.BlockSpec((None, 8, ROLL_W), lambda p, i: (p, 0, 0))),
        out_shape=(jax.ShapeDtypeStruct((S, D_GRP), BF16), jax.ShapeDtypeStruct((S, D_GRP), BF16),
                   jax.ShapeDtypeStruct((S, D_GRP), BF16), jax.ShapeDtypeStruct((N_PAIR, 8, ROLL_W), F32)),
        scratch_shapes=[pltpu.VMEM((S + PADK, 128), BF16), pltpu.VMEM((S + PADK, 128), BF16),
                        pltpu.VMEM((S + PADK, 128), F32), pltpu.VMEM((S + PADK, 128), F32),
                        pltpu.VMEM((2, TQ, WIN), F32)],
        name=name,
    )(proj, proj, proj, do, bias)


def _mem_fwd_call(q, k, v, *, name, tq=512):
    S = q.shape[0]
    scale = MEM_HD ** -0.5

    def body(q_ref, k_ref, v_ref, o_ref):
        s = _dot_nt(q_ref[...], k_ref[...]) * scale
        p = jnp.exp(s - jnp.max(s, axis=1, keepdims=True))
        p = p / jnp.sum(p, axis=1, keepdims=True)
        o_ref[...] = _dot(p.astype(BF16), v_ref[...]).astype(BF16)

    return pl.pallas_call(
        body, grid=(MEM_HEADS, S // tq),
        in_specs=[pl.BlockSpec((tq, MEM_HD), lambda h, i: (i, h)),
                  pl.BlockSpec((N_MEM, MEM_HD), lambda h, i: (0, h)),
                  pl.BlockSpec((N_MEM, MEM_HD), lambda h, i: (0, h))],
        out_specs=pl.BlockSpec((tq, MEM_HD), lambda h, i: (i, h)),
        out_shape=jax.ShapeDtypeStruct((S, D), BF16), name=name, compiler_params=_cparams(),
    )(q, k, v)


def _mem_bwd_call(q, k, v, do, *, name, tq=512):
    S = q.shape[0]
    n = S // tq
    scale = MEM_HD ** -0.5

    def body(q_ref, k_ref, v_ref, do_ref, dq_ref, dk_ref, dv_ref, dka_ref, dva_ref):
        i = pl.program_id(1)

        @pl.when(i == 0)
        def _():
            dka_ref[...] = jnp.zeros_like(dka_ref)
            dva_ref[...] = jnp.zeros_like(dva_ref)
        qb = q_ref[...]
        kb = k_ref[...]
        dob = do_ref[...]
        s = _dot_nt(qb, kb) * scale
        p = jnp.exp(s - jnp.max(s, axis=1, keepdims=True))
        p = p / jnp.sum(p, axis=1, keepdims=True)
        dp = _dot_nt(dob, v_ref[...])
        ds = p * (dp - jnp.sum(p * dp, axis=1, keepdims=True))
        dsb = (ds * scale).astype(BF16)
        dq_ref[...] = _dot(dsb, kb).astype(BF16)
        dka_ref[...] += _dot_tn(dsb, qb)
        dva_ref[...] += _dot_tn(p.astype(BF16), dob)

        @pl.when(i == n - 1)
        def _():
            dk_ref[...] = dka_ref[...].astype(BF16)
            dv_ref[...] = dva_ref[...].astype(BF16)

    kv = pl.BlockSpec((N_MEM, MEM_HD), lambda h, i: (0, h))
    qs = pl.BlockSpec((tq, MEM_HD), lambda h, i: (i, h))
    return pl.pallas_call(
        body, grid=(MEM_HEADS, n), in_specs=[qs, kv, kv, qs], out_specs=(qs, kv, kv),
        out_shape=(jax.ShapeDtypeStruct((S, D), BF16), jax.ShapeDtypeStruct((N_MEM, D), BF16),
                   jax.ShapeDtypeStruct((N_MEM, D), BF16)),
        scratch_shapes=[pltpu.VMEM((N_MEM, MEM_HD), F32), pltpu.VMEM((N_MEM, MEM_HD), F32)],
        name=name, compiler_params=_cparams(),
    )(q, k, v, do)


def _rel_table_to_g(rel):
    return jnp.concatenate([
        jnp.broadcast_to(rel[:, N_REL - 1:N_REL], (8, 640)),
        rel[:, 1:N_REL - 1][:, ::-1],
        jnp.broadcast_to(rel[:, 0:1], (8, 129)),
    ], axis=1)


def _g_to_rel_table(dg):
    return dg[:, 639:896][:, ::-1]


def _place():
    x, y, c = lax.axis_index("x"), lax.axis_index("y"), lax.axis_index("c")
    others = [(1 - x, y), (x, 1 - y), (1 - x, 1 - y)]
    return x, y, c, others


def _half(c, rows):
    hr = rows // 2
    return pl.ds(pl.multiple_of(c * hr, 16), hr)


def _dma_sems(*shape):
    return pltpu.SemaphoreType.DMA(shape)


def _cast_slab_call(w, chip_arr, *, name, tm=256):
    rows, cols = w.shape
    tm = min(tm, rows)

    def body(chip_ref, w_ref, o_ref):
        o_ref[...] = w_ref[...].astype(BF16)

    return pl.pallas_call(
        body,
        grid_spec=pltpu.PrefetchScalarGridSpec(
            num_scalar_prefetch=1, grid=(rows // tm,),
            in_specs=[pl.BlockSpec((tm, cols), lambda i, chip: (i, 0))],
            out_specs=pl.BlockSpec((None, tm, cols), lambda i, chip: (chip[0], i, 0))),
        out_shape=jax.ShapeDtypeStruct((N_CHIP, rows, cols), BF16), name=name, compiler_params=_cparams(),
    )(chip_arr, w)


def _ag_ici_task(gathered):
    n = len(gathered)

    def copies(ins, outs, sems):
        send_sems, recv_sems = sems
        x, y, c, others = _place()
        me = 2 * x + y
        for k in range(n):
            mine = _half(c, gathered[k].shape[1])
            for t, (ox, oy) in enumerate(others):
                yield [pltpu.make_async_remote_copy(
                    src_ref=ins[k].at[me, mine], dst_ref=outs[k].at[slab, mine],
                    send_sem=send_sems.at[k, t], recv_sem=recv_sems.at[k, t],
                    device_id=(ox, oy, c), device_id_type=MESH) for slab in (me, 2 * ox + oy)]

    def issue(ins, outs, sems):
        for outgoing, _ in copies(ins, outs, sems):
            outgoing.start()

    def drain(ins, outs, sems):
        for outgoing, incoming in copies(ins, outs, sems):
            incoming.wait_recv()
            outgoing.wait_send()

    return _Task(gathered, [jax.ShapeDtypeStruct(g.shape, g.dtype) for g in gathered],
                 [_dma_sems(n, 3), _dma_sems(n, 3)], issue, drain, aliases={k: k for k in range(n)})


def _ag_d2d_task(gathered):
    n = len(gathered)

    def copies(ins, outs, sems):
        send_sems, recv_sems = sems
        x, y, c, others = _place()
        for k in range(n):
            rows = gathered[k].shape[1]
            mine, theirs = _half(c, rows), _half(1 - c, rows)
            for t, (ox, oy) in enumerate(others):
                slab = 2 * ox + oy
                pair = [pltpu.make_async_remote_copy(
                    src_ref=ins[k].at[slab, half], dst_ref=outs[k].at[slab, half],
                    send_sem=send_sems.at[k, t], recv_sem=recv_sems.at[k, t],
                    device_id=(x, y, 1 - c), device_id_type=MESH) for half in (mine, theirs)]
                yield pair

    def issue(ins, outs, sems):
        for outgoing, _ in copies(ins, outs, sems):
            outgoing.start()

    def drain(ins, outs, sems):
        for outgoing, incoming in copies(ins, outs, sems):
            incoming.wait_recv()
            outgoing.wait_send()

    return _Task(gathered, [jax.ShapeDtypeStruct(g.shape, g.dtype) for g in gathered],
                 [_dma_sems(n, 3), _dma_sems(n, 3)], issue, drain, aliases={k: k for k in range(n)})


def _rs_pair_task(ds):
    n = len(ds)

    def copies(ins, outs, sems):
        send_sems, recv_sems = sems
        x, y, c, _ = _place()
        for k in range(n):
            yield pltpu.make_async_remote_copy(
                src_ref=ins[k].at[:, _half(1 - c, ds[k].shape[1])], dst_ref=outs[k],
                send_sem=send_sems.at[k], recv_sem=recv_sems.at[k],
                device_id=(x, y, 1 - c), device_id_type=MESH)

    def issue(ins, outs, sems):
        for cp in copies(ins, outs, sems):
            cp.start()

    def drain(ins, outs, sems):
        for cp in copies(ins, outs, sems):
            cp.wait()

    return _Task(ds, [jax.ShapeDtypeStruct((N_CHIP, d.shape[1] // 2, d.shape[2]), d.dtype) for d in ds],
                 [_dma_sems(n), _dma_sems(n)], issue, drain)


def _pair_add_call(d, r1, c_arr, *, name, tm=256):
    _, rows, cols = d.shape
    hr = rows // 2
    tm = min(tm, hr)
    nb = hr // tm

    def body(c_ref, d_ref, r_ref, o_ref):
        o_ref[...] = (d_ref[...].astype(F32) + r_ref[...].astype(F32)).astype(BF16)

    return pl.pallas_call(
        body,
        grid_spec=pltpu.PrefetchScalarGridSpec(
            num_scalar_prefetch=1, grid=(N_CHIP, nb),
            in_specs=[pl.BlockSpec((None, tm, cols), lambda j, i, c: (j, c[0] * nb + i, 0)),
                      pl.BlockSpec((None, tm, cols), lambda j, i, c: (j, i, 0))],
            out_specs=pl.BlockSpec((None, tm, cols), lambda j, i, c: (j, i, 0))),
        out_shape=jax.ShapeDtypeStruct((N_CHIP, hr, cols), BF16), name=name, compiler_params=_cparams(),
    )(c_arr, d, r1)


def _rs_chip_task(ps):
    n = len(ps)

    def copies(ins, outs, sems):
        send_sems, recv_sems = sems
        x, y, c, others = _place()
        for k in range(n):
            for t, (ox, oy) in enumerate(others):
                yield pltpu.make_async_remote_copy(
                    src_ref=ins[k].at[2 * ox + oy], dst_ref=outs[k].at[t],
                    send_sem=send_sems.at[k, t], recv_sem=recv_sems.at[k, t],
                    device_id=(ox, oy, c), device_id_type=MESH)

    def issue(ins, outs, sems):
        for cp in copies(ins, outs, sems):
            cp.start()

    def drain(ins, outs, sems):
        for cp in copies(ins, outs, sems):
            cp.wait()

    return _Task(ps, [jax.ShapeDtypeStruct((3,) + p.shape[1:], p.dtype) for p in ps],
                 [_dma_sems(n, 3), _dma_sems(n, 3)], issue, drain)


def _chip_sum_call(p, r2, place_arr, *, name, tm=256):
    _, hr, cols = r2.shape
    tm = min(tm, hr)
    nb = hr // tm

    def body(place_ref, p_ref, r_ref, o_ref):
        acc = p_ref[...].astype(F32)
        for j in range(3):
            acc = acc + r_ref[j].astype(F32)
        o_ref[...] = acc

    return pl.pallas_call(
        body,
        grid_spec=pltpu.PrefetchScalarGridSpec(
            num_scalar_prefetch=1, grid=(nb,),
            in_specs=[pl.BlockSpec((None, tm, cols), lambda i, pc: (pc[0], i, 0)),
                      pl.BlockSpec((3, tm, cols), lambda i, pc: (0, i, 0))],
            out_specs=pl.BlockSpec((tm, cols), lambda i, pc: (pc[1] * nb + i, 0))),
        out_shape=jax.ShapeDtypeStruct((2 * hr, cols), F32), name=name, compiler_params=_cparams(),
    )(place_arr, p, r2)


def _rs_gather_task(gs):
    n = len(gs)

    def copies(ins, outs, sems):
        send_sems, recv_sems = sems
        x, y, c, _ = _place()
        for k in range(n):
            rows = gs[k].shape[0]
            mine, theirs = _half(c, rows), _half(1 - c, rows)
            yield [pltpu.make_async_remote_copy(
                src_ref=ins[k].at[mine], dst_ref=outs[k].at[half],
                send_sem=send_sems.at[k], recv_sem=recv_sems.at[k],
                device_id=(x, y, 1 - c), device_id_type=MESH) for half in (mine, theirs)]

    def issue(ins, outs, sems):
        for outgoing, _ in copies(ins, outs, sems):
            outgoing.start()

    def drain(ins, outs, sems):
        for outgoing, incoming in copies(ins, outs, sems):
            incoming.wait_recv()
            outgoing.wait_send()

    return _Task(gs, [jax.ShapeDtypeStruct(g.shape, g.dtype) for g in gs],
                 [_dma_sems(n), _dma_sems(n)], issue, drain, aliases={k: k for k in range(n)})


def _adamw(w, g, m, v):
    m = ADAM_B1 * m + (1.0 - ADAM_B1) * g
    v = ADAM_B2 * v + (1.0 - ADAM_B2) * jnp.square(g)
    m_hat = m / (1.0 - ADAM_B1 ** ADAM_STEP)
    v_hat = v / (1.0 - ADAM_B2 ** ADAM_STEP)
    delta = -ADAM_LR * (m_hat / (jnp.sqrt(v_hat) + ADAM_EPS) + ADAM_WD * w)
    return delta, m, v


def _adamw_call(items, *, name, tm=256, task=None):
    n = len(items)
    cols = items[0][0].shape[1]
    tiles = [it[0].shape[0] // tm for it in items]
    steps = max(tiles)

    def body(*refs):
        i = pl.program_id(0)
        ins, outs = refs[:4 * n], refs[4 * n:]
        for k in range(n):
            def update(k=k):
                g = ins[4 * k + 1][...]
                res = _adamw(ins[4 * k][...], g, ins[4 * k + 2][...], ins[4 * k + 3][...])
                outs[4 * k][...] = g
                for j in range(3):
                    outs[4 * k + 1 + j][...] = res[j]
            if tiles[k] == steps:
                update()
            else:
                pl.when(i < tiles[k])(update)

    in_specs, out_specs, out_shape, args = [], [], [], []
    for it, t in zip(items, tiles):
        spec = pl.BlockSpec((tm, cols), lambda i, t=t: (jnp.minimum(i, t - 1), 0))
        in_specs += [spec] * 4
        out_specs += [spec] * 4
        out_shape += [jax.ShapeDtypeStruct(it[0].shape, F32)] * 4
        args += list(it)
    res = _pallas(body, grid=(steps,), in_specs=in_specs, out_specs=out_specs, out_shape=out_shape,
                  name=name, task=task)(*args)
    outs, extra = res if task is not None else (res, None)
    grouped = [tuple(outs[4 * k:4 * k + 4]) for k in range(n)]
    return (grouped, extra) if task is not None else grouped


N_DEV = 8
SMALL_ROWS = 24
SMALL_LAYOUT = {
    "g_mix_pre": (0, 0, 1, D), "g_mix_post": (1, 0, 1, D), "g_mem_kv": (2, 0, 1, D), "g_mem_pre": (3, 0, 1, D),
    "g_mem_post": (4, 0, 1, D), "g_ff_pre": (5, 0, 1, D), "g_ff_post": (6, 0, 1, D),
    "g_fox_out": (7, 0, 1, D_GRP), "g_chk_out": (7, D_GRP, 1, D_GRP), "b_fgt": (8, 0, 1, 8),
    "rel_bias": (16, 0, 8, N_REL),
}
SMALL = list(SMALL_LAYOUT)


LOSS_ROW = 9


def _small_call(grads, ws, ms, vs, loss_blk, task, *, name):
    n = len(SMALL)
    t_in, t_out = len(task.arrays), len(task.out_shapes)

    def body(*refs):
        g_refs, w_refs, m_refs, v_refs = (refs[j * n:(j + 1) * n] for j in range(4))
        p = 4 * n
        loss_ref, tins = refs[p], refs[p + 1:p + 1 + t_in]
        p += 1 + t_in
        outs, loss_out, touts = refs[p:p + 4 * n], refs[p + 4 * n], refs[p + 4 * n + 1:p + 4 * n + 1 + t_out]
        p += 4 * n + 1 + t_out
        mine, slots, send_sems, recv_sems = refs[p:p + 4]
        tsems = refs[p + 4:]
        task.issue(tins, touts, tsems)
        x, y, c, _ = _place()
        me = 4 * x + 2 * y + c
        mine[...] = jnp.zeros_like(mine)
        for k, name_k in enumerate(SMALL):
            r, l, nr, nl = SMALL_LAYOUT[name_k]
            mine[r:r + nr, l:l + nl] = g_refs[k][0:nr, 0:nl]
        mine[LOSS_ROW:LOSS_ROW + 1, 0:128] = loss_ref[0:1, :]
        slots[me] = mine[...]
        peers = [(dx, dy, dc) for dx in (0, 1) for dy in (0, 1) for dc in (0, 1)][1:]
        cps = []
        for t, (dx, dy, dc) in enumerate(peers):
            px, py, pc = (x + dx) % 2, (y + dy) % 2, (c + dc) % 2
            cps.append(pltpu.make_async_remote_copy(
                src_ref=mine, dst_ref=slots.at[me], send_sem=send_sems.at[t], recv_sem=recv_sems.at[t],
                device_id=(px, py, pc), device_id_type=MESH))
            cps[-1].start()
        for t, (dx, dy, dc) in enumerate(peers):
            px, py, pc = (x + dx) % 2, (y + dy) % 2, (c + dc) % 2
            pltpu.make_async_remote_copy(
                src_ref=mine, dst_ref=slots.at[4 * px + 2 * py + pc], send_sem=send_sems.at[t],
                recv_sem=recv_sems.at[t], device_id=(px, py, pc), device_id_type=MESH).wait_recv()
        for cp in cps:
            cp.wait_send()
        total = slots[0]
        for j in range(1, N_DEV):
            total = total + slots[j]
        for k, name_k in enumerate(SMALL):
            r, l, nr, nl = SMALL_LAYOUT[name_k]
            g = total[r:r + nr, l:l + nl]
            d, mn, vn = _adamw(w_refs[k][...], g, m_refs[k][...], v_refs[k][...])
            for j, val in enumerate((g, d, mn, vn)):
                outs[4 * k + j][...] = val
        loss_out[...] = jnp.broadcast_to(total[LOSS_ROW:LOSS_ROW + 1, 0:128], loss_out.shape)
        task.drain(tins, touts, tsems)

    vm = pl.BlockSpec(memory_space=pltpu.VMEM)
    out_shape = [jax.ShapeDtypeStruct(ws[k].shape, F32) for k in SMALL for _ in range(4)]
    out_shape += [jax.ShapeDtypeStruct((8, 128), F32)] + list(task.out_shapes)
    res = pl.pallas_call(
        body, in_specs=[vm] * (4 * n + 1) + [ANY] * t_in, out_specs=[vm] * (4 * n + 1) + [ANY] * t_out,
        out_shape=out_shape,
        scratch_shapes=[pltpu.VMEM((SMALL_ROWS, D), F32), pltpu.VMEM((N_DEV, SMALL_ROWS, D), F32),
                        _dma_sems(N_DEV - 1), _dma_sems(N_DEV - 1)] + list(task.sems),
        input_output_aliases={4 * n + 1 + i: 4 * n + 1 + j for i, j in task.aliases.items()},
        name=name,
    )(*[d[k] for d in (grads, ws, ms, vs) for k in SMALL], loss_blk, *task.arrays)
    return ({k: tuple(res[4 * i:4 * i + 4]) for i, k in enumerate(SMALL)}, res[4 * n], list(res[4 * n + 1:]))


WEIGHTS = ["w_in", "b_fgt", "rel_bias", "g_fox_out", "g_chk_out", "w_out", "g_mix_pre", "g_mix_post", "g_mem_kv",
           "w_mq", "w_mk", "w_mv", "w_mo", "g_mem_pre", "g_mem_post", "w_ff1", "w_ff2", "g_ff_pre", "g_ff_post"]
BIG = ["w_in", "w_out", "w_mq", "w_mk", "w_mv", "w_mo", "w_ff1", "w_ff2"]


def _in_cols_to_all(w):
    pad = jnp.zeros(w.shape[:-1] + (D_ALL - D_IN,), w.dtype)
    return jnp.concatenate([w[..., :1536], w[..., 1544:D_IN], w[..., 1536:1544], pad], axis=-1)


def _all_cols_to_in(w):
    return jnp.concatenate([w[..., :1536], w[..., 3072:3080], w[..., 1536:3072]], axis=-1)


def kernel(x, mem, w_in, b_fgt, rel_bias, g_fox_out, g_chk_out, w_out, g_mix_pre, g_mix_post, g_mem_kv, w_mq, w_mk, w_mv, w_mo, g_mem_pre, g_mem_post, w_ff1, w_ff2, g_ff_pre, g_ff_post, loss_target, m_w_in, m_b_fgt, m_rel_bias, m_g_fox_out, m_g_chk_out, m_w_out, m_g_mix_pre, m_g_mix_post, m_g_mem_kv, m_w_mq, m_w_mk, m_w_mv, m_w_mo, m_g_mem_pre, m_g_mem_post, m_w_ff1, m_w_ff2, m_g_ff_pre, m_g_ff_post, v_w_in, v_b_fgt, v_rel_bias, v_g_fox_out, v_g_chk_out, v_w_out, v_g_mix_pre, v_g_mix_post, v_g_mem_kv, v_w_mq, v_w_mk, v_w_mv, v_w_mo, v_g_mem_pre, v_g_mem_post, v_w_ff1, v_w_ff2, v_g_ff_pre, v_g_ff_post):
    w = dict(w_in=w_in, b_fgt=b_fgt, rel_bias=rel_bias, g_fox_out=g_fox_out, g_chk_out=g_chk_out, w_out=w_out,
             g_mix_pre=g_mix_pre, g_mix_post=g_mix_post, g_mem_kv=g_mem_kv, w_mq=w_mq, w_mk=w_mk, w_mv=w_mv,
             w_mo=w_mo, g_mem_pre=g_mem_pre, g_mem_post=g_mem_post, w_ff1=w_ff1, w_ff2=w_ff2, g_ff_pre=g_ff_pre,
             g_ff_post=g_ff_post)
    m = dict(w_in=m_w_in, b_fgt=m_b_fgt, rel_bias=m_rel_bias, g_fox_out=m_g_fox_out, g_chk_out=m_g_chk_out,
             w_out=m_w_out, g_mix_pre=m_g_mix_pre, g_mix_post=m_g_mix_post, g_mem_kv=m_g_mem_kv, w_mq=m_w_mq,
             w_mk=m_w_mk, w_mv=m_w_mv, w_mo=m_w_mo, g_mem_pre=m_g_mem_pre, g_mem_post=m_g_mem_post,
             w_ff1=m_w_ff1, w_ff2=m_w_ff2, g_ff_pre=m_g_ff_pre, g_ff_post=m_g_ff_post)
    v = dict(w_in=v_w_in, b_fgt=v_b_fgt, rel_bias=v_rel_bias, g_fox_out=v_g_fox_out, g_chk_out=v_g_chk_out,
             w_out=v_w_out, g_mix_pre=v_g_mix_pre, g_mix_post=v_g_mix_post, g_mem_kv=v_g_mem_kv, w_mq=v_w_mq,
             w_mk=v_w_mk, w_mv=v_w_mv, w_mo=v_w_mo, g_mem_pre=v_g_mem_pre, g_mem_post=v_g_mem_post,
             w_ff1=v_w_ff1, w_ff2=v_w_ff2, g_ff_pre=v_g_ff_pre, g_ff_post=v_g_ff_post)

    def rows(d, k):
        return d[k][0] if k == "rel_bias" else d[k]

    xs, mems, target = x[0], mem[0], loss_target[0]
    S = xs.shape[0]
    sp = {k: rows(w, k) for k in SMALL}
    b_pad = jnp.pad(sp["b_fgt"], ((0, 0), (0, 120)))
    chip = 2 * lax.axis_index("x") + lax.axis_index("y")
    chip_arr = jnp.reshape(chip, (1,)).astype(jnp.int32)
    c_arr = jnp.reshape(lax.axis_index("c"), (1,)).astype(jnp.int32)
    place_arr = jnp.concatenate([chip_arr, c_arr])
    slab = {k: _cast_slab_call(w[k][0], chip_arr, name="cast_" + k) for k in BIG}

    def gather_ici(names):
        return _ag_ici_task([slab[k] for k in names])

    def pair_add(k, d, r1):
        return _pair_add_call(d, r1, c_arr, name="rs_pair_add_" + k)

    g_in, = _comm_call(gather_ici(["w_in"]), name="ag_w_in")
    h1, (g_in,) = _rms_fwd_call(xs, sp["g_mix_pre"], name="rms_mix_pre", task=_ag_d2d_task([g_in]))
    w_all = _in_cols_to_all(g_in.transpose(1, 0, 2).reshape(D, D_IN))
    proj, (g_out,) = _mm_nn(h1, w_all[:, :3072], "plain", name="mm_proj", task=gather_ici(["w_out"]))
    fl_raw = _mm_nn(h1, w_all[:, 3072:], "plain", name="mm_gate", out_dtype=F32, tn=128)
    c_rep, c_t = _fox_prep_call(fl_raw, b_pad, name="fox_prep")
    bias = _chk_bias_call(_rel_table_to_g(sp["rel_bias"]), name="chk_bias")
    mid = ["w_mq", "w_mk", "w_mv", "w_mo", "w_ff1"]
    (yf, lse), got = _fox_fwd_call(proj, c_rep, c_t, name="fox_fwd",
                                   task=_merge_tasks([gather_ici(mid), _ag_d2d_task([g_out])]))
    g_mid, g_out = got[:5], got[5]
    yc, got = _chk_fwd_call(proj, bias, name="chk_fwd",
                            task=_merge_tasks([gather_ici(["w_ff2"]), _ag_d2d_task(g_mid)]))
    g_ff2, (g_mq, g_mk, g_mv, g_mo, g_ff1) = got[0], got[1:]
    yn = _mix_norm_fwd_call(yf, yc, sp["g_fox_out"], sp["g_chk_out"], name="mix_norm_fwd")
    z, (g_ff2,) = _mm_nn(yn, g_out, "rows", name="mm_out", out_dtype=F32, task=_ag_d2d_task([g_ff2]))
    x1, h2 = _post_pre_call(xs, z, sp["g_mix_post"], sp["g_mem_pre"], name="post_mix")
    memn = _rms_fwd_call(mems, sp["g_mem_kv"], name="rms_mem_kv")
    q2 = _mm_nn(h2, g_mq, "rows", name="mm_mq")
    k2 = _mm_nn(memn, g_mk, "rows", name="mm_mk")
    v2 = _mm_nn(memn, g_mv, "rows", name="mm_mv")
    o2 = _mem_fwd_call(q2, k2, v2, name="mem_fwd")
    y2 = _mm_nn(o2, g_mo, "rows", name="mm_mo", out_dtype=F32)
    x2, h3 = _post_pre_call(x1, y2, sp["g_mem_post"], sp["g_ff_pre"], name="post_mem")
    act, relu = _mm_nn(h3, g_ff1, "cols", name="mm_ff1", epi="relu2")
    y3 = _mm_nn(act, g_ff2, "rows", name="mm_ff2", out_dtype=F32)
    loss_blk, dx3, dy3, dg_ff_post = _final_call(x2, y3, sp["g_ff_post"], target, name="final")

    d_ff2 = _mm_tn(act, dy3, name="mm_dff2").reshape(N_CHIP, D_FF // N_CHIP, D)
    du, (r1,) = _mm_nt(dy3, g_ff2, "rows", name="mm_du", mul2r=relu, task=_rs_pair_task([d_ff2]))
    p_ff2 = pair_add("w_ff2", d_ff2, r1)
    d_ff1 = _mm_tn(h3, du, name="mm_dff1", cols4=True)
    dh3, (r1,) = _mm_nt(du, g_ff1, "cols", name="mm_dh3", out_dtype=F32, task=_rs_pair_task([d_ff1]))
    p_ff1 = pair_add("w_ff1", d_ff1, r1)
    dx2, dy2, dg_ff_pre, dg_mem_post = _bwd_mid_call(dx3, x2, dh3, sp["g_ff_pre"], y2, sp["g_mem_post"], name="bwd_ff")
    d_mo = _mm_tn(o2, dy2, name="mm_dmo").reshape(N_CHIP, D // N_CHIP, D)
    do2 = _mm_nt(dy2, g_mo, "rows", name="mm_do2")
    dq2, dk2, dv2 = _mem_bwd_call(q2, k2, v2, do2, name="mem_bwd")
    d_mq = _mm_tn(h2, dq2, name="mm_dmq").reshape(N_CHIP, D // N_CHIP, D)
    dh2 = _mm_nt(dq2, g_mq, "rows", name="mm_dh2", out_dtype=F32)
    d_mk = _mm_tn(memn, dk2, name="mm_dmk").reshape(N_CHIP, D // N_CHIP, D)
    d_mv = _mm_tn(memn, dv2, name="mm_dmv").reshape(N_CHIP, D // N_CHIP, D)
    dmn_k = _mm_nt(dk2, g_mk, "rows", name="mm_dmemk", out_dtype=F32)
    dmn_v = _mm_nt(dv2, g_mv, "rows", name="mm_dmemv", out_dtype=F32)
    dg_mem_kv = _gain_grad_call(mems, sp["g_mem_kv"], dmn_k, dmn_v, name="gain_mem_kv")
    dx1, dz, dg_mem_pre, dg_mix_post = _bwd_mid_call(dx2, x1, dh2, sp["g_mem_pre"], z, sp["g_mix_post"], name="bwd_mem")
    d_out = _mm_tn(yn, dz, name="mm_dout").reshape(N_CHIP, D // N_CHIP, D)
    late = ["w_mo", "w_mq", "w_mk", "w_mv", "w_out"]
    d_late = [d_mo, d_mq, d_mk, d_mv, d_out]
    dyn, r1_late = _mm_nt(dz, g_out, "rows", name="mm_dyn", out_dtype=F32, task=_rs_pair_task(d_late))
    p_late = [pair_add(k, d, r1) for k, d, r1 in zip(late, d_late, r1_late)]
    dof, doc, delta, dg_fox, dg_chk = _mix_norm_bwd_call(dyn, yf, yc, sp["g_fox_out"], sp["g_chk_out"], name="mix_norm_bwd")
    (dqf, dkf, dvf, dct, dcq), r2_ff = _fox_bwd_call(proj, dof, lse, delta, c_rep, c_t, name="fox_bwd",
                                                      task=_rs_chip_task([p_ff2, p_ff1]))
    (dqc, dkc, dvc, dgrev), r2_late = _chk_bwd_call(proj, doc, bias, name="chk_bwd", task=_rs_chip_task(p_late))
    first = ["w_ff2", "w_ff1"] + late
    f_first = [_chip_sum_call(p, r, place_arr, name="rs_chip_sum_" + k)
               for k, p, r in zip(first, [p_ff2, p_ff1] + p_late, r2_ff + r2_late)]
    dc8 = dct[:, :, 0:2, :].transpose(0, 2, 1, 3).reshape(8, S) + dcq[:, ::HEAD].T
    dc_rows = jnp.concatenate([dc8, jnp.zeros((120, S), F32)], axis=0)
    dfl, db_fgt = _fox_gate_bwd_call(dc_rows, fl_raw, b_pad, name="fox_gate_bwd")
    dproj = jnp.concatenate([dqf, dkf, dvf, dqc, dkc, dvc, dfl], axis=1)
    d_all, g_first = _mm_tn(h1, dproj, name="mm_dwin", tn=640, task=_rs_gather_task(f_first))
    grads = dict(zip(first, g_first))
    d_in = _all_cols_to_in(d_all).reshape(D, N_CHIP, D_IN // N_CHIP).transpose(1, 0, 2)
    dh1, (r1,) = _mm_nt(dproj, w_all, "plain", name="mm_dh1", out_dtype=F32, task=_rs_pair_task([d_in]))
    p_in = pair_add("w_in", d_in, r1)
    delta_w, new_m, new_v = {}, {}, {}
    upd, (r2_in,) = _adamw_call([(w[k][0], grads[k], m[k][0], v[k][0]) for k in first], name="adamw_7", tm=64,
                                task=_rs_chip_task([p_in]))
    for k, res in zip(first, upd):
        grads[k], delta_w[k], new_m[k], new_v[k] = res
    f_in = _chip_sum_call(p_in, r2_in, place_arr, name="rs_chip_sum_w_in")
    grad_x, dg_mix_pre = _bwd_last_call(dx1, xs, dh1, sp["g_mix_pre"], name="bwd_mix")

    small_g = {"g_mix_pre": dg_mix_pre, "g_mix_post": dg_mix_post, "g_mem_kv": dg_mem_kv, "g_mem_pre": dg_mem_pre,
               "g_mem_post": dg_mem_post, "g_ff_pre": dg_ff_pre, "g_ff_post": dg_ff_post, "g_fox_out": dg_fox,
               "g_chk_out": dg_chk, "b_fgt": db_fgt,
               "rel_bias": _g_to_rel_table(dgrev[:, 0:2, :].reshape(8, ROLL_W))}
    small, loss_out, (g_w_in,) = _small_call(
        small_g, sp, {k: rows(m, k) for k in SMALL}, {k: rows(v, k) for k in SMALL}, loss_blk,
        _rs_gather_task([f_in]), name="small_allreduce_adamw")
    loss = loss_out[0, 0]
    (grads["w_in"], delta_w["w_in"], new_m["w_in"], new_v["w_in"]), = _adamw_call(
        [(w["w_in"][0], g_w_in, m["w_in"][0], v["w_in"][0])], name="adamw_w_in")
    for k in SMALL:
        vals = small[k]
        if k == "rel_bias":
            vals = tuple(a[None] for a in vals)
        grads[k], delta_w[k], new_m[k], new_v[k] = vals

    def out(d, k):
        return d[k][None] if k in BIG else d[k]

    return (loss, grad_x[None], *[out(grads, k) for k in WEIGHTS], *[out(delta_w, k) for k in WEIGHTS],
            *[out(new_m, k) for k in WEIGHTS], *[out(new_v, k) for k in WEIGHTS])
```

```python
import functools

import jax
import jax.numpy as jnp
from jax import lax
from jax.experimental import pallas as pl
from jax.experimental.pallas import tpu as pltpu

F32 = jnp.float32
BF16 = jnp.bfloat16

D = 1024
HEAD = 64
N_PAIR = 4
D_GRP = 512
CHUNK = 64
LEFT = 8
MAX_REL = 128
N_REL = 2 * MAX_REL + 1
N_MEM = 256
MEM_HEADS = 4
MEM_HD = 256
D_FF = 4096
D_IN = 3080
D_ALL = 3200
EPS = 1e-6
TQ = 256
WIN = (LEFT + TQ // CHUNK) * CHUNK
PADK = LEFT * CHUNK
ROLL_W = 1024
NEG = -1e30
N_CHIP = 4
VMEM_LIMIT = 48 * 1024 * 1024

ADAM_LR = 0.001
ADAM_B1 = 0.9
ADAM_B2 = 0.999
ADAM_EPS = 1e-08
ADAM_WD = 0.01
ADAM_STEP = 10

MESH = pl.DeviceIdType.MESH


def _cparams():
    return pltpu.CompilerParams(vmem_limit_bytes=VMEM_LIMIT)


ANY = pl.BlockSpec(memory_space=pl.ANY)


class _Task:
    def __init__(self, arrays, out_shapes, sems, issue, drain, aliases=None):
        self.arrays, self.out_shapes, self.sems = list(arrays), list(out_shapes), list(sems)
        self.issue, self.drain, self.aliases = issue, drain, dict(aliases or {})


def _merge_tasks(tasks):
    tasks = [t for t in tasks if t is not None]
    if len(tasks) == 1:
        return tasks[0]
    cuts, a, o, s = [], 0, 0, 0
    aliases = {}
    for t in tasks:
        cuts.append((a, o, s))
        aliases.update({a + i: o + j for i, j in t.aliases.items()})
        a, o, s = a + len(t.arrays), o + len(t.out_shapes), s + len(t.sems)

    def part(fn_name):
        def run(ins, outs, sems):
            for t, (a0, o0, s0) in zip(tasks, cuts):
                getattr(t, fn_name)(ins[a0:a0 + len(t.arrays)], outs[o0:o0 + len(t.out_shapes)],
                                    sems[s0:s0 + len(t.sems)])
        return run

    return _Task([x for t in tasks for x in t.arrays], [x for t in tasks for x in t.out_shapes],
                 [x for t in tasks for x in t.sems], part("issue"), part("drain"), aliases)


def _pallas(body, *, grid, in_specs, out_specs, out_shape, name, scratch_shapes=(), task=None):
    if task is None:
        return pl.pallas_call(body, grid=grid, in_specs=list(in_specs), out_specs=out_specs, out_shape=out_shape,
                              scratch_shapes=list(scratch_shapes), name=name, compiler_params=_cparams())
    single = not isinstance(out_shape, (tuple, list))
    o_shapes = [out_shape] if single else list(out_shape)
    o_specs = [out_specs] if single else list(out_specs)
    n_in, n_out, n_scr = len(in_specs), len(o_shapes), len(scratch_shapes)
    t_in, t_out = len(task.arrays), len(task.out_shapes)

    def carried(*refs):
        cut = [n_in, t_in, n_out, t_out, n_scr]
        parts, p = [], 0
        for c in cut:
            parts.append(refs[p:p + c])
            p += c
        ins, tins, outs, touts, scr = parts
        tsems = refs[p:]
        ids = [pl.program_id(a) for a in range(len(grid))]
        first = functools.reduce(jnp.logical_and, [i == 0 for i in ids])
        last = functools.reduce(jnp.logical_and, [i == g - 1 for i, g in zip(ids, grid)])

        @pl.when(first)
        def _():
            task.issue(tins, touts, tsems)
        body(*ins, *outs, *scr)

        @pl.when(last)
        def _():
            task.drain(tins, touts, tsems)

    call = pl.pallas_call(
        carried, grid=grid, in_specs=list(in_specs) + [ANY] * t_in, out_specs=o_specs + [ANY] * t_out,
        out_shape=o_shapes + list(task.out_shapes), scratch_shapes=list(scratch_shapes) + list(task.sems),
        input_output_aliases={n_in + i: n_out + j for i, j in task.aliases.items()},
        name=name, compiler_params=_cparams())

    def run(*args):
        res = call(*args, *task.arrays)
        outs = res[:n_out]
        return (outs[0] if single else tuple(outs)), list(res[n_out:])

    return run


def _comm_call(task, *, name):
    t_in, t_out = len(task.arrays), len(task.out_shapes)

    def body(*refs):
        tins, touts, tsems = refs[:t_in], refs[t_in:t_in + t_out], refs[t_in + t_out:]
        task.issue(tins, touts, tsems)
        task.drain(tins, touts, tsems)

    return pl.pallas_call(
        body, in_specs=[ANY] * t_in, out_specs=[ANY] * t_out, out_shape=list(task.out_shapes),
        scratch_shapes=list(task.sems), input_output_aliases=dict(task.aliases), name=name,
    )(*task.arrays)


def _dot(a, b):
    return jnp.dot(a, b, preferred_element_type=F32)


def _dot_nt(a, b):
    return lax.dot_general(a, b, (((1,), (1,)), ((), ())), preferred_element_type=F32)


def _dot_tn(a, b):
    return lax.dot_general(a, b, (((0,), (0,)), ((), ())), preferred_element_type=F32)


def _split3(x):
    hi = x.astype(BF16)
    r1 = x - hi.astype(F32)
    mid = r1.astype(BF16)
    lo = (r1 - mid.astype(F32)).astype(BF16)
    return hi, mid, lo


def _dot3(x, m01):
    hi, mid, lo = _split3(x)
    return _dot(hi, m01) + _dot(mid, m01) + _dot(lo, m01)


def _dot3_l(m01, x):
    hi, mid, lo = _split3(x)
    return _dot(m01, hi) + _dot(m01, mid) + _dot(m01, lo)


def _mm_nn(a, b, kind, *, name, out_dtype=BF16, tm=512, tn=512, epi=None, task=None):
    M, K = a.shape
    if kind == "plain":
        N = b.shape[1]
        b_spec = pl.BlockSpec((K, tn), lambda m, n: (0, n))
    elif kind == "rows":
        N = b.shape[2]
        b_spec = pl.BlockSpec((N_CHIP, K // N_CHIP, tn), lambda m, n: (0, 0, n))
    else:
        nq = b.shape[2]
        N = N_CHIP * nq
        per = nq // tn
        b_spec = pl.BlockSpec((None, K, tn), lambda m, n: (n // per, 0, n % per))
    tm = min(tm, M)
    kq = K // N_CHIP

    def body(a_ref, b_ref, *o_refs):
        if kind == "rows":
            acc = _dot(a_ref[:, 0:kq], b_ref[0])
            for j in range(1, N_CHIP):
                acc += _dot(a_ref[:, j * kq:(j + 1) * kq], b_ref[j])
        else:
            acc = _dot(a_ref[...], b_ref[...])
        if epi == "relu2":
            r = jnp.maximum(acc, 0.0)
            o_refs[0][...] = (r * r).astype(BF16)
            o_refs[1][...] = r.astype(BF16)
        else:
            o_refs[0][...] = acc.astype(out_dtype)

    o_spec = pl.BlockSpec((tm, tn), lambda m, n: (m, n))
    if epi == "relu2":
        out_shape = (jax.ShapeDtypeStruct((M, N), BF16), jax.ShapeDtypeStruct((M, N), BF16))
        out_specs = (o_spec, o_spec)
    else:
        out_shape = jax.ShapeDtypeStruct((M, N), out_dtype)
        out_specs = o_spec
    return _pallas(
        body, grid=(M // tm, N // tn),
        in_specs=[pl.BlockSpec((tm, K), lambda m, n: (m, 0)), b_spec],
        out_specs=out_specs, out_shape=out_shape, name=name, task=task,
    )(a, b)


def _mm_nt(a, b, kind, *, name, out_dtype=BF16, tm=512, tn=512, mul2r=None, task=None, rows=None):
    M, K = a.shape
    if kind == "plain":
        first, N = rows if rows is not None else (0, b.shape[0])
        n0 = first // tn
        b_spec = pl.BlockSpec((tn, K), lambda m, n: (n0 + n, 0))
    elif kind == "rows":
        nq = b.shape[1]
        N = N_CHIP * nq
        tn = min(tn, nq)
        per = nq // tn
        b_spec = pl.BlockSpec((None, tn, K), lambda m, n: (n // per, n % per, 0))
    else:
        N = b.shape[1]
        b_spec = pl.BlockSpec((N_CHIP, tn, K // N_CHIP), lambda m, n: (0, n, 0))
    tm = min(tm, M)
    kq = K // N_CHIP

    def body(a_ref, b_ref, *rest):
        o_ref = rest[-1]
        if kind == "cols":
            acc = _dot_nt(a_ref[:, 0:kq], b_ref[0])
            for j in range(1, N_CHIP):
                acc += _dot_nt(a_ref[:, j * kq:(j + 1) * kq], b_ref[j])
        else:
            acc = _dot_nt(a_ref[...], b_ref[...])
        if mul2r is not None:
            acc = acc * (2.0 * rest[0][...].astype(F32))
        o_ref[...] = acc.astype(out_dtype)

    in_specs = [pl.BlockSpec((tm, K), lambda m, n: (m, 0)), b_spec]
    args = [a, b]
    if mul2r is not None:
        in_specs.append(pl.BlockSpec((tm, tn), lambda m, n: (m, n)))
        args.append(mul2r)
    return _pallas(
        body, grid=(M // tm, N // tn), in_specs=in_specs,
        out_specs=pl.BlockSpec((tm, tn), lambda m, n: (m, n)),
        out_shape=jax.ShapeDtypeStruct((M, N), out_dtype), name=name, task=task,
    )(*args)


def _mm_tn(a, b, *, name, out_dtype=BF16, tk=512, tn=512, cols4=False, task=None):
    M, K1 = a.shape
    N = b.shape[1]
    tk = min(tk, K1)
    tn = min(tn, N)

    def body(a_ref, b_ref, o_ref):
        o_ref[...] = _dot_tn(a_ref[...], b_ref[...]).astype(out_dtype)

    if cols4:
        per = (N // N_CHIP) // tn
        out_shape = jax.ShapeDtypeStruct((N_CHIP, K1, N // N_CHIP), out_dtype)
        o_spec = pl.BlockSpec((None, tk, tn), lambda k, n: (n // per, k, n % per))
    else:
        out_shape = jax.ShapeDtypeStruct((K1, N), out_dtype)
        o_spec = pl.BlockSpec((tk, tn), lambda k, n: (k, n))
    return _pallas(
        body, grid=(K1 // tk, N // tn),
        in_specs=[pl.BlockSpec((M, tk), lambda k, n: (0, k)), pl.BlockSpec((M, tn), lambda k, n: (0, n))],
        out_specs=o_spec, out_shape=out_shape, name=name, task=task,
    )(a, b)


def _rms(x, g):
    r = lax.rsqrt(jnp.mean(x * x, axis=-1, keepdims=True) + EPS)
    return x * r * g


def _rms_bwd(x, g, dy):
    r = lax.rsqrt(jnp.mean(x * x, axis=-1, keepdims=True) + EPS)
    xh = x * r
    dg = jnp.sum(dy * xh, axis=0, keepdims=True)
    dxh = dy * g
    dx = r * (dxh - xh * jnp.mean(dxh * xh, axis=-1, keepdims=True))
    return dx, dg


def _row_spec(tm, n):
    return pl.BlockSpec((tm, n), lambda i: (i, 0))


def _vec_spec(n):
    return pl.BlockSpec((1, n), lambda i: (0, 0))


def _acc_spec(n):
    return pl.BlockSpec((8, n), lambda i: (0, 0))


def _acc_add(ref, row, i):
    @pl.when(i == 0)
    def _():
        ref[...] = jnp.zeros_like(ref)
    ref[0:1, :] += row


def _rms_fwd_call(x, g, *, name, tm=256, task=None):
    M, n = x.shape
    tm = min(tm, M)

    def body(x_ref, g_ref, h_ref):
        h_ref[...] = _rms(x_ref[...], g_ref[...]).astype(BF16)

    return _pallas(
        body, grid=(M // tm,), in_specs=[_row_spec(tm, n), _vec_spec(n)], out_specs=_row_spec(tm, n),
        out_shape=jax.ShapeDtypeStruct((M, n), BF16), name=name, task=task,
    )(x, g)


def _post_pre_call(xres, z, g_post, g_pre, *, name, tm=256):
    M, n = xres.shape

    def body(x_ref, z_ref, gp_ref, gn_ref, xo_ref, h_ref):
        xn = x_ref[...] + _rms(z_ref[...], gp_ref[...])
        xo_ref[...] = xn
        h_ref[...] = _rms(xn, gn_ref[...]).astype(BF16)

    return pl.pallas_call(
        body, grid=(M // tm,),
        in_specs=[_row_spec(tm, n), _row_spec(tm, n), _vec_spec(n), _vec_spec(n)],
        out_specs=(_row_spec(tm, n), _row_spec(tm, n)),
        out_shape=(jax.ShapeDtypeStruct((M, n), F32), jax.ShapeDtypeStruct((M, n), BF16)),
        name=name, compiler_params=_cparams(),
    )(xres, z, g_post, g_pre)


def _final_call(x2, y3, g_post, target, *, name, tm=256):
    M, n = x2.shape

    def body(x_ref, y_ref, g_ref, t_ref, loss_ref, dx_ref, dy_ref, dg_ref):
        i = pl.program_id(0)
        y = y_ref[...]
        g = g_ref[...]
        diff = x_ref[...] + _rms(y, g) - t_ref[...]
        part = 0.5 * jnp.sum(jnp.sum(diff * diff, axis=1, keepdims=True), axis=0, keepdims=True) / n

        @pl.when(i == 0)
        def _():
            loss_ref[...] = jnp.zeros_like(loss_ref)
        loss_ref[...] += jnp.broadcast_to(part, loss_ref.shape)
        dx = diff / n
        dx_ref[...] = dx
        dy, dg = _rms_bwd(y, g, dx)
        dy_ref[...] = dy.astype(BF16)
        _acc_add(dg_ref, dg, i)

    return pl.pallas_call(
        body, grid=(M // tm,),
        in_specs=[_row_spec(tm, n), _row_spec(tm, n), _vec_spec(n), _row_spec(tm, n)],
        out_specs=(pl.BlockSpec((8, 128), lambda i: (0, 0)), _row_spec(tm, n), _row_spec(tm, n), _acc_spec(n)),
        out_shape=(jax.ShapeDtypeStruct((8, 128), F32), jax.ShapeDtypeStruct((M, n), F32),
                   jax.ShapeDtypeStruct((M, n), BF16), jax.ShapeDtypeStruct((8, n), F32)),
        name=name, compiler_params=_cparams(),
    )(x2, y3, g_post, target)


def _bwd_mid_call(dx_in, x, dh, g_pre, y, g_post, *, name, tm=256):
    M, n = x.shape

    def body(dxi_ref, x_ref, dh_ref, gpre_ref, y_ref, gpost_ref, dx_ref, dy_ref, dgpre_ref, dgpost_ref):
        i = pl.program_id(0)
        d1, dg1 = _rms_bwd(x_ref[...], gpre_ref[...], dh_ref[...])
        dx = dxi_ref[...] + d1
        dx_ref[...] = dx
        dy, dg2 = _rms_bwd(y_ref[...], gpost_ref[...], dx)
        dy_ref[...] = dy.astype(BF16)
        _acc_add(dgpre_ref, dg1, i)
        _acc_add(dgpost_ref, dg2, i)

    return pl.pallas_call(
        body, grid=(M // tm,),
        in_specs=[_row_spec(tm, n), _row_spec(tm, n), _row_spec(tm, n), _vec_spec(n), _row_spec(tm, n), _vec_spec(n)],
        out_specs=(_row_spec(tm, n), _row_spec(tm, n), _acc_spec(n), _acc_spec(n)),
        out_shape=(jax.ShapeDtypeStruct((M, n), F32), jax.ShapeDtypeStruct((M, n), BF16),
                   jax.ShapeDtypeStruct((8, n), F32), jax.ShapeDtypeStruct((8, n), F32)),
        name=name, compiler_params=_cparams(),
    )(dx_in, x, dh, g_pre, y, g_post)


def _bwd_last_call(dx_in, x, dh, g_pre, *, name, tm=256, task=None):
    M, n = x.shape

    def body(dxi_ref, x_ref, dh_ref, g_ref, dx_ref, dg_ref):
        i = pl.program_id(0)
        d1, dg1 = _rms_bwd(x_ref[...], g_ref[...], dh_ref[...])
        dx_ref[...] = dxi_ref[...] + d1
        _acc_add(dg_ref, dg1, i)

    return _pallas(
        body, grid=(M // tm,),
        in_specs=[_row_spec(tm, n), _row_spec(tm, n), _row_spec(tm, n), _vec_spec(n)],
        out_specs=(_row_spec(tm, n), _acc_spec(n)),
        out_shape=(jax.ShapeDtypeStruct((M, n), F32), jax.ShapeDtypeStruct((8, n), F32)),
        name=name, task=task,
    )(dx_in, x, dh, g_pre)


def _gain_grad_call(x, g, dy_a, dy_b, *, name):
    M, n = x.shape

    def body(x_ref, g_ref, a_ref, b_ref, dg_ref):
        _, dg = _rms_bwd(x_ref[...], g_ref[...], a_ref[...] + b_ref[...])
        dg_ref[...] = jnp.zeros_like(dg_ref)
        dg_ref[0:1, :] = dg

    return pl.pallas_call(
        body, grid=(1,),
        in_specs=[_row_spec(M, n), _vec_spec(n), _row_spec(M, n), _row_spec(M, n)],
        out_specs=_acc_spec(n), out_shape=jax.ShapeDtypeStruct((8, n), F32),
        name=name, compiler_params=_cparams(),
    )(x, g, dy_a, dy_b)


def _head_group_matrix():
    a = lax.broadcasted_iota(jnp.int32, (D_GRP, D_GRP), 0) // HEAD
    b = lax.broadcasted_iota(jnp.int32, (D_GRP, D_GRP), 1) // HEAD
    return jnp.where(a == b, 1.0, 0.0).astype(BF16)


def _mix_norm_fwd_call(yf, yc, gf, gc, *, name, tm=256):
    M = yf.shape[0]

    def body(yf_ref, yc_ref, gf_ref, gc_ref, o_ref):
        o_ref[:, 0:D_GRP] = _rms(yf_ref[...], gf_ref[...]).astype(BF16)
        o_ref[:, D_GRP:D] = _rms(yc_ref[...], gc_ref[...]).astype(BF16)

    return pl.pallas_call(
        body, grid=(M // tm,),
        in_specs=[_row_spec(tm, D_GRP), _row_spec(tm, D_GRP), _vec_spec(D_GRP), _vec_spec(D_GRP)],
        out_specs=_row_spec(tm, D), out_shape=jax.ShapeDtypeStruct((M, D), BF16),
        name=name, compiler_params=_cparams(),
    )(yf, yc, gf, gc)


def _mix_norm_bwd_call(dyn, yf, yc, gf, gc, *, name, tm=256):
    M = yf.shape[0]

    def body(dyn_ref, yf_ref, yc_ref, gf_ref, gc_ref, dof_ref, doc_ref, delta_ref, dgf_ref, dgc_ref):
        i = pl.program_id(0)
        yf_ = yf_ref[...]
        dof, dgf = _rms_bwd(yf_, gf_ref[...], dyn_ref[:, 0:D_GRP])
        doc, dgc = _rms_bwd(yc_ref[...], gc_ref[...], dyn_ref[:, D_GRP:D])
        dof_b = dof.astype(BF16)
        dof_ref[...] = dof_b
        doc_ref[...] = doc.astype(BF16)
        prod = dof_b.astype(F32) * yf_
        hi = prod.astype(BF16)
        lo = (prod - hi.astype(F32)).astype(BF16)
        grp = _head_group_matrix()
        delta_ref[...] = _dot(hi, grp) + _dot(lo, grp)
        _acc_add(dgf_ref, dgf, i)
        _acc_add(dgc_ref, dgc, i)

    return pl.pallas_call(
        body, grid=(M // tm,),
        in_specs=[_row_spec(tm, D), _row_spec(tm, D_GRP), _row_spec(tm, D_GRP), _vec_spec(D_GRP), _vec_spec(D_GRP)],
        out_specs=(_row_spec(tm, D_GRP), _row_spec(tm, D_GRP), _row_spec(tm, D_GRP), _acc_spec(D_GRP), _acc_spec(D_GRP)),
        out_shape=(jax.ShapeDtypeStruct((M, D_GRP), BF16), jax.ShapeDtypeStruct((M, D_GRP), BF16),
                   jax.ShapeDtypeStruct((M, D_GRP), F32), jax.ShapeDtypeStruct((8, D_GRP), F32),
                   jax.ShapeDtypeStruct((8, D_GRP), F32)),
        name=name, compiler_params=_cparams(),
    )(dyn, yf, yc, gf, gc)


def _tri(n, lower_incl):
    a = lax.broadcasted_iota(jnp.int32, (n, n), 0)
    b = lax.broadcasted_iota(jnp.int32, (n, n), 1)
    return jnp.where(a >= b, 1.0, 0.0).astype(BF16) if lower_incl else jnp.where(a <= b, 1.0, 0.0).astype(BF16)


def _fox_prep_call(fl_raw, b_pad, *, name):
    S = fl_raw.shape[0]
    nb = S // TQ

    def body(fl_ref, b_ref, crep_ref, ct_ref, carry_ref):
        i = pl.program_id(0)

        @pl.when(i == 0)
        def _():
            carry_ref[...] = jnp.zeros_like(carry_ref)
        logf = jax.nn.log_sigmoid(fl_ref[...] + b_ref[...])
        cb = _dot3_l(_tri(TQ, True), logf) + carry_ref[0:1, :]
        carry_ref[0:1, :] = cb[TQ - 1:TQ, :]
        a = lax.broadcasted_iota(jnp.int32, (128, D_GRP), 0)
        b = lax.broadcasted_iota(jnp.int32, (128, D_GRP), 1) // HEAD
        expand = jnp.where(a == b, 1.0, 0.0).astype(BF16)
        crep = _dot3(cb, expand)
        crep_ref[...] = crep
        ct_ref[...] = crep.T

    return pl.pallas_call(
        body, grid=(nb,),
        in_specs=[_row_spec(TQ, 128), _vec_spec(128)],
        out_specs=(_row_spec(TQ, D_GRP), pl.BlockSpec((None, D_GRP, TQ), lambda i: (i, 0, 0))),
        out_shape=(jax.ShapeDtypeStruct((S, D_GRP), F32), jax.ShapeDtypeStruct((nb, D_GRP, TQ), F32)),
        scratch_shapes=[pltpu.VMEM((8, 128), F32)],
        name=name, compiler_params=_cparams(),
    )(fl_raw, b_pad)


def _lane_masks():
    lane = lax.broadcasted_iota(jnp.int32, (1, 128), 1)
    return lane < HEAD, lane >= HEAD


def _fox_fwd_call(proj, c_rep, c_t, *, name, task=None):
    S = proj.shape[0]
    nq = S // TQ
    scale = HEAD ** -0.5

    def body(q_ref, k_ref, v_ref, c_ref, ct_ref, o_ref, lse_ref):
        i = pl.program_id(1)
        m_lo, m_hi = _lane_masks()
        masks = (m_lo, m_hi)
        q = q_ref[...]
        qm = [jnp.where(mk, q, jnp.zeros_like(q)) for mk in masks]
        cq = c_ref[...]
        cqh = [cq[:, 0:1], cq[:, HEAD:HEAD + 1]]
        row = lax.broadcasted_iota(jnp.int32, (TQ, TQ), 0)
        col = lax.broadcasted_iota(jnp.int32, (TQ, TQ), 1)

        def step(j, carry, masked):
            ms, ls, acc = carry
            start = pl.multiple_of(j * TQ, TQ)
            k = k_ref[pl.ds(start, TQ), :]
            v = v_ref[pl.ds(start, TQ), :]
            ct = ct_ref[j]
            new_m, new_l, pv, alpha_l = [], [], [], []
            for h in range(2):
                s = _dot_nt(qm[h], k) * scale + (cqh[h] - ct[HEAD * h:HEAD * h + 1, :])
                if masked:
                    s = jnp.where(row >= col, s, NEG)
                mn = jnp.maximum(ms[h], jnp.max(s, axis=1, keepdims=True))
                alpha = jnp.exp(ms[h] - mn)
                p = jnp.exp(s - mn)
                new_l.append(alpha * ls[h] + jnp.sum(p, axis=1, keepdims=True))
                new_m.append(mn)
                alpha_l.append(alpha)
                pv.append(_dot(p.astype(BF16), jnp.where(masks[h], v, jnp.zeros_like(v))))
            alpha_lane = jnp.where(m_lo, alpha_l[0], alpha_l[1])
            acc = acc * alpha_lane + pv[0] + pv[1]
            return (tuple(new_m), tuple(new_l), acc)

        init = ((jnp.full((TQ, 1), NEG, F32),) * 2, (jnp.zeros((TQ, 1), F32),) * 2, jnp.zeros((TQ, 128), F32))
        carry = lax.fori_loop(0, i, lambda j, c: step(j, c, False), init)
        ms, ls, acc = step(i, carry, True)
        l_lane = jnp.where(m_lo, ls[0], ls[1])
        o_ref[...] = acc / l_lane
        lse_ref[...] = jnp.where(m_lo, ms[0] + jnp.log(ls[0]), ms[1] + jnp.log(ls[1]))

    return _pallas(
        body, grid=(N_PAIR, nq),
        in_specs=[pl.BlockSpec((TQ, 128), lambda p, i: (i, p)),
                  pl.BlockSpec((S, 128), lambda p, i: (0, N_PAIR + p)),
                  pl.BlockSpec((S, 128), lambda p, i: (0, 2 * N_PAIR + p)),
                  pl.BlockSpec((TQ, 128), lambda p, i: (i, p)),
                  pl.BlockSpec((nq, 128, TQ), lambda p, i: (0, p, 0))],
        out_specs=(pl.BlockSpec((TQ, 128), lambda p, i: (i, p)), pl.BlockSpec((TQ, 128), lambda p, i: (i, p))),
        out_shape=(jax.ShapeDtypeStruct((S, D_GRP), F32), jax.ShapeDtypeStruct((S, D_GRP), F32)),
        name=name, task=task,
    )(proj, proj, proj, c_rep, c_t)


def _fox_bwd_call(proj, do, lse_rep, delta_rep, c_rep, c_t, *, name, task=None):
    S = proj.shape[0]
    nq = S // TQ
    scale = HEAD ** -0.5

    def body(q_ref, k_ref, v_ref, do_ref, lse_ref, dl_ref, c_ref, ct_ref, dq_ref, dk_ref, dv_ref, dct_ref, dcq_ref, dqa_ref):
        j = pl.program_id(1)
        m_lo, m_hi = _lane_masks()
        masks = (m_lo, m_hi)

        @pl.when(j == 0)
        def _():
            dqa_ref[...] = jnp.zeros_like(dqa_ref)
            dcq_ref[...] = jnp.zeros_like(dcq_ref)
        k = k_ref[...]
        v = v_ref[...]
        km = [jnp.where(mk, k, jnp.zeros_like(k)) for mk in masks]
        ct = ct_ref[...]
        row = lax.broadcasted_iota(jnp.int32, (TQ, TQ), 0)
        col = lax.broadcasted_iota(jnp.int32, (TQ, TQ), 1)

        def step(i, carry, masked):
            dk, dv, dcs = carry
            start = pl.multiple_of(i * TQ, TQ)
            q = q_ref[pl.ds(start, TQ), :]
            do = do_ref[pl.ds(start, TQ), :]
            lse = lse_ref[pl.ds(start, TQ), :]
            dl = dl_ref[pl.ds(start, TQ), :]
            cq = c_ref[pl.ds(start, TQ), :]
            dq = jnp.zeros((TQ, 128), F32)
            new_dcs = []
            rows = []
            for h in range(2):
                lo = HEAD * h
                qm = jnp.where(masks[h], q, jnp.zeros_like(q))
                dom = jnp.where(masks[h], do, jnp.zeros_like(do))
                s = _dot_nt(qm, k) * scale + (cq[:, lo:lo + 1] - ct[lo:lo + 1, :])
                p = jnp.exp(s - lse[:, lo:lo + 1])
                if masked:
                    p = jnp.where(row >= col, p, 0.0)
                dp = _dot_nt(dom, v)
                ds = p * (dp - dl[:, lo:lo + 1])
                new_dcs.append(dcs[h] + jnp.sum(ds, axis=0, keepdims=True))
                rows.append(jnp.sum(ds, axis=1, keepdims=True))
                dsb = (ds * scale).astype(BF16)
                dv = dv + _dot_tn(p.astype(BF16), dom)
                dk = dk + _dot_tn(dsb, qm)
                dq = dq + _dot(dsb, km[h])
            dqa_ref[pl.ds(start, TQ), :] += dq
            dcq_ref[pl.ds(start, TQ), :] += jnp.where(m_lo, rows[0], rows[1])
            return (dk, dv, tuple(new_dcs))

        init = (jnp.zeros((TQ, 128), F32), jnp.zeros((TQ, 128), F32), (jnp.zeros((1, TQ), F32),) * 2)
        carry = step(j, init, True)
        dk, dv, dcs = lax.fori_loop(j + 1, nq, lambda i, c: step(i, c, False), carry)
        dk_ref[...] = dk.astype(BF16)
        dv_ref[...] = dv.astype(BF16)
        dct_ref[...] = jnp.zeros_like(dct_ref)
        dct_ref[0:1, :] = -dcs[0]
        dct_ref[1:2, :] = -dcs[1]

        @pl.when(j == nq - 1)
        def _():
            dq_ref[...] = dqa_ref[...].astype(BF16)

    res = lambda p, j: (0, p)
    return _pallas(
        body, grid=(N_PAIR, nq), task=task,
        in_specs=[pl.BlockSpec((S, 128), res),
                  pl.BlockSpec((TQ, 128), lambda p, j: (j, N_PAIR + p)),
                  pl.BlockSpec((TQ, 128), lambda p, j: (j, 2 * N_PAIR + p)),
                  pl.BlockSpec((S, 128), res), pl.BlockSpec((S, 128), res), pl.BlockSpec((S, 128), res),
                  pl.BlockSpec((S, 128), res),
                  pl.BlockSpec((None, 128, TQ), lambda p, j: (j, p, 0))],
        out_specs=(pl.BlockSpec((S, 128), res),
                   pl.BlockSpec((TQ, 128), lambda p, j: (j, p)), pl.BlockSpec((TQ, 128), lambda p, j: (j, p)),
                   pl.BlockSpec((None, None, 8, TQ), lambda p, j: (p, j, 0, 0)),
                   pl.BlockSpec((S, 128), res)),
        out_shape=(jax.ShapeDtypeStruct((S, D_GRP), BF16), jax.ShapeDtypeStruct((S, D_GRP), BF16),
                   jax.ShapeDtypeStruct((S, D_GRP), BF16), jax.ShapeDtypeStruct((N_PAIR, nq, 8, TQ), F32),
                   jax.ShapeDtypeStruct((S, D_GRP), F32)),
        scratch_shapes=[pltpu.VMEM((S, 128), F32)],
        name=name,
    )(proj, proj, proj, do, lse_rep, delta_rep, c_rep, c_t)


def _fox_gate_bwd_call(dc_rows, fl_raw, b_pad, *, name):
    S = fl_raw.shape[0]
    nb = S // TQ

    def body(dc_ref, fl_ref, b_ref, dfl_ref, db_ref, carry_ref):
        i = pl.program_id(0)

        @pl.when(i == 0)
        def _():
            carry_ref[...] = jnp.zeros_like(carry_ref)
        rc = _dot3(dc_ref[...], _tri(TQ, True)) + carry_ref[:, 0:1]
        carry_ref[...] = jnp.broadcast_to(rc[:, 0:1], carry_ref.shape)
        fl = fl_ref[...] + b_ref[...]
        dfl = rc.T * jax.nn.sigmoid(-fl)
        dfl_ref[...] = dfl.astype(BF16)
        _acc_add(db_ref, jnp.sum(dfl, axis=0, keepdims=True), i)

    rev = lambda i: (nb - 1 - i, 0)
    return pl.pallas_call(
        body, grid=(nb,),
        in_specs=[pl.BlockSpec((128, TQ), lambda i: (0, nb - 1 - i)), pl.BlockSpec((TQ, 128), rev), _vec_spec(128)],
        out_specs=(pl.BlockSpec((TQ, 128), rev), _acc_spec(128)),
        out_shape=(jax.ShapeDtypeStruct((S, 128), BF16), jax.ShapeDtypeStruct((8, 128), F32)),
        scratch_shapes=[pltpu.VMEM((128, 128), F32)],
        name=name, compiler_params=_cparams(),
    )(dc_rows, fl_raw, b_pad)


def _chk_bias_call(g_rev, *, name):
    def body(g_ref, o_ref):
        x = jnp.broadcast_to(g_ref[...], (TQ, ROLL_W))
        rolled = pltpu.roll(x, ROLL_W - (TQ - 1), 1, stride=1, stride_axis=0)
        qc = lax.broadcasted_iota(jnp.int32, (TQ, WIN), 0) // CHUNK
        kc = lax.broadcasted_iota(jnp.int32, (TQ, WIN), 1) // CHUNK
        band = (kc >= qc) & (kc <= qc + LEFT)
        o_ref[...] = jnp.where(band, rolled[:, 0:WIN], NEG)

    return pl.pallas_call(
        body, grid=(8,),
        in_specs=[pl.BlockSpec((None, 1, ROLL_W), lambda h: (h, 0, 0))],
        out_specs=pl.BlockSpec((None, TQ, WIN), lambda h: (h, 0, 0)),
        out_shape=jax.ShapeDtypeStruct((8, TQ, WIN), F32), name=name, compiler_params=_cparams(),
    )(g_rev.reshape(8, 1, ROLL_W))


def _chk_scores(i, qm, kwin, bias, scale):
    s = _dot_nt(qm, kwin) * scale + bias
    kc = lax.broadcasted_iota(jnp.int32, (TQ, WIN), 1) // CHUNK
    return jnp.where(kc + i * (TQ // CHUNK) >= LEFT, s, NEG)


def _chk_fwd_call(proj, bias, *, name, task=None):
    S = proj.shape[0]
    nq = S // TQ
    scale = HEAD ** -0.5

    def body(q_ref, k_ref, v_ref, b_ref, o_ref, kp_ref, vp_ref):
        i = pl.program_id(1)

        @pl.when(i == 0)
        def _():
            kp_ref[0:PADK, :] = jnp.zeros((PADK, 128), BF16)
            vp_ref[0:PADK, :] = jnp.zeros((PADK, 128), BF16)
            kp_ref[PADK:PADK + S, :] = k_ref[...]
            vp_ref[PADK:PADK + S, :] = v_ref[...]
        masks = _lane_masks()
        q = q_ref[...]
        start = pl.multiple_of(i * TQ, TQ)
        kwin = kp_ref[pl.ds(start, WIN), :]
        vwin = vp_ref[pl.ds(start, WIN), :]
        out = jnp.zeros((TQ, 128), F32)
        for h in range(2):
            qm = jnp.where(masks[h], q, jnp.zeros_like(q))
            s = _chk_scores(i, qm, kwin, b_ref[h], scale)
            p = jnp.exp(s - jnp.max(s, axis=1, keepdims=True))
            p = p / jnp.sum(p, axis=1, keepdims=True)
            out = out + _dot(p.astype(BF16), jnp.where(masks[h], vwin, jnp.zeros_like(vwin)))
        o_ref[...] = out

    c0 = 3 * N_PAIR
    return _pallas(
        body, grid=(N_PAIR, nq), task=task,
        in_specs=[pl.BlockSpec((TQ, 128), lambda p, i: (i, c0 + p)),
                  pl.BlockSpec((S, 128), lambda p, i: (0, c0 + N_PAIR + p)),
                  pl.BlockSpec((S, 128), lambda p, i: (0, c0 + 2 * N_PAIR + p)),
                  pl.BlockSpec((2, TQ, WIN), lambda p, i: (p, 0, 0))],
        out_specs=pl.BlockSpec((TQ, 128), lambda p, i: (i, p)),
        out_shape=jax.ShapeDtypeStruct((S, D_GRP), F32),
        scratch_shapes=[pltpu.VMEM((S + PADK, 128), BF16), pltpu.VMEM((S + PADK, 128), BF16)],
        name=name,
    )(proj, proj, proj, bias)


def _chk_bwd_call(proj, do, bias, *, name, task=None):
    S = proj.shape[0]
    nq = S // TQ
    scale = HEAD ** -0.5

    def body(q_ref, k_ref, v_ref, do_ref, b_ref, dq_ref, dk_ref, dv_ref, dg_ref, kp_ref, vp_ref, dkp_ref, dvp_ref, db_ref):
        i = pl.program_id(1)

        @pl.when(i == 0)
        def _():
            kp_ref[0:PADK, :] = jnp.zeros((PADK, 128), BF16)
            vp_ref[0:PADK, :] = jnp.zeros((PADK, 128), BF16)
            kp_ref[PADK:PADK + S, :] = k_ref[...]
            vp_ref[PADK:PADK + S, :] = v_ref[...]
            dkp_ref[...] = jnp.zeros_like(dkp_ref)
            dvp_ref[...] = jnp.zeros_like(dvp_ref)
            db_ref[...] = jnp.zeros_like(db_ref)
        masks = _lane_masks()
        q = q_ref[...]
        dout = do_ref[...]
        start = pl.multiple_of(i * TQ, TQ)
        kwin = kp_ref[pl.ds(start, WIN), :]
        vwin = vp_ref[pl.ds(start, WIN), :]
        dq = jnp.zeros((TQ, 128), F32)
        dkw = jnp.zeros((WIN, 128), F32)
        dvw = jnp.zeros((WIN, 128), F32)
        for h in range(2):
            qm = jnp.where(masks[h], q, jnp.zeros_like(q))
            dom = jnp.where(masks[h], dout, jnp.zeros_like(dout))
            s = _chk_scores(i, qm, kwin, b_ref[h], scale)
            p = jnp.exp(s - jnp.max(s, axis=1, keepdims=True))
            p = p / jnp.sum(p, axis=1, keepdims=True)
            dp = _dot_nt(dom, vwin)
            ds = p * (dp - jnp.sum(p * dp, axis=1, keepdims=True))
            db_ref[h] += ds
            dsb = (ds * scale).astype(BF16)
            dq = dq + _dot(dsb, jnp.where(masks[h], kwin, jnp.zeros_like(kwin)))
            dkw = dkw + _dot_tn(dsb, qm)
            dvw = dvw + _dot_tn(p.astype(BF16), dom)
        dq_ref[...] = dq.astype(BF16)
        dkp_ref[pl.ds(start, WIN), :] += dkw
        dvp_ref[pl.ds(start, WIN), :] += dvw

        @pl.when(i == nq - 1)
        def _():
            dk_ref[...] = dkp_ref[PADK:PADK + S, :].astype(BF16)
            dv_ref[...] = dvp_ref[PADK:PADK + S, :].astype(BF16)
            a = lax.broadcasted_iota(jnp.int32, (TQ, TQ), 0)
            b = lax.broadcasted_iota(jnp.int32, (TQ, TQ), 1)
            flip = jnp.where(a + b == TQ - 1, 1.0, 0.0).astype(BF16)
            e = lax.broadcasted_iota(jnp.int32, (1, ROLL_W), 1)
            dg_ref[...] = jnp.zeros_like(dg_ref)
            for h in range(2):
                rev = _dot3_l(flip, db_ref[h])
                wide = jnp.concatenate([rev, jnp.zeros((TQ, ROLL_W - WIN), F32)], axis=1)
                diag = pltpu.roll(wide, 0, 1, stride=1, stride_axis=0)
                dg = jnp.sum(diag, axis=0, keepdims=True)
                lo = jnp.sum(jnp.where(e <= 639, dg, 0.0), axis=1, keepdims=True)
                hi = jnp.sum(jnp.where(e >= 895, dg, 0.0), axis=1, keepdims=True)
                dg_ref[h:h + 1, :] = jnp.where(e == 639, lo, jnp.where(e == 895, hi, dg))

    c0 = 3 * N_PAIR
    res = lambda p, i: (0, p)
    return _pallas(
        body, grid=(N_PAIR, nq), task=task,
        in_specs=[pl.BlockSpec((TQ, 128), lambda p, i: (i, c0 + p)),
                  pl.BlockSpec((S, 128), lambda p, i: (0, c0 + N_PAIR + p)),
                  pl.BlockSpec((S, 128), lambda p, i: (0, c0 + 2 * N_PAIR + p)),
                  pl.BlockSpec((TQ, 128), lambda p, i: (i, p)),
                  pl.BlockSpec((2, TQ, WIN), lambda p, i: (p, 0, 0))],
        out_specs=(pl.BlockSpec((TQ, 128), lambda p, i: (i, p)), pl.BlockSpec((S, 128), res),
                   pl.BlockSpec((S, 128), res), pl.BlockSpec((None, 8, ROLL_W), lambda p, i: (p, 0, 0))),
        out_shape=(jax.ShapeDtypeStruct((S, D_GRP), BF16), jax.ShapeDtypeStruct((S, D_GRP), BF16),
                   jax.ShapeDtypeStruct((S, D_GRP), BF16), jax.ShapeDtypeStruct((N_PAIR, 8, ROLL_W), F32)),
        scratch_shapes=[pltpu.VMEM((S + PADK, 128), BF16), pltpu.VMEM((S + PADK, 128), BF16),
                        pltpu.VMEM((S + PADK, 128), F32), pltpu.VMEM((S + PADK, 128), F32),
                        pltpu.VMEM((2, TQ, WIN), F32)],
        name=name,
    )(proj, proj, proj, do, bias)


def _mem_fwd_call(q, k, v, *, name, tq=512):
    S = q.shape[0]
    scale = MEM_HD ** -0.5

    def body(q_ref, k_ref, v_ref, o_ref):
        s = _dot_nt(q_ref[...], k_ref[...]) * scale
        p = jnp.exp(s - jnp.max(s, axis=1, keepdims=True))
        p = p / jnp.sum(p, axis=1, keepdims=True)
        o_ref[...] = _dot(p.astype(BF16), v_ref[...]).astype(BF16)

    return pl.pallas_call(
        body, grid=(MEM_HEADS, S // tq),
        in_specs=[pl.BlockSpec((tq, MEM_HD), lambda h, i: (i, h)),
                  pl.BlockSpec((N_MEM, MEM_HD), lambda h, i: (0, h)),
                  pl.BlockSpec((N_MEM, MEM_HD), lambda h, i: (0, h))],
        out_specs=pl.BlockSpec((tq, MEM_HD), lambda h, i: (i, h)),
        out_shape=jax.ShapeDtypeStruct((S, D), BF16), name=name, compiler_params=_cparams(),
    )(q, k, v)


def _mem_bwd_call(q, k, v, do, *, name, tq=512):
    S = q.shape[0]
    n = S // tq
    scale = MEM_HD ** -0.5

    def body(q_ref, k_ref, v_ref, do_ref, dq_ref, dk_ref, dv_ref, dka_ref, dva_ref):
        i = pl.program_id(1)

        @pl.when(i == 0)
        def _():
            dka_ref[...] = jnp.zeros_like(dka_ref)
            dva_ref[...] = jnp.zeros_like(dva_ref)
        qb = q_ref[...]
        kb = k_ref[...]
        dob = do_ref[...]
        s = _dot_nt(qb, kb) * scale
        p = jnp.exp(s - jnp.max(s, axis=1, keepdims=True))
        p = p / jnp.sum(p, axis=1, keepdims=True)
        dp = _dot_nt(dob, v_ref[...])
        ds = p * (dp - jnp.sum(p * dp, axis=1, keepdims=True))
        dsb = (ds * scale).astype(BF16)
        dq_ref[...] = _dot(dsb, kb).astype(BF16)
        dka_ref[...] += _dot_tn(dsb, qb)
        dva_ref[...] += _dot_tn(p.astype(BF16), dob)

        @pl.when(i == n - 1)
        def _():
            dk_ref[...] = dka_ref[...].astype(BF16)
            dv_ref[...] = dva_ref[...].astype(BF16)

    kv = pl.BlockSpec((N_MEM, MEM_HD), lambda h, i: (0, h))
    qs = pl.BlockSpec((tq, MEM_HD), lambda h, i: (i, h))
    return pl.pallas_call(
        body, grid=(MEM_HEADS, n), in_specs=[qs, kv, kv, qs], out_specs=(qs, kv, kv),
        out_shape=(jax.ShapeDtypeStruct((S, D), BF16), jax.ShapeDtypeStruct((N_MEM, D), BF16),
                   jax.ShapeDtypeStruct((N_MEM, D), BF16)),
        scratch_shapes=[pltpu.VMEM((N_MEM, MEM_HD), F32), pltpu.VMEM((N_MEM, MEM_HD), F32)],
        name=name, compiler_params=_cparams(),
    )(q, k, v, do)


def _rel_table_to_g(rel):
    return jnp.concatenate([
        jnp.broadcast_to(rel[:, N_REL - 1:N_REL], (8, 640)),
        rel[:, 1:N_REL - 1][:, ::-1],
        jnp.broadcast_to(rel[:, 0:1], (8, 129)),
    ], axis=1)


def _g_to_rel_table(dg):
    return dg[:, 639:896][:, ::-1]


def _place():
    x, y, c = lax.axis_index("x"), lax.axis_index("y"), lax.axis_index("c")
    others = [(1 - x, y), (x, 1 - y), (1 - x, 1 - y)]
    return x, y, c, others


def _half(c, rows):
    hr = rows // 2
    return pl.ds(pl.multiple_of(c * hr, 16), hr)


def _dma_sems(*shape):
    return pltpu.SemaphoreType.DMA(shape)


def _cast_slab_call(w, chip_arr, *, name, tm=256, pad_rows=0):
    rows, cols = w.shape
    if pad_rows:
        tm = rows
    tm = min(tm, rows)

    def body(chip_ref, w_ref, o_ref):
        o_ref[0:tm, :] = w_ref[...].astype(BF16)
        if pad_rows:
            o_ref[tm:tm + pad_rows, :] = jnp.zeros((pad_rows, cols), BF16)

    return pl.pallas_call(
        body,
        grid_spec=pltpu.PrefetchScalarGridSpec(
            num_scalar_prefetch=1, grid=(rows // tm,),
            in_specs=[pl.BlockSpec((tm, cols), lambda i, chip: (i, 0))],
            out_specs=pl.BlockSpec((None, tm + pad_rows, cols), lambda i, chip: (chip[0], i, 0))),
        out_shape=jax.ShapeDtypeStruct((N_CHIP, rows + pad_rows, cols), BF16), name=name,
        compiler_params=_cparams(),
    )(chip_arr, w)


def _ag_ici_task(gathered):
    n = len(gathered)

    def copies(ins, outs, sems):
        send_sems, recv_sems = sems
        x, y, c, others = _place()
        me = 2 * x + y
        for k in range(n):
            mine = _half(c, gathered[k].shape[1])
            for t, (ox, oy) in enumerate(others):
                yield [pltpu.make_async_remote_copy(
                    src_ref=ins[k].at[me, mine], dst_ref=outs[k].at[slab, mine],
                    send_sem=send_sems.at[k, t], recv_sem=recv_sems.at[k, t],
                    device_id=(ox, oy, c), device_id_type=MESH) for slab in (me, 2 * ox + oy)]

    def issue(ins, outs, sems):
        for outgoing, _ in copies(ins, outs, sems):
            outgoing.start()

    def drain(ins, outs, sems):
        for outgoing, incoming in copies(ins, outs, sems):
            incoming.wait_recv()
            outgoing.wait_send()

    return _Task(gathered, [jax.ShapeDtypeStruct(g.shape, g.dtype) for g in gathered],
                 [_dma_sems(n, 3), _dma_sems(n, 3)], issue, drain, aliases={k: k for k in range(n)})


def _ag_d2d_task(gathered):
    n = len(gathered)

    def copies(ins, outs, sems):
        send_sems, recv_sems = sems
        x, y, c, others = _place()
        for k in range(n):
            rows = gathered[k].shape[1]
            mine, theirs = _half(c, rows), _half(1 - c, rows)
            for t, (ox, oy) in enumerate(others):
                slab = 2 * ox + oy
                pair = [pltpu.make_async_remote_copy(
                    src_ref=ins[k].at[slab, half], dst_ref=outs[k].at[slab, half],
                    send_sem=send_sems.at[k, t], recv_sem=recv_sems.at[k, t],
                    device_id=(x, y, 1 - c), device_id_type=MESH) for half in (mine, theirs)]
                yield pair

    def issue(ins, outs, sems):
        for outgoing, _ in copies(ins, outs, sems):
            outgoing.start()

    def drain(ins, outs, sems):
        for outgoing, incoming in copies(ins, outs, sems):
            incoming.wait_recv()
            outgoing.wait_send()

    return _Task(gathered, [jax.ShapeDtypeStruct(g.shape, g.dtype) for g in gathered],
                 [_dma_sems(n, 3), _dma_sems(n, 3)], issue, drain, aliases={k: k for k in range(n)})


def _rs_pair_task(ds):
    n = len(ds)

    def copies(ins, outs, sems):
        send_sems, recv_sems = sems
        x, y, c, _ = _place()
        for k in range(n):
            yield pltpu.make_async_remote_copy(
                src_ref=ins[k].at[:, _half(1 - c, ds[k].shape[1])], dst_ref=outs[k],
                send_sem=send_sems.at[k], recv_sem=recv_sems.at[k],
                device_id=(x, y, 1 - c), device_id_type=MESH)

    def issue(ins, outs, sems):
        for cp in copies(ins, outs, sems):
            cp.start()

    def drain(ins, outs, sems):
        for cp in copies(ins, outs, sems):
            cp.wait()

    return _Task(ds, [jax.ShapeDtypeStruct((N_CHIP, d.shape[1] // 2, d.shape[2]), d.dtype) for d in ds],
                 [_dma_sems(n), _dma_sems(n)], issue, drain)


def _pair_add_call(d, r1, c_arr, *, name, tm=256):
    _, rows, cols = d.shape
    hr = rows // 2
    tm = tm if hr % tm == 0 else hr
    nb = hr // tm

    def body(c_ref, d_ref, r_ref, o_ref):
        o_ref[...] = (d_ref[...].astype(F32) + r_ref[...].astype(F32)).astype(BF16)

    return pl.pallas_call(
        body,
        grid_spec=pltpu.PrefetchScalarGridSpec(
            num_scalar_prefetch=1, grid=(N_CHIP, nb),
            in_specs=[pl.BlockSpec((None, tm, cols), lambda j, i, c: (j, c[0] * nb + i, 0)),
                      pl.BlockSpec((None, tm, cols), lambda j, i, c: (j, i, 0))],
            out_specs=pl.BlockSpec((None, tm, cols), lambda j, i, c: (j, i, 0))),
        out_shape=jax.ShapeDtypeStruct((N_CHIP, hr, cols), BF16), name=name, compiler_params=_cparams(),
    )(c_arr, d, r1)


def _rs_chip_task(ps):
    n = len(ps)

    def copies(ins, outs, sems):
        send_sems, recv_sems = sems
        x, y, c, others = _place()
        for k in range(n):
            for t, (ox, oy) in enumerate(others):
                yield pltpu.make_async_remote_copy(
                    src_ref=ins[k].at[2 * ox + oy], dst_ref=outs[k].at[t],
                    send_sem=send_sems.at[k, t], recv_sem=recv_sems.at[k, t],
                    device_id=(ox, oy, c), device_id_type=MESH)

    def issue(ins, outs, sems):
        for cp in copies(ins, outs, sems):
            cp.start()

    def drain(ins, outs, sems):
        for cp in copies(ins, outs, sems):
            cp.wait()

    return _Task(ps, [jax.ShapeDtypeStruct((3,) + p.shape[1:], p.dtype) for p in ps],
                 [_dma_sems(n, 3), _dma_sems(n, 3)], issue, drain)


def _chip_sum_call(p, r2, place_arr, *, name, tm=256):
    _, hr, cols = r2.shape
    tm = tm if hr % tm == 0 else hr
    nb = hr // tm

    def body(place_ref, p_ref, r_ref, o_ref):
        acc = p_ref[...].astype(F32)
        for j in range(3):
            acc = acc + r_ref[j].astype(F32)
        o_ref[...] = acc

    return pl.pallas_call(
        body,
        grid_spec=pltpu.PrefetchScalarGridSpec(
            num_scalar_prefetch=1, grid=(nb,),
            in_specs=[pl.BlockSpec((None, tm, cols), lambda i, pc: (pc[0], i, 0)),
                      pl.BlockSpec((3, tm, cols), lambda i, pc: (0, i, 0))],
            out_specs=pl.BlockSpec((tm, cols), lambda i, pc: (pc[1] * nb + i, 0))),
        out_shape=jax.ShapeDtypeStruct((2 * hr, cols), F32), name=name, compiler_params=_cparams(),
    )(place_arr, p, r2)


def _rs_gather_task(gs):
    n = len(gs)

    def copies(ins, outs, sems):
        send_sems, recv_sems = sems
        x, y, c, _ = _place()
        for k in range(n):
            rows = gs[k].shape[0]
            mine, theirs = _half(c, rows), _half(1 - c, rows)
            yield [pltpu.make_async_remote_copy(
                src_ref=ins[k].at[mine], dst_ref=outs[k].at[half],
                send_sem=send_sems.at[k], recv_sem=recv_sems.at[k],
                device_id=(x, y, 1 - c), device_id_type=MESH) for half in (mine, theirs)]

    def issue(ins, outs, sems):
        for outgoing, _ in copies(ins, outs, sems):
            outgoing.start()

    def drain(ins, outs, sems):
        for outgoing, incoming in copies(ins, outs, sems):
            incoming.wait_recv()
            outgoing.wait_send()

    return _Task(gs, [jax.ShapeDtypeStruct(g.shape, g.dtype) for g in gs],
                 [_dma_sems(n), _dma_sems(n)], issue, drain, aliases={k: k for k in range(n)})


def _adamw(w, g, m, v):
    m = ADAM_B1 * m + (1.0 - ADAM_B1) * g
    v = ADAM_B2 * v + (1.0 - ADAM_B2) * jnp.square(g)
    m_hat = m / (1.0 - ADAM_B1 ** ADAM_STEP)
    v_hat = v / (1.0 - ADAM_B2 ** ADAM_STEP)
    delta = -ADAM_LR * (m_hat / (jnp.sqrt(v_hat) + ADAM_EPS) + ADAM_WD * w)
    return delta, m, v


def _adamw_call(items, *, name, tm=256, task=None):
    n = len(items)
    cols = items[0][0].shape[1]
    tiles = [it[0].shape[0] // tm for it in items]
    steps = max(tiles)

    def body(*refs):
        i = pl.program_id(0)
        ins, outs = refs[:4 * n], refs[4 * n:]
        for k in range(n):
            def update(k=k):
                g = ins[4 * k + 1][...]
                res = _adamw(ins[4 * k][...], g, ins[4 * k + 2][...], ins[4 * k + 3][...])
                outs[4 * k][...] = g
                for j in range(3):
                    outs[4 * k + 1 + j][...] = res[j]
            if tiles[k] == steps:
                update()
            else:
                pl.when(i < tiles[k])(update)

    in_specs, out_specs, out_shape, args = [], [], [], []
    for it, t in zip(items, tiles):
        spec = pl.BlockSpec((tm, cols), lambda i, t=t: (jnp.minimum(i, t - 1), 0))
        in_specs += [spec] * 4
        out_specs += [spec] * 4
        out_shape += [jax.ShapeDtypeStruct(it[0].shape, F32)] * 4
        args += list(it)
    res = _pallas(body, grid=(steps,), in_specs=in_specs, out_specs=out_specs, out_shape=out_shape,
                  name=name, task=task)(*args)
    outs, extra = res if task is not None else (res, None)
    grouped = [tuple(outs[4 * k:4 * k + 4]) for k in range(n)]
    return (grouped, extra) if task is not None else grouped


def _adamw_cols_call(w, g_pad, m, v, *, name, tn=256):
    rows, cols = w.shape

    def body(w_ref, g_ref, m_ref, v_ref, go_ref, d_ref, mo_ref, vo_ref):
        g = g_ref[0:rows, :]
        d, mn, vn = _adamw(w_ref[...], g, m_ref[...], v_ref[...])
        go_ref[...] = g
        d_ref[...] = d
        mo_ref[...] = mn
        vo_ref[...] = vn

    spec = pl.BlockSpec((rows, tn), lambda j: (0, j))
    gspec = pl.BlockSpec((g_pad.shape[0], tn), lambda j: (0, j))
    return _pallas(body, grid=(cols // tn,), in_specs=[spec, gspec, spec, spec], out_specs=(spec,) * 4,
                   out_shape=(jax.ShapeDtypeStruct((rows, cols), F32),) * 4, name=name)(w, g_pad, m, v)


N_DEV = 8
SMALL_ROWS = 24
SMALL_LAYOUT = {
    "g_mix_pre": (0, 0, 1, D), "g_mix_post": (1, 0, 1, D), "g_mem_kv": (2, 0, 1, D), "g_mem_pre": (3, 0, 1, D),
    "g_mem_post": (4, 0, 1, D), "g_ff_pre": (5, 0, 1, D), "g_ff_post": (6, 0, 1, D),
    "g_fox_out": (7, 0, 1, D_GRP), "g_chk_out": (7, D_GRP, 1, D_GRP), "b_fgt": (8, 0, 1, 8),
    "rel_bias": (16, 0, 8, N_REL),
}
SMALL = list(SMALL_LAYOUT)


LOSS_ROW = 9


def _small_call(grads, ws, ms, vs, loss_blk, task, *, name):
    n = len(SMALL)
    t_in, t_out = len(task.arrays), len(task.out_shapes)

    def body(*refs):
        g_refs, w_refs, m_refs, v_refs = (refs[j * n:(j + 1) * n] for j in range(4))
        p = 4 * n
        loss_ref, tins = refs[p], refs[p + 1:p + 1 + t_in]
        p += 1 + t_in
        outs, loss_out, touts = refs[p:p + 4 * n], refs[p + 4 * n], refs[p + 4 * n + 1:p + 4 * n + 1 + t_out]
        p += 4 * n + 1 + t_out
        mine, slots, send_sems, recv_sems = refs[p:p + 4]
        tsems = refs[p + 4:]
        task.issue(tins, touts, tsems)
        x, y, c, _ = _place()
        me = 4 * x + 2 * y + c
        mine[...] = jnp.zeros_like(mine)
        for k, name_k in enumerate(SMALL):
            r, l, nr, nl = SMALL_LAYOUT[name_k]
            mine[r:r + nr, l:l + nl] = g_refs[k][0:nr, 0:nl]
        mine[LOSS_ROW:LOSS_ROW + 1, 0:128] = loss_ref[0:1, :]
        slots[me] = mine[...]
        peers = [(dx, dy, dc) for dx in (0, 1) for dy in (0, 1) for dc in (0, 1)][1:]
        cps = []
        for t, (dx, dy, dc) in enumerate(peers):
            px, py, pc = (x + dx) % 2, (y + dy) % 2, (c + dc) % 2
            cps.append(pltpu.make_async_remote_copy(
                src_ref=mine, dst_ref=slots.at[me], send_sem=send_sems.at[t], recv_sem=recv_sems.at[t],
                device_id=(px, py, pc), device_id_type=MESH))
            cps[-1].start()
        for t, (dx, dy, dc) in enumerate(peers):
            px, py, pc = (x + dx) % 2, (y + dy) % 2, (c + dc) % 2
            pltpu.make_async_remote_copy(
                src_ref=mine, dst_ref=slots.at[4 * px + 2 * py + pc], send_sem=send_sems.at[t],
                recv_sem=recv_sems.at[t], device_id=(px, py, pc), device_id_type=MESH).wait_recv()
        for cp in cps:
            cp.wait_send()
        total = slots[0]
        for j in range(1, N_DEV):
            total = total + slots[j]
        for k, name_k in enumerate(SMALL):
            r, l, nr, nl = SMALL_LAYOUT[name_k]
            g = total[r:r + nr, l:l + nl]
            d, mn, vn = _adamw(w_refs[k][...], g, m_refs[k][...], v_refs[k][...])
            for j, val in enumerate((g, d, mn, vn)):
                outs[4 * k + j][...] = val
        loss_out[...] = jnp.broadcast_to(total[LOSS_ROW:LOSS_ROW + 1, 0:128], loss_out.shape)
        task.drain(tins, touts, tsems)

    vm = pl.BlockSpec(memory_space=pltpu.VMEM)
    out_shape = [jax.ShapeDtypeStruct(ws[k].shape, F32) for k in SMALL for _ in range(4)]
    out_shape += [jax.ShapeDtypeStruct((8, 128), F32)] + list(task.out_shapes)
    res = pl.pallas_call(
        body, in_specs=[vm] * (4 * n + 1) + [ANY] * t_in, out_specs=[vm] * (4 * n + 1) + [ANY] * t_out,
        out_shape=out_shape,
        scratch_shapes=[pltpu.VMEM((SMALL_ROWS, D), F32), pltpu.VMEM((N_DEV, SMALL_ROWS, D), F32),
                        _dma_sems(N_DEV - 1), _dma_sems(N_DEV - 1)] + list(task.sems),
        input_output_aliases={4 * n + 1 + i: 4 * n + 1 + j for i, j in task.aliases.items()},
        name=name,
    )(*[d[k] for d in (grads, ws, ms, vs) for k in SMALL], loss_blk, *task.arrays)
    return ({k: tuple(res[4 * i:4 * i + 4]) for i, k in enumerate(SMALL)}, res[4 * n], list(res[4 * n + 1:]))


WEIGHTS = ["w_in", "b_fgt", "rel_bias", "g_fox_out", "g_chk_out", "w_out", "g_mix_pre", "g_mix_post", "g_mem_kv",
           "w_mq", "w_mk", "w_mv", "w_mo", "g_mem_pre", "g_mem_post", "w_ff1", "w_ff2", "g_ff_pre", "g_ff_post"]
BIG = ["w_in", "w_out", "w_mq", "w_mk", "w_mv", "w_mo", "w_ff1", "w_ff2"]


IN_SHARD = D_IN // N_CHIP
IN_PAD = 800


IN_PIECES = [(0, 0, 770), (800, 770, 766), (1566, 3072, 4), (1600, 3076, 4), (1604, 1536, 766), (2400, 2302, 770)]
PAD_ZEROS = [(800 * j + IN_SHARD, IN_PAD - IN_SHARD) for j in range(N_CHIP)]
ALL_ZEROS = [(D_IN, D_ALL - D_IN)]


def _reorder_rows_call(src, to_all, *, name, tn=256):
    rows, cols = src.shape
    zeros = ALL_ZEROS if to_all else PAD_ZEROS

    def body(s_ref, o_ref):
        for pad0, all0, cnt in IN_PIECES:
            s0, d0 = (pad0, all0) if to_all else (all0, pad0)
            o_ref[d0:d0 + cnt, :] = s_ref[s0:s0 + cnt, :]
        for z0, cnt in zeros:
            o_ref[z0:z0 + cnt, :] = jnp.zeros((cnt, tn), src.dtype)

    spec = pl.BlockSpec((rows, tn), lambda j: (0, j))
    return _pallas(body, grid=(cols // tn,), in_specs=[spec], out_specs=spec,
                   out_shape=jax.ShapeDtypeStruct((rows, cols), src.dtype), name=name)(src)


def kernel(x, mem, w_in, b_fgt, rel_bias, g_fox_out, g_chk_out, w_out, g_mix_pre, g_mix_post, g_mem_kv, w_mq, w_mk, w_mv, w_mo, g_mem_pre, g_mem_post, w_ff1, w_ff2, g_ff_pre, g_ff_post, loss_target, m_w_in, m_b_fgt, m_rel_bias, m_g_fox_out, m_g_chk_out, m_w_out, m_g_mix_pre, m_g_mix_post, m_g_mem_kv, m_w_mq, m_w_mk, m_w_mv, m_w_mo, m_g_mem_pre, m_g_mem_post, m_w_ff1, m_w_ff2, m_g_ff_pre, m_g_ff_post, v_w_in, v_b_fgt, v_rel_bias, v_g_fox_out, v_g_chk_out, v_w_out, v_g_mix_pre, v_g_mix_post, v_g_mem_kv, v_w_mq, v_w_mk, v_w_mv, v_w_mo, v_g_mem_pre, v_g_mem_post, v_w_ff1, v_w_ff2, v_g_ff_pre, v_g_ff_post):
    w = dict(w_in=w_in, b_fgt=b_fgt, rel_bias=rel_bias, g_fox_out=g_fox_out, g_chk_out=g_chk_out, w_out=w_out,
             g_mix_pre=g_mix_pre, g_mix_post=g_mix_post, g_mem_kv=g_mem_kv, w_mq=w_mq, w_mk=w_mk, w_mv=w_mv,
             w_mo=w_mo, g_mem_pre=g_mem_pre, g_mem_post=g_mem_post, w_ff1=w_ff1, w_ff2=w_ff2, g_ff_pre=g_ff_pre,
             g_ff_post=g_ff_post)
    m = dict(w_in=m_w_in, b_fgt=m_b_fgt, rel_bias=m_rel_bias, g_fox_out=m_g_fox_out, g_chk_out=m_g_chk_out,
             w_out=m_w_out, g_mix_pre=m_g_mix_pre, g_mix_post=m_g_mix_post, g_mem_kv=m_g_mem_kv, w_mq=m_w_mq,
             w_mk=m_w_mk, w_mv=m_w_mv, w_mo=m_w_mo, g_mem_pre=m_g_mem_pre, g_mem_post=m_g_mem_post,
             w_ff1=m_w_ff1, w_ff2=m_w_ff2, g_ff_pre=m_g_ff_pre, g_ff_post=m_g_ff_post)
    v = dict(w_in=v_w_in, b_fgt=v_b_fgt, rel_bias=v_rel_bias, g_fox_out=v_g_fox_out, g_chk_out=v_g_chk_out,
             w_out=v_w_out, g_mix_pre=v_g_mix_pre, g_mix_post=v_g_mix_post, g_mem_kv=v_g_mem_kv, w_mq=v_w_mq,
             w_mk=v_w_mk, w_mv=v_w_mv, w_mo=v_w_mo, g_mem_pre=v_g_mem_pre, g_mem_post=v_g_mem_post,
             w_ff1=v_w_ff1, w_ff2=v_w_ff2, g_ff_pre=v_g_ff_pre, g_ff_post=v_g_ff_post)

    def rows(d, k):
        return d[k][0] if k == "rel_bias" else d[k]

    xs, mems, target = x[0], mem[0], loss_target[0]
    S = xs.shape[0]
    sp = {k: rows(w, k) for k in SMALL}
    b_pad = jnp.pad(sp["b_fgt"], ((0, 0), (0, 120)))
    chip = 2 * lax.axis_index("x") + lax.axis_index("y")
    chip_arr = jnp.reshape(chip, (1,)).astype(jnp.int32)
    c_arr = jnp.reshape(lax.axis_index("c"), (1,)).astype(jnp.int32)
    place_arr = jnp.concatenate([chip_arr, c_arr])
    w_in_t, m_in_t, v_in_t = w["w_in"][0].T, m["w_in"][0].T, v["w_in"][0].T
    slab = {k: _cast_slab_call(w[k][0], chip_arr, name="cast_" + k) for k in BIG[1:]}
    slab["w_in"] = _cast_slab_call(w_in_t, chip_arr, name="cast_w_in", pad_rows=IN_PAD - IN_SHARD)

    def gather_ici(names):
        return _ag_ici_task([slab[k] for k in names])

    def pair_add(k, d, r1):
        return _pair_add_call(d, r1, c_arr, name="rs_pair_add_" + k)

    g_in, = _comm_call(gather_ici(["w_in"]), name="ag_w_in")
    h1, (g_in,) = _rms_fwd_call(xs, sp["g_mix_pre"], name="rms_mix_pre", task=_ag_d2d_task([g_in]))
    w_all_t = _reorder_rows_call(g_in.reshape(N_CHIP * IN_PAD, D), True, name="w_in_rows")
    proj, (g_out,) = _mm_nt(h1, w_all_t, "plain", rows=(0, 3072), name="mm_proj", task=gather_ici(["w_out"]))
    fl_raw = _mm_nt(h1, w_all_t, "plain", rows=(3072, 128), name="mm_gate", out_dtype=F32, tn=128)
    c_rep, c_t = _fox_prep_call(fl_raw, b_pad, name="fox_prep")
    bias = _chk_bias_call(_rel_table_to_g(sp["rel_bias"]), name="chk_bias")
    mid = ["w_mq", "w_mk", "w_mv", "w_mo", "w_ff1"]
    (yf, lse), got = _fox_fwd_call(proj, c_rep, c_t, name="fox_fwd",
                                   task=_merge_tasks([gather_ici(mid), _ag_d2d_task([g_out])]))
    g_mid, g_out = got[:5], got[5]
    yc, got = _chk_fwd_call(proj, bias, name="chk_fwd",
                            task=_merge_tasks([gather_ici(["w_ff2"]), _ag_d2d_task(g_mid)]))
    g_ff2, (g_mq, g_mk, g_mv, g_mo, g_ff1) = got[0], got[1:]
    yn = _mix_norm_fwd_call(yf, yc, sp["g_fox_out"], sp["g_chk_out"], name="mix_norm_fwd")
    z, (g_ff2,) = _mm_nn(yn, g_out, "rows", name="mm_out", out_dtype=F32, task=_ag_d2d_task([g_ff2]))
    x1, h2 = _post_pre_call(xs, z, sp["g_mix_post"], sp["g_mem_pre"], name="post_mix")
    memn = _rms_fwd_call(mems, sp["g_mem_kv"], name="rms_mem_kv")
    q2 = _mm_nn(h2, g_mq, "rows", name="mm_mq")
    k2 = _mm_nn(memn, g_mk, "rows", name="mm_mk")
    v2 = _mm_nn(memn, g_mv, "rows", name="mm_mv")
    o2 = _mem_fwd_call(q2, k2, v2, name="mem_fwd")
    y2 = _mm_nn(o2, g_mo, "rows", name="mm_mo", out_dtype=F32)
    x2, h3 = _post_pre_call(x1, y2, sp["g_mem_post"], sp["g_ff_pre"], name="post_mem")
    act, relu = _mm_nn(h3, g_ff1, "cols", name="mm_ff1", epi="relu2")
    y3 = _mm_nn(act, g_ff2, "rows", name="mm_ff2", out_dtype=F32)
    loss_blk, dx3, dy3, dg_ff_post = _final_call(x2, y3, sp["g_ff_post"], target, name="final")

    d_ff2 = _mm_tn(act, dy3, name="mm_dff2").reshape(N_CHIP, D_FF // N_CHIP, D)
    du, (r1,) = _mm_nt(dy3, g_ff2, "rows", name="mm_du", mul2r=relu, task=_rs_pair_task([d_ff2]))
    p_ff2 = pair_add("w_ff2", d_ff2, r1)
    d_ff1 = _mm_tn(h3, du, name="mm_dff1", cols4=True)
    dh3, (r1,) = _mm_nt(du, g_ff1, "cols", name="mm_dh3", out_dtype=F32, task=_rs_pair_task([d_ff1]))
    p_ff1 = pair_add("w_ff1", d_ff1, r1)
    dx2, dy2, dg_ff_pre, dg_mem_post = _bwd_mid_call(dx3, x2, dh3, sp["g_ff_pre"], y2, sp["g_mem_post"], name="bwd_ff")
    d_mo = _mm_tn(o2, dy2, name="mm_dmo").reshape(N_CHIP, D // N_CHIP, D)
    do2 = _mm_nt(dy2, g_mo, "rows", name="mm_do2")
    dq2, dk2, dv2 = _mem_bwd_call(q2, k2, v2, do2, name="mem_bwd")
    d_mq = _mm_tn(h2, dq2, name="mm_dmq").reshape(N_CHIP, D // N_CHIP, D)
    dh2 = _mm_nt(dq2, g_mq, "rows", name="mm_dh2", out_dtype=F32)
    d_mk = _mm_tn(memn, dk2, name="mm_dmk").reshape(N_CHIP, D // N_CHIP, D)
    d_mv = _mm_tn(memn, dv2, name="mm_dmv").reshape(N_CHIP, D // N_CHIP, D)
    dmn_k = _mm_nt(dk2, g_mk, "rows", name="mm_dmemk", out_dtype=F32)
    dmn_v = _mm_nt(dv2, g_mv, "rows", name="mm_dmemv", out_dtype=F32)
    dg_mem_kv = _gain_grad_call(mems, sp["g_mem_kv"], dmn_k, dmn_v, name="gain_mem_kv")
    dx1, dz, dg_mem_pre, dg_mix_post = _bwd_mid_call(dx2, x1, dh2, sp["g_mem_pre"], z, sp["g_mix_post"], name="bwd_mem")
    d_out = _mm_tn(yn, dz, name="mm_dout").reshape(N_CHIP, D // N_CHIP, D)
    late = ["w_mo", "w_mq", "w_mk", "w_mv", "w_out"]
    d_late = [d_mo, d_mq, d_mk, d_mv, d_out]
    dyn, r1_late = _mm_nt(dz, g_out, "rows", name="mm_dyn", out_dtype=F32, task=_rs_pair_task(d_late))
    p_late = [pair_add(k, d, r1) for k, d, r1 in zip(late, d_late, r1_late)]
    dof, doc, delta, dg_fox, dg_chk = _mix_norm_bwd_call(dyn, yf, yc, sp["g_fox_out"], sp["g_chk_out"], name="mix_norm_bwd")
    (dqf, dkf, dvf, dct, dcq), r2_ff = _fox_bwd_call(proj, dof, lse, delta, c_rep, c_t, name="fox_bwd",
                                                      task=_rs_chip_task([p_ff2, p_ff1]))
    (dqc, dkc, dvc, dgrev), r2_late = _chk_bwd_call(proj, doc, bias, name="chk_bwd", task=_rs_chip_task(p_late))
    first = ["w_ff2", "w_ff1"] + late
    f_first = [_chip_sum_call(p, r, place_arr, name="rs_chip_sum_" + k)
               for k, p, r in zip(first, [p_ff2, p_ff1] + p_late, r2_ff + r2_late)]
    dc8 = dct[:, :, 0:2, :].transpose(0, 2, 1, 3).reshape(8, S) + dcq[:, ::HEAD].T
    dc_rows = jnp.concatenate([dc8, jnp.zeros((120, S), F32)], axis=0)
    dfl, db_fgt = _fox_gate_bwd_call(dc_rows, fl_raw, b_pad, name="fox_gate_bwd")
    dproj = jnp.concatenate([dqf, dkf, dvf, dqc, dkc, dvc, dfl], axis=1)
    d_all_t, g_first = _mm_tn(dproj, h1, name="mm_dwin", tk=640, task=_rs_gather_task(f_first))
    grads = dict(zip(first, g_first))
    d_in = _reorder_rows_call(d_all_t, False, name="d_in_rows").reshape(N_CHIP, IN_PAD, D)
    dh1, (r1,) = _mm_nn(dproj, w_all_t, "plain", name="mm_dh1", out_dtype=F32, task=_rs_pair_task([d_in]))
    p_in = pair_add("w_in", d_in, r1)
    delta_w, new_m, new_v = {}, {}, {}
    upd, (r2_in,) = _adamw_call([(w[k][0], grads[k], m[k][0], v[k][0]) for k in first], name="adamw_7", tm=64,
                                task=_rs_chip_task([p_in]))
    for k, res in zip(first, upd):
        grads[k], delta_w[k], new_m[k], new_v[k] = res
    f_in = _chip_sum_call(p_in, r2_in, place_arr, name="rs_chip_sum_w_in")
    grad_x, dg_mix_pre = _bwd_last_call(dx1, xs, dh1, sp["g_mix_pre"], name="bwd_mix")

    small_g = {"g_mix_pre": dg_mix_pre, "g_mix_post": dg_mix_post, "g_mem_kv": dg_mem_kv, "g_mem_pre": dg_mem_pre,
               "g_mem_post": dg_mem_post, "g_ff_pre": dg_ff_pre, "g_ff_post": dg_ff_post, "g_fox_out": dg_fox,
               "g_chk_out": dg_chk, "b_fgt": db_fgt,
               "rel_bias": _g_to_rel_table(dgrev[:, 0:2, :].reshape(8, ROLL_W))}
    small, loss_out, (g_w_in,) = _small_call(
        small_g, sp, {k: rows(m, k) for k in SMALL}, {k: rows(v, k) for k in SMALL}, loss_blk,
        _rs_gather_task([f_in]), name="small_allreduce_adamw")
    loss = loss_out[0, 0]
    res = _adamw_cols_call(w_in_t, g_w_in, m_in_t, v_in_t, name="adamw_w_in")
    grads["w_in"], delta_w["w_in"], new_m["w_in"], new_v["w_in"] = (a.T for a in res)
    for k in SMALL:
        vals = small[k]
        if k == "rel_bias":
            vals = tuple(a[None] for a in vals)
        grads[k], delta_w[k], new_m[k], new_v[k] = vals

    def out(d, k):
        return d[k][None] if k in BIG else d[k]

    return (loss, grad_x[None], *[out(grads, k) for k in WEIGHTS], *[out(delta_w, k) for k in WEIGHTS],
            *[out(new_m, k) for k in WEIGHTS], *[out(new_v, k) for k in WEIGHTS])
```

```python
import functools

import jax
import jax.numpy as jnp
from jax import lax
from jax.experimental import pallas as pl
from jax.experimental.pallas import tpu as pltpu

F32 = jnp.float32
BF16 = jnp.bfloat16

D = 1024
HEAD = 64
N_PAIR = 4
D_GRP = 512
CHUNK = 64
LEFT = 8
MAX_REL = 128
N_REL = 2 * MAX_REL + 1
N_MEM = 256
MEM_HEADS = 4
MEM_HD = 256
D_FF = 4096
D_IN = 3080
D_ALL = 3200
EPS = 1e-6
TQ = 256
WIN = (LEFT + TQ // CHUNK) * CHUNK
PADK = LEFT * CHUNK
ROLL_W = 1024
NEG = -1e30
N_CHIP = 4
VMEM_LIMIT = 48 * 1024 * 1024

ADAM_LR = 0.001
ADAM_B1 = 0.9
ADAM_B2 = 0.999
ADAM_EPS = 1e-08
ADAM_WD = 0.01
ADAM_STEP = 10

MESH = pl.DeviceIdType.MESH


def _cparams():
    return pltpu.CompilerParams(vmem_limit_bytes=VMEM_LIMIT)


ANY = pl.BlockSpec(memory_space=pl.ANY)


class _Task:
    def __init__(self, arrays, out_shapes, sems, issue, drain, aliases=None):
        self.arrays, self.out_shapes, self.sems = list(arrays), list(out_shapes), list(sems)
        self.issue, self.drain, self.aliases = issue, drain, dict(aliases or {})


def _merge_tasks(tasks):
    tasks = [t for t in tasks if t is not None]
    if len(tasks) == 1:
        return tasks[0]
    cuts, a, o, s = [], 0, 0, 0
    aliases = {}
    for t in tasks:
        cuts.append((a, o, s))
        aliases.update({a + i: o + j for i, j in t.aliases.items()})
        a, o, s = a + len(t.arrays), o + len(t.out_shapes), s + len(t.sems)

    def part(fn_name):
        def run(ins, outs, sems):
            for t, (a0, o0, s0) in zip(tasks, cuts):
                getattr(t, fn_name)(ins[a0:a0 + len(t.arrays)], outs[o0:o0 + len(t.out_shapes)],
                                    sems[s0:s0 + len(t.sems)])
        return run

    return _Task([x for t in tasks for x in t.arrays], [x for t in tasks for x in t.out_shapes],
                 [x for t in tasks for x in t.sems], part("issue"), part("drain"), aliases)


def _pallas(body, *, grid, in_specs, out_specs, out_shape, name, scratch_shapes=(), task=None):
    if task is None:
        return pl.pallas_call(body, grid=grid, in_specs=list(in_specs), out_specs=out_specs, out_shape=out_shape,
                              scratch_shapes=list(scratch_shapes), name=name, compiler_params=_cparams())
    single = not isinstance(out_shape, (tuple, list))
    o_shapes = [out_shape] if single else list(out_shape)
    o_specs = [out_specs] if single else list(out_specs)
    n_in, n_out, n_scr = len(in_specs), len(o_shapes), len(scratch_shapes)
    t_in, t_out = len(task.arrays), len(task.out_shapes)

    def carried(*refs):
        cut = [n_in, t_in, n_out, t_out, n_scr]
        parts, p = [], 0
        for c in cut:
            parts.append(refs[p:p + c])
            p += c
        ins, tins, outs, touts, scr = parts
        tsems = refs[p:]
        ids = [pl.program_id(a) for a in range(len(grid))]
        first = functools.reduce(jnp.logical_and, [i == 0 for i in ids])
        last = functools.reduce(jnp.logical_and, [i == g - 1 for i, g in zip(ids, grid)])

        @pl.when(first)
        def _():
            task.issue(tins, touts, tsems)
        body(*ins, *outs, *scr)

        @pl.when(last)
        def _():
            task.drain(tins, touts, tsems)

    call = pl.pallas_call(
        carried, grid=grid, in_specs=list(in_specs) + [ANY] * t_in, out_specs=o_specs + [ANY] * t_out,
        out_shape=o_shapes + list(task.out_shapes), scratch_shapes=list(scratch_shapes) + list(task.sems),
        input_output_aliases={n_in + i: n_out + j for i, j in task.aliases.items()},
        name=name, compiler_params=_cparams())

    def run(*args):
        res = call(*args, *task.arrays)
        outs = res[:n_out]
        return (outs[0] if single else tuple(outs)), list(res[n_out:])

    return run


def _comm_call(task, *, name):
    t_in, t_out = len(task.arrays), len(task.out_shapes)

    def body(*refs):
        tins, touts, tsems = refs[:t_in], refs[t_in:t_in + t_out], refs[t_in + t_out:]
        task.issue(tins, touts, tsems)
        task.drain(tins, touts, tsems)

    return pl.pallas_call(
        body, in_specs=[ANY] * t_in, out_specs=[ANY] * t_out, out_shape=list(task.out_shapes),
        scratch_shapes=list(task.sems), input_output_aliases=dict(task.aliases), name=name,
    )(*task.arrays)


def _dot(a, b):
    return jnp.dot(a, b, preferred_element_type=F32)


def _dot_nt(a, b):
    return lax.dot_general(a, b, (((1,), (1,)), ((), ())), preferred_element_type=F32)


def _dot_tn(a, b):
    return lax.dot_general(a, b, (((0,), (0,)), ((), ())), preferred_element_type=F32)


def _split3(x):
    hi = x.astype(BF16)
    r1 = x - hi.astype(F32)
    mid = r1.astype(BF16)
    lo = (r1 - mid.astype(F32)).astype(BF16)
    return hi, mid, lo


def _dot3(x, m01):
    hi, mid, lo = _split3(x)
    return _dot(hi, m01) + _dot(mid, m01) + _dot(lo, m01)


def _dot3_l(m01, x):
    hi, mid, lo = _split3(x)
    return _dot(m01, hi) + _dot(m01, mid) + _dot(m01, lo)


def _mm_nn(a, b, kind, *, name, out_dtype=BF16, tm=2048, tn=512, epi=None, task=None):
    M, K = a.shape
    if kind == "plain":
        N = b.shape[1]
        b_spec = pl.BlockSpec((K, tn), lambda m, n: (0, n))
    elif kind == "rows":
        N = b.shape[2]
        b_spec = pl.BlockSpec((N_CHIP, K // N_CHIP, tn), lambda m, n: (0, 0, n))
    else:
        nq = b.shape[2]
        N = N_CHIP * nq
        per = nq // tn
        b_spec = pl.BlockSpec((None, K, tn), lambda m, n: (n // per, 0, n % per))
    tm = min(tm, M)
    kq = K // N_CHIP

    def body(a_ref, b_ref, *o_refs):
        if kind == "rows":
            acc = _dot(a_ref[:, 0:kq], b_ref[0])
            for j in range(1, N_CHIP):
                acc += _dot(a_ref[:, j * kq:(j + 1) * kq], b_ref[j])
        else:
            acc = _dot(a_ref[...], b_ref[...])
        if epi == "relu2":
            r = jnp.maximum(acc, 0.0)
            o_refs[0][...] = (r * r).astype(BF16)
            o_refs[1][...] = r.astype(BF16)
        else:
            o_refs[0][...] = acc.astype(out_dtype)

    o_spec = pl.BlockSpec((tm, tn), lambda m, n: (m, n))
    if epi == "relu2":
        out_shape = (jax.ShapeDtypeStruct((M, N), BF16), jax.ShapeDtypeStruct((M, N), BF16))
        out_specs = (o_spec, o_spec)
    else:
        out_shape = jax.ShapeDtypeStruct((M, N), out_dtype)
        out_specs = o_spec
    return _pallas(
        body, grid=(M // tm, N // tn),
        in_specs=[pl.BlockSpec((tm, K), lambda m, n: (m, 0)), b_spec],
        out_specs=out_specs, out_shape=out_shape, name=name, task=task,
    )(a, b)


def _mm_nt(a, b, kind, *, name, out_dtype=BF16, tm=2048, tn=512, mul2r=None, task=None, rows=None):
    M, K = a.shape
    if kind == "plain":
        first, N = rows if rows is not None else (0, b.shape[0])
        n0 = first // tn
        b_spec = pl.BlockSpec((tn, K), lambda m, n: (n0 + n, 0))
    elif kind == "rows":
        nq = b.shape[1]
        N = N_CHIP * nq
        tn = min(tn, nq)
        per = nq // tn
        b_spec = pl.BlockSpec((None, tn, K), lambda m, n: (n // per, n % per, 0))
    else:
        N = b.shape[1]
        b_spec = pl.BlockSpec((N_CHIP, tn, K // N_CHIP), lambda m, n: (0, n, 0))
    tm = min(tm, M)
    kq = K // N_CHIP

    def body(a_ref, b_ref, *rest):
        o_ref = rest[-1]
        if kind == "cols":
            acc = _dot_nt(a_ref[:, 0:kq], b_ref[0])
            for j in range(1, N_CHIP):
                acc += _dot_nt(a_ref[:, j * kq:(j + 1) * kq], b_ref[j])
        else:
            acc = _dot_nt(a_ref[...], b_ref[...])
        if mul2r is not None:
            acc = acc * (2.0 * rest[0][...].astype(F32))
        o_ref[...] = acc.astype(out_dtype)

    in_specs = [pl.BlockSpec((tm, K), lambda m, n: (m, 0)), b_spec]
    args = [a, b]
    if mul2r is not None:
        in_specs.append(pl.BlockSpec((tm, tn), lambda m, n: (m, n)))
        args.append(mul2r)
    return _pallas(
        body, grid=(M // tm, N // tn), in_specs=in_specs,
        out_specs=pl.BlockSpec((tm, tn), lambda m, n: (m, n)),
        out_shape=jax.ShapeDtypeStruct((M, N), out_dtype), name=name, task=task,
    )(*args)


def _mm_tn(a, b, *, name, out_dtype=BF16, tk=1024, tn=512, cols4=False, task=None):
    M, K1 = a.shape
    N = b.shape[1]
    tk = min(tk, K1)
    tn = min(tn, N)

    def body(a_ref, b_ref, o_ref):
        o_ref[...] = _dot_tn(a_ref[...], b_ref[...]).astype(out_dtype)

    if cols4:
        per = (N // N_CHIP) // tn
        out_shape = jax.ShapeDtypeStruct((N_CHIP, K1, N // N_CHIP), out_dtype)
        o_spec = pl.BlockSpec((None, tk, tn), lambda k, n: (n // per, k, n % per))
    else:
        out_shape = jax.ShapeDtypeStruct((K1, N), out_dtype)
        o_spec = pl.BlockSpec((tk, tn), lambda k, n: (k, n))
    return _pallas(
        body, grid=(K1 // tk, N // tn),
        in_specs=[pl.BlockSpec((M, tk), lambda k, n: (0, k)), pl.BlockSpec((M, tn), lambda k, n: (0, n))],
        out_specs=o_spec, out_shape=out_shape, name=name, task=task,
    )(a, b)


def _rms(x, g):
    r = lax.rsqrt(jnp.mean(x * x, axis=-1, keepdims=True) + EPS)
    return x * r * g


def _rms_bwd(x, g, dy):
    r = lax.rsqrt(jnp.mean(x * x, axis=-1, keepdims=True) + EPS)
    xh = x * r
    dg = jnp.sum(dy * xh, axis=0, keepdims=True)
    dxh = dy * g
    dx = r * (dxh - xh * jnp.mean(dxh * xh, axis=-1, keepdims=True))
    return dx, dg


def _row_spec(tm, n):
    return pl.BlockSpec((tm, n), lambda i: (i, 0))


def _vec_spec(n):
    return pl.BlockSpec((1, n), lambda i: (0, 0))


def _acc_spec(n):
    return pl.BlockSpec((8, n), lambda i: (0, 0))


def _acc_add(ref, row, i):
    @pl.when(i == 0)
    def _():
        ref[...] = jnp.zeros_like(ref)
    ref[0:1, :] += row


def _rms_fwd_call(x, g, *, name, tm=256, task=None):
    M, n = x.shape
    tm = min(tm, M)

    def body(x_ref, g_ref, h_ref):
        h_ref[...] = _rms(x_ref[...], g_ref[...]).astype(BF16)

    return _pallas(
        body, grid=(M // tm,), in_specs=[_row_spec(tm, n), _vec_spec(n)], out_specs=_row_spec(tm, n),
        out_shape=jax.ShapeDtypeStruct((M, n), BF16), name=name, task=task,
    )(x, g)


def _post_pre_call(xres, z, g_post, g_pre, *, name, tm=256):
    M, n = xres.shape

    def body(x_ref, z_ref, gp_ref, gn_ref, xo_ref, h_ref):
        xn = x_ref[...] + _rms(z_ref[...], gp_ref[...])
        xo_ref[...] = xn
        h_ref[...] = _rms(xn, gn_ref[...]).astype(BF16)

    return pl.pallas_call(
        body, grid=(M // tm,),
        in_specs=[_row_spec(tm, n), _row_spec(tm, n), _vec_spec(n), _vec_spec(n)],
        out_specs=(_row_spec(tm, n), _row_spec(tm, n)),
        out_shape=(jax.ShapeDtypeStruct((M, n), F32), jax.ShapeDtypeStruct((M, n), BF16)),
        name=name, compiler_params=_cparams(),
    )(xres, z, g_post, g_pre)


def _final_call(x2, y3, g_post, target, *, name, tm=256):
    M, n = x2.shape

    def body(x_ref, y_ref, g_ref, t_ref, loss_ref, dx_ref, dy_ref, dg_ref):
        i = pl.program_id(0)
        y = y_ref[...]
        g = g_ref[...]
        diff = x_ref[...] + _rms(y, g) - t_ref[...]
        part = 0.5 * jnp.sum(jnp.sum(diff * diff, axis=1, keepdims=True), axis=0, keepdims=True) / n

        @pl.when(i == 0)
        def _():
            loss_ref[...] = jnp.zeros_like(loss_ref)
        loss_ref[...] += jnp.broadcast_to(part, loss_ref.shape)
        dx = diff / n
        dx_ref[...] = dx
        dy, dg = _rms_bwd(y, g, dx)
        dy_ref[...] = dy.astype(BF16)
        _acc_add(dg_ref, dg, i)

    return pl.pallas_call(
        body, grid=(M // tm,),
        in_specs=[_row_spec(tm, n), _row_spec(tm, n), _vec_spec(n), _row_spec(tm, n)],
        out_specs=(pl.BlockSpec((8, 128), lambda i: (0, 0)), _row_spec(tm, n), _row_spec(tm, n), _acc_spec(n)),
        out_shape=(jax.ShapeDtypeStruct((8, 128), F32), jax.ShapeDtypeStruct((M, n), F32),
                   jax.ShapeDtypeStruct((M, n), BF16), jax.ShapeDtypeStruct((8, n), F32)),
        name=name, compiler_params=_cparams(),
    )(x2, y3, g_post, target)


def _bwd_mid_call(dx_in, x, dh, g_pre, y, g_post, *, name, tm=256):
    M, n = x.shape

    def body(dxi_ref, x_ref, dh_ref, gpre_ref, y_ref, gpost_ref, dx_ref, dy_ref, dgpre_ref, dgpost_ref):
        i = pl.program_id(0)
        d1, dg1 = _rms_bwd(x_ref[...], gpre_ref[...], dh_ref[...])
        dx = dxi_ref[...] + d1
        dx_ref[...] = dx
        dy, dg2 = _rms_bwd(y_ref[...], gpost_ref[...], dx)
        dy_ref[...] = dy.astype(BF16)
        _acc_add(dgpre_ref, dg1, i)
        _acc_add(dgpost_ref, dg2, i)

    return pl.pallas_call(
        body, grid=(M // tm,),
        in_specs=[_row_spec(tm, n), _row_spec(tm, n), _row_spec(tm, n), _vec_spec(n), _row_spec(tm, n), _vec_spec(n)],
        out_specs=(_row_spec(tm, n), _row_spec(tm, n), _acc_spec(n), _acc_spec(n)),
        out_shape=(jax.ShapeDtypeStruct((M, n), F32), jax.ShapeDtypeStruct((M, n), BF16),
                   jax.ShapeDtypeStruct((8, n), F32), jax.ShapeDtypeStruct((8, n), F32)),
        name=name, compiler_params=_cparams(),
    )(dx_in, x, dh, g_pre, y, g_post)


def _bwd_last_call(dx_in, x, dh, g_pre, *, name, tm=256, task=None):
    M, n = x.shape

    def body(dxi_ref, x_ref, dh_ref, g_ref, dx_ref, dg_ref):
        i = pl.program_id(0)
        d1, dg1 = _rms_bwd(x_ref[...], g_ref[...], dh_ref[...])
        dx_ref[...] = dxi_ref[...] + d1
        _acc_add(dg_ref, dg1, i)

    return _pallas(
        body, grid=(M // tm,),
        in_specs=[_row_spec(tm, n), _row_spec(tm, n), _row_spec(tm, n), _vec_spec(n)],
        out_specs=(_row_spec(tm, n), _acc_spec(n)),
        out_shape=(jax.ShapeDtypeStruct((M, n), F32), jax.ShapeDtypeStruct((8, n), F32)),
        name=name, task=task,
    )(dx_in, x, dh, g_pre)


def _gain_grad_call(x, g, dy_a, dy_b, *, name):
    M, n = x.shape

    def body(x_ref, g_ref, a_ref, b_ref, dg_ref):
        _, dg = _rms_bwd(x_ref[...], g_ref[...], a_ref[...] + b_ref[...])
        dg_ref[...] = jnp.zeros_like(dg_ref)
        dg_ref[0:1, :] = dg

    return pl.pallas_call(
        body, grid=(1,),
        in_specs=[_row_spec(M, n), _vec_spec(n), _row_spec(M, n), _row_spec(M, n)],
        out_specs=_acc_spec(n), out_shape=jax.ShapeDtypeStruct((8, n), F32),
        name=name, compiler_params=_cparams(),
    )(x, g, dy_a, dy_b)


def _head_group_matrix():
    a = lax.broadcasted_iota(jnp.int32, (D_GRP, D_GRP), 0) // HEAD
    b = lax.broadcasted_iota(jnp.int32, (D_GRP, D_GRP), 1) // HEAD
    return jnp.where(a == b, 1.0, 0.0).astype(BF16)


def _mix_norm_fwd_call(yf, yc, gf, gc, *, name, tm=256):
    M = yf.shape[0]

    def body(yf_ref, yc_ref, gf_ref, gc_ref, o_ref):
        o_ref[:, 0:D_GRP] = _rms(yf_ref[...], gf_ref[...]).astype(BF16)
        o_ref[:, D_GRP:D] = _rms(yc_ref[...], gc_ref[...]).astype(BF16)

    return pl.pallas_call(
        body, grid=(M // tm,),
        in_specs=[_row_spec(tm, D_GRP), _row_spec(tm, D_GRP), _vec_spec(D_GRP), _vec_spec(D_GRP)],
        out_specs=_row_spec(tm, D), out_shape=jax.ShapeDtypeStruct((M, D), BF16),
        name=name, compiler_params=_cparams(),
    )(yf, yc, gf, gc)


def _mix_norm_bwd_call(dyn, yf, yc, gf, gc, *, name, tm=256):
    M = yf.shape[0]

    def body(dyn_ref, yf_ref, yc_ref, gf_ref, gc_ref, dof_ref, doc_ref, delta_ref, dgf_ref, dgc_ref):
        i = pl.program_id(0)
        yf_ = yf_ref[...]
        dof, dgf = _rms_bwd(yf_, gf_ref[...], dyn_ref[:, 0:D_GRP])
        doc, dgc = _rms_bwd(yc_ref[...], gc_ref[...], dyn_ref[:, D_GRP:D])
        dof_b = dof.astype(BF16)
        dof_ref[...] = dof_b
        doc_ref[...] = doc.astype(BF16)
        prod = dof_b.astype(F32) * yf_
        hi = prod.astype(BF16)
        lo = (prod - hi.astype(F32)).astype(BF16)
        grp = _head_group_matrix()
        delta_ref[...] = _dot(hi, grp) + _dot(lo, grp)
        _acc_add(dgf_ref, dgf, i)
        _acc_add(dgc_ref, dgc, i)

    return pl.pallas_call(
        body, grid=(M // tm,),
        in_specs=[_row_spec(tm, D), _row_spec(tm, D_GRP), _row_spec(tm, D_GRP), _vec_spec(D_GRP), _vec_spec(D_GRP)],
        out_specs=(_row_spec(tm, D_GRP), _row_spec(tm, D_GRP), _row_spec(tm, D_GRP), _acc_spec(D_GRP), _acc_spec(D_GRP)),
        out_shape=(jax.ShapeDtypeStruct((M, D_GRP), BF16), jax.ShapeDtypeStruct((M, D_GRP), BF16),
                   jax.ShapeDtypeStruct((M, D_GRP), F32), jax.ShapeDtypeStruct((8, D_GRP), F32),
                   jax.ShapeDtypeStruct((8, D_GRP), F32)),
        name=name, compiler_params=_cparams(),
    )(dyn, yf, yc, gf, gc)


def _tri(n, lower_incl):
    a = lax.broadcasted_iota(jnp.int32, (n, n), 0)
    b = lax.broadcasted_iota(jnp.int32, (n, n), 1)
    return jnp.where(a >= b, 1.0, 0.0).astype(BF16) if lower_incl else jnp.where(a <= b, 1.0, 0.0).astype(BF16)


def _fox_prep_call(fl_raw, b_pad, *, name):
    S = fl_raw.shape[0]
    nb = S // TQ

    def body(fl_ref, b_ref, crep_ref, ct_ref, carry_ref):
        i = pl.program_id(0)

        @pl.when(i == 0)
        def _():
            carry_ref[...] = jnp.zeros_like(carry_ref)
        logf = jax.nn.log_sigmoid(fl_ref[...] + b_ref[...])
        cb = _dot3_l(_tri(TQ, True), logf) + carry_ref[0:1, :]
        carry_ref[0:1, :] = cb[TQ - 1:TQ, :]
        a = lax.broadcasted_iota(jnp.int32, (128, D_GRP), 0)
        b = lax.broadcasted_iota(jnp.int32, (128, D_GRP), 1) // HEAD
        expand = jnp.where(a == b, 1.0, 0.0).astype(BF16)
        crep = _dot3(cb, expand)
        crep_ref[...] = crep
        ct_ref[...] = crep.T

    return pl.pallas_call(
        body, grid=(nb,),
        in_specs=[_row_spec(TQ, 128), _vec_spec(128)],
        out_specs=(_row_spec(TQ, D_GRP), pl.BlockSpec((None, D_GRP, TQ), lambda i: (i, 0, 0))),
        out_shape=(jax.ShapeDtypeStruct((S, D_GRP), F32), jax.ShapeDtypeStruct((nb, D_GRP, TQ), F32)),
        scratch_shapes=[pltpu.VMEM((8, 128), F32)],
        name=name, compiler_params=_cparams(),
    )(fl_raw, b_pad)


def _lane_masks():
    lane = lax.broadcasted_iota(jnp.int32, (1, 128), 1)
    return lane < HEAD, lane >= HEAD


def _fox_fwd_call(proj, c_rep, c_t, *, name, task=None):
    S = proj.shape[0]
    nq = S // TQ
    scale = HEAD ** -0.5

    def body(q_ref, k_ref, v_ref, c_ref, ct_ref, o_ref, lse_ref):
        i = pl.program_id(1)
        m_lo, m_hi = _lane_masks()
        masks = (m_lo, m_hi)
        q = q_ref[...]
        qm = [jnp.where(mk, q, jnp.zeros_like(q)) for mk in masks]
        cq = c_ref[...]
        cqh = [cq[:, 0:1], cq[:, HEAD:HEAD + 1]]
        row = lax.broadcasted_iota(jnp.int32, (TQ, TQ), 0)
        col = lax.broadcasted_iota(jnp.int32, (TQ, TQ), 1)

        def step(j, carry, masked):
            ms, ls, acc = carry
            start = pl.multiple_of(j * TQ, TQ)
            k = k_ref[pl.ds(start, TQ), :]
            v = v_ref[pl.ds(start, TQ), :]
            ct = ct_ref[j]
            new_m, new_l, pv, alpha_l = [], [], [], []
            for h in range(2):
                s = _dot_nt(qm[h], k) * scale + (cqh[h] - ct[HEAD * h:HEAD * h + 1, :])
                if masked:
                    s = jnp.where(row >= col, s, NEG)
                mn = jnp.maximum(ms[h], jnp.max(s, axis=1, keepdims=True))
                alpha = jnp.exp(ms[h] - mn)
                p = jnp.exp(s - mn)
                new_l.append(alpha * ls[h] + jnp.sum(p, axis=1, keepdims=True))
                new_m.append(mn)
                alpha_l.append(alpha)
                pv.append(_dot(p.astype(BF16), jnp.where(masks[h], v, jnp.zeros_like(v))))
            alpha_lane = jnp.where(m_lo, alpha_l[0], alpha_l[1])
            acc = acc * alpha_lane + pv[0] + pv[1]
            return (tuple(new_m), tuple(new_l), acc)

        init = ((jnp.full((TQ, 1), NEG, F32),) * 2, (jnp.zeros((TQ, 1), F32),) * 2, jnp.zeros((TQ, 128), F32))
        carry = lax.fori_loop(0, i, lambda j, c: step(j, c, False), init)
        ms, ls, acc = step(i, carry, True)
        l_lane = jnp.where(m_lo, ls[0], ls[1])
        o_ref[...] = acc / l_lane
        lse_ref[...] = jnp.where(m_lo, ms[0] + jnp.log(ls[0]), ms[1] + jnp.log(ls[1]))

    return _pallas(
        body, grid=(N_PAIR, nq),
        in_specs=[pl.BlockSpec((TQ, 128), lambda p, i: (i, p)),
                  pl.BlockSpec((S, 128), lambda p, i: (0, N_PAIR + p)),
                  pl.BlockSpec((S, 128), lambda p, i: (0, 2 * N_PAIR + p)),
                  pl.BlockSpec((TQ, 128), lambda p, i: (i, p)),
                  pl.BlockSpec((nq, 128, TQ), lambda p, i: (0, p, 0))],
        out_specs=(pl.BlockSpec((TQ, 128), lambda p, i: (i, p)), pl.BlockSpec((TQ, 128), lambda p, i: (i, p))),
        out_shape=(jax.ShapeDtypeStruct((S, D_GRP), F32), jax.ShapeDtypeStruct((S, D_GRP), F32)),
        name=name, task=task,
    )(proj, proj, proj, c_rep, c_t)


def _fox_bwd_call(proj, do, lse_rep, delta_rep, c_rep, c_t, *, name, task=None):
    S = proj.shape[0]
    nq = S // TQ
    scale = HEAD ** -0.5

    def body(q_ref, k_ref, v_ref, do_ref, lse_ref, dl_ref, c_ref, ct_ref, dq_ref, dk_ref, dv_ref, dct_ref, dcq_ref, dqa_ref):
        j = pl.program_id(1)
        m_lo, m_hi = _lane_masks()
        masks = (m_lo, m_hi)

        @pl.when(j == 0)
        def _():
            dqa_ref[...] = jnp.zeros_like(dqa_ref)
            dcq_ref[...] = jnp.zeros_like(dcq_ref)
        k = k_ref[...]
        v = v_ref[...]
        km = [jnp.where(mk, k, jnp.zeros_like(k)) for mk in masks]
        ct = ct_ref[...]
        row = lax.broadcasted_iota(jnp.int32, (TQ, TQ), 0)
        col = lax.broadcasted_iota(jnp.int32, (TQ, TQ), 1)

        def step(i, carry, masked):
            dk, dv, dcs = carry
            start = pl.multiple_of(i * TQ, TQ)
            q = q_ref[pl.ds(start, TQ), :]
            do = do_ref[pl.ds(start, TQ), :]
            lse = lse_ref[pl.ds(start, TQ), :]
            dl = dl_ref[pl.ds(start, TQ), :]
            cq = c_ref[pl.ds(start, TQ), :]
            dq = jnp.zeros((TQ, 128), F32)
            new_dcs = []
            rows = []
            for h in range(2):
                lo = HEAD * h
                qm = jnp.where(masks[h], q, jnp.zeros_like(q))
                dom = jnp.where(masks[h], do, jnp.zeros_like(do))
                s = _dot_nt(qm, k) * scale + (cq[:, lo:lo + 1] - ct[lo:lo + 1, :])
                p = jnp.exp(s - lse[:, lo:lo + 1])
                if masked:
                    p = jnp.where(row >= col, p, 0.0)
                dp = _dot_nt(dom, v)
                ds = p * (dp - dl[:, lo:lo + 1])
                new_dcs.append(dcs[h] + jnp.sum(ds, axis=0, keepdims=True))
                rows.append(jnp.sum(ds, axis=1, keepdims=True))
                dsb = (ds * scale).astype(BF16)
                dv = dv + _dot_tn(p.astype(BF16), dom)
                dk = dk + _dot_tn(dsb, qm)
                dq = dq + _dot(dsb, km[h])
            dqa_ref[pl.ds(start, TQ), :] += dq
            dcq_ref[pl.ds(start, TQ), :] += jnp.where(m_lo, rows[0], rows[1])
            return (dk, dv, tuple(new_dcs))

        init = (jnp.zeros((TQ, 128), F32), jnp.zeros((TQ, 128), F32), (jnp.zeros((1, TQ), F32),) * 2)
        carry = step(j, init, True)
        dk, dv, dcs = lax.fori_loop(j + 1, nq, lambda i, c: step(i, c, False), carry)
        dk_ref[...] = dk.astype(BF16)
        dv_ref[...] = dv.astype(BF16)
        dct_ref[...] = jnp.zeros_like(dct_ref)
        dct_ref[0:1, :] = -dcs[0]
        dct_ref[1:2, :] = -dcs[1]

        @pl.when(j == nq - 1)
        def _():
            dq_ref[...] = dqa_ref[...].astype(BF16)

    res = lambda p, j: (0, p)
    return _pallas(
        body, grid=(N_PAIR, nq), task=task,
        in_specs=[pl.BlockSpec((S, 128), res),
                  pl.BlockSpec((TQ, 128), lambda p, j: (j, N_PAIR + p)),
                  pl.BlockSpec((TQ, 128), lambda p, j: (j, 2 * N_PAIR + p)),
                  pl.BlockSpec((S, 128), res), pl.BlockSpec((S, 128), res), pl.BlockSpec((S, 128), res),
                  pl.BlockSpec((S, 128), res),
                  pl.BlockSpec((None, 128, TQ), lambda p, j: (j, p, 0))],
        out_specs=(pl.BlockSpec((S, 128), res),
                   pl.BlockSpec((TQ, 128), lambda p, j: (j, p)), pl.BlockSpec((TQ, 128), lambda p, j: (j, p)),
                   pl.BlockSpec((None, None, 8, TQ), lambda p, j: (p, j, 0, 0)),
                   pl.BlockSpec((S, 128), res)),
        out_shape=(jax.ShapeDtypeStruct((S, D_GRP), BF16), jax.ShapeDtypeStruct((S, D_GRP), BF16),
                   jax.ShapeDtypeStruct((S, D_GRP), BF16), jax.ShapeDtypeStruct((N_PAIR, nq, 8, TQ), F32),
                   jax.ShapeDtypeStruct((S, D_GRP), F32)),
        scratch_shapes=[pltpu.VMEM((S, 128), F32)],
        name=name,
    )(proj, proj, proj, do, lse_rep, delta_rep, c_rep, c_t)


def _fox_gate_bwd_call(dc_rows, fl_raw, b_pad, *, name):
    S = fl_raw.shape[0]
    nb = S // TQ

    def body(dc_ref, fl_ref, b_ref, dfl_ref, db_ref, carry_ref):
        i = pl.program_id(0)

        @pl.when(i == 0)
        def _():
            carry_ref[...] = jnp.zeros_like(carry_ref)
        rc = _dot3(dc_ref[...], _tri(TQ, True)) + carry_ref[:, 0:1]
        carry_ref[...] = jnp.broadcast_to(rc[:, 0:1], carry_ref.shape)
        fl = fl_ref[...] + b_ref[...]
        dfl = rc.T * jax.nn.sigmoid(-fl)
        dfl_ref[...] = dfl.astype(BF16)
        _acc_add(db_ref, jnp.sum(dfl, axis=0, keepdims=True), i)

    rev = lambda i: (nb - 1 - i, 0)
    return pl.pallas_call(
        body, grid=(nb,),
        in_specs=[pl.BlockSpec((128, TQ), lambda i: (0, nb - 1 - i)), pl.BlockSpec((TQ, 128), rev), _vec_spec(128)],
        out_specs=(pl.BlockSpec((TQ, 128), rev), _acc_spec(128)),
        out_shape=(jax.ShapeDtypeStruct((S, 128), BF16), jax.ShapeDtypeStruct((8, 128), F32)),
        scratch_shapes=[pltpu.VMEM((128, 128), F32)],
        name=name, compiler_params=_cparams(),
    )(dc_rows, fl_raw, b_pad)


def _chk_bias_call(g_rev, *, name):
    def body(g_ref, o_ref):
        x = jnp.broadcast_to(g_ref[...], (TQ, ROLL_W))
        rolled = pltpu.roll(x, ROLL_W - (TQ - 1), 1, stride=1, stride_axis=0)
        qc = lax.broadcasted_iota(jnp.int32, (TQ, WIN), 0) // CHUNK
        kc = lax.broadcasted_iota(jnp.int32, (TQ, WIN), 1) // CHUNK
        band = (kc >= qc) & (kc <= qc + LEFT)
        o_ref[...] = jnp.where(band, rolled[:, 0:WIN], NEG)

    return pl.pallas_call(
        body, grid=(8,),
        in_specs=[pl.BlockSpec((None, 1, ROLL_W), lambda h: (h, 0, 0))],
        out_specs=pl.BlockSpec((None, TQ, WIN), lambda h: (h, 0, 0)),
        out_shape=jax.ShapeDtypeStruct((8, TQ, WIN), F32), name=name, compiler_params=_cparams(),
    )(g_rev.reshape(8, 1, ROLL_W))


def _chk_scores(i, qm, kwin, bias, scale):
    s = _dot_nt(qm, kwin) * scale + bias
    kc = lax.broadcasted_iota(jnp.int32, (TQ, WIN), 1) // CHUNK
    return jnp.where(kc + i * (TQ // CHUNK) >= LEFT, s, NEG)


def _chk_fwd_call(proj, bias, *, name, task=None):
    S = proj.shape[0]
    nq = S // TQ
    scale = HEAD ** -0.5

    def body(q_ref, k_ref, v_ref, b_ref, o_ref, kp_ref, vp_ref):
        i = pl.program_id(1)

        @pl.when(i == 0)
        def _():
            kp_ref[0:PADK, :] = jnp.zeros((PADK, 128), BF16)
            vp_ref[0:PADK, :] = jnp.zeros((PADK, 128), BF16)
            kp_ref[PADK:PADK + S, :] = k_ref[...]
            vp_ref[PADK:PADK + S, :] = v_ref[...]
        masks = _lane_masks()
        q = q_ref[...]
        start = pl.multiple_of(i * TQ, TQ)
        kwin = kp_ref[pl.ds(start, WIN), :]
        vwin = vp_ref[pl.ds(start, WIN), :]
        out = jnp.zeros((TQ, 128), F32)
        for h in range(2):
            qm = jnp.where(masks[h], q, jnp.zeros_like(q))
            s = _chk_scores(i, qm, kwin, b_ref[h], scale)
            p = jnp.exp(s - jnp.max(s, axis=1, keepdims=True))
            p = p / jnp.sum(p, axis=1, keepdims=True)
            out = out + _dot(p.astype(BF16), jnp.where(masks[h], vwin, jnp.zeros_like(vwin)))
        o_ref[...] = out

    c0 = 3 * N_PAIR
    return _pallas(
        body, grid=(N_PAIR, nq), task=task,
        in_specs=[pl.BlockSpec((TQ, 128), lambda p, i: (i, c0 + p)),
                  pl.BlockSpec((S, 128), lambda p, i: (0, c0 + N_PAIR + p)),
                  pl.BlockSpec((S, 128), lambda p, i: (0, c0 + 2 * N_PAIR + p)),
                  pl.BlockSpec((2, TQ, WIN), lambda p, i: (p, 0, 0))],
        out_specs=pl.BlockSpec((TQ, 128), lambda p, i: (i, p)),
        out_shape=jax.ShapeDtypeStruct((S, D_GRP), F32),
        scratch_shapes=[pltpu.VMEM((S + PADK, 128), BF16), pltpu.VMEM((S + PADK, 128), BF16)],
        name=name,
    )(proj, proj, proj, bias)


def _chk_bwd_call(proj, do, bias, *, name, task=None):
    S = proj.shape[0]
    nq = S // TQ
    scale = HEAD ** -0.5

    def body(q_ref, k_ref, v_ref, do_ref, b_ref, dq_ref, dk_ref, dv_ref, dg_ref, kp_ref, vp_ref, dkp_ref, dvp_ref, db_ref):
        i = pl.program_id(1)

        @pl.when(i == 0)
        def _():
            kp_ref[0:PADK, :] = jnp.zeros((PADK, 128), BF16)
            vp_ref[0:PADK, :] = jnp.zeros((PADK, 128), BF16)
            kp_ref[PADK:PADK + S, :] = k_ref[...]
            vp_ref[PADK:PADK + S, :] = v_ref[...]
            dkp_ref[...] = jnp.zeros_like(dkp_ref)
            dvp_ref[...] = jnp.zeros_like(dvp_ref)
            db_ref[...] = jnp.zeros_like(db_ref)
        masks = _lane_masks()
        q = q_ref[...]
        dout = do_ref[...]
        start = pl.multiple_of(i * TQ, TQ)
        kwin = kp_ref[pl.ds(start, WIN), :]
        vwin = vp_ref[pl.ds(start, WIN), :]
        dq = jnp.zeros((TQ, 128), F32)
        dkw = jnp.zeros((WIN, 128), F32)
        dvw = jnp.zeros((WIN, 128), F32)
        for h in range(2):
            qm = jnp.where(masks[h], q, jnp.zeros_like(q))
            dom = jnp.where(masks[h], dout, jnp.zeros_like(dout))
            s = _chk_scores(i, qm, kwin, b_ref[h], scale)
            p = jnp.exp(s - jnp.max(s, axis=1, keepdims=True))
            p = p / jnp.sum(p, axis=1, keepdims=True)
            dp = _dot_nt(dom, vwin)
            ds = p * (dp - jnp.sum(p * dp, axis=1, keepdims=True))
            db_ref[h] += ds
            dsb = (ds * scale).astype(BF16)
            dq = dq + _dot(dsb, jnp.where(masks[h], kwin, jnp.zeros_like(kwin)))
            dkw = dkw + _dot_tn(dsb, qm)
            dvw = dvw + _dot_tn(p.astype(BF16), dom)
        dq_ref[...] = dq.astype(BF16)
        dkp_ref[pl.ds(start, WIN), :] += dkw
        dvp_ref[pl.ds(start, WIN), :] += dvw

        @pl.when(i == nq - 1)
        def _():
            dk_ref[...] = dkp_ref[PADK:PADK + S, :].astype(BF16)
            dv_ref[...] = dvp_ref[PADK:PADK + S, :].astype(BF16)
            a = lax.broadcasted_iota(jnp.int32, (TQ, TQ), 0)
            b = lax.broadcasted_iota(jnp.int32, (TQ, TQ), 1)
            flip = jnp.where(a + b == TQ - 1, 1.0, 0.0).astype(BF16)
            e = lax.broadcasted_iota(jnp.int32, (1, ROLL_W), 1)
            dg_ref[...] = jnp.zeros_like(dg_ref)
            for h in range(2):
                rev = _dot3_l(flip, db_ref[h])
                wide = jnp.concatenate([rev, jnp.zeros((TQ, ROLL_W - WIN), F32)], axis=1)
                diag = pltpu.roll(wide, 0, 1, stride=1, stride_axis=0)
                dg = jnp.sum(diag, axis=0, keepdims=True)
                lo = jnp.sum(jnp.where(e <= 639, dg, 0.0), axis=1, keepdims=True)
                hi = jnp.sum(jnp.where(e >= 895, dg, 0.0), axis=1, keepdims=True)
                dg_ref[h:h + 1, :] = jnp.where(e == 639, lo, jnp.where(e == 895, hi, dg))

    c0 = 3 * N_PAIR
    res = lambda p, i: (0, p)
    return _pallas(
        body, grid=(N_PAIR, nq), task=task,
        in_specs=[pl.BlockSpec((TQ, 128), lambda p, i: (i, c0 + p)),
                  pl.BlockSpec((S, 128), lambda p, i: (0, c0 + N_PAIR + p)),
                  pl.BlockSpec((S, 128), lambda p, i: (0, c0 + 2 * N_PAIR + p)),
                  pl.BlockSpec((TQ, 128), lambda p, i: (i, p)),
                  pl.BlockSpec((2, TQ, WIN), lambda p, i: (p, 0, 0))],
        out_specs=(pl.BlockSpec((TQ, 128), lambda p, i: (i, p)), pl.BlockSpec((S, 128), res),
                   pl.BlockSpec((S, 128), res), pl.BlockSpec((None, 8, ROLL_W), lambda p, i: (p, 0, 0))),
        out_shape=(jax.ShapeDtypeStruct((S, D_GRP), BF16), jax.ShapeDtypeStruct((S, D_GRP), BF16),
                   jax.ShapeDtypeStruct((S, D_GRP), BF16), jax.ShapeDtypeStruct((N_PAIR, 8, ROLL_W), F32)),
        scratch_shapes=[pltpu.VMEM((S + PADK, 128), BF16), pltpu.VMEM((S + PADK, 128), BF16),
                        pltpu.VMEM((S + PADK, 128), F32), pltpu.VMEM((S + PADK, 128), F32),
                        pltpu.VMEM((2, TQ, WIN), F32)],
        name=name,
    )(proj, proj, proj, do, bias)


def _mem_fwd_call(q, k, v, *, name, tq=512):
    S = q.shape[0]
    scale = MEM_HD ** -0.5

    def body(q_ref, k_ref, v_ref, o_ref):
        s = _dot_nt(q_ref[...], k_ref[...]) * scale
        p = jnp.exp(s - jnp.max(s, axis=1, keepdims=True))
        p = p / jnp.sum(p, axis=1, keepdims=True)
        o_ref[...] = _dot(p.astype(BF16), v_ref[...]).astype(BF16)

    return pl.pallas_call(
        body, grid=(MEM_HEADS, S // tq),
        in_specs=[pl.BlockSpec((tq, MEM_HD), lambda h, i: (i, h)),
                  pl.BlockSpec((N_MEM, MEM_HD), lambda h, i: (0, h)),
                  pl.BlockSpec((N_MEM, MEM_HD), lambda h, i: (0, h))],
        out_specs=pl.BlockSpec((tq, MEM_HD), lambda h, i: (i, h)),
        out_shape=jax.ShapeDtypeStruct((S, D), BF16), name=name, compiler_params=_cparams(),
    )(q, k, v)


def _mem_bwd_call(q, k, v, do, *, name, tq=512):
    S = q.shape[0]
    n = S // tq
    scale = MEM_HD ** -0.5

    def body(q_ref, k_ref, v_ref, do_ref, dq_ref, dk_ref, dv_ref, dka_ref, dva_ref):
        i = pl.program_id(1)

        @pl.when(i == 0)
        def _():
            dka_ref[...] = jnp.zeros_like(dka_ref)
            dva_ref[...] = jnp.zeros_like(dva_ref)
        qb = q_ref[...]
        kb = k_ref[...]
        dob = do_ref[...]
        s = _dot_nt(qb, kb) * scale
        p = jnp.exp(s - jnp.max(s, axis=1, keepdims=True))
        p = p / jnp.sum(p, axis=1, keepdims=True)
        dp = _dot_nt(dob, v_ref[...])
        ds = p * (dp - jnp.sum(p * dp, axis=1, keepdims=True))
        dsb = (ds * scale).astype(BF16)
        dq_ref[...] = _dot(dsb, kb).astype(BF16)
        dka_ref[...] += _dot_tn(dsb, qb)
        dva_ref[...] += _dot_tn(p.astype(BF16), dob)

        @pl.when(i == n - 1)
        def _():
            dk_ref[...] = dka_ref[...].astype(BF16)
            dv_ref[...] = dva_ref[...].astype(BF16)

    kv = pl.BlockSpec((N_MEM, MEM_HD), lambda h, i: (0, h))
    qs = pl.BlockSpec((tq, MEM_HD), lambda h, i: (i, h))
    return pl.pallas_call(
        body, grid=(MEM_HEADS, n), in_specs=[qs, kv, kv, qs], out_specs=(qs, kv, kv),
        out_shape=(jax.ShapeDtypeStruct((S, D), BF16), jax.ShapeDtypeStruct((N_MEM, D), BF16),
                   jax.ShapeDtypeStruct((N_MEM, D), BF16)),
        scratch_shapes=[pltpu.VMEM((N_MEM, MEM_HD), F32), pltpu.VMEM((N_MEM, MEM_HD), F32)],
        name=name, compiler_params=_cparams(),
    )(q, k, v, do)


def _rel_table_to_g(rel):
    return jnp.concatenate([
        jnp.broadcast_to(rel[:, N_REL - 1:N_REL], (8, 640)),
        rel[:, 1:N_REL - 1][:, ::-1],
        jnp.broadcast_to(rel[:, 0:1], (8, 129)),
    ], axis=1)


def _g_to_rel_table(dg):
    return dg[:, 639:896][:, ::-1]


def _place():
    x, y, c = lax.axis_index("x"), lax.axis_index("y"), lax.axis_index("c")
    others = [(1 - x, y), (x, 1 - y), (1 - x, 1 - y)]
    return x, y, c, others


def _half(c, rows):
    hr = rows // 2
    return pl.ds(pl.multiple_of(c * hr, 16), hr)


def _dma_sems(*shape):
    return pltpu.SemaphoreType.DMA(shape)


def _cast_slab_call(w, chip_arr, *, name, tm=256, pad_rows=0):
    rows, cols = w.shape
    if pad_rows:
        tm = rows
    tm = min(tm, rows)

    def body(chip_ref, w_ref, o_ref):
        o_ref[0:tm, :] = w_ref[...].astype(BF16)
        if pad_rows:
            o_ref[tm:tm + pad_rows, :] = jnp.zeros((pad_rows, cols), BF16)

    return pl.pallas_call(
        body,
        grid_spec=pltpu.PrefetchScalarGridSpec(
            num_scalar_prefetch=1, grid=(rows // tm,),
            in_specs=[pl.BlockSpec((tm, cols), lambda i, chip: (i, 0))],
            out_specs=pl.BlockSpec((None, tm + pad_rows, cols), lambda i, chip: (chip[0], i, 0))),
        out_shape=jax.ShapeDtypeStruct((N_CHIP, rows + pad_rows, cols), BF16), name=name,
        compiler_params=_cparams(),
    )(chip_arr, w)


def _ag_ici_task(gathered):
    n = len(gathered)

    def copies(ins, outs, sems):
        send_sems, recv_sems = sems
        x, y, c, others = _place()
        me = 2 * x + y
        for k in range(n):
            mine = _half(c, gathered[k].shape[1])
            for t, (ox, oy) in enumerate(others):
                yield [pltpu.make_async_remote_copy(
                    src_ref=ins[k].at[me, mine], dst_ref=outs[k].at[slab, mine],
                    send_sem=send_sems.at[k, t], recv_sem=recv_sems.at[k, t],
                    device_id=(ox, oy, c), device_id_type=MESH) for slab in (me, 2 * ox + oy)]

    def issue(ins, outs, sems):
        for outgoing, _ in copies(ins, outs, sems):
            outgoing.start()

    def drain(ins, outs, sems):
        for outgoing, incoming in copies(ins, outs, sems):
            incoming.wait_recv()
            outgoing.wait_send()

    return _Task(gathered, [jax.ShapeDtypeStruct(g.shape, g.dtype) for g in gathered],
                 [_dma_sems(n, 3), _dma_sems(n, 3)], issue, drain, aliases={k: k for k in range(n)})


def _ag_d2d_task(gathered):
    n = len(gathered)

    def copies(ins, outs, sems):
        send_sems, recv_sems = sems
        x, y, c, others = _place()
        for k in range(n):
            rows = gathered[k].shape[1]
            mine, theirs = _half(c, rows), _half(1 - c, rows)
            for t, (ox, oy) in enumerate(others):
                slab = 2 * ox + oy
                pair = [pltpu.make_async_remote_copy(
                    src_ref=ins[k].at[slab, half], dst_ref=outs[k].at[slab, half],
                    send_sem=send_sems.at[k, t], recv_sem=recv_sems.at[k, t],
                    device_id=(x, y, 1 - c), device_id_type=MESH) for half in (mine, theirs)]
                yield pair

    def issue(ins, outs, sems):
        for outgoing, _ in copies(ins, outs, sems):
            outgoing.start()

    def drain(ins, outs, sems):
        for outgoing, incoming in copies(ins, outs, sems):
            incoming.wait_recv()
            outgoing.wait_send()

    return _Task(gathered, [jax.ShapeDtypeStruct(g.shape, g.dtype) for g in gathered],
                 [_dma_sems(n, 3), _dma_sems(n, 3)], issue, drain, aliases={k: k for k in range(n)})


def _rs_pair_task(ds):
    n = len(ds)

    def copies(ins, outs, sems):
        send_sems, recv_sems = sems
        x, y, c, _ = _place()
        for k in range(n):
            yield pltpu.make_async_remote_copy(
                src_ref=ins[k].at[:, _half(1 - c, ds[k].shape[1])], dst_ref=outs[k],
                send_sem=send_sems.at[k], recv_sem=recv_sems.at[k],
                device_id=(x, y, 1 - c), device_id_type=MESH)

    def issue(ins, outs, sems):
        for cp in copies(ins, outs, sems):
            cp.start()

    def drain(ins, outs, sems):
        for cp in copies(ins, outs, sems):
            cp.wait()

    return _Task(ds, [jax.ShapeDtypeStruct((N_CHIP, d.shape[1] // 2, d.shape[2]), d.dtype) for d in ds],
                 [_dma_sems(n), _dma_sems(n)], issue, drain)


def _pair_add_call(d, r1, c_arr, *, name, tm=256):
    _, rows, cols = d.shape
    hr = rows // 2
    tm = tm if hr % tm == 0 else hr
    nb = hr // tm

    def body(c_ref, d_ref, r_ref, o_ref):
        o_ref[...] = (d_ref[...].astype(F32) + r_ref[...].astype(F32)).astype(BF16)

    return pl.pallas_call(
        body,
        grid_spec=pltpu.PrefetchScalarGridSpec(
            num_scalar_prefetch=1, grid=(N_CHIP, nb),
            in_specs=[pl.BlockSpec((None, tm, cols), lambda j, i, c: (j, c[0] * nb + i, 0)),
                      pl.BlockSpec((None, tm, cols), lambda j, i, c: (j, i, 0))],
            out_specs=pl.BlockSpec((None, tm, cols), lambda j, i, c: (j, i, 0))),
        out_shape=jax.ShapeDtypeStruct((N_CHIP, hr, cols), BF16), name=name, compiler_params=_cparams(),
    )(c_arr, d, r1)


def _rs_chip_task(ps):
    n = len(ps)

    def copies(ins, outs, sems):
        send_sems, recv_sems = sems
        x, y, c, others = _place()
        for k in range(n):
            for t, (ox, oy) in enumerate(others):
                yield pltpu.make_async_remote_copy(
                    src_ref=ins[k].at[2 * ox + oy], dst_ref=outs[k].at[t],
                    send_sem=send_sems.at[k, t], recv_sem=recv_sems.at[k, t],
                    device_id=(ox, oy, c), device_id_type=MESH)

    def issue(ins, outs, sems):
        for cp in copies(ins, outs, sems):
            cp.start()

    def drain(ins, outs, sems):
        for cp in copies(ins, outs, sems):
            cp.wait()

    return _Task(ps, [jax.ShapeDtypeStruct((3,) + p.shape[1:], p.dtype) for p in ps],
                 [_dma_sems(n, 3), _dma_sems(n, 3)], issue, drain)


def _chip_sum_call(p, r2, place_arr, *, name, tm=256):
    _, hr, cols = r2.shape
    tm = tm if hr % tm == 0 else hr
    nb = hr // tm

    def body(place_ref, p_ref, r_ref, o_ref):
        acc = p_ref[...].astype(F32)
        for j in range(3):
            acc = acc + r_ref[j].astype(F32)
        o_ref[...] = acc

    return pl.pallas_call(
        body,
        grid_spec=pltpu.PrefetchScalarGridSpec(
            num_scalar_prefetch=1, grid=(nb,),
            in_specs=[pl.BlockSpec((None, tm, cols), lambda i, pc: (pc[0], i, 0)),
                      pl.BlockSpec((3, tm, cols), lambda i, pc: (0, i, 0))],
            out_specs=pl.BlockSpec((tm, cols), lambda i, pc: (pc[1] * nb + i, 0))),
        out_shape=jax.ShapeDtypeStruct((2 * hr, cols), F32), name=name, compiler_params=_cparams(),
    )(place_arr, p, r2)


def _rs_gather_task(gs):
    n = len(gs)

    def copies(ins, outs, sems):
        send_sems, recv_sems = sems
        x, y, c, _ = _place()
        for k in range(n):
            rows = gs[k].shape[0]
            mine, theirs = _half(c, rows), _half(1 - c, rows)
            yield [pltpu.make_async_remote_copy(
                src_ref=ins[k].at[mine], dst_ref=outs[k].at[half],
                send_sem=send_sems.at[k], recv_sem=recv_sems.at[k],
                device_id=(x, y, 1 - c), device_id_type=MESH) for half in (mine, theirs)]

    def issue(ins, outs, sems):
        for outgoing, _ in copies(ins, outs, sems):
            outgoing.start()

    def drain(ins, outs, sems):
        for outgoing, incoming in copies(ins, outs, sems):
            incoming.wait_recv()
            outgoing.wait_send()

    return _Task(gs, [jax.ShapeDtypeStruct(g.shape, g.dtype) for g in gs],
                 [_dma_sems(n), _dma_sems(n)], issue, drain, aliases={k: k for k in range(n)})


def _adamw(w, g, m, v):
    m = ADAM_B1 * m + (1.0 - ADAM_B1) * g
    v = ADAM_B2 * v + (1.0 - ADAM_B2) * jnp.square(g)
    m_hat = m / (1.0 - ADAM_B1 ** ADAM_STEP)
    v_hat = v / (1.0 - ADAM_B2 ** ADAM_STEP)
    delta = -ADAM_LR * (m_hat / (jnp.sqrt(v_hat) + ADAM_EPS) + ADAM_WD * w)
    return delta, m, v


def _adamw_call(items, *, name, tm=256, task=None):
    n = len(items)
    cols = items[0][0].shape[1]
    tiles = [it[0].shape[0] // tm for it in items]
    steps = max(tiles)

    def body(*refs):
        i = pl.program_id(0)
        ins, outs = refs[:4 * n], refs[4 * n:]
        for k in range(n):
            def update(k=k):
                g = ins[4 * k + 1][...]
                res = _adamw(ins[4 * k][...], g, ins[4 * k + 2][...], ins[4 * k + 3][...])
                outs[4 * k][...] = g
                for j in range(3):
                    outs[4 * k + 1 + j][...] = res[j]
            if tiles[k] == steps:
                update()
            else:
                pl.when(i < tiles[k])(update)

    in_specs, out_specs, out_shape, args = [], [], [], []
    for it, t in zip(items, tiles):
        spec = pl.BlockSpec((tm, cols), lambda i, t=t: (jnp.minimum(i, t - 1), 0))
        in_specs += [spec] * 4
        out_specs += [spec] * 4
        out_shape += [jax.ShapeDtypeStruct(it[0].shape, F32)] * 4
        args += list(it)
    res = _pallas(body, grid=(steps,), in_specs=in_specs, out_specs=out_specs, out_shape=out_shape,
                  name=name, task=task)(*args)
    outs, extra = res if task is not None else (res, None)
    grouped = [tuple(outs[4 * k:4 * k + 4]) for k in range(n)]
    return (grouped, extra) if task is not None else grouped


def _adamw_cols_call(w, g_pad, m, v, *, name, tn=256):
    rows, cols = w.shape

    def body(w_ref, g_ref, m_ref, v_ref, go_ref, d_ref, mo_ref, vo_ref):
        g = g_ref[0:rows, :]
        d, mn, vn = _adamw(w_ref[...], g, m_ref[...], v_ref[...])
        go_ref[...] = g
        d_ref[...] = d
        mo_ref[...] = mn
        vo_ref[...] = vn

    spec = pl.BlockSpec((rows, tn), lambda j: (0, j))
    gspec = pl.BlockSpec((g_pad.shape[0], tn), lambda j: (0, j))
    return _pallas(body, grid=(cols // tn,), in_specs=[spec, gspec, spec, spec], out_specs=(spec,) * 4,
                   out_shape=(jax.ShapeDtypeStruct((rows, cols), F32),) * 4, name=name)(w, g_pad, m, v)


N_DEV = 8
SMALL_ROWS = 24
SMALL_LAYOUT = {
    "g_mix_pre": (0, 0, 1, D), "g_mix_post": (1, 0, 1, D), "g_mem_kv": (2, 0, 1, D), "g_mem_pre": (3, 0, 1, D),
    "g_mem_post": (4, 0, 1, D), "g_ff_pre": (5, 0, 1, D), "g_ff_post": (6, 0, 1, D),
    "g_fox_out": (7, 0, 1, D_GRP), "g_chk_out": (7, D_GRP, 1, D_GRP), "b_fgt": (8, 0, 1, 8),
    "rel_bias": (16, 0, 8, N_REL),
}
SMALL = list(SMALL_LAYOUT)


LOSS_ROW = 9


def _small_call(grads, ws, ms, vs, loss_blk, task, *, name):
    n = len(SMALL)
    t_in, t_out = len(task.arrays), len(task.out_shapes)

    def body(*refs):
        g_refs, w_refs, m_refs, v_refs = (refs[j * n:(j + 1) * n] for j in range(4))
        p = 4 * n
        loss_ref, tins = refs[p], refs[p + 1:p + 1 + t_in]
        p += 1 + t_in
        outs, loss_out, touts = refs[p:p + 4 * n], refs[p + 4 * n], refs[p + 4 * n + 1:p + 4 * n + 1 + t_out]
        p += 4 * n + 1 + t_out
        mine, slots, send_sems, recv_sems = refs[p:p + 4]
        tsems = refs[p + 4:]
        task.issue(tins, touts, tsems)
        x, y, c, _ = _place()
        me = 4 * x + 2 * y + c
        mine[...] = jnp.zeros_like(mine)
        for k, name_k in enumerate(SMALL):
            r, l, nr, nl = SMALL_LAYOUT[name_k]
            mine[r:r + nr, l:l + nl] = g_refs[k][0:nr, 0:nl]
        mine[LOSS_ROW:LOSS_ROW + 1, 0:128] = loss_ref[0:1, :]
        slots[me] = mine[...]
        peers = [(dx, dy, dc) for dx in (0, 1) for dy in (0, 1) for dc in (0, 1)][1:]
        cps = []
        for t, (dx, dy, dc) in enumerate(peers):
            px, py, pc = (x + dx) % 2, (y + dy) % 2, (c + dc) % 2
            cps.append(pltpu.make_async_remote_copy(
                src_ref=mine, dst_ref=slots.at[me], send_sem=send_sems.at[t], recv_sem=recv_sems.at[t],
                device_id=(px, py, pc), device_id_type=MESH))
            cps[-1].start()
        for t, (dx, dy, dc) in enumerate(peers):
            px, py, pc = (x + dx) % 2, (y + dy) % 2, (c + dc) % 2
            pltpu.make_async_remote_copy(
                src_ref=mine, dst_ref=slots.at[4 * px + 2 * py + pc], send_sem=send_sems.at[t],
                recv_sem=recv_sems.at[t], device_id=(px, py, pc), device_id_type=MESH).wait_recv()
        for cp in cps:
            cp.wait_send()
        total = slots[0]
        for j in range(1, N_DEV):
            total = total + slots[j]
        for k, name_k in enumerate(SMALL):
            r, l, nr, nl = SMALL_LAYOUT[name_k]
            g = total[r:r + nr, l:l + nl]
            d, mn, vn = _adamw(w_refs[k][...], g, m_refs[k][...], v_refs[k][...])
            for j, val in enumerate((g, d, mn, vn)):
                outs[4 * k + j][...] = val
        loss_out[...] = jnp.broadcast_to(total[LOSS_ROW:LOSS_ROW + 1, 0:128], loss_out.shape)
        task.drain(tins, touts, tsems)

    vm = pl.BlockSpec(memory_space=pltpu.VMEM)
    out_shape = [jax.ShapeDtypeStruct(ws[k].shape, F32) for k in SMALL for _ in range(4)]
    out_shape += [jax.ShapeDtypeStruct((8, 128), F32)] + list(task.out_shapes)
    res = pl.pallas_call(
        body, in_specs=[vm] * (4 * n + 1) + [ANY] * t_in, out_specs=[vm] * (4 * n + 1) + [ANY] * t_out,
        out_shape=out_shape,
        scratch_shapes=[pltpu.VMEM((SMALL_ROWS, D), F32), pltpu.VMEM((N_DEV, SMALL_ROWS, D), F32),
                        _dma_sems(N_DEV - 1), _dma_sems(N_DEV - 1)] + list(task.sems),
        input_output_aliases={4 * n + 1 + i: 4 * n + 1 + j for i, j in task.aliases.items()},
        name=name,
    )(*[d[k] for d in (grads, ws, ms, vs) for k in SMALL], loss_blk, *task.arrays)
    return ({k: tuple(res[4 * i:4 * i + 4]) for i, k in enumerate(SMALL)}, res[4 * n], list(res[4 * n + 1:]))


WEIGHTS = ["w_in", "b_fgt", "rel_bias", "g_fox_out", "g_chk_out", "w_out", "g_mix_pre", "g_mix_post", "g_mem_kv",
           "w_mq", "w_mk", "w_mv", "w_mo", "g_mem_pre", "g_mem_post", "w_ff1", "w_ff2", "g_ff_pre", "g_ff_post"]
BIG = ["w_in", "w_out", "w_mq", "w_mk", "w_mv", "w_mo", "w_ff1", "w_ff2"]


IN_SHARD = D_IN // N_CHIP
IN_PAD = 800


IN_PIECES = [(0, 0, 770), (800, 770, 766), (1566, 3072, 4), (1600, 3076, 4), (1604, 1536, 766), (2400, 2302, 770)]
PAD_ZEROS = [(800 * j + IN_SHARD, IN_PAD - IN_SHARD) for j in range(N_CHIP)]
ALL_ZEROS = [(D_IN, D_ALL - D_IN)]


def _reorder_rows_call(src, to_all, *, name, tn=256):
    rows, cols = src.shape
    zeros = ALL_ZEROS if to_all else PAD_ZEROS

    def body(s_ref, o_ref):
        for pad0, all0, cnt in IN_PIECES:
            s0, d0 = (pad0, all0) if to_all else (all0, pad0)
            o_ref[d0:d0 + cnt, :] = s_ref[s0:s0 + cnt, :]
        for z0, cnt in zeros:
            o_ref[z0:z0 + cnt, :] = jnp.zeros((cnt, tn), src.dtype)

    spec = pl.BlockSpec((rows, tn), lambda j: (0, j))
    return _pallas(body, grid=(cols // tn,), in_specs=[spec], out_specs=spec,
                   out_shape=jax.ShapeDtypeStruct((rows, cols), src.dtype), name=name)(src)


def kernel(x, mem, w_in, b_fgt, rel_bias, g_fox_out, g_chk_out, w_out, g_mix_pre, g_mix_post, g_mem_kv, w_mq, w_mk, w_mv, w_mo, g_mem_pre, g_mem_post, w_ff1, w_ff2, g_ff_pre, g_ff_post, loss_target, m_w_in, m_b_fgt, m_rel_bias, m_g_fox_out, m_g_chk_out, m_w_out, m_g_mix_pre, m_g_mix_post, m_g_mem_kv, m_w_mq, m_w_mk, m_w_mv, m_w_mo, m_g_mem_pre, m_g_mem_post, m_w_ff1, m_w_ff2, m_g_ff_pre, m_g_ff_post, v_w_in, v_b_fgt, v_rel_bias, v_g_fox_out, v_g_chk_out, v_w_out, v_g_mix_pre, v_g_mix_post, v_g_mem_kv, v_w_mq, v_w_mk, v_w_mv, v_w_mo, v_g_mem_pre, v_g_mem_post, v_w_ff1, v_w_ff2, v_g_ff_pre, v_g_ff_post):
    w = dict(w_in=w_in, b_fgt=b_fgt, rel_bias=rel_bias, g_fox_out=g_fox_out, g_chk_out=g_chk_out, w_out=w_out,
             g_mix_pre=g_mix_pre, g_mix_post=g_mix_post, g_mem_kv=g_mem_kv, w_mq=w_mq, w_mk=w_mk, w_mv=w_mv,
             w_mo=w_mo, g_mem_pre=g_mem_pre, g_mem_post=g_mem_post, w_ff1=w_ff1, w_ff2=w_ff2, g_ff_pre=g_ff_pre,
             g_ff_post=g_ff_post)
    m = dict(w_in=m_w_in, b_fgt=m_b_fgt, rel_bias=m_rel_bias, g_fox_out=m_g_fox_out, g_chk_out=m_g_chk_out,
             w_out=m_w_out, g_mix_pre=m_g_mix_pre, g_mix_post=m_g_mix_post, g_mem_kv=m_g_mem_kv, w_mq=m_w_mq,
             w_mk=m_w_mk, w_mv=m_w_mv, w_mo=m_w_mo, g_mem_pre=m_g_mem_pre, g_mem_post=m_g_mem_post,
             w_ff1=m_w_ff1, w_ff2=m_w_ff2, g_ff_pre=m_g_ff_pre, g_ff_post=m_g_ff_post)
    v = dict(w_in=v_w_in, b_fgt=v_b_fgt, rel_bias=v_rel_bias, g_fox_out=v_g_fox_out, g_chk_out=v_g_chk_out,
             w_out=v_w_out, g_mix_pre=v_g_mix_pre, g_mix_post=v_g_mix_post, g_mem_kv=v_g_mem_kv, w_mq=v_w_mq,
             w_mk=v_w_mk, w_mv=v_w_mv, w_mo=v_w_mo, g_mem_pre=v_g_mem_pre, g_mem_post=v_g_mem_post,
             w_ff1=v_w_ff1, w_ff2=v_w_ff2, g_ff_pre=v_g_ff_pre, g_ff_post=v_g_ff_post)

    def rows(d, k):
        return d[k][0] if k == "rel_bias" else d[k]

    xs, mems, target = x[0], mem[0], loss_target[0]
    S = xs.shape[0]
    sp = {k: rows(w, k) for k in SMALL}
    b_pad = jnp.pad(sp["b_fgt"], ((0, 0), (0, 120)))
    chip = 2 * lax.axis_index("x") + lax.axis_index("y")
    chip_arr = jnp.reshape(chip, (1,)).astype(jnp.int32)
    c_arr = jnp.reshape(lax.axis_index("c"), (1,)).astype(jnp.int32)
    place_arr = jnp.concatenate([chip_arr, c_arr])
    w_in_t, m_in_t, v_in_t = w["w_in"][0].T, m["w_in"][0].T, v["w_in"][0].T
    slab = {k: _cast_slab_call(w[k][0], chip_arr, name="cast_" + k) for k in BIG[1:]}
    slab["w_in"] = _cast_slab_call(w_in_t, chip_arr, name="cast_w_in", pad_rows=IN_PAD - IN_SHARD)

    def gather_ici(names):
        return _ag_ici_task([slab[k] for k in names])

    def pair_add(k, d, r1):
        return _pair_add_call(d, r1, c_arr, name="rs_pair_add_" + k)

    g_in, = _comm_call(gather_ici(["w_in"]), name="ag_w_in")
    h1, (g_in,) = _rms_fwd_call(xs, sp["g_mix_pre"], name="rms_mix_pre", task=_ag_d2d_task([g_in]))
    w_all_t = _reorder_rows_call(g_in.reshape(N_CHIP * IN_PAD, D), True, name="w_in_rows")
    proj, (g_out,) = _mm_nt(h1, w_all_t, "plain", rows=(0, 3072), name="mm_proj", task=gather_ici(["w_out"]))
    fl_raw = _mm_nt(h1, w_all_t, "plain", rows=(3072, 128), name="mm_gate", out_dtype=F32, tn=128)
    c_rep, c_t = _fox_prep_call(fl_raw, b_pad, name="fox_prep")
    bias = _chk_bias_call(_rel_table_to_g(sp["rel_bias"]), name="chk_bias")
    mid = ["w_mq", "w_mk", "w_mv", "w_mo", "w_ff1"]
    (yf, lse), got = _fox_fwd_call(proj, c_rep, c_t, name="fox_fwd",
                                   task=_merge_tasks([gather_ici(mid), _ag_d2d_task([g_out])]))
    g_mid, g_out = got[:5], got[5]
    yc, got = _chk_fwd_call(proj, bias, name="chk_fwd",
                            task=_merge_tasks([gather_ici(["w_ff2"]), _ag_d2d_task(g_mid)]))
    g_ff2, (g_mq, g_mk, g_mv, g_mo, g_ff1) = got[0], got[1:]
    yn = _mix_norm_fwd_call(yf, yc, sp["g_fox_out"], sp["g_chk_out"], name="mix_norm_fwd")
    z, (g_ff2,) = _mm_nn(yn, g_out, "rows", name="mm_out", out_dtype=F32, task=_ag_d2d_task([g_ff2]))
    x1, h2 = _post_pre_call(xs, z, sp["g_mix_post"], sp["g_mem_pre"], name="post_mix")
    memn = _rms_fwd_call(mems, sp["g_mem_kv"], name="rms_mem_kv")
    q2 = _mm_nn(h2, g_mq, "rows", name="mm_mq")
    k2 = _mm_nn(memn, g_mk, "rows", name="mm_mk")
    v2 = _mm_nn(memn, g_mv, "rows", name="mm_mv")
    o2 = _mem_fwd_call(q2, k2, v2, name="mem_fwd")
    y2 = _mm_nn(o2, g_mo, "rows", name="mm_mo", out_dtype=F32)
    x2, h3 = _post_pre_call(x1, y2, sp["g_mem_post"], sp["g_ff_pre"], name="post_mem")
    act, relu = _mm_nn(h3, g_ff1, "cols", name="mm_ff1", epi="relu2")
    y3 = _mm_nn(act, g_ff2, "rows", name="mm_ff2", out_dtype=F32, tm=1024)
    loss_blk, dx3, dy3, dg_ff_post = _final_call(x2, y3, sp["g_ff_post"], target, name="final")

    d_ff2 = _mm_tn(act, dy3, name="mm_dff2", tk=512, tn=1024).reshape(N_CHIP, D_FF // N_CHIP, D)
    du, (r1,) = _mm_nt(dy3, g_ff2, "rows", name="mm_du", mul2r=relu, task=_rs_pair_task([d_ff2]))
    p_ff2 = pair_add("w_ff2", d_ff2, r1)
    d_ff1 = _mm_tn(h3, du, name="mm_dff1", cols4=True)
    dh3, (r1,) = _mm_nt(du, g_ff1, "cols", name="mm_dh3", out_dtype=F32, tm=1024, task=_rs_pair_task([d_ff1]))
    p_ff1 = pair_add("w_ff1", d_ff1, r1)
    dx2, dy2, dg_ff_pre, dg_mem_post = _bwd_mid_call(dx3, x2, dh3, sp["g_ff_pre"], y2, sp["g_mem_post"], name="bwd_ff")
    d_mo = _mm_tn(o2, dy2, name="mm_dmo").reshape(N_CHIP, D // N_CHIP, D)
    do2 = _mm_nt(dy2, g_mo, "rows", name="mm_do2")
    dq2, dk2, dv2 = _mem_bwd_call(q2, k2, v2, do2, name="mem_bwd")
    d_mq = _mm_tn(h2, dq2, name="mm_dmq").reshape(N_CHIP, D // N_CHIP, D)
    dh2 = _mm_nt(dq2, g_mq, "rows", name="mm_dh2", out_dtype=F32)
    d_mk = _mm_tn(memn, dk2, name="mm_dmk").reshape(N_CHIP, D // N_CHIP, D)
    d_mv = _mm_tn(memn, dv2, name="mm_dmv").reshape(N_CHIP, D // N_CHIP, D)
    dmn_k = _mm_nt(dk2, g_mk, "rows", name="mm_dmemk", out_dtype=F32)
    dmn_v = _mm_nt(dv2, g_mv, "rows", name="mm_dmemv", out_dtype=F32)
    dg_mem_kv = _gain_grad_call(mems, sp["g_mem_kv"], dmn_k, dmn_v, name="gain_mem_kv")
    dx1, dz, dg_mem_pre, dg_mix_post = _bwd_mid_call(dx2, x1, dh2, sp["g_mem_pre"], z, sp["g_mix_post"], name="bwd_mem")
    d_out = _mm_tn(yn, dz, name="mm_dout").reshape(N_CHIP, D // N_CHIP, D)
    late = ["w_mo", "w_mq", "w_mk", "w_mv", "w_out"]
    d_late = [d_mo, d_mq, d_mk, d_mv, d_out]
    dyn, r1_late = _mm_nt(dz, g_out, "rows", name="mm_dyn", out_dtype=F32, task=_rs_pair_task(d_late))
    p_late = [pair_add(k, d, r1) for k, d, r1 in zip(late, d_late, r1_late)]
    dof, doc, delta, dg_fox, dg_chk = _mix_norm_bwd_call(dyn, yf, yc, sp["g_fox_out"], sp["g_chk_out"], name="mix_norm_bwd")
    (dqf, dkf, dvf, dct, dcq), r2_ff = _fox_bwd_call(proj, dof, lse, delta, c_rep, c_t, name="fox_bwd",
                                                      task=_rs_chip_task([p_ff2, p_ff1]))
    (dqc, dkc, dvc, dgrev), r2_late = _chk_bwd_call(proj, doc, bias, name="chk_bwd", task=_rs_chip_task(p_late))
    first = ["w_ff2", "w_ff1"] + late
    f_first = [_chip_sum_call(p, r, place_arr, name="rs_chip_sum_" + k)
               for k, p, r in zip(first, [p_ff2, p_ff1] + p_late, r2_ff + r2_late)]
    dc8 = dct[:, :, 0:2, :].transpose(0, 2, 1, 3).reshape(8, S) + dcq[:, ::HEAD].T
    dc_rows = jnp.concatenate([dc8, jnp.zeros((120, S), F32)], axis=0)
    dfl, db_fgt = _fox_gate_bwd_call(dc_rows, fl_raw, b_pad, name="fox_gate_bwd")
    dproj = jnp.concatenate([dqf, dkf, dvf, dqc, dkc, dvc, dfl], axis=1)
    d_all_t, g_first = _mm_tn(dproj, h1, name="mm_dwin", tk=640, tn=1024, task=_rs_gather_task(f_first))
    grads = dict(zip(first, g_first))
    d_in = _reorder_rows_call(d_all_t, False, name="d_in_rows").reshape(N_CHIP, IN_PAD, D)
    dh1, (r1,) = _mm_nn(dproj, w_all_t, "plain", name="mm_dh1", out_dtype=F32, tm=1024,
                        task=_rs_pair_task([d_in]))
    p_in = pair_add("w_in", d_in, r1)
    delta_w, new_m, new_v = {}, {}, {}
    upd, (r2_in,) = _adamw_call([(w[k][0], grads[k], m[k][0], v[k][0]) for k in first], name="adamw_7", tm=64,
                                task=_rs_chip_task([p_in]))
    for k, res in zip(first, upd):
        grads[k], delta_w[k], new_m[k], new_v[k] = res
    f_in = _chip_sum_call(p_in, r2_in, place_arr, name="rs_chip_sum_w_in")
    grad_x, dg_mix_pre = _bwd_last_call(dx1, xs, dh1, sp["g_mix_pre"], name="bwd_mix")

    small_g = {"g_mix_pre": dg_mix_pre, "g_mix_post": dg_mix_post, "g_mem_kv": dg_mem_kv, "g_mem_pre": dg_mem_pre,
               "g_mem_post": dg_mem_post, "g_ff_pre": dg_ff_pre, "g_ff_post": dg_ff_post, "g_fox_out": dg_fox,
               "g_chk_out": dg_chk, "b_fgt": db_fgt,
               "rel_bias": _g_to_rel_table(dgrev[:, 0:2, :].reshape(8, ROLL_W))}
    small, loss_out, (g_w_in,) = _small_call(
        small_g, sp, {k: rows(m, k) for k in SMALL}, {k: rows(v, k) for k in SMALL}, loss_blk,
        _rs_gather_task([f_in]), name="small_allreduce_adamw")
    loss = loss_out[0, 0]
    res = _adamw_cols_call(w_in_t, g_w_in, m_in_t, v_in_t, name="adamw_w_in")
    grads["w_in"], delta_w["w_in"], new_m["w_in"], new_v["w_in"] = (a.T for a in res)
    for k in SMALL:
        vals = small[k]
        if k == "rel_bias":
            vals = tuple(a[None] for a in vals)
        grads[k], delta_w[k], new_m[k], new_v[k] = vals

    def out(d, k):
        return d[k][None] if k in BIG else d[k]

    return (loss, grad_x[None], *[out(grads, k) for k in WEIGHTS], *[out(delta_w, k) for k in WEIGHTS],
            *[out(new_m, k) for k in WEIGHTS], *[out(new_v, k) for k in WEIGHTS])
```

```python
import functools

import jax
import jax.numpy as jnp
from jax import lax
from jax.experimental import pallas as pl
from jax.experimental.pallas import tpu as pltpu

F32 = jnp.float32
BF16 = jnp.bfloat16

D = 1024
HEAD = 64
N_PAIR = 4
D_GRP = 512
CHUNK = 64
LEFT = 8
MAX_REL = 128
N_REL = 2 * MAX_REL + 1
N_MEM = 256
MEM_HEADS = 4
MEM_HD = 256
D_FF = 4096
D_IN = 3080
D_ALL = 3200
EPS = 1e-6
TQ = 256
WIN = (LEFT + TQ // CHUNK) * CHUNK
PADK = LEFT * CHUNK
ROLL_W = 1024
NEG = -1e30
N_CHIP = 4
VMEM_LIMIT = 48 * 1024 * 1024

ADAM_LR = 0.001
ADAM_B1 = 0.9
ADAM_B2 = 0.999
ADAM_EPS = 1e-08
ADAM_WD = 0.01
ADAM_STEP = 10

MESH = pl.DeviceIdType.MESH


def _cparams():
    return pltpu.CompilerParams(vmem_limit_bytes=VMEM_LIMIT)


ANY = pl.BlockSpec(memory_space=pl.ANY)


def _hbm(x):
    return pltpu.with_memory_space_constraint(x, pltpu.HBM)


class _Task:
    def __init__(self, arrays, out_shapes, sems, issue, drain, aliases=None):
        self.arrays, self.out_shapes, self.sems = list(arrays), list(out_shapes), list(sems)
        self.issue, self.drain, self.aliases = issue, drain, dict(aliases or {})


def _merge_tasks(tasks):
    tasks = [t for t in tasks if t is not None]
    if len(tasks) == 1:
        return tasks[0]
    cuts, a, o, s = [], 0, 0, 0
    aliases = {}
    for t in tasks:
        cuts.append((a, o, s))
        aliases.update({a + i: o + j for i, j in t.aliases.items()})
        a, o, s = a + len(t.arrays), o + len(t.out_shapes), s + len(t.sems)

    def part(fn_name):
        def run(ins, outs, sems):
            for t, (a0, o0, s0) in zip(tasks, cuts):
                getattr(t, fn_name)(ins[a0:a0 + len(t.arrays)], outs[o0:o0 + len(t.out_shapes)],
                                    sems[s0:s0 + len(t.sems)])
        return run

    return _Task([x for t in tasks for x in t.arrays], [x for t in tasks for x in t.out_shapes],
                 [x for t in tasks for x in t.sems], part("issue"), part("drain"), aliases)


def _pallas(body, *, grid, in_specs, out_specs, out_shape, name, scratch_shapes=(), task=None):
    if task is None:
        plain = pl.pallas_call(body, grid=grid, in_specs=list(in_specs), out_specs=out_specs, out_shape=out_shape,
                               scratch_shapes=list(scratch_shapes), name=name, compiler_params=_cparams())
        return lambda *args: plain(*map(_hbm, args))
    single = not isinstance(out_shape, (tuple, list))
    o_shapes = [out_shape] if single else list(out_shape)
    o_specs = [out_specs] if single else list(out_specs)
    n_in, n_out, n_scr = len(in_specs), len(o_shapes), len(scratch_shapes)
    t_in, t_out = len(task.arrays), len(task.out_shapes)

    def carried(*refs):
        cut = [n_in, t_in, n_out, t_out, n_scr]
        parts, p = [], 0
        for c in cut:
            parts.append(refs[p:p + c])
            p += c
        ins, tins, outs, touts, scr = parts
        tsems = refs[p:]
        ids = [pl.program_id(a) for a in range(len(grid))]
        first = functools.reduce(jnp.logical_and, [i == 0 for i in ids])
        last = functools.reduce(jnp.logical_and, [i == g - 1 for i, g in zip(ids, grid)])

        @pl.when(first)
        def _():
            task.issue(tins, touts, tsems)
        body(*ins, *outs, *scr)

        @pl.when(last)
        def _():
            task.drain(tins, touts, tsems)

    call = pl.pallas_call(
        carried, grid=grid, in_specs=list(in_specs) + [ANY] * t_in, out_specs=o_specs + [ANY] * t_out,
        out_shape=o_shapes + list(task.out_shapes), scratch_shapes=list(scratch_shapes) + list(task.sems),
        input_output_aliases={n_in + i: n_out + j for i, j in task.aliases.items()},
        name=name, compiler_params=_cparams())

    def run(*args):
        res = call(*map(_hbm, args), *map(_hbm, task.arrays))
        outs = res[:n_out]
        return (outs[0] if single else tuple(outs)), list(res[n_out:])

    return run


def _comm_call(task, *, name):
    t_in, t_out = len(task.arrays), len(task.out_shapes)

    def body(*refs):
        tins, touts, tsems = refs[:t_in], refs[t_in:t_in + t_out], refs[t_in + t_out:]
        task.issue(tins, touts, tsems)
        task.drain(tins, touts, tsems)

    return pl.pallas_call(
        body, in_specs=[ANY] * t_in, out_specs=[ANY] * t_out, out_shape=list(task.out_shapes),
        scratch_shapes=list(task.sems), input_output_aliases=dict(task.aliases), name=name,
    )(*map(_hbm, task.arrays))


def _dot(a, b):
    return jnp.dot(a, b, preferred_element_type=F32)


def _dot_nt(a, b):
    return lax.dot_general(a, b, (((1,), (1,)), ((), ())), preferred_element_type=F32)


def _dot_tn(a, b):
    return lax.dot_general(a, b, (((0,), (0,)), ((), ())), preferred_element_type=F32)


def _split3(x):
    hi = x.astype(BF16)
    r1 = x - hi.astype(F32)
    mid = r1.astype(BF16)
    lo = (r1 - mid.astype(F32)).astype(BF16)
    return hi, mid, lo


def _dot3(x, m01):
    hi, mid, lo = _split3(x)
    return _dot(hi, m01) + _dot(mid, m01) + _dot(lo, m01)


def _dot3_l(m01, x):
    hi, mid, lo = _split3(x)
    return _dot(m01, hi) + _dot(m01, mid) + _dot(m01, lo)


def _mm_nn(a, b, kind, *, name, out_dtype=BF16, tm=2048, tn=512, epi=None, task=None):
    M, K = a.shape
    if kind == "plain":
        N = b.shape[1]
        b_spec = pl.BlockSpec((K, tn), lambda m, n: (0, n))
    elif kind == "rows":
        N = b.shape[2]
        b_spec = pl.BlockSpec((N_CHIP, K // N_CHIP, tn), lambda m, n: (0, 0, n))
    else:
        nq = b.shape[2]
        N = N_CHIP * nq
        per = nq // tn
        b_spec = pl.BlockSpec((None, K, tn), lambda m, n: (n // per, 0, n % per))
    tm = min(tm, M)
    kq = K // N_CHIP

    def body(a_ref, b_ref, *o_refs):
        if kind == "rows":
            acc = _dot(a_ref[:, 0:kq], b_ref[0])
            for j in range(1, N_CHIP):
                acc += _dot(a_ref[:, j * kq:(j + 1) * kq], b_ref[j])
        else:
            acc = _dot(a_ref[...], b_ref[...])
        if epi == "relu2":
            r = jnp.maximum(acc, 0.0)
            o_refs[0][...] = (r * r).astype(BF16)
            o_refs[1][...] = r.astype(BF16)
        else:
            o_refs[0][...] = acc.astype(out_dtype)

    o_spec = pl.BlockSpec((tm, tn), lambda m, n: (m, n))
    if epi == "relu2":
        out_shape = (jax.ShapeDtypeStruct((M, N), BF16), jax.ShapeDtypeStruct((M, N), BF16))
        out_specs = (o_spec, o_spec)
    else:
        out_shape = jax.ShapeDtypeStruct((M, N), out_dtype)
        out_specs = o_spec
    return _pallas(
        body, grid=(M // tm, N // tn),
        in_specs=[pl.BlockSpec((tm, K), lambda m, n: (m, 0)), b_spec],
        out_specs=out_specs, out_shape=out_shape, name=name, task=task,
    )(a, b)


def _mm_nt(a, b, kind, *, name, out_dtype=BF16, tm=2048, tn=512, mul2r=None, task=None, rows=None):
    M, K = a.shape
    if kind == "plain":
        first, N = rows if rows is not None else (0, b.shape[0])
        n0 = first // tn
        b_spec = pl.BlockSpec((tn, K), lambda m, n: (n0 + n, 0))
    elif kind == "rows":
        nq = b.shape[1]
        N = N_CHIP * nq
        tn = min(tn, nq)
        per = nq // tn
        b_spec = pl.BlockSpec((None, tn, K), lambda m, n: (n // per, n % per, 0))
    else:
        N = b.shape[1]
        b_spec = pl.BlockSpec((N_CHIP, tn, K // N_CHIP), lambda m, n: (0, n, 0))
    tm = min(tm, M)
    kq = K // N_CHIP

    def body(a_ref, b_ref, *rest):
        o_ref = rest[-1]
        if kind == "cols":
            acc = _dot_nt(a_ref[:, 0:kq], b_ref[0])
            for j in range(1, N_CHIP):
                acc += _dot_nt(a_ref[:, j * kq:(j + 1) * kq], b_ref[j])
        else:
            acc = _dot_nt(a_ref[...], b_ref[...])
        if mul2r is not None:
            acc = acc * (2.0 * rest[0][...].astype(F32))
        o_ref[...] = acc.astype(out_dtype)

    in_specs = [pl.BlockSpec((tm, K), lambda m, n: (m, 0)), b_spec]
    args = [a, b]
    if mul2r is not None:
        in_specs.append(pl.BlockSpec((tm, tn), lambda m, n: (m, n)))
        args.append(mul2r)
    return _pallas(
        body, grid=(M // tm, N // tn), in_specs=in_specs,
        out_specs=pl.BlockSpec((tm, tn), lambda m, n: (m, n)),
        out_shape=jax.ShapeDtypeStruct((M, N), out_dtype), name=name, task=task,
    )(*args)


def _mm_tn(a, b, *, name, out_dtype=BF16, tk=1024, tn=512, cols4=False, task=None):
    M, K1 = a.shape
    N = b.shape[1]
    tk = min(tk, K1)
    tn = min(tn, N)

    def body(a_ref, b_ref, o_ref):
        o_ref[...] = _dot_tn(a_ref[...], b_ref[...]).astype(out_dtype)

    if cols4:
        per = (N // N_CHIP) // tn
        out_shape = jax.ShapeDtypeStruct((N_CHIP, K1, N // N_CHIP), out_dtype)
        o_spec = pl.BlockSpec((None, tk, tn), lambda k, n: (n // per, k, n % per))
    else:
        out_shape = jax.ShapeDtypeStruct((K1, N), out_dtype)
        o_spec = pl.BlockSpec((tk, tn), lambda k, n: (k, n))
    return _pallas(
        body, grid=(K1 // tk, N // tn),
        in_specs=[pl.BlockSpec((M, tk), lambda k, n: (0, k)), pl.BlockSpec((M, tn), lambda k, n: (0, n))],
        out_specs=o_spec, out_shape=out_shape, name=name, task=task,
    )(a, b)


def _rms(x, g):
    r = lax.rsqrt(jnp.mean(x * x, axis=-1, keepdims=True) + EPS)
    return x * r * g


def _rms_bwd(x, g, dy):
    r = lax.rsqrt(jnp.mean(x * x, axis=-1, keepdims=True) + EPS)
    xh = x * r
    dg = jnp.sum(dy * xh, axis=0, keepdims=True)
    dxh = dy * g
    dx = r * (dxh - xh * jnp.mean(dxh * xh, axis=-1, keepdims=True))
    return dx, dg


def _row_spec(tm, n):
    return pl.BlockSpec((tm, n), lambda i: (i, 0))


def _vec_spec(n):
    return pl.BlockSpec((1, n), lambda i: (0, 0))


def _acc_spec(n):
    return pl.BlockSpec((8, n), lambda i: (0, 0))


def _acc_add(ref, row, i):
    @pl.when(i == 0)
    def _():
        ref[...] = jnp.zeros_like(ref)
    ref[0:1, :] += row


def _rms_fwd_call(x, g, *, name, tm=256, task=None):
    M, n = x.shape
    tm = min(tm, M)

    def body(x_ref, g_ref, h_ref):
        h_ref[...] = _rms(x_ref[...], g_ref[...]).astype(BF16)

    return _pallas(
        body, grid=(M // tm,), in_specs=[_row_spec(tm, n), _vec_spec(n)], out_specs=_row_spec(tm, n),
        out_shape=jax.ShapeDtypeStruct((M, n), BF16), name=name, task=task,
    )(x, g)


def _post_pre_call(xres, z, g_post, g_pre, *, name, tm=256):
    M, n = xres.shape

    def body(x_ref, z_ref, gp_ref, gn_ref, xo_ref, h_ref):
        xn = x_ref[...] + _rms(z_ref[...], gp_ref[...])
        xo_ref[...] = xn
        h_ref[...] = _rms(xn, gn_ref[...]).astype(BF16)

    return pl.pallas_call(
        body, grid=(M // tm,),
        in_specs=[_row_spec(tm, n), _row_spec(tm, n), _vec_spec(n), _vec_spec(n)],
        out_specs=(_row_spec(tm, n), _row_spec(tm, n)),
        out_shape=(jax.ShapeDtypeStruct((M, n), F32), jax.ShapeDtypeStruct((M, n), BF16)),
        name=name, compiler_params=_cparams(),
    )(xres, z, g_post, g_pre)


def _final_call(x2, y3, g_post, target, *, name, tm=256):
    M, n = x2.shape

    def body(x_ref, y_ref, g_ref, t_ref, loss_ref, dx_ref, dy_ref, dg_ref):
        i = pl.program_id(0)
        y = y_ref[...]
        g = g_ref[...]
        diff = x_ref[...] + _rms(y, g) - t_ref[...]
        part = 0.5 * jnp.sum(jnp.sum(diff * diff, axis=1, keepdims=True), axis=0, keepdims=True) / n

        @pl.when(i == 0)
        def _():
            loss_ref[...] = jnp.zeros_like(loss_ref)
        loss_ref[...] += jnp.broadcast_to(part, loss_ref.shape)
        dx = diff / n
        dx_ref[...] = dx
        dy, dg = _rms_bwd(y, g, dx)
        dy_ref[...] = dy.astype(BF16)
        _acc_add(dg_ref, dg, i)

    return pl.pallas_call(
        body, grid=(M // tm,),
        in_specs=[_row_spec(tm, n), _row_spec(tm, n), _vec_spec(n), _row_spec(tm, n)],
        out_specs=(pl.BlockSpec((8, 128), lambda i: (0, 0)), _row_spec(tm, n), _row_spec(tm, n), _acc_spec(n)),
        out_shape=(jax.ShapeDtypeStruct((8, 128), F32), jax.ShapeDtypeStruct((M, n), F32),
                   jax.ShapeDtypeStruct((M, n), BF16), jax.ShapeDtypeStruct((8, n), F32)),
        name=name, compiler_params=_cparams(),
    )(x2, y3, g_post, target)


def _bwd_mid_call(dx_in, x, dh, g_pre, y, g_post, *, name, tm=256):
    M, n = x.shape

    def body(dxi_ref, x_ref, dh_ref, gpre_ref, y_ref, gpost_ref, dx_ref, dy_ref, dgpre_ref, dgpost_ref):
        i = pl.program_id(0)
        d1, dg1 = _rms_bwd(x_ref[...], gpre_ref[...], dh_ref[...])
        dx = dxi_ref[...] + d1
        dx_ref[...] = dx
        dy, dg2 = _rms_bwd(y_ref[...], gpost_ref[...], dx)
        dy_ref[...] = dy.astype(BF16)
        _acc_add(dgpre_ref, dg1, i)
        _acc_add(dgpost_ref, dg2, i)

    return pl.pallas_call(
        body, grid=(M // tm,),
        in_specs=[_row_spec(tm, n), _row_spec(tm, n), _row_spec(tm, n), _vec_spec(n), _row_spec(tm, n), _vec_spec(n)],
        out_specs=(_row_spec(tm, n), _row_spec(tm, n), _acc_spec(n), _acc_spec(n)),
        out_shape=(jax.ShapeDtypeStruct((M, n), F32), jax.ShapeDtypeStruct((M, n), BF16),
                   jax.ShapeDtypeStruct((8, n), F32), jax.ShapeDtypeStruct((8, n), F32)),
        name=name, compiler_params=_cparams(),
    )(dx_in, x, dh, g_pre, y, g_post)


def _bwd_last_call(dx_in, x, dh, g_pre, *, name, tm=256, task=None):
    M, n = x.shape

    def body(dxi_ref, x_ref, dh_ref, g_ref, dx_ref, dg_ref):
        i = pl.program_id(0)
        d1, dg1 = _rms_bwd(x_ref[...], g_ref[...], dh_ref[...])
        dx_ref[...] = dxi_ref[...] + d1
        _acc_add(dg_ref, dg1, i)

    return _pallas(
        body, grid=(M // tm,),
        in_specs=[_row_spec(tm, n), _row_spec(tm, n), _row_spec(tm, n), _vec_spec(n)],
        out_specs=(_row_spec(tm, n), _acc_spec(n)),
        out_shape=(jax.ShapeDtypeStruct((M, n), F32), jax.ShapeDtypeStruct((8, n), F32)),
        name=name, task=task,
    )(dx_in, x, dh, g_pre)


def _gain_grad_call(x, g, dy_a, dy_b, *, name):
    M, n = x.shape

    def body(x_ref, g_ref, a_ref, b_ref, dg_ref):
        _, dg = _rms_bwd(x_ref[...], g_ref[...], a_ref[...] + b_ref[...])
        dg_ref[...] = jnp.zeros_like(dg_ref)
        dg_ref[0:1, :] = dg

    return pl.pallas_call(
        body, grid=(1,),
        in_specs=[_row_spec(M, n), _vec_spec(n), _row_spec(M, n), _row_spec(M, n)],
        out_specs=_acc_spec(n), out_shape=jax.ShapeDtypeStruct((8, n), F32),
        name=name, compiler_params=_cparams(),
    )(x, g, dy_a, dy_b)


def _head_group_matrix():
    a = lax.broadcasted_iota(jnp.int32, (D_GRP, D_GRP), 0) // HEAD
    b = lax.broadcasted_iota(jnp.int32, (D_GRP, D_GRP), 1) // HEAD
    return jnp.where(a == b, 1.0, 0.0).astype(BF16)


def _mix_norm_fwd_call(yf, yc, gf, gc, *, name, tm=256):
    M = yf.shape[0]

    def body(yf_ref, yc_ref, gf_ref, gc_ref, o_ref):
        o_ref[:, 0:D_GRP] = _rms(yf_ref[...], gf_ref[...]).astype(BF16)
        o_ref[:, D_GRP:D] = _rms(yc_ref[...], gc_ref[...]).astype(BF16)

    return pl.pallas_call(
        body, grid=(M // tm,),
        in_specs=[_row_spec(tm, D_GRP), _row_spec(tm, D_GRP), _vec_spec(D_GRP), _vec_spec(D_GRP)],
        out_specs=_row_spec(tm, D), out_shape=jax.ShapeDtypeStruct((M, D), BF16),
        name=name, compiler_params=_cparams(),
    )(yf, yc, gf, gc)


def _mix_norm_bwd_call(dyn, yf, yc, gf, gc, *, name, tm=256):
    M = yf.shape[0]

    def body(dyn_ref, yf_ref, yc_ref, gf_ref, gc_ref, dof_ref, doc_ref, delta_ref, dgf_ref, dgc_ref):
        i = pl.program_id(0)
        yf_ = yf_ref[...]
        dof, dgf = _rms_bwd(yf_, gf_ref[...], dyn_ref[:, 0:D_GRP])
        doc, dgc = _rms_bwd(yc_ref[...], gc_ref[...], dyn_ref[:, D_GRP:D])
        dof_b = dof.astype(BF16)
        dof_ref[...] = dof_b
        doc_ref[...] = doc.astype(BF16)
        prod = dof_b.astype(F32) * yf_
        hi = prod.astype(BF16)
        lo = (prod - hi.astype(F32)).astype(BF16)
        grp = _head_group_matrix()
        delta_ref[...] = _dot(hi, grp) + _dot(lo, grp)
        _acc_add(dgf_ref, dgf, i)
        _acc_add(dgc_ref, dgc, i)

    return pl.pallas_call(
        body, grid=(M // tm,),
        in_specs=[_row_spec(tm, D), _row_spec(tm, D_GRP), _row_spec(tm, D_GRP), _vec_spec(D_GRP), _vec_spec(D_GRP)],
        out_specs=(_row_spec(tm, D_GRP), _row_spec(tm, D_GRP), _row_spec(tm, D_GRP), _acc_spec(D_GRP), _acc_spec(D_GRP)),
        out_shape=(jax.ShapeDtypeStruct((M, D_GRP), BF16), jax.ShapeDtypeStruct((M, D_GRP), BF16),
                   jax.ShapeDtypeStruct((M, D_GRP), F32), jax.ShapeDtypeStruct((8, D_GRP), F32),
                   jax.ShapeDtypeStruct((8, D_GRP), F32)),
        name=name, compiler_params=_cparams(),
    )(dyn, yf, yc, gf, gc)


def _tri(n, lower_incl):
    a = lax.broadcasted_iota(jnp.int32, (n, n), 0)
    b = lax.broadcasted_iota(jnp.int32, (n, n), 1)
    return jnp.where(a >= b, 1.0, 0.0).astype(BF16) if lower_incl else jnp.where(a <= b, 1.0, 0.0).astype(BF16)


def _fox_prep_call(fl_raw, b_pad, *, name):
    S = fl_raw.shape[0]
    nb = S // TQ

    def body(fl_ref, b_ref, crep_ref, ct_ref, carry_ref):
        i = pl.program_id(0)

        @pl.when(i == 0)
        def _():
            carry_ref[...] = jnp.zeros_like(carry_ref)
        logf = jax.nn.log_sigmoid(fl_ref[...] + b_ref[...])
        cb = _dot3_l(_tri(TQ, True), logf) + carry_ref[0:1, :]
        carry_ref[0:1, :] = cb[TQ - 1:TQ, :]
        a = lax.broadcasted_iota(jnp.int32, (128, D_GRP), 0)
        b = lax.broadcasted_iota(jnp.int32, (128, D_GRP), 1) // HEAD
        expand = jnp.where(a == b, 1.0, 0.0).astype(BF16)
        crep = _dot3(cb, expand)
        crep_ref[...] = crep
        ct_ref[...] = crep.T

    return pl.pallas_call(
        body, grid=(nb,),
        in_specs=[_row_spec(TQ, 128), _vec_spec(128)],
        out_specs=(_row_spec(TQ, D_GRP), pl.BlockSpec((None, D_GRP, TQ), lambda i: (i, 0, 0))),
        out_shape=(jax.ShapeDtypeStruct((S, D_GRP), F32), jax.ShapeDtypeStruct((nb, D_GRP, TQ), F32)),
        scratch_shapes=[pltpu.VMEM((8, 128), F32)],
        name=name, compiler_params=_cparams(),
    )(fl_raw, b_pad)


def _lane_masks():
    lane = lax.broadcasted_iota(jnp.int32, (1, 128), 1)
    return lane < HEAD, lane >= HEAD


def _fox_fwd_call(proj, c_rep, c_t, *, name, task=None):
    S = proj.shape[0]
    nq = S // TQ
    scale = HEAD ** -0.5

    def body(q_ref, k_ref, v_ref, c_ref, ct_ref, o_ref, lse_ref):
        i = pl.program_id(1)
        m_lo, m_hi = _lane_masks()
        masks = (m_lo, m_hi)
        q = q_ref[...]
        qm = [jnp.where(mk, q, jnp.zeros_like(q)) for mk in masks]
        cq = c_ref[...]
        cqh = [cq[:, 0:1], cq[:, HEAD:HEAD + 1]]
        row = lax.broadcasted_iota(jnp.int32, (TQ, TQ), 0)
        col = lax.broadcasted_iota(jnp.int32, (TQ, TQ), 1)

        def step(j, carry, masked):
            ms, ls, acc = carry
            start = pl.multiple_of(j * TQ, TQ)
            k = k_ref[pl.ds(start, TQ), :]
            v = v_ref[pl.ds(start, TQ), :]
            ct = ct_ref[j]
            new_m, new_l, pv, alpha_l = [], [], [], []
            for h in range(2):
                s = _dot_nt(qm[h], k) * scale + (cqh[h] - ct[HEAD * h:HEAD * h + 1, :])
                if masked:
                    s = jnp.where(row >= col, s, NEG)
                mn = jnp.maximum(ms[h], jnp.max(s, axis=1, keepdims=True))
                alpha = jnp.exp(ms[h] - mn)
                p = jnp.exp(s - mn)
                new_l.append(alpha * ls[h] + jnp.sum(p, axis=1, keepdims=True))
                new_m.append(mn)
                alpha_l.append(alpha)
                pv.append(_dot(p.astype(BF16), jnp.where(masks[h], v, jnp.zeros_like(v))))
            alpha_lane = jnp.where(m_lo, alpha_l[0], alpha_l[1])
            acc = acc * alpha_lane + pv[0] + pv[1]
            return (tuple(new_m), tuple(new_l), acc)

        init = ((jnp.full((TQ, 1), NEG, F32),) * 2, (jnp.zeros((TQ, 1), F32),) * 2, jnp.zeros((TQ, 128), F32))
        carry = lax.fori_loop(0, i, lambda j, c: step(j, c, False), init)
        ms, ls, acc = step(i, carry, True)
        l_lane = jnp.where(m_lo, ls[0], ls[1])
        o_ref[...] = acc / l_lane
        lse_ref[...] = jnp.where(m_lo, ms[0] + jnp.log(ls[0]), ms[1] + jnp.log(ls[1]))

    return _pallas(
        body, grid=(N_PAIR, nq),
        in_specs=[pl.BlockSpec((TQ, 128), lambda p, i: (i, p)),
                  pl.BlockSpec((S, 128), lambda p, i: (0, N_PAIR + p)),
                  pl.BlockSpec((S, 128), lambda p, i: (0, 2 * N_PAIR + p)),
                  pl.BlockSpec((TQ, 128), lambda p, i: (i, p)),
                  pl.BlockSpec((nq, 128, TQ), lambda p, i: (0, p, 0))],
        out_specs=(pl.BlockSpec((TQ, 128), lambda p, i: (i, p)), pl.BlockSpec((TQ, 128), lambda p, i: (i, p))),
        out_shape=(jax.ShapeDtypeStruct((S, D_GRP), F32), jax.ShapeDtypeStruct((S, D_GRP), F32)),
        name=name, task=task,
    )(proj, proj, proj, c_rep, c_t)


def _fox_bwd_call(proj, do, lse_rep, delta_rep, c_rep, c_t, *, name, task=None):
    S = proj.shape[0]
    nq = S // TQ
    scale = HEAD ** -0.5

    def body(q_ref, k_ref, v_ref, do_ref, lse_ref, dl_ref, c_ref, ct_ref, dq_ref, dk_ref, dv_ref, dct_ref, dcq_ref, dqa_ref):
        j = pl.program_id(1)
        m_lo, m_hi = _lane_masks()
        masks = (m_lo, m_hi)

        @pl.when(j == 0)
        def _():
            dqa_ref[...] = jnp.zeros_like(dqa_ref)
            dcq_ref[...] = jnp.zeros_like(dcq_ref)
        k = k_ref[...]
        v = v_ref[...]
        km = [jnp.where(mk, k, jnp.zeros_like(k)) for mk in masks]
        ct = ct_ref[...]
        row = lax.broadcasted_iota(jnp.int32, (TQ, TQ), 0)
        col = lax.broadcasted_iota(jnp.int32, (TQ, TQ), 1)

        def step(i, carry, masked):
            dk, dv, dcs = carry
            start = pl.multiple_of(i * TQ, TQ)
            q = q_ref[pl.ds(start, TQ), :]
            do = do_ref[pl.ds(start, TQ), :]
            lse = lse_ref[pl.ds(start, TQ), :]
            dl = dl_ref[pl.ds(start, TQ), :]
            cq = c_ref[pl.ds(start, TQ), :]
            dq = jnp.zeros((TQ, 128), F32)
            new_dcs = []
            rows = []
            for h in range(2):
                lo = HEAD * h
                qm = jnp.where(masks[h], q, jnp.zeros_like(q))
                dom = jnp.where(masks[h], do, jnp.zeros_like(do))
                s = _dot_nt(qm, k) * scale + (cq[:, lo:lo + 1] - ct[lo:lo + 1, :])
                p = jnp.exp(s - lse[:, lo:lo + 1])
                if masked:
                    p = jnp.where(row >= col, p, 0.0)
                dp = _dot_nt(dom, v)
                ds = p * (dp - dl[:, lo:lo + 1])
                new_dcs.append(dcs[h] + jnp.sum(ds, axis=0, keepdims=True))
                rows.append(jnp.sum(ds, axis=1, keepdims=True))
                dsb = (ds * scale).astype(BF16)
                dv = dv + _dot_tn(p.astype(BF16), dom)
                dk = dk + _dot_tn(dsb, qm)
                dq = dq + _dot(dsb, km[h])
            dqa_ref[pl.ds(start, TQ), :] += dq
            dcq_ref[pl.ds(start, TQ), :] += jnp.where(m_lo, rows[0], rows[1])
            return (dk, dv, tuple(new_dcs))

        init = (jnp.zeros((TQ, 128), F32), jnp.zeros((TQ, 128), F32), (jnp.zeros((1, TQ), F32),) * 2)
        carry = step(j, init, True)
        dk, dv, dcs = lax.fori_loop(j + 1, nq, lambda i, c: step(i, c, False), carry)
        dk_ref[...] = dk.astype(BF16)
        dv_ref[...] = dv.astype(BF16)
        dct_ref[...] = jnp.zeros_like(dct_ref)
        dct_ref[0:1, :] = -dcs[0]
        dct_ref[1:2, :] = -dcs[1]

        @pl.when(j == nq - 1)
        def _():
            dq_ref[...] = dqa_ref[...].astype(BF16)

    res = lambda p, j: (0, p)
    return _pallas(
        body, grid=(N_PAIR, nq), task=task,
        in_specs=[pl.BlockSpec((S, 128), res),
                  pl.BlockSpec((TQ, 128), lambda p, j: (j, N_PAIR + p)),
                  pl.BlockSpec((TQ, 128), lambda p, j: (j, 2 * N_PAIR + p)),
                  pl.BlockSpec((S, 128), res), pl.BlockSpec((S, 128), res), pl.BlockSpec((S, 128), res),
                  pl.BlockSpec((S, 128), res),
                  pl.BlockSpec((None, 128, TQ), lambda p, j: (j, p, 0))],
        out_specs=(pl.BlockSpec((S, 128), res),
                   pl.BlockSpec((TQ, 128), lambda p, j: (j, p)), pl.BlockSpec((TQ, 128), lambda p, j: (j, p)),
                   pl.BlockSpec((None, None, 8, TQ), lambda p, j: (p, j, 0, 0)),
                   pl.BlockSpec((S, 128), res)),
        out_shape=(jax.ShapeDtypeStruct((S, D_GRP), BF16), jax.ShapeDtypeStruct((S, D_GRP), BF16),
                   jax.ShapeDtypeStruct((S, D_GRP), BF16), jax.ShapeDtypeStruct((N_PAIR, nq, 8, TQ), F32),
                   jax.ShapeDtypeStruct((S, D_GRP), F32)),
        scratch_shapes=[pltpu.VMEM((S, 128), F32)],
        name=name,
    )(proj, proj, proj, do, lse_rep, delta_rep, c_rep, c_t)


def _fox_gate_bwd_call(dc_rows, fl_raw, b_pad, *, name):
    S = fl_raw.shape[0]
    nb = S // TQ

    def body(dc_ref, fl_ref, b_ref, dfl_ref, db_ref, carry_ref):
        i = pl.program_id(0)

        @pl.when(i == 0)
        def _():
            carry_ref[...] = jnp.zeros_like(carry_ref)
        rc = _dot3(dc_ref[...], _tri(TQ, True)) + carry_ref[:, 0:1]
        carry_ref[...] = jnp.broadcast_to(rc[:, 0:1], carry_ref.shape)
        fl = fl_ref[...] + b_ref[...]
        dfl = rc.T * jax.nn.sigmoid(-fl)
        dfl_ref[...] = dfl.astype(BF16)
        _acc_add(db_ref, jnp.sum(dfl, axis=0, keepdims=True), i)

    rev = lambda i: (nb - 1 - i, 0)
    return pl.pallas_call(
        body, grid=(nb,),
        in_specs=[pl.BlockSpec((128, TQ), lambda i: (0, nb - 1 - i)), pl.BlockSpec((TQ, 128), rev), _vec_spec(128)],
        out_specs=(pl.BlockSpec((TQ, 128), rev), _acc_spec(128)),
        out_shape=(jax.ShapeDtypeStruct((S, 128), BF16), jax.ShapeDtypeStruct((8, 128), F32)),
        scratch_shapes=[pltpu.VMEM((128, 128), F32)],
        name=name, compiler_params=_cparams(),
    )(dc_rows, fl_raw, b_pad)


def _chk_bias_call(g_rev, *, name):
    def body(g_ref, o_ref):
        x = jnp.broadcast_to(g_ref[...], (TQ, ROLL_W))
        rolled = pltpu.roll(x, ROLL_W - (TQ - 1), 1, stride=1, stride_axis=0)
        qc = lax.broadcasted_iota(jnp.int32, (TQ, WIN), 0) // CHUNK
        kc = lax.broadcasted_iota(jnp.int32, (TQ, WIN), 1) // CHUNK
        band = (kc >= qc) & (kc <= qc + LEFT)
        o_ref[...] = jnp.where(band, rolled[:, 0:WIN], NEG)

    return pl.pallas_call(
        body, grid=(8,),
        in_specs=[pl.BlockSpec((None, 1, ROLL_W), lambda h: (h, 0, 0))],
        out_specs=pl.BlockSpec((None, TQ, WIN), lambda h: (h, 0, 0)),
        out_shape=jax.ShapeDtypeStruct((8, TQ, WIN), F32), name=name, compiler_params=_cparams(),
    )(g_rev.reshape(8, 1, ROLL_W))


def _chk_scores(i, qm, kwin, bias, scale):
    s = _dot_nt(qm, kwin) * scale + bias
    kc = lax.broadcasted_iota(jnp.int32, (TQ, WIN), 1) // CHUNK
    return jnp.where(kc + i * (TQ // CHUNK) >= LEFT, s, NEG)


def _chk_fwd_call(proj, bias, *, name, task=None):
    S = proj.shape[0]
    nq = S // TQ
    scale = HEAD ** -0.5

    def body(q_ref, k_ref, v_ref, b_ref, o_ref, kp_ref, vp_ref):
        i = pl.program_id(1)

        @pl.when(i == 0)
        def _():
            kp_ref[0:PADK, :] = jnp.zeros((PADK, 128), BF16)
            vp_ref[0:PADK, :] = jnp.zeros((PADK, 128), BF16)
            kp_ref[PADK:PADK + S, :] = k_ref[...]
            vp_ref[PADK:PADK + S, :] = v_ref[...]
        masks = _lane_masks()
        q = q_ref[...]
        start = pl.multiple_of(i * TQ, TQ)
        kwin = kp_ref[pl.ds(start, WIN), :]
        vwin = vp_ref[pl.ds(start, WIN), :]
        out = jnp.zeros((TQ, 128), F32)
        for h in range(2):
            qm = jnp.where(masks[h], q, jnp.zeros_like(q))
            s = _chk_scores(i, qm, kwin, b_ref[h], scale)
            p = jnp.exp(s - jnp.max(s, axis=1, keepdims=True))
            p = p / jnp.sum(p, axis=1, keepdims=True)
            out = out + _dot(p.astype(BF16), jnp.where(masks[h], vwin, jnp.zeros_like(vwin)))
        o_ref[...] = out

    c0 = 3 * N_PAIR
    return _pallas(
        body, grid=(N_PAIR, nq), task=task,
        in_specs=[pl.BlockSpec((TQ, 128), lambda p, i: (i, c0 + p)),
                  pl.BlockSpec((S, 128), lambda p, i: (0, c0 + N_PAIR + p)),
                  pl.BlockSpec((S, 128), lambda p, i: (0, c0 + 2 * N_PAIR + p)),
                  pl.BlockSpec((2, TQ, WIN), lambda p, i: (p, 0, 0))],
        out_specs=pl.BlockSpec((TQ, 128), lambda p, i: (i, p)),
        out_shape=jax.ShapeDtypeStruct((S, D_GRP), F32),
        scratch_shapes=[pltpu.VMEM((S + PADK, 128), BF16), pltpu.VMEM((S + PADK, 128), BF16)],
        name=name,
    )(proj, proj, proj, bias)


def _chk_bwd_call(proj, do, bias, *, name, task=None):
    S = proj.shape[0]
    nq = S // TQ
    scale = HEAD ** -0.5

    def body(q_ref, k_ref, v_ref, do_ref, b_ref, dq_ref, dk_ref, dv_ref, dg_ref, kp_ref, vp_ref, dkp_ref, dvp_ref, db_ref):
        i = pl.program_id(1)

        @pl.when(i == 0)
        def _():
            kp_ref[0:PADK, :] = jnp.zeros((PADK, 128), BF16)
            vp_ref[0:PADK, :] = jnp.zeros((PADK, 128), BF16)
            kp_ref[PADK:PADK + S, :] = k_ref[...]
            vp_ref[PADK:PADK + S, :] = v_ref[...]
            dkp_ref[...] = jnp.zeros_like(dkp_ref)
            dvp_ref[...] = jnp.zeros_like(dvp_ref)
            db_ref[...] = jnp.zeros_like(db_ref)
        masks = _lane_masks()
        q = q_ref[...]
        dout = do_ref[...]
        start = pl.multiple_of(i * TQ, TQ)
        kwin = kp_ref[pl.ds(start, WIN), :]
        vwin = vp_ref[pl.ds(start, WIN), :]
        dq = jnp.zeros((TQ, 128), F32)
        dkw = jnp.zeros((WIN, 128), F32)
        dvw = jnp.zeros((WIN, 128), F32)
        for h in range(2):
            qm = jnp.where(masks[h], q, jnp.zeros_like(q))
            dom = jnp.where(masks[h], dout, jnp.zeros_like(dout))
            s = _chk_scores(i, qm, kwin, b_ref[h], scale)
            p = jnp.exp(s - jnp.max(s, axis=1, keepdims=True))
            p = p / jnp.sum(p, axis=1, keepdims=True)
            dp = _dot_nt(dom, vwin)
            ds = p * (dp - jnp.sum(p * dp, axis=1, keepdims=True))
            db_ref[h] += ds
            dsb = (ds * scale).astype(BF16)
            dq = dq + _dot(dsb, jnp.where(masks[h], kwin, jnp.zeros_like(kwin)))
            dkw = dkw + _dot_tn(dsb, qm)
            dvw = dvw + _dot_tn(p.astype(BF16), dom)
        dq_ref[...] = dq.astype(BF16)
        dkp_ref[pl.ds(start, WIN), :] += dkw
        dvp_ref[pl.ds(start, WIN), :] += dvw

        @pl.when(i == nq - 1)
        def _():
            dk_ref[...] = dkp_ref[PADK:PADK + S, :].astype(BF16)
            dv_ref[...] = dvp_ref[PADK:PADK + S, :].astype(BF16)
            a = lax.broadcasted_iota(jnp.int32, (TQ, TQ), 0)
            b = lax.broadcasted_iota(jnp.int32, (TQ, TQ), 1)
            flip = jnp.where(a + b == TQ - 1, 1.0, 0.0).astype(BF16)
            e = lax.broadcasted_iota(jnp.int32, (1, ROLL_W), 1)
            dg_ref[...] = jnp.zeros_like(dg_ref)
            for h in range(2):
                rev = _dot3_l(flip, db_ref[h])
                wide = jnp.concatenate([rev, jnp.zeros((TQ, ROLL_W - WIN), F32)], axis=1)
                diag = pltpu.roll(wide, 0, 1, stride=1, stride_axis=0)
                dg = jnp.sum(diag, axis=0, keepdims=True)
                lo = jnp.sum(jnp.where(e <= 639, dg, 0.0), axis=1, keepdims=True)
                hi = jnp.sum(jnp.where(e >= 895, dg, 0.0), axis=1, keepdims=True)
                dg_ref[h:h + 1, :] = jnp.where(e == 639, lo, jnp.where(e == 895, hi, dg))

    c0 = 3 * N_PAIR
    res = lambda p, i: (0, p)
    return _pallas(
        body, grid=(N_PAIR, nq), task=task,
        in_specs=[pl.BlockSpec((TQ, 128), lambda p, i: (i, c0 + p)),
                  pl.BlockSpec((S, 128), lambda p, i: (0, c0 + N_PAIR + p)),
                  pl.BlockSpec((S, 128), lambda p, i: (0, c0 + 2 * N_PAIR + p)),
                  pl.BlockSpec((TQ, 128), lambda p, i: (i, p)),
                  pl.BlockSpec((2, TQ, WIN), lambda p, i: (p, 0, 0))],
        out_specs=(pl.BlockSpec((TQ, 128), lambda p, i: (i, p)), pl.BlockSpec((S, 128), res),
                   pl.BlockSpec((S, 128), res), pl.BlockSpec((None, 8, ROLL_W), lambda p, i: (p, 0, 0))),
        out_shape=(jax.ShapeDtypeStruct((S, D_GRP), BF16), jax.ShapeDtypeStruct((S, D_GRP), BF16),
                   jax.ShapeDtypeStruct((S, D_GRP), BF16), jax.ShapeDtypeStruct((N_PAIR, 8, ROLL_W), F32)),
        scratch_shapes=[pltpu.VMEM((S + PADK, 128), BF16), pltpu.VMEM((S + PADK, 128), BF16),
                        pltpu.VMEM((S + PADK, 128), F32), pltpu.VMEM((S + PADK, 128), F32),
                        pltpu.VMEM((2, TQ, WIN), F32)],
        name=name,
    )(proj, proj, proj, do, bias)


def _mem_fwd_call(q, k, v, *, name, tq=512):
    S = q.shape[0]
    scale = MEM_HD ** -0.5

    def body(q_ref, k_ref, v_ref, o_ref):
        s = _dot_nt(q_ref[...], k_ref[...]) * scale
        p = jnp.exp(s - jnp.max(s, axis=1, keepdims=True))
        p = p / jnp.sum(p, axis=1, keepdims=True)
        o_ref[...] = _dot(p.astype(BF16), v_ref[...]).astype(BF16)

    return pl.pallas_call(
        body, grid=(MEM_HEADS, S // tq),
        in_specs=[pl.BlockSpec((tq, MEM_HD), lambda h, i: (i, h)),
                  pl.BlockSpec((N_MEM, MEM_HD), lambda h, i: (0, h)),
                  pl.BlockSpec((N_MEM, MEM_HD), lambda h, i: (0, h))],
        out_specs=pl.BlockSpec((tq, MEM_HD), lambda h, i: (i, h)),
        out_shape=jax.ShapeDtypeStruct((S, D), BF16), name=name, compiler_params=_cparams(),
    )(q, k, v)


def _mem_bwd_call(q, k, v, do, *, name, tq=512):
    S = q.shape[0]
    n = S // tq
    scale = MEM_HD ** -0.5

    def body(q_ref, k_ref, v_ref, do_ref, dq_ref, dk_ref, dv_ref, dka_ref, dva_ref):
        i = pl.program_id(1)

        @pl.when(i == 0)
        def _():
            dka_ref[...] = jnp.zeros_like(dka_ref)
            dva_ref[...] = jnp.zeros_like(dva_ref)
        qb = q_ref[...]
        kb = k_ref[...]
        dob = do_ref[...]
        s = _dot_nt(qb, kb) * scale
        p = jnp.exp(s - jnp.max(s, axis=1, keepdims=True))
        p = p / jnp.sum(p, axis=1, keepdims=True)
        dp = _dot_nt(dob, v_ref[...])
        ds = p * (dp - jnp.sum(p * dp, axis=1, keepdims=True))
        dsb = (ds * scale).astype(BF16)
        dq_ref[...] = _dot(dsb, kb).astype(BF16)
        dka_ref[...] += _dot_tn(dsb, qb)
        dva_ref[...] += _dot_tn(p.astype(BF16), dob)

        @pl.when(i == n - 1)
        def _():
            dk_ref[...] = dka_ref[...].astype(BF16)
            dv_ref[...] = dva_ref[...].astype(BF16)

    kv = pl.BlockSpec((N_MEM, MEM_HD), lambda h, i: (0, h))
    qs = pl.BlockSpec((tq, MEM_HD), lambda h, i: (i, h))
    return pl.pallas_call(
        body, grid=(MEM_HEADS, n), in_specs=[qs, kv, kv, qs], out_specs=(qs, kv, kv),
        out_shape=(jax.ShapeDtypeStruct((S, D), BF16), jax.ShapeDtypeStruct((N_MEM, D), BF16),
                   jax.ShapeDtypeStruct((N_MEM, D), BF16)),
        scratch_shapes=[pltpu.VMEM((N_MEM, MEM_HD), F32), pltpu.VMEM((N_MEM, MEM_HD), F32)],
        name=name, compiler_params=_cparams(),
    )(q, k, v, do)


def _rel_table_to_g(rel):
    return jnp.concatenate([
        jnp.broadcast_to(rel[:, N_REL - 1:N_REL], (8, 640)),
        rel[:, 1:N_REL - 1][:, ::-1],
        jnp.broadcast_to(rel[:, 0:1], (8, 129)),
    ], axis=1)


def _g_to_rel_table(dg):
    return dg[:, 639:896][:, ::-1]


def _place():
    x, y, c = lax.axis_index("x"), lax.axis_index("y"), lax.axis_index("c")
    others = [(1 - x, y), (x, 1 - y), (1 - x, 1 - y)]
    return x, y, c, others


def _half(c, rows):
    hr = rows // 2
    return pl.ds(pl.multiple_of(c * hr, 16), hr)


def _dma_sems(*shape):
    return pltpu.SemaphoreType.DMA(shape)


def _cast_slab_call(w, chip_arr, *, name, tm=256, pad_rows=0):
    rows, cols = w.shape
    if pad_rows:
        tm = rows
    tm = min(tm, rows)

    def body(chip_ref, w_ref, o_ref):
        o_ref[0:tm, :] = w_ref[...].astype(BF16)
        if pad_rows:
            o_ref[tm:tm + pad_rows, :] = jnp.zeros((pad_rows, cols), BF16)

    return pl.pallas_call(
        body,
        grid_spec=pltpu.PrefetchScalarGridSpec(
            num_scalar_prefetch=1, grid=(rows // tm,),
            in_specs=[pl.BlockSpec((tm, cols), lambda i, chip: (i, 0))],
            out_specs=pl.BlockSpec((None, tm + pad_rows, cols), lambda i, chip: (chip[0], i, 0))),
        out_shape=jax.ShapeDtypeStruct((N_CHIP, rows + pad_rows, cols), BF16), name=name,
        compiler_params=_cparams(),
    )(chip_arr, _hbm(w))


def _ag_ici_task(gathered):
    n = len(gathered)

    def copies(ins, outs, sems):
        send_sems, recv_sems = sems
        x, y, c, others = _place()
        me = 2 * x + y
        for k in range(n):
            mine = _half(c, gathered[k].shape[1])
            for t, (ox, oy) in enumerate(others):
                yield [pltpu.make_async_remote_copy(
                    src_ref=ins[k].at[me, mine], dst_ref=outs[k].at[slab, mine],
                    send_sem=send_sems.at[k, t], recv_sem=recv_sems.at[k, t],
                    device_id=(ox, oy, c), device_id_type=MESH) for slab in (me, 2 * ox + oy)]

    def issue(ins, outs, sems):
        for outgoing, _ in copies(ins, outs, sems):
            outgoing.start()

    def drain(ins, outs, sems):
        for outgoing, incoming in copies(ins, outs, sems):
            incoming.wait_recv()
            outgoing.wait_send()

    return _Task(gathered, [jax.ShapeDtypeStruct(g.shape, g.dtype) for g in gathered],
                 [_dma_sems(n, 3), _dma_sems(n, 3)], issue, drain, aliases={k: k for k in range(n)})


def _ag_d2d_task(gathered):
    n = len(gathered)

    def copies(ins, outs, sems):
        send_sems, recv_sems = sems
        x, y, c, others = _place()
        for k in range(n):
            rows = gathered[k].shape[1]
            mine, theirs = _half(c, rows), _half(1 - c, rows)
            for t, (ox, oy) in enumerate(others):
                slab = 2 * ox + oy
                pair = [pltpu.make_async_remote_copy(
                    src_ref=ins[k].at[slab, half], dst_ref=outs[k].at[slab, half],
                    send_sem=send_sems.at[k, t], recv_sem=recv_sems.at[k, t],
                    device_id=(x, y, 1 - c), device_id_type=MESH) for half in (mine, theirs)]
                yield pair

    def issue(ins, outs, sems):
        for outgoing, _ in copies(ins, outs, sems):
            outgoing.start()

    def drain(ins, outs, sems):
        for outgoing, incoming in copies(ins, outs, sems):
            incoming.wait_recv()
            outgoing.wait_send()

    return _Task(gathered, [jax.ShapeDtypeStruct(g.shape, g.dtype) for g in gathered],
                 [_dma_sems(n, 3), _dma_sems(n, 3)], issue, drain, aliases={k: k for k in range(n)})


def _rs_pair_task(ds):
    n = len(ds)

    def copies(ins, outs, sems):
        send_sems, recv_sems = sems
        x, y, c, _ = _place()
        for k in range(n):
            yield pltpu.make_async_remote_copy(
                src_ref=ins[k].at[:, _half(1 - c, ds[k].shape[1])], dst_ref=outs[k],
                send_sem=send_sems.at[k], recv_sem=recv_sems.at[k],
                device_id=(x, y, 1 - c), device_id_type=MESH)

    def issue(ins, outs, sems):
        for cp in copies(ins, outs, sems):
            cp.start()

    def drain(ins, outs, sems):
        for cp in copies(ins, outs, sems):
            cp.wait()

    return _Task(ds, [jax.ShapeDtypeStruct((N_CHIP, d.shape[1] // 2, d.shape[2]), d.dtype) for d in ds],
                 [_dma_sems(n), _dma_sems(n)], issue, drain)


def _pair_add_call(d, r1, c_arr, *, name, tm=256):
    _, rows, cols = d.shape
    hr = rows // 2
    tm = tm if hr % tm == 0 else hr
    nb = hr // tm

    def body(c_ref, d_ref, r_ref, o_ref):
        o_ref[...] = (d_ref[...].astype(F32) + r_ref[...].astype(F32)).astype(BF16)

    return pl.pallas_call(
        body,
        grid_spec=pltpu.PrefetchScalarGridSpec(
            num_scalar_prefetch=1, grid=(N_CHIP, nb),
            in_specs=[pl.BlockSpec((None, tm, cols), lambda j, i, c: (j, c[0] * nb + i, 0)),
                      pl.BlockSpec((None, tm, cols), lambda j, i, c: (j, i, 0))],
            out_specs=pl.BlockSpec((None, tm, cols), lambda j, i, c: (j, i, 0))),
        out_shape=jax.ShapeDtypeStruct((N_CHIP, hr, cols), BF16), name=name, compiler_params=_cparams(),
    )(c_arr, _hbm(d), _hbm(r1))


def _rs_chip_task(ps):
    n = len(ps)

    def copies(ins, outs, sems):
        send_sems, recv_sems = sems
        x, y, c, others = _place()
        for k in range(n):
            for t, (ox, oy) in enumerate(others):
                yield pltpu.make_async_remote_copy(
                    src_ref=ins[k].at[2 * ox + oy], dst_ref=outs[k].at[t],
                    send_sem=send_sems.at[k, t], recv_sem=recv_sems.at[k, t],
                    device_id=(ox, oy, c), device_id_type=MESH)

    def issue(ins, outs, sems):
        for cp in copies(ins, outs, sems):
            cp.start()

    def drain(ins, outs, sems):
        for cp in copies(ins, outs, sems):
            cp.wait()

    return _Task(ps, [jax.ShapeDtypeStruct((3,) + p.shape[1:], p.dtype) for p in ps],
                 [_dma_sems(n, 3), _dma_sems(n, 3)], issue, drain)


def _chip_sum_call(p, r2, place_arr, *, name, tm=256):
    _, hr, cols = r2.shape
    tm = tm if hr % tm == 0 else hr
    nb = hr // tm

    def body(place_ref, p_ref, r_ref, o_ref):
        acc = p_ref[...].astype(F32)
        for j in range(3):
            acc = acc + r_ref[j].astype(F32)
        o_ref[...] = acc

    return pl.pallas_call(
        body,
        grid_spec=pltpu.PrefetchScalarGridSpec(
            num_scalar_prefetch=1, grid=(nb,),
            in_specs=[pl.BlockSpec((None, tm, cols), lambda i, pc: (pc[0], i, 0)),
                      pl.BlockSpec((3, tm, cols), lambda i, pc: (0, i, 0))],
            out_specs=pl.BlockSpec((tm, cols), lambda i, pc: (pc[1] * nb + i, 0))),
        out_shape=jax.ShapeDtypeStruct((2 * hr, cols), F32), name=name, compiler_params=_cparams(),
    )(place_arr, _hbm(p), _hbm(r2))


def _rs_gather_task(gs):
    n = len(gs)

    def copies(ins, outs, sems):
        send_sems, recv_sems = sems
        x, y, c, _ = _place()
        for k in range(n):
            rows = gs[k].shape[0]
            mine, theirs = _half(c, rows), _half(1 - c, rows)
            yield [pltpu.make_async_remote_copy(
                src_ref=ins[k].at[mine], dst_ref=outs[k].at[half],
                send_sem=send_sems.at[k], recv_sem=recv_sems.at[k],
                device_id=(x, y, 1 - c), device_id_type=MESH) for half in (mine, theirs)]

    def issue(ins, outs, sems):
        for outgoing, _ in copies(ins, outs, sems):
            outgoing.start()

    def drain(ins, outs, sems):
        for outgoing, incoming in copies(ins, outs, sems):
            incoming.wait_recv()
            outgoing.wait_send()

    return _Task(gs, [jax.ShapeDtypeStruct(g.shape, g.dtype) for g in gs],
                 [_dma_sems(n), _dma_sems(n)], issue, drain, aliases={k: k for k in range(n)})


def _adamw(w, g, m, v):
    m = ADAM_B1 * m + (1.0 - ADAM_B1) * g
    v = ADAM_B2 * v + (1.0 - ADAM_B2) * jnp.square(g)
    m_hat = m / (1.0 - ADAM_B1 ** ADAM_STEP)
    v_hat = v / (1.0 - ADAM_B2 ** ADAM_STEP)
    delta = -ADAM_LR * (m_hat / (jnp.sqrt(v_hat) + ADAM_EPS) + ADAM_WD * w)
    return delta, m, v


def _adamw_call(items, *, name, tm=256, task=None):
    n = len(items)
    cols = items[0][0].shape[1]
    tiles = [it[0].shape[0] // tm for it in items]
    steps = max(tiles)

    def body(*refs):
        i = pl.program_id(0)
        ins, outs = refs[:4 * n], refs[4 * n:]
        for k in range(n):
            def update(k=k):
                g = ins[4 * k + 1][...]
                res = _adamw(ins[4 * k][...], g, ins[4 * k + 2][...], ins[4 * k + 3][...])
                outs[4 * k][...] = g
                for j in range(3):
                    outs[4 * k + 1 + j][...] = res[j]
            if tiles[k] == steps:
                update()
            else:
                pl.when(i < tiles[k])(update)

    in_specs, out_specs, out_shape, args = [], [], [], []
    for it, t in zip(items, tiles):
        spec = pl.BlockSpec((tm, cols), lambda i, t=t: (jnp.minimum(i, t - 1), 0))
        in_specs += [spec] * 4
        out_specs += [spec] * 4
        out_shape += [jax.ShapeDtypeStruct(it[0].shape, F32)] * 4
        args += list(it)
    res = _pallas(body, grid=(steps,), in_specs=in_specs, out_specs=out_specs, out_shape=out_shape,
                  name=name, task=task)(*args)
    outs, extra = res if task is not None else (res, None)
    grouped = [tuple(outs[4 * k:4 * k + 4]) for k in range(n)]
    return (grouped, extra) if task is not None else grouped


def _adamw_cols_call(w, g_pad, m, v, *, name, tn=256):
    rows, cols = w.shape

    def body(w_ref, g_ref, m_ref, v_ref, go_ref, d_ref, mo_ref, vo_ref):
        g = g_ref[0:rows, :]
        d, mn, vn = _adamw(w_ref[...], g, m_ref[...], v_ref[...])
        go_ref[...] = g
        d_ref[...] = d
        mo_ref[...] = mn
        vo_ref[...] = vn

    spec = pl.BlockSpec((rows, tn), lambda j: (0, j))
    gspec = pl.BlockSpec((g_pad.shape[0], tn), lambda j: (0, j))
    return _pallas(body, grid=(cols // tn,), in_specs=[spec, gspec, spec, spec], out_specs=(spec,) * 4,
                   out_shape=(jax.ShapeDtypeStruct((rows, cols), F32),) * 4, name=name)(w, g_pad, m, v)


N_DEV = 8
SMALL_ROWS = 24
SMALL_LAYOUT = {
    "g_mix_pre": (0, 0, 1, D), "g_mix_post": (1, 0, 1, D), "g_mem_kv": (2, 0, 1, D), "g_mem_pre": (3, 0, 1, D),
    "g_mem_post": (4, 0, 1, D), "g_ff_pre": (5, 0, 1, D), "g_ff_post": (6, 0, 1, D),
    "g_fox_out": (7, 0, 1, D_GRP), "g_chk_out": (7, D_GRP, 1, D_GRP), "b_fgt": (8, 0, 1, 8),
    "rel_bias": (16, 0, 8, N_REL),
}
SMALL = list(SMALL_LAYOUT)


LOSS_ROW = 9


def _small_call(grads, ws, ms, vs, loss_blk, task, *, name):
    n = len(SMALL)
    t_in, t_out = len(task.arrays), len(task.out_shapes)

    def body(*refs):
        g_refs, w_refs, m_refs, v_refs = (refs[j * n:(j + 1) * n] for j in range(4))
        p = 4 * n
        loss_ref, tins = refs[p], refs[p + 1:p + 1 + t_in]
        p += 1 + t_in
        outs, loss_out, touts = refs[p:p + 4 * n], refs[p + 4 * n], refs[p + 4 * n + 1:p + 4 * n + 1 + t_out]
        p += 4 * n + 1 + t_out
        mine, slots, send_sems, recv_sems = refs[p:p + 4]
        tsems = refs[p + 4:]
        task.issue(tins, touts, tsems)
        x, y, c, _ = _place()
        me = 4 * x + 2 * y + c
        mine[...] = jnp.zeros_like(mine)
        for k, name_k in enumerate(SMALL):
            r, l, nr, nl = SMALL_LAYOUT[name_k]
            mine[r:r + nr, l:l + nl] = g_refs[k][0:nr, 0:nl]
        mine[LOSS_ROW:LOSS_ROW + 1, 0:128] = loss_ref[0:1, :]
        slots[me] = mine[...]
        peers = [(dx, dy, dc) for dx in (0, 1) for dy in (0, 1) for dc in (0, 1)][1:]
        cps = []
        for t, (dx, dy, dc) in enumerate(peers):
            px, py, pc = (x + dx) % 2, (y + dy) % 2, (c + dc) % 2
            cps.append(pltpu.make_async_remote_copy(
                src_ref=mine, dst_ref=slots.at[me], send_sem=send_sems.at[t], recv_sem=recv_sems.at[t],
                device_id=(px, py, pc), device_id_type=MESH))
            cps[-1].start()
        for t, (dx, dy, dc) in enumerate(peers):
            px, py, pc = (x + dx) % 2, (y + dy) % 2, (c + dc) % 2
            pltpu.make_async_remote_copy(
                src_ref=mine, dst_ref=slots.at[4 * px + 2 * py + pc], send_sem=send_sems.at[t],
                recv_sem=recv_sems.at[t], device_id=(px, py, pc), device_id_type=MESH).wait_recv()
        for cp in cps:
            cp.wait_send()
        total = slots[0]
        for j in range(1, N_DEV):
            total = total + slots[j]
        for k, name_k in enumerate(SMALL):
            r, l, nr, nl = SMALL_LAYOUT[name_k]
            g = total[r:r + nr, l:l + nl]
            d, mn, vn = _adamw(w_refs[k][...], g, m_refs[k][...], v_refs[k][...])
            for j, val in enumerate((g, d, mn, vn)):
                outs[4 * k + j][...] = val
        loss_out[...] = jnp.broadcast_to(total[LOSS_ROW:LOSS_ROW + 1, 0:128], loss_out.shape)
        task.drain(tins, touts, tsems)

    vm = pl.BlockSpec(memory_space=pltpu.VMEM)
    out_shape = [jax.ShapeDtypeStruct(ws[k].shape, F32) for k in SMALL for _ in range(4)]
    out_shape += [jax.ShapeDtypeStruct((8, 128), F32)] + list(task.out_shapes)
    res = pl.pallas_call(
        body, in_specs=[vm] * (4 * n + 1) + [ANY] * t_in, out_specs=[vm] * (4 * n + 1) + [ANY] * t_out,
        out_shape=out_shape,
        scratch_shapes=[pltpu.VMEM((SMALL_ROWS, D), F32), pltpu.VMEM((N_DEV, SMALL_ROWS, D), F32),
                        _dma_sems(N_DEV - 1), _dma_sems(N_DEV - 1)] + list(task.sems),
        input_output_aliases={4 * n + 1 + i: 4 * n + 1 + j for i, j in task.aliases.items()},
        name=name,
    )(*[d[k] for d in (grads, ws, ms, vs) for k in SMALL], loss_blk, *map(_hbm, task.arrays))
    return ({k: tuple(res[4 * i:4 * i + 4]) for i, k in enumerate(SMALL)}, res[4 * n], list(res[4 * n + 1:]))


WEIGHTS = ["w_in", "b_fgt", "rel_bias", "g_fox_out", "g_chk_out", "w_out", "g_mix_pre", "g_mix_post", "g_mem_kv",
           "w_mq", "w_mk", "w_mv", "w_mo", "g_mem_pre", "g_mem_post", "w_ff1", "w_ff2", "g_ff_pre", "g_ff_post"]
BIG = ["w_in", "w_out", "w_mq", "w_mk", "w_mv", "w_mo", "w_ff1", "w_ff2"]


IN_SHARD = D_IN // N_CHIP
IN_PAD = 800


IN_PIECES = [(0, 0, 770), (800, 770, 766), (1566, 3072, 4), (1600, 3076, 4), (1604, 1536, 766), (2400, 2302, 770)]
PAD_ZEROS = [(800 * j + IN_SHARD, IN_PAD - IN_SHARD) for j in range(N_CHIP)]
ALL_ZEROS = [(D_IN, D_ALL - D_IN)]


def _reorder_rows_call(src, to_all, *, name, tn=256):
    rows, cols = src.shape
    zeros = ALL_ZEROS if to_all else PAD_ZEROS

    def body(s_ref, o_ref):
        for pad0, all0, cnt in IN_PIECES:
            s0, d0 = (pad0, all0) if to_all else (all0, pad0)
            o_ref[d0:d0 + cnt, :] = s_ref[s0:s0 + cnt, :]
        for z0, cnt in zeros:
            o_ref[z0:z0 + cnt, :] = jnp.zeros((cnt, tn), src.dtype)

    spec = pl.BlockSpec((rows, tn), lambda j: (0, j))
    return _pallas(body, grid=(cols // tn,), in_specs=[spec], out_specs=spec,
                   out_shape=jax.ShapeDtypeStruct((rows, cols), src.dtype), name=name)(src)


def kernel(x, mem, w_in, b_fgt, rel_bias, g_fox_out, g_chk_out, w_out, g_mix_pre, g_mix_post, g_mem_kv, w_mq, w_mk, w_mv, w_mo, g_mem_pre, g_mem_post, w_ff1, w_ff2, g_ff_pre, g_ff_post, loss_target, m_w_in, m_b_fgt, m_rel_bias, m_g_fox_out, m_g_chk_out, m_w_out, m_g_mix_pre, m_g_mix_post, m_g_mem_kv, m_w_mq, m_w_mk, m_w_mv, m_w_mo, m_g_mem_pre, m_g_mem_post, m_w_ff1, m_w_ff2, m_g_ff_pre, m_g_ff_post, v_w_in, v_b_fgt, v_rel_bias, v_g_fox_out, v_g_chk_out, v_w_out, v_g_mix_pre, v_g_mix_post, v_g_mem_kv, v_w_mq, v_w_mk, v_w_mv, v_w_mo, v_g_mem_pre, v_g_mem_post, v_w_ff1, v_w_ff2, v_g_ff_pre, v_g_ff_post):
    w = dict(w_in=w_in, b_fgt=b_fgt, rel_bias=rel_bias, g_fox_out=g_fox_out, g_chk_out=g_chk_out, w_out=w_out,
             g_mix_pre=g_mix_pre, g_mix_post=g_mix_post, g_mem_kv=g_mem_kv, w_mq=w_mq, w_mk=w_mk, w_mv=w_mv,
             w_mo=w_mo, g_mem_pre=g_mem_pre, g_mem_post=g_mem_post, w_ff1=w_ff1, w_ff2=w_ff2, g_ff_pre=g_ff_pre,
             g_ff_post=g_ff_post)
    m = dict(w_in=m_w_in, b_fgt=m_b_fgt, rel_bias=m_rel_bias, g_fox_out=m_g_fox_out, g_chk_out=m_g_chk_out,
             w_out=m_w_out, g_mix_pre=m_g_mix_pre, g_mix_post=m_g_mix_post, g_mem_kv=m_g_mem_kv, w_mq=m_w_mq,
             w_mk=m_w_mk, w_mv=m_w_mv, w_mo=m_w_mo, g_mem_pre=m_g_mem_pre, g_mem_post=m_g_mem_post,
             w_ff1=m_w_ff1, w_ff2=m_w_ff2, g_ff_pre=m_g_ff_pre, g_ff_post=m_g_ff_post)
    v = dict(w_in=v_w_in, b_fgt=v_b_fgt, rel_bias=v_rel_bias, g_fox_out=v_g_fox_out, g_chk_out=v_g_chk_out,
             w_out=v_w_out, g_mix_pre=v_g_mix_pre, g_mix_post=v_g_mix_post, g_mem_kv=v_g_mem_kv, w_mq=v_w_mq,
             w_mk=v_w_mk, w_mv=v_w_mv, w_mo=v_w_mo, g_mem_pre=v_g_mem_pre, g_mem_post=v_g_mem_post,
             w_ff1=v_w_ff1, w_ff2=v_w_ff2, g_ff_pre=v_g_ff_pre, g_ff_post=v_g_ff_post)

    def rows(d, k):
        return d[k][0] if k == "rel_bias" else d[k]

    xs, mems, target = x[0], mem[0], loss_target[0]
    S = xs.shape[0]
    sp = {k: rows(w, k) for k in SMALL}
    b_pad = jnp.pad(sp["b_fgt"], ((0, 0), (0, 120)))
    chip = 2 * lax.axis_index("x") + lax.axis_index("y")
    chip_arr = jnp.reshape(chip, (1,)).astype(jnp.int32)
    c_arr = jnp.reshape(lax.axis_index("c"), (1,)).astype(jnp.int32)
    place_arr = jnp.concatenate([chip_arr, c_arr])
    w_in_t, m_in_t, v_in_t = w["w_in"][0].T, m["w_in"][0].T, v["w_in"][0].T
    slab = {k: _cast_slab_call(w[k][0], chip_arr, name="cast_" + k) for k in BIG[1:]}
    slab["w_in"] = _cast_slab_call(w_in_t, chip_arr, name="cast_w_in", pad_rows=IN_PAD - IN_SHARD)

    def gather_ici(names):
        return _ag_ici_task([slab[k] for k in names])

    def pair_add(k, d, r1):
        return _pair_add_call(d, r1, c_arr, name="rs_pair_add_" + k)

    g_in, = _comm_call(gather_ici(["w_in"]), name="ag_w_in")
    h1, (g_in,) = _rms_fwd_call(xs, sp["g_mix_pre"], name="rms_mix_pre", task=_ag_d2d_task([g_in]))
    w_all_t = _reorder_rows_call(g_in.reshape(N_CHIP * IN_PAD, D), True, name="w_in_rows")
    proj, (g_out,) = _mm_nt(h1, w_all_t, "plain", rows=(0, 3072), name="mm_proj", task=gather_ici(["w_out"]))
    fl_raw = _mm_nt(h1, w_all_t, "plain", rows=(3072, 128), name="mm_gate", out_dtype=F32, tn=128)
    c_rep, c_t = _fox_prep_call(fl_raw, b_pad, name="fox_prep")
    bias = _chk_bias_call(_rel_table_to_g(sp["rel_bias"]), name="chk_bias")
    mid = ["w_mq", "w_mk", "w_mv", "w_mo", "w_ff1"]
    (yf, lse), got = _fox_fwd_call(proj, c_rep, c_t, name="fox_fwd",
                                   task=_merge_tasks([gather_ici(mid), _ag_d2d_task([g_out])]))
    g_mid, g_out = got[:5], got[5]
    yc, got = _chk_fwd_call(proj, bias, name="chk_fwd",
                            task=_merge_tasks([gather_ici(["w_ff2"]), _ag_d2d_task(g_mid)]))
    g_ff2, (g_mq, g_mk, g_mv, g_mo, g_ff1) = got[0], got[1:]
    yn = _mix_norm_fwd_call(yf, yc, sp["g_fox_out"], sp["g_chk_out"], name="mix_norm_fwd")
    z, (g_ff2,) = _mm_nn(yn, g_out, "rows", name="mm_out", out_dtype=F32, task=_ag_d2d_task([g_ff2]))
    x1, h2 = _post_pre_call(xs, z, sp["g_mix_post"], sp["g_mem_pre"], name="post_mix")
    memn = _rms_fwd_call(mems, sp["g_mem_kv"], name="rms_mem_kv")
    q2 = _mm_nn(h2, g_mq, "rows", name="mm_mq")
    k2 = _mm_nn(memn, g_mk, "rows", name="mm_mk")
    v2 = _mm_nn(memn, g_mv, "rows", name="mm_mv")
    o2 = _mem_fwd_call(q2, k2, v2, name="mem_fwd")
    y2 = _mm_nn(o2, g_mo, "rows", name="mm_mo", out_dtype=F32)
    x2, h3 = _post_pre_call(x1, y2, sp["g_mem_post"], sp["g_ff_pre"], name="post_mem")
    act, relu = _mm_nn(h3, g_ff1, "cols", name="mm_ff1", epi="relu2")
    y3 = _mm_nn(act, g_ff2, "rows", name="mm_ff2", out_dtype=F32, tm=1024)
    loss_blk, dx3, dy3, dg_ff_post = _final_call(x2, y3, sp["g_ff_post"], target, name="final")

    d_ff2 = _mm_tn(act, dy3, name="mm_dff2", tk=512, tn=1024).reshape(N_CHIP, D_FF // N_CHIP, D)
    du, (r1,) = _mm_nt(dy3, g_ff2, "rows", name="mm_du", mul2r=relu, task=_rs_pair_task([d_ff2]))
    p_ff2 = pair_add("w_ff2", d_ff2, r1)
    d_ff1 = _mm_tn(h3, du, name="mm_dff1", cols4=True)
    dh3, (r1,) = _mm_nt(du, g_ff1, "cols", name="mm_dh3", out_dtype=F32, tm=1024, task=_rs_pair_task([d_ff1]))
    p_ff1 = pair_add("w_ff1", d_ff1, r1)
    dx2, dy2, dg_ff_pre, dg_mem_post = _bwd_mid_call(dx3, x2, dh3, sp["g_ff_pre"], y2, sp["g_mem_post"], name="bwd_ff")
    d_mo = _mm_tn(o2, dy2, name="mm_dmo").reshape(N_CHIP, D // N_CHIP, D)
    do2 = _mm_nt(dy2, g_mo, "rows", name="mm_do2")
    dq2, dk2, dv2 = _mem_bwd_call(q2, k2, v2, do2, name="mem_bwd")
    d_mq = _mm_tn(h2, dq2, name="mm_dmq").reshape(N_CHIP, D // N_CHIP, D)
    dh2 = _mm_nt(dq2, g_mq, "rows", name="mm_dh2", out_dtype=F32)
    d_mk = _mm_tn(memn, dk2, name="mm_dmk").reshape(N_CHIP, D // N_CHIP, D)
    d_mv = _mm_tn(memn, dv2, name="mm_dmv").reshape(N_CHIP, D // N_CHIP, D)
    dmn_k = _mm_nt(dk2, g_mk, "rows", name="mm_dmemk", out_dtype=F32)
    dmn_v = _mm_nt(dv2, g_mv, "rows", name="mm_dmemv", out_dtype=F32)
    dg_mem_kv = _gain_grad_call(mems, sp["g_mem_kv"], dmn_k, dmn_v, name="gain_mem_kv")
    dx1, dz, dg_mem_pre, dg_mix_post = _bwd_mid_call(dx2, x1, dh2, sp["g_mem_pre"], z, sp["g_mix_post"], name="bwd_mem")
    d_out = _mm_tn(yn, dz, name="mm_dout").reshape(N_CHIP, D // N_CHIP, D)
    late = ["w_mo", "w_mq", "w_mk", "w_mv", "w_out"]
    d_late = [d_mo, d_mq, d_mk, d_mv, d_out]
    dyn, r1_late = _mm_nt(dz, g_out, "rows", name="mm_dyn", out_dtype=F32, task=_rs_pair_task(d_late))
    p_late = [pair_add(k, d, r1) for k, d, r1 in zip(late, d_late, r1_late)]
    dof, doc, delta, dg_fox, dg_chk = _mix_norm_bwd_call(dyn, yf, yc, sp["g_fox_out"], sp["g_chk_out"], name="mix_norm_bwd")
    (dqf, dkf, dvf, dct, dcq), r2_ff = _fox_bwd_call(proj, dof, lse, delta, c_rep, c_t, name="fox_bwd",
                                                      task=_rs_chip_task([p_ff2, p_ff1]))
    (dqc, dkc, dvc, dgrev), r2_late = _chk_bwd_call(proj, doc, bias, name="chk_bwd", task=_rs_chip_task(p_late))
    first = ["w_ff2", "w_ff1"] + late
    f_first = [_chip_sum_call(p, r, place_arr, name="rs_chip_sum_" + k)
               for k, p, r in zip(first, [p_ff2, p_ff1] + p_late, r2_ff + r2_late)]
    dc8 = dct[:, :, 0:2, :].transpose(0, 2, 1, 3).reshape(8, S) + dcq[:, ::HEAD].T
    dc_rows = jnp.concatenate([dc8, jnp.zeros((120, S), F32)], axis=0)
    dfl, db_fgt = _fox_gate_bwd_call(dc_rows, fl_raw, b_pad, name="fox_gate_bwd")
    dproj = jnp.concatenate([dqf, dkf, dvf, dqc, dkc, dvc, dfl], axis=1)
    d_all_t, g_first = _mm_tn(dproj, h1, name="mm_dwin", tk=640, tn=1024, task=_rs_gather_task(f_first))
    grads = dict(zip(first, g_first))
    d_in = _reorder_rows_call(d_all_t, False, name="d_in_rows").reshape(N_CHIP, IN_PAD, D)
    delta_w, new_m, new_v = {}, {}, {}

    def adamw_items(names):
        return [(w[k][0], grads[k], m[k][0], v[k][0]) for k in names]

    upd_late, (r1,) = _adamw_call(adamw_items(late), name="adamw_late", tm=64, task=_rs_pair_task([d_in]))
    p_in = pair_add("w_in", d_in, r1)
    dh1, (r2_in,) = _mm_nn(dproj, w_all_t, "plain", name="mm_dh1", out_dtype=F32, tm=1024,
                           task=_rs_chip_task([p_in]))
    f_in = _chip_sum_call(p_in, r2_in, place_arr, name="rs_chip_sum_w_in")
    upd_ff = _adamw_call(adamw_items(first[:2]), name="adamw_ff")
    for k, res in zip(late + first[:2], upd_late + upd_ff):
        grads[k], delta_w[k], new_m[k], new_v[k] = res
    grad_x, dg_mix_pre = _bwd_last_call(dx1, xs, dh1, sp["g_mix_pre"], name="bwd_mix")

    small_g = {"g_mix_pre": dg_mix_pre, "g_mix_post": dg_mix_post, "g_mem_kv": dg_mem_kv, "g_mem_pre": dg_mem_pre,
               "g_mem_post": dg_mem_post, "g_ff_pre": dg_ff_pre, "g_ff_post": dg_ff_post, "g_fox_out": dg_fox,
               "g_chk_out": dg_chk, "b_fgt": db_fgt,
               "rel_bias": _g_to_rel_table(dgrev[:, 0:2, :].reshape(8, ROLL_W))}
    small, loss_out, (g_w_in,) = _small_call(
        small_g, sp, {k: rows(m, k) for k in SMALL}, {k: rows(v, k) for k in SMALL}, loss_blk,
        _rs_gather_task([f_in]), name="small_allreduce_adamw")
    loss = loss_out[0, 0]
    res = _adamw_cols_call(w_in_t, g_w_in, m_in_t, v_in_t, name="adamw_w_in")
    grads["w_in"], delta_w["w_in"], new_m["w_in"], new_v["w_in"] = (a.T for a in res)
    for k in SMALL:
        vals = small[k]
        if k == "rel_bias":
            vals = tuple(a[None] for a in vals)
        grads[k], delta_w[k], new_m[k], new_v[k] = vals

    def out(d, k):
        return d[k][None] if k in BIG else d[k]

    return (loss, grad_x[None], *[out(grads, k) for k in WEIGHTS], *[out(delta_w, k) for k in WEIGHTS],
            *[out(new_m, k) for k in WEIGHTS], *[out(new_v, k) for k in WEIGHTS])
```

```python
import functools

import jax
import jax.numpy as jnp
from jax import lax
from jax.experimental import pallas as pl
from jax.experimental.pallas import tpu as pltpu

F32 = jnp.float32
BF16 = jnp.bfloat16

D = 1024
HEAD = 64
N_PAIR = 4
D_GRP = 512
CHUNK = 64
LEFT = 8
MAX_REL = 128
N_REL = 2 * MAX_REL + 1
N_MEM = 256
MEM_HEADS = 4
MEM_HD = 256
D_FF = 4096
D_IN = 3080
D_ALL = 3200
EPS = 1e-6
TQ = 256
WIN = (LEFT + TQ // CHUNK) * CHUNK
PADK = LEFT * CHUNK
ROLL_W = 1024
NEG = -1e30
N_CHIP = 4
VMEM_LIMIT = 48 * 1024 * 1024

ADAM_LR = 0.001
ADAM_B1 = 0.9
ADAM_B2 = 0.999
ADAM_EPS = 1e-08
ADAM_WD = 0.01
ADAM_STEP = 10

MESH = pl.DeviceIdType.MESH


def _cparams():
    return pltpu.CompilerParams(vmem_limit_bytes=VMEM_LIMIT)


ANY = pl.BlockSpec(memory_space=pl.ANY)


class _Task:
    def __init__(self, arrays, out_shapes, sems, issue, drain, aliases=None):
        self.arrays, self.out_shapes, self.sems = list(arrays), list(out_shapes), list(sems)
        self.issue, self.drain, self.aliases = issue, drain, dict(aliases or {})


def _merge_tasks(tasks):
    tasks = [t for t in tasks if t is not None]
    if len(tasks) == 1:
        return tasks[0]
    cuts, a, o, s = [], 0, 0, 0
    aliases = {}
    for t in tasks:
        cuts.append((a, o, s))
        aliases.update({a + i: o + j for i, j in t.aliases.items()})
        a, o, s = a + len(t.arrays), o + len(t.out_shapes), s + len(t.sems)

    def part(fn_name):
        def run(ins, outs, sems):
            for t, (a0, o0, s0) in zip(tasks, cuts):
                getattr(t, fn_name)(ins[a0:a0 + len(t.arrays)], outs[o0:o0 + len(t.out_shapes)],
                                    sems[s0:s0 + len(t.sems)])
        return run

    return _Task([x for t in tasks for x in t.arrays], [x for t in tasks for x in t.out_shapes],
                 [x for t in tasks for x in t.sems], part("issue"), part("drain"), aliases)


def _pallas(body, *, grid, in_specs, out_specs, out_shape, name, scratch_shapes=(), task=None):
    if task is None:
        return pl.pallas_call(body, grid=grid, in_specs=list(in_specs), out_specs=out_specs, out_shape=out_shape,
                              scratch_shapes=list(scratch_shapes), name=name, compiler_params=_cparams())
    single = not isinstance(out_shape, (tuple, list))
    o_shapes = [out_shape] if single else list(out_shape)
    o_specs = [out_specs] if single else list(out_specs)
    n_in, n_out, n_scr = len(in_specs), len(o_shapes), len(scratch_shapes)
    t_in, t_out = len(task.arrays), len(task.out_shapes)

    def carried(*refs):
        cut = [n_in, t_in, n_out, t_out, n_scr]
        parts, p = [], 0
        for c in cut:
            parts.append(refs[p:p + c])
            p += c
        ins, tins, outs, touts, scr = parts
        tsems = refs[p:]
        ids = [pl.program_id(a) for a in range(len(grid))]
        first = functools.reduce(jnp.logical_and, [i == 0 for i in ids])
        last = functools.reduce(jnp.logical_and, [i == g - 1 for i, g in zip(ids, grid)])

        @pl.when(first)
        def _():
            task.issue(tins, touts, tsems)
        body(*ins, *outs, *scr)

        @pl.when(last)
        def _():
            task.drain(tins, touts, tsems)

    call = pl.pallas_call(
        carried, grid=grid, in_specs=list(in_specs) + [ANY] * t_in, out_specs=o_specs + [ANY] * t_out,
        out_shape=o_shapes + list(task.out_shapes), scratch_shapes=list(scratch_shapes) + list(task.sems),
        input_output_aliases={n_in + i: n_out + j for i, j in task.aliases.items()},
        name=name, compiler_params=_cparams())

    def run(*args):
        res = call(*args, *task.arrays)
        outs = res[:n_out]
        return (outs[0] if single else tuple(outs)), list(res[n_out:])

    return run


def _comm_call(task, *, name):
    t_in, t_out = len(task.arrays), len(task.out_shapes)

    def body(*refs):
        tins, touts, tsems = refs[:t_in], refs[t_in:t_in + t_out], refs[t_in + t_out:]
        task.issue(tins, touts, tsems)
        task.drain(tins, touts, tsems)

    return pl.pallas_call(
        body, in_specs=[ANY] * t_in, out_specs=[ANY] * t_out, out_shape=list(task.out_shapes),
        scratch_shapes=list(task.sems), input_output_aliases=dict(task.aliases), name=name,
    )(*task.arrays)


def _dot(a, b):
    return jnp.dot(a, b, preferred_element_type=F32)


def _dot_nt(a, b):
    return lax.dot_general(a, b, (((1,), (1,)), ((), ())), preferred_element_type=F32)


def _dot_tn(a, b):
    return lax.dot_general(a, b, (((0,), (0,)), ((), ())), preferred_element_type=F32)


def _split3(x):
    hi = x.astype(BF16)
    r1 = x - hi.astype(F32)
    mid = r1.astype(BF16)
    lo = (r1 - mid.astype(F32)).astype(BF16)
    return hi, mid, lo


def _dot3(x, m01):
    hi, mid, lo = _split3(x)
    return _dot(hi, m01) + _dot(mid, m01) + _dot(lo, m01)


def _dot3_l(m01, x):
    hi, mid, lo = _split3(x)
    return _dot(m01, hi) + _dot(m01, mid) + _dot(m01, lo)


def _mm_nn(a, b, kind, *, name, out_dtype=BF16, tm=2048, tn=512, epi=None, task=None):
    M, K = a.shape
    if kind == "plain":
        N = b.shape[1]
        b_spec = pl.BlockSpec((K, tn), lambda m, n: (0, n))
    elif kind == "rows":
        N = b.shape[2]
        b_spec = pl.BlockSpec((N_CHIP, K // N_CHIP, tn), lambda m, n: (0, 0, n))
    else:
        nq = b.shape[2]
        N = N_CHIP * nq
        per = nq // tn
        b_spec = pl.BlockSpec((None, K, tn), lambda m, n: (n // per, 0, n % per))
    tm = min(tm, M)
    kq = K // N_CHIP

    def body(a_ref, b_ref, *o_refs):
        if kind == "rows":
            acc = _dot(a_ref[:, 0:kq], b_ref[0])
            for j in range(1, N_CHIP):
                acc += _dot(a_ref[:, j * kq:(j + 1) * kq], b_ref[j])
        else:
            acc = _dot(a_ref[...], b_ref[...])
        if epi == "relu2":
            r = jnp.maximum(acc, 0.0)
            o_refs[0][...] = (r * r).astype(BF16)
            o_refs[1][...] = r.astype(BF16)
        else:
            o_refs[0][...] = acc.astype(out_dtype)

    o_spec = pl.BlockSpec((tm, tn), lambda m, n: (m, n))
    if epi == "relu2":
        out_shape = (jax.ShapeDtypeStruct((M, N), BF16), jax.ShapeDtypeStruct((M, N), BF16))
        out_specs = (o_spec, o_spec)
    else:
        out_shape = jax.ShapeDtypeStruct((M, N), out_dtype)
        out_specs = o_spec
    return _pallas(
        body, grid=(M // tm, N // tn),
        in_specs=[pl.BlockSpec((tm, K), lambda m, n: (m, 0)), b_spec],
        out_specs=out_specs, out_shape=out_shape, name=name, task=task,
    )(a, b)


def _mm_nt(a, b, kind, *, name, out_dtype=BF16, tm=2048, tn=512, mul2r=None, task=None, rows=None):
    M, K = a.shape
    if kind == "plain":
        first, N = rows if rows is not None else (0, b.shape[0])
        n0 = first // tn
        b_spec = pl.BlockSpec((tn, K), lambda m, n: (n0 + n, 0))
    elif kind == "rows":
        nq = b.shape[1]
        N = N_CHIP * nq
        tn = min(tn, nq)
        per = nq // tn
        b_spec = pl.BlockSpec((None, tn, K), lambda m, n: (n // per, n % per, 0))
    else:
        N = b.shape[1]
        b_spec = pl.BlockSpec((N_CHIP, tn, K // N_CHIP), lambda m, n: (0, n, 0))
    tm = min(tm, M)
    kq = K // N_CHIP

    def body(a_ref, b_ref, *rest):
        o_ref = rest[-1]
        if kind == "cols":
            acc = _dot_nt(a_ref[:, 0:kq], b_ref[0])
            for j in range(1, N_CHIP):
                acc += _dot_nt(a_ref[:, j * kq:(j + 1) * kq], b_ref[j])
        else:
            acc = _dot_nt(a_ref[...], b_ref[...])
        if mul2r is not None:
            acc = acc * (2.0 * rest[0][...].astype(F32))
        o_ref[...] = acc.astype(out_dtype)

    in_specs = [pl.BlockSpec((tm, K), lambda m, n: (m, 0)), b_spec]
    args = [a, b]
    if mul2r is not None:
        in_specs.append(pl.BlockSpec((tm, tn), lambda m, n: (m, n)))
        args.append(mul2r)
    return _pallas(
        body, grid=(M // tm, N // tn), in_specs=in_specs,
        out_specs=pl.BlockSpec((tm, tn), lambda m, n: (m, n)),
        out_shape=jax.ShapeDtypeStruct((M, N), out_dtype), name=name, task=task,
    )(*args)


def _mm_tn(a, b, *, name, out_dtype=BF16, tk=1024, tn=512, cols4=False, task=None):
    M, K1 = a.shape
    N = b.shape[1]
    tk = min(tk, K1)
    tn = min(tn, N)

    def body(a_ref, b_ref, o_ref):
        o_ref[...] = _dot_tn(a_ref[...], b_ref[...]).astype(out_dtype)

    if cols4:
        per = (N // N_CHIP) // tn
        out_shape = jax.ShapeDtypeStruct((N_CHIP, K1, N // N_CHIP), out_dtype)
        o_spec = pl.BlockSpec((None, tk, tn), lambda k, n: (n // per, k, n % per))
    else:
        out_shape = jax.ShapeDtypeStruct((K1, N), out_dtype)
        o_spec = pl.BlockSpec((tk, tn), lambda k, n: (k, n))
    return _pallas(
        body, grid=(K1 // tk, N // tn),
        in_specs=[pl.BlockSpec((M, tk), lambda k, n: (0, k)), pl.BlockSpec((M, tn), lambda k, n: (0, n))],
        out_specs=o_spec, out_shape=out_shape, name=name, task=task,
    )(a, b)


def _rms(x, g):
    r = lax.rsqrt(jnp.mean(x * x, axis=-1, keepdims=True) + EPS)
    return x * r * g


def _rms_bwd(x, g, dy):
    r = lax.rsqrt(jnp.mean(x * x, axis=-1, keepdims=True) + EPS)
    xh = x * r
    dg = jnp.sum(dy * xh, axis=0, keepdims=True)
    dxh = dy * g
    dx = r * (dxh - xh * jnp.mean(dxh * xh, axis=-1, keepdims=True))
    return dx, dg


def _row_spec(tm, n):
    return pl.BlockSpec((tm, n), lambda i: (i, 0))


def _vec_spec(n):
    return pl.BlockSpec((1, n), lambda i: (0, 0))


def _acc_spec(n):
    return pl.BlockSpec((8, n), lambda i: (0, 0))


def _acc_add(ref, row, i):
    @pl.when(i == 0)
    def _():
        ref[...] = jnp.zeros_like(ref)
    ref[0:1, :] += row


def _rms_fwd_call(x, g, *, name, tm=256, task=None):
    M, n = x.shape
    tm = min(tm, M)

    def body(x_ref, g_ref, h_ref):
        h_ref[...] = _rms(x_ref[...], g_ref[...]).astype(BF16)

    return _pallas(
        body, grid=(M // tm,), in_specs=[_row_spec(tm, n), _vec_spec(n)], out_specs=_row_spec(tm, n),
        out_shape=jax.ShapeDtypeStruct((M, n), BF16), name=name, task=task,
    )(x, g)


def _post_pre_call(xres, z, g_post, g_pre, *, name, tm=256):
    M, n = xres.shape

    def body(x_ref, z_ref, gp_ref, gn_ref, xo_ref, h_ref):
        xn = x_ref[...] + _rms(z_ref[...], gp_ref[...])
        xo_ref[...] = xn
        h_ref[...] = _rms(xn, gn_ref[...]).astype(BF16)

    return pl.pallas_call(
        body, grid=(M // tm,),
        in_specs=[_row_spec(tm, n), _row_spec(tm, n), _vec_spec(n), _vec_spec(n)],
        out_specs=(_row_spec(tm, n), _row_spec(tm, n)),
        out_shape=(jax.ShapeDtypeStruct((M, n), F32), jax.ShapeDtypeStruct((M, n), BF16)),
        name=name, compiler_params=_cparams(),
    )(xres, z, g_post, g_pre)


def _final_call(x2, y3, g_post, target, *, name, tm=256):
    M, n = x2.shape

    def body(x_ref, y_ref, g_ref, t_ref, loss_ref, dx_ref, dy_ref, dg_ref):
        i = pl.program_id(0)
        y = y_ref[...]
        g = g_ref[...]
        diff = x_ref[...] + _rms(y, g) - t_ref[...]
        part = 0.5 * jnp.sum(jnp.sum(diff * diff, axis=1, keepdims=True), axis=0, keepdims=True) / n

        @pl.when(i == 0)
        def _():
            loss_ref[...] = jnp.zeros_like(loss_ref)
        loss_ref[...] += jnp.broadcast_to(part, loss_ref.shape)
        dx = diff / n
        dx_ref[...] = dx
        dy, dg = _rms_bwd(y, g, dx)
        dy_ref[...] = dy.astype(BF16)
        _acc_add(dg_ref, dg, i)

    return pl.pallas_call(
        body, grid=(M // tm,),
        in_specs=[_row_spec(tm, n), _row_spec(tm, n), _vec_spec(n), _row_spec(tm, n)],
        out_specs=(pl.BlockSpec((8, 128), lambda i: (0, 0)), _row_spec(tm, n), _row_spec(tm, n), _acc_spec(n)),
        out_shape=(jax.ShapeDtypeStruct((8, 128), F32), jax.ShapeDtypeStruct((M, n), F32),
                   jax.ShapeDtypeStruct((M, n), BF16), jax.ShapeDtypeStruct((8, n), F32)),
        name=name, compiler_params=_cparams(),
    )(x2, y3, g_post, target)


def _bwd_mid_call(dx_in, x, dh, g_pre, y, g_post, *, name, tm=256):
    M, n = x.shape

    def body(dxi_ref, x_ref, dh_ref, gpre_ref, y_ref, gpost_ref, dx_ref, dy_ref, dgpre_ref, dgpost_ref):
        i = pl.program_id(0)
        d1, dg1 = _rms_bwd(x_ref[...], gpre_ref[...], dh_ref[...])
        dx = dxi_ref[...] + d1
        dx_ref[...] = dx
        dy, dg2 = _rms_bwd(y_ref[...], gpost_ref[...], dx)
        dy_ref[...] = dy.astype(BF16)
        _acc_add(dgpre_ref, dg1, i)
        _acc_add(dgpost_ref, dg2, i)

    return pl.pallas_call(
        body, grid=(M // tm,),
        in_specs=[_row_spec(tm, n), _row_spec(tm, n), _row_spec(tm, n), _vec_spec(n), _row_spec(tm, n), _vec_spec(n)],
        out_specs=(_row_spec(tm, n), _row_spec(tm, n), _acc_spec(n), _acc_spec(n)),
        out_shape=(jax.ShapeDtypeStruct((M, n), F32), jax.ShapeDtypeStruct((M, n), BF16),
                   jax.ShapeDtypeStruct((8, n), F32), jax.ShapeDtypeStruct((8, n), F32)),
        name=name, compiler_params=_cparams(),
    )(dx_in, x, dh, g_pre, y, g_post)


def _bwd_last_call(dx_in, x, dh, g_pre, *, name, tm=256, task=None):
    M, n = x.shape

    def body(dxi_ref, x_ref, dh_ref, g_ref, dx_ref, dg_ref):
        i = pl.program_id(0)
        d1, dg1 = _rms_bwd(x_ref[...], g_ref[...], dh_ref[...])
        dx_ref[...] = dxi_ref[...] + d1
        _acc_add(dg_ref, dg1, i)

    return _pallas(
        body, grid=(M // tm,),
        in_specs=[_row_spec(tm, n), _row_spec(tm, n), _row_spec(tm, n), _vec_spec(n)],
        out_specs=(_row_spec(tm, n), _acc_spec(n)),
        out_shape=(jax.ShapeDtypeStruct((M, n), F32), jax.ShapeDtypeStruct((8, n), F32)),
        name=name, task=task,
    )(dx_in, x, dh, g_pre)


def _gain_grad_call(x, g, dy_a, dy_b, *, name):
    M, n = x.shape

    def body(x_ref, g_ref, a_ref, b_ref, dg_ref):
        _, dg = _rms_bwd(x_ref[...], g_ref[...], a_ref[...] + b_ref[...])
        dg_ref[...] = jnp.zeros_like(dg_ref)
        dg_ref[0:1, :] = dg

    return pl.pallas_call(
        body, grid=(1,),
        in_specs=[_row_spec(M, n), _vec_spec(n), _row_spec(M, n), _row_spec(M, n)],
        out_specs=_acc_spec(n), out_shape=jax.ShapeDtypeStruct((8, n), F32),
        name=name, compiler_params=_cparams(),
    )(x, g, dy_a, dy_b)


def _head_group_matrix():
    a = lax.broadcasted_iota(jnp.int32, (D_GRP, D_GRP), 0) // HEAD
    b = lax.broadcasted_iota(jnp.int32, (D_GRP, D_GRP), 1) // HEAD
    return jnp.where(a == b, 1.0, 0.0).astype(BF16)


def _mix_norm_fwd_call(yf, yc, gf, gc, *, name, tm=256):
    M = yf.shape[0]

    def body(yf_ref, yc_ref, gf_ref, gc_ref, o_ref):
        o_ref[:, 0:D_GRP] = _rms(yf_ref[...], gf_ref[...]).astype(BF16)
        o_ref[:, D_GRP:D] = _rms(yc_ref[...], gc_ref[...]).astype(BF16)

    return pl.pallas_call(
        body, grid=(M // tm,),
        in_specs=[_row_spec(tm, D_GRP), _row_spec(tm, D_GRP), _vec_spec(D_GRP), _vec_spec(D_GRP)],
        out_specs=_row_spec(tm, D), out_shape=jax.ShapeDtypeStruct((M, D), BF16),
        name=name, compiler_params=_cparams(),
    )(yf, yc, gf, gc)


def _mix_norm_bwd_call(dyn, yf, yc, gf, gc, *, name, tm=256):
    M = yf.shape[0]

    def body(dyn_ref, yf_ref, yc_ref, gf_ref, gc_ref, dof_ref, doc_ref, delta_ref, dgf_ref, dgc_ref):
        i = pl.program_id(0)
        yf_ = yf_ref[...]
        dof, dgf = _rms_bwd(yf_, gf_ref[...], dyn_ref[:, 0:D_GRP])
        doc, dgc = _rms_bwd(yc_ref[...], gc_ref[...], dyn_ref[:, D_GRP:D])
        dof_b = dof.astype(BF16)
        dof_ref[...] = dof_b
        doc_ref[...] = doc.astype(BF16)
        prod = dof_b.astype(F32) * yf_
        hi = prod.astype(BF16)
        lo = (prod - hi.astype(F32)).astype(BF16)
        grp = _head_group_matrix()
        delta_ref[...] = _dot(hi, grp) + _dot(lo, grp)
        _acc_add(dgf_ref, dgf, i)
        _acc_add(dgc_ref, dgc, i)

    return pl.pallas_call(
        body, grid=(M // tm,),
        in_specs=[_row_spec(tm, D), _row_spec(tm, D_GRP), _row_spec(tm, D_GRP), _vec_spec(D_GRP), _vec_spec(D_GRP)],
        out_specs=(_row_spec(tm, D_GRP), _row_spec(tm, D_GRP), _row_spec(tm, D_GRP), _acc_spec(D_GRP), _acc_spec(D_GRP)),
        out_shape=(jax.ShapeDtypeStruct((M, D_GRP), BF16), jax.ShapeDtypeStruct((M, D_GRP), BF16),
                   jax.ShapeDtypeStruct((M, D_GRP), F32), jax.ShapeDtypeStruct((8, D_GRP), F32),
                   jax.ShapeDtypeStruct((8, D_GRP), F32)),
        name=name, compiler_params=_cparams(),
    )(dyn, yf, yc, gf, gc)


def _tri(n, lower_incl):
    a = lax.broadcasted_iota(jnp.int32, (n, n), 0)
    b = lax.broadcasted_iota(jnp.int32, (n, n), 1)
    return jnp.where(a >= b, 1.0, 0.0).astype(BF16) if lower_incl else jnp.where(a <= b, 1.0, 0.0).astype(BF16)


def _fox_prep_call(fl_raw, b_pad, *, name):
    S = fl_raw.shape[0]
    nb = S // TQ

    def body(fl_ref, b_ref, crep_ref, ct_ref, carry_ref):
        i = pl.program_id(0)

        @pl.when(i == 0)
        def _():
            carry_ref[...] = jnp.zeros_like(carry_ref)
        logf = jax.nn.log_sigmoid(fl_ref[...] + b_ref[...])
        cb = _dot3_l(_tri(TQ, True), logf) + carry_ref[0:1, :]
        carry_ref[0:1, :] = cb[TQ - 1:TQ, :]
        a = lax.broadcasted_iota(jnp.int32, (128, D_GRP), 0)
        b = lax.broadcasted_iota(jnp.int32, (128, D_GRP), 1) // HEAD
        expand = jnp.where(a == b, 1.0, 0.0).astype(BF16)
        crep = _dot3(cb, expand)
        crep_ref[...] = crep
        ct_ref[...] = crep.T

    return pl.pallas_call(
        body, grid=(nb,),
        in_specs=[_row_spec(TQ, 128), _vec_spec(128)],
        out_specs=(_row_spec(TQ, D_GRP), pl.BlockSpec((None, D_GRP, TQ), lambda i: (i, 0, 0))),
        out_shape=(jax.ShapeDtypeStruct((S, D_GRP), F32), jax.ShapeDtypeStruct((nb, D_GRP, TQ), F32)),
        scratch_shapes=[pltpu.VMEM((8, 128), F32)],
        name=name, compiler_params=_cparams(),
    )(fl_raw, b_pad)


def _lane_masks():
    lane = lax.broadcasted_iota(jnp.int32, (1, 128), 1)
    return lane < HEAD, lane >= HEAD


def _fox_fwd_call(proj, c_rep, c_t, *, name, task=None):
    S = proj.shape[0]
    nq = S // TQ
    scale = HEAD ** -0.5

    def body(q_ref, k_ref, v_ref, c_ref, ct_ref, o_ref, lse_ref):
        i = pl.program_id(1)
        m_lo, m_hi = _lane_masks()
        masks = (m_lo, m_hi)
        q = q_ref[...]
        qm = [jnp.where(mk, q, jnp.zeros_like(q)) for mk in masks]
        cq = c_ref[...]
        cqh = [cq[:, 0:1], cq[:, HEAD:HEAD + 1]]
        row = lax.broadcasted_iota(jnp.int32, (TQ, TQ), 0)
        col = lax.broadcasted_iota(jnp.int32, (TQ, TQ), 1)

        def step(j, carry, masked):
            ms, ls, acc = carry
            start = pl.multiple_of(j * TQ, TQ)
            k = k_ref[pl.ds(start, TQ), :]
            v = v_ref[pl.ds(start, TQ), :]
            ct = ct_ref[j]
            new_m, new_l, pv, alpha_l = [], [], [], []
            for h in range(2):
                s = _dot_nt(qm[h], k) * scale + (cqh[h] - ct[HEAD * h:HEAD * h + 1, :])
                if masked:
                    s = jnp.where(row >= col, s, NEG)
                mn = jnp.maximum(ms[h], jnp.max(s, axis=1, keepdims=True))
                alpha = jnp.exp(ms[h] - mn)
                p = jnp.exp(s - mn)
                new_l.append(alpha * ls[h] + jnp.sum(p, axis=1, keepdims=True))
                new_m.append(mn)
                alpha_l.append(alpha)
                pv.append(_dot(p.astype(BF16), jnp.where(masks[h], v, jnp.zeros_like(v))))
            alpha_lane = jnp.where(m_lo, alpha_l[0], alpha_l[1])
            acc = acc * alpha_lane + pv[0] + pv[1]
            return (tuple(new_m), tuple(new_l), acc)

        init = ((jnp.full((TQ, 1), NEG, F32),) * 2, (jnp.zeros((TQ, 1), F32),) * 2, jnp.zeros((TQ, 128), F32))
        carry = lax.fori_loop(0, i, lambda j, c: step(j, c, False), init)
        ms, ls, acc = step(i, carry, True)
        l_lane = jnp.where(m_lo, ls[0], ls[1])
        o_ref[...] = acc / l_lane
        lse_ref[...] = jnp.where(m_lo, ms[0] + jnp.log(ls[0]), ms[1] + jnp.log(ls[1]))

    return _pallas(
        body, grid=(N_PAIR, nq),
        in_specs=[pl.BlockSpec((TQ, 128), lambda p, i: (i, p)),
                  pl.BlockSpec((S, 128), lambda p, i: (0, N_PAIR + p)),
                  pl.BlockSpec((S, 128), lambda p, i: (0, 2 * N_PAIR + p)),
                  pl.BlockSpec((TQ, 128), lambda p, i: (i, p)),
                  pl.BlockSpec((nq, 128, TQ), lambda p, i: (0, p, 0))],
        out_specs=(pl.BlockSpec((TQ, 128), lambda p, i: (i, p)), pl.BlockSpec((TQ, 128), lambda p, i: (i, p))),
        out_shape=(jax.ShapeDtypeStruct((S, D_GRP), F32), jax.ShapeDtypeStruct((S, D_GRP), F32)),
        name=name, task=task,
    )(proj, proj, proj, c_rep, c_t)


def _fox_bwd_call(proj, do, lse_rep, delta_rep, c_rep, c_t, *, name, task=None):
    S = proj.shape[0]
    nq = S // TQ
    scale = HEAD ** -0.5

    def body(q_ref, k_ref, v_ref, do_ref, lse_ref, dl_ref, c_ref, ct_ref, dq_ref, dk_ref, dv_ref, dct_ref, dcq_ref, dqa_ref):
        j = pl.program_id(1)
        m_lo, m_hi = _lane_masks()
        masks = (m_lo, m_hi)

        @pl.when(j == 0)
        def _():
            dqa_ref[...] = jnp.zeros_like(dqa_ref)
            dcq_ref[...] = jnp.zeros_like(dcq_ref)
        k = k_ref[...]
        v = v_ref[...]
        km = [jnp.where(mk, k, jnp.zeros_like(k)) for mk in masks]
        ct = ct_ref[...]
        row = lax.broadcasted_iota(jnp.int32, (TQ, TQ), 0)
        col = lax.broadcasted_iota(jnp.int32, (TQ, TQ), 1)

        def step(i, carry, masked):
            dk, dv, dcs = carry
            start = pl.multiple_of(i * TQ, TQ)
            q = q_ref[pl.ds(start, TQ), :]
            do = do_ref[pl.ds(start, TQ), :]
            lse = lse_ref[pl.ds(start, TQ), :]
            dl = dl_ref[pl.ds(start, TQ), :]
            cq = c_ref[pl.ds(start, TQ), :]
            dq = jnp.zeros((TQ, 128), F32)
            new_dcs = []
            rows = []
            for h in range(2):
                lo = HEAD * h
                qm = jnp.where(masks[h], q, jnp.zeros_like(q))
                dom = jnp.where(masks[h], do, jnp.zeros_like(do))
                s = _dot_nt(qm, k) * scale + (cq[:, lo:lo + 1] - ct[lo:lo + 1, :])
                p = jnp.exp(s - lse[:, lo:lo + 1])
                if masked:
                    p = jnp.where(row >= col, p, 0.0)
                dp = _dot_nt(dom, v)
                ds = p * (dp - dl[:, lo:lo + 1])
                new_dcs.append(dcs[h] + jnp.sum(ds, axis=0, keepdims=True))
                rows.append(jnp.sum(ds, axis=1, keepdims=True))
                dsb = (ds * scale).astype(BF16)
                dv = dv + _dot_tn(p.astype(BF16), dom)
                dk = dk + _dot_tn(dsb, qm)
                dq = dq + _dot(dsb, km[h])
            dqa_ref[pl.ds(start, TQ), :] += dq
            dcq_ref[pl.ds(start, TQ), :] += jnp.where(m_lo, rows[0], rows[1])
            return (dk, dv, tuple(new_dcs))

        init = (jnp.zeros((TQ, 128), F32), jnp.zeros((TQ, 128), F32), (jnp.zeros((1, TQ), F32),) * 2)
        carry = step(j, init, True)
        dk, dv, dcs = lax.fori_loop(j + 1, nq, lambda i, c: step(i, c, False), carry)
        dk_ref[...] = dk.astype(BF16)
        dv_ref[...] = dv.astype(BF16)
        dct_ref[...] = jnp.zeros_like(dct_ref)
        dct_ref[0:1, :] = -dcs[0]
        dct_ref[1:2, :] = -dcs[1]

        @pl.when(j == nq - 1)
        def _():
            dq_ref[...] = dqa_ref[...].astype(BF16)

    res = lambda p, j: (0, p)
    return _pallas(
        body, grid=(N_PAIR, nq), task=task,
        in_specs=[pl.BlockSpec((S, 128), res),
                  pl.BlockSpec((TQ, 128), lambda p, j: (j, N_PAIR + p)),
                  pl.BlockSpec((TQ, 128), lambda p, j: (j, 2 * N_PAIR + p)),
                  pl.BlockSpec((S, 128), res), pl.BlockSpec((S, 128), res), pl.BlockSpec((S, 128), res),
                  pl.BlockSpec((S, 128), res),
                  pl.BlockSpec((None, 128, TQ), lambda p, j: (j, p, 0))],
        out_specs=(pl.BlockSpec((S, 128), res),
                   pl.BlockSpec((TQ, 128), lambda p, j: (j, p)), pl.BlockSpec((TQ, 128), lambda p, j: (j, p)),
                   pl.BlockSpec((None, None, 8, TQ), lambda p, j: (p, j, 0, 0)),
                   pl.BlockSpec((S, 128), res)),
        out_shape=(jax.ShapeDtypeStruct((S, D_GRP), BF16), jax.ShapeDtypeStruct((S, D_GRP), BF16),
                   jax.ShapeDtypeStruct((S, D_GRP), BF16), jax.ShapeDtypeStruct((N_PAIR, nq, 8, TQ), F32),
                   jax.ShapeDtypeStruct((S, D_GRP), F32)),
        scratch_shapes=[pltpu.VMEM((S, 128), F32)],
        name=name,
    )(proj, proj, proj, do, lse_rep, delta_rep, c_rep, c_t)


def _fox_gate_bwd_call(dc_rows, fl_raw, b_pad, *, name):
    S = fl_raw.shape[0]
    nb = S // TQ

    def body(dc_ref, fl_ref, b_ref, dfl_ref, db_ref, carry_ref):
        i = pl.program_id(0)

        @pl.when(i == 0)
        def _():
            carry_ref[...] = jnp.zeros_like(carry_ref)
        rc = _dot3(dc_ref[...], _tri(TQ, True)) + carry_ref[:, 0:1]
        carry_ref[...] = jnp.broadcast_to(rc[:, 0:1], carry_ref.shape)
        fl = fl_ref[...] + b_ref[...]
        dfl = rc.T * jax.nn.sigmoid(-fl)
        dfl_ref[...] = dfl.astype(BF16)
        _acc_add(db_ref, jnp.sum(dfl, axis=0, keepdims=True), i)

    rev = lambda i: (nb - 1 - i, 0)
    return pl.pallas_call(
        body, grid=(nb,),
        in_specs=[pl.BlockSpec((128, TQ), lambda i: (0, nb - 1 - i)), pl.BlockSpec((TQ, 128), rev), _vec_spec(128)],
        out_specs=(pl.BlockSpec((TQ, 128), rev), _acc_spec(128)),
        out_shape=(jax.ShapeDtypeStruct((S, 128), BF16), jax.ShapeDtypeStruct((8, 128), F32)),
        scratch_shapes=[pltpu.VMEM((128, 128), F32)],
        name=name, compiler_params=_cparams(),
    )(dc_rows, fl_raw, b_pad)


def _chk_bias_call(g_rev, *, name):
    def body(g_ref, o_ref):
        x = jnp.broadcast_to(g_ref[...], (TQ, ROLL_W))
        rolled = pltpu.roll(x, ROLL_W - (TQ - 1), 1, stride=1, stride_axis=0)
        qc = lax.broadcasted_iota(jnp.int32, (TQ, WIN), 0) // CHUNK
        kc = lax.broadcasted_iota(jnp.int32, (TQ, WIN), 1) // CHUNK
        band = (kc >= qc) & (kc <= qc + LEFT)
        o_ref[...] = jnp.where(band, rolled[:, 0:WIN], NEG)

    return pl.pallas_call(
        body, grid=(8,),
        in_specs=[pl.BlockSpec((None, 1, ROLL_W), lambda h: (h, 0, 0))],
        out_specs=pl.BlockSpec((None, TQ, WIN), lambda h: (h, 0, 0)),
        out_shape=jax.ShapeDtypeStruct((8, TQ, WIN), F32), name=name, compiler_params=_cparams(),
    )(g_rev.reshape(8, 1, ROLL_W))


def _chk_scores(i, qm, kwin, bias, scale):
    s = _dot_nt(qm, kwin) * scale + bias
    kc = lax.broadcasted_iota(jnp.int32, (TQ, WIN), 1) // CHUNK
    return jnp.where(kc + i * (TQ // CHUNK) >= LEFT, s, NEG)


def _chk_fwd_call(proj, bias, *, name, task=None):
    S = proj.shape[0]
    nq = S // TQ
    scale = HEAD ** -0.5

    def body(q_ref, k_ref, v_ref, b_ref, o_ref, kp_ref, vp_ref):
        i = pl.program_id(1)

        @pl.when(i == 0)
        def _():
            kp_ref[0:PADK, :] = jnp.zeros((PADK, 128), BF16)
            vp_ref[0:PADK, :] = jnp.zeros((PADK, 128), BF16)
            kp_ref[PADK:PADK + S, :] = k_ref[...]
            vp_ref[PADK:PADK + S, :] = v_ref[...]
        masks = _lane_masks()
        q = q_ref[...]
        start = pl.multiple_of(i * TQ, TQ)
        kwin = kp_ref[pl.ds(start, WIN), :]
        vwin = vp_ref[pl.ds(start, WIN), :]
        out = jnp.zeros((TQ, 128), F32)
        for h in range(2):
            qm = jnp.where(masks[h], q, jnp.zeros_like(q))
            s = _chk_scores(i, qm, kwin, b_ref[h], scale)
            p = jnp.exp(s - jnp.max(s, axis=1, keepdims=True))
            p = p / jnp.sum(p, axis=1, keepdims=True)
            out = out + _dot(p.astype(BF16), jnp.where(masks[h], vwin, jnp.zeros_like(vwin)))
        o_ref[...] = out

    c0 = 3 * N_PAIR
    return _pallas(
        body, grid=(N_PAIR, nq), task=task,
        in_specs=[pl.BlockSpec((TQ, 128), lambda p, i: (i, c0 + p)),
                  pl.BlockSpec((S, 128), lambda p, i: (0, c0 + N_PAIR + p)),
                  pl.BlockSpec((S, 128), lambda p, i: (0, c0 + 2 * N_PAIR + p)),
                  pl.BlockSpec((2, TQ, WIN), lambda p, i: (p, 0, 0))],
        out_specs=pl.BlockSpec((TQ, 128), lambda p, i: (i, p)),
        out_shape=jax.ShapeDtypeStruct((S, D_GRP), F32),
        scratch_shapes=[pltpu.VMEM((S + PADK, 128), BF16), pltpu.VMEM((S + PADK, 128), BF16)],
        name=name,
    )(proj, proj, proj, bias)


def _chk_bwd_call(proj, do, bias, *, name, task=None):
    S = proj.shape[0]
    nq = S // TQ
    scale = HEAD ** -0.5

    def body(q_ref, k_ref, v_ref, do_ref, b_ref, dq_ref, dk_ref, dv_ref, dg_ref, kp_ref, vp_ref, dkp_ref, dvp_ref, db_ref):
        i = pl.program_id(1)

        @pl.when(i == 0)
        def _():
            kp_ref[0:PADK, :] = jnp.zeros((PADK, 128), BF16)
            vp_ref[0:PADK, :] = jnp.zeros((PADK, 128), BF16)
            kp_ref[PADK:PADK + S, :] = k_ref[...]
            vp_ref[PADK:PADK + S, :] = v_ref[...]
            dkp_ref[...] = jnp.zeros_like(dkp_ref)
            dvp_ref[...] = jnp.zeros_like(dvp_ref)
            db_ref[...] = jnp.zeros_like(db_ref)
        masks = _lane_masks()
        q = q_ref[...]
        dout = do_ref[...]
        start = pl.multiple_of(i * TQ, TQ)
        kwin = kp_ref[pl.ds(start, WIN), :]
        vwin = vp_ref[pl.ds(start, WIN), :]
        dq = jnp.zeros((TQ, 128), F32)
        dkw = jnp.zeros((WIN, 128), F32)
        dvw = jnp.zeros((WIN, 128), F32)
        for h in range(2):
            qm = jnp.where(masks[h], q, jnp.zeros_like(q))
            dom = jnp.where(masks[h], dout, jnp.zeros_like(dout))
            s = _chk_scores(i, qm, kwin, b_ref[h], scale)
            p = jnp.exp(s - jnp.max(s, axis=1, keepdims=True))
            p = p / jnp.sum(p, axis=1, keepdims=True)
            dp = _dot_nt(dom, vwin)
            ds = p * (dp - jnp.sum(p * dp, axis=1, keepdims=True))
            db_ref[h] += ds
            dsb = (ds * scale).astype(BF16)
            dq = dq + _dot(dsb, jnp.where(masks[h], kwin, jnp.zeros_like(kwin)))
            dkw = dkw + _dot_tn(dsb, qm)
            dvw = dvw + _dot_tn(p.astype(BF16), dom)
        dq_ref[...] = dq.astype(BF16)
        dkp_ref[pl.ds(start, WIN), :] += dkw
        dvp_ref[pl.ds(start, WIN), :] += dvw

        @pl.when(i == nq - 1)
        def _():
            dk_ref[...] = dkp_ref[PADK:PADK + S, :].astype(BF16)
            dv_ref[...] = dvp_ref[PADK:PADK + S, :].astype(BF16)
            a = lax.broadcasted_iota(jnp.int32, (TQ, TQ), 0)
            b = lax.broadcasted_iota(jnp.int32, (TQ, TQ), 1)
            flip = jnp.where(a + b == TQ - 1, 1.0, 0.0).astype(BF16)
            e = lax.broadcasted_iota(jnp.int32, (1, ROLL_W), 1)
            dg_ref[...] = jnp.zeros_like(dg_ref)
            for h in range(2):
                rev = _dot3_l(flip, db_ref[h])
                wide = jnp.concatenate([rev, jnp.zeros((TQ, ROLL_W - WIN), F32)], axis=1)
                diag = pltpu.roll(wide, 0, 1, stride=1, stride_axis=0)
                dg = jnp.sum(diag, axis=0, keepdims=True)
                lo = jnp.sum(jnp.where(e <= 639, dg, 0.0), axis=1, keepdims=True)
                hi = jnp.sum(jnp.where(e >= 895, dg, 0.0), axis=1, keepdims=True)
                dg_ref[h:h + 1, :] = jnp.where(e == 639, lo, jnp.where(e == 895, hi, dg))

    c0 = 3 * N_PAIR
    res = lambda p, i: (0, p)
    return _pallas(
        body, grid=(N_PAIR, nq), task=task,
        in_specs=[pl.BlockSpec((TQ, 128), lambda p, i: (i, c0 + p)),
                  pl.BlockSpec((S, 128), lambda p, i: (0, c0 + N_PAIR + p)),
                  pl.BlockSpec((S, 128), lambda p, i: (0, c0 + 2 * N_PAIR + p)),
                  pl.BlockSpec((TQ, 128), lambda p, i: (i, p)),
                  pl.BlockSpec((2, TQ, WIN), lambda p, i: (p, 0, 0))],
        out_specs=(pl.BlockSpec((TQ, 128), lambda p, i: (i, p)), pl.BlockSpec((S, 128), res),
                   pl.BlockSpec((S, 128), res), pl.BlockSpec((None, 8, ROLL_W), lambda p, i: (p, 0, 0))),
        out_shape=(jax.ShapeDtypeStruct((S, D_GRP), BF16), jax.ShapeDtypeStruct((S, D_GRP), BF16),
                   jax.ShapeDtypeStruct((S, D_GRP), BF16), jax.ShapeDtypeStruct((N_PAIR, 8, ROLL_W), F32)),
        scratch_shapes=[pltpu.VMEM((S + PADK, 128), BF16), pltpu.VMEM((S + PADK, 128), BF16),
                        pltpu.VMEM((S + PADK, 128), F32), pltpu.VMEM((S + PADK, 128), F32),
                        pltpu.VMEM((2, TQ, WIN), F32)],
        name=name,
    )(proj, proj, proj, do, bias)


def _mem_fwd_call(q, k, v, *, name, tq=512):
    S = q.shape[0]
    scale = MEM_HD ** -0.5

    def body(q_ref, k_ref, v_ref, o_ref):
        s = _dot_nt(q_ref[...], k_ref[...]) * scale
        p = jnp.exp(s - jnp.max(s, axis=1, keepdims=True))
        p = p / jnp.sum(p, axis=1, keepdims=True)
        o_ref[...] = _dot(p.astype(BF16), v_ref[...]).astype(BF16)

    return pl.pallas_call(
        body, grid=(MEM_HEADS, S // tq),
        in_specs=[pl.BlockSpec((tq, MEM_HD), lambda h, i: (i, h)),
                  pl.BlockSpec((N_MEM, MEM_HD), lambda h, i: (0, h)),
                  pl.BlockSpec((N_MEM, MEM_HD), lambda h, i: (0, h))],
        out_specs=pl.BlockSpec((tq, MEM_HD), lambda h, i: (i, h)),
        out_shape=jax.ShapeDtypeStruct((S, D), BF16), name=name, compiler_params=_cparams(),
    )(q, k, v)


def _mem_bwd_call(q, k, v, do, *, name, tq=512):
    S = q.shape[0]
    n = S // tq
    scale = MEM_HD ** -0.5

    def body(q_ref, k_ref, v_ref, do_ref, dq_ref, dk_ref, dv_ref, dka_ref, dva_ref):
        i = pl.program_id(1)

        @pl.when(i == 0)
        def _():
            dka_ref[...] = jnp.zeros_like(dka_ref)
            dva_ref[...] = jnp.zeros_like(dva_ref)
        qb = q_ref[...]
        kb = k_ref[...]
        dob = do_ref[...]
        s = _dot_nt(qb, kb) * scale
        p = jnp.exp(s - jnp.max(s, axis=1, keepdims=True))
        p = p / jnp.sum(p, axis=1, keepdims=True)
        dp = _dot_nt(dob, v_ref[...])
        ds = p * (dp - jnp.sum(p * dp, axis=1, keepdims=True))
        dsb = (ds * scale).astype(BF16)
        dq_ref[...] = _dot(dsb, kb).astype(BF16)
        dka_ref[...] += _dot_tn(dsb, qb)
        dva_ref[...] += _dot_tn(p.astype(BF16), dob)

        @pl.when(i == n - 1)
        def _():
            dk_ref[...] = dka_ref[...].astype(BF16)
            dv_ref[...] = dva_ref[...].astype(BF16)

    kv = pl.BlockSpec((N_MEM, MEM_HD), lambda h, i: (0, h))
    qs = pl.BlockSpec((tq, MEM_HD), lambda h, i: (i, h))
    return pl.pallas_call(
        body, grid=(MEM_HEADS, n), in_specs=[qs, kv, kv, qs], out_specs=(qs, kv, kv),
        out_shape=(jax.ShapeDtypeStruct((S, D), BF16), jax.ShapeDtypeStruct((N_MEM, D), BF16),
                   jax.ShapeDtypeStruct((N_MEM, D), BF16)),
        scratch_shapes=[pltpu.VMEM((N_MEM, MEM_HD), F32), pltpu.VMEM((N_MEM, MEM_HD), F32)],
        name=name, compiler_params=_cparams(),
    )(q, k, v, do)


def _rel_table_to_g(rel):
    return jnp.concatenate([
        jnp.broadcast_to(rel[:, N_REL - 1:N_REL], (8, 640)),
        rel[:, 1:N_REL - 1][:, ::-1],
        jnp.broadcast_to(rel[:, 0:1], (8, 129)),
    ], axis=1)


def _g_to_rel_table(dg):
    return dg[:, 639:896][:, ::-1]


def _place():
    x, y, c = lax.axis_index("x"), lax.axis_index("y"), lax.axis_index("c")
    others = [(1 - x, y), (x, 1 - y), (1 - x, 1 - y)]
    return x, y, c, others


def _half(c, rows):
    hr = rows // 2
    return pl.ds(pl.multiple_of(c * hr, 16), hr)


def _dma_sems(*shape):
    return pltpu.SemaphoreType.DMA(shape)


def _cast_slab_call(w, chip_arr, *, name, tm=256, pad_rows=0):
    rows, cols = w.shape
    if pad_rows:
        tm = rows
    tm = min(tm, rows)

    def body(chip_ref, w_ref, o_ref):
        o_ref[0:tm, :] = w_ref[...].astype(BF16)
        if pad_rows:
            o_ref[tm:tm + pad_rows, :] = jnp.zeros((pad_rows, cols), BF16)

    return pl.pallas_call(
        body,
        grid_spec=pltpu.PrefetchScalarGridSpec(
            num_scalar_prefetch=1, grid=(rows // tm,),
            in_specs=[pl.BlockSpec((tm, cols), lambda i, chip: (i, 0))],
            out_specs=pl.BlockSpec((None, tm + pad_rows, cols), lambda i, chip: (chip[0], i, 0))),
        out_shape=jax.ShapeDtypeStruct((N_CHIP, rows + pad_rows, cols), BF16), name=name,
        compiler_params=_cparams(),
    )(chip_arr, w)


def _ag_ici_task(gathered):
    n = len(gathered)

    def copies(ins, outs, sems):
        send_sems, recv_sems = sems
        x, y, c, others = _place()
        me = 2 * x + y
        for k in range(n):
            mine = _half(c, gathered[k].shape[1])
            for t, (ox, oy) in enumerate(others):
                yield [pltpu.make_async_remote_copy(
                    src_ref=ins[k].at[me, mine], dst_ref=outs[k].at[slab, mine],
                    send_sem=send_sems.at[k, t], recv_sem=recv_sems.at[k, t],
                    device_id=(ox, oy, c), device_id_type=MESH) for slab in (me, 2 * ox + oy)]

    def issue(ins, outs, sems):
        for outgoing, _ in copies(ins, outs, sems):
            outgoing.start()

    def drain(ins, outs, sems):
        for outgoing, incoming in copies(ins, outs, sems):
            incoming.wait_recv()
            outgoing.wait_send()

    return _Task(gathered, [jax.ShapeDtypeStruct(g.shape, g.dtype) for g in gathered],
                 [_dma_sems(n, 3), _dma_sems(n, 3)], issue, drain, aliases={k: k for k in range(n)})


def _ag_d2d_task(gathered):
    n = len(gathered)

    def copies(ins, outs, sems):
        send_sems, recv_sems = sems
        x, y, c, others = _place()
        for k in range(n):
            rows = gathered[k].shape[1]
            mine, theirs = _half(c, rows), _half(1 - c, rows)
            for t, (ox, oy) in enumerate(others):
                slab = 2 * ox + oy
                pair = [pltpu.make_async_remote_copy(
                    src_ref=ins[k].at[slab, half], dst_ref=outs[k].at[slab, half],
                    send_sem=send_sems.at[k, t], recv_sem=recv_sems.at[k, t],
                    device_id=(x, y, 1 - c), device_id_type=MESH) for half in (mine, theirs)]
                yield pair

    def issue(ins, outs, sems):
        for outgoing, _ in copies(ins, outs, sems):
            outgoing.start()

    def drain(ins, outs, sems):
        for outgoing, incoming in copies(ins, outs, sems):
            incoming.wait_recv()
            outgoing.wait_send()

    return _Task(gathered, [jax.ShapeDtypeStruct(g.shape, g.dtype) for g in gathered],
                 [_dma_sems(n, 3), _dma_sems(n, 3)], issue, drain, aliases={k: k for k in range(n)})


def _rs_pair_task(ds):
    n = len(ds)

    def copies(ins, outs, sems):
        send_sems, recv_sems = sems
        x, y, c, _ = _place()
        for k in range(n):
            yield pltpu.make_async_remote_copy(
                src_ref=ins[k].at[:, _half(1 - c, ds[k].shape[1])], dst_ref=outs[k],
                send_sem=send_sems.at[k], recv_sem=recv_sems.at[k],
                device_id=(x, y, 1 - c), device_id_type=MESH)

    def issue(ins, outs, sems):
        for cp in copies(ins, outs, sems):
            cp.start()

    def drain(ins, outs, sems):
        for cp in copies(ins, outs, sems):
            cp.wait()

    return _Task(ds, [jax.ShapeDtypeStruct((N_CHIP, d.shape[1] // 2, d.shape[2]), d.dtype) for d in ds],
                 [_dma_sems(n), _dma_sems(n)], issue, drain)


def _pair_add_call(d, r1, c_arr, *, name, tm=256):
    _, rows, cols = d.shape
    hr = rows // 2
    tm = tm if hr % tm == 0 else hr
    nb = hr // tm

    def body(c_ref, d_ref, r_ref, o_ref):
        o_ref[...] = (d_ref[...].astype(F32) + r_ref[...].astype(F32)).astype(BF16)

    return pl.pallas_call(
        body,
        grid_spec=pltpu.PrefetchScalarGridSpec(
            num_scalar_prefetch=1, grid=(N_CHIP, nb),
            in_specs=[pl.BlockSpec((None, tm, cols), lambda j, i, c: (j, c[0] * nb + i, 0)),
                      pl.BlockSpec((None, tm, cols), lambda j, i, c: (j, i, 0))],
            out_specs=pl.BlockSpec((None, tm, cols), lambda j, i, c: (j, i, 0))),
        out_shape=jax.ShapeDtypeStruct((N_CHIP, hr, cols), BF16), name=name, compiler_params=_cparams(),
    )(c_arr, d, r1)


def _rs_chip_task(ps):
    n = len(ps)

    def copies(ins, outs, sems):
        send_sems, recv_sems = sems
        x, y, c, others = _place()
        for k in range(n):
            for t, (ox, oy) in enumerate(others):
                yield pltpu.make_async_remote_copy(
                    src_ref=ins[k].at[2 * ox + oy], dst_ref=outs[k].at[t],
                    send_sem=send_sems.at[k, t], recv_sem=recv_sems.at[k, t],
                    device_id=(ox, oy, c), device_id_type=MESH)

    def issue(ins, outs, sems):
        for cp in copies(ins, outs, sems):
            cp.start()

    def drain(ins, outs, sems):
        for cp in copies(ins, outs, sems):
            cp.wait()

    return _Task(ps, [jax.ShapeDtypeStruct((3,) + p.shape[1:], p.dtype) for p in ps],
                 [_dma_sems(n, 3), _dma_sems(n, 3)], issue, drain)


def _chip_sum_call(p, r2, place_arr, *, name, tm=256):
    _, hr, cols = r2.shape
    tm = tm if hr % tm == 0 else hr
    nb = hr // tm

    def body(place_ref, p_ref, r_ref, o_ref):
        acc = p_ref[...].astype(F32)
        for j in range(3):
            acc = acc + r_ref[j].astype(F32)
        o_ref[...] = acc

    return pl.pallas_call(
        body,
        grid_spec=pltpu.PrefetchScalarGridSpec(
            num_scalar_prefetch=1, grid=(nb,),
            in_specs=[pl.BlockSpec((None, tm, cols), lambda i, pc: (pc[0], i, 0)),
                      pl.BlockSpec((3, tm, cols), lambda i, pc: (0, i, 0))],
            out_specs=pl.BlockSpec((tm, cols), lambda i, pc: (pc[1] * nb + i, 0))),
        out_shape=jax.ShapeDtypeStruct((2 * hr, cols), F32), name=name, compiler_params=_cparams(),
    )(place_arr, p, r2)


def _rs_gather_task(gs):
    n = len(gs)

    def copies(ins, outs, sems):
        send_sems, recv_sems = sems
        x, y, c, _ = _place()
        for k in range(n):
            rows = gs[k].shape[0]
            mine, theirs = _half(c, rows), _half(1 - c, rows)
            yield [pltpu.make_async_remote_copy(
                src_ref=ins[k].at[mine], dst_ref=outs[k].at[half],
                send_sem=send_sems.at[k], recv_sem=recv_sems.at[k],
                device_id=(x, y, 1 - c), device_id_type=MESH) for half in (mine, theirs)]

    def issue(ins, outs, sems):
        for outgoing, _ in copies(ins, outs, sems):
            outgoing.start()

    def drain(ins, outs, sems):
        for outgoing, incoming in copies(ins, outs, sems):
            incoming.wait_recv()
            outgoing.wait_send()

    return _Task(gs, [jax.ShapeDtypeStruct(g.shape, g.dtype) for g in gs],
                 [_dma_sems(n), _dma_sems(n)], issue, drain, aliases={k: k for k in range(n)})


def _adamw(w, g, m, v):
    m = ADAM_B1 * m + (1.0 - ADAM_B1) * g
    v = ADAM_B2 * v + (1.0 - ADAM_B2) * jnp.square(g)
    m_hat = m / (1.0 - ADAM_B1 ** ADAM_STEP)
    v_hat = v / (1.0 - ADAM_B2 ** ADAM_STEP)
    delta = -ADAM_LR * (m_hat / (jnp.sqrt(v_hat) + ADAM_EPS) + ADAM_WD * w)
    return delta, m, v


def _adamw_call(items, *, name, tm=256, task=None):
    n = len(items)
    cols = items[0][0].shape[1]
    tiles = [it[0].shape[0] // tm for it in items]
    steps = max(tiles)

    def body(*refs):
        i = pl.program_id(0)
        ins, outs = refs[:4 * n], refs[4 * n:]
        for k in range(n):
            def update(k=k):
                g = ins[4 * k + 1][...]
                res = _adamw(ins[4 * k][...], g, ins[4 * k + 2][...], ins[4 * k + 3][...])
                outs[4 * k][...] = g
                for j in range(3):
                    outs[4 * k + 1 + j][...] = res[j]
            if tiles[k] == steps:
                update()
            else:
                pl.when(i < tiles[k])(update)

    in_specs, out_specs, out_shape, args = [], [], [], []
    for it, t in zip(items, tiles):
        spec = pl.BlockSpec((tm, cols), lambda i, t=t: (jnp.minimum(i, t - 1), 0))
        in_specs += [spec] * 4
        out_specs += [spec] * 4
        out_shape += [jax.ShapeDtypeStruct(it[0].shape, F32)] * 4
        args += list(it)
    res = _pallas(body, grid=(steps,), in_specs=in_specs, out_specs=out_specs, out_shape=out_shape,
                  name=name, task=task)(*args)
    outs, extra = res if task is not None else (res, None)
    grouped = [tuple(outs[4 * k:4 * k + 4]) for k in range(n)]
    return (grouped, extra) if task is not None else grouped


def _adamw_cols_call(w, g_pad, m, v, *, name, tn=256):
    rows, cols = w.shape

    def body(w_ref, g_ref, m_ref, v_ref, go_ref, d_ref, mo_ref, vo_ref):
        g = g_ref[0:rows, :]
        d, mn, vn = _adamw(w_ref[...], g, m_ref[...], v_ref[...])
        go_ref[...] = g
        d_ref[...] = d
        mo_ref[...] = mn
        vo_ref[...] = vn

    spec = pl.BlockSpec((rows, tn), lambda j: (0, j))
    gspec = pl.BlockSpec((g_pad.shape[0], tn), lambda j: (0, j))
    return _pallas(body, grid=(cols // tn,), in_specs=[spec, gspec, spec, spec], out_specs=(spec,) * 4,
                   out_shape=(jax.ShapeDtypeStruct((rows, cols), F32),) * 4, name=name)(w, g_pad, m, v)


N_DEV = 8
SMALL_ROWS = 24
SMALL_LAYOUT = {
    "g_mix_pre": (0, 0, 1, D), "g_mix_post": (1, 0, 1, D), "g_mem_kv": (2, 0, 1, D), "g_mem_pre": (3, 0, 1, D),
    "g_mem_post": (4, 0, 1, D), "g_ff_pre": (5, 0, 1, D), "g_ff_post": (6, 0, 1, D),
    "g_fox_out": (7, 0, 1, D_GRP), "g_chk_out": (7, D_GRP, 1, D_GRP), "b_fgt": (8, 0, 1, 8),
    "rel_bias": (16, 0, 8, N_REL),
}
SMALL = list(SMALL_LAYOUT)


LOSS_ROW = 9


def _small_call(grads, ws, ms, vs, loss_blk, task, *, name):
    n = len(SMALL)
    t_in, t_out = len(task.arrays), len(task.out_shapes)

    def body(*refs):
        g_refs, w_refs, m_refs, v_refs = (refs[j * n:(j + 1) * n] for j in range(4))
        p = 4 * n
        loss_ref, tins = refs[p], refs[p + 1:p + 1 + t_in]
        p += 1 + t_in
        outs, loss_out, touts = refs[p:p + 4 * n], refs[p + 4 * n], refs[p + 4 * n + 1:p + 4 * n + 1 + t_out]
        p += 4 * n + 1 + t_out
        mine, slots, send_sems, recv_sems = refs[p:p + 4]
        tsems = refs[p + 4:]
        task.issue(tins, touts, tsems)
        x, y, c, _ = _place()
        me = 4 * x + 2 * y + c
        mine[...] = jnp.zeros_like(mine)
        for k, name_k in enumerate(SMALL):
            r, l, nr, nl = SMALL_LAYOUT[name_k]
            mine[r:r + nr, l:l + nl] = g_refs[k][0:nr, 0:nl]
        mine[LOSS_ROW:LOSS_ROW + 1, 0:128] = loss_ref[0:1, :]
        slots[me] = mine[...]
        peers = [(dx, dy, dc) for dx in (0, 1) for dy in (0, 1) for dc in (0, 1)][1:]
        cps = []
        for t, (dx, dy, dc) in enumerate(peers):
            px, py, pc = (x + dx) % 2, (y + dy) % 2, (c + dc) % 2
            cps.append(pltpu.make_async_remote_copy(
                src_ref=mine, dst_ref=slots.at[me], send_sem=send_sems.at[t], recv_sem=recv_sems.at[t],
                device_id=(px, py, pc), device_id_type=MESH))
            cps[-1].start()
        for t, (dx, dy, dc) in enumerate(peers):
            px, py, pc = (x + dx) % 2, (y + dy) % 2, (c + dc) % 2
            pltpu.make_async_remote_copy(
                src_ref=mine, dst_ref=slots.at[4 * px + 2 * py + pc], send_sem=send_sems.at[t],
                recv_sem=recv_sems.at[t], device_id=(px, py, pc), device_id_type=MESH).wait_recv()
        for cp in cps:
            cp.wait_send()
        total = slots[0]
        for j in range(1, N_DEV):
            total = total + slots[j]
        for k, name_k in enumerate(SMALL):
            r, l, nr, nl = SMALL_LAYOUT[name_k]
            g = total[r:r + nr, l:l + nl]
            d, mn, vn = _adamw(w_refs[k][...], g, m_refs[k][...], v_refs[k][...])
            for j, val in enumerate((g, d, mn, vn)):
                outs[4 * k + j][...] = val
        loss_out[...] = jnp.broadcast_to(total[LOSS_ROW:LOSS_ROW + 1, 0:128], loss_out.shape)
        task.drain(tins, touts, tsems)

    vm = pl.BlockSpec(memory_space=pltpu.VMEM)
    out_shape = [jax.ShapeDtypeStruct(ws[k].shape, F32) for k in SMALL for _ in range(4)]
    out_shape += [jax.ShapeDtypeStruct((8, 128), F32)] + list(task.out_shapes)
    res = pl.pallas_call(
        body, in_specs=[vm] * (4 * n + 1) + [ANY] * t_in, out_specs=[vm] * (4 * n + 1) + [ANY] * t_out,
        out_shape=out_shape,
        scratch_shapes=[pltpu.VMEM((SMALL_ROWS, D), F32), pltpu.VMEM((N_DEV, SMALL_ROWS, D), F32),
                        _dma_sems(N_DEV - 1), _dma_sems(N_DEV - 1)] + list(task.sems),
        input_output_aliases={4 * n + 1 + i: 4 * n + 1 + j for i, j in task.aliases.items()},
        name=name,
    )(*[d[k] for d in (grads, ws, ms, vs) for k in SMALL], loss_blk, *task.arrays)
    return ({k: tuple(res[4 * i:4 * i + 4]) for i, k in enumerate(SMALL)}, res[4 * n], list(res[4 * n + 1:]))


WEIGHTS = ["w_in", "b_fgt", "rel_bias", "g_fox_out", "g_chk_out", "w_out", "g_mix_pre", "g_mix_post", "g_mem_kv",
           "w_mq", "w_mk", "w_mv", "w_mo", "g_mem_pre", "g_mem_post", "w_ff1", "w_ff2", "g_ff_pre", "g_ff_post"]
BIG = ["w_in", "w_out", "w_mq", "w_mk", "w_mv", "w_mo", "w_ff1", "w_ff2"]


IN_SHARD = D_IN // N_CHIP
IN_PAD = 800


IN_PIECES = [(0, 0, 770), (800, 770, 766), (1566, 3072, 4), (1600, 3076, 4), (1604, 1536, 766), (2400, 2302, 770)]
PAD_ZEROS = [(800 * j + IN_SHARD, IN_PAD - IN_SHARD) for j in range(N_CHIP)]
ALL_ZEROS = [(D_IN, D_ALL - D_IN)]


def _reorder_rows_call(src, to_all, *, name, tn=256):
    rows, cols = src.shape
    zeros = ALL_ZEROS if to_all else PAD_ZEROS

    def body(s_ref, o_ref):
        for pad0, all0, cnt in IN_PIECES:
            s0, d0 = (pad0, all0) if to_all else (all0, pad0)
            o_ref[d0:d0 + cnt, :] = s_ref[s0:s0 + cnt, :]
        for z0, cnt in zeros:
            o_ref[z0:z0 + cnt, :] = jnp.zeros((cnt, tn), src.dtype)

    spec = pl.BlockSpec((rows, tn), lambda j: (0, j))
    return _pallas(body, grid=(cols // tn,), in_specs=[spec], out_specs=spec,
                   out_shape=jax.ShapeDtypeStruct((rows, cols), src.dtype), name=name)(src)


def kernel(x, mem, w_in, b_fgt, rel_bias, g_fox_out, g_chk_out, w_out, g_mix_pre, g_mix_post, g_mem_kv, w_mq, w_mk, w_mv, w_mo, g_mem_pre, g_mem_post, w_ff1, w_ff2, g_ff_pre, g_ff_post, loss_target, m_w_in, m_b_fgt, m_rel_bias, m_g_fox_out, m_g_chk_out, m_w_out, m_g_mix_pre, m_g_mix_post, m_g_mem_kv, m_w_mq, m_w_mk, m_w_mv, m_w_mo, m_g_mem_pre, m_g_mem_post, m_w_ff1, m_w_ff2, m_g_ff_pre, m_g_ff_post, v_w_in, v_b_fgt, v_rel_bias, v_g_fox_out, v_g_chk_out, v_w_out, v_g_mix_pre, v_g_mix_post, v_g_mem_kv, v_w_mq, v_w_mk, v_w_mv, v_w_mo, v_g_mem_pre, v_g_mem_post, v_w_ff1, v_w_ff2, v_g_ff_pre, v_g_ff_post):
    w = dict(w_in=w_in, b_fgt=b_fgt, rel_bias=rel_bias, g_fox_out=g_fox_out, g_chk_out=g_chk_out, w_out=w_out,
             g_mix_pre=g_mix_pre, g_mix_post=g_mix_post, g_mem_kv=g_mem_kv, w_mq=w_mq, w_mk=w_mk, w_mv=w_mv,
             w_mo=w_mo, g_mem_pre=g_mem_pre, g_mem_post=g_mem_post, w_ff1=w_ff1, w_ff2=w_ff2, g_ff_pre=g_ff_pre,
             g_ff_post=g_ff_post)
    m = dict(w_in=m_w_in, b_fgt=m_b_fgt, rel_bias=m_rel_bias, g_fox_out=m_g_fox_out, g_chk_out=m_g_chk_out,
             w_out=m_w_out, g_mix_pre=m_g_mix_pre, g_mix_post=m_g_mix_post, g_mem_kv=m_g_mem_kv, w_mq=m_w_mq,
             w_mk=m_w_mk, w_mv=m_w_mv, w_mo=m_w_mo, g_mem_pre=m_g_mem_pre, g_mem_post=m_g_mem_post,
             w_ff1=m_w_ff1, w_ff2=m_w_ff2, g_ff_pre=m_g_ff_pre, g_ff_post=m_g_ff_post)
    v = dict(w_in=v_w_in, b_fgt=v_b_fgt, rel_bias=v_rel_bias, g_fox_out=v_g_fox_out, g_chk_out=v_g_chk_out,
             w_out=v_w_out, g_mix_pre=v_g_mix_pre, g_mix_post=v_g_mix_post, g_mem_kv=v_g_mem_kv, w_mq=v_w_mq,
             w_mk=v_w_mk, w_mv=v_w_mv, w_mo=v_w_mo, g_mem_pre=v_g_mem_pre, g_mem_post=v_g_mem_post,
             w_ff1=v_w_ff1, w_ff2=v_w_ff2, g_ff_pre=v_g_ff_pre, g_ff_post=v_g_ff_post)

    def rows(d, k):
        return d[k][0] if k == "rel_bias" else d[k]

    xs, mems, target = x[0], mem[0], loss_target[0]
    S = xs.shape[0]
    sp = {k: rows(w, k) for k in SMALL}
    b_pad = jnp.pad(sp["b_fgt"], ((0, 0), (0, 120)))
    chip = 2 * lax.axis_index("x") + lax.axis_index("y")
    chip_arr = jnp.reshape(chip, (1,)).astype(jnp.int32)
    c_arr = jnp.reshape(lax.axis_index("c"), (1,)).astype(jnp.int32)
    place_arr = jnp.concatenate([chip_arr, c_arr])
    w_in_t, m_in_t, v_in_t = w["w_in"][0].T, m["w_in"][0].T, v["w_in"][0].T
    slab = {k: _cast_slab_call(w[k][0], chip_arr, name="cast_" + k) for k in BIG[1:]}
    slab["w_in"] = _cast_slab_call(w_in_t, chip_arr, name="cast_w_in", pad_rows=IN_PAD - IN_SHARD)

    def gather_ici(names):
        return _ag_ici_task([slab[k] for k in names])

    def pair_add(k, d, r1):
        return _pair_add_call(d, r1, c_arr, name="rs_pair_add_" + k)

    g_in, = _comm_call(gather_ici(["w_in"]), name="ag_w_in")
    h1, (g_in,) = _rms_fwd_call(xs, sp["g_mix_pre"], name="rms_mix_pre", task=_ag_d2d_task([g_in]))
    w_all_t = _reorder_rows_call(g_in.reshape(N_CHIP * IN_PAD, D), True, name="w_in_rows")
    proj, (g_out,) = _mm_nt(h1, w_all_t, "plain", rows=(0, 3072), name="mm_proj", task=gather_ici(["w_out"]))
    fl_raw = _mm_nt(h1, w_all_t, "plain", rows=(3072, 128), name="mm_gate", out_dtype=F32, tn=128)
    c_rep, c_t = _fox_prep_call(fl_raw, b_pad, name="fox_prep")
    bias = _chk_bias_call(_rel_table_to_g(sp["rel_bias"]), name="chk_bias")
    mid = ["w_mq", "w_mk", "w_mv", "w_mo", "w_ff1"]
    (yf, lse), got = _fox_fwd_call(proj, c_rep, c_t, name="fox_fwd",
                                   task=_merge_tasks([gather_ici(mid), _ag_d2d_task([g_out])]))
    g_mid, g_out = got[:5], got[5]
    yc, got = _chk_fwd_call(proj, bias, name="chk_fwd",
                            task=_merge_tasks([gather_ici(["w_ff2"]), _ag_d2d_task(g_mid)]))
    g_ff2, (g_mq, g_mk, g_mv, g_mo, g_ff1) = got[0], got[1:]
    yn = _mix_norm_fwd_call(yf, yc, sp["g_fox_out"], sp["g_chk_out"], name="mix_norm_fwd")
    z, (g_ff2,) = _mm_nn(yn, g_out, "rows", name="mm_out", out_dtype=F32, task=_ag_d2d_task([g_ff2]))
    x1, h2 = _post_pre_call(xs, z, sp["g_mix_post"], sp["g_mem_pre"], name="post_mix")
    memn = _rms_fwd_call(mems, sp["g_mem_kv"], name="rms_mem_kv")
    q2 = _mm_nn(h2, g_mq, "rows", name="mm_mq")
    k2 = _mm_nn(memn, g_mk, "rows", name="mm_mk")
    v2 = _mm_nn(memn, g_mv, "rows", name="mm_mv")
    o2 = _mem_fwd_call(q2, k2, v2, name="mem_fwd")
    y2 = _mm_nn(o2, g_mo, "rows", name="mm_mo", out_dtype=F32)
    x2, h3 = _post_pre_call(x1, y2, sp["g_mem_post"], sp["g_ff_pre"], name="post_mem")
    act, relu = _mm_nn(h3, g_ff1, "cols", name="mm_ff1", epi="relu2")
    y3 = _mm_nn(act, g_ff2, "rows", name="mm_ff2", out_dtype=F32, tm=1024)
    loss_blk, dx3, dy3, dg_ff_post = _final_call(x2, y3, sp["g_ff_post"], target, name="final")

    d_ff2 = _mm_tn(act, dy3, name="mm_dff2", tk=512, tn=1024).reshape(N_CHIP, D_FF // N_CHIP, D)
    du, (r1,) = _mm_nt(dy3, g_ff2, "rows", name="mm_du", mul2r=relu, task=_rs_pair_task([d_ff2]))
    p_ff2 = pair_add("w_ff2", d_ff2, r1)
    d_ff1 = _mm_tn(h3, du, name="mm_dff1", cols4=True)
    dh3, (r1,) = _mm_nt(du, g_ff1, "cols", name="mm_dh3", out_dtype=F32, tm=1024, task=_rs_pair_task([d_ff1]))
    p_ff1 = pair_add("w_ff1", d_ff1, r1)
    dx2, dy2, dg_ff_pre, dg_mem_post = _bwd_mid_call(dx3, x2, dh3, sp["g_ff_pre"], y2, sp["g_mem_post"], name="bwd_ff")
    d_mo = _mm_tn(o2, dy2, name="mm_dmo").reshape(N_CHIP, D // N_CHIP, D)
    do2 = _mm_nt(dy2, g_mo, "rows", name="mm_do2")
    dq2, dk2, dv2 = _mem_bwd_call(q2, k2, v2, do2, name="mem_bwd")
    d_mq = _mm_tn(h2, dq2, name="mm_dmq").reshape(N_CHIP, D // N_CHIP, D)
    dh2 = _mm_nt(dq2, g_mq, "rows", name="mm_dh2", out_dtype=F32)
    d_mk = _mm_tn(memn, dk2, name="mm_dmk").reshape(N_CHIP, D // N_CHIP, D)
    d_mv = _mm_tn(memn, dv2, name="mm_dmv").reshape(N_CHIP, D // N_CHIP, D)
    dmn_k = _mm_nt(dk2, g_mk, "rows", name="mm_dmemk", out_dtype=F32)
    dmn_v = _mm_nt(dv2, g_mv, "rows", name="mm_dmemv", out_dtype=F32)
    dg_mem_kv = _gain_grad_call(mems, sp["g_mem_kv"], dmn_k, dmn_v, name="gain_mem_kv")
    dx1, dz, dg_mem_pre, dg_mix_post = _bwd_mid_call(dx2, x1, dh2, sp["g_mem_pre"], z, sp["g_mix_post"], name="bwd_mem")
    d_out = _mm_tn(yn, dz, name="mm_dout").reshape(N_CHIP, D // N_CHIP, D)
    late = ["w_mo", "w_mq", "w_mk", "w_mv", "w_out"]
    d_late = [d_mo, d_mq, d_mk, d_mv, d_out]
    dyn, r1_late = _mm_nt(dz, g_out, "rows", name="mm_dyn", out_dtype=F32, task=_rs_pair_task(d_late))
    p_late = [pair_add(k, d, r1) for k, d, r1 in zip(late, d_late, r1_late)]
    dof, doc, delta, dg_fox, dg_chk = _mix_norm_bwd_call(dyn, yf, yc, sp["g_fox_out"], sp["g_chk_out"], name="mix_norm_bwd")
    (dqf, dkf, dvf, dct, dcq), r2_ff = _fox_bwd_call(proj, dof, lse, delta, c_rep, c_t, name="fox_bwd",
                                                      task=_rs_chip_task([p_ff2, p_ff1]))
    (dqc, dkc, dvc, dgrev), r2_late = _chk_bwd_call(proj, doc, bias, name="chk_bwd", task=_rs_chip_task(p_late))
    first = ["w_ff2", "w_ff1"] + late
    f_first = [_chip_sum_call(p, r, place_arr, name="rs_chip_sum_" + k)
               for k, p, r in zip(first, [p_ff2, p_ff1] + p_late, r2_ff + r2_late)]
    dc8 = dct[:, :, 0:2, :].transpose(0, 2, 1, 3).reshape(8, S) + dcq[:, ::HEAD].T
    dc_rows = jnp.concatenate([dc8, jnp.zeros((120, S), F32)], axis=0)
    dfl, db_fgt = _fox_gate_bwd_call(dc_rows, fl_raw, b_pad, name="fox_gate_bwd")
    dproj = jnp.concatenate([dqf, dkf, dvf, dqc, dkc, dvc, dfl], axis=1)
    d_all_t, g_first = _mm_tn(dproj, h1, name="mm_dwin", tk=640, tn=1024, task=_rs_gather_task(f_first))
    grads = dict(zip(first, g_first))
    d_in = _reorder_rows_call(d_all_t, False, name="d_in_rows").reshape(N_CHIP, IN_PAD, D)
    delta_w, new_m, new_v = {}, {}, {}

    def adamw_items(names):
        return [(w[k][0], grads[k], m[k][0], v[k][0]) for k in names]

    upd_late, (r1,) = _adamw_call(adamw_items(late), name="adamw_late", tm=64, task=_rs_pair_task([d_in]))
    p_in = pair_add("w_in", d_in, r1)
    dh1, (r2_in,) = _mm_nn(dproj, w_all_t, "plain", name="mm_dh1", out_dtype=F32, tm=1024,
                           task=_rs_chip_task([p_in]))
    f_in = _chip_sum_call(p_in, r2_in, place_arr, name="rs_chip_sum_w_in")
    upd_ff = _adamw_call(adamw_items(first[:2]), name="adamw_ff")
    for k, res in zip(late + first[:2], upd_late + upd_ff):
        grads[k], delta_w[k], new_m[k], new_v[k] = res
    grad_x, dg_mix_pre = _bwd_last_call(dx1, xs, dh1, sp["g_mix_pre"], name="bwd_mix")

    small_g = {"g_mix_pre": dg_mix_pre, "g_mix_post": dg_mix_post, "g_mem_kv": dg_mem_kv, "g_mem_pre": dg_mem_pre,
               "g_mem_post": dg_mem_post, "g_ff_pre": dg_ff_pre, "g_ff_post": dg_ff_post, "g_fox_out": dg_fox,
               "g_chk_out": dg_chk, "b_fgt": db_fgt,
               "rel_bias": _g_to_rel_table(dgrev[:, 0:2, :].reshape(8, ROLL_W))}
    small, loss_out, (g_w_in,) = _small_call(
        small_g, sp, {k: rows(m, k) for k in SMALL}, {k: rows(v, k) for k in SMALL}, loss_blk,
        _rs_gather_task([f_in]), name="small_allreduce_adamw")
    loss = loss_out[0, 0]
    res = _adamw_cols_call(w_in_t, g_w_in, m_in_t, v_in_t, name="adamw_w_in")
    grads["w_in"], delta_w["w_in"], new_m["w_in"], new_v["w_in"] = (a.T for a in res)
    for k in SMALL:
        vals = small[k]
        if k == "rel_bias":
            vals = tuple(a[None] for a in vals)
        grads[k], delta_w[k], new_m[k], new_v[k] = vals

    def out(d, k):
        return d[k][None] if k in BIG else d[k]

    return (loss, grad_x[None], *[out(grads, k) for k in WEIGHTS], *[out(delta_w, k) for k in WEIGHTS],
            *[out(new_m, k) for k in WEIGHTS], *[out(new_v, k) for k in WEIGHTS])
```

```python
import functools

import jax
import jax.numpy as jnp
from jax import lax
from jax.experimental import pallas as pl
from jax.experimental.pallas import tpu as pltpu

F32 = jnp.float32
BF16 = jnp.bfloat16

D = 1024
HEAD = 64
N_PAIR = 4
D_GRP = 512
CHUNK = 64
LEFT = 8
MAX_REL = 128
N_REL = 2 * MAX_REL + 1
N_MEM = 256
MEM_HEADS = 4
MEM_HD = 256
D_FF = 4096
D_IN = 3080
D_ALL = 3200
EPS = 1e-6
TQ = 256
WIN = (LEFT + TQ // CHUNK) * CHUNK
PADK = LEFT * CHUNK
ROLL_W = 1024
NEG = -1e30
N_CHIP = 4
VMEM_LIMIT = 48 * 1024 * 1024

ADAM_LR = 0.001
ADAM_B1 = 0.9
ADAM_B2 = 0.999
ADAM_EPS = 1e-08
ADAM_WD = 0.01
ADAM_STEP = 10

MESH = pl.DeviceIdType.MESH


def _cparams():
    return pltpu.CompilerParams(vmem_limit_bytes=VMEM_LIMIT)


ANY = pl.BlockSpec(memory_space=pl.ANY)


class _Task:
    def __init__(self, arrays, out_shapes, sems, issue, drain, aliases=None):
        self.arrays, self.out_shapes, self.sems = list(arrays), list(out_shapes), list(sems)
        self.issue, self.drain, self.aliases = issue, drain, dict(aliases or {})


def _merge_tasks(tasks):
    tasks = [t for t in tasks if t is not None]
    if len(tasks) == 1:
        return tasks[0]
    cuts, a, o, s = [], 0, 0, 0
    aliases = {}
    for t in tasks:
        cuts.append((a, o, s))
        aliases.update({a + i: o + j for i, j in t.aliases.items()})
        a, o, s = a + len(t.arrays), o + len(t.out_shapes), s + len(t.sems)

    def part(fn_name):
        def run(ins, outs, sems):
            for t, (a0, o0, s0) in zip(tasks, cuts):
                getattr(t, fn_name)(ins[a0:a0 + len(t.arrays)], outs[o0:o0 + len(t.out_shapes)],
                                    sems[s0:s0 + len(t.sems)])
        return run

    return _Task([x for t in tasks for x in t.arrays], [x for t in tasks for x in t.out_shapes],
                 [x for t in tasks for x in t.sems], part("issue"), part("drain"), aliases)


def _pallas(body, *, grid, in_specs, out_specs, out_shape, name, scratch_shapes=(), task=None):
    if task is None:
        return pl.pallas_call(body, grid=grid, in_specs=list(in_specs), out_specs=out_specs, out_shape=out_shape,
                              scratch_shapes=list(scratch_shapes), name=name, compiler_params=_cparams())
    single = not isinstance(out_shape, (tuple, list))
    o_shapes = [out_shape] if single else list(out_shape)
    o_specs = [out_specs] if single else list(out_specs)
    n_in, n_out, n_scr = len(in_specs), len(o_shapes), len(scratch_shapes)
    t_in, t_out = len(task.arrays), len(task.out_shapes)

    def carried(*refs):
        cut = [n_in, t_in, n_out, t_out, n_scr]
        parts, p = [], 0
        for c in cut:
            parts.append(refs[p:p + c])
            p += c
        ins, tins, outs, touts, scr = parts
        tsems = refs[p:]
        ids = [pl.program_id(a) for a in range(len(grid))]
        first = functools.reduce(jnp.logical_and, [i == 0 for i in ids])
        last = functools.reduce(jnp.logical_and, [i == g - 1 for i, g in zip(ids, grid)])

        @pl.when(first)
        def _():
            task.issue(tins, touts, tsems)
        body(*ins, *outs, *scr)

        @pl.when(last)
        def _():
            task.drain(tins, touts, tsems)

    call = pl.pallas_call(
        carried, grid=grid, in_specs=list(in_specs) + [ANY] * t_in, out_specs=o_specs + [ANY] * t_out,
        out_shape=o_shapes + list(task.out_shapes), scratch_shapes=list(scratch_shapes) + list(task.sems),
        input_output_aliases={n_in + i: n_out + j for i, j in task.aliases.items()},
        name=name, compiler_params=_cparams())

    def run(*args):
        res = call(*args, *task.arrays)
        outs = res[:n_out]
        return (outs[0] if single else tuple(outs)), list(res[n_out:])

    return run


def _comm_call(task, *, name):
    t_in, t_out = len(task.arrays), len(task.out_shapes)

    def body(*refs):
        tins, touts, tsems = refs[:t_in], refs[t_in:t_in + t_out], refs[t_in + t_out:]
        task.issue(tins, touts, tsems)
        task.drain(tins, touts, tsems)

    return pl.pallas_call(
        body, in_specs=[ANY] * t_in, out_specs=[ANY] * t_out, out_shape=list(task.out_shapes),
        scratch_shapes=list(task.sems), input_output_aliases=dict(task.aliases), name=name,
    )(*task.arrays)


def _dot(a, b):
    return jnp.dot(a, b, preferred_element_type=F32)


def _dot_nt(a, b):
    return lax.dot_general(a, b, (((1,), (1,)), ((), ())), preferred_element_type=F32)


def _dot_tn(a, b):
    return lax.dot_general(a, b, (((0,), (0,)), ((), ())), preferred_element_type=F32)


def _split3(x):
    hi = x.astype(BF16)
    r1 = x - hi.astype(F32)
    mid = r1.astype(BF16)
    lo = (r1 - mid.astype(F32)).astype(BF16)
    return hi, mid, lo


def _dot3(x, m01):
    hi, mid, lo = _split3(x)
    return _dot(hi, m01) + _dot(mid, m01) + _dot(lo, m01)


def _dot3_l(m01, x):
    hi, mid, lo = _split3(x)
    return _dot(m01, hi) + _dot(m01, mid) + _dot(m01, lo)


def _mm_nn(a, b, kind, *, name, out_dtype=BF16, tm=2048, tn=512, epi=None, task=None):
    M, K = a.shape
    if kind == "plain":
        N = b.shape[1]
        b_spec = pl.BlockSpec((K, tn), lambda m, n: (0, n))
    elif kind == "rows":
        N = b.shape[2]
        b_spec = pl.BlockSpec((N_CHIP, K // N_CHIP, tn), lambda m, n: (0, 0, n))
    else:
        nq = b.shape[2]
        N = N_CHIP * nq
        per = nq // tn
        b_spec = pl.BlockSpec((None, K, tn), lambda m, n: (n // per, 0, n % per))
    tm = min(tm, M)
    kq = K // N_CHIP

    def body(a_ref, b_ref, *o_refs):
        if kind == "rows":
            acc = _dot(a_ref[:, 0:kq], b_ref[0])
            for j in range(1, N_CHIP):
                acc += _dot(a_ref[:, j * kq:(j + 1) * kq], b_ref[j])
        else:
            acc = _dot(a_ref[...], b_ref[...])
        if epi == "relu2":
            r = jnp.maximum(acc, 0.0)
            o_refs[0][...] = (r * r).astype(BF16)
            o_refs[1][...] = r.astype(BF16)
        else:
            o_refs[0][...] = acc.astype(out_dtype)

    o_spec = pl.BlockSpec((tm, tn), lambda m, n: (m, n))
    if epi == "relu2":
        out_shape = (jax.ShapeDtypeStruct((M, N), BF16), jax.ShapeDtypeStruct((M, N), BF16))
        out_specs = (o_spec, o_spec)
    else:
        out_shape = jax.ShapeDtypeStruct((M, N), out_dtype)
        out_specs = o_spec
    return _pallas(
        body, grid=(M // tm, N // tn),
        in_specs=[pl.BlockSpec((tm, K), lambda m, n: (m, 0)), b_spec],
        out_specs=out_specs, out_shape=out_shape, name=name, task=task,
    )(a, b)


def _mm_nt(a, b, kind, *, name, out_dtype=BF16, tm=2048, tn=512, mul2r=None, task=None, rows=None):
    M, K = a.shape
    if kind == "plain":
        first, N = rows if rows is not None else (0, b.shape[0])
        n0 = first // tn
        b_spec = pl.BlockSpec((tn, K), lambda m, n: (n0 + n, 0))
    elif kind == "rows":
        nq = b.shape[1]
        N = N_CHIP * nq
        tn = min(tn, nq)
        per = nq // tn
        b_spec = pl.BlockSpec((None, tn, K), lambda m, n: (n // per, n % per, 0))
    else:
        N = b.shape[1]
        b_spec = pl.BlockSpec((N_CHIP, tn, K // N_CHIP), lambda m, n: (0, n, 0))
    tm = min(tm, M)
    kq = K // N_CHIP

    def body(a_ref, b_ref, *rest):
        o_ref = rest[-1]
        if kind == "cols":
            acc = _dot_nt(a_ref[:, 0:kq], b_ref[0])
            for j in range(1, N_CHIP):
                acc += _dot_nt(a_ref[:, j * kq:(j + 1) * kq], b_ref[j])
        else:
            acc = _dot_nt(a_ref[...], b_ref[...])
        if mul2r is not None:
            acc = acc * (2.0 * rest[0][...].astype(F32))
        o_ref[...] = acc.astype(out_dtype)

    in_specs = [pl.BlockSpec((tm, K), lambda m, n: (m, 0)), b_spec]
    args = [a, b]
    if mul2r is not None:
        in_specs.append(pl.BlockSpec((tm, tn), lambda m, n: (m, n)))
        args.append(mul2r)
    return _pallas(
        body, grid=(M // tm, N // tn), in_specs=in_specs,
        out_specs=pl.BlockSpec((tm, tn), lambda m, n: (m, n)),
        out_shape=jax.ShapeDtypeStruct((M, N), out_dtype), name=name, task=task,
    )(*args)


def _mm_tn(a, b, *, name, out_dtype=BF16, tk=1024, tn=512, cols4=False, task=None):
    M, K1 = a.shape
    N = b.shape[1]
    tk = min(tk, K1)
    tn = min(tn, N)

    def body(a_ref, b_ref, o_ref):
        o_ref[...] = _dot_tn(a_ref[...], b_ref[...]).astype(out_dtype)

    if cols4:
        per = (N // N_CHIP) // tn
        out_shape = jax.ShapeDtypeStruct((N_CHIP, K1, N // N_CHIP), out_dtype)
        o_spec = pl.BlockSpec((None, tk, tn), lambda k, n: (n // per, k, n % per))
    else:
        out_shape = jax.ShapeDtypeStruct((K1, N), out_dtype)
        o_spec = pl.BlockSpec((tk, tn), lambda k, n: (k, n))
    return _pallas(
        body, grid=(K1 // tk, N // tn),
        in_specs=[pl.BlockSpec((M, tk), lambda k, n: (0, k)), pl.BlockSpec((M, tn), lambda k, n: (0, n))],
        out_specs=o_spec, out_shape=out_shape, name=name, task=task,
    )(a, b)


def _rms(x, g):
    r = lax.rsqrt(jnp.mean(x * x, axis=-1, keepdims=True) + EPS)
    return x * r * g


def _rms_bwd(x, g, dy):
    r = lax.rsqrt(jnp.mean(x * x, axis=-1, keepdims=True) + EPS)
    xh = x * r
    dg = jnp.sum(dy * xh, axis=0, keepdims=True)
    dxh = dy * g
    dx = r * (dxh - xh * jnp.mean(dxh * xh, axis=-1, keepdims=True))
    return dx, dg


def _row_spec(tm, n):
    return pl.BlockSpec((tm, n), lambda i: (i, 0))


def _vec_spec(n):
    return pl.BlockSpec((1, n), lambda i: (0, 0))


def _acc_spec(n):
    return pl.BlockSpec((8, n), lambda i: (0, 0))


def _acc_add(ref, row, i):
    @pl.when(i == 0)
    def _():
        ref[...] = jnp.zeros_like(ref)
    ref[0:1, :] += row


def _rms_fwd_call(x, g, *, name, tm=256, task=None):
    M, n = x.shape
    tm = min(tm, M)

    def body(x_ref, g_ref, h_ref):
        h_ref[...] = _rms(x_ref[...], g_ref[...]).astype(BF16)

    return _pallas(
        body, grid=(M // tm,), in_specs=[_row_spec(tm, n), _vec_spec(n)], out_specs=_row_spec(tm, n),
        out_shape=jax.ShapeDtypeStruct((M, n), BF16), name=name, task=task,
    )(x, g)


def _post_pre_call(xres, z, g_post, g_pre, *, name, tm=256):
    M, n = xres.shape

    def body(x_ref, z_ref, gp_ref, gn_ref, xo_ref, h_ref):
        xn = x_ref[...] + _rms(z_ref[...], gp_ref[...])
        xo_ref[...] = xn
        h_ref[...] = _rms(xn, gn_ref[...]).astype(BF16)

    return pl.pallas_call(
        body, grid=(M // tm,),
        in_specs=[_row_spec(tm, n), _row_spec(tm, n), _vec_spec(n), _vec_spec(n)],
        out_specs=(_row_spec(tm, n), _row_spec(tm, n)),
        out_shape=(jax.ShapeDtypeStruct((M, n), F32), jax.ShapeDtypeStruct((M, n), BF16)),
        name=name, compiler_params=_cparams(),
    )(xres, z, g_post, g_pre)


def _final_call(x2, y3, g_post, target, *, name, tm=256):
    M, n = x2.shape

    def body(x_ref, y_ref, g_ref, t_ref, loss_ref, dx_ref, dy_ref, dg_ref):
        i = pl.program_id(0)
        y = y_ref[...]
        g = g_ref[...]
        diff = x_ref[...] + _rms(y, g) - t_ref[...]
        part = 0.5 * jnp.sum(jnp.sum(diff * diff, axis=1, keepdims=True), axis=0, keepdims=True) / n

        @pl.when(i == 0)
        def _():
            loss_ref[...] = jnp.zeros_like(loss_ref)
        loss_ref[...] += jnp.broadcast_to(part, loss_ref.shape)
        dx = diff / n
        dx_ref[...] = dx
        dy, dg = _rms_bwd(y, g, dx)
        dy_ref[...] = dy.astype(BF16)
        _acc_add(dg_ref, dg, i)

    return pl.pallas_call(
        body, grid=(M // tm,),
        in_specs=[_row_spec(tm, n), _row_spec(tm, n), _vec_spec(n), _row_spec(tm, n)],
        out_specs=(pl.BlockSpec((8, 128), lambda i: (0, 0)), _row_spec(tm, n), _row_spec(tm, n), _acc_spec(n)),
        out_shape=(jax.ShapeDtypeStruct((8, 128), F32), jax.ShapeDtypeStruct((M, n), F32),
                   jax.ShapeDtypeStruct((M, n), BF16), jax.ShapeDtypeStruct((8, n), F32)),
        name=name, compiler_params=_cparams(),
    )(x2, y3, g_post, target)


def _bwd_mid_call(dx_in, x, dh, g_pre, y, g_post, *, name, tm=256):
    M, n = x.shape

    def body(dxi_ref, x_ref, dh_ref, gpre_ref, y_ref, gpost_ref, dx_ref, dy_ref, dgpre_ref, dgpost_ref):
        i = pl.program_id(0)
        d1, dg1 = _rms_bwd(x_ref[...], gpre_ref[...], dh_ref[...])
        dx = dxi_ref[...] + d1
        dx_ref[...] = dx
        dy, dg2 = _rms_bwd(y_ref[...], gpost_ref[...], dx)
        dy_ref[...] = dy.astype(BF16)
        _acc_add(dgpre_ref, dg1, i)
        _acc_add(dgpost_ref, dg2, i)

    return pl.pallas_call(
        body, grid=(M // tm,),
        in_specs=[_row_spec(tm, n), _row_spec(tm, n), _row_spec(tm, n), _vec_spec(n), _row_spec(tm, n), _vec_spec(n)],
        out_specs=(_row_spec(tm, n), _row_spec(tm, n), _acc_spec(n), _acc_spec(n)),
        out_shape=(jax.ShapeDtypeStruct((M, n), F32), jax.ShapeDtypeStruct((M, n), BF16),
                   jax.ShapeDtypeStruct((8, n), F32), jax.ShapeDtypeStruct((8, n), F32)),
        name=name, compiler_params=_cparams(),
    )(dx_in, x, dh, g_pre, y, g_post)


def _bwd_last_call(dx_in, x, dh, g_pre, *, name, tm=256, task=None):
    M, n = x.shape

    def body(dxi_ref, x_ref, dh_ref, g_ref, dx_ref, dg_ref):
        i = pl.program_id(0)
        d1, dg1 = _rms_bwd(x_ref[...], g_ref[...], dh_ref[...])
        dx_ref[...] = dxi_ref[...] + d1
        _acc_add(dg_ref, dg1, i)

    return _pallas(
        body, grid=(M // tm,),
        in_specs=[_row_spec(tm, n), _row_spec(tm, n), _row_spec(tm, n), _vec_spec(n)],
        out_specs=(_row_spec(tm, n), _acc_spec(n)),
        out_shape=(jax.ShapeDtypeStruct((M, n), F32), jax.ShapeDtypeStruct((8, n), F32)),
        name=name, task=task,
    )(dx_in, x, dh, g_pre)


def _gain_grad_call(x, g, dy_a, dy_b, *, name):
    M, n = x.shape

    def body(x_ref, g_ref, a_ref, b_ref, dg_ref):
        _, dg = _rms_bwd(x_ref[...], g_ref[...], a_ref[...] + b_ref[...])
        dg_ref[...] = jnp.zeros_like(dg_ref)
        dg_ref[0:1, :] = dg

    return pl.pallas_call(
        body, grid=(1,),
        in_specs=[_row_spec(M, n), _vec_spec(n), _row_spec(M, n), _row_spec(M, n)],
        out_specs=_acc_spec(n), out_shape=jax.ShapeDtypeStruct((8, n), F32),
        name=name, compiler_params=_cparams(),
    )(x, g, dy_a, dy_b)


def _head_group_matrix():
    a = lax.broadcasted_iota(jnp.int32, (D_GRP, D_GRP), 0) // HEAD
    b = lax.broadcasted_iota(jnp.int32, (D_GRP, D_GRP), 1) // HEAD
    return jnp.where(a == b, 1.0, 0.0).astype(BF16)


def _mix_norm_fwd_call(yf, yc, gf, gc, *, name, tm=256):
    M = yf.shape[0]

    def body(yf_ref, yc_ref, gf_ref, gc_ref, o_ref):
        o_ref[:, 0:D_GRP] = _rms(yf_ref[...], gf_ref[...]).astype(BF16)
        o_ref[:, D_GRP:D] = _rms(yc_ref[...], gc_ref[...]).astype(BF16)

    return pl.pallas_call(
        body, grid=(M // tm,),
        in_specs=[_row_spec(tm, D_GRP), _row_spec(tm, D_GRP), _vec_spec(D_GRP), _vec_spec(D_GRP)],
        out_specs=_row_spec(tm, D), out_shape=jax.ShapeDtypeStruct((M, D), BF16),
        name=name, compiler_params=_cparams(),
    )(yf, yc, gf, gc)


def _mix_norm_bwd_call(dyn, yf, yc, gf, gc, *, name, tm=256):
    M = yf.shape[0]

    def body(dyn_ref, yf_ref, yc_ref, gf_ref, gc_ref, dof_ref, doc_ref, delta_ref, dgf_ref, dgc_ref):
        i = pl.program_id(0)
        yf_ = yf_ref[...]
        dof, dgf = _rms_bwd(yf_, gf_ref[...], dyn_ref[:, 0:D_GRP])
        doc, dgc = _rms_bwd(yc_ref[...], gc_ref[...], dyn_ref[:, D_GRP:D])
        dof_b = dof.astype(BF16)
        dof_ref[...] = dof_b
        doc_ref[...] = doc.astype(BF16)
        prod = dof_b.astype(F32) * yf_
        hi = prod.astype(BF16)
        lo = (prod - hi.astype(F32)).astype(BF16)
        grp = _head_group_matrix()
        delta_ref[...] = _dot(hi, grp) + _dot(lo, grp)
        _acc_add(dgf_ref, dgf, i)
        _acc_add(dgc_ref, dgc, i)

    return pl.pallas_call(
        body, grid=(M // tm,),
        in_specs=[_row_spec(tm, D), _row_spec(tm, D_GRP), _row_spec(tm, D_GRP), _vec_spec(D_GRP), _vec_spec(D_GRP)],
        out_specs=(_row_spec(tm, D_GRP), _row_spec(tm, D_GRP), _row_spec(tm, D_GRP), _acc_spec(D_GRP), _acc_spec(D_GRP)),
        out_shape=(jax.ShapeDtypeStruct((M, D_GRP), BF16), jax.ShapeDtypeStruct((M, D_GRP), BF16),
                   jax.ShapeDtypeStruct((M, D_GRP), F32), jax.ShapeDtypeStruct((8, D_GRP), F32),
                   jax.ShapeDtypeStruct((8, D_GRP), F32)),
        name=name, compiler_params=_cparams(),
    )(dyn, yf, yc, gf, gc)


def _tri(n, lower_incl):
    a = lax.broadcasted_iota(jnp.int32, (n, n), 0)
    b = lax.broadcasted_iota(jnp.int32, (n, n), 1)
    return jnp.where(a >= b, 1.0, 0.0).astype(BF16) if lower_incl else jnp.where(a <= b, 1.0, 0.0).astype(BF16)


def _fox_prep_call(fl_raw, b_pad, *, name):
    S = fl_raw.shape[0]
    nb = S // TQ

    def body(fl_ref, b_ref, crep_ref, ct_ref, carry_ref):
        i = pl.program_id(0)

        @pl.when(i == 0)
        def _():
            carry_ref[...] = jnp.zeros_like(carry_ref)
        logf = jax.nn.log_sigmoid(fl_ref[...] + b_ref[...])
        cb = _dot3_l(_tri(TQ, True), logf) + carry_ref[0:1, :]
        carry_ref[0:1, :] = cb[TQ - 1:TQ, :]
        a = lax.broadcasted_iota(jnp.int32, (128, D_GRP), 0)
        b = lax.broadcasted_iota(jnp.int32, (128, D_GRP), 1) // HEAD
        expand = jnp.where(a == b, 1.0, 0.0).astype(BF16)
        crep = _dot3(cb, expand)
        crep_ref[...] = crep
        ct_ref[...] = crep.T

    return pl.pallas_call(
        body, grid=(nb,),
        in_specs=[_row_spec(TQ, 128), _vec_spec(128)],
        out_specs=(_row_spec(TQ, D_GRP), pl.BlockSpec((None, D_GRP, TQ), lambda i: (i, 0, 0))),
        out_shape=(jax.ShapeDtypeStruct((S, D_GRP), F32), jax.ShapeDtypeStruct((nb, D_GRP, TQ), F32)),
        scratch_shapes=[pltpu.VMEM((8, 128), F32)],
        name=name, compiler_params=_cparams(),
    )(fl_raw, b_pad)


def _lane_masks():
    lane = lax.broadcasted_iota(jnp.int32, (1, 128), 1)
    return lane < HEAD, lane >= HEAD


def _fox_fwd_call(proj, c_rep, c_t, *, name, task=None):
    S = proj.shape[0]
    nq = S // TQ
    scale = HEAD ** -0.5

    def body(q_ref, k_ref, v_ref, c_ref, ct_ref, o_ref, lse_ref):
        i = pl.program_id(1)
        m_lo, m_hi = _lane_masks()
        masks = (m_lo, m_hi)
        q = q_ref[...]
        qm = [jnp.where(mk, q, jnp.zeros_like(q)) for mk in masks]
        cq = c_ref[...]
        cqh = [cq[:, 0:1], cq[:, HEAD:HEAD + 1]]
        row = lax.broadcasted_iota(jnp.int32, (TQ, TQ), 0)
        col = lax.broadcasted_iota(jnp.int32, (TQ, TQ), 1)

        def scores(j):
            start = pl.multiple_of(j * TQ, TQ)
            k = k_ref[pl.ds(start, TQ), :]
            ct = ct_ref[j]
            return tuple(_dot_nt(qm[h], k) * scale + (cqh[h] - ct[HEAD * h:HEAD * h + 1, :]) for h in range(2))

        def update(j, ss, state, masked):
            ms, ls, acc = state
            start = pl.multiple_of(j * TQ, TQ)
            v = v_ref[pl.ds(start, TQ), :]
            new_m, new_l, pv, alpha_l = [], [], [], []
            for h in range(2):
                s = ss[h]
                if masked:
                    s = jnp.where(row >= col, s, NEG)
                mn = jnp.maximum(ms[h], jnp.max(s, axis=1, keepdims=True))
                alpha = jnp.exp(ms[h] - mn)
                p = jnp.exp(s - mn)
                new_l.append(alpha * ls[h] + jnp.sum(p, axis=1, keepdims=True))
                new_m.append(mn)
                alpha_l.append(alpha)
                pv.append(_dot(p.astype(BF16), jnp.where(masks[h], v, jnp.zeros_like(v))))
            alpha_lane = jnp.where(m_lo, alpha_l[0], alpha_l[1])
            acc = acc * alpha_lane + pv[0] + pv[1]
            return (tuple(new_m), tuple(new_l), acc)

        def step(j, carry):
            ss, state = carry
            return (scores(j + 1), update(j, ss, state, False))

        init = ((jnp.full((TQ, 1), NEG, F32),) * 2, (jnp.zeros((TQ, 1), F32),) * 2, jnp.zeros((TQ, 128), F32))
        ss, state = lax.fori_loop(0, i, step, (scores(0), init))
        ms, ls, acc = update(i, ss, state, True)
        l_lane = jnp.where(m_lo, ls[0], ls[1])
        o_ref[...] = acc / l_lane
        lse_ref[...] = jnp.where(m_lo, ms[0] + jnp.log(ls[0]), ms[1] + jnp.log(ls[1]))

    return _pallas(
        body, grid=(N_PAIR, nq),
        in_specs=[pl.BlockSpec((TQ, 128), lambda p, i: (i, p)),
                  pl.BlockSpec((S, 128), lambda p, i: (0, N_PAIR + p)),
                  pl.BlockSpec((S, 128), lambda p, i: (0, 2 * N_PAIR + p)),
                  pl.BlockSpec((TQ, 128), lambda p, i: (i, p)),
                  pl.BlockSpec((nq, 128, TQ), lambda p, i: (0, p, 0))],
        out_specs=(pl.BlockSpec((TQ, 128), lambda p, i: (i, p)), pl.BlockSpec((TQ, 128), lambda p, i: (i, p))),
        out_shape=(jax.ShapeDtypeStruct((S, D_GRP), F32), jax.ShapeDtypeStruct((S, D_GRP), F32)),
        name=name, task=task,
    )(proj, proj, proj, c_rep, c_t)


def _fox_bwd_call(proj, do, lse_rep, delta_rep, c_rep, c_t, *, name, task=None):
    S = proj.shape[0]
    nq = S // TQ
    scale = HEAD ** -0.5

    def body(q_ref, k_ref, v_ref, do_ref, lse_ref, dl_ref, c_ref, ct_ref, dq_ref, dk_ref, dv_ref, dct_ref, dcq_ref, dqa_ref):
        j = pl.program_id(1)
        m_lo, m_hi = _lane_masks()
        masks = (m_lo, m_hi)

        @pl.when(j == 0)
        def _():
            dqa_ref[...] = jnp.zeros_like(dqa_ref)
            dcq_ref[...] = jnp.zeros_like(dcq_ref)
        k = k_ref[...]
        v = v_ref[...]
        km = [jnp.where(mk, k, jnp.zeros_like(k)) for mk in masks]
        ct = ct_ref[...]
        row = lax.broadcasted_iota(jnp.int32, (TQ, TQ), 0)
        col = lax.broadcasted_iota(jnp.int32, (TQ, TQ), 1)

        def probs(i):
            start = pl.multiple_of(i * TQ, TQ)
            q = q_ref[pl.ds(start, TQ), :]
            do = do_ref[pl.ds(start, TQ), :]
            lse = lse_ref[pl.ds(start, TQ), :]
            cq = c_ref[pl.ds(start, TQ), :]
            out = []
            for h in range(2):
                lo = HEAD * h
                qm = jnp.where(masks[h], q, jnp.zeros_like(q))
                dom = jnp.where(masks[h], do, jnp.zeros_like(do))
                s = _dot_nt(qm, k) * scale + (cq[:, lo:lo + 1] - ct[lo:lo + 1, :])
                out.append((jnp.exp(s - lse[:, lo:lo + 1]), _dot_nt(dom, v)))
            return tuple(out)

        def update(i, pd, carry, masked):
            dk, dv, dcs = carry
            start = pl.multiple_of(i * TQ, TQ)
            q = q_ref[pl.ds(start, TQ), :]
            do = do_ref[pl.ds(start, TQ), :]
            dl = dl_ref[pl.ds(start, TQ), :]
            dq = jnp.zeros((TQ, 128), F32)
            new_dcs = []
            rows = []
            for h in range(2):
                lo = HEAD * h
                qm = jnp.where(masks[h], q, jnp.zeros_like(q))
                dom = jnp.where(masks[h], do, jnp.zeros_like(do))
                p, dp = pd[h]
                if masked:
                    p = jnp.where(row >= col, p, 0.0)
                ds = p * (dp - dl[:, lo:lo + 1])
                new_dcs.append(dcs[h] + jnp.sum(ds, axis=0, keepdims=True))
                rows.append(jnp.sum(ds, axis=1, keepdims=True))
                dsb = (ds * scale).astype(BF16)
                dv = dv + _dot_tn(p.astype(BF16), dom)
                dk = dk + _dot_tn(dsb, qm)
                dq = dq + _dot(dsb, km[h])
            dqa_ref[pl.ds(start, TQ), :] += dq
            dcq_ref[pl.ds(start, TQ), :] += jnp.where(m_lo, rows[0], rows[1])
            return (dk, dv, tuple(new_dcs))

        def step(i, carry):
            pd, sums = carry
            return (probs(jnp.minimum(i + 1, nq - 1)), update(i, pd, sums, False))

        init = (jnp.zeros((TQ, 128), F32), jnp.zeros((TQ, 128), F32), (jnp.zeros((1, TQ), F32),) * 2)
        first = probs(j)
        second = probs(jnp.minimum(j + 1, nq - 1))
        _, (dk, dv, dcs) = lax.fori_loop(j + 1, nq, step, (second, update(j, first, init, True)))
        dk_ref[...] = dk.astype(BF16)
        dv_ref[...] = dv.astype(BF16)
        dct_ref[...] = jnp.zeros_like(dct_ref)
        dct_ref[0:1, :] = -dcs[0]
        dct_ref[1:2, :] = -dcs[1]

        @pl.when(j == nq - 1)
        def _():
            dq_ref[...] = dqa_ref[...].astype(BF16)

    res = lambda p, j: (0, p)
    return _pallas(
        body, grid=(N_PAIR, nq), task=task,
        in_specs=[pl.BlockSpec((S, 128), res),
                  pl.BlockSpec((TQ, 128), lambda p, j: (j, N_PAIR + p)),
                  pl.BlockSpec((TQ, 128), lambda p, j: (j, 2 * N_PAIR + p)),
                  pl.BlockSpec((S, 128), res), pl.BlockSpec((S, 128), res), pl.BlockSpec((S, 128), res),
                  pl.BlockSpec((S, 128), res),
                  pl.BlockSpec((None, 128, TQ), lambda p, j: (j, p, 0))],
        out_specs=(pl.BlockSpec((S, 128), res),
                   pl.BlockSpec((TQ, 128), lambda p, j: (j, p)), pl.BlockSpec((TQ, 128), lambda p, j: (j, p)),
                   pl.BlockSpec((None, None, 8, TQ), lambda p, j: (p, j, 0, 0)),
                   pl.BlockSpec((S, 128), res)),
        out_shape=(jax.ShapeDtypeStruct((S, D_GRP), BF16), jax.ShapeDtypeStruct((S, D_GRP), BF16),
                   jax.ShapeDtypeStruct((S, D_GRP), BF16), jax.ShapeDtypeStruct((N_PAIR, nq, 8, TQ), F32),
                   jax.ShapeDtypeStruct((S, D_GRP), F32)),
        scratch_shapes=[pltpu.VMEM((S, 128), F32)],
        name=name,
    )(proj, proj, proj, do, lse_rep, delta_rep, c_rep, c_t)


def _fox_gate_bwd_call(dc_rows, fl_raw, b_pad, *, name):
    S = fl_raw.shape[0]
    nb = S // TQ

    def body(dc_ref, fl_ref, b_ref, dfl_ref, db_ref, carry_ref):
        i = pl.program_id(0)

        @pl.when(i == 0)
        def _():
            carry_ref[...] = jnp.zeros_like(carry_ref)
        rc = _dot3(dc_ref[...], _tri(TQ, True)) + carry_ref[:, 0:1]
        carry_ref[...] = jnp.broadcast_to(rc[:, 0:1], carry_ref.shape)
        fl = fl_ref[...] + b_ref[...]
        dfl = rc.T * jax.nn.sigmoid(-fl)
        dfl_ref[...] = dfl.astype(BF16)
        _acc_add(db_ref, jnp.sum(dfl, axis=0, keepdims=True), i)

    rev = lambda i: (nb - 1 - i, 0)
    return pl.pallas_call(
        body, grid=(nb,),
        in_specs=[pl.BlockSpec((128, TQ), lambda i: (0, nb - 1 - i)), pl.BlockSpec((TQ, 128), rev), _vec_spec(128)],
        out_specs=(pl.BlockSpec((TQ, 128), rev), _acc_spec(128)),
        out_shape=(jax.ShapeDtypeStruct((S, 128), BF16), jax.ShapeDtypeStruct((8, 128), F32)),
        scratch_shapes=[pltpu.VMEM((128, 128), F32)],
        name=name, compiler_params=_cparams(),
    )(dc_rows, fl_raw, b_pad)


def _chk_bias_call(g_rev, *, name):
    def body(g_ref, o_ref):
        x = jnp.broadcast_to(g_ref[...], (TQ, ROLL_W))
        rolled = pltpu.roll(x, ROLL_W - (TQ - 1), 1, stride=1, stride_axis=0)
        qc = lax.broadcasted_iota(jnp.int32, (TQ, WIN), 0) // CHUNK
        kc = lax.broadcasted_iota(jnp.int32, (TQ, WIN), 1) // CHUNK
        band = (kc >= qc) & (kc <= qc + LEFT)
        o_ref[...] = jnp.where(band, rolled[:, 0:WIN], NEG)

    return pl.pallas_call(
        body, grid=(8,),
        in_specs=[pl.BlockSpec((None, 1, ROLL_W), lambda h: (h, 0, 0))],
        out_specs=pl.BlockSpec((None, TQ, WIN), lambda h: (h, 0, 0)),
        out_shape=jax.ShapeDtypeStruct((8, TQ, WIN), F32), name=name, compiler_params=_cparams(),
    )(g_rev.reshape(8, 1, ROLL_W))


def _chk_scores(i, qm, kwin, bias, scale):
    s = _dot_nt(qm, kwin) * scale + bias
    kc = lax.broadcasted_iota(jnp.int32, (TQ, WIN), 1) // CHUNK
    return jnp.where(kc + i * (TQ // CHUNK) >= LEFT, s, NEG)


def _chk_fwd_call(proj, bias, *, name, task=None):
    S = proj.shape[0]
    nq = S // TQ
    scale = HEAD ** -0.5

    def body(q_ref, k_ref, v_ref, b_ref, o_ref, kp_ref, vp_ref):
        i = pl.program_id(1)

        @pl.when(i == 0)
        def _():
            kp_ref[0:PADK, :] = jnp.zeros((PADK, 128), BF16)
            vp_ref[0:PADK, :] = jnp.zeros((PADK, 128), BF16)
            kp_ref[PADK:PADK + S, :] = k_ref[...]
            vp_ref[PADK:PADK + S, :] = v_ref[...]
        masks = _lane_masks()
        q = q_ref[...]
        start = pl.multiple_of(i * TQ, TQ)
        kwin = kp_ref[pl.ds(start, WIN), :]
        vwin = vp_ref[pl.ds(start, WIN), :]
        out = jnp.zeros((TQ, 128), F32)
        for h in range(2):
            qm = jnp.where(masks[h], q, jnp.zeros_like(q))
            s = _chk_scores(i, qm, kwin, b_ref[h], scale)
            p = jnp.exp(s - jnp.max(s, axis=1, keepdims=True))
            p = p / jnp.sum(p, axis=1, keepdims=True)
            out = out + _dot(p.astype(BF16), jnp.where(masks[h], vwin, jnp.zeros_like(vwin)))
        o_ref[...] = out

    c0 = 3 * N_PAIR
    return _pallas(
        body, grid=(N_PAIR, nq), task=task,
        in_specs=[pl.BlockSpec((TQ, 128), lambda p, i: (i, c0 + p)),
                  pl.BlockSpec((S, 128), lambda p, i: (0, c0 + N_PAIR + p)),
                  pl.BlockSpec((S, 128), lambda p, i: (0, c0 + 2 * N_PAIR + p)),
                  pl.BlockSpec((2, TQ, WIN), lambda p, i: (p, 0, 0))],
        out_specs=pl.BlockSpec((TQ, 128), lambda p, i: (i, p)),
        out_shape=jax.ShapeDtypeStruct((S, D_GRP), F32),
        scratch_shapes=[pltpu.VMEM((S + PADK, 128), BF16), pltpu.VMEM((S + PADK, 128), BF16)],
        name=name,
    )(proj, proj, proj, bias)


def _chk_bwd_call(proj, do, bias, *, name, task=None):
    S = proj.shape[0]
    nq = S // TQ
    scale = HEAD ** -0.5

    def body(q_ref, k_ref, v_ref, do_ref, b_ref, dq_ref, dk_ref, dv_ref, dg_ref, kp_ref, vp_ref, dkp_ref, dvp_ref, db_ref):
        i = pl.program_id(1)

        @pl.when(i == 0)
        def _():
            kp_ref[0:PADK, :] = jnp.zeros((PADK, 128), BF16)
            vp_ref[0:PADK, :] = jnp.zeros((PADK, 128), BF16)
            kp_ref[PADK:PADK + S, :] = k_ref[...]
            vp_ref[PADK:PADK + S, :] = v_ref[...]
            dkp_ref[...] = jnp.zeros_like(dkp_ref)
            dvp_ref[...] = jnp.zeros_like(dvp_ref)
            db_ref[...] = jnp.zeros_like(db_ref)
        masks = _lane_masks()
        q = q_ref[...]
        dout = do_ref[...]
        start = pl.multiple_of(i * TQ, TQ)
        kwin = kp_ref[pl.ds(start, WIN), :]
        vwin = vp_ref[pl.ds(start, WIN), :]
        dq = jnp.zeros((TQ, 128), F32)
        dkw = jnp.zeros((WIN, 128), F32)
        dvw = jnp.zeros((WIN, 128), F32)
        for h in range(2):
            qm = jnp.where(masks[h], q, jnp.zeros_like(q))
            dom = jnp.where(masks[h], dout, jnp.zeros_like(dout))
            s = _chk_scores(i, qm, kwin, b_ref[h], scale)
            p = jnp.exp(s - jnp.max(s, axis=1, keepdims=True))
            p = p / jnp.sum(p, axis=1, keepdims=True)
            dp = _dot_nt(dom, vwin)
            ds = p * (dp - jnp.sum(p * dp, axis=1, keepdims=True))
            db_ref[h] += ds
            dsb = (ds * scale).astype(BF16)
            dq = dq + _dot(dsb, jnp.where(masks[h], kwin, jnp.zeros_like(kwin)))
            dkw = dkw + _dot_tn(dsb, qm)
            dvw = dvw + _dot_tn(p.astype(BF16), dom)
        dq_ref[...] = dq.astype(BF16)
        dkp_ref[pl.ds(start, WIN), :] += dkw
        dvp_ref[pl.ds(start, WIN), :] += dvw

        @pl.when(i == nq - 1)
        def _():
            dk_ref[...] = dkp_ref[PADK:PADK + S, :].astype(BF16)
            dv_ref[...] = dvp_ref[PADK:PADK + S, :].astype(BF16)
            a = lax.broadcasted_iota(jnp.int32, (TQ, TQ), 0)
            b = lax.broadcasted_iota(jnp.int32, (TQ, TQ), 1)
            flip = jnp.where(a + b == TQ - 1, 1.0, 0.0).astype(BF16)
            e = lax.broadcasted_iota(jnp.int32, (1, ROLL_W), 1)
            dg_ref[...] = jnp.zeros_like(dg_ref)
            for h in range(2):
                rev = _dot3_l(flip, db_ref[h])
                wide = jnp.concatenate([rev, jnp.zeros((TQ, ROLL_W - WIN), F32)], axis=1)
                diag = pltpu.roll(wide, 0, 1, stride=1, stride_axis=0)
                dg = jnp.sum(diag, axis=0, keepdims=True)
                lo = jnp.sum(jnp.where(e <= 639, dg, 0.0), axis=1, keepdims=True)
                hi = jnp.sum(jnp.where(e >= 895, dg, 0.0), axis=1, keepdims=True)
                dg_ref[h:h + 1, :] = jnp.where(e == 639, lo, jnp.where(e == 895, hi, dg))

    c0 = 3 * N_PAIR
    res = lambda p, i: (0, p)
    return _pallas(
        body, grid=(N_PAIR, nq), task=task,
        in_specs=[pl.BlockSpec((TQ, 128), lambda p, i: (i, c0 + p)),
                  pl.BlockSpec((S, 128), lambda p, i: (0, c0 + N_PAIR + p)),
                  pl.BlockSpec((S, 128), lambda p, i: (0, c0 + 2 * N_PAIR + p)),
                  pl.BlockSpec((TQ, 128), lambda p, i: (i, p)),
                  pl.BlockSpec((2, TQ, WIN), lambda p, i: (p, 0, 0))],
        out_specs=(pl.BlockSpec((TQ, 128), lambda p, i: (i, p)), pl.BlockSpec((S, 128), res),
                   pl.BlockSpec((S, 128), res), pl.BlockSpec((None, 8, ROLL_W), lambda p, i: (p, 0, 0))),
        out_shape=(jax.ShapeDtypeStruct((S, D_GRP), BF16), jax.ShapeDtypeStruct((S, D_GRP), BF16),
                   jax.ShapeDtypeStruct((S, D_GRP), BF16), jax.ShapeDtypeStruct((N_PAIR, 8, ROLL_W), F32)),
        scratch_shapes=[pltpu.VMEM((S + PADK, 128), BF16), pltpu.VMEM((S + PADK, 128), BF16),
                        pltpu.VMEM((S + PADK, 128), F32), pltpu.VMEM((S + PADK, 128), F32),
                        pltpu.VMEM((2, TQ, WIN), F32)],
        name=name,
    )(proj, proj, proj, do, bias)


def _mem_fwd_call(q, k, v, *, name, tq=512):
    S = q.shape[0]
    scale = MEM_HD ** -0.5

    def body(q_ref, k_ref, v_ref, o_ref):
        s = _dot_nt(q_ref[...], k_ref[...]) * scale
        p = jnp.exp(s - jnp.max(s, axis=1, keepdims=True))
        p = p / jnp.sum(p, axis=1, keepdims=True)
        o_ref[...] = _dot(p.astype(BF16), v_ref[...]).astype(BF16)

    return pl.pallas_call(
        body, grid=(MEM_HEADS, S // tq),
        in_specs=[pl.BlockSpec((tq, MEM_HD), lambda h, i: (i, h)),
                  pl.BlockSpec((N_MEM, MEM_HD), lambda h, i: (0, h)),
                  pl.BlockSpec((N_MEM, MEM_HD), lambda h, i: (0, h))],
        out_specs=pl.BlockSpec((tq, MEM_HD), lambda h, i: (i, h)),
        out_shape=jax.ShapeDtypeStruct((S, D), BF16), name=name, compiler_params=_cparams(),
    )(q, k, v)


def _mem_bwd_call(q, k, v, do, *, name, tq=512):
    S = q.shape[0]
    n = S // tq
    scale = MEM_HD ** -0.5

    def body(q_ref, k_ref, v_ref, do_ref, dq_ref, dk_ref, dv_ref, dka_ref, dva_ref):
        i = pl.program_id(1)

        @pl.when(i == 0)
        def _():
            dka_ref[...] = jnp.zeros_like(dka_ref)
            dva_ref[...] = jnp.zeros_like(dva_ref)
        qb = q_ref[...]
        kb = k_ref[...]
        dob = do_ref[...]
        s = _dot_nt(qb, kb) * scale
        p = jnp.exp(s - jnp.max(s, axis=1, keepdims=True))
        p = p / jnp.sum(p, axis=1, keepdims=True)
        dp = _dot_nt(dob, v_ref[...])
        ds = p * (dp - jnp.sum(p * dp, axis=1, keepdims=True))
        dsb = (ds * scale).astype(BF16)
        dq_ref[...] = _dot(dsb, kb).astype(BF16)
        dka_ref[...] += _dot_tn(dsb, qb)
        dva_ref[...] += _dot_tn(p.astype(BF16), dob)

        @pl.when(i == n - 1)
        def _():
            dk_ref[...] = dka_ref[...].astype(BF16)
            dv_ref[...] = dva_ref[...].astype(BF16)

    kv = pl.BlockSpec((N_MEM, MEM_HD), lambda h, i: (0, h))
    qs = pl.BlockSpec((tq, MEM_HD), lambda h, i: (i, h))
    return pl.pallas_call(
        body, grid=(MEM_HEADS, n), in_specs=[qs, kv, kv, qs], out_specs=(qs, kv, kv),
        out_shape=(jax.ShapeDtypeStruct((S, D), BF16), jax.ShapeDtypeStruct((N_MEM, D), BF16),
                   jax.ShapeDtypeStruct((N_MEM, D), BF16)),
        scratch_shapes=[pltpu.VMEM((N_MEM, MEM_HD), F32), pltpu.VMEM((N_MEM, MEM_HD), F32)],
        name=name, compiler_params=_cparams(),
    )(q, k, v, do)


def _rel_table_to_g(rel):
    return jnp.concatenate([
        jnp.broadcast_to(rel[:, N_REL - 1:N_REL], (8, 640)),
        rel[:, 1:N_REL - 1][:, ::-1],
        jnp.broadcast_to(rel[:, 0:1], (8, 129)),
    ], axis=1)


def _g_to_rel_table(dg):
    return dg[:, 639:896][:, ::-1]


def _place():
    x, y, c = lax.axis_index("x"), lax.axis_index("y"), lax.axis_index("c")
    others = [(1 - x, y), (x, 1 - y), (1 - x, 1 - y)]
    return x, y, c, others


def _half(c, rows):
    hr = rows // 2
    return pl.ds(pl.multiple_of(c * hr, 16), hr)


def _dma_sems(*shape):
    return pltpu.SemaphoreType.DMA(shape)


def _cast_slab_call(w, chip_arr, *, name, tm=256, pad_rows=0):
    rows, cols = w.shape
    if pad_rows:
        tm = rows
    tm = min(tm, rows)

    def body(chip_ref, w_ref, o_ref):
        o_ref[0:tm, :] = w_ref[...].astype(BF16)
        if pad_rows:
            o_ref[tm:tm + pad_rows, :] = jnp.zeros((pad_rows, cols), BF16)

    return pl.pallas_call(
        body,
        grid_spec=pltpu.PrefetchScalarGridSpec(
            num_scalar_prefetch=1, grid=(rows // tm,),
            in_specs=[pl.BlockSpec((tm, cols), lambda i, chip: (i, 0))],
            out_specs=pl.BlockSpec((None, tm + pad_rows, cols), lambda i, chip: (chip[0], i, 0))),
        out_shape=jax.ShapeDtypeStruct((N_CHIP, rows + pad_rows, cols), BF16), name=name,
        compiler_params=_cparams(),
    )(chip_arr, w)


def _ag_ici_task(gathered):
    n = len(gathered)

    def copies(ins, outs, sems):
        send_sems, recv_sems = sems
        x, y, c, others = _place()
        me = 2 * x + y
        for k in range(n):
            mine = _half(c, gathered[k].shape[1])
            for t, (ox, oy) in enumerate(others):
                yield [pltpu.make_async_remote_copy(
                    src_ref=ins[k].at[me, mine], dst_ref=outs[k].at[slab, mine],
                    send_sem=send_sems.at[k, t], recv_sem=recv_sems.at[k, t],
                    device_id=(ox, oy, c), device_id_type=MESH) for slab in (me, 2 * ox + oy)]

    def issue(ins, outs, sems):
        for outgoing, _ in copies(ins, outs, sems):
            outgoing.start()

    def drain(ins, outs, sems):
        for outgoing, incoming in copies(ins, outs, sems):
            incoming.wait_recv()
            outgoing.wait_send()

    return _Task(gathered, [jax.ShapeDtypeStruct(g.shape, g.dtype) for g in gathered],
                 [_dma_sems(n, 3), _dma_sems(n, 3)], issue, drain, aliases={k: k for k in range(n)})


def _ag_d2d_task(gathered):
    n = len(gathered)

    def copies(ins, outs, sems):
        send_sems, recv_sems = sems
        x, y, c, others = _place()
        for k in range(n):
            rows = gathered[k].shape[1]
            mine, theirs = _half(c, rows), _half(1 - c, rows)
            for t, (ox, oy) in enumerate(others):
                slab = 2 * ox + oy
                pair = [pltpu.make_async_remote_copy(
                    src_ref=ins[k].at[slab, half], dst_ref=outs[k].at[slab, half],
                    send_sem=send_sems.at[k, t], recv_sem=recv_sems.at[k, t],
                    device_id=(x, y, 1 - c), device_id_type=MESH) for half in (mine, theirs)]
                yield pair

    def issue(ins, outs, sems):
        for outgoing, _ in copies(ins, outs, sems):
            outgoing.start()

    def drain(ins, outs, sems):
        for outgoing, incoming in copies(ins, outs, sems):
            incoming.wait_recv()
            outgoing.wait_send()

    return _Task(gathered, [jax.ShapeDtypeStruct(g.shape, g.dtype) for g in gathered],
                 [_dma_sems(n, 3), _dma_sems(n, 3)], issue, drain, aliases={k: k for k in range(n)})


def _rs_pair_task(ds):
    n = len(ds)

    def copies(ins, outs, sems):
        send_sems, recv_sems = sems
        x, y, c, _ = _place()
        for k in range(n):
            yield pltpu.make_async_remote_copy(
                src_ref=ins[k].at[:, _half(1 - c, ds[k].shape[1])], dst_ref=outs[k],
                send_sem=send_sems.at[k], recv_sem=recv_sems.at[k],
                device_id=(x, y, 1 - c), device_id_type=MESH)

    def issue(ins, outs, sems):
        for cp in copies(ins, outs, sems):
            cp.start()

    def drain(ins, outs, sems):
        for cp in copies(ins, outs, sems):
            cp.wait()

    return _Task(ds, [jax.ShapeDtypeStruct((N_CHIP, d.shape[1] // 2, d.shape[2]), d.dtype) for d in ds],
                 [_dma_sems(n), _dma_sems(n)], issue, drain)


def _pair_add_call(d, r1, c_arr, *, name, tm=256):
    _, rows, cols = d.shape
    hr = rows // 2
    tm = tm if hr % tm == 0 else hr
    nb = hr // tm

    def body(c_ref, d_ref, r_ref, o_ref):
        o_ref[...] = (d_ref[...].astype(F32) + r_ref[...].astype(F32)).astype(BF16)

    return pl.pallas_call(
        body,
        grid_spec=pltpu.PrefetchScalarGridSpec(
            num_scalar_prefetch=1, grid=(N_CHIP, nb),
            in_specs=[pl.BlockSpec((None, tm, cols), lambda j, i, c: (j, c[0] * nb + i, 0)),
                      pl.BlockSpec((None, tm, cols), lambda j, i, c: (j, i, 0))],
            out_specs=pl.BlockSpec((None, tm, cols), lambda j, i, c: (j, i, 0))),
        out_shape=jax.ShapeDtypeStruct((N_CHIP, hr, cols), BF16), name=name, compiler_params=_cparams(),
    )(c_arr, d, r1)


def _rs_chip_task(ps):
    n = len(ps)

    def copies(ins, outs, sems):
        send_sems, recv_sems = sems
        x, y, c, others = _place()
        for k in range(n):
            for t, (ox, oy) in enumerate(others):
                yield pltpu.make_async_remote_copy(
                    src_ref=ins[k].at[2 * ox + oy], dst_ref=outs[k].at[t],
                    send_sem=send_sems.at[k, t], recv_sem=recv_sems.at[k, t],
                    device_id=(ox, oy, c), device_id_type=MESH)

    def issue(ins, outs, sems):
        for cp in copies(ins, outs, sems):
            cp.start()

    def drain(ins, outs, sems):
        for cp in copies(ins, outs, sems):
            cp.wait()

    return _Task(ps, [jax.ShapeDtypeStruct((3,) + p.shape[1:], p.dtype) for p in ps],
                 [_dma_sems(n, 3), _dma_sems(n, 3)], issue, drain)


def _chip_sum_call(p, r2, place_arr, *, name, tm=256):
    _, hr, cols = r2.shape
    tm = tm if hr % tm == 0 else hr
    nb = hr // tm

    def body(place_ref, p_ref, r_ref, o_ref):
        acc = p_ref[...].astype(F32)
        for j in range(3):
            acc = acc + r_ref[j].astype(F32)
        o_ref[...] = acc

    return pl.pallas_call(
        body,
        grid_spec=pltpu.PrefetchScalarGridSpec(
            num_scalar_prefetch=1, grid=(nb,),
            in_specs=[pl.BlockSpec((None, tm, cols), lambda i, pc: (pc[0], i, 0)),
                      pl.BlockSpec((3, tm, cols), lambda i, pc: (0, i, 0))],
            out_specs=pl.BlockSpec((tm, cols), lambda i, pc: (pc[1] * nb + i, 0))),
        out_shape=jax.ShapeDtypeStruct((2 * hr, cols), F32), name=name, compiler_params=_cparams(),
    )(place_arr, p, r2)


def _rs_gather_task(gs):
    n = len(gs)

    def copies(ins, outs, sems):
        send_sems, recv_sems = sems
        x, y, c, _ = _place()
        for k in range(n):
            rows = gs[k].shape[0]
            mine, theirs = _half(c, rows), _half(1 - c, rows)
            yield [pltpu.make_async_remote_copy(
                src_ref=ins[k].at[mine], dst_ref=outs[k].at[half],
                send_sem=send_sems.at[k], recv_sem=recv_sems.at[k],
                device_id=(x, y, 1 - c), device_id_type=MESH) for half in (mine, theirs)]

    def issue(ins, outs, sems):
        for outgoing, _ in copies(ins, outs, sems):
            outgoing.start()

    def drain(ins, outs, sems):
        for outgoing, incoming in copies(ins, outs, sems):
            incoming.wait_recv()
            outgoing.wait_send()

    return _Task(gs, [jax.ShapeDtypeStruct(g.shape, g.dtype) for g in gs],
                 [_dma_sems(n), _dma_sems(n)], issue, drain, aliases={k: k for k in range(n)})


def _adamw(w, g, m, v):
    m = ADAM_B1 * m + (1.0 - ADAM_B1) * g
    v = ADAM_B2 * v + (1.0 - ADAM_B2) * jnp.square(g)
    m_hat = m / (1.0 - ADAM_B1 ** ADAM_STEP)
    v_hat = v / (1.0 - ADAM_B2 ** ADAM_STEP)
    delta = -ADAM_LR * (m_hat / (jnp.sqrt(v_hat) + ADAM_EPS) + ADAM_WD * w)
    return delta, m, v


def _adamw_call(items, *, name, tm=256, task=None):
    n = len(items)
    cols = items[0][0].shape[1]
    tiles = [it[0].shape[0] // tm for it in items]
    steps = max(tiles)

    def body(*refs):
        i = pl.program_id(0)
        ins, outs = refs[:4 * n], refs[4 * n:]
        for k in range(n):
            def update(k=k):
                g = ins[4 * k + 1][...]
                res = _adamw(ins[4 * k][...], g, ins[4 * k + 2][...], ins[4 * k + 3][...])
                outs[4 * k][...] = g
                for j in range(3):
                    outs[4 * k + 1 + j][...] = res[j]
            if tiles[k] == steps:
                update()
            else:
                pl.when(i < tiles[k])(update)

    in_specs, out_specs, out_shape, args = [], [], [], []
    for it, t in zip(items, tiles):
        spec = pl.BlockSpec((tm, cols), lambda i, t=t: (jnp.minimum(i, t - 1), 0))
        in_specs += [spec] * 4
        out_specs += [spec] * 4
        out_shape += [jax.ShapeDtypeStruct(it[0].shape, F32)] * 4
        args += list(it)
    res = _pallas(body, grid=(steps,), in_specs=in_specs, out_specs=out_specs, out_shape=out_shape,
                  name=name, task=task)(*args)
    outs, extra = res if task is not None else (res, None)
    grouped = [tuple(outs[4 * k:4 * k + 4]) for k in range(n)]
    return (grouped, extra) if task is not None else grouped


def _adamw_cols_call(w, g_pad, m, v, *, name, tn=256):
    rows, cols = w.shape

    def body(w_ref, g_ref, m_ref, v_ref, go_ref, d_ref, mo_ref, vo_ref):
        g = g_ref[0:rows, :]
        d, mn, vn = _adamw(w_ref[...], g, m_ref[...], v_ref[...])
        go_ref[...] = g
        d_ref[...] = d
        mo_ref[...] = mn
        vo_ref[...] = vn

    spec = pl.BlockSpec((rows, tn), lambda j: (0, j))
    gspec = pl.BlockSpec((g_pad.shape[0], tn), lambda j: (0, j))
    return _pallas(body, grid=(cols // tn,), in_specs=[spec, gspec, spec, spec], out_specs=(spec,) * 4,
                   out_shape=(jax.ShapeDtypeStruct((rows, cols), F32),) * 4, name=name)(w, g_pad, m, v)


N_DEV = 8
SMALL_ROWS = 24
SMALL_LAYOUT = {
    "g_mix_pre": (0, 0, 1, D), "g_mix_post": (1, 0, 1, D), "g_mem_kv": (2, 0, 1, D), "g_mem_pre": (3, 0, 1, D),
    "g_mem_post": (4, 0, 1, D), "g_ff_pre": (5, 0, 1, D), "g_ff_post": (6, 0, 1, D),
    "g_fox_out": (7, 0, 1, D_GRP), "g_chk_out": (7, D_GRP, 1, D_GRP), "b_fgt": (8, 0, 1, 8),
    "rel_bias": (16, 0, 8, N_REL),
}
SMALL = list(SMALL_LAYOUT)


LOSS_ROW = 9


def _small_call(grads, ws, ms, vs, loss_blk, task, *, name):
    n = len(SMALL)
    t_in, t_out = len(task.arrays), len(task.out_shapes)

    def body(*refs):
        g_refs, w_refs, m_refs, v_refs = (refs[j * n:(j + 1) * n] for j in range(4))
        p = 4 * n
        loss_ref, tins = refs[p], refs[p + 1:p + 1 + t_in]
        p += 1 + t_in
        outs, loss_out, touts = refs[p:p + 4 * n], refs[p + 4 * n], refs[p + 4 * n + 1:p + 4 * n + 1 + t_out]
        p += 4 * n + 1 + t_out
        mine, slots, send_sems, recv_sems = refs[p:p + 4]
        tsems = refs[p + 4:]
        task.issue(tins, touts, tsems)
        x, y, c, _ = _place()
        me = 4 * x + 2 * y + c
        mine[...] = jnp.zeros_like(mine)
        for k, name_k in enumerate(SMALL):
            r, l, nr, nl = SMALL_LAYOUT[name_k]
            mine[r:r + nr, l:l + nl] = g_refs[k][0:nr, 0:nl]
        mine[LOSS_ROW:LOSS_ROW + 1, 0:128] = loss_ref[0:1, :]
        slots[me] = mine[...]
        peers = [(dx, dy, dc) for dx in (0, 1) for dy in (0, 1) for dc in (0, 1)][1:]
        cps = []
        for t, (dx, dy, dc) in enumerate(peers):
            px, py, pc = (x + dx) % 2, (y + dy) % 2, (c + dc) % 2
            cps.append(pltpu.make_async_remote_copy(
                src_ref=mine, dst_ref=slots.at[me], send_sem=send_sems.at[t], recv_sem=recv_sems.at[t],
                device_id=(px, py, pc), device_id_type=MESH))
            cps[-1].start()
        for t, (dx, dy, dc) in enumerate(peers):
            px, py, pc = (x + dx) % 2, (y + dy) % 2, (c + dc) % 2
            pltpu.make_async_remote_copy(
                src_ref=mine, dst_ref=slots.at[4 * px + 2 * py + pc], send_sem=send_sems.at[t],
                recv_sem=recv_sems.at[t], device_id=(px, py, pc), device_id_type=MESH).wait_recv()
        for cp in cps:
            cp.wait_send()
        total = slots[0]
        for j in range(1, N_DEV):
            total = total + slots[j]
        for k, name_k in enumerate(SMALL):
            r, l, nr, nl = SMALL_LAYOUT[name_k]
            g = total[r:r + nr, l:l + nl]
            d, mn, vn = _adamw(w_refs[k][...], g, m_refs[k][...], v_refs[k][...])
            for j, val in enumerate((g, d, mn, vn)):
                outs[4 * k + j][...] = val
        loss_out[...] = jnp.broadcast_to(total[LOSS_ROW:LOSS_ROW + 1, 0:128], loss_out.shape)
        task.drain(tins, touts, tsems)

    vm = pl.BlockSpec(memory_space=pltpu.VMEM)
    out_shape = [jax.ShapeDtypeStruct(ws[k].shape, F32) for k in SMALL for _ in range(4)]
    out_shape += [jax.ShapeDtypeStruct((8, 128), F32)] + list(task.out_shapes)
    res = pl.pallas_call(
        body, in_specs=[vm] * (4 * n + 1) + [ANY] * t_in, out_specs=[vm] * (4 * n + 1) + [ANY] * t_out,
        out_shape=out_shape,
        scratch_shapes=[pltpu.VMEM((SMALL_ROWS, D), F32), pltpu.VMEM((N_DEV, SMALL_ROWS, D), F32),
                        _dma_sems(N_DEV - 1), _dma_sems(N_DEV - 1)] + list(task.sems),
        input_output_aliases={4 * n + 1 + i: 4 * n + 1 + j for i, j in task.aliases.items()},
        name=name,
    )(*[d[k] for d in (grads, ws, ms, vs) for k in SMALL], loss_blk, *task.arrays)
    return ({k: tuple(res[4 * i:4 * i + 4]) for i, k in enumerate(SMALL)}, res[4 * n], list(res[4 * n + 1:]))


WEIGHTS = ["w_in", "b_fgt", "rel_bias", "g_fox_out", "g_chk_out", "w_out", "g_mix_pre", "g_mix_post", "g_mem_kv",
           "w_mq", "w_mk", "w_mv", "w_mo", "g_mem_pre", "g_mem_post", "w_ff1", "w_ff2", "g_ff_pre", "g_ff_post"]
BIG = ["w_in", "w_out", "w_mq", "w_mk", "w_mv", "w_mo", "w_ff1", "w_ff2"]


IN_SHARD = D_IN // N_CHIP
IN_PAD = 800


IN_PIECES = [(0, 0, 770), (800, 770, 766), (1566, 3072, 4), (1600, 3076, 4), (1604, 1536, 766), (2400, 2302, 770)]
PAD_ZEROS = [(800 * j + IN_SHARD, IN_PAD - IN_SHARD) for j in range(N_CHIP)]
ALL_ZEROS = [(D_IN, D_ALL - D_IN)]


def _reorder_rows_call(src, to_all, *, name, tn=256):
    rows, cols = src.shape
    zeros = ALL_ZEROS if to_all else PAD_ZEROS

    def body(s_ref, o_ref):
        for pad0, all0, cnt in IN_PIECES:
            s0, d0 = (pad0, all0) if to_all else (all0, pad0)
            o_ref[d0:d0 + cnt, :] = s_ref[s0:s0 + cnt, :]
        for z0, cnt in zeros:
            o_ref[z0:z0 + cnt, :] = jnp.zeros((cnt, tn), src.dtype)

    spec = pl.BlockSpec((rows, tn), lambda j: (0, j))
    return _pallas(body, grid=(cols // tn,), in_specs=[spec], out_specs=spec,
                   out_shape=jax.ShapeDtypeStruct((rows, cols), src.dtype), name=name)(src)


def kernel(x, mem, w_in, b_fgt, rel_bias, g_fox_out, g_chk_out, w_out, g_mix_pre, g_mix_post, g_mem_kv, w_mq, w_mk, w_mv, w_mo, g_mem_pre, g_mem_post, w_ff1, w_ff2, g_ff_pre, g_ff_post, loss_target, m_w_in, m_b_fgt, m_rel_bias, m_g_fox_out, m_g_chk_out, m_w_out, m_g_mix_pre, m_g_mix_post, m_g_mem_kv, m_w_mq, m_w_mk, m_w_mv, m_w_mo, m_g_mem_pre, m_g_mem_post, m_w_ff1, m_w_ff2, m_g_ff_pre, m_g_ff_post, v_w_in, v_b_fgt, v_rel_bias, v_g_fox_out, v_g_chk_out, v_w_out, v_g_mix_pre, v_g_mix_post, v_g_mem_kv, v_w_mq, v_w_mk, v_w_mv, v_w_mo, v_g_mem_pre, v_g_mem_post, v_w_ff1, v_w_ff2, v_g_ff_pre, v_g_ff_post):
    w = dict(w_in=w_in, b_fgt=b_fgt, rel_bias=rel_bias, g_fox_out=g_fox_out, g_chk_out=g_chk_out, w_out=w_out,
             g_mix_pre=g_mix_pre, g_mix_post=g_mix_post, g_mem_kv=g_mem_kv, w_mq=w_mq, w_mk=w_mk, w_mv=w_mv,
             w_mo=w_mo, g_mem_pre=g_mem_pre, g_mem_post=g_mem_post, w_ff1=w_ff1, w_ff2=w_ff2, g_ff_pre=g_ff_pre,
             g_ff_post=g_ff_post)
    m = dict(w_in=m_w_in, b_fgt=m_b_fgt, rel_bias=m_rel_bias, g_fox_out=m_g_fox_out, g_chk_out=m_g_chk_out,
             w_out=m_w_out, g_mix_pre=m_g_mix_pre, g_mix_post=m_g_mix_post, g_mem_kv=m_g_mem_kv, w_mq=m_w_mq,
             w_mk=m_w_mk, w_mv=m_w_mv, w_mo=m_w_mo, g_mem_pre=m_g_mem_pre, g_mem_post=m_g_mem_post,
             w_ff1=m_w_ff1, w_ff2=m_w_ff2, g_ff_pre=m_g_ff_pre, g_ff_post=m_g_ff_post)
    v = dict(w_in=v_w_in, b_fgt=v_b_fgt, rel_bias=v_rel_bias, g_fox_out=v_g_fox_out, g_chk_out=v_g_chk_out,
             w_out=v_w_out, g_mix_pre=v_g_mix_pre, g_mix_post=v_g_mix_post, g_mem_kv=v_g_mem_kv, w_mq=v_w_mq,
             w_mk=v_w_mk, w_mv=v_w_mv, w_mo=v_w_mo, g_mem_pre=v_g_mem_pre, g_mem_post=v_g_mem_post,
             w_ff1=v_w_ff1, w_ff2=v_w_ff2, g_ff_pre=v_g_ff_pre, g_ff_post=v_g_ff_post)

    def rows(d, k):
        return d[k][0] if k == "rel_bias" else d[k]

    xs, mems, target = x[0], mem[0], loss_target[0]
    S = xs.shape[0]
    sp = {k: rows(w, k) for k in SMALL}
    b_pad = jnp.pad(sp["b_fgt"], ((0, 0), (0, 120)))
    chip = 2 * lax.axis_index("x") + lax.axis_index("y")
    chip_arr = jnp.reshape(chip, (1,)).astype(jnp.int32)
    c_arr = jnp.reshape(lax.axis_index("c"), (1,)).astype(jnp.int32)
    place_arr = jnp.concatenate([chip_arr, c_arr])
    w_in_t, m_in_t, v_in_t = w["w_in"][0].T, m["w_in"][0].T, v["w_in"][0].T
    slab = {k: _cast_slab_call(w[k][0], chip_arr, name="cast_" + k) for k in BIG[1:]}
    slab["w_in"] = _cast_slab_call(w_in_t, chip_arr, name="cast_w_in", pad_rows=IN_PAD - IN_SHARD)

    def gather_ici(names):
        return _ag_ici_task([slab[k] for k in names])

    def pair_add(k, d, r1):
        return _pair_add_call(d, r1, c_arr, name="rs_pair_add_" + k)

    g_in, = _comm_call(gather_ici(["w_in"]), name="ag_w_in")
    h1, (g_in,) = _rms_fwd_call(xs, sp["g_mix_pre"], name="rms_mix_pre", task=_ag_d2d_task([g_in]))
    w_all_t = _reorder_rows_call(g_in.reshape(N_CHIP * IN_PAD, D), True, name="w_in_rows")
    proj, (g_out,) = _mm_nt(h1, w_all_t, "plain", rows=(0, 3072), name="mm_proj", task=gather_ici(["w_out"]))
    fl_raw = _mm_nt(h1, w_all_t, "plain", rows=(3072, 128), name="mm_gate", out_dtype=F32, tn=128)
    c_rep, c_t = _fox_prep_call(fl_raw, b_pad, name="fox_prep")
    bias = _chk_bias_call(_rel_table_to_g(sp["rel_bias"]), name="chk_bias")
    mid = ["w_mq", "w_mk", "w_mv", "w_mo", "w_ff1"]
    (yf, lse), got = _fox_fwd_call(proj, c_rep, c_t, name="fox_fwd",
                                   task=_merge_tasks([gather_ici(mid), _ag_d2d_task([g_out])]))
    g_mid, g_out = got[:5], got[5]
    yc, got = _chk_fwd_call(proj, bias, name="chk_fwd",
                            task=_merge_tasks([gather_ici(["w_ff2"]), _ag_d2d_task(g_mid)]))
    g_ff2, (g_mq, g_mk, g_mv, g_mo, g_ff1) = got[0], got[1:]
    yn = _mix_norm_fwd_call(yf, yc, sp["g_fox_out"], sp["g_chk_out"], name="mix_norm_fwd")
    z, (g_ff2,) = _mm_nn(yn, g_out, "rows", name="mm_out", out_dtype=F32, task=_ag_d2d_task([g_ff2]))
    x1, h2 = _post_pre_call(xs, z, sp["g_mix_post"], sp["g_mem_pre"], name="post_mix")
    memn = _rms_fwd_call(mems, sp["g_mem_kv"], name="rms_mem_kv")
    q2 = _mm_nn(h2, g_mq, "rows", name="mm_mq")
    k2 = _mm_nn(memn, g_mk, "rows", name="mm_mk")
    v2 = _mm_nn(memn, g_mv, "rows", name="mm_mv")
    o2 = _mem_fwd_call(q2, k2, v2, name="mem_fwd")
    y2 = _mm_nn(o2, g_mo, "rows", name="mm_mo", out_dtype=F32)
    x2, h3 = _post_pre_call(x1, y2, sp["g_mem_post"], sp["g_ff_pre"], name="post_mem")
    act, relu = _mm_nn(h3, g_ff1, "cols", name="mm_ff1", epi="relu2")
    y3 = _mm_nn(act, g_ff2, "rows", name="mm_ff2", out_dtype=F32, tm=1024)
    loss_blk, dx3, dy3, dg_ff_post = _final_call(x2, y3, sp["g_ff_post"], target, name="final")

    d_ff2 = _mm_tn(act, dy3, name="mm_dff2", tk=512, tn=1024).reshape(N_CHIP, D_FF // N_CHIP, D)
    du, (r1,) = _mm_nt(dy3, g_ff2, "rows", name="mm_du", mul2r=relu, task=_rs_pair_task([d_ff2]))
    p_ff2 = pair_add("w_ff2", d_ff2, r1)
    d_ff1 = _mm_tn(h3, du, name="mm_dff1", cols4=True)
    dh3, (r1,) = _mm_nt(du, g_ff1, "cols", name="mm_dh3", out_dtype=F32, tm=1024, task=_rs_pair_task([d_ff1]))
    p_ff1 = pair_add("w_ff1", d_ff1, r1)
    dx2, dy2, dg_ff_pre, dg_mem_post = _bwd_mid_call(dx3, x2, dh3, sp["g_ff_pre"], y2, sp["g_mem_post"], name="bwd_ff")
    d_mo = _mm_tn(o2, dy2, name="mm_dmo").reshape(N_CHIP, D // N_CHIP, D)
    do2 = _mm_nt(dy2, g_mo, "rows", name="mm_do2")
    dq2, dk2, dv2 = _mem_bwd_call(q2, k2, v2, do2, name="mem_bwd")
    d_mq = _mm_tn(h2, dq2, name="mm_dmq").reshape(N_CHIP, D // N_CHIP, D)
    dh2 = _mm_nt(dq2, g_mq, "rows", name="mm_dh2", out_dtype=F32)
    d_mk = _mm_tn(memn, dk2, name="mm_dmk").reshape(N_CHIP, D // N_CHIP, D)
    d_mv = _mm_tn(memn, dv2, name="mm_dmv").reshape(N_CHIP, D // N_CHIP, D)
    dmn_k = _mm_nt(dk2, g_mk, "rows", name="mm_dmemk", out_dtype=F32)
    dmn_v = _mm_nt(dv2, g_mv, "rows", name="mm_dmemv", out_dtype=F32)
    dg_mem_kv = _gain_grad_call(mems, sp["g_mem_kv"], dmn_k, dmn_v, name="gain_mem_kv")
    dx1, dz, dg_mem_pre, dg_mix_post = _bwd_mid_call(dx2, x1, dh2, sp["g_mem_pre"], z, sp["g_mix_post"], name="bwd_mem")
    d_out = _mm_tn(yn, dz, name="mm_dout").reshape(N_CHIP, D // N_CHIP, D)
    late = ["w_mo", "w_mq", "w_mk", "w_mv", "w_out"]
    d_late = [d_mo, d_mq, d_mk, d_mv, d_out]
    dyn, r1_late = _mm_nt(dz, g_out, "rows", name="mm_dyn", out_dtype=F32, task=_rs_pair_task(d_late))
    p_late = [pair_add(k, d, r1) for k, d, r1 in zip(late, d_late, r1_late)]
    dof, doc, delta, dg_fox, dg_chk = _mix_norm_bwd_call(dyn, yf, yc, sp["g_fox_out"], sp["g_chk_out"], name="mix_norm_bwd")
    (dqf, dkf, dvf, dct, dcq), r2_ff = _fox_bwd_call(proj, dof, lse, delta, c_rep, c_t, name="fox_bwd",
                                                      task=_rs_chip_task([p_ff2, p_ff1]))
    (dqc, dkc, dvc, dgrev), r2_late = _chk_bwd_call(proj, doc, bias, name="chk_bwd", task=_rs_chip_task(p_late))
    first = ["w_ff2", "w_ff1"] + late
    f_first = [_chip_sum_call(p, r, place_arr, name="rs_chip_sum_" + k)
               for k, p, r in zip(first, [p_ff2, p_ff1] + p_late, r2_ff + r2_late)]
    dc8 = dct[:, :, 0:2, :].transpose(0, 2, 1, 3).reshape(8, S) + dcq[:, ::HEAD].T
    dc_rows = jnp.concatenate([dc8, jnp.zeros((120, S), F32)], axis=0)
    dfl, db_fgt = _fox_gate_bwd_call(dc_rows, fl_raw, b_pad, name="fox_gate_bwd")
    dproj = jnp.concatenate([dqf, dkf, dvf, dqc, dkc, dvc, dfl], axis=1)
    d_all_t, g_first = _mm_tn(dproj, h1, name="mm_dwin", tk=640, tn=1024, task=_rs_gather_task(f_first))
    grads = dict(zip(first, g_first))
    d_in = _reorder_rows_call(d_all_t, False, name="d_in_rows").reshape(N_CHIP, IN_PAD, D)
    delta_w, new_m, new_v = {}, {}, {}

    def adamw_items(names):
        return [(w[k][0], grads[k], m[k][0], v[k][0]) for k in names]

    upd_late, (r1,) = _adamw_call(adamw_items(late), name="adamw_late", tm=64, task=_rs_pair_task([d_in]))
    p_in = pair_add("w_in", d_in, r1)
    dh1, (r2_in,) = _mm_nn(dproj, w_all_t, "plain", name="mm_dh1", out_dtype=F32, tm=1024,
                           task=_rs_chip_task([p_in]))
    f_in = _chip_sum_call(p_in, r2_in, place_arr, name="rs_chip_sum_w_in")
    upd_ff = _adamw_call(adamw_items(first[:2]), name="adamw_ff")
    for k, res in zip(late + first[:2], upd_late + upd_ff):
        grads[k], delta_w[k], new_m[k], new_v[k] = res
    grad_x, dg_mix_pre = _bwd_last_call(dx1, xs, dh1, sp["g_mix_pre"], name="bwd_mix")

    small_g = {"g_mix_pre": dg_mix_pre, "g_mix_post": dg_mix_post, "g_mem_kv": dg_mem_kv, "g_mem_pre": dg_mem_pre,
               "g_mem_post": dg_mem_post, "g_ff_pre": dg_ff_pre, "g_ff_post": dg_ff_post, "g_fox_out": dg_fox,
               "g_chk_out": dg_chk, "b_fgt": db_fgt,
               "rel_bias": _g_to_rel_table(dgrev[:, 0:2, :].reshape(8, ROLL_W))}
    small, loss_out, (g_w_in,) = _small_call(
        small_g, sp, {k: rows(m, k) for k in SMALL}, {k: rows(v, k) for k in SMALL}, loss_blk,
        _rs_gather_task([f_in]), name="small_allreduce_adamw")
    loss = loss_out[0, 0]
    res = _adamw_cols_call(w_in_t, g_w_in, m_in_t, v_in_t, name="adamw_w_in")
    grads["w_in"], delta_w["w_in"], new_m["w_in"], new_v["w_in"] = (a.T for a in res)
    for k in SMALL:
        vals = small[k]
        if k == "rel_bias":
            vals = tuple(a[None] for a in vals)
        grads[k], delta_w[k], new_m[k], new_v[k] = vals

    def out(d, k):
        return d[k][None] if k in BIG else d[k]

    return (loss, grad_x[None], *[out(grads, k) for k in WEIGHTS], *[out(delta_w, k) for k in WEIGHTS],
            *[out(new_m, k) for k in WEIGHTS], *[out(new_v, k) for k in WEIGHTS])
```

```python
import functools

import jax
import jax.numpy as jnp
from jax import lax
from jax.experimental import pallas as pl
from jax.experimental.pallas import tpu as pltpu

F32 = jnp.float32
BF16 = jnp.bfloat16

D = 1024
HEAD = 64
N_PAIR = 4
D_GRP = 512
CHUNK = 64
LEFT = 8
MAX_REL = 128
N_REL = 2 * MAX_REL + 1
N_MEM = 256
MEM_HEADS = 4
MEM_HD = 256
D_FF = 4096
D_IN = 3080
D_ALL = 3200
EPS = 1e-6
TQ = 256
WIN = (LEFT + TQ // CHUNK) * CHUNK
PADK = LEFT * CHUNK
ROLL_W = 1024
NEG = -1e30
N_CHIP = 4
VMEM_LIMIT = 48 * 1024 * 1024

ADAM_LR = 0.001
ADAM_B1 = 0.9
ADAM_B2 = 0.999
ADAM_EPS = 1e-08
ADAM_WD = 0.01
ADAM_STEP = 10

MESH = pl.DeviceIdType.MESH


def _cparams():
    return pltpu.CompilerParams(vmem_limit_bytes=VMEM_LIMIT)


ANY = pl.BlockSpec(memory_space=pl.ANY)


class _Task:
    def __init__(self, arrays, out_shapes, sems, issue, drain, aliases=None):
        self.arrays, self.out_shapes, self.sems = list(arrays), list(out_shapes), list(sems)
        self.issue, self.drain, self.aliases = issue, drain, dict(aliases or {})


def _merge_tasks(tasks):
    tasks = [t for t in tasks if t is not None]
    if len(tasks) == 1:
        return tasks[0]
    cuts, a, o, s = [], 0, 0, 0
    aliases = {}
    for t in tasks:
        cuts.append((a, o, s))
        aliases.update({a + i: o + j for i, j in t.aliases.items()})
        a, o, s = a + len(t.arrays), o + len(t.out_shapes), s + len(t.sems)

    def part(fn_name):
        def run(ins, outs, sems):
            for t, (a0, o0, s0) in zip(tasks, cuts):
                getattr(t, fn_name)(ins[a0:a0 + len(t.arrays)], outs[o0:o0 + len(t.out_shapes)],
                                    sems[s0:s0 + len(t.sems)])
        return run

    return _Task([x for t in tasks for x in t.arrays], [x for t in tasks for x in t.out_shapes],
                 [x for t in tasks for x in t.sems], part("issue"), part("drain"), aliases)


def _pallas(body, *, grid, in_specs, out_specs, out_shape, name, scratch_shapes=(), task=None):
    if task is None:
        return pl.pallas_call(body, grid=grid, in_specs=list(in_specs), out_specs=out_specs, out_shape=out_shape,
                              scratch_shapes=list(scratch_shapes), name=name, compiler_params=_cparams())
    single = not isinstance(out_shape, (tuple, list))
    o_shapes = [out_shape] if single else list(out_shape)
    o_specs = [out_specs] if single else list(out_specs)
    n_in, n_out, n_scr = len(in_specs), len(o_shapes), len(scratch_shapes)
    t_in, t_out = len(task.arrays), len(task.out_shapes)

    def carried(*refs):
        cut = [n_in, t_in, n_out, t_out, n_scr]
        parts, p = [], 0
        for c in cut:
            parts.append(refs[p:p + c])
            p += c
        ins, tins, outs, touts, scr = parts
        tsems = refs[p:]
        ids = [pl.program_id(a) for a in range(len(grid))]
        first = functools.reduce(jnp.logical_and, [i == 0 for i in ids])
        last = functools.reduce(jnp.logical_and, [i == g - 1 for i, g in zip(ids, grid)])

        @pl.when(first)
        def _():
            task.issue(tins, touts, tsems)
        body(*ins, *outs, *scr)

        @pl.when(last)
        def _():
            task.drain(tins, touts, tsems)

    call = pl.pallas_call(
        carried, grid=grid, in_specs=list(in_specs) + [ANY] * t_in, out_specs=o_specs + [ANY] * t_out,
        out_shape=o_shapes + list(task.out_shapes), scratch_shapes=list(scratch_shapes) + list(task.sems),
        input_output_aliases={n_in + i: n_out + j for i, j in task.aliases.items()},
        name=name, compiler_params=_cparams())

    def run(*args):
        res = call(*args, *task.arrays)
        outs = res[:n_out]
        return (outs[0] if single else tuple(outs)), list(res[n_out:])

    return run


def _comm_call(task, *, name):
    t_in, t_out = len(task.arrays), len(task.out_shapes)

    def body(*refs):
        tins, touts, tsems = refs[:t_in], refs[t_in:t_in + t_out], refs[t_in + t_out:]
        task.issue(tins, touts, tsems)
        task.drain(tins, touts, tsems)

    return pl.pallas_call(
        body, in_specs=[ANY] * t_in, out_specs=[ANY] * t_out, out_shape=list(task.out_shapes),
        scratch_shapes=list(task.sems), input_output_aliases=dict(task.aliases), name=name,
    )(*task.arrays)


def _dot(a, b):
    return jnp.dot(a, b, preferred_element_type=F32)


def _dot_nt(a, b):
    return lax.dot_general(a, b, (((1,), (1,)), ((), ())), preferred_element_type=F32)


def _dot_tn(a, b):
    return lax.dot_general(a, b, (((0,), (0,)), ((), ())), preferred_element_type=F32)


def _split3(x):
    hi = x.astype(BF16)
    r1 = x - hi.astype(F32)
    mid = r1.astype(BF16)
    lo = (r1 - mid.astype(F32)).astype(BF16)
    return hi, mid, lo


def _dot3(x, m01):
    hi, mid, lo = _split3(x)
    return _dot(hi, m01) + _dot(mid, m01) + _dot(lo, m01)


def _dot3_l(m01, x):
    hi, mid, lo = _split3(x)
    return _dot(m01, hi) + _dot(m01, mid) + _dot(m01, lo)


def _mm_nn(a, b, kind, *, name, out_dtype=BF16, tm=2048, tn=512, epi=None, task=None):
    M, K = a.shape
    if kind == "plain":
        N = b.shape[1]
        b_spec = pl.BlockSpec((K, tn), lambda m, n: (0, n))
    elif kind == "rows":
        N = b.shape[2]
        b_spec = pl.BlockSpec((N_CHIP, K // N_CHIP, tn), lambda m, n: (0, 0, n))
    else:
        nq = b.shape[2]
        N = N_CHIP * nq
        per = nq // tn
        b_spec = pl.BlockSpec((None, K, tn), lambda m, n: (n // per, 0, n % per))
    tm = min(tm, M)
    kq = K // N_CHIP

    def body(a_ref, b_ref, *o_refs):
        if kind == "rows":
            acc = _dot(a_ref[:, 0:kq], b_ref[0])
            for j in range(1, N_CHIP):
                acc += _dot(a_ref[:, j * kq:(j + 1) * kq], b_ref[j])
        else:
            acc = _dot(a_ref[...], b_ref[...])
        if epi == "relu2":
            r = jnp.maximum(acc, 0.0)
            o_refs[0][...] = (r * r).astype(BF16)
            o_refs[1][...] = r.astype(BF16)
        else:
            o_refs[0][...] = acc.astype(out_dtype)

    o_spec = pl.BlockSpec((tm, tn), lambda m, n: (m, n))
    if epi == "relu2":
        out_shape = (jax.ShapeDtypeStruct((M, N), BF16), jax.ShapeDtypeStruct((M, N), BF16))
        out_specs = (o_spec, o_spec)
    else:
        out_shape = jax.ShapeDtypeStruct((M, N), out_dtype)
        out_specs = o_spec
    return _pallas(
        body, grid=(M // tm, N // tn),
        in_specs=[pl.BlockSpec((tm, K), lambda m, n: (m, 0)), b_spec],
        out_specs=out_specs, out_shape=out_shape, name=name, task=task,
    )(a, b)


def _mm_nt(a, b, kind, *, name, out_dtype=BF16, tm=2048, tn=512, mul2r=None, task=None, rows=None):
    M, K = a.shape
    if kind == "plain":
        first, N = rows if rows is not None else (0, b.shape[0])
        n0 = first // tn
        b_spec = pl.BlockSpec((tn, K), lambda m, n: (n0 + n, 0))
    elif kind == "rows":
        nq = b.shape[1]
        N = N_CHIP * nq
        tn = min(tn, nq)
        per = nq // tn
        b_spec = pl.BlockSpec((None, tn, K), lambda m, n: (n // per, n % per, 0))
    else:
        N = b.shape[1]
        b_spec = pl.BlockSpec((N_CHIP, tn, K // N_CHIP), lambda m, n: (0, n, 0))
    tm = min(tm, M)
    kq = K // N_CHIP

    def body(a_ref, b_ref, *rest):
        o_ref = rest[-1]
        if kind == "cols":
            acc = _dot_nt(a_ref[:, 0:kq], b_ref[0])
            for j in range(1, N_CHIP):
                acc += _dot_nt(a_ref[:, j * kq:(j + 1) * kq], b_ref[j])
        else:
            acc = _dot_nt(a_ref[...], b_ref[...])
        if mul2r is not None:
            acc = acc * (2.0 * rest[0][...].astype(F32))
        o_ref[...] = acc.astype(out_dtype)

    in_specs = [pl.BlockSpec((tm, K), lambda m, n: (m, 0)), b_spec]
    args = [a, b]
    if mul2r is not None:
        in_specs.append(pl.BlockSpec((tm, tn), lambda m, n: (m, n)))
        args.append(mul2r)
    return _pallas(
        body, grid=(M // tm, N // tn), in_specs=in_specs,
        out_specs=pl.BlockSpec((tm, tn), lambda m, n: (m, n)),
        out_shape=jax.ShapeDtypeStruct((M, N), out_dtype), name=name, task=task,
    )(*args)


def _mm_tn(a, b, *, name, out_dtype=BF16, tk=1024, tn=512, cols4=False, task=None):
    M, K1 = a.shape
    N = b.shape[1]
    tk = min(tk, K1)
    tn = min(tn, N)

    def body(a_ref, b_ref, o_ref):
        o_ref[...] = _dot_tn(a_ref[...], b_ref[...]).astype(out_dtype)

    if cols4:
        per = (N // N_CHIP) // tn
        out_shape = jax.ShapeDtypeStruct((N_CHIP, K1, N // N_CHIP), out_dtype)
        o_spec = pl.BlockSpec((None, tk, tn), lambda k, n: (n // per, k, n % per))
    else:
        out_shape = jax.ShapeDtypeStruct((K1, N), out_dtype)
        o_spec = pl.BlockSpec((tk, tn), lambda k, n: (k, n))
    return _pallas(
        body, grid=(K1 // tk, N // tn),
        in_specs=[pl.BlockSpec((M, tk), lambda k, n: (0, k)), pl.BlockSpec((M, tn), lambda k, n: (0, n))],
        out_specs=o_spec, out_shape=out_shape, name=name, task=task,
    )(a, b)


def _rms(x, g):
    r = lax.rsqrt(jnp.mean(x * x, axis=-1, keepdims=True) + EPS)
    return x * r * g


def _rms_bwd(x, g, dy):
    r = lax.rsqrt(jnp.mean(x * x, axis=-1, keepdims=True) + EPS)
    xh = x * r
    dg = jnp.sum(dy * xh, axis=0, keepdims=True)
    dxh = dy * g
    dx = r * (dxh - xh * jnp.mean(dxh * xh, axis=-1, keepdims=True))
    return dx, dg


def _row_spec(tm, n):
    return pl.BlockSpec((tm, n), lambda i: (i, 0))


def _vec_spec(n):
    return pl.BlockSpec((1, n), lambda i: (0, 0))


def _acc_spec(n):
    return pl.BlockSpec((8, n), lambda i: (0, 0))


def _acc_add(ref, row, i):
    @pl.when(i == 0)
    def _():
        ref[...] = jnp.zeros_like(ref)
    ref[0:1, :] += row


def _rms_fwd_call(x, g, *, name, tm=256, task=None):
    M, n = x.shape
    tm = min(tm, M)

    def body(x_ref, g_ref, h_ref):
        h_ref[...] = _rms(x_ref[...], g_ref[...]).astype(BF16)

    return _pallas(
        body, grid=(M // tm,), in_specs=[_row_spec(tm, n), _vec_spec(n)], out_specs=_row_spec(tm, n),
        out_shape=jax.ShapeDtypeStruct((M, n), BF16), name=name, task=task,
    )(x, g)


def _post_pre_call(xres, z, g_post, g_pre, *, name, tm=256):
    M, n = xres.shape

    def body(x_ref, z_ref, gp_ref, gn_ref, xo_ref, h_ref):
        xn = x_ref[...] + _rms(z_ref[...], gp_ref[...])
        xo_ref[...] = xn
        h_ref[...] = _rms(xn, gn_ref[...]).astype(BF16)

    return pl.pallas_call(
        body, grid=(M // tm,),
        in_specs=[_row_spec(tm, n), _row_spec(tm, n), _vec_spec(n), _vec_spec(n)],
        out_specs=(_row_spec(tm, n), _row_spec(tm, n)),
        out_shape=(jax.ShapeDtypeStruct((M, n), F32), jax.ShapeDtypeStruct((M, n), BF16)),
        name=name, compiler_params=_cparams(),
    )(xres, z, g_post, g_pre)


def _final_call(x2, y3, g_post, target, *, name, tm=256):
    M, n = x2.shape

    def body(x_ref, y_ref, g_ref, t_ref, loss_ref, dx_ref, dy_ref, dg_ref):
        i = pl.program_id(0)
        y = y_ref[...]
        g = g_ref[...]
        diff = x_ref[...] + _rms(y, g) - t_ref[...]
        part = 0.5 * jnp.sum(jnp.sum(diff * diff, axis=1, keepdims=True), axis=0, keepdims=True) / n

        @pl.when(i == 0)
        def _():
            loss_ref[...] = jnp.zeros_like(loss_ref)
        loss_ref[...] += jnp.broadcast_to(part, loss_ref.shape)
        dx = diff / n
        dx_ref[...] = dx
        dy, dg = _rms_bwd(y, g, dx)
        dy_ref[...] = dy.astype(BF16)
        _acc_add(dg_ref, dg, i)

    return pl.pallas_call(
        body, grid=(M // tm,),
        in_specs=[_row_spec(tm, n), _row_spec(tm, n), _vec_spec(n), _row_spec(tm, n)],
        out_specs=(pl.BlockSpec((8, 128), lambda i: (0, 0)), _row_spec(tm, n), _row_spec(tm, n), _acc_spec(n)),
        out_shape=(jax.ShapeDtypeStruct((8, 128), F32), jax.ShapeDtypeStruct((M, n), F32),
                   jax.ShapeDtypeStruct((M, n), BF16), jax.ShapeDtypeStruct((8, n), F32)),
        name=name, compiler_params=_cparams(),
    )(x2, y3, g_post, target)


def _bwd_mid_call(dx_in, x, dh, g_pre, y, g_post, *, name, tm=256):
    M, n = x.shape

    def body(dxi_ref, x_ref, dh_ref, gpre_ref, y_ref, gpost_ref, dx_ref, dy_ref, dgpre_ref, dgpost_ref):
        i = pl.program_id(0)
        d1, dg1 = _rms_bwd(x_ref[...], gpre_ref[...], dh_ref[...])
        dx = dxi_ref[...] + d1
        dx_ref[...] = dx
        dy, dg2 = _rms_bwd(y_ref[...], gpost_ref[...], dx)
        dy_ref[...] = dy.astype(BF16)
        _acc_add(dgpre_ref, dg1, i)
        _acc_add(dgpost_ref, dg2, i)

    return pl.pallas_call(
        body, grid=(M // tm,),
        in_specs=[_row_spec(tm, n), _row_spec(tm, n), _row_spec(tm, n), _vec_spec(n), _row_spec(tm, n), _vec_spec(n)],
        out_specs=(_row_spec(tm, n), _row_spec(tm, n), _acc_spec(n), _acc_spec(n)),
        out_shape=(jax.ShapeDtypeStruct((M, n), F32), jax.ShapeDtypeStruct((M, n), BF16),
                   jax.ShapeDtypeStruct((8, n), F32), jax.ShapeDtypeStruct((8, n), F32)),
        name=name, compiler_params=_cparams(),
    )(dx_in, x, dh, g_pre, y, g_post)


def _bwd_last_call(dx_in, x, dh, g_pre, *, name, tm=256, task=None):
    M, n = x.shape

    def body(dxi_ref, x_ref, dh_ref, g_ref, dx_ref, dg_ref):
        i = pl.program_id(0)
        d1, dg1 = _rms_bwd(x_ref[...], g_ref[...], dh_ref[...])
        dx_ref[...] = dxi_ref[...] + d1
        _acc_add(dg_ref, dg1, i)

    return _pallas(
        body, grid=(M // tm,),
        in_specs=[_row_spec(tm, n), _row_spec(tm, n), _row_spec(tm, n), _vec_spec(n)],
        out_specs=(_row_spec(tm, n), _acc_spec(n)),
        out_shape=(jax.ShapeDtypeStruct((M, n), F32), jax.ShapeDtypeStruct((8, n), F32)),
        name=name, task=task,
    )(dx_in, x, dh, g_pre)


def _gain_grad_call(x, g, dy_a, dy_b, *, name):
    M, n = x.shape

    def body(x_ref, g_ref, a_ref, b_ref, dg_ref):
        _, dg = _rms_bwd(x_ref[...], g_ref[...], a_ref[...] + b_ref[...])
        dg_ref[...] = jnp.zeros_like(dg_ref)
        dg_ref[0:1, :] = dg

    return pl.pallas_call(
        body, grid=(1,),
        in_specs=[_row_spec(M, n), _vec_spec(n), _row_spec(M, n), _row_spec(M, n)],
        out_specs=_acc_spec(n), out_shape=jax.ShapeDtypeStruct((8, n), F32),
        name=name, compiler_params=_cparams(),
    )(x, g, dy_a, dy_b)


def _head_group_matrix():
    a = lax.broadcasted_iota(jnp.int32, (D_GRP, D_GRP), 0) // HEAD
    b = lax.broadcasted_iota(jnp.int32, (D_GRP, D_GRP), 1) // HEAD
    return jnp.where(a == b, 1.0, 0.0).astype(BF16)


def _mix_norm_fwd_call(yf, yc, gf, gc, *, name, tm=256):
    M = yf.shape[0]

    def body(yf_ref, yc_ref, gf_ref, gc_ref, o_ref):
        o_ref[:, 0:D_GRP] = _rms(yf_ref[...], gf_ref[...]).astype(BF16)
        o_ref[:, D_GRP:D] = _rms(yc_ref[...], gc_ref[...]).astype(BF16)

    return pl.pallas_call(
        body, grid=(M // tm,),
        in_specs=[_row_spec(tm, D_GRP), _row_spec(tm, D_GRP), _vec_spec(D_GRP), _vec_spec(D_GRP)],
        out_specs=_row_spec(tm, D), out_shape=jax.ShapeDtypeStruct((M, D), BF16),
        name=name, compiler_params=_cparams(),
    )(yf, yc, gf, gc)


def _mix_norm_bwd_call(dyn, yf, yc, gf, gc, *, name, tm=256):
    M = yf.shape[0]

    def body(dyn_ref, yf_ref, yc_ref, gf_ref, gc_ref, dof_ref, doc_ref, delta_ref, dgf_ref, dgc_ref):
        i = pl.program_id(0)
        yf_ = yf_ref[...]
        dof, dgf = _rms_bwd(yf_, gf_ref[...], dyn_ref[:, 0:D_GRP])
        doc, dgc = _rms_bwd(yc_ref[...], gc_ref[...], dyn_ref[:, D_GRP:D])
        dof_b = dof.astype(BF16)
        dof_ref[...] = dof_b
        doc_ref[...] = doc.astype(BF16)
        prod = dof_b.astype(F32) * yf_
        hi = prod.astype(BF16)
        lo = (prod - hi.astype(F32)).astype(BF16)
        grp = _head_group_matrix()
        delta_ref[...] = _dot(hi, grp) + _dot(lo, grp)
        _acc_add(dgf_ref, dgf, i)
        _acc_add(dgc_ref, dgc, i)

    return pl.pallas_call(
        body, grid=(M // tm,),
        in_specs=[_row_spec(tm, D), _row_spec(tm, D_GRP), _row_spec(tm, D_GRP), _vec_spec(D_GRP), _vec_spec(D_GRP)],
        out_specs=(_row_spec(tm, D_GRP), _row_spec(tm, D_GRP), _row_spec(tm, D_GRP), _acc_spec(D_GRP), _acc_spec(D_GRP)),
        out_shape=(jax.ShapeDtypeStruct((M, D_GRP), BF16), jax.ShapeDtypeStruct((M, D_GRP), BF16),
                   jax.ShapeDtypeStruct((M, D_GRP), F32), jax.ShapeDtypeStruct((8, D_GRP), F32),
                   jax.ShapeDtypeStruct((8, D_GRP), F32)),
        name=name, compiler_params=_cparams(),
    )(dyn, yf, yc, gf, gc)


def _tri(n, lower_incl):
    a = lax.broadcasted_iota(jnp.int32, (n, n), 0)
    b = lax.broadcasted_iota(jnp.int32, (n, n), 1)
    return jnp.where(a >= b, 1.0, 0.0).astype(BF16) if lower_incl else jnp.where(a <= b, 1.0, 0.0).astype(BF16)


def _fox_prep_call(fl_raw, b_pad, *, name):
    S = fl_raw.shape[0]
    nb = S // TQ

    def body(fl_ref, b_ref, crep_ref, ct_ref, carry_ref):
        i = pl.program_id(0)

        @pl.when(i == 0)
        def _():
            carry_ref[...] = jnp.zeros_like(carry_ref)
        logf = jax.nn.log_sigmoid(fl_ref[...] + b_ref[...])
        cb = _dot3_l(_tri(TQ, True), logf) + carry_ref[0:1, :]
        carry_ref[0:1, :] = cb[TQ - 1:TQ, :]
        a = lax.broadcasted_iota(jnp.int32, (128, D_GRP), 0)
        b = lax.broadcasted_iota(jnp.int32, (128, D_GRP), 1) // HEAD
        expand = jnp.where(a == b, 1.0, 0.0).astype(BF16)
        crep = _dot3(cb, expand)
        crep_ref[...] = crep
        ct_ref[...] = crep.T

    return pl.pallas_call(
        body, grid=(nb,),
        in_specs=[_row_spec(TQ, 128), _vec_spec(128)],
        out_specs=(_row_spec(TQ, D_GRP), pl.BlockSpec((None, D_GRP, TQ), lambda i: (i, 0, 0))),
        out_shape=(jax.ShapeDtypeStruct((S, D_GRP), F32), jax.ShapeDtypeStruct((nb, D_GRP, TQ), F32)),
        scratch_shapes=[pltpu.VMEM((8, 128), F32)],
        name=name, compiler_params=_cparams(),
    )(fl_raw, b_pad)


def _lane_masks():
    lane = lax.broadcasted_iota(jnp.int32, (1, 128), 1)
    return lane < HEAD, lane >= HEAD


def _fox_fwd_call(proj, c_rep, c_t, *, name, task=None):
    S = proj.shape[0]
    nq = S // TQ
    scale = HEAD ** -0.5

    def body(q_ref, k_ref, v_ref, c_ref, ct_ref, o_ref, lse_ref):
        i = pl.program_id(1)
        m_lo, m_hi = _lane_masks()
        masks = (m_lo, m_hi)
        q = q_ref[...]
        qm = [jnp.where(mk, q, jnp.zeros_like(q)) for mk in masks]
        cq = c_ref[...]
        cqh = [cq[:, 0:1], cq[:, HEAD:HEAD + 1]]
        row = lax.broadcasted_iota(jnp.int32, (TQ, TQ), 0)
        col = lax.broadcasted_iota(jnp.int32, (TQ, TQ), 1)

        def scores(j):
            start = pl.multiple_of(j * TQ, TQ)
            k = k_ref[pl.ds(start, TQ), :]
            ct = ct_ref[j]
            return tuple(_dot_nt(qm[h], k) * scale + (cqh[h] - ct[HEAD * h:HEAD * h + 1, :]) for h in range(2))

        def update(j, ss, state, masked):
            ms, ls, acc = state
            start = pl.multiple_of(j * TQ, TQ)
            v = v_ref[pl.ds(start, TQ), :]
            new_m, new_l, pv, alpha_l = [], [], [], []
            for h in range(2):
                s = ss[h]
                if masked:
                    s = jnp.where(row >= col, s, NEG)
                mn = jnp.maximum(ms[h], jnp.max(s, axis=1, keepdims=True))
                alpha = jnp.exp(ms[h] - mn)
                p = jnp.exp(s - mn)
                new_l.append(alpha * ls[h] + jnp.sum(p, axis=1, keepdims=True))
                new_m.append(mn)
                alpha_l.append(alpha)
                pv.append(_dot(p.astype(BF16), jnp.where(masks[h], v, jnp.zeros_like(v))))
            alpha_lane = jnp.where(m_lo, alpha_l[0], alpha_l[1])
            acc = acc * alpha_lane + pv[0] + pv[1]
            return (tuple(new_m), tuple(new_l), acc)

        def step(j, carry):
            ss, state = carry
            return (scores(j + 1), update(j, ss, state, False))

        init = ((jnp.full((TQ, 1), NEG, F32),) * 2, (jnp.zeros((TQ, 1), F32),) * 2, jnp.zeros((TQ, 128), F32))
        ss, state = lax.fori_loop(0, i, step, (scores(0), init))
        ms, ls, acc = update(i, ss, state, True)
        l_lane = jnp.where(m_lo, ls[0], ls[1])
        o_ref[...] = acc / l_lane
        lse_ref[...] = jnp.where(m_lo, ms[0] + jnp.log(ls[0]), ms[1] + jnp.log(ls[1]))

    return _pallas(
        body, grid=(N_PAIR, nq),
        in_specs=[pl.BlockSpec((TQ, 128), lambda p, i: (i, p)),
                  pl.BlockSpec((S, 128), lambda p, i: (0, N_PAIR + p)),
                  pl.BlockSpec((S, 128), lambda p, i: (0, 2 * N_PAIR + p)),
                  pl.BlockSpec((TQ, 128), lambda p, i: (i, p)),
                  pl.BlockSpec((nq, 128, TQ), lambda p, i: (0, p, 0))],
        out_specs=(pl.BlockSpec((TQ, 128), lambda p, i: (i, p)), pl.BlockSpec((TQ, 128), lambda p, i: (i, p))),
        out_shape=(jax.ShapeDtypeStruct((S, D_GRP), F32), jax.ShapeDtypeStruct((S, D_GRP), F32)),
        name=name, task=task,
    )(proj, proj, proj, c_rep, c_t)


def _fox_bwd_call(proj, do, lse_rep, delta_rep, c_rep, c_t, *, name, task=None):
    S = proj.shape[0]
    nq = S // TQ
    scale = HEAD ** -0.5

    def body(q_ref, k_ref, v_ref, do_ref, lse_ref, dl_ref, c_ref, ct_ref, dq_ref, dk_ref, dv_ref, dct_ref, dcq_ref, dqa_ref):
        j = pl.program_id(1)
        m_lo, m_hi = _lane_masks()
        masks = (m_lo, m_hi)

        @pl.when(j == 0)
        def _():
            dqa_ref[...] = jnp.zeros_like(dqa_ref)
            dcq_ref[...] = jnp.zeros_like(dcq_ref)
        k = k_ref[...]
        v = v_ref[...]
        km = [jnp.where(mk, k, jnp.zeros_like(k)) for mk in masks]
        ct = ct_ref[...]
        row = lax.broadcasted_iota(jnp.int32, (TQ, TQ), 0)
        col = lax.broadcasted_iota(jnp.int32, (TQ, TQ), 1)

        def probs(i):
            start = pl.multiple_of(i * TQ, TQ)
            q = q_ref[pl.ds(start, TQ), :]
            do = do_ref[pl.ds(start, TQ), :]
            lse = lse_ref[pl.ds(start, TQ), :]
            cq = c_ref[pl.ds(start, TQ), :]
            out = []
            for h in range(2):
                lo = HEAD * h
                qm = jnp.where(masks[h], q, jnp.zeros_like(q))
                dom = jnp.where(masks[h], do, jnp.zeros_like(do))
                s = _dot_nt(qm, k) * scale + (cq[:, lo:lo + 1] - ct[lo:lo + 1, :])
                out.append((jnp.exp(s - lse[:, lo:lo + 1]), _dot_nt(dom, v)))
            return tuple(out)

        def update(i, pd, carry, masked):
            dk, dv, dcs = carry
            start = pl.multiple_of(i * TQ, TQ)
            q = q_ref[pl.ds(start, TQ), :]
            do = do_ref[pl.ds(start, TQ), :]
            dl = dl_ref[pl.ds(start, TQ), :]
            dq = jnp.zeros((TQ, 128), F32)
            new_dcs = []
            rows = []
            for h in range(2):
                lo = HEAD * h
                qm = jnp.where(masks[h], q, jnp.zeros_like(q))
                dom = jnp.where(masks[h], do, jnp.zeros_like(do))
                p, dp = pd[h]
                if masked:
                    p = jnp.where(row >= col, p, 0.0)
                ds = p * (dp - dl[:, lo:lo + 1])
                new_dcs.append(dcs[h] + jnp.sum(ds, axis=0, keepdims=True))
                rows.append(jnp.sum(ds, axis=1, keepdims=True))
                dsb = (ds * scale).astype(BF16)
                dv = dv + _dot_tn(p.astype(BF16), dom)
                dk = dk + _dot_tn(dsb, qm)
                dq = dq + _dot(dsb, km[h])
            dqa_ref[pl.ds(start, TQ), :] += dq
            dcq_ref[pl.ds(start, TQ), :] += jnp.where(m_lo, rows[0], rows[1])
            return (dk, dv, tuple(new_dcs))

        def step(i, carry):
            pd, sums = carry
            return (probs(jnp.minimum(i + 1, nq - 1)), update(i, pd, sums, False))

        init = (jnp.zeros((TQ, 128), F32), jnp.zeros((TQ, 128), F32), (jnp.zeros((1, TQ), F32),) * 2)
        first = probs(j)
        second = probs(jnp.minimum(j + 1, nq - 1))
        _, (dk, dv, dcs) = lax.fori_loop(j + 1, nq, step, (second, update(j, first, init, True)))
        dk_ref[...] = dk.astype(BF16)
        dv_ref[...] = dv.astype(BF16)
        dct_ref[...] = jnp.zeros_like(dct_ref)
        dct_ref[0:1, :] = -dcs[0]
        dct_ref[1:2, :] = -dcs[1]

        @pl.when(j == nq - 1)
        def _():
            dq_ref[...] = dqa_ref[...].astype(BF16)

    res = lambda p, j: (0, p)
    return _pallas(
        body, grid=(N_PAIR, nq), task=task,
        in_specs=[pl.BlockSpec((S, 128), res),
                  pl.BlockSpec((TQ, 128), lambda p, j: (j, N_PAIR + p)),
                  pl.BlockSpec((TQ, 128), lambda p, j: (j, 2 * N_PAIR + p)),
                  pl.BlockSpec((S, 128), res), pl.BlockSpec((S, 128), res), pl.BlockSpec((S, 128), res),
                  pl.BlockSpec((S, 128), res),
                  pl.BlockSpec((None, 128, TQ), lambda p, j: (j, p, 0))],
        out_specs=(pl.BlockSpec((S, 128), res),
                   pl.BlockSpec((TQ, 128), lambda p, j: (j, p)), pl.BlockSpec((TQ, 128), lambda p, j: (j, p)),
                   pl.BlockSpec((None, None, 8, TQ), lambda p, j: (p, j, 0, 0)),
                   pl.BlockSpec((S, 128), res)),
        out_shape=(jax.ShapeDtypeStruct((S, D_GRP), BF16), jax.ShapeDtypeStruct((S, D_GRP), BF16),
                   jax.ShapeDtypeStruct((S, D_GRP), BF16), jax.ShapeDtypeStruct((N_PAIR, nq, 8, TQ), F32),
                   jax.ShapeDtypeStruct((S, D_GRP), F32)),
        scratch_shapes=[pltpu.VMEM((S, 128), F32)],
        name=name,
    )(proj, proj, proj, do, lse_rep, delta_rep, c_rep, c_t)


def _fox_gate_bwd_call(dc_rows, fl_raw, b_pad, *, name):
    S = fl_raw.shape[0]
    nb = S // TQ

    def body(dc_ref, fl_ref, b_ref, dfl_ref, db_ref, carry_ref):
        i = pl.program_id(0)

        @pl.when(i == 0)
        def _():
            carry_ref[...] = jnp.zeros_like(carry_ref)
        rc = _dot3(dc_ref[...], _tri(TQ, True)) + carry_ref[:, 0:1]
        carry_ref[...] = jnp.broadcast_to(rc[:, 0:1], carry_ref.shape)
        fl = fl_ref[...] + b_ref[...]
        dfl = rc.T * jax.nn.sigmoid(-fl)
        dfl_ref[...] = dfl.astype(BF16)
        _acc_add(db_ref, jnp.sum(dfl, axis=0, keepdims=True), i)

    rev = lambda i: (nb - 1 - i, 0)
    return pl.pallas_call(
        body, grid=(nb,),
        in_specs=[pl.BlockSpec((128, TQ), lambda i: (0, nb - 1 - i)), pl.BlockSpec((TQ, 128), rev), _vec_spec(128)],
        out_specs=(pl.BlockSpec((TQ, 128), rev), _acc_spec(128)),
        out_shape=(jax.ShapeDtypeStruct((S, 128), BF16), jax.ShapeDtypeStruct((8, 128), F32)),
        scratch_shapes=[pltpu.VMEM((128, 128), F32)],
        name=name, compiler_params=_cparams(),
    )(dc_rows, fl_raw, b_pad)


def _chk_bias_call(g_rev, *, name):
    def body(g_ref, o_ref):
        x = jnp.broadcast_to(g_ref[...], (TQ, ROLL_W))
        rolled = pltpu.roll(x, ROLL_W - (TQ - 1), 1, stride=1, stride_axis=0)
        qc = lax.broadcasted_iota(jnp.int32, (TQ, WIN), 0) // CHUNK
        kc = lax.broadcasted_iota(jnp.int32, (TQ, WIN), 1) // CHUNK
        band = (kc >= qc) & (kc <= qc + LEFT)
        o_ref[...] = jnp.where(band, rolled[:, 0:WIN], NEG)

    return pl.pallas_call(
        body, grid=(8,),
        in_specs=[pl.BlockSpec((None, 1, ROLL_W), lambda h: (h, 0, 0))],
        out_specs=pl.BlockSpec((None, TQ, WIN), lambda h: (h, 0, 0)),
        out_shape=jax.ShapeDtypeStruct((8, TQ, WIN), F32), name=name, compiler_params=_cparams(),
    )(g_rev.reshape(8, 1, ROLL_W))


def _chk_scores(i, qm, kwin, bias, scale):
    s = _dot_nt(qm, kwin) * scale + bias
    kc = lax.broadcasted_iota(jnp.int32, (TQ, WIN), 1) // CHUNK
    return jnp.where(kc + i * (TQ // CHUNK) >= LEFT, s, NEG)


def _chk_fwd_call(proj, bias, *, name, task=None):
    S = proj.shape[0]
    nq = S // TQ
    scale = HEAD ** -0.5

    def body(q_ref, k_ref, v_ref, b_ref, o_ref, kp_ref, vp_ref):
        i = pl.program_id(1)

        @pl.when(i == 0)
        def _():
            kp_ref[0:PADK, :] = jnp.zeros((PADK, 128), BF16)
            vp_ref[0:PADK, :] = jnp.zeros((PADK, 128), BF16)
            kp_ref[PADK:PADK + S, :] = k_ref[...]
            vp_ref[PADK:PADK + S, :] = v_ref[...]
        masks = _lane_masks()
        q = q_ref[...]
        start = pl.multiple_of(i * TQ, TQ)
        kwin = kp_ref[pl.ds(start, WIN), :]
        vwin = vp_ref[pl.ds(start, WIN), :]
        ss = [_chk_scores(i, jnp.where(masks[h], q, jnp.zeros_like(q)), kwin, b_ref[h], scale) for h in range(2)]
        ps = []
        for s in ss:
            p = jnp.exp(s - jnp.max(s, axis=1, keepdims=True))
            ps.append((p / jnp.sum(p, axis=1, keepdims=True)).astype(BF16))
        o_ref[...] = (_dot(ps[0], jnp.where(masks[0], vwin, jnp.zeros_like(vwin)))
                      + _dot(ps[1], jnp.where(masks[1], vwin, jnp.zeros_like(vwin))))

    c0 = 3 * N_PAIR
    return _pallas(
        body, grid=(N_PAIR, nq), task=task,
        in_specs=[pl.BlockSpec((TQ, 128), lambda p, i: (i, c0 + p)),
                  pl.BlockSpec((S, 128), lambda p, i: (0, c0 + N_PAIR + p)),
                  pl.BlockSpec((S, 128), lambda p, i: (0, c0 + 2 * N_PAIR + p)),
                  pl.BlockSpec((2, TQ, WIN), lambda p, i: (p, 0, 0))],
        out_specs=pl.BlockSpec((TQ, 128), lambda p, i: (i, p)),
        out_shape=jax.ShapeDtypeStruct((S, D_GRP), F32),
        scratch_shapes=[pltpu.VMEM((S + PADK, 128), BF16), pltpu.VMEM((S + PADK, 128), BF16)],
        name=name,
    )(proj, proj, proj, bias)


def _chk_bwd_call(proj, do, bias, *, name, task=None):
    S = proj.shape[0]
    nq = S // TQ
    scale = HEAD ** -0.5

    def body(q_ref, k_ref, v_ref, do_ref, b_ref, dq_ref, dk_ref, dv_ref, dg_ref, kp_ref, vp_ref, dkp_ref, dvp_ref, db_ref):
        i = pl.program_id(1)

        @pl.when(i == 0)
        def _():
            kp_ref[0:PADK, :] = jnp.zeros((PADK, 128), BF16)
            vp_ref[0:PADK, :] = jnp.zeros((PADK, 128), BF16)
            kp_ref[PADK:PADK + S, :] = k_ref[...]
            vp_ref[PADK:PADK + S, :] = v_ref[...]
            dkp_ref[...] = jnp.zeros_like(dkp_ref)
            dvp_ref[...] = jnp.zeros_like(dvp_ref)
            db_ref[...] = jnp.zeros_like(db_ref)
        masks = _lane_masks()
        q = q_ref[...]
        dout = do_ref[...]
        start = pl.multiple_of(i * TQ, TQ)
        kwin = kp_ref[pl.ds(start, WIN), :]
        vwin = vp_ref[pl.ds(start, WIN), :]
        qm = [jnp.where(mk, q, jnp.zeros_like(q)) for mk in masks]
        dom = [jnp.where(mk, dout, jnp.zeros_like(dout)) for mk in masks]
        ss = [_chk_scores(i, qm[h], kwin, b_ref[h], scale) for h in range(2)]
        dps = [_dot_nt(dom[h], vwin) for h in range(2)]
        pbs, dsbs = [], []
        for h in range(2):
            p = jnp.exp(ss[h] - jnp.max(ss[h], axis=1, keepdims=True))
            p = p / jnp.sum(p, axis=1, keepdims=True)
            ds = p * (dps[h] - jnp.sum(p * dps[h], axis=1, keepdims=True))
            db_ref[h] += ds
            pbs.append(p.astype(BF16))
            dsbs.append((ds * scale).astype(BF16))
        dq_ref[...] = (_dot(dsbs[0], jnp.where(masks[0], kwin, jnp.zeros_like(kwin)))
                       + _dot(dsbs[1], jnp.where(masks[1], kwin, jnp.zeros_like(kwin)))).astype(BF16)
        dkp_ref[pl.ds(start, WIN), :] += _dot_tn(dsbs[0], qm[0]) + _dot_tn(dsbs[1], qm[1])
        dvp_ref[pl.ds(start, WIN), :] += _dot_tn(pbs[0], dom[0]) + _dot_tn(pbs[1], dom[1])

        @pl.when(i == nq - 1)
        def _():
            dk_ref[...] = dkp_ref[PADK:PADK + S, :].astype(BF16)
            dv_ref[...] = dvp_ref[PADK:PADK + S, :].astype(BF16)
            a = lax.broadcasted_iota(jnp.int32, (TQ, TQ), 0)
            b = lax.broadcasted_iota(jnp.int32, (TQ, TQ), 1)
            flip = jnp.where(a + b == TQ - 1, 1.0, 0.0).astype(BF16)
            e = lax.broadcasted_iota(jnp.int32, (1, ROLL_W), 1)
            dg_ref[...] = jnp.zeros_like(dg_ref)
            for h in range(2):
                rev = _dot3_l(flip, db_ref[h])
                wide = jnp.concatenate([rev, jnp.zeros((TQ, ROLL_W - WIN), F32)], axis=1)
                diag = pltpu.roll(wide, 0, 1, stride=1, stride_axis=0)
                dg = jnp.sum(diag, axis=0, keepdims=True)
                lo = jnp.sum(jnp.where(e <= 639, dg, 0.0), axis=1, keepdims=True)
                hi = jnp.sum(jnp.where(e >= 895, dg, 0.0), axis=1, keepdims=True)
                dg_ref[h:h + 1, :] = jnp.where(e == 639, lo, jnp.where(e == 895, hi, dg))

    c0 = 3 * N_PAIR
    res = lambda p, i: (0, p)
    return _pallas(
        body, grid=(N_PAIR, nq), task=task,
        in_specs=[pl.BlockSpec((TQ, 128), lambda p, i: (i, c0 + p)),
                  pl.BlockSpec((S, 128), lambda p, i: (0, c0 + N_PAIR + p)),
                  pl.BlockSpec((S, 128), lambda p, i: (0, c0 + 2 * N_PAIR + p)),
                  pl.BlockSpec((TQ, 128), lambda p, i: (i, p)),
                  pl.BlockSpec((2, TQ, WIN), lambda p, i: (p, 0, 0))],
        out_specs=(pl.BlockSpec((TQ, 128), lambda p, i: (i, p)), pl.BlockSpec((S, 128), res),
                   pl.BlockSpec((S, 128), res), pl.BlockSpec((None, 8, ROLL_W), lambda p, i: (p, 0, 0))),
        out_shape=(jax.ShapeDtypeStruct((S, D_GRP), BF16), jax.ShapeDtypeStruct((S, D_GRP), BF16),
                   jax.ShapeDtypeStruct((S, D_GRP), BF16), jax.ShapeDtypeStruct((N_PAIR, 8, ROLL_W), F32)),
        scratch_shapes=[pltpu.VMEM((S + PADK, 128), BF16), pltpu.VMEM((S + PADK, 128), BF16),
                        pltpu.VMEM((S + PADK, 128), F32), pltpu.VMEM((S + PADK, 128), F32),
                        pltpu.VMEM((2, TQ, WIN), F32)],
        name=name,
    )(proj, proj, proj, do, bias)


def _mem_fwd_call(q, k, v, *, name, tq=512):
    S = q.shape[0]
    scale = MEM_HD ** -0.5

    def body(q_ref, k_ref, v_ref, o_ref):
        s = _dot_nt(q_ref[...], k_ref[...]) * scale
        p = jnp.exp(s - jnp.max(s, axis=1, keepdims=True))
        p = p / jnp.sum(p, axis=1, keepdims=True)
        o_ref[...] = _dot(p.astype(BF16), v_ref[...]).astype(BF16)

    return pl.pallas_call(
        body, grid=(MEM_HEADS, S // tq),
        in_specs=[pl.BlockSpec((tq, MEM_HD), lambda h, i: (i, h)),
                  pl.BlockSpec((N_MEM, MEM_HD), lambda h, i: (0, h)),
                  pl.BlockSpec((N_MEM, MEM_HD), lambda h, i: (0, h))],
        out_specs=pl.BlockSpec((tq, MEM_HD), lambda h, i: (i, h)),
        out_shape=jax.ShapeDtypeStruct((S, D), BF16), name=name, compiler_params=_cparams(),
    )(q, k, v)


def _mem_bwd_call(q, k, v, do, *, name, tq=512):
    S = q.shape[0]
    n = S // tq
    scale = MEM_HD ** -0.5

    def body(q_ref, k_ref, v_ref, do_ref, dq_ref, dk_ref, dv_ref, dka_ref, dva_ref):
        i = pl.program_id(1)

        @pl.when(i == 0)
        def _():
            dka_ref[...] = jnp.zeros_like(dka_ref)
            dva_ref[...] = jnp.zeros_like(dva_ref)
        qb = q_ref[...]
        kb = k_ref[...]
        dob = do_ref[...]
        s = _dot_nt(qb, kb) * scale
        p = jnp.exp(s - jnp.max(s, axis=1, keepdims=True))
        p = p / jnp.sum(p, axis=1, keepdims=True)
        dp = _dot_nt(dob, v_ref[...])
        ds = p * (dp - jnp.sum(p * dp, axis=1, keepdims=True))
        dsb = (ds * scale).astype(BF16)
        dq_ref[...] = _dot(dsb, kb).astype(BF16)
        dka_ref[...] += _dot_tn(dsb, qb)
        dva_ref[...] += _dot_tn(p.astype(BF16), dob)

        @pl.when(i == n - 1)
        def _():
            dk_ref[...] = dka_ref[...].astype(BF16)
            dv_ref[...] = dva_ref[...].astype(BF16)

    kv = pl.BlockSpec((N_MEM, MEM_HD), lambda h, i: (0, h))
    qs = pl.BlockSpec((tq, MEM_HD), lambda h, i: (i, h))
    return pl.pallas_call(
        body, grid=(MEM_HEADS, n), in_specs=[qs, kv, kv, qs], out_specs=(qs, kv, kv),
        out_shape=(jax.ShapeDtypeStruct((S, D), BF16), jax.ShapeDtypeStruct((N_MEM, D), BF16),
                   jax.ShapeDtypeStruct((N_MEM, D), BF16)),
        scratch_shapes=[pltpu.VMEM((N_MEM, MEM_HD), F32), pltpu.VMEM((N_MEM, MEM_HD), F32)],
        name=name, compiler_params=_cparams(),
    )(q, k, v, do)


def _rel_table_to_g(rel):
    return jnp.concatenate([
        jnp.broadcast_to(rel[:, N_REL - 1:N_REL], (8, 640)),
        rel[:, 1:N_REL - 1][:, ::-1],
        jnp.broadcast_to(rel[:, 0:1], (8, 129)),
    ], axis=1)


def _g_to_rel_table(dg):
    return dg[:, 639:896][:, ::-1]


def _place():
    x, y, c = lax.axis_index("x"), lax.axis_index("y"), lax.axis_index("c")
    others = [(1 - x, y), (x, 1 - y), (1 - x, 1 - y)]
    return x, y, c, others


def _half(c, rows):
    hr = rows // 2
    return pl.ds(pl.multiple_of(c * hr, 16), hr)


def _dma_sems(*shape):
    return pltpu.SemaphoreType.DMA(shape)


def _cast_slab_call(w, chip_arr, *, name, tm=256, pad_rows=0):
    rows, cols = w.shape
    if pad_rows:
        tm = rows
    tm = min(tm, rows)

    def body(chip_ref, w_ref, o_ref):
        o_ref[0:tm, :] = w_ref[...].astype(BF16)
        if pad_rows:
            o_ref[tm:tm + pad_rows, :] = jnp.zeros((pad_rows, cols), BF16)

    return pl.pallas_call(
        body,
        grid_spec=pltpu.PrefetchScalarGridSpec(
            num_scalar_prefetch=1, grid=(rows // tm,),
            in_specs=[pl.BlockSpec((tm, cols), lambda i, chip: (i, 0))],
            out_specs=pl.BlockSpec((None, tm + pad_rows, cols), lambda i, chip: (chip[0], i, 0))),
        out_shape=jax.ShapeDtypeStruct((N_CHIP, rows + pad_rows, cols), BF16), name=name,
        compiler_params=_cparams(),
    )(chip_arr, w)


def _ag_ici_task(gathered):
    n = len(gathered)

    def copies(ins, outs, sems):
        send_sems, recv_sems = sems
        x, y, c, others = _place()
        me = 2 * x + y
        for k in range(n):
            mine = _half(c, gathered[k].shape[1])
            for t, (ox, oy) in enumerate(others):
                yield [pltpu.make_async_remote_copy(
                    src_ref=ins[k].at[me, mine], dst_ref=outs[k].at[slab, mine],
                    send_sem=send_sems.at[k, t], recv_sem=recv_sems.at[k, t],
                    device_id=(ox, oy, c), device_id_type=MESH) for slab in (me, 2 * ox + oy)]

    def issue(ins, outs, sems):
        for outgoing, _ in copies(ins, outs, sems):
            outgoing.start()

    def drain(ins, outs, sems):
        for outgoing, incoming in copies(ins, outs, sems):
            incoming.wait_recv()
            outgoing.wait_send()

    return _Task(gathered, [jax.ShapeDtypeStruct(g.shape, g.dtype) for g in gathered],
                 [_dma_sems(n, 3), _dma_sems(n, 3)], issue, drain, aliases={k: k for k in range(n)})


def _ag_d2d_task(gathered):
    n = len(gathered)

    def copies(ins, outs, sems):
        send_sems, recv_sems = sems
        x, y, c, others = _place()
        for k in range(n):
            rows = gathered[k].shape[1]
            mine, theirs = _half(c, rows), _half(1 - c, rows)
            for t, (ox, oy) in enumerate(others):
                slab = 2 * ox + oy
                pair = [pltpu.make_async_remote_copy(
                    src_ref=ins[k].at[slab, half], dst_ref=outs[k].at[slab, half],
                    send_sem=send_sems.at[k, t], recv_sem=recv_sems.at[k, t],
                    device_id=(x, y, 1 - c), device_id_type=MESH) for half in (mine, theirs)]
                yield pair

    def issue(ins, outs, sems):
        for outgoing, _ in copies(ins, outs, sems):
            outgoing.start()

    def drain(ins, outs, sems):
        for outgoing, incoming in copies(ins, outs, sems):
            incoming.wait_recv()
            outgoing.wait_send()

    return _Task(gathered, [jax.ShapeDtypeStruct(g.shape, g.dtype) for g in gathered],
                 [_dma_sems(n, 3), _dma_sems(n, 3)], issue, drain, aliases={k: k for k in range(n)})


def _rs_pair_task(ds):
    n = len(ds)

    def copies(ins, outs, sems):
        send_sems, recv_sems = sems
        x, y, c, _ = _place()
        for k in range(n):
            yield pltpu.make_async_remote_copy(
                src_ref=ins[k].at[:, _half(1 - c, ds[k].shape[1])], dst_ref=outs[k],
                send_sem=send_sems.at[k], recv_sem=recv_sems.at[k],
                device_id=(x, y, 1 - c), device_id_type=MESH)

    def issue(ins, outs, sems):
        for cp in copies(ins, outs, sems):
            cp.start()

    def drain(ins, outs, sems):
        for cp in copies(ins, outs, sems):
            cp.wait()

    return _Task(ds, [jax.ShapeDtypeStruct((N_CHIP, d.shape[1] // 2, d.shape[2]), d.dtype) for d in ds],
                 [_dma_sems(n), _dma_sems(n)], issue, drain)


def _pair_add_call(d, r1, c_arr, *, name, tm=256):
    _, rows, cols = d.shape
    hr = rows // 2
    tm = tm if hr % tm == 0 else hr
    nb = hr // tm

    def body(c_ref, d_ref, r_ref, o_ref):
        o_ref[...] = (d_ref[...].astype(F32) + r_ref[...].astype(F32)).astype(BF16)

    return pl.pallas_call(
        body,
        grid_spec=pltpu.PrefetchScalarGridSpec(
            num_scalar_prefetch=1, grid=(N_CHIP, nb),
            in_specs=[pl.BlockSpec((None, tm, cols), lambda j, i, c: (j, c[0] * nb + i, 0)),
                      pl.BlockSpec((None, tm, cols), lambda j, i, c: (j, i, 0))],
            out_specs=pl.BlockSpec((None, tm, cols), lambda j, i, c: (j, i, 0))),
        out_shape=jax.ShapeDtypeStruct((N_CHIP, hr, cols), BF16), name=name, compiler_params=_cparams(),
    )(c_arr, d, r1)


def _rs_chip_task(ps):
    n = len(ps)

    def copies(ins, outs, sems):
        send_sems, recv_sems = sems
        x, y, c, others = _place()
        for k in range(n):
            for t, (ox, oy) in enumerate(others):
                yield pltpu.make_async_remote_copy(
                    src_ref=ins[k].at[2 * ox + oy], dst_ref=outs[k].at[t],
                    send_sem=send_sems.at[k, t], recv_sem=recv_sems.at[k, t],
                    device_id=(ox, oy, c), device_id_type=MESH)

    def issue(ins, outs, sems):
        for cp in copies(ins, outs, sems):
            cp.start()

    def drain(ins, outs, sems):
        for cp in copies(ins, outs, sems):
            cp.wait()

    return _Task(ps, [jax.ShapeDtypeStruct((3,) + p.shape[1:], p.dtype) for p in ps],
                 [_dma_sems(n, 3), _dma_sems(n, 3)], issue, drain)


def _chip_sum_call(p, r2, place_arr, *, name, tm=256):
    _, hr, cols = r2.shape
    tm = tm if hr % tm == 0 else hr
    nb = hr // tm

    def body(place_ref, p_ref, r_ref, o_ref):
        acc = p_ref[...].astype(F32)
        for j in range(3):
            acc = acc + r_ref[j].astype(F32)
        o_ref[...] = acc

    return pl.pallas_call(
        body,
        grid_spec=pltpu.PrefetchScalarGridSpec(
            num_scalar_prefetch=1, grid=(nb,),
            in_specs=[pl.BlockSpec((None, tm, cols), lambda i, pc: (pc[0], i, 0)),
                      pl.BlockSpec((3, tm, cols), lambda i, pc: (0, i, 0))],
            out_specs=pl.BlockSpec((tm, cols), lambda i, pc: (pc[1] * nb + i, 0))),
        out_shape=jax.ShapeDtypeStruct((2 * hr, cols), F32), name=name, compiler_params=_cparams(),
    )(place_arr, p, r2)


def _rs_gather_task(gs):
    n = len(gs)

    def copies(ins, outs, sems):
        send_sems, recv_sems = sems
        x, y, c, _ = _place()
        for k in range(n):
            rows = gs[k].shape[0]
            mine, theirs = _half(c, rows), _half(1 - c, rows)
            yield [pltpu.make_async_remote_copy(
                src_ref=ins[k].at[mine], dst_ref=outs[k].at[half],
                send_sem=send_sems.at[k], recv_sem=recv_sems.at[k],
                device_id=(x, y, 1 - c), device_id_type=MESH) for half in (mine, theirs)]

    def issue(ins, outs, sems):
        for outgoing, _ in copies(ins, outs, sems):
            outgoing.start()

    def drain(ins, outs, sems):
        for outgoing, incoming in copies(ins, outs, sems):
            incoming.wait_recv()
            outgoing.wait_send()

    return _Task(gs, [jax.ShapeDtypeStruct(g.shape, g.dtype) for g in gs],
                 [_dma_sems(n), _dma_sems(n)], issue, drain, aliases={k: k for k in range(n)})


def _adamw(w, g, m, v):
    m = ADAM_B1 * m + (1.0 - ADAM_B1) * g
    v = ADAM_B2 * v + (1.0 - ADAM_B2) * jnp.square(g)
    m_hat = m / (1.0 - ADAM_B1 ** ADAM_STEP)
    v_hat = v / (1.0 - ADAM_B2 ** ADAM_STEP)
    delta = -ADAM_LR * (m_hat / (jnp.sqrt(v_hat) + ADAM_EPS) + ADAM_WD * w)
    return delta, m, v


def _adamw_call(items, *, name, tm=256, task=None):
    n = len(items)
    cols = items[0][0].shape[1]
    tiles = [it[0].shape[0] // tm for it in items]
    steps = max(tiles)

    def body(*refs):
        i = pl.program_id(0)
        ins, outs = refs[:4 * n], refs[4 * n:]
        for k in range(n):
            def update(k=k):
                g = ins[4 * k + 1][...]
                res = _adamw(ins[4 * k][...], g, ins[4 * k + 2][...], ins[4 * k + 3][...])
                outs[4 * k][...] = g
                for j in range(3):
                    outs[4 * k + 1 + j][...] = res[j]
            if tiles[k] == steps:
                update()
            else:
                pl.when(i < tiles[k])(update)

    in_specs, out_specs, out_shape, args = [], [], [], []
    for it, t in zip(items, tiles):
        spec = pl.BlockSpec((tm, cols), lambda i, t=t: (jnp.minimum(i, t - 1), 0))
        in_specs += [spec] * 4
        out_specs += [spec] * 4
        out_shape += [jax.ShapeDtypeStruct(it[0].shape, F32)] * 4
        args += list(it)
    res = _pallas(body, grid=(steps,), in_specs=in_specs, out_specs=out_specs, out_shape=out_shape,
                  name=name, task=task)(*args)
    outs, extra = res if task is not None else (res, None)
    grouped = [tuple(outs[4 * k:4 * k + 4]) for k in range(n)]
    return (grouped, extra) if task is not None else grouped


def _adamw_cols_call(w, g_pad, m, v, *, name, tn=256):
    rows, cols = w.shape

    def body(w_ref, g_ref, m_ref, v_ref, go_ref, d_ref, mo_ref, vo_ref):
        g = g_ref[0:rows, :]
        d, mn, vn = _adamw(w_ref[...], g, m_ref[...], v_ref[...])
        go_ref[...] = g
        d_ref[...] = d
        mo_ref[...] = mn
        vo_ref[...] = vn

    spec = pl.BlockSpec((rows, tn), lambda j: (0, j))
    gspec = pl.BlockSpec((g_pad.shape[0], tn), lambda j: (0, j))
    return _pallas(body, grid=(cols // tn,), in_specs=[spec, gspec, spec, spec], out_specs=(spec,) * 4,
                   out_shape=(jax.ShapeDtypeStruct((rows, cols), F32),) * 4, name=name)(w, g_pad, m, v)


N_DEV = 8
SMALL_ROWS = 24
SMALL_LAYOUT = {
    "g_mix_pre": (0, 0, 1, D), "g_mix_post": (1, 0, 1, D), "g_mem_kv": (2, 0, 1, D), "g_mem_pre": (3, 0, 1, D),
    "g_mem_post": (4, 0, 1, D), "g_ff_pre": (5, 0, 1, D), "g_ff_post": (6, 0, 1, D),
    "g_fox_out": (7, 0, 1, D_GRP), "g_chk_out": (7, D_GRP, 1, D_GRP), "b_fgt": (8, 0, 1, 8),
    "rel_bias": (16, 0, 8, N_REL),
}
SMALL = list(SMALL_LAYOUT)


LOSS_ROW = 9


def _small_call(grads, ws, ms, vs, loss_blk, task, *, name):
    n = len(SMALL)
    t_in, t_out = len(task.arrays), len(task.out_shapes)

    def body(*refs):
        g_refs, w_refs, m_refs, v_refs = (refs[j * n:(j + 1) * n] for j in range(4))
        p = 4 * n
        loss_ref, tins = refs[p], refs[p + 1:p + 1 + t_in]
        p += 1 + t_in
        outs, loss_out, touts = refs[p:p + 4 * n], refs[p + 4 * n], refs[p + 4 * n + 1:p + 4 * n + 1 + t_out]
        p += 4 * n + 1 + t_out
        mine, slots, send_sems, recv_sems = refs[p:p + 4]
        tsems = refs[p + 4:]
        task.issue(tins, touts, tsems)
        x, y, c, _ = _place()
        me = 4 * x + 2 * y + c
        mine[...] = jnp.zeros_like(mine)
        for k, name_k in enumerate(SMALL):
            r, l, nr, nl = SMALL_LAYOUT[name_k]
            mine[r:r + nr, l:l + nl] = g_refs[k][0:nr, 0:nl]
        mine[LOSS_ROW:LOSS_ROW + 1, 0:128] = loss_ref[0:1, :]
        slots[me] = mine[...]
        peers = [(dx, dy, dc) for dx in (0, 1) for dy in (0, 1) for dc in (0, 1)][1:]
        cps = []
        for t, (dx, dy, dc) in enumerate(peers):
            px, py, pc = (x + dx) % 2, (y + dy) % 2, (c + dc) % 2
            cps.append(pltpu.make_async_remote_copy(
                src_ref=mine, dst_ref=slots.at[me], send_sem=send_sems.at[t], recv_sem=recv_sems.at[t],
                device_id=(px, py, pc), device_id_type=MESH))
            cps[-1].start()
        for t, (dx, dy, dc) in enumerate(peers):
            px, py, pc = (x + dx) % 2, (y + dy) % 2, (c + dc) % 2
            pltpu.make_async_remote_copy(
                src_ref=mine, dst_ref=slots.at[4 * px + 2 * py + pc], send_sem=send_sems.at[t],
                recv_sem=recv_sems.at[t], device_id=(px, py, pc), device_id_type=MESH).wait_recv()
        for cp in cps:
            cp.wait_send()
        total = slots[0]
        for j in range(1, N_DEV):
            total = total + slots[j]
        for k, name_k in enumerate(SMALL):
            r, l, nr, nl = SMALL_LAYOUT[name_k]
            g = total[r:r + nr, l:l + nl]
            d, mn, vn = _adamw(w_refs[k][...], g, m_refs[k][...], v_refs[k][...])
            for j, val in enumerate((g, d, mn, vn)):
                outs[4 * k + j][...] = val
        loss_out[...] = jnp.broadcast_to(total[LOSS_ROW:LOSS_ROW + 1, 0:128], loss_out.shape)
        task.drain(tins, touts, tsems)

    vm = pl.BlockSpec(memory_space=pltpu.VMEM)
    out_shape = [jax.ShapeDtypeStruct(ws[k].shape, F32) for k in SMALL for _ in range(4)]
    out_shape += [jax.ShapeDtypeStruct((8, 128), F32)] + list(task.out_shapes)
    res = pl.pallas_call(
        body, in_specs=[vm] * (4 * n + 1) + [ANY] * t_in, out_specs=[vm] * (4 * n + 1) + [ANY] * t_out,
        out_shape=out_shape,
        scratch_shapes=[pltpu.VMEM((SMALL_ROWS, D), F32), pltpu.VMEM((N_DEV, SMALL_ROWS, D), F32),
                        _dma_sems(N_DEV - 1), _dma_sems(N_DEV - 1)] + list(task.sems),
        input_output_aliases={4 * n + 1 + i: 4 * n + 1 + j for i, j in task.aliases.items()},
        name=name,
    )(*[d[k] for d in (grads, ws, ms, vs) for k in SMALL], loss_blk, *task.arrays)
    return ({k: tuple(res[4 * i:4 * i + 4]) for i, k in enumerate(SMALL)}, res[4 * n], list(res[4 * n + 1:]))


WEIGHTS = ["w_in", "b_fgt", "rel_bias", "g_fox_out", "g_chk_out", "w_out", "g_mix_pre", "g_mix_post", "g_mem_kv",
           "w_mq", "w_mk", "w_mv", "w_mo", "g_mem_pre", "g_mem_post", "w_ff1", "w_ff2", "g_ff_pre", "g_ff_post"]
BIG = ["w_in", "w_out", "w_mq", "w_mk", "w_mv", "w_mo", "w_ff1", "w_ff2"]


IN_SHARD = D_IN // N_CHIP
IN_PAD = 800


IN_PIECES = [(0, 0, 770), (800, 770, 766), (1566, 3072, 4), (1600, 3076, 4), (1604, 1536, 766), (2400, 2302, 770)]
PAD_ZEROS = [(800 * j + IN_SHARD, IN_PAD - IN_SHARD) for j in range(N_CHIP)]
ALL_ZEROS = [(D_IN, D_ALL - D_IN)]


def _reorder_rows_call(src, to_all, *, name, tn=256):
    rows, cols = src.shape
    zeros = ALL_ZEROS if to_all else PAD_ZEROS

    def body(s_ref, o_ref):
        for pad0, all0, cnt in IN_PIECES:
            s0, d0 = (pad0, all0) if to_all else (all0, pad0)
            o_ref[d0:d0 + cnt, :] = s_ref[s0:s0 + cnt, :]
        for z0, cnt in zeros:
            o_ref[z0:z0 + cnt, :] = jnp.zeros((cnt, tn), src.dtype)

    spec = pl.BlockSpec((rows, tn), lambda j: (0, j))
    return _pallas(body, grid=(cols // tn,), in_specs=[spec], out_specs=spec,
                   out_shape=jax.ShapeDtypeStruct((rows, cols), src.dtype), name=name)(src)


def kernel(x, mem, w_in, b_fgt, rel_bias, g_fox_out, g_chk_out, w_out, g_mix_pre, g_mix_post, g_mem_kv, w_mq, w_mk, w_mv, w_mo, g_mem_pre, g_mem_post, w_ff1, w_ff2, g_ff_pre, g_ff_post, loss_target, m_w_in, m_b_fgt, m_rel_bias, m_g_fox_out, m_g_chk_out, m_w_out, m_g_mix_pre, m_g_mix_post, m_g_mem_kv, m_w_mq, m_w_mk, m_w_mv, m_w_mo, m_g_mem_pre, m_g_mem_post, m_w_ff1, m_w_ff2, m_g_ff_pre, m_g_ff_post, v_w_in, v_b_fgt, v_rel_bias, v_g_fox_out, v_g_chk_out, v_w_out, v_g_mix_pre, v_g_mix_post, v_g_mem_kv, v_w_mq, v_w_mk, v_w_mv, v_w_mo, v_g_mem_pre, v_g_mem_post, v_w_ff1, v_w_ff2, v_g_ff_pre, v_g_ff_post):
    w = dict(w_in=w_in, b_fgt=b_fgt, rel_bias=rel_bias, g_fox_out=g_fox_out, g_chk_out=g_chk_out, w_out=w_out,
             g_mix_pre=g_mix_pre, g_mix_post=g_mix_post, g_mem_kv=g_mem_kv, w_mq=w_mq, w_mk=w_mk, w_mv=w_mv,
             w_mo=w_mo, g_mem_pre=g_mem_pre, g_mem_post=g_mem_post, w_ff1=w_ff1, w_ff2=w_ff2, g_ff_pre=g_ff_pre,
             g_ff_post=g_ff_post)
    m = dict(w_in=m_w_in, b_fgt=m_b_fgt, rel_bias=m_rel_bias, g_fox_out=m_g_fox_out, g_chk_out=m_g_chk_out,
             w_out=m_w_out, g_mix_pre=m_g_mix_pre, g_mix_post=m_g_mix_post, g_mem_kv=m_g_mem_kv, w_mq=m_w_mq,
             w_mk=m_w_mk, w_mv=m_w_mv, w_mo=m_w_mo, g_mem_pre=m_g_mem_pre, g_mem_post=m_g_mem_post,
             w_ff1=m_w_ff1, w_ff2=m_w_ff2, g_ff_pre=m_g_ff_pre, g_ff_post=m_g_ff_post)
    v = dict(w_in=v_w_in, b_fgt=v_b_fgt, rel_bias=v_rel_bias, g_fox_out=v_g_fox_out, g_chk_out=v_g_chk_out,
             w_out=v_w_out, g_mix_pre=v_g_mix_pre, g_mix_post=v_g_mix_post, g_mem_kv=v_g_mem_kv, w_mq=v_w_mq,
             w_mk=v_w_mk, w_mv=v_w_mv, w_mo=v_w_mo, g_mem_pre=v_g_mem_pre, g_mem_post=v_g_mem_post,
             w_ff1=v_w_ff1, w_ff2=v_w_ff2, g_ff_pre=v_g_ff_pre, g_ff_post=v_g_ff_post)

    def rows(d, k):
        return d[k][0] if k == "rel_bias" else d[k]

    xs, mems, target = x[0], mem[0], loss_target[0]
    S = xs.shape[0]
    sp = {k: rows(w, k) for k in SMALL}
    b_pad = jnp.pad(sp["b_fgt"], ((0, 0), (0, 120)))
    chip = 2 * lax.axis_index("x") + lax.axis_index("y")
    chip_arr = jnp.reshape(chip, (1,)).astype(jnp.int32)
    c_arr = jnp.reshape(lax.axis_index("c"), (1,)).astype(jnp.int32)
    place_arr = jnp.concatenate([chip_arr, c_arr])
    w_in_t, m_in_t, v_in_t = w["w_in"][0].T, m["w_in"][0].T, v["w_in"][0].T
    slab = {k: _cast_slab_call(w[k][0], chip_arr, name="cast_" + k) for k in BIG[1:]}
    slab["w_in"] = _cast_slab_call(w_in_t, chip_arr, name="cast_w_in", pad_rows=IN_PAD - IN_SHARD)

    def gather_ici(names):
        return _ag_ici_task([slab[k] for k in names])

    def pair_add(k, d, r1):
        return _pair_add_call(d, r1, c_arr, name="rs_pair_add_" + k)

    g_in, = _comm_call(gather_ici(["w_in"]), name="ag_w_in")
    h1, (g_in,) = _rms_fwd_call(xs, sp["g_mix_pre"], name="rms_mix_pre", task=_ag_d2d_task([g_in]))
    w_all_t = _reorder_rows_call(g_in.reshape(N_CHIP * IN_PAD, D), True, name="w_in_rows")
    proj, (g_out, g_mq) = _mm_nt(h1, w_all_t, "plain", rows=(0, 3072), name="mm_proj",
                                 task=gather_ici(["w_out", "w_mq"]))
    fl_raw = _mm_nt(h1, w_all_t, "plain", rows=(3072, 128), name="mm_gate", out_dtype=F32, tn=128)
    c_rep, c_t = _fox_prep_call(fl_raw, b_pad, name="fox_prep")
    bias = _chk_bias_call(_rel_table_to_g(sp["rel_bias"]), name="chk_bias")
    mid = ["w_mk", "w_mv", "w_mo", "w_ff1"]
    (yf, lse), got = _fox_fwd_call(proj, c_rep, c_t, name="fox_fwd",
                                   task=_merge_tasks([gather_ici(mid), _ag_d2d_task([g_out, g_mq])]))
    g_mid, (g_out, g_mq) = got[:4], got[4:]
    yc, got = _chk_fwd_call(proj, bias, name="chk_fwd",
                            task=_merge_tasks([gather_ici(["w_ff2"]), _ag_d2d_task(g_mid)]))
    g_ff2, (g_mk, g_mv, g_mo, g_ff1) = got[0], got[1:]
    yn = _mix_norm_fwd_call(yf, yc, sp["g_fox_out"], sp["g_chk_out"], name="mix_norm_fwd")
    z, (g_ff2,) = _mm_nn(yn, g_out, "rows", name="mm_out", out_dtype=F32, task=_ag_d2d_task([g_ff2]))
    x1, h2 = _post_pre_call(xs, z, sp["g_mix_post"], sp["g_mem_pre"], name="post_mix")
    memn = _rms_fwd_call(mems, sp["g_mem_kv"], name="rms_mem_kv")
    q2 = _mm_nn(h2, g_mq, "rows", name="mm_mq")
    k2 = _mm_nn(memn, g_mk, "rows", name="mm_mk")
    v2 = _mm_nn(memn, g_mv, "rows", name="mm_mv")
    o2 = _mem_fwd_call(q2, k2, v2, name="mem_fwd")
    y2 = _mm_nn(o2, g_mo, "rows", name="mm_mo", out_dtype=F32)
    x2, h3 = _post_pre_call(x1, y2, sp["g_mem_post"], sp["g_ff_pre"], name="post_mem")
    act, relu = _mm_nn(h3, g_ff1, "cols", name="mm_ff1", epi="relu2")
    y3 = _mm_nn(act, g_ff2, "rows", name="mm_ff2", out_dtype=F32, tm=1024)
    loss_blk, dx3, dy3, dg_ff_post = _final_call(x2, y3, sp["g_ff_post"], target, name="final")

    d_ff2 = _mm_tn(act, dy3, name="mm_dff2", tk=512, tn=1024).reshape(N_CHIP, D_FF // N_CHIP, D)
    du, (r1,) = _mm_nt(dy3, g_ff2, "rows", name="mm_du", mul2r=relu, task=_rs_pair_task([d_ff2]))
    p_ff2 = pair_add("w_ff2", d_ff2, r1)
    d_ff1 = _mm_tn(h3, du, name="mm_dff1", cols4=True)
    dh3, (r1,) = _mm_nt(du, g_ff1, "cols", name="mm_dh3", out_dtype=F32, tm=1024, task=_rs_pair_task([d_ff1]))
    p_ff1 = pair_add("w_ff1", d_ff1, r1)
    dx2, dy2, dg_ff_pre, dg_mem_post = _bwd_mid_call(dx3, x2, dh3, sp["g_ff_pre"], y2, sp["g_mem_post"], name="bwd_ff")
    d_mo = _mm_tn(o2, dy2, name="mm_dmo").reshape(N_CHIP, D // N_CHIP, D)
    do2 = _mm_nt(dy2, g_mo, "rows", name="mm_do2")
    dq2, dk2, dv2 = _mem_bwd_call(q2, k2, v2, do2, name="mem_bwd")
    d_mq = _mm_tn(h2, dq2, name="mm_dmq").reshape(N_CHIP, D // N_CHIP, D)
    dh2 = _mm_nt(dq2, g_mq, "rows", name="mm_dh2", out_dtype=F32)
    d_mk = _mm_tn(memn, dk2, name="mm_dmk").reshape(N_CHIP, D // N_CHIP, D)
    d_mv = _mm_tn(memn, dv2, name="mm_dmv").reshape(N_CHIP, D // N_CHIP, D)
    dmn_k = _mm_nt(dk2, g_mk, "rows", name="mm_dmemk", out_dtype=F32)
    dmn_v = _mm_nt(dv2, g_mv, "rows", name="mm_dmemv", out_dtype=F32)
    dg_mem_kv = _gain_grad_call(mems, sp["g_mem_kv"], dmn_k, dmn_v, name="gain_mem_kv")
    dx1, dz, dg_mem_pre, dg_mix_post = _bwd_mid_call(dx2, x1, dh2, sp["g_mem_pre"], z, sp["g_mix_post"], name="bwd_mem")
    d_out = _mm_tn(yn, dz, name="mm_dout").reshape(N_CHIP, D // N_CHIP, D)
    late = ["w_mo", "w_mq", "w_mk", "w_mv", "w_out"]
    d_late = [d_mo, d_mq, d_mk, d_mv, d_out]
    dyn, r1_late = _mm_nt(dz, g_out, "rows", name="mm_dyn", out_dtype=F32, task=_rs_pair_task(d_late))
    p_late = [pair_add(k, d, r1) for k, d, r1 in zip(late, d_late, r1_late)]
    dof, doc, delta, dg_fox, dg_chk = _mix_norm_bwd_call(dyn, yf, yc, sp["g_fox_out"], sp["g_chk_out"], name="mix_norm_bwd")
    (dqf, dkf, dvf, dct, dcq), r2_ff = _fox_bwd_call(proj, dof, lse, delta, c_rep, c_t, name="fox_bwd",
                                                      task=_rs_chip_task([p_ff2, p_ff1]))
    (dqc, dkc, dvc, dgrev), r2_late = _chk_bwd_call(proj, doc, bias, name="chk_bwd", task=_rs_chip_task(p_late))
    first = ["w_ff2", "w_ff1"] + late
    f_first = [_chip_sum_call(p, r, place_arr, name="rs_chip_sum_" + k)
               for k, p, r in zip(first, [p_ff2, p_ff1] + p_late, r2_ff + r2_late)]
    dc8 = dct[:, :, 0:2, :].transpose(0, 2, 1, 3).reshape(8, S) + dcq[:, ::HEAD].T
    dc_rows = jnp.concatenate([dc8, jnp.zeros((120, S), F32)], axis=0)
    dfl, db_fgt = _fox_gate_bwd_call(dc_rows, fl_raw, b_pad, name="fox_gate_bwd")
    dproj = jnp.concatenate([dqf, dkf, dvf, dqc, dkc, dvc, dfl], axis=1)
    d_all_t, g_first = _mm_tn(dproj, h1, name="mm_dwin", tk=640, tn=1024, task=_rs_gather_task(f_first))
    grads = dict(zip(first, g_first))
    d_in = _reorder_rows_call(d_all_t, False, name="d_in_rows").reshape(N_CHIP, IN_PAD, D)
    delta_w, new_m, new_v = {}, {}, {}

    def adamw_items(names):
        return [(w[k][0], grads[k], m[k][0], v[k][0]) for k in names]

    upd_late, (r1,) = _adamw_call(adamw_items(late), name="adamw_late", tm=64, task=_rs_pair_task([d_in]))
    p_in = pair_add("w_in", d_in, r1)
    dh1, (r2_in,) = _mm_nn(dproj, w_all_t, "plain", name="mm_dh1", out_dtype=F32, tm=1024,
                           task=_rs_chip_task([p_in]))
    f_in = _chip_sum_call(p_in, r2_in, place_arr, name="rs_chip_sum_w_in")
    upd_ff = _adamw_call(adamw_items(first[:2]), name="adamw_ff")
    for k, res in zip(late + first[:2], upd_late + upd_ff):
        grads[k], delta_w[k], new_m[k], new_v[k] = res
    grad_x, dg_mix_pre = _bwd_last_call(dx1, xs, dh1, sp["g_mix_pre"], name="bwd_mix")

    small_g = {"g_mix_pre": dg_mix_pre, "g_mix_post": dg_mix_post, "g_mem_kv": dg_mem_kv, "g_mem_pre": dg_mem_pre,
               "g_mem_post": dg_mem_post, "g_ff_pre": dg_ff_pre, "g_ff_post": dg_ff_post, "g_fox_out": dg_fox,
               "g_chk_out": dg_chk, "b_fgt": db_fgt,
               "rel_bias": _g_to_rel_table(dgrev[:, 0:2, :].reshape(8, ROLL_W))}
    small, loss_out, (g_w_in,) = _small_call(
        small_g, sp, {k: rows(m, k) for k in SMALL}, {k: rows(v, k) for k in SMALL}, loss_blk,
        _rs_gather_task([f_in]), name="small_allreduce_adamw")
    loss = loss_out[0, 0]
    res = _adamw_cols_call(w_in_t, g_w_in, m_in_t, v_in_t, name="adamw_w_in")
    grads["w_in"], delta_w["w_in"], new_m["w_in"], new_v["w_in"] = (a.T for a in res)
    for k in SMALL:
        vals = small[k]
        if k == "rel_bias":
            vals = tuple(a[None] for a in vals)
        grads[k], delta_w[k], new_m[k], new_v[k] = vals

    def out(d, k):
        return d[k][None] if k in BIG else d[k]

    return (loss, grad_x[None], *[out(grads, k) for k in WEIGHTS], *[out(delta_w, k) for k in WEIGHTS],
            *[out(new_m, k) for k in WEIGHTS], *[out(new_v, k) for k in WEIGHTS])
```

```python
import functools

import jax
import jax.numpy as jnp
from jax import lax
from jax.experimental import pallas as pl
from jax.experimental.pallas import tpu as pltpu

F32 = jnp.float32
BF16 = jnp.bfloat16

D = 1024
HEAD = 64
N_PAIR = 4
D_GRP = 512
CHUNK = 64
LEFT = 8
MAX_REL = 128
N_REL = 2 * MAX_REL + 1
N_MEM = 256
MEM_HEADS = 4
MEM_HD = 256
D_FF = 4096
D_IN = 3080
D_ALL = 3200
EPS = 1e-6
TQ = 256
WIN = (LEFT + TQ // CHUNK) * CHUNK
PADK = LEFT * CHUNK
ROLL_W = 1024
NEG = -1e30
N_CHIP = 4
VMEM_LIMIT = 48 * 1024 * 1024

ADAM_LR = 0.001
ADAM_B1 = 0.9
ADAM_B2 = 0.999
ADAM_EPS = 1e-08
ADAM_WD = 0.01
ADAM_STEP = 10

MESH = pl.DeviceIdType.MESH


def _cparams():
    return pltpu.CompilerParams(vmem_limit_bytes=VMEM_LIMIT)


ANY = pl.BlockSpec(memory_space=pl.ANY)


class _Task:
    def __init__(self, arrays, out_shapes, sems, issue, drain, aliases=None):
        self.arrays, self.out_shapes, self.sems = list(arrays), list(out_shapes), list(sems)
        self.issue, self.drain, self.aliases = issue, drain, dict(aliases or {})


def _merge_tasks(tasks):
    tasks = [t for t in tasks if t is not None]
    if len(tasks) == 1:
        return tasks[0]
    cuts, a, o, s = [], 0, 0, 0
    aliases = {}
    for t in tasks:
        cuts.append((a, o, s))
        aliases.update({a + i: o + j for i, j in t.aliases.items()})
        a, o, s = a + len(t.arrays), o + len(t.out_shapes), s + len(t.sems)

    def part(fn_name):
        def run(ins, outs, sems):
            for t, (a0, o0, s0) in zip(tasks, cuts):
                getattr(t, fn_name)(ins[a0:a0 + len(t.arrays)], outs[o0:o0 + len(t.out_shapes)],
                                    sems[s0:s0 + len(t.sems)])
        return run

    return _Task([x for t in tasks for x in t.arrays], [x for t in tasks for x in t.out_shapes],
                 [x for t in tasks for x in t.sems], part("issue"), part("drain"), aliases)


def _pallas(body, *, grid, in_specs, out_specs, out_shape, name, scratch_shapes=(), task=None):
    if task is None:
        return pl.pallas_call(body, grid=grid, in_specs=list(in_specs), out_specs=out_specs, out_shape=out_shape,
                              scratch_shapes=list(scratch_shapes), name=name, compiler_params=_cparams())
    single = not isinstance(out_shape, (tuple, list))
    o_shapes = [out_shape] if single else list(out_shape)
    o_specs = [out_specs] if single else list(out_specs)
    n_in, n_out, n_scr = len(in_specs), len(o_shapes), len(scratch_shapes)
    t_in, t_out = len(task.arrays), len(task.out_shapes)

    def carried(*refs):
        cut = [n_in, t_in, n_out, t_out, n_scr]
        parts, p = [], 0
        for c in cut:
            parts.append(refs[p:p + c])
            p += c
        ins, tins, outs, touts, scr = parts
        tsems = refs[p:]
        ids = [pl.program_id(a) for a in range(len(grid))]
        first = functools.reduce(jnp.logical_and, [i == 0 for i in ids])
        last = functools.reduce(jnp.logical_and, [i == g - 1 for i, g in zip(ids, grid)])

        @pl.when(first)
        def _():
            task.issue(tins, touts, tsems)
        body(*ins, *outs, *scr)

        @pl.when(last)
        def _():
            task.drain(tins, touts, tsems)

    call = pl.pallas_call(
        carried, grid=grid, in_specs=list(in_specs) + [ANY] * t_in, out_specs=o_specs + [ANY] * t_out,
        out_shape=o_shapes + list(task.out_shapes), scratch_shapes=list(scratch_shapes) + list(task.sems),
        input_output_aliases={n_in + i: n_out + j for i, j in task.aliases.items()},
        name=name, compiler_params=_cparams())

    def run(*args):
        res = call(*args, *task.arrays)
        outs = res[:n_out]
        return (outs[0] if single else tuple(outs)), list(res[n_out:])

    return run


def _comm_call(task, *, name):
    t_in, t_out = len(task.arrays), len(task.out_shapes)

    def body(*refs):
        tins, touts, tsems = refs[:t_in], refs[t_in:t_in + t_out], refs[t_in + t_out:]
        task.issue(tins, touts, tsems)
        task.drain(tins, touts, tsems)

    return pl.pallas_call(
        body, in_specs=[ANY] * t_in, out_specs=[ANY] * t_out, out_shape=list(task.out_shapes),
        scratch_shapes=list(task.sems), input_output_aliases=dict(task.aliases), name=name,
    )(*task.arrays)


def _dot(a, b):
    return jnp.dot(a, b, preferred_element_type=F32)


def _dot_nt(a, b):
    return lax.dot_general(a, b, (((1,), (1,)), ((), ())), preferred_element_type=F32)


def _dot_tn(a, b):
    return lax.dot_general(a, b, (((0,), (0,)), ((), ())), preferred_element_type=F32)


def _split3(x):
    hi = x.astype(BF16)
    r1 = x - hi.astype(F32)
    mid = r1.astype(BF16)
    lo = (r1 - mid.astype(F32)).astype(BF16)
    return hi, mid, lo


def _dot3(x, m01):
    hi, mid, lo = _split3(x)
    return _dot(hi, m01) + _dot(mid, m01) + _dot(lo, m01)


def _dot3_l(m01, x):
    hi, mid, lo = _split3(x)
    return _dot(m01, hi) + _dot(m01, mid) + _dot(m01, lo)


def _mm_nn(a, b, kind, *, name, out_dtype=BF16, tm=2048, tn=512, epi=None, task=None):
    M, K = a.shape
    if kind == "plain":
        N = b.shape[1]
        b_spec = pl.BlockSpec((K, tn), lambda m, n: (0, n))
    elif kind == "rows":
        N = b.shape[2]
        b_spec = pl.BlockSpec((N_CHIP, K // N_CHIP, tn), lambda m, n: (0, 0, n))
    else:
        nq = b.shape[2]
        N = N_CHIP * nq
        per = nq // tn
        b_spec = pl.BlockSpec((None, K, tn), lambda m, n: (n // per, 0, n % per))
    tm = min(tm, M)
    kq = K // N_CHIP

    def body(a_ref, b_ref, *o_refs):
        if kind == "rows":
            acc = _dot(a_ref[:, 0:kq], b_ref[0])
            for j in range(1, N_CHIP):
                acc += _dot(a_ref[:, j * kq:(j + 1) * kq], b_ref[j])
        else:
            acc = _dot(a_ref[...], b_ref[...])
        if epi == "relu2":
            r = jnp.maximum(acc, 0.0)
            o_refs[0][...] = (r * r).astype(BF16)
            o_refs[1][...] = r.astype(BF16)
        else:
            o_refs[0][...] = acc.astype(out_dtype)

    o_spec = pl.BlockSpec((tm, tn), lambda m, n: (m, n))
    if epi == "relu2":
        out_shape = (jax.ShapeDtypeStruct((M, N), BF16), jax.ShapeDtypeStruct((M, N), BF16))
        out_specs = (o_spec, o_spec)
    else:
        out_shape = jax.ShapeDtypeStruct((M, N), out_dtype)
        out_specs = o_spec
    return _pallas(
        body, grid=(M // tm, N // tn),
        in_specs=[pl.BlockSpec((tm, K), lambda m, n: (m, 0)), b_spec],
        out_specs=out_specs, out_shape=out_shape, name=name, task=task,
    )(a, b)


def _mm_nt(a, b, kind, *, name, out_dtype=BF16, tm=2048, tn=512, mul2r=None, task=None, rows=None):
    M, K = a.shape
    if kind == "plain":
        first, N = rows if rows is not None else (0, b.shape[0])
        n0 = first // tn
        b_spec = pl.BlockSpec((tn, K), lambda m, n: (n0 + n, 0))
    elif kind == "rows":
        nq = b.shape[1]
        N = N_CHIP * nq
        tn = min(tn, nq)
        per = nq // tn
        b_spec = pl.BlockSpec((None, tn, K), lambda m, n: (n // per, n % per, 0))
    else:
        N = b.shape[1]
        b_spec = pl.BlockSpec((N_CHIP, tn, K // N_CHIP), lambda m, n: (0, n, 0))
    tm = min(tm, M)
    kq = K // N_CHIP

    def body(a_ref, b_ref, *rest):
        o_ref = rest[-1]
        if kind == "cols":
            acc = _dot_nt(a_ref[:, 0:kq], b_ref[0])
            for j in range(1, N_CHIP):
                acc += _dot_nt(a_ref[:, j * kq:(j + 1) * kq], b_ref[j])
        else:
            acc = _dot_nt(a_ref[...], b_ref[...])
        if mul2r is not None:
            acc = acc * (2.0 * rest[0][...].astype(F32))
        o_ref[...] = acc.astype(out_dtype)

    in_specs = [pl.BlockSpec((tm, K), lambda m, n: (m, 0)), b_spec]
    args = [a, b]
    if mul2r is not None:
        in_specs.append(pl.BlockSpec((tm, tn), lambda m, n: (m, n)))
        args.append(mul2r)
    return _pallas(
        body, grid=(M // tm, N // tn), in_specs=in_specs,
        out_specs=pl.BlockSpec((tm, tn), lambda m, n: (m, n)),
        out_shape=jax.ShapeDtypeStruct((M, N), out_dtype), name=name, task=task,
    )(*args)


def _mm_tn(a, b, *, name, out_dtype=BF16, tk=1024, tn=512, cols4=False, task=None):
    M, K1 = a.shape
    N = b.shape[1]
    tk = min(tk, K1)
    tn = min(tn, N)

    def body(a_ref, b_ref, o_ref):
        o_ref[...] = _dot_tn(a_ref[...], b_ref[...]).astype(out_dtype)

    if cols4:
        per = (N // N_CHIP) // tn
        out_shape = jax.ShapeDtypeStruct((N_CHIP, K1, N // N_CHIP), out_dtype)
        o_spec = pl.BlockSpec((None, tk, tn), lambda k, n: (n // per, k, n % per))
    else:
        out_shape = jax.ShapeDtypeStruct((K1, N), out_dtype)
        o_spec = pl.BlockSpec((tk, tn), lambda k, n: (k, n))
    return _pallas(
        body, grid=(K1 // tk, N // tn),
        in_specs=[pl.BlockSpec((M, tk), lambda k, n: (0, k)), pl.BlockSpec((M, tn), lambda k, n: (0, n))],
        out_specs=o_spec, out_shape=out_shape, name=name, task=task,
    )(a, b)


def _rms(x, g):
    r = lax.rsqrt(jnp.mean(x * x, axis=-1, keepdims=True) + EPS)
    return x * r * g


def _rms_bwd(x, g, dy):
    r = lax.rsqrt(jnp.mean(x * x, axis=-1, keepdims=True) + EPS)
    xh = x * r
    dg = jnp.sum(dy * xh, axis=0, keepdims=True)
    dxh = dy * g
    dx = r * (dxh - xh * jnp.mean(dxh * xh, axis=-1, keepdims=True))
    return dx, dg


def _row_spec(tm, n):
    return pl.BlockSpec((tm, n), lambda i: (i, 0))


def _vec_spec(n):
    return pl.BlockSpec((1, n), lambda i: (0, 0))


def _acc_spec(n):
    return pl.BlockSpec((8, n), lambda i: (0, 0))


def _acc_add(ref, row, i):
    @pl.when(i == 0)
    def _():
        ref[...] = jnp.zeros_like(ref)
    ref[0:1, :] += row


def _rms_fwd_call(x, g, *, name, tm=256, task=None):
    M, n = x.shape
    tm = min(tm, M)

    def body(x_ref, g_ref, h_ref):
        h_ref[...] = _rms(x_ref[...], g_ref[...]).astype(BF16)

    return _pallas(
        body, grid=(M // tm,), in_specs=[_row_spec(tm, n), _vec_spec(n)], out_specs=_row_spec(tm, n),
        out_shape=jax.ShapeDtypeStruct((M, n), BF16), name=name, task=task,
    )(x, g)


def _post_pre_call(xres, z, g_post, g_pre, *, name, tm=256):
    M, n = xres.shape

    def body(x_ref, z_ref, gp_ref, gn_ref, xo_ref, h_ref):
        xn = x_ref[...] + _rms(z_ref[...], gp_ref[...])
        xo_ref[...] = xn
        h_ref[...] = _rms(xn, gn_ref[...]).astype(BF16)

    return pl.pallas_call(
        body, grid=(M // tm,),
        in_specs=[_row_spec(tm, n), _row_spec(tm, n), _vec_spec(n), _vec_spec(n)],
        out_specs=(_row_spec(tm, n), _row_spec(tm, n)),
        out_shape=(jax.ShapeDtypeStruct((M, n), F32), jax.ShapeDtypeStruct((M, n), BF16)),
        name=name, compiler_params=_cparams(),
    )(xres, z, g_post, g_pre)


def _final_call(x2, y3, g_post, target, *, name, tm=256):
    M, n = x2.shape

    def body(x_ref, y_ref, g_ref, t_ref, loss_ref, dx_ref, dy_ref, dg_ref):
        i = pl.program_id(0)
        y = y_ref[...]
        g = g_ref[...]
        diff = x_ref[...] + _rms(y, g) - t_ref[...]
        part = 0.5 * jnp.sum(jnp.sum(diff * diff, axis=1, keepdims=True), axis=0, keepdims=True) / n

        @pl.when(i == 0)
        def _():
            loss_ref[...] = jnp.zeros_like(loss_ref)
        loss_ref[...] += jnp.broadcast_to(part, loss_ref.shape)
        dx = diff / n
        dx_ref[...] = dx
        dy, dg = _rms_bwd(y, g, dx)
        dy_ref[...] = dy.astype(BF16)
        _acc_add(dg_ref, dg, i)

    return pl.pallas_call(
        body, grid=(M // tm,),
        in_specs=[_row_spec(tm, n), _row_spec(tm, n), _vec_spec(n), _row_spec(tm, n)],
        out_specs=(pl.BlockSpec((8, 128), lambda i: (0, 0)), _row_spec(tm, n), _row_spec(tm, n), _acc_spec(n)),
        out_shape=(jax.ShapeDtypeStruct((8, 128), F32), jax.ShapeDtypeStruct((M, n), F32),
                   jax.ShapeDtypeStruct((M, n), BF16), jax.ShapeDtypeStruct((8, n), F32)),
        name=name, compiler_params=_cparams(),
    )(x2, y3, g_post, target)


def _bwd_mid_call(dx_in, x, dh, g_pre, y, g_post, *, name, tm=256):
    M, n = x.shape

    def body(dxi_ref, x_ref, dh_ref, gpre_ref, y_ref, gpost_ref, dx_ref, dy_ref, dgpre_ref, dgpost_ref):
        i = pl.program_id(0)
        d1, dg1 = _rms_bwd(x_ref[...], gpre_ref[...], dh_ref[...])
        dx = dxi_ref[...] + d1
        dx_ref[...] = dx
        dy, dg2 = _rms_bwd(y_ref[...], gpost_ref[...], dx)
        dy_ref[...] = dy.astype(BF16)
        _acc_add(dgpre_ref, dg1, i)
        _acc_add(dgpost_ref, dg2, i)

    return pl.pallas_call(
        body, grid=(M // tm,),
        in_specs=[_row_spec(tm, n), _row_spec(tm, n), _row_spec(tm, n), _vec_spec(n), _row_spec(tm, n), _vec_spec(n)],
        out_specs=(_row_spec(tm, n), _row_spec(tm, n), _acc_spec(n), _acc_spec(n)),
        out_shape=(jax.ShapeDtypeStruct((M, n), F32), jax.ShapeDtypeStruct((M, n), BF16),
                   jax.ShapeDtypeStruct((8, n), F32), jax.ShapeDtypeStruct((8, n), F32)),
        name=name, compiler_params=_cparams(),
    )(dx_in, x, dh, g_pre, y, g_post)


def _bwd_last_call(dx_in, x, dh, g_pre, *, name, tm=256, task=None):
    M, n = x.shape

    def body(dxi_ref, x_ref, dh_ref, g_ref, dx_ref, dg_ref):
        i = pl.program_id(0)
        d1, dg1 = _rms_bwd(x_ref[...], g_ref[...], dh_ref[...])
        dx_ref[...] = dxi_ref[...] + d1
        _acc_add(dg_ref, dg1, i)

    return _pallas(
        body, grid=(M // tm,),
        in_specs=[_row_spec(tm, n), _row_spec(tm, n), _row_spec(tm, n), _vec_spec(n)],
        out_specs=(_row_spec(tm, n), _acc_spec(n)),
        out_shape=(jax.ShapeDtypeStruct((M, n), F32), jax.ShapeDtypeStruct((8, n), F32)),
        name=name, task=task,
    )(dx_in, x, dh, g_pre)


def _gain_grad_call(x, g, dy_a, dy_b, *, name):
    M, n = x.shape

    def body(x_ref, g_ref, a_ref, b_ref, dg_ref):
        _, dg = _rms_bwd(x_ref[...], g_ref[...], a_ref[...] + b_ref[...])
        dg_ref[...] = jnp.zeros_like(dg_ref)
        dg_ref[0:1, :] = dg

    return pl.pallas_call(
        body, grid=(1,),
        in_specs=[_row_spec(M, n), _vec_spec(n), _row_spec(M, n), _row_spec(M, n)],
        out_specs=_acc_spec(n), out_shape=jax.ShapeDtypeStruct((8, n), F32),
        name=name, compiler_params=_cparams(),
    )(x, g, dy_a, dy_b)


def _head_group_matrix():
    a = lax.broadcasted_iota(jnp.int32, (D_GRP, D_GRP), 0) // HEAD
    b = lax.broadcasted_iota(jnp.int32, (D_GRP, D_GRP), 1) // HEAD
    return jnp.where(a == b, 1.0, 0.0).astype(BF16)


def _mix_norm_fwd_call(yf, yc, gf, gc, *, name, tm=256):
    M = yf.shape[0]

    def body(yf_ref, yc_ref, gf_ref, gc_ref, o_ref):
        o_ref[:, 0:D_GRP] = _rms(yf_ref[...], gf_ref[...]).astype(BF16)
        o_ref[:, D_GRP:D] = _rms(yc_ref[...], gc_ref[...]).astype(BF16)

    return pl.pallas_call(
        body, grid=(M // tm,),
        in_specs=[_row_spec(tm, D_GRP), _row_spec(tm, D_GRP), _vec_spec(D_GRP), _vec_spec(D_GRP)],
        out_specs=_row_spec(tm, D), out_shape=jax.ShapeDtypeStruct((M, D), BF16),
        name=name, compiler_params=_cparams(),
    )(yf, yc, gf, gc)


def _mix_norm_bwd_call(dyn, yf, yc, gf, gc, *, name, tm=TQ):
    M = yf.shape[0]

    def body(dyn_ref, yf_ref, yc_ref, gf_ref, gc_ref, dof_ref, doc_ref, delta_ref, dgf_ref, dgc_ref):
        i = pl.program_id(0)
        yf_ = yf_ref[...]
        dof, dgf = _rms_bwd(yf_, gf_ref[...], dyn_ref[:, 0:D_GRP])
        doc, dgc = _rms_bwd(yc_ref[...], gc_ref[...], dyn_ref[:, D_GRP:D])
        dof_b = dof.astype(BF16)
        dof_ref[...] = dof_b
        doc_ref[...] = doc.astype(BF16)
        prod = dof_b.astype(F32) * yf_
        hi = prod.astype(BF16)
        lo = (prod - hi.astype(F32)).astype(BF16)
        grp = _head_group_matrix()
        delta_ref[...] = (_dot(hi, grp) + _dot(lo, grp)).T
        _acc_add(dgf_ref, dgf, i)
        _acc_add(dgc_ref, dgc, i)

    return pl.pallas_call(
        body, grid=(M // tm,),
        in_specs=[_row_spec(tm, D), _row_spec(tm, D_GRP), _row_spec(tm, D_GRP), _vec_spec(D_GRP), _vec_spec(D_GRP)],
        out_specs=(_row_spec(tm, D_GRP), _row_spec(tm, D_GRP), pl.BlockSpec((None, D_GRP, tm), lambda i: (i, 0, 0)),
                   _acc_spec(D_GRP), _acc_spec(D_GRP)),
        out_shape=(jax.ShapeDtypeStruct((M, D_GRP), BF16), jax.ShapeDtypeStruct((M, D_GRP), BF16),
                   jax.ShapeDtypeStruct((M // tm, D_GRP, tm), F32), jax.ShapeDtypeStruct((8, D_GRP), F32),
                   jax.ShapeDtypeStruct((8, D_GRP), F32)),
        name=name, compiler_params=_cparams(),
    )(dyn, yf, yc, gf, gc)


def _tri(n, lower_incl):
    a = lax.broadcasted_iota(jnp.int32, (n, n), 0)
    b = lax.broadcasted_iota(jnp.int32, (n, n), 1)
    return jnp.where(a >= b, 1.0, 0.0).astype(BF16) if lower_incl else jnp.where(a <= b, 1.0, 0.0).astype(BF16)


def _fox_prep_call(fl_raw, b_pad, *, name):
    S = fl_raw.shape[0]
    nb = S // TQ

    def body(fl_ref, b_ref, crep_ref, ct_ref, carry_ref):
        i = pl.program_id(0)

        @pl.when(i == 0)
        def _():
            carry_ref[...] = jnp.zeros_like(carry_ref)
        logf = jax.nn.log_sigmoid(fl_ref[...] + b_ref[...])
        cb = _dot3_l(_tri(TQ, True), logf) + carry_ref[0:1, :]
        carry_ref[0:1, :] = cb[TQ - 1:TQ, :]
        a = lax.broadcasted_iota(jnp.int32, (128, D_GRP), 0)
        b = lax.broadcasted_iota(jnp.int32, (128, D_GRP), 1) // HEAD
        expand = jnp.where(a == b, 1.0, 0.0).astype(BF16)
        crep = _dot3(cb, expand)
        crep_ref[...] = crep
        ct_ref[...] = crep.T

    return pl.pallas_call(
        body, grid=(nb,),
        in_specs=[_row_spec(TQ, 128), _vec_spec(128)],
        out_specs=(_row_spec(TQ, D_GRP), pl.BlockSpec((None, D_GRP, TQ), lambda i: (i, 0, 0))),
        out_shape=(jax.ShapeDtypeStruct((S, D_GRP), F32), jax.ShapeDtypeStruct((nb, D_GRP, TQ), F32)),
        scratch_shapes=[pltpu.VMEM((8, 128), F32)],
        name=name, compiler_params=_cparams(),
    )(fl_raw, b_pad)


def _lane_masks():
    lane = lax.broadcasted_iota(jnp.int32, (1, 128), 1)
    return lane < HEAD, lane >= HEAD


def _fox_fwd_call(proj, c_rep, c_t, *, name, task=None):
    S = proj.shape[0]
    nq = S // TQ
    scale = HEAD ** -0.5

    def body(q_ref, k_ref, v_ref, c_ref, ct_ref, o_ref, lse_ref):
        i = pl.program_id(1)
        m_lo, m_hi = _lane_masks()
        masks = (m_lo, m_hi)
        q = q_ref[...]
        qm = [jnp.where(mk, q, jnp.zeros_like(q)) for mk in masks]
        cq = c_ref[...]
        cqh = [cq[:, 0:1], cq[:, HEAD:HEAD + 1]]
        row = lax.broadcasted_iota(jnp.int32, (TQ, TQ), 0)
        col = lax.broadcasted_iota(jnp.int32, (TQ, TQ), 1)

        def scores(j):
            start = pl.multiple_of(j * TQ, TQ)
            k = k_ref[pl.ds(start, TQ), :]
            ct = ct_ref[j]
            return tuple(_dot_nt(qm[h], k) * scale + (cqh[h] - ct[HEAD * h:HEAD * h + 1, :]) for h in range(2))

        def update(j, ss, state, masked):
            ms, ls, acc = state
            start = pl.multiple_of(j * TQ, TQ)
            v = v_ref[pl.ds(start, TQ), :]
            new_m, new_l, pv, alpha_l = [], [], [], []
            for h in range(2):
                s = ss[h]
                if masked:
                    s = jnp.where(row >= col, s, NEG)
                mn = jnp.maximum(ms[h], jnp.max(s, axis=1, keepdims=True))
                alpha = jnp.exp(ms[h] - mn)
                p = jnp.exp(s - mn)
                new_l.append(alpha * ls[h] + jnp.sum(p, axis=1, keepdims=True))
                new_m.append(mn)
                alpha_l.append(alpha)
                pv.append(_dot(p.astype(BF16), jnp.where(masks[h], v, jnp.zeros_like(v))))
            alpha_lane = jnp.where(m_lo, alpha_l[0], alpha_l[1])
            acc = acc * alpha_lane + pv[0] + pv[1]
            return (tuple(new_m), tuple(new_l), acc)

        def step(j, carry):
            ss, state = carry
            return (scores(j + 1), update(j, ss, state, False))

        init = ((jnp.full((TQ, 1), NEG, F32),) * 2, (jnp.zeros((TQ, 1), F32),) * 2, jnp.zeros((TQ, 128), F32))
        ss, state = lax.fori_loop(0, i, step, (scores(0), init))
        ms, ls, acc = update(i, ss, state, True)
        l_lane = jnp.where(m_lo, ls[0], ls[1])
        o_ref[...] = acc / l_lane
        lse_ref[...] = jnp.where(m_lo, ms[0] + jnp.log(ls[0]), ms[1] + jnp.log(ls[1])).T

    return _pallas(
        body, grid=(N_PAIR, nq),
        in_specs=[pl.BlockSpec((TQ, 128), lambda p, i: (i, p)),
                  pl.BlockSpec((S, 128), lambda p, i: (0, N_PAIR + p)),
                  pl.BlockSpec((S, 128), lambda p, i: (0, 2 * N_PAIR + p)),
                  pl.BlockSpec((TQ, 128), lambda p, i: (i, p)),
                  pl.BlockSpec((nq, 128, TQ), lambda p, i: (0, p, 0))],
        out_specs=(pl.BlockSpec((TQ, 128), lambda p, i: (i, p)), pl.BlockSpec((None, 128, TQ), lambda p, i: (i, p, 0))),
        out_shape=(jax.ShapeDtypeStruct((S, D_GRP), F32), jax.ShapeDtypeStruct((nq, D_GRP, TQ), F32)),
        name=name, task=task,
    )(proj, proj, proj, c_rep, c_t)


def _fox_bwd_call(proj, do, lse_t, delta_t, c_rep, c_t, *, name, task=None):
    S = proj.shape[0]
    nq = S // TQ
    scale = HEAD ** -0.5

    def body(q_ref, k_ref, v_ref, do_ref, lse_ref, dl_ref, ck_ref, ct_ref,
             dq_ref, dk_ref, dv_ref, dcq_ref, dck_ref, dqa_ref):
        j = pl.program_id(1)
        m_lo, m_hi = _lane_masks()
        masks = (m_lo, m_hi)

        @pl.when(j == 0)
        def _():
            dqa_ref[...] = jnp.zeros_like(dqa_ref)
            dcq_ref[...] = jnp.zeros_like(dcq_ref)
        k = k_ref[...]
        v = v_ref[...]
        km = [jnp.where(mk, k, jnp.zeros_like(k)) for mk in masks]
        ck = ck_ref[...]
        krow = lax.broadcasted_iota(jnp.int32, (TQ, TQ), 0)
        qcol = lax.broadcasted_iota(jnp.int32, (TQ, TQ), 1)

        def probs(i):
            start = pl.multiple_of(i * TQ, TQ)
            q = q_ref[pl.ds(start, TQ), :]
            do = do_ref[pl.ds(start, TQ), :]
            lse = lse_ref[i]
            cq = ct_ref[i]
            out = []
            for h in range(2):
                lo = HEAD * h
                qm = jnp.where(masks[h], q, jnp.zeros_like(q))
                dom = jnp.where(masks[h], do, jnp.zeros_like(do))
                st = _dot_nt(k, qm) * scale + (cq[lo:lo + 1, :] - ck[:, lo:lo + 1])
                out.append((jnp.exp(st - lse[lo:lo + 1, :]), _dot_nt(v, dom)))
            return tuple(out)

        def update(i, pd, carry, masked):
            dk, dv, dck = carry
            start = pl.multiple_of(i * TQ, TQ)
            q = q_ref[pl.ds(start, TQ), :]
            do = do_ref[pl.ds(start, TQ), :]
            dl = dl_ref[i]
            dq = jnp.zeros((TQ, 128), F32)
            new_dck = []
            for h in range(2):
                lo = HEAD * h
                qm = jnp.where(masks[h], q, jnp.zeros_like(q))
                dom = jnp.where(masks[h], do, jnp.zeros_like(do))
                pt, dpt = pd[h]
                if masked:
                    pt = jnp.where(qcol >= krow, pt, 0.0)
                dst = pt * (dpt - dl[lo:lo + 1, :])
                dcq_ref[i, h:h + 1, :] += jnp.sum(dst, axis=0, keepdims=True)
                new_dck.append(dck[h] + jnp.sum(dst, axis=1, keepdims=True))
                dsb = (dst * scale).astype(BF16)
                dv = dv + _dot(pt.astype(BF16), dom)
                dk = dk + _dot(dsb, qm)
                dq = dq + _dot_tn(dsb, km[h])
            dqa_ref[pl.ds(start, TQ), :] += dq
            return (dk, dv, tuple(new_dck))

        def step(i, carry):
            pd, sums = carry
            return (probs(jnp.minimum(i + 1, nq - 1)), update(i, pd, sums, False))

        init = (jnp.zeros((TQ, 128), F32), jnp.zeros((TQ, 128), F32), (jnp.zeros((TQ, 1), F32),) * 2)
        first = probs(j)
        second = probs(jnp.minimum(j + 1, nq - 1))
        _, (dk, dv, dck) = lax.fori_loop(j + 1, nq, step, (second, update(j, first, init, True)))
        dk_ref[...] = dk.astype(BF16)
        dv_ref[...] = dv.astype(BF16)
        dck_ref[...] = -jnp.where(m_lo, dck[0], dck[1])

        @pl.when(j == nq - 1)
        def _():
            dq_ref[...] = dqa_ref[...].astype(BF16)

    res = lambda p, j: (0, p)
    stat = pl.BlockSpec((nq, 128, TQ), lambda p, j: (0, p, 0))
    blk = pl.BlockSpec((TQ, 128), lambda p, j: (j, p))
    return _pallas(
        body, grid=(N_PAIR, nq), task=task,
        in_specs=[pl.BlockSpec((S, 128), res),
                  pl.BlockSpec((TQ, 128), lambda p, j: (j, N_PAIR + p)),
                  pl.BlockSpec((TQ, 128), lambda p, j: (j, 2 * N_PAIR + p)),
                  pl.BlockSpec((S, 128), res), stat, stat, blk, stat],
        out_specs=(pl.BlockSpec((S, 128), res), blk, blk,
                   pl.BlockSpec((None, nq, 8, TQ), lambda p, j: (p, 0, 0, 0)), blk),
        out_shape=(jax.ShapeDtypeStruct((S, D_GRP), BF16), jax.ShapeDtypeStruct((S, D_GRP), BF16),
                   jax.ShapeDtypeStruct((S, D_GRP), BF16), jax.ShapeDtypeStruct((N_PAIR, nq, 8, TQ), F32),
                   jax.ShapeDtypeStruct((S, D_GRP), F32)),
        scratch_shapes=[pltpu.VMEM((S, 128), F32)],
        name=name,
    )(proj, proj, proj, do, lse_t, delta_t, c_rep, c_t)


def _fox_gate_bwd_call(dc_rows, fl_raw, b_pad, *, name):
    S = fl_raw.shape[0]
    nb = S // TQ

    def body(dc_ref, fl_ref, b_ref, dfl_ref, db_ref, carry_ref):
        i = pl.program_id(0)

        @pl.when(i == 0)
        def _():
            carry_ref[...] = jnp.zeros_like(carry_ref)
        rc = _dot3(dc_ref[...], _tri(TQ, True)) + carry_ref[:, 0:1]
        carry_ref[...] = jnp.broadcast_to(rc[:, 0:1], carry_ref.shape)
        fl = fl_ref[...] + b_ref[...]
        dfl = rc.T * jax.nn.sigmoid(-fl)
        dfl_ref[...] = dfl.astype(BF16)
        _acc_add(db_ref, jnp.sum(dfl, axis=0, keepdims=True), i)

    rev = lambda i: (nb - 1 - i, 0)
    return pl.pallas_call(
        body, grid=(nb,),
        in_specs=[pl.BlockSpec((128, TQ), lambda i: (0, nb - 1 - i)), pl.BlockSpec((TQ, 128), rev), _vec_spec(128)],
        out_specs=(pl.BlockSpec((TQ, 128), rev), _acc_spec(128)),
        out_shape=(jax.ShapeDtypeStruct((S, 128), BF16), jax.ShapeDtypeStruct((8, 128), F32)),
        scratch_shapes=[pltpu.VMEM((128, 128), F32)],
        name=name, compiler_params=_cparams(),
    )(dc_rows, fl_raw, b_pad)


def _chk_bias_call(g_rev, *, name):
    def body(g_ref, o_ref):
        x = jnp.broadcast_to(g_ref[...], (TQ, ROLL_W))
        rolled = pltpu.roll(x, ROLL_W - (TQ - 1), 1, stride=1, stride_axis=0)
        qc = lax.broadcasted_iota(jnp.int32, (TQ, WIN), 0) // CHUNK
        kc = lax.broadcasted_iota(jnp.int32, (TQ, WIN), 1) // CHUNK
        band = (kc >= qc) & (kc <= qc + LEFT)
        o_ref[...] = jnp.where(band, rolled[:, 0:WIN], NEG)

    return pl.pallas_call(
        body, grid=(8,),
        in_specs=[pl.BlockSpec((None, 1, ROLL_W), lambda h: (h, 0, 0))],
        out_specs=pl.BlockSpec((None, TQ, WIN), lambda h: (h, 0, 0)),
        out_shape=jax.ShapeDtypeStruct((8, TQ, WIN), F32), name=name, compiler_params=_cparams(),
    )(g_rev.reshape(8, 1, ROLL_W))


def _chk_scores(i, qm, kwin, bias, scale):
    s = _dot_nt(qm, kwin) * scale + bias
    kc = lax.broadcasted_iota(jnp.int32, (TQ, WIN), 1) // CHUNK
    return jnp.where(kc + i * (TQ // CHUNK) >= LEFT, s, NEG)


def _chk_fwd_call(proj, bias, *, name, task=None):
    S = proj.shape[0]
    nq = S // TQ
    scale = HEAD ** -0.5

    def body(q_ref, k_ref, v_ref, b_ref, o_ref, kp_ref, vp_ref):
        i = pl.program_id(1)

        @pl.when(i == 0)
        def _():
            kp_ref[0:PADK, :] = jnp.zeros((PADK, 128), BF16)
            vp_ref[0:PADK, :] = jnp.zeros((PADK, 128), BF16)
            kp_ref[PADK:PADK + S, :] = k_ref[...]
            vp_ref[PADK:PADK + S, :] = v_ref[...]
        masks = _lane_masks()
        q = q_ref[...]
        start = pl.multiple_of(i * TQ, TQ)
        kwin = kp_ref[pl.ds(start, WIN), :]
        vwin = vp_ref[pl.ds(start, WIN), :]
        ss = [_chk_scores(i, jnp.where(masks[h], q, jnp.zeros_like(q)), kwin, b_ref[h], scale) for h in range(2)]
        ps = []
        for s in ss:
            p = jnp.exp(s - jnp.max(s, axis=1, keepdims=True))
            ps.append((p / jnp.sum(p, axis=1, keepdims=True)).astype(BF16))
        o_ref[...] = (_dot(ps[0], jnp.where(masks[0], vwin, jnp.zeros_like(vwin)))
                      + _dot(ps[1], jnp.where(masks[1], vwin, jnp.zeros_like(vwin))))

    c0 = 3 * N_PAIR
    return _pallas(
        body, grid=(N_PAIR, nq), task=task,
        in_specs=[pl.BlockSpec((TQ, 128), lambda p, i: (i, c0 + p)),
                  pl.BlockSpec((S, 128), lambda p, i: (0, c0 + N_PAIR + p)),
                  pl.BlockSpec((S, 128), lambda p, i: (0, c0 + 2 * N_PAIR + p)),
                  pl.BlockSpec((2, TQ, WIN), lambda p, i: (p, 0, 0))],
        out_specs=pl.BlockSpec((TQ, 128), lambda p, i: (i, p)),
        out_shape=jax.ShapeDtypeStruct((S, D_GRP), F32),
        scratch_shapes=[pltpu.VMEM((S + PADK, 128), BF16), pltpu.VMEM((S + PADK, 128), BF16)],
        name=name,
    )(proj, proj, proj, bias)


def _chk_bwd_call(proj, do, bias, *, name, task=None):
    S = proj.shape[0]
    nq = S // TQ
    scale = HEAD ** -0.5

    def body(q_ref, k_ref, v_ref, do_ref, b_ref, dq_ref, dk_ref, dv_ref, dg_ref, kp_ref, vp_ref, dkp_ref, dvp_ref, db_ref):
        i = pl.program_id(1)

        @pl.when(i == 0)
        def _():
            kp_ref[0:PADK, :] = jnp.zeros((PADK, 128), BF16)
            vp_ref[0:PADK, :] = jnp.zeros((PADK, 128), BF16)
            kp_ref[PADK:PADK + S, :] = k_ref[...]
            vp_ref[PADK:PADK + S, :] = v_ref[...]
            dkp_ref[...] = jnp.zeros_like(dkp_ref)
            dvp_ref[...] = jnp.zeros_like(dvp_ref)
            db_ref[...] = jnp.zeros_like(db_ref)
        masks = _lane_masks()
        q = q_ref[...]
        dout = do_ref[...]
        start = pl.multiple_of(i * TQ, TQ)
        kwin = kp_ref[pl.ds(start, WIN), :]
        vwin = vp_ref[pl.ds(start, WIN), :]
        qm = [jnp.where(mk, q, jnp.zeros_like(q)) for mk in masks]
        dom = [jnp.where(mk, dout, jnp.zeros_like(dout)) for mk in masks]
        ss = [_chk_scores(i, qm[h], kwin, b_ref[h], scale) for h in range(2)]
        dps = [_dot_nt(dom[h], vwin) for h in range(2)]
        pbs, dsbs = [], []
        for h in range(2):
            p = jnp.exp(ss[h] - jnp.max(ss[h], axis=1, keepdims=True))
            p = p / jnp.sum(p, axis=1, keepdims=True)
            ds = p * (dps[h] - jnp.sum(p * dps[h], axis=1, keepdims=True))
            db_ref[h] += ds
            pbs.append(p.astype(BF16))
            dsbs.append((ds * scale).astype(BF16))
        dq_ref[...] = (_dot(dsbs[0], jnp.where(masks[0], kwin, jnp.zeros_like(kwin)))
                       + _dot(dsbs[1], jnp.where(masks[1], kwin, jnp.zeros_like(kwin)))).astype(BF16)
        dkp_ref[pl.ds(start, WIN), :] += _dot_tn(dsbs[0], qm[0]) + _dot_tn(dsbs[1], qm[1])
        dvp_ref[pl.ds(start, WIN), :] += _dot_tn(pbs[0], dom[0]) + _dot_tn(pbs[1], dom[1])

        @pl.when(i == nq - 1)
        def _():
            dk_ref[...] = dkp_ref[PADK:PADK + S, :].astype(BF16)
            dv_ref[...] = dvp_ref[PADK:PADK + S, :].astype(BF16)
            a = lax.broadcasted_iota(jnp.int32, (TQ, TQ), 0)
            b = lax.broadcasted_iota(jnp.int32, (TQ, TQ), 1)
            flip = jnp.where(a + b == TQ - 1, 1.0, 0.0).astype(BF16)
            e = lax.broadcasted_iota(jnp.int32, (1, ROLL_W), 1)
            dg_ref[...] = jnp.zeros_like(dg_ref)
            for h in range(2):
                rev = _dot3_l(flip, db_ref[h])
                wide = jnp.concatenate([rev, jnp.zeros((TQ, ROLL_W - WIN), F32)], axis=1)
                diag = pltpu.roll(wide, 0, 1, stride=1, stride_axis=0)
                dg = jnp.sum(diag, axis=0, keepdims=True)
                lo = jnp.sum(jnp.where(e <= 639, dg, 0.0), axis=1, keepdims=True)
                hi = jnp.sum(jnp.where(e >= 895, dg, 0.0), axis=1, keepdims=True)
                dg_ref[h:h + 1, :] = jnp.where(e == 639, lo, jnp.where(e == 895, hi, dg))

    c0 = 3 * N_PAIR
    res = lambda p, i: (0, p)
    return _pallas(
        body, grid=(N_PAIR, nq), task=task,
        in_specs=[pl.BlockSpec((TQ, 128), lambda p, i: (i, c0 + p)),
                  pl.BlockSpec((S, 128), lambda p, i: (0, c0 + N_PAIR + p)),
                  pl.BlockSpec((S, 128), lambda p, i: (0, c0 + 2 * N_PAIR + p)),
                  pl.BlockSpec((TQ, 128), lambda p, i: (i, p)),
                  pl.BlockSpec((2, TQ, WIN), lambda p, i: (p, 0, 0))],
        out_specs=(pl.BlockSpec((TQ, 128), lambda p, i: (i, p)), pl.BlockSpec((S, 128), res),
                   pl.BlockSpec((S, 128), res), pl.BlockSpec((None, 8, ROLL_W), lambda p, i: (p, 0, 0))),
        out_shape=(jax.ShapeDtypeStruct((S, D_GRP), BF16), jax.ShapeDtypeStruct((S, D_GRP), BF16),
                   jax.ShapeDtypeStruct((S, D_GRP), BF16), jax.ShapeDtypeStruct((N_PAIR, 8, ROLL_W), F32)),
        scratch_shapes=[pltpu.VMEM((S + PADK, 128), BF16), pltpu.VMEM((S + PADK, 128), BF16),
                        pltpu.VMEM((S + PADK, 128), F32), pltpu.VMEM((S + PADK, 128), F32),
                        pltpu.VMEM((2, TQ, WIN), F32)],
        name=name,
    )(proj, proj, proj, do, bias)


def _mem_fwd_call(q, k, v, *, name, tq=512):
    S = q.shape[0]
    scale = MEM_HD ** -0.5

    def body(q_ref, k_ref, v_ref, o_ref):
        s = _dot_nt(q_ref[...], k_ref[...]) * scale
        p = jnp.exp(s - jnp.max(s, axis=1, keepdims=True))
        p = p / jnp.sum(p, axis=1, keepdims=True)
        o_ref[...] = _dot(p.astype(BF16), v_ref[...]).astype(BF16)

    return pl.pallas_call(
        body, grid=(MEM_HEADS, S // tq),
        in_specs=[pl.BlockSpec((tq, MEM_HD), lambda h, i: (i, h)),
                  pl.BlockSpec((N_MEM, MEM_HD), lambda h, i: (0, h)),
                  pl.BlockSpec((N_MEM, MEM_HD), lambda h, i: (0, h))],
        out_specs=pl.BlockSpec((tq, MEM_HD), lambda h, i: (i, h)),
        out_shape=jax.ShapeDtypeStruct((S, D), BF16), name=name, compiler_params=_cparams(),
    )(q, k, v)


def _mem_bwd_call(q, k, v, do, *, name, tq=512):
    S = q.shape[0]
    n = S // tq
    scale = MEM_HD ** -0.5

    def body(q_ref, k_ref, v_ref, do_ref, dq_ref, dk_ref, dv_ref, dka_ref, dva_ref):
        i = pl.program_id(1)

        @pl.when(i == 0)
        def _():
            dka_ref[...] = jnp.zeros_like(dka_ref)
            dva_ref[...] = jnp.zeros_like(dva_ref)
        qb = q_ref[...]
        kb = k_ref[...]
        dob = do_ref[...]
        s = _dot_nt(qb, kb) * scale
        p = jnp.exp(s - jnp.max(s, axis=1, keepdims=True))
        p = p / jnp.sum(p, axis=1, keepdims=True)
        dp = _dot_nt(dob, v_ref[...])
        ds = p * (dp - jnp.sum(p * dp, axis=1, keepdims=True))
        dsb = (ds * scale).astype(BF16)
        dq_ref[...] = _dot(dsb, kb).astype(BF16)
        dka_ref[...] += _dot_tn(dsb, qb)
        dva_ref[...] += _dot_tn(p.astype(BF16), dob)

        @pl.when(i == n - 1)
        def _():
            dk_ref[...] = dka_ref[...].astype(BF16)
            dv_ref[...] = dva_ref[...].astype(BF16)

    kv = pl.BlockSpec((N_MEM, MEM_HD), lambda h, i: (0, h))
    qs = pl.BlockSpec((tq, MEM_HD), lambda h, i: (i, h))
    return pl.pallas_call(
        body, grid=(MEM_HEADS, n), in_specs=[qs, kv, kv, qs], out_specs=(qs, kv, kv),
        out_shape=(jax.ShapeDtypeStruct((S, D), BF16), jax.ShapeDtypeStruct((N_MEM, D), BF16),
                   jax.ShapeDtypeStruct((N_MEM, D), BF16)),
        scratch_shapes=[pltpu.VMEM((N_MEM, MEM_HD), F32), pltpu.VMEM((N_MEM, MEM_HD), F32)],
        name=name, compiler_params=_cparams(),
    )(q, k, v, do)


def _rel_table_to_g(rel):
    return jnp.concatenate([
        jnp.broadcast_to(rel[:, N_REL - 1:N_REL], (8, 640)),
        rel[:, 1:N_REL - 1][:, ::-1],
        jnp.broadcast_to(rel[:, 0:1], (8, 129)),
    ], axis=1)


def _g_to_rel_table(dg):
    return dg[:, 639:896][:, ::-1]


def _place():
    x, y, c = lax.axis_index("x"), lax.axis_index("y"), lax.axis_index("c")
    others = [(1 - x, y), (x, 1 - y), (1 - x, 1 - y)]
    return x, y, c, others


def _half(c, rows):
    hr = rows // 2
    return pl.ds(pl.multiple_of(c * hr, 16), hr)


def _dma_sems(*shape):
    return pltpu.SemaphoreType.DMA(shape)


def _cast_slab_call(w, chip_arr, *, name, tm=256, pad_rows=0):
    rows, cols = w.shape
    if pad_rows:
        tm = rows
    tm = min(tm, rows)

    def body(chip_ref, w_ref, o_ref):
        o_ref[0:tm, :] = w_ref[...].astype(BF16)
        if pad_rows:
            o_ref[tm:tm + pad_rows, :] = jnp.zeros((pad_rows, cols), BF16)

    return pl.pallas_call(
        body,
        grid_spec=pltpu.PrefetchScalarGridSpec(
            num_scalar_prefetch=1, grid=(rows // tm,),
            in_specs=[pl.BlockSpec((tm, cols), lambda i, chip: (i, 0))],
            out_specs=pl.BlockSpec((None, tm + pad_rows, cols), lambda i, chip: (chip[0], i, 0))),
        out_shape=jax.ShapeDtypeStruct((N_CHIP, rows + pad_rows, cols), BF16), name=name,
        compiler_params=_cparams(),
    )(chip_arr, w)


def _ag_ici_task(gathered):
    n = len(gathered)

    def copies(ins, outs, sems):
        send_sems, recv_sems = sems
        x, y, c, others = _place()
        me = 2 * x + y
        for k in range(n):
            mine = _half(c, gathered[k].shape[1])
            for t, (ox, oy) in enumerate(others):
                yield [pltpu.make_async_remote_copy(
                    src_ref=ins[k].at[me, mine], dst_ref=outs[k].at[slab, mine],
                    send_sem=send_sems.at[k, t], recv_sem=recv_sems.at[k, t],
                    device_id=(ox, oy, c), device_id_type=MESH) for slab in (me, 2 * ox + oy)]

    def issue(ins, outs, sems):
        for outgoing, _ in copies(ins, outs, sems):
            outgoing.start()

    def drain(ins, outs, sems):
        for outgoing, incoming in copies(ins, outs, sems):
            incoming.wait_recv()
            outgoing.wait_send()

    return _Task(gathered, [jax.ShapeDtypeStruct(g.shape, g.dtype) for g in gathered],
                 [_dma_sems(n, 3), _dma_sems(n, 3)], issue, drain, aliases={k: k for k in range(n)})


def _ag_d2d_task(gathered):
    n = len(gathered)

    def copies(ins, outs, sems):
        send_sems, recv_sems = sems
        x, y, c, others = _place()
        for k in range(n):
            rows = gathered[k].shape[1]
            mine, theirs = _half(c, rows), _half(1 - c, rows)
            for t, (ox, oy) in enumerate(others):
                slab = 2 * ox + oy
                pair = [pltpu.make_async_remote_copy(
                    src_ref=ins[k].at[slab, half], dst_ref=outs[k].at[slab, half],
                    send_sem=send_sems.at[k, t], recv_sem=recv_sems.at[k, t],
                    device_id=(x, y, 1 - c), device_id_type=MESH) for half in (mine, theirs)]
                yield pair

    def issue(ins, outs, sems):
        for outgoing, _ in copies(ins, outs, sems):
            outgoing.start()

    def drain(ins, outs, sems):
        for outgoing, incoming in copies(ins, outs, sems):
            incoming.wait_recv()
            outgoing.wait_send()

    return _Task(gathered, [jax.ShapeDtypeStruct(g.shape, g.dtype) for g in gathered],
                 [_dma_sems(n, 3), _dma_sems(n, 3)], issue, drain, aliases={k: k for k in range(n)})


def _rs_pair_task(ds):
    n = len(ds)

    def copies(ins, outs, sems):
        send_sems, recv_sems = sems
        x, y, c, _ = _place()
        for k in range(n):
            yield pltpu.make_async_remote_copy(
                src_ref=ins[k].at[:, _half(1 - c, ds[k].shape[1])], dst_ref=outs[k],
                send_sem=send_sems.at[k], recv_sem=recv_sems.at[k],
                device_id=(x, y, 1 - c), device_id_type=MESH)

    def issue(ins, outs, sems):
        for cp in copies(ins, outs, sems):
            cp.start()

    def drain(ins, outs, sems):
        for cp in copies(ins, outs, sems):
            cp.wait()

    return _Task(ds, [jax.ShapeDtypeStruct((N_CHIP, d.shape[1] // 2, d.shape[2]), d.dtype) for d in ds],
                 [_dma_sems(n), _dma_sems(n)], issue, drain)


def _pair_add_call(d, r1, c_arr, *, name, tm=256):
    _, rows, cols = d.shape
    hr = rows // 2
    tm = tm if hr % tm == 0 else hr
    nb = hr // tm

    def body(c_ref, d_ref, r_ref, o_ref):
        o_ref[...] = (d_ref[...].astype(F32) + r_ref[...].astype(F32)).astype(BF16)

    return pl.pallas_call(
        body,
        grid_spec=pltpu.PrefetchScalarGridSpec(
            num_scalar_prefetch=1, grid=(N_CHIP, nb),
            in_specs=[pl.BlockSpec((None, tm, cols), lambda j, i, c: (j, c[0] * nb + i, 0)),
                      pl.BlockSpec((None, tm, cols), lambda j, i, c: (j, i, 0))],
            out_specs=pl.BlockSpec((None, tm, cols), lambda j, i, c: (j, i, 0))),
        out_shape=jax.ShapeDtypeStruct((N_CHIP, hr, cols), BF16), name=name, compiler_params=_cparams(),
    )(c_arr, d, r1)


def _rs_chip_task(ps):
    n = len(ps)

    def copies(ins, outs, sems):
        send_sems, recv_sems = sems
        x, y, c, others = _place()
        for k in range(n):
            for t, (ox, oy) in enumerate(others):
                yield pltpu.make_async_remote_copy(
                    src_ref=ins[k].at[2 * ox + oy], dst_ref=outs[k].at[t],
                    send_sem=send_sems.at[k, t], recv_sem=recv_sems.at[k, t],
                    device_id=(ox, oy, c), device_id_type=MESH)

    def issue(ins, outs, sems):
        for cp in copies(ins, outs, sems):
            cp.start()

    def drain(ins, outs, sems):
        for cp in copies(ins, outs, sems):
            cp.wait()

    return _Task(ps, [jax.ShapeDtypeStruct((3,) + p.shape[1:], p.dtype) for p in ps],
                 [_dma_sems(n, 3), _dma_sems(n, 3)], issue, drain)


def _chip_sum_call(p, r2, place_arr, *, name, tm=256):
    _, hr, cols = r2.shape
    tm = tm if hr % tm == 0 else hr
    nb = hr // tm

    def body(place_ref, p_ref, r_ref, o_ref):
        acc = p_ref[...].astype(F32)
        for j in range(3):
            acc = acc + r_ref[j].astype(F32)
        o_ref[...] = acc

    return pl.pallas_call(
        body,
        grid_spec=pltpu.PrefetchScalarGridSpec(
            num_scalar_prefetch=1, grid=(nb,),
            in_specs=[pl.BlockSpec((None, tm, cols), lambda i, pc: (pc[0], i, 0)),
                      pl.BlockSpec((3, tm, cols), lambda i, pc: (0, i, 0))],
            out_specs=pl.BlockSpec((tm, cols), lambda i, pc: (pc[1] * nb + i, 0))),
        out_shape=jax.ShapeDtypeStruct((2 * hr, cols), F32), name=name, compiler_params=_cparams(),
    )(place_arr, p, r2)


def _rs_gather_task(gs):
    n = len(gs)

    def copies(ins, outs, sems):
        send_sems, recv_sems = sems
        x, y, c, _ = _place()
        for k in range(n):
            rows = gs[k].shape[0]
            mine, theirs = _half(c, rows), _half(1 - c, rows)
            yield [pltpu.make_async_remote_copy(
                src_ref=ins[k].at[mine], dst_ref=outs[k].at[half],
                send_sem=send_sems.at[k], recv_sem=recv_sems.at[k],
                device_id=(x, y, 1 - c), device_id_type=MESH) for half in (mine, theirs)]

    def issue(ins, outs, sems):
        for outgoing, _ in copies(ins, outs, sems):
            outgoing.start()

    def drain(ins, outs, sems):
        for outgoing, incoming in copies(ins, outs, sems):
            incoming.wait_recv()
            outgoing.wait_send()

    return _Task(gs, [jax.ShapeDtypeStruct(g.shape, g.dtype) for g in gs],
                 [_dma_sems(n), _dma_sems(n)], issue, drain, aliases={k: k for k in range(n)})


def _adamw(w, g, m, v):
    m = ADAM_B1 * m + (1.0 - ADAM_B1) * g
    v = ADAM_B2 * v + (1.0 - ADAM_B2) * jnp.square(g)
    m_hat = m / (1.0 - ADAM_B1 ** ADAM_STEP)
    v_hat = v / (1.0 - ADAM_B2 ** ADAM_STEP)
    delta = -ADAM_LR * (m_hat / (jnp.sqrt(v_hat) + ADAM_EPS) + ADAM_WD * w)
    return delta, m, v


def _adamw_call(items, *, name, tm=256, task=None):
    n = len(items)
    cols = items[0][0].shape[1]
    tiles = [it[0].shape[0] // tm for it in items]
    steps = max(tiles)

    def body(*refs):
        i = pl.program_id(0)
        ins, outs = refs[:4 * n], refs[4 * n:]
        for k in range(n):
            def update(k=k):
                g = ins[4 * k + 1][...]
                res = _adamw(ins[4 * k][...], g, ins[4 * k + 2][...], ins[4 * k + 3][...])
                outs[4 * k][...] = g
                for j in range(3):
                    outs[4 * k + 1 + j][...] = res[j]
            if tiles[k] == steps:
                update()
            else:
                pl.when(i < tiles[k])(update)

    in_specs, out_specs, out_shape, args = [], [], [], []
    for it, t in zip(items, tiles):
        spec = pl.BlockSpec((tm, cols), lambda i, t=t: (jnp.minimum(i, t - 1), 0))
        in_specs += [spec] * 4
        out_specs += [spec] * 4
        out_shape += [jax.ShapeDtypeStruct(it[0].shape, F32)] * 4
        args += list(it)
    res = _pallas(body, grid=(steps,), in_specs=in_specs, out_specs=out_specs, out_shape=out_shape,
                  name=name, task=task)(*args)
    outs, extra = res if task is not None else (res, None)
    grouped = [tuple(outs[4 * k:4 * k + 4]) for k in range(n)]
    return (grouped, extra) if task is not None else grouped


def _adamw_cols_call(w, g_pad, m, v, *, name, tn=256):
    rows, cols = w.shape

    def body(w_ref, g_ref, m_ref, v_ref, go_ref, d_ref, mo_ref, vo_ref):
        g = g_ref[0:rows, :]
        d, mn, vn = _adamw(w_ref[...], g, m_ref[...], v_ref[...])
        go_ref[...] = g
        d_ref[...] = d
        mo_ref[...] = mn
        vo_ref[...] = vn

    spec = pl.BlockSpec((rows, tn), lambda j: (0, j))
    gspec = pl.BlockSpec((g_pad.shape[0], tn), lambda j: (0, j))
    return _pallas(body, grid=(cols // tn,), in_specs=[spec, gspec, spec, spec], out_specs=(spec,) * 4,
                   out_shape=(jax.ShapeDtypeStruct((rows, cols), F32),) * 4, name=name)(w, g_pad, m, v)


N_DEV = 8
SMALL_ROWS = 24
SMALL_LAYOUT = {
    "g_mix_pre": (0, 0, 1, D), "g_mix_post": (1, 0, 1, D), "g_mem_kv": (2, 0, 1, D), "g_mem_pre": (3, 0, 1, D),
    "g_mem_post": (4, 0, 1, D), "g_ff_pre": (5, 0, 1, D), "g_ff_post": (6, 0, 1, D),
    "g_fox_out": (7, 0, 1, D_GRP), "g_chk_out": (7, D_GRP, 1, D_GRP), "b_fgt": (8, 0, 1, 8),
    "rel_bias": (16, 0, 8, N_REL),
}
SMALL = list(SMALL_LAYOUT)


LOSS_ROW = 9


def _small_call(grads, ws, ms, vs, loss_blk, task, *, name):
    n = len(SMALL)
    t_in, t_out = len(task.arrays), len(task.out_shapes)

    def body(*refs):
        g_refs, w_refs, m_refs, v_refs = (refs[j * n:(j + 1) * n] for j in range(4))
        p = 4 * n
        loss_ref, tins = refs[p], refs[p + 1:p + 1 + t_in]
        p += 1 + t_in
        outs, loss_out, touts = refs[p:p + 4 * n], refs[p + 4 * n], refs[p + 4 * n + 1:p + 4 * n + 1 + t_out]
        p += 4 * n + 1 + t_out
        mine, slots, send_sems, recv_sems = refs[p:p + 4]
        tsems = refs[p + 4:]
        task.issue(tins, touts, tsems)
        x, y, c, _ = _place()
        me = 4 * x + 2 * y + c
        mine[...] = jnp.zeros_like(mine)
        for k, name_k in enumerate(SMALL):
            r, l, nr, nl = SMALL_LAYOUT[name_k]
            mine[r:r + nr, l:l + nl] = g_refs[k][0:nr, 0:nl]
        mine[LOSS_ROW:LOSS_ROW + 1, 0:128] = loss_ref[0:1, :]
        slots[me] = mine[...]
        peers = [(dx, dy, dc) for dx in (0, 1) for dy in (0, 1) for dc in (0, 1)][1:]
        cps = []
        for t, (dx, dy, dc) in enumerate(peers):
            px, py, pc = (x + dx) % 2, (y + dy) % 2, (c + dc) % 2
            cps.append(pltpu.make_async_remote_copy(
                src_ref=mine, dst_ref=slots.at[me], send_sem=send_sems.at[t], recv_sem=recv_sems.at[t],
                device_id=(px, py, pc), device_id_type=MESH))
            cps[-1].start()
        for t, (dx, dy, dc) in enumerate(peers):
            px, py, pc = (x + dx) % 2, (y + dy) % 2, (c + dc) % 2
            pltpu.make_async_remote_copy(
                src_ref=mine, dst_ref=slots.at[4 * px + 2 * py + pc], send_sem=send_sems.at[t],
                recv_sem=recv_sems.at[t], device_id=(px, py, pc), device_id_type=MESH).wait_recv()
        for cp in cps:
            cp.wait_send()
        total = slots[0]
        for j in range(1, N_DEV):
            total = total + slots[j]
        for k, name_k in enumerate(SMALL):
            r, l, nr, nl = SMALL_LAYOUT[name_k]
            g = total[r:r + nr, l:l + nl]
            d, mn, vn = _adamw(w_refs[k][...], g, m_refs[k][...], v_refs[k][...])
            for j, val in enumerate((g, d, mn, vn)):
                outs[4 * k + j][...] = val
        loss_out[...] = jnp.broadcast_to(total[LOSS_ROW:LOSS_ROW + 1, 0:128], loss_out.shape)
        task.drain(tins, touts, tsems)

    vm = pl.BlockSpec(memory_space=pltpu.VMEM)
    out_shape = [jax.ShapeDtypeStruct(ws[k].shape, F32) for k in SMALL for _ in range(4)]
    out_shape += [jax.ShapeDtypeStruct((8, 128), F32)] + list(task.out_shapes)
    res = pl.pallas_call(
        body, in_specs=[vm] * (4 * n + 1) + [ANY] * t_in, out_specs=[vm] * (4 * n + 1) + [ANY] * t_out,
        out_shape=out_shape,
        scratch_shapes=[pltpu.VMEM((SMALL_ROWS, D), F32), pltpu.VMEM((N_DEV, SMALL_ROWS, D), F32),
                        _dma_sems(N_DEV - 1), _dma_sems(N_DEV - 1)] + list(task.sems),
        input_output_aliases={4 * n + 1 + i: 4 * n + 1 + j for i, j in task.aliases.items()},
        name=name,
    )(*[d[k] for d in (grads, ws, ms, vs) for k in SMALL], loss_blk, *task.arrays)
    return ({k: tuple(res[4 * i:4 * i + 4]) for i, k in enumerate(SMALL)}, res[4 * n], list(res[4 * n + 1:]))


WEIGHTS = ["w_in", "b_fgt", "rel_bias", "g_fox_out", "g_chk_out", "w_out", "g_mix_pre", "g_mix_post", "g_mem_kv",
           "w_mq", "w_mk", "w_mv", "w_mo", "g_mem_pre", "g_mem_post", "w_ff1", "w_ff2", "g_ff_pre", "g_ff_post"]
BIG = ["w_in", "w_out", "w_mq", "w_mk", "w_mv", "w_mo", "w_ff1", "w_ff2"]


IN_SHARD = D_IN // N_CHIP
IN_PAD = 800


IN_PIECES = [(0, 0, 770), (800, 770, 766), (1566, 3072, 4), (1600, 3076, 4), (1604, 1536, 766), (2400, 2302, 770)]
PAD_ZEROS = [(800 * j + IN_SHARD, IN_PAD - IN_SHARD) for j in range(N_CHIP)]
ALL_ZEROS = [(D_IN, D_ALL - D_IN)]


def _reorder_rows_call(src, to_all, *, name, tn=256):
    rows, cols = src.shape
    zeros = ALL_ZEROS if to_all else PAD_ZEROS

    def body(s_ref, o_ref):
        for pad0, all0, cnt in IN_PIECES:
            s0, d0 = (pad0, all0) if to_all else (all0, pad0)
            o_ref[d0:d0 + cnt, :] = s_ref[s0:s0 + cnt, :]
        for z0, cnt in zeros:
            o_ref[z0:z0 + cnt, :] = jnp.zeros((cnt, tn), src.dtype)

    spec = pl.BlockSpec((rows, tn), lambda j: (0, j))
    return _pallas(body, grid=(cols // tn,), in_specs=[spec], out_specs=spec,
                   out_shape=jax.ShapeDtypeStruct((rows, cols), src.dtype), name=name)(src)


def kernel(x, mem, w_in, b_fgt, rel_bias, g_fox_out, g_chk_out, w_out, g_mix_pre, g_mix_post, g_mem_kv, w_mq, w_mk, w_mv, w_mo, g_mem_pre, g_mem_post, w_ff1, w_ff2, g_ff_pre, g_ff_post, loss_target, m_w_in, m_b_fgt, m_rel_bias, m_g_fox_out, m_g_chk_out, m_w_out, m_g_mix_pre, m_g_mix_post, m_g_mem_kv, m_w_mq, m_w_mk, m_w_mv, m_w_mo, m_g_mem_pre, m_g_mem_post, m_w_ff1, m_w_ff2, m_g_ff_pre, m_g_ff_post, v_w_in, v_b_fgt, v_rel_bias, v_g_fox_out, v_g_chk_out, v_w_out, v_g_mix_pre, v_g_mix_post, v_g_mem_kv, v_w_mq, v_w_mk, v_w_mv, v_w_mo, v_g_mem_pre, v_g_mem_post, v_w_ff1, v_w_ff2, v_g_ff_pre, v_g_ff_post):
    w = dict(w_in=w_in, b_fgt=b_fgt, rel_bias=rel_bias, g_fox_out=g_fox_out, g_chk_out=g_chk_out, w_out=w_out,
             g_mix_pre=g_mix_pre, g_mix_post=g_mix_post, g_mem_kv=g_mem_kv, w_mq=w_mq, w_mk=w_mk, w_mv=w_mv,
             w_mo=w_mo, g_mem_pre=g_mem_pre, g_mem_post=g_mem_post, w_ff1=w_ff1, w_ff2=w_ff2, g_ff_pre=g_ff_pre,
             g_ff_post=g_ff_post)
    m = dict(w_in=m_w_in, b_fgt=m_b_fgt, rel_bias=m_rel_bias, g_fox_out=m_g_fox_out, g_chk_out=m_g_chk_out,
             w_out=m_w_out, g_mix_pre=m_g_mix_pre, g_mix_post=m_g_mix_post, g_mem_kv=m_g_mem_kv, w_mq=m_w_mq,
             w_mk=m_w_mk, w_mv=m_w_mv, w_mo=m_w_mo, g_mem_pre=m_g_mem_pre, g_mem_post=m_g_mem_post,
             w_ff1=m_w_ff1, w_ff2=m_w_ff2, g_ff_pre=m_g_ff_pre, g_ff_post=m_g_ff_post)
    v = dict(w_in=v_w_in, b_fgt=v_b_fgt, rel_bias=v_rel_bias, g_fox_out=v_g_fox_out, g_chk_out=v_g_chk_out,
             w_out=v_w_out, g_mix_pre=v_g_mix_pre, g_mix_post=v_g_mix_post, g_mem_kv=v_g_mem_kv, w_mq=v_w_mq,
             w_mk=v_w_mk, w_mv=v_w_mv, w_mo=v_w_mo, g_mem_pre=v_g_mem_pre, g_mem_post=v_g_mem_post,
             w_ff1=v_w_ff1, w_ff2=v_w_ff2, g_ff_pre=v_g_ff_pre, g_ff_post=v_g_ff_post)

    def rows(d, k):
        return d[k][0] if k == "rel_bias" else d[k]

    xs, mems, target = x[0], mem[0], loss_target[0]
    S = xs.shape[0]
    sp = {k: rows(w, k) for k in SMALL}
    b_pad = jnp.pad(sp["b_fgt"], ((0, 0), (0, 120)))
    chip = 2 * lax.axis_index("x") + lax.axis_index("y")
    chip_arr = jnp.reshape(chip, (1,)).astype(jnp.int32)
    c_arr = jnp.reshape(lax.axis_index("c"), (1,)).astype(jnp.int32)
    place_arr = jnp.concatenate([chip_arr, c_arr])
    w_in_t, m_in_t, v_in_t = w["w_in"][0].T, m["w_in"][0].T, v["w_in"][0].T
    slab = {k: _cast_slab_call(w[k][0], chip_arr, name="cast_" + k) for k in BIG[1:]}
    slab["w_in"] = _cast_slab_call(w_in_t, chip_arr, name="cast_w_in", pad_rows=IN_PAD - IN_SHARD)

    def gather_ici(names):
        return _ag_ici_task([slab[k] for k in names])

    def pair_add(k, d, r1):
        return _pair_add_call(d, r1, c_arr, name="rs_pair_add_" + k)

    g_in, = _comm_call(gather_ici(["w_in"]), name="ag_w_in")
    h1, (g_in,) = _rms_fwd_call(xs, sp["g_mix_pre"], name="rms_mix_pre", task=_ag_d2d_task([g_in]))
    w_all_t = _reorder_rows_call(g_in.reshape(N_CHIP * IN_PAD, D), True, name="w_in_rows")
    proj, (g_out, g_mq) = _mm_nt(h1, w_all_t, "plain", rows=(0, 3072), name="mm_proj",
                                 task=gather_ici(["w_out", "w_mq"]))
    fl_raw = _mm_nt(h1, w_all_t, "plain", rows=(3072, 128), name="mm_gate", out_dtype=F32, tn=128)
    c_rep, c_t = _fox_prep_call(fl_raw, b_pad, name="fox_prep")
    bias = _chk_bias_call(_rel_table_to_g(sp["rel_bias"]), name="chk_bias")
    mid = ["w_mk", "w_mv", "w_mo", "w_ff1"]
    (yf, lse), got = _fox_fwd_call(proj, c_rep, c_t, name="fox_fwd",
                                   task=_merge_tasks([gather_ici(mid), _ag_d2d_task([g_out, g_mq])]))
    g_mid, (g_out, g_mq) = got[:4], got[4:]
    yc, got = _chk_fwd_call(proj, bias, name="chk_fwd",
                            task=_merge_tasks([gather_ici(["w_ff2"]), _ag_d2d_task(g_mid)]))
    g_ff2, (g_mk, g_mv, g_mo, g_ff1) = got[0], got[1:]
    yn = _mix_norm_fwd_call(yf, yc, sp["g_fox_out"], sp["g_chk_out"], name="mix_norm_fwd")
    z, (g_ff2,) = _mm_nn(yn, g_out, "rows", name="mm_out", out_dtype=F32, task=_ag_d2d_task([g_ff2]))
    x1, h2 = _post_pre_call(xs, z, sp["g_mix_post"], sp["g_mem_pre"], name="post_mix")
    memn = _rms_fwd_call(mems, sp["g_mem_kv"], name="rms_mem_kv")
    q2 = _mm_nn(h2, g_mq, "rows", name="mm_mq")
    k2 = _mm_nn(memn, g_mk, "rows", name="mm_mk")
    v2 = _mm_nn(memn, g_mv, "rows", name="mm_mv")
    o2 = _mem_fwd_call(q2, k2, v2, name="mem_fwd")
    y2 = _mm_nn(o2, g_mo, "rows", name="mm_mo", out_dtype=F32)
    x2, h3 = _post_pre_call(x1, y2, sp["g_mem_post"], sp["g_ff_pre"], name="post_mem")
    act, relu = _mm_nn(h3, g_ff1, "cols", name="mm_ff1", epi="relu2")
    y3 = _mm_nn(act, g_ff2, "rows", name="mm_ff2", out_dtype=F32, tm=1024)
    loss_blk, dx3, dy3, dg_ff_post = _final_call(x2, y3, sp["g_ff_post"], target, name="final")

    d_ff2 = _mm_tn(act, dy3, name="mm_dff2", tk=512, tn=1024).reshape(N_CHIP, D_FF // N_CHIP, D)
    du, (r1,) = _mm_nt(dy3, g_ff2, "rows", name="mm_du", mul2r=relu, task=_rs_pair_task([d_ff2]))
    p_ff2 = pair_add("w_ff2", d_ff2, r1)
    d_ff1 = _mm_tn(h3, du, name="mm_dff1", cols4=True)
    dh3, (r1,) = _mm_nt(du, g_ff1, "cols", name="mm_dh3", out_dtype=F32, tm=1024, task=_rs_pair_task([d_ff1]))
    p_ff1 = pair_add("w_ff1", d_ff1, r1)
    dx2, dy2, dg_ff_pre, dg_mem_post = _bwd_mid_call(dx3, x2, dh3, sp["g_ff_pre"], y2, sp["g_mem_post"], name="bwd_ff")
    d_mo = _mm_tn(o2, dy2, name="mm_dmo").reshape(N_CHIP, D // N_CHIP, D)
    do2 = _mm_nt(dy2, g_mo, "rows", name="mm_do2")
    dq2, dk2, dv2 = _mem_bwd_call(q2, k2, v2, do2, name="mem_bwd")
    d_mq = _mm_tn(h2, dq2, name="mm_dmq").reshape(N_CHIP, D // N_CHIP, D)
    dh2 = _mm_nt(dq2, g_mq, "rows", name="mm_dh2", out_dtype=F32)
    d_mk = _mm_tn(memn, dk2, name="mm_dmk").reshape(N_CHIP, D // N_CHIP, D)
    d_mv = _mm_tn(memn, dv2, name="mm_dmv").reshape(N_CHIP, D // N_CHIP, D)
    dmn_k = _mm_nt(dk2, g_mk, "rows", name="mm_dmemk", out_dtype=F32)
    dmn_v = _mm_nt(dv2, g_mv, "rows", name="mm_dmemv", out_dtype=F32)
    dg_mem_kv = _gain_grad_call(mems, sp["g_mem_kv"], dmn_k, dmn_v, name="gain_mem_kv")
    dx1, dz, dg_mem_pre, dg_mix_post = _bwd_mid_call(dx2, x1, dh2, sp["g_mem_pre"], z, sp["g_mix_post"], name="bwd_mem")
    d_out = _mm_tn(yn, dz, name="mm_dout").reshape(N_CHIP, D // N_CHIP, D)
    late = ["w_mo", "w_mq", "w_mk", "w_mv", "w_out"]
    d_late = [d_mo, d_mq, d_mk, d_mv, d_out]
    dyn, r1_late = _mm_nt(dz, g_out, "rows", name="mm_dyn", out_dtype=F32, task=_rs_pair_task(d_late))
    p_late = [pair_add(k, d, r1) for k, d, r1 in zip(late, d_late, r1_late)]
    dof, doc, delta, dg_fox, dg_chk = _mix_norm_bwd_call(dyn, yf, yc, sp["g_fox_out"], sp["g_chk_out"], name="mix_norm_bwd")
    (dqf, dkf, dvf, dcq, dck), r2_ff = _fox_bwd_call(proj, dof, lse, delta, c_rep, c_t, name="fox_bwd",
                                                      task=_rs_chip_task([p_ff2, p_ff1]))
    (dqc, dkc, dvc, dgrev), r2_late = _chk_bwd_call(proj, doc, bias, name="chk_bwd", task=_rs_chip_task(p_late))
    first = ["w_ff2", "w_ff1"] + late
    f_first = [_chip_sum_call(p, r, place_arr, name="rs_chip_sum_" + k)
               for k, p, r in zip(first, [p_ff2, p_ff1] + p_late, r2_ff + r2_late)]
    dc8 = dcq[:, :, 0:2, :].transpose(0, 2, 1, 3).reshape(8, S) + dck[:, ::HEAD].T
    dc_rows = jnp.concatenate([dc8, jnp.zeros((120, S), F32)], axis=0)
    dfl, db_fgt = _fox_gate_bwd_call(dc_rows, fl_raw, b_pad, name="fox_gate_bwd")
    dproj = jnp.concatenate([dqf, dkf, dvf, dqc, dkc, dvc, dfl], axis=1)
    d_all_t, g_first = _mm_tn(dproj, h1, name="mm_dwin", tk=640, tn=1024, task=_rs_gather_task(f_first))
    grads = dict(zip(first, g_first))
    d_in = _reorder_rows_call(d_all_t, False, name="d_in_rows").reshape(N_CHIP, IN_PAD, D)
    delta_w, new_m, new_v = {}, {}, {}

    def adamw_items(names):
        return [(w[k][0], grads[k], m[k][0], v[k][0]) for k in names]

    upd_late, (r1,) = _adamw_call(adamw_items(late), name="adamw_late", tm=64, task=_rs_pair_task([d_in]))
    p_in = pair_add("w_in", d_in, r1)
    dh1, (r2_in,) = _mm_nn(dproj, w_all_t, "plain", name="mm_dh1", out_dtype=F32, tm=1024,
                           task=_rs_chip_task([p_in]))
    f_in = _chip_sum_call(p_in, r2_in, place_arr, name="rs_chip_sum_w_in")
    upd_ff = _adamw_call(adamw_items(first[:2]), name="adamw_ff")
    for k, res in zip(late + first[:2], upd_late + upd_ff):
        grads[k], delta_w[k], new_m[k], new_v[k] = res
    grad_x, dg_mix_pre = _bwd_last_call(dx1, xs, dh1, sp["g_mix_pre"], name="bwd_mix")

    small_g = {"g_mix_pre": dg_mix_pre, "g_mix_post": dg_mix_post, "g_mem_kv": dg_mem_kv, "g_mem_pre": dg_mem_pre,
               "g_mem_post": dg_mem_post, "g_ff_pre": dg_ff_pre, "g_ff_post": dg_ff_post, "g_fox_out": dg_fox,
               "g_chk_out": dg_chk, "b_fgt": db_fgt,
               "rel_bias": _g_to_rel_table(dgrev[:, 0:2, :].reshape(8, ROLL_W))}
    small, loss_out, (g_w_in,) = _small_call(
        small_g, sp, {k: rows(m, k) for k in SMALL}, {k: rows(v, k) for k in SMALL}, loss_blk,
        _rs_gather_task([f_in]), name="small_allreduce_adamw")
    loss = loss_out[0, 0]
    res = _adamw_cols_call(w_in_t, g_w_in, m_in_t, v_in_t, name="adamw_w_in")
    grads["w_in"], delta_w["w_in"], new_m["w_in"], new_v["w_in"] = (a.T for a in res)
    for k in SMALL:
        vals = small[k]
        if k == "rel_bias":
            vals = tuple(a[None] for a in vals)
        grads[k], delta_w[k], new_m[k], new_v[k] = vals

    def out(d, k):
        return d[k][None] if k in BIG else d[k]

    return (loss, grad_x[None], *[out(grads, k) for k in WEIGHTS], *[out(delta_w, k) for k in WEIGHTS],
            *[out(new_m, k) for k in WEIGHTS], *[out(new_v, k) for k in WEIGHTS])
```

```python
import functools

import jax
import jax.numpy as jnp
from jax import lax
from jax.experimental import pallas as pl
from jax.experimental.pallas import tpu as pltpu

F32 = jnp.float32
BF16 = jnp.bfloat16

D = 1024
HEAD = 64
N_PAIR = 4
D_GRP = 512
CHUNK = 64
LEFT = 8
MAX_REL = 128
N_REL = 2 * MAX_REL + 1
N_MEM = 256
MEM_HEADS = 4
MEM_HD = 256
D_FF = 4096
D_IN = 3080
D_ALL = 3200
EPS = 1e-6
TQ = 256
WIN = (LEFT + TQ // CHUNK) * CHUNK
PADK = LEFT * CHUNK
ROLL_W = 1024
NEG = -1e30
N_CHIP = 4
VMEM_LIMIT = 48 * 1024 * 1024

ADAM_LR = 0.001
ADAM_B1 = 0.9
ADAM_B2 = 0.999
ADAM_EPS = 1e-08
ADAM_WD = 0.01
ADAM_STEP = 10

MESH = pl.DeviceIdType.MESH


def _cparams():
    return pltpu.CompilerParams(vmem_limit_bytes=VMEM_LIMIT)


ANY = pl.BlockSpec(memory_space=pl.ANY)


class _Task:
    def __init__(self, arrays, out_shapes, sems, issue, drain, aliases=None):
        self.arrays, self.out_shapes, self.sems = list(arrays), list(out_shapes), list(sems)
        self.issue, self.drain, self.aliases = issue, drain, dict(aliases or {})


def _merge_tasks(tasks):
    tasks = [t for t in tasks if t is not None]
    if len(tasks) == 1:
        return tasks[0]
    cuts, a, o, s = [], 0, 0, 0
    aliases = {}
    for t in tasks:
        cuts.append((a, o, s))
        aliases.update({a + i: o + j for i, j in t.aliases.items()})
        a, o, s = a + len(t.arrays), o + len(t.out_shapes), s + len(t.sems)

    def part(fn_name):
        def run(ins, outs, sems):
            for t, (a0, o0, s0) in zip(tasks, cuts):
                getattr(t, fn_name)(ins[a0:a0 + len(t.arrays)], outs[o0:o0 + len(t.out_shapes)],
                                    sems[s0:s0 + len(t.sems)])
        return run

    return _Task([x for t in tasks for x in t.arrays], [x for t in tasks for x in t.out_shapes],
                 [x for t in tasks for x in t.sems], part("issue"), part("drain"), aliases)


def _pallas(body, *, grid, in_specs, out_specs, out_shape, name, scratch_shapes=(), task=None, prefetch=0):
    def make(kernel, i_specs, o_specs, o_shape, scratch, aliases):
        if prefetch:
            spec = pltpu.PrefetchScalarGridSpec(num_scalar_prefetch=prefetch, grid=grid, in_specs=i_specs,
                                                out_specs=o_specs, scratch_shapes=scratch)
            return pl.pallas_call(kernel, grid_spec=spec, out_shape=o_shape, input_output_aliases=aliases,
                                  name=name, compiler_params=_cparams())
        return pl.pallas_call(kernel, grid=grid, in_specs=i_specs, out_specs=o_specs, out_shape=o_shape,
                              scratch_shapes=scratch, input_output_aliases=aliases, name=name,
                              compiler_params=_cparams())

    if task is None:
        return make(body, list(in_specs), out_specs, out_shape, list(scratch_shapes), {})
    single = not isinstance(out_shape, (tuple, list))
    o_shapes = [out_shape] if single else list(out_shape)
    o_specs = [out_specs] if single else list(out_specs)
    n_in, n_out, n_scr = len(in_specs), len(o_shapes), len(scratch_shapes)
    t_in, t_out = len(task.arrays), len(task.out_shapes)

    def carried(*refs):
        cut = [prefetch, n_in, t_in, n_out, t_out, n_scr]
        parts, p = [], 0
        for c in cut:
            parts.append(refs[p:p + c])
            p += c
        scalars, ins, tins, outs, touts, scr = parts
        tsems = refs[p:]
        ids = [pl.program_id(a) for a in range(len(grid))]
        first = functools.reduce(jnp.logical_and, [i == 0 for i in ids])
        last = functools.reduce(jnp.logical_and, [i == g - 1 for i, g in zip(ids, grid)])

        @pl.when(first)
        def _():
            task.issue(tins, touts, tsems)
        body(*scalars, *ins, *outs, *scr)

        @pl.when(last)
        def _():
            task.drain(tins, touts, tsems)

    call = make(carried, list(in_specs) + [ANY] * t_in, o_specs + [ANY] * t_out,
                o_shapes + list(task.out_shapes), list(scratch_shapes) + list(task.sems),
                {prefetch + n_in + i: n_out + j for i, j in task.aliases.items()})

    def run(*args):
        res = call(*args, *task.arrays)
        outs = res[:n_out]
        return (outs[0] if single else tuple(outs)), list(res[n_out:])

    return run


def _dot(a, b):
    return jnp.dot(a, b, preferred_element_type=F32)


def _dot_nt(a, b):
    return lax.dot_general(a, b, (((1,), (1,)), ((), ())), preferred_element_type=F32)


def _dot_tn(a, b):
    return lax.dot_general(a, b, (((0,), (0,)), ((), ())), preferred_element_type=F32)


def _split3(x):
    hi = x.astype(BF16)
    r1 = x - hi.astype(F32)
    mid = r1.astype(BF16)
    lo = (r1 - mid.astype(F32)).astype(BF16)
    return hi, mid, lo


def _dot3(x, m01):
    hi, mid, lo = _split3(x)
    return _dot(hi, m01) + _dot(mid, m01) + _dot(lo, m01)


def _dot3_l(m01, x):
    hi, mid, lo = _split3(x)
    return _dot(m01, hi) + _dot(m01, mid) + _dot(m01, lo)


def _mm_nn(a, b, kind, *, name, out_dtype=BF16, tm=2048, tn=512, epi=None, task=None):
    M, K = a.shape
    if kind == "plain":
        N = b.shape[1]
        b_spec = pl.BlockSpec((K, tn), lambda m, n: (0, n))
    elif kind == "rows":
        N = b.shape[2]
        b_spec = pl.BlockSpec((N_CHIP, K // N_CHIP, tn), lambda m, n: (0, 0, n))
    else:
        nq = b.shape[2]
        N = N_CHIP * nq
        per = nq // tn
        b_spec = pl.BlockSpec((None, K, tn), lambda m, n: (n // per, 0, n % per))
    tm = min(tm, M)
    kq = K // N_CHIP

    def body(a_ref, b_ref, *o_refs):
        if kind == "rows":
            acc = _dot(a_ref[:, 0:kq], b_ref[0])
            for j in range(1, N_CHIP):
                acc += _dot(a_ref[:, j * kq:(j + 1) * kq], b_ref[j])
        else:
            acc = _dot(a_ref[...], b_ref[...])
        if epi == "relu2":
            r = jnp.maximum(acc, 0.0)
            o_refs[0][...] = (r * r).astype(BF16)
            o_refs[1][...] = r.astype(BF16)
        else:
            o_refs[0][...] = acc.astype(out_dtype)

    o_spec = pl.BlockSpec((tm, tn), lambda m, n: (m, n))
    if epi == "relu2":
        out_shape = (jax.ShapeDtypeStruct((M, N), BF16), jax.ShapeDtypeStruct((M, N), BF16))
        out_specs = (o_spec, o_spec)
    else:
        out_shape = jax.ShapeDtypeStruct((M, N), out_dtype)
        out_specs = o_spec
    return _pallas(
        body, grid=(M // tm, N // tn),
        in_specs=[pl.BlockSpec((tm, K), lambda m, n: (m, 0)), b_spec],
        out_specs=out_specs, out_shape=out_shape, name=name, task=task,
    )(a, b)


def _mm_nt(a, b, kind, *, name, out_dtype=BF16, tm=2048, tn=512, mul2r=None, task=None, rows=None):
    M, K = a.shape
    if kind == "plain":
        first, N = rows if rows is not None else (0, b.shape[0])
        n0 = first // tn
        b_spec = pl.BlockSpec((tn, K), lambda m, n: (n0 + n, 0))
    elif kind == "rows":
        nq = b.shape[1]
        N = N_CHIP * nq
        tn = min(tn, nq)
        per = nq // tn
        b_spec = pl.BlockSpec((None, tn, K), lambda m, n: (n // per, n % per, 0))
    else:
        N = b.shape[1]
        b_spec = pl.BlockSpec((N_CHIP, tn, K // N_CHIP), lambda m, n: (0, n, 0))
    tm = min(tm, M)
    kq = K // N_CHIP

    def body(a_ref, b_ref, *rest):
        o_ref = rest[-1]
        if kind == "cols":
            acc = _dot_nt(a_ref[:, 0:kq], b_ref[0])
            for j in range(1, N_CHIP):
                acc += _dot_nt(a_ref[:, j * kq:(j + 1) * kq], b_ref[j])
        else:
            acc = _dot_nt(a_ref[...], b_ref[...])
        if mul2r is not None:
            acc = acc * (2.0 * rest[0][...].astype(F32))
        o_ref[...] = acc.astype(out_dtype)

    in_specs = [pl.BlockSpec((tm, K), lambda m, n: (m, 0)), b_spec]
    args = [a, b]
    if mul2r is not None:
        in_specs.append(pl.BlockSpec((tm, tn), lambda m, n: (m, n)))
        args.append(mul2r)
    return _pallas(
        body, grid=(M // tm, N // tn), in_specs=in_specs,
        out_specs=pl.BlockSpec((tm, tn), lambda m, n: (m, n)),
        out_shape=jax.ShapeDtypeStruct((M, N), out_dtype), name=name, task=task,
    )(*args)


def _mm_tn(a, b, *, name, out_dtype=BF16, tk=1024, tn=512, cols4=False, task=None):
    M, K1 = a.shape
    N = b.shape[1]
    tk = min(tk, K1)
    tn = min(tn, N)

    def body(a_ref, b_ref, o_ref):
        o_ref[...] = _dot_tn(a_ref[...], b_ref[...]).astype(out_dtype)

    if cols4:
        per = (N // N_CHIP) // tn
        out_shape = jax.ShapeDtypeStruct((N_CHIP, K1, N // N_CHIP), out_dtype)
        o_spec = pl.BlockSpec((None, tk, tn), lambda k, n: (n // per, k, n % per))
    else:
        out_shape = jax.ShapeDtypeStruct((K1, N), out_dtype)
        o_spec = pl.BlockSpec((tk, tn), lambda k, n: (k, n))
    return _pallas(
        body, grid=(K1 // tk, N // tn),
        in_specs=[pl.BlockSpec((M, tk), lambda k, n: (0, k)), pl.BlockSpec((M, tn), lambda k, n: (0, n))],
        out_specs=o_spec, out_shape=out_shape, name=name, task=task,
    )(a, b)


def _rms(x, g):
    r = lax.rsqrt(jnp.mean(x * x, axis=-1, keepdims=True) + EPS)
    return x * r * g


def _rms_bwd(x, g, dy):
    r = lax.rsqrt(jnp.mean(x * x, axis=-1, keepdims=True) + EPS)
    xh = x * r
    dg = jnp.sum(dy * xh, axis=0, keepdims=True)
    dxh = dy * g
    dx = r * (dxh - xh * jnp.mean(dxh * xh, axis=-1, keepdims=True))
    return dx, dg


def _row_spec(tm, n):
    return pl.BlockSpec((tm, n), lambda i: (i, 0))


def _vec_spec(n):
    return pl.BlockSpec((1, n), lambda i: (0, 0))


def _acc_spec(n):
    return pl.BlockSpec((8, n), lambda i: (0, 0))


def _acc_add(ref, row, i):
    @pl.when(i == 0)
    def _():
        ref[...] = jnp.zeros_like(ref)
    ref[0:1, :] += row


def _rms_fwd_call(x, g, *, name, tm=256, task=None):
    M, n = x.shape
    tm = min(tm, M)

    def body(x_ref, g_ref, h_ref):
        h_ref[...] = _rms(x_ref[...], g_ref[...]).astype(BF16)

    return _pallas(
        body, grid=(M // tm,), in_specs=[_row_spec(tm, n), _vec_spec(n)], out_specs=_row_spec(tm, n),
        out_shape=jax.ShapeDtypeStruct((M, n), BF16), name=name, task=task,
    )(x, g)


def _post_pre_call(xres, z, g_post, g_pre, *, name, tm=256):
    M, n = xres.shape

    def body(x_ref, z_ref, gp_ref, gn_ref, xo_ref, h_ref):
        xn = x_ref[...] + _rms(z_ref[...], gp_ref[...])
        xo_ref[...] = xn
        h_ref[...] = _rms(xn, gn_ref[...]).astype(BF16)

    return pl.pallas_call(
        body, grid=(M // tm,),
        in_specs=[_row_spec(tm, n), _row_spec(tm, n), _vec_spec(n), _vec_spec(n)],
        out_specs=(_row_spec(tm, n), _row_spec(tm, n)),
        out_shape=(jax.ShapeDtypeStruct((M, n), F32), jax.ShapeDtypeStruct((M, n), BF16)),
        name=name, compiler_params=_cparams(),
    )(xres, z, g_post, g_pre)


def _final_call(x2, y3, g_post, target, *, name, tm=256):
    M, n = x2.shape

    def body(x_ref, y_ref, g_ref, t_ref, loss_ref, dx_ref, dy_ref, dg_ref):
        i = pl.program_id(0)
        y = y_ref[...]
        g = g_ref[...]
        diff = x_ref[...] + _rms(y, g) - t_ref[...]
        part = 0.5 * jnp.sum(jnp.sum(diff * diff, axis=1, keepdims=True), axis=0, keepdims=True) / n

        @pl.when(i == 0)
        def _():
            loss_ref[...] = jnp.zeros_like(loss_ref)
        loss_ref[...] += jnp.broadcast_to(part, loss_ref.shape)
        dx = diff / n
        dx_ref[...] = dx
        dy, dg = _rms_bwd(y, g, dx)
        dy_ref[...] = dy.astype(BF16)
        _acc_add(dg_ref, dg, i)

    return pl.pallas_call(
        body, grid=(M // tm,),
        in_specs=[_row_spec(tm, n), _row_spec(tm, n), _vec_spec(n), _row_spec(tm, n)],
        out_specs=(pl.BlockSpec((8, 128), lambda i: (0, 0)), _row_spec(tm, n), _row_spec(tm, n), _acc_spec(n)),
        out_shape=(jax.ShapeDtypeStruct((8, 128), F32), jax.ShapeDtypeStruct((M, n), F32),
                   jax.ShapeDtypeStruct((M, n), BF16), jax.ShapeDtypeStruct((8, n), F32)),
        name=name, compiler_params=_cparams(),
    )(x2, y3, g_post, target)


def _bwd_mid_call(dx_in, x, dh, g_pre, y, g_post, *, name, tm=256):
    M, n = x.shape

    def body(dxi_ref, x_ref, dh_ref, gpre_ref, y_ref, gpost_ref, dx_ref, dy_ref, dgpre_ref, dgpost_ref):
        i = pl.program_id(0)
        d1, dg1 = _rms_bwd(x_ref[...], gpre_ref[...], dh_ref[...])
        dx = dxi_ref[...] + d1
        dx_ref[...] = dx
        dy, dg2 = _rms_bwd(y_ref[...], gpost_ref[...], dx)
        dy_ref[...] = dy.astype(BF16)
        _acc_add(dgpre_ref, dg1, i)
        _acc_add(dgpost_ref, dg2, i)

    return pl.pallas_call(
        body, grid=(M // tm,),
        in_specs=[_row_spec(tm, n), _row_spec(tm, n), _row_spec(tm, n), _vec_spec(n), _row_spec(tm, n), _vec_spec(n)],
        out_specs=(_row_spec(tm, n), _row_spec(tm, n), _acc_spec(n), _acc_spec(n)),
        out_shape=(jax.ShapeDtypeStruct((M, n), F32), jax.ShapeDtypeStruct((M, n), BF16),
                   jax.ShapeDtypeStruct((8, n), F32), jax.ShapeDtypeStruct((8, n), F32)),
        name=name, compiler_params=_cparams(),
    )(dx_in, x, dh, g_pre, y, g_post)


def _bwd_last_call(dx_in, x, dh, g_pre, *, name, tm=256, task=None):
    M, n = x.shape

    def body(dxi_ref, x_ref, dh_ref, g_ref, dx_ref, dg_ref):
        i = pl.program_id(0)
        d1, dg1 = _rms_bwd(x_ref[...], g_ref[...], dh_ref[...])
        dx_ref[...] = dxi_ref[...] + d1
        _acc_add(dg_ref, dg1, i)

    return _pallas(
        body, grid=(M // tm,),
        in_specs=[_row_spec(tm, n), _row_spec(tm, n), _row_spec(tm, n), _vec_spec(n)],
        out_specs=(_row_spec(tm, n), _acc_spec(n)),
        out_shape=(jax.ShapeDtypeStruct((M, n), F32), jax.ShapeDtypeStruct((8, n), F32)),
        name=name, task=task,
    )(dx_in, x, dh, g_pre)


def _gain_grad_call(x, g, dy_a, dy_b, *, name):
    M, n = x.shape

    def body(x_ref, g_ref, a_ref, b_ref, dg_ref):
        _, dg = _rms_bwd(x_ref[...], g_ref[...], a_ref[...] + b_ref[...])
        dg_ref[...] = jnp.zeros_like(dg_ref)
        dg_ref[0:1, :] = dg

    return pl.pallas_call(
        body, grid=(1,),
        in_specs=[_row_spec(M, n), _vec_spec(n), _row_spec(M, n), _row_spec(M, n)],
        out_specs=_acc_spec(n), out_shape=jax.ShapeDtypeStruct((8, n), F32),
        name=name, compiler_params=_cparams(),
    )(x, g, dy_a, dy_b)


def _head_group_matrix():
    a = lax.broadcasted_iota(jnp.int32, (D_GRP, D_GRP), 0) // HEAD
    b = lax.broadcasted_iota(jnp.int32, (D_GRP, D_GRP), 1) // HEAD
    return jnp.where(a == b, 1.0, 0.0).astype(BF16)


def _mix_norm_fwd_call(yf, yc, gf, gc, *, name, tm=256):
    M = yf.shape[0]

    def body(yf_ref, yc_ref, gf_ref, gc_ref, o_ref):
        o_ref[:, 0:D_GRP] = _rms(yf_ref[...], gf_ref[...]).astype(BF16)
        o_ref[:, D_GRP:D] = _rms(yc_ref[...], gc_ref[...]).astype(BF16)

    return pl.pallas_call(
        body, grid=(M // tm,),
        in_specs=[_row_spec(tm, D_GRP), _row_spec(tm, D_GRP), _vec_spec(D_GRP), _vec_spec(D_GRP)],
        out_specs=_row_spec(tm, D), out_shape=jax.ShapeDtypeStruct((M, D), BF16),
        name=name, compiler_params=_cparams(),
    )(yf, yc, gf, gc)


def _mix_norm_bwd_call(dyn, yf, yc, gf, gc, *, name, tm=TQ):
    M = yf.shape[0]

    def body(dyn_ref, yf_ref, yc_ref, gf_ref, gc_ref, dof_ref, doc_ref, delta_ref, dgf_ref, dgc_ref):
        i = pl.program_id(0)
        yf_ = yf_ref[...]
        dof, dgf = _rms_bwd(yf_, gf_ref[...], dyn_ref[:, 0:D_GRP])
        doc, dgc = _rms_bwd(yc_ref[...], gc_ref[...], dyn_ref[:, D_GRP:D])
        dof_b = dof.astype(BF16)
        dof_ref[...] = dof_b
        doc_ref[...] = doc.astype(BF16)
        prod = dof_b.astype(F32) * yf_
        hi = prod.astype(BF16)
        lo = (prod - hi.astype(F32)).astype(BF16)
        grp = _head_group_matrix()
        delta_ref[...] = (_dot(hi, grp) + _dot(lo, grp)).T
        _acc_add(dgf_ref, dgf, i)
        _acc_add(dgc_ref, dgc, i)

    return pl.pallas_call(
        body, grid=(M // tm,),
        in_specs=[_row_spec(tm, D), _row_spec(tm, D_GRP), _row_spec(tm, D_GRP), _vec_spec(D_GRP), _vec_spec(D_GRP)],
        out_specs=(_row_spec(tm, D_GRP), _row_spec(tm, D_GRP), pl.BlockSpec((None, D_GRP, tm), lambda i: (i, 0, 0)),
                   _acc_spec(D_GRP), _acc_spec(D_GRP)),
        out_shape=(jax.ShapeDtypeStruct((M, D_GRP), BF16), jax.ShapeDtypeStruct((M, D_GRP), BF16),
                   jax.ShapeDtypeStruct((M // tm, D_GRP, tm), F32), jax.ShapeDtypeStruct((8, D_GRP), F32),
                   jax.ShapeDtypeStruct((8, D_GRP), F32)),
        name=name, compiler_params=_cparams(),
    )(dyn, yf, yc, gf, gc)


def _tri(n, lower_incl):
    a = lax.broadcasted_iota(jnp.int32, (n, n), 0)
    b = lax.broadcasted_iota(jnp.int32, (n, n), 1)
    return jnp.where(a >= b, 1.0, 0.0).astype(BF16) if lower_incl else jnp.where(a <= b, 1.0, 0.0).astype(BF16)


def _fox_prep_call(fl_raw, b_pad, *, name):
    S = fl_raw.shape[0]
    nb = S // TQ

    def body(fl_ref, b_ref, crep_ref, ct_ref, carry_ref):
        i = pl.program_id(0)

        @pl.when(i == 0)
        def _():
            carry_ref[...] = jnp.zeros_like(carry_ref)
        logf = jax.nn.log_sigmoid(fl_ref[...] + b_ref[...])
        cb = _dot3_l(_tri(TQ, True), logf) + carry_ref[0:1, :]
        carry_ref[0:1, :] = cb[TQ - 1:TQ, :]
        a = lax.broadcasted_iota(jnp.int32, (128, D_GRP), 0)
        b = lax.broadcasted_iota(jnp.int32, (128, D_GRP), 1) // HEAD
        expand = jnp.where(a == b, 1.0, 0.0).astype(BF16)
        crep = _dot3(cb, expand)
        crep_ref[...] = crep
        ct_ref[...] = crep.T

    return pl.pallas_call(
        body, grid=(nb,),
        in_specs=[_row_spec(TQ, 128), _vec_spec(128)],
        out_specs=(_row_spec(TQ, D_GRP), pl.BlockSpec((None, D_GRP, TQ), lambda i: (i, 0, 0))),
        out_shape=(jax.ShapeDtypeStruct((S, D_GRP), F32), jax.ShapeDtypeStruct((nb, D_GRP, TQ), F32)),
        scratch_shapes=[pltpu.VMEM((8, 128), F32)],
        name=name, compiler_params=_cparams(),
    )(fl_raw, b_pad)


def _lane_masks():
    lane = lax.broadcasted_iota(jnp.int32, (1, 128), 1)
    return lane < HEAD, lane >= HEAD


def _fox_fwd_call(proj, c_rep, c_t, *, name, task=None):
    S = proj.shape[0]
    nq = S // TQ
    scale = HEAD ** -0.5

    def body(q_ref, k_ref, v_ref, c_ref, ct_ref, o_ref, lse_ref):
        i = pl.program_id(1)
        m_lo, m_hi = _lane_masks()
        masks = (m_lo, m_hi)
        q = q_ref[...]
        qm = [jnp.where(mk, q, jnp.zeros_like(q)) for mk in masks]
        cq = c_ref[...]
        cqh = [cq[:, 0:1], cq[:, HEAD:HEAD + 1]]
        row = lax.broadcasted_iota(jnp.int32, (TQ, TQ), 0)
        col = lax.broadcasted_iota(jnp.int32, (TQ, TQ), 1)

        def scores(j):
            start = pl.multiple_of(j * TQ, TQ)
            k = k_ref[pl.ds(start, TQ), :]
            ct = ct_ref[j]
            return tuple(_dot_nt(qm[h], k) * scale + (cqh[h] - ct[HEAD * h:HEAD * h + 1, :]) for h in range(2))

        def update(j, ss, state, masked):
            ms, ls, acc = state
            start = pl.multiple_of(j * TQ, TQ)
            v = v_ref[pl.ds(start, TQ), :]
            new_m, new_l, pv, alpha_l = [], [], [], []
            for h in range(2):
                s = ss[h]
                if masked:
                    s = jnp.where(row >= col, s, NEG)
                mn = jnp.maximum(ms[h], jnp.max(s, axis=1, keepdims=True))
                alpha = jnp.exp(ms[h] - mn)
                p = jnp.exp(s - mn)
                new_l.append(alpha * ls[h] + jnp.sum(p, axis=1, keepdims=True))
                new_m.append(mn)
                alpha_l.append(alpha)
                pv.append(_dot(p.astype(BF16), jnp.where(masks[h], v, jnp.zeros_like(v))))
            alpha_lane = jnp.where(m_lo, alpha_l[0], alpha_l[1])
            acc = acc * alpha_lane + pv[0] + pv[1]
            return (tuple(new_m), tuple(new_l), acc)

        def step(j, carry):
            ss, state = carry
            return (scores(j + 1), update(j, ss, state, False))

        init = ((jnp.full((TQ, 1), NEG, F32),) * 2, (jnp.zeros((TQ, 1), F32),) * 2, jnp.zeros((TQ, 128), F32))
        ss, state = lax.fori_loop(0, i, step, (scores(0), init))
        ms, ls, acc = update(i, ss, state, True)
        l_lane = jnp.where(m_lo, ls[0], ls[1])
        o_ref[...] = acc / l_lane
        lse_ref[...] = jnp.where(m_lo, ms[0] + jnp.log(ls[0]), ms[1] + jnp.log(ls[1])).T

    return _pallas(
        body, grid=(N_PAIR, nq),
        in_specs=[pl.BlockSpec((TQ, 128), lambda p, i: (i, p)),
                  pl.BlockSpec((S, 128), lambda p, i: (0, N_PAIR + p)),
                  pl.BlockSpec((S, 128), lambda p, i: (0, 2 * N_PAIR + p)),
                  pl.BlockSpec((TQ, 128), lambda p, i: (i, p)),
                  pl.BlockSpec((nq, 128, TQ), lambda p, i: (0, p, 0))],
        out_specs=(pl.BlockSpec((TQ, 128), lambda p, i: (i, p)), pl.BlockSpec((None, 128, TQ), lambda p, i: (i, p, 0))),
        out_shape=(jax.ShapeDtypeStruct((S, D_GRP), F32), jax.ShapeDtypeStruct((nq, D_GRP, TQ), F32)),
        name=name, task=task,
    )(proj, proj, proj, c_rep, c_t)


def _fox_bwd_call(proj, do, lse_t, delta_t, c_rep, c_t, *, name, task=None):
    S = proj.shape[0]
    nq = S // TQ
    scale = HEAD ** -0.5

    def body(q_ref, k_ref, v_ref, do_ref, lse_ref, dl_ref, ck_ref, ct_ref,
             dq_ref, dk_ref, dv_ref, dcq_ref, dck_ref, dqa_ref):
        j = pl.program_id(1)
        m_lo, m_hi = _lane_masks()
        masks = (m_lo, m_hi)

        @pl.when(j == 0)
        def _():
            dqa_ref[...] = jnp.zeros_like(dqa_ref)
            dcq_ref[...] = jnp.zeros_like(dcq_ref)
        k = k_ref[...]
        v = v_ref[...]
        km = [jnp.where(mk, k, jnp.zeros_like(k)) for mk in masks]
        ck = ck_ref[...]
        krow = lax.broadcasted_iota(jnp.int32, (TQ, TQ), 0)
        qcol = lax.broadcasted_iota(jnp.int32, (TQ, TQ), 1)

        def probs(i):
            start = pl.multiple_of(i * TQ, TQ)
            q = q_ref[pl.ds(start, TQ), :]
            do = do_ref[pl.ds(start, TQ), :]
            lse = lse_ref[i]
            cq = ct_ref[i]
            out = []
            for h in range(2):
                lo = HEAD * h
                qm = jnp.where(masks[h], q, jnp.zeros_like(q))
                dom = jnp.where(masks[h], do, jnp.zeros_like(do))
                st = _dot_nt(k, qm) * scale + (cq[lo:lo + 1, :] - ck[:, lo:lo + 1])
                out.append((jnp.exp(st - lse[lo:lo + 1, :]), _dot_nt(v, dom)))
            return tuple(out)

        def update(i, pd, carry, masked):
            dk, dv, dck = carry
            start = pl.multiple_of(i * TQ, TQ)
            q = q_ref[pl.ds(start, TQ), :]
            do = do_ref[pl.ds(start, TQ), :]
            dl = dl_ref[i]
            dq = jnp.zeros((TQ, 128), F32)
            new_dck = []
            for h in range(2):
                lo = HEAD * h
                qm = jnp.where(masks[h], q, jnp.zeros_like(q))
                dom = jnp.where(masks[h], do, jnp.zeros_like(do))
                pt, dpt = pd[h]
                if masked:
                    pt = jnp.where(qcol >= krow, pt, 0.0)
                dst = pt * (dpt - dl[lo:lo + 1, :])
                dcq_ref[i, h:h + 1, :] += jnp.sum(dst, axis=0, keepdims=True)
                new_dck.append(dck[h] + jnp.sum(dst, axis=1, keepdims=True))
                dsb = (dst * scale).astype(BF16)
                dv = dv + _dot(pt.astype(BF16), dom)
                dk = dk + _dot(dsb, qm)
                dq = dq + _dot_tn(dsb, km[h])
            dqa_ref[pl.ds(start, TQ), :] += dq
            return (dk, dv, tuple(new_dck))

        def step(i, carry):
            pd, sums = carry
            return (probs(jnp.minimum(i + 1, nq - 1)), update(i, pd, sums, False))

        init = (jnp.zeros((TQ, 128), F32), jnp.zeros((TQ, 128), F32), (jnp.zeros((TQ, 1), F32),) * 2)
        first = probs(j)
        second = probs(jnp.minimum(j + 1, nq - 1))
        _, (dk, dv, dck) = lax.fori_loop(j + 1, nq, step, (second, update(j, first, init, True)))
        dk_ref[...] = dk.astype(BF16)
        dv_ref[...] = dv.astype(BF16)
        dck_ref[...] = -jnp.where(m_lo, dck[0], dck[1])

        @pl.when(j == nq - 1)
        def _():
            dq_ref[...] = dqa_ref[...].astype(BF16)

    res = lambda p, j: (0, p)
    stat = pl.BlockSpec((nq, 128, TQ), lambda p, j: (0, p, 0))
    blk = pl.BlockSpec((TQ, 128), lambda p, j: (j, p))
    return _pallas(
        body, grid=(N_PAIR, nq), task=task,
        in_specs=[pl.BlockSpec((S, 128), res),
                  pl.BlockSpec((TQ, 128), lambda p, j: (j, N_PAIR + p)),
                  pl.BlockSpec((TQ, 128), lambda p, j: (j, 2 * N_PAIR + p)),
                  pl.BlockSpec((S, 128), res), stat, stat, blk, stat],
        out_specs=(pl.BlockSpec((S, 128), res), blk, blk,
                   pl.BlockSpec((None, nq, 8, TQ), lambda p, j: (p, 0, 0, 0)), blk),
        out_shape=(jax.ShapeDtypeStruct((S, D_GRP), BF16), jax.ShapeDtypeStruct((S, D_GRP), BF16),
                   jax.ShapeDtypeStruct((S, D_GRP), BF16), jax.ShapeDtypeStruct((N_PAIR, nq, 8, TQ), F32),
                   jax.ShapeDtypeStruct((S, D_GRP), F32)),
        scratch_shapes=[pltpu.VMEM((S, 128), F32)],
        name=name,
    )(proj, proj, proj, do, lse_t, delta_t, c_rep, c_t)


def _fox_gate_bwd_call(dc_rows, fl_raw, b_pad, *, name):
    S = fl_raw.shape[0]
    nb = S // TQ

    def body(dc_ref, fl_ref, b_ref, dfl_ref, db_ref, carry_ref):
        i = pl.program_id(0)

        @pl.when(i == 0)
        def _():
            carry_ref[...] = jnp.zeros_like(carry_ref)
        rc = _dot3(dc_ref[...], _tri(TQ, True)) + carry_ref[:, 0:1]
        carry_ref[...] = jnp.broadcast_to(rc[:, 0:1], carry_ref.shape)
        fl = fl_ref[...] + b_ref[...]
        dfl = rc.T * jax.nn.sigmoid(-fl)
        dfl_ref[...] = dfl.astype(BF16)
        _acc_add(db_ref, jnp.sum(dfl, axis=0, keepdims=True), i)

    rev = lambda i: (nb - 1 - i, 0)
    return pl.pallas_call(
        body, grid=(nb,),
        in_specs=[pl.BlockSpec((128, TQ), lambda i: (0, nb - 1 - i)), pl.BlockSpec((TQ, 128), rev), _vec_spec(128)],
        out_specs=(pl.BlockSpec((TQ, 128), rev), _acc_spec(128)),
        out_shape=(jax.ShapeDtypeStruct((S, 128), BF16), jax.ShapeDtypeStruct((8, 128), F32)),
        scratch_shapes=[pltpu.VMEM((128, 128), F32)],
        name=name, compiler_params=_cparams(),
    )(dc_rows, fl_raw, b_pad)


def _chk_bias_call(g_rev, *, name):
    def body(g_ref, o_ref):
        x = jnp.broadcast_to(g_ref[...], (TQ, ROLL_W))
        rolled = pltpu.roll(x, ROLL_W - (TQ - 1), 1, stride=1, stride_axis=0)
        qc = lax.broadcasted_iota(jnp.int32, (TQ, WIN), 0) // CHUNK
        kc = lax.broadcasted_iota(jnp.int32, (TQ, WIN), 1) // CHUNK
        band = (kc >= qc) & (kc <= qc + LEFT)
        o_ref[...] = jnp.where(band, rolled[:, 0:WIN], NEG)

    return pl.pallas_call(
        body, grid=(8,),
        in_specs=[pl.BlockSpec((None, 1, ROLL_W), lambda h: (h, 0, 0))],
        out_specs=pl.BlockSpec((None, TQ, WIN), lambda h: (h, 0, 0)),
        out_shape=jax.ShapeDtypeStruct((8, TQ, WIN), F32), name=name, compiler_params=_cparams(),
    )(g_rev.reshape(8, 1, ROLL_W))


def _chk_scores(i, qm, kwin, bias, scale):
    s = _dot_nt(qm, kwin) * scale + bias
    kc = lax.broadcasted_iota(jnp.int32, (TQ, WIN), 1) // CHUNK
    return jnp.where(kc + i * (TQ // CHUNK) >= LEFT, s, NEG)


def _chk_fwd_call(proj, bias, *, name, task=None):
    S = proj.shape[0]
    nq = S // TQ
    scale = HEAD ** -0.5

    def body(q_ref, k_ref, v_ref, b_ref, o_ref, kp_ref, vp_ref):
        i = pl.program_id(1)

        @pl.when(i == 0)
        def _():
            kp_ref[0:PADK, :] = jnp.zeros((PADK, 128), BF16)
            vp_ref[0:PADK, :] = jnp.zeros((PADK, 128), BF16)
            kp_ref[PADK:PADK + S, :] = k_ref[...]
            vp_ref[PADK:PADK + S, :] = v_ref[...]
        masks = _lane_masks()
        q = q_ref[...]
        start = pl.multiple_of(i * TQ, TQ)
        kwin = kp_ref[pl.ds(start, WIN), :]
        vwin = vp_ref[pl.ds(start, WIN), :]
        ss = [_chk_scores(i, jnp.where(masks[h], q, jnp.zeros_like(q)), kwin, b_ref[h], scale) for h in range(2)]
        ps = []
        for s in ss:
            p = jnp.exp(s - jnp.max(s, axis=1, keepdims=True))
            ps.append((p / jnp.sum(p, axis=1, keepdims=True)).astype(BF16))
        o_ref[...] = (_dot(ps[0], jnp.where(masks[0], vwin, jnp.zeros_like(vwin)))
                      + _dot(ps[1], jnp.where(masks[1], vwin, jnp.zeros_like(vwin))))

    c0 = 3 * N_PAIR
    return _pallas(
        body, grid=(N_PAIR, nq), task=task,
        in_specs=[pl.BlockSpec((TQ, 128), lambda p, i: (i, c0 + p)),
                  pl.BlockSpec((S, 128), lambda p, i: (0, c0 + N_PAIR + p)),
                  pl.BlockSpec((S, 128), lambda p, i: (0, c0 + 2 * N_PAIR + p)),
                  pl.BlockSpec((2, TQ, WIN), lambda p, i: (p, 0, 0))],
        out_specs=pl.BlockSpec((TQ, 128), lambda p, i: (i, p)),
        out_shape=jax.ShapeDtypeStruct((S, D_GRP), F32),
        scratch_shapes=[pltpu.VMEM((S + PADK, 128), BF16), pltpu.VMEM((S + PADK, 128), BF16)],
        name=name,
    )(proj, proj, proj, bias)


def _chk_bwd_call(proj, do, bias, *, name, task=None):
    S = proj.shape[0]
    nq = S // TQ
    scale = HEAD ** -0.5

    def body(q_ref, k_ref, v_ref, do_ref, b_ref, dq_ref, dk_ref, dv_ref, dg_ref, kp_ref, vp_ref, dkp_ref, dvp_ref, db_ref):
        i = pl.program_id(1)

        @pl.when(i == 0)
        def _():
            kp_ref[0:PADK, :] = jnp.zeros((PADK, 128), BF16)
            vp_ref[0:PADK, :] = jnp.zeros((PADK, 128), BF16)
            kp_ref[PADK:PADK + S, :] = k_ref[...]
            vp_ref[PADK:PADK + S, :] = v_ref[...]
            dkp_ref[...] = jnp.zeros_like(dkp_ref)
            dvp_ref[...] = jnp.zeros_like(dvp_ref)
            db_ref[...] = jnp.zeros_like(db_ref)
        masks = _lane_masks()
        q = q_ref[...]
        dout = do_ref[...]
        start = pl.multiple_of(i * TQ, TQ)
        kwin = kp_ref[pl.ds(start, WIN), :]
        vwin = vp_ref[pl.ds(start, WIN), :]
        qm = [jnp.where(mk, q, jnp.zeros_like(q)) for mk in masks]
        dom = [jnp.where(mk, dout, jnp.zeros_like(dout)) for mk in masks]
        ss = [_chk_scores(i, qm[h], kwin, b_ref[h], scale) for h in range(2)]
        dps = [_dot_nt(dom[h], vwin) for h in range(2)]
        pbs, dsbs = [], []
        for h in range(2):
            p = jnp.exp(ss[h] - jnp.max(ss[h], axis=1, keepdims=True))
            p = p / jnp.sum(p, axis=1, keepdims=True)
            ds = p * (dps[h] - jnp.sum(p * dps[h], axis=1, keepdims=True))
            db_ref[h] += ds
            pbs.append(p.astype(BF16))
            dsbs.append((ds * scale).astype(BF16))
        dq_ref[...] = (_dot(dsbs[0], jnp.where(masks[0], kwin, jnp.zeros_like(kwin)))
                       + _dot(dsbs[1], jnp.where(masks[1], kwin, jnp.zeros_like(kwin)))).astype(BF16)
        dkp_ref[pl.ds(start, WIN), :] += _dot_tn(dsbs[0], qm[0]) + _dot_tn(dsbs[1], qm[1])
        dvp_ref[pl.ds(start, WIN), :] += _dot_tn(pbs[0], dom[0]) + _dot_tn(pbs[1], dom[1])

        @pl.when(i == nq - 1)
        def _():
            dk_ref[...] = dkp_ref[PADK:PADK + S, :].astype(BF16)
            dv_ref[...] = dvp_ref[PADK:PADK + S, :].astype(BF16)
            a = lax.broadcasted_iota(jnp.int32, (TQ, TQ), 0)
            b = lax.broadcasted_iota(jnp.int32, (TQ, TQ), 1)
            flip = jnp.where(a + b == TQ - 1, 1.0, 0.0).astype(BF16)
            e = lax.broadcasted_iota(jnp.int32, (1, ROLL_W), 1)
            dg_ref[...] = jnp.zeros_like(dg_ref)
            for h in range(2):
                rev = _dot3_l(flip, db_ref[h])
                wide = jnp.concatenate([rev, jnp.zeros((TQ, ROLL_W - WIN), F32)], axis=1)
                diag = pltpu.roll(wide, 0, 1, stride=1, stride_axis=0)
                dg = jnp.sum(diag, axis=0, keepdims=True)
                lo = jnp.sum(jnp.where(e <= 639, dg, 0.0), axis=1, keepdims=True)
                hi = jnp.sum(jnp.where(e >= 895, dg, 0.0), axis=1, keepdims=True)
                dg_ref[h:h + 1, :] = jnp.where(e == 639, lo, jnp.where(e == 895, hi, dg))

    c0 = 3 * N_PAIR
    res = lambda p, i: (0, p)
    return _pallas(
        body, grid=(N_PAIR, nq), task=task,
        in_specs=[pl.BlockSpec((TQ, 128), lambda p, i: (i, c0 + p)),
                  pl.BlockSpec((S, 128), lambda p, i: (0, c0 + N_PAIR + p)),
                  pl.BlockSpec((S, 128), lambda p, i: (0, c0 + 2 * N_PAIR + p)),
                  pl.BlockSpec((TQ, 128), lambda p, i: (i, p)),
                  pl.BlockSpec((2, TQ, WIN), lambda p, i: (p, 0, 0))],
        out_specs=(pl.BlockSpec((TQ, 128), lambda p, i: (i, p)), pl.BlockSpec((S, 128), res),
                   pl.BlockSpec((S, 128), res), pl.BlockSpec((None, 8, ROLL_W), lambda p, i: (p, 0, 0))),
        out_shape=(jax.ShapeDtypeStruct((S, D_GRP), BF16), jax.ShapeDtypeStruct((S, D_GRP), BF16),
                   jax.ShapeDtypeStruct((S, D_GRP), BF16), jax.ShapeDtypeStruct((N_PAIR, 8, ROLL_W), F32)),
        scratch_shapes=[pltpu.VMEM((S + PADK, 128), BF16), pltpu.VMEM((S + PADK, 128), BF16),
                        pltpu.VMEM((S + PADK, 128), F32), pltpu.VMEM((S + PADK, 128), F32),
                        pltpu.VMEM((2, TQ, WIN), F32)],
        name=name,
    )(proj, proj, proj, do, bias)


def _mem_fwd_call(q, k, v, *, name, tq=512):
    S = q.shape[0]
    scale = MEM_HD ** -0.5

    def body(q_ref, k_ref, v_ref, o_ref):
        s = _dot_nt(q_ref[...], k_ref[...]) * scale
        p = jnp.exp(s - jnp.max(s, axis=1, keepdims=True))
        p = p / jnp.sum(p, axis=1, keepdims=True)
        o_ref[...] = _dot(p.astype(BF16), v_ref[...]).astype(BF16)

    return pl.pallas_call(
        body, grid=(MEM_HEADS, S // tq),
        in_specs=[pl.BlockSpec((tq, MEM_HD), lambda h, i: (i, h)),
                  pl.BlockSpec((N_MEM, MEM_HD), lambda h, i: (0, h)),
                  pl.BlockSpec((N_MEM, MEM_HD), lambda h, i: (0, h))],
        out_specs=pl.BlockSpec((tq, MEM_HD), lambda h, i: (i, h)),
        out_shape=jax.ShapeDtypeStruct((S, D), BF16), name=name, compiler_params=_cparams(),
    )(q, k, v)


def _mem_bwd_call(q, k, v, do, *, name, tq=512):
    S = q.shape[0]
    n = S // tq
    scale = MEM_HD ** -0.5

    def body(q_ref, k_ref, v_ref, do_ref, dq_ref, dk_ref, dv_ref, dka_ref, dva_ref):
        i = pl.program_id(1)

        @pl.when(i == 0)
        def _():
            dka_ref[...] = jnp.zeros_like(dka_ref)
            dva_ref[...] = jnp.zeros_like(dva_ref)
        qb = q_ref[...]
        kb = k_ref[...]
        dob = do_ref[...]
        s = _dot_nt(qb, kb) * scale
        p = jnp.exp(s - jnp.max(s, axis=1, keepdims=True))
        p = p / jnp.sum(p, axis=1, keepdims=True)
        dp = _dot_nt(dob, v_ref[...])
        ds = p * (dp - jnp.sum(p * dp, axis=1, keepdims=True))
        dsb = (ds * scale).astype(BF16)
        dq_ref[...] = _dot(dsb, kb).astype(BF16)
        dka_ref[...] += _dot_tn(dsb, qb)
        dva_ref[...] += _dot_tn(p.astype(BF16), dob)

        @pl.when(i == n - 1)
        def _():
            dk_ref[...] = dka_ref[...].astype(BF16)
            dv_ref[...] = dva_ref[...].astype(BF16)

    kv = pl.BlockSpec((N_MEM, MEM_HD), lambda h, i: (0, h))
    qs = pl.BlockSpec((tq, MEM_HD), lambda h, i: (i, h))
    return pl.pallas_call(
        body, grid=(MEM_HEADS, n), in_specs=[qs, kv, kv, qs], out_specs=(qs, kv, kv),
        out_shape=(jax.ShapeDtypeStruct((S, D), BF16), jax.ShapeDtypeStruct((N_MEM, D), BF16),
                   jax.ShapeDtypeStruct((N_MEM, D), BF16)),
        scratch_shapes=[pltpu.VMEM((N_MEM, MEM_HD), F32), pltpu.VMEM((N_MEM, MEM_HD), F32)],
        name=name, compiler_params=_cparams(),
    )(q, k, v, do)


def _rel_table_to_g(rel):
    return jnp.concatenate([
        jnp.broadcast_to(rel[:, N_REL - 1:N_REL], (8, 640)),
        rel[:, 1:N_REL - 1][:, ::-1],
        jnp.broadcast_to(rel[:, 0:1], (8, 129)),
    ], axis=1)


def _g_to_rel_table(dg):
    return dg[:, 639:896][:, ::-1]


def _place():
    x, y, c = lax.axis_index("x"), lax.axis_index("y"), lax.axis_index("c")
    others = [(1 - x, y), (x, 1 - y), (1 - x, 1 - y)]
    return x, y, c, others


def _half(c, rows):
    hr = rows // 2
    return pl.ds(pl.multiple_of(c * hr, 16), hr)


def _dma_sems(*shape):
    return pltpu.SemaphoreType.DMA(shape)


def _cast_slabs_call(ws, chip_arr, *, name, tm=256, task=None):
    n = len(ws)
    cols = ws[0].shape[1]
    tiles = [w.shape[0] // tm for w in ws]
    steps = max(tiles)

    def body(chip_ref, *refs):
        i = pl.program_id(0)
        for k in range(n):
            def cast(k=k):
                refs[n + k][...] = refs[k][...].astype(BF16)
            if tiles[k] == steps:
                cast()
            else:
                pl.when(i < tiles[k])(cast)

    in_specs = [pl.BlockSpec((tm, cols), lambda i, chip, t=t: (jnp.minimum(i, t - 1), 0)) for t in tiles]
    out_specs = [pl.BlockSpec((None, tm, cols), lambda i, chip, t=t: (chip[0], jnp.minimum(i, t - 1), 0)) for t in tiles]
    out_shape = [jax.ShapeDtypeStruct((N_CHIP,) + w.shape, BF16) for w in ws]
    return _pallas(body, grid=(steps,), in_specs=in_specs, out_specs=out_specs, out_shape=out_shape, name=name,
                   task=task, prefetch=1)(chip_arr, *ws)


def _cast_slab_call(w, chip_arr, *, name, tm=256, pad_rows=0):
    rows, cols = w.shape
    if pad_rows:
        tm = rows
    tm = min(tm, rows)

    def body(chip_ref, w_ref, o_ref):
        o_ref[0:tm, :] = w_ref[...].astype(BF16)
        if pad_rows:
            o_ref[tm:tm + pad_rows, :] = jnp.zeros((pad_rows, cols), BF16)

    return pl.pallas_call(
        body,
        grid_spec=pltpu.PrefetchScalarGridSpec(
            num_scalar_prefetch=1, grid=(rows // tm,),
            in_specs=[pl.BlockSpec((tm, cols), lambda i, chip: (i, 0))],
            out_specs=pl.BlockSpec((None, tm + pad_rows, cols), lambda i, chip: (chip[0], i, 0))),
        out_shape=jax.ShapeDtypeStruct((N_CHIP, rows + pad_rows, cols), BF16), name=name,
        compiler_params=_cparams(),
    )(chip_arr, w)


def _ag_ici_task(gathered):
    n = len(gathered)

    def copies(ins, outs, sems):
        send_sems, recv_sems = sems
        x, y, c, others = _place()
        me = 2 * x + y
        for k in range(n):
            mine = _half(c, gathered[k].shape[1])
            for t, (ox, oy) in enumerate(others):
                yield [pltpu.make_async_remote_copy(
                    src_ref=ins[k].at[me, mine], dst_ref=outs[k].at[slab, mine],
                    send_sem=send_sems.at[k, t], recv_sem=recv_sems.at[k, t],
                    device_id=(ox, oy, c), device_id_type=MESH) for slab in (me, 2 * ox + oy)]

    def issue(ins, outs, sems):
        for outgoing, _ in copies(ins, outs, sems):
            outgoing.start()

    def drain(ins, outs, sems):
        for outgoing, incoming in copies(ins, outs, sems):
            incoming.wait_recv()
            outgoing.wait_send()

    return _Task(gathered, [jax.ShapeDtypeStruct(g.shape, g.dtype) for g in gathered],
                 [_dma_sems(n, 3), _dma_sems(n, 3)], issue, drain, aliases={k: k for k in range(n)})


def _ag_d2d_task(gathered):
    n = len(gathered)

    def copies(ins, outs, sems):
        send_sems, recv_sems = sems
        x, y, c, others = _place()
        for k in range(n):
            rows = gathered[k].shape[1]
            mine, theirs = _half(c, rows), _half(1 - c, rows)
            for t, (ox, oy) in enumerate(others):
                slab = 2 * ox + oy
                pair = [pltpu.make_async_remote_copy(
                    src_ref=ins[k].at[slab, half], dst_ref=outs[k].at[slab, half],
                    send_sem=send_sems.at[k, t], recv_sem=recv_sems.at[k, t],
                    device_id=(x, y, 1 - c), device_id_type=MESH) for half in (mine, theirs)]
                yield pair

    def issue(ins, outs, sems):
        for outgoing, _ in copies(ins, outs, sems):
            outgoing.start()

    def drain(ins, outs, sems):
        for outgoing, incoming in copies(ins, outs, sems):
            incoming.wait_recv()
            outgoing.wait_send()

    return _Task(gathered, [jax.ShapeDtypeStruct(g.shape, g.dtype) for g in gathered],
                 [_dma_sems(n, 3), _dma_sems(n, 3)], issue, drain, aliases={k: k for k in range(n)})


def _rs_pair_task(ds):
    n = len(ds)

    def copies(ins, outs, sems):
        send_sems, recv_sems = sems
        x, y, c, _ = _place()
        for k in range(n):
            yield pltpu.make_async_remote_copy(
                src_ref=ins[k].at[:, _half(1 - c, ds[k].shape[1])], dst_ref=outs[k],
                send_sem=send_sems.at[k], recv_sem=recv_sems.at[k],
                device_id=(x, y, 1 - c), device_id_type=MESH)

    def issue(ins, outs, sems):
        for cp in copies(ins, outs, sems):
            cp.start()

    def drain(ins, outs, sems):
        for cp in copies(ins, outs, sems):
            cp.wait()

    return _Task(ds, [jax.ShapeDtypeStruct((N_CHIP, d.shape[1] // 2, d.shape[2]), d.dtype) for d in ds],
                 [_dma_sems(n), _dma_sems(n)], issue, drain)


def _pair_add_call(d, r1, c_arr, *, name, tm=256):
    _, rows, cols = d.shape
    hr = rows // 2
    tm = tm if hr % tm == 0 else hr
    nb = hr // tm

    def body(c_ref, d_ref, r_ref, o_ref):
        o_ref[...] = (d_ref[...].astype(F32) + r_ref[...].astype(F32)).astype(BF16)

    return pl.pallas_call(
        body,
        grid_spec=pltpu.PrefetchScalarGridSpec(
            num_scalar_prefetch=1, grid=(N_CHIP, nb),
            in_specs=[pl.BlockSpec((None, tm, cols), lambda j, i, c: (j, c[0] * nb + i, 0)),
                      pl.BlockSpec((None, tm, cols), lambda j, i, c: (j, i, 0))],
            out_specs=pl.BlockSpec((None, tm, cols), lambda j, i, c: (j, i, 0))),
        out_shape=jax.ShapeDtypeStruct((N_CHIP, hr, cols), BF16), name=name, compiler_params=_cparams(),
    )(c_arr, d, r1)


def _rs_chip_task(ps):
    n = len(ps)

    def copies(ins, outs, sems):
        send_sems, recv_sems = sems
        x, y, c, others = _place()
        for k in range(n):
            for t, (ox, oy) in enumerate(others):
                yield pltpu.make_async_remote_copy(
                    src_ref=ins[k].at[2 * ox + oy], dst_ref=outs[k].at[t],
                    send_sem=send_sems.at[k, t], recv_sem=recv_sems.at[k, t],
                    device_id=(ox, oy, c), device_id_type=MESH)

    def issue(ins, outs, sems):
        for cp in copies(ins, outs, sems):
            cp.start()

    def drain(ins, outs, sems):
        for cp in copies(ins, outs, sems):
            cp.wait()

    return _Task(ps, [jax.ShapeDtypeStruct((3,) + p.shape[1:], p.dtype) for p in ps],
                 [_dma_sems(n, 3), _dma_sems(n, 3)], issue, drain)


def _chip_sum_call(p, r2, place_arr, *, name, tm=256):
    _, hr, cols = r2.shape
    tm = tm if hr % tm == 0 else hr
    nb = hr // tm

    def body(place_ref, p_ref, r_ref, o_ref):
        acc = p_ref[...].astype(F32)
        for j in range(3):
            acc = acc + r_ref[j].astype(F32)
        o_ref[...] = acc

    return pl.pallas_call(
        body,
        grid_spec=pltpu.PrefetchScalarGridSpec(
            num_scalar_prefetch=1, grid=(nb,),
            in_specs=[pl.BlockSpec((None, tm, cols), lambda i, pc: (pc[0], i, 0)),
                      pl.BlockSpec((3, tm, cols), lambda i, pc: (0, i, 0))],
            out_specs=pl.BlockSpec((tm, cols), lambda i, pc: (pc[1] * nb + i, 0))),
        out_shape=jax.ShapeDtypeStruct((2 * hr, cols), F32), name=name, compiler_params=_cparams(),
    )(place_arr, p, r2)


def _rs_gather_task(gs):
    n = len(gs)

    def copies(ins, outs, sems):
        send_sems, recv_sems = sems
        x, y, c, _ = _place()
        for k in range(n):
            rows = gs[k].shape[0]
            mine, theirs = _half(c, rows), _half(1 - c, rows)
            yield [pltpu.make_async_remote_copy(
                src_ref=ins[k].at[mine], dst_ref=outs[k].at[half],
                send_sem=send_sems.at[k], recv_sem=recv_sems.at[k],
                device_id=(x, y, 1 - c), device_id_type=MESH) for half in (mine, theirs)]

    def issue(ins, outs, sems):
        for outgoing, _ in copies(ins, outs, sems):
            outgoing.start()

    def drain(ins, outs, sems):
        for outgoing, incoming in copies(ins, outs, sems):
            incoming.wait_recv()
            outgoing.wait_send()

    return _Task(gs, [jax.ShapeDtypeStruct(g.shape, g.dtype) for g in gs],
                 [_dma_sems(n), _dma_sems(n)], issue, drain, aliases={k: k for k in range(n)})


def _adamw(w, g, m, v):
    m = ADAM_B1 * m + (1.0 - ADAM_B1) * g
    v = ADAM_B2 * v + (1.0 - ADAM_B2) * jnp.square(g)
    m_hat = m / (1.0 - ADAM_B1 ** ADAM_STEP)
    v_hat = v / (1.0 - ADAM_B2 ** ADAM_STEP)
    delta = -ADAM_LR * (m_hat / (jnp.sqrt(v_hat) + ADAM_EPS) + ADAM_WD * w)
    return delta, m, v


def _adamw_call(items, *, name, tm=256, task=None):
    n = len(items)
    cols = items[0][0].shape[1]
    tiles = [it[0].shape[0] // tm for it in items]
    steps = max(tiles)

    def body(*refs):
        i = pl.program_id(0)
        ins, outs = refs[:4 * n], refs[4 * n:]
        for k in range(n):
            def update(k=k):
                g = ins[4 * k + 1][...]
                res = _adamw(ins[4 * k][...], g, ins[4 * k + 2][...], ins[4 * k + 3][...])
                outs[4 * k][...] = g
                for j in range(3):
                    outs[4 * k + 1 + j][...] = res[j]
            if tiles[k] == steps:
                update()
            else:
                pl.when(i < tiles[k])(update)

    in_specs, out_specs, out_shape, args = [], [], [], []
    for it, t in zip(items, tiles):
        spec = pl.BlockSpec((tm, cols), lambda i, t=t: (jnp.minimum(i, t - 1), 0))
        in_specs += [spec] * 4
        out_specs += [spec] * 4
        out_shape += [jax.ShapeDtypeStruct(it[0].shape, F32)] * 4
        args += list(it)
    res = _pallas(body, grid=(steps,), in_specs=in_specs, out_specs=out_specs, out_shape=out_shape,
                  name=name, task=task)(*args)
    outs, extra = res if task is not None else (res, None)
    grouped = [tuple(outs[4 * k:4 * k + 4]) for k in range(n)]
    return (grouped, extra) if task is not None else grouped


def _adamw_cols_call(w, g_pad, m, v, *, name, tn=256):
    rows, cols = w.shape

    def body(w_ref, g_ref, m_ref, v_ref, go_ref, d_ref, mo_ref, vo_ref):
        g = g_ref[0:rows, :]
        d, mn, vn = _adamw(w_ref[...], g, m_ref[...], v_ref[...])
        go_ref[...] = g
        d_ref[...] = d
        mo_ref[...] = mn
        vo_ref[...] = vn

    spec = pl.BlockSpec((rows, tn), lambda j: (0, j))
    gspec = pl.BlockSpec((g_pad.shape[0], tn), lambda j: (0, j))
    return _pallas(body, grid=(cols // tn,), in_specs=[spec, gspec, spec, spec], out_specs=(spec,) * 4,
                   out_shape=(jax.ShapeDtypeStruct((rows, cols), F32),) * 4, name=name)(w, g_pad, m, v)


N_DEV = 8
SMALL_ROWS = 24
SMALL_LAYOUT = {
    "g_mix_pre": (0, 0, 1, D), "g_mix_post": (1, 0, 1, D), "g_mem_kv": (2, 0, 1, D), "g_mem_pre": (3, 0, 1, D),
    "g_mem_post": (4, 0, 1, D), "g_ff_pre": (5, 0, 1, D), "g_ff_post": (6, 0, 1, D),
    "g_fox_out": (7, 0, 1, D_GRP), "g_chk_out": (7, D_GRP, 1, D_GRP), "b_fgt": (8, 0, 1, 8),
    "rel_bias": (16, 0, 8, N_REL),
}
SMALL = list(SMALL_LAYOUT)


LOSS_ROW = 9


def _small_call(grads, ws, ms, vs, loss_blk, task, *, name):
    n = len(SMALL)
    t_in, t_out = len(task.arrays), len(task.out_shapes)

    def body(*refs):
        g_refs, w_refs, m_refs, v_refs = (refs[j * n:(j + 1) * n] for j in range(4))
        p = 4 * n
        loss_ref, tins = refs[p], refs[p + 1:p + 1 + t_in]
        p += 1 + t_in
        outs, loss_out, touts = refs[p:p + 4 * n], refs[p + 4 * n], refs[p + 4 * n + 1:p + 4 * n + 1 + t_out]
        p += 4 * n + 1 + t_out
        mine, slots, send_sems, recv_sems = refs[p:p + 4]
        tsems = refs[p + 4:]
        task.issue(tins, touts, tsems)
        x, y, c, _ = _place()
        me = 4 * x + 2 * y + c
        mine[...] = jnp.zeros_like(mine)
        for k, name_k in enumerate(SMALL):
            r, l, nr, nl = SMALL_LAYOUT[name_k]
            mine[r:r + nr, l:l + nl] = g_refs[k][0:nr, 0:nl]
        mine[LOSS_ROW:LOSS_ROW + 1, 0:128] = loss_ref[0:1, :]
        slots[me] = mine[...]
        peers = [(dx, dy, dc) for dx in (0, 1) for dy in (0, 1) for dc in (0, 1)][1:]
        cps = []
        for t, (dx, dy, dc) in enumerate(peers):
            px, py, pc = (x + dx) % 2, (y + dy) % 2, (c + dc) % 2
            cps.append(pltpu.make_async_remote_copy(
                src_ref=mine, dst_ref=slots.at[me], send_sem=send_sems.at[t], recv_sem=recv_sems.at[t],
                device_id=(px, py, pc), device_id_type=MESH))
            cps[-1].start()
        for t, (dx, dy, dc) in enumerate(peers):
            px, py, pc = (x + dx) % 2, (y + dy) % 2, (c + dc) % 2
            pltpu.make_async_remote_copy(
                src_ref=mine, dst_ref=slots.at[4 * px + 2 * py + pc], send_sem=send_sems.at[t],
                recv_sem=recv_sems.at[t], device_id=(px, py, pc), device_id_type=MESH).wait_recv()
        for cp in cps:
            cp.wait_send()
        total = slots[0]
        for j in range(1, N_DEV):
            total = total + slots[j]
        for k, name_k in enumerate(SMALL):
            r, l, nr, nl = SMALL_LAYOUT[name_k]
            g = total[r:r + nr, l:l + nl]
            d, mn, vn = _adamw(w_refs[k][...], g, m_refs[k][...], v_refs[k][...])
            for j, val in enumerate((g, d, mn, vn)):
                outs[4 * k + j][...] = val
        loss_out[...] = jnp.broadcast_to(total[LOSS_ROW:LOSS_ROW + 1, 0:128], loss_out.shape)
        task.drain(tins, touts, tsems)

    vm = pl.BlockSpec(memory_space=pltpu.VMEM)
    out_shape = [jax.ShapeDtypeStruct(ws[k].shape, F32) for k in SMALL for _ in range(4)]
    out_shape += [jax.ShapeDtypeStruct((8, 128), F32)] + list(task.out_shapes)
    res = pl.pallas_call(
        body, in_specs=[vm] * (4 * n + 1) + [ANY] * t_in, out_specs=[vm] * (4 * n + 1) + [ANY] * t_out,
        out_shape=out_shape,
        scratch_shapes=[pltpu.VMEM((SMALL_ROWS, D), F32), pltpu.VMEM((N_DEV, SMALL_ROWS, D), F32),
                        _dma_sems(N_DEV - 1), _dma_sems(N_DEV - 1)] + list(task.sems),
        input_output_aliases={4 * n + 1 + i: 4 * n + 1 + j for i, j in task.aliases.items()},
        name=name,
    )(*[d[k] for d in (grads, ws, ms, vs) for k in SMALL], loss_blk, *task.arrays)
    return ({k: tuple(res[4 * i:4 * i + 4]) for i, k in enumerate(SMALL)}, res[4 * n], list(res[4 * n + 1:]))


WEIGHTS = ["w_in", "b_fgt", "rel_bias", "g_fox_out", "g_chk_out", "w_out", "g_mix_pre", "g_mix_post", "g_mem_kv",
           "w_mq", "w_mk", "w_mv", "w_mo", "g_mem_pre", "g_mem_post", "w_ff1", "w_ff2", "g_ff_pre", "g_ff_post"]
BIG = ["w_in", "w_out", "w_mq", "w_mk", "w_mv", "w_mo", "w_ff1", "w_ff2"]


IN_SHARD = D_IN // N_CHIP
IN_PAD = 800


IN_PIECES = [(0, 0, 770), (800, 770, 766), (1566, 3072, 4), (1600, 3076, 4), (1604, 1536, 766), (2400, 2302, 770)]
PAD_ZEROS = [(800 * j + IN_SHARD, IN_PAD - IN_SHARD) for j in range(N_CHIP)]
ALL_ZEROS = [(D_IN, D_ALL - D_IN)]


def _reorder_rows_call(src, to_all, *, name, tn=256):
    rows, cols = src.shape
    zeros = ALL_ZEROS if to_all else PAD_ZEROS

    def body(s_ref, o_ref):
        for pad0, all0, cnt in IN_PIECES:
            s0, d0 = (pad0, all0) if to_all else (all0, pad0)
            o_ref[d0:d0 + cnt, :] = s_ref[s0:s0 + cnt, :]
        for z0, cnt in zeros:
            o_ref[z0:z0 + cnt, :] = jnp.zeros((cnt, tn), src.dtype)

    spec = pl.BlockSpec((rows, tn), lambda j: (0, j))
    return _pallas(body, grid=(cols // tn,), in_specs=[spec], out_specs=spec,
                   out_shape=jax.ShapeDtypeStruct((rows, cols), src.dtype), name=name)(src)


def kernel(x, mem, w_in, b_fgt, rel_bias, g_fox_out, g_chk_out, w_out, g_mix_pre, g_mix_post, g_mem_kv, w_mq, w_mk, w_mv, w_mo, g_mem_pre, g_mem_post, w_ff1, w_ff2, g_ff_pre, g_ff_post, loss_target, m_w_in, m_b_fgt, m_rel_bias, m_g_fox_out, m_g_chk_out, m_w_out, m_g_mix_pre, m_g_mix_post, m_g_mem_kv, m_w_mq, m_w_mk, m_w_mv, m_w_mo, m_g_mem_pre, m_g_mem_post, m_w_ff1, m_w_ff2, m_g_ff_pre, m_g_ff_post, v_w_in, v_b_fgt, v_rel_bias, v_g_fox_out, v_g_chk_out, v_w_out, v_g_mix_pre, v_g_mix_post, v_g_mem_kv, v_w_mq, v_w_mk, v_w_mv, v_w_mo, v_g_mem_pre, v_g_mem_post, v_w_ff1, v_w_ff2, v_g_ff_pre, v_g_ff_post):
    w = dict(w_in=w_in, b_fgt=b_fgt, rel_bias=rel_bias, g_fox_out=g_fox_out, g_chk_out=g_chk_out, w_out=w_out,
             g_mix_pre=g_mix_pre, g_mix_post=g_mix_post, g_mem_kv=g_mem_kv, w_mq=w_mq, w_mk=w_mk, w_mv=w_mv,
             w_mo=w_mo, g_mem_pre=g_mem_pre, g_mem_post=g_mem_post, w_ff1=w_ff1, w_ff2=w_ff2, g_ff_pre=g_ff_pre,
             g_ff_post=g_ff_post)
    m = dict(w_in=m_w_in, b_fgt=m_b_fgt, rel_bias=m_rel_bias, g_fox_out=m_g_fox_out, g_chk_out=m_g_chk_out,
             w_out=m_w_out, g_mix_pre=m_g_mix_pre, g_mix_post=m_g_mix_post, g_mem_kv=m_g_mem_kv, w_mq=m_w_mq,
             w_mk=m_w_mk, w_mv=m_w_mv, w_mo=m_w_mo, g_mem_pre=m_g_mem_pre, g_mem_post=m_g_mem_post,
             w_ff1=m_w_ff1, w_ff2=m_w_ff2, g_ff_pre=m_g_ff_pre, g_ff_post=m_g_ff_post)
    v = dict(w_in=v_w_in, b_fgt=v_b_fgt, rel_bias=v_rel_bias, g_fox_out=v_g_fox_out, g_chk_out=v_g_chk_out,
             w_out=v_w_out, g_mix_pre=v_g_mix_pre, g_mix_post=v_g_mix_post, g_mem_kv=v_g_mem_kv, w_mq=v_w_mq,
             w_mk=v_w_mk, w_mv=v_w_mv, w_mo=v_w_mo, g_mem_pre=v_g_mem_pre, g_mem_post=v_g_mem_post,
             w_ff1=v_w_ff1, w_ff2=v_w_ff2, g_ff_pre=v_g_ff_pre, g_ff_post=v_g_ff_post)

    def rows(d, k):
        return d[k][0] if k == "rel_bias" else d[k]

    xs, mems, target = x[0], mem[0], loss_target[0]
    S = xs.shape[0]
    sp = {k: rows(w, k) for k in SMALL}
    b_pad = jnp.pad(sp["b_fgt"], ((0, 0), (0, 120)))
    chip = 2 * lax.axis_index("x") + lax.axis_index("y")
    chip_arr = jnp.reshape(chip, (1,)).astype(jnp.int32)
    c_arr = jnp.reshape(lax.axis_index("c"), (1,)).astype(jnp.int32)
    place_arr = jnp.concatenate([chip_arr, c_arr])
    w_in_t, m_in_t, v_in_t = w["w_in"][0].T, m["w_in"][0].T, v["w_in"][0].T
    slab = {"w_in": _cast_slab_call(w_in_t, chip_arr, name="cast_w_in", pad_rows=IN_PAD - IN_SHARD)}

    def gather_ici(names):
        return _ag_ici_task([slab[k] for k in names])

    def pair_add(k, d, r1):
        return _pair_add_call(d, r1, c_arr, name="rs_pair_add_" + k)

    rest, (g_in,) = _cast_slabs_call([w[k][0] for k in BIG[1:]], chip_arr, name="cast_rest",
                                     task=gather_ici(["w_in"]))
    slab.update(zip(BIG[1:], rest))
    h1, (g_in,) = _rms_fwd_call(xs, sp["g_mix_pre"], name="rms_mix_pre", task=_ag_d2d_task([g_in]))
    w_all_t = _reorder_rows_call(g_in.reshape(N_CHIP * IN_PAD, D), True, name="w_in_rows")
    proj, (g_out, g_mq) = _mm_nt(h1, w_all_t, "plain", rows=(0, 3072), name="mm_proj",
                                 task=gather_ici(["w_out", "w_mq"]))
    fl_raw = _mm_nt(h1, w_all_t, "plain", rows=(3072, 128), name="mm_gate", out_dtype=F32, tn=128)
    c_rep, c_t = _fox_prep_call(fl_raw, b_pad, name="fox_prep")
    bias = _chk_bias_call(_rel_table_to_g(sp["rel_bias"]), name="chk_bias")
    mid = ["w_mk", "w_mv", "w_mo", "w_ff1"]
    (yf, lse), got = _fox_fwd_call(proj, c_rep, c_t, name="fox_fwd",
                                   task=_merge_tasks([gather_ici(mid), _ag_d2d_task([g_out, g_mq])]))
    g_mid, (g_out, g_mq) = got[:4], got[4:]
    yc, got = _chk_fwd_call(proj, bias, name="chk_fwd",
                            task=_merge_tasks([gather_ici(["w_ff2"]), _ag_d2d_task(g_mid)]))
    g_ff2, (g_mk, g_mv, g_mo, g_ff1) = got[0], got[1:]
    yn = _mix_norm_fwd_call(yf, yc, sp["g_fox_out"], sp["g_chk_out"], name="mix_norm_fwd")
    z, (g_ff2,) = _mm_nn(yn, g_out, "rows", name="mm_out", out_dtype=F32, task=_ag_d2d_task([g_ff2]))
    x1, h2 = _post_pre_call(xs, z, sp["g_mix_post"], sp["g_mem_pre"], name="post_mix")
    memn = _rms_fwd_call(mems, sp["g_mem_kv"], name="rms_mem_kv")
    q2 = _mm_nn(h2, g_mq, "rows", name="mm_mq")
    k2 = _mm_nn(memn, g_mk, "rows", name="mm_mk")
    v2 = _mm_nn(memn, g_mv, "rows", name="mm_mv")
    o2 = _mem_fwd_call(q2, k2, v2, name="mem_fwd")
    y2 = _mm_nn(o2, g_mo, "rows", name="mm_mo", out_dtype=F32)
    x2, h3 = _post_pre_call(x1, y2, sp["g_mem_post"], sp["g_ff_pre"], name="post_mem")
    act, relu = _mm_nn(h3, g_ff1, "cols", name="mm_ff1", epi="relu2")
    y3 = _mm_nn(act, g_ff2, "rows", name="mm_ff2", out_dtype=F32, tm=1024)
    loss_blk, dx3, dy3, dg_ff_post = _final_call(x2, y3, sp["g_ff_post"], target, name="final")

    d_ff2 = _mm_tn(act, dy3, name="mm_dff2", tk=512, tn=1024).reshape(N_CHIP, D_FF // N_CHIP, D)
    du, (r1,) = _mm_nt(dy3, g_ff2, "rows", name="mm_du", mul2r=relu, task=_rs_pair_task([d_ff2]))
    p_ff2 = pair_add("w_ff2", d_ff2, r1)
    d_ff1 = _mm_tn(h3, du, name="mm_dff1", cols4=True)
    dh3, (r1,) = _mm_nt(du, g_ff1, "cols", name="mm_dh3", out_dtype=F32, tm=1024, task=_rs_pair_task([d_ff1]))
    p_ff1 = pair_add("w_ff1", d_ff1, r1)
    dx2, dy2, dg_ff_pre, dg_mem_post = _bwd_mid_call(dx3, x2, dh3, sp["g_ff_pre"], y2, sp["g_mem_post"], name="bwd_ff")
    d_mo = _mm_tn(o2, dy2, name="mm_dmo").reshape(N_CHIP, D // N_CHIP, D)
    do2 = _mm_nt(dy2, g_mo, "rows", name="mm_do2")
    dq2, dk2, dv2 = _mem_bwd_call(q2, k2, v2, do2, name="mem_bwd")
    d_mq = _mm_tn(h2, dq2, name="mm_dmq").reshape(N_CHIP, D // N_CHIP, D)
    dh2 = _mm_nt(dq2, g_mq, "rows", name="mm_dh2", out_dtype=F32)
    d_mk = _mm_tn(memn, dk2, name="mm_dmk").reshape(N_CHIP, D // N_CHIP, D)
    d_mv = _mm_tn(memn, dv2, name="mm_dmv").reshape(N_CHIP, D // N_CHIP, D)
    dmn_k = _mm_nt(dk2, g_mk, "rows", name="mm_dmemk", out_dtype=F32)
    dmn_v = _mm_nt(dv2, g_mv, "rows", name="mm_dmemv", out_dtype=F32)
    dg_mem_kv = _gain_grad_call(mems, sp["g_mem_kv"], dmn_k, dmn_v, name="gain_mem_kv")
    dx1, dz, dg_mem_pre, dg_mix_post = _bwd_mid_call(dx2, x1, dh2, sp["g_mem_pre"], z, sp["g_mix_post"], name="bwd_mem")
    d_out = _mm_tn(yn, dz, name="mm_dout").reshape(N_CHIP, D // N_CHIP, D)
    late = ["w_mo", "w_mq", "w_mk", "w_mv", "w_out"]
    d_late = [d_mo, d_mq, d_mk, d_mv, d_out]
    dyn, r1_late = _mm_nt(dz, g_out, "rows", name="mm_dyn", out_dtype=F32, task=_rs_pair_task(d_late))
    p_late = [pair_add(k, d, r1) for k, d, r1 in zip(late, d_late, r1_late)]
    dof, doc, delta, dg_fox, dg_chk = _mix_norm_bwd_call(dyn, yf, yc, sp["g_fox_out"], sp["g_chk_out"], name="mix_norm_bwd")
    (dqf, dkf, dvf, dcq, dck), r2_ff = _fox_bwd_call(proj, dof, lse, delta, c_rep, c_t, name="fox_bwd",
                                                      task=_rs_chip_task([p_ff2, p_ff1]))
    (dqc, dkc, dvc, dgrev), r2_late = _chk_bwd_call(proj, doc, bias, name="chk_bwd", task=_rs_chip_task(p_late))
    first = ["w_ff2", "w_ff1"] + late
    f_first = [_chip_sum_call(p, r, place_arr, name="rs_chip_sum_" + k)
               for k, p, r in zip(first, [p_ff2, p_ff1] + p_late, r2_ff + r2_late)]
    dc8 = dcq[:, :, 0:2, :].transpose(0, 2, 1, 3).reshape(8, S) + dck[:, ::HEAD].T
    dc_rows = jnp.concatenate([dc8, jnp.zeros((120, S), F32)], axis=0)
    dfl, db_fgt = _fox_gate_bwd_call(dc_rows, fl_raw, b_pad, name="fox_gate_bwd")
    dproj = jnp.concatenate([dqf, dkf, dvf, dqc, dkc, dvc, dfl], axis=1)
    d_all_t, g_first = _mm_tn(dproj, h1, name="mm_dwin", tk=640, tn=1024, task=_rs_gather_task(f_first))
    grads = dict(zip(first, g_first))
    d_in = _reorder_rows_call(d_all_t, False, name="d_in_rows").reshape(N_CHIP, IN_PAD, D)
    delta_w, new_m, new_v = {}, {}, {}

    def adamw_items(names):
        return [(w[k][0], grads[k], m[k][0], v[k][0]) for k in names]

    upd_late, (r1,) = _adamw_call(adamw_items(late), name="adamw_late", tm=64, task=_rs_pair_task([d_in]))
    p_in = pair_add("w_in", d_in, r1)
    dh1, (r2_in,) = _mm_nn(dproj, w_all_t, "plain", name="mm_dh1", out_dtype=F32, tm=1024,
                           task=_rs_chip_task([p_in]))
    f_in = _chip_sum_call(p_in, r2_in, place_arr, name="rs_chip_sum_w_in")
    upd_ff = _adamw_call(adamw_items(first[:2]), name="adamw_ff")
    for k, res in zip(late + first[:2], upd_late + upd_ff):
        grads[k], delta_w[k], new_m[k], new_v[k] = res
    grad_x, dg_mix_pre = _bwd_last_call(dx1, xs, dh1, sp["g_mix_pre"], name="bwd_mix")

    small_g = {"g_mix_pre": dg_mix_pre, "g_mix_post": dg_mix_post, "g_mem_kv": dg_mem_kv, "g_mem_pre": dg_mem_pre,
               "g_mem_post": dg_mem_post, "g_ff_pre": dg_ff_pre, "g_ff_post": dg_ff_post, "g_fox_out": dg_fox,
               "g_chk_out": dg_chk, "b_fgt": db_fgt,
               "rel_bias": _g_to_rel_table(dgrev[:, 0:2, :].reshape(8, ROLL_W))}
    small, loss_out, (g_w_in,) = _small_call(
        small_g, sp, {k: rows(m, k) for k in SMALL}, {k: rows(v, k) for k in SMALL}, loss_blk,
        _rs_gather_task([f_in]), name="small_allreduce_adamw")
    loss = loss_out[0, 0]
    res = _adamw_cols_call(w_in_t, g_w_in, m_in_t, v_in_t, name="adamw_w_in")
    grads["w_in"], delta_w["w_in"], new_m["w_in"], new_v["w_in"] = (a.T for a in res)
    for k in SMALL:
        vals = small[k]
        if k == "rel_bias":
            vals = tuple(a[None] for a in vals)
        grads[k], delta_w[k], new_m[k], new_v[k] = vals

    def out(d, k):
        return d[k][None] if k in BIG else d[k]

    return (loss, grad_x[None], *[out(grads, k) for k in WEIGHTS], *[out(delta_w, k) for k in WEIGHTS],
            *[out(new_m, k) for k in WEIGHTS], *[out(new_v, k) for k in WEIGHTS])
```

```python
import functools

import jax
import jax.numpy as jnp
from jax import lax
from jax.experimental import pallas as pl
from jax.experimental.pallas import tpu as pltpu

F32 = jnp.float32
BF16 = jnp.bfloat16

D = 1024
HEAD = 64
N_PAIR = 4
D_GRP = 512
CHUNK = 64
LEFT = 8
MAX_REL = 128
N_REL = 2 * MAX_REL + 1
N_MEM = 256
MEM_HEADS = 4
MEM_HD = 256
D_FF = 4096
D_IN = 3080
D_ALL = 3200
EPS = 1e-6
TQ = 256
WIN = (LEFT + TQ // CHUNK) * CHUNK
PADK = LEFT * CHUNK
ROLL_W = 1024
NEG = -1e30
N_CHIP = 4
VMEM_LIMIT = 48 * 1024 * 1024

ADAM_LR = 0.001
ADAM_B1 = 0.9
ADAM_B2 = 0.999
ADAM_EPS = 1e-08
ADAM_WD = 0.01
ADAM_STEP = 10

MESH = pl.DeviceIdType.MESH


def _cparams():
    return pltpu.CompilerParams(vmem_limit_bytes=VMEM_LIMIT)


ANY = pl.BlockSpec(memory_space=pl.ANY)


class _Task:
    def __init__(self, arrays, out_shapes, sems, issue, drain, aliases=None):
        self.arrays, self.out_shapes, self.sems = list(arrays), list(out_shapes), list(sems)
        self.issue, self.drain, self.aliases = issue, drain, dict(aliases or {})


def _merge_tasks(tasks):
    tasks = [t for t in tasks if t is not None]
    if len(tasks) == 1:
        return tasks[0]
    cuts, a, o, s = [], 0, 0, 0
    aliases = {}
    for t in tasks:
        cuts.append((a, o, s))
        aliases.update({a + i: o + j for i, j in t.aliases.items()})
        a, o, s = a + len(t.arrays), o + len(t.out_shapes), s + len(t.sems)

    def part(fn_name):
        def run(ins, outs, sems):
            for t, (a0, o0, s0) in zip(tasks, cuts):
                getattr(t, fn_name)(ins[a0:a0 + len(t.arrays)], outs[o0:o0 + len(t.out_shapes)],
                                    sems[s0:s0 + len(t.sems)])
        return run

    return _Task([x for t in tasks for x in t.arrays], [x for t in tasks for x in t.out_shapes],
                 [x for t in tasks for x in t.sems], part("issue"), part("drain"), aliases)


def _pallas(body, *, grid, in_specs, out_specs, out_shape, name, scratch_shapes=(), task=None, prefetch=0):
    def make(kernel, i_specs, o_specs, o_shape, scratch, aliases):
        if prefetch:
            spec = pltpu.PrefetchScalarGridSpec(num_scalar_prefetch=prefetch, grid=grid, in_specs=i_specs,
                                                out_specs=o_specs, scratch_shapes=scratch)
            return pl.pallas_call(kernel, grid_spec=spec, out_shape=o_shape, input_output_aliases=aliases,
                                  name=name, compiler_params=_cparams())
        return pl.pallas_call(kernel, grid=grid, in_specs=i_specs, out_specs=o_specs, out_shape=o_shape,
                              scratch_shapes=scratch, input_output_aliases=aliases, name=name,
                              compiler_params=_cparams())

    if task is None:
        return make(body, list(in_specs), out_specs, out_shape, list(scratch_shapes), {})
    single = not isinstance(out_shape, (tuple, list))
    o_shapes = [out_shape] if single else list(out_shape)
    o_specs = [out_specs] if single else list(out_specs)
    n_in, n_out, n_scr = len(in_specs), len(o_shapes), len(scratch_shapes)
    t_in, t_out = len(task.arrays), len(task.out_shapes)

    def carried(*refs):
        cut = [prefetch, n_in, t_in, n_out, t_out, n_scr]
        parts, p = [], 0
        for c in cut:
            parts.append(refs[p:p + c])
            p += c
        scalars, ins, tins, outs, touts, scr = parts
        tsems = refs[p:]
        ids = [pl.program_id(a) for a in range(len(grid))]
        first = functools.reduce(jnp.logical_and, [i == 0 for i in ids])
        last = functools.reduce(jnp.logical_and, [i == g - 1 for i, g in zip(ids, grid)])

        @pl.when(first)
        def _():
            task.issue(tins, touts, tsems)
        body(*scalars, *ins, *outs, *scr)

        @pl.when(last)
        def _():
            task.drain(tins, touts, tsems)

    call = make(carried, list(in_specs) + [ANY] * t_in, o_specs + [ANY] * t_out,
                o_shapes + list(task.out_shapes), list(scratch_shapes) + list(task.sems),
                {prefetch + n_in + i: n_out + j for i, j in task.aliases.items()})

    def run(*args):
        res = call(*args, *task.arrays)
        outs = res[:n_out]
        return (outs[0] if single else tuple(outs)), list(res[n_out:])

    return run


def _dot(a, b):
    return jnp.dot(a, b, preferred_element_type=F32)


def _dot_nt(a, b):
    return lax.dot_general(a, b, (((1,), (1,)), ((), ())), preferred_element_type=F32)


def _dot_tn(a, b):
    return lax.dot_general(a, b, (((0,), (0,)), ((), ())), preferred_element_type=F32)


def _split3(x):
    hi = x.astype(BF16)
    r1 = x - hi.astype(F32)
    mid = r1.astype(BF16)
    lo = (r1 - mid.astype(F32)).astype(BF16)
    return hi, mid, lo


def _dot3(x, m01):
    hi, mid, lo = _split3(x)
    return _dot(hi, m01) + _dot(mid, m01) + _dot(lo, m01)


def _dot3_l(m01, x):
    hi, mid, lo = _split3(x)
    return _dot(m01, hi) + _dot(m01, mid) + _dot(m01, lo)


def _mm_nn(a, b, kind, *, name, out_dtype=BF16, tm=2048, tn=512, epi=None, task=None):
    M, K = a.shape
    if kind == "plain":
        N = b.shape[1]
        b_spec = pl.BlockSpec((K, tn), lambda m, n: (0, n))
    elif kind == "rows":
        N = b.shape[2]
        b_spec = pl.BlockSpec((N_CHIP, K // N_CHIP, tn), lambda m, n: (0, 0, n))
    else:
        nq = b.shape[2]
        N = N_CHIP * nq
        per = nq // tn
        b_spec = pl.BlockSpec((None, K, tn), lambda m, n: (n // per, 0, n % per))
    tm = min(tm, M)
    kq = K // N_CHIP

    def body(a_ref, b_ref, *o_refs):
        if kind == "rows":
            acc = _dot(a_ref[:, 0:kq], b_ref[0])
            for j in range(1, N_CHIP):
                acc += _dot(a_ref[:, j * kq:(j + 1) * kq], b_ref[j])
        else:
            acc = _dot(a_ref[...], b_ref[...])
        if epi == "relu2":
            r = jnp.maximum(acc, 0.0)
            o_refs[0][...] = (r * r).astype(BF16)
            o_refs[1][...] = r.astype(BF16)
        else:
            o_refs[0][...] = acc.astype(out_dtype)

    o_spec = pl.BlockSpec((tm, tn), lambda m, n: (m, n))
    if epi == "relu2":
        out_shape = (jax.ShapeDtypeStruct((M, N), BF16), jax.ShapeDtypeStruct((M, N), BF16))
        out_specs = (o_spec, o_spec)
    else:
        out_shape = jax.ShapeDtypeStruct((M, N), out_dtype)
        out_specs = o_spec
    return _pallas(
        body, grid=(M // tm, N // tn),
        in_specs=[pl.BlockSpec((tm, K), lambda m, n: (m, 0)), b_spec],
        out_specs=out_specs, out_shape=out_shape, name=name, task=task,
    )(a, b)


def _mm_nt(a, b, kind, *, name, out_dtype=BF16, tm=2048, tn=512, mul2r=None, task=None, rows=None):
    M, K = a.shape
    if kind == "plain":
        first, N = rows if rows is not None else (0, b.shape[0])
        n0 = first // tn
        b_spec = pl.BlockSpec((tn, K), lambda m, n: (n0 + n, 0))
    elif kind == "rows":
        nq = b.shape[1]
        N = N_CHIP * nq
        tn = min(tn, nq)
        per = nq // tn
        b_spec = pl.BlockSpec((None, tn, K), lambda m, n: (n // per, n % per, 0))
    else:
        N = b.shape[1]
        b_spec = pl.BlockSpec((N_CHIP, tn, K // N_CHIP), lambda m, n: (0, n, 0))
    tm = min(tm, M)
    kq = K // N_CHIP

    def body(a_ref, b_ref, *rest):
        o_ref = rest[-1]
        if kind == "cols":
            acc = _dot_nt(a_ref[:, 0:kq], b_ref[0])
            for j in range(1, N_CHIP):
                acc += _dot_nt(a_ref[:, j * kq:(j + 1) * kq], b_ref[j])
        else:
            acc = _dot_nt(a_ref[...], b_ref[...])
        if mul2r is not None:
            acc = acc * (2.0 * rest[0][...].astype(F32))
        o_ref[...] = acc.astype(out_dtype)

    in_specs = [pl.BlockSpec((tm, K), lambda m, n: (m, 0)), b_spec]
    args = [a, b]
    if mul2r is not None:
        in_specs.append(pl.BlockSpec((tm, tn), lambda m, n: (m, n)))
        args.append(mul2r)
    return _pallas(
        body, grid=(M // tm, N // tn), in_specs=in_specs,
        out_specs=pl.BlockSpec((tm, tn), lambda m, n: (m, n)),
        out_shape=jax.ShapeDtypeStruct((M, N), out_dtype), name=name, task=task,
    )(*args)


def _mm_tn(a, b, *, name, out_dtype=BF16, tk=1024, tn=512, cols4=False, task=None):
    M, K1 = a.shape
    N = b.shape[1]
    tk = min(tk, K1)
    tn = min(tn, N)

    def body(a_ref, b_ref, o_ref):
        o_ref[...] = _dot_tn(a_ref[...], b_ref[...]).astype(out_dtype)

    if cols4:
        per = (N // N_CHIP) // tn
        out_shape = jax.ShapeDtypeStruct((N_CHIP, K1, N // N_CHIP), out_dtype)
        o_spec = pl.BlockSpec((None, tk, tn), lambda k, n: (n // per, k, n % per))
    else:
        out_shape = jax.ShapeDtypeStruct((K1, N), out_dtype)
        o_spec = pl.BlockSpec((tk, tn), lambda k, n: (k, n))
    return _pallas(
        body, grid=(K1 // tk, N // tn),
        in_specs=[pl.BlockSpec((M, tk), lambda k, n: (0, k)), pl.BlockSpec((M, tn), lambda k, n: (0, n))],
        out_specs=o_spec, out_shape=out_shape, name=name, task=task,
    )(a, b)


def _rms(x, g):
    r = lax.rsqrt(jnp.mean(x * x, axis=-1, keepdims=True) + EPS)
    return x * r * g


def _rms_bwd(x, g, dy):
    r = lax.rsqrt(jnp.mean(x * x, axis=-1, keepdims=True) + EPS)
    xh = x * r
    dg = jnp.sum(dy * xh, axis=0, keepdims=True)
    dxh = dy * g
    dx = r * (dxh - xh * jnp.mean(dxh * xh, axis=-1, keepdims=True))
    return dx, dg


def _row_spec(tm, n):
    return pl.BlockSpec((tm, n), lambda i: (i, 0))


def _vec_spec(n):
    return pl.BlockSpec((1, n), lambda i: (0, 0))


def _acc_spec(n):
    return pl.BlockSpec((8, n), lambda i: (0, 0))


def _acc_add(ref, row, i):
    @pl.when(i == 0)
    def _():
        ref[...] = jnp.zeros_like(ref)
    ref[0:1, :] += row


def _rms_fwd_call(x, g, *, name, tm=256, task=None):
    M, n = x.shape
    tm = min(tm, M)

    def body(x_ref, g_ref, h_ref):
        h_ref[...] = _rms(x_ref[...], g_ref[...]).astype(BF16)

    return _pallas(
        body, grid=(M // tm,), in_specs=[_row_spec(tm, n), _vec_spec(n)], out_specs=_row_spec(tm, n),
        out_shape=jax.ShapeDtypeStruct((M, n), BF16), name=name, task=task,
    )(x, g)


def _post_pre_call(xres, z, g_post, g_pre, *, name, tm=256):
    M, n = xres.shape

    def body(x_ref, z_ref, gp_ref, gn_ref, xo_ref, h_ref):
        xn = x_ref[...] + _rms(z_ref[...], gp_ref[...])
        xo_ref[...] = xn
        h_ref[...] = _rms(xn, gn_ref[...]).astype(BF16)

    return pl.pallas_call(
        body, grid=(M // tm,),
        in_specs=[_row_spec(tm, n), _row_spec(tm, n), _vec_spec(n), _vec_spec(n)],
        out_specs=(_row_spec(tm, n), _row_spec(tm, n)),
        out_shape=(jax.ShapeDtypeStruct((M, n), F32), jax.ShapeDtypeStruct((M, n), BF16)),
        name=name, compiler_params=_cparams(),
    )(xres, z, g_post, g_pre)


def _final_call(x2, y3, g_post, target, *, name, tm=256):
    M, n = x2.shape

    def body(x_ref, y_ref, g_ref, t_ref, loss_ref, dx_ref, dy_ref, dg_ref):
        i = pl.program_id(0)
        y = y_ref[...]
        g = g_ref[...]
        diff = x_ref[...] + _rms(y, g) - t_ref[...]
        part = 0.5 * jnp.sum(jnp.sum(diff * diff, axis=1, keepdims=True), axis=0, keepdims=True) / n

        @pl.when(i == 0)
        def _():
            loss_ref[...] = jnp.zeros_like(loss_ref)
        loss_ref[...] += jnp.broadcast_to(part, loss_ref.shape)
        dx = diff / n
        dx_ref[...] = dx
        dy, dg = _rms_bwd(y, g, dx)
        dy_ref[...] = dy.astype(BF16)
        _acc_add(dg_ref, dg, i)

    return pl.pallas_call(
        body, grid=(M // tm,),
        in_specs=[_row_spec(tm, n), _row_spec(tm, n), _vec_spec(n), _row_spec(tm, n)],
        out_specs=(pl.BlockSpec((8, 128), lambda i: (0, 0)), _row_spec(tm, n), _row_spec(tm, n), _acc_spec(n)),
        out_shape=(jax.ShapeDtypeStruct((8, 128), F32), jax.ShapeDtypeStruct((M, n), F32),
                   jax.ShapeDtypeStruct((M, n), BF16), jax.ShapeDtypeStruct((8, n), F32)),
        name=name, compiler_params=_cparams(),
    )(x2, y3, g_post, target)


def _bwd_mid_call(dx_in, x, dh, g_pre, y, g_post, *, name, tm=256):
    M, n = x.shape

    def body(dxi_ref, x_ref, dh_ref, gpre_ref, y_ref, gpost_ref, dx_ref, dy_ref, dgpre_ref, dgpost_ref):
        i = pl.program_id(0)
        d1, dg1 = _rms_bwd(x_ref[...], gpre_ref[...], dh_ref[...])
        dx = dxi_ref[...] + d1
        dx_ref[...] = dx
        dy, dg2 = _rms_bwd(y_ref[...], gpost_ref[...], dx)
        dy_ref[...] = dy.astype(BF16)
        _acc_add(dgpre_ref, dg1, i)
        _acc_add(dgpost_ref, dg2, i)

    return pl.pallas_call(
        body, grid=(M // tm,),
        in_specs=[_row_spec(tm, n), _row_spec(tm, n), _row_spec(tm, n), _vec_spec(n), _row_spec(tm, n), _vec_spec(n)],
        out_specs=(_row_spec(tm, n), _row_spec(tm, n), _acc_spec(n), _acc_spec(n)),
        out_shape=(jax.ShapeDtypeStruct((M, n), F32), jax.ShapeDtypeStruct((M, n), BF16),
                   jax.ShapeDtypeStruct((8, n), F32), jax.ShapeDtypeStruct((8, n), F32)),
        name=name, compiler_params=_cparams(),
    )(dx_in, x, dh, g_pre, y, g_post)


def _bwd_last_call(dx_in, x, dh, g_pre, *, name, tm=256, task=None):
    M, n = x.shape

    def body(dxi_ref, x_ref, dh_ref, g_ref, dx_ref, dg_ref):
        i = pl.program_id(0)
        d1, dg1 = _rms_bwd(x_ref[...], g_ref[...], dh_ref[...])
        dx_ref[...] = dxi_ref[...] + d1
        _acc_add(dg_ref, dg1, i)

    return _pallas(
        body, grid=(M // tm,),
        in_specs=[_row_spec(tm, n), _row_spec(tm, n), _row_spec(tm, n), _vec_spec(n)],
        out_specs=(_row_spec(tm, n), _acc_spec(n)),
        out_shape=(jax.ShapeDtypeStruct((M, n), F32), jax.ShapeDtypeStruct((8, n), F32)),
        name=name, task=task,
    )(dx_in, x, dh, g_pre)


def _gain_grad_call(x, g, dy_a, dy_b, *, name):
    M, n = x.shape

    def body(x_ref, g_ref, a_ref, b_ref, dg_ref):
        _, dg = _rms_bwd(x_ref[...], g_ref[...], a_ref[...] + b_ref[...])
        dg_ref[...] = jnp.zeros_like(dg_ref)
        dg_ref[0:1, :] = dg

    return pl.pallas_call(
        body, grid=(1,),
        in_specs=[_row_spec(M, n), _vec_spec(n), _row_spec(M, n), _row_spec(M, n)],
        out_specs=_acc_spec(n), out_shape=jax.ShapeDtypeStruct((8, n), F32),
        name=name, compiler_params=_cparams(),
    )(x, g, dy_a, dy_b)


def _head_group_matrix():
    a = lax.broadcasted_iota(jnp.int32, (D_GRP, D_GRP), 0) // HEAD
    b = lax.broadcasted_iota(jnp.int32, (D_GRP, D_GRP), 1) // HEAD
    return jnp.where(a == b, 1.0, 0.0).astype(BF16)


def _mix_norm_fwd_call(yf, yc, gf, gc, *, name, tm=256):
    M = yf.shape[0]

    def body(yf_ref, yc_ref, gf_ref, gc_ref, o_ref):
        o_ref[:, 0:D_GRP] = _rms(yf_ref[...], gf_ref[...]).astype(BF16)
        o_ref[:, D_GRP:D] = _rms(yc_ref[...], gc_ref[...]).astype(BF16)

    return pl.pallas_call(
        body, grid=(M // tm,),
        in_specs=[_row_spec(tm, D_GRP), _row_spec(tm, D_GRP), _vec_spec(D_GRP), _vec_spec(D_GRP)],
        out_specs=_row_spec(tm, D), out_shape=jax.ShapeDtypeStruct((M, D), BF16),
        name=name, compiler_params=_cparams(),
    )(yf, yc, gf, gc)


def _mix_norm_bwd_call(dyn, yf, yc, gf, gc, *, name, tm=TQ):
    M = yf.shape[0]

    def body(dyn_ref, yf_ref, yc_ref, gf_ref, gc_ref, dof_ref, doc_ref, delta_ref, dgf_ref, dgc_ref):
        i = pl.program_id(0)
        yf_ = yf_ref[...]
        dof, dgf = _rms_bwd(yf_, gf_ref[...], dyn_ref[:, 0:D_GRP])
        doc, dgc = _rms_bwd(yc_ref[...], gc_ref[...], dyn_ref[:, D_GRP:D])
        dof_b = dof.astype(BF16)
        dof_ref[...] = dof_b
        doc_ref[...] = doc.astype(BF16)
        prod = dof_b.astype(F32) * yf_
        hi = prod.astype(BF16)
        lo = (prod - hi.astype(F32)).astype(BF16)
        grp = _head_group_matrix()
        delta_ref[...] = (_dot(hi, grp) + _dot(lo, grp)).T
        _acc_add(dgf_ref, dgf, i)
        _acc_add(dgc_ref, dgc, i)

    return pl.pallas_call(
        body, grid=(M // tm,),
        in_specs=[_row_spec(tm, D), _row_spec(tm, D_GRP), _row_spec(tm, D_GRP), _vec_spec(D_GRP), _vec_spec(D_GRP)],
        out_specs=(_row_spec(tm, D_GRP), _row_spec(tm, D_GRP), pl.BlockSpec((None, D_GRP, tm), lambda i: (i, 0, 0)),
                   _acc_spec(D_GRP), _acc_spec(D_GRP)),
        out_shape=(jax.ShapeDtypeStruct((M, D_GRP), BF16), jax.ShapeDtypeStruct((M, D_GRP), BF16),
                   jax.ShapeDtypeStruct((M // tm, D_GRP, tm), F32), jax.ShapeDtypeStruct((8, D_GRP), F32),
                   jax.ShapeDtypeStruct((8, D_GRP), F32)),
        name=name, compiler_params=_cparams(),
    )(dyn, yf, yc, gf, gc)


def _tri(n, lower_incl):
    a = lax.broadcasted_iota(jnp.int32, (n, n), 0)
    b = lax.broadcasted_iota(jnp.int32, (n, n), 1)
    return jnp.where(a >= b, 1.0, 0.0).astype(BF16) if lower_incl else jnp.where(a <= b, 1.0, 0.0).astype(BF16)


def _fox_prep_call(fl_raw, b_pad, *, name):
    S = fl_raw.shape[0]
    nb = S // TQ

    def body(fl_ref, b_ref, crep_ref, ct_ref, carry_ref):
        i = pl.program_id(0)

        @pl.when(i == 0)
        def _():
            carry_ref[...] = jnp.zeros_like(carry_ref)
        logf = jax.nn.log_sigmoid(fl_ref[...] + b_ref[...])
        cb = _dot3_l(_tri(TQ, True), logf) + carry_ref[0:1, :]
        carry_ref[0:1, :] = cb[TQ - 1:TQ, :]
        a = lax.broadcasted_iota(jnp.int32, (128, D_GRP), 0)
        b = lax.broadcasted_iota(jnp.int32, (128, D_GRP), 1) // HEAD
        expand = jnp.where(a == b, 1.0, 0.0).astype(BF16)
        crep = _dot3(cb, expand)
        crep_ref[...] = crep
        ct_ref[...] = crep.T

    return pl.pallas_call(
        body, grid=(nb,),
        in_specs=[_row_spec(TQ, 128), _vec_spec(128)],
        out_specs=(_row_spec(TQ, D_GRP), pl.BlockSpec((None, D_GRP, TQ), lambda i: (i, 0, 0))),
        out_shape=(jax.ShapeDtypeStruct((S, D_GRP), F32), jax.ShapeDtypeStruct((nb, D_GRP, TQ), F32)),
        scratch_shapes=[pltpu.VMEM((8, 128), F32)],
        name=name, compiler_params=_cparams(),
    )(fl_raw, b_pad)


def _lane_masks():
    lane = lax.broadcasted_iota(jnp.int32, (1, 128), 1)
    return lane < HEAD, lane >= HEAD


def _fox_fwd_call(proj, c_rep, c_t, *, name, task=None):
    S = proj.shape[0]
    nq = S // TQ
    scale = HEAD ** -0.5

    def body(q_ref, k_ref, v_ref, c_ref, ct_ref, o_ref, lse_ref):
        i = pl.program_id(1)
        m_lo, m_hi = _lane_masks()
        masks = (m_lo, m_hi)
        q = q_ref[...]
        qm = [jnp.where(mk, q, jnp.zeros_like(q)) for mk in masks]
        cq = c_ref[...]
        cqh = [cq[:, 0:1], cq[:, HEAD:HEAD + 1]]
        row = lax.broadcasted_iota(jnp.int32, (TQ, TQ), 0)
        col = lax.broadcasted_iota(jnp.int32, (TQ, TQ), 1)

        def scores(j):
            start = pl.multiple_of(j * TQ, TQ)
            k = k_ref[pl.ds(start, TQ), :]
            ct = ct_ref[j]
            return tuple(_dot_nt(qm[h], k) * scale + (cqh[h] - ct[HEAD * h:HEAD * h + 1, :]) for h in range(2))

        def update(j, ss, state, masked):
            ms, ls, acc = state
            start = pl.multiple_of(j * TQ, TQ)
            v = v_ref[pl.ds(start, TQ), :]
            new_m, new_l, pv, alpha_l = [], [], [], []
            for h in range(2):
                s = ss[h]
                if masked:
                    s = jnp.where(row >= col, s, NEG)
                mn = jnp.maximum(ms[h], jnp.max(s, axis=1, keepdims=True))
                alpha = jnp.exp(ms[h] - mn)
                p = jnp.exp(s - mn)
                new_l.append(alpha * ls[h] + jnp.sum(p, axis=1, keepdims=True))
                new_m.append(mn)
                alpha_l.append(alpha)
                pv.append(_dot(p.astype(BF16), jnp.where(masks[h], v, jnp.zeros_like(v))))
            alpha_lane = jnp.where(m_lo, alpha_l[0], alpha_l[1])
            acc = acc * alpha_lane + pv[0] + pv[1]
            return (tuple(new_m), tuple(new_l), acc)

        def step(j, carry):
            ss, state = carry
            return (scores(j + 1), update(j, ss, state, False))

        init = ((jnp.full((TQ, 1), NEG, F32),) * 2, (jnp.zeros((TQ, 1), F32),) * 2, jnp.zeros((TQ, 128), F32))
        ss, state = lax.fori_loop(0, i, step, (scores(0), init))
        ms, ls, acc = update(i, ss, state, True)
        l_lane = jnp.where(m_lo, ls[0], ls[1])
        o_ref[...] = acc / l_lane
        lse_ref[...] = jnp.where(m_lo, ms[0] + jnp.log(ls[0]), ms[1] + jnp.log(ls[1])).T

    return _pallas(
        body, grid=(N_PAIR, nq),
        in_specs=[pl.BlockSpec((TQ, 128), lambda p, i: (i, p)),
                  pl.BlockSpec((S, 128), lambda p, i: (0, N_PAIR + p)),
                  pl.BlockSpec((S, 128), lambda p, i: (0, 2 * N_PAIR + p)),
                  pl.BlockSpec((TQ, 128), lambda p, i: (i, p)),
                  pl.BlockSpec((nq, 128, TQ), lambda p, i: (0, p, 0))],
        out_specs=(pl.BlockSpec((TQ, 128), lambda p, i: (i, p)), pl.BlockSpec((None, 128, TQ), lambda p, i: (i, p, 0))),
        out_shape=(jax.ShapeDtypeStruct((S, D_GRP), F32), jax.ShapeDtypeStruct((nq, D_GRP, TQ), F32)),
        name=name, task=task,
    )(proj, proj, proj, c_rep, c_t)


def _fox_bwd_call(proj, do, lse_t, delta_t, c_rep, c_t, *, name, task=None):
    S = proj.shape[0]
    nq = S // TQ
    scale = HEAD ** -0.5

    def body(q_ref, k_ref, v_ref, do_ref, lse_ref, dl_ref, ck_ref, ct_ref,
             dq_ref, dk_ref, dv_ref, dcq_ref, dck_ref, dqa_ref):
        j = pl.program_id(1)
        m_lo, m_hi = _lane_masks()
        masks = (m_lo, m_hi)

        @pl.when(j == 0)
        def _():
            dqa_ref[...] = jnp.zeros_like(dqa_ref)
            dcq_ref[...] = jnp.zeros_like(dcq_ref)
        k = k_ref[...]
        v = v_ref[...]
        km = [jnp.where(mk, k, jnp.zeros_like(k)) for mk in masks]
        ck = ck_ref[...]
        krow = lax.broadcasted_iota(jnp.int32, (TQ, TQ), 0)
        qcol = lax.broadcasted_iota(jnp.int32, (TQ, TQ), 1)

        def probs(i):
            start = pl.multiple_of(i * TQ, TQ)
            q = q_ref[pl.ds(start, TQ), :]
            do = do_ref[pl.ds(start, TQ), :]
            lse = lse_ref[i]
            cq = ct_ref[i]
            out = []
            for h in range(2):
                lo = HEAD * h
                qm = jnp.where(masks[h], q, jnp.zeros_like(q))
                dom = jnp.where(masks[h], do, jnp.zeros_like(do))
                st = _dot_nt(k, qm) * scale + (cq[lo:lo + 1, :] - ck[:, lo:lo + 1])
                out.append((jnp.exp(st - lse[lo:lo + 1, :]), _dot_nt(v, dom)))
            return tuple(out)

        def update(i, pd, carry, masked):
            dk, dv, dck = carry
            start = pl.multiple_of(i * TQ, TQ)
            q = q_ref[pl.ds(start, TQ), :]
            do = do_ref[pl.ds(start, TQ), :]
            dl = dl_ref[i]
            dq = jnp.zeros((TQ, 128), F32)
            new_dck = []
            for h in range(2):
                lo = HEAD * h
                qm = jnp.where(masks[h], q, jnp.zeros_like(q))
                dom = jnp.where(masks[h], do, jnp.zeros_like(do))
                pt, dpt = pd[h]
                if masked:
                    pt = jnp.where(qcol >= krow, pt, 0.0)
                dst = pt * (dpt - dl[lo:lo + 1, :])
                dcq_ref[i, h:h + 1, :] += jnp.sum(dst, axis=0, keepdims=True)
                new_dck.append(dck[h] + jnp.sum(dst, axis=1, keepdims=True))
                dsb = (dst * scale).astype(BF16)
                dv = dv + _dot(pt.astype(BF16), dom)
                dk = dk + _dot(dsb, qm)
                dq = dq + _dot_tn(dsb, km[h])
            dqa_ref[pl.ds(start, TQ), :] += dq
            return (dk, dv, tuple(new_dck))

        def step(i, carry):
            pd, sums = carry
            return (probs(jnp.minimum(i + 1, nq - 1)), update(i, pd, sums, False))

        init = (jnp.zeros((TQ, 128), F32), jnp.zeros((TQ, 128), F32), (jnp.zeros((TQ, 1), F32),) * 2)
        first = probs(j)
        second = probs(jnp.minimum(j + 1, nq - 1))
        _, (dk, dv, dck) = lax.fori_loop(j + 1, nq, step, (second, update(j, first, init, True)))
        dk_ref[...] = dk.astype(BF16)
        dv_ref[...] = dv.astype(BF16)
        dck_ref[...] = -jnp.where(m_lo, dck[0], dck[1])

        @pl.when(j == nq - 1)
        def _():
            dq_ref[...] = dqa_ref[...].astype(BF16)

    res = lambda p, j: (0, p)
    stat = pl.BlockSpec((nq, 128, TQ), lambda p, j: (0, p, 0))
    blk = pl.BlockSpec((TQ, 128), lambda p, j: (j, p))
    return _pallas(
        body, grid=(N_PAIR, nq), task=task,
        in_specs=[pl.BlockSpec((S, 128), res),
                  pl.BlockSpec((TQ, 128), lambda p, j: (j, N_PAIR + p)),
                  pl.BlockSpec((TQ, 128), lambda p, j: (j, 2 * N_PAIR + p)),
                  pl.BlockSpec((S, 128), res), stat, stat, blk, stat],
        out_specs=(pl.BlockSpec((S, 128), res), blk, blk,
                   pl.BlockSpec((None, nq, 8, TQ), lambda p, j: (p, 0, 0, 0)), blk),
        out_shape=(jax.ShapeDtypeStruct((S, D_GRP), BF16), jax.ShapeDtypeStruct((S, D_GRP), BF16),
                   jax.ShapeDtypeStruct((S, D_GRP), BF16), jax.ShapeDtypeStruct((N_PAIR, nq, 8, TQ), F32),
                   jax.ShapeDtypeStruct((S, D_GRP), F32)),
        scratch_shapes=[pltpu.VMEM((S, 128), F32)],
        name=name,
    )(proj, proj, proj, do, lse_t, delta_t, c_rep, c_t)


def _fox_gate_bwd_call(dc_rows, fl_raw, b_pad, *, name):
    S = fl_raw.shape[0]
    nb = S // TQ

    def body(dc_ref, fl_ref, b_ref, dfl_ref, db_ref, carry_ref):
        i = pl.program_id(0)

        @pl.when(i == 0)
        def _():
            carry_ref[...] = jnp.zeros_like(carry_ref)
        rc = _dot3(dc_ref[...], _tri(TQ, True)) + carry_ref[:, 0:1]
        carry_ref[...] = jnp.broadcast_to(rc[:, 0:1], carry_ref.shape)
        fl = fl_ref[...] + b_ref[...]
        dfl = rc.T * jax.nn.sigmoid(-fl)
        dfl_ref[...] = dfl.astype(BF16)
        _acc_add(db_ref, jnp.sum(dfl, axis=0, keepdims=True), i)

    rev = lambda i: (nb - 1 - i, 0)
    return pl.pallas_call(
        body, grid=(nb,),
        in_specs=[pl.BlockSpec((128, TQ), lambda i: (0, nb - 1 - i)), pl.BlockSpec((TQ, 128), rev), _vec_spec(128)],
        out_specs=(pl.BlockSpec((TQ, 128), rev), _acc_spec(128)),
        out_shape=(jax.ShapeDtypeStruct((S, 128), BF16), jax.ShapeDtypeStruct((8, 128), F32)),
        scratch_shapes=[pltpu.VMEM((128, 128), F32)],
        name=name, compiler_params=_cparams(),
    )(dc_rows, fl_raw, b_pad)


def _chk_bias_call(g_rev, *, name):
    def body(g_ref, o_ref):
        x = jnp.broadcast_to(g_ref[...], (TQ, ROLL_W))
        rolled = pltpu.roll(x, ROLL_W - (TQ - 1), 1, stride=1, stride_axis=0)
        qc = lax.broadcasted_iota(jnp.int32, (TQ, WIN), 0) // CHUNK
        kc = lax.broadcasted_iota(jnp.int32, (TQ, WIN), 1) // CHUNK
        band = (kc >= qc) & (kc <= qc + LEFT)
        o_ref[...] = jnp.where(band, rolled[:, 0:WIN], NEG)

    return pl.pallas_call(
        body, grid=(8,),
        in_specs=[pl.BlockSpec((None, 1, ROLL_W), lambda h: (h, 0, 0))],
        out_specs=pl.BlockSpec((None, TQ, WIN), lambda h: (h, 0, 0)),
        out_shape=jax.ShapeDtypeStruct((8, TQ, WIN), F32), name=name, compiler_params=_cparams(),
    )(g_rev.reshape(8, 1, ROLL_W))


def _chk_scores(i, qm, kwin, bias, scale):
    s = _dot_nt(qm, kwin) * scale + bias
    kc = lax.broadcasted_iota(jnp.int32, (TQ, WIN), 1) // CHUNK
    return jnp.where(kc + i * (TQ // CHUNK) >= LEFT, s, NEG)


def _chk_fwd_call(proj, bias, *, name, task=None):
    S = proj.shape[0]
    nq = S // TQ
    scale = HEAD ** -0.5

    def body(q_ref, k_ref, v_ref, b_ref, o_ref, kp_ref, vp_ref):
        i = pl.program_id(1)

        @pl.when(i == 0)
        def _():
            kp_ref[0:PADK, :] = jnp.zeros((PADK, 128), BF16)
            vp_ref[0:PADK, :] = jnp.zeros((PADK, 128), BF16)
            kp_ref[PADK:PADK + S, :] = k_ref[...]
            vp_ref[PADK:PADK + S, :] = v_ref[...]
        masks = _lane_masks()
        q = q_ref[...]
        start = pl.multiple_of(i * TQ, TQ)
        kwin = kp_ref[pl.ds(start, WIN), :]
        vwin = vp_ref[pl.ds(start, WIN), :]
        ss = [_chk_scores(i, jnp.where(masks[h], q, jnp.zeros_like(q)), kwin, b_ref[h], scale) for h in range(2)]
        ps = []
        for s in ss:
            p = jnp.exp(s - jnp.max(s, axis=1, keepdims=True))
            ps.append((p / jnp.sum(p, axis=1, keepdims=True)).astype(BF16))
        o_ref[...] = (_dot(ps[0], jnp.where(masks[0], vwin, jnp.zeros_like(vwin)))
                      + _dot(ps[1], jnp.where(masks[1], vwin, jnp.zeros_like(vwin))))

    c0 = 3 * N_PAIR
    return _pallas(
        body, grid=(N_PAIR, nq), task=task,
        in_specs=[pl.BlockSpec((TQ, 128), lambda p, i: (i, c0 + p)),
                  pl.BlockSpec((S, 128), lambda p, i: (0, c0 + N_PAIR + p)),
                  pl.BlockSpec((S, 128), lambda p, i: (0, c0 + 2 * N_PAIR + p)),
                  pl.BlockSpec((2, TQ, WIN), lambda p, i: (p, 0, 0))],
        out_specs=pl.BlockSpec((TQ, 128), lambda p, i: (i, p)),
        out_shape=jax.ShapeDtypeStruct((S, D_GRP), F32),
        scratch_shapes=[pltpu.VMEM((S + PADK, 128), BF16), pltpu.VMEM((S + PADK, 128), BF16)],
        name=name,
    )(proj, proj, proj, bias)


def _chk_bwd_call(proj, do, bias, *, name, task=None):
    S = proj.shape[0]
    nq = S // TQ
    scale = HEAD ** -0.5

    def body(q_ref, k_ref, v_ref, do_ref, b_ref, dq_ref, dk_ref, dv_ref, dg_ref, kp_ref, vp_ref, dkp_ref, dvp_ref, db_ref):
        i = pl.program_id(1)

        @pl.when(i == 0)
        def _():
            kp_ref[0:PADK, :] = jnp.zeros((PADK, 128), BF16)
            vp_ref[0:PADK, :] = jnp.zeros((PADK, 128), BF16)
            kp_ref[PADK:PADK + S, :] = k_ref[...]
            vp_ref[PADK:PADK + S, :] = v_ref[...]
            dkp_ref[...] = jnp.zeros_like(dkp_ref)
            dvp_ref[...] = jnp.zeros_like(dvp_ref)
            db_ref[...] = jnp.zeros_like(db_ref)
        masks = _lane_masks()
        q = q_ref[...]
        dout = do_ref[...]
        start = pl.multiple_of(i * TQ, TQ)
        kwin = kp_ref[pl.ds(start, WIN), :]
        vwin = vp_ref[pl.ds(start, WIN), :]
        qm = [jnp.where(mk, q, jnp.zeros_like(q)) for mk in masks]
        dom = [jnp.where(mk, dout, jnp.zeros_like(dout)) for mk in masks]
        ss = [_chk_scores(i, qm[h], kwin, b_ref[h], scale) for h in range(2)]
        dps = [_dot_nt(dom[h], vwin) for h in range(2)]
        pbs, dsbs = [], []
        for h in range(2):
            p = jnp.exp(ss[h] - jnp.max(ss[h], axis=1, keepdims=True))
            p = p / jnp.sum(p, axis=1, keepdims=True)
            ds = p * (dps[h] - jnp.sum(p * dps[h], axis=1, keepdims=True))
            db_ref[h] += ds
            pbs.append(p.astype(BF16))
            dsbs.append((ds * scale).astype(BF16))
        dq_ref[...] = (_dot(dsbs[0], jnp.where(masks[0], kwin, jnp.zeros_like(kwin)))
                       + _dot(dsbs[1], jnp.where(masks[1], kwin, jnp.zeros_like(kwin)))).astype(BF16)
        dkp_ref[pl.ds(start, WIN), :] += _dot_tn(dsbs[0], qm[0]) + _dot_tn(dsbs[1], qm[1])
        dvp_ref[pl.ds(start, WIN), :] += _dot_tn(pbs[0], dom[0]) + _dot_tn(pbs[1], dom[1])

        @pl.when(i == nq - 1)
        def _():
            dk_ref[...] = dkp_ref[PADK:PADK + S, :].astype(BF16)
            dv_ref[...] = dvp_ref[PADK:PADK + S, :].astype(BF16)
            a = lax.broadcasted_iota(jnp.int32, (TQ, TQ), 0)
            b = lax.broadcasted_iota(jnp.int32, (TQ, TQ), 1)
            flip = jnp.where(a + b == TQ - 1, 1.0, 0.0).astype(BF16)
            e = lax.broadcasted_iota(jnp.int32, (1, ROLL_W), 1)
            dg_ref[...] = jnp.zeros_like(dg_ref)
            for h in range(2):
                rev = _dot3_l(flip, db_ref[h])
                wide = jnp.concatenate([rev, jnp.zeros((TQ, ROLL_W - WIN), F32)], axis=1)
                diag = pltpu.roll(wide, 0, 1, stride=1, stride_axis=0)
                dg = jnp.sum(diag, axis=0, keepdims=True)
                lo = jnp.sum(jnp.where(e <= 639, dg, 0.0), axis=1, keepdims=True)
                hi = jnp.sum(jnp.where(e >= 895, dg, 0.0), axis=1, keepdims=True)
                dg_ref[h:h + 1, :] = jnp.where(e == 639, lo, jnp.where(e == 895, hi, dg))

    c0 = 3 * N_PAIR
    res = lambda p, i: (0, p)
    return _pallas(
        body, grid=(N_PAIR, nq), task=task,
        in_specs=[pl.BlockSpec((TQ, 128), lambda p, i: (i, c0 + p)),
                  pl.BlockSpec((S, 128), lambda p, i: (0, c0 + N_PAIR + p)),
                  pl.BlockSpec((S, 128), lambda p, i: (0, c0 + 2 * N_PAIR + p)),
                  pl.BlockSpec((TQ, 128), lambda p, i: (i, p)),
                  pl.BlockSpec((2, TQ, WIN), lambda p, i: (p, 0, 0))],
        out_specs=(pl.BlockSpec((TQ, 128), lambda p, i: (i, p)), pl.BlockSpec((S, 128), res),
                   pl.BlockSpec((S, 128), res), pl.BlockSpec((None, 8, ROLL_W), lambda p, i: (p, 0, 0))),
        out_shape=(jax.ShapeDtypeStruct((S, D_GRP), BF16), jax.ShapeDtypeStruct((S, D_GRP), BF16),
                   jax.ShapeDtypeStruct((S, D_GRP), BF16), jax.ShapeDtypeStruct((N_PAIR, 8, ROLL_W), F32)),
        scratch_shapes=[pltpu.VMEM((S + PADK, 128), BF16), pltpu.VMEM((S + PADK, 128), BF16),
                        pltpu.VMEM((S + PADK, 128), F32), pltpu.VMEM((S + PADK, 128), F32),
                        pltpu.VMEM((2, TQ, WIN), F32)],
        name=name,
    )(proj, proj, proj, do, bias)


def _mem_fwd_call(q, k, v, *, name, tq=512):
    S = q.shape[0]
    scale = MEM_HD ** -0.5

    def body(q_ref, k_ref, v_ref, o_ref):
        s = _dot_nt(q_ref[...], k_ref[...]) * scale
        p = jnp.exp(s - jnp.max(s, axis=1, keepdims=True))
        p = p / jnp.sum(p, axis=1, keepdims=True)
        o_ref[...] = _dot(p.astype(BF16), v_ref[...]).astype(BF16)

    return pl.pallas_call(
        body, grid=(MEM_HEADS, S // tq),
        in_specs=[pl.BlockSpec((tq, MEM_HD), lambda h, i: (i, h)),
                  pl.BlockSpec((N_MEM, MEM_HD), lambda h, i: (0, h)),
                  pl.BlockSpec((N_MEM, MEM_HD), lambda h, i: (0, h))],
        out_specs=pl.BlockSpec((tq, MEM_HD), lambda h, i: (i, h)),
        out_shape=jax.ShapeDtypeStruct((S, D), BF16), name=name, compiler_params=_cparams(),
    )(q, k, v)


def _mem_bwd_call(q, k, v, do, *, name, tq=512):
    S = q.shape[0]
    n = S // tq
    scale = MEM_HD ** -0.5

    def body(q_ref, k_ref, v_ref, do_ref, dq_ref, dk_ref, dv_ref, dka_ref, dva_ref):
        i = pl.program_id(1)

        @pl.when(i == 0)
        def _():
            dka_ref[...] = jnp.zeros_like(dka_ref)
            dva_ref[...] = jnp.zeros_like(dva_ref)
        qb = q_ref[...]
        kb = k_ref[...]
        dob = do_ref[...]
        s = _dot_nt(qb, kb) * scale
        p = jnp.exp(s - jnp.max(s, axis=1, keepdims=True))
        p = p / jnp.sum(p, axis=1, keepdims=True)
        dp = _dot_nt(dob, v_ref[...])
        ds = p * (dp - jnp.sum(p * dp, axis=1, keepdims=True))
        dsb = (ds * scale).astype(BF16)
        dq_ref[...] = _dot(dsb, kb).astype(BF16)
        dka_ref[...] += _dot_tn(dsb, qb)
        dva_ref[...] += _dot_tn(p.astype(BF16), dob)

        @pl.when(i == n - 1)
        def _():
            dk_ref[...] = dka_ref[...].astype(BF16)
            dv_ref[...] = dva_ref[...].astype(BF16)

    kv = pl.BlockSpec((N_MEM, MEM_HD), lambda h, i: (0, h))
    qs = pl.BlockSpec((tq, MEM_HD), lambda h, i: (i, h))
    return pl.pallas_call(
        body, grid=(MEM_HEADS, n), in_specs=[qs, kv, kv, qs], out_specs=(qs, kv, kv),
        out_shape=(jax.ShapeDtypeStruct((S, D), BF16), jax.ShapeDtypeStruct((N_MEM, D), BF16),
                   jax.ShapeDtypeStruct((N_MEM, D), BF16)),
        scratch_shapes=[pltpu.VMEM((N_MEM, MEM_HD), F32), pltpu.VMEM((N_MEM, MEM_HD), F32)],
        name=name, compiler_params=_cparams(),
    )(q, k, v, do)


def _rel_table_to_g(rel):
    return jnp.concatenate([
        jnp.broadcast_to(rel[:, N_REL - 1:N_REL], (8, 640)),
        rel[:, 1:N_REL - 1][:, ::-1],
        jnp.broadcast_to(rel[:, 0:1], (8, 129)),
    ], axis=1)


def _g_to_rel_table(dg):
    return dg[:, 639:896][:, ::-1]


def _place():
    x, y, c = lax.axis_index("x"), lax.axis_index("y"), lax.axis_index("c")
    others = [(1 - x, y), (x, 1 - y), (1 - x, 1 - y)]
    return x, y, c, others


def _half(c, rows):
    hr = rows // 2
    return pl.ds(pl.multiple_of(c * hr, 16), hr)


def _dma_sems(*shape):
    return pltpu.SemaphoreType.DMA(shape)


def _cast_slabs_call(ws, chip_arr, *, name, tm=256, task=None):
    n = len(ws)
    cols = ws[0].shape[1]
    tiles = [w.shape[0] // tm for w in ws]
    steps = max(tiles)

    def body(chip_ref, *refs):
        i = pl.program_id(0)
        for k in range(n):
            def cast(k=k):
                refs[n + k][...] = refs[k][...].astype(BF16)
            if tiles[k] == steps:
                cast()
            else:
                pl.when(i < tiles[k])(cast)

    in_specs = [pl.BlockSpec((tm, cols), lambda i, chip, t=t: (jnp.minimum(i, t - 1), 0)) for t in tiles]
    out_specs = [pl.BlockSpec((None, tm, cols), lambda i, chip, t=t: (chip[0], jnp.minimum(i, t - 1), 0)) for t in tiles]
    out_shape = [jax.ShapeDtypeStruct((N_CHIP,) + w.shape, BF16) for w in ws]
    return _pallas(body, grid=(steps,), in_specs=in_specs, out_specs=out_specs, out_shape=out_shape, name=name,
                   task=task, prefetch=1)(chip_arr, *ws)


def _cast_slab_call(w, chip_arr, *, name, tm=256, pad_rows=0):
    rows, cols = w.shape
    if pad_rows:
        tm = rows
    tm = min(tm, rows)

    def body(chip_ref, w_ref, o_ref):
        o_ref[0:tm, :] = w_ref[...].astype(BF16)
        if pad_rows:
            o_ref[tm:tm + pad_rows, :] = jnp.zeros((pad_rows, cols), BF16)

    return pl.pallas_call(
        body,
        grid_spec=pltpu.PrefetchScalarGridSpec(
            num_scalar_prefetch=1, grid=(rows // tm,),
            in_specs=[pl.BlockSpec((tm, cols), lambda i, chip: (i, 0))],
            out_specs=pl.BlockSpec((None, tm + pad_rows, cols), lambda i, chip: (chip[0], i, 0))),
        out_shape=jax.ShapeDtypeStruct((N_CHIP, rows + pad_rows, cols), BF16), name=name,
        compiler_params=_cparams(),
    )(chip_arr, w)


def _ag_ici_task(gathered):
    n = len(gathered)

    def copies(ins, outs, sems):
        send_sems, recv_sems = sems
        x, y, c, others = _place()
        me = 2 * x + y
        for k in range(n):
            mine = _half(c, gathered[k].shape[1])
            for t, (ox, oy) in enumerate(others):
                yield [pltpu.make_async_remote_copy(
                    src_ref=ins[k].at[me, mine], dst_ref=outs[k].at[slab, mine],
                    send_sem=send_sems.at[k, t], recv_sem=recv_sems.at[k, t],
                    device_id=(ox, oy, c), device_id_type=MESH) for slab in (me, 2 * ox + oy)]

    def issue(ins, outs, sems):
        for outgoing, _ in copies(ins, outs, sems):
            outgoing.start()

    def drain(ins, outs, sems):
        for outgoing, incoming in copies(ins, outs, sems):
            incoming.wait_recv()
            outgoing.wait_send()

    return _Task(gathered, [jax.ShapeDtypeStruct(g.shape, g.dtype) for g in gathered],
                 [_dma_sems(n, 3), _dma_sems(n, 3)], issue, drain, aliases={k: k for k in range(n)})


def _ag_d2d_task(gathered):
    n = len(gathered)

    def copies(ins, outs, sems):
        send_sems, recv_sems = sems
        x, y, c, others = _place()
        for k in range(n):
            rows = gathered[k].shape[1]
            mine, theirs = _half(c, rows), _half(1 - c, rows)
            for t, (ox, oy) in enumerate(others):
                slab = 2 * ox + oy
                pair = [pltpu.make_async_remote_copy(
                    src_ref=ins[k].at[slab, half], dst_ref=outs[k].at[slab, half],
                    send_sem=send_sems.at[k, t], recv_sem=recv_sems.at[k, t],
                    device_id=(x, y, 1 - c), device_id_type=MESH) for half in (mine, theirs)]
                yield pair

    def issue(ins, outs, sems):
        for outgoing, _ in copies(ins, outs, sems):
            outgoing.start()

    def drain(ins, outs, sems):
        for outgoing, incoming in copies(ins, outs, sems):
            incoming.wait_recv()
            outgoing.wait_send()

    return _Task(gathered, [jax.ShapeDtypeStruct(g.shape, g.dtype) for g in gathered],
                 [_dma_sems(n, 3), _dma_sems(n, 3)], issue, drain, aliases={k: k for k in range(n)})


def _rs_pair_task(ds):
    n = len(ds)

    def copies(ins, outs, sems):
        send_sems, recv_sems = sems
        x, y, c, _ = _place()
        for k in range(n):
            yield pltpu.make_async_remote_copy(
                src_ref=ins[k].at[:, _half(1 - c, ds[k].shape[1])], dst_ref=outs[k],
                send_sem=send_sems.at[k], recv_sem=recv_sems.at[k],
                device_id=(x, y, 1 - c), device_id_type=MESH)

    def issue(ins, outs, sems):
        for cp in copies(ins, outs, sems):
            cp.start()

    def drain(ins, outs, sems):
        for cp in copies(ins, outs, sems):
            cp.wait()

    return _Task(ds, [jax.ShapeDtypeStruct((N_CHIP, d.shape[1] // 2, d.shape[2]), d.dtype) for d in ds],
                 [_dma_sems(n), _dma_sems(n)], issue, drain)


def _pair_add_call(d, r1, c_arr, *, name, tm=256):
    _, rows, cols = d.shape
    hr = rows // 2
    tm = tm if hr % tm == 0 else hr
    nb = hr // tm

    def body(c_ref, d_ref, r_ref, o_ref):
        o_ref[...] = (d_ref[...].astype(F32) + r_ref[...].astype(F32)).astype(BF16)

    return pl.pallas_call(
        body,
        grid_spec=pltpu.PrefetchScalarGridSpec(
            num_scalar_prefetch=1, grid=(N_CHIP, nb),
            in_specs=[pl.BlockSpec((None, tm, cols), lambda j, i, c: (j, c[0] * nb + i, 0)),
                      pl.BlockSpec((None, tm, cols), lambda j, i, c: (j, i, 0))],
            out_specs=pl.BlockSpec((None, tm, cols), lambda j, i, c: (j, i, 0))),
        out_shape=jax.ShapeDtypeStruct((N_CHIP, hr, cols), BF16), name=name, compiler_params=_cparams(),
    )(c_arr, d, r1)


def _rs_chip_task(ps):
    n = len(ps)

    def copies(ins, outs, sems):
        send_sems, recv_sems = sems
        x, y, c, others = _place()
        for k in range(n):
            for t, (ox, oy) in enumerate(others):
                yield pltpu.make_async_remote_copy(
                    src_ref=ins[k].at[2 * ox + oy], dst_ref=outs[k].at[t],
                    send_sem=send_sems.at[k, t], recv_sem=recv_sems.at[k, t],
                    device_id=(ox, oy, c), device_id_type=MESH)

    def issue(ins, outs, sems):
        for cp in copies(ins, outs, sems):
            cp.start()

    def drain(ins, outs, sems):
        for cp in copies(ins, outs, sems):
            cp.wait()

    return _Task(ps, [jax.ShapeDtypeStruct((3,) + p.shape[1:], p.dtype) for p in ps],
                 [_dma_sems(n, 3), _dma_sems(n, 3)], issue, drain)


def _chip_sum_call(p, r2, place_arr, *, name, tm=256):
    _, hr, cols = r2.shape
    tm = tm if hr % tm == 0 else hr
    nb = hr // tm

    def body(place_ref, p_ref, r_ref, o_ref):
        acc = p_ref[...].astype(F32)
        for j in range(3):
            acc = acc + r_ref[j].astype(F32)
        o_ref[...] = acc

    return pl.pallas_call(
        body,
        grid_spec=pltpu.PrefetchScalarGridSpec(
            num_scalar_prefetch=1, grid=(nb,),
            in_specs=[pl.BlockSpec((None, tm, cols), lambda i, pc: (pc[0], i, 0)),
                      pl.BlockSpec((3, tm, cols), lambda i, pc: (0, i, 0))],
            out_specs=pl.BlockSpec((tm, cols), lambda i, pc: (pc[1] * nb + i, 0))),
        out_shape=jax.ShapeDtypeStruct((2 * hr, cols), F32), name=name, compiler_params=_cparams(),
    )(place_arr, p, r2)


def _rs_gather_task(gs):
    n = len(gs)

    def copies(ins, outs, sems):
        send_sems, recv_sems = sems
        x, y, c, _ = _place()
        for k in range(n):
            rows = gs[k].shape[0]
            mine, theirs = _half(c, rows), _half(1 - c, rows)
            yield [pltpu.make_async_remote_copy(
                src_ref=ins[k].at[mine], dst_ref=outs[k].at[half],
                send_sem=send_sems.at[k], recv_sem=recv_sems.at[k],
                device_id=(x, y, 1 - c), device_id_type=MESH) for half in (mine, theirs)]

    def issue(ins, outs, sems):
        for outgoing, _ in copies(ins, outs, sems):
            outgoing.start()

    def drain(ins, outs, sems):
        for outgoing, incoming in copies(ins, outs, sems):
            incoming.wait_recv()
            outgoing.wait_send()

    return _Task(gs, [jax.ShapeDtypeStruct(g.shape, g.dtype) for g in gs],
                 [_dma_sems(n), _dma_sems(n)], issue, drain, aliases={k: k for k in range(n)})


def _adamw(w, g, m, v):
    m = ADAM_B1 * m + (1.0 - ADAM_B1) * g
    v = ADAM_B2 * v + (1.0 - ADAM_B2) * jnp.square(g)
    m_hat = m / (1.0 - ADAM_B1 ** ADAM_STEP)
    v_hat = v / (1.0 - ADAM_B2 ** ADAM_STEP)
    delta = -ADAM_LR * (m_hat / (jnp.sqrt(v_hat) + ADAM_EPS) + ADAM_WD * w)
    return delta, m, v


def _adamw_call(items, *, name, tm=256, task=None):
    n = len(items)
    cols = items[0][0].shape[1]
    tiles = [it[0].shape[0] // tm for it in items]
    steps = max(tiles)

    def body(*refs):
        i = pl.program_id(0)
        ins, outs = refs[:4 * n], refs[4 * n:]
        for k in range(n):
            def update(k=k):
                g = ins[4 * k + 1][...]
                res = _adamw(ins[4 * k][...], g, ins[4 * k + 2][...], ins[4 * k + 3][...])
                outs[4 * k][...] = g
                for j in range(3):
                    outs[4 * k + 1 + j][...] = res[j]
            if tiles[k] == steps:
                update()
            else:
                pl.when(i < tiles[k])(update)

    in_specs, out_specs, out_shape, args = [], [], [], []
    for it, t in zip(items, tiles):
        spec = pl.BlockSpec((tm, cols), lambda i, t=t: (jnp.minimum(i, t - 1), 0))
        in_specs += [spec] * 4
        out_specs += [spec] * 4
        out_shape += [jax.ShapeDtypeStruct(it[0].shape, F32)] * 4
        args += list(it)
    res = _pallas(body, grid=(steps,), in_specs=in_specs, out_specs=out_specs, out_shape=out_shape,
                  name=name, task=task)(*args)
    outs, extra = res if task is not None else (res, None)
    grouped = [tuple(outs[4 * k:4 * k + 4]) for k in range(n)]
    return (grouped, extra) if task is not None else grouped


def _adamw_cols_call(w, g_pad, m, v, *, name, tn=256):
    rows, cols = w.shape

    def body(w_ref, g_ref, m_ref, v_ref, go_ref, d_ref, mo_ref, vo_ref):
        g = g_ref[0:rows, :]
        d, mn, vn = _adamw(w_ref[...], g, m_ref[...], v_ref[...])
        go_ref[...] = g
        d_ref[...] = d
        mo_ref[...] = mn
        vo_ref[...] = vn

    spec = pl.BlockSpec((rows, tn), lambda j: (0, j))
    gspec = pl.BlockSpec((g_pad.shape[0], tn), lambda j: (0, j))
    return _pallas(body, grid=(cols // tn,), in_specs=[spec, gspec, spec, spec], out_specs=(spec,) * 4,
                   out_shape=(jax.ShapeDtypeStruct((rows, cols), F32),) * 4, name=name)(w, g_pad, m, v)


N_DEV = 8
SMALL_ROWS = 24
SMALL_LAYOUT = {
    "g_mix_pre": (0, 0, 1, D), "g_mix_post": (1, 0, 1, D), "g_mem_kv": (2, 0, 1, D), "g_mem_pre": (3, 0, 1, D),
    "g_mem_post": (4, 0, 1, D), "g_ff_pre": (5, 0, 1, D), "g_ff_post": (6, 0, 1, D),
    "g_fox_out": (7, 0, 1, D_GRP), "g_chk_out": (7, D_GRP, 1, D_GRP), "b_fgt": (8, 0, 1, 8),
    "rel_bias": (16, 0, 8, N_REL),
}
SMALL = list(SMALL_LAYOUT)


LOSS_ROW = 9


def _small_call(grads, ws, ms, vs, loss_blk, task, *, name):
    n = len(SMALL)
    t_in, t_out = len(task.arrays), len(task.out_shapes)

    def body(*refs):
        g_refs, w_refs, m_refs, v_refs = (refs[j * n:(j + 1) * n] for j in range(4))
        p = 4 * n
        loss_ref, tins = refs[p], refs[p + 1:p + 1 + t_in]
        p += 1 + t_in
        outs, loss_out, touts = refs[p:p + 4 * n], refs[p + 4 * n], refs[p + 4 * n + 1:p + 4 * n + 1 + t_out]
        p += 4 * n + 1 + t_out
        mine, slots, send_sems, recv_sems = refs[p:p + 4]
        tsems = refs[p + 4:]
        task.issue(tins, touts, tsems)
        x, y, c, _ = _place()
        me = 4 * x + 2 * y + c
        mine[...] = jnp.zeros_like(mine)
        for k, name_k in enumerate(SMALL):
            r, l, nr, nl = SMALL_LAYOUT[name_k]
            mine[r:r + nr, l:l + nl] = g_refs[k][0:nr, 0:nl]
        mine[LOSS_ROW:LOSS_ROW + 1, 0:128] = loss_ref[0:1, :]
        slots[me] = mine[...]
        peers = [(dx, dy, dc) for dx in (0, 1) for dy in (0, 1) for dc in (0, 1)][1:]
        cps = []
        for t, (dx, dy, dc) in enumerate(peers):
            px, py, pc = (x + dx) % 2, (y + dy) % 2, (c + dc) % 2
            cps.append(pltpu.make_async_remote_copy(
                src_ref=mine, dst_ref=slots.at[me], send_sem=send_sems.at[t], recv_sem=recv_sems.at[t],
                device_id=(px, py, pc), device_id_type=MESH))
            cps[-1].start()
        for t, (dx, dy, dc) in enumerate(peers):
            px, py, pc = (x + dx) % 2, (y + dy) % 2, (c + dc) % 2
            pltpu.make_async_remote_copy(
                src_ref=mine, dst_ref=slots.at[4 * px + 2 * py + pc], send_sem=send_sems.at[t],
                recv_sem=recv_sems.at[t], device_id=(px, py, pc), device_id_type=MESH).wait_recv()
        for cp in cps:
            cp.wait_send()
        total = slots[0]
        for j in range(1, N_DEV):
            total = total + slots[j]
        for k, name_k in enumerate(SMALL):
            r, l, nr, nl = SMALL_LAYOUT[name_k]
            g = total[r:r + nr, l:l + nl]
            d, mn, vn = _adamw(w_refs[k][...], g, m_refs[k][...], v_refs[k][...])
            for j, val in enumerate((g, d, mn, vn)):
                outs[4 * k + j][...] = val
        loss_out[...] = jnp.broadcast_to(total[LOSS_ROW:LOSS_ROW + 1, 0:128], loss_out.shape)
        task.drain(tins, touts, tsems)

    vm = pl.BlockSpec(memory_space=pltpu.VMEM)
    out_shape = [jax.ShapeDtypeStruct(ws[k].shape, F32) for k in SMALL for _ in range(4)]
    out_shape += [jax.ShapeDtypeStruct((8, 128), F32)] + list(task.out_shapes)
    res = pl.pallas_call(
        body, in_specs=[vm] * (4 * n + 1) + [ANY] * t_in, out_specs=[vm] * (4 * n + 1) + [ANY] * t_out,
        out_shape=out_shape,
        scratch_shapes=[pltpu.VMEM((SMALL_ROWS, D), F32), pltpu.VMEM((N_DEV, SMALL_ROWS, D), F32),
                        _dma_sems(N_DEV - 1), _dma_sems(N_DEV - 1)] + list(task.sems),
        input_output_aliases={4 * n + 1 + i: 4 * n + 1 + j for i, j in task.aliases.items()},
        name=name,
    )(*[d[k] for d in (grads, ws, ms, vs) for k in SMALL], loss_blk, *task.arrays)
    return ({k: tuple(res[4 * i:4 * i + 4]) for i, k in enumerate(SMALL)}, res[4 * n], list(res[4 * n + 1:]))


WEIGHTS = ["w_in", "b_fgt", "rel_bias", "g_fox_out", "g_chk_out", "w_out", "g_mix_pre", "g_mix_post", "g_mem_kv",
           "w_mq", "w_mk", "w_mv", "w_mo", "g_mem_pre", "g_mem_post", "w_ff1", "w_ff2", "g_ff_pre", "g_ff_post"]
BIG = ["w_in", "w_out", "w_mq", "w_mk", "w_mv", "w_mo", "w_ff1", "w_ff2"]


IN_SHARD = D_IN // N_CHIP
IN_PAD = 800


IN_PIECES = [(0, 0, 770), (800, 770, 766), (1566, 3072, 4), (1600, 3076, 4), (1604, 1536, 766), (2400, 2302, 770)]
PAD_ZEROS = [(800 * j + IN_SHARD, IN_PAD - IN_SHARD) for j in range(N_CHIP)]
ALL_ZEROS = [(D_IN, D_ALL - D_IN)]


def _reorder_rows_call(src, to_all, *, name, tn=256):
    rows, cols = src.shape
    zeros = ALL_ZEROS if to_all else PAD_ZEROS

    def body(s_ref, o_ref):
        for pad0, all0, cnt in IN_PIECES:
            s0, d0 = (pad0, all0) if to_all else (all0, pad0)
            o_ref[d0:d0 + cnt, :] = s_ref[s0:s0 + cnt, :]
        for z0, cnt in zeros:
            o_ref[z0:z0 + cnt, :] = jnp.zeros((cnt, tn), src.dtype)

    spec = pl.BlockSpec((rows, tn), lambda j: (0, j))
    return _pallas(body, grid=(cols // tn,), in_specs=[spec], out_specs=spec,
                   out_shape=jax.ShapeDtypeStruct((rows, cols), src.dtype), name=name)(src)


def kernel(x, mem, w_in, b_fgt, rel_bias, g_fox_out, g_chk_out, w_out, g_mix_pre, g_mix_post, g_mem_kv, w_mq, w_mk, w_mv, w_mo, g_mem_pre, g_mem_post, w_ff1, w_ff2, g_ff_pre, g_ff_post, loss_target, m_w_in, m_b_fgt, m_rel_bias, m_g_fox_out, m_g_chk_out, m_w_out, m_g_mix_pre, m_g_mix_post, m_g_mem_kv, m_w_mq, m_w_mk, m_w_mv, m_w_mo, m_g_mem_pre, m_g_mem_post, m_w_ff1, m_w_ff2, m_g_ff_pre, m_g_ff_post, v_w_in, v_b_fgt, v_rel_bias, v_g_fox_out, v_g_chk_out, v_w_out, v_g_mix_pre, v_g_mix_post, v_g_mem_kv, v_w_mq, v_w_mk, v_w_mv, v_w_mo, v_g_mem_pre, v_g_mem_post, v_w_ff1, v_w_ff2, v_g_ff_pre, v_g_ff_post):
    w = dict(w_in=w_in, b_fgt=b_fgt, rel_bias=rel_bias, g_fox_out=g_fox_out, g_chk_out=g_chk_out, w_out=w_out,
             g_mix_pre=g_mix_pre, g_mix_post=g_mix_post, g_mem_kv=g_mem_kv, w_mq=w_mq, w_mk=w_mk, w_mv=w_mv,
             w_mo=w_mo, g_mem_pre=g_mem_pre, g_mem_post=g_mem_post, w_ff1=w_ff1, w_ff2=w_ff2, g_ff_pre=g_ff_pre,
             g_ff_post=g_ff_post)
    m = dict(w_in=m_w_in, b_fgt=m_b_fgt, rel_bias=m_rel_bias, g_fox_out=m_g_fox_out, g_chk_out=m_g_chk_out,
             w_out=m_w_out, g_mix_pre=m_g_mix_pre, g_mix_post=m_g_mix_post, g_mem_kv=m_g_mem_kv, w_mq=m_w_mq,
             w_mk=m_w_mk, w_mv=m_w_mv, w_mo=m_w_mo, g_mem_pre=m_g_mem_pre, g_mem_post=m_g_mem_post,
             w_ff1=m_w_ff1, w_ff2=m_w_ff2, g_ff_pre=m_g_ff_pre, g_ff_post=m_g_ff_post)
    v = dict(w_in=v_w_in, b_fgt=v_b_fgt, rel_bias=v_rel_bias, g_fox_out=v_g_fox_out, g_chk_out=v_g_chk_out,
             w_out=v_w_out, g_mix_pre=v_g_mix_pre, g_mix_post=v_g_mix_post, g_mem_kv=v_g_mem_kv, w_mq=v_w_mq,
             w_mk=v_w_mk, w_mv=v_w_mv, w_mo=v_w_mo, g_mem_pre=v_g_mem_pre, g_mem_post=v_g_mem_post,
             w_ff1=v_w_ff1, w_ff2=v_w_ff2, g_ff_pre=v_g_ff_pre, g_ff_post=v_g_ff_post)

    def rows(d, k):
        return d[k][0] if k == "rel_bias" else d[k]

    xs, mems, target = x[0], mem[0], loss_target[0]
    S = xs.shape[0]
    sp = {k: rows(w, k) for k in SMALL}
    sg = {k: pltpu.with_memory_space_constraint(a, pltpu.HBM) for k, a in sp.items()}
    b_pad = pltpu.with_memory_space_constraint(jnp.pad(sp["b_fgt"], ((0, 0), (0, 120))), pltpu.HBM)
    chip = 2 * lax.axis_index("x") + lax.axis_index("y")
    chip_arr = jnp.reshape(chip, (1,)).astype(jnp.int32)
    c_arr = jnp.reshape(lax.axis_index("c"), (1,)).astype(jnp.int32)
    place_arr = jnp.concatenate([chip_arr, c_arr])
    w_in_t, m_in_t, v_in_t = w["w_in"][0].T, m["w_in"][0].T, v["w_in"][0].T
    slab = {"w_in": _cast_slab_call(w_in_t, chip_arr, name="cast_w_in", pad_rows=IN_PAD - IN_SHARD)}

    def gather_ici(names):
        return _ag_ici_task([slab[k] for k in names])

    def pair_add(k, d, r1):
        return _pair_add_call(d, r1, c_arr, name="rs_pair_add_" + k)

    rest, (g_in,) = _cast_slabs_call([w[k][0] for k in BIG[1:]], chip_arr, name="cast_rest",
                                     task=gather_ici(["w_in"]))
    slab.update(zip(BIG[1:], rest))
    h1, (g_in,) = _rms_fwd_call(xs, sg["g_mix_pre"], name="rms_mix_pre", task=_ag_d2d_task([g_in]))
    w_all_t = _reorder_rows_call(g_in.reshape(N_CHIP * IN_PAD, D), True, name="w_in_rows")
    proj, (g_out, g_mq) = _mm_nt(h1, w_all_t, "plain", rows=(0, 3072), name="mm_proj",
                                 task=gather_ici(["w_out", "w_mq"]))
    fl_raw = _mm_nt(h1, w_all_t, "plain", rows=(3072, 128), name="mm_gate", out_dtype=F32, tn=128)
    c_rep, c_t = _fox_prep_call(fl_raw, b_pad, name="fox_prep")
    bias = _chk_bias_call(_rel_table_to_g(sp["rel_bias"]), name="chk_bias")
    mid = ["w_mk", "w_mv", "w_mo", "w_ff1"]
    (yf, lse), got = _fox_fwd_call(proj, c_rep, c_t, name="fox_fwd",
                                   task=_merge_tasks([gather_ici(mid), _ag_d2d_task([g_out, g_mq])]))
    g_mid, (g_out, g_mq) = got[:4], got[4:]
    yc, got = _chk_fwd_call(proj, bias, name="chk_fwd",
                            task=_merge_tasks([gather_ici(["w_ff2"]), _ag_d2d_task(g_mid)]))
    g_ff2, (g_mk, g_mv, g_mo, g_ff1) = got[0], got[1:]
    yn = _mix_norm_fwd_call(yf, yc, sg["g_fox_out"], sg["g_chk_out"], name="mix_norm_fwd")
    z, (g_ff2,) = _mm_nn(yn, g_out, "rows", name="mm_out", out_dtype=F32, task=_ag_d2d_task([g_ff2]))
    x1, h2 = _post_pre_call(xs, z, sg["g_mix_post"], sg["g_mem_pre"], name="post_mix")
    memn = _rms_fwd_call(mems, sg["g_mem_kv"], name="rms_mem_kv")
    q2 = _mm_nn(h2, g_mq, "rows", name="mm_mq")
    k2 = _mm_nn(memn, g_mk, "rows", name="mm_mk")
    v2 = _mm_nn(memn, g_mv, "rows", name="mm_mv")
    o2 = _mem_fwd_call(q2, k2, v2, name="mem_fwd")
    y2 = _mm_nn(o2, g_mo, "rows", name="mm_mo", out_dtype=F32)
    x2, h3 = _post_pre_call(x1, y2, sg["g_mem_post"], sg["g_ff_pre"], name="post_mem")
    act, relu = _mm_nn(h3, g_ff1, "cols", name="mm_ff1", epi="relu2")
    y3 = _mm_nn(act, g_ff2, "rows", name="mm_ff2", out_dtype=F32, tm=1024)
    loss_blk, dx3, dy3, dg_ff_post = _final_call(x2, y3, sg["g_ff_post"], target, name="final")

    d_ff2 = _mm_tn(act, dy3, name="mm_dff2", tk=512, tn=1024).reshape(N_CHIP, D_FF // N_CHIP, D)
    du, (r1,) = _mm_nt(dy3, g_ff2, "rows", name="mm_du", mul2r=relu, task=_rs_pair_task([d_ff2]))
    p_ff2 = pair_add("w_ff2", d_ff2, r1)
    d_ff1 = _mm_tn(h3, du, name="mm_dff1", cols4=True)
    dh3, (r1,) = _mm_nt(du, g_ff1, "cols", name="mm_dh3", out_dtype=F32, tm=1024, task=_rs_pair_task([d_ff1]))
    p_ff1 = pair_add("w_ff1", d_ff1, r1)
    dx2, dy2, dg_ff_pre, dg_mem_post = _bwd_mid_call(dx3, x2, dh3, sg["g_ff_pre"], y2, sg["g_mem_post"], name="bwd_ff")
    d_mo = _mm_tn(o2, dy2, name="mm_dmo").reshape(N_CHIP, D // N_CHIP, D)
    do2 = _mm_nt(dy2, g_mo, "rows", name="mm_do2")
    dq2, dk2, dv2 = _mem_bwd_call(q2, k2, v2, do2, name="mem_bwd")
    d_mq = _mm_tn(h2, dq2, name="mm_dmq").reshape(N_CHIP, D // N_CHIP, D)
    dh2 = _mm_nt(dq2, g_mq, "rows", name="mm_dh2", out_dtype=F32)
    d_mk = _mm_tn(memn, dk2, name="mm_dmk").reshape(N_CHIP, D // N_CHIP, D)
    d_mv = _mm_tn(memn, dv2, name="mm_dmv").reshape(N_CHIP, D // N_CHIP, D)
    dmn_k = _mm_nt(dk2, g_mk, "rows", name="mm_dmemk", out_dtype=F32)
    dmn_v = _mm_nt(dv2, g_mv, "rows", name="mm_dmemv", out_dtype=F32)
    dg_mem_kv = _gain_grad_call(mems, sg["g_mem_kv"], dmn_k, dmn_v, name="gain_mem_kv")
    dx1, dz, dg_mem_pre, dg_mix_post = _bwd_mid_call(dx2, x1, dh2, sg["g_mem_pre"], z, sg["g_mix_post"], name="bwd_mem")
    d_out = _mm_tn(yn, dz, name="mm_dout").reshape(N_CHIP, D // N_CHIP, D)
    late = ["w_mo", "w_mq", "w_mk", "w_mv", "w_out"]
    d_late = [d_mo, d_mq, d_mk, d_mv, d_out]
    dyn, r1_late = _mm_nt(dz, g_out, "rows", name="mm_dyn", out_dtype=F32, task=_rs_pair_task(d_late))
    p_late = [pair_add(k, d, r1) for k, d, r1 in zip(late, d_late, r1_late)]
    dof, doc, delta, dg_fox, dg_chk = _mix_norm_bwd_call(dyn, yf, yc, sg["g_fox_out"], sg["g_chk_out"], name="mix_norm_bwd")
    (dqf, dkf, dvf, dcq, dck), r2_ff = _fox_bwd_call(proj, dof, lse, delta, c_rep, c_t, name="fox_bwd",
                                                      task=_rs_chip_task([p_ff2, p_ff1]))
    (dqc, dkc, dvc, dgrev), r2_late = _chk_bwd_call(proj, doc, bias, name="chk_bwd", task=_rs_chip_task(p_late))
    first = ["w_ff2", "w_ff1"] + late
    f_first = [_chip_sum_call(p, r, place_arr, name="rs_chip_sum_" + k)
               for k, p, r in zip(first, [p_ff2, p_ff1] + p_late, r2_ff + r2_late)]
    dc8 = dcq[:, :, 0:2, :].transpose(0, 2, 1, 3).reshape(8, S) + dck[:, ::HEAD].T
    dc_rows = jnp.concatenate([dc8, jnp.zeros((120, S), F32)], axis=0)
    dfl, db_fgt = _fox_gate_bwd_call(dc_rows, fl_raw, b_pad, name="fox_gate_bwd")
    dproj = jnp.concatenate([dqf, dkf, dvf, dqc, dkc, dvc, dfl], axis=1)
    d_all_t, g_first = _mm_tn(dproj, h1, name="mm_dwin", tk=640, tn=1024, task=_rs_gather_task(f_first))
    grads = dict(zip(first, g_first))
    d_in = _reorder_rows_call(d_all_t, False, name="d_in_rows").reshape(N_CHIP, IN_PAD, D)
    delta_w, new_m, new_v = {}, {}, {}

    def adamw_items(names):
        return [(w[k][0], grads[k], m[k][0], v[k][0]) for k in names]

    upd_late, (r1,) = _adamw_call(adamw_items(late), name="adamw_late", tm=64, task=_rs_pair_task([d_in]))
    p_in = pair_add("w_in", d_in, r1)
    dh1, (r2_in,) = _mm_nn(dproj, w_all_t, "plain", name="mm_dh1", out_dtype=F32, tm=1024,
                           task=_rs_chip_task([p_in]))
    f_in = _chip_sum_call(p_in, r2_in, place_arr, name="rs_chip_sum_w_in")
    upd_ff = _adamw_call(adamw_items(first[:2]), name="adamw_ff")
    for k, res in zip(late + first[:2], upd_late + upd_ff):
        grads[k], delta_w[k], new_m[k], new_v[k] = res
    grad_x, dg_mix_pre = _bwd_last_call(dx1, xs, dh1, sg["g_mix_pre"], name="bwd_mix")

    small_g = {"g_mix_pre": dg_mix_pre, "g_mix_post": dg_mix_post, "g_mem_kv": dg_mem_kv, "g_mem_pre": dg_mem_pre,
               "g_mem_post": dg_mem_post, "g_ff_pre": dg_ff_pre, "g_ff_post": dg_ff_post, "g_fox_out": dg_fox,
               "g_chk_out": dg_chk, "b_fgt": db_fgt,
               "rel_bias": _g_to_rel_table(dgrev[:, 0:2, :].reshape(8, ROLL_W))}
    small, loss_out, (g_w_in,) = _small_call(
        small_g, sp, {k: rows(m, k) for k in SMALL}, {k: rows(v, k) for k in SMALL}, loss_blk,
        _rs_gather_task([f_in]), name="small_allreduce_adamw")
    loss = loss_out[0, 0]
    res = _adamw_cols_call(w_in_t, g_w_in, m_in_t, v_in_t, name="adamw_w_in")
    grads["w_in"], delta_w["w_in"], new_m["w_in"], new_v["w_in"] = (a.T for a in res)
    for k in SMALL:
        vals = small[k]
        if k == "rel_bias":
            vals = tuple(a[None] for a in vals)
        grads[k], delta_w[k], new_m[k], new_v[k] = vals

    def out(d, k):
        return d[k][None] if k in BIG else d[k]

    return (loss, grad_x[None], *[out(grads, k) for k in WEIGHTS], *[out(delta_w, k) for k in WEIGHTS],
            *[out(new_m, k) for k in WEIGHTS], *[out(new_v, k) for k in WEIGHTS])
```

```python
import functools

import jax
import jax.numpy as jnp
from jax import lax
from jax.experimental import pallas as pl
from jax.experimental.pallas import tpu as pltpu

F32 = jnp.float32
BF16 = jnp.bfloat16

D = 1024
HEAD = 64
N_PAIR = 4
D_GRP = 512
CHUNK = 64
LEFT = 8
MAX_REL = 128
N_REL = 2 * MAX_REL + 1
N_MEM = 256
MEM_HEADS = 4
MEM_HD = 256
D_FF = 4096
D_IN = 3080
D_ALL = 3200
EPS = 1e-6
TQ = 256
WIN = (LEFT + TQ // CHUNK) * CHUNK
PADK = LEFT * CHUNK
ROLL_W = 1024
NEG = -1e30
N_CHIP = 4
VMEM_LIMIT = 48 * 1024 * 1024

ADAM_LR = 0.001
ADAM_B1 = 0.9
ADAM_B2 = 0.999
ADAM_EPS = 1e-08
ADAM_WD = 0.01
ADAM_STEP = 10

MESH = pl.DeviceIdType.MESH


def _cparams():
    return pltpu.CompilerParams(vmem_limit_bytes=VMEM_LIMIT)


ANY = pl.BlockSpec(memory_space=pl.ANY)


class _Task:
    def __init__(self, arrays, out_shapes, sems, issue, drain, aliases=None):
        self.arrays, self.out_shapes, self.sems = list(arrays), list(out_shapes), list(sems)
        self.issue, self.drain, self.aliases = issue, drain, dict(aliases or {})


def _merge_tasks(tasks):
    tasks = [t for t in tasks if t is not None]
    if len(tasks) == 1:
        return tasks[0]
    cuts, a, o, s = [], 0, 0, 0
    aliases = {}
    for t in tasks:
        cuts.append((a, o, s))
        aliases.update({a + i: o + j for i, j in t.aliases.items()})
        a, o, s = a + len(t.arrays), o + len(t.out_shapes), s + len(t.sems)

    def part(fn_name):
        def run(ins, outs, sems):
            for t, (a0, o0, s0) in zip(tasks, cuts):
                getattr(t, fn_name)(ins[a0:a0 + len(t.arrays)], outs[o0:o0 + len(t.out_shapes)],
                                    sems[s0:s0 + len(t.sems)])
        return run

    return _Task([x for t in tasks for x in t.arrays], [x for t in tasks for x in t.out_shapes],
                 [x for t in tasks for x in t.sems], part("issue"), part("drain"), aliases)


def _pallas(body, *, grid, in_specs, out_specs, out_shape, name, scratch_shapes=(), task=None, prefetch=0):
    def make(kernel, i_specs, o_specs, o_shape, scratch, aliases):
        if prefetch:
            spec = pltpu.PrefetchScalarGridSpec(num_scalar_prefetch=prefetch, grid=grid, in_specs=i_specs,
                                                out_specs=o_specs, scratch_shapes=scratch)
            return pl.pallas_call(kernel, grid_spec=spec, out_shape=o_shape, input_output_aliases=aliases,
                                  name=name, compiler_params=_cparams())
        return pl.pallas_call(kernel, grid=grid, in_specs=i_specs, out_specs=o_specs, out_shape=o_shape,
                              scratch_shapes=scratch, input_output_aliases=aliases, name=name,
                              compiler_params=_cparams())

    if task is None:
        return make(body, list(in_specs), out_specs, out_shape, list(scratch_shapes), {})
    single = not isinstance(out_shape, (tuple, list))
    o_shapes = [out_shape] if single else list(out_shape)
    o_specs = [out_specs] if single else list(out_specs)
    n_in, n_out, n_scr = len(in_specs), len(o_shapes), len(scratch_shapes)
    t_in, t_out = len(task.arrays), len(task.out_shapes)

    def carried(*refs):
        cut = [prefetch, n_in, t_in, n_out, t_out, n_scr]
        parts, p = [], 0
        for c in cut:
            parts.append(refs[p:p + c])
            p += c
        scalars, ins, tins, outs, touts, scr = parts
        tsems = refs[p:]
        ids = [pl.program_id(a) for a in range(len(grid))]
        first = functools.reduce(jnp.logical_and, [i == 0 for i in ids])
        last = functools.reduce(jnp.logical_and, [i == g - 1 for i, g in zip(ids, grid)])

        @pl.when(first)
        def _():
            task.issue(tins, touts, tsems)
        body(*scalars, *ins, *outs, *scr)

        @pl.when(last)
        def _():
            task.drain(tins, touts, tsems)

    call = make(carried, list(in_specs) + [ANY] * t_in, o_specs + [ANY] * t_out,
                o_shapes + list(task.out_shapes), list(scratch_shapes) + list(task.sems),
                {prefetch + n_in + i: n_out + j for i, j in task.aliases.items()})

    def run(*args):
        res = call(*args, *task.arrays)
        outs = res[:n_out]
        return (outs[0] if single else tuple(outs)), list(res[n_out:])

    return run


def _dot(a, b):
    return jnp.dot(a, b, preferred_element_type=F32)


def _dot_nt(a, b):
    return lax.dot_general(a, b, (((1,), (1,)), ((), ())), preferred_element_type=F32)


def _dot_tn(a, b):
    return lax.dot_general(a, b, (((0,), (0,)), ((), ())), preferred_element_type=F32)


def _split3(x):
    hi = x.astype(BF16)
    r1 = x - hi.astype(F32)
    mid = r1.astype(BF16)
    lo = (r1 - mid.astype(F32)).astype(BF16)
    return hi, mid, lo


def _dot3(x, m01):
    hi, mid, lo = _split3(x)
    return _dot(hi, m01) + _dot(mid, m01) + _dot(lo, m01)


def _dot3_l(m01, x):
    hi, mid, lo = _split3(x)
    return _dot(m01, hi) + _dot(m01, mid) + _dot(m01, lo)


def _mm_nn(a, b, kind, *, name, out_dtype=BF16, tm=2048, tn=512, epi=None, task=None):
    M, K = a.shape
    if kind == "plain":
        N = b.shape[1]
        b_spec = pl.BlockSpec((K, tn), lambda m, n: (0, n))
    elif kind == "rows":
        N = b.shape[2]
        b_spec = pl.BlockSpec((N_CHIP, K // N_CHIP, tn), lambda m, n: (0, 0, n))
    else:
        nq = b.shape[2]
        N = N_CHIP * nq
        per = nq // tn
        b_spec = pl.BlockSpec((None, K, tn), lambda m, n: (n // per, 0, n % per))
    tm = min(tm, M)
    kq = K // N_CHIP

    def body(a_ref, b_ref, *o_refs):
        if kind == "rows":
            acc = _dot(a_ref[:, 0:kq], b_ref[0])
            for j in range(1, N_CHIP):
                acc += _dot(a_ref[:, j * kq:(j + 1) * kq], b_ref[j])
        else:
            acc = _dot(a_ref[...], b_ref[...])
        if epi == "relu2":
            r = jnp.maximum(acc, 0.0)
            o_refs[0][...] = (r * r).astype(BF16)
            o_refs[1][...] = r.astype(BF16)
        else:
            o_refs[0][...] = acc.astype(out_dtype)

    o_spec = pl.BlockSpec((tm, tn), lambda m, n: (m, n))
    if epi == "relu2":
        out_shape = (jax.ShapeDtypeStruct((M, N), BF16), jax.ShapeDtypeStruct((M, N), BF16))
        out_specs = (o_spec, o_spec)
    else:
        out_shape = jax.ShapeDtypeStruct((M, N), out_dtype)
        out_specs = o_spec
    return _pallas(
        body, grid=(M // tm, N // tn),
        in_specs=[pl.BlockSpec((tm, K), lambda m, n: (m, 0)), b_spec],
        out_specs=out_specs, out_shape=out_shape, name=name, task=task,
    )(a, b)


def _mm_nt(a, b, kind, *, name, out_dtype=BF16, tm=2048, tn=512, mul2r=None, task=None, rows=None):
    M, K = a.shape
    if kind == "plain":
        first, N = rows if rows is not None else (0, b.shape[0])
        n0 = first // tn
        b_spec = pl.BlockSpec((tn, K), lambda m, n: (n0 + n, 0))
    elif kind == "rows":
        nq = b.shape[1]
        N = N_CHIP * nq
        tn = min(tn, nq)
        per = nq // tn
        b_spec = pl.BlockSpec((None, tn, K), lambda m, n: (n // per, n % per, 0))
    else:
        N = b.shape[1]
        b_spec = pl.BlockSpec((N_CHIP, tn, K // N_CHIP), lambda m, n: (0, n, 0))
    tm = min(tm, M)
    kq = K // N_CHIP

    def body(a_ref, b_ref, *rest):
        o_ref = rest[-1]
        if kind == "cols":
            acc = _dot_nt(a_ref[:, 0:kq], b_ref[0])
            for j in range(1, N_CHIP):
                acc += _dot_nt(a_ref[:, j * kq:(j + 1) * kq], b_ref[j])
        else:
            acc = _dot_nt(a_ref[...], b_ref[...])
        if mul2r is not None:
            acc = acc * (2.0 * rest[0][...].astype(F32))
        o_ref[...] = acc.astype(out_dtype)

    in_specs = [pl.BlockSpec((tm, K), lambda m, n: (m, 0)), b_spec]
    args = [a, b]
    if mul2r is not None:
        in_specs.append(pl.BlockSpec((tm, tn), lambda m, n: (m, n)))
        args.append(mul2r)
    return _pallas(
        body, grid=(M // tm, N // tn), in_specs=in_specs,
        out_specs=pl.BlockSpec((tm, tn), lambda m, n: (m, n)),
        out_shape=jax.ShapeDtypeStruct((M, N), out_dtype), name=name, task=task,
    )(*args)


def _mm_tn(a, b, *, name, out_dtype=BF16, tk=1024, tn=512, cols4=False, task=None):
    M, K1 = a.shape
    N = b.shape[1]
    tk = min(tk, K1)
    tn = min(tn, N)

    def body(a_ref, b_ref, o_ref):
        o_ref[...] = _dot_tn(a_ref[...], b_ref[...]).astype(out_dtype)

    if cols4:
        per = (N // N_CHIP) // tn
        out_shape = jax.ShapeDtypeStruct((N_CHIP, K1, N // N_CHIP), out_dtype)
        o_spec = pl.BlockSpec((None, tk, tn), lambda k, n: (n // per, k, n % per))
    else:
        out_shape = jax.ShapeDtypeStruct((K1, N), out_dtype)
        o_spec = pl.BlockSpec((tk, tn), lambda k, n: (k, n))
    return _pallas(
        body, grid=(K1 // tk, N // tn),
        in_specs=[pl.BlockSpec((M, tk), lambda k, n: (0, k)), pl.BlockSpec((M, tn), lambda k, n: (0, n))],
        out_specs=o_spec, out_shape=out_shape, name=name, task=task,
    )(a, b)


def _rms(x, g):
    r = lax.rsqrt(jnp.mean(x * x, axis=-1, keepdims=True) + EPS)
    return x * r * g


def _rms_bwd(x, g, dy):
    r = lax.rsqrt(jnp.mean(x * x, axis=-1, keepdims=True) + EPS)
    xh = x * r
    dg = jnp.sum(dy * xh, axis=0, keepdims=True)
    dxh = dy * g
    dx = r * (dxh - xh * jnp.mean(dxh * xh, axis=-1, keepdims=True))
    return dx, dg


def _row_spec(tm, n):
    return pl.BlockSpec((tm, n), lambda i: (i, 0))


def _vec_spec(n):
    return pl.BlockSpec((1, n), lambda i: (0, 0))


def _acc_spec(n):
    return pl.BlockSpec((8, n), lambda i: (0, 0))


def _acc_add(ref, row, i):
    @pl.when(i == 0)
    def _():
        ref[...] = jnp.zeros_like(ref)
    ref[0:1, :] += row


def _rms_fwd_call(x, g, *, name, tm=256, task=None):
    M, n = x.shape
    tm = min(tm, M)

    def body(x_ref, g_ref, h_ref):
        h_ref[...] = _rms(x_ref[...], g_ref[...]).astype(BF16)

    return _pallas(
        body, grid=(M // tm,), in_specs=[_row_spec(tm, n), _vec_spec(n)], out_specs=_row_spec(tm, n),
        out_shape=jax.ShapeDtypeStruct((M, n), BF16), name=name, task=task,
    )(x, g)


def _post_pre_call(xres, z, g_post, g_pre, *, name, tm=512):
    M, n = xres.shape

    def body(x_ref, z_ref, gp_ref, gn_ref, xo_ref, h_ref):
        xn = x_ref[...] + _rms(z_ref[...], gp_ref[...])
        xo_ref[...] = xn
        h_ref[...] = _rms(xn, gn_ref[...]).astype(BF16)

    return pl.pallas_call(
        body, grid=(M // tm,),
        in_specs=[_row_spec(tm, n), _row_spec(tm, n), _vec_spec(n), _vec_spec(n)],
        out_specs=(_row_spec(tm, n), _row_spec(tm, n)),
        out_shape=(jax.ShapeDtypeStruct((M, n), F32), jax.ShapeDtypeStruct((M, n), BF16)),
        name=name, compiler_params=_cparams(),
    )(xres, z, g_post, g_pre)


def _final_call(x2, y3, g_post, target, *, name, tm=512):
    M, n = x2.shape

    def body(x_ref, y_ref, g_ref, t_ref, loss_ref, dx_ref, dy_ref, dg_ref):
        i = pl.program_id(0)
        y = y_ref[...]
        g = g_ref[...]
        diff = x_ref[...] + _rms(y, g) - t_ref[...]
        part = 0.5 * jnp.sum(jnp.sum(diff * diff, axis=1, keepdims=True), axis=0, keepdims=True) / n

        @pl.when(i == 0)
        def _():
            loss_ref[...] = jnp.zeros_like(loss_ref)
        loss_ref[...] += jnp.broadcast_to(part, loss_ref.shape)
        dx = diff / n
        dx_ref[...] = dx
        dy, dg = _rms_bwd(y, g, dx)
        dy_ref[...] = dy.astype(BF16)
        _acc_add(dg_ref, dg, i)

    return pl.pallas_call(
        body, grid=(M // tm,),
        in_specs=[_row_spec(tm, n), _row_spec(tm, n), _vec_spec(n), _row_spec(tm, n)],
        out_specs=(pl.BlockSpec((8, 128), lambda i: (0, 0)), _row_spec(tm, n), _row_spec(tm, n), _acc_spec(n)),
        out_shape=(jax.ShapeDtypeStruct((8, 128), F32), jax.ShapeDtypeStruct((M, n), F32),
                   jax.ShapeDtypeStruct((M, n), BF16), jax.ShapeDtypeStruct((8, n), F32)),
        name=name, compiler_params=_cparams(),
    )(x2, y3, g_post, target)


def _bwd_mid_call(dx_in, x, dh, g_pre, y, g_post, *, name, tm=512):
    M, n = x.shape

    def body(dxi_ref, x_ref, dh_ref, gpre_ref, y_ref, gpost_ref, dx_ref, dy_ref, dgpre_ref, dgpost_ref):
        i = pl.program_id(0)
        d1, dg1 = _rms_bwd(x_ref[...], gpre_ref[...], dh_ref[...])
        dx = dxi_ref[...] + d1
        dx_ref[...] = dx
        dy, dg2 = _rms_bwd(y_ref[...], gpost_ref[...], dx)
        dy_ref[...] = dy.astype(BF16)
        _acc_add(dgpre_ref, dg1, i)
        _acc_add(dgpost_ref, dg2, i)

    return pl.pallas_call(
        body, grid=(M // tm,),
        in_specs=[_row_spec(tm, n), _row_spec(tm, n), _row_spec(tm, n), _vec_spec(n), _row_spec(tm, n), _vec_spec(n)],
        out_specs=(_row_spec(tm, n), _row_spec(tm, n), _acc_spec(n), _acc_spec(n)),
        out_shape=(jax.ShapeDtypeStruct((M, n), F32), jax.ShapeDtypeStruct((M, n), BF16),
                   jax.ShapeDtypeStruct((8, n), F32), jax.ShapeDtypeStruct((8, n), F32)),
        name=name, compiler_params=_cparams(),
    )(dx_in, x, dh, g_pre, y, g_post)


def _bwd_last_call(dx_in, x, dh, g_pre, *, name, tm=512, task=None):
    M, n = x.shape

    def body(dxi_ref, x_ref, dh_ref, g_ref, dx_ref, dg_ref):
        i = pl.program_id(0)
        d1, dg1 = _rms_bwd(x_ref[...], g_ref[...], dh_ref[...])
        dx_ref[...] = dxi_ref[...] + d1
        _acc_add(dg_ref, dg1, i)

    return _pallas(
        body, grid=(M // tm,),
        in_specs=[_row_spec(tm, n), _row_spec(tm, n), _row_spec(tm, n), _vec_spec(n)],
        out_specs=(_row_spec(tm, n), _acc_spec(n)),
        out_shape=(jax.ShapeDtypeStruct((M, n), F32), jax.ShapeDtypeStruct((8, n), F32)),
        name=name, task=task,
    )(dx_in, x, dh, g_pre)


def _gain_grad_call(x, g, dy_a, dy_b, *, name):
    M, n = x.shape

    def body(x_ref, g_ref, a_ref, b_ref, dg_ref):
        _, dg = _rms_bwd(x_ref[...], g_ref[...], a_ref[...] + b_ref[...])
        dg_ref[...] = jnp.zeros_like(dg_ref)
        dg_ref[0:1, :] = dg

    return pl.pallas_call(
        body, grid=(1,),
        in_specs=[_row_spec(M, n), _vec_spec(n), _row_spec(M, n), _row_spec(M, n)],
        out_specs=_acc_spec(n), out_shape=jax.ShapeDtypeStruct((8, n), F32),
        name=name, compiler_params=_cparams(),
    )(x, g, dy_a, dy_b)


def _head_group_matrix():
    a = lax.broadcasted_iota(jnp.int32, (D_GRP, D_GRP), 0) // HEAD
    b = lax.broadcasted_iota(jnp.int32, (D_GRP, D_GRP), 1) // HEAD
    return jnp.where(a == b, 1.0, 0.0).astype(BF16)


def _mix_norm_fwd_call(yf, yc, gf, gc, *, name, tm=256):
    M = yf.shape[0]

    def body(yf_ref, yc_ref, gf_ref, gc_ref, o_ref):
        o_ref[:, 0:D_GRP] = _rms(yf_ref[...], gf_ref[...]).astype(BF16)
        o_ref[:, D_GRP:D] = _rms(yc_ref[...], gc_ref[...]).astype(BF16)

    return pl.pallas_call(
        body, grid=(M // tm,),
        in_specs=[_row_spec(tm, D_GRP), _row_spec(tm, D_GRP), _vec_spec(D_GRP), _vec_spec(D_GRP)],
        out_specs=_row_spec(tm, D), out_shape=jax.ShapeDtypeStruct((M, D), BF16),
        name=name, compiler_params=_cparams(),
    )(yf, yc, gf, gc)


def _mix_norm_bwd_call(dyn, yf, yc, gf, gc, *, name, tm=TQ):
    M = yf.shape[0]

    def body(dyn_ref, yf_ref, yc_ref, gf_ref, gc_ref, dof_ref, doc_ref, delta_ref, dgf_ref, dgc_ref):
        i = pl.program_id(0)
        yf_ = yf_ref[...]
        dof, dgf = _rms_bwd(yf_, gf_ref[...], dyn_ref[:, 0:D_GRP])
        doc, dgc = _rms_bwd(yc_ref[...], gc_ref[...], dyn_ref[:, D_GRP:D])
        dof_b = dof.astype(BF16)
        dof_ref[...] = dof_b
        doc_ref[...] = doc.astype(BF16)
        prod = dof_b.astype(F32) * yf_
        hi = prod.astype(BF16)
        lo = (prod - hi.astype(F32)).astype(BF16)
        grp = _head_group_matrix()
        delta_ref[...] = (_dot(hi, grp) + _dot(lo, grp)).T
        _acc_add(dgf_ref, dgf, i)
        _acc_add(dgc_ref, dgc, i)

    return pl.pallas_call(
        body, grid=(M // tm,),
        in_specs=[_row_spec(tm, D), _row_spec(tm, D_GRP), _row_spec(tm, D_GRP), _vec_spec(D_GRP), _vec_spec(D_GRP)],
        out_specs=(_row_spec(tm, D_GRP), _row_spec(tm, D_GRP), pl.BlockSpec((None, D_GRP, tm), lambda i: (i, 0, 0)),
                   _acc_spec(D_GRP), _acc_spec(D_GRP)),
        out_shape=(jax.ShapeDtypeStruct((M, D_GRP), BF16), jax.ShapeDtypeStruct((M, D_GRP), BF16),
                   jax.ShapeDtypeStruct((M // tm, D_GRP, tm), F32), jax.ShapeDtypeStruct((8, D_GRP), F32),
                   jax.ShapeDtypeStruct((8, D_GRP), F32)),
        name=name, compiler_params=_cparams(),
    )(dyn, yf, yc, gf, gc)


def _tri(n, lower_incl):
    a = lax.broadcasted_iota(jnp.int32, (n, n), 0)
    b = lax.broadcasted_iota(jnp.int32, (n, n), 1)
    return jnp.where(a >= b, 1.0, 0.0).astype(BF16) if lower_incl else jnp.where(a <= b, 1.0, 0.0).astype(BF16)


def _fox_prep_call(fl_raw, b_pad, *, name):
    S = fl_raw.shape[0]
    nb = S // TQ

    def body(fl_ref, b_ref, crep_ref, ct_ref, carry_ref):
        i = pl.program_id(0)

        @pl.when(i == 0)
        def _():
            carry_ref[...] = jnp.zeros_like(carry_ref)
        logf = jax.nn.log_sigmoid(fl_ref[...] + b_ref[...])
        cb = _dot3_l(_tri(TQ, True), logf) + carry_ref[0:1, :]
        carry_ref[0:1, :] = cb[TQ - 1:TQ, :]
        a = lax.broadcasted_iota(jnp.int32, (128, D_GRP), 0)
        b = lax.broadcasted_iota(jnp.int32, (128, D_GRP), 1) // HEAD
        expand = jnp.where(a == b, 1.0, 0.0).astype(BF16)
        crep = _dot3(cb, expand)
        crep_ref[...] = crep
        ct_ref[...] = crep.T

    return pl.pallas_call(
        body, grid=(nb,),
        in_specs=[_row_spec(TQ, 128), _vec_spec(128)],
        out_specs=(_row_spec(TQ, D_GRP), pl.BlockSpec((None, D_GRP, TQ), lambda i: (i, 0, 0))),
        out_shape=(jax.ShapeDtypeStruct((S, D_GRP), F32), jax.ShapeDtypeStruct((nb, D_GRP, TQ), F32)),
        scratch_shapes=[pltpu.VMEM((8, 128), F32)],
        name=name, compiler_params=_cparams(),
    )(fl_raw, b_pad)


def _lane_masks():
    lane = lax.broadcasted_iota(jnp.int32, (1, 128), 1)
    return lane < HEAD, lane >= HEAD


def _fox_fwd_call(proj, c_rep, c_t, *, name, task=None):
    S = proj.shape[0]
    nq = S // TQ
    scale = HEAD ** -0.5

    def body(q_ref, k_ref, v_ref, c_ref, ct_ref, o_ref, lse_ref):
        i = pl.program_id(1)
        m_lo, m_hi = _lane_masks()
        masks = (m_lo, m_hi)
        q = q_ref[...]
        qm = [jnp.where(mk, q, jnp.zeros_like(q)) for mk in masks]
        cq = c_ref[...]
        cqh = [cq[:, 0:1], cq[:, HEAD:HEAD + 1]]
        row = lax.broadcasted_iota(jnp.int32, (TQ, TQ), 0)
        col = lax.broadcasted_iota(jnp.int32, (TQ, TQ), 1)

        def scores(j):
            start = pl.multiple_of(j * TQ, TQ)
            k = k_ref[pl.ds(start, TQ), :]
            ct = ct_ref[j]
            return tuple(_dot_nt(qm[h], k) * scale + (cqh[h] - ct[HEAD * h:HEAD * h + 1, :]) for h in range(2))

        def update(j, ss, state, masked):
            ms, ls, acc = state
            start = pl.multiple_of(j * TQ, TQ)
            v = v_ref[pl.ds(start, TQ), :]
            new_m, new_l, pv, alpha_l = [], [], [], []
            for h in range(2):
                s = ss[h]
                if masked:
                    s = jnp.where(row >= col, s, NEG)
                mn = jnp.maximum(ms[h], jnp.max(s, axis=1, keepdims=True))
                alpha = jnp.exp(ms[h] - mn)
                p = jnp.exp(s - mn)
                new_l.append(alpha * ls[h] + jnp.sum(p, axis=1, keepdims=True))
                new_m.append(mn)
                alpha_l.append(alpha)
                pv.append(_dot(p.astype(BF16), jnp.where(masks[h], v, jnp.zeros_like(v))))
            alpha_lane = jnp.where(m_lo, alpha_l[0], alpha_l[1])
            acc = acc * alpha_lane + pv[0] + pv[1]
            return (tuple(new_m), tuple(new_l), acc)

        def step(j, carry):
            ss, state = carry
            return (scores(j + 1), update(j, ss, state, False))

        init = ((jnp.full((TQ, 1), NEG, F32),) * 2, (jnp.zeros((TQ, 1), F32),) * 2, jnp.zeros((TQ, 128), F32))
        ss, state = lax.fori_loop(0, i, step, (scores(0), init))
        ms, ls, acc = update(i, ss, state, True)
        l_lane = jnp.where(m_lo, ls[0], ls[1])
        o_ref[...] = acc / l_lane
        lse_ref[...] = jnp.where(m_lo, ms[0] + jnp.log(ls[0]), ms[1] + jnp.log(ls[1])).T

    return _pallas(
        body, grid=(N_PAIR, nq),
        in_specs=[pl.BlockSpec((TQ, 128), lambda p, i: (i, p)),
                  pl.BlockSpec((S, 128), lambda p, i: (0, N_PAIR + p)),
                  pl.BlockSpec((S, 128), lambda p, i: (0, 2 * N_PAIR + p)),
                  pl.BlockSpec((TQ, 128), lambda p, i: (i, p)),
                  pl.BlockSpec((nq, 128, TQ), lambda p, i: (0, p, 0))],
        out_specs=(pl.BlockSpec((TQ, 128), lambda p, i: (i, p)), pl.BlockSpec((None, 128, TQ), lambda p, i: (i, p, 0))),
        out_shape=(jax.ShapeDtypeStruct((S, D_GRP), F32), jax.ShapeDtypeStruct((nq, D_GRP, TQ), F32)),
        name=name, task=task,
    )(proj, proj, proj, c_rep, c_t)


def _fox_bwd_call(proj, do, lse_t, delta_t, c_rep, c_t, *, name, task=None):
    S = proj.shape[0]
    nq = S // TQ
    scale = HEAD ** -0.5

    def body(q_ref, k_ref, v_ref, do_ref, lse_ref, dl_ref, ck_ref, ct_ref,
             dq_ref, dk_ref, dv_ref, dcq_ref, dck_ref, dqa_ref):
        j = pl.program_id(1)
        m_lo, m_hi = _lane_masks()
        masks = (m_lo, m_hi)

        @pl.when(j == 0)
        def _():
            dqa_ref[...] = jnp.zeros_like(dqa_ref)
            dcq_ref[...] = jnp.zeros_like(dcq_ref)
        k = k_ref[...]
        v = v_ref[...]
        km = [jnp.where(mk, k, jnp.zeros_like(k)) for mk in masks]
        ck = ck_ref[...]
        krow = lax.broadcasted_iota(jnp.int32, (TQ, TQ), 0)
        qcol = lax.broadcasted_iota(jnp.int32, (TQ, TQ), 1)

        def probs(i):
            start = pl.multiple_of(i * TQ, TQ)
            q = q_ref[pl.ds(start, TQ), :]
            do = do_ref[pl.ds(start, TQ), :]
            lse = lse_ref[i]
            cq = ct_ref[i]
            out = []
            for h in range(2):
                lo = HEAD * h
                qm = jnp.where(masks[h], q, jnp.zeros_like(q))
                dom = jnp.where(masks[h], do, jnp.zeros_like(do))
                st = _dot_nt(k, qm) * scale + (cq[lo:lo + 1, :] - ck[:, lo:lo + 1])
                out.append((jnp.exp(st - lse[lo:lo + 1, :]), _dot_nt(v, dom)))
            return tuple(out)

        def update(i, pd, carry, masked):
            dk, dv, dck = carry
            start = pl.multiple_of(i * TQ, TQ)
            q = q_ref[pl.ds(start, TQ), :]
            do = do_ref[pl.ds(start, TQ), :]
            dl = dl_ref[i]
            dq = jnp.zeros((TQ, 128), F32)
            new_dck = []
            for h in range(2):
                lo = HEAD * h
                qm = jnp.where(masks[h], q, jnp.zeros_like(q))
                dom = jnp.where(masks[h], do, jnp.zeros_like(do))
                pt, dpt = pd[h]
                if masked:
                    pt = jnp.where(qcol >= krow, pt, 0.0)
                dst = pt * (dpt - dl[lo:lo + 1, :])
                dcq_ref[i, h:h + 1, :] += jnp.sum(dst, axis=0, keepdims=True)
                new_dck.append(dck[h] + jnp.sum(dst, axis=1, keepdims=True))
                dsb = (dst * scale).astype(BF16)
                dv = dv + _dot(pt.astype(BF16), dom)
                dk = dk + _dot(dsb, qm)
                dq = dq + _dot_tn(dsb, km[h])
            dqa_ref[pl.ds(start, TQ), :] += dq
            return (dk, dv, tuple(new_dck))

        def step(i, carry):
            pd, sums = carry
            return (probs(jnp.minimum(i + 1, nq - 1)), update(i, pd, sums, False))

        init = (jnp.zeros((TQ, 128), F32), jnp.zeros((TQ, 128), F32), (jnp.zeros((TQ, 1), F32),) * 2)
        first = probs(j)
        second = probs(jnp.minimum(j + 1, nq - 1))
        _, (dk, dv, dck) = lax.fori_loop(j + 1, nq, step, (second, update(j, first, init, True)))
        dk_ref[...] = dk.astype(BF16)
        dv_ref[...] = dv.astype(BF16)
        dck_ref[...] = -jnp.where(m_lo, dck[0], dck[1])

        @pl.when(j == nq - 1)
        def _():
            dq_ref[...] = dqa_ref[...].astype(BF16)

    res = lambda p, j: (0, p)
    stat = pl.BlockSpec((nq, 128, TQ), lambda p, j: (0, p, 0))
    blk = pl.BlockSpec((TQ, 128), lambda p, j: (j, p))
    return _pallas(
        body, grid=(N_PAIR, nq), task=task,
        in_specs=[pl.BlockSpec((S, 128), res),
                  pl.BlockSpec((TQ, 128), lambda p, j: (j, N_PAIR + p)),
                  pl.BlockSpec((TQ, 128), lambda p, j: (j, 2 * N_PAIR + p)),
                  pl.BlockSpec((S, 128), res), stat, stat, blk, stat],
        out_specs=(pl.BlockSpec((S, 128), res), blk, blk,
                   pl.BlockSpec((None, nq, 8, TQ), lambda p, j: (p, 0, 0, 0)), blk),
        out_shape=(jax.ShapeDtypeStruct((S, D_GRP), BF16), jax.ShapeDtypeStruct((S, D_GRP), BF16),
                   jax.ShapeDtypeStruct((S, D_GRP), BF16), jax.ShapeDtypeStruct((N_PAIR, nq, 8, TQ), F32),
                   jax.ShapeDtypeStruct((S, D_GRP), F32)),
        scratch_shapes=[pltpu.VMEM((S, 128), F32)],
        name=name,
    )(proj, proj, proj, do, lse_t, delta_t, c_rep, c_t)


def _fox_gate_bwd_call(dc_rows, fl_raw, b_pad, *, name):
    S = fl_raw.shape[0]
    nb = S // TQ

    def body(dc_ref, fl_ref, b_ref, dfl_ref, db_ref, carry_ref):
        i = pl.program_id(0)

        @pl.when(i == 0)
        def _():
            carry_ref[...] = jnp.zeros_like(carry_ref)
        rc = _dot3(dc_ref[...], _tri(TQ, True)) + carry_ref[:, 0:1]
        carry_ref[...] = jnp.broadcast_to(rc[:, 0:1], carry_ref.shape)
        fl = fl_ref[...] + b_ref[...]
        dfl = rc.T * jax.nn.sigmoid(-fl)
        dfl_ref[...] = dfl.astype(BF16)
        _acc_add(db_ref, jnp.sum(dfl, axis=0, keepdims=True), i)

    rev = lambda i: (nb - 1 - i, 0)
    return pl.pallas_call(
        body, grid=(nb,),
        in_specs=[pl.BlockSpec((128, TQ), lambda i: (0, nb - 1 - i)), pl.BlockSpec((TQ, 128), rev), _vec_spec(128)],
        out_specs=(pl.BlockSpec((TQ, 128), rev), _acc_spec(128)),
        out_shape=(jax.ShapeDtypeStruct((S, 128), BF16), jax.ShapeDtypeStruct((8, 128), F32)),
        scratch_shapes=[pltpu.VMEM((128, 128), F32)],
        name=name, compiler_params=_cparams(),
    )(dc_rows, fl_raw, b_pad)


def _chk_bias_call(g_rev, *, name):
    def body(g_ref, o_ref):
        x = jnp.broadcast_to(g_ref[...], (TQ, ROLL_W))
        rolled = pltpu.roll(x, ROLL_W - (TQ - 1), 1, stride=1, stride_axis=0)
        qc = lax.broadcasted_iota(jnp.int32, (TQ, WIN), 0) // CHUNK
        kc = lax.broadcasted_iota(jnp.int32, (TQ, WIN), 1) // CHUNK
        band = (kc >= qc) & (kc <= qc + LEFT)
        o_ref[...] = jnp.where(band, rolled[:, 0:WIN], NEG)

    return pl.pallas_call(
        body, grid=(8,),
        in_specs=[pl.BlockSpec((None, 1, ROLL_W), lambda h: (h, 0, 0))],
        out_specs=pl.BlockSpec((None, TQ, WIN), lambda h: (h, 0, 0)),
        out_shape=jax.ShapeDtypeStruct((8, TQ, WIN), F32), name=name, compiler_params=_cparams(),
    )(g_rev.reshape(8, 1, ROLL_W))


def _chk_scores(i, qm, kwin, bias, scale):
    s = _dot_nt(qm, kwin) * scale + bias
    kc = lax.broadcasted_iota(jnp.int32, (TQ, WIN), 1) // CHUNK
    return jnp.where(kc + i * (TQ // CHUNK) >= LEFT, s, NEG)


def _chk_fwd_call(proj, bias, *, name, task=None):
    S = proj.shape[0]
    nq = S // TQ
    scale = HEAD ** -0.5

    def body(q_ref, k_ref, v_ref, b_ref, o_ref, kp_ref, vp_ref):
        i = pl.program_id(1)

        @pl.when(i == 0)
        def _():
            kp_ref[0:PADK, :] = jnp.zeros((PADK, 128), BF16)
            vp_ref[0:PADK, :] = jnp.zeros((PADK, 128), BF16)
            kp_ref[PADK:PADK + S, :] = k_ref[...]
            vp_ref[PADK:PADK + S, :] = v_ref[...]
        masks = _lane_masks()
        q = q_ref[...]
        start = pl.multiple_of(i * TQ, TQ)
        kwin = kp_ref[pl.ds(start, WIN), :]
        vwin = vp_ref[pl.ds(start, WIN), :]
        ss = [_chk_scores(i, jnp.where(masks[h], q, jnp.zeros_like(q)), kwin, b_ref[h], scale) for h in range(2)]
        ps = []
        for s in ss:
            p = jnp.exp(s - jnp.max(s, axis=1, keepdims=True))
            ps.append((p / jnp.sum(p, axis=1, keepdims=True)).astype(BF16))
        o_ref[...] = (_dot(ps[0], jnp.where(masks[0], vwin, jnp.zeros_like(vwin)))
                      + _dot(ps[1], jnp.where(masks[1], vwin, jnp.zeros_like(vwin))))

    c0 = 3 * N_PAIR
    return _pallas(
        body, grid=(N_PAIR, nq), task=task,
        in_specs=[pl.BlockSpec((TQ, 128), lambda p, i: (i, c0 + p)),
                  pl.BlockSpec((S, 128), lambda p, i: (0, c0 + N_PAIR + p)),
                  pl.BlockSpec((S, 128), lambda p, i: (0, c0 + 2 * N_PAIR + p)),
                  pl.BlockSpec((2, TQ, WIN), lambda p, i: (p, 0, 0))],
        out_specs=pl.BlockSpec((TQ, 128), lambda p, i: (i, p)),
        out_shape=jax.ShapeDtypeStruct((S, D_GRP), F32),
        scratch_shapes=[pltpu.VMEM((S + PADK, 128), BF16), pltpu.VMEM((S + PADK, 128), BF16)],
        name=name,
    )(proj, proj, proj, bias)


def _chk_bwd_call(proj, do, bias, *, name, task=None):
    S = proj.shape[0]
    nq = S // TQ
    scale = HEAD ** -0.5

    def body(q_ref, k_ref, v_ref, do_ref, b_ref, dq_ref, dk_ref, dv_ref, dg_ref, kp_ref, vp_ref, dkp_ref, dvp_ref, db_ref):
        i = pl.program_id(1)

        @pl.when(i == 0)
        def _():
            kp_ref[0:PADK, :] = jnp.zeros((PADK, 128), BF16)
            vp_ref[0:PADK, :] = jnp.zeros((PADK, 128), BF16)
            kp_ref[PADK:PADK + S, :] = k_ref[...]
            vp_ref[PADK:PADK + S, :] = v_ref[...]
            dkp_ref[...] = jnp.zeros_like(dkp_ref)
            dvp_ref[...] = jnp.zeros_like(dvp_ref)
            db_ref[...] = jnp.zeros_like(db_ref)
        masks = _lane_masks()
        q = q_ref[...]
        dout = do_ref[...]
        start = pl.multiple_of(i * TQ, TQ)
        kwin = kp_ref[pl.ds(start, WIN), :]
        vwin = vp_ref[pl.ds(start, WIN), :]
        qm = [jnp.where(mk, q, jnp.zeros_like(q)) for mk in masks]
        dom = [jnp.where(mk, dout, jnp.zeros_like(dout)) for mk in masks]
        ss = [_chk_scores(i, qm[h], kwin, b_ref[h], scale) for h in range(2)]
        dps = [_dot_nt(dom[h], vwin) for h in range(2)]
        pbs, dsbs = [], []
        for h in range(2):
            p = jnp.exp(ss[h] - jnp.max(ss[h], axis=1, keepdims=True))
            p = p / jnp.sum(p, axis=1, keepdims=True)
            ds = p * (dps[h] - jnp.sum(p * dps[h], axis=1, keepdims=True))
            db_ref[h] += ds
            pbs.append(p.astype(BF16))
            dsbs.append((ds * scale).astype(BF16))
        dq_ref[...] = (_dot(dsbs[0], jnp.where(masks[0], kwin, jnp.zeros_like(kwin)))
                       + _dot(dsbs[1], jnp.where(masks[1], kwin, jnp.zeros_like(kwin)))).astype(BF16)
        dkp_ref[pl.ds(start, WIN), :] += _dot_tn(dsbs[0], qm[0]) + _dot_tn(dsbs[1], qm[1])
        dvp_ref[pl.ds(start, WIN), :] += _dot_tn(pbs[0], dom[0]) + _dot_tn(pbs[1], dom[1])

        @pl.when(i == nq - 1)
        def _():
            dk_ref[...] = dkp_ref[PADK:PADK + S, :].astype(BF16)
            dv_ref[...] = dvp_ref[PADK:PADK + S, :].astype(BF16)
            a = lax.broadcasted_iota(jnp.int32, (TQ, TQ), 0)
            b = lax.broadcasted_iota(jnp.int32, (TQ, TQ), 1)
            flip = jnp.where(a + b == TQ - 1, 1.0, 0.0).astype(BF16)
            e = lax.broadcasted_iota(jnp.int32, (1, ROLL_W), 1)
            dg_ref[...] = jnp.zeros_like(dg_ref)
            for h in range(2):
                rev = _dot3_l(flip, db_ref[h])
                wide = jnp.concatenate([rev, jnp.zeros((TQ, ROLL_W - WIN), F32)], axis=1)
                diag = pltpu.roll(wide, 0, 1, stride=1, stride_axis=0)
                dg = jnp.sum(diag, axis=0, keepdims=True)
                lo = jnp.sum(jnp.where(e <= 639, dg, 0.0), axis=1, keepdims=True)
                hi = jnp.sum(jnp.where(e >= 895, dg, 0.0), axis=1, keepdims=True)
                dg_ref[h:h + 1, :] = jnp.where(e == 639, lo, jnp.where(e == 895, hi, dg))

    c0 = 3 * N_PAIR
    res = lambda p, i: (0, p)
    return _pallas(
        body, grid=(N_PAIR, nq), task=task,
        in_specs=[pl.BlockSpec((TQ, 128), lambda p, i: (i, c0 + p)),
                  pl.BlockSpec((S, 128), lambda p, i: (0, c0 + N_PAIR + p)),
                  pl.BlockSpec((S, 128), lambda p, i: (0, c0 + 2 * N_PAIR + p)),
                  pl.BlockSpec((TQ, 128), lambda p, i: (i, p)),
                  pl.BlockSpec((2, TQ, WIN), lambda p, i: (p, 0, 0))],
        out_specs=(pl.BlockSpec((TQ, 128), lambda p, i: (i, p)), pl.BlockSpec((S, 128), res),
                   pl.BlockSpec((S, 128), res), pl.BlockSpec((None, 8, ROLL_W), lambda p, i: (p, 0, 0))),
        out_shape=(jax.ShapeDtypeStruct((S, D_GRP), BF16), jax.ShapeDtypeStruct((S, D_GRP), BF16),
                   jax.ShapeDtypeStruct((S, D_GRP), BF16), jax.ShapeDtypeStruct((N_PAIR, 8, ROLL_W), F32)),
        scratch_shapes=[pltpu.VMEM((S + PADK, 128), BF16), pltpu.VMEM((S + PADK, 128), BF16),
                        pltpu.VMEM((S + PADK, 128), F32), pltpu.VMEM((S + PADK, 128), F32),
                        pltpu.VMEM((2, TQ, WIN), F32)],
        name=name,
    )(proj, proj, proj, do, bias)


def _mem_fwd_call(q, k, v, *, name, tq=512):
    S = q.shape[0]
    scale = MEM_HD ** -0.5

    def body(q_ref, k_ref, v_ref, o_ref):
        s = _dot_nt(q_ref[...], k_ref[...]) * scale
        p = jnp.exp(s - jnp.max(s, axis=1, keepdims=True))
        p = p / jnp.sum(p, axis=1, keepdims=True)
        o_ref[...] = _dot(p.astype(BF16), v_ref[...]).astype(BF16)

    return pl.pallas_call(
        body, grid=(MEM_HEADS, S // tq),
        in_specs=[pl.BlockSpec((tq, MEM_HD), lambda h, i: (i, h)),
                  pl.BlockSpec((N_MEM, MEM_HD), lambda h, i: (0, h)),
                  pl.BlockSpec((N_MEM, MEM_HD), lambda h, i: (0, h))],
        out_specs=pl.BlockSpec((tq, MEM_HD), lambda h, i: (i, h)),
        out_shape=jax.ShapeDtypeStruct((S, D), BF16), name=name, compiler_params=_cparams(),
    )(q, k, v)


def _mem_bwd_call(q, k, v, do, *, name, tq=512):
    S = q.shape[0]
    n = S // tq
    scale = MEM_HD ** -0.5

    def body(q_ref, k_ref, v_ref, do_ref, dq_ref, dk_ref, dv_ref, dka_ref, dva_ref):
        i = pl.program_id(1)

        @pl.when(i == 0)
        def _():
            dka_ref[...] = jnp.zeros_like(dka_ref)
            dva_ref[...] = jnp.zeros_like(dva_ref)
        qb = q_ref[...]
        kb = k_ref[...]
        dob = do_ref[...]
        s = _dot_nt(qb, kb) * scale
        p = jnp.exp(s - jnp.max(s, axis=1, keepdims=True))
        p = p / jnp.sum(p, axis=1, keepdims=True)
        dp = _dot_nt(dob, v_ref[...])
        ds = p * (dp - jnp.sum(p * dp, axis=1, keepdims=True))
        dsb = (ds * scale).astype(BF16)
        dq_ref[...] = _dot(dsb, kb).astype(BF16)
        dka_ref[...] += _dot_tn(dsb, qb)
        dva_ref[...] += _dot_tn(p.astype(BF16), dob)

        @pl.when(i == n - 1)
        def _():
            dk_ref[...] = dka_ref[...].astype(BF16)
            dv_ref[...] = dva_ref[...].astype(BF16)

    kv = pl.BlockSpec((N_MEM, MEM_HD), lambda h, i: (0, h))
    qs = pl.BlockSpec((tq, MEM_HD), lambda h, i: (i, h))
    return pl.pallas_call(
        body, grid=(MEM_HEADS, n), in_specs=[qs, kv, kv, qs], out_specs=(qs, kv, kv),
        out_shape=(jax.ShapeDtypeStruct((S, D), BF16), jax.ShapeDtypeStruct((N_MEM, D), BF16),
                   jax.ShapeDtypeStruct((N_MEM, D), BF16)),
        scratch_shapes=[pltpu.VMEM((N_MEM, MEM_HD), F32), pltpu.VMEM((N_MEM, MEM_HD), F32)],
        name=name, compiler_params=_cparams(),
    )(q, k, v, do)


def _rel_table_to_g(rel):
    return jnp.concatenate([
        jnp.broadcast_to(rel[:, N_REL - 1:N_REL], (8, 640)),
        rel[:, 1:N_REL - 1][:, ::-1],
        jnp.broadcast_to(rel[:, 0:1], (8, 129)),
    ], axis=1)


def _g_to_rel_table(dg):
    return dg[:, 639:896][:, ::-1]


def _place():
    x, y, c = lax.axis_index("x"), lax.axis_index("y"), lax.axis_index("c")
    others = [(1 - x, y), (x, 1 - y), (1 - x, 1 - y)]
    return x, y, c, others


def _half(c, rows):
    hr = rows // 2
    return pl.ds(pl.multiple_of(c * hr, 16), hr)


def _dma_sems(*shape):
    return pltpu.SemaphoreType.DMA(shape)


def _cast_slabs_call(ws, chip_arr, *, name, tm=256, task=None):
    n = len(ws)
    cols = ws[0].shape[1]
    tiles = [w.shape[0] // tm for w in ws]
    steps = max(tiles)

    def body(chip_ref, *refs):
        i = pl.program_id(0)
        for k in range(n):
            def cast(k=k):
                refs[n + k][...] = refs[k][...].astype(BF16)
            if tiles[k] == steps:
                cast()
            else:
                pl.when(i < tiles[k])(cast)

    in_specs = [pl.BlockSpec((tm, cols), lambda i, chip, t=t: (jnp.minimum(i, t - 1), 0)) for t in tiles]
    out_specs = [pl.BlockSpec((None, tm, cols), lambda i, chip, t=t: (chip[0], jnp.minimum(i, t - 1), 0)) for t in tiles]
    out_shape = [jax.ShapeDtypeStruct((N_CHIP,) + w.shape, BF16) for w in ws]
    return _pallas(body, grid=(steps,), in_specs=in_specs, out_specs=out_specs, out_shape=out_shape, name=name,
                   task=task, prefetch=1)(chip_arr, *ws)


def _cast_slab_call(w, chip_arr, *, name, tm=256, pad_rows=0):
    rows, cols = w.shape
    if pad_rows:
        tm = rows
    tm = min(tm, rows)

    def body(chip_ref, w_ref, o_ref):
        o_ref[0:tm, :] = w_ref[...].astype(BF16)
        if pad_rows:
            o_ref[tm:tm + pad_rows, :] = jnp.zeros((pad_rows, cols), BF16)

    return pl.pallas_call(
        body,
        grid_spec=pltpu.PrefetchScalarGridSpec(
            num_scalar_prefetch=1, grid=(rows // tm,),
            in_specs=[pl.BlockSpec((tm, cols), lambda i, chip: (i, 0))],
            out_specs=pl.BlockSpec((None, tm + pad_rows, cols), lambda i, chip: (chip[0], i, 0))),
        out_shape=jax.ShapeDtypeStruct((N_CHIP, rows + pad_rows, cols), BF16), name=name,
        compiler_params=_cparams(),
    )(chip_arr, w)


def _ag_ici_task(gathered):
    n = len(gathered)

    def copies(ins, outs, sems):
        send_sems, recv_sems = sems
        x, y, c, others = _place()
        me = 2 * x + y
        for k in range(n):
            mine = _half(c, gathered[k].shape[1])
            for t, (ox, oy) in enumerate(others):
                yield [pltpu.make_async_remote_copy(
                    src_ref=ins[k].at[me, mine], dst_ref=outs[k].at[slab, mine],
                    send_sem=send_sems.at[k, t], recv_sem=recv_sems.at[k, t],
                    device_id=(ox, oy, c), device_id_type=MESH) for slab in (me, 2 * ox + oy)]

    def issue(ins, outs, sems):
        for outgoing, _ in copies(ins, outs, sems):
            outgoing.start()

    def drain(ins, outs, sems):
        for outgoing, incoming in copies(ins, outs, sems):
            incoming.wait_recv()
            outgoing.wait_send()

    return _Task(gathered, [jax.ShapeDtypeStruct(g.shape, g.dtype) for g in gathered],
                 [_dma_sems(n, 3), _dma_sems(n, 3)], issue, drain, aliases={k: k for k in range(n)})


def _ag_d2d_task(gathered):
    n = len(gathered)

    def copies(ins, outs, sems):
        send_sems, recv_sems = sems
        x, y, c, others = _place()
        for k in range(n):
            rows = gathered[k].shape[1]
            mine, theirs = _half(c, rows), _half(1 - c, rows)
            for t, (ox, oy) in enumerate(others):
                slab = 2 * ox + oy
                pair = [pltpu.make_async_remote_copy(
                    src_ref=ins[k].at[slab, half], dst_ref=outs[k].at[slab, half],
                    send_sem=send_sems.at[k, t], recv_sem=recv_sems.at[k, t],
                    device_id=(x, y, 1 - c), device_id_type=MESH) for half in (mine, theirs)]
                yield pair

    def issue(ins, outs, sems):
        for outgoing, _ in copies(ins, outs, sems):
            outgoing.start()

    def drain(ins, outs, sems):
        for outgoing, incoming in copies(ins, outs, sems):
            incoming.wait_recv()
            outgoing.wait_send()

    return _Task(gathered, [jax.ShapeDtypeStruct(g.shape, g.dtype) for g in gathered],
                 [_dma_sems(n, 3), _dma_sems(n, 3)], issue, drain, aliases={k: k for k in range(n)})


def _rs_pair_task(ds):
    n = len(ds)

    def copies(ins, outs, sems):
        send_sems, recv_sems = sems
        x, y, c, _ = _place()
        for k in range(n):
            yield pltpu.make_async_remote_copy(
                src_ref=ins[k].at[:, _half(1 - c, ds[k].shape[1])], dst_ref=outs[k],
                send_sem=send_sems.at[k], recv_sem=recv_sems.at[k],
                device_id=(x, y, 1 - c), device_id_type=MESH)

    def issue(ins, outs, sems):
        for cp in copies(ins, outs, sems):
            cp.start()

    def drain(ins, outs, sems):
        for cp in copies(ins, outs, sems):
            cp.wait()

    return _Task(ds, [jax.ShapeDtypeStruct((N_CHIP, d.shape[1] // 2, d.shape[2]), d.dtype) for d in ds],
                 [_dma_sems(n), _dma_sems(n)], issue, drain)


def _pair_add_call(d, r1, c_arr, *, name, tm=512):
    _, rows, cols = d.shape
    hr = rows // 2
    tm = tm if hr % tm == 0 else hr
    nb = hr // tm

    def body(c_ref, d_ref, r_ref, o_ref):
        o_ref[...] = (d_ref[...].astype(F32) + r_ref[...].astype(F32)).astype(BF16)

    return pl.pallas_call(
        body,
        grid_spec=pltpu.PrefetchScalarGridSpec(
            num_scalar_prefetch=1, grid=(N_CHIP, nb),
            in_specs=[pl.BlockSpec((None, tm, cols), lambda j, i, c: (j, c[0] * nb + i, 0)),
                      pl.BlockSpec((None, tm, cols), lambda j, i, c: (j, i, 0))],
            out_specs=pl.BlockSpec((None, tm, cols), lambda j, i, c: (j, i, 0))),
        out_shape=jax.ShapeDtypeStruct((N_CHIP, hr, cols), BF16), name=name, compiler_params=_cparams(),
    )(c_arr, d, r1)


def _rs_chip_task(ps):
    n = len(ps)

    def copies(ins, outs, sems):
        send_sems, recv_sems = sems
        x, y, c, others = _place()
        for k in range(n):
            for t, (ox, oy) in enumerate(others):
                yield pltpu.make_async_remote_copy(
                    src_ref=ins[k].at[2 * ox + oy], dst_ref=outs[k].at[t],
                    send_sem=send_sems.at[k, t], recv_sem=recv_sems.at[k, t],
                    device_id=(ox, oy, c), device_id_type=MESH)

    def issue(ins, outs, sems):
        for cp in copies(ins, outs, sems):
            cp.start()

    def drain(ins, outs, sems):
        for cp in copies(ins, outs, sems):
            cp.wait()

    return _Task(ps, [jax.ShapeDtypeStruct((3,) + p.shape[1:], p.dtype) for p in ps],
                 [_dma_sems(n, 3), _dma_sems(n, 3)], issue, drain)


def _chip_sum_call(p, r2, place_arr, *, name, tm=512):
    _, hr, cols = r2.shape
    tm = tm if hr % tm == 0 else hr
    nb = hr // tm

    def body(place_ref, p_ref, r_ref, o_ref):
        acc = p_ref[...].astype(F32)
        for j in range(3):
            acc = acc + r_ref[j].astype(F32)
        o_ref[...] = acc

    return pl.pallas_call(
        body,
        grid_spec=pltpu.PrefetchScalarGridSpec(
            num_scalar_prefetch=1, grid=(nb,),
            in_specs=[pl.BlockSpec((None, tm, cols), lambda i, pc: (pc[0], i, 0)),
                      pl.BlockSpec((3, tm, cols), lambda i, pc: (0, i, 0))],
            out_specs=pl.BlockSpec((tm, cols), lambda i, pc: (pc[1] * nb + i, 0))),
        out_shape=jax.ShapeDtypeStruct((2 * hr, cols), F32), name=name, compiler_params=_cparams(),
    )(place_arr, p, r2)


def _rs_gather_task(gs):
    n = len(gs)

    def copies(ins, outs, sems):
        send_sems, recv_sems = sems
        x, y, c, _ = _place()
        for k in range(n):
            rows = gs[k].shape[0]
            mine, theirs = _half(c, rows), _half(1 - c, rows)
            yield [pltpu.make_async_remote_copy(
                src_ref=ins[k].at[mine], dst_ref=outs[k].at[half],
                send_sem=send_sems.at[k], recv_sem=recv_sems.at[k],
                device_id=(x, y, 1 - c), device_id_type=MESH) for half in (mine, theirs)]

    def issue(ins, outs, sems):
        for outgoing, _ in copies(ins, outs, sems):
            outgoing.start()

    def drain(ins, outs, sems):
        for outgoing, incoming in copies(ins, outs, sems):
            incoming.wait_recv()
            outgoing.wait_send()

    return _Task(gs, [jax.ShapeDtypeStruct(g.shape, g.dtype) for g in gs],
                 [_dma_sems(n), _dma_sems(n)], issue, drain, aliases={k: k for k in range(n)})


def _adamw(w, g, m, v):
    m = ADAM_B1 * m + (1.0 - ADAM_B1) * g
    v = ADAM_B2 * v + (1.0 - ADAM_B2) * jnp.square(g)
    m_hat = m / (1.0 - ADAM_B1 ** ADAM_STEP)
    v_hat = v / (1.0 - ADAM_B2 ** ADAM_STEP)
    delta = -ADAM_LR * (m_hat / (jnp.sqrt(v_hat) + ADAM_EPS) + ADAM_WD * w)
    return delta, m, v


def _adamw_call(items, *, name, tm=256, task=None):
    n = len(items)
    cols = items[0][0].shape[1]
    tiles = [it[0].shape[0] // tm for it in items]
    steps = max(tiles)

    def body(*refs):
        i = pl.program_id(0)
        ins, outs = refs[:4 * n], refs[4 * n:]
        for k in range(n):
            def update(k=k):
                g = ins[4 * k + 1][...]
                res = _adamw(ins[4 * k][...], g, ins[4 * k + 2][...], ins[4 * k + 3][...])
                outs[4 * k][...] = g
                for j in range(3):
                    outs[4 * k + 1 + j][...] = res[j]
            if tiles[k] == steps:
                update()
            else:
                pl.when(i < tiles[k])(update)

    in_specs, out_specs, out_shape, args = [], [], [], []
    for it, t in zip(items, tiles):
        spec = pl.BlockSpec((tm, cols), lambda i, t=t: (jnp.minimum(i, t - 1), 0))
        in_specs += [spec] * 4
        out_specs += [spec] * 4
        out_shape += [jax.ShapeDtypeStruct(it[0].shape, F32)] * 4
        args += list(it)
    res = _pallas(body, grid=(steps,), in_specs=in_specs, out_specs=out_specs, out_shape=out_shape,
                  name=name, task=task)(*args)
    outs, extra = res if task is not None else (res, None)
    grouped = [tuple(outs[4 * k:4 * k + 4]) for k in range(n)]
    return (grouped, extra) if task is not None else grouped


def _adamw_cols_call(w, g_pad, m, v, *, name, tn=256):
    rows, cols = w.shape

    def body(w_ref, g_ref, m_ref, v_ref, go_ref, d_ref, mo_ref, vo_ref):
        g = g_ref[0:rows, :]
        d, mn, vn = _adamw(w_ref[...], g, m_ref[...], v_ref[...])
        go_ref[...] = g
        d_ref[...] = d
        mo_ref[...] = mn
        vo_ref[...] = vn

    spec = pl.BlockSpec((rows, tn), lambda j: (0, j))
    gspec = pl.BlockSpec((g_pad.shape[0], tn), lambda j: (0, j))
    return _pallas(body, grid=(cols // tn,), in_specs=[spec, gspec, spec, spec], out_specs=(spec,) * 4,
                   out_shape=(jax.ShapeDtypeStruct((rows, cols), F32),) * 4, name=name)(w, g_pad, m, v)


N_DEV = 8
SMALL_ROWS = 24
SMALL_LAYOUT = {
    "g_mix_pre": (0, 0, 1, D), "g_mix_post": (1, 0, 1, D), "g_mem_kv": (2, 0, 1, D), "g_mem_pre": (3, 0, 1, D),
    "g_mem_post": (4, 0, 1, D), "g_ff_pre": (5, 0, 1, D), "g_ff_post": (6, 0, 1, D),
    "g_fox_out": (7, 0, 1, D_GRP), "g_chk_out": (7, D_GRP, 1, D_GRP), "b_fgt": (8, 0, 1, 8),
    "rel_bias": (16, 0, 8, N_REL),
}
SMALL = list(SMALL_LAYOUT)


LOSS_ROW = 9


def _small_call(grads, ws, ms, vs, loss_blk, task, *, name):
    n = len(SMALL)
    t_in, t_out = len(task.arrays), len(task.out_shapes)

    def body(*refs):
        g_refs, w_refs, m_refs, v_refs = (refs[j * n:(j + 1) * n] for j in range(4))
        p = 4 * n
        loss_ref, tins = refs[p], refs[p + 1:p + 1 + t_in]
        p += 1 + t_in
        outs, loss_out, touts = refs[p:p + 4 * n], refs[p + 4 * n], refs[p + 4 * n + 1:p + 4 * n + 1 + t_out]
        p += 4 * n + 1 + t_out
        mine, slots, send_sems, recv_sems = refs[p:p + 4]
        tsems = refs[p + 4:]
        task.issue(tins, touts, tsems)
        x, y, c, _ = _place()
        me = 4 * x + 2 * y + c
        mine[...] = jnp.zeros_like(mine)
        for k, name_k in enumerate(SMALL):
            r, l, nr, nl = SMALL_LAYOUT[name_k]
            mine[r:r + nr, l:l + nl] = g_refs[k][0:nr, 0:nl]
        mine[LOSS_ROW:LOSS_ROW + 1, 0:128] = loss_ref[0:1, :]
        slots[me] = mine[...]
        peers = [(dx, dy, dc) for dx in (0, 1) for dy in (0, 1) for dc in (0, 1)][1:]
        cps = []
        for t, (dx, dy, dc) in enumerate(peers):
            px, py, pc = (x + dx) % 2, (y + dy) % 2, (c + dc) % 2
            cps.append(pltpu.make_async_remote_copy(
                src_ref=mine, dst_ref=slots.at[me], send_sem=send_sems.at[t], recv_sem=recv_sems.at[t],
                device_id=(px, py, pc), device_id_type=MESH))
            cps[-1].start()
        for t, (dx, dy, dc) in enumerate(peers):
            px, py, pc = (x + dx) % 2, (y + dy) % 2, (c + dc) % 2
            pltpu.make_async_remote_copy(
                src_ref=mine, dst_ref=slots.at[4 * px + 2 * py + pc], send_sem=send_sems.at[t],
                recv_sem=recv_sems.at[t], device_id=(px, py, pc), device_id_type=MESH).wait_recv()
        for cp in cps:
            cp.wait_send()
        total = slots[0]
        for j in range(1, N_DEV):
            total = total + slots[j]
        for k, name_k in enumerate(SMALL):
            r, l, nr, nl = SMALL_LAYOUT[name_k]
            g = total[r:r + nr, l:l + nl]
            d, mn, vn = _adamw(w_refs[k][...], g, m_refs[k][...], v_refs[k][...])
            for j, val in enumerate((g, d, mn, vn)):
                outs[4 * k + j][...] = val
        loss_out[...] = jnp.broadcast_to(total[LOSS_ROW:LOSS_ROW + 1, 0:128], loss_out.shape)
        task.drain(tins, touts, tsems)

    vm = pl.BlockSpec(memory_space=pltpu.VMEM)
    out_shape = [jax.ShapeDtypeStruct(ws[k].shape, F32) for k in SMALL for _ in range(4)]
    out_shape += [jax.ShapeDtypeStruct((8, 128), F32)] + list(task.out_shapes)
    res = pl.pallas_call(
        body, in_specs=[vm] * (4 * n + 1) + [ANY] * t_in, out_specs=[vm] * (4 * n + 1) + [ANY] * t_out,
        out_shape=out_shape,
        scratch_shapes=[pltpu.VMEM((SMALL_ROWS, D), F32), pltpu.VMEM((N_DEV, SMALL_ROWS, D), F32),
                        _dma_sems(N_DEV - 1), _dma_sems(N_DEV - 1)] + list(task.sems),
        input_output_aliases={4 * n + 1 + i: 4 * n + 1 + j for i, j in task.aliases.items()},
        name=name,
    )(*[d[k] for d in (grads, ws, ms, vs) for k in SMALL], loss_blk, *task.arrays)
    return ({k: tuple(res[4 * i:4 * i + 4]) for i, k in enumerate(SMALL)}, res[4 * n], list(res[4 * n + 1:]))


WEIGHTS = ["w_in", "b_fgt", "rel_bias", "g_fox_out", "g_chk_out", "w_out", "g_mix_pre", "g_mix_post", "g_mem_kv",
           "w_mq", "w_mk", "w_mv", "w_mo", "g_mem_pre", "g_mem_post", "w_ff1", "w_ff2", "g_ff_pre", "g_ff_post"]
BIG = ["w_in", "w_out", "w_mq", "w_mk", "w_mv", "w_mo", "w_ff1", "w_ff2"]


IN_SHARD = D_IN // N_CHIP
IN_PAD = 800


IN_PIECES = [(0, 0, 770), (800, 770, 766), (1566, 3072, 4), (1600, 3076, 4), (1604, 1536, 766), (2400, 2302, 770)]
PAD_ZEROS = [(800 * j + IN_SHARD, IN_PAD - IN_SHARD) for j in range(N_CHIP)]
ALL_ZEROS = [(D_IN, D_ALL - D_IN)]


def _reorder_rows_call(src, to_all, *, name, tn=256):
    rows, cols = src.shape
    zeros = ALL_ZEROS if to_all else PAD_ZEROS

    def body(s_ref, o_ref):
        for pad0, all0, cnt in IN_PIECES:
            s0, d0 = (pad0, all0) if to_all else (all0, pad0)
            o_ref[d0:d0 + cnt, :] = s_ref[s0:s0 + cnt, :]
        for z0, cnt in zeros:
            o_ref[z0:z0 + cnt, :] = jnp.zeros((cnt, tn), src.dtype)

    spec = pl.BlockSpec((rows, tn), lambda j: (0, j))
    return _pallas(body, grid=(cols // tn,), in_specs=[spec], out_specs=spec,
                   out_shape=jax.ShapeDtypeStruct((rows, cols), src.dtype), name=name)(src)


def kernel(x, mem, w_in, b_fgt, rel_bias, g_fox_out, g_chk_out, w_out, g_mix_pre, g_mix_post, g_mem_kv, w_mq, w_mk, w_mv, w_mo, g_mem_pre, g_mem_post, w_ff1, w_ff2, g_ff_pre, g_ff_post, loss_target, m_w_in, m_b_fgt, m_rel_bias, m_g_fox_out, m_g_chk_out, m_w_out, m_g_mix_pre, m_g_mix_post, m_g_mem_kv, m_w_mq, m_w_mk, m_w_mv, m_w_mo, m_g_mem_pre, m_g_mem_post, m_w_ff1, m_w_ff2, m_g_ff_pre, m_g_ff_post, v_w_in, v_b_fgt, v_rel_bias, v_g_fox_out, v_g_chk_out, v_w_out, v_g_mix_pre, v_g_mix_post, v_g_mem_kv, v_w_mq, v_w_mk, v_w_mv, v_w_mo, v_g_mem_pre, v_g_mem_post, v_w_ff1, v_w_ff2, v_g_ff_pre, v_g_ff_post):
    w = dict(w_in=w_in, b_fgt=b_fgt, rel_bias=rel_bias, g_fox_out=g_fox_out, g_chk_out=g_chk_out, w_out=w_out,
             g_mix_pre=g_mix_pre, g_mix_post=g_mix_post, g_mem_kv=g_mem_kv, w_mq=w_mq, w_mk=w_mk, w_mv=w_mv,
             w_mo=w_mo, g_mem_pre=g_mem_pre, g_mem_post=g_mem_post, w_ff1=w_ff1, w_ff2=w_ff2, g_ff_pre=g_ff_pre,
             g_ff_post=g_ff_post)
    m = dict(w_in=m_w_in, b_fgt=m_b_fgt, rel_bias=m_rel_bias, g_fox_out=m_g_fox_out, g_chk_out=m_g_chk_out,
             w_out=m_w_out, g_mix_pre=m_g_mix_pre, g_mix_post=m_g_mix_post, g_mem_kv=m_g_mem_kv, w_mq=m_w_mq,
             w_mk=m_w_mk, w_mv=m_w_mv, w_mo=m_w_mo, g_mem_pre=m_g_mem_pre, g_mem_post=m_g_mem_post,
             w_ff1=m_w_ff1, w_ff2=m_w_ff2, g_ff_pre=m_g_ff_pre, g_ff_post=m_g_ff_post)
    v = dict(w_in=v_w_in, b_fgt=v_b_fgt, rel_bias=v_rel_bias, g_fox_out=v_g_fox_out, g_chk_out=v_g_chk_out,
             w_out=v_w_out, g_mix_pre=v_g_mix_pre, g_mix_post=v_g_mix_post, g_mem_kv=v_g_mem_kv, w_mq=v_w_mq,
             w_mk=v_w_mk, w_mv=v_w_mv, w_mo=v_w_mo, g_mem_pre=v_g_mem_pre, g_mem_post=v_g_mem_post,
             w_ff1=v_w_ff1, w_ff2=v_w_ff2, g_ff_pre=v_g_ff_pre, g_ff_post=v_g_ff_post)

    def rows(d, k):
        return d[k][0] if k == "rel_bias" else d[k]

    xs, mems, target = x[0], mem[0], loss_target[0]
    S = xs.shape[0]
    sp = {k: rows(w, k) for k in SMALL}
    b_pad = jnp.pad(sp["b_fgt"], ((0, 0), (0, 120)))
    chip = 2 * lax.axis_index("x") + lax.axis_index("y")
    chip_arr = jnp.reshape(chip, (1,)).astype(jnp.int32)
    c_arr = jnp.reshape(lax.axis_index("c"), (1,)).astype(jnp.int32)
    place_arr = jnp.concatenate([chip_arr, c_arr])
    w_in_t, m_in_t, v_in_t = w["w_in"][0].T, m["w_in"][0].T, v["w_in"][0].T
    slab = {"w_in": _cast_slab_call(w_in_t, chip_arr, name="cast_w_in", pad_rows=IN_PAD - IN_SHARD)}

    def gather_ici(names):
        return _ag_ici_task([slab[k] for k in names])

    def pair_add(k, d, r1):
        return _pair_add_call(d, r1, c_arr, name="rs_pair_add_" + k)

    rest, (g_in,) = _cast_slabs_call([w[k][0] for k in BIG[1:]], chip_arr, name="cast_rest",
                                     task=gather_ici(["w_in"]))
    slab.update(zip(BIG[1:], rest))
    h1, (g_in,) = _rms_fwd_call(xs, sp["g_mix_pre"], name="rms_mix_pre", task=_ag_d2d_task([g_in]))
    w_all_t = _reorder_rows_call(g_in.reshape(N_CHIP * IN_PAD, D), True, name="w_in_rows")
    proj, (g_out, g_mq) = _mm_nt(h1, w_all_t, "plain", rows=(0, 3072), name="mm_proj",
                                 task=gather_ici(["w_out", "w_mq"]))
    fl_raw = _mm_nt(h1, w_all_t, "plain", rows=(3072, 128), name="mm_gate", out_dtype=F32, tn=128)
    c_rep, c_t = _fox_prep_call(fl_raw, b_pad, name="fox_prep")
    bias = _chk_bias_call(_rel_table_to_g(sp["rel_bias"]), name="chk_bias")
    mid = ["w_mk", "w_mv", "w_mo", "w_ff1"]
    (yf, lse), got = _fox_fwd_call(proj, c_rep, c_t, name="fox_fwd",
                                   task=_merge_tasks([gather_ici(mid), _ag_d2d_task([g_out, g_mq])]))
    g_mid, (g_out, g_mq) = got[:4], got[4:]
    yc, got = _chk_fwd_call(proj, bias, name="chk_fwd",
                            task=_merge_tasks([gather_ici(["w_ff2"]), _ag_d2d_task(g_mid)]))
    g_ff2, (g_mk, g_mv, g_mo, g_ff1) = got[0], got[1:]
    yn = _mix_norm_fwd_call(yf, yc, sp["g_fox_out"], sp["g_chk_out"], name="mix_norm_fwd")
    z, (g_ff2,) = _mm_nn(yn, g_out, "rows", name="mm_out", out_dtype=F32, task=_ag_d2d_task([g_ff2]))
    x1, h2 = _post_pre_call(xs, z, sp["g_mix_post"], sp["g_mem_pre"], name="post_mix")
    memn = _rms_fwd_call(mems, sp["g_mem_kv"], name="rms_mem_kv")
    q2 = _mm_nn(h2, g_mq, "rows", name="mm_mq")
    k2 = _mm_nn(memn, g_mk, "rows", name="mm_mk")
    v2 = _mm_nn(memn, g_mv, "rows", name="mm_mv")
    o2 = _mem_fwd_call(q2, k2, v2, name="mem_fwd")
    y2 = _mm_nn(o2, g_mo, "rows", name="mm_mo", out_dtype=F32)
    x2, h3 = _post_pre_call(x1, y2, sp["g_mem_post"], sp["g_ff_pre"], name="post_mem")
    act, relu = _mm_nn(h3, g_ff1, "cols", name="mm_ff1", epi="relu2")
    y3 = _mm_nn(act, g_ff2, "rows", name="mm_ff2", out_dtype=F32, tm=1024)
    loss_blk, dx3, dy3, dg_ff_post = _final_call(x2, y3, sp["g_ff_post"], target, name="final")

    d_ff2 = _mm_tn(act, dy3, name="mm_dff2", tk=512, tn=1024).reshape(N_CHIP, D_FF // N_CHIP, D)
    du, (r1,) = _mm_nt(dy3, g_ff2, "rows", name="mm_du", mul2r=relu, task=_rs_pair_task([d_ff2]))
    p_ff2 = pair_add("w_ff2", d_ff2, r1)
    d_ff1 = _mm_tn(h3, du, name="mm_dff1", cols4=True)
    dh3, (r1,) = _mm_nt(du, g_ff1, "cols", name="mm_dh3", out_dtype=F32, tm=1024, task=_rs_pair_task([d_ff1]))
    p_ff1 = pair_add("w_ff1", d_ff1, r1)
    dx2, dy2, dg_ff_pre, dg_mem_post = _bwd_mid_call(dx3, x2, dh3, sp["g_ff_pre"], y2, sp["g_mem_post"], name="bwd_ff")
    d_mo = _mm_tn(o2, dy2, name="mm_dmo").reshape(N_CHIP, D // N_CHIP, D)
    do2 = _mm_nt(dy2, g_mo, "rows", name="mm_do2")
    dq2, dk2, dv2 = _mem_bwd_call(q2, k2, v2, do2, name="mem_bwd")
    d_mq = _mm_tn(h2, dq2, name="mm_dmq").reshape(N_CHIP, D // N_CHIP, D)
    dh2 = _mm_nt(dq2, g_mq, "rows", name="mm_dh2", out_dtype=F32)
    d_mk = _mm_tn(memn, dk2, name="mm_dmk").reshape(N_CHIP, D // N_CHIP, D)
    d_mv = _mm_tn(memn, dv2, name="mm_dmv").reshape(N_CHIP, D // N_CHIP, D)
    dmn_k = _mm_nt(dk2, g_mk, "rows", name="mm_dmemk", out_dtype=F32)
    dmn_v = _mm_nt(dv2, g_mv, "rows", name="mm_dmemv", out_dtype=F32)
    dg_mem_kv = _gain_grad_call(mems, sp["g_mem_kv"], dmn_k, dmn_v, name="gain_mem_kv")
    dx1, dz, dg_mem_pre, dg_mix_post = _bwd_mid_call(dx2, x1, dh2, sp["g_mem_pre"], z, sp["g_mix_post"], name="bwd_mem")
    d_out = _mm_tn(yn, dz, name="mm_dout").reshape(N_CHIP, D // N_CHIP, D)
    late = ["w_mo", "w_mq", "w_mk", "w_mv", "w_out"]
    d_late = [d_mo, d_mq, d_mk, d_mv, d_out]
    dyn, r1_late = _mm_nt(dz, g_out, "rows", name="mm_dyn", out_dtype=F32, task=_rs_pair_task(d_late))
    p_late = [pair_add(k, d, r1) for k, d, r1 in zip(late, d_late, r1_late)]
    dof, doc, delta, dg_fox, dg_chk = _mix_norm_bwd_call(dyn, yf, yc, sp["g_fox_out"], sp["g_chk_out"], name="mix_norm_bwd")
    (dqf, dkf, dvf, dcq, dck), r2_ff = _fox_bwd_call(proj, dof, lse, delta, c_rep, c_t, name="fox_bwd",
                                                      task=_rs_chip_task([p_ff2, p_ff1]))
    (dqc, dkc, dvc, dgrev), r2_late = _chk_bwd_call(proj, doc, bias, name="chk_bwd", task=_rs_chip_task(p_late))
    first = ["w_ff2", "w_ff1"] + late
    f_first = [_chip_sum_call(p, r, place_arr, name="rs_chip_sum_" + k)
               for k, p, r in zip(first, [p_ff2, p_ff1] + p_late, r2_ff + r2_late)]
    dc8 = dcq[:, :, 0:2, :].transpose(0, 2, 1, 3).reshape(8, S) + dck[:, ::HEAD].T
    dc_rows = jnp.concatenate([dc8, jnp.zeros((120, S), F32)], axis=0)
    dfl, db_fgt = _fox_gate_bwd_call(dc_rows, fl_raw, b_pad, name="fox_gate_bwd")
    dproj = jnp.concatenate([dqf, dkf, dvf, dqc, dkc, dvc, dfl], axis=1)
    d_all_t, g_first = _mm_tn(dproj, h1, name="mm_dwin", tk=640, tn=1024, task=_rs_gather_task(f_first))
    grads = dict(zip(first, g_first))
    d_in = _reorder_rows_call(d_all_t, False, name="d_in_rows").reshape(N_CHIP, IN_PAD, D)
    delta_w, new_m, new_v = {}, {}, {}

    def adamw_items(names):
        return [(w[k][0], grads[k], m[k][0], v[k][0]) for k in names]

    upd_late, (r1,) = _adamw_call(adamw_items(late), name="adamw_late", tm=64, task=_rs_pair_task([d_in]))
    p_in = pair_add("w_in", d_in, r1)
    dh1, (r2_in,) = _mm_nn(dproj, w_all_t, "plain", name="mm_dh1", out_dtype=F32, tm=1024,
                           task=_rs_chip_task([p_in]))
    f_in = _chip_sum_call(p_in, r2_in, place_arr, name="rs_chip_sum_w_in")
    upd_ff = _adamw_call(adamw_items(first[:2]), name="adamw_ff")
    for k, res in zip(late + first[:2], upd_late + upd_ff):
        grads[k], delta_w[k], new_m[k], new_v[k] = res
    grad_x, dg_mix_pre = _bwd_last_call(dx1, xs, dh1, sp["g_mix_pre"], name="bwd_mix")

    small_g = {"g_mix_pre": dg_mix_pre, "g_mix_post": dg_mix_post, "g_mem_kv": dg_mem_kv, "g_mem_pre": dg_mem_pre,
               "g_mem_post": dg_mem_post, "g_ff_pre": dg_ff_pre, "g_ff_post": dg_ff_post, "g_fox_out": dg_fox,
               "g_chk_out": dg_chk, "b_fgt": db_fgt,
               "rel_bias": _g_to_rel_table(dgrev[:, 0:2, :].reshape(8, ROLL_W))}
    small, loss_out, (g_w_in,) = _small_call(
        small_g, sp, {k: rows(m, k) for k in SMALL}, {k: rows(v, k) for k in SMALL}, loss_blk,
        _rs_gather_task([f_in]), name="small_allreduce_adamw")
    loss = loss_out[0, 0]
    res = _adamw_cols_call(w_in_t, g_w_in, m_in_t, v_in_t, name="adamw_w_in")
    grads["w_in"], delta_w["w_in"], new_m["w_in"], new_v["w_in"] = (a.T for a in res)
    for k in SMALL:
        vals = small[k]
        if k == "rel_bias":
            vals = tuple(a[None] for a in vals)
        grads[k], delta_w[k], new_m[k], new_v[k] = vals

    def out(d, k):
        return d[k][None] if k in BIG else d[k]

    return (loss, grad_x[None], *[out(grads, k) for k in WEIGHTS], *[out(delta_w, k) for k in WEIGHTS],
            *[out(new_m, k) for k in WEIGHTS], *[out(new_v, k) for k in WEIGHTS])
```

```python
import functools

import jax
import jax.numpy as jnp
from jax import lax
from jax.experimental import pallas as pl
from jax.experimental.pallas import tpu as pltpu

F32 = jnp.float32
BF16 = jnp.bfloat16

D = 1024
HEAD = 64
N_PAIR = 4
D_GRP = 512
CHUNK = 64
LEFT = 8
MAX_REL = 128
N_REL = 2 * MAX_REL + 1
N_MEM = 256
MEM_HEADS = 4
MEM_HD = 256
D_FF = 4096
D_IN = 3080
D_ALL = 3200
EPS = 1e-6
TQ = 256
WIN = (LEFT + TQ // CHUNK) * CHUNK
PADK = LEFT * CHUNK
ROLL_W = 1024
NEG = -1e30
N_CHIP = 4
VMEM_LIMIT = 48 * 1024 * 1024

ADAM_LR = 0.001
ADAM_B1 = 0.9
ADAM_B2 = 0.999
ADAM_EPS = 1e-08
ADAM_WD = 0.01
ADAM_STEP = 10

MESH = pl.DeviceIdType.MESH


def _cparams():
    return pltpu.CompilerParams(vmem_limit_bytes=VMEM_LIMIT)


ANY = pl.BlockSpec(memory_space=pl.ANY)


class _Task:
    def __init__(self, arrays, out_shapes, sems, issue, drain, aliases=None):
        self.arrays, self.out_shapes, self.sems = list(arrays), list(out_shapes), list(sems)
        self.issue, self.drain, self.aliases = issue, drain, dict(aliases or {})


def _merge_tasks(tasks):
    tasks = [t for t in tasks if t is not None]
    if len(tasks) == 1:
        return tasks[0]
    cuts, a, o, s = [], 0, 0, 0
    aliases = {}
    for t in tasks:
        cuts.append((a, o, s))
        aliases.update({a + i: o + j for i, j in t.aliases.items()})
        a, o, s = a + len(t.arrays), o + len(t.out_shapes), s + len(t.sems)

    def part(fn_name):
        def run(ins, outs, sems):
            for t, (a0, o0, s0) in zip(tasks, cuts):
                getattr(t, fn_name)(ins[a0:a0 + len(t.arrays)], outs[o0:o0 + len(t.out_shapes)],
                                    sems[s0:s0 + len(t.sems)])
        return run

    return _Task([x for t in tasks for x in t.arrays], [x for t in tasks for x in t.out_shapes],
                 [x for t in tasks for x in t.sems], part("issue"), part("drain"), aliases)


def _pallas(body, *, grid, in_specs, out_specs, out_shape, name, scratch_shapes=(), task=None, prefetch=0):
    def make(kernel, i_specs, o_specs, o_shape, scratch, aliases):
        if prefetch:
            spec = pltpu.PrefetchScalarGridSpec(num_scalar_prefetch=prefetch, grid=grid, in_specs=i_specs,
                                                out_specs=o_specs, scratch_shapes=scratch)
            return pl.pallas_call(kernel, grid_spec=spec, out_shape=o_shape, input_output_aliases=aliases,
                                  name=name, compiler_params=_cparams())
        return pl.pallas_call(kernel, grid=grid, in_specs=i_specs, out_specs=o_specs, out_shape=o_shape,
                              scratch_shapes=scratch, input_output_aliases=aliases, name=name,
                              compiler_params=_cparams())

    if task is None:
        return make(body, list(in_specs), out_specs, out_shape, list(scratch_shapes), {})
    single = not isinstance(out_shape, (tuple, list))
    o_shapes = [out_shape] if single else list(out_shape)
    o_specs = [out_specs] if single else list(out_specs)
    n_in, n_out, n_scr = len(in_specs), len(o_shapes), len(scratch_shapes)
    t_in, t_out = len(task.arrays), len(task.out_shapes)

    def carried(*refs):
        cut = [prefetch, n_in, t_in, n_out, t_out, n_scr]
        parts, p = [], 0
        for c in cut:
            parts.append(refs[p:p + c])
            p += c
        scalars, ins, tins, outs, touts, scr = parts
        tsems = refs[p:]
        ids = [pl.program_id(a) for a in range(len(grid))]
        first = functools.reduce(jnp.logical_and, [i == 0 for i in ids])
        last = functools.reduce(jnp.logical_and, [i == g - 1 for i, g in zip(ids, grid)])

        @pl.when(first)
        def _():
            task.issue(tins, touts, tsems)
        body(*scalars, *ins, *outs, *scr)

        @pl.when(last)
        def _():
            task.drain(tins, touts, tsems)

    call = make(carried, list(in_specs) + [ANY] * t_in, o_specs + [ANY] * t_out,
                o_shapes + list(task.out_shapes), list(scratch_shapes) + list(task.sems),
                {prefetch + n_in + i: n_out + j for i, j in task.aliases.items()})

    def run(*args):
        res = call(*args, *task.arrays)
        outs = res[:n_out]
        return (outs[0] if single else tuple(outs)), list(res[n_out:])

    return run


def _dot(a, b):
    return jnp.dot(a, b, preferred_element_type=F32)


def _dot_nt(a, b):
    return lax.dot_general(a, b, (((1,), (1,)), ((), ())), preferred_element_type=F32)


def _dot_tn(a, b):
    return lax.dot_general(a, b, (((0,), (0,)), ((), ())), preferred_element_type=F32)


def _split3(x):
    hi = x.astype(BF16)
    r1 = x - hi.astype(F32)
    mid = r1.astype(BF16)
    lo = (r1 - mid.astype(F32)).astype(BF16)
    return hi, mid, lo


def _dot3(x, m01):
    hi, mid, lo = _split3(x)
    return _dot(hi, m01) + _dot(mid, m01) + _dot(lo, m01)


def _dot3_l(m01, x):
    hi, mid, lo = _split3(x)
    return _dot(m01, hi) + _dot(m01, mid) + _dot(m01, lo)


def _mm_nn(a, b, kind, *, name, out_dtype=BF16, tm=2048, tn=512, epi=None, task=None):
    M, K = a.shape
    if kind == "plain":
        N = b.shape[1]
        b_spec = pl.BlockSpec((K, tn), lambda m, n: (0, n))
    elif kind == "rows":
        N = b.shape[2]
        b_spec = pl.BlockSpec((N_CHIP, K // N_CHIP, tn), lambda m, n: (0, 0, n))
    else:
        nq = b.shape[2]
        N = N_CHIP * nq
        per = nq // tn
        b_spec = pl.BlockSpec((None, K, tn), lambda m, n: (n // per, 0, n % per))
    tm = min(tm, M)
    kq = K // N_CHIP

    def body(a_ref, b_ref, *o_refs):
        if kind == "rows":
            acc = _dot(a_ref[:, 0:kq], b_ref[0])
            for j in range(1, N_CHIP):
                acc += _dot(a_ref[:, j * kq:(j + 1) * kq], b_ref[j])
        else:
            acc = _dot(a_ref[...], b_ref[...])
        if epi == "relu2":
            r = jnp.maximum(acc, 0.0)
            o_refs[0][...] = (r * r).astype(BF16)
            o_refs[1][...] = r.astype(BF16)
        else:
            o_refs[0][...] = acc.astype(out_dtype)

    o_spec = pl.BlockSpec((tm, tn), lambda m, n: (m, n))
    if epi == "relu2":
        out_shape = (jax.ShapeDtypeStruct((M, N), BF16), jax.ShapeDtypeStruct((M, N), BF16))
        out_specs = (o_spec, o_spec)
    else:
        out_shape = jax.ShapeDtypeStruct((M, N), out_dtype)
        out_specs = o_spec
    return _pallas(
        body, grid=(M // tm, N // tn),
        in_specs=[pl.BlockSpec((tm, K), lambda m, n: (m, 0)), b_spec],
        out_specs=out_specs, out_shape=out_shape, name=name, task=task,
    )(a, b)


def _mm_nt(a, b, kind, *, name, out_dtype=BF16, tm=2048, tn=512, mul2r=None, task=None, rows=None):
    M, K = a.shape
    if kind == "plain":
        first, N = rows if rows is not None else (0, b.shape[0])
        n0 = first // tn
        b_spec = pl.BlockSpec((tn, K), lambda m, n: (n0 + n, 0))
    elif kind == "rows":
        nq = b.shape[1]
        N = N_CHIP * nq
        tn = min(tn, nq)
        per = nq // tn
        b_spec = pl.BlockSpec((None, tn, K), lambda m, n: (n // per, n % per, 0))
    else:
        N = b.shape[1]
        b_spec = pl.BlockSpec((N_CHIP, tn, K // N_CHIP), lambda m, n: (0, n, 0))
    tm = min(tm, M)
    kq = K // N_CHIP

    def body(a_ref, b_ref, *rest):
        o_ref = rest[-1]
        if kind == "cols":
            acc = _dot_nt(a_ref[:, 0:kq], b_ref[0])
            for j in range(1, N_CHIP):
                acc += _dot_nt(a_ref[:, j * kq:(j + 1) * kq], b_ref[j])
        else:
            acc = _dot_nt(a_ref[...], b_ref[...])
        if mul2r is not None:
            acc = acc * (2.0 * rest[0][...].astype(F32))
        o_ref[...] = acc.astype(out_dtype)

    in_specs = [pl.BlockSpec((tm, K), lambda m, n: (m, 0)), b_spec]
    args = [a, b]
    if mul2r is not None:
        in_specs.append(pl.BlockSpec((tm, tn), lambda m, n: (m, n)))
        args.append(mul2r)
    return _pallas(
        body, grid=(M // tm, N // tn), in_specs=in_specs,
        out_specs=pl.BlockSpec((tm, tn), lambda m, n: (m, n)),
        out_shape=jax.ShapeDtypeStruct((M, N), out_dtype), name=name, task=task,
    )(*args)


def _mm_tn(a, b, *, name, out_dtype=BF16, tk=1024, tn=512, cols4=False, task=None):
    M, K1 = a.shape
    N = b.shape[1]
    tk = min(tk, K1)
    tn = min(tn, N)

    def body(a_ref, b_ref, o_ref):
        o_ref[...] = _dot_tn(a_ref[...], b_ref[...]).astype(out_dtype)

    if cols4:
        per = (N // N_CHIP) // tn
        out_shape = jax.ShapeDtypeStruct((N_CHIP, K1, N // N_CHIP), out_dtype)
        o_spec = pl.BlockSpec((None, tk, tn), lambda k, n: (n // per, k, n % per))
    else:
        out_shape = jax.ShapeDtypeStruct((K1, N), out_dtype)
        o_spec = pl.BlockSpec((tk, tn), lambda k, n: (k, n))
    return _pallas(
        body, grid=(K1 // tk, N // tn),
        in_specs=[pl.BlockSpec((M, tk), lambda k, n: (0, k)), pl.BlockSpec((M, tn), lambda k, n: (0, n))],
        out_specs=o_spec, out_shape=out_shape, name=name, task=task,
    )(a, b)


def _rms(x, g):
    r = lax.rsqrt(jnp.mean(x * x, axis=-1, keepdims=True) + EPS)
    return x * r * g


def _rms_bwd(x, g, dy):
    r = lax.rsqrt(jnp.mean(x * x, axis=-1, keepdims=True) + EPS)
    xh = x * r
    dg = jnp.sum(dy * xh, axis=0, keepdims=True)
    dxh = dy * g
    dx = r * (dxh - xh * jnp.mean(dxh * xh, axis=-1, keepdims=True))
    return dx, dg


def _row_spec(tm, n):
    return pl.BlockSpec((tm, n), lambda i: (i, 0))


def _vec_spec(n):
    return pl.BlockSpec((1, n), lambda i: (0, 0))


def _acc_spec(n):
    return pl.BlockSpec((8, n), lambda i: (0, 0))


def _acc_add(ref, row, i):
    @pl.when(i == 0)
    def _():
        ref[...] = jnp.zeros_like(ref)
    ref[0:1, :] += row


def _rms_fwd_call(x, g, *, name, tm=512, task=None):
    M, n = x.shape
    tm = min(tm, M)

    def body(x_ref, g_ref, h_ref):
        h_ref[...] = _rms(x_ref[...], g_ref[...]).astype(BF16)

    return _pallas(
        body, grid=(M // tm,), in_specs=[_row_spec(tm, n), _vec_spec(n)], out_specs=_row_spec(tm, n),
        out_shape=jax.ShapeDtypeStruct((M, n), BF16), name=name, task=task,
    )(x, g)


def _post_pre_call(xres, z, g_post, g_pre, *, name, tm=512):
    M, n = xres.shape

    def body(x_ref, z_ref, gp_ref, gn_ref, xo_ref, h_ref):
        xn = x_ref[...] + _rms(z_ref[...], gp_ref[...])
        xo_ref[...] = xn
        h_ref[...] = _rms(xn, gn_ref[...]).astype(BF16)

    return pl.pallas_call(
        body, grid=(M // tm,),
        in_specs=[_row_spec(tm, n), _row_spec(tm, n), _vec_spec(n), _vec_spec(n)],
        out_specs=(_row_spec(tm, n), _row_spec(tm, n)),
        out_shape=(jax.ShapeDtypeStruct((M, n), F32), jax.ShapeDtypeStruct((M, n), BF16)),
        name=name, compiler_params=_cparams(),
    )(xres, z, g_post, g_pre)


def _final_call(x2, y3, g_post, target, *, name, tm=512):
    M, n = x2.shape

    def body(x_ref, y_ref, g_ref, t_ref, loss_ref, dx_ref, dy_ref, dg_ref):
        i = pl.program_id(0)
        y = y_ref[...]
        g = g_ref[...]
        diff = x_ref[...] + _rms(y, g) - t_ref[...]
        part = 0.5 * jnp.sum(jnp.sum(diff * diff, axis=1, keepdims=True), axis=0, keepdims=True) / n

        @pl.when(i == 0)
        def _():
            loss_ref[...] = jnp.zeros_like(loss_ref)
        loss_ref[...] += jnp.broadcast_to(part, loss_ref.shape)
        dx = diff / n
        dx_ref[...] = dx
        dy, dg = _rms_bwd(y, g, dx)
        dy_ref[...] = dy.astype(BF16)
        _acc_add(dg_ref, dg, i)

    return pl.pallas_call(
        body, grid=(M // tm,),
        in_specs=[_row_spec(tm, n), _row_spec(tm, n), _vec_spec(n), _row_spec(tm, n)],
        out_specs=(pl.BlockSpec((8, 128), lambda i: (0, 0)), _row_spec(tm, n), _row_spec(tm, n), _acc_spec(n)),
        out_shape=(jax.ShapeDtypeStruct((8, 128), F32), jax.ShapeDtypeStruct((M, n), F32),
                   jax.ShapeDtypeStruct((M, n), BF16), jax.ShapeDtypeStruct((8, n), F32)),
        name=name, compiler_params=_cparams(),
    )(x2, y3, g_post, target)


def _bwd_mid_call(dx_in, x, dh, g_pre, y, g_post, *, name, tm=512):
    M, n = x.shape

    def body(dxi_ref, x_ref, dh_ref, gpre_ref, y_ref, gpost_ref, dx_ref, dy_ref, dgpre_ref, dgpost_ref):
        i = pl.program_id(0)
        d1, dg1 = _rms_bwd(x_ref[...], gpre_ref[...], dh_ref[...])
        dx = dxi_ref[...] + d1
        dx_ref[...] = dx
        dy, dg2 = _rms_bwd(y_ref[...], gpost_ref[...], dx)
        dy_ref[...] = dy.astype(BF16)
        _acc_add(dgpre_ref, dg1, i)
        _acc_add(dgpost_ref, dg2, i)

    return pl.pallas_call(
        body, grid=(M // tm,),
        in_specs=[_row_spec(tm, n), _row_spec(tm, n), _row_spec(tm, n), _vec_spec(n), _row_spec(tm, n), _vec_spec(n)],
        out_specs=(_row_spec(tm, n), _row_spec(tm, n), _acc_spec(n), _acc_spec(n)),
        out_shape=(jax.ShapeDtypeStruct((M, n), F32), jax.ShapeDtypeStruct((M, n), BF16),
                   jax.ShapeDtypeStruct((8, n), F32), jax.ShapeDtypeStruct((8, n), F32)),
        name=name, compiler_params=_cparams(),
    )(dx_in, x, dh, g_pre, y, g_post)


def _bwd_last_call(dx_in, x, dh, g_pre, *, name, tm=512, task=None):
    M, n = x.shape

    def body(dxi_ref, x_ref, dh_ref, g_ref, dx_ref, dg_ref):
        i = pl.program_id(0)
        d1, dg1 = _rms_bwd(x_ref[...], g_ref[...], dh_ref[...])
        dx_ref[...] = dxi_ref[...] + d1
        _acc_add(dg_ref, dg1, i)

    return _pallas(
        body, grid=(M // tm,),
        in_specs=[_row_spec(tm, n), _row_spec(tm, n), _row_spec(tm, n), _vec_spec(n)],
        out_specs=(_row_spec(tm, n), _acc_spec(n)),
        out_shape=(jax.ShapeDtypeStruct((M, n), F32), jax.ShapeDtypeStruct((8, n), F32)),
        name=name, task=task,
    )(dx_in, x, dh, g_pre)


def _gain_grad_call(x, g, dy_a, dy_b, *, name):
    M, n = x.shape

    def body(x_ref, g_ref, a_ref, b_ref, dg_ref):
        _, dg = _rms_bwd(x_ref[...], g_ref[...], a_ref[...] + b_ref[...])
        dg_ref[...] = jnp.zeros_like(dg_ref)
        dg_ref[0:1, :] = dg

    return pl.pallas_call(
        body, grid=(1,),
        in_specs=[_row_spec(M, n), _vec_spec(n), _row_spec(M, n), _row_spec(M, n)],
        out_specs=_acc_spec(n), out_shape=jax.ShapeDtypeStruct((8, n), F32),
        name=name, compiler_params=_cparams(),
    )(x, g, dy_a, dy_b)


def _head_group_matrix():
    a = lax.broadcasted_iota(jnp.int32, (D_GRP, D_GRP), 0) // HEAD
    b = lax.broadcasted_iota(jnp.int32, (D_GRP, D_GRP), 1) // HEAD
    return jnp.where(a == b, 1.0, 0.0).astype(BF16)


def _mix_norm_fwd_call(yf, yc, gf, gc, *, name, tm=512):
    M = yf.shape[0]

    def body(yf_ref, yc_ref, gf_ref, gc_ref, o_ref):
        o_ref[:, 0:D_GRP] = _rms(yf_ref[...], gf_ref[...]).astype(BF16)
        o_ref[:, D_GRP:D] = _rms(yc_ref[...], gc_ref[...]).astype(BF16)

    return pl.pallas_call(
        body, grid=(M // tm,),
        in_specs=[_row_spec(tm, D_GRP), _row_spec(tm, D_GRP), _vec_spec(D_GRP), _vec_spec(D_GRP)],
        out_specs=_row_spec(tm, D), out_shape=jax.ShapeDtypeStruct((M, D), BF16),
        name=name, compiler_params=_cparams(),
    )(yf, yc, gf, gc)


def _mix_norm_bwd_call(dyn, yf, yc, gf, gc, *, name, tm=TQ):
    M = yf.shape[0]

    def body(dyn_ref, yf_ref, yc_ref, gf_ref, gc_ref, dof_ref, doc_ref, delta_ref, dgf_ref, dgc_ref):
        i = pl.program_id(0)
        yf_ = yf_ref[...]
        dof, dgf = _rms_bwd(yf_, gf_ref[...], dyn_ref[:, 0:D_GRP])
        doc, dgc = _rms_bwd(yc_ref[...], gc_ref[...], dyn_ref[:, D_GRP:D])
        dof_b = dof.astype(BF16)
        dof_ref[...] = dof_b
        doc_ref[...] = doc.astype(BF16)
        prod = dof_b.astype(F32) * yf_
        hi = prod.astype(BF16)
        lo = (prod - hi.astype(F32)).astype(BF16)
        grp = _head_group_matrix()
        delta_ref[...] = (_dot(hi, grp) + _dot(lo, grp)).T
        _acc_add(dgf_ref, dgf, i)
        _acc_add(dgc_ref, dgc, i)

    return pl.pallas_call(
        body, grid=(M // tm,),
        in_specs=[_row_spec(tm, D), _row_spec(tm, D_GRP), _row_spec(tm, D_GRP), _vec_spec(D_GRP), _vec_spec(D_GRP)],
        out_specs=(_row_spec(tm, D_GRP), _row_spec(tm, D_GRP), pl.BlockSpec((None, D_GRP, tm), lambda i: (i, 0, 0)),
                   _acc_spec(D_GRP), _acc_spec(D_GRP)),
        out_shape=(jax.ShapeDtypeStruct((M, D_GRP), BF16), jax.ShapeDtypeStruct((M, D_GRP), BF16),
                   jax.ShapeDtypeStruct((M // tm, D_GRP, tm), F32), jax.ShapeDtypeStruct((8, D_GRP), F32),
                   jax.ShapeDtypeStruct((8, D_GRP), F32)),
        name=name, compiler_params=_cparams(),
    )(dyn, yf, yc, gf, gc)


def _tri(n, lower_incl):
    a = lax.broadcasted_iota(jnp.int32, (n, n), 0)
    b = lax.broadcasted_iota(jnp.int32, (n, n), 1)
    return jnp.where(a >= b, 1.0, 0.0).astype(BF16) if lower_incl else jnp.where(a <= b, 1.0, 0.0).astype(BF16)


def _fox_prep_call(fl_raw, b_pad, *, name):
    S = fl_raw.shape[0]
    nb = S // TQ

    def body(fl_ref, b_ref, crep_ref, ct_ref, carry_ref):
        i = pl.program_id(0)

        @pl.when(i == 0)
        def _():
            carry_ref[...] = jnp.zeros_like(carry_ref)
        logf = jax.nn.log_sigmoid(fl_ref[...] + b_ref[...])
        cb = _dot3_l(_tri(TQ, True), logf) + carry_ref[0:1, :]
        carry_ref[0:1, :] = cb[TQ - 1:TQ, :]
        a = lax.broadcasted_iota(jnp.int32, (128, D_GRP), 0)
        b = lax.broadcasted_iota(jnp.int32, (128, D_GRP), 1) // HEAD
        expand = jnp.where(a == b, 1.0, 0.0).astype(BF16)
        crep = _dot3(cb, expand)
        crep_ref[...] = crep
        ct_ref[...] = crep.T

    return pl.pallas_call(
        body, grid=(nb,),
        in_specs=[_row_spec(TQ, 128), _vec_spec(128)],
        out_specs=(_row_spec(TQ, D_GRP), pl.BlockSpec((None, D_GRP, TQ), lambda i: (i, 0, 0))),
        out_shape=(jax.ShapeDtypeStruct((S, D_GRP), F32), jax.ShapeDtypeStruct((nb, D_GRP, TQ), F32)),
        scratch_shapes=[pltpu.VMEM((8, 128), F32)],
        name=name, compiler_params=_cparams(),
    )(fl_raw, b_pad)


def _lane_masks():
    lane = lax.broadcasted_iota(jnp.int32, (1, 128), 1)
    return lane < HEAD, lane >= HEAD


def _fox_fwd_call(proj, c_rep, c_t, *, name, task=None):
    S = proj.shape[0]
    nq = S // TQ
    scale = HEAD ** -0.5

    def body(q_ref, k_ref, v_ref, c_ref, ct_ref, o_ref, lse_ref):
        i = pl.program_id(1)
        m_lo, m_hi = _lane_masks()
        masks = (m_lo, m_hi)
        q = q_ref[...] * scale
        qm = [jnp.where(mk, q, jnp.zeros_like(q)) for mk in masks]
        cq = c_ref[...]
        cqh = [cq[:, 0:1], cq[:, HEAD:HEAD + 1]]
        row = lax.broadcasted_iota(jnp.int32, (TQ, TQ), 0)
        col = lax.broadcasted_iota(jnp.int32, (TQ, TQ), 1)

        def scores(j):
            start = pl.multiple_of(j * TQ, TQ)
            k = k_ref[pl.ds(start, TQ), :]
            ct = ct_ref[j]
            return tuple(_dot_nt(qm[h], k) + (cqh[h] - ct[HEAD * h:HEAD * h + 1, :]) for h in range(2))

        def update(j, ss, state, masked):
            ms, ls, acc = state
            start = pl.multiple_of(j * TQ, TQ)
            v = v_ref[pl.ds(start, TQ), :]
            new_m, new_l, pv, alpha_l = [], [], [], []
            for h in range(2):
                s = ss[h]
                if masked:
                    s = jnp.where(row >= col, s, NEG)
                mn = jnp.maximum(ms[h], jnp.max(s, axis=1, keepdims=True))
                alpha = jnp.exp(ms[h] - mn)
                p = jnp.exp(s - mn)
                new_l.append(alpha * ls[h] + jnp.sum(p, axis=1, keepdims=True))
                new_m.append(mn)
                alpha_l.append(alpha)
                pv.append(_dot(p.astype(BF16), jnp.where(masks[h], v, jnp.zeros_like(v))))
            alpha_lane = jnp.where(m_lo, alpha_l[0], alpha_l[1])
            acc = acc * alpha_lane + pv[0] + pv[1]
            return (tuple(new_m), tuple(new_l), acc)

        def step(j, carry):
            ss, state = carry
            return (scores(j + 1), update(j, ss, state, False))

        init = ((jnp.full((TQ, 1), NEG, F32),) * 2, (jnp.zeros((TQ, 1), F32),) * 2, jnp.zeros((TQ, 128), F32))
        ss, state = lax.fori_loop(0, i, step, (scores(0), init))
        ms, ls, acc = update(i, ss, state, True)
        l_lane = jnp.where(m_lo, ls[0], ls[1])
        o_ref[...] = acc / l_lane
        lse_ref[...] = jnp.where(m_lo, ms[0] + jnp.log(ls[0]), ms[1] + jnp.log(ls[1])).T

    return _pallas(
        body, grid=(N_PAIR, nq),
        in_specs=[pl.BlockSpec((TQ, 128), lambda p, i: (i, p)),
                  pl.BlockSpec((S, 128), lambda p, i: (0, N_PAIR + p)),
                  pl.BlockSpec((S, 128), lambda p, i: (0, 2 * N_PAIR + p)),
                  pl.BlockSpec((TQ, 128), lambda p, i: (i, p)),
                  pl.BlockSpec((nq, 128, TQ), lambda p, i: (0, p, 0))],
        out_specs=(pl.BlockSpec((TQ, 128), lambda p, i: (i, p)), pl.BlockSpec((None, 128, TQ), lambda p, i: (i, p, 0))),
        out_shape=(jax.ShapeDtypeStruct((S, D_GRP), F32), jax.ShapeDtypeStruct((nq, D_GRP, TQ), F32)),
        name=name, task=task,
    )(proj, proj, proj, c_rep, c_t)


def _fox_bwd_call(proj, do, lse_t, delta_t, c_rep, c_t, *, name, task=None):
    S = proj.shape[0]
    nq = S // TQ
    scale = HEAD ** -0.5

    def body(q_ref, k_ref, v_ref, do_ref, lse_ref, dl_ref, ck_ref, ct_ref,
             dq_ref, dk_ref, dv_ref, dcq_ref, dck_ref, dqa_ref):
        j = pl.program_id(1)
        m_lo, m_hi = _lane_masks()
        masks = (m_lo, m_hi)

        @pl.when(j == 0)
        def _():
            dqa_ref[...] = jnp.zeros_like(dqa_ref)
            dcq_ref[...] = jnp.zeros_like(dcq_ref)
        k = k_ref[...]
        v = v_ref[...]
        km = [jnp.where(mk, k, jnp.zeros_like(k)) for mk in masks]
        ck = ck_ref[...]
        krow = lax.broadcasted_iota(jnp.int32, (TQ, TQ), 0)
        qcol = lax.broadcasted_iota(jnp.int32, (TQ, TQ), 1)

        def probs(i):
            start = pl.multiple_of(i * TQ, TQ)
            q = q_ref[pl.ds(start, TQ), :]
            do = do_ref[pl.ds(start, TQ), :]
            lse = lse_ref[i]
            cq = ct_ref[i]
            out = []
            for h in range(2):
                lo = HEAD * h
                qm = jnp.where(masks[h], q * scale, jnp.zeros_like(q))
                dom = jnp.where(masks[h], do, jnp.zeros_like(do))
                st = _dot_nt(k, qm) + (cq[lo:lo + 1, :] - ck[:, lo:lo + 1])
                out.append((jnp.exp(st - lse[lo:lo + 1, :]), _dot_nt(v, dom)))
            return tuple(out)

        def update(i, pd, carry, masked):
            dk, dv, dck = carry
            start = pl.multiple_of(i * TQ, TQ)
            q = q_ref[pl.ds(start, TQ), :]
            do = do_ref[pl.ds(start, TQ), :]
            dl = dl_ref[i]
            dq = jnp.zeros((TQ, 128), F32)
            new_dck = []
            for h in range(2):
                lo = HEAD * h
                qm = jnp.where(masks[h], q, jnp.zeros_like(q))
                dom = jnp.where(masks[h], do, jnp.zeros_like(do))
                pt, dpt = pd[h]
                if masked:
                    pt = jnp.where(qcol >= krow, pt, 0.0)
                dst = pt * (dpt - dl[lo:lo + 1, :])
                dcq_ref[i, h:h + 1, :] += jnp.sum(dst, axis=0, keepdims=True)
                new_dck.append(dck[h] + jnp.sum(dst, axis=1, keepdims=True))
                dsb = (dst * scale).astype(BF16)
                dv = dv + _dot(pt.astype(BF16), dom)
                dk = dk + _dot(dsb, qm)
                dq = dq + _dot_tn(dsb, km[h])
            dqa_ref[pl.ds(start, TQ), :] += dq
            return (dk, dv, tuple(new_dck))

        def step(i, carry):
            pd, sums = carry
            return (probs(jnp.minimum(i + 1, nq - 1)), update(i, pd, sums, False))

        init = (jnp.zeros((TQ, 128), F32), jnp.zeros((TQ, 128), F32), (jnp.zeros((TQ, 1), F32),) * 2)
        first = probs(j)
        second = probs(jnp.minimum(j + 1, nq - 1))
        _, (dk, dv, dck) = lax.fori_loop(j + 1, nq, step, (second, update(j, first, init, True)))
        dk_ref[...] = dk.astype(BF16)
        dv_ref[...] = dv.astype(BF16)
        dck_ref[...] = -jnp.where(m_lo, dck[0], dck[1])

        @pl.when(j == nq - 1)
        def _():
            dq_ref[...] = dqa_ref[...].astype(BF16)

    res = lambda p, j: (0, p)
    stat = pl.BlockSpec((nq, 128, TQ), lambda p, j: (0, p, 0))
    blk = pl.BlockSpec((TQ, 128), lambda p, j: (j, p))
    return _pallas(
        body, grid=(N_PAIR, nq), task=task,
        in_specs=[pl.BlockSpec((S, 128), res),
                  pl.BlockSpec((TQ, 128), lambda p, j: (j, N_PAIR + p)),
                  pl.BlockSpec((TQ, 128), lambda p, j: (j, 2 * N_PAIR + p)),
                  pl.BlockSpec((S, 128), res), stat, stat, blk, stat],
        out_specs=(pl.BlockSpec((S, 128), res), blk, blk,
                   pl.BlockSpec((None, nq, 8, TQ), lambda p, j: (p, 0, 0, 0)), blk),
        out_shape=(jax.ShapeDtypeStruct((S, D_GRP), BF16), jax.ShapeDtypeStruct((S, D_GRP), BF16),
                   jax.ShapeDtypeStruct((S, D_GRP), BF16), jax.ShapeDtypeStruct((N_PAIR, nq, 8, TQ), F32),
                   jax.ShapeDtypeStruct((S, D_GRP), F32)),
        scratch_shapes=[pltpu.VMEM((S, 128), F32)],
        name=name,
    )(proj, proj, proj, do, lse_t, delta_t, c_rep, c_t)


def _fox_gate_bwd_call(dc_rows, fl_raw, b_pad, *, name):
    S = fl_raw.shape[0]
    nb = S // TQ

    def body(dc_ref, fl_ref, b_ref, dfl_ref, db_ref, carry_ref):
        i = pl.program_id(0)

        @pl.when(i == 0)
        def _():
            carry_ref[...] = jnp.zeros_like(carry_ref)
        rc = _dot3(dc_ref[...], _tri(TQ, True)) + carry_ref[:, 0:1]
        carry_ref[...] = jnp.broadcast_to(rc[:, 0:1], carry_ref.shape)
        fl = fl_ref[...] + b_ref[...]
        dfl = rc.T * jax.nn.sigmoid(-fl)
        dfl_ref[...] = dfl.astype(BF16)
        _acc_add(db_ref, jnp.sum(dfl, axis=0, keepdims=True), i)

    rev = lambda i: (nb - 1 - i, 0)
    return pl.pallas_call(
        body, grid=(nb,),
        in_specs=[pl.BlockSpec((128, TQ), lambda i: (0, nb - 1 - i)), pl.BlockSpec((TQ, 128), rev), _vec_spec(128)],
        out_specs=(pl.BlockSpec((TQ, 128), rev), _acc_spec(128)),
        out_shape=(jax.ShapeDtypeStruct((S, 128), BF16), jax.ShapeDtypeStruct((8, 128), F32)),
        scratch_shapes=[pltpu.VMEM((128, 128), F32)],
        name=name, compiler_params=_cparams(),
    )(dc_rows, fl_raw, b_pad)


def _chk_bias_call(g_rev, *, name):
    def body(g_ref, o_ref):
        x = jnp.broadcast_to(g_ref[...], (TQ, ROLL_W))
        rolled = pltpu.roll(x, ROLL_W - (TQ - 1), 1, stride=1, stride_axis=0)
        qc = lax.broadcasted_iota(jnp.int32, (TQ, WIN), 0) // CHUNK
        kc = lax.broadcasted_iota(jnp.int32, (TQ, WIN), 1) // CHUNK
        band = (kc >= qc) & (kc <= qc + LEFT)
        o_ref[...] = jnp.where(band, rolled[:, 0:WIN], NEG)

    return pl.pallas_call(
        body, grid=(8,),
        in_specs=[pl.BlockSpec((None, 1, ROLL_W), lambda h: (h, 0, 0))],
        out_specs=pl.BlockSpec((None, TQ, WIN), lambda h: (h, 0, 0)),
        out_shape=jax.ShapeDtypeStruct((8, TQ, WIN), F32), name=name, compiler_params=_cparams(),
    )(g_rev.reshape(8, 1, ROLL_W))


def _chk_scores(i, qm, kwin, bias, scale):
    s = _dot_nt(qm * scale, kwin) + bias
    kc = lax.broadcasted_iota(jnp.int32, (TQ, WIN), 1) // CHUNK
    return jnp.where(kc + i * (TQ // CHUNK) >= LEFT, s, NEG)


def _chk_fwd_call(proj, bias, *, name, task=None):
    S = proj.shape[0]
    nq = S // TQ
    scale = HEAD ** -0.5

    def body(q_ref, k_ref, v_ref, b_ref, o_ref, kp_ref, vp_ref):
        i = pl.program_id(1)

        @pl.when(i == 0)
        def _():
            kp_ref[0:PADK, :] = jnp.zeros((PADK, 128), BF16)
            vp_ref[0:PADK, :] = jnp.zeros((PADK, 128), BF16)
            kp_ref[PADK:PADK + S, :] = k_ref[...]
            vp_ref[PADK:PADK + S, :] = v_ref[...]
        masks = _lane_masks()
        q = q_ref[...]
        start = pl.multiple_of(i * TQ, TQ)
        kwin = kp_ref[pl.ds(start, WIN), :]
        vwin = vp_ref[pl.ds(start, WIN), :]
        ss = [_chk_scores(i, jnp.where(masks[h], q, jnp.zeros_like(q)), kwin, b_ref[h], scale) for h in range(2)]
        ps = []
        for s in ss:
            p = jnp.exp(s - jnp.max(s, axis=1, keepdims=True))
            ps.append((p / jnp.sum(p, axis=1, keepdims=True)).astype(BF16))
        o_ref[...] = (_dot(ps[0], jnp.where(masks[0], vwin, jnp.zeros_like(vwin)))
                      + _dot(ps[1], jnp.where(masks[1], vwin, jnp.zeros_like(vwin))))

    c0 = 3 * N_PAIR
    return _pallas(
        body, grid=(N_PAIR, nq), task=task,
        in_specs=[pl.BlockSpec((TQ, 128), lambda p, i: (i, c0 + p)),
                  pl.BlockSpec((S, 128), lambda p, i: (0, c0 + N_PAIR + p)),
                  pl.BlockSpec((S, 128), lambda p, i: (0, c0 + 2 * N_PAIR + p)),
                  pl.BlockSpec((2, TQ, WIN), lambda p, i: (p, 0, 0))],
        out_specs=pl.BlockSpec((TQ, 128), lambda p, i: (i, p)),
        out_shape=jax.ShapeDtypeStruct((S, D_GRP), F32),
        scratch_shapes=[pltpu.VMEM((S + PADK, 128), BF16), pltpu.VMEM((S + PADK, 128), BF16)],
        name=name,
    )(proj, proj, proj, bias)


def _chk_bwd_call(proj, do, bias, *, name, task=None):
    S = proj.shape[0]
    nq = S // TQ
    scale = HEAD ** -0.5

    def body(q_ref, k_ref, v_ref, do_ref, b_ref, dq_ref, dk_ref, dv_ref, dg_ref, kp_ref, vp_ref, dkp_ref, dvp_ref, db_ref):
        i = pl.program_id(1)

        @pl.when(i == 0)
        def _():
            kp_ref[0:PADK, :] = jnp.zeros((PADK, 128), BF16)
            vp_ref[0:PADK, :] = jnp.zeros((PADK, 128), BF16)
            kp_ref[PADK:PADK + S, :] = k_ref[...]
            vp_ref[PADK:PADK + S, :] = v_ref[...]
            dkp_ref[...] = jnp.zeros_like(dkp_ref)
            dvp_ref[...] = jnp.zeros_like(dvp_ref)
            db_ref[...] = jnp.zeros_like(db_ref)
        masks = _lane_masks()
        q = q_ref[...]
        dout = do_ref[...]
        start = pl.multiple_of(i * TQ, TQ)
        kwin = kp_ref[pl.ds(start, WIN), :]
        vwin = vp_ref[pl.ds(start, WIN), :]
        qm = [jnp.where(mk, q, jnp.zeros_like(q)) for mk in masks]
        dom = [jnp.where(mk, dout, jnp.zeros_like(dout)) for mk in masks]
        ss = [_chk_scores(i, qm[h], kwin, b_ref[h], scale) for h in range(2)]
        dps = [_dot_nt(dom[h], vwin) for h in range(2)]
        pbs, dsbs = [], []
        for h in range(2):
            p = jnp.exp(ss[h] - jnp.max(ss[h], axis=1, keepdims=True))
            p = p / jnp.sum(p, axis=1, keepdims=True)
            ds = p * (dps[h] - jnp.sum(p * dps[h], axis=1, keepdims=True))
            db_ref[h] += ds
            pbs.append(p.astype(BF16))
            dsbs.append((ds * scale).astype(BF16))
        dq_ref[...] = (_dot(dsbs[0], jnp.where(masks[0], kwin, jnp.zeros_like(kwin)))
                       + _dot(dsbs[1], jnp.where(masks[1], kwin, jnp.zeros_like(kwin)))).astype(BF16)
        dkp_ref[pl.ds(start, WIN), :] += _dot_tn(dsbs[0], qm[0]) + _dot_tn(dsbs[1], qm[1])
        dvp_ref[pl.ds(start, WIN), :] += _dot_tn(pbs[0], dom[0]) + _dot_tn(pbs[1], dom[1])

        @pl.when(i == nq - 1)
        def _():
            dk_ref[...] = dkp_ref[PADK:PADK + S, :].astype(BF16)
            dv_ref[...] = dvp_ref[PADK:PADK + S, :].astype(BF16)
            a = lax.broadcasted_iota(jnp.int32, (TQ, TQ), 0)
            b = lax.broadcasted_iota(jnp.int32, (TQ, TQ), 1)
            flip = jnp.where(a + b == TQ - 1, 1.0, 0.0).astype(BF16)
            e = lax.broadcasted_iota(jnp.int32, (1, ROLL_W), 1)
            dg_ref[...] = jnp.zeros_like(dg_ref)
            for h in range(2):
                rev = _dot3_l(flip, db_ref[h])
                wide = jnp.concatenate([rev, jnp.zeros((TQ, ROLL_W - WIN), F32)], axis=1)
                diag = pltpu.roll(wide, 0, 1, stride=1, stride_axis=0)
                dg = jnp.sum(diag, axis=0, keepdims=True)
                lo = jnp.sum(jnp.where(e <= 639, dg, 0.0), axis=1, keepdims=True)
                hi = jnp.sum(jnp.where(e >= 895, dg, 0.0), axis=1, keepdims=True)
                dg_ref[h:h + 1, :] = jnp.where(e == 639, lo, jnp.where(e == 895, hi, dg))

    c0 = 3 * N_PAIR
    res = lambda p, i: (0, p)
    return _pallas(
        body, grid=(N_PAIR, nq), task=task,
        in_specs=[pl.BlockSpec((TQ, 128), lambda p, i: (i, c0 + p)),
                  pl.BlockSpec((S, 128), lambda p, i: (0, c0 + N_PAIR + p)),
                  pl.BlockSpec((S, 128), lambda p, i: (0, c0 + 2 * N_PAIR + p)),
                  pl.BlockSpec((TQ, 128), lambda p, i: (i, p)),
                  pl.BlockSpec((2, TQ, WIN), lambda p, i: (p, 0, 0))],
        out_specs=(pl.BlockSpec((TQ, 128), lambda p, i: (i, p)), pl.BlockSpec((S, 128), res),
                   pl.BlockSpec((S, 128), res), pl.BlockSpec((None, 8, ROLL_W), lambda p, i: (p, 0, 0))),
        out_shape=(jax.ShapeDtypeStruct((S, D_GRP), BF16), jax.ShapeDtypeStruct((S, D_GRP), BF16),
                   jax.ShapeDtypeStruct((S, D_GRP), BF16), jax.ShapeDtypeStruct((N_PAIR, 8, ROLL_W), F32)),
        scratch_shapes=[pltpu.VMEM((S + PADK, 128), BF16), pltpu.VMEM((S + PADK, 128), BF16),
                        pltpu.VMEM((S + PADK, 128), F32), pltpu.VMEM((S + PADK, 128), F32),
                        pltpu.VMEM((2, TQ, WIN), F32)],
        name=name,
    )(proj, proj, proj, do, bias)


def _mem_fwd_call(q, k, v, *, name, tq=512):
    S = q.shape[0]
    scale = MEM_HD ** -0.5

    def body(q_ref, k_ref, v_ref, o_ref):
        s = _dot_nt(q_ref[...] * scale, k_ref[...])
        p = jnp.exp(s - jnp.max(s, axis=1, keepdims=True))
        p = p / jnp.sum(p, axis=1, keepdims=True)
        o_ref[...] = _dot(p.astype(BF16), v_ref[...]).astype(BF16)

    return pl.pallas_call(
        body, grid=(MEM_HEADS, S // tq),
        in_specs=[pl.BlockSpec((tq, MEM_HD), lambda h, i: (i, h)),
                  pl.BlockSpec((N_MEM, MEM_HD), lambda h, i: (0, h)),
                  pl.BlockSpec((N_MEM, MEM_HD), lambda h, i: (0, h))],
        out_specs=pl.BlockSpec((tq, MEM_HD), lambda h, i: (i, h)),
        out_shape=jax.ShapeDtypeStruct((S, D), BF16), name=name, compiler_params=_cparams(),
    )(q, k, v)


def _mem_bwd_call(q, k, v, do, *, name, tq=512):
    S = q.shape[0]
    n = S // tq
    scale = MEM_HD ** -0.5

    def body(q_ref, k_ref, v_ref, do_ref, dq_ref, dk_ref, dv_ref, dka_ref, dva_ref):
        i = pl.program_id(1)

        @pl.when(i == 0)
        def _():
            dka_ref[...] = jnp.zeros_like(dka_ref)
            dva_ref[...] = jnp.zeros_like(dva_ref)
        qb = q_ref[...]
        kb = k_ref[...]
        dob = do_ref[...]
        s = _dot_nt(qb * scale, kb)
        p = jnp.exp(s - jnp.max(s, axis=1, keepdims=True))
        p = p / jnp.sum(p, axis=1, keepdims=True)
        dp = _dot_nt(dob, v_ref[...])
        ds = p * (dp - jnp.sum(p * dp, axis=1, keepdims=True))
        dsb = (ds * scale).astype(BF16)
        dq_ref[...] = _dot(dsb, kb).astype(BF16)
        dka_ref[...] += _dot_tn(dsb, qb)
        dva_ref[...] += _dot_tn(p.astype(BF16), dob)

        @pl.when(i == n - 1)
        def _():
            dk_ref[...] = dka_ref[...].astype(BF16)
            dv_ref[...] = dva_ref[...].astype(BF16)

    kv = pl.BlockSpec((N_MEM, MEM_HD), lambda h, i: (0, h))
    qs = pl.BlockSpec((tq, MEM_HD), lambda h, i: (i, h))
    return pl.pallas_call(
        body, grid=(MEM_HEADS, n), in_specs=[qs, kv, kv, qs], out_specs=(qs, kv, kv),
        out_shape=(jax.ShapeDtypeStruct((S, D), BF16), jax.ShapeDtypeStruct((N_MEM, D), BF16),
                   jax.ShapeDtypeStruct((N_MEM, D), BF16)),
        scratch_shapes=[pltpu.VMEM((N_MEM, MEM_HD), F32), pltpu.VMEM((N_MEM, MEM_HD), F32)],
        name=name, compiler_params=_cparams(),
    )(q, k, v, do)


def _rel_table_to_g(rel):
    return jnp.concatenate([
        jnp.broadcast_to(rel[:, N_REL - 1:N_REL], (8, 640)),
        rel[:, 1:N_REL - 1][:, ::-1],
        jnp.broadcast_to(rel[:, 0:1], (8, 129)),
    ], axis=1)


def _g_to_rel_table(dg):
    return dg[:, 639:896][:, ::-1]


def _place():
    x, y, c = lax.axis_index("x"), lax.axis_index("y"), lax.axis_index("c")
    others = [(1 - x, y), (x, 1 - y), (1 - x, 1 - y)]
    return x, y, c, others


def _half(c, rows):
    hr = rows // 2
    return pl.ds(pl.multiple_of(c * hr, 16), hr)


def _dma_sems(*shape):
    return pltpu.SemaphoreType.DMA(shape)


def _cast_slabs_call(ws, chip_arr, *, name, tm=256, task=None):
    n = len(ws)
    cols = ws[0].shape[1]
    tiles = [w.shape[0] // tm for w in ws]
    steps = max(tiles)

    def body(chip_ref, *refs):
        i = pl.program_id(0)
        for k in range(n):
            def cast(k=k):
                refs[n + k][...] = refs[k][...].astype(BF16)
            if tiles[k] == steps:
                cast()
            else:
                pl.when(i < tiles[k])(cast)

    in_specs = [pl.BlockSpec((tm, cols), lambda i, chip, t=t: (jnp.minimum(i, t - 1), 0)) for t in tiles]
    out_specs = [pl.BlockSpec((None, tm, cols), lambda i, chip, t=t: (chip[0], jnp.minimum(i, t - 1), 0)) for t in tiles]
    out_shape = [jax.ShapeDtypeStruct((N_CHIP,) + w.shape, BF16) for w in ws]
    return _pallas(body, grid=(steps,), in_specs=in_specs, out_specs=out_specs, out_shape=out_shape, name=name,
                   task=task, prefetch=1)(chip_arr, *ws)


def _cast_slab_call(w, chip_arr, *, name, tm=256, pad_rows=0):
    rows, cols = w.shape
    if pad_rows:
        tm = rows
    tm = min(tm, rows)

    def body(chip_ref, w_ref, o_ref):
        o_ref[0:tm, :] = w_ref[...].astype(BF16)
        if pad_rows:
            o_ref[tm:tm + pad_rows, :] = jnp.zeros((pad_rows, cols), BF16)

    return pl.pallas_call(
        body,
        grid_spec=pltpu.PrefetchScalarGridSpec(
            num_scalar_prefetch=1, grid=(rows // tm,),
            in_specs=[pl.BlockSpec((tm, cols), lambda i, chip: (i, 0))],
            out_specs=pl.BlockSpec((None, tm + pad_rows, cols), lambda i, chip: (chip[0], i, 0))),
        out_shape=jax.ShapeDtypeStruct((N_CHIP, rows + pad_rows, cols), BF16), name=name,
        compiler_params=_cparams(),
    )(chip_arr, w)


def _ag_ici_task(gathered):
    n = len(gathered)

    def copies(ins, outs, sems):
        send_sems, recv_sems = sems
        x, y, c, others = _place()
        me = 2 * x + y
        for k in range(n):
            mine = _half(c, gathered[k].shape[1])
            for t, (ox, oy) in enumerate(others):
                yield [pltpu.make_async_remote_copy(
                    src_ref=ins[k].at[me, mine], dst_ref=outs[k].at[slab, mine],
                    send_sem=send_sems.at[k, t], recv_sem=recv_sems.at[k, t],
                    device_id=(ox, oy, c), device_id_type=MESH) for slab in (me, 2 * ox + oy)]

    def issue(ins, outs, sems):
        for outgoing, _ in copies(ins, outs, sems):
            outgoing.start()

    def drain(ins, outs, sems):
        for outgoing, incoming in copies(ins, outs, sems):
            incoming.wait_recv()
            outgoing.wait_send()

    return _Task(gathered, [jax.ShapeDtypeStruct(g.shape, g.dtype) for g in gathered],
                 [_dma_sems(n, 3), _dma_sems(n, 3)], issue, drain, aliases={k: k for k in range(n)})


def _ag_d2d_task(gathered):
    n = len(gathered)

    def copies(ins, outs, sems):
        send_sems, recv_sems = sems
        x, y, c, others = _place()
        for k in range(n):
            rows = gathered[k].shape[1]
            mine, theirs = _half(c, rows), _half(1 - c, rows)
            for t, (ox, oy) in enumerate(others):
                slab = 2 * ox + oy
                pair = [pltpu.make_async_remote_copy(
                    src_ref=ins[k].at[slab, half], dst_ref=outs[k].at[slab, half],
                    send_sem=send_sems.at[k, t], recv_sem=recv_sems.at[k, t],
                    device_id=(x, y, 1 - c), device_id_type=MESH) for half in (mine, theirs)]
                yield pair

    def issue(ins, outs, sems):
        for outgoing, _ in copies(ins, outs, sems):
            outgoing.start()

    def drain(ins, outs, sems):
        for outgoing, incoming in copies(ins, outs, sems):
            incoming.wait_recv()
            outgoing.wait_send()

    return _Task(gathered, [jax.ShapeDtypeStruct(g.shape, g.dtype) for g in gathered],
                 [_dma_sems(n, 3), _dma_sems(n, 3)], issue, drain, aliases={k: k for k in range(n)})


def _rs_pair_task(ds):
    n = len(ds)

    def copies(ins, outs, sems):
        send_sems, recv_sems = sems
        x, y, c, _ = _place()
        for k in range(n):
            yield pltpu.make_async_remote_copy(
                src_ref=ins[k].at[:, _half(1 - c, ds[k].shape[1])], dst_ref=outs[k],
                send_sem=send_sems.at[k], recv_sem=recv_sems.at[k],
                device_id=(x, y, 1 - c), device_id_type=MESH)

    def issue(ins, outs, sems):
        for cp in copies(ins, outs, sems):
            cp.start()

    def drain(ins, outs, sems):
        for cp in copies(ins, outs, sems):
            cp.wait()

    return _Task(ds, [jax.ShapeDtypeStruct((N_CHIP, d.shape[1] // 2, d.shape[2]), d.dtype) for d in ds],
                 [_dma_sems(n), _dma_sems(n)], issue, drain)


def _pair_add_call(d, r1, c_arr, *, name, tm=512):
    _, rows, cols = d.shape
    hr = rows // 2
    tm = tm if hr % tm == 0 else hr
    nb = hr // tm

    def body(c_ref, d_ref, r_ref, o_ref):
        o_ref[...] = (d_ref[...].astype(F32) + r_ref[...].astype(F32)).astype(BF16)

    return pl.pallas_call(
        body,
        grid_spec=pltpu.PrefetchScalarGridSpec(
            num_scalar_prefetch=1, grid=(N_CHIP, nb),
            in_specs=[pl.BlockSpec((None, tm, cols), lambda j, i, c: (j, c[0] * nb + i, 0)),
                      pl.BlockSpec((None, tm, cols), lambda j, i, c: (j, i, 0))],
            out_specs=pl.BlockSpec((None, tm, cols), lambda j, i, c: (j, i, 0))),
        out_shape=jax.ShapeDtypeStruct((N_CHIP, hr, cols), BF16), name=name, compiler_params=_cparams(),
    )(c_arr, d, r1)


def _rs_chip_task(ps):
    n = len(ps)

    def copies(ins, outs, sems):
        send_sems, recv_sems = sems
        x, y, c, others = _place()
        for k in range(n):
            for t, (ox, oy) in enumerate(others):
                yield pltpu.make_async_remote_copy(
                    src_ref=ins[k].at[2 * ox + oy], dst_ref=outs[k].at[t],
                    send_sem=send_sems.at[k, t], recv_sem=recv_sems.at[k, t],
                    device_id=(ox, oy, c), device_id_type=MESH)

    def issue(ins, outs, sems):
        for cp in copies(ins, outs, sems):
            cp.start()

    def drain(ins, outs, sems):
        for cp in copies(ins, outs, sems):
            cp.wait()

    return _Task(ps, [jax.ShapeDtypeStruct((3,) + p.shape[1:], p.dtype) for p in ps],
                 [_dma_sems(n, 3), _dma_sems(n, 3)], issue, drain)


def _chip_sum_call(p, r2, place_arr, *, name, tm=512):
    _, hr, cols = r2.shape
    tm = tm if hr % tm == 0 else hr
    nb = hr // tm

    def body(place_ref, p_ref, r_ref, o_ref):
        acc = p_ref[...].astype(F32)
        for j in range(3):
            acc = acc + r_ref[j].astype(F32)
        o_ref[...] = acc

    return pl.pallas_call(
        body,
        grid_spec=pltpu.PrefetchScalarGridSpec(
            num_scalar_prefetch=1, grid=(nb,),
            in_specs=[pl.BlockSpec((None, tm, cols), lambda i, pc: (pc[0], i, 0)),
                      pl.BlockSpec((3, tm, cols), lambda i, pc: (0, i, 0))],
            out_specs=pl.BlockSpec((tm, cols), lambda i, pc: (pc[1] * nb + i, 0))),
        out_shape=jax.ShapeDtypeStruct((2 * hr, cols), F32), name=name, compiler_params=_cparams(),
    )(place_arr, p, r2)


def _rs_gather_task(gs):
    n = len(gs)

    def copies(ins, outs, sems):
        send_sems, recv_sems = sems
        x, y, c, _ = _place()
        for k in range(n):
            rows = gs[k].shape[0]
            mine, theirs = _half(c, rows), _half(1 - c, rows)
            yield [pltpu.make_async_remote_copy(
                src_ref=ins[k].at[mine], dst_ref=outs[k].at[half],
                send_sem=send_sems.at[k], recv_sem=recv_sems.at[k],
                device_id=(x, y, 1 - c), device_id_type=MESH) for half in (mine, theirs)]

    def issue(ins, outs, sems):
        for outgoing, _ in copies(ins, outs, sems):
            outgoing.start()

    def drain(ins, outs, sems):
        for outgoing, incoming in copies(ins, outs, sems):
            incoming.wait_recv()
            outgoing.wait_send()

    return _Task(gs, [jax.ShapeDtypeStruct(g.shape, g.dtype) for g in gs],
                 [_dma_sems(n), _dma_sems(n)], issue, drain, aliases={k: k for k in range(n)})


def _adamw(w, g, m, v):
    m = ADAM_B1 * m + (1.0 - ADAM_B1) * g
    v = ADAM_B2 * v + (1.0 - ADAM_B2) * jnp.square(g)
    m_hat = m / (1.0 - ADAM_B1 ** ADAM_STEP)
    v_hat = v / (1.0 - ADAM_B2 ** ADAM_STEP)
    delta = -ADAM_LR * (m_hat / (jnp.sqrt(v_hat) + ADAM_EPS) + ADAM_WD * w)
    return delta, m, v


def _adamw_call(items, *, name, tm=256, task=None):
    n = len(items)
    cols = items[0][0].shape[1]
    tiles = [it[0].shape[0] // tm for it in items]
    steps = max(tiles)

    def body(*refs):
        i = pl.program_id(0)
        ins, outs = refs[:4 * n], refs[4 * n:]
        for k in range(n):
            def update(k=k):
                g = ins[4 * k + 1][...]
                res = _adamw(ins[4 * k][...], g, ins[4 * k + 2][...], ins[4 * k + 3][...])
                outs[4 * k][...] = g
                for j in range(3):
                    outs[4 * k + 1 + j][...] = res[j]
            if tiles[k] == steps:
                update()
            else:
                pl.when(i < tiles[k])(update)

    in_specs, out_specs, out_shape, args = [], [], [], []
    for it, t in zip(items, tiles):
        spec = pl.BlockSpec((tm, cols), lambda i, t=t: (jnp.minimum(i, t - 1), 0))
        in_specs += [spec] * 4
        out_specs += [spec] * 4
        out_shape += [jax.ShapeDtypeStruct(it[0].shape, F32)] * 4
        args += list(it)
    res = _pallas(body, grid=(steps,), in_specs=in_specs, out_specs=out_specs, out_shape=out_shape,
                  name=name, task=task)(*args)
    outs, extra = res if task is not None else (res, None)
    grouped = [tuple(outs[4 * k:4 * k + 4]) for k in range(n)]
    return (grouped, extra) if task is not None else grouped


def _adamw_cols_call(w, g_pad, m, v, *, name, tn=256):
    rows, cols = w.shape

    def body(w_ref, g_ref, m_ref, v_ref, go_ref, d_ref, mo_ref, vo_ref):
        g = g_ref[0:rows, :]
        d, mn, vn = _adamw(w_ref[...], g, m_ref[...], v_ref[...])
        go_ref[...] = g
        d_ref[...] = d
        mo_ref[...] = mn
        vo_ref[...] = vn

    spec = pl.BlockSpec((rows, tn), lambda j: (0, j))
    gspec = pl.BlockSpec((g_pad.shape[0], tn), lambda j: (0, j))
    return _pallas(body, grid=(cols // tn,), in_specs=[spec, gspec, spec, spec], out_specs=(spec,) * 4,
                   out_shape=(jax.ShapeDtypeStruct((rows, cols), F32),) * 4, name=name)(w, g_pad, m, v)


N_DEV = 8
SMALL_ROWS = 24
SMALL_LAYOUT = {
    "g_mix_pre": (0, 0, 1, D), "g_mix_post": (1, 0, 1, D), "g_mem_kv": (2, 0, 1, D), "g_mem_pre": (3, 0, 1, D),
    "g_mem_post": (4, 0, 1, D), "g_ff_pre": (5, 0, 1, D), "g_ff_post": (6, 0, 1, D),
    "g_fox_out": (7, 0, 1, D_GRP), "g_chk_out": (7, D_GRP, 1, D_GRP), "b_fgt": (8, 0, 1, 8),
    "rel_bias": (16, 0, 8, N_REL),
}
SMALL = list(SMALL_LAYOUT)


LOSS_ROW = 9


def _small_call(grads, ws, ms, vs, loss_blk, task, *, name):
    n = len(SMALL)
    t_in, t_out = len(task.arrays), len(task.out_shapes)

    def body(*refs):
        g_refs, w_refs, m_refs, v_refs = (refs[j * n:(j + 1) * n] for j in range(4))
        p = 4 * n
        loss_ref, tins = refs[p], refs[p + 1:p + 1 + t_in]
        p += 1 + t_in
        outs, loss_out, touts = refs[p:p + 4 * n], refs[p + 4 * n], refs[p + 4 * n + 1:p + 4 * n + 1 + t_out]
        p += 4 * n + 1 + t_out
        mine, slots, send_sems, recv_sems = refs[p:p + 4]
        tsems = refs[p + 4:]
        task.issue(tins, touts, tsems)
        x, y, c, _ = _place()
        me = 4 * x + 2 * y + c
        mine[...] = jnp.zeros_like(mine)
        for k, name_k in enumerate(SMALL):
            r, l, nr, nl = SMALL_LAYOUT[name_k]
            mine[r:r + nr, l:l + nl] = g_refs[k][0:nr, 0:nl]
        mine[LOSS_ROW:LOSS_ROW + 1, 0:128] = loss_ref[0:1, :]
        slots[me] = mine[...]
        peers = [(dx, dy, dc) for dx in (0, 1) for dy in (0, 1) for dc in (0, 1)][1:]
        cps = []
        for t, (dx, dy, dc) in enumerate(peers):
            px, py, pc = (x + dx) % 2, (y + dy) % 2, (c + dc) % 2
            cps.append(pltpu.make_async_remote_copy(
                src_ref=mine, dst_ref=slots.at[me], send_sem=send_sems.at[t], recv_sem=recv_sems.at[t],
                device_id=(px, py, pc), device_id_type=MESH))
            cps[-1].start()
        for t, (dx, dy, dc) in enumerate(peers):
            px, py, pc = (x + dx) % 2, (y + dy) % 2, (c + dc) % 2
            pltpu.make_async_remote_copy(
                src_ref=mine, dst_ref=slots.at[4 * px + 2 * py + pc], send_sem=send_sems.at[t],
                recv_sem=recv_sems.at[t], device_id=(px, py, pc), device_id_type=MESH).wait_recv()
        for cp in cps:
            cp.wait_send()
        total = slots[0]
        for j in range(1, N_DEV):
            total = total + slots[j]
        for k, name_k in enumerate(SMALL):
            r, l, nr, nl = SMALL_LAYOUT[name_k]
            g = total[r:r + nr, l:l + nl]
            d, mn, vn = _adamw(w_refs[k][...], g, m_refs[k][...], v_refs[k][...])
            for j, val in enumerate((g, d, mn, vn)):
                outs[4 * k + j][...] = val
        loss_out[...] = jnp.broadcast_to(total[LOSS_ROW:LOSS_ROW + 1, 0:128], loss_out.shape)
        task.drain(tins, touts, tsems)

    vm = pl.BlockSpec(memory_space=pltpu.VMEM)
    out_shape = [jax.ShapeDtypeStruct(ws[k].shape, F32) for k in SMALL for _ in range(4)]
    out_shape += [jax.ShapeDtypeStruct((8, 128), F32)] + list(task.out_shapes)
    res = pl.pallas_call(
        body, in_specs=[vm] * (4 * n + 1) + [ANY] * t_in, out_specs=[vm] * (4 * n + 1) + [ANY] * t_out,
        out_shape=out_shape,
        scratch_shapes=[pltpu.VMEM((SMALL_ROWS, D), F32), pltpu.VMEM((N_DEV, SMALL_ROWS, D), F32),
                        _dma_sems(N_DEV - 1), _dma_sems(N_DEV - 1)] + list(task.sems),
        input_output_aliases={4 * n + 1 + i: 4 * n + 1 + j for i, j in task.aliases.items()},
        name=name,
    )(*[d[k] for d in (grads, ws, ms, vs) for k in SMALL], loss_blk, *task.arrays)
    return ({k: tuple(res[4 * i:4 * i + 4]) for i, k in enumerate(SMALL)}, res[4 * n], list(res[4 * n + 1:]))


WEIGHTS = ["w_in", "b_fgt", "rel_bias", "g_fox_out", "g_chk_out", "w_out", "g_mix_pre", "g_mix_post", "g_mem_kv",
           "w_mq", "w_mk", "w_mv", "w_mo", "g_mem_pre", "g_mem_post", "w_ff1", "w_ff2", "g_ff_pre", "g_ff_post"]
BIG = ["w_in", "w_out", "w_mq", "w_mk", "w_mv", "w_mo", "w_ff1", "w_ff2"]


IN_SHARD = D_IN // N_CHIP
IN_PAD = 800


IN_PIECES = [(0, 0, 770), (800, 770, 766), (1566, 3072, 4), (1600, 3076, 4), (1604, 1536, 766), (2400, 2302, 770)]
PAD_ZEROS = [(800 * j + IN_SHARD, IN_PAD - IN_SHARD) for j in range(N_CHIP)]
ALL_ZEROS = [(D_IN, D_ALL - D_IN)]


def _reorder_rows_call(src, to_all, *, name, tn=256):
    rows, cols = src.shape
    zeros = ALL_ZEROS if to_all else PAD_ZEROS

    def body(s_ref, o_ref):
        for pad0, all0, cnt in IN_PIECES:
            s0, d0 = (pad0, all0) if to_all else (all0, pad0)
            o_ref[d0:d0 + cnt, :] = s_ref[s0:s0 + cnt, :]
        for z0, cnt in zeros:
            o_ref[z0:z0 + cnt, :] = jnp.zeros((cnt, tn), src.dtype)

    spec = pl.BlockSpec((rows, tn), lambda j: (0, j))
    return _pallas(body, grid=(cols // tn,), in_specs=[spec], out_specs=spec,
                   out_shape=jax.ShapeDtypeStruct((rows, cols), src.dtype), name=name)(src)


def kernel(x, mem, w_in, b_fgt, rel_bias, g_fox_out, g_chk_out, w_out, g_mix_pre, g_mix_post, g_mem_kv, w_mq, w_mk, w_mv, w_mo, g_mem_pre, g_mem_post, w_ff1, w_ff2, g_ff_pre, g_ff_post, loss_target, m_w_in, m_b_fgt, m_rel_bias, m_g_fox_out, m_g_chk_out, m_w_out, m_g_mix_pre, m_g_mix_post, m_g_mem_kv, m_w_mq, m_w_mk, m_w_mv, m_w_mo, m_g_mem_pre, m_g_mem_post, m_w_ff1, m_w_ff2, m_g_ff_pre, m_g_ff_post, v_w_in, v_b_fgt, v_rel_bias, v_g_fox_out, v_g_chk_out, v_w_out, v_g_mix_pre, v_g_mix_post, v_g_mem_kv, v_w_mq, v_w_mk, v_w_mv, v_w_mo, v_g_mem_pre, v_g_mem_post, v_w_ff1, v_w_ff2, v_g_ff_pre, v_g_ff_post):
    w = dict(w_in=w_in, b_fgt=b_fgt, rel_bias=rel_bias, g_fox_out=g_fox_out, g_chk_out=g_chk_out, w_out=w_out,
             g_mix_pre=g_mix_pre, g_mix_post=g_mix_post, g_mem_kv=g_mem_kv, w_mq=w_mq, w_mk=w_mk, w_mv=w_mv,
             w_mo=w_mo, g_mem_pre=g_mem_pre, g_mem_post=g_mem_post, w_ff1=w_ff1, w_ff2=w_ff2, g_ff_pre=g_ff_pre,
             g_ff_post=g_ff_post)
    m = dict(w_in=m_w_in, b_fgt=m_b_fgt, rel_bias=m_rel_bias, g_fox_out=m_g_fox_out, g_chk_out=m_g_chk_out,
             w_out=m_w_out, g_mix_pre=m_g_mix_pre, g_mix_post=m_g_mix_post, g_mem_kv=m_g_mem_kv, w_mq=m_w_mq,
             w_mk=m_w_mk, w_mv=m_w_mv, w_mo=m_w_mo, g_mem_pre=m_g_mem_pre, g_mem_post=m_g_mem_post,
             w_ff1=m_w_ff1, w_ff2=m_w_ff2, g_ff_pre=m_g_ff_pre, g_ff_post=m_g_ff_post)
    v = dict(w_in=v_w_in, b_fgt=v_b_fgt, rel_bias=v_rel_bias, g_fox_out=v_g_fox_out, g_chk_out=v_g_chk_out,
             w_out=v_w_out, g_mix_pre=v_g_mix_pre, g_mix_post=v_g_mix_post, g_mem_kv=v_g_mem_kv, w_mq=v_w_mq,
             w_mk=v_w_mk, w_mv=v_w_mv, w_mo=v_w_mo, g_mem_pre=v_g_mem_pre, g_mem_post=v_g_mem_post,
             w_ff1=v_w_ff1, w_ff2=v_w_ff2, g_ff_pre=v_g_ff_pre, g_ff_post=v_g_ff_post)

    def rows(d, k):
        return d[k][0] if k == "rel_bias" else d[k]

    xs, mems, target = x[0], mem[0], loss_target[0]
    S = xs.shape[0]
    sp = {k: rows(w, k) for k in SMALL}
    b_pad = jnp.pad(sp["b_fgt"], ((0, 0), (0, 120)))
    chip = 2 * lax.axis_index("x") + lax.axis_index("y")
    chip_arr = jnp.reshape(chip, (1,)).astype(jnp.int32)
    c_arr = jnp.reshape(lax.axis_index("c"), (1,)).astype(jnp.int32)
    place_arr = jnp.concatenate([chip_arr, c_arr])
    w_in_t, m_in_t, v_in_t = w["w_in"][0].T, m["w_in"][0].T, v["w_in"][0].T
    slab = {"w_in": _cast_slab_call(w_in_t, chip_arr, name="cast_w_in", pad_rows=IN_PAD - IN_SHARD)}

    def gather_ici(names):
        return _ag_ici_task([slab[k] for k in names])

    def pair_add(k, d, r1):
        return _pair_add_call(d, r1, c_arr, name="rs_pair_add_" + k)

    rest, (g_in,) = _cast_slabs_call([w[k][0] for k in BIG[1:]], chip_arr, name="cast_rest",
                                     task=gather_ici(["w_in"]))
    slab.update(zip(BIG[1:], rest))
    h1, (g_in,) = _rms_fwd_call(xs, sp["g_mix_pre"], name="rms_mix_pre", task=_ag_d2d_task([g_in]))
    w_all_t = _reorder_rows_call(g_in.reshape(N_CHIP * IN_PAD, D), True, name="w_in_rows")
    proj, (g_out, g_mq) = _mm_nt(h1, w_all_t, "plain", rows=(0, 3072), name="mm_proj",
                                 task=gather_ici(["w_out", "w_mq"]))
    fl_raw = _mm_nt(h1, w_all_t, "plain", rows=(3072, 128), name="mm_gate", out_dtype=F32, tn=128)
    c_rep, c_t = _fox_prep_call(fl_raw, b_pad, name="fox_prep")
    bias = _chk_bias_call(_rel_table_to_g(sp["rel_bias"]), name="chk_bias")
    mid = ["w_mk", "w_mv", "w_mo", "w_ff1"]
    (yf, lse), got = _fox_fwd_call(proj, c_rep, c_t, name="fox_fwd",
                                   task=_merge_tasks([gather_ici(mid), _ag_d2d_task([g_out, g_mq])]))
    g_mid, (g_out, g_mq) = got[:4], got[4:]
    yc, got = _chk_fwd_call(proj, bias, name="chk_fwd",
                            task=_merge_tasks([gather_ici(["w_ff2"]), _ag_d2d_task(g_mid)]))
    g_ff2, (g_mk, g_mv, g_mo, g_ff1) = got[0], got[1:]
    yn = _mix_norm_fwd_call(yf, yc, sp["g_fox_out"], sp["g_chk_out"], name="mix_norm_fwd")
    z, (g_ff2,) = _mm_nn(yn, g_out, "rows", name="mm_out", out_dtype=F32, task=_ag_d2d_task([g_ff2]))
    x1, h2 = _post_pre_call(xs, z, sp["g_mix_post"], sp["g_mem_pre"], name="post_mix")
    memn = _rms_fwd_call(mems, sp["g_mem_kv"], name="rms_mem_kv")
    q2 = _mm_nn(h2, g_mq, "rows", name="mm_mq")
    k2 = _mm_nn(memn, g_mk, "rows", name="mm_mk")
    v2 = _mm_nn(memn, g_mv, "rows", name="mm_mv")
    o2 = _mem_fwd_call(q2, k2, v2, name="mem_fwd")
    y2 = _mm_nn(o2, g_mo, "rows", name="mm_mo", out_dtype=F32)
    x2, h3 = _post_pre_call(x1, y2, sp["g_mem_post"], sp["g_ff_pre"], name="post_mem")
    act, relu = _mm_nn(h3, g_ff1, "cols", name="mm_ff1", epi="relu2")
    y3 = _mm_nn(act, g_ff2, "rows", name="mm_ff2", out_dtype=F32, tm=1024)
    loss_blk, dx3, dy3, dg_ff_post = _final_call(x2, y3, sp["g_ff_post"], target, name="final")

    d_ff2 = _mm_tn(act, dy3, name="mm_dff2", tk=512, tn=1024).reshape(N_CHIP, D_FF // N_CHIP, D)
    du, (r1,) = _mm_nt(dy3, g_ff2, "rows", name="mm_du", mul2r=relu, task=_rs_pair_task([d_ff2]))
    p_ff2 = pair_add("w_ff2", d_ff2, r1)
    d_ff1 = _mm_tn(h3, du, name="mm_dff1", cols4=True)
    dh3, (r1,) = _mm_nt(du, g_ff1, "cols", name="mm_dh3", out_dtype=F32, tm=1024, task=_rs_pair_task([d_ff1]))
    p_ff1 = pair_add("w_ff1", d_ff1, r1)
    dx2, dy2, dg_ff_pre, dg_mem_post = _bwd_mid_call(dx3, x2, dh3, sp["g_ff_pre"], y2, sp["g_mem_post"], name="bwd_ff")
    d_mo = _mm_tn(o2, dy2, name="mm_dmo").reshape(N_CHIP, D // N_CHIP, D)
    do2 = _mm_nt(dy2, g_mo, "rows", name="mm_do2")
    dq2, dk2, dv2 = _mem_bwd_call(q2, k2, v2, do2, name="mem_bwd")
    d_mq = _mm_tn(h2, dq2, name="mm_dmq").reshape(N_CHIP, D // N_CHIP, D)
    dh2 = _mm_nt(dq2, g_mq, "rows", name="mm_dh2", out_dtype=F32)
    d_mk = _mm_tn(memn, dk2, name="mm_dmk").reshape(N_CHIP, D // N_CHIP, D)
    d_mv = _mm_tn(memn, dv2, name="mm_dmv").reshape(N_CHIP, D // N_CHIP, D)
    dmn_k = _mm_nt(dk2, g_mk, "rows", name="mm_dmemk", out_dtype=F32)
    dmn_v = _mm_nt(dv2, g_mv, "rows", name="mm_dmemv", out_dtype=F32)
    dg_mem_kv = _gain_grad_call(mems, sp["g_mem_kv"], dmn_k, dmn_v, name="gain_mem_kv")
    dx1, dz, dg_mem_pre, dg_mix_post = _bwd_mid_call(dx2, x1, dh2, sp["g_mem_pre"], z, sp["g_mix_post"], name="bwd_mem")
    d_out = _mm_tn(yn, dz, name="mm_dout").reshape(N_CHIP, D // N_CHIP, D)
    late = ["w_mo", "w_mq", "w_mk", "w_mv", "w_out"]
    d_late = [d_mo, d_mq, d_mk, d_mv, d_out]
    dyn, r1_late = _mm_nt(dz, g_out, "rows", name="mm_dyn", out_dtype=F32, task=_rs_pair_task(d_late))
    p_late = [pair_add(k, d, r1) for k, d, r1 in zip(late, d_late, r1_late)]
    dof, doc, delta, dg_fox, dg_chk = _mix_norm_bwd_call(dyn, yf, yc, sp["g_fox_out"], sp["g_chk_out"], name="mix_norm_bwd")
    (dqf, dkf, dvf, dcq, dck), r2_ff = _fox_bwd_call(proj, dof, lse, delta, c_rep, c_t, name="fox_bwd",
                                                      task=_rs_chip_task([p_ff2, p_ff1]))
    (dqc, dkc, dvc, dgrev), r2_late = _chk_bwd_call(proj, doc, bias, name="chk_bwd", task=_rs_chip_task(p_late))
    first = ["w_ff2", "w_ff1"] + late
    f_first = [_chip_sum_call(p, r, place_arr, name="rs_chip_sum_" + k)
               for k, p, r in zip(first, [p_ff2, p_ff1] + p_late, r2_ff + r2_late)]
    dc8 = dcq[:, :, 0:2, :].transpose(0, 2, 1, 3).reshape(8, S) + dck[:, ::HEAD].T
    dc_rows = jnp.concatenate([dc8, jnp.zeros((120, S), F32)], axis=0)
    dfl, db_fgt = _fox_gate_bwd_call(dc_rows, fl_raw, b_pad, name="fox_gate_bwd")
    dproj = jnp.concatenate([dqf, dkf, dvf, dqc, dkc, dvc, dfl], axis=1)
    d_all_t, g_first = _mm_tn(dproj, h1, name="mm_dwin", tk=640, tn=1024, task=_rs_gather_task(f_first))
    grads = dict(zip(first, g_first))
    d_in = _reorder_rows_call(d_all_t, False, name="d_in_rows").reshape(N_CHIP, IN_PAD, D)
    delta_w, new_m, new_v = {}, {}, {}

    def adamw_items(names):
        return [(w[k][0], grads[k], m[k][0], v[k][0]) for k in names]

    upd_late, (r1,) = _adamw_call(adamw_items(late), name="adamw_late", tm=64, task=_rs_pair_task([d_in]))
    p_in = pair_add("w_in", d_in, r1)
    dh1, (r2_in,) = _mm_nn(dproj, w_all_t, "plain", name="mm_dh1", out_dtype=F32, tm=1024,
                           task=_rs_chip_task([p_in]))
    f_in = _chip_sum_call(p_in, r2_in, place_arr, name="rs_chip_sum_w_in")
    upd_ff = _adamw_call(adamw_items(first[:2]), name="adamw_ff")
    for k, res in zip(late + first[:2], upd_late + upd_ff):
        grads[k], delta_w[k], new_m[k], new_v[k] = res
    grad_x, dg_mix_pre = _bwd_last_call(dx1, xs, dh1, sp["g_mix_pre"], name="bwd_mix")

    small_g = {"g_mix_pre": dg_mix_pre, "g_mix_post": dg_mix_post, "g_mem_kv": dg_mem_kv, "g_mem_pre": dg_mem_pre,
               "g_mem_post": dg_mem_post, "g_ff_pre": dg_ff_pre, "g_ff_post": dg_ff_post, "g_fox_out": dg_fox,
               "g_chk_out": dg_chk, "b_fgt": db_fgt,
               "rel_bias": _g_to_rel_table(dgrev[:, 0:2, :].reshape(8, ROLL_W))}
    small, loss_out, (g_w_in,) = _small_call(
        small_g, sp, {k: rows(m, k) for k in SMALL}, {k: rows(v, k) for k in SMALL}, loss_blk,
        _rs_gather_task([f_in]), name="small_allreduce_adamw")
    loss = loss_out[0, 0]
    res = _adamw_cols_call(w_in_t, g_w_in, m_in_t, v_in_t, name="adamw_w_in")
    grads["w_in"], delta_w["w_in"], new_m["w_in"], new_v["w_in"] = (a.T for a in res)
    for k in SMALL:
        vals = small[k]
        if k == "rel_bias":
            vals = tuple(a[None] for a in vals)
        grads[k], delta_w[k], new_m[k], new_v[k] = vals

    def out(d, k):
        return d[k][None] if k in BIG else d[k]

    return (loss, grad_x[None], *[out(grads, k) for k in WEIGHTS], *[out(delta_w, k) for k in WEIGHTS],
            *[out(new_m, k) for k in WEIGHTS], *[out(new_v, k) for k in WEIGHTS])
```

```python
import functools

import jax
import jax.numpy as jnp
from jax import lax
from jax.experimental import pallas as pl
from jax.experimental.pallas import tpu as pltpu

F32 = jnp.float32
BF16 = jnp.bfloat16

D = 1024
HEAD = 64
N_PAIR = 4
D_GRP = 512
CHUNK = 64
LEFT = 8
MAX_REL = 128
N_REL = 2 * MAX_REL + 1
N_MEM = 256
MEM_HEADS = 4
MEM_HD = 256
D_FF = 4096
D_IN = 3080
D_ALL = 3200
EPS = 1e-6
TQ = 256
WIN = (LEFT + TQ // CHUNK) * CHUNK
PADK = LEFT * CHUNK
ROLL_W = 1024
NEG = -1e30
N_CHIP = 4
VMEM_LIMIT = 48 * 1024 * 1024

ADAM_LR = 0.001
ADAM_B1 = 0.9
ADAM_B2 = 0.999
ADAM_EPS = 1e-08
ADAM_WD = 0.01
ADAM_STEP = 10

MESH = pl.DeviceIdType.MESH


def _cparams():
    return pltpu.CompilerParams(vmem_limit_bytes=VMEM_LIMIT)


ANY = pl.BlockSpec(memory_space=pl.ANY)


class _Task:
    def __init__(self, arrays, out_shapes, sems, issue, drain, aliases=None):
        self.arrays, self.out_shapes, self.sems = list(arrays), list(out_shapes), list(sems)
        self.issue, self.drain, self.aliases = issue, drain, dict(aliases or {})


def _merge_tasks(tasks):
    tasks = [t for t in tasks if t is not None]
    if len(tasks) == 1:
        return tasks[0]
    cuts, a, o, s = [], 0, 0, 0
    aliases = {}
    for t in tasks:
        cuts.append((a, o, s))
        aliases.update({a + i: o + j for i, j in t.aliases.items()})
        a, o, s = a + len(t.arrays), o + len(t.out_shapes), s + len(t.sems)

    def part(fn_name):
        def run(ins, outs, sems):
            for t, (a0, o0, s0) in zip(tasks, cuts):
                getattr(t, fn_name)(ins[a0:a0 + len(t.arrays)], outs[o0:o0 + len(t.out_shapes)],
                                    sems[s0:s0 + len(t.sems)])
        return run

    return _Task([x for t in tasks for x in t.arrays], [x for t in tasks for x in t.out_shapes],
                 [x for t in tasks for x in t.sems], part("issue"), part("drain"), aliases)


def _pallas(body, *, grid, in_specs, out_specs, out_shape, name, scratch_shapes=(), task=None, prefetch=0):
    def make(kernel, i_specs, o_specs, o_shape, scratch, aliases):
        if prefetch:
            spec = pltpu.PrefetchScalarGridSpec(num_scalar_prefetch=prefetch, grid=grid, in_specs=i_specs,
                                                out_specs=o_specs, scratch_shapes=scratch)
            return pl.pallas_call(kernel, grid_spec=spec, out_shape=o_shape, input_output_aliases=aliases,
                                  name=name, compiler_params=_cparams())
        return pl.pallas_call(kernel, grid=grid, in_specs=i_specs, out_specs=o_specs, out_shape=o_shape,
                              scratch_shapes=scratch, input_output_aliases=aliases, name=name,
                              compiler_params=_cparams())

    if task is None:
        return make(body, list(in_specs), out_specs, out_shape, list(scratch_shapes), {})
    single = not isinstance(out_shape, (tuple, list))
    o_shapes = [out_shape] if single else list(out_shape)
    o_specs = [out_specs] if single else list(out_specs)
    n_in, n_out, n_scr = len(in_specs), len(o_shapes), len(scratch_shapes)
    t_in, t_out = len(task.arrays), len(task.out_shapes)

    def carried(*refs):
        cut = [prefetch, n_in, t_in, n_out, t_out, n_scr]
        parts, p = [], 0
        for c in cut:
            parts.append(refs[p:p + c])
            p += c
        scalars, ins, tins, outs, touts, scr = parts
        tsems = refs[p:]
        ids = [pl.program_id(a) for a in range(len(grid))]
        first = functools.reduce(jnp.logical_and, [i == 0 for i in ids])
        last = functools.reduce(jnp.logical_and, [i == g - 1 for i, g in zip(ids, grid)])

        @pl.when(first)
        def _():
            task.issue(tins, touts, tsems)
        body(*scalars, *ins, *outs, *scr)

        @pl.when(last)
        def _():
            task.drain(tins, touts, tsems)

    call = make(carried, list(in_specs) + [ANY] * t_in, o_specs + [ANY] * t_out,
                o_shapes + list(task.out_shapes), list(scratch_shapes) + list(task.sems),
                {prefetch + n_in + i: n_out + j for i, j in task.aliases.items()})

    def run(*args):
        res = call(*args, *task.arrays)
        outs = res[:n_out]
        return (outs[0] if single else tuple(outs)), list(res[n_out:])

    return run


def _dot(a, b):
    return jnp.dot(a, b, preferred_element_type=F32)


def _dot_nt(a, b):
    return lax.dot_general(a, b, (((1,), (1,)), ((), ())), preferred_element_type=F32)


def _dot_tn(a, b):
    return lax.dot_general(a, b, (((0,), (0,)), ((), ())), preferred_element_type=F32)


def _split3(x):
    hi = x.astype(BF16)
    r1 = x - hi.astype(F32)
    mid = r1.astype(BF16)
    lo = (r1 - mid.astype(F32)).astype(BF16)
    return hi, mid, lo


def _dot3(x, m01):
    hi, mid, lo = _split3(x)
    return _dot(hi, m01) + _dot(mid, m01) + _dot(lo, m01)


def _dot3_l(m01, x):
    hi, mid, lo = _split3(x)
    return _dot(m01, hi) + _dot(m01, mid) + _dot(m01, lo)


def _mm_nn(a, b, kind, *, name, out_dtype=BF16, tm=2048, tn=512, epi=None, task=None):
    M, K = a.shape
    if kind == "plain":
        N = b.shape[1]
        b_spec = pl.BlockSpec((K, tn), lambda m, n: (0, n))
    elif kind == "rows":
        N = b.shape[2]
        b_spec = pl.BlockSpec((N_CHIP, K // N_CHIP, tn), lambda m, n: (0, 0, n))
    else:
        nq = b.shape[2]
        N = N_CHIP * nq
        per = nq // tn
        b_spec = pl.BlockSpec((None, K, tn), lambda m, n: (n // per, 0, n % per))
    tm = min(tm, M)
    kq = K // N_CHIP

    def body(a_ref, b_ref, *o_refs):
        if kind == "rows":
            acc = _dot(a_ref[:, 0:kq], b_ref[0])
            for j in range(1, N_CHIP):
                acc += _dot(a_ref[:, j * kq:(j + 1) * kq], b_ref[j])
        else:
            acc = _dot(a_ref[...], b_ref[...])
        if epi == "relu2":
            r = jnp.maximum(acc, 0.0)
            o_refs[0][...] = (r * r).astype(BF16)
            o_refs[1][...] = r.astype(BF16)
        else:
            o_refs[0][...] = acc.astype(out_dtype)

    o_spec = pl.BlockSpec((tm, tn), lambda m, n: (m, n))
    if epi == "relu2":
        out_shape = (jax.ShapeDtypeStruct((M, N), BF16), jax.ShapeDtypeStruct((M, N), BF16))
        out_specs = (o_spec, o_spec)
    else:
        out_shape = jax.ShapeDtypeStruct((M, N), out_dtype)
        out_specs = o_spec
    return _pallas(
        body, grid=(M // tm, N // tn),
        in_specs=[pl.BlockSpec((tm, K), lambda m, n: (m, 0)), b_spec],
        out_specs=out_specs, out_shape=out_shape, name=name, task=task,
    )(a, b)


def _mm_nt(a, b, kind, *, name, out_dtype=BF16, tm=2048, tn=512, mul2r=None, task=None, rows=None):
    M, K = a.shape
    if kind == "plain":
        first, N = rows if rows is not None else (0, b.shape[0])
        n0 = first // tn
        b_spec = pl.BlockSpec((tn, K), lambda m, n: (n0 + n, 0))
    elif kind == "rows":
        nq = b.shape[1]
        N = N_CHIP * nq
        tn = min(tn, nq)
        per = nq // tn
        b_spec = pl.BlockSpec((None, tn, K), lambda m, n: (n // per, n % per, 0))
    else:
        N = b.shape[1]
        b_spec = pl.BlockSpec((N_CHIP, tn, K // N_CHIP), lambda m, n: (0, n, 0))
    tm = min(tm, M)
    kq = K // N_CHIP

    def body(a_ref, b_ref, *rest):
        o_ref = rest[-1]
        if kind == "cols":
            acc = _dot_nt(a_ref[:, 0:kq], b_ref[0])
            for j in range(1, N_CHIP):
                acc += _dot_nt(a_ref[:, j * kq:(j + 1) * kq], b_ref[j])
        else:
            acc = _dot_nt(a_ref[...], b_ref[...])
        if mul2r is not None:
            acc = acc * (2.0 * rest[0][...].astype(F32))
        o_ref[...] = acc.astype(out_dtype)

    in_specs = [pl.BlockSpec((tm, K), lambda m, n: (m, 0)), b_spec]
    args = [a, b]
    if mul2r is not None:
        in_specs.append(pl.BlockSpec((tm, tn), lambda m, n: (m, n)))
        args.append(mul2r)
    return _pallas(
        body, grid=(M // tm, N // tn), in_specs=in_specs,
        out_specs=pl.BlockSpec((tm, tn), lambda m, n: (m, n)),
        out_shape=jax.ShapeDtypeStruct((M, N), out_dtype), name=name, task=task,
    )(*args)


def _mm_tn(a, b, *, name, out_dtype=BF16, tk=1024, tn=512, cols4=False, task=None):
    M, K1 = a.shape
    N = b.shape[1]
    tk = min(tk, K1)
    tn = min(tn, N)

    def body(a_ref, b_ref, o_ref):
        o_ref[...] = _dot_tn(a_ref[...], b_ref[...]).astype(out_dtype)

    if cols4:
        per = (N // N_CHIP) // tn
        out_shape = jax.ShapeDtypeStruct((N_CHIP, K1, N // N_CHIP), out_dtype)
        o_spec = pl.BlockSpec((None, tk, tn), lambda k, n: (n // per, k, n % per))
    else:
        out_shape = jax.ShapeDtypeStruct((K1, N), out_dtype)
        o_spec = pl.BlockSpec((tk, tn), lambda k, n: (k, n))
    return _pallas(
        body, grid=(K1 // tk, N // tn),
        in_specs=[pl.BlockSpec((M, tk), lambda k, n: (0, k)), pl.BlockSpec((M, tn), lambda k, n: (0, n))],
        out_specs=o_spec, out_shape=out_shape, name=name, task=task,
    )(a, b)


def _rms(x, g):
    r = lax.rsqrt(jnp.mean(x * x, axis=-1, keepdims=True) + EPS)
    return x * r * g


def _rms_bwd(x, g, dy):
    r = lax.rsqrt(jnp.mean(x * x, axis=-1, keepdims=True) + EPS)
    xh = x * r
    dg = jnp.sum(dy * xh, axis=0, keepdims=True)
    dxh = dy * g
    dx = r * (dxh - xh * jnp.mean(dxh * xh, axis=-1, keepdims=True))
    return dx, dg


def _row_spec(tm, n):
    return pl.BlockSpec((tm, n), lambda i: (i, 0))


def _vec_spec(n):
    return pl.BlockSpec((1, n), lambda i: (0, 0))


def _acc_spec(n):
    return pl.BlockSpec((8, n), lambda i: (0, 0))


def _acc_add(ref, row, i):
    @pl.when(i == 0)
    def _():
        ref[...] = jnp.zeros_like(ref)
    ref[0:1, :] += row


def _rms_fwd_call(x, g, *, name, tm=512, task=None):
    M, n = x.shape
    tm = min(tm, M)

    def body(x_ref, g_ref, h_ref):
        h_ref[...] = _rms(x_ref[...], g_ref[...]).astype(BF16)

    return _pallas(
        body, grid=(M // tm,), in_specs=[_row_spec(tm, n), _vec_spec(n)], out_specs=_row_spec(tm, n),
        out_shape=jax.ShapeDtypeStruct((M, n), BF16), name=name, task=task,
    )(x, g)


def _post_pre_call(xres, z, g_post, g_pre, *, name, tm=512):
    M, n = xres.shape

    def body(x_ref, z_ref, gp_ref, gn_ref, xo_ref, h_ref):
        xn = x_ref[...] + _rms(z_ref[...], gp_ref[...])
        xo_ref[...] = xn
        h_ref[...] = _rms(xn, gn_ref[...]).astype(BF16)

    return pl.pallas_call(
        body, grid=(M // tm,),
        in_specs=[_row_spec(tm, n), _row_spec(tm, n), _vec_spec(n), _vec_spec(n)],
        out_specs=(_row_spec(tm, n), _row_spec(tm, n)),
        out_shape=(jax.ShapeDtypeStruct((M, n), F32), jax.ShapeDtypeStruct((M, n), BF16)),
        name=name, compiler_params=_cparams(),
    )(xres, z, g_post, g_pre)


def _final_call(x2, y3, g_post, target, *, name, tm=512):
    M, n = x2.shape

    def body(x_ref, y_ref, g_ref, t_ref, loss_ref, dx_ref, dy_ref, dg_ref):
        i = pl.program_id(0)
        y = y_ref[...]
        g = g_ref[...]
        diff = x_ref[...] + _rms(y, g) - t_ref[...]
        part = 0.5 * jnp.sum(jnp.sum(diff * diff, axis=1, keepdims=True), axis=0, keepdims=True) / n

        @pl.when(i == 0)
        def _():
            loss_ref[...] = jnp.zeros_like(loss_ref)
        loss_ref[...] += jnp.broadcast_to(part, loss_ref.shape)
        dx = diff / n
        dx_ref[...] = dx
        dy, dg = _rms_bwd(y, g, dx)
        dy_ref[...] = dy.astype(BF16)
        _acc_add(dg_ref, dg, i)

    return pl.pallas_call(
        body, grid=(M // tm,),
        in_specs=[_row_spec(tm, n), _row_spec(tm, n), _vec_spec(n), _row_spec(tm, n)],
        out_specs=(pl.BlockSpec((8, 128), lambda i: (0, 0)), _row_spec(tm, n), _row_spec(tm, n), _acc_spec(n)),
        out_shape=(jax.ShapeDtypeStruct((8, 128), F32), jax.ShapeDtypeStruct((M, n), F32),
                   jax.ShapeDtypeStruct((M, n), BF16), jax.ShapeDtypeStruct((8, n), F32)),
        name=name, compiler_params=_cparams(),
    )(x2, y3, g_post, target)


def _bwd_mid_call(dx_in, x, dh, g_pre, y, g_post, *, name, tm=512):
    M, n = x.shape

    def body(dxi_ref, x_ref, dh_ref, gpre_ref, y_ref, gpost_ref, dx_ref, dy_ref, dgpre_ref, dgpost_ref):
        i = pl.program_id(0)
        d1, dg1 = _rms_bwd(x_ref[...], gpre_ref[...], dh_ref[...])
        dx = dxi_ref[...] + d1
        dx_ref[...] = dx
        dy, dg2 = _rms_bwd(y_ref[...], gpost_ref[...], dx)
        dy_ref[...] = dy.astype(BF16)
        _acc_add(dgpre_ref, dg1, i)
        _acc_add(dgpost_ref, dg2, i)

    return pl.pallas_call(
        body, grid=(M // tm,),
        in_specs=[_row_spec(tm, n), _row_spec(tm, n), _row_spec(tm, n), _vec_spec(n), _row_spec(tm, n), _vec_spec(n)],
        out_specs=(_row_spec(tm, n), _row_spec(tm, n), _acc_spec(n), _acc_spec(n)),
        out_shape=(jax.ShapeDtypeStruct((M, n), F32), jax.ShapeDtypeStruct((M, n), BF16),
                   jax.ShapeDtypeStruct((8, n), F32), jax.ShapeDtypeStruct((8, n), F32)),
        name=name, compiler_params=_cparams(),
    )(dx_in, x, dh, g_pre, y, g_post)


def _bwd_last_call(dx_in, x, dh, g_pre, *, name, tm=512, task=None):
    M, n = x.shape

    def body(dxi_ref, x_ref, dh_ref, g_ref, dx_ref, dg_ref):
        i = pl.program_id(0)
        d1, dg1 = _rms_bwd(x_ref[...], g_ref[...], dh_ref[...])
        dx_ref[...] = dxi_ref[...] + d1
        _acc_add(dg_ref, dg1, i)

    return _pallas(
        body, grid=(M // tm,),
        in_specs=[_row_spec(tm, n), _row_spec(tm, n), _row_spec(tm, n), _vec_spec(n)],
        out_specs=(_row_spec(tm, n), _acc_spec(n)),
        out_shape=(jax.ShapeDtypeStruct((M, n), F32), jax.ShapeDtypeStruct((8, n), F32)),
        name=name, task=task,
    )(dx_in, x, dh, g_pre)


def _gain_grad_call(x, g, dy_a, dy_b, *, name):
    M, n = x.shape

    def body(x_ref, g_ref, a_ref, b_ref, dg_ref):
        _, dg = _rms_bwd(x_ref[...], g_ref[...], a_ref[...] + b_ref[...])
        dg_ref[...] = jnp.zeros_like(dg_ref)
        dg_ref[0:1, :] = dg

    return pl.pallas_call(
        body, grid=(1,),
        in_specs=[_row_spec(M, n), _vec_spec(n), _row_spec(M, n), _row_spec(M, n)],
        out_specs=_acc_spec(n), out_shape=jax.ShapeDtypeStruct((8, n), F32),
        name=name, compiler_params=_cparams(),
    )(x, g, dy_a, dy_b)


def _head_group_matrix():
    a = lax.broadcasted_iota(jnp.int32, (D_GRP, D_GRP), 0) // HEAD
    b = lax.broadcasted_iota(jnp.int32, (D_GRP, D_GRP), 1) // HEAD
    return jnp.where(a == b, 1.0, 0.0).astype(BF16)


def _mix_norm_fwd_call(yf, yc, gf, gc, *, name, tm=512):
    M = yf.shape[0]

    def body(yf_ref, yc_ref, gf_ref, gc_ref, o_ref):
        o_ref[:, 0:D_GRP] = _rms(yf_ref[...], gf_ref[...]).astype(BF16)
        o_ref[:, D_GRP:D] = _rms(yc_ref[...], gc_ref[...]).astype(BF16)

    return pl.pallas_call(
        body, grid=(M // tm,),
        in_specs=[_row_spec(tm, D_GRP), _row_spec(tm, D_GRP), _vec_spec(D_GRP), _vec_spec(D_GRP)],
        out_specs=_row_spec(tm, D), out_shape=jax.ShapeDtypeStruct((M, D), BF16),
        name=name, compiler_params=_cparams(),
    )(yf, yc, gf, gc)


def _mix_norm_bwd_call(dyn, yf, yc, gf, gc, *, name, tm=TQ):
    M = yf.shape[0]

    def body(dyn_ref, yf_ref, yc_ref, gf_ref, gc_ref, dof_ref, doc_ref, delta_ref, dgf_ref, dgc_ref):
        i = pl.program_id(0)
        yf_ = yf_ref[...]
        dof, dgf = _rms_bwd(yf_, gf_ref[...], dyn_ref[:, 0:D_GRP])
        doc, dgc = _rms_bwd(yc_ref[...], gc_ref[...], dyn_ref[:, D_GRP:D])
        dof_b = dof.astype(BF16)
        dof_ref[...] = dof_b
        doc_ref[...] = doc.astype(BF16)
        prod = dof_b.astype(F32) * yf_
        hi = prod.astype(BF16)
        lo = (prod - hi.astype(F32)).astype(BF16)
        grp = _head_group_matrix()
        delta_ref[...] = (_dot(hi, grp) + _dot(lo, grp)).T
        _acc_add(dgf_ref, dgf, i)
        _acc_add(dgc_ref, dgc, i)

    return pl.pallas_call(
        body, grid=(M // tm,),
        in_specs=[_row_spec(tm, D), _row_spec(tm, D_GRP), _row_spec(tm, D_GRP), _vec_spec(D_GRP), _vec_spec(D_GRP)],
        out_specs=(_row_spec(tm, D_GRP), _row_spec(tm, D_GRP), pl.BlockSpec((None, D_GRP, tm), lambda i: (i, 0, 0)),
                   _acc_spec(D_GRP), _acc_spec(D_GRP)),
        out_shape=(jax.ShapeDtypeStruct((M, D_GRP), BF16), jax.ShapeDtypeStruct((M, D_GRP), BF16),
                   jax.ShapeDtypeStruct((M // tm, D_GRP, tm), F32), jax.ShapeDtypeStruct((8, D_GRP), F32),
                   jax.ShapeDtypeStruct((8, D_GRP), F32)),
        name=name, compiler_params=_cparams(),
    )(dyn, yf, yc, gf, gc)


def _tri(n, lower_incl):
    a = lax.broadcasted_iota(jnp.int32, (n, n), 0)
    b = lax.broadcasted_iota(jnp.int32, (n, n), 1)
    return jnp.where(a >= b, 1.0, 0.0).astype(BF16) if lower_incl else jnp.where(a <= b, 1.0, 0.0).astype(BF16)


def _fox_prep_call(fl_raw, b_pad, *, name):
    S = fl_raw.shape[0]
    nb = S // TQ

    def body(fl_ref, b_ref, crep_ref, ct_ref, carry_ref):
        i = pl.program_id(0)

        @pl.when(i == 0)
        def _():
            carry_ref[...] = jnp.zeros_like(carry_ref)
        logf = jax.nn.log_sigmoid(fl_ref[...] + b_ref[...])
        cb = _dot3_l(_tri(TQ, True), logf) + carry_ref[0:1, :]
        carry_ref[0:1, :] = cb[TQ - 1:TQ, :]
        a = lax.broadcasted_iota(jnp.int32, (128, D_GRP), 0)
        b = lax.broadcasted_iota(jnp.int32, (128, D_GRP), 1) // HEAD
        expand = jnp.where(a == b, 1.0, 0.0).astype(BF16)
        crep = _dot3(cb, expand)
        crep_ref[...] = crep
        ct_ref[...] = crep.T

    return pl.pallas_call(
        body, grid=(nb,),
        in_specs=[_row_spec(TQ, 128), _vec_spec(128)],
        out_specs=(_row_spec(TQ, D_GRP), pl.BlockSpec((None, D_GRP, TQ), lambda i: (i, 0, 0))),
        out_shape=(jax.ShapeDtypeStruct((S, D_GRP), F32), jax.ShapeDtypeStruct((nb, D_GRP, TQ), F32)),
        scratch_shapes=[pltpu.VMEM((8, 128), F32)],
        name=name, compiler_params=_cparams(),
    )(fl_raw, b_pad)


def _lane_masks():
    lane = lax.broadcasted_iota(jnp.int32, (1, 128), 1)
    return lane < HEAD, lane >= HEAD


def _fox_fwd_call(proj, c_rep, c_t, *, name, task=None):
    S = proj.shape[0]
    nq = S // TQ
    scale = HEAD ** -0.5

    def body(q_ref, k_ref, v_ref, c_ref, ct_ref, o_ref, lse_ref):
        i = pl.program_id(1)
        m_lo, m_hi = _lane_masks()
        masks = (m_lo, m_hi)
        q = q_ref[...] * scale
        qm = [jnp.where(mk, q, jnp.zeros_like(q)) for mk in masks]
        cq = c_ref[...]
        cqh = [cq[:, 0:1], cq[:, HEAD:HEAD + 1]]
        row = lax.broadcasted_iota(jnp.int32, (TQ, TQ), 0)
        col = lax.broadcasted_iota(jnp.int32, (TQ, TQ), 1)

        def scores(j):
            start = pl.multiple_of(j * TQ, TQ)
            k = k_ref[pl.ds(start, TQ), :]
            ct = ct_ref[j]
            return tuple(_dot_nt(qm[h], k) + (cqh[h] - ct[HEAD * h:HEAD * h + 1, :]) for h in range(2))

        def update(j, ss, state, masked):
            ms, ls, acc = state
            start = pl.multiple_of(j * TQ, TQ)
            v = v_ref[pl.ds(start, TQ), :]
            new_m, new_l, pv, alpha_l = [], [], [], []
            for h in range(2):
                s = ss[h]
                if masked:
                    s = jnp.where(row >= col, s, NEG)
                mn = jnp.maximum(ms[h], jnp.max(s, axis=1, keepdims=True))
                alpha = jnp.exp(ms[h] - mn)
                p = jnp.exp(s - mn)
                new_l.append(alpha * ls[h] + jnp.sum(p, axis=1, keepdims=True))
                new_m.append(mn)
                alpha_l.append(alpha)
                pv.append(_dot(p.astype(BF16), jnp.where(masks[h], v, jnp.zeros_like(v))))
            alpha_lane = jnp.where(m_lo, alpha_l[0], alpha_l[1])
            acc = acc * alpha_lane + pv[0] + pv[1]
            return (tuple(new_m), tuple(new_l), acc)

        def step(j, carry):
            ss, state = carry
            return (scores(j + 1), update(j, ss, state, False))

        init = ((jnp.full((TQ, 1), NEG, F32),) * 2, (jnp.zeros((TQ, 1), F32),) * 2, jnp.zeros((TQ, 128), F32))
        ss, state = lax.fori_loop(0, i, step, (scores(0), init))
        ms, ls, acc = update(i, ss, state, True)
        l_lane = jnp.where(m_lo, ls[0], ls[1])
        o_ref[...] = acc / l_lane
        lse_ref[...] = jnp.where(m_lo, ms[0] + jnp.log(ls[0]), ms[1] + jnp.log(ls[1])).T

    return _pallas(
        body, grid=(N_PAIR, nq),
        in_specs=[pl.BlockSpec((TQ, 128), lambda p, i: (i, p)),
                  pl.BlockSpec((S, 128), lambda p, i: (0, N_PAIR + p)),
                  pl.BlockSpec((S, 128), lambda p, i: (0, 2 * N_PAIR + p)),
                  pl.BlockSpec((TQ, 128), lambda p, i: (i, p)),
                  pl.BlockSpec((nq, 128, TQ), lambda p, i: (0, p, 0))],
        out_specs=(pl.BlockSpec((TQ, 128), lambda p, i: (i, p)), pl.BlockSpec((None, 128, TQ), lambda p, i: (i, p, 0))),
        out_shape=(jax.ShapeDtypeStruct((S, D_GRP), F32), jax.ShapeDtypeStruct((nq, D_GRP, TQ), F32)),
        name=name, task=task,
    )(proj, proj, proj, c_rep, c_t)


def _fox_bwd_call(proj, do, lse_t, delta_t, c_rep, c_t, *, name, task=None):
    S = proj.shape[0]
    nq = S // TQ
    scale = HEAD ** -0.5

    def body(q_ref, k_ref, v_ref, do_ref, lse_ref, dl_ref, ck_ref, ct_ref,
             dq_ref, dk_ref, dv_ref, dcq_ref, dck_ref, dqa_ref):
        j = pl.program_id(1)
        m_lo, m_hi = _lane_masks()
        masks = (m_lo, m_hi)

        @pl.when(j == 0)
        def _():
            dqa_ref[...] = jnp.zeros_like(dqa_ref)
            dcq_ref[...] = jnp.zeros_like(dcq_ref)
        k = k_ref[...]
        v = v_ref[...]
        km = [jnp.where(mk, k, jnp.zeros_like(k)) for mk in masks]
        ck = ck_ref[...]
        krow = lax.broadcasted_iota(jnp.int32, (TQ, TQ), 0)
        qcol = lax.broadcasted_iota(jnp.int32, (TQ, TQ), 1)

        def probs(i):
            start = pl.multiple_of(i * TQ, TQ)
            q = q_ref[pl.ds(start, TQ), :]
            do = do_ref[pl.ds(start, TQ), :]
            lse = lse_ref[i]
            cq = ct_ref[i]
            out = []
            for h in range(2):
                lo = HEAD * h
                qm = jnp.where(masks[h], q * scale, jnp.zeros_like(q))
                dom = jnp.where(masks[h], do, jnp.zeros_like(do))
                st = _dot_nt(k, qm) + (cq[lo:lo + 1, :] - ck[:, lo:lo + 1])
                out.append((jnp.exp(st - lse[lo:lo + 1, :]), _dot_nt(v, dom)))
            return tuple(out)

        def update(i, pd, carry, masked):
            dk, dv, dck = carry
            start = pl.multiple_of(i * TQ, TQ)
            q = q_ref[pl.ds(start, TQ), :]
            do = do_ref[pl.ds(start, TQ), :]
            dl = dl_ref[i]
            dq = jnp.zeros((TQ, 128), F32)
            new_dck = []
            for h in range(2):
                lo = HEAD * h
                qm = jnp.where(masks[h], q, jnp.zeros_like(q))
                dom = jnp.where(masks[h], do, jnp.zeros_like(do))
                pt, dpt = pd[h]
                if masked:
                    pt = jnp.where(qcol >= krow, pt, 0.0)
                dst = pt * (dpt - dl[lo:lo + 1, :])
                dcq_ref[i, h:h + 1, :] += jnp.sum(dst, axis=0, keepdims=True)
                new_dck.append(dck[h] + jnp.sum(dst, axis=1, keepdims=True))
                dsb = (dst * scale).astype(BF16)
                dv = dv + _dot(pt.astype(BF16), dom)
                dk = dk + _dot(dsb, qm)
                dq = dq + _dot_tn(dsb, km[h])
            dqa_ref[pl.ds(start, TQ), :] += dq
            return (dk, dv, tuple(new_dck))

        def step(i, carry):
            pd, sums = carry
            return (probs(jnp.minimum(i + 1, nq - 1)), update(i, pd, sums, False))

        init = (jnp.zeros((TQ, 128), F32), jnp.zeros((TQ, 128), F32), (jnp.zeros((TQ, 1), F32),) * 2)
        first = probs(j)
        second = probs(jnp.minimum(j + 1, nq - 1))
        _, (dk, dv, dck) = lax.fori_loop(j + 1, nq, step, (second, update(j, first, init, True)))
        dk_ref[...] = dk.astype(BF16)
        dv_ref[...] = dv.astype(BF16)
        dck_ref[...] = -jnp.where(m_lo, dck[0], dck[1])

        @pl.when(j == nq - 1)
        def _():
            dq_ref[...] = dqa_ref[...].astype(BF16)

    res = lambda p, j: (0, p)
    stat = pl.BlockSpec((nq, 128, TQ), lambda p, j: (0, p, 0))
    blk = pl.BlockSpec((TQ, 128), lambda p, j: (j, p))
    return _pallas(
        body, grid=(N_PAIR, nq), task=task,
        in_specs=[pl.BlockSpec((S, 128), res),
                  pl.BlockSpec((TQ, 128), lambda p, j: (j, N_PAIR + p)),
                  pl.BlockSpec((TQ, 128), lambda p, j: (j, 2 * N_PAIR + p)),
                  pl.BlockSpec((S, 128), res), stat, stat, blk, stat],
        out_specs=(pl.BlockSpec((S, 128), res), blk, blk,
                   pl.BlockSpec((None, nq, 8, TQ), lambda p, j: (p, 0, 0, 0)), blk),
        out_shape=(jax.ShapeDtypeStruct((S, D_GRP), BF16), jax.ShapeDtypeStruct((S, D_GRP), BF16),
                   jax.ShapeDtypeStruct((S, D_GRP), BF16), jax.ShapeDtypeStruct((N_PAIR, nq, 8, TQ), F32),
                   jax.ShapeDtypeStruct((S, D_GRP), F32)),
        scratch_shapes=[pltpu.VMEM((S, 128), F32)],
        name=name,
    )(proj, proj, proj, do, lse_t, delta_t, c_rep, c_t)


def _fox_gate_bwd_call(dc_rows, fl_raw, b_pad, *, name):
    S = fl_raw.shape[0]
    nb = S // TQ

    def body(dc_ref, fl_ref, b_ref, dfl_ref, db_ref, carry_ref):
        i = pl.program_id(0)

        @pl.when(i == 0)
        def _():
            carry_ref[...] = jnp.zeros_like(carry_ref)
        rc = _dot3(dc_ref[...], _tri(TQ, True)) + carry_ref[:, 0:1]
        carry_ref[...] = jnp.broadcast_to(rc[:, 0:1], carry_ref.shape)
        fl = fl_ref[...] + b_ref[...]
        dfl = rc.T * jax.nn.sigmoid(-fl)
        dfl_ref[...] = dfl.astype(BF16)
        _acc_add(db_ref, jnp.sum(dfl, axis=0, keepdims=True), i)

    rev = lambda i: (nb - 1 - i, 0)
    return pl.pallas_call(
        body, grid=(nb,),
        in_specs=[pl.BlockSpec((128, TQ), lambda i: (0, nb - 1 - i)), pl.BlockSpec((TQ, 128), rev), _vec_spec(128)],
        out_specs=(pl.BlockSpec((TQ, 128), rev), _acc_spec(128)),
        out_shape=(jax.ShapeDtypeStruct((S, 128), BF16), jax.ShapeDtypeStruct((8, 128), F32)),
        scratch_shapes=[pltpu.VMEM((128, 128), F32)],
        name=name, compiler_params=_cparams(),
    )(dc_rows, fl_raw, b_pad)


def _chk_bias_call(g_rev, *, name):
    def body(g_ref, o_ref):
        x = jnp.broadcast_to(g_ref[...], (TQ, ROLL_W))
        rolled = pltpu.roll(x, ROLL_W - (TQ - 1), 1, stride=1, stride_axis=0)
        qc = lax.broadcasted_iota(jnp.int32, (TQ, WIN), 0) // CHUNK
        kc = lax.broadcasted_iota(jnp.int32, (TQ, WIN), 1) // CHUNK
        band = (kc >= qc) & (kc <= qc + LEFT)
        o_ref[...] = jnp.where(band, rolled[:, 0:WIN], NEG)

    return pl.pallas_call(
        body, grid=(8,),
        in_specs=[pl.BlockSpec((None, 1, ROLL_W), lambda h: (h, 0, 0))],
        out_specs=pl.BlockSpec((None, TQ, WIN), lambda h: (h, 0, 0)),
        out_shape=jax.ShapeDtypeStruct((8, TQ, WIN), F32), name=name, compiler_params=_cparams(),
    )(g_rev.reshape(8, 1, ROLL_W))


def _chk_scores(i, qm, kwin, bias, scale):
    s = _dot_nt(qm * scale, kwin) + bias
    kc = lax.broadcasted_iota(jnp.int32, (TQ, WIN), 1) // CHUNK
    return jnp.where(kc + i * (TQ // CHUNK) >= LEFT, s, NEG)


def _chk_fwd_call(proj, bias, *, name, task=None):
    S = proj.shape[0]
    nq = S // TQ
    scale = HEAD ** -0.5

    def body(q_ref, k_ref, v_ref, b_ref, o_ref, kp_ref, vp_ref):
        i = pl.program_id(1)

        @pl.when(i == 0)
        def _():
            kp_ref[0:PADK, :] = jnp.zeros((PADK, 128), BF16)
            vp_ref[0:PADK, :] = jnp.zeros((PADK, 128), BF16)
            kp_ref[PADK:PADK + S, :] = k_ref[...]
            vp_ref[PADK:PADK + S, :] = v_ref[...]
        masks = _lane_masks()
        q = q_ref[...]
        start = pl.multiple_of(i * TQ, TQ)
        kwin = kp_ref[pl.ds(start, WIN), :]
        vwin = vp_ref[pl.ds(start, WIN), :]
        ss = [_chk_scores(i, jnp.where(masks[h], q, jnp.zeros_like(q)), kwin, b_ref[h], scale) for h in range(2)]
        ps = []
        for s in ss:
            p = jnp.exp(s - jnp.max(s, axis=1, keepdims=True))
            ps.append((p / jnp.sum(p, axis=1, keepdims=True)).astype(BF16))
        o_ref[...] = (_dot(ps[0], jnp.where(masks[0], vwin, jnp.zeros_like(vwin)))
                      + _dot(ps[1], jnp.where(masks[1], vwin, jnp.zeros_like(vwin))))

    c0 = 3 * N_PAIR
    return _pallas(
        body, grid=(N_PAIR, nq), task=task,
        in_specs=[pl.BlockSpec((TQ, 128), lambda p, i: (i, c0 + p)),
                  pl.BlockSpec((S, 128), lambda p, i: (0, c0 + N_PAIR + p)),
                  pl.BlockSpec((S, 128), lambda p, i: (0, c0 + 2 * N_PAIR + p)),
                  pl.BlockSpec((2, TQ, WIN), lambda p, i: (p, 0, 0))],
        out_specs=pl.BlockSpec((TQ, 128), lambda p, i: (i, p)),
        out_shape=jax.ShapeDtypeStruct((S, D_GRP), F32),
        scratch_shapes=[pltpu.VMEM((S + PADK, 128), BF16), pltpu.VMEM((S + PADK, 128), BF16)],
        name=name,
    )(proj, proj, proj, bias)


def _chk_bwd_call(proj, do, bias, *, name, task=None):
    S = proj.shape[0]
    nq = S // TQ
    scale = HEAD ** -0.5

    def body(q_ref, k_ref, v_ref, do_ref, b_ref, dq_ref, dk_ref, dv_ref, dg_ref, kp_ref, vp_ref, dkp_ref, dvp_ref, db_ref):
        i = pl.program_id(1)

        @pl.when(i == 0)
        def _():
            kp_ref[0:PADK, :] = jnp.zeros((PADK, 128), BF16)
            vp_ref[0:PADK, :] = jnp.zeros((PADK, 128), BF16)
            kp_ref[PADK:PADK + S, :] = k_ref[...]
            vp_ref[PADK:PADK + S, :] = v_ref[...]
            dkp_ref[...] = jnp.zeros_like(dkp_ref)
            dvp_ref[...] = jnp.zeros_like(dvp_ref)
            db_ref[...] = jnp.zeros_like(db_ref)
        masks = _lane_masks()
        q = q_ref[...]
        dout = do_ref[...]
        start = pl.multiple_of(i * TQ, TQ)
        kwin = kp_ref[pl.ds(start, WIN), :]
        vwin = vp_ref[pl.ds(start, WIN), :]
        qm = [jnp.where(mk, q, jnp.zeros_like(q)) for mk in masks]
        dom = [jnp.where(mk, dout, jnp.zeros_like(dout)) for mk in masks]
        ss = [_chk_scores(i, qm[h], kwin, b_ref[h], scale) for h in range(2)]
        dps = [_dot_nt(dom[h], vwin) for h in range(2)]
        pbs, dsbs = [], []
        for h in range(2):
            p = jnp.exp(ss[h] - jnp.max(ss[h], axis=1, keepdims=True))
            p = p / jnp.sum(p, axis=1, keepdims=True)
            ds = p * (dps[h] - jnp.sum(p * dps[h], axis=1, keepdims=True))
            db_ref[h] += ds
            pbs.append(p.astype(BF16))
            dsbs.append((ds * scale).astype(BF16))
        dq_ref[...] = (_dot(dsbs[0], jnp.where(masks[0], kwin, jnp.zeros_like(kwin)))
                       + _dot(dsbs[1], jnp.where(masks[1], kwin, jnp.zeros_like(kwin)))).astype(BF16)
        dkp_ref[pl.ds(start, WIN), :] += _dot_tn(dsbs[0], qm[0]) + _dot_tn(dsbs[1], qm[1])
        dvp_ref[pl.ds(start, WIN), :] += _dot_tn(pbs[0], dom[0]) + _dot_tn(pbs[1], dom[1])

        @pl.when(i == nq - 1)
        def _():
            dk_ref[...] = dkp_ref[PADK:PADK + S, :].astype(BF16)
            dv_ref[...] = dvp_ref[PADK:PADK + S, :].astype(BF16)
            a = lax.broadcasted_iota(jnp.int32, (TQ, TQ), 0)
            b = lax.broadcasted_iota(jnp.int32, (TQ, TQ), 1)
            flip = jnp.where(a + b == TQ - 1, 1.0, 0.0).astype(BF16)
            e = lax.broadcasted_iota(jnp.int32, (1, ROLL_W), 1)
            dg_ref[...] = jnp.zeros_like(dg_ref)
            for h in range(2):
                rev = _dot3_l(flip, db_ref[h])
                wide = jnp.concatenate([rev, jnp.zeros((TQ, ROLL_W - WIN), F32)], axis=1)
                diag = pltpu.roll(wide, 0, 1, stride=1, stride_axis=0)
                dg = jnp.sum(diag, axis=0, keepdims=True)
                lo = jnp.sum(jnp.where(e <= 639, dg, 0.0), axis=1, keepdims=True)
                hi = jnp.sum(jnp.where(e >= 895, dg, 0.0), axis=1, keepdims=True)
                dg_ref[h:h + 1, :] = jnp.where(e == 639, lo, jnp.where(e == 895, hi, dg))

    c0 = 3 * N_PAIR
    res = lambda p, i: (0, p)
    return _pallas(
        body, grid=(N_PAIR, nq), task=task,
        in_specs=[pl.BlockSpec((TQ, 128), lambda p, i: (i, c0 + p)),
                  pl.BlockSpec((S, 128), lambda p, i: (0, c0 + N_PAIR + p)),
                  pl.BlockSpec((S, 128), lambda p, i: (0, c0 + 2 * N_PAIR + p)),
                  pl.BlockSpec((TQ, 128), lambda p, i: (i, p)),
                  pl.BlockSpec((2, TQ, WIN), lambda p, i: (p, 0, 0))],
        out_specs=(pl.BlockSpec((TQ, 128), lambda p, i: (i, p)), pl.BlockSpec((S, 128), res),
                   pl.BlockSpec((S, 128), res), pl.BlockSpec((None, 8, ROLL_W), lambda p, i: (p, 0, 0))),
        out_shape=(jax.ShapeDtypeStruct((S, D_GRP), BF16), jax.ShapeDtypeStruct((S, D_GRP), BF16),
                   jax.ShapeDtypeStruct((S, D_GRP), BF16), jax.ShapeDtypeStruct((N_PAIR, 8, ROLL_W), F32)),
        scratch_shapes=[pltpu.VMEM((S + PADK, 128), BF16), pltpu.VMEM((S + PADK, 128), BF16),
                        pltpu.VMEM((S + PADK, 128), F32), pltpu.VMEM((S + PADK, 128), F32),
                        pltpu.VMEM((2, TQ, WIN), F32)],
        name=name,
    )(proj, proj, proj, do, bias)


def _mem_fwd_call(q, k, v, *, name, tq=512):
    S = q.shape[0]
    scale = MEM_HD ** -0.5

    def body(q_ref, k_ref, v_ref, o_ref):
        s = _dot_nt(q_ref[...] * scale, k_ref[...])
        p = jnp.exp(s - jnp.max(s, axis=1, keepdims=True))
        p = p / jnp.sum(p, axis=1, keepdims=True)
        o_ref[...] = _dot(p.astype(BF16), v_ref[...]).astype(BF16)

    return pl.pallas_call(
        body, grid=(MEM_HEADS, S // tq),
        in_specs=[pl.BlockSpec((tq, MEM_HD), lambda h, i: (i, h)),
                  pl.BlockSpec((N_MEM, MEM_HD), lambda h, i: (0, h)),
                  pl.BlockSpec((N_MEM, MEM_HD), lambda h, i: (0, h))],
        out_specs=pl.BlockSpec((tq, MEM_HD), lambda h, i: (i, h)),
        out_shape=jax.ShapeDtypeStruct((S, D), BF16), name=name, compiler_params=_cparams(),
    )(q, k, v)


def _mem_bwd_call(q, k, v, do, *, name, tq=512):
    S = q.shape[0]
    n = S // tq
    scale = MEM_HD ** -0.5

    def body(q_ref, k_ref, v_ref, do_ref, dq_ref, dk_ref, dv_ref, dka_ref, dva_ref):
        i = pl.program_id(1)

        @pl.when(i == 0)
        def _():
            dka_ref[...] = jnp.zeros_like(dka_ref)
            dva_ref[...] = jnp.zeros_like(dva_ref)
        qb = q_ref[...]
        kb = k_ref[...]
        dob = do_ref[...]
        s = _dot_nt(qb * scale, kb)
        p = jnp.exp(s - jnp.max(s, axis=1, keepdims=True))
        p = p / jnp.sum(p, axis=1, keepdims=True)
        dp = _dot_nt(dob, v_ref[...])
        ds = p * (dp - jnp.sum(p * dp, axis=1, keepdims=True))
        dsb = (ds * scale).astype(BF16)
        dq_ref[...] = _dot(dsb, kb).astype(BF16)
        dka_ref[...] += _dot_tn(dsb, qb)
        dva_ref[...] += _dot_tn(p.astype(BF16), dob)

        @pl.when(i == n - 1)
        def _():
            dk_ref[...] = dka_ref[...].astype(BF16)
            dv_ref[...] = dva_ref[...].astype(BF16)

    kv = pl.BlockSpec((N_MEM, MEM_HD), lambda h, i: (0, h))
    qs = pl.BlockSpec((tq, MEM_HD), lambda h, i: (i, h))
    return pl.pallas_call(
        body, grid=(MEM_HEADS, n), in_specs=[qs, kv, kv, qs], out_specs=(qs, kv, kv),
        out_shape=(jax.ShapeDtypeStruct((S, D), BF16), jax.ShapeDtypeStruct((N_MEM, D), BF16),
                   jax.ShapeDtypeStruct((N_MEM, D), BF16)),
        scratch_shapes=[pltpu.VMEM((N_MEM, MEM_HD), F32), pltpu.VMEM((N_MEM, MEM_HD), F32)],
        name=name, compiler_params=_cparams(),
    )(q, k, v, do)


def _rel_table_to_g(rel):
    return jnp.concatenate([
        jnp.broadcast_to(rel[:, N_REL - 1:N_REL], (8, 640)),
        rel[:, 1:N_REL - 1][:, ::-1],
        jnp.broadcast_to(rel[:, 0:1], (8, 129)),
    ], axis=1)


def _g_to_rel_table(dg):
    return dg[:, 639:896][:, ::-1]


def _place():
    x, y, c = lax.axis_index("x"), lax.axis_index("y"), lax.axis_index("c")
    others = [(1 - x, y), (x, 1 - y), (1 - x, 1 - y)]
    return x, y, c, others


def _half(c, rows):
    hr = rows // 2
    return pl.ds(pl.multiple_of(c * hr, 16), hr)


def _dma_sems(*shape):
    return pltpu.SemaphoreType.DMA(shape)


def _cast_slabs_call(ws, chip_arr, *, name, tm=256, task=None):
    n = len(ws)
    cols = ws[0].shape[1]
    tiles = [w.shape[0] // tm for w in ws]
    steps = max(tiles)

    def body(chip_ref, *refs):
        i = pl.program_id(0)
        for k in range(n):
            def cast(k=k):
                refs[n + k][...] = refs[k][...].astype(BF16)
            if tiles[k] == steps:
                cast()
            else:
                pl.when(i < tiles[k])(cast)

    in_specs = [pl.BlockSpec((tm, cols), lambda i, chip, t=t: (jnp.minimum(i, t - 1), 0)) for t in tiles]
    out_specs = [pl.BlockSpec((None, tm, cols), lambda i, chip, t=t: (chip[0], jnp.minimum(i, t - 1), 0)) for t in tiles]
    out_shape = [jax.ShapeDtypeStruct((N_CHIP,) + w.shape, BF16) for w in ws]
    return _pallas(body, grid=(steps,), in_specs=in_specs, out_specs=out_specs, out_shape=out_shape, name=name,
                   task=task, prefetch=1)(chip_arr, *ws)


def _cast_slab_call(w, chip_arr, *, name, tm=256, pad_rows=0):
    rows, cols = w.shape
    if pad_rows:
        tm = rows
    tm = min(tm, rows)

    def body(chip_ref, w_ref, o_ref):
        o_ref[0:tm, :] = w_ref[...].astype(BF16)
        if pad_rows:
            o_ref[tm:tm + pad_rows, :] = jnp.zeros((pad_rows, cols), BF16)

    return pl.pallas_call(
        body,
        grid_spec=pltpu.PrefetchScalarGridSpec(
            num_scalar_prefetch=1, grid=(rows // tm,),
            in_specs=[pl.BlockSpec((tm, cols), lambda i, chip: (i, 0))],
            out_specs=pl.BlockSpec((None, tm + pad_rows, cols), lambda i, chip: (chip[0], i, 0))),
        out_shape=jax.ShapeDtypeStruct((N_CHIP, rows + pad_rows, cols), BF16), name=name,
        compiler_params=_cparams(),
    )(chip_arr, w)


def _ag_ici_task(gathered):
    n = len(gathered)

    def copies(ins, outs, sems):
        send_sems, recv_sems = sems
        x, y, c, others = _place()
        me = 2 * x + y
        for k in range(n):
            mine = _half(c, gathered[k].shape[1])
            for t, (ox, oy) in enumerate(others):
                yield [pltpu.make_async_remote_copy(
                    src_ref=ins[k].at[me, mine], dst_ref=outs[k].at[slab, mine],
                    send_sem=send_sems.at[k, t], recv_sem=recv_sems.at[k, t],
                    device_id=(ox, oy, c), device_id_type=MESH) for slab in (me, 2 * ox + oy)]

    def issue(ins, outs, sems):
        for outgoing, _ in copies(ins, outs, sems):
            outgoing.start()

    def drain(ins, outs, sems):
        for outgoing, incoming in copies(ins, outs, sems):
            incoming.wait_recv()
            outgoing.wait_send()

    return _Task(gathered, [jax.ShapeDtypeStruct(g.shape, g.dtype) for g in gathered],
                 [_dma_sems(n, 3), _dma_sems(n, 3)], issue, drain, aliases={k: k for k in range(n)})


def _ag_d2d_task(gathered):
    n = len(gathered)

    def copies(ins, outs, sems):
        send_sems, recv_sems = sems
        x, y, c, others = _place()
        for k in range(n):
            rows = gathered[k].shape[1]
            mine, theirs = _half(c, rows), _half(1 - c, rows)
            for t, (ox, oy) in enumerate(others):
                slab = 2 * ox + oy
                pair = [pltpu.make_async_remote_copy(
                    src_ref=ins[k].at[slab, half], dst_ref=outs[k].at[slab, half],
                    send_sem=send_sems.at[k, t], recv_sem=recv_sems.at[k, t],
                    device_id=(x, y, 1 - c), device_id_type=MESH) for half in (mine, theirs)]
                yield pair

    def issue(ins, outs, sems):
        for outgoing, _ in copies(ins, outs, sems):
            outgoing.start()

    def drain(ins, outs, sems):
        for outgoing, incoming in copies(ins, outs, sems):
            incoming.wait_recv()
            outgoing.wait_send()

    return _Task(gathered, [jax.ShapeDtypeStruct(g.shape, g.dtype) for g in gathered],
                 [_dma_sems(n, 3), _dma_sems(n, 3)], issue, drain, aliases={k: k for k in range(n)})


def _rs_pair_task(ds):
    n = len(ds)

    def copies(ins, outs, sems):
        send_sems, recv_sems = sems
        x, y, c, _ = _place()
        for k in range(n):
            yield pltpu.make_async_remote_copy(
                src_ref=ins[k].at[:, _half(1 - c, ds[k].shape[1])], dst_ref=outs[k],
                send_sem=send_sems.at[k], recv_sem=recv_sems.at[k],
                device_id=(x, y, 1 - c), device_id_type=MESH)

    def issue(ins, outs, sems):
        for cp in copies(ins, outs, sems):
            cp.start()

    def drain(ins, outs, sems):
        for cp in copies(ins, outs, sems):
            cp.wait()

    return _Task(ds, [jax.ShapeDtypeStruct((N_CHIP, d.shape[1] // 2, d.shape[2]), d.dtype) for d in ds],
                 [_dma_sems(n), _dma_sems(n)], issue, drain)


def _pair_add_call(d, r1, c_arr, *, name, tm=512):
    _, rows, cols = d.shape
    hr = rows // 2
    tm = tm if hr % tm == 0 else hr
    nb = hr // tm

    def body(c_ref, d_ref, r_ref, o_ref):
        o_ref[...] = (d_ref[...].astype(F32) + r_ref[...].astype(F32)).astype(BF16)

    return pl.pallas_call(
        body,
        grid_spec=pltpu.PrefetchScalarGridSpec(
            num_scalar_prefetch=1, grid=(N_CHIP, nb),
            in_specs=[pl.BlockSpec((None, tm, cols), lambda j, i, c: (j, c[0] * nb + i, 0)),
                      pl.BlockSpec((None, tm, cols), lambda j, i, c: (j, i, 0))],
            out_specs=pl.BlockSpec((None, tm, cols), lambda j, i, c: (j, i, 0))),
        out_shape=jax.ShapeDtypeStruct((N_CHIP, hr, cols), BF16), name=name, compiler_params=_cparams(),
    )(c_arr, d, r1)


def _rs_chip_task(ps, relations=(0, 1, 2), slots=None):
    n = len(ps)

    def copies(ins, outs, sems):
        send_sems, recv_sems = sems
        x, y, c, others = _place()
        for k in range(n):
            for t, (ox, oy) in enumerate(others):
                if t in relations:
                    yield pltpu.make_async_remote_copy(
                        src_ref=ins[k].at[2 * ox + oy], dst_ref=outs[k].at[t],
                        send_sem=send_sems.at[k, t], recv_sem=recv_sems.at[k, t],
                        device_id=(ox, oy, c), device_id_type=MESH)

    def issue(ins, outs, sems):
        for cp in copies(ins, outs, sems):
            cp.start()

    def drain(ins, outs, sems):
        for cp in copies(ins, outs, sems):
            cp.wait()

    return _Task(list(ps) + list(slots or []), [jax.ShapeDtypeStruct((3,) + p.shape[1:], p.dtype) for p in ps],
                 [_dma_sems(n, 3), _dma_sems(n, 3)], issue, drain,
                 aliases={n + k: k for k in range(n)} if slots else None)


def _chip_sum_call(p, r2, place_arr, *, name, tm=512):
    _, hr, cols = r2.shape
    tm = tm if hr % tm == 0 else hr
    nb = hr // tm

    def body(place_ref, p_ref, r_ref, o_ref):
        acc = p_ref[...].astype(F32)
        for j in range(3):
            acc = acc + r_ref[j].astype(F32)
        o_ref[...] = acc

    return pl.pallas_call(
        body,
        grid_spec=pltpu.PrefetchScalarGridSpec(
            num_scalar_prefetch=1, grid=(nb,),
            in_specs=[pl.BlockSpec((None, tm, cols), lambda i, pc: (pc[0], i, 0)),
                      pl.BlockSpec((3, tm, cols), lambda i, pc: (0, i, 0))],
            out_specs=pl.BlockSpec((tm, cols), lambda i, pc: (pc[1] * nb + i, 0))),
        out_shape=jax.ShapeDtypeStruct((2 * hr, cols), F32), name=name, compiler_params=_cparams(),
    )(place_arr, p, r2)


def _rs_gather_task(gs):
    n = len(gs)

    def copies(ins, outs, sems):
        send_sems, recv_sems = sems
        x, y, c, _ = _place()
        for k in range(n):
            rows = gs[k].shape[0]
            mine, theirs = _half(c, rows), _half(1 - c, rows)
            yield [pltpu.make_async_remote_copy(
                src_ref=ins[k].at[mine], dst_ref=outs[k].at[half],
                send_sem=send_sems.at[k], recv_sem=recv_sems.at[k],
                device_id=(x, y, 1 - c), device_id_type=MESH) for half in (mine, theirs)]

    def issue(ins, outs, sems):
        for outgoing, _ in copies(ins, outs, sems):
            outgoing.start()

    def drain(ins, outs, sems):
        for outgoing, incoming in copies(ins, outs, sems):
            incoming.wait_recv()
            outgoing.wait_send()

    return _Task(gs, [jax.ShapeDtypeStruct(g.shape, g.dtype) for g in gs],
                 [_dma_sems(n), _dma_sems(n)], issue, drain, aliases={k: k for k in range(n)})


def _adamw(w, g, m, v):
    m = ADAM_B1 * m + (1.0 - ADAM_B1) * g
    v = ADAM_B2 * v + (1.0 - ADAM_B2) * jnp.square(g)
    m_hat = m / (1.0 - ADAM_B1 ** ADAM_STEP)
    v_hat = v / (1.0 - ADAM_B2 ** ADAM_STEP)
    delta = -ADAM_LR * (m_hat / (jnp.sqrt(v_hat) + ADAM_EPS) + ADAM_WD * w)
    return delta, m, v


def _adamw_call(items, *, name, tm=256, task=None):
    n = len(items)
    cols = items[0][0].shape[1]
    tiles = [it[0].shape[0] // tm for it in items]
    steps = max(tiles)

    def body(*refs):
        i = pl.program_id(0)
        ins, outs = refs[:4 * n], refs[4 * n:]
        for k in range(n):
            def update(k=k):
                g = ins[4 * k + 1][...]
                res = _adamw(ins[4 * k][...], g, ins[4 * k + 2][...], ins[4 * k + 3][...])
                outs[4 * k][...] = g
                for j in range(3):
                    outs[4 * k + 1 + j][...] = res[j]
            if tiles[k] == steps:
                update()
            else:
                pl.when(i < tiles[k])(update)

    in_specs, out_specs, out_shape, args = [], [], [], []
    for it, t in zip(items, tiles):
        spec = pl.BlockSpec((tm, cols), lambda i, t=t: (jnp.minimum(i, t - 1), 0))
        in_specs += [spec] * 4
        out_specs += [spec] * 4
        out_shape += [jax.ShapeDtypeStruct(it[0].shape, F32)] * 4
        args += list(it)
    res = _pallas(body, grid=(steps,), in_specs=in_specs, out_specs=out_specs, out_shape=out_shape,
                  name=name, task=task)(*args)
    outs, extra = res if task is not None else (res, None)
    grouped = [tuple(outs[4 * k:4 * k + 4]) for k in range(n)]
    return (grouped, extra) if task is not None else grouped


def _adamw_cols_call(w, g_pad, m, v, *, name, tn=256):
    rows, cols = w.shape

    def body(w_ref, g_ref, m_ref, v_ref, go_ref, d_ref, mo_ref, vo_ref):
        g = g_ref[0:rows, :]
        d, mn, vn = _adamw(w_ref[...], g, m_ref[...], v_ref[...])
        go_ref[...] = g
        d_ref[...] = d
        mo_ref[...] = mn
        vo_ref[...] = vn

    spec = pl.BlockSpec((rows, tn), lambda j: (0, j))
    gspec = pl.BlockSpec((g_pad.shape[0], tn), lambda j: (0, j))
    return _pallas(body, grid=(cols // tn,), in_specs=[spec, gspec, spec, spec], out_specs=(spec,) * 4,
                   out_shape=(jax.ShapeDtypeStruct((rows, cols), F32),) * 4, name=name)(w, g_pad, m, v)


N_DEV = 8
SMALL_ROWS = 24
SMALL_LAYOUT = {
    "g_mix_pre": (0, 0, 1, D), "g_mix_post": (1, 0, 1, D), "g_mem_kv": (2, 0, 1, D), "g_mem_pre": (3, 0, 1, D),
    "g_mem_post": (4, 0, 1, D), "g_ff_pre": (5, 0, 1, D), "g_ff_post": (6, 0, 1, D),
    "g_fox_out": (7, 0, 1, D_GRP), "g_chk_out": (7, D_GRP, 1, D_GRP), "b_fgt": (8, 0, 1, 8),
    "rel_bias": (16, 0, 8, N_REL),
}
SMALL = list(SMALL_LAYOUT)


LOSS_ROW = 9


def _small_call(grads, ws, ms, vs, loss_blk, task, *, name):
    n = len(SMALL)
    t_in, t_out = len(task.arrays), len(task.out_shapes)

    def body(*refs):
        g_refs, w_refs, m_refs, v_refs = (refs[j * n:(j + 1) * n] for j in range(4))
        p = 4 * n
        loss_ref, tins = refs[p], refs[p + 1:p + 1 + t_in]
        p += 1 + t_in
        outs, loss_out, touts = refs[p:p + 4 * n], refs[p + 4 * n], refs[p + 4 * n + 1:p + 4 * n + 1 + t_out]
        p += 4 * n + 1 + t_out
        mine, slots, send_sems, recv_sems = refs[p:p + 4]
        tsems = refs[p + 4:]
        task.issue(tins, touts, tsems)
        x, y, c, _ = _place()
        me = 4 * x + 2 * y + c
        mine[...] = jnp.zeros_like(mine)
        for k, name_k in enumerate(SMALL):
            r, l, nr, nl = SMALL_LAYOUT[name_k]
            mine[r:r + nr, l:l + nl] = g_refs[k][0:nr, 0:nl]
        mine[LOSS_ROW:LOSS_ROW + 1, 0:128] = loss_ref[0:1, :]
        slots[me] = mine[...]
        peers = [(dx, dy, dc) for dx in (0, 1) for dy in (0, 1) for dc in (0, 1)][1:]
        cps = []
        for t, (dx, dy, dc) in enumerate(peers):
            px, py, pc = (x + dx) % 2, (y + dy) % 2, (c + dc) % 2
            cps.append(pltpu.make_async_remote_copy(
                src_ref=mine, dst_ref=slots.at[me], send_sem=send_sems.at[t], recv_sem=recv_sems.at[t],
                device_id=(px, py, pc), device_id_type=MESH))
            cps[-1].start()
        for t, (dx, dy, dc) in enumerate(peers):
            px, py, pc = (x + dx) % 2, (y + dy) % 2, (c + dc) % 2
            pltpu.make_async_remote_copy(
                src_ref=mine, dst_ref=slots.at[4 * px + 2 * py + pc], send_sem=send_sems.at[t],
                recv_sem=recv_sems.at[t], device_id=(px, py, pc), device_id_type=MESH).wait_recv()
        for cp in cps:
            cp.wait_send()
        total = slots[0]
        for j in range(1, N_DEV):
            total = total + slots[j]
        for k, name_k in enumerate(SMALL):
            r, l, nr, nl = SMALL_LAYOUT[name_k]
            g = total[r:r + nr, l:l + nl]
            d, mn, vn = _adamw(w_refs[k][...], g, m_refs[k][...], v_refs[k][...])
            for j, val in enumerate((g, d, mn, vn)):
                outs[4 * k + j][...] = val
        loss_out[...] = jnp.broadcast_to(total[LOSS_ROW:LOSS_ROW + 1, 0:128], loss_out.shape)
        task.drain(tins, touts, tsems)

    vm = pl.BlockSpec(memory_space=pltpu.VMEM)
    out_shape = [jax.ShapeDtypeStruct(ws[k].shape, F32) for k in SMALL for _ in range(4)]
    out_shape += [jax.ShapeDtypeStruct((8, 128), F32)] + list(task.out_shapes)
    res = pl.pallas_call(
        body, in_specs=[vm] * (4 * n + 1) + [ANY] * t_in, out_specs=[vm] * (4 * n + 1) + [ANY] * t_out,
        out_shape=out_shape,
        scratch_shapes=[pltpu.VMEM((SMALL_ROWS, D), F32), pltpu.VMEM((N_DEV, SMALL_ROWS, D), F32),
                        _dma_sems(N_DEV - 1), _dma_sems(N_DEV - 1)] + list(task.sems),
        input_output_aliases={4 * n + 1 + i: 4 * n + 1 + j for i, j in task.aliases.items()},
        name=name,
    )(*[d[k] for d in (grads, ws, ms, vs) for k in SMALL], loss_blk, *task.arrays)
    return ({k: tuple(res[4 * i:4 * i + 4]) for i, k in enumerate(SMALL)}, res[4 * n], list(res[4 * n + 1:]))


WEIGHTS = ["w_in", "b_fgt", "rel_bias", "g_fox_out", "g_chk_out", "w_out", "g_mix_pre", "g_mix_post", "g_mem_kv",
           "w_mq", "w_mk", "w_mv", "w_mo", "g_mem_pre", "g_mem_post", "w_ff1", "w_ff2", "g_ff_pre", "g_ff_post"]
BIG = ["w_in", "w_out", "w_mq", "w_mk", "w_mv", "w_mo", "w_ff1", "w_ff2"]


IN_SHARD = D_IN // N_CHIP
IN_PAD = 800


IN_PIECES = [(0, 0, 770), (800, 770, 766), (1566, 3072, 4), (1600, 3076, 4), (1604, 1536, 766), (2400, 2302, 770)]
PAD_ZEROS = [(800 * j + IN_SHARD, IN_PAD - IN_SHARD) for j in range(N_CHIP)]
ALL_ZEROS = [(D_IN, D_ALL - D_IN)]


def _reorder_rows_call(src, to_all, *, name, tn=256):
    rows, cols = src.shape
    zeros = ALL_ZEROS if to_all else PAD_ZEROS

    def body(s_ref, o_ref):
        for pad0, all0, cnt in IN_PIECES:
            s0, d0 = (pad0, all0) if to_all else (all0, pad0)
            o_ref[d0:d0 + cnt, :] = s_ref[s0:s0 + cnt, :]
        for z0, cnt in zeros:
            o_ref[z0:z0 + cnt, :] = jnp.zeros((cnt, tn), src.dtype)

    spec = pl.BlockSpec((rows, tn), lambda j: (0, j))
    return _pallas(body, grid=(cols // tn,), in_specs=[spec], out_specs=spec,
                   out_shape=jax.ShapeDtypeStruct((rows, cols), src.dtype), name=name)(src)


def kernel(x, mem, w_in, b_fgt, rel_bias, g_fox_out, g_chk_out, w_out, g_mix_pre, g_mix_post, g_mem_kv, w_mq, w_mk, w_mv, w_mo, g_mem_pre, g_mem_post, w_ff1, w_ff2, g_ff_pre, g_ff_post, loss_target, m_w_in, m_b_fgt, m_rel_bias, m_g_fox_out, m_g_chk_out, m_w_out, m_g_mix_pre, m_g_mix_post, m_g_mem_kv, m_w_mq, m_w_mk, m_w_mv, m_w_mo, m_g_mem_pre, m_g_mem_post, m_w_ff1, m_w_ff2, m_g_ff_pre, m_g_ff_post, v_w_in, v_b_fgt, v_rel_bias, v_g_fox_out, v_g_chk_out, v_w_out, v_g_mix_pre, v_g_mix_post, v_g_mem_kv, v_w_mq, v_w_mk, v_w_mv, v_w_mo, v_g_mem_pre, v_g_mem_post, v_w_ff1, v_w_ff2, v_g_ff_pre, v_g_ff_post):
    w = dict(w_in=w_in, b_fgt=b_fgt, rel_bias=rel_bias, g_fox_out=g_fox_out, g_chk_out=g_chk_out, w_out=w_out,
             g_mix_pre=g_mix_pre, g_mix_post=g_mix_post, g_mem_kv=g_mem_kv, w_mq=w_mq, w_mk=w_mk, w_mv=w_mv,
             w_mo=w_mo, g_mem_pre=g_mem_pre, g_mem_post=g_mem_post, w_ff1=w_ff1, w_ff2=w_ff2, g_ff_pre=g_ff_pre,
             g_ff_post=g_ff_post)
    m = dict(w_in=m_w_in, b_fgt=m_b_fgt, rel_bias=m_rel_bias, g_fox_out=m_g_fox_out, g_chk_out=m_g_chk_out,
             w_out=m_w_out, g_mix_pre=m_g_mix_pre, g_mix_post=m_g_mix_post, g_mem_kv=m_g_mem_kv, w_mq=m_w_mq,
             w_mk=m_w_mk, w_mv=m_w_mv, w_mo=m_w_mo, g_mem_pre=m_g_mem_pre, g_mem_post=m_g_mem_post,
             w_ff1=m_w_ff1, w_ff2=m_w_ff2, g_ff_pre=m_g_ff_pre, g_ff_post=m_g_ff_post)
    v = dict(w_in=v_w_in, b_fgt=v_b_fgt, rel_bias=v_rel_bias, g_fox_out=v_g_fox_out, g_chk_out=v_g_chk_out,
             w_out=v_w_out, g_mix_pre=v_g_mix_pre, g_mix_post=v_g_mix_post, g_mem_kv=v_g_mem_kv, w_mq=v_w_mq,
             w_mk=v_w_mk, w_mv=v_w_mv, w_mo=v_w_mo, g_mem_pre=v_g_mem_pre, g_mem_post=v_g_mem_post,
             w_ff1=v_w_ff1, w_ff2=v_w_ff2, g_ff_pre=v_g_ff_pre, g_ff_post=v_g_ff_post)

    def rows(d, k):
        return d[k][0] if k == "rel_bias" else d[k]

    xs, mems, target = x[0], mem[0], loss_target[0]
    S = xs.shape[0]
    sp = {k: rows(w, k) for k in SMALL}
    b_pad = jnp.pad(sp["b_fgt"], ((0, 0), (0, 120)))
    chip = 2 * lax.axis_index("x") + lax.axis_index("y")
    chip_arr = jnp.reshape(chip, (1,)).astype(jnp.int32)
    c_arr = jnp.reshape(lax.axis_index("c"), (1,)).astype(jnp.int32)
    place_arr = jnp.concatenate([chip_arr, c_arr])
    w_in_t, m_in_t, v_in_t = w["w_in"][0].T, m["w_in"][0].T, v["w_in"][0].T
    slab = {"w_in": _cast_slab_call(w_in_t, chip_arr, name="cast_w_in", pad_rows=IN_PAD - IN_SHARD)}

    def gather_ici(names):
        return _ag_ici_task([slab[k] for k in names])

    def pair_add(k, d, r1):
        return _pair_add_call(d, r1, c_arr, name="rs_pair_add_" + k)

    rest, (g_in,) = _cast_slabs_call([w[k][0] for k in BIG[1:]], chip_arr, name="cast_rest",
                                     task=gather_ici(["w_in"]))
    slab.update(zip(BIG[1:], rest))
    h1, (g_in,) = _rms_fwd_call(xs, sp["g_mix_pre"], name="rms_mix_pre", task=_ag_d2d_task([g_in]))
    w_all_t = _reorder_rows_call(g_in.reshape(N_CHIP * IN_PAD, D), True, name="w_in_rows")
    proj, (g_out, g_mq) = _mm_nt(h1, w_all_t, "plain", rows=(0, 3072), name="mm_proj",
                                 task=gather_ici(["w_out", "w_mq"]))
    fl_raw = _mm_nt(h1, w_all_t, "plain", rows=(3072, 128), name="mm_gate", out_dtype=F32, tn=128)
    c_rep, c_t = _fox_prep_call(fl_raw, b_pad, name="fox_prep")
    bias = _chk_bias_call(_rel_table_to_g(sp["rel_bias"]), name="chk_bias")
    mid = ["w_mk", "w_mv", "w_mo", "w_ff1"]
    (yf, lse), got = _fox_fwd_call(proj, c_rep, c_t, name="fox_fwd",
                                   task=_merge_tasks([gather_ici(mid), _ag_d2d_task([g_out, g_mq])]))
    g_mid, (g_out, g_mq) = got[:4], got[4:]
    yc, got = _chk_fwd_call(proj, bias, name="chk_fwd",
                            task=_merge_tasks([gather_ici(["w_ff2"]), _ag_d2d_task(g_mid)]))
    g_ff2, (g_mk, g_mv, g_mo, g_ff1) = got[0], got[1:]
    yn = _mix_norm_fwd_call(yf, yc, sp["g_fox_out"], sp["g_chk_out"], name="mix_norm_fwd")
    z, (g_ff2,) = _mm_nn(yn, g_out, "rows", name="mm_out", out_dtype=F32, task=_ag_d2d_task([g_ff2]))
    x1, h2 = _post_pre_call(xs, z, sp["g_mix_post"], sp["g_mem_pre"], name="post_mix")
    memn = _rms_fwd_call(mems, sp["g_mem_kv"], name="rms_mem_kv")
    q2 = _mm_nn(h2, g_mq, "rows", name="mm_mq")
    k2 = _mm_nn(memn, g_mk, "rows", name="mm_mk")
    v2 = _mm_nn(memn, g_mv, "rows", name="mm_mv")
    o2 = _mem_fwd_call(q2, k2, v2, name="mem_fwd")
    y2 = _mm_nn(o2, g_mo, "rows", name="mm_mo", out_dtype=F32)
    x2, h3 = _post_pre_call(x1, y2, sp["g_mem_post"], sp["g_ff_pre"], name="post_mem")
    act, relu = _mm_nn(h3, g_ff1, "cols", name="mm_ff1", epi="relu2")
    y3 = _mm_nn(act, g_ff2, "rows", name="mm_ff2", out_dtype=F32, tm=1024)
    loss_blk, dx3, dy3, dg_ff_post = _final_call(x2, y3, sp["g_ff_post"], target, name="final")

    d_ff2 = _mm_tn(act, dy3, name="mm_dff2", tk=512, tn=1024).reshape(N_CHIP, D_FF // N_CHIP, D)
    du, (r1,) = _mm_nt(dy3, g_ff2, "rows", name="mm_du", mul2r=relu, task=_rs_pair_task([d_ff2]))
    p_ff2 = pair_add("w_ff2", d_ff2, r1)
    d_ff1 = _mm_tn(h3, du, name="mm_dff1", cols4=True)
    dh3, (r1,) = _mm_nt(du, g_ff1, "cols", name="mm_dh3", out_dtype=F32, tm=1024, task=_rs_pair_task([d_ff1]))
    p_ff1 = pair_add("w_ff1", d_ff1, r1)
    dx2, dy2, dg_ff_pre, dg_mem_post = _bwd_mid_call(dx3, x2, dh3, sp["g_ff_pre"], y2, sp["g_mem_post"], name="bwd_ff")
    d_mo = _mm_tn(o2, dy2, name="mm_dmo").reshape(N_CHIP, D // N_CHIP, D)
    do2 = _mm_nt(dy2, g_mo, "rows", name="mm_do2")
    dq2, dk2, dv2 = _mem_bwd_call(q2, k2, v2, do2, name="mem_bwd")
    d_mq = _mm_tn(h2, dq2, name="mm_dmq").reshape(N_CHIP, D // N_CHIP, D)
    dh2 = _mm_nt(dq2, g_mq, "rows", name="mm_dh2", out_dtype=F32)
    d_mk = _mm_tn(memn, dk2, name="mm_dmk").reshape(N_CHIP, D // N_CHIP, D)
    d_mv = _mm_tn(memn, dv2, name="mm_dmv").reshape(N_CHIP, D // N_CHIP, D)
    dmn_k = _mm_nt(dk2, g_mk, "rows", name="mm_dmemk", out_dtype=F32)
    dmn_v = _mm_nt(dv2, g_mv, "rows", name="mm_dmemv", out_dtype=F32)
    dg_mem_kv = _gain_grad_call(mems, sp["g_mem_kv"], dmn_k, dmn_v, name="gain_mem_kv")
    dx1, dz, dg_mem_pre, dg_mix_post = _bwd_mid_call(dx2, x1, dh2, sp["g_mem_pre"], z, sp["g_mix_post"], name="bwd_mem")
    d_out = _mm_tn(yn, dz, name="mm_dout").reshape(N_CHIP, D // N_CHIP, D)
    late = ["w_mo", "w_mq", "w_mk", "w_mv", "w_out"]
    d_late = [d_mo, d_mq, d_mk, d_mv, d_out]
    dyn, r1_late = _mm_nt(dz, g_out, "rows", name="mm_dyn", out_dtype=F32, task=_rs_pair_task(d_late))
    p_late = [pair_add(k, d, r1) for k, d, r1 in zip(late, d_late, r1_late)]
    dof, doc, delta, dg_fox, dg_chk = _mix_norm_bwd_call(dyn, yf, yc, sp["g_fox_out"], sp["g_chk_out"], name="mix_norm_bwd")
    (dqf, dkf, dvf, dcq, dck), r2_ff = _fox_bwd_call(proj, dof, lse, delta, c_rep, c_t, name="fox_bwd",
                                                      task=_rs_chip_task([p_ff2, p_ff1]))
    (dqc, dkc, dvc, dgrev), r2_late = _chk_bwd_call(proj, doc, bias, name="chk_bwd", task=_rs_chip_task(p_late))
    first = ["w_ff2", "w_ff1"] + late
    f_first = [_chip_sum_call(p, r, place_arr, name="rs_chip_sum_" + k)
               for k, p, r in zip(first, [p_ff2, p_ff1] + p_late, r2_ff + r2_late)]
    dc8 = dcq[:, :, 0:2, :].transpose(0, 2, 1, 3).reshape(8, S) + dck[:, ::HEAD].T
    dc_rows = jnp.concatenate([dc8, jnp.zeros((120, S), F32)], axis=0)
    dfl, db_fgt = _fox_gate_bwd_call(dc_rows, fl_raw, b_pad, name="fox_gate_bwd")
    dproj = jnp.concatenate([dqf, dkf, dvf, dqc, dkc, dvc, dfl], axis=1)
    d_all_t, g_first = _mm_tn(dproj, h1, name="mm_dwin", tk=640, tn=1024, task=_rs_gather_task(f_first))
    grads = dict(zip(first, g_first))
    d_in = _reorder_rows_call(d_all_t, False, name="d_in_rows").reshape(N_CHIP, IN_PAD, D)
    delta_w, new_m, new_v = {}, {}, {}

    def adamw_items(names):
        return [(w[k][0], grads[k], m[k][0], v[k][0]) for k in names]

    upd_late, (r1,) = _adamw_call(adamw_items(late), name="adamw_late", tm=64, task=_rs_pair_task([d_in]))
    p_in = pair_add("w_in", d_in, r1)
    dh1, r2_in = _mm_nn(dproj, w_all_t, "plain", name="mm_dh1", out_dtype=F32, tm=1024,
                        task=_rs_chip_task([p_in], relations=(0, 1)))
    (grad_x, dg_mix_pre), (r2_in,) = _bwd_last_call(dx1, xs, dh1, sp["g_mix_pre"], name="bwd_mix",
                                                    task=_rs_chip_task([p_in], relations=(2,), slots=r2_in))
    f_in = _chip_sum_call(p_in, r2_in, place_arr, name="rs_chip_sum_w_in")
    upd_ff = _adamw_call(adamw_items(first[:2]), name="adamw_ff")
    for k, res in zip(late + first[:2], upd_late + upd_ff):
        grads[k], delta_w[k], new_m[k], new_v[k] = res

    small_g = {"g_mix_pre": dg_mix_pre, "g_mix_post": dg_mix_post, "g_mem_kv": dg_mem_kv, "g_mem_pre": dg_mem_pre,
               "g_mem_post": dg_mem_post, "g_ff_pre": dg_ff_pre, "g_ff_post": dg_ff_post, "g_fox_out": dg_fox,
               "g_chk_out": dg_chk, "b_fgt": db_fgt,
               "rel_bias": _g_to_rel_table(dgrev[:, 0:2, :].reshape(8, ROLL_W))}
    small, loss_out, (g_w_in,) = _small_call(
        small_g, sp, {k: rows(m, k) for k in SMALL}, {k: rows(v, k) for k in SMALL}, loss_blk,
        _rs_gather_task([f_in]), name="small_allreduce_adamw")
    loss = loss_out[0, 0]
    res = _adamw_cols_call(w_in_t, g_w_in, m_in_t, v_in_t, name="adamw_w_in")
    grads["w_in"], delta_w["w_in"], new_m["w_in"], new_v["w_in"] = (a.T for a in res)
    for k in SMALL:
        vals = small[k]
        if k == "rel_bias":
            vals = tuple(a[None] for a in vals)
        grads[k], delta_w[k], new_m[k], new_v[k] = vals

    def out(d, k):
        return d[k][None] if k in BIG else d[k]

    return (loss, grad_x[None], *[out(grads, k) for k in WEIGHTS], *[out(delta_w, k) for k in WEIGHTS],
            *[out(new_m, k) for k in WEIGHTS], *[out(new_v, k) for k in WEIGHTS])
```

```python
import functools

import jax
import jax.numpy as jnp
from jax import lax
from jax.experimental import pallas as pl
from jax.experimental.pallas import tpu as pltpu

F32 = jnp.float32
BF16 = jnp.bfloat16

D = 1024
HEAD = 64
N_PAIR = 4
D_GRP = 512
CHUNK = 64
LEFT = 8
MAX_REL = 128
N_REL = 2 * MAX_REL + 1
N_MEM = 256
MEM_HEADS = 4
MEM_HD = 256
D_FF = 4096
D_IN = 3080
D_ALL = 3200
EPS = 1e-6
TQ = 256
WIN = (LEFT + TQ // CHUNK) * CHUNK
PADK = LEFT * CHUNK
ROLL_W = 1024
NEG = -1e30
N_CHIP = 4
VMEM_LIMIT = 48 * 1024 * 1024

ADAM_LR = 0.001
ADAM_B1 = 0.9
ADAM_B2 = 0.999
ADAM_EPS = 1e-08
ADAM_WD = 0.01
ADAM_STEP = 10

MESH = pl.DeviceIdType.MESH


def _cparams():
    return pltpu.CompilerParams(vmem_limit_bytes=VMEM_LIMIT)


ANY = pl.BlockSpec(memory_space=pl.ANY)


class _Task:
    def __init__(self, arrays, out_shapes, sems, issue, drain, aliases=None):
        self.arrays, self.out_shapes, self.sems = list(arrays), list(out_shapes), list(sems)
        self.issue, self.drain, self.aliases = issue, drain, dict(aliases or {})


def _merge_tasks(tasks):
    tasks = [t for t in tasks if t is not None]
    if len(tasks) == 1:
        return tasks[0]
    cuts, a, o, s = [], 0, 0, 0
    aliases = {}
    for t in tasks:
        cuts.append((a, o, s))
        aliases.update({a + i: o + j for i, j in t.aliases.items()})
        a, o, s = a + len(t.arrays), o + len(t.out_shapes), s + len(t.sems)

    def part(fn_name):
        def run(ins, outs, sems):
            for t, (a0, o0, s0) in zip(tasks, cuts):
                getattr(t, fn_name)(ins[a0:a0 + len(t.arrays)], outs[o0:o0 + len(t.out_shapes)],
                                    sems[s0:s0 + len(t.sems)])
        return run

    return _Task([x for t in tasks for x in t.arrays], [x for t in tasks for x in t.out_shapes],
                 [x for t in tasks for x in t.sems], part("issue"), part("drain"), aliases)


def _pallas(body, *, grid, in_specs, out_specs, out_shape, name, scratch_shapes=(), task=None, prefetch=0):
    def make(kernel, i_specs, o_specs, o_shape, scratch, aliases):
        if prefetch:
            spec = pltpu.PrefetchScalarGridSpec(num_scalar_prefetch=prefetch, grid=grid, in_specs=i_specs,
                                                out_specs=o_specs, scratch_shapes=scratch)
            return pl.pallas_call(kernel, grid_spec=spec, out_shape=o_shape, input_output_aliases=aliases,
                                  name=name, compiler_params=_cparams())
        return pl.pallas_call(kernel, grid=grid, in_specs=i_specs, out_specs=o_specs, out_shape=o_shape,
                              scratch_shapes=scratch, input_output_aliases=aliases, name=name,
                              compiler_params=_cparams())

    if task is None:
        return make(body, list(in_specs), out_specs, out_shape, list(scratch_shapes), {})
    single = not isinstance(out_shape, (tuple, list))
    o_shapes = [out_shape] if single else list(out_shape)
    o_specs = [out_specs] if single else list(out_specs)
    n_in, n_out, n_scr = len(in_specs), len(o_shapes), len(scratch_shapes)
    t_in, t_out = len(task.arrays), len(task.out_shapes)

    def carried(*refs):
        cut = [prefetch, n_in, t_in, n_out, t_out, n_scr]
        parts, p = [], 0
        for c in cut:
            parts.append(refs[p:p + c])
            p += c
        scalars, ins, tins, outs, touts, scr = parts
        tsems = refs[p:]
        ids = [pl.program_id(a) for a in range(len(grid))]
        first = functools.reduce(jnp.logical_and, [i == 0 for i in ids])
        last = functools.reduce(jnp.logical_and, [i == g - 1 for i, g in zip(ids, grid)])

        @pl.when(first)
        def _():
            task.issue(tins, touts, tsems)
        body(*scalars, *ins, *outs, *scr)

        @pl.when(last)
        def _():
            task.drain(tins, touts, tsems)

    call = make(carried, list(in_specs) + [ANY] * t_in, o_specs + [ANY] * t_out,
                o_shapes + list(task.out_shapes), list(scratch_shapes) + list(task.sems),
                {prefetch + n_in + i: n_out + j for i, j in task.aliases.items()})

    def run(*args):
        res = call(*args, *task.arrays)
        outs = res[:n_out]
        return (outs[0] if single else tuple(outs)), list(res[n_out:])

    return run


def _dot(a, b):
    return jnp.dot(a, b, preferred_element_type=F32)


def _dot_nt(a, b):
    return lax.dot_general(a, b, (((1,), (1,)), ((), ())), preferred_element_type=F32)


def _dot_tn(a, b):
    return lax.dot_general(a, b, (((0,), (0,)), ((), ())), preferred_element_type=F32)


def _split3(x):
    hi = x.astype(BF16)
    r1 = x - hi.astype(F32)
    mid = r1.astype(BF16)
    lo = (r1 - mid.astype(F32)).astype(BF16)
    return hi, mid, lo


def _dot3(x, m01):
    hi, mid, lo = _split3(x)
    return _dot(hi, m01) + _dot(mid, m01) + _dot(lo, m01)


def _dot3_l(m01, x):
    hi, mid, lo = _split3(x)
    return _dot(m01, hi) + _dot(m01, mid) + _dot(m01, lo)


def _mm_nn(a, b, kind, *, name, out_dtype=BF16, tm=2048, tn=512, epi=None, task=None):
    M, K = a.shape
    if kind == "plain":
        N = b.shape[1]
        b_spec = pl.BlockSpec((K, tn), lambda m, n: (0, n))
    elif kind == "rows":
        N = b.shape[2]
        b_spec = pl.BlockSpec((N_CHIP, K // N_CHIP, tn), lambda m, n: (0, 0, n))
    else:
        nq = b.shape[2]
        N = N_CHIP * nq
        per = nq // tn
        b_spec = pl.BlockSpec((None, K, tn), lambda m, n: (n // per, 0, n % per))
    tm = min(tm, M)
    kq = K // N_CHIP

    def body(a_ref, b_ref, *o_refs):
        if kind == "rows":
            acc = _dot(a_ref[:, 0:kq], b_ref[0])
            for j in range(1, N_CHIP):
                acc += _dot(a_ref[:, j * kq:(j + 1) * kq], b_ref[j])
        else:
            acc = _dot(a_ref[...], b_ref[...])
        if epi == "relu2":
            r = jnp.maximum(acc, 0.0)
            o_refs[0][...] = (r * r).astype(BF16)
            o_refs[1][...] = r.astype(BF16)
        else:
            o_refs[0][...] = acc.astype(out_dtype)

    o_spec = pl.BlockSpec((tm, tn), lambda m, n: (m, n))
    if epi == "relu2":
        out_shape = (jax.ShapeDtypeStruct((M, N), BF16), jax.ShapeDtypeStruct((M, N), BF16))
        out_specs = (o_spec, o_spec)
    else:
        out_shape = jax.ShapeDtypeStruct((M, N), out_dtype)
        out_specs = o_spec
    return _pallas(
        body, grid=(M // tm, N // tn),
        in_specs=[pl.BlockSpec((tm, K), lambda m, n: (m, 0)), b_spec],
        out_specs=out_specs, out_shape=out_shape, name=name, task=task,
    )(a, b)


def _mm_nt(a, b, kind, *, name, out_dtype=BF16, tm=2048, tn=512, mul2r=None, task=None, rows=None):
    M, K = a.shape
    if kind == "plain":
        first, N = rows if rows is not None else (0, b.shape[0])
        n0 = first // tn
        b_spec = pl.BlockSpec((tn, K), lambda m, n: (n0 + n, 0))
    elif kind == "rows":
        nq = b.shape[1]
        N = N_CHIP * nq
        tn = min(tn, nq)
        per = nq // tn
        b_spec = pl.BlockSpec((None, tn, K), lambda m, n: (n // per, n % per, 0))
    else:
        N = b.shape[1]
        b_spec = pl.BlockSpec((N_CHIP, tn, K // N_CHIP), lambda m, n: (0, n, 0))
    tm = min(tm, M)
    kq = K // N_CHIP

    def body(a_ref, b_ref, *rest):
        o_ref = rest[-1]
        if kind == "cols":
            acc = _dot_nt(a_ref[:, 0:kq], b_ref[0])
            for j in range(1, N_CHIP):
                acc += _dot_nt(a_ref[:, j * kq:(j + 1) * kq], b_ref[j])
        else:
            acc = _dot_nt(a_ref[...], b_ref[...])
        if mul2r is not None:
            acc = acc * (2.0 * rest[0][...].astype(F32))
        o_ref[...] = acc.astype(out_dtype)

    in_specs = [pl.BlockSpec((tm, K), lambda m, n: (m, 0)), b_spec]
    args = [a, b]
    if mul2r is not None:
        in_specs.append(pl.BlockSpec((tm, tn), lambda m, n: (m, n)))
        args.append(mul2r)
    return _pallas(
        body, grid=(M // tm, N // tn), in_specs=in_specs,
        out_specs=pl.BlockSpec((tm, tn), lambda m, n: (m, n)),
        out_shape=jax.ShapeDtypeStruct((M, N), out_dtype), name=name, task=task,
    )(*args)


def _mm_tn(a, b, *, name, out_dtype=BF16, tk=1024, tn=512, cols4=False, task=None):
    M, K1 = a.shape
    N = b.shape[1]
    tk = min(tk, K1)
    tn = min(tn, N)

    def body(a_ref, b_ref, o_ref):
        o_ref[...] = _dot_tn(a_ref[...], b_ref[...]).astype(out_dtype)

    if cols4:
        per = (N // N_CHIP) // tn
        out_shape = jax.ShapeDtypeStruct((N_CHIP, K1, N // N_CHIP), out_dtype)
        o_spec = pl.BlockSpec((None, tk, tn), lambda k, n: (n // per, k, n % per))
    else:
        out_shape = jax.ShapeDtypeStruct((K1, N), out_dtype)
        o_spec = pl.BlockSpec((tk, tn), lambda k, n: (k, n))
    return _pallas(
        body, grid=(K1 // tk, N // tn),
        in_specs=[pl.BlockSpec((M, tk), lambda k, n: (0, k)), pl.BlockSpec((M, tn), lambda k, n: (0, n))],
        out_specs=o_spec, out_shape=out_shape, name=name, task=task,
    )(a, b)


def _rms(x, g):
    r = lax.rsqrt(jnp.mean(x * x, axis=-1, keepdims=True) + EPS)
    return x * r * g


def _rms_bwd(x, g, dy):
    r = lax.rsqrt(jnp.mean(x * x, axis=-1, keepdims=True) + EPS)
    xh = x * r
    dg = jnp.sum(dy * xh, axis=0, keepdims=True)
    dxh = dy * g
    dx = r * (dxh - xh * jnp.mean(dxh * xh, axis=-1, keepdims=True))
    return dx, dg


def _row_spec(tm, n):
    return pl.BlockSpec((tm, n), lambda i: (i, 0))


def _vec_spec(n):
    return pl.BlockSpec((1, n), lambda i: (0, 0))


def _acc_spec(n):
    return pl.BlockSpec((8, n), lambda i: (0, 0))


def _acc_add(ref, row, i):
    @pl.when(i == 0)
    def _():
        ref[...] = jnp.zeros_like(ref)
    ref[0:1, :] += row


def _rms_fwd_call(x, g, *, name, tm=512, task=None):
    M, n = x.shape
    tm = min(tm, M)

    def body(x_ref, g_ref, h_ref):
        h_ref[...] = _rms(x_ref[...], g_ref[...]).astype(BF16)

    return _pallas(
        body, grid=(M // tm,), in_specs=[_row_spec(tm, n), _vec_spec(n)], out_specs=_row_spec(tm, n),
        out_shape=jax.ShapeDtypeStruct((M, n), BF16), name=name, task=task,
    )(x, g)


def _post_pre_call(xres, z, g_post, g_pre, *, name, tm=512):
    M, n = xres.shape

    def body(x_ref, z_ref, gp_ref, gn_ref, xo_ref, h_ref):
        xn = x_ref[...] + _rms(z_ref[...], gp_ref[...])
        xo_ref[...] = xn
        h_ref[...] = _rms(xn, gn_ref[...]).astype(BF16)

    return pl.pallas_call(
        body, grid=(M // tm,),
        in_specs=[_row_spec(tm, n), _row_spec(tm, n), _vec_spec(n), _vec_spec(n)],
        out_specs=(_row_spec(tm, n), _row_spec(tm, n)),
        out_shape=(jax.ShapeDtypeStruct((M, n), F32), jax.ShapeDtypeStruct((M, n), BF16)),
        name=name, compiler_params=_cparams(),
    )(xres, z, g_post, g_pre)


def _final_call(x2, y3, g_post, target, *, name, tm=512):
    M, n = x2.shape

    def body(x_ref, y_ref, g_ref, t_ref, loss_ref, dx_ref, dy_ref, dg_ref):
        i = pl.program_id(0)
        y = y_ref[...]
        g = g_ref[...]
        diff = x_ref[...] + _rms(y, g) - t_ref[...]
        part = 0.5 * jnp.sum(jnp.sum(diff * diff, axis=1, keepdims=True), axis=0, keepdims=True) / n

        @pl.when(i == 0)
        def _():
            loss_ref[...] = jnp.zeros_like(loss_ref)
        loss_ref[...] += jnp.broadcast_to(part, loss_ref.shape)
        dx = diff / n
        dx_ref[...] = dx
        dy, dg = _rms_bwd(y, g, dx)
        dy_ref[...] = dy.astype(BF16)
        _acc_add(dg_ref, dg, i)

    return pl.pallas_call(
        body, grid=(M // tm,),
        in_specs=[_row_spec(tm, n), _row_spec(tm, n), _vec_spec(n), _row_spec(tm, n)],
        out_specs=(pl.BlockSpec((8, 128), lambda i: (0, 0)), _row_spec(tm, n), _row_spec(tm, n), _acc_spec(n)),
        out_shape=(jax.ShapeDtypeStruct((8, 128), F32), jax.ShapeDtypeStruct((M, n), F32),
                   jax.ShapeDtypeStruct((M, n), BF16), jax.ShapeDtypeStruct((8, n), F32)),
        name=name, compiler_params=_cparams(),
    )(x2, y3, g_post, target)


def _bwd_mid_call(dx_in, x, dh, g_pre, y, g_post, *, name, tm=512):
    M, n = x.shape

    def body(dxi_ref, x_ref, dh_ref, gpre_ref, y_ref, gpost_ref, dx_ref, dy_ref, dgpre_ref, dgpost_ref):
        i = pl.program_id(0)
        d1, dg1 = _rms_bwd(x_ref[...], gpre_ref[...], dh_ref[...])
        dx = dxi_ref[...] + d1
        dx_ref[...] = dx
        dy, dg2 = _rms_bwd(y_ref[...], gpost_ref[...], dx)
        dy_ref[...] = dy.astype(BF16)
        _acc_add(dgpre_ref, dg1, i)
        _acc_add(dgpost_ref, dg2, i)

    return pl.pallas_call(
        body, grid=(M // tm,),
        in_specs=[_row_spec(tm, n), _row_spec(tm, n), _row_spec(tm, n), _vec_spec(n), _row_spec(tm, n), _vec_spec(n)],
        out_specs=(_row_spec(tm, n), _row_spec(tm, n), _acc_spec(n), _acc_spec(n)),
        out_shape=(jax.ShapeDtypeStruct((M, n), F32), jax.ShapeDtypeStruct((M, n), BF16),
                   jax.ShapeDtypeStruct((8, n), F32), jax.ShapeDtypeStruct((8, n), F32)),
        name=name, compiler_params=_cparams(),
    )(dx_in, x, dh, g_pre, y, g_post)


def _bwd_last_call(dx_in, x, dh, g_pre, *, name, tm=512, task=None):
    M, n = x.shape

    def body(dxi_ref, x_ref, dh_ref, g_ref, dx_ref, dg_ref):
        i = pl.program_id(0)
        d1, dg1 = _rms_bwd(x_ref[...], g_ref[...], dh_ref[...])
        dx_ref[...] = dxi_ref[...] + d1
        _acc_add(dg_ref, dg1, i)

    return _pallas(
        body, grid=(M // tm,),
        in_specs=[_row_spec(tm, n), _row_spec(tm, n), _row_spec(tm, n), _vec_spec(n)],
        out_specs=(_row_spec(tm, n), _acc_spec(n)),
        out_shape=(jax.ShapeDtypeStruct((M, n), F32), jax.ShapeDtypeStruct((8, n), F32)),
        name=name, task=task,
    )(dx_in, x, dh, g_pre)


def _gain_grad_call(x, g, dy_a, dy_b, *, name):
    M, n = x.shape

    def body(x_ref, g_ref, a_ref, b_ref, dg_ref):
        _, dg = _rms_bwd(x_ref[...], g_ref[...], a_ref[...] + b_ref[...])
        dg_ref[...] = jnp.zeros_like(dg_ref)
        dg_ref[0:1, :] = dg

    return pl.pallas_call(
        body, grid=(1,),
        in_specs=[_row_spec(M, n), _vec_spec(n), _row_spec(M, n), _row_spec(M, n)],
        out_specs=_acc_spec(n), out_shape=jax.ShapeDtypeStruct((8, n), F32),
        name=name, compiler_params=_cparams(),
    )(x, g, dy_a, dy_b)


def _head_group_matrix():
    a = lax.broadcasted_iota(jnp.int32, (D_GRP, D_GRP), 0) // HEAD
    b = lax.broadcasted_iota(jnp.int32, (D_GRP, D_GRP), 1) // HEAD
    return jnp.where(a == b, 1.0, 0.0).astype(BF16)


def _mix_norm_fwd_call(yf, yc, gf, gc, *, name, tm=512):
    M = yf.shape[0]

    def body(yf_ref, yc_ref, gf_ref, gc_ref, o_ref):
        o_ref[:, 0:D_GRP] = _rms(yf_ref[...], gf_ref[...]).astype(BF16)
        o_ref[:, D_GRP:D] = _rms(yc_ref[...], gc_ref[...]).astype(BF16)

    return pl.pallas_call(
        body, grid=(M // tm,),
        in_specs=[_row_spec(tm, D_GRP), _row_spec(tm, D_GRP), _vec_spec(D_GRP), _vec_spec(D_GRP)],
        out_specs=_row_spec(tm, D), out_shape=jax.ShapeDtypeStruct((M, D), BF16),
        name=name, compiler_params=_cparams(),
    )(yf, yc, gf, gc)


def _mix_norm_bwd_call(dyn, yf, yc, gf, gc, *, name, tm=TQ):
    M = yf.shape[0]

    def body(dyn_ref, yf_ref, yc_ref, gf_ref, gc_ref, dof_ref, doc_ref, delta_ref, dgf_ref, dgc_ref):
        i = pl.program_id(0)
        yf_ = yf_ref[...]
        dof, dgf = _rms_bwd(yf_, gf_ref[...], dyn_ref[:, 0:D_GRP])
        doc, dgc = _rms_bwd(yc_ref[...], gc_ref[...], dyn_ref[:, D_GRP:D])
        dof_b = dof.astype(BF16)
        dof_ref[...] = dof_b
        doc_ref[...] = doc.astype(BF16)
        prod = dof_b.astype(F32) * yf_
        hi = prod.astype(BF16)
        lo = (prod - hi.astype(F32)).astype(BF16)
        grp = _head_group_matrix()
        delta_ref[...] = (_dot(hi, grp) + _dot(lo, grp)).T
        _acc_add(dgf_ref, dgf, i)
        _acc_add(dgc_ref, dgc, i)

    return pl.pallas_call(
        body, grid=(M // tm,),
        in_specs=[_row_spec(tm, D), _row_spec(tm, D_GRP), _row_spec(tm, D_GRP), _vec_spec(D_GRP), _vec_spec(D_GRP)],
        out_specs=(_row_spec(tm, D_GRP), _row_spec(tm, D_GRP), pl.BlockSpec((None, D_GRP, tm), lambda i: (i, 0, 0)),
                   _acc_spec(D_GRP), _acc_spec(D_GRP)),
        out_shape=(jax.ShapeDtypeStruct((M, D_GRP), BF16), jax.ShapeDtypeStruct((M, D_GRP), BF16),
                   jax.ShapeDtypeStruct((M // tm, D_GRP, tm), F32), jax.ShapeDtypeStruct((8, D_GRP), F32),
                   jax.ShapeDtypeStruct((8, D_GRP), F32)),
        name=name, compiler_params=_cparams(),
    )(dyn, yf, yc, gf, gc)


def _tri(n, lower_incl):
    a = lax.broadcasted_iota(jnp.int32, (n, n), 0)
    b = lax.broadcasted_iota(jnp.int32, (n, n), 1)
    return jnp.where(a >= b, 1.0, 0.0).astype(BF16) if lower_incl else jnp.where(a <= b, 1.0, 0.0).astype(BF16)


def _fox_prep_call(fl_raw, b_pad, *, name):
    S = fl_raw.shape[0]
    nb = S // TQ

    def body(fl_ref, b_ref, crep_ref, ct_ref, carry_ref):
        i = pl.program_id(0)

        @pl.when(i == 0)
        def _():
            carry_ref[...] = jnp.zeros_like(carry_ref)
        logf = jax.nn.log_sigmoid(fl_ref[...] + b_ref[...])
        cb = _dot3_l(_tri(TQ, True), logf) + carry_ref[0:1, :]
        carry_ref[0:1, :] = cb[TQ - 1:TQ, :]
        a = lax.broadcasted_iota(jnp.int32, (128, D_GRP), 0)
        b = lax.broadcasted_iota(jnp.int32, (128, D_GRP), 1) // HEAD
        expand = jnp.where(a == b, 1.0, 0.0).astype(BF16)
        crep = _dot3(cb, expand)
        crep_ref[...] = crep
        ct_ref[...] = crep.T

    return pl.pallas_call(
        body, grid=(nb,),
        in_specs=[_row_spec(TQ, 128), _vec_spec(128)],
        out_specs=(_row_spec(TQ, D_GRP), pl.BlockSpec((None, D_GRP, TQ), lambda i: (i, 0, 0))),
        out_shape=(jax.ShapeDtypeStruct((S, D_GRP), F32), jax.ShapeDtypeStruct((nb, D_GRP, TQ), F32)),
        scratch_shapes=[pltpu.VMEM((8, 128), F32)],
        name=name, compiler_params=_cparams(),
    )(fl_raw, b_pad)


def _lane_masks():
    lane = lax.broadcasted_iota(jnp.int32, (1, 128), 1)
    return lane < HEAD, lane >= HEAD


def _fox_fwd_call(proj, c_rep, c_t, *, name, task=None):
    S = proj.shape[0]
    nq = S // TQ
    scale = HEAD ** -0.5

    def body(q_ref, k_ref, v_ref, c_ref, ct_ref, o_ref, lse_ref):
        i = pl.program_id(1)
        m_lo, m_hi = _lane_masks()
        masks = (m_lo, m_hi)
        q = q_ref[...] * scale
        qm = [jnp.where(mk, q, jnp.zeros_like(q)) for mk in masks]
        cq = c_ref[...]
        cqh = [cq[:, 0:1], cq[:, HEAD:HEAD + 1]]
        row = lax.broadcasted_iota(jnp.int32, (TQ, TQ), 0)
        col = lax.broadcasted_iota(jnp.int32, (TQ, TQ), 1)

        def scores(j):
            start = pl.multiple_of(j * TQ, TQ)
            k = k_ref[pl.ds(start, TQ), :]
            ct = ct_ref[j]
            return tuple(_dot_nt(qm[h], k) + (cqh[h] - ct[HEAD * h:HEAD * h + 1, :]) for h in range(2))

        def update(j, ss, state, masked):
            ms, ls, acc = state
            start = pl.multiple_of(j * TQ, TQ)
            v = v_ref[pl.ds(start, TQ), :]
            new_m, new_l, pv, alpha_l = [], [], [], []
            for h in range(2):
                s = ss[h]
                if masked:
                    s = jnp.where(row >= col, s, NEG)
                mn = jnp.maximum(ms[h], jnp.max(s, axis=1, keepdims=True))
                alpha = jnp.exp(ms[h] - mn)
                p = jnp.exp(s - mn)
                new_l.append(alpha * ls[h] + jnp.sum(p, axis=1, keepdims=True))
                new_m.append(mn)
                alpha_l.append(alpha)
                pv.append(_dot(p.astype(BF16), jnp.where(masks[h], v, jnp.zeros_like(v))))
            alpha_lane = jnp.where(m_lo, alpha_l[0], alpha_l[1])
            acc = acc * alpha_lane + pv[0] + pv[1]
            return (tuple(new_m), tuple(new_l), acc)

        def step(j, carry):
            ss, state = carry
            return (scores(j + 1), update(j, ss, state, False))

        init = ((jnp.full((TQ, 1), NEG, F32),) * 2, (jnp.zeros((TQ, 1), F32),) * 2, jnp.zeros((TQ, 128), F32))
        ss, state = lax.fori_loop(0, i, step, (scores(0), init))
        ms, ls, acc = update(i, ss, state, True)
        l_lane = jnp.where(m_lo, ls[0], ls[1])
        o_ref[...] = acc / l_lane
        lse_ref[...] = jnp.where(m_lo, ms[0] + jnp.log(ls[0]), ms[1] + jnp.log(ls[1])).T

    return _pallas(
        body, grid=(N_PAIR, nq),
        in_specs=[pl.BlockSpec((TQ, 128), lambda p, i: (i, p)),
                  pl.BlockSpec((S, 128), lambda p, i: (0, N_PAIR + p)),
                  pl.BlockSpec((S, 128), lambda p, i: (0, 2 * N_PAIR + p)),
                  pl.BlockSpec((TQ, 128), lambda p, i: (i, p)),
                  pl.BlockSpec((nq, 128, TQ), lambda p, i: (0, p, 0))],
        out_specs=(pl.BlockSpec((TQ, 128), lambda p, i: (i, p)), pl.BlockSpec((None, 128, TQ), lambda p, i: (i, p, 0))),
        out_shape=(jax.ShapeDtypeStruct((S, D_GRP), F32), jax.ShapeDtypeStruct((nq, D_GRP, TQ), F32)),
        name=name, task=task,
    )(proj, proj, proj, c_rep, c_t)


def _fox_bwd_call(proj, do, lse_t, delta_t, c_rep, c_t, *, name, task=None):
    S = proj.shape[0]
    nq = S // TQ
    scale = HEAD ** -0.5

    def body(q_ref, k_ref, v_ref, do_ref, lse_ref, dl_ref, ck_ref, ct_ref,
             dq_ref, dk_ref, dv_ref, dcq_ref, dck_ref, dqa_ref):
        j = pl.program_id(1)
        m_lo, m_hi = _lane_masks()
        masks = (m_lo, m_hi)

        @pl.when(j == 0)
        def _():
            dqa_ref[...] = jnp.zeros_like(dqa_ref)
            dcq_ref[...] = jnp.zeros_like(dcq_ref)
        k = k_ref[...]
        v = v_ref[...]
        km = [jnp.where(mk, k, jnp.zeros_like(k)) for mk in masks]
        ck = ck_ref[...]
        krow = lax.broadcasted_iota(jnp.int32, (TQ, TQ), 0)
        qcol = lax.broadcasted_iota(jnp.int32, (TQ, TQ), 1)

        def probs(i):
            start = pl.multiple_of(i * TQ, TQ)
            q = q_ref[pl.ds(start, TQ), :]
            do = do_ref[pl.ds(start, TQ), :]
            lse = lse_ref[i]
            cq = ct_ref[i]
            out = []
            for h in range(2):
                lo = HEAD * h
                qm = jnp.where(masks[h], q * scale, jnp.zeros_like(q))
                dom = jnp.where(masks[h], do, jnp.zeros_like(do))
                st = _dot_nt(k, qm) + (cq[lo:lo + 1, :] - ck[:, lo:lo + 1])
                out.append((jnp.exp(st - lse[lo:lo + 1, :]), _dot_nt(v, dom)))
            return tuple(out)

        def update(i, pd, carry, masked):
            dk, dv, dck = carry
            start = pl.multiple_of(i * TQ, TQ)
            q = q_ref[pl.ds(start, TQ), :]
            do = do_ref[pl.ds(start, TQ), :]
            dl = dl_ref[i]
            dq = jnp.zeros((TQ, 128), F32)
            new_dck = []
            for h in range(2):
                lo = HEAD * h
                qm = jnp.where(masks[h], q, jnp.zeros_like(q))
                dom = jnp.where(masks[h], do, jnp.zeros_like(do))
                pt, dpt = pd[h]
                if masked:
                    pt = jnp.where(qcol >= krow, pt, 0.0)
                dst = pt * (dpt - dl[lo:lo + 1, :])
                dcq_ref[i, h:h + 1, :] += jnp.sum(dst, axis=0, keepdims=True)
                new_dck.append(dck[h] + jnp.sum(dst, axis=1, keepdims=True))
                dsb = (dst * scale).astype(BF16)
                dv = dv + _dot(pt.astype(BF16), dom)
                dk = dk + _dot(dsb, qm)
                dq = dq + _dot_tn(dsb, km[h])
            dqa_ref[pl.ds(start, TQ), :] += dq
            return (dk, dv, tuple(new_dck))

        def step(i, carry):
            pd, sums = carry
            return (probs(jnp.minimum(i + 1, nq - 1)), update(i, pd, sums, False))

        init = (jnp.zeros((TQ, 128), F32), jnp.zeros((TQ, 128), F32), (jnp.zeros((TQ, 1), F32),) * 2)
        first = probs(j)
        second = probs(jnp.minimum(j + 1, nq - 1))
        _, (dk, dv, dck) = lax.fori_loop(j + 1, nq, step, (second, update(j, first, init, True)))
        dk_ref[...] = dk.astype(BF16)
        dv_ref[...] = dv.astype(BF16)
        dck_ref[...] = -jnp.where(m_lo, dck[0], dck[1])

        @pl.when(j == nq - 1)
        def _():
            dq_ref[...] = dqa_ref[...].astype(BF16)

    res = lambda p, j: (0, p)
    stat = pl.BlockSpec((nq, 128, TQ), lambda p, j: (0, p, 0))
    blk = pl.BlockSpec((TQ, 128), lambda p, j: (j, p))
    return _pallas(
        body, grid=(N_PAIR, nq), task=task,
        in_specs=[pl.BlockSpec((S, 128), res),
                  pl.BlockSpec((TQ, 128), lambda p, j: (j, N_PAIR + p)),
                  pl.BlockSpec((TQ, 128), lambda p, j: (j, 2 * N_PAIR + p)),
                  pl.BlockSpec((S, 128), res), stat, stat, blk, stat],
        out_specs=(pl.BlockSpec((S, 128), res), blk, blk,
                   pl.BlockSpec((None, nq, 8, TQ), lambda p, j: (p, 0, 0, 0)), blk),
        out_shape=(jax.ShapeDtypeStruct((S, D_GRP), BF16), jax.ShapeDtypeStruct((S, D_GRP), BF16),
                   jax.ShapeDtypeStruct((S, D_GRP), BF16), jax.ShapeDtypeStruct((N_PAIR, nq, 8, TQ), F32),
                   jax.ShapeDtypeStruct((S, D_GRP), F32)),
        scratch_shapes=[pltpu.VMEM((S, 128), F32)],
        name=name,
    )(proj, proj, proj, do, lse_t, delta_t, c_rep, c_t)


def _fox_gate_bwd_call(dc_rows, fl_raw, b_pad, *, name):
    S = fl_raw.shape[0]
    nb = S // TQ

    def body(dc_ref, fl_ref, b_ref, dfl_ref, db_ref, carry_ref):
        i = pl.program_id(0)

        @pl.when(i == 0)
        def _():
            carry_ref[...] = jnp.zeros_like(carry_ref)
        rc = _dot3(dc_ref[...], _tri(TQ, True)) + carry_ref[:, 0:1]
        carry_ref[...] = jnp.broadcast_to(rc[:, 0:1], carry_ref.shape)
        fl = fl_ref[...] + b_ref[...]
        dfl = rc.T * jax.nn.sigmoid(-fl)
        dfl_ref[...] = dfl.astype(BF16)
        _acc_add(db_ref, jnp.sum(dfl, axis=0, keepdims=True), i)

    rev = lambda i: (nb - 1 - i, 0)
    return pl.pallas_call(
        body, grid=(nb,),
        in_specs=[pl.BlockSpec((128, TQ), lambda i: (0, nb - 1 - i)), pl.BlockSpec((TQ, 128), rev), _vec_spec(128)],
        out_specs=(pl.BlockSpec((TQ, 128), rev), _acc_spec(128)),
        out_shape=(jax.ShapeDtypeStruct((S, 128), BF16), jax.ShapeDtypeStruct((8, 128), F32)),
        scratch_shapes=[pltpu.VMEM((128, 128), F32)],
        name=name, compiler_params=_cparams(),
    )(dc_rows, fl_raw, b_pad)


def _chk_bias_call(g_rev, *, name):
    def body(g_ref, o_ref):
        x = jnp.broadcast_to(g_ref[...], (TQ, ROLL_W))
        rolled = pltpu.roll(x, ROLL_W - (TQ - 1), 1, stride=1, stride_axis=0)
        qc = lax.broadcasted_iota(jnp.int32, (TQ, WIN), 0) // CHUNK
        kc = lax.broadcasted_iota(jnp.int32, (TQ, WIN), 1) // CHUNK
        band = (kc >= qc) & (kc <= qc + LEFT)
        o_ref[...] = jnp.where(band, rolled[:, 0:WIN], NEG)

    return pl.pallas_call(
        body, grid=(8,),
        in_specs=[pl.BlockSpec((None, 1, ROLL_W), lambda h: (h, 0, 0))],
        out_specs=pl.BlockSpec((None, TQ, WIN), lambda h: (h, 0, 0)),
        out_shape=jax.ShapeDtypeStruct((8, TQ, WIN), F32), name=name, compiler_params=_cparams(),
    )(g_rev.reshape(8, 1, ROLL_W))


def _chk_scores(i, qm, kwin, bias, scale):
    s = _dot_nt(qm * scale, kwin) + bias
    kc = lax.broadcasted_iota(jnp.int32, (TQ, WIN), 1) // CHUNK
    return jnp.where(kc + i * (TQ // CHUNK) >= LEFT, s, NEG)


def _chk_fwd_call(proj, bias, *, name, task=None):
    S = proj.shape[0]
    nq = S // TQ
    scale = HEAD ** -0.5

    def body(q_ref, k_ref, v_ref, b_ref, o_ref, kp_ref, vp_ref):
        i = pl.program_id(1)

        @pl.when(i == 0)
        def _():
            kp_ref[0:PADK, :] = jnp.zeros((PADK, 128), BF16)
            vp_ref[0:PADK, :] = jnp.zeros((PADK, 128), BF16)
            kp_ref[PADK:PADK + S, :] = k_ref[...]
            vp_ref[PADK:PADK + S, :] = v_ref[...]
        masks = _lane_masks()
        q = q_ref[...]
        start = pl.multiple_of(i * TQ, TQ)
        kwin = kp_ref[pl.ds(start, WIN), :]
        vwin = vp_ref[pl.ds(start, WIN), :]
        ss = [_chk_scores(i, jnp.where(masks[h], q, jnp.zeros_like(q)), kwin, b_ref[h], scale) for h in range(2)]
        ps = []
        for s in ss:
            p = jnp.exp(s - jnp.max(s, axis=1, keepdims=True))
            ps.append((p / jnp.sum(p, axis=1, keepdims=True)).astype(BF16))
        o_ref[...] = (_dot(ps[0], jnp.where(masks[0], vwin, jnp.zeros_like(vwin)))
                      + _dot(ps[1], jnp.where(masks[1], vwin, jnp.zeros_like(vwin))))

    c0 = 3 * N_PAIR
    return _pallas(
        body, grid=(N_PAIR, nq), task=task,
        in_specs=[pl.BlockSpec((TQ, 128), lambda p, i: (i, c0 + p)),
                  pl.BlockSpec((S, 128), lambda p, i: (0, c0 + N_PAIR + p)),
                  pl.BlockSpec((S, 128), lambda p, i: (0, c0 + 2 * N_PAIR + p)),
                  pl.BlockSpec((2, TQ, WIN), lambda p, i: (p, 0, 0))],
        out_specs=pl.BlockSpec((TQ, 128), lambda p, i: (i, p)),
        out_shape=jax.ShapeDtypeStruct((S, D_GRP), F32),
        scratch_shapes=[pltpu.VMEM((S + PADK, 128), BF16), pltpu.VMEM((S + PADK, 128), BF16)],
        name=name,
    )(proj, proj, proj, bias)


def _chk_bwd_call(proj, do, bias, *, name, task=None):
    S = proj.shape[0]
    nq = S // TQ
    scale = HEAD ** -0.5

    def body(q_ref, k_ref, v_ref, do_ref, b_ref, dq_ref, dk_ref, dv_ref, dg_ref, kp_ref, vp_ref, dkp_ref, dvp_ref, db_ref):
        i = pl.program_id(1)

        @pl.when(i == 0)
        def _():
            kp_ref[0:PADK, :] = jnp.zeros((PADK, 128), BF16)
            vp_ref[0:PADK, :] = jnp.zeros((PADK, 128), BF16)
            kp_ref[PADK:PADK + S, :] = k_ref[...]
            vp_ref[PADK:PADK + S, :] = v_ref[...]
            dkp_ref[...] = jnp.zeros_like(dkp_ref)
            dvp_ref[...] = jnp.zeros_like(dvp_ref)
            db_ref[...] = jnp.zeros_like(db_ref)
        masks = _lane_masks()
        q = q_ref[...]
        dout = do_ref[...]
        start = pl.multiple_of(i * TQ, TQ)
        kwin = kp_ref[pl.ds(start, WIN), :]
        vwin = vp_ref[pl.ds(start, WIN), :]
        qm = [jnp.where(mk, q, jnp.zeros_like(q)) for mk in masks]
        dom = [jnp.where(mk, dout, jnp.zeros_like(dout)) for mk in masks]
        ss = [_chk_scores(i, qm[h], kwin, b_ref[h], scale) for h in range(2)]
        dps = [_dot_nt(dom[h], vwin) for h in range(2)]
        pbs, dsbs = [], []
        for h in range(2):
            p = jnp.exp(ss[h] - jnp.max(ss[h], axis=1, keepdims=True))
            p = p / jnp.sum(p, axis=1, keepdims=True)
            ds = p * (dps[h] - jnp.sum(p * dps[h], axis=1, keepdims=True))
            db_ref[h] += ds
            pbs.append(p.astype(BF16))
            dsbs.append((ds * scale).astype(BF16))
        dq_ref[...] = (_dot(dsbs[0], jnp.where(masks[0], kwin, jnp.zeros_like(kwin)))
                       + _dot(dsbs[1], jnp.where(masks[1], kwin, jnp.zeros_like(kwin)))).astype(BF16)
        dkp_ref[pl.ds(start, WIN), :] += _dot_tn(dsbs[0], qm[0]) + _dot_tn(dsbs[1], qm[1])
        dvp_ref[pl.ds(start, WIN), :] += _dot_tn(pbs[0], dom[0]) + _dot_tn(pbs[1], dom[1])

        @pl.when(i == nq - 1)
        def _():
            dk_ref[...] = dkp_ref[PADK:PADK + S, :].astype(BF16)
            dv_ref[...] = dvp_ref[PADK:PADK + S, :].astype(BF16)
            a = lax.broadcasted_iota(jnp.int32, (TQ, TQ), 0)
            b = lax.broadcasted_iota(jnp.int32, (TQ, TQ), 1)
            flip = jnp.where(a + b == TQ - 1, 1.0, 0.0).astype(BF16)
            e = lax.broadcasted_iota(jnp.int32, (1, ROLL_W), 1)
            dg_ref[...] = jnp.zeros_like(dg_ref)
            for h in range(2):
                rev = _dot3_l(flip, db_ref[h])
                wide = jnp.concatenate([rev, jnp.zeros((TQ, ROLL_W - WIN), F32)], axis=1)
                diag = pltpu.roll(wide, 0, 1, stride=1, stride_axis=0)
                dg = jnp.sum(diag, axis=0, keepdims=True)
                lo = jnp.sum(jnp.where(e <= 639, dg, 0.0), axis=1, keepdims=True)
                hi = jnp.sum(jnp.where(e >= 895, dg, 0.0), axis=1, keepdims=True)
                dg_ref[h:h + 1, :] = jnp.where(e == 639, lo, jnp.where(e == 895, hi, dg))

    c0 = 3 * N_PAIR
    res = lambda p, i: (0, p)
    return _pallas(
        body, grid=(N_PAIR, nq), task=task,
        in_specs=[pl.BlockSpec((TQ, 128), lambda p, i: (i, c0 + p)),
                  pl.BlockSpec((S, 128), lambda p, i: (0, c0 + N_PAIR + p)),
                  pl.BlockSpec((S, 128), lambda p, i: (0, c0 + 2 * N_PAIR + p)),
                  pl.BlockSpec((TQ, 128), lambda p, i: (i, p)),
                  pl.BlockSpec((2, TQ, WIN), lambda p, i: (p, 0, 0))],
        out_specs=(pl.BlockSpec((TQ, 128), lambda p, i: (i, p)), pl.BlockSpec((S, 128), res),
                   pl.BlockSpec((S, 128), res), pl.BlockSpec((None, 8, ROLL_W), lambda p, i: (p, 0, 0))),
        out_shape=(jax.ShapeDtypeStruct((S, D_GRP), BF16), jax.ShapeDtypeStruct((S, D_GRP), BF16),
                   jax.ShapeDtypeStruct((S, D_GRP), BF16), jax.ShapeDtypeStruct((N_PAIR, 8, ROLL_W), F32)),
        scratch_shapes=[pltpu.VMEM((S + PADK, 128), BF16), pltpu.VMEM((S + PADK, 128), BF16),
                        pltpu.VMEM((S + PADK, 128), F32), pltpu.VMEM((S + PADK, 128), F32),
                        pltpu.VMEM((2, TQ, WIN), F32)],
        name=name,
    )(proj, proj, proj, do, bias)


def _mem_fwd_call(q, k, v, *, name, tq=1024):
    S = q.shape[0]
    scale = MEM_HD ** -0.5

    def body(q_ref, k_ref, v_ref, o_ref):
        s = _dot_nt(q_ref[...] * scale, k_ref[...])
        p = jnp.exp(s - jnp.max(s, axis=1, keepdims=True))
        p = p / jnp.sum(p, axis=1, keepdims=True)
        o_ref[...] = _dot(p.astype(BF16), v_ref[...]).astype(BF16)

    return pl.pallas_call(
        body, grid=(MEM_HEADS, S // tq),
        in_specs=[pl.BlockSpec((tq, MEM_HD), lambda h, i: (i, h)),
                  pl.BlockSpec((N_MEM, MEM_HD), lambda h, i: (0, h)),
                  pl.BlockSpec((N_MEM, MEM_HD), lambda h, i: (0, h))],
        out_specs=pl.BlockSpec((tq, MEM_HD), lambda h, i: (i, h)),
        out_shape=jax.ShapeDtypeStruct((S, D), BF16), name=name, compiler_params=_cparams(),
    )(q, k, v)


def _mem_bwd_call(q, k, v, do, *, name, tq=1024):
    S = q.shape[0]
    n = S // tq
    scale = MEM_HD ** -0.5

    def body(q_ref, k_ref, v_ref, do_ref, dq_ref, dk_ref, dv_ref, dka_ref, dva_ref):
        i = pl.program_id(1)

        @pl.when(i == 0)
        def _():
            dka_ref[...] = jnp.zeros_like(dka_ref)
            dva_ref[...] = jnp.zeros_like(dva_ref)
        qb = q_ref[...]
        kb = k_ref[...]
        dob = do_ref[...]
        s = _dot_nt(qb * scale, kb)
        p = jnp.exp(s - jnp.max(s, axis=1, keepdims=True))
        p = p / jnp.sum(p, axis=1, keepdims=True)
        dp = _dot_nt(dob, v_ref[...])
        ds = p * (dp - jnp.sum(p * dp, axis=1, keepdims=True))
        dsb = (ds * scale).astype(BF16)
        dq_ref[...] = _dot(dsb, kb).astype(BF16)
        dka_ref[...] += _dot_tn(dsb, qb)
        dva_ref[...] += _dot_tn(p.astype(BF16), dob)

        @pl.when(i == n - 1)
        def _():
            dk_ref[...] = dka_ref[...].astype(BF16)
            dv_ref[...] = dva_ref[...].astype(BF16)

    kv = pl.BlockSpec((N_MEM, MEM_HD), lambda h, i: (0, h))
    qs = pl.BlockSpec((tq, MEM_HD), lambda h, i: (i, h))
    return pl.pallas_call(
        body, grid=(MEM_HEADS, n), in_specs=[qs, kv, kv, qs], out_specs=(qs, kv, kv),
        out_shape=(jax.ShapeDtypeStruct((S, D), BF16), jax.ShapeDtypeStruct((N_MEM, D), BF16),
                   jax.ShapeDtypeStruct((N_MEM, D), BF16)),
        scratch_shapes=[pltpu.VMEM((N_MEM, MEM_HD), F32), pltpu.VMEM((N_MEM, MEM_HD), F32)],
        name=name, compiler_params=_cparams(),
    )(q, k, v, do)


def _rel_table_to_g(rel):
    return jnp.concatenate([
        jnp.broadcast_to(rel[:, N_REL - 1:N_REL], (8, 640)),
        rel[:, 1:N_REL - 1][:, ::-1],
        jnp.broadcast_to(rel[:, 0:1], (8, 129)),
    ], axis=1)


def _g_to_rel_table(dg):
    return dg[:, 639:896][:, ::-1]


def _place():
    x, y, c = lax.axis_index("x"), lax.axis_index("y"), lax.axis_index("c")
    others = [(1 - x, y), (x, 1 - y), (1 - x, 1 - y)]
    return x, y, c, others


def _half(c, rows):
    hr = rows // 2
    return pl.ds(pl.multiple_of(c * hr, 16), hr)


def _dma_sems(*shape):
    return pltpu.SemaphoreType.DMA(shape)


def _cast_slabs_call(ws, chip_arr, *, name, tm=256, task=None):
    n = len(ws)
    cols = ws[0].shape[1]
    tiles = [w.shape[0] // tm for w in ws]
    steps = max(tiles)

    def body(chip_ref, *refs):
        i = pl.program_id(0)
        for k in range(n):
            def cast(k=k):
                refs[n + k][...] = refs[k][...].astype(BF16)
            if tiles[k] == steps:
                cast()
            else:
                pl.when(i < tiles[k])(cast)

    in_specs = [pl.BlockSpec((tm, cols), lambda i, chip, t=t: (jnp.minimum(i, t - 1), 0)) for t in tiles]
    out_specs = [pl.BlockSpec((None, tm, cols), lambda i, chip, t=t: (chip[0], jnp.minimum(i, t - 1), 0)) for t in tiles]
    out_shape = [jax.ShapeDtypeStruct((N_CHIP,) + w.shape, BF16) for w in ws]
    return _pallas(body, grid=(steps,), in_specs=in_specs, out_specs=out_specs, out_shape=out_shape, name=name,
                   task=task, prefetch=1)(chip_arr, *ws)


def _cast_slab_call(w, chip_arr, *, name, tm=256, pad_rows=0):
    rows, cols = w.shape
    if pad_rows:
        tm = rows
    tm = min(tm, rows)

    def body(chip_ref, w_ref, o_ref):
        o_ref[0:tm, :] = w_ref[...].astype(BF16)
        if pad_rows:
            o_ref[tm:tm + pad_rows, :] = jnp.zeros((pad_rows, cols), BF16)

    return pl.pallas_call(
        body,
        grid_spec=pltpu.PrefetchScalarGridSpec(
            num_scalar_prefetch=1, grid=(rows // tm,),
            in_specs=[pl.BlockSpec((tm, cols), lambda i, chip: (i, 0))],
            out_specs=pl.BlockSpec((None, tm + pad_rows, cols), lambda i, chip: (chip[0], i, 0))),
        out_shape=jax.ShapeDtypeStruct((N_CHIP, rows + pad_rows, cols), BF16), name=name,
        compiler_params=_cparams(),
    )(chip_arr, w)


def _ag_ici_task(gathered):
    n = len(gathered)

    def copies(ins, outs, sems):
        send_sems, recv_sems = sems
        x, y, c, others = _place()
        me = 2 * x + y
        for k in range(n):
            mine = _half(c, gathered[k].shape[1])
            for t, (ox, oy) in enumerate(others):
                yield [pltpu.make_async_remote_copy(
                    src_ref=ins[k].at[me, mine], dst_ref=outs[k].at[slab, mine],
                    send_sem=send_sems.at[k, t], recv_sem=recv_sems.at[k, t],
                    device_id=(ox, oy, c), device_id_type=MESH) for slab in (me, 2 * ox + oy)]

    def issue(ins, outs, sems):
        for outgoing, _ in copies(ins, outs, sems):
            outgoing.start()

    def drain(ins, outs, sems):
        for outgoing, incoming in copies(ins, outs, sems):
            incoming.wait_recv()
            outgoing.wait_send()

    return _Task(gathered, [jax.ShapeDtypeStruct(g.shape, g.dtype) for g in gathered],
                 [_dma_sems(n, 3), _dma_sems(n, 3)], issue, drain, aliases={k: k for k in range(n)})


def _ag_d2d_task(gathered):
    n = len(gathered)

    def copies(ins, outs, sems):
        send_sems, recv_sems = sems
        x, y, c, others = _place()
        for k in range(n):
            rows = gathered[k].shape[1]
            mine, theirs = _half(c, rows), _half(1 - c, rows)
            for t, (ox, oy) in enumerate(others):
                slab = 2 * ox + oy
                pair = [pltpu.make_async_remote_copy(
                    src_ref=ins[k].at[slab, half], dst_ref=outs[k].at[slab, half],
                    send_sem=send_sems.at[k, t], recv_sem=recv_sems.at[k, t],
                    device_id=(x, y, 1 - c), device_id_type=MESH) for half in (mine, theirs)]
                yield pair

    def issue(ins, outs, sems):
        for outgoing, _ in copies(ins, outs, sems):
            outgoing.start()

    def drain(ins, outs, sems):
        for outgoing, incoming in copies(ins, outs, sems):
            incoming.wait_recv()
            outgoing.wait_send()

    return _Task(gathered, [jax.ShapeDtypeStruct(g.shape, g.dtype) for g in gathered],
                 [_dma_sems(n, 3), _dma_sems(n, 3)], issue, drain, aliases={k: k for k in range(n)})


def _rs_pair_task(ds):
    n = len(ds)

    def copies(ins, outs, sems):
        send_sems, recv_sems = sems
        x, y, c, _ = _place()
        for k in range(n):
            yield pltpu.make_async_remote_copy(
                src_ref=ins[k].at[:, _half(1 - c, ds[k].shape[1])], dst_ref=outs[k],
                send_sem=send_sems.at[k], recv_sem=recv_sems.at[k],
                device_id=(x, y, 1 - c), device_id_type=MESH)

    def issue(ins, outs, sems):
        for cp in copies(ins, outs, sems):
            cp.start()

    def drain(ins, outs, sems):
        for cp in copies(ins, outs, sems):
            cp.wait()

    return _Task(ds, [jax.ShapeDtypeStruct((N_CHIP, d.shape[1] // 2, d.shape[2]), d.dtype) for d in ds],
                 [_dma_sems(n), _dma_sems(n)], issue, drain)


def _pair_add_call(d, r1, c_arr, *, name, tm=512):
    _, rows, cols = d.shape
    hr = rows // 2
    tm = tm if hr % tm == 0 else hr
    nb = hr // tm

    def body(c_ref, d_ref, r_ref, o_ref):
        o_ref[...] = (d_ref[...].astype(F32) + r_ref[...].astype(F32)).astype(BF16)

    return pl.pallas_call(
        body,
        grid_spec=pltpu.PrefetchScalarGridSpec(
            num_scalar_prefetch=1, grid=(N_CHIP, nb),
            in_specs=[pl.BlockSpec((None, tm, cols), lambda j, i, c: (j, c[0] * nb + i, 0)),
                      pl.BlockSpec((None, tm, cols), lambda j, i, c: (j, i, 0))],
            out_specs=pl.BlockSpec((None, tm, cols), lambda j, i, c: (j, i, 0))),
        out_shape=jax.ShapeDtypeStruct((N_CHIP, hr, cols), BF16), name=name, compiler_params=_cparams(),
    )(c_arr, d, r1)


def _rs_chip_task(ps):
    n = len(ps)

    def copies(ins, outs, sems):
        send_sems, recv_sems = sems
        x, y, c, others = _place()
        for k in range(n):
            for t, (ox, oy) in enumerate(others):
                yield pltpu.make_async_remote_copy(
                    src_ref=ins[k].at[2 * ox + oy], dst_ref=outs[k].at[t],
                    send_sem=send_sems.at[k, t], recv_sem=recv_sems.at[k, t],
                    device_id=(ox, oy, c), device_id_type=MESH)

    def issue(ins, outs, sems):
        for cp in copies(ins, outs, sems):
            cp.start()

    def drain(ins, outs, sems):
        for cp in copies(ins, outs, sems):
            cp.wait()

    return _Task(ps, [jax.ShapeDtypeStruct((3,) + p.shape[1:], p.dtype) for p in ps],
                 [_dma_sems(n, 3), _dma_sems(n, 3)], issue, drain)


def _chip_sum_call(p, r2, place_arr, *, name, tm=512):
    _, hr, cols = r2.shape
    tm = tm if hr % tm == 0 else hr
    nb = hr // tm

    def body(place_ref, p_ref, r_ref, o_ref):
        acc = p_ref[...].astype(F32)
        for j in range(3):
            acc = acc + r_ref[j].astype(F32)
        o_ref[...] = acc

    return pl.pallas_call(
        body,
        grid_spec=pltpu.PrefetchScalarGridSpec(
            num_scalar_prefetch=1, grid=(nb,),
            in_specs=[pl.BlockSpec((None, tm, cols), lambda i, pc: (pc[0], i, 0)),
                      pl.BlockSpec((3, tm, cols), lambda i, pc: (0, i, 0))],
            out_specs=pl.BlockSpec((tm, cols), lambda i, pc: (pc[1] * nb + i, 0))),
        out_shape=jax.ShapeDtypeStruct((2 * hr, cols), F32), name=name, compiler_params=_cparams(),
    )(place_arr, p, r2)


def _rs_gather_task(gs):
    n = len(gs)

    def copies(ins, outs, sems):
        send_sems, recv_sems = sems
        x, y, c, _ = _place()
        for k in range(n):
            rows = gs[k].shape[0]
            mine, theirs = _half(c, rows), _half(1 - c, rows)
            yield [pltpu.make_async_remote_copy(
                src_ref=ins[k].at[mine], dst_ref=outs[k].at[half],
                send_sem=send_sems.at[k], recv_sem=recv_sems.at[k],
                device_id=(x, y, 1 - c), device_id_type=MESH) for half in (mine, theirs)]

    def issue(ins, outs, sems):
        for outgoing, _ in copies(ins, outs, sems):
            outgoing.start()

    def drain(ins, outs, sems):
        for outgoing, incoming in copies(ins, outs, sems):
            incoming.wait_recv()
            outgoing.wait_send()

    return _Task(gs, [jax.ShapeDtypeStruct(g.shape, g.dtype) for g in gs],
                 [_dma_sems(n), _dma_sems(n)], issue, drain, aliases={k: k for k in range(n)})


def _adamw(w, g, m, v):
    m = ADAM_B1 * m + (1.0 - ADAM_B1) * g
    v = ADAM_B2 * v + (1.0 - ADAM_B2) * jnp.square(g)
    m_hat = m / (1.0 - ADAM_B1 ** ADAM_STEP)
    v_hat = v / (1.0 - ADAM_B2 ** ADAM_STEP)
    delta = -ADAM_LR * (m_hat / (jnp.sqrt(v_hat) + ADAM_EPS) + ADAM_WD * w)
    return delta, m, v


def _adamw_call(items, *, name, tm=256, task=None):
    n = len(items)
    cols = items[0][0].shape[1]
    tiles = [it[0].shape[0] // tm for it in items]
    steps = max(tiles)

    def body(*refs):
        i = pl.program_id(0)
        ins, outs = refs[:4 * n], refs[4 * n:]
        for k in range(n):
            def update(k=k):
                g = ins[4 * k + 1][...]
                res = _adamw(ins[4 * k][...], g, ins[4 * k + 2][...], ins[4 * k + 3][...])
                outs[4 * k][...] = g
                for j in range(3):
                    outs[4 * k + 1 + j][...] = res[j]
            if tiles[k] == steps:
                update()
            else:
                pl.when(i < tiles[k])(update)

    in_specs, out_specs, out_shape, args = [], [], [], []
    for it, t in zip(items, tiles):
        spec = pl.BlockSpec((tm, cols), lambda i, t=t: (jnp.minimum(i, t - 1), 0))
        in_specs += [spec] * 4
        out_specs += [spec] * 4
        out_shape += [jax.ShapeDtypeStruct(it[0].shape, F32)] * 4
        args += list(it)
    res = _pallas(body, grid=(steps,), in_specs=in_specs, out_specs=out_specs, out_shape=out_shape,
                  name=name, task=task)(*args)
    outs, extra = res if task is not None else (res, None)
    grouped = [tuple(outs[4 * k:4 * k + 4]) for k in range(n)]
    return (grouped, extra) if task is not None else grouped


def _adamw_cols_call(w, g_pad, m, v, *, name, tn=256):
    rows, cols = w.shape

    def body(w_ref, g_ref, m_ref, v_ref, go_ref, d_ref, mo_ref, vo_ref):
        g = g_ref[0:rows, :]
        d, mn, vn = _adamw(w_ref[...], g, m_ref[...], v_ref[...])
        go_ref[...] = g
        d_ref[...] = d
        mo_ref[...] = mn
        vo_ref[...] = vn

    spec = pl.BlockSpec((rows, tn), lambda j: (0, j))
    gspec = pl.BlockSpec((g_pad.shape[0], tn), lambda j: (0, j))
    return _pallas(body, grid=(cols // tn,), in_specs=[spec, gspec, spec, spec], out_specs=(spec,) * 4,
                   out_shape=(jax.ShapeDtypeStruct((rows, cols), F32),) * 4, name=name)(w, g_pad, m, v)


N_DEV = 8
SMALL_ROWS = 24
SMALL_LAYOUT = {
    "g_mix_pre": (0, 0, 1, D), "g_mix_post": (1, 0, 1, D), "g_mem_kv": (2, 0, 1, D), "g_mem_pre": (3, 0, 1, D),
    "g_mem_post": (4, 0, 1, D), "g_ff_pre": (5, 0, 1, D), "g_ff_post": (6, 0, 1, D),
    "g_fox_out": (7, 0, 1, D_GRP), "g_chk_out": (7, D_GRP, 1, D_GRP), "b_fgt": (8, 0, 1, 8),
    "rel_bias": (16, 0, 8, N_REL),
}
SMALL = list(SMALL_LAYOUT)


LOSS_ROW = 9


def _small_call(grads, ws, ms, vs, loss_blk, task, *, name):
    n = len(SMALL)
    t_in, t_out = len(task.arrays), len(task.out_shapes)

    def body(*refs):
        g_refs, w_refs, m_refs, v_refs = (refs[j * n:(j + 1) * n] for j in range(4))
        p = 4 * n
        loss_ref, tins = refs[p], refs[p + 1:p + 1 + t_in]
        p += 1 + t_in
        outs, loss_out, touts = refs[p:p + 4 * n], refs[p + 4 * n], refs[p + 4 * n + 1:p + 4 * n + 1 + t_out]
        p += 4 * n + 1 + t_out
        mine, slots, send_sems, recv_sems = refs[p:p + 4]
        tsems = refs[p + 4:]
        task.issue(tins, touts, tsems)
        x, y, c, _ = _place()
        me = 4 * x + 2 * y + c
        mine[...] = jnp.zeros_like(mine)
        for k, name_k in enumerate(SMALL):
            r, l, nr, nl = SMALL_LAYOUT[name_k]
            mine[r:r + nr, l:l + nl] = g_refs[k][0:nr, 0:nl]
        mine[LOSS_ROW:LOSS_ROW + 1, 0:128] = loss_ref[0:1, :]
        slots[me] = mine[...]
        peers = [(dx, dy, dc) for dx in (0, 1) for dy in (0, 1) for dc in (0, 1)][1:]
        cps = []
        for t, (dx, dy, dc) in enumerate(peers):
            px, py, pc = (x + dx) % 2, (y + dy) % 2, (c + dc) % 2
            cps.append(pltpu.make_async_remote_copy(
                src_ref=mine, dst_ref=slots.at[me], send_sem=send_sems.at[t], recv_sem=recv_sems.at[t],
                device_id=(px, py, pc), device_id_type=MESH))
            cps[-1].start()
        for t, (dx, dy, dc) in enumerate(peers):
            px, py, pc = (x + dx) % 2, (y + dy) % 2, (c + dc) % 2
            pltpu.make_async_remote_copy(
                src_ref=mine, dst_ref=slots.at[4 * px + 2 * py + pc], send_sem=send_sems.at[t],
                recv_sem=recv_sems.at[t], device_id=(px, py, pc), device_id_type=MESH).wait_recv()
        for cp in cps:
            cp.wait_send()
        total = slots[0]
        for j in range(1, N_DEV):
            total = total + slots[j]
        for k, name_k in enumerate(SMALL):
            r, l, nr, nl = SMALL_LAYOUT[name_k]
            g = total[r:r + nr, l:l + nl]
            d, mn, vn = _adamw(w_refs[k][...], g, m_refs[k][...], v_refs[k][...])
            for j, val in enumerate((g, d, mn, vn)):
                outs[4 * k + j][...] = val
        loss_out[...] = jnp.broadcast_to(total[LOSS_ROW:LOSS_ROW + 1, 0:128], loss_out.shape)
        task.drain(tins, touts, tsems)

    vm = pl.BlockSpec(memory_space=pltpu.VMEM)
    out_shape = [jax.ShapeDtypeStruct(ws[k].shape, F32) for k in SMALL for _ in range(4)]
    out_shape += [jax.ShapeDtypeStruct((8, 128), F32)] + list(task.out_shapes)
    res = pl.pallas_call(
        body, in_specs=[vm] * (4 * n + 1) + [ANY] * t_in, out_specs=[vm] * (4 * n + 1) + [ANY] * t_out,
        out_shape=out_shape,
        scratch_shapes=[pltpu.VMEM((SMALL_ROWS, D), F32), pltpu.VMEM((N_DEV, SMALL_ROWS, D), F32),
                        _dma_sems(N_DEV - 1), _dma_sems(N_DEV - 1)] + list(task.sems),
        input_output_aliases={4 * n + 1 + i: 4 * n + 1 + j for i, j in task.aliases.items()},
        name=name,
    )(*[d[k] for d in (grads, ws, ms, vs) for k in SMALL], loss_blk, *task.arrays)
    return ({k: tuple(res[4 * i:4 * i + 4]) for i, k in enumerate(SMALL)}, res[4 * n], list(res[4 * n + 1:]))


WEIGHTS = ["w_in", "b_fgt", "rel_bias", "g_fox_out", "g_chk_out", "w_out", "g_mix_pre", "g_mix_post", "g_mem_kv",
           "w_mq", "w_mk", "w_mv", "w_mo", "g_mem_pre", "g_mem_post", "w_ff1", "w_ff2", "g_ff_pre", "g_ff_post"]
BIG = ["w_in", "w_out", "w_mq", "w_mk", "w_mv", "w_mo", "w_ff1", "w_ff2"]


IN_SHARD = D_IN // N_CHIP
IN_PAD = 800


IN_PIECES = [(0, 0, 770), (800, 770, 766), (1566, 3072, 4), (1600, 3076, 4), (1604, 1536, 766), (2400, 2302, 770)]
PAD_ZEROS = [(800 * j + IN_SHARD, IN_PAD - IN_SHARD) for j in range(N_CHIP)]
ALL_ZEROS = [(D_IN, D_ALL - D_IN)]


def _reorder_rows_call(src, to_all, *, name, tn=256):
    rows, cols = src.shape
    zeros = ALL_ZEROS if to_all else PAD_ZEROS

    def body(s_ref, o_ref):
        for pad0, all0, cnt in IN_PIECES:
            s0, d0 = (pad0, all0) if to_all else (all0, pad0)
            o_ref[d0:d0 + cnt, :] = s_ref[s0:s0 + cnt, :]
        for z0, cnt in zeros:
            o_ref[z0:z0 + cnt, :] = jnp.zeros((cnt, tn), src.dtype)

    spec = pl.BlockSpec((rows, tn), lambda j: (0, j))
    return _pallas(body, grid=(cols // tn,), in_specs=[spec], out_specs=spec,
                   out_shape=jax.ShapeDtypeStruct((rows, cols), src.dtype), name=name)(src)


def kernel(x, mem, w_in, b_fgt, rel_bias, g_fox_out, g_chk_out, w_out, g_mix_pre, g_mix_post, g_mem_kv, w_mq, w_mk, w_mv, w_mo, g_mem_pre, g_mem_post, w_ff1, w_ff2, g_ff_pre, g_ff_post, loss_target, m_w_in, m_b_fgt, m_rel_bias, m_g_fox_out, m_g_chk_out, m_w_out, m_g_mix_pre, m_g_mix_post, m_g_mem_kv, m_w_mq, m_w_mk, m_w_mv, m_w_mo, m_g_mem_pre, m_g_mem_post, m_w_ff1, m_w_ff2, m_g_ff_pre, m_g_ff_post, v_w_in, v_b_fgt, v_rel_bias, v_g_fox_out, v_g_chk_out, v_w_out, v_g_mix_pre, v_g_mix_post, v_g_mem_kv, v_w_mq, v_w_mk, v_w_mv, v_w_mo, v_g_mem_pre, v_g_mem_post, v_w_ff1, v_w_ff2, v_g_ff_pre, v_g_ff_post):
    w = dict(w_in=w_in, b_fgt=b_fgt, rel_bias=rel_bias, g_fox_out=g_fox_out, g_chk_out=g_chk_out, w_out=w_out,
             g_mix_pre=g_mix_pre, g_mix_post=g_mix_post, g_mem_kv=g_mem_kv, w_mq=w_mq, w_mk=w_mk, w_mv=w_mv,
             w_mo=w_mo, g_mem_pre=g_mem_pre, g_mem_post=g_mem_post, w_ff1=w_ff1, w_ff2=w_ff2, g_ff_pre=g_ff_pre,
             g_ff_post=g_ff_post)
    m = dict(w_in=m_w_in, b_fgt=m_b_fgt, rel_bias=m_rel_bias, g_fox_out=m_g_fox_out, g_chk_out=m_g_chk_out,
             w_out=m_w_out, g_mix_pre=m_g_mix_pre, g_mix_post=m_g_mix_post, g_mem_kv=m_g_mem_kv, w_mq=m_w_mq,
             w_mk=m_w_mk, w_mv=m_w_mv, w_mo=m_w_mo, g_mem_pre=m_g_mem_pre, g_mem_post=m_g_mem_post,
             w_ff1=m_w_ff1, w_ff2=m_w_ff2, g_ff_pre=m_g_ff_pre, g_ff_post=m_g_ff_post)
    v = dict(w_in=v_w_in, b_fgt=v_b_fgt, rel_bias=v_rel_bias, g_fox_out=v_g_fox_out, g_chk_out=v_g_chk_out,
             w_out=v_w_out, g_mix_pre=v_g_mix_pre, g_mix_post=v_g_mix_post, g_mem_kv=v_g_mem_kv, w_mq=v_w_mq,
             w_mk=v_w_mk, w_mv=v_w_mv, w_mo=v_w_mo, g_mem_pre=v_g_mem_pre, g_mem_post=v_g_mem_post,
             w_ff1=v_w_ff1, w_ff2=v_w_ff2, g_ff_pre=v_g_ff_pre, g_ff_post=v_g_ff_post)

    def rows(d, k):
        return d[k][0] if k == "rel_bias" else d[k]

    xs, mems, target = x[0], mem[0], loss_target[0]
    S = xs.shape[0]
    sp = {k: rows(w, k) for k in SMALL}
    b_pad = jnp.pad(sp["b_fgt"], ((0, 0), (0, 120)))
    chip = 2 * lax.axis_index("x") + lax.axis_index("y")
    chip_arr = jnp.reshape(chip, (1,)).astype(jnp.int32)
    c_arr = jnp.reshape(lax.axis_index("c"), (1,)).astype(jnp.int32)
    place_arr = jnp.concatenate([chip_arr, c_arr])
    w_in_t, m_in_t, v_in_t = w["w_in"][0].T, m["w_in"][0].T, v["w_in"][0].T
    slab = {"w_in": _cast_slab_call(w_in_t, chip_arr, name="cast_w_in", pad_rows=IN_PAD - IN_SHARD)}

    def gather_ici(names):
        return _ag_ici_task([slab[k] for k in names])

    def pair_add(k, d, r1):
        return _pair_add_call(d, r1, c_arr, name="rs_pair_add_" + k)

    rest, (g_in,) = _cast_slabs_call([w[k][0] for k in BIG[1:]], chip_arr, name="cast_rest",
                                     task=gather_ici(["w_in"]))
    slab.update(zip(BIG[1:], rest))
    h1, (g_in,) = _rms_fwd_call(xs, sp["g_mix_pre"], name="rms_mix_pre", task=_ag_d2d_task([g_in]))
    w_all_t = _reorder_rows_call(g_in.reshape(N_CHIP * IN_PAD, D), True, name="w_in_rows")
    proj, (g_out, g_mq) = _mm_nt(h1, w_all_t, "plain", rows=(0, 3072), name="mm_proj",
                                 task=gather_ici(["w_out", "w_mq"]))
    fl_raw = _mm_nt(h1, w_all_t, "plain", rows=(3072, 128), name="mm_gate", out_dtype=F32, tn=128)
    c_rep, c_t = _fox_prep_call(fl_raw, b_pad, name="fox_prep")
    bias = _chk_bias_call(_rel_table_to_g(sp["rel_bias"]), name="chk_bias")
    mid = ["w_mk", "w_mv", "w_mo", "w_ff1"]
    (yf, lse), got = _fox_fwd_call(proj, c_rep, c_t, name="fox_fwd",
                                   task=_merge_tasks([gather_ici(mid), _ag_d2d_task([g_out, g_mq])]))
    g_mid, (g_out, g_mq) = got[:4], got[4:]
    yc, got = _chk_fwd_call(proj, bias, name="chk_fwd",
                            task=_merge_tasks([gather_ici(["w_ff2"]), _ag_d2d_task(g_mid)]))
    g_ff2, (g_mk, g_mv, g_mo, g_ff1) = got[0], got[1:]
    yn = _mix_norm_fwd_call(yf, yc, sp["g_fox_out"], sp["g_chk_out"], name="mix_norm_fwd")
    z, (g_ff2,) = _mm_nn(yn, g_out, "rows", name="mm_out", out_dtype=F32, task=_ag_d2d_task([g_ff2]))
    x1, h2 = _post_pre_call(xs, z, sp["g_mix_post"], sp["g_mem_pre"], name="post_mix")
    memn = _rms_fwd_call(mems, sp["g_mem_kv"], name="rms_mem_kv")
    q2 = _mm_nn(h2, g_mq, "rows", name="mm_mq")
    k2 = _mm_nn(memn, g_mk, "rows", name="mm_mk")
    v2 = _mm_nn(memn, g_mv, "rows", name="mm_mv")
    o2 = _mem_fwd_call(q2, k2, v2, name="mem_fwd")
    y2 = _mm_nn(o2, g_mo, "rows", name="mm_mo", out_dtype=F32)
    x2, h3 = _post_pre_call(x1, y2, sp["g_mem_post"], sp["g_ff_pre"], name="post_mem")
    act, relu = _mm_nn(h3, g_ff1, "cols", name="mm_ff1", epi="relu2")
    y3 = _mm_nn(act, g_ff2, "rows", name="mm_ff2", out_dtype=F32, tm=1024)
    loss_blk, dx3, dy3, dg_ff_post = _final_call(x2, y3, sp["g_ff_post"], target, name="final")

    d_ff2 = _mm_tn(act, dy3, name="mm_dff2", tk=512, tn=1024).reshape(N_CHIP, D_FF // N_CHIP, D)
    du, (r1,) = _mm_nt(dy3, g_ff2, "rows", name="mm_du", mul2r=relu, task=_rs_pair_task([d_ff2]))
    p_ff2 = pair_add("w_ff2", d_ff2, r1)
    d_ff1 = _mm_tn(h3, du, name="mm_dff1", cols4=True)
    dh3, (r1,) = _mm_nt(du, g_ff1, "cols", name="mm_dh3", out_dtype=F32, tm=1024, task=_rs_pair_task([d_ff1]))
    p_ff1 = pair_add("w_ff1", d_ff1, r1)
    dx2, dy2, dg_ff_pre, dg_mem_post = _bwd_mid_call(dx3, x2, dh3, sp["g_ff_pre"], y2, sp["g_mem_post"], name="bwd_ff")
    d_mo = _mm_tn(o2, dy2, name="mm_dmo").reshape(N_CHIP, D // N_CHIP, D)
    do2 = _mm_nt(dy2, g_mo, "rows", name="mm_do2")
    dq2, dk2, dv2 = _mem_bwd_call(q2, k2, v2, do2, name="mem_bwd")
    d_mq = _mm_tn(h2, dq2, name="mm_dmq").reshape(N_CHIP, D // N_CHIP, D)
    dh2 = _mm_nt(dq2, g_mq, "rows", name="mm_dh2", out_dtype=F32)
    d_mk = _mm_tn(memn, dk2, name="mm_dmk").reshape(N_CHIP, D // N_CHIP, D)
    d_mv = _mm_tn(memn, dv2, name="mm_dmv").reshape(N_CHIP, D // N_CHIP, D)
    dmn_k = _mm_nt(dk2, g_mk, "rows", name="mm_dmemk", out_dtype=F32)
    dmn_v = _mm_nt(dv2, g_mv, "rows", name="mm_dmemv", out_dtype=F32)
    dg_mem_kv = _gain_grad_call(mems, sp["g_mem_kv"], dmn_k, dmn_v, name="gain_mem_kv")
    dx1, dz, dg_mem_pre, dg_mix_post = _bwd_mid_call(dx2, x1, dh2, sp["g_mem_pre"], z, sp["g_mix_post"], name="bwd_mem")
    d_out = _mm_tn(yn, dz, name="mm_dout").reshape(N_CHIP, D // N_CHIP, D)
    late = ["w_mo", "w_mq", "w_mk", "w_mv", "w_out"]
    d_late = [d_mo, d_mq, d_mk, d_mv, d_out]
    dyn, r1_late = _mm_nt(dz, g_out, "rows", name="mm_dyn", out_dtype=F32, task=_rs_pair_task(d_late))
    p_late = [pair_add(k, d, r1) for k, d, r1 in zip(late, d_late, r1_late)]
    dof, doc, delta, dg_fox, dg_chk = _mix_norm_bwd_call(dyn, yf, yc, sp["g_fox_out"], sp["g_chk_out"], name="mix_norm_bwd")
    (dqf, dkf, dvf, dcq, dck), r2_ff = _fox_bwd_call(proj, dof, lse, delta, c_rep, c_t, name="fox_bwd",
                                                      task=_rs_chip_task([p_ff2, p_ff1]))
    (dqc, dkc, dvc, dgrev), r2_late = _chk_bwd_call(proj, doc, bias, name="chk_bwd", task=_rs_chip_task(p_late))
    first = ["w_ff2", "w_ff1"] + late
    f_first = [_chip_sum_call(p, r, place_arr, name="rs_chip_sum_" + k)
               for k, p, r in zip(first, [p_ff2, p_ff1] + p_late, r2_ff + r2_late)]
    dc8 = dcq[:, :, 0:2, :].transpose(0, 2, 1, 3).reshape(8, S) + dck[:, ::HEAD].T
    dc_rows = jnp.concatenate([dc8, jnp.zeros((120, S), F32)], axis=0)
    dfl, db_fgt = _fox_gate_bwd_call(dc_rows, fl_raw, b_pad, name="fox_gate_bwd")
    dproj = jnp.concatenate([dqf, dkf, dvf, dqc, dkc, dvc, dfl], axis=1)
    d_all_t, g_first = _mm_tn(dproj, h1, name="mm_dwin", tk=640, tn=1024, task=_rs_gather_task(f_first))
    grads = dict(zip(first, g_first))
    d_in = _reorder_rows_call(d_all_t, False, name="d_in_rows").reshape(N_CHIP, IN_PAD, D)
    delta_w, new_m, new_v = {}, {}, {}

    def adamw_items(names):
        return [(w[k][0], grads[k], m[k][0], v[k][0]) for k in names]

    upd_late, (r1,) = _adamw_call(adamw_items(late), name="adamw_late", tm=64, task=_rs_pair_task([d_in]))
    p_in = pair_add("w_in", d_in, r1)
    dh1, (r2_in,) = _mm_nn(dproj, w_all_t, "plain", name="mm_dh1", out_dtype=F32, tm=1024,
                           task=_rs_chip_task([p_in]))
    f_in = _chip_sum_call(p_in, r2_in, place_arr, name="rs_chip_sum_w_in")
    upd_ff = _adamw_call(adamw_items(first[:2]), name="adamw_ff")
    for k, res in zip(late + first[:2], upd_late + upd_ff):
        grads[k], delta_w[k], new_m[k], new_v[k] = res
    grad_x, dg_mix_pre = _bwd_last_call(dx1, xs, dh1, sp["g_mix_pre"], name="bwd_mix")

    small_g = {"g_mix_pre": dg_mix_pre, "g_mix_post": dg_mix_post, "g_mem_kv": dg_mem_kv, "g_mem_pre": dg_mem_pre,
               "g_mem_post": dg_mem_post, "g_ff_pre": dg_ff_pre, "g_ff_post": dg_ff_post, "g_fox_out": dg_fox,
               "g_chk_out": dg_chk, "b_fgt": db_fgt,
               "rel_bias": _g_to_rel_table(dgrev[:, 0:2, :].reshape(8, ROLL_W))}
    small, loss_out, (g_w_in,) = _small_call(
        small_g, sp, {k: rows(m, k) for k in SMALL}, {k: rows(v, k) for k in SMALL}, loss_blk,
        _rs_gather_task([f_in]), name="small_allreduce_adamw")
    loss = loss_out[0, 0]
    res = _adamw_cols_call(w_in_t, g_w_in, m_in_t, v_in_t, name="adamw_w_in")
    grads["w_in"], delta_w["w_in"], new_m["w_in"], new_v["w_in"] = (a.T for a in res)
    for k in SMALL:
        vals = small[k]
        if k == "rel_bias":
            vals = tuple(a[None] for a in vals)
        grads[k], delta_w[k], new_m[k], new_v[k] = vals

    def out(d, k):
        return d[k][None] if k in BIG else d[k]

    return (loss, grad_x[None], *[out(grads, k) for k in WEIGHTS], *[out(delta_w, k) for k in WEIGHTS],
            *[out(new_m, k) for k in WEIGHTS], *[out(new_v, k) for k in WEIGHTS])
```

```python
import functools

import jax
import jax.numpy as jnp
from jax import lax
from jax.experimental import pallas as pl
from jax.experimental.pallas import tpu as pltpu

F32 = jnp.float32
BF16 = jnp.bfloat16

D = 1024
HEAD = 64
N_PAIR = 4
D_GRP = 512
CHUNK = 64
LEFT = 8
MAX_REL = 128
N_REL = 2 * MAX_REL + 1
N_MEM = 256
MEM_HEADS = 4
MEM_HD = 256
D_FF = 4096
D_IN = 3080
D_ALL = 3200
EPS = 1e-6
TQ = 256
WIN = (LEFT + TQ // CHUNK) * CHUNK
PADK = LEFT * CHUNK
ROLL_W = 1024
NEG = -1e30
N_CHIP = 4
VMEM_LIMIT = 48 * 1024 * 1024

ADAM_LR = 0.001
ADAM_B1 = 0.9
ADAM_B2 = 0.999
ADAM_EPS = 1e-08
ADAM_WD = 0.01
ADAM_STEP = 10

MESH = pl.DeviceIdType.MESH


def _cparams():
    return pltpu.CompilerParams(vmem_limit_bytes=VMEM_LIMIT)


ANY = pl.BlockSpec(memory_space=pl.ANY)


class _Task:
    def __init__(self, arrays, out_shapes, sems, issue, drain, aliases=None):
        self.arrays, self.out_shapes, self.sems = list(arrays), list(out_shapes), list(sems)
        self.issue, self.drain, self.aliases = issue, drain, dict(aliases or {})


def _merge_tasks(tasks):
    tasks = [t for t in tasks if t is not None]
    if len(tasks) == 1:
        return tasks[0]
    cuts, a, o, s = [], 0, 0, 0
    aliases = {}
    for t in tasks:
        cuts.append((a, o, s))
        aliases.update({a + i: o + j for i, j in t.aliases.items()})
        a, o, s = a + len(t.arrays), o + len(t.out_shapes), s + len(t.sems)

    def part(fn_name):
        def run(ins, outs, sems):
            for t, (a0, o0, s0) in zip(tasks, cuts):
                getattr(t, fn_name)(ins[a0:a0 + len(t.arrays)], outs[o0:o0 + len(t.out_shapes)],
                                    sems[s0:s0 + len(t.sems)])
        return run

    return _Task([x for t in tasks for x in t.arrays], [x for t in tasks for x in t.out_shapes],
                 [x for t in tasks for x in t.sems], part("issue"), part("drain"), aliases)


def _pallas(body, *, grid, in_specs, out_specs, out_shape, name, scratch_shapes=(), task=None, prefetch=0):
    def make(kernel, i_specs, o_specs, o_shape, scratch, aliases):
        if prefetch:
            spec = pltpu.PrefetchScalarGridSpec(num_scalar_prefetch=prefetch, grid=grid, in_specs=i_specs,
                                                out_specs=o_specs, scratch_shapes=scratch)
            return pl.pallas_call(kernel, grid_spec=spec, out_shape=o_shape, input_output_aliases=aliases,
                                  name=name, compiler_params=_cparams())
        return pl.pallas_call(kernel, grid=grid, in_specs=i_specs, out_specs=o_specs, out_shape=o_shape,
                              scratch_shapes=scratch, input_output_aliases=aliases, name=name,
                              compiler_params=_cparams())

    if task is None:
        return make(body, list(in_specs), out_specs, out_shape, list(scratch_shapes), {})
    single = not isinstance(out_shape, (tuple, list))
    o_shapes = [out_shape] if single else list(out_shape)
    o_specs = [out_specs] if single else list(out_specs)
    n_in, n_out, n_scr = len(in_specs), len(o_shapes), len(scratch_shapes)
    t_in, t_out = len(task.arrays), len(task.out_shapes)

    def carried(*refs):
        cut = [prefetch, n_in, t_in, n_out, t_out, n_scr]
        parts, p = [], 0
        for c in cut:
            parts.append(refs[p:p + c])
            p += c
        scalars, ins, tins, outs, touts, scr = parts
        tsems = refs[p:]
        ids = [pl.program_id(a) for a in range(len(grid))]
        first = functools.reduce(jnp.logical_and, [i == 0 for i in ids])
        last = functools.reduce(jnp.logical_and, [i == g - 1 for i, g in zip(ids, grid)])

        @pl.when(first)
        def _():
            task.issue(tins, touts, tsems)
        body(*scalars, *ins, *outs, *scr)

        @pl.when(last)
        def _():
            task.drain(tins, touts, tsems)

    call = make(carried, list(in_specs) + [ANY] * t_in, o_specs + [ANY] * t_out,
                o_shapes + list(task.out_shapes), list(scratch_shapes) + list(task.sems),
                {prefetch + n_in + i: n_out + j for i, j in task.aliases.items()})

    def run(*args):
        res = call(*args, *task.arrays)
        outs = res[:n_out]
        return (outs[0] if single else tuple(outs)), list(res[n_out:])

    return run


def _dot(a, b):
    return jnp.dot(a, b, preferred_element_type=F32)


def _dot_nt(a, b):
    return lax.dot_general(a, b, (((1,), (1,)), ((), ())), preferred_element_type=F32)


def _dot_tn(a, b):
    return lax.dot_general(a, b, (((0,), (0,)), ((), ())), preferred_element_type=F32)


def _split3(x):
    hi = x.astype(BF16)
    r1 = x - hi.astype(F32)
    mid = r1.astype(BF16)
    lo = (r1 - mid.astype(F32)).astype(BF16)
    return hi, mid, lo


def _dot3(x, m01):
    hi, mid, lo = _split3(x)
    return _dot(hi, m01) + _dot(mid, m01) + _dot(lo, m01)


def _dot3_l(m01, x):
    hi, mid, lo = _split3(x)
    return _dot(m01, hi) + _dot(m01, mid) + _dot(m01, lo)


def _mm_nn(a, b, kind, *, name, out_dtype=BF16, tm=2048, tn=512, epi=None, task=None):
    M, K = a.shape
    if kind == "plain":
        N = b.shape[1]
        b_spec = pl.BlockSpec((K, tn), lambda m, n: (0, n))
    elif kind == "rows":
        N = b.shape[2]
        b_spec = pl.BlockSpec((N_CHIP, K // N_CHIP, tn), lambda m, n: (0, 0, n))
    else:
        nq = b.shape[2]
        N = N_CHIP * nq
        per = nq // tn
        b_spec = pl.BlockSpec((None, K, tn), lambda m, n: (n // per, 0, n % per))
    tm = min(tm, M)
    kq = K // N_CHIP

    def body(a_ref, b_ref, *o_refs):
        if kind == "rows":
            acc = _dot(a_ref[:, 0:kq], b_ref[0])
            for j in range(1, N_CHIP):
                acc += _dot(a_ref[:, j * kq:(j + 1) * kq], b_ref[j])
        else:
            acc = _dot(a_ref[...], b_ref[...])
        if epi == "relu2":
            r = jnp.maximum(acc, 0.0)
            o_refs[0][...] = (r * r).astype(BF16)
            o_refs[1][...] = r.astype(BF16)
        else:
            o_refs[0][...] = acc.astype(out_dtype)

    o_spec = pl.BlockSpec((tm, tn), lambda m, n: (m, n))
    if epi == "relu2":
        out_shape = (jax.ShapeDtypeStruct((M, N), BF16), jax.ShapeDtypeStruct((M, N), BF16))
        out_specs = (o_spec, o_spec)
    else:
        out_shape = jax.ShapeDtypeStruct((M, N), out_dtype)
        out_specs = o_spec
    return _pallas(
        body, grid=(M // tm, N // tn),
        in_specs=[pl.BlockSpec((tm, K), lambda m, n: (m, 0)), b_spec],
        out_specs=out_specs, out_shape=out_shape, name=name, task=task,
    )(a, b)


def _mm_nt(a, b, kind, *, name, out_dtype=BF16, tm=2048, tn=512, mul2r=None, task=None, rows=None):
    M, K = a.shape
    if kind == "plain":
        first, N = rows if rows is not None else (0, b.shape[0])
        n0 = first // tn
        b_spec = pl.BlockSpec((tn, K), lambda m, n: (n0 + n, 0))
    elif kind == "rows":
        nq = b.shape[1]
        N = N_CHIP * nq
        tn = min(tn, nq)
        per = nq // tn
        b_spec = pl.BlockSpec((None, tn, K), lambda m, n: (n // per, n % per, 0))
    else:
        N = b.shape[1]
        b_spec = pl.BlockSpec((N_CHIP, tn, K // N_CHIP), lambda m, n: (0, n, 0))
    tm = min(tm, M)
    kq = K // N_CHIP

    def body(a_ref, b_ref, *rest):
        o_ref = rest[-1]
        if kind == "cols":
            acc = _dot_nt(a_ref[:, 0:kq], b_ref[0])
            for j in range(1, N_CHIP):
                acc += _dot_nt(a_ref[:, j * kq:(j + 1) * kq], b_ref[j])
        else:
            acc = _dot_nt(a_ref[...], b_ref[...])
        if mul2r is not None:
            acc = acc * (2.0 * rest[0][...].astype(F32))
        o_ref[...] = acc.astype(out_dtype)

    in_specs = [pl.BlockSpec((tm, K), lambda m, n: (m, 0)), b_spec]
    args = [a, b]
    if mul2r is not None:
        in_specs.append(pl.BlockSpec((tm, tn), lambda m, n: (m, n)))
        args.append(mul2r)
    return _pallas(
        body, grid=(M // tm, N // tn), in_specs=in_specs,
        out_specs=pl.BlockSpec((tm, tn), lambda m, n: (m, n)),
        out_shape=jax.ShapeDtypeStruct((M, N), out_dtype), name=name, task=task,
    )(*args)


def _mm_tn(a, b, *, name, out_dtype=BF16, tk=1024, tn=512, cols4=False, task=None):
    M, K1 = a.shape
    N = b.shape[1]
    tk = min(tk, K1)
    tn = min(tn, N)

    def body(a_ref, b_ref, o_ref):
        o_ref[...] = _dot_tn(a_ref[...], b_ref[...]).astype(out_dtype)

    if cols4:
        per = (N // N_CHIP) // tn
        out_shape = jax.ShapeDtypeStruct((N_CHIP, K1, N // N_CHIP), out_dtype)
        o_spec = pl.BlockSpec((None, tk, tn), lambda k, n: (n // per, k, n % per))
    else:
        out_shape = jax.ShapeDtypeStruct((K1, N), out_dtype)
        o_spec = pl.BlockSpec((tk, tn), lambda k, n: (k, n))
    return _pallas(
        body, grid=(K1 // tk, N // tn),
        in_specs=[pl.BlockSpec((M, tk), lambda k, n: (0, k)), pl.BlockSpec((M, tn), lambda k, n: (0, n))],
        out_specs=o_spec, out_shape=out_shape, name=name, task=task,
    )(a, b)


def _rms(x, g):
    r = lax.rsqrt(jnp.mean(x * x, axis=-1, keepdims=True) + EPS)
    return x * r * g


def _rms_bwd(x, g, dy):
    r = lax.rsqrt(jnp.mean(x * x, axis=-1, keepdims=True) + EPS)
    xh = x * r
    dg = jnp.sum(dy * xh, axis=0, keepdims=True)
    dxh = dy * g
    dx = r * (dxh - xh * jnp.mean(dxh * xh, axis=-1, keepdims=True))
    return dx, dg


def _row_spec(tm, n):
    return pl.BlockSpec((tm, n), lambda i: (i, 0))


def _vec_spec(n):
    return pl.BlockSpec((1, n), lambda i: (0, 0))


def _acc_spec(n):
    return pl.BlockSpec((8, n), lambda i: (0, 0))


def _acc_add(ref, row, i):
    @pl.when(i == 0)
    def _():
        ref[...] = jnp.zeros_like(ref)
    ref[0:1, :] += row


def _rms_fwd_call(x, g, *, name, tm=512, task=None):
    M, n = x.shape
    tm = min(tm, M)

    def body(x_ref, g_ref, h_ref):
        h_ref[...] = _rms(x_ref[...], g_ref[...]).astype(BF16)

    return _pallas(
        body, grid=(M // tm,), in_specs=[_row_spec(tm, n), _vec_spec(n)], out_specs=_row_spec(tm, n),
        out_shape=jax.ShapeDtypeStruct((M, n), BF16), name=name, task=task,
    )(x, g)


def _post_pre_call(xres, z, g_post, g_pre, *, name, tm=512):
    M, n = xres.shape

    def body(x_ref, z_ref, gp_ref, gn_ref, xo_ref, h_ref):
        xn = x_ref[...] + _rms(z_ref[...], gp_ref[...])
        xo_ref[...] = xn
        h_ref[...] = _rms(xn, gn_ref[...]).astype(BF16)

    return pl.pallas_call(
        body, grid=(M // tm,),
        in_specs=[_row_spec(tm, n), _row_spec(tm, n), _vec_spec(n), _vec_spec(n)],
        out_specs=(_row_spec(tm, n), _row_spec(tm, n)),
        out_shape=(jax.ShapeDtypeStruct((M, n), F32), jax.ShapeDtypeStruct((M, n), BF16)),
        name=name, compiler_params=_cparams(),
    )(xres, z, g_post, g_pre)


def _final_call(x2, y3, g_post, target, *, name, tm=512):
    M, n = x2.shape

    def body(x_ref, y_ref, g_ref, t_ref, loss_ref, dx_ref, dy_ref, dg_ref):
        i = pl.program_id(0)
        y = y_ref[...]
        g = g_ref[...]
        diff = x_ref[...] + _rms(y, g) - t_ref[...]
        part = 0.5 * jnp.sum(jnp.sum(diff * diff, axis=1, keepdims=True), axis=0, keepdims=True) / n

        @pl.when(i == 0)
        def _():
            loss_ref[...] = jnp.zeros_like(loss_ref)
        loss_ref[...] += jnp.broadcast_to(part, loss_ref.shape)
        dx = diff / n
        dx_ref[...] = dx
        dy, dg = _rms_bwd(y, g, dx)
        dy_ref[...] = dy.astype(BF16)
        _acc_add(dg_ref, dg, i)

    return pl.pallas_call(
        body, grid=(M // tm,),
        in_specs=[_row_spec(tm, n), _row_spec(tm, n), _vec_spec(n), _row_spec(tm, n)],
        out_specs=(pl.BlockSpec((8, 128), lambda i: (0, 0)), _row_spec(tm, n), _row_spec(tm, n), _acc_spec(n)),
        out_shape=(jax.ShapeDtypeStruct((8, 128), F32), jax.ShapeDtypeStruct((M, n), F32),
                   jax.ShapeDtypeStruct((M, n), BF16), jax.ShapeDtypeStruct((8, n), F32)),
        name=name, compiler_params=_cparams(),
    )(x2, y3, g_post, target)


def _bwd_mid_call(dx_in, x, dh, g_pre, y, g_post, *, name, tm=512):
    M, n = x.shape

    def body(dxi_ref, x_ref, dh_ref, gpre_ref, y_ref, gpost_ref, dx_ref, dy_ref, dgpre_ref, dgpost_ref):
        i = pl.program_id(0)
        d1, dg1 = _rms_bwd(x_ref[...], gpre_ref[...], dh_ref[...])
        dx = dxi_ref[...] + d1
        dx_ref[...] = dx
        dy, dg2 = _rms_bwd(y_ref[...], gpost_ref[...], dx)
        dy_ref[...] = dy.astype(BF16)
        _acc_add(dgpre_ref, dg1, i)
        _acc_add(dgpost_ref, dg2, i)

    return pl.pallas_call(
        body, grid=(M // tm,),
        in_specs=[_row_spec(tm, n), _row_spec(tm, n), _row_spec(tm, n), _vec_spec(n), _row_spec(tm, n), _vec_spec(n)],
        out_specs=(_row_spec(tm, n), _row_spec(tm, n), _acc_spec(n), _acc_spec(n)),
        out_shape=(jax.ShapeDtypeStruct((M, n), F32), jax.ShapeDtypeStruct((M, n), BF16),
                   jax.ShapeDtypeStruct((8, n), F32), jax.ShapeDtypeStruct((8, n), F32)),
        name=name, compiler_params=_cparams(),
    )(dx_in, x, dh, g_pre, y, g_post)


def _bwd_last_call(dx_in, x, dh, g_pre, *, name, tm=512, task=None):
    M, n = x.shape

    def body(dxi_ref, x_ref, dh_ref, g_ref, dx_ref, dg_ref):
        i = pl.program_id(0)
        d1, dg1 = _rms_bwd(x_ref[...], g_ref[...], dh_ref[...])
        dx_ref[...] = dxi_ref[...] + d1
        _acc_add(dg_ref, dg1, i)

    return _pallas(
        body, grid=(M // tm,),
        in_specs=[_row_spec(tm, n), _row_spec(tm, n), _row_spec(tm, n), _vec_spec(n)],
        out_specs=(_row_spec(tm, n), _acc_spec(n)),
        out_shape=(jax.ShapeDtypeStruct((M, n), F32), jax.ShapeDtypeStruct((8, n), F32)),
        name=name, task=task,
    )(dx_in, x, dh, g_pre)


def _gain_grad_call(x, g, dy_a, dy_b, *, name):
    M, n = x.shape

    def body(x_ref, g_ref, a_ref, b_ref, dg_ref):
        _, dg = _rms_bwd(x_ref[...], g_ref[...], a_ref[...] + b_ref[...])
        dg_ref[...] = jnp.zeros_like(dg_ref)
        dg_ref[0:1, :] = dg

    return pl.pallas_call(
        body, grid=(1,),
        in_specs=[_row_spec(M, n), _vec_spec(n), _row_spec(M, n), _row_spec(M, n)],
        out_specs=_acc_spec(n), out_shape=jax.ShapeDtypeStruct((8, n), F32),
        name=name, compiler_params=_cparams(),
    )(x, g, dy_a, dy_b)


def _head_group_matrix():
    a = lax.broadcasted_iota(jnp.int32, (D_GRP, D_GRP), 0) // HEAD
    b = lax.broadcasted_iota(jnp.int32, (D_GRP, D_GRP), 1) // HEAD
    return jnp.where(a == b, 1.0, 0.0).astype(BF16)


def _mix_norm_fwd_call(yf, yc, gf, gc, *, name, tm=512):
    M = yf.shape[0]

    def body(yf_ref, yc_ref, gf_ref, gc_ref, o_ref):
        o_ref[:, 0:D_GRP] = _rms(yf_ref[...], gf_ref[...]).astype(BF16)
        o_ref[:, D_GRP:D] = _rms(yc_ref[...], gc_ref[...]).astype(BF16)

    return pl.pallas_call(
        body, grid=(M // tm,),
        in_specs=[_row_spec(tm, D_GRP), _row_spec(tm, D_GRP), _vec_spec(D_GRP), _vec_spec(D_GRP)],
        out_specs=_row_spec(tm, D), out_shape=jax.ShapeDtypeStruct((M, D), BF16),
        name=name, compiler_params=_cparams(),
    )(yf, yc, gf, gc)


def _mix_norm_bwd_call(dyn, yf, yc, gf, gc, *, name, tm=2 * TQ):
    M = yf.shape[0]

    def body(dyn_ref, yf_ref, yc_ref, gf_ref, gc_ref, dof_ref, doc_ref, delta_ref, dgf_ref, dgc_ref):
        i = pl.program_id(0)
        yf_ = yf_ref[...]
        dof, dgf = _rms_bwd(yf_, gf_ref[...], dyn_ref[:, 0:D_GRP])
        doc, dgc = _rms_bwd(yc_ref[...], gc_ref[...], dyn_ref[:, D_GRP:D])
        dof_b = dof.astype(BF16)
        dof_ref[...] = dof_b
        doc_ref[...] = doc.astype(BF16)
        prod = dof_b.astype(F32) * yf_
        hi = prod.astype(BF16)
        lo = (prod - hi.astype(F32)).astype(BF16)
        grp = _head_group_matrix()
        delta = _dot(hi, grp) + _dot(lo, grp)
        for b in range(tm // TQ):
            delta_ref[b] = delta[b * TQ:(b + 1) * TQ, :].T
        _acc_add(dgf_ref, dgf, i)
        _acc_add(dgc_ref, dgc, i)

    return pl.pallas_call(
        body, grid=(M // tm,),
        in_specs=[_row_spec(tm, D), _row_spec(tm, D_GRP), _row_spec(tm, D_GRP), _vec_spec(D_GRP), _vec_spec(D_GRP)],
        out_specs=(_row_spec(tm, D_GRP), _row_spec(tm, D_GRP),
                   pl.BlockSpec((tm // TQ, D_GRP, TQ), lambda i: (i, 0, 0)), _acc_spec(D_GRP), _acc_spec(D_GRP)),
        out_shape=(jax.ShapeDtypeStruct((M, D_GRP), BF16), jax.ShapeDtypeStruct((M, D_GRP), BF16),
                   jax.ShapeDtypeStruct((M // TQ, D_GRP, TQ), F32), jax.ShapeDtypeStruct((8, D_GRP), F32),
                   jax.ShapeDtypeStruct((8, D_GRP), F32)),
        name=name, compiler_params=_cparams(),
    )(dyn, yf, yc, gf, gc)


def _tri(n, lower_incl):
    a = lax.broadcasted_iota(jnp.int32, (n, n), 0)
    b = lax.broadcasted_iota(jnp.int32, (n, n), 1)
    return jnp.where(a >= b, 1.0, 0.0).astype(BF16) if lower_incl else jnp.where(a <= b, 1.0, 0.0).astype(BF16)


def _fox_prep_call(fl_raw, b_pad, *, name):
    S = fl_raw.shape[0]
    nb = S // TQ

    def body(fl_ref, b_ref, crep_ref, ct_ref, carry_ref):
        i = pl.program_id(0)

        @pl.when(i == 0)
        def _():
            carry_ref[...] = jnp.zeros_like(carry_ref)
        logf = jax.nn.log_sigmoid(fl_ref[...] + b_ref[...])
        cb = _dot3_l(_tri(TQ, True), logf) + carry_ref[0:1, :]
        carry_ref[0:1, :] = cb[TQ - 1:TQ, :]
        a = lax.broadcasted_iota(jnp.int32, (128, D_GRP), 0)
        b = lax.broadcasted_iota(jnp.int32, (128, D_GRP), 1) // HEAD
        expand = jnp.where(a == b, 1.0, 0.0).astype(BF16)
        crep = _dot3(cb, expand)
        crep_ref[...] = crep
        ct_ref[...] = crep.T

    return pl.pallas_call(
        body, grid=(nb,),
        in_specs=[_row_spec(TQ, 128), _vec_spec(128)],
        out_specs=(_row_spec(TQ, D_GRP), pl.BlockSpec((None, D_GRP, TQ), lambda i: (i, 0, 0))),
        out_shape=(jax.ShapeDtypeStruct((S, D_GRP), F32), jax.ShapeDtypeStruct((nb, D_GRP, TQ), F32)),
        scratch_shapes=[pltpu.VMEM((8, 128), F32)],
        name=name, compiler_params=_cparams(),
    )(fl_raw, b_pad)


def _lane_masks():
    lane = lax.broadcasted_iota(jnp.int32, (1, 128), 1)
    return lane < HEAD, lane >= HEAD


def _fox_fwd_call(proj, c_rep, c_t, *, name, task=None):
    S = proj.shape[0]
    nq = S // TQ
    scale = HEAD ** -0.5

    def body(q_ref, k_ref, v_ref, c_ref, ct_ref, o_ref, lse_ref):
        i = pl.program_id(1)
        m_lo, m_hi = _lane_masks()
        masks = (m_lo, m_hi)
        q = q_ref[...] * scale
        qm = [jnp.where(mk, q, jnp.zeros_like(q)) for mk in masks]
        cq = c_ref[...]
        cqh = [cq[:, 0:1], cq[:, HEAD:HEAD + 1]]
        row = lax.broadcasted_iota(jnp.int32, (TQ, TQ), 0)
        col = lax.broadcasted_iota(jnp.int32, (TQ, TQ), 1)

        def scores(j):
            start = pl.multiple_of(j * TQ, TQ)
            k = k_ref[pl.ds(start, TQ), :]
            ct = ct_ref[j]
            return tuple(_dot_nt(qm[h], k) + (cqh[h] - ct[HEAD * h:HEAD * h + 1, :]) for h in range(2))

        def update(j, ss, state, masked):
            ms, ls, acc = state
            start = pl.multiple_of(j * TQ, TQ)
            v = v_ref[pl.ds(start, TQ), :]
            new_m, new_l, pv, alpha_l = [], [], [], []
            for h in range(2):
                s = ss[h]
                if masked:
                    s = jnp.where(row >= col, s, NEG)
                mn = jnp.maximum(ms[h], jnp.max(s, axis=1, keepdims=True))
                alpha = jnp.exp(ms[h] - mn)
                p = jnp.exp(s - mn)
                new_l.append(alpha * ls[h] + jnp.sum(p, axis=1, keepdims=True))
                new_m.append(mn)
                alpha_l.append(alpha)
                pv.append(_dot(p.astype(BF16), jnp.where(masks[h], v, jnp.zeros_like(v))))
            alpha_lane = jnp.where(m_lo, alpha_l[0], alpha_l[1])
            acc = acc * alpha_lane + pv[0] + pv[1]
            return (tuple(new_m), tuple(new_l), acc)

        def step(j, carry):
            ss, state = carry
            return (scores(j + 1), update(j, ss, state, False))

        init = ((jnp.full((TQ, 1), NEG, F32),) * 2, (jnp.zeros((TQ, 1), F32),) * 2, jnp.zeros((TQ, 128), F32))
        ss, state = lax.fori_loop(0, i, step, (scores(0), init))
        ms, ls, acc = update(i, ss, state, True)
        l_lane = jnp.where(m_lo, ls[0], ls[1])
        o_ref[...] = acc / l_lane
        lse_ref[...] = jnp.where(m_lo, ms[0] + jnp.log(ls[0]), ms[1] + jnp.log(ls[1])).T

    return _pallas(
        body, grid=(N_PAIR, nq),
        in_specs=[pl.BlockSpec((TQ, 128), lambda p, i: (i, p)),
                  pl.BlockSpec((S, 128), lambda p, i: (0, N_PAIR + p)),
                  pl.BlockSpec((S, 128), lambda p, i: (0, 2 * N_PAIR + p)),
                  pl.BlockSpec((TQ, 128), lambda p, i: (i, p)),
                  pl.BlockSpec((nq, 128, TQ), lambda p, i: (0, p, 0))],
        out_specs=(pl.BlockSpec((TQ, 128), lambda p, i: (i, p)), pl.BlockSpec((None, 128, TQ), lambda p, i: (i, p, 0))),
        out_shape=(jax.ShapeDtypeStruct((S, D_GRP), F32), jax.ShapeDtypeStruct((nq, D_GRP, TQ), F32)),
        name=name, task=task,
    )(proj, proj, proj, c_rep, c_t)


def _fox_bwd_call(proj, do, lse_t, delta_t, c_rep, c_t, *, name, task=None):
    S = proj.shape[0]
    nq = S // TQ
    scale = HEAD ** -0.5

    def body(q_ref, k_ref, v_ref, do_ref, lse_ref, dl_ref, ck_ref, ct_ref,
             dq_ref, dk_ref, dv_ref, dcq_ref, dck_ref, dqa_ref):
        j = pl.program_id(1)
        m_lo, m_hi = _lane_masks()
        masks = (m_lo, m_hi)

        @pl.when(j == 0)
        def _():
            dqa_ref[...] = jnp.zeros_like(dqa_ref)
            dcq_ref[...] = jnp.zeros_like(dcq_ref)
        k = k_ref[...]
        v = v_ref[...]
        km = [jnp.where(mk, k, jnp.zeros_like(k)) for mk in masks]
        ck = ck_ref[...]
        krow = lax.broadcasted_iota(jnp.int32, (TQ, TQ), 0)
        qcol = lax.broadcasted_iota(jnp.int32, (TQ, TQ), 1)

        def probs(i):
            start = pl.multiple_of(i * TQ, TQ)
            q = q_ref[pl.ds(start, TQ), :]
            do = do_ref[pl.ds(start, TQ), :]
            lse = lse_ref[i]
            cq = ct_ref[i]
            out = []
            for h in range(2):
                lo = HEAD * h
                qm = jnp.where(masks[h], q * scale, jnp.zeros_like(q))
                dom = jnp.where(masks[h], do, jnp.zeros_like(do))
                st = _dot_nt(k, qm) + (cq[lo:lo + 1, :] - ck[:, lo:lo + 1])
                out.append((jnp.exp(st - lse[lo:lo + 1, :]), _dot_nt(v, dom)))
            return tuple(out)

        def update(i, pd, carry, masked):
            dk, dv, dck = carry
            start = pl.multiple_of(i * TQ, TQ)
            q = q_ref[pl.ds(start, TQ), :]
            do = do_ref[pl.ds(start, TQ), :]
            dl = dl_ref[i]
            dq = jnp.zeros((TQ, 128), F32)
            new_dck = []
            for h in range(2):
                lo = HEAD * h
                qm = jnp.where(masks[h], q, jnp.zeros_like(q))
                dom = jnp.where(masks[h], do, jnp.zeros_like(do))
                pt, dpt = pd[h]
                if masked:
                    pt = jnp.where(qcol >= krow, pt, 0.0)
                dst = pt * (dpt - dl[lo:lo + 1, :])
                dcq_ref[i, h:h + 1, :] += jnp.sum(dst, axis=0, keepdims=True)
                new_dck.append(dck[h] + jnp.sum(dst, axis=1, keepdims=True))
                dsb = (dst * scale).astype(BF16)
                dv = dv + _dot(pt.astype(BF16), dom)
                dk = dk + _dot(dsb, qm)
                dq = dq + _dot_tn(dsb, km[h])
            dqa_ref[pl.ds(start, TQ), :] += dq
            return (dk, dv, tuple(new_dck))

        def step(i, carry):
            pd, sums = carry
            return (probs(jnp.minimum(i + 1, nq - 1)), update(i, pd, sums, False))

        init = (jnp.zeros((TQ, 128), F32), jnp.zeros((TQ, 128), F32), (jnp.zeros((TQ, 1), F32),) * 2)
        first = probs(j)
        second = probs(jnp.minimum(j + 1, nq - 1))
        _, (dk, dv, dck) = lax.fori_loop(j + 1, nq, step, (second, update(j, first, init, True)))
        dk_ref[...] = dk.astype(BF16)
        dv_ref[...] = dv.astype(BF16)
        dck_ref[...] = -jnp.where(m_lo, dck[0], dck[1])

        @pl.when(j == nq - 1)
        def _():
            dq_ref[...] = dqa_ref[...].astype(BF16)

    res = lambda p, j: (0, p)
    stat = pl.BlockSpec((nq, 128, TQ), lambda p, j: (0, p, 0))
    blk = pl.BlockSpec((TQ, 128), lambda p, j: (j, p))
    return _pallas(
        body, grid=(N_PAIR, nq), task=task,
        in_specs=[pl.BlockSpec((S, 128), res),
                  pl.BlockSpec((TQ, 128), lambda p, j: (j, N_PAIR + p)),
                  pl.BlockSpec((TQ, 128), lambda p, j: (j, 2 * N_PAIR + p)),
                  pl.BlockSpec((S, 128), res), stat, stat, blk, stat],
        out_specs=(pl.BlockSpec((S, 128), res), blk, blk,
                   pl.BlockSpec((None, nq, 8, TQ), lambda p, j: (p, 0, 0, 0)), blk),
        out_shape=(jax.ShapeDtypeStruct((S, D_GRP), BF16), jax.ShapeDtypeStruct((S, D_GRP), BF16),
                   jax.ShapeDtypeStruct((S, D_GRP), BF16), jax.ShapeDtypeStruct((N_PAIR, nq, 8, TQ), F32),
                   jax.ShapeDtypeStruct((S, D_GRP), F32)),
        scratch_shapes=[pltpu.VMEM((S, 128), F32)],
        name=name,
    )(proj, proj, proj, do, lse_t, delta_t, c_rep, c_t)


def _fox_gate_bwd_call(dc_rows, fl_raw, b_pad, *, name):
    S = fl_raw.shape[0]
    nb = S // TQ

    def body(dc_ref, fl_ref, b_ref, dfl_ref, db_ref, carry_ref):
        i = pl.program_id(0)

        @pl.when(i == 0)
        def _():
            carry_ref[...] = jnp.zeros_like(carry_ref)
        rc = _dot3(dc_ref[...], _tri(TQ, True)) + carry_ref[:, 0:1]
        carry_ref[...] = jnp.broadcast_to(rc[:, 0:1], carry_ref.shape)
        fl = fl_ref[...] + b_ref[...]
        dfl = rc.T * jax.nn.sigmoid(-fl)
        dfl_ref[...] = dfl.astype(BF16)
        _acc_add(db_ref, jnp.sum(dfl, axis=0, keepdims=True), i)

    rev = lambda i: (nb - 1 - i, 0)
    return pl.pallas_call(
        body, grid=(nb,),
        in_specs=[pl.BlockSpec((128, TQ), lambda i: (0, nb - 1 - i)), pl.BlockSpec((TQ, 128), rev), _vec_spec(128)],
        out_specs=(pl.BlockSpec((TQ, 128), rev), _acc_spec(128)),
        out_shape=(jax.ShapeDtypeStruct((S, 128), BF16), jax.ShapeDtypeStruct((8, 128), F32)),
        scratch_shapes=[pltpu.VMEM((128, 128), F32)],
        name=name, compiler_params=_cparams(),
    )(dc_rows, fl_raw, b_pad)


def _chk_bias_call(g_rev, *, name):
    def body(g_ref, o_ref):
        x = jnp.broadcast_to(g_ref[...], (TQ, ROLL_W))
        rolled = pltpu.roll(x, ROLL_W - (TQ - 1), 1, stride=1, stride_axis=0)
        qc = lax.broadcasted_iota(jnp.int32, (TQ, WIN), 0) // CHUNK
        kc = lax.broadcasted_iota(jnp.int32, (TQ, WIN), 1) // CHUNK
        band = (kc >= qc) & (kc <= qc + LEFT)
        o_ref[...] = jnp.where(band, rolled[:, 0:WIN], NEG)

    return pl.pallas_call(
        body, grid=(8,),
        in_specs=[pl.BlockSpec((None, 1, ROLL_W), lambda h: (h, 0, 0))],
        out_specs=pl.BlockSpec((None, TQ, WIN), lambda h: (h, 0, 0)),
        out_shape=jax.ShapeDtypeStruct((8, TQ, WIN), F32), name=name, compiler_params=_cparams(),
    )(g_rev.reshape(8, 1, ROLL_W))


def _chk_scores(i, qm, kwin, bias, scale):
    s = _dot_nt(qm * scale, kwin) + bias
    kc = lax.broadcasted_iota(jnp.int32, (TQ, WIN), 1) // CHUNK
    return jnp.where(kc + i * (TQ // CHUNK) >= LEFT, s, NEG)


def _chk_fwd_call(proj, bias, *, name, task=None):
    S = proj.shape[0]
    nq = S // TQ
    scale = HEAD ** -0.5

    def body(q_ref, k_ref, v_ref, b_ref, o_ref, kp_ref, vp_ref):
        i = pl.program_id(1)

        @pl.when(i == 0)
        def _():
            kp_ref[0:PADK, :] = jnp.zeros((PADK, 128), BF16)
            vp_ref[0:PADK, :] = jnp.zeros((PADK, 128), BF16)
            kp_ref[PADK:PADK + S, :] = k_ref[...]
            vp_ref[PADK:PADK + S, :] = v_ref[...]
        masks = _lane_masks()
        q = q_ref[...]
        start = pl.multiple_of(i * TQ, TQ)
        kwin = kp_ref[pl.ds(start, WIN), :]
        vwin = vp_ref[pl.ds(start, WIN), :]
        ss = [_chk_scores(i, jnp.where(masks[h], q, jnp.zeros_like(q)), kwin, b_ref[h], scale) for h in range(2)]
        ps = []
        for s in ss:
            p = jnp.exp(s - jnp.max(s, axis=1, keepdims=True))
            ps.append((p / jnp.sum(p, axis=1, keepdims=True)).astype(BF16))
        o_ref[...] = (_dot(ps[0], jnp.where(masks[0], vwin, jnp.zeros_like(vwin)))
                      + _dot(ps[1], jnp.where(masks[1], vwin, jnp.zeros_like(vwin))))

    c0 = 3 * N_PAIR
    return _pallas(
        body, grid=(N_PAIR, nq), task=task,
        in_specs=[pl.BlockSpec((TQ, 128), lambda p, i: (i, c0 + p)),
                  pl.BlockSpec((S, 128), lambda p, i: (0, c0 + N_PAIR + p)),
                  pl.BlockSpec((S, 128), lambda p, i: (0, c0 + 2 * N_PAIR + p)),
                  pl.BlockSpec((2, TQ, WIN), lambda p, i: (p, 0, 0))],
        out_specs=pl.BlockSpec((TQ, 128), lambda p, i: (i, p)),
        out_shape=jax.ShapeDtypeStruct((S, D_GRP), F32),
        scratch_shapes=[pltpu.VMEM((S + PADK, 128), BF16), pltpu.VMEM((S + PADK, 128), BF16)],
        name=name,
    )(proj, proj, proj, bias)


def _chk_bwd_call(proj, do, bias, *, name, task=None):
    S = proj.shape[0]
    nq = S // TQ
    scale = HEAD ** -0.5

    def body(q_ref, k_ref, v_ref, do_ref, b_ref, dq_ref, dk_ref, dv_ref, dg_ref, kp_ref, vp_ref, dkp_ref, dvp_ref, db_ref):
        i = pl.program_id(1)

        @pl.when(i == 0)
        def _():
            kp_ref[0:PADK, :] = jnp.zeros((PADK, 128), BF16)
            vp_ref[0:PADK, :] = jnp.zeros((PADK, 128), BF16)
            kp_ref[PADK:PADK + S, :] = k_ref[...]
            vp_ref[PADK:PADK + S, :] = v_ref[...]
            dkp_ref[...] = jnp.zeros_like(dkp_ref)
            dvp_ref[...] = jnp.zeros_like(dvp_ref)
            db_ref[...] = jnp.zeros_like(db_ref)
        masks = _lane_masks()
        q = q_ref[...]
        dout = do_ref[...]
        start = pl.multiple_of(i * TQ, TQ)
        kwin = kp_ref[pl.ds(start, WIN), :]
        vwin = vp_ref[pl.ds(start, WIN), :]
        qm = [jnp.where(mk, q, jnp.zeros_like(q)) for mk in masks]
        dom = [jnp.where(mk, dout, jnp.zeros_like(dout)) for mk in masks]
        ss = [_chk_scores(i, qm[h], kwin, b_ref[h], scale) for h in range(2)]
        dps = [_dot_nt(dom[h], vwin) for h in range(2)]
        pbs, dsbs = [], []
        for h in range(2):
            p = jnp.exp(ss[h] - jnp.max(ss[h], axis=1, keepdims=True))
            p = p / jnp.sum(p, axis=1, keepdims=True)
            ds = p * (dps[h] - jnp.sum(p * dps[h], axis=1, keepdims=True))
            db_ref[h] += ds
            pbs.append(p.astype(BF16))
            dsbs.append((ds * scale).astype(BF16))
        dq_ref[...] = (_dot(dsbs[0], jnp.where(masks[0], kwin, jnp.zeros_like(kwin)))
                       + _dot(dsbs[1], jnp.where(masks[1], kwin, jnp.zeros_like(kwin)))).astype(BF16)
        dkp_ref[pl.ds(start, WIN), :] += _dot_tn(dsbs[0], qm[0]) + _dot_tn(dsbs[1], qm[1])
        dvp_ref[pl.ds(start, WIN), :] += _dot_tn(pbs[0], dom[0]) + _dot_tn(pbs[1], dom[1])

        @pl.when(i == nq - 1)
        def _():
            dk_ref[...] = dkp_ref[PADK:PADK + S, :].astype(BF16)
            dv_ref[...] = dvp_ref[PADK:PADK + S, :].astype(BF16)
            a = lax.broadcasted_iota(jnp.int32, (TQ, TQ), 0)
            b = lax.broadcasted_iota(jnp.int32, (TQ, TQ), 1)
            flip = jnp.where(a + b == TQ - 1, 1.0, 0.0).astype(BF16)
            e = lax.broadcasted_iota(jnp.int32, (1, ROLL_W), 1)
            dg_ref[...] = jnp.zeros_like(dg_ref)
            for h in range(2):
                rev = _dot3_l(flip, db_ref[h])
                wide = jnp.concatenate([rev, jnp.zeros((TQ, ROLL_W - WIN), F32)], axis=1)
                diag = pltpu.roll(wide, 0, 1, stride=1, stride_axis=0)
                dg = jnp.sum(diag, axis=0, keepdims=True)
                lo = jnp.sum(jnp.where(e <= 639, dg, 0.0), axis=1, keepdims=True)
                hi = jnp.sum(jnp.where(e >= 895, dg, 0.0), axis=1, keepdims=True)
                dg_ref[h:h + 1, :] = jnp.where(e == 639, lo, jnp.where(e == 895, hi, dg))

    c0 = 3 * N_PAIR
    res = lambda p, i: (0, p)
    return _pallas(
        body, grid=(N_PAIR, nq), task=task,
        in_specs=[pl.BlockSpec((TQ, 128), lambda p, i: (i, c0 + p)),
                  pl.BlockSpec((S, 128), lambda p, i: (0, c0 + N_PAIR + p)),
                  pl.BlockSpec((S, 128), lambda p, i: (0, c0 + 2 * N_PAIR + p)),
                  pl.BlockSpec((TQ, 128), lambda p, i: (i, p)),
                  pl.BlockSpec((2, TQ, WIN), lambda p, i: (p, 0, 0))],
        out_specs=(pl.BlockSpec((TQ, 128), lambda p, i: (i, p)), pl.BlockSpec((S, 128), res),
                   pl.BlockSpec((S, 128), res), pl.BlockSpec((None, 8, ROLL_W), lambda p, i: (p, 0, 0))),
        out_shape=(jax.ShapeDtypeStruct((S, D_GRP), BF16), jax.ShapeDtypeStruct((S, D_GRP), BF16),
                   jax.ShapeDtypeStruct((S, D_GRP), BF16), jax.ShapeDtypeStruct((N_PAIR, 8, ROLL_W), F32)),
        scratch_shapes=[pltpu.VMEM((S + PADK, 128), BF16), pltpu.VMEM((S + PADK, 128), BF16),
                        pltpu.VMEM((S + PADK, 128), F32), pltpu.VMEM((S + PADK, 128), F32),
                        pltpu.VMEM((2, TQ, WIN), F32)],
        name=name,
    )(proj, proj, proj, do, bias)


def _mem_fwd_call(q, k, v, *, name, tq=2048):
    S = q.shape[0]
    scale = MEM_HD ** -0.5

    def body(q_ref, k_ref, v_ref, o_ref):
        s = _dot_nt(q_ref[...] * scale, k_ref[...])
        p = jnp.exp(s - jnp.max(s, axis=1, keepdims=True))
        p = p / jnp.sum(p, axis=1, keepdims=True)
        o_ref[...] = _dot(p.astype(BF16), v_ref[...]).astype(BF16)

    return pl.pallas_call(
        body, grid=(MEM_HEADS, S // tq),
        in_specs=[pl.BlockSpec((tq, MEM_HD), lambda h, i: (i, h)),
                  pl.BlockSpec((N_MEM, MEM_HD), lambda h, i: (0, h)),
                  pl.BlockSpec((N_MEM, MEM_HD), lambda h, i: (0, h))],
        out_specs=pl.BlockSpec((tq, MEM_HD), lambda h, i: (i, h)),
        out_shape=jax.ShapeDtypeStruct((S, D), BF16), name=name, compiler_params=_cparams(),
    )(q, k, v)


def _mem_bwd_call(q, k, v, do, *, name, tq=2048):
    S = q.shape[0]
    n = S // tq
    scale = MEM_HD ** -0.5

    def body(q_ref, k_ref, v_ref, do_ref, dq_ref, dk_ref, dv_ref, dka_ref, dva_ref):
        i = pl.program_id(1)

        @pl.when(i == 0)
        def _():
            dka_ref[...] = jnp.zeros_like(dka_ref)
            dva_ref[...] = jnp.zeros_like(dva_ref)
        qb = q_ref[...]
        kb = k_ref[...]
        dob = do_ref[...]
        s = _dot_nt(qb * scale, kb)
        p = jnp.exp(s - jnp.max(s, axis=1, keepdims=True))
        p = p / jnp.sum(p, axis=1, keepdims=True)
        dp = _dot_nt(dob, v_ref[...])
        ds = p * (dp - jnp.sum(p * dp, axis=1, keepdims=True))
        dsb = (ds * scale).astype(BF16)
        dq_ref[...] = _dot(dsb, kb).astype(BF16)
        dka_ref[...] += _dot_tn(dsb, qb)
        dva_ref[...] += _dot_tn(p.astype(BF16), dob)

        @pl.when(i == n - 1)
        def _():
            dk_ref[...] = dka_ref[...].astype(BF16)
            dv_ref[...] = dva_ref[...].astype(BF16)

    kv = pl.BlockSpec((N_MEM, MEM_HD), lambda h, i: (0, h))
    qs = pl.BlockSpec((tq, MEM_HD), lambda h, i: (i, h))
    return pl.pallas_call(
        body, grid=(MEM_HEADS, n), in_specs=[qs, kv, kv, qs], out_specs=(qs, kv, kv),
        out_shape=(jax.ShapeDtypeStruct((S, D), BF16), jax.ShapeDtypeStruct((N_MEM, D), BF16),
                   jax.ShapeDtypeStruct((N_MEM, D), BF16)),
        scratch_shapes=[pltpu.VMEM((N_MEM, MEM_HD), F32), pltpu.VMEM((N_MEM, MEM_HD), F32)],
        name=name, compiler_params=_cparams(),
    )(q, k, v, do)


def _rel_table_to_g(rel):
    return jnp.concatenate([
        jnp.broadcast_to(rel[:, N_REL - 1:N_REL], (8, 640)),
        rel[:, 1:N_REL - 1][:, ::-1],
        jnp.broadcast_to(rel[:, 0:1], (8, 129)),
    ], axis=1)


def _g_to_rel_table(dg):
    return dg[:, 639:896][:, ::-1]


def _place():
    x, y, c = lax.axis_index("x"), lax.axis_index("y"), lax.axis_index("c")
    others = [(1 - x, y), (x, 1 - y), (1 - x, 1 - y)]
    return x, y, c, others


def _half(c, rows):
    hr = rows // 2
    return pl.ds(pl.multiple_of(c * hr, 16), hr)


def _dma_sems(*shape):
    return pltpu.SemaphoreType.DMA(shape)


def _cast_slabs_call(ws, chip_arr, *, name, tm=256, task=None):
    n = len(ws)
    cols = ws[0].shape[1]
    tiles = [w.shape[0] // tm for w in ws]
    steps = max(tiles)

    def body(chip_ref, *refs):
        i = pl.program_id(0)
        for k in range(n):
            def cast(k=k):
                refs[n + k][...] = refs[k][...].astype(BF16)
            if tiles[k] == steps:
                cast()
            else:
                pl.when(i < tiles[k])(cast)

    in_specs = [pl.BlockSpec((tm, cols), lambda i, chip, t=t: (jnp.minimum(i, t - 1), 0)) for t in tiles]
    out_specs = [pl.BlockSpec((None, tm, cols), lambda i, chip, t=t: (chip[0], jnp.minimum(i, t - 1), 0)) for t in tiles]
    out_shape = [jax.ShapeDtypeStruct((N_CHIP,) + w.shape, BF16) for w in ws]
    return _pallas(body, grid=(steps,), in_specs=in_specs, out_specs=out_specs, out_shape=out_shape, name=name,
                   task=task, prefetch=1)(chip_arr, *ws)


def _cast_slab_call(w, chip_arr, *, name, tm=256, pad_rows=0):
    rows, cols = w.shape
    if pad_rows:
        tm = rows
    tm = min(tm, rows)

    def body(chip_ref, w_ref, o_ref):
        o_ref[0:tm, :] = w_ref[...].astype(BF16)
        if pad_rows:
            o_ref[tm:tm + pad_rows, :] = jnp.zeros((pad_rows, cols), BF16)

    return pl.pallas_call(
        body,
        grid_spec=pltpu.PrefetchScalarGridSpec(
            num_scalar_prefetch=1, grid=(rows // tm,),
            in_specs=[pl.BlockSpec((tm, cols), lambda i, chip: (i, 0))],
            out_specs=pl.BlockSpec((None, tm + pad_rows, cols), lambda i, chip: (chip[0], i, 0))),
        out_shape=jax.ShapeDtypeStruct((N_CHIP, rows + pad_rows, cols), BF16), name=name,
        compiler_params=_cparams(),
    )(chip_arr, w)


def _ag_ici_task(gathered):
    n = len(gathered)

    def copies(ins, outs, sems):
        send_sems, recv_sems = sems
        x, y, c, others = _place()
        me = 2 * x + y
        for k in range(n):
            mine = _half(c, gathered[k].shape[1])
            for t, (ox, oy) in enumerate(others):
                yield [pltpu.make_async_remote_copy(
                    src_ref=ins[k].at[me, mine], dst_ref=outs[k].at[slab, mine],
                    send_sem=send_sems.at[k, t], recv_sem=recv_sems.at[k, t],
                    device_id=(ox, oy, c), device_id_type=MESH) for slab in (me, 2 * ox + oy)]

    def issue(ins, outs, sems):
        for outgoing, _ in copies(ins, outs, sems):
            outgoing.start()

    def drain(ins, outs, sems):
        for outgoing, incoming in copies(ins, outs, sems):
            incoming.wait_recv()
            outgoing.wait_send()

    return _Task(gathered, [jax.ShapeDtypeStruct(g.shape, g.dtype) for g in gathered],
                 [_dma_sems(n, 3), _dma_sems(n, 3)], issue, drain, aliases={k: k for k in range(n)})


def _ag_d2d_task(gathered):
    n = len(gathered)

    def copies(ins, outs, sems):
        send_sems, recv_sems = sems
        x, y, c, others = _place()
        for k in range(n):
            rows = gathered[k].shape[1]
            mine, theirs = _half(c, rows), _half(1 - c, rows)
            for t, (ox, oy) in enumerate(others):
                slab = 2 * ox + oy
                pair = [pltpu.make_async_remote_copy(
                    src_ref=ins[k].at[slab, half], dst_ref=outs[k].at[slab, half],
                    send_sem=send_sems.at[k, t], recv_sem=recv_sems.at[k, t],
                    device_id=(x, y, 1 - c), device_id_type=MESH) for half in (mine, theirs)]
                yield pair

    def issue(ins, outs, sems):
        for outgoing, _ in copies(ins, outs, sems):
            outgoing.start()

    def drain(ins, outs, sems):
        for outgoing, incoming in copies(ins, outs, sems):
            incoming.wait_recv()
            outgoing.wait_send()

    return _Task(gathered, [jax.ShapeDtypeStruct(g.shape, g.dtype) for g in gathered],
                 [_dma_sems(n, 3), _dma_sems(n, 3)], issue, drain, aliases={k: k for k in range(n)})


def _rs_pair_task(ds):
    n = len(ds)

    def copies(ins, outs, sems):
        send_sems, recv_sems = sems
        x, y, c, _ = _place()
        for k in range(n):
            yield pltpu.make_async_remote_copy(
                src_ref=ins[k].at[:, _half(1 - c, ds[k].shape[1])], dst_ref=outs[k],
                send_sem=send_sems.at[k], recv_sem=recv_sems.at[k],
                device_id=(x, y, 1 - c), device_id_type=MESH)

    def issue(ins, outs, sems):
        for cp in copies(ins, outs, sems):
            cp.start()

    def drain(ins, outs, sems):
        for cp in copies(ins, outs, sems):
            cp.wait()

    return _Task(ds, [jax.ShapeDtypeStruct((N_CHIP, d.shape[1] // 2, d.shape[2]), d.dtype) for d in ds],
                 [_dma_sems(n), _dma_sems(n)], issue, drain)


def _pair_add_call(d, r1, c_arr, *, name, tm=512):
    _, rows, cols = d.shape
    hr = rows // 2
    tm = tm if hr % tm == 0 else hr
    nb = hr // tm

    def body(c_ref, d_ref, r_ref, o_ref):
        o_ref[...] = (d_ref[...].astype(F32) + r_ref[...].astype(F32)).astype(BF16)

    return pl.pallas_call(
        body,
        grid_spec=pltpu.PrefetchScalarGridSpec(
            num_scalar_prefetch=1, grid=(N_CHIP, nb),
            in_specs=[pl.BlockSpec((None, tm, cols), lambda j, i, c: (j, c[0] * nb + i, 0)),
                      pl.BlockSpec((None, tm, cols), lambda j, i, c: (j, i, 0))],
            out_specs=pl.BlockSpec((None, tm, cols), lambda j, i, c: (j, i, 0))),
        out_shape=jax.ShapeDtypeStruct((N_CHIP, hr, cols), BF16), name=name, compiler_params=_cparams(),
    )(c_arr, d, r1)


def _rs_chip_task(ps):
    n = len(ps)

    def copies(ins, outs, sems):
        send_sems, recv_sems = sems
        x, y, c, others = _place()
        for k in range(n):
            for t, (ox, oy) in enumerate(others):
                yield pltpu.make_async_remote_copy(
                    src_ref=ins[k].at[2 * ox + oy], dst_ref=outs[k].at[t],
                    send_sem=send_sems.at[k, t], recv_sem=recv_sems.at[k, t],
                    device_id=(ox, oy, c), device_id_type=MESH)

    def issue(ins, outs, sems):
        for cp in copies(ins, outs, sems):
            cp.start()

    def drain(ins, outs, sems):
        for cp in copies(ins, outs, sems):
            cp.wait()

    return _Task(ps, [jax.ShapeDtypeStruct((3,) + p.shape[1:], p.dtype) for p in ps],
                 [_dma_sems(n, 3), _dma_sems(n, 3)], issue, drain)


def _chip_sum_call(p, r2, place_arr, *, name, tm=512):
    _, hr, cols = r2.shape
    tm = tm if hr % tm == 0 else hr
    nb = hr // tm

    def body(place_ref, p_ref, r_ref, o_ref):
        acc = p_ref[...].astype(F32)
        for j in range(3):
            acc = acc + r_ref[j].astype(F32)
        o_ref[...] = acc

    return pl.pallas_call(
        body,
        grid_spec=pltpu.PrefetchScalarGridSpec(
            num_scalar_prefetch=1, grid=(nb,),
            in_specs=[pl.BlockSpec((None, tm, cols), lambda i, pc: (pc[0], i, 0)),
                      pl.BlockSpec((3, tm, cols), lambda i, pc: (0, i, 0))],
            out_specs=pl.BlockSpec((tm, cols), lambda i, pc: (pc[1] * nb + i, 0))),
        out_shape=jax.ShapeDtypeStruct((2 * hr, cols), F32), name=name, compiler_params=_cparams(),
    )(place_arr, p, r2)


def _rs_gather_task(gs):
    n = len(gs)

    def copies(ins, outs, sems):
        send_sems, recv_sems = sems
        x, y, c, _ = _place()
        for k in range(n):
            rows = gs[k].shape[0]
            mine, theirs = _half(c, rows), _half(1 - c, rows)
            yield [pltpu.make_async_remote_copy(
                src_ref=ins[k].at[mine], dst_ref=outs[k].at[half],
                send_sem=send_sems.at[k], recv_sem=recv_sems.at[k],
                device_id=(x, y, 1 - c), device_id_type=MESH) for half in (mine, theirs)]

    def issue(ins, outs, sems):
        for outgoing, _ in copies(ins, outs, sems):
            outgoing.start()

    def drain(ins, outs, sems):
        for outgoing, incoming in copies(ins, outs, sems):
            incoming.wait_recv()
            outgoing.wait_send()

    return _Task(gs, [jax.ShapeDtypeStruct(g.shape, g.dtype) for g in gs],
                 [_dma_sems(n), _dma_sems(n)], issue, drain, aliases={k: k for k in range(n)})


def _adamw(w, g, m, v):
    m = ADAM_B1 * m + (1.0 - ADAM_B1) * g
    v = ADAM_B2 * v + (1.0 - ADAM_B2) * jnp.square(g)
    m_hat = m / (1.0 - ADAM_B1 ** ADAM_STEP)
    v_hat = v / (1.0 - ADAM_B2 ** ADAM_STEP)
    delta = -ADAM_LR * (m_hat / (jnp.sqrt(v_hat) + ADAM_EPS) + ADAM_WD * w)
    return delta, m, v


def _adamw_call(items, *, name, tm=256, task=None):
    n = len(items)
    cols = items[0][0].shape[1]
    tiles = [it[0].shape[0] // tm for it in items]
    steps = max(tiles)

    def body(*refs):
        i = pl.program_id(0)
        ins, outs = refs[:4 * n], refs[4 * n:]
        for k in range(n):
            def update(k=k):
                g = ins[4 * k + 1][...]
                res = _adamw(ins[4 * k][...], g, ins[4 * k + 2][...], ins[4 * k + 3][...])
                outs[4 * k][...] = g
                for j in range(3):
                    outs[4 * k + 1 + j][...] = res[j]
            if tiles[k] == steps:
                update()
            else:
                pl.when(i < tiles[k])(update)

    in_specs, out_specs, out_shape, args = [], [], [], []
    for it, t in zip(items, tiles):
        spec = pl.BlockSpec((tm, cols), lambda i, t=t: (jnp.minimum(i, t - 1), 0))
        in_specs += [spec] * 4
        out_specs += [spec] * 4
        out_shape += [jax.ShapeDtypeStruct(it[0].shape, F32)] * 4
        args += list(it)
    res = _pallas(body, grid=(steps,), in_specs=in_specs, out_specs=out_specs, out_shape=out_shape,
                  name=name, task=task)(*args)
    outs, extra = res if task is not None else (res, None)
    grouped = [tuple(outs[4 * k:4 * k + 4]) for k in range(n)]
    return (grouped, extra) if task is not None else grouped


def _adamw_cols_call(w, g_pad, m, v, *, name, tn=256):
    rows, cols = w.shape

    def body(w_ref, g_ref, m_ref, v_ref, go_ref, d_ref, mo_ref, vo_ref):
        g = g_ref[0:rows, :]
        d, mn, vn = _adamw(w_ref[...], g, m_ref[...], v_ref[...])
        go_ref[...] = g
        d_ref[...] = d
        mo_ref[...] = mn
        vo_ref[...] = vn

    spec = pl.BlockSpec((rows, tn), lambda j: (0, j))
    gspec = pl.BlockSpec((g_pad.shape[0], tn), lambda j: (0, j))
    return _pallas(body, grid=(cols // tn,), in_specs=[spec, gspec, spec, spec], out_specs=(spec,) * 4,
                   out_shape=(jax.ShapeDtypeStruct((rows, cols), F32),) * 4, name=name)(w, g_pad, m, v)


N_DEV = 8
SMALL_ROWS = 24
SMALL_LAYOUT = {
    "g_mix_pre": (0, 0, 1, D), "g_mix_post": (1, 0, 1, D), "g_mem_kv": (2, 0, 1, D), "g_mem_pre": (3, 0, 1, D),
    "g_mem_post": (4, 0, 1, D), "g_ff_pre": (5, 0, 1, D), "g_ff_post": (6, 0, 1, D),
    "g_fox_out": (7, 0, 1, D_GRP), "g_chk_out": (7, D_GRP, 1, D_GRP), "b_fgt": (8, 0, 1, 8),
    "rel_bias": (16, 0, 8, N_REL),
}
SMALL = list(SMALL_LAYOUT)


LOSS_ROW = 9


def _small_call(grads, ws, ms, vs, loss_blk, task, *, name):
    n = len(SMALL)
    t_in, t_out = len(task.arrays), len(task.out_shapes)

    def body(*refs):
        g_refs, w_refs, m_refs, v_refs = (refs[j * n:(j + 1) * n] for j in range(4))
        p = 4 * n
        loss_ref, tins = refs[p], refs[p + 1:p + 1 + t_in]
        p += 1 + t_in
        outs, loss_out, touts = refs[p:p + 4 * n], refs[p + 4 * n], refs[p + 4 * n + 1:p + 4 * n + 1 + t_out]
        p += 4 * n + 1 + t_out
        mine, slots, send_sems, recv_sems = refs[p:p + 4]
        tsems = refs[p + 4:]
        task.issue(tins, touts, tsems)
        x, y, c, _ = _place()
        me = 4 * x + 2 * y + c
        mine[...] = jnp.zeros_like(mine)
        for k, name_k in enumerate(SMALL):
            r, l, nr, nl = SMALL_LAYOUT[name_k]
            mine[r:r + nr, l:l + nl] = g_refs[k][0:nr, 0:nl]
        mine[LOSS_ROW:LOSS_ROW + 1, 0:128] = loss_ref[0:1, :]
        slots[me] = mine[...]
        peers = [(dx, dy, dc) for dx in (0, 1) for dy in (0, 1) for dc in (0, 1)][1:]
        cps = []
        for t, (dx, dy, dc) in enumerate(peers):
            px, py, pc = (x + dx) % 2, (y + dy) % 2, (c + dc) % 2
            cps.append(pltpu.make_async_remote_copy(
                src_ref=mine, dst_ref=slots.at[me], send_sem=send_sems.at[t], recv_sem=recv_sems.at[t],
                device_id=(px, py, pc), device_id_type=MESH))
            cps[-1].start()
        for t, (dx, dy, dc) in enumerate(peers):
            px, py, pc = (x + dx) % 2, (y + dy) % 2, (c + dc) % 2
            pltpu.make_async_remote_copy(
                src_ref=mine, dst_ref=slots.at[4 * px + 2 * py + pc], send_sem=send_sems.at[t],
                recv_sem=recv_sems.at[t], device_id=(px, py, pc), device_id_type=MESH).wait_recv()
        for cp in cps:
            cp.wait_send()
        total = slots[0]
        for j in range(1, N_DEV):
            total = total + slots[j]
        for k, name_k in enumerate(SMALL):
            r, l, nr, nl = SMALL_LAYOUT[name_k]
            g = total[r:r + nr, l:l + nl]
            d, mn, vn = _adamw(w_refs[k][...], g, m_refs[k][...], v_refs[k][...])
            for j, val in enumerate((g, d, mn, vn)):
                outs[4 * k + j][...] = val
        loss_out[...] = jnp.broadcast_to(total[LOSS_ROW:LOSS_ROW + 1, 0:128], loss_out.shape)
        task.drain(tins, touts, tsems)

    vm = pl.BlockSpec(memory_space=pltpu.VMEM)
    out_shape = [jax.ShapeDtypeStruct(ws[k].shape, F32) for k in SMALL for _ in range(4)]
    out_shape += [jax.ShapeDtypeStruct((8, 128), F32)] + list(task.out_shapes)
    res = pl.pallas_call(
        body, in_specs=[vm] * (4 * n + 1) + [ANY] * t_in, out_specs=[vm] * (4 * n + 1) + [ANY] * t_out,
        out_shape=out_shape,
        scratch_shapes=[pltpu.VMEM((SMALL_ROWS, D), F32), pltpu.VMEM((N_DEV, SMALL_ROWS, D), F32),
                        _dma_sems(N_DEV - 1), _dma_sems(N_DEV - 1)] + list(task.sems),
        input_output_aliases={4 * n + 1 + i: 4 * n + 1 + j for i, j in task.aliases.items()},
        name=name,
    )(*[d[k] for d in (grads, ws, ms, vs) for k in SMALL], loss_blk, *task.arrays)
    return ({k: tuple(res[4 * i:4 * i + 4]) for i, k in enumerate(SMALL)}, res[4 * n], list(res[4 * n + 1:]))


WEIGHTS = ["w_in", "b_fgt", "rel_bias", "g_fox_out", "g_chk_out", "w_out", "g_mix_pre", "g_mix_post", "g_mem_kv",
           "w_mq", "w_mk", "w_mv", "w_mo", "g_mem_pre", "g_mem_post", "w_ff1", "w_ff2", "g_ff_pre", "g_ff_post"]
BIG = ["w_in", "w_out", "w_mq", "w_mk", "w_mv", "w_mo", "w_ff1", "w_ff2"]


IN_SHARD = D_IN // N_CHIP
IN_PAD = 800


IN_PIECES = [(0, 0, 770), (800, 770, 766), (1566, 3072, 4), (1600, 3076, 4), (1604, 1536, 766), (2400, 2302, 770)]
PAD_ZEROS = [(800 * j + IN_SHARD, IN_PAD - IN_SHARD) for j in range(N_CHIP)]
ALL_ZEROS = [(D_IN, D_ALL - D_IN)]


def _reorder_rows_call(src, to_all, *, name, tn=256):
    rows, cols = src.shape
    zeros = ALL_ZEROS if to_all else PAD_ZEROS

    def body(s_ref, o_ref):
        for pad0, all0, cnt in IN_PIECES:
            s0, d0 = (pad0, all0) if to_all else (all0, pad0)
            o_ref[d0:d0 + cnt, :] = s_ref[s0:s0 + cnt, :]
        for z0, cnt in zeros:
            o_ref[z0:z0 + cnt, :] = jnp.zeros((cnt, tn), src.dtype)

    spec = pl.BlockSpec((rows, tn), lambda j: (0, j))
    return _pallas(body, grid=(cols // tn,), in_specs=[spec], out_specs=spec,
                   out_shape=jax.ShapeDtypeStruct((rows, cols), src.dtype), name=name)(src)


def kernel(x, mem, w_in, b_fgt, rel_bias, g_fox_out, g_chk_out, w_out, g_mix_pre, g_mix_post, g_mem_kv, w_mq, w_mk, w_mv, w_mo, g_mem_pre, g_mem_post, w_ff1, w_ff2, g_ff_pre, g_ff_post, loss_target, m_w_in, m_b_fgt, m_rel_bias, m_g_fox_out, m_g_chk_out, m_w_out, m_g_mix_pre, m_g_mix_post, m_g_mem_kv, m_w_mq, m_w_mk, m_w_mv, m_w_mo, m_g_mem_pre, m_g_mem_post, m_w_ff1, m_w_ff2, m_g_ff_pre, m_g_ff_post, v_w_in, v_b_fgt, v_rel_bias, v_g_fox_out, v_g_chk_out, v_w_out, v_g_mix_pre, v_g_mix_post, v_g_mem_kv, v_w_mq, v_w_mk, v_w_mv, v_w_mo, v_g_mem_pre, v_g_mem_post, v_w_ff1, v_w_ff2, v_g_ff_pre, v_g_ff_post):
    w = dict(w_in=w_in, b_fgt=b_fgt, rel_bias=rel_bias, g_fox_out=g_fox_out, g_chk_out=g_chk_out, w_out=w_out,
             g_mix_pre=g_mix_pre, g_mix_post=g_mix_post, g_mem_kv=g_mem_kv, w_mq=w_mq, w_mk=w_mk, w_mv=w_mv,
             w_mo=w_mo, g_mem_pre=g_mem_pre, g_mem_post=g_mem_post, w_ff1=w_ff1, w_ff2=w_ff2, g_ff_pre=g_ff_pre,
             g_ff_post=g_ff_post)
    m = dict(w_in=m_w_in, b_fgt=m_b_fgt, rel_bias=m_rel_bias, g_fox_out=m_g_fox_out, g_chk_out=m_g_chk_out,
             w_out=m_w_out, g_mix_pre=m_g_mix_pre, g_mix_post=m_g_mix_post, g_mem_kv=m_g_mem_kv, w_mq=m_w_mq,
             w_mk=m_w_mk, w_mv=m_w_mv, w_mo=m_w_mo, g_mem_pre=m_g_mem_pre, g_mem_post=m_g_mem_post,
             w_ff1=m_w_ff1, w_ff2=m_w_ff2, g_ff_pre=m_g_ff_pre, g_ff_post=m_g_ff_post)
    v = dict(w_in=v_w_in, b_fgt=v_b_fgt, rel_bias=v_rel_bias, g_fox_out=v_g_fox_out, g_chk_out=v_g_chk_out,
             w_out=v_w_out, g_mix_pre=v_g_mix_pre, g_mix_post=v_g_mix_post, g_mem_kv=v_g_mem_kv, w_mq=v_w_mq,
             w_mk=v_w_mk, w_mv=v_w_mv, w_mo=v_w_mo, g_mem_pre=v_g_mem_pre, g_mem_post=v_g_mem_post,
             w_ff1=v_w_ff1, w_ff2=v_w_ff2, g_ff_pre=v_g_ff_pre, g_ff_post=v_g_ff_post)

    def rows(d, k):
        return d[k][0] if k == "rel_bias" else d[k]

    xs, mems, target = x[0], mem[0], loss_target[0]
    S = xs.shape[0]
    sp = {k: rows(w, k) for k in SMALL}
    b_pad = jnp.pad(sp["b_fgt"], ((0, 0), (0, 120)))
    chip = 2 * lax.axis_index("x") + lax.axis_index("y")
    chip_arr = jnp.reshape(chip, (1,)).astype(jnp.int32)
    c_arr = jnp.reshape(lax.axis_index("c"), (1,)).astype(jnp.int32)
    place_arr = jnp.concatenate([chip_arr, c_arr])
    w_in_t, m_in_t, v_in_t = w["w_in"][0].T, m["w_in"][0].T, v["w_in"][0].T
    slab = {"w_in": _cast_slab_call(w_in_t, chip_arr, name="cast_w_in", pad_rows=IN_PAD - IN_SHARD)}

    def gather_ici(names):
        return _ag_ici_task([slab[k] for k in names])

    def pair_add(k, d, r1):
        return _pair_add_call(d, r1, c_arr, name="rs_pair_add_" + k)

    rest, (g_in,) = _cast_slabs_call([w[k][0] for k in BIG[1:]], chip_arr, name="cast_rest",
                                     task=gather_ici(["w_in"]))
    slab.update(zip(BIG[1:], rest))
    h1, (g_in,) = _rms_fwd_call(xs, sp["g_mix_pre"], name="rms_mix_pre", task=_ag_d2d_task([g_in]))
    w_all_t = _reorder_rows_call(g_in.reshape(N_CHIP * IN_PAD, D), True, name="w_in_rows")
    proj, (g_out, g_mq) = _mm_nt(h1, w_all_t, "plain", rows=(0, 3072), name="mm_proj",
                                 task=gather_ici(["w_out", "w_mq"]))
    fl_raw = _mm_nt(h1, w_all_t, "plain", rows=(3072, 128), name="mm_gate", out_dtype=F32, tn=128)
    c_rep, c_t = _fox_prep_call(fl_raw, b_pad, name="fox_prep")
    bias = _chk_bias_call(_rel_table_to_g(sp["rel_bias"]), name="chk_bias")
    mid = ["w_mk", "w_mv", "w_mo", "w_ff1"]
    (yf, lse), got = _fox_fwd_call(proj, c_rep, c_t, name="fox_fwd",
                                   task=_merge_tasks([gather_ici(mid), _ag_d2d_task([g_out, g_mq])]))
    g_mid, (g_out, g_mq) = got[:4], got[4:]
    yc, got = _chk_fwd_call(proj, bias, name="chk_fwd",
                            task=_merge_tasks([gather_ici(["w_ff2"]), _ag_d2d_task(g_mid)]))
    g_ff2, (g_mk, g_mv, g_mo, g_ff1) = got[0], got[1:]
    yn = _mix_norm_fwd_call(yf, yc, sp["g_fox_out"], sp["g_chk_out"], name="mix_norm_fwd")
    z, (g_ff2,) = _mm_nn(yn, g_out, "rows", name="mm_out", out_dtype=F32, task=_ag_d2d_task([g_ff2]))
    x1, h2 = _post_pre_call(xs, z, sp["g_mix_post"], sp["g_mem_pre"], name="post_mix")
    memn = _rms_fwd_call(mems, sp["g_mem_kv"], name="rms_mem_kv")
    q2 = _mm_nn(h2, g_mq, "rows", name="mm_mq")
    k2 = _mm_nn(memn, g_mk, "rows", name="mm_mk")
    v2 = _mm_nn(memn, g_mv, "rows", name="mm_mv")
    o2 = _mem_fwd_call(q2, k2, v2, name="mem_fwd")
    y2 = _mm_nn(o2, g_mo, "rows", name="mm_mo", out_dtype=F32)
    x2, h3 = _post_pre_call(x1, y2, sp["g_mem_post"], sp["g_ff_pre"], name="post_mem")
    act, relu = _mm_nn(h3, g_ff1, "cols", name="mm_ff1", epi="relu2")
    y3 = _mm_nn(act, g_ff2, "rows", name="mm_ff2", out_dtype=F32, tm=1024)
    loss_blk, dx3, dy3, dg_ff_post = _final_call(x2, y3, sp["g_ff_post"], target, name="final")

    d_ff2 = _mm_tn(act, dy3, name="mm_dff2", tk=512, tn=1024).reshape(N_CHIP, D_FF // N_CHIP, D)
    du, (r1,) = _mm_nt(dy3, g_ff2, "rows", name="mm_du", mul2r=relu, task=_rs_pair_task([d_ff2]))
    p_ff2 = pair_add("w_ff2", d_ff2, r1)
    d_ff1 = _mm_tn(h3, du, name="mm_dff1", cols4=True)
    dh3, (r1,) = _mm_nt(du, g_ff1, "cols", name="mm_dh3", out_dtype=F32, tm=1024, task=_rs_pair_task([d_ff1]))
    p_ff1 = pair_add("w_ff1", d_ff1, r1)
    dx2, dy2, dg_ff_pre, dg_mem_post = _bwd_mid_call(dx3, x2, dh3, sp["g_ff_pre"], y2, sp["g_mem_post"], name="bwd_ff")
    d_mo = _mm_tn(o2, dy2, name="mm_dmo").reshape(N_CHIP, D // N_CHIP, D)
    do2 = _mm_nt(dy2, g_mo, "rows", name="mm_do2")
    dq2, dk2, dv2 = _mem_bwd_call(q2, k2, v2, do2, name="mem_bwd")
    d_mq = _mm_tn(h2, dq2, name="mm_dmq").reshape(N_CHIP, D // N_CHIP, D)
    dh2 = _mm_nt(dq2, g_mq, "rows", name="mm_dh2", out_dtype=F32)
    d_mk = _mm_tn(memn, dk2, name="mm_dmk").reshape(N_CHIP, D // N_CHIP, D)
    d_mv = _mm_tn(memn, dv2, name="mm_dmv").reshape(N_CHIP, D // N_CHIP, D)
    dmn_k = _mm_nt(dk2, g_mk, "rows", name="mm_dmemk", out_dtype=F32)
    dmn_v = _mm_nt(dv2, g_mv, "rows", name="mm_dmemv", out_dtype=F32)
    dg_mem_kv = _gain_grad_call(mems, sp["g_mem_kv"], dmn_k, dmn_v, name="gain_mem_kv")
    dx1, dz, dg_mem_pre, dg_mix_post = _bwd_mid_call(dx2, x1, dh2, sp["g_mem_pre"], z, sp["g_mix_post"], name="bwd_mem")
    d_out = _mm_tn(yn, dz, name="mm_dout").reshape(N_CHIP, D // N_CHIP, D)
    late = ["w_mo", "w_mq", "w_mk", "w_mv", "w_out"]
    d_late = [d_mo, d_mq, d_mk, d_mv, d_out]
    dyn, r1_late = _mm_nt(dz, g_out, "rows", name="mm_dyn", out_dtype=F32, task=_rs_pair_task(d_late))
    p_late = [pair_add(k, d, r1) for k, d, r1 in zip(late, d_late, r1_late)]
    dof, doc, delta, dg_fox, dg_chk = _mix_norm_bwd_call(dyn, yf, yc, sp["g_fox_out"], sp["g_chk_out"], name="mix_norm_bwd")
    (dqf, dkf, dvf, dcq, dck), r2_ff = _fox_bwd_call(proj, dof, lse, delta, c_rep, c_t, name="fox_bwd",
                                                      task=_rs_chip_task([p_ff2, p_ff1]))
    (dqc, dkc, dvc, dgrev), r2_late = _chk_bwd_call(proj, doc, bias, name="chk_bwd", task=_rs_chip_task(p_late))
    first = ["w_ff2", "w_ff1"] + late
    f_first = [_chip_sum_call(p, r, place_arr, name="rs_chip_sum_" + k)
               for k, p, r in zip(first, [p_ff2, p_ff1] + p_late, r2_ff + r2_late)]
    dc8 = dcq[:, :, 0:2, :].transpose(0, 2, 1, 3).reshape(8, S) + dck[:, ::HEAD].T
    dc_rows = jnp.concatenate([dc8, jnp.zeros((120, S), F32)], axis=0)
    dfl, db_fgt = _fox_gate_bwd_call(dc_rows, fl_raw, b_pad, name="fox_gate_bwd")
    dproj = jnp.concatenate([dqf, dkf, dvf, dqc, dkc, dvc, dfl], axis=1)
    d_all_t, g_first = _mm_tn(dproj, h1, name="mm_dwin", tk=640, tn=1024, task=_rs_gather_task(f_first))
    grads = dict(zip(first, g_first))
    d_in = _reorder_rows_call(d_all_t, False, name="d_in_rows").reshape(N_CHIP, IN_PAD, D)
    delta_w, new_m, new_v = {}, {}, {}

    def adamw_items(names):
        return [(w[k][0], grads[k], m[k][0], v[k][0]) for k in names]

    upd_late, (r1,) = _adamw_call(adamw_items(late), name="adamw_late", tm=64, task=_rs_pair_task([d_in]))
    p_in = pair_add("w_in", d_in, r1)
    dh1, (r2_in,) = _mm_nn(dproj, w_all_t, "plain", name="mm_dh1", out_dtype=F32, tm=1024,
                           task=_rs_chip_task([p_in]))
    f_in = _chip_sum_call(p_in, r2_in, place_arr, name="rs_chip_sum_w_in")
    upd_ff = _adamw_call(adamw_items(first[:2]), name="adamw_ff")
    for k, res in zip(late + first[:2], upd_late + upd_ff):
        grads[k], delta_w[k], new_m[k], new_v[k] = res
    grad_x, dg_mix_pre = _bwd_last_call(dx1, xs, dh1, sp["g_mix_pre"], name="bwd_mix")

    small_g = {"g_mix_pre": dg_mix_pre, "g_mix_post": dg_mix_post, "g_mem_kv": dg_mem_kv, "g_mem_pre": dg_mem_pre,
               "g_mem_post": dg_mem_post, "g_ff_pre": dg_ff_pre, "g_ff_post": dg_ff_post, "g_fox_out": dg_fox,
               "g_chk_out": dg_chk, "b_fgt": db_fgt,
               "rel_bias": _g_to_rel_table(dgrev[:, 0:2, :].reshape(8, ROLL_W))}
    small, loss_out, (g_w_in,) = _small_call(
        small_g, sp, {k: rows(m, k) for k in SMALL}, {k: rows(v, k) for k in SMALL}, loss_blk,
        _rs_gather_task([f_in]), name="small_allreduce_adamw")
    loss = loss_out[0, 0]
    res = _adamw_cols_call(w_in_t, g_w_in, m_in_t, v_in_t, name="adamw_w_in")
    grads["w_in"], delta_w["w_in"], new_m["w_in"], new_v["w_in"] = (a.T for a in res)
    for k in SMALL:
        vals = small[k]
        if k == "rel_bias":
            vals = tuple(a[None] for a in vals)
        grads[k], delta_w[k], new_m[k], new_v[k] = vals

    def out(d, k):
        return d[k][None] if k in BIG else d[k]

    return (loss, grad_x[None], *[out(grads, k) for k in WEIGHTS], *[out(delta_w, k) for k in WEIGHTS],
            *[out(new_m, k) for k in WEIGHTS], *[out(new_v, k) for k in WEIGHTS])
```

```python
import functools

import jax
import jax.numpy as jnp
from jax import lax
from jax.experimental import pallas as pl
from jax.experimental.pallas import tpu as pltpu

F32 = jnp.float32
BF16 = jnp.bfloat16

D = 1024
HEAD = 64
N_PAIR = 4
D_GRP = 512
CHUNK = 64
LEFT = 8
MAX_REL = 128
N_REL = 2 * MAX_REL + 1
N_MEM = 256
MEM_HEADS = 4
MEM_HD = 256
D_FF = 4096
D_IN = 3080
D_ALL = 3200
EPS = 1e-6
TQ = 256
WIN = (LEFT + TQ // CHUNK) * CHUNK
PADK = LEFT * CHUNK
ROLL_W = 1024
NEG = -1e30
N_CHIP = 4
VMEM_LIMIT = 48 * 1024 * 1024

ADAM_LR = 0.001
ADAM_B1 = 0.9
ADAM_B2 = 0.999
ADAM_EPS = 1e-08
ADAM_WD = 0.01
ADAM_STEP = 10

MESH = pl.DeviceIdType.MESH


def _cparams():
    return pltpu.CompilerParams(vmem_limit_bytes=VMEM_LIMIT)


ANY = pl.BlockSpec(memory_space=pl.ANY)


class _Task:
    def __init__(self, arrays, out_shapes, sems, issue, drain, aliases=None):
        self.arrays, self.out_shapes, self.sems = list(arrays), list(out_shapes), list(sems)
        self.issue, self.drain, self.aliases = issue, drain, dict(aliases or {})


def _merge_tasks(tasks):
    tasks = [t for t in tasks if t is not None]
    if len(tasks) == 1:
        return tasks[0]
    cuts, a, o, s = [], 0, 0, 0
    aliases = {}
    for t in tasks:
        cuts.append((a, o, s))
        aliases.update({a + i: o + j for i, j in t.aliases.items()})
        a, o, s = a + len(t.arrays), o + len(t.out_shapes), s + len(t.sems)

    def part(fn_name):
        def run(ins, outs, sems):
            for t, (a0, o0, s0) in zip(tasks, cuts):
                getattr(t, fn_name)(ins[a0:a0 + len(t.arrays)], outs[o0:o0 + len(t.out_shapes)],
                                    sems[s0:s0 + len(t.sems)])
        return run

    return _Task([x for t in tasks for x in t.arrays], [x for t in tasks for x in t.out_shapes],
                 [x for t in tasks for x in t.sems], part("issue"), part("drain"), aliases)


def _pallas(body, *, grid, in_specs, out_specs, out_shape, name, scratch_shapes=(), task=None, prefetch=0):
    def make(kernel, i_specs, o_specs, o_shape, scratch, aliases):
        if prefetch:
            spec = pltpu.PrefetchScalarGridSpec(num_scalar_prefetch=prefetch, grid=grid, in_specs=i_specs,
                                                out_specs=o_specs, scratch_shapes=scratch)
            return pl.pallas_call(kernel, grid_spec=spec, out_shape=o_shape, input_output_aliases=aliases,
                                  name=name, compiler_params=_cparams())
        return pl.pallas_call(kernel, grid=grid, in_specs=i_specs, out_specs=o_specs, out_shape=o_shape,
                              scratch_shapes=scratch, input_output_aliases=aliases, name=name,
                              compiler_params=_cparams())

    if task is None:
        return make(body, list(in_specs), out_specs, out_shape, list(scratch_shapes), {})
    single = not isinstance(out_shape, (tuple, list))
    o_shapes = [out_shape] if single else list(out_shape)
    o_specs = [out_specs] if single else list(out_specs)
    n_in, n_out, n_scr = len(in_specs), len(o_shapes), len(scratch_shapes)
    t_in, t_out = len(task.arrays), len(task.out_shapes)

    def carried(*refs):
        cut = [prefetch, n_in, t_in, n_out, t_out, n_scr]
        parts, p = [], 0
        for c in cut:
            parts.append(refs[p:p + c])
            p += c
        scalars, ins, tins, outs, touts, scr = parts
        tsems = refs[p:]
        ids = [pl.program_id(a) for a in range(len(grid))]
        first = functools.reduce(jnp.logical_and, [i == 0 for i in ids])
        last = functools.reduce(jnp.logical_and, [i == g - 1 for i, g in zip(ids, grid)])

        @pl.when(first)
        def _():
            task.issue(tins, touts, tsems)
        body(*scalars, *ins, *outs, *scr)

        @pl.when(last)
        def _():
            task.drain(tins, touts, tsems)

    call = make(carried, list(in_specs) + [ANY] * t_in, o_specs + [ANY] * t_out,
                o_shapes + list(task.out_shapes), list(scratch_shapes) + list(task.sems),
                {prefetch + n_in + i: n_out + j for i, j in task.aliases.items()})

    def run(*args):
        res = call(*args, *task.arrays)
        outs = res[:n_out]
        return (outs[0] if single else tuple(outs)), list(res[n_out:])

    return run


def _dot(a, b):
    return jnp.dot(a, b, preferred_element_type=F32)


def _dot_nt(a, b):
    return lax.dot_general(a, b, (((1,), (1,)), ((), ())), preferred_element_type=F32)


def _dot_tn(a, b):
    return lax.dot_general(a, b, (((0,), (0,)), ((), ())), preferred_element_type=F32)


def _split3(x):
    hi = x.astype(BF16)
    r1 = x - hi.astype(F32)
    mid = r1.astype(BF16)
    lo = (r1 - mid.astype(F32)).astype(BF16)
    return hi, mid, lo


def _dot3(x, m01):
    hi, mid, lo = _split3(x)
    return _dot(hi, m01) + _dot(mid, m01) + _dot(lo, m01)


def _dot3_l(m01, x):
    hi, mid, lo = _split3(x)
    return _dot(m01, hi) + _dot(m01, mid) + _dot(m01, lo)


def _mm_nn(a, b, kind, *, name, out_dtype=BF16, tm=2048, tn=512, epi=None, task=None):
    M, K = a.shape
    if kind == "plain":
        N = b.shape[1]
        b_spec = pl.BlockSpec((K, tn), lambda m, n: (0, n))
    elif kind == "rows":
        N = b.shape[2]
        b_spec = pl.BlockSpec((N_CHIP, K // N_CHIP, tn), lambda m, n: (0, 0, n))
    else:
        nq = b.shape[2]
        N = N_CHIP * nq
        per = nq // tn
        b_spec = pl.BlockSpec((None, K, tn), lambda m, n: (n // per, 0, n % per))
    tm = min(tm, M)
    kq = K // N_CHIP

    def body(a_ref, b_ref, *o_refs):
        if kind == "rows":
            acc = _dot(a_ref[:, 0:kq], b_ref[0])
            for j in range(1, N_CHIP):
                acc += _dot(a_ref[:, j * kq:(j + 1) * kq], b_ref[j])
        else:
            acc = _dot(a_ref[...], b_ref[...])
        if epi == "relu2":
            r = jnp.maximum(acc, 0.0)
            o_refs[0][...] = (r * r).astype(BF16)
            o_refs[1][...] = r.astype(BF16)
        else:
            o_refs[0][...] = acc.astype(out_dtype)

    o_spec = pl.BlockSpec((tm, tn), lambda m, n: (m, n))
    if epi == "relu2":
        out_shape = (jax.ShapeDtypeStruct((M, N), BF16), jax.ShapeDtypeStruct((M, N), BF16))
        out_specs = (o_spec, o_spec)
    else:
        out_shape = jax.ShapeDtypeStruct((M, N), out_dtype)
        out_specs = o_spec
    return _pallas(
        body, grid=(M // tm, N // tn),
        in_specs=[pl.BlockSpec((tm, K), lambda m, n: (m, 0)), b_spec],
        out_specs=out_specs, out_shape=out_shape, name=name, task=task,
    )(a, b)


def _mm_nt(a, b, kind, *, name, out_dtype=BF16, tm=2048, tn=512, mul2r=None, task=None, rows=None):
    M, K = a.shape
    if kind == "plain":
        first, N = rows if rows is not None else (0, b.shape[0])
        n0 = first // tn
        b_spec = pl.BlockSpec((tn, K), lambda m, n: (n0 + n, 0))
    elif kind == "rows":
        nq = b.shape[1]
        N = N_CHIP * nq
        tn = min(tn, nq)
        per = nq // tn
        b_spec = pl.BlockSpec((None, tn, K), lambda m, n: (n // per, n % per, 0))
    else:
        N = b.shape[1]
        b_spec = pl.BlockSpec((N_CHIP, tn, K // N_CHIP), lambda m, n: (0, n, 0))
    tm = min(tm, M)
    kq = K // N_CHIP

    def body(a_ref, b_ref, *rest):
        o_ref = rest[-1]
        if kind == "cols":
            acc = _dot_nt(a_ref[:, 0:kq], b_ref[0])
            for j in range(1, N_CHIP):
                acc += _dot_nt(a_ref[:, j * kq:(j + 1) * kq], b_ref[j])
        else:
            acc = _dot_nt(a_ref[...], b_ref[...])
        if mul2r is not None:
            acc = acc * (2.0 * rest[0][...].astype(F32))
        o_ref[...] = acc.astype(out_dtype)

    in_specs = [pl.BlockSpec((tm, K), lambda m, n: (m, 0)), b_spec]
    args = [a, b]
    if mul2r is not None:
        in_specs.append(pl.BlockSpec((tm, tn), lambda m, n: (m, n)))
        args.append(mul2r)
    return _pallas(
        body, grid=(M // tm, N // tn), in_specs=in_specs,
        out_specs=pl.BlockSpec((tm, tn), lambda m, n: (m, n)),
        out_shape=jax.ShapeDtypeStruct((M, N), out_dtype), name=name, task=task,
    )(*args)


def _mm_tn(a, b, *, name, out_dtype=BF16, tk=1024, tn=512, cols4=False, task=None):
    M, K1 = a.shape
    N = b.shape[1]
    tk = min(tk, K1)
    tn = min(tn, N)

    def body(a_ref, b_ref, o_ref):
        o_ref[...] = _dot_tn(a_ref[...], b_ref[...]).astype(out_dtype)

    if cols4:
        per = (N // N_CHIP) // tn
        out_shape = jax.ShapeDtypeStruct((N_CHIP, K1, N // N_CHIP), out_dtype)
        o_spec = pl.BlockSpec((None, tk, tn), lambda k, n: (n // per, k, n % per))
    else:
        out_shape = jax.ShapeDtypeStruct((K1, N), out_dtype)
        o_spec = pl.BlockSpec((tk, tn), lambda k, n: (k, n))
    return _pallas(
        body, grid=(K1 // tk, N // tn),
        in_specs=[pl.BlockSpec((M, tk), lambda k, n: (0, k)), pl.BlockSpec((M, tn), lambda k, n: (0, n))],
        out_specs=o_spec, out_shape=out_shape, name=name, task=task,
    )(a, b)


def _rms(x, g):
    r = lax.rsqrt(jnp.mean(x * x, axis=-1, keepdims=True) + EPS)
    return x * r * g


def _rms_bwd(x, g, dy):
    r = lax.rsqrt(jnp.mean(x * x, axis=-1, keepdims=True) + EPS)
    xh = x * r
    dg = jnp.sum(dy * xh, axis=0, keepdims=True)
    dxh = dy * g
    dx = r * (dxh - xh * jnp.mean(dxh * xh, axis=-1, keepdims=True))
    return dx, dg


def _row_spec(tm, n):
    return pl.BlockSpec((tm, n), lambda i: (i, 0))


def _vec_spec(n):
    return pl.BlockSpec((1, n), lambda i: (0, 0))


def _acc_spec(n):
    return pl.BlockSpec((8, n), lambda i: (0, 0))


def _acc_add(ref, row, i):
    @pl.when(i == 0)
    def _():
        ref[...] = jnp.zeros_like(ref)
    ref[0:1, :] += row


def _rms_fwd_call(x, g, *, name, tm=1024, task=None):
    M, n = x.shape
    tm = min(tm, M)

    def body(x_ref, g_ref, h_ref):
        h_ref[...] = _rms(x_ref[...], g_ref[...]).astype(BF16)

    return _pallas(
        body, grid=(M // tm,), in_specs=[_row_spec(tm, n), _vec_spec(n)], out_specs=_row_spec(tm, n),
        out_shape=jax.ShapeDtypeStruct((M, n), BF16), name=name, task=task,
    )(x, g)


def _post_pre_call(xres, z, g_post, g_pre, *, name, tm=1024):
    M, n = xres.shape

    def body(x_ref, z_ref, gp_ref, gn_ref, xo_ref, h_ref):
        xn = x_ref[...] + _rms(z_ref[...], gp_ref[...])
        xo_ref[...] = xn
        h_ref[...] = _rms(xn, gn_ref[...]).astype(BF16)

    return pl.pallas_call(
        body, grid=(M // tm,),
        in_specs=[_row_spec(tm, n), _row_spec(tm, n), _vec_spec(n), _vec_spec(n)],
        out_specs=(_row_spec(tm, n), _row_spec(tm, n)),
        out_shape=(jax.ShapeDtypeStruct((M, n), F32), jax.ShapeDtypeStruct((M, n), BF16)),
        name=name, compiler_params=_cparams(),
    )(xres, z, g_post, g_pre)


def _final_call(x2, y3, g_post, target, *, name, tm=512):
    M, n = x2.shape

    def body(x_ref, y_ref, g_ref, t_ref, loss_ref, dx_ref, dy_ref, dg_ref):
        i = pl.program_id(0)
        y = y_ref[...]
        g = g_ref[...]
        diff = x_ref[...] + _rms(y, g) - t_ref[...]
        part = 0.5 * jnp.sum(jnp.sum(diff * diff, axis=1, keepdims=True), axis=0, keepdims=True) / n

        @pl.when(i == 0)
        def _():
            loss_ref[...] = jnp.zeros_like(loss_ref)
        loss_ref[...] += jnp.broadcast_to(part, loss_ref.shape)
        dx = diff / n
        dx_ref[...] = dx
        dy, dg = _rms_bwd(y, g, dx)
        dy_ref[...] = dy.astype(BF16)
        _acc_add(dg_ref, dg, i)

    return pl.pallas_call(
        body, grid=(M // tm,),
        in_specs=[_row_spec(tm, n), _row_spec(tm, n), _vec_spec(n), _row_spec(tm, n)],
        out_specs=(pl.BlockSpec((8, 128), lambda i: (0, 0)), _row_spec(tm, n), _row_spec(tm, n), _acc_spec(n)),
        out_shape=(jax.ShapeDtypeStruct((8, 128), F32), jax.ShapeDtypeStruct((M, n), F32),
                   jax.ShapeDtypeStruct((M, n), BF16), jax.ShapeDtypeStruct((8, n), F32)),
        name=name, compiler_params=_cparams(),
    )(x2, y3, g_post, target)


def _bwd_mid_call(dx_in, x, dh, g_pre, y, g_post, *, name, tm=512):
    M, n = x.shape

    def body(dxi_ref, x_ref, dh_ref, gpre_ref, y_ref, gpost_ref, dx_ref, dy_ref, dgpre_ref, dgpost_ref):
        i = pl.program_id(0)
        d1, dg1 = _rms_bwd(x_ref[...], gpre_ref[...], dh_ref[...])
        dx = dxi_ref[...] + d1
        dx_ref[...] = dx
        dy, dg2 = _rms_bwd(y_ref[...], gpost_ref[...], dx)
        dy_ref[...] = dy.astype(BF16)
        _acc_add(dgpre_ref, dg1, i)
        _acc_add(dgpost_ref, dg2, i)

    return pl.pallas_call(
        body, grid=(M // tm,),
        in_specs=[_row_spec(tm, n), _row_spec(tm, n), _row_spec(tm, n), _vec_spec(n), _row_spec(tm, n), _vec_spec(n)],
        out_specs=(_row_spec(tm, n), _row_spec(tm, n), _acc_spec(n), _acc_spec(n)),
        out_shape=(jax.ShapeDtypeStruct((M, n), F32), jax.ShapeDtypeStruct((M, n), BF16),
                   jax.ShapeDtypeStruct((8, n), F32), jax.ShapeDtypeStruct((8, n), F32)),
        name=name, compiler_params=_cparams(),
    )(dx_in, x, dh, g_pre, y, g_post)


def _bwd_last_call(dx_in, x, dh, g_pre, *, name, tm=1024, task=None):
    M, n = x.shape

    def body(dxi_ref, x_ref, dh_ref, g_ref, dx_ref, dg_ref):
        i = pl.program_id(0)
        d1, dg1 = _rms_bwd(x_ref[...], g_ref[...], dh_ref[...])
        dx_ref[...] = dxi_ref[...] + d1
        _acc_add(dg_ref, dg1, i)

    return _pallas(
        body, grid=(M // tm,),
        in_specs=[_row_spec(tm, n), _row_spec(tm, n), _row_spec(tm, n), _vec_spec(n)],
        out_specs=(_row_spec(tm, n), _acc_spec(n)),
        out_shape=(jax.ShapeDtypeStruct((M, n), F32), jax.ShapeDtypeStruct((8, n), F32)),
        name=name, task=task,
    )(dx_in, x, dh, g_pre)


def _gain_grad_call(x, g, dy_a, dy_b, *, name):
    M, n = x.shape

    def body(x_ref, g_ref, a_ref, b_ref, dg_ref):
        _, dg = _rms_bwd(x_ref[...], g_ref[...], a_ref[...] + b_ref[...])
        dg_ref[...] = jnp.zeros_like(dg_ref)
        dg_ref[0:1, :] = dg

    return pl.pallas_call(
        body, grid=(1,),
        in_specs=[_row_spec(M, n), _vec_spec(n), _row_spec(M, n), _row_spec(M, n)],
        out_specs=_acc_spec(n), out_shape=jax.ShapeDtypeStruct((8, n), F32),
        name=name, compiler_params=_cparams(),
    )(x, g, dy_a, dy_b)


def _head_group_matrix():
    a = lax.broadcasted_iota(jnp.int32, (D_GRP, D_GRP), 0) // HEAD
    b = lax.broadcasted_iota(jnp.int32, (D_GRP, D_GRP), 1) // HEAD
    return jnp.where(a == b, 1.0, 0.0).astype(BF16)


def _mix_norm_fwd_call(yf, yc, gf, gc, *, name, tm=1024):
    M = yf.shape[0]

    def body(yf_ref, yc_ref, gf_ref, gc_ref, o_ref):
        o_ref[:, 0:D_GRP] = _rms(yf_ref[...], gf_ref[...]).astype(BF16)
        o_ref[:, D_GRP:D] = _rms(yc_ref[...], gc_ref[...]).astype(BF16)

    return pl.pallas_call(
        body, grid=(M // tm,),
        in_specs=[_row_spec(tm, D_GRP), _row_spec(tm, D_GRP), _vec_spec(D_GRP), _vec_spec(D_GRP)],
        out_specs=_row_spec(tm, D), out_shape=jax.ShapeDtypeStruct((M, D), BF16),
        name=name, compiler_params=_cparams(),
    )(yf, yc, gf, gc)


def _mix_norm_bwd_call(dyn, yf, yc, gf, gc, *, name, tm=2 * TQ):
    M = yf.shape[0]

    def body(dyn_ref, yf_ref, yc_ref, gf_ref, gc_ref, dof_ref, doc_ref, delta_ref, dgf_ref, dgc_ref):
        i = pl.program_id(0)
        yf_ = yf_ref[...]
        dof, dgf = _rms_bwd(yf_, gf_ref[...], dyn_ref[:, 0:D_GRP])
        doc, dgc = _rms_bwd(yc_ref[...], gc_ref[...], dyn_ref[:, D_GRP:D])
        dof_b = dof.astype(BF16)
        dof_ref[...] = dof_b
        doc_ref[...] = doc.astype(BF16)
        prod = dof_b.astype(F32) * yf_
        hi = prod.astype(BF16)
        lo = (prod - hi.astype(F32)).astype(BF16)
        grp = _head_group_matrix()
        delta = _dot(hi, grp) + _dot(lo, grp)
        for b in range(tm // TQ):
            delta_ref[b] = delta[b * TQ:(b + 1) * TQ, :].T
        _acc_add(dgf_ref, dgf, i)
        _acc_add(dgc_ref, dgc, i)

    return pl.pallas_call(
        body, grid=(M // tm,),
        in_specs=[_row_spec(tm, D), _row_spec(tm, D_GRP), _row_spec(tm, D_GRP), _vec_spec(D_GRP), _vec_spec(D_GRP)],
        out_specs=(_row_spec(tm, D_GRP), _row_spec(tm, D_GRP),
                   pl.BlockSpec((tm // TQ, D_GRP, TQ), lambda i: (i, 0, 0)), _acc_spec(D_GRP), _acc_spec(D_GRP)),
        out_shape=(jax.ShapeDtypeStruct((M, D_GRP), BF16), jax.ShapeDtypeStruct((M, D_GRP), BF16),
                   jax.ShapeDtypeStruct((M // TQ, D_GRP, TQ), F32), jax.ShapeDtypeStruct((8, D_GRP), F32),
                   jax.ShapeDtypeStruct((8, D_GRP), F32)),
        name=name, compiler_params=_cparams(),
    )(dyn, yf, yc, gf, gc)


def _tri(n, lower_incl):
    a = lax.broadcasted_iota(jnp.int32, (n, n), 0)
    b = lax.broadcasted_iota(jnp.int32, (n, n), 1)
    return jnp.where(a >= b, 1.0, 0.0).astype(BF16) if lower_incl else jnp.where(a <= b, 1.0, 0.0).astype(BF16)


def _fox_prep_call(fl_raw, b_pad, *, name):
    S = fl_raw.shape[0]
    nb = S // TQ

    def body(fl_ref, b_ref, crep_ref, ct_ref, carry_ref):
        i = pl.program_id(0)

        @pl.when(i == 0)
        def _():
            carry_ref[...] = jnp.zeros_like(carry_ref)
        logf = jax.nn.log_sigmoid(fl_ref[...] + b_ref[...])
        cb = _dot3_l(_tri(TQ, True), logf) + carry_ref[0:1, :]
        carry_ref[0:1, :] = cb[TQ - 1:TQ, :]
        a = lax.broadcasted_iota(jnp.int32, (128, D_GRP), 0)
        b = lax.broadcasted_iota(jnp.int32, (128, D_GRP), 1) // HEAD
        expand = jnp.where(a == b, 1.0, 0.0).astype(BF16)
        crep = _dot3(cb, expand)
        crep_ref[...] = crep
        ct_ref[...] = crep.T

    return pl.pallas_call(
        body, grid=(nb,),
        in_specs=[_row_spec(TQ, 128), _vec_spec(128)],
        out_specs=(_row_spec(TQ, D_GRP), pl.BlockSpec((None, D_GRP, TQ), lambda i: (i, 0, 0))),
        out_shape=(jax.ShapeDtypeStruct((S, D_GRP), F32), jax.ShapeDtypeStruct((nb, D_GRP, TQ), F32)),
        scratch_shapes=[pltpu.VMEM((8, 128), F32)],
        name=name, compiler_params=_cparams(),
    )(fl_raw, b_pad)


def _lane_masks():
    lane = lax.broadcasted_iota(jnp.int32, (1, 128), 1)
    return lane < HEAD, lane >= HEAD


def _fox_fwd_call(proj, c_rep, c_t, *, name, task=None):
    S = proj.shape[0]
    nq = S // TQ
    scale = HEAD ** -0.5

    def body(q_ref, k_ref, v_ref, c_ref, ct_ref, o_ref, lse_ref):
        i = pl.program_id(1)
        m_lo, m_hi = _lane_masks()
        masks = (m_lo, m_hi)
        q = q_ref[...] * scale
        qm = [jnp.where(mk, q, jnp.zeros_like(q)) for mk in masks]
        cq = c_ref[...]
        cqh = [cq[:, 0:1], cq[:, HEAD:HEAD + 1]]
        row = lax.broadcasted_iota(jnp.int32, (TQ, TQ), 0)
        col = lax.broadcasted_iota(jnp.int32, (TQ, TQ), 1)

        def scores(j):
            start = pl.multiple_of(j * TQ, TQ)
            k = k_ref[pl.ds(start, TQ), :]
            ct = ct_ref[j]
            return tuple(_dot_nt(qm[h], k) + (cqh[h] - ct[HEAD * h:HEAD * h + 1, :]) for h in range(2))

        def update(j, ss, state, masked):
            ms, ls, acc = state
            start = pl.multiple_of(j * TQ, TQ)
            v = v_ref[pl.ds(start, TQ), :]
            new_m, new_l, pv, alpha_l = [], [], [], []
            for h in range(2):
                s = ss[h]
                if masked:
                    s = jnp.where(row >= col, s, NEG)
                mn = jnp.maximum(ms[h], jnp.max(s, axis=1, keepdims=True))
                alpha = jnp.exp(ms[h] - mn)
                p = jnp.exp(s - mn)
                new_l.append(alpha * ls[h] + jnp.sum(p, axis=1, keepdims=True))
                new_m.append(mn)
                alpha_l.append(alpha)
                pv.append(_dot(p.astype(BF16), jnp.where(masks[h], v, jnp.zeros_like(v))))
            alpha_lane = jnp.where(m_lo, alpha_l[0], alpha_l[1])
            acc = acc * alpha_lane + pv[0] + pv[1]
            return (tuple(new_m), tuple(new_l), acc)

        def step(j, carry):
            ss, state = carry
            return (scores(j + 1), update(j, ss, state, False))

        init = ((jnp.full((TQ, 1), NEG, F32),) * 2, (jnp.zeros((TQ, 1), F32),) * 2, jnp.zeros((TQ, 128), F32))
        ss, state = lax.fori_loop(0, i, step, (scores(0), init))
        ms, ls, acc = update(i, ss, state, True)
        l_lane = jnp.where(m_lo, ls[0], ls[1])
        o_ref[...] = acc / l_lane
        lse_ref[...] = jnp.where(m_lo, ms[0] + jnp.log(ls[0]), ms[1] + jnp.log(ls[1])).T

    return _pallas(
        body, grid=(N_PAIR, nq),
        in_specs=[pl.BlockSpec((TQ, 128), lambda p, i: (i, p)),
                  pl.BlockSpec((S, 128), lambda p, i: (0, N_PAIR + p)),
                  pl.BlockSpec((S, 128), lambda p, i: (0, 2 * N_PAIR + p)),
                  pl.BlockSpec((TQ, 128), lambda p, i: (i, p)),
                  pl.BlockSpec((nq, 128, TQ), lambda p, i: (0, p, 0))],
        out_specs=(pl.BlockSpec((TQ, 128), lambda p, i: (i, p)), pl.BlockSpec((None, 128, TQ), lambda p, i: (i, p, 0))),
        out_shape=(jax.ShapeDtypeStruct((S, D_GRP), F32), jax.ShapeDtypeStruct((nq, D_GRP, TQ), F32)),
        name=name, task=task,
    )(proj, proj, proj, c_rep, c_t)


def _fox_bwd_call(proj, do, lse_t, delta_t, c_rep, c_t, *, name, task=None):
    S = proj.shape[0]
    nq = S // TQ
    scale = HEAD ** -0.5

    def body(q_ref, k_ref, v_ref, do_ref, lse_ref, dl_ref, ck_ref, ct_ref,
             dq_ref, dk_ref, dv_ref, dcq_ref, dck_ref, dqa_ref):
        j = pl.program_id(1)
        m_lo, m_hi = _lane_masks()
        masks = (m_lo, m_hi)

        @pl.when(j == 0)
        def _():
            dqa_ref[...] = jnp.zeros_like(dqa_ref)
            dcq_ref[...] = jnp.zeros_like(dcq_ref)
        k = k_ref[...]
        v = v_ref[...]
        km = [jnp.where(mk, k, jnp.zeros_like(k)) for mk in masks]
        ck = ck_ref[...]
        krow = lax.broadcasted_iota(jnp.int32, (TQ, TQ), 0)
        qcol = lax.broadcasted_iota(jnp.int32, (TQ, TQ), 1)

        def probs(i):
            start = pl.multiple_of(i * TQ, TQ)
            q = q_ref[pl.ds(start, TQ), :]
            do = do_ref[pl.ds(start, TQ), :]
            lse = lse_ref[i]
            cq = ct_ref[i]
            out = []
            for h in range(2):
                lo = HEAD * h
                qm = jnp.where(masks[h], q * scale, jnp.zeros_like(q))
                dom = jnp.where(masks[h], do, jnp.zeros_like(do))
                st = _dot_nt(k, qm) + (cq[lo:lo + 1, :] - ck[:, lo:lo + 1])
                out.append((jnp.exp(st - lse[lo:lo + 1, :]), _dot_nt(v, dom)))
            return tuple(out)

        def update(i, pd, carry, masked):
            dk, dv, dck = carry
            start = pl.multiple_of(i * TQ, TQ)
            q = q_ref[pl.ds(start, TQ), :]
            do = do_ref[pl.ds(start, TQ), :]
            dl = dl_ref[i]
            dq = jnp.zeros((TQ, 128), F32)
            new_dck = []
            for h in range(2):
                lo = HEAD * h
                qm = jnp.where(masks[h], q, jnp.zeros_like(q))
                dom = jnp.where(masks[h], do, jnp.zeros_like(do))
                pt, dpt = pd[h]
                if masked:
                    pt = jnp.where(qcol >= krow, pt, 0.0)
                dst = pt * (dpt - dl[lo:lo + 1, :])
                dcq_ref[i, h:h + 1, :] += jnp.sum(dst, axis=0, keepdims=True)
                new_dck.append(dck[h] + jnp.sum(dst, axis=1, keepdims=True))
                dsb = (dst * scale).astype(BF16)
                dv = dv + _dot(pt.astype(BF16), dom)
                dk = dk + _dot(dsb, qm)
                dq = dq + _dot_tn(dsb, km[h])
            dqa_ref[pl.ds(start, TQ), :] += dq
            return (dk, dv, tuple(new_dck))

        def step(i, carry):
            pd, sums = carry
            return (probs(jnp.minimum(i + 1, nq - 1)), update(i, pd, sums, False))

        init = (jnp.zeros((TQ, 128), F32), jnp.zeros((TQ, 128), F32), (jnp.zeros((TQ, 1), F32),) * 2)
        first = probs(j)
        second = probs(jnp.minimum(j + 1, nq - 1))
        _, (dk, dv, dck) = lax.fori_loop(j + 1, nq, step, (second, update(j, first, init, True)))
        dk_ref[...] = dk.astype(BF16)
        dv_ref[...] = dv.astype(BF16)
        dck_ref[...] = -jnp.where(m_lo, dck[0], dck[1])

        @pl.when(j == nq - 1)
        def _():
            dq_ref[...] = dqa_ref[...].astype(BF16)

    res = lambda p, j: (0, p)
    stat = pl.BlockSpec((nq, 128, TQ), lambda p, j: (0, p, 0))
    blk = pl.BlockSpec((TQ, 128), lambda p, j: (j, p))
    return _pallas(
        body, grid=(N_PAIR, nq), task=task,
        in_specs=[pl.BlockSpec((S, 128), res),
                  pl.BlockSpec((TQ, 128), lambda p, j: (j, N_PAIR + p)),
                  pl.BlockSpec((TQ, 128), lambda p, j: (j, 2 * N_PAIR + p)),
                  pl.BlockSpec((S, 128), res), stat, stat, blk, stat],
        out_specs=(pl.BlockSpec((S, 128), res), blk, blk,
                   pl.BlockSpec((None, nq, 8, TQ), lambda p, j: (p, 0, 0, 0)), blk),
        out_shape=(jax.ShapeDtypeStruct((S, D_GRP), BF16), jax.ShapeDtypeStruct((S, D_GRP), BF16),
                   jax.ShapeDtypeStruct((S, D_GRP), BF16), jax.ShapeDtypeStruct((N_PAIR, nq, 8, TQ), F32),
                   jax.ShapeDtypeStruct((S, D_GRP), F32)),
        scratch_shapes=[pltpu.VMEM((S, 128), F32)],
        name=name,
    )(proj, proj, proj, do, lse_t, delta_t, c_rep, c_t)


def _fox_gate_bwd_call(dc_rows, fl_raw, b_pad, *, name):
    S = fl_raw.shape[0]
    nb = S // TQ

    def body(dc_ref, fl_ref, b_ref, dfl_ref, db_ref, carry_ref):
        i = pl.program_id(0)

        @pl.when(i == 0)
        def _():
            carry_ref[...] = jnp.zeros_like(carry_ref)
        rc = _dot3(dc_ref[...], _tri(TQ, True)) + carry_ref[:, 0:1]
        carry_ref[...] = jnp.broadcast_to(rc[:, 0:1], carry_ref.shape)
        fl = fl_ref[...] + b_ref[...]
        dfl = rc.T * jax.nn.sigmoid(-fl)
        dfl_ref[...] = dfl.astype(BF16)
        _acc_add(db_ref, jnp.sum(dfl, axis=0, keepdims=True), i)

    rev = lambda i: (nb - 1 - i, 0)
    return pl.pallas_call(
        body, grid=(nb,),
        in_specs=[pl.BlockSpec((128, TQ), lambda i: (0, nb - 1 - i)), pl.BlockSpec((TQ, 128), rev), _vec_spec(128)],
        out_specs=(pl.BlockSpec((TQ, 128), rev), _acc_spec(128)),
        out_shape=(jax.ShapeDtypeStruct((S, 128), BF16), jax.ShapeDtypeStruct((8, 128), F32)),
        scratch_shapes=[pltpu.VMEM((128, 128), F32)],
        name=name, compiler_params=_cparams(),
    )(dc_rows, fl_raw, b_pad)


def _chk_bias_call(g_rev, *, name):
    def body(g_ref, o_ref):
        x = jnp.broadcast_to(g_ref[...], (TQ, ROLL_W))
        rolled = pltpu.roll(x, ROLL_W - (TQ - 1), 1, stride=1, stride_axis=0)
        qc = lax.broadcasted_iota(jnp.int32, (TQ, WIN), 0) // CHUNK
        kc = lax.broadcasted_iota(jnp.int32, (TQ, WIN), 1) // CHUNK
        band = (kc >= qc) & (kc <= qc + LEFT)
        o_ref[...] = jnp.where(band, rolled[:, 0:WIN], NEG)

    return pl.pallas_call(
        body, grid=(8,),
        in_specs=[pl.BlockSpec((None, 1, ROLL_W), lambda h: (h, 0, 0))],
        out_specs=pl.BlockSpec((None, TQ, WIN), lambda h: (h, 0, 0)),
        out_shape=jax.ShapeDtypeStruct((8, TQ, WIN), F32), name=name, compiler_params=_cparams(),
    )(g_rev.reshape(8, 1, ROLL_W))


def _chk_scores(i, qm, kwin, bias, scale):
    s = _dot_nt(qm * scale, kwin) + bias
    kc = lax.broadcasted_iota(jnp.int32, (TQ, WIN), 1) // CHUNK
    return jnp.where(kc + i * (TQ // CHUNK) >= LEFT, s, NEG)


def _chk_fwd_call(proj, bias, *, name, task=None):
    S = proj.shape[0]
    nq = S // TQ
    scale = HEAD ** -0.5

    def body(q_ref, k_ref, v_ref, b_ref, o_ref, kp_ref, vp_ref):
        i = pl.program_id(1)

        @pl.when(i == 0)
        def _():
            kp_ref[0:PADK, :] = jnp.zeros((PADK, 128), BF16)
            vp_ref[0:PADK, :] = jnp.zeros((PADK, 128), BF16)
            kp_ref[PADK:PADK + S, :] = k_ref[...]
            vp_ref[PADK:PADK + S, :] = v_ref[...]
        masks = _lane_masks()
        q = q_ref[...]
        start = pl.multiple_of(i * TQ, TQ)
        kwin = kp_ref[pl.ds(start, WIN), :]
        vwin = vp_ref[pl.ds(start, WIN), :]
        ss = [_chk_scores(i, jnp.where(masks[h], q, jnp.zeros_like(q)), kwin, b_ref[h], scale) for h in range(2)]
        ps = []
        for s in ss:
            p = jnp.exp(s - jnp.max(s, axis=1, keepdims=True))
            ps.append((p / jnp.sum(p, axis=1, keepdims=True)).astype(BF16))
        o_ref[...] = (_dot(ps[0], jnp.where(masks[0], vwin, jnp.zeros_like(vwin)))
                      + _dot(ps[1], jnp.where(masks[1], vwin, jnp.zeros_like(vwin))))

    c0 = 3 * N_PAIR
    return _pallas(
        body, grid=(N_PAIR, nq), task=task,
        in_specs=[pl.BlockSpec((TQ, 128), lambda p, i: (i, c0 + p)),
                  pl.BlockSpec((S, 128), lambda p, i: (0, c0 + N_PAIR + p)),
                  pl.BlockSpec((S, 128), lambda p, i: (0, c0 + 2 * N_PAIR + p)),
                  pl.BlockSpec((2, TQ, WIN), lambda p, i: (p, 0, 0))],
        out_specs=pl.BlockSpec((TQ, 128), lambda p, i: (i, p)),
        out_shape=jax.ShapeDtypeStruct((S, D_GRP), F32),
        scratch_shapes=[pltpu.VMEM((S + PADK, 128), BF16), pltpu.VMEM((S + PADK, 128), BF16)],
        name=name,
    )(proj, proj, proj, bias)


def _chk_bwd_call(proj, do, bias, *, name, task=None):
    S = proj.shape[0]
    nq = S // TQ
    scale = HEAD ** -0.5

    def body(q_ref, k_ref, v_ref, do_ref, b_ref, dq_ref, dk_ref, dv_ref, dg_ref, kp_ref, vp_ref, dkp_ref, dvp_ref, db_ref):
        i = pl.program_id(1)

        @pl.when(i == 0)
        def _():
            kp_ref[0:PADK, :] = jnp.zeros((PADK, 128), BF16)
            vp_ref[0:PADK, :] = jnp.zeros((PADK, 128), BF16)
            kp_ref[PADK:PADK + S, :] = k_ref[...]
            vp_ref[PADK:PADK + S, :] = v_ref[...]
            dkp_ref[...] = jnp.zeros_like(dkp_ref)
            dvp_ref[...] = jnp.zeros_like(dvp_ref)
            db_ref[...] = jnp.zeros_like(db_ref)
        masks = _lane_masks()
        q = q_ref[...]
        dout = do_ref[...]
        start = pl.multiple_of(i * TQ, TQ)
        kwin = kp_ref[pl.ds(start, WIN), :]
        vwin = vp_ref[pl.ds(start, WIN), :]
        qm = [jnp.where(mk, q, jnp.zeros_like(q)) for mk in masks]
        dom = [jnp.where(mk, dout, jnp.zeros_like(dout)) for mk in masks]
        ss = [_chk_scores(i, qm[h], kwin, b_ref[h], scale) for h in range(2)]
        dps = [_dot_nt(dom[h], vwin) for h in range(2)]
        pbs, dsbs = [], []
        for h in range(2):
            p = jnp.exp(ss[h] - jnp.max(ss[h], axis=1, keepdims=True))
            p = p / jnp.sum(p, axis=1, keepdims=True)
            ds = p * (dps[h] - jnp.sum(p * dps[h], axis=1, keepdims=True))
            db_ref[h] += ds
            pbs.append(p.astype(BF16))
            dsbs.append((ds * scale).astype(BF16))
        dq_ref[...] = (_dot(dsbs[0], jnp.where(masks[0], kwin, jnp.zeros_like(kwin)))
                       + _dot(dsbs[1], jnp.where(masks[1], kwin, jnp.zeros_like(kwin)))).astype(BF16)
        dkp_ref[pl.ds(start, WIN), :] += _dot_tn(dsbs[0], qm[0]) + _dot_tn(dsbs[1], qm[1])
        dvp_ref[pl.ds(start, WIN), :] += _dot_tn(pbs[0], dom[0]) + _dot_tn(pbs[1], dom[1])

        @pl.when(i == nq - 1)
        def _():
            dk_ref[...] = dkp_ref[PADK:PADK + S, :].astype(BF16)
            dv_ref[...] = dvp_ref[PADK:PADK + S, :].astype(BF16)
            a = lax.broadcasted_iota(jnp.int32, (TQ, TQ), 0)
            b = lax.broadcasted_iota(jnp.int32, (TQ, TQ), 1)
            flip = jnp.where(a + b == TQ - 1, 1.0, 0.0).astype(BF16)
            e = lax.broadcasted_iota(jnp.int32, (1, ROLL_W), 1)
            dg_ref[...] = jnp.zeros_like(dg_ref)
            for h in range(2):
                rev = _dot3_l(flip, db_ref[h])
                wide = jnp.concatenate([rev, jnp.zeros((TQ, ROLL_W - WIN), F32)], axis=1)
                diag = pltpu.roll(wide, 0, 1, stride=1, stride_axis=0)
                dg = jnp.sum(diag, axis=0, keepdims=True)
                lo = jnp.sum(jnp.where(e <= 639, dg, 0.0), axis=1, keepdims=True)
                hi = jnp.sum(jnp.where(e >= 895, dg, 0.0), axis=1, keepdims=True)
                dg_ref[h:h + 1, :] = jnp.where(e == 639, lo, jnp.where(e == 895, hi, dg))

    c0 = 3 * N_PAIR
    res = lambda p, i: (0, p)
    return _pallas(
        body, grid=(N_PAIR, nq), task=task,
        in_specs=[pl.BlockSpec((TQ, 128), lambda p, i: (i, c0 + p)),
                  pl.BlockSpec((S, 128), lambda p, i: (0, c0 + N_PAIR + p)),
                  pl.BlockSpec((S, 128), lambda p, i: (0, c0 + 2 * N_PAIR + p)),
                  pl.BlockSpec((TQ, 128), lambda p, i: (i, p)),
                  pl.BlockSpec((2, TQ, WIN), lambda p, i: (p, 0, 0))],
        out_specs=(pl.BlockSpec((TQ, 128), lambda p, i: (i, p)), pl.BlockSpec((S, 128), res),
                   pl.BlockSpec((S, 128), res), pl.BlockSpec((None, 8, ROLL_W), lambda p, i: (p, 0, 0))),
        out_shape=(jax.ShapeDtypeStruct((S, D_GRP), BF16), jax.ShapeDtypeStruct((S, D_GRP), BF16),
                   jax.ShapeDtypeStruct((S, D_GRP), BF16), jax.ShapeDtypeStruct((N_PAIR, 8, ROLL_W), F32)),
        scratch_shapes=[pltpu.VMEM((S + PADK, 128), BF16), pltpu.VMEM((S + PADK, 128), BF16),
                        pltpu.VMEM((S + PADK, 128), F32), pltpu.VMEM((S + PADK, 128), F32),
                        pltpu.VMEM((2, TQ, WIN), F32)],
        name=name,
    )(proj, proj, proj, do, bias)


def _mem_fwd_call(q, k, v, *, name, tq=2048):
    S = q.shape[0]
    scale = MEM_HD ** -0.5

    def body(q_ref, k_ref, v_ref, o_ref):
        s = _dot_nt(q_ref[...] * scale, k_ref[...])
        p = jnp.exp(s - jnp.max(s, axis=1, keepdims=True))
        p = p / jnp.sum(p, axis=1, keepdims=True)
        o_ref[...] = _dot(p.astype(BF16), v_ref[...]).astype(BF16)

    return pl.pallas_call(
        body, grid=(MEM_HEADS, S // tq),
        in_specs=[pl.BlockSpec((tq, MEM_HD), lambda h, i: (i, h)),
                  pl.BlockSpec((N_MEM, MEM_HD), lambda h, i: (0, h)),
                  pl.BlockSpec((N_MEM, MEM_HD), lambda h, i: (0, h))],
        out_specs=pl.BlockSpec((tq, MEM_HD), lambda h, i: (i, h)),
        out_shape=jax.ShapeDtypeStruct((S, D), BF16), name=name, compiler_params=_cparams(),
    )(q, k, v)


def _mem_bwd_call(q, k, v, do, *, name, tq=2048):
    S = q.shape[0]
    n = S // tq
    scale = MEM_HD ** -0.5

    def body(q_ref, k_ref, v_ref, do_ref, dq_ref, dk_ref, dv_ref, dka_ref, dva_ref):
        i = pl.program_id(1)

        @pl.when(i == 0)
        def _():
            dka_ref[...] = jnp.zeros_like(dka_ref)
            dva_ref[...] = jnp.zeros_like(dva_ref)
        qb = q_ref[...]
        kb = k_ref[...]
        dob = do_ref[...]
        s = _dot_nt(qb * scale, kb)
        p = jnp.exp(s - jnp.max(s, axis=1, keepdims=True))
        p = p / jnp.sum(p, axis=1, keepdims=True)
        dp = _dot_nt(dob, v_ref[...])
        ds = p * (dp - jnp.sum(p * dp, axis=1, keepdims=True))
        dsb = (ds * scale).astype(BF16)
        dq_ref[...] = _dot(dsb, kb).astype(BF16)
        dka_ref[...] += _dot_tn(dsb, qb)
        dva_ref[...] += _dot_tn(p.astype(BF16), dob)

        @pl.when(i == n - 1)
        def _():
            dk_ref[...] = dka_ref[...].astype(BF16)
            dv_ref[...] = dva_ref[...].astype(BF16)

    kv = pl.BlockSpec((N_MEM, MEM_HD), lambda h, i: (0, h))
    qs = pl.BlockSpec((tq, MEM_HD), lambda h, i: (i, h))
    return pl.pallas_call(
        body, grid=(MEM_HEADS, n), in_specs=[qs, kv, kv, qs], out_specs=(qs, kv, kv),
        out_shape=(jax.ShapeDtypeStruct((S, D), BF16), jax.ShapeDtypeStruct((N_MEM, D), BF16),
                   jax.ShapeDtypeStruct((N_MEM, D), BF16)),
        scratch_shapes=[pltpu.VMEM((N_MEM, MEM_HD), F32), pltpu.VMEM((N_MEM, MEM_HD), F32)],
        name=name, compiler_params=_cparams(),
    )(q, k, v, do)


def _rel_table_to_g(rel):
    return jnp.concatenate([
        jnp.broadcast_to(rel[:, N_REL - 1:N_REL], (8, 640)),
        rel[:, 1:N_REL - 1][:, ::-1],
        jnp.broadcast_to(rel[:, 0:1], (8, 129)),
    ], axis=1)


def _g_to_rel_table(dg):
    return dg[:, 639:896][:, ::-1]


def _place():
    x, y, c = lax.axis_index("x"), lax.axis_index("y"), lax.axis_index("c")
    others = [(1 - x, y), (x, 1 - y), (1 - x, 1 - y)]
    return x, y, c, others


def _half(c, rows):
    hr = rows // 2
    return pl.ds(pl.multiple_of(c * hr, 16), hr)


def _dma_sems(*shape):
    return pltpu.SemaphoreType.DMA(shape)


def _cast_slabs_call(ws, chip_arr, *, name, tm=256, task=None):
    n = len(ws)
    cols = ws[0].shape[1]
    tiles = [w.shape[0] // tm for w in ws]
    steps = max(tiles)

    def body(chip_ref, *refs):
        i = pl.program_id(0)
        for k in range(n):
            def cast(k=k):
                refs[n + k][...] = refs[k][...].astype(BF16)
            if tiles[k] == steps:
                cast()
            else:
                pl.when(i < tiles[k])(cast)

    in_specs = [pl.BlockSpec((tm, cols), lambda i, chip, t=t: (jnp.minimum(i, t - 1), 0)) for t in tiles]
    out_specs = [pl.BlockSpec((None, tm, cols), lambda i, chip, t=t: (chip[0], jnp.minimum(i, t - 1), 0)) for t in tiles]
    out_shape = [jax.ShapeDtypeStruct((N_CHIP,) + w.shape, BF16) for w in ws]
    return _pallas(body, grid=(steps,), in_specs=in_specs, out_specs=out_specs, out_shape=out_shape, name=name,
                   task=task, prefetch=1)(chip_arr, *ws)


def _cast_slab_call(w, chip_arr, *, name, tm=256, pad_rows=0):
    rows, cols = w.shape
    if pad_rows:
        tm = rows
    tm = min(tm, rows)

    def body(chip_ref, w_ref, o_ref):
        o_ref[0:tm, :] = w_ref[...].astype(BF16)
        if pad_rows:
            o_ref[tm:tm + pad_rows, :] = jnp.zeros((pad_rows, cols), BF16)

    return pl.pallas_call(
        body,
        grid_spec=pltpu.PrefetchScalarGridSpec(
            num_scalar_prefetch=1, grid=(rows // tm,),
            in_specs=[pl.BlockSpec((tm, cols), lambda i, chip: (i, 0))],
            out_specs=pl.BlockSpec((None, tm + pad_rows, cols), lambda i, chip: (chip[0], i, 0))),
        out_shape=jax.ShapeDtypeStruct((N_CHIP, rows + pad_rows, cols), BF16), name=name,
        compiler_params=_cparams(),
    )(chip_arr, w)


def _ag_ici_task(gathered):
    n = len(gathered)

    def copies(ins, outs, sems):
        send_sems, recv_sems = sems
        x, y, c, others = _place()
        me = 2 * x + y
        for k in range(n):
            mine = _half(c, gathered[k].shape[1])
            for t, (ox, oy) in enumerate(others):
                yield [pltpu.make_async_remote_copy(
                    src_ref=ins[k].at[me, mine], dst_ref=outs[k].at[slab, mine],
                    send_sem=send_sems.at[k, t], recv_sem=recv_sems.at[k, t],
                    device_id=(ox, oy, c), device_id_type=MESH) for slab in (me, 2 * ox + oy)]

    def issue(ins, outs, sems):
        for outgoing, _ in copies(ins, outs, sems):
            outgoing.start()

    def drain(ins, outs, sems):
        for outgoing, incoming in copies(ins, outs, sems):
            incoming.wait_recv()
            outgoing.wait_send()

    return _Task(gathered, [jax.ShapeDtypeStruct(g.shape, g.dtype) for g in gathered],
                 [_dma_sems(n, 3), _dma_sems(n, 3)], issue, drain, aliases={k: k for k in range(n)})


def _ag_d2d_task(gathered):
    n = len(gathered)

    def copies(ins, outs, sems):
        send_sems, recv_sems = sems
        x, y, c, others = _place()
        for k in range(n):
            rows = gathered[k].shape[1]
            mine, theirs = _half(c, rows), _half(1 - c, rows)
            for t, (ox, oy) in enumerate(others):
                slab = 2 * ox + oy
                pair = [pltpu.make_async_remote_copy(
                    src_ref=ins[k].at[slab, half], dst_ref=outs[k].at[slab, half],
                    send_sem=send_sems.at[k, t], recv_sem=recv_sems.at[k, t],
                    device_id=(x, y, 1 - c), device_id_type=MESH) for half in (mine, theirs)]
                yield pair

    def issue(ins, outs, sems):
        for outgoing, _ in copies(ins, outs, sems):
            outgoing.start()

    def drain(ins, outs, sems):
        for outgoing, incoming in copies(ins, outs, sems):
            incoming.wait_recv()
            outgoing.wait_send()

    return _Task(gathered, [jax.ShapeDtypeStruct(g.shape, g.dtype) for g in gathered],
                 [_dma_sems(n, 3), _dma_sems(n, 3)], issue, drain, aliases={k: k for k in range(n)})


def _rs_pair_task(ds):
    n = len(ds)

    def copies(ins, outs, sems):
        send_sems, recv_sems = sems
        x, y, c, _ = _place()
        for k in range(n):
            yield pltpu.make_async_remote_copy(
                src_ref=ins[k].at[:, _half(1 - c, ds[k].shape[1])], dst_ref=outs[k],
                send_sem=send_sems.at[k], recv_sem=recv_sems.at[k],
                device_id=(x, y, 1 - c), device_id_type=MESH)

    def issue(ins, outs, sems):
        for cp in copies(ins, outs, sems):
            cp.start()

    def drain(ins, outs, sems):
        for cp in copies(ins, outs, sems):
            cp.wait()

    return _Task(ds, [jax.ShapeDtypeStruct((N_CHIP, d.shape[1] // 2, d.shape[2]), d.dtype) for d in ds],
                 [_dma_sems(n), _dma_sems(n)], issue, drain)


def _pair_add_call(d, r1, c_arr, *, name, tm=512):
    _, rows, cols = d.shape
    hr = rows // 2
    tm = tm if hr % tm == 0 else hr
    nb = hr // tm

    def body(c_ref, d_ref, r_ref, o_ref):
        o_ref[...] = (d_ref[...].astype(F32) + r_ref[...].astype(F32)).astype(BF16)

    return pl.pallas_call(
        body,
        grid_spec=pltpu.PrefetchScalarGridSpec(
            num_scalar_prefetch=1, grid=(N_CHIP, nb),
            in_specs=[pl.BlockSpec((None, tm, cols), lambda j, i, c: (j, c[0] * nb + i, 0)),
                      pl.BlockSpec((None, tm, cols), lambda j, i, c: (j, i, 0))],
            out_specs=pl.BlockSpec((None, tm, cols), lambda j, i, c: (j, i, 0))),
        out_shape=jax.ShapeDtypeStruct((N_CHIP, hr, cols), BF16), name=name, compiler_params=_cparams(),
    )(c_arr, d, r1)


def _rs_chip_task(ps):
    n = len(ps)

    def copies(ins, outs, sems):
        send_sems, recv_sems = sems
        x, y, c, others = _place()
        for k in range(n):
            for t, (ox, oy) in enumerate(others):
                yield pltpu.make_async_remote_copy(
                    src_ref=ins[k].at[2 * ox + oy], dst_ref=outs[k].at[t],
                    send_sem=send_sems.at[k, t], recv_sem=recv_sems.at[k, t],
                    device_id=(ox, oy, c), device_id_type=MESH)

    def issue(ins, outs, sems):
        for cp in copies(ins, outs, sems):
            cp.start()

    def drain(ins, outs, sems):
        for cp in copies(ins, outs, sems):
            cp.wait()

    return _Task(ps, [jax.ShapeDtypeStruct((3,) + p.shape[1:], p.dtype) for p in ps],
                 [_dma_sems(n, 3), _dma_sems(n, 3)], issue, drain)


def _chip_sum_call(p, r2, place_arr, *, name, tm=512):
    _, hr, cols = r2.shape
    tm = tm if hr % tm == 0 else hr
    nb = hr // tm

    def body(place_ref, p_ref, r_ref, o_ref):
        acc = p_ref[...].astype(F32)
        for j in range(3):
            acc = acc + r_ref[j].astype(F32)
        o_ref[...] = acc

    return pl.pallas_call(
        body,
        grid_spec=pltpu.PrefetchScalarGridSpec(
            num_scalar_prefetch=1, grid=(nb,),
            in_specs=[pl.BlockSpec((None, tm, cols), lambda i, pc: (pc[0], i, 0)),
                      pl.BlockSpec((3, tm, cols), lambda i, pc: (0, i, 0))],
            out_specs=pl.BlockSpec((tm, cols), lambda i, pc: (pc[1] * nb + i, 0))),
        out_shape=jax.ShapeDtypeStruct((2 * hr, cols), F32), name=name, compiler_params=_cparams(),
    )(place_arr, p, r2)


def _rs_gather_task(gs):
    n = len(gs)

    def copies(ins, outs, sems):
        send_sems, recv_sems = sems
        x, y, c, _ = _place()
        for k in range(n):
            rows = gs[k].shape[0]
            mine, theirs = _half(c, rows), _half(1 - c, rows)
            yield [pltpu.make_async_remote_copy(
                src_ref=ins[k].at[mine], dst_ref=outs[k].at[half],
                send_sem=send_sems.at[k], recv_sem=recv_sems.at[k],
                device_id=(x, y, 1 - c), device_id_type=MESH) for half in (mine, theirs)]

    def issue(ins, outs, sems):
        for outgoing, _ in copies(ins, outs, sems):
            outgoing.start()

    def drain(ins, outs, sems):
        for outgoing, incoming in copies(ins, outs, sems):
            incoming.wait_recv()
            outgoing.wait_send()

    return _Task(gs, [jax.ShapeDtypeStruct(g.shape, g.dtype) for g in gs],
                 [_dma_sems(n), _dma_sems(n)], issue, drain, aliases={k: k for k in range(n)})


def _adamw(w, g, m, v):
    m = ADAM_B1 * m + (1.0 - ADAM_B1) * g
    v = ADAM_B2 * v + (1.0 - ADAM_B2) * jnp.square(g)
    m_hat = m / (1.0 - ADAM_B1 ** ADAM_STEP)
    v_hat = v / (1.0 - ADAM_B2 ** ADAM_STEP)
    delta = -ADAM_LR * (m_hat / (jnp.sqrt(v_hat) + ADAM_EPS) + ADAM_WD * w)
    return delta, m, v


def _adamw_call(items, *, name, tm=256, task=None):
    n = len(items)
    cols = items[0][0].shape[1]
    tiles = [it[0].shape[0] // tm for it in items]
    steps = max(tiles)

    def body(*refs):
        i = pl.program_id(0)
        ins, outs = refs[:4 * n], refs[4 * n:]
        for k in range(n):
            def update(k=k):
                g = ins[4 * k + 1][...]
                res = _adamw(ins[4 * k][...], g, ins[4 * k + 2][...], ins[4 * k + 3][...])
                outs[4 * k][...] = g
                for j in range(3):
                    outs[4 * k + 1 + j][...] = res[j]
            if tiles[k] == steps:
                update()
            else:
                pl.when(i < tiles[k])(update)

    in_specs, out_specs, out_shape, args = [], [], [], []
    for it, t in zip(items, tiles):
        spec = pl.BlockSpec((tm, cols), lambda i, t=t: (jnp.minimum(i, t - 1), 0))
        in_specs += [spec] * 4
        out_specs += [spec] * 4
        out_shape += [jax.ShapeDtypeStruct(it[0].shape, F32)] * 4
        args += list(it)
    res = _pallas(body, grid=(steps,), in_specs=in_specs, out_specs=out_specs, out_shape=out_shape,
                  name=name, task=task)(*args)
    outs, extra = res if task is not None else (res, None)
    grouped = [tuple(outs[4 * k:4 * k + 4]) for k in range(n)]
    return (grouped, extra) if task is not None else grouped


def _adamw_cols_call(w, g_pad, m, v, *, name, tn=512):
    rows, cols = w.shape

    def body(w_ref, g_ref, m_ref, v_ref, go_ref, d_ref, mo_ref, vo_ref):
        g = g_ref[0:rows, :]
        d, mn, vn = _adamw(w_ref[...], g, m_ref[...], v_ref[...])
        go_ref[...] = g
        d_ref[...] = d
        mo_ref[...] = mn
        vo_ref[...] = vn

    spec = pl.BlockSpec((rows, tn), lambda j: (0, j))
    gspec = pl.BlockSpec((g_pad.shape[0], tn), lambda j: (0, j))
    return _pallas(body, grid=(cols // tn,), in_specs=[spec, gspec, spec, spec], out_specs=(spec,) * 4,
                   out_shape=(jax.ShapeDtypeStruct((rows, cols), F32),) * 4, name=name)(w, g_pad, m, v)


N_DEV = 8
SMALL_ROWS = 24
SMALL_LAYOUT = {
    "g_mix_pre": (0, 0, 1, D), "g_mix_post": (1, 0, 1, D), "g_mem_kv": (2, 0, 1, D), "g_mem_pre": (3, 0, 1, D),
    "g_mem_post": (4, 0, 1, D), "g_ff_pre": (5, 0, 1, D), "g_ff_post": (6, 0, 1, D),
    "g_fox_out": (7, 0, 1, D_GRP), "g_chk_out": (7, D_GRP, 1, D_GRP), "b_fgt": (8, 0, 1, 8),
    "rel_bias": (16, 0, 8, N_REL),
}
SMALL = list(SMALL_LAYOUT)


LOSS_ROW = 9


def _small_call(grads, ws, ms, vs, loss_blk, task, *, name):
    n = len(SMALL)
    t_in, t_out = len(task.arrays), len(task.out_shapes)

    def body(*refs):
        g_refs, w_refs, m_refs, v_refs = (refs[j * n:(j + 1) * n] for j in range(4))
        p = 4 * n
        loss_ref, tins = refs[p], refs[p + 1:p + 1 + t_in]
        p += 1 + t_in
        outs, loss_out, touts = refs[p:p + 4 * n], refs[p + 4 * n], refs[p + 4 * n + 1:p + 4 * n + 1 + t_out]
        p += 4 * n + 1 + t_out
        mine, slots, send_sems, recv_sems = refs[p:p + 4]
        tsems = refs[p + 4:]
        task.issue(tins, touts, tsems)
        x, y, c, _ = _place()
        me = 4 * x + 2 * y + c
        mine[...] = jnp.zeros_like(mine)
        for k, name_k in enumerate(SMALL):
            r, l, nr, nl = SMALL_LAYOUT[name_k]
            mine[r:r + nr, l:l + nl] = g_refs[k][0:nr, 0:nl]
        mine[LOSS_ROW:LOSS_ROW + 1, 0:128] = loss_ref[0:1, :]
        slots[me] = mine[...]
        peers = [(dx, dy, dc) for dx in (0, 1) for dy in (0, 1) for dc in (0, 1)][1:]
        cps = []
        for t, (dx, dy, dc) in enumerate(peers):
            px, py, pc = (x + dx) % 2, (y + dy) % 2, (c + dc) % 2
            cps.append(pltpu.make_async_remote_copy(
                src_ref=mine, dst_ref=slots.at[me], send_sem=send_sems.at[t], recv_sem=recv_sems.at[t],
                device_id=(px, py, pc), device_id_type=MESH))
            cps[-1].start()
        for t, (dx, dy, dc) in enumerate(peers):
            px, py, pc = (x + dx) % 2, (y + dy) % 2, (c + dc) % 2
            pltpu.make_async_remote_copy(
                src_ref=mine, dst_ref=slots.at[4 * px + 2 * py + pc], send_sem=send_sems.at[t],
                recv_sem=recv_sems.at[t], device_id=(px, py, pc), device_id_type=MESH).wait_recv()
        for cp in cps:
            cp.wait_send()
        total = slots[0]
        for j in range(1, N_DEV):
            total = total + slots[j]
        for k, name_k in enumerate(SMALL):
            r, l, nr, nl = SMALL_LAYOUT[name_k]
            g = total[r:r + nr, l:l + nl]
            d, mn, vn = _adamw(w_refs[k][...], g, m_refs[k][...], v_refs[k][...])
            for j, val in enumerate((g, d, mn, vn)):
                outs[4 * k + j][...] = val
        loss_out[...] = jnp.broadcast_to(total[LOSS_ROW:LOSS_ROW + 1, 0:128], loss_out.shape)
        task.drain(tins, touts, tsems)

    vm = pl.BlockSpec(memory_space=pltpu.VMEM)
    out_shape = [jax.ShapeDtypeStruct(ws[k].shape, F32) for k in SMALL for _ in range(4)]
    out_shape += [jax.ShapeDtypeStruct((8, 128), F32)] + list(task.out_shapes)
    res = pl.pallas_call(
        body, in_specs=[vm] * (4 * n + 1) + [ANY] * t_in, out_specs=[vm] * (4 * n + 1) + [ANY] * t_out,
        out_shape=out_shape,
        scratch_shapes=[pltpu.VMEM((SMALL_ROWS, D), F32), pltpu.VMEM((N_DEV, SMALL_ROWS, D), F32),
                        _dma_sems(N_DEV - 1), _dma_sems(N_DEV - 1)] + list(task.sems),
        input_output_aliases={4 * n + 1 + i: 4 * n + 1 + j for i, j in task.aliases.items()},
        name=name,
    )(*[d[k] for d in (grads, ws, ms, vs) for k in SMALL], loss_blk, *task.arrays)
    return ({k: tuple(res[4 * i:4 * i + 4]) for i, k in enumerate(SMALL)}, res[4 * n], list(res[4 * n + 1:]))


WEIGHTS = ["w_in", "b_fgt", "rel_bias", "g_fox_out", "g_chk_out", "w_out", "g_mix_pre", "g_mix_post", "g_mem_kv",
           "w_mq", "w_mk", "w_mv", "w_mo", "g_mem_pre", "g_mem_post", "w_ff1", "w_ff2", "g_ff_pre", "g_ff_post"]
BIG = ["w_in", "w_out", "w_mq", "w_mk", "w_mv", "w_mo", "w_ff1", "w_ff2"]


IN_SHARD = D_IN // N_CHIP
IN_PAD = 800


IN_PIECES = [(0, 0, 770), (800, 770, 766), (1566, 3072, 4), (1600, 3076, 4), (1604, 1536, 766), (2400, 2302, 770)]
PAD_ZEROS = [(800 * j + IN_SHARD, IN_PAD - IN_SHARD) for j in range(N_CHIP)]
ALL_ZEROS = [(D_IN, D_ALL - D_IN)]


def _reorder_rows_call(src, to_all, *, name, tn=512):
    rows, cols = src.shape
    zeros = ALL_ZEROS if to_all else PAD_ZEROS

    def body(s_ref, o_ref):
        for pad0, all0, cnt in IN_PIECES:
            s0, d0 = (pad0, all0) if to_all else (all0, pad0)
            o_ref[d0:d0 + cnt, :] = s_ref[s0:s0 + cnt, :]
        for z0, cnt in zeros:
            o_ref[z0:z0 + cnt, :] = jnp.zeros((cnt, tn), src.dtype)

    spec = pl.BlockSpec((rows, tn), lambda j: (0, j))
    return _pallas(body, grid=(cols // tn,), in_specs=[spec], out_specs=spec,
                   out_shape=jax.ShapeDtypeStruct((rows, cols), src.dtype), name=name)(src)


def kernel(x, mem, w_in, b_fgt, rel_bias, g_fox_out, g_chk_out, w_out, g_mix_pre, g_mix_post, g_mem_kv, w_mq, w_mk, w_mv, w_mo, g_mem_pre, g_mem_post, w_ff1, w_ff2, g_ff_pre, g_ff_post, loss_target, m_w_in, m_b_fgt, m_rel_bias, m_g_fox_out, m_g_chk_out, m_w_out, m_g_mix_pre, m_g_mix_post, m_g_mem_kv, m_w_mq, m_w_mk, m_w_mv, m_w_mo, m_g_mem_pre, m_g_mem_post, m_w_ff1, m_w_ff2, m_g_ff_pre, m_g_ff_post, v_w_in, v_b_fgt, v_rel_bias, v_g_fox_out, v_g_chk_out, v_w_out, v_g_mix_pre, v_g_mix_post, v_g_mem_kv, v_w_mq, v_w_mk, v_w_mv, v_w_mo, v_g_mem_pre, v_g_mem_post, v_w_ff1, v_w_ff2, v_g_ff_pre, v_g_ff_post):
    w = dict(w_in=w_in, b_fgt=b_fgt, rel_bias=rel_bias, g_fox_out=g_fox_out, g_chk_out=g_chk_out, w_out=w_out,
             g_mix_pre=g_mix_pre, g_mix_post=g_mix_post, g_mem_kv=g_mem_kv, w_mq=w_mq, w_mk=w_mk, w_mv=w_mv,
             w_mo=w_mo, g_mem_pre=g_mem_pre, g_mem_post=g_mem_post, w_ff1=w_ff1, w_ff2=w_ff2, g_ff_pre=g_ff_pre,
             g_ff_post=g_ff_post)
    m = dict(w_in=m_w_in, b_fgt=m_b_fgt, rel_bias=m_rel_bias, g_fox_out=m_g_fox_out, g_chk_out=m_g_chk_out,
             w_out=m_w_out, g_mix_pre=m_g_mix_pre, g_mix_post=m_g_mix_post, g_mem_kv=m_g_mem_kv, w_mq=m_w_mq,
             w_mk=m_w_mk, w_mv=m_w_mv, w_mo=m_w_mo, g_mem_pre=m_g_mem_pre, g_mem_post=m_g_mem_post,
             w_ff1=m_w_ff1, w_ff2=m_w_ff2, g_ff_pre=m_g_ff_pre, g_ff_post=m_g_ff_post)
    v = dict(w_in=v_w_in, b_fgt=v_b_fgt, rel_bias=v_rel_bias, g_fox_out=v_g_fox_out, g_chk_out=v_g_chk_out,
             w_out=v_w_out, g_mix_pre=v_g_mix_pre, g_mix_post=v_g_mix_post, g_mem_kv=v_g_mem_kv, w_mq=v_w_mq,
             w_mk=v_w_mk, w_mv=v_w_mv, w_mo=v_w_mo, g_mem_pre=v_g_mem_pre, g_mem_post=v_g_mem_post,
             w_ff1=v_w_ff1, w_ff2=v_w_ff2, g_ff_pre=v_g_ff_pre, g_ff_post=v_g_ff_post)

    def rows(d, k):
        return d[k][0] if k == "rel_bias" else d[k]

    xs, mems, target = x[0], mem[0], loss_target[0]
    S = xs.shape[0]
    sp = {k: rows(w, k) for k in SMALL}
    b_pad = jnp.pad(sp["b_fgt"], ((0, 0), (0, 120)))
    chip = 2 * lax.axis_index("x") + lax.axis_index("y")
    chip_arr = jnp.reshape(chip, (1,)).astype(jnp.int32)
    c_arr = jnp.reshape(lax.axis_index("c"), (1,)).astype(jnp.int32)
    place_arr = jnp.concatenate([chip_arr, c_arr])
    w_in_t, m_in_t, v_in_t = w["w_in"][0].T, m["w_in"][0].T, v["w_in"][0].T
    slab = {"w_in": _cast_slab_call(w_in_t, chip_arr, name="cast_w_in", pad_rows=IN_PAD - IN_SHARD)}

    def gather_ici(names):
        return _ag_ici_task([slab[k] for k in names])

    def pair_add(k, d, r1):
        return _pair_add_call(d, r1, c_arr, name="rs_pair_add_" + k)

    rest, (g_in,) = _cast_slabs_call([w[k][0] for k in BIG[1:]], chip_arr, name="cast_rest",
                                     task=gather_ici(["w_in"]))
    slab.update(zip(BIG[1:], rest))
    h1, (g_in,) = _rms_fwd_call(xs, sp["g_mix_pre"], name="rms_mix_pre", task=_ag_d2d_task([g_in]))
    w_all_t = _reorder_rows_call(g_in.reshape(N_CHIP * IN_PAD, D), True, name="w_in_rows")
    proj, (g_out, g_mq) = _mm_nt(h1, w_all_t, "plain", rows=(0, 3072), name="mm_proj",
                                 task=gather_ici(["w_out", "w_mq"]))
    fl_raw = _mm_nt(h1, w_all_t, "plain", rows=(3072, 128), name="mm_gate", out_dtype=F32, tn=128)
    c_rep, c_t = _fox_prep_call(fl_raw, b_pad, name="fox_prep")
    bias = _chk_bias_call(_rel_table_to_g(sp["rel_bias"]), name="chk_bias")
    mid = ["w_mk", "w_mv", "w_mo", "w_ff1"]
    (yf, lse), got = _fox_fwd_call(proj, c_rep, c_t, name="fox_fwd",
                                   task=_merge_tasks([gather_ici(mid), _ag_d2d_task([g_out, g_mq])]))
    g_mid, (g_out, g_mq) = got[:4], got[4:]
    yc, got = _chk_fwd_call(proj, bias, name="chk_fwd",
                            task=_merge_tasks([gather_ici(["w_ff2"]), _ag_d2d_task(g_mid)]))
    g_ff2, (g_mk, g_mv, g_mo, g_ff1) = got[0], got[1:]
    yn = _mix_norm_fwd_call(yf, yc, sp["g_fox_out"], sp["g_chk_out"], name="mix_norm_fwd")
    z, (g_ff2,) = _mm_nn(yn, g_out, "rows", name="mm_out", out_dtype=F32, task=_ag_d2d_task([g_ff2]))
    x1, h2 = _post_pre_call(xs, z, sp["g_mix_post"], sp["g_mem_pre"], name="post_mix")
    memn = _rms_fwd_call(mems, sp["g_mem_kv"], name="rms_mem_kv")
    q2 = _mm_nn(h2, g_mq, "rows", name="mm_mq")
    k2 = _mm_nn(memn, g_mk, "rows", name="mm_mk")
    v2 = _mm_nn(memn, g_mv, "rows", name="mm_mv")
    o2 = _mem_fwd_call(q2, k2, v2, name="mem_fwd")
    y2 = _mm_nn(o2, g_mo, "rows", name="mm_mo", out_dtype=F32)
    x2, h3 = _post_pre_call(x1, y2, sp["g_mem_post"], sp["g_ff_pre"], name="post_mem")
    act, relu = _mm_nn(h3, g_ff1, "cols", name="mm_ff1", epi="relu2")
    y3 = _mm_nn(act, g_ff2, "rows", name="mm_ff2", out_dtype=F32, tm=1024)
    loss_blk, dx3, dy3, dg_ff_post = _final_call(x2, y3, sp["g_ff_post"], target, name="final")

    d_ff2 = _mm_tn(act, dy3, name="mm_dff2", tk=512, tn=1024).reshape(N_CHIP, D_FF // N_CHIP, D)
    du, (r1,) = _mm_nt(dy3, g_ff2, "rows", name="mm_du", mul2r=relu, task=_rs_pair_task([d_ff2]))
    p_ff2 = pair_add("w_ff2", d_ff2, r1)
    d_ff1 = _mm_tn(h3, du, name="mm_dff1", cols4=True)
    dh3, (r1,) = _mm_nt(du, g_ff1, "cols", name="mm_dh3", out_dtype=F32, tm=1024, task=_rs_pair_task([d_ff1]))
    p_ff1 = pair_add("w_ff1", d_ff1, r1)
    dx2, dy2, dg_ff_pre, dg_mem_post = _bwd_mid_call(dx3, x2, dh3, sp["g_ff_pre"], y2, sp["g_mem_post"], name="bwd_ff")
    d_mo = _mm_tn(o2, dy2, name="mm_dmo").reshape(N_CHIP, D // N_CHIP, D)
    do2 = _mm_nt(dy2, g_mo, "rows", name="mm_do2")
    dq2, dk2, dv2 = _mem_bwd_call(q2, k2, v2, do2, name="mem_bwd")
    d_mq = _mm_tn(h2, dq2, name="mm_dmq").reshape(N_CHIP, D // N_CHIP, D)
    dh2 = _mm_nt(dq2, g_mq, "rows", name="mm_dh2", out_dtype=F32)
    d_mk = _mm_tn(memn, dk2, name="mm_dmk").reshape(N_CHIP, D // N_CHIP, D)
    d_mv = _mm_tn(memn, dv2, name="mm_dmv").reshape(N_CHIP, D // N_CHIP, D)
    dmn_k = _mm_nt(dk2, g_mk, "rows", name="mm_dmemk", out_dtype=F32)
    dmn_v = _mm_nt(dv2, g_mv, "rows", name="mm_dmemv", out_dtype=F32)
    dg_mem_kv = _gain_grad_call(mems, sp["g_mem_kv"], dmn_k, dmn_v, name="gain_mem_kv")
    dx1, dz, dg_mem_pre, dg_mix_post = _bwd_mid_call(dx2, x1, dh2, sp["g_mem_pre"], z, sp["g_mix_post"], name="bwd_mem")
    d_out = _mm_tn(yn, dz, name="mm_dout").reshape(N_CHIP, D // N_CHIP, D)
    late = ["w_mo", "w_mq", "w_mk", "w_mv", "w_out"]
    d_late = [d_mo, d_mq, d_mk, d_mv, d_out]
    dyn, r1_late = _mm_nt(dz, g_out, "rows", name="mm_dyn", out_dtype=F32, task=_rs_pair_task(d_late))
    p_late = [pair_add(k, d, r1) for k, d, r1 in zip(late, d_late, r1_late)]
    dof, doc, delta, dg_fox, dg_chk = _mix_norm_bwd_call(dyn, yf, yc, sp["g_fox_out"], sp["g_chk_out"], name="mix_norm_bwd")
    (dqf, dkf, dvf, dcq, dck), r2_ff = _fox_bwd_call(proj, dof, lse, delta, c_rep, c_t, name="fox_bwd",
                                                      task=_rs_chip_task([p_ff2, p_ff1]))
    (dqc, dkc, dvc, dgrev), r2_late = _chk_bwd_call(proj, doc, bias, name="chk_bwd", task=_rs_chip_task(p_late))
    first = ["w_ff2", "w_ff1"] + late
    f_first = [_chip_sum_call(p, r, place_arr, name="rs_chip_sum_" + k)
               for k, p, r in zip(first, [p_ff2, p_ff1] + p_late, r2_ff + r2_late)]
    dc8 = dcq[:, :, 0:2, :].transpose(0, 2, 1, 3).reshape(8, S) + dck[:, ::HEAD].T
    dc_rows = jnp.concatenate([dc8, jnp.zeros((120, S), F32)], axis=0)
    dfl, db_fgt = _fox_gate_bwd_call(dc_rows, fl_raw, b_pad, name="fox_gate_bwd")
    dproj = jnp.concatenate([dqf, dkf, dvf, dqc, dkc, dvc, dfl], axis=1)
    d_all_t, g_first = _mm_tn(dproj, h1, name="mm_dwin", tk=640, tn=1024, task=_rs_gather_task(f_first))
    grads = dict(zip(first, g_first))
    d_in = _reorder_rows_call(d_all_t, False, name="d_in_rows").reshape(N_CHIP, IN_PAD, D)
    delta_w, new_m, new_v = {}, {}, {}

    def adamw_items(names):
        return [(w[k][0], grads[k], m[k][0], v[k][0]) for k in names]

    upd_late, (r1,) = _adamw_call(adamw_items(late), name="adamw_late", tm=64, task=_rs_pair_task([d_in]))
    p_in = pair_add("w_in", d_in, r1)
    dh1, (r2_in,) = _mm_nn(dproj, w_all_t, "plain", name="mm_dh1", out_dtype=F32, tm=1024,
                           task=_rs_chip_task([p_in]))
    f_in = _chip_sum_call(p_in, r2_in, place_arr, name="rs_chip_sum_w_in")
    upd_ff = _adamw_call(adamw_items(first[:2]), name="adamw_ff")
    for k, res in zip(late + first[:2], upd_late + upd_ff):
        grads[k], delta_w[k], new_m[k], new_v[k] = res
    grad_x, dg_mix_pre = _bwd_last_call(dx1, xs, dh1, sp["g_mix_pre"], name="bwd_mix")

    small_g = {"g_mix_pre": dg_mix_pre, "g_mix_post": dg_mix_post, "g_mem_kv": dg_mem_kv, "g_mem_pre": dg_mem_pre,
               "g_mem_post": dg_mem_post, "g_ff_pre": dg_ff_pre, "g_ff_post": dg_ff_post, "g_fox_out": dg_fox,
               "g_chk_out": dg_chk, "b_fgt": db_fgt,
               "rel_bias": _g_to_rel_table(dgrev[:, 0:2, :].reshape(8, ROLL_W))}
    small, loss_out, (g_w_in,) = _small_call(
        small_g, sp, {k: rows(m, k) for k in SMALL}, {k: rows(v, k) for k in SMALL}, loss_blk,
        _rs_gather_task([f_in]), name="small_allreduce_adamw")
    loss = loss_out[0, 0]
    res = _adamw_cols_call(w_in_t, g_w_in, m_in_t, v_in_t, name="adamw_w_in")
    grads["w_in"], delta_w["w_in"], new_m["w_in"], new_v["w_in"] = (a.T for a in res)
    for k in SMALL:
        vals = small[k]
        if k == "rel_bias":
            vals = tuple(a[None] for a in vals)
        grads[k], delta_w[k], new_m[k], new_v[k] = vals

    def out(d, k):
        return d[k][None] if k in BIG else d[k]

    return (loss, grad_x[None], *[out(grads, k) for k in WEIGHTS], *[out(delta_w, k) for k in WEIGHTS],
            *[out(new_m, k) for k in WEIGHTS], *[out(new_v, k) for k in WEIGHTS])
```

```python
import functools

import jax
import jax.numpy as jnp
from jax import lax
from jax.experimental import pallas as pl
from jax.experimental.pallas import tpu as pltpu

F32 = jnp.float32
BF16 = jnp.bfloat16

D = 1024
HEAD = 64
N_PAIR = 4
D_GRP = 512
CHUNK = 64
LEFT = 8
MAX_REL = 128
N_REL = 2 * MAX_REL + 1
N_MEM = 256
MEM_HEADS = 4
MEM_HD = 256
D_FF = 4096
D_IN = 3080
D_ALL = 3200
EPS = 1e-6
TQ = 256
WIN = (LEFT + TQ // CHUNK) * CHUNK
PADK = LEFT * CHUNK
ROLL_W = 1024
NEG = -1e30
N_CHIP = 4
VMEM_LIMIT = 48 * 1024 * 1024

ADAM_LR = 0.001
ADAM_B1 = 0.9
ADAM_B2 = 0.999
ADAM_EPS = 1e-08
ADAM_WD = 0.01
ADAM_STEP = 10

MESH = pl.DeviceIdType.MESH


def _cparams():
    return pltpu.CompilerParams(vmem_limit_bytes=VMEM_LIMIT)


ANY = pl.BlockSpec(memory_space=pl.ANY)


class _Task:
    def __init__(self, arrays, out_shapes, sems, issue, drain, aliases=None):
        self.arrays, self.out_shapes, self.sems = list(arrays), list(out_shapes), list(sems)
        self.issue, self.drain, self.aliases = issue, drain, dict(aliases or {})


def _merge_tasks(tasks):
    tasks = [t for t in tasks if t is not None]
    if len(tasks) == 1:
        return tasks[0]
    cuts, a, o, s = [], 0, 0, 0
    aliases = {}
    for t in tasks:
        cuts.append((a, o, s))
        aliases.update({a + i: o + j for i, j in t.aliases.items()})
        a, o, s = a + len(t.arrays), o + len(t.out_shapes), s + len(t.sems)

    def part(fn_name):
        def run(ins, outs, sems):
            for t, (a0, o0, s0) in zip(tasks, cuts):
                getattr(t, fn_name)(ins[a0:a0 + len(t.arrays)], outs[o0:o0 + len(t.out_shapes)],
                                    sems[s0:s0 + len(t.sems)])
        return run

    return _Task([x for t in tasks for x in t.arrays], [x for t in tasks for x in t.out_shapes],
                 [x for t in tasks for x in t.sems], part("issue"), part("drain"), aliases)


def _pallas(body, *, grid, in_specs, out_specs, out_shape, name, scratch_shapes=(), task=None, prefetch=0):
    def make(kernel, i_specs, o_specs, o_shape, scratch, aliases):
        if prefetch:
            spec = pltpu.PrefetchScalarGridSpec(num_scalar_prefetch=prefetch, grid=grid, in_specs=i_specs,
                                                out_specs=o_specs, scratch_shapes=scratch)
            return pl.pallas_call(kernel, grid_spec=spec, out_shape=o_shape, input_output_aliases=aliases,
                                  name=name, compiler_params=_cparams())
        return pl.pallas_call(kernel, grid=grid, in_specs=i_specs, out_specs=o_specs, out_shape=o_shape,
                              scratch_shapes=scratch, input_output_aliases=aliases, name=name,
                              compiler_params=_cparams())

    if task is None:
        return make(body, list(in_specs), out_specs, out_shape, list(scratch_shapes), {})
    single = not isinstance(out_shape, (tuple, list))
    o_shapes = [out_shape] if single else list(out_shape)
    o_specs = [out_specs] if single else list(out_specs)
    n_in, n_out, n_scr = len(in_specs), len(o_shapes), len(scratch_shapes)
    t_in, t_out = len(task.arrays), len(task.out_shapes)

    def carried(*refs):
        cut = [prefetch, n_in, t_in, n_out, t_out, n_scr]
        parts, p = [], 0
        for c in cut:
            parts.append(refs[p:p + c])
            p += c
        scalars, ins, tins, outs, touts, scr = parts
        tsems = refs[p:]
        ids = [pl.program_id(a) for a in range(len(grid))]
        first = functools.reduce(jnp.logical_and, [i == 0 for i in ids])
        last = functools.reduce(jnp.logical_and, [i == g - 1 for i, g in zip(ids, grid)])

        @pl.when(first)
        def _():
            task.issue(tins, touts, tsems)
        body(*scalars, *ins, *outs, *scr)

        @pl.when(last)
        def _():
            task.drain(tins, touts, tsems)

    call = make(carried, list(in_specs) + [ANY] * t_in, o_specs + [ANY] * t_out,
                o_shapes + list(task.out_shapes), list(scratch_shapes) + list(task.sems),
                {prefetch + n_in + i: n_out + j for i, j in task.aliases.items()})

    def run(*args):
        res = call(*args, *task.arrays)
        outs = res[:n_out]
        return (outs[0] if single else tuple(outs)), list(res[n_out:])

    return run


def _dot(a, b):
    return jnp.dot(a, b, preferred_element_type=F32)


def _dot_nt(a, b):
    return lax.dot_general(a, b, (((1,), (1,)), ((), ())), preferred_element_type=F32)


def _dot_tn(a, b):
    return lax.dot_general(a, b, (((0,), (0,)), ((), ())), preferred_element_type=F32)


def _split3(x):
    hi = x.astype(BF16)
    r1 = x - hi.astype(F32)
    mid = r1.astype(BF16)
    lo = (r1 - mid.astype(F32)).astype(BF16)
    return hi, mid, lo


def _dot3(x, m01):
    hi, mid, lo = _split3(x)
    return _dot(hi, m01) + _dot(mid, m01) + _dot(lo, m01)


def _dot3_l(m01, x):
    hi, mid, lo = _split3(x)
    return _dot(m01, hi) + _dot(m01, mid) + _dot(m01, lo)


def _mm_nn(a, b, kind, *, name, out_dtype=BF16, tm=2048, tn=512, epi=None, task=None):
    M, K = a.shape
    if kind == "plain":
        N = b.shape[1]
        b_spec = pl.BlockSpec((K, tn), lambda m, n: (0, n))
    elif kind == "rows":
        N = b.shape[2]
        b_spec = pl.BlockSpec((N_CHIP, K // N_CHIP, tn), lambda m, n: (0, 0, n))
    else:
        nq = b.shape[2]
        N = N_CHIP * nq
        per = nq // tn
        b_spec = pl.BlockSpec((None, K, tn), lambda m, n: (n // per, 0, n % per))
    tm = min(tm, M)
    kq = K // N_CHIP

    def body(a_ref, b_ref, *o_refs):
        if kind == "rows":
            acc = _dot(a_ref[:, 0:kq], b_ref[0])
            for j in range(1, N_CHIP):
                acc += _dot(a_ref[:, j * kq:(j + 1) * kq], b_ref[j])
        else:
            acc = _dot(a_ref[...], b_ref[...])
        if epi == "relu2":
            r = jnp.maximum(acc, 0.0)
            o_refs[0][...] = (r * r).astype(BF16)
            o_refs[1][...] = r.astype(BF16)
        else:
            o_refs[0][...] = acc.astype(out_dtype)

    o_spec = pl.BlockSpec((tm, tn), lambda m, n: (m, n))
    if epi == "relu2":
        out_shape = (jax.ShapeDtypeStruct((M, N), BF16), jax.ShapeDtypeStruct((M, N), BF16))
        out_specs = (o_spec, o_spec)
    else:
        out_shape = jax.ShapeDtypeStruct((M, N), out_dtype)
        out_specs = o_spec
    return _pallas(
        body, grid=(M // tm, N // tn),
        in_specs=[pl.BlockSpec((tm, K), lambda m, n: (m, 0)), b_spec],
        out_specs=out_specs, out_shape=out_shape, name=name, task=task,
    )(a, b)


def _mm_nt(a, b, kind, *, name, out_dtype=BF16, tm=2048, tn=512, mul2r=None, task=None, rows=None):
    M, K = a.shape
    if kind == "plain":
        first, N = rows if rows is not None else (0, b.shape[0])
        n0 = first // tn
        b_spec = pl.BlockSpec((tn, K), lambda m, n: (n0 + n, 0))
    elif kind == "rows":
        nq = b.shape[1]
        N = N_CHIP * nq
        tn = min(tn, nq)
        per = nq // tn
        b_spec = pl.BlockSpec((None, tn, K), lambda m, n: (n // per, n % per, 0))
    else:
        N = b.shape[1]
        b_spec = pl.BlockSpec((N_CHIP, tn, K // N_CHIP), lambda m, n: (0, n, 0))
    tm = min(tm, M)
    kq = K // N_CHIP

    def body(a_ref, b_ref, *rest):
        o_ref = rest[-1]
        if kind == "cols":
            acc = _dot_nt(a_ref[:, 0:kq], b_ref[0])
            for j in range(1, N_CHIP):
                acc += _dot_nt(a_ref[:, j * kq:(j + 1) * kq], b_ref[j])
        else:
            acc = _dot_nt(a_ref[...], b_ref[...])
        if mul2r is not None:
            acc = acc * (2.0 * rest[0][...].astype(F32))
        o_ref[...] = acc.astype(out_dtype)

    in_specs = [pl.BlockSpec((tm, K), lambda m, n: (m, 0)), b_spec]
    args = [a, b]
    if mul2r is not None:
        in_specs.append(pl.BlockSpec((tm, tn), lambda m, n: (m, n)))
        args.append(mul2r)
    return _pallas(
        body, grid=(M // tm, N // tn), in_specs=in_specs,
        out_specs=pl.BlockSpec((tm, tn), lambda m, n: (m, n)),
        out_shape=jax.ShapeDtypeStruct((M, N), out_dtype), name=name, task=task,
    )(*args)


def _mm_tn(a, b, *, name, out_dtype=BF16, tk=1024, tn=512, cols4=False, task=None):
    M, K1 = a.shape
    N = b.shape[1]
    tk = min(tk, K1)
    tn = min(tn, N)

    def body(a_ref, b_ref, o_ref):
        o_ref[...] = _dot_tn(a_ref[...], b_ref[...]).astype(out_dtype)

    if cols4:
        per = (N // N_CHIP) // tn
        out_shape = jax.ShapeDtypeStruct((N_CHIP, K1, N // N_CHIP), out_dtype)
        o_spec = pl.BlockSpec((None, tk, tn), lambda k, n: (n // per, k, n % per))
    else:
        out_shape = jax.ShapeDtypeStruct((K1, N), out_dtype)
        o_spec = pl.BlockSpec((tk, tn), lambda k, n: (k, n))
    return _pallas(
        body, grid=(K1 // tk, N // tn),
        in_specs=[pl.BlockSpec((M, tk), lambda k, n: (0, k)), pl.BlockSpec((M, tn), lambda k, n: (0, n))],
        out_specs=o_spec, out_shape=out_shape, name=name, task=task,
    )(a, b)


def _rms(x, g):
    r = lax.rsqrt(jnp.mean(x * x, axis=-1, keepdims=True) + EPS)
    return x * r * g


def _rms_bwd(x, g, dy):
    r = lax.rsqrt(jnp.mean(x * x, axis=-1, keepdims=True) + EPS)
    xh = x * r
    dg = jnp.sum(dy * xh, axis=0, keepdims=True)
    dxh = dy * g
    dx = r * (dxh - xh * jnp.mean(dxh * xh, axis=-1, keepdims=True))
    return dx, dg


def _row_spec(tm, n):
    return pl.BlockSpec((tm, n), lambda i: (i, 0))


def _vec_spec(n):
    return pl.BlockSpec((1, n), lambda i: (0, 0))


def _acc_spec(n):
    return pl.BlockSpec((8, n), lambda i: (0, 0))


def _acc_add(ref, row, i):
    @pl.when(i == 0)
    def _():
        ref[...] = jnp.zeros_like(ref)
    ref[0:1, :] += row


def _rms_fwd_call(x, g, *, name, tm=1024, task=None):
    M, n = x.shape
    tm = min(tm, M)

    def body(x_ref, g_ref, h_ref):
        h_ref[...] = _rms(x_ref[...], g_ref[...]).astype(BF16)

    return _pallas(
        body, grid=(M // tm,), in_specs=[_row_spec(tm, n), _vec_spec(n)], out_specs=_row_spec(tm, n),
        out_shape=jax.ShapeDtypeStruct((M, n), BF16), name=name, task=task,
    )(x, g)


def _post_pre_call(xres, z, g_post, g_pre, *, name, tm=1024):
    M, n = xres.shape

    def body(x_ref, z_ref, gp_ref, gn_ref, xo_ref, h_ref):
        xn = x_ref[...] + _rms(z_ref[...], gp_ref[...])
        xo_ref[...] = xn
        h_ref[...] = _rms(xn, gn_ref[...]).astype(BF16)

    return pl.pallas_call(
        body, grid=(M // tm,),
        in_specs=[_row_spec(tm, n), _row_spec(tm, n), _vec_spec(n), _vec_spec(n)],
        out_specs=(_row_spec(tm, n), _row_spec(tm, n)),
        out_shape=(jax.ShapeDtypeStruct((M, n), F32), jax.ShapeDtypeStruct((M, n), BF16)),
        name=name, compiler_params=_cparams(),
    )(xres, z, g_post, g_pre)


def _final_call(x2, y3, g_post, target, *, name, tm=512):
    M, n = x2.shape

    def body(x_ref, y_ref, g_ref, t_ref, loss_ref, dx_ref, dy_ref, dg_ref):
        i = pl.program_id(0)
        y = y_ref[...]
        g = g_ref[...]
        diff = x_ref[...] + _rms(y, g) - t_ref[...]
        part = 0.5 * jnp.sum(jnp.sum(diff * diff, axis=1, keepdims=True), axis=0, keepdims=True) / n

        @pl.when(i == 0)
        def _():
            loss_ref[...] = jnp.zeros_like(loss_ref)
        loss_ref[...] += jnp.broadcast_to(part, loss_ref.shape)
        dx = diff / n
        dx_ref[...] = dx
        dy, dg = _rms_bwd(y, g, dx)
        dy_ref[...] = dy.astype(BF16)
        _acc_add(dg_ref, dg, i)

    return pl.pallas_call(
        body, grid=(M // tm,),
        in_specs=[_row_spec(tm, n), _row_spec(tm, n), _vec_spec(n), _row_spec(tm, n)],
        out_specs=(pl.BlockSpec((8, 128), lambda i: (0, 0)), _row_spec(tm, n), _row_spec(tm, n), _acc_spec(n)),
        out_shape=(jax.ShapeDtypeStruct((8, 128), F32), jax.ShapeDtypeStruct((M, n), F32),
                   jax.ShapeDtypeStruct((M, n), BF16), jax.ShapeDtypeStruct((8, n), F32)),
        name=name, compiler_params=_cparams(),
    )(x2, y3, g_post, target)


def _bwd_mid_call(dx_in, x, dh, g_pre, y, g_post, *, name, tm=512):
    M, n = x.shape

    def body(dxi_ref, x_ref, dh_ref, gpre_ref, y_ref, gpost_ref, dx_ref, dy_ref, dgpre_ref, dgpost_ref):
        i = pl.program_id(0)
        d1, dg1 = _rms_bwd(x_ref[...], gpre_ref[...], dh_ref[...])
        dx = dxi_ref[...] + d1
        dx_ref[...] = dx
        dy, dg2 = _rms_bwd(y_ref[...], gpost_ref[...], dx)
        dy_ref[...] = dy.astype(BF16)
        _acc_add(dgpre_ref, dg1, i)
        _acc_add(dgpost_ref, dg2, i)

    return pl.pallas_call(
        body, grid=(M // tm,),
        in_specs=[_row_spec(tm, n), _row_spec(tm, n), _row_spec(tm, n), _vec_spec(n), _row_spec(tm, n), _vec_spec(n)],
        out_specs=(_row_spec(tm, n), _row_spec(tm, n), _acc_spec(n), _acc_spec(n)),
        out_shape=(jax.ShapeDtypeStruct((M, n), F32), jax.ShapeDtypeStruct((M, n), BF16),
                   jax.ShapeDtypeStruct((8, n), F32), jax.ShapeDtypeStruct((8, n), F32)),
        name=name, compiler_params=_cparams(),
    )(dx_in, x, dh, g_pre, y, g_post)


def _bwd_last_call(dx_in, x, dh, g_pre, *, name, tm=1024, task=None):
    M, n = x.shape

    def body(dxi_ref, x_ref, dh_ref, g_ref, dx_ref, dg_ref):
        i = pl.program_id(0)
        d1, dg1 = _rms_bwd(x_ref[...], g_ref[...], dh_ref[...])
        dx_ref[...] = dxi_ref[...] + d1
        _acc_add(dg_ref, dg1, i)

    return _pallas(
        body, grid=(M // tm,),
        in_specs=[_row_spec(tm, n), _row_spec(tm, n), _row_spec(tm, n), _vec_spec(n)],
        out_specs=(_row_spec(tm, n), _acc_spec(n)),
        out_shape=(jax.ShapeDtypeStruct((M, n), F32), jax.ShapeDtypeStruct((8, n), F32)),
        name=name, task=task,
    )(dx_in, x, dh, g_pre)


def _gain_grad_call(x, g, dy_a, dy_b, *, name):
    M, n = x.shape

    def body(x_ref, g_ref, a_ref, b_ref, dg_ref):
        _, dg = _rms_bwd(x_ref[...], g_ref[...], a_ref[...] + b_ref[...])
        dg_ref[...] = jnp.zeros_like(dg_ref)
        dg_ref[0:1, :] = dg

    return pl.pallas_call(
        body, grid=(1,),
        in_specs=[_row_spec(M, n), _vec_spec(n), _row_spec(M, n), _row_spec(M, n)],
        out_specs=_acc_spec(n), out_shape=jax.ShapeDtypeStruct((8, n), F32),
        name=name, compiler_params=_cparams(),
    )(x, g, dy_a, dy_b)


def _head_group_matrix():
    a = lax.broadcasted_iota(jnp.int32, (D_GRP, D_GRP), 0) // HEAD
    b = lax.broadcasted_iota(jnp.int32, (D_GRP, D_GRP), 1) // HEAD
    return jnp.where(a == b, 1.0, 0.0).astype(BF16)


def _mix_norm_fwd_call(yf, yc, gf, gc, *, name, tm=1024):
    M = yf.shape[0]

    def body(yf_ref, yc_ref, gf_ref, gc_ref, o_ref):
        o_ref[:, 0:D_GRP] = _rms(yf_ref[...], gf_ref[...]).astype(BF16)
        o_ref[:, D_GRP:D] = _rms(yc_ref[...], gc_ref[...]).astype(BF16)

    return pl.pallas_call(
        body, grid=(M // tm,),
        in_specs=[_row_spec(tm, D_GRP), _row_spec(tm, D_GRP), _vec_spec(D_GRP), _vec_spec(D_GRP)],
        out_specs=_row_spec(tm, D), out_shape=jax.ShapeDtypeStruct((M, D), BF16),
        name=name, compiler_params=_cparams(),
    )(yf, yc, gf, gc)


def _mix_norm_bwd_call(dyn, yf, yc, gf, gc, *, name, tm=2 * TQ):
    M = yf.shape[0]

    def body(dyn_ref, yf_ref, yc_ref, gf_ref, gc_ref, dof_ref, doc_ref, delta_ref, dgf_ref, dgc_ref):
        i = pl.program_id(0)
        yf_ = yf_ref[...]
        dof, dgf = _rms_bwd(yf_, gf_ref[...], dyn_ref[:, 0:D_GRP])
        doc, dgc = _rms_bwd(yc_ref[...], gc_ref[...], dyn_ref[:, D_GRP:D])
        dof_b = dof.astype(BF16)
        dof_ref[...] = dof_b
        doc_ref[...] = doc.astype(BF16)
        prod = dof_b.astype(F32) * yf_
        hi = prod.astype(BF16)
        lo = (prod - hi.astype(F32)).astype(BF16)
        grp = _head_group_matrix()
        delta = _dot(hi, grp) + _dot(lo, grp)
        for b in range(tm // TQ):
            delta_ref[b] = delta[b * TQ:(b + 1) * TQ, :].T
        _acc_add(dgf_ref, dgf, i)
        _acc_add(dgc_ref, dgc, i)

    return pl.pallas_call(
        body, grid=(M // tm,),
        in_specs=[_row_spec(tm, D), _row_spec(tm, D_GRP), _row_spec(tm, D_GRP), _vec_spec(D_GRP), _vec_spec(D_GRP)],
        out_specs=(_row_spec(tm, D_GRP), _row_spec(tm, D_GRP),
                   pl.BlockSpec((tm // TQ, D_GRP, TQ), lambda i: (i, 0, 0)), _acc_spec(D_GRP), _acc_spec(D_GRP)),
        out_shape=(jax.ShapeDtypeStruct((M, D_GRP), BF16), jax.ShapeDtypeStruct((M, D_GRP), BF16),
                   jax.ShapeDtypeStruct((M // TQ, D_GRP, TQ), F32), jax.ShapeDtypeStruct((8, D_GRP), F32),
                   jax.ShapeDtypeStruct((8, D_GRP), F32)),
        name=name, compiler_params=_cparams(),
    )(dyn, yf, yc, gf, gc)


def _tri(n, lower_incl):
    a = lax.broadcasted_iota(jnp.int32, (n, n), 0)
    b = lax.broadcasted_iota(jnp.int32, (n, n), 1)
    return jnp.where(a >= b, 1.0, 0.0).astype(BF16) if lower_incl else jnp.where(a <= b, 1.0, 0.0).astype(BF16)


def _fox_prep_call(fl_raw, b_pad, *, name):
    S = fl_raw.shape[0]
    nb = S // TQ

    def body(fl_ref, b_ref, crep_ref, ct_ref, carry_ref):
        i = pl.program_id(0)

        @pl.when(i == 0)
        def _():
            carry_ref[...] = jnp.zeros_like(carry_ref)
        logf = jax.nn.log_sigmoid(fl_ref[...] + b_ref[...])
        cb = _dot3_l(_tri(TQ, True), logf) + carry_ref[0:1, :]
        carry_ref[0:1, :] = cb[TQ - 1:TQ, :]
        a = lax.broadcasted_iota(jnp.int32, (128, D_GRP), 0)
        b = lax.broadcasted_iota(jnp.int32, (128, D_GRP), 1) // HEAD
        expand = jnp.where(a == b, 1.0, 0.0).astype(BF16)
        crep = _dot3(cb, expand)
        crep_ref[...] = crep
        ct_ref[...] = crep.T

    return pl.pallas_call(
        body, grid=(nb,),
        in_specs=[_row_spec(TQ, 128), _vec_spec(128)],
        out_specs=(_row_spec(TQ, D_GRP), pl.BlockSpec((None, D_GRP, TQ), lambda i: (i, 0, 0))),
        out_shape=(jax.ShapeDtypeStruct((S, D_GRP), F32), jax.ShapeDtypeStruct((nb, D_GRP, TQ), F32)),
        scratch_shapes=[pltpu.VMEM((8, 128), F32)],
        name=name, compiler_params=_cparams(),
    )(fl_raw, b_pad)


def _lane_masks():
    lane = lax.broadcasted_iota(jnp.int32, (1, 128), 1)
    return lane < HEAD, lane >= HEAD


def _fox_fwd_call(proj, c_rep, c_t, *, name, task=None):
    S = proj.shape[0]
    nq = S // TQ
    scale = HEAD ** -0.5

    def body(q_ref, k_ref, v_ref, c_ref, ct_ref, o_ref, lse_ref):
        i = pl.program_id(1)
        m_lo, m_hi = _lane_masks()
        masks = (m_lo, m_hi)
        q = q_ref[...] * scale
        qm = [jnp.where(mk, q, jnp.zeros_like(q)) for mk in masks]
        cq = c_ref[...]
        cqh = [cq[:, 0:1], cq[:, HEAD:HEAD + 1]]
        row = lax.broadcasted_iota(jnp.int32, (TQ, TQ), 0)
        col = lax.broadcasted_iota(jnp.int32, (TQ, TQ), 1)

        def scores(j):
            start = pl.multiple_of(j * TQ, TQ)
            k = k_ref[pl.ds(start, TQ), :]
            ct = ct_ref[j]
            return tuple(_dot_nt(qm[h], k) + (cqh[h] - ct[HEAD * h:HEAD * h + 1, :]) for h in range(2))

        def update(j, ss, state, masked):
            ms, ls, acc = state
            start = pl.multiple_of(j * TQ, TQ)
            v = v_ref[pl.ds(start, TQ), :]
            new_m, new_l, pv, alpha_l = [], [], [], []
            for h in range(2):
                s = ss[h]
                if masked:
                    s = jnp.where(row >= col, s, NEG)
                mn = jnp.maximum(ms[h], jnp.max(s, axis=1, keepdims=True))
                alpha = jnp.exp(ms[h] - mn)
                p = jnp.exp(s - mn)
                new_l.append(alpha * ls[h] + jnp.sum(p, axis=1, keepdims=True))
                new_m.append(mn)
                alpha_l.append(alpha)
                pv.append(_dot(p.astype(BF16), jnp.where(masks[h], v, jnp.zeros_like(v))))
            alpha_lane = jnp.where(m_lo, alpha_l[0], alpha_l[1])
            acc = acc * alpha_lane + pv[0] + pv[1]
            return (tuple(new_m), tuple(new_l), acc)

        def step(j, carry):
            ss, state = carry
            return (scores(j + 1), update(j, ss, state, False))

        init = ((jnp.full((TQ, 1), NEG, F32),) * 2, (jnp.zeros((TQ, 1), F32),) * 2, jnp.zeros((TQ, 128), F32))
        ss, state = lax.fori_loop(0, i, step, (scores(0), init))
        ms, ls, acc = update(i, ss, state, True)
        l_lane = jnp.where(m_lo, ls[0], ls[1])
        o_ref[...] = acc / l_lane
        lse_ref[...] = jnp.where(m_lo, ms[0] + jnp.log(ls[0]), ms[1] + jnp.log(ls[1])).T

    return _pallas(
        body, grid=(N_PAIR, nq),
        in_specs=[pl.BlockSpec((TQ, 128), lambda p, i: (i, p)),
                  pl.BlockSpec((S, 128), lambda p, i: (0, N_PAIR + p)),
                  pl.BlockSpec((S, 128), lambda p, i: (0, 2 * N_PAIR + p)),
                  pl.BlockSpec((TQ, 128), lambda p, i: (i, p)),
                  pl.BlockSpec((nq, 128, TQ), lambda p, i: (0, p, 0))],
        out_specs=(pl.BlockSpec((TQ, 128), lambda p, i: (i, p)), pl.BlockSpec((None, 128, TQ), lambda p, i: (i, p, 0))),
        out_shape=(jax.ShapeDtypeStruct((S, D_GRP), F32), jax.ShapeDtypeStruct((nq, D_GRP, TQ), F32)),
        name=name, task=task,
    )(proj, proj, proj, c_rep, c_t)


def _fox_bwd_call(proj, do, lse_t, delta_t, c_rep, c_t, *, name, task=None):
    S = proj.shape[0]
    nq = S // TQ
    scale = HEAD ** -0.5

    def body(q_ref, k_ref, v_ref, do_ref, lse_ref, dl_ref, ck_ref, ct_ref,
             dq_ref, dk_ref, dv_ref, dcq_ref, dck_ref, dqa_ref):
        j = pl.program_id(1)
        m_lo, m_hi = _lane_masks()
        masks = (m_lo, m_hi)

        @pl.when(j == 0)
        def _():
            dqa_ref[...] = jnp.zeros_like(dqa_ref)
            dcq_ref[...] = jnp.zeros_like(dcq_ref)
        k = k_ref[...]
        v = v_ref[...]
        km = [jnp.where(mk, k, jnp.zeros_like(k)) for mk in masks]
        ck = ck_ref[...]
        krow = lax.broadcasted_iota(jnp.int32, (TQ, TQ), 0)
        qcol = lax.broadcasted_iota(jnp.int32, (TQ, TQ), 1)

        def probs(i):
            start = pl.multiple_of(i * TQ, TQ)
            q = q_ref[pl.ds(start, TQ), :]
            do = do_ref[pl.ds(start, TQ), :]
            lse = lse_ref[i]
            cq = ct_ref[i]
            out = []
            for h in range(2):
                lo = HEAD * h
                qm = jnp.where(masks[h], q * scale, jnp.zeros_like(q))
                dom = jnp.where(masks[h], do, jnp.zeros_like(do))
                st = _dot_nt(k, qm) + (cq[lo:lo + 1, :] - ck[:, lo:lo + 1])
                out.append((jnp.exp(st - lse[lo:lo + 1, :]), _dot_nt(v, dom)))
            return tuple(out)

        def update(i, pd, carry, masked):
            dk, dv, dck = carry
            start = pl.multiple_of(i * TQ, TQ)
            q = q_ref[pl.ds(start, TQ), :]
            do = do_ref[pl.ds(start, TQ), :]
            dl = dl_ref[i]
            dq = jnp.zeros((TQ, 128), F32)
            new_dck = []
            for h in range(2):
                lo = HEAD * h
                qm = jnp.where(masks[h], q, jnp.zeros_like(q))
                dom = jnp.where(masks[h], do, jnp.zeros_like(do))
                pt, dpt = pd[h]
                if masked:
                    pt = jnp.where(qcol >= krow, pt, 0.0)
                dst = pt * (dpt - dl[lo:lo + 1, :])
                dcq_ref[i, h:h + 1, :] += jnp.sum(dst, axis=0, keepdims=True)
                new_dck.append(dck[h] + jnp.sum(dst, axis=1, keepdims=True))
                dsb = (dst * scale).astype(BF16)
                dv = dv + _dot(pt.astype(BF16), dom)
                dk = dk + _dot(dsb, qm)
                dq = dq + _dot_tn(dsb, km[h])
            dqa_ref[pl.ds(start, TQ), :] += dq
            return (dk, dv, tuple(new_dck))

        def step(i, carry):
            pd, sums = carry
            return (probs(jnp.minimum(i + 1, nq - 1)), update(i, pd, sums, False))

        init = (jnp.zeros((TQ, 128), F32), jnp.zeros((TQ, 128), F32), (jnp.zeros((TQ, 1), F32),) * 2)
        first = probs(j)
        second = probs(jnp.minimum(j + 1, nq - 1))
        _, (dk, dv, dck) = lax.fori_loop(j + 1, nq, step, (second, update(j, first, init, True)))
        dk_ref[...] = dk.astype(BF16)
        dv_ref[...] = dv.astype(BF16)
        dck_ref[...] = -jnp.where(m_lo, dck[0], dck[1])

        @pl.when(j == nq - 1)
        def _():
            dq_ref[...] = dqa_ref[...].astype(BF16)

    res = lambda p, j: (0, p)
    stat = pl.BlockSpec((nq, 128, TQ), lambda p, j: (0, p, 0))
    blk = pl.BlockSpec((TQ, 128), lambda p, j: (j, p))
    return _pallas(
        body, grid=(N_PAIR, nq), task=task,
        in_specs=[pl.BlockSpec((S, 128), res),
                  pl.BlockSpec((TQ, 128), lambda p, j: (j, N_PAIR + p)),
                  pl.BlockSpec((TQ, 128), lambda p, j: (j, 2 * N_PAIR + p)),
                  pl.BlockSpec((S, 128), res), stat, stat, blk, stat],
        out_specs=(pl.BlockSpec((S, 128), res), blk, blk,
                   pl.BlockSpec((None, nq, 8, TQ), lambda p, j: (p, 0, 0, 0)), blk),
        out_shape=(jax.ShapeDtypeStruct((S, D_GRP), BF16), jax.ShapeDtypeStruct((S, D_GRP), BF16),
                   jax.ShapeDtypeStruct((S, D_GRP), BF16), jax.ShapeDtypeStruct((N_PAIR, nq, 8, TQ), F32),
                   jax.ShapeDtypeStruct((S, D_GRP), F32)),
        scratch_shapes=[pltpu.VMEM((S, 128), F32)],
        name=name,
    )(proj, proj, proj, do, lse_t, delta_t, c_rep, c_t)


def _fox_gate_bwd_call(dc_rows, dck, fl_raw, b_pad, *, name):
    S = fl_raw.shape[0]
    nb = S // TQ

    def body(dc_ref, dck_ref, fl_ref, b_ref, dfl_ref, db_ref, carry_ref):
        i = pl.program_id(0)

        @pl.when(i == 0)
        def _():
            carry_ref[...] = jnp.zeros_like(carry_ref)
        lane = lax.broadcasted_iota(jnp.int32, (D_GRP, 128), 0)
        head = lax.broadcasted_iota(jnp.int32, (D_GRP, 128), 1)
        pick = ((lane == head * HEAD) & (head < D_GRP // HEAD)).astype(BF16)
        dc = dc_ref[...] + _dot3(dck_ref[...], pick).T
        rc = _dot3(dc, _tri(TQ, True)) + carry_ref[:, 0:1]
        carry_ref[...] = jnp.broadcast_to(rc[:, 0:1], carry_ref.shape)
        fl = fl_ref[...] + b_ref[...]
        dfl = rc.T * jax.nn.sigmoid(-fl)
        dfl_ref[...] = dfl.astype(BF16)
        _acc_add(db_ref, jnp.sum(dfl, axis=0, keepdims=True), i)

    rev = lambda i: (nb - 1 - i, 0)
    return pl.pallas_call(
        body, grid=(nb,),
        in_specs=[pl.BlockSpec((128, TQ), lambda i: (0, nb - 1 - i)), pl.BlockSpec((TQ, D_GRP), rev),
                  pl.BlockSpec((TQ, 128), rev), _vec_spec(128)],
        out_specs=(pl.BlockSpec((TQ, 128), rev), _acc_spec(128)),
        out_shape=(jax.ShapeDtypeStruct((S, 128), BF16), jax.ShapeDtypeStruct((8, 128), F32)),
        scratch_shapes=[pltpu.VMEM((128, 128), F32)],
        name=name, compiler_params=_cparams(),
    )(dc_rows, dck, fl_raw, b_pad)


def _chk_bias_call(g_rev, *, name):
    def body(g_ref, o_ref):
        x = jnp.broadcast_to(g_ref[...], (TQ, ROLL_W))
        rolled = pltpu.roll(x, ROLL_W - (TQ - 1), 1, stride=1, stride_axis=0)
        qc = lax.broadcasted_iota(jnp.int32, (TQ, WIN), 0) // CHUNK
        kc = lax.broadcasted_iota(jnp.int32, (TQ, WIN), 1) // CHUNK
        band = (kc >= qc) & (kc <= qc + LEFT)
        o_ref[...] = jnp.where(band, rolled[:, 0:WIN], NEG)

    return pl.pallas_call(
        body, grid=(8,),
        in_specs=[pl.BlockSpec((None, 1, ROLL_W), lambda h: (h, 0, 0))],
        out_specs=pl.BlockSpec((None, TQ, WIN), lambda h: (h, 0, 0)),
        out_shape=jax.ShapeDtypeStruct((8, TQ, WIN), F32), name=name, compiler_params=_cparams(),
    )(g_rev.reshape(8, 1, ROLL_W))


def _chk_scores(i, qm, kwin, bias, scale):
    s = _dot_nt(qm * scale, kwin) + bias
    kc = lax.broadcasted_iota(jnp.int32, (TQ, WIN), 1) // CHUNK
    return jnp.where(kc + i * (TQ // CHUNK) >= LEFT, s, NEG)


def _chk_fwd_call(proj, bias, *, name, task=None):
    S = proj.shape[0]
    nq = S // TQ
    scale = HEAD ** -0.5

    def body(q_ref, k_ref, v_ref, b_ref, o_ref, kp_ref, vp_ref):
        i = pl.program_id(1)

        @pl.when(i == 0)
        def _():
            kp_ref[0:PADK, :] = jnp.zeros((PADK, 128), BF16)
            vp_ref[0:PADK, :] = jnp.zeros((PADK, 128), BF16)
            kp_ref[PADK:PADK + S, :] = k_ref[...]
            vp_ref[PADK:PADK + S, :] = v_ref[...]
        masks = _lane_masks()
        q = q_ref[...]
        start = pl.multiple_of(i * TQ, TQ)
        kwin = kp_ref[pl.ds(start, WIN), :]
        vwin = vp_ref[pl.ds(start, WIN), :]
        ss = [_chk_scores(i, jnp.where(masks[h], q, jnp.zeros_like(q)), kwin, b_ref[h], scale) for h in range(2)]
        ps = []
        for s in ss:
            p = jnp.exp(s - jnp.max(s, axis=1, keepdims=True))
            ps.append((p / jnp.sum(p, axis=1, keepdims=True)).astype(BF16))
        o_ref[...] = (_dot(ps[0], jnp.where(masks[0], vwin, jnp.zeros_like(vwin)))
                      + _dot(ps[1], jnp.where(masks[1], vwin, jnp.zeros_like(vwin))))

    c0 = 3 * N_PAIR
    return _pallas(
        body, grid=(N_PAIR, nq), task=task,
        in_specs=[pl.BlockSpec((TQ, 128), lambda p, i: (i, c0 + p)),
                  pl.BlockSpec((S, 128), lambda p, i: (0, c0 + N_PAIR + p)),
                  pl.BlockSpec((S, 128), lambda p, i: (0, c0 + 2 * N_PAIR + p)),
                  pl.BlockSpec((2, TQ, WIN), lambda p, i: (p, 0, 0))],
        out_specs=pl.BlockSpec((TQ, 128), lambda p, i: (i, p)),
        out_shape=jax.ShapeDtypeStruct((S, D_GRP), F32),
        scratch_shapes=[pltpu.VMEM((S + PADK, 128), BF16), pltpu.VMEM((S + PADK, 128), BF16)],
        name=name,
    )(proj, proj, proj, bias)


def _chk_bwd_call(proj, do, bias, *, name, task=None):
    S = proj.shape[0]
    nq = S // TQ
    scale = HEAD ** -0.5

    def body(q_ref, k_ref, v_ref, do_ref, b_ref, dq_ref, dk_ref, dv_ref, dg_ref, kp_ref, vp_ref, dkp_ref, dvp_ref, db_ref):
        i = pl.program_id(1)

        @pl.when(i == 0)
        def _():
            kp_ref[0:PADK, :] = jnp.zeros((PADK, 128), BF16)
            vp_ref[0:PADK, :] = jnp.zeros((PADK, 128), BF16)
            kp_ref[PADK:PADK + S, :] = k_ref[...]
            vp_ref[PADK:PADK + S, :] = v_ref[...]
            dkp_ref[...] = jnp.zeros_like(dkp_ref)
            dvp_ref[...] = jnp.zeros_like(dvp_ref)
            db_ref[...] = jnp.zeros_like(db_ref)
        masks = _lane_masks()
        q = q_ref[...]
        dout = do_ref[...]
        start = pl.multiple_of(i * TQ, TQ)
        kwin = kp_ref[pl.ds(start, WIN), :]
        vwin = vp_ref[pl.ds(start, WIN), :]
        qm = [jnp.where(mk, q, jnp.zeros_like(q)) for mk in masks]
        dom = [jnp.where(mk, dout, jnp.zeros_like(dout)) for mk in masks]
        ss = [_chk_scores(i, qm[h], kwin, b_ref[h], scale) for h in range(2)]
        dps = [_dot_nt(dom[h], vwin) for h in range(2)]
        pbs, dsbs = [], []
        for h in range(2):
            p = jnp.exp(ss[h] - jnp.max(ss[h], axis=1, keepdims=True))
            p = p / jnp.sum(p, axis=1, keepdims=True)
            ds = p * (dps[h] - jnp.sum(p * dps[h], axis=1, keepdims=True))
            db_ref[h] += ds
            pbs.append(p.astype(BF16))
            dsbs.append((ds * scale).astype(BF16))
        dq_ref[...] = (_dot(dsbs[0], jnp.where(masks[0], kwin, jnp.zeros_like(kwin)))
                       + _dot(dsbs[1], jnp.where(masks[1], kwin, jnp.zeros_like(kwin)))).astype(BF16)
        dkp_ref[pl.ds(start, WIN), :] += _dot_tn(dsbs[0], qm[0]) + _dot_tn(dsbs[1], qm[1])
        dvp_ref[pl.ds(start, WIN), :] += _dot_tn(pbs[0], dom[0]) + _dot_tn(pbs[1], dom[1])

        @pl.when(i == nq - 1)
        def _():
            dk_ref[...] = dkp_ref[PADK:PADK + S, :].astype(BF16)
            dv_ref[...] = dvp_ref[PADK:PADK + S, :].astype(BF16)
            a = lax.broadcasted_iota(jnp.int32, (TQ, TQ), 0)
            b = lax.broadcasted_iota(jnp.int32, (TQ, TQ), 1)
            flip = jnp.where(a + b == TQ - 1, 1.0, 0.0).astype(BF16)
            e = lax.broadcasted_iota(jnp.int32, (1, ROLL_W), 1)
            dg_ref[...] = jnp.zeros_like(dg_ref)
            for h in range(2):
                rev = _dot3_l(flip, db_ref[h])
                wide = jnp.concatenate([rev, jnp.zeros((TQ, ROLL_W - WIN), F32)], axis=1)
                diag = pltpu.roll(wide, 0, 1, stride=1, stride_axis=0)
                dg = jnp.sum(diag, axis=0, keepdims=True)
                lo = jnp.sum(jnp.where(e <= 639, dg, 0.0), axis=1, keepdims=True)
                hi = jnp.sum(jnp.where(e >= 895, dg, 0.0), axis=1, keepdims=True)
                dg_ref[h:h + 1, :] = jnp.where(e == 639, lo, jnp.where(e == 895, hi, dg))

    c0 = 3 * N_PAIR
    res = lambda p, i: (0, p)
    return _pallas(
        body, grid=(N_PAIR, nq), task=task,
        in_specs=[pl.BlockSpec((TQ, 128), lambda p, i: (i, c0 + p)),
                  pl.BlockSpec((S, 128), lambda p, i: (0, c0 + N_PAIR + p)),
                  pl.BlockSpec((S, 128), lambda p, i: (0, c0 + 2 * N_PAIR + p)),
                  pl.BlockSpec((TQ, 128), lambda p, i: (i, p)),
                  pl.BlockSpec((2, TQ, WIN), lambda p, i: (p, 0, 0))],
        out_specs=(pl.BlockSpec((TQ, 128), lambda p, i: (i, p)), pl.BlockSpec((S, 128), res),
                   pl.BlockSpec((S, 128), res), pl.BlockSpec((None, 8, ROLL_W), lambda p, i: (p, 0, 0))),
        out_shape=(jax.ShapeDtypeStruct((S, D_GRP), BF16), jax.ShapeDtypeStruct((S, D_GRP), BF16),
                   jax.ShapeDtypeStruct((S, D_GRP), BF16), jax.ShapeDtypeStruct((N_PAIR, 8, ROLL_W), F32)),
        scratch_shapes=[pltpu.VMEM((S + PADK, 128), BF16), pltpu.VMEM((S + PADK, 128), BF16),
                        pltpu.VMEM((S + PADK, 128), F32), pltpu.VMEM((S + PADK, 128), F32),
                        pltpu.VMEM((2, TQ, WIN), F32)],
        name=name,
    )(proj, proj, proj, do, bias)


def _mem_fwd_call(q, k, v, *, name, tq=2048):
    S = q.shape[0]
    scale = MEM_HD ** -0.5

    def body(q_ref, k_ref, v_ref, o_ref):
        s = _dot_nt(q_ref[...] * scale, k_ref[...])
        p = jnp.exp(s - jnp.max(s, axis=1, keepdims=True))
        p = p / jnp.sum(p, axis=1, keepdims=True)
        o_ref[...] = _dot(p.astype(BF16), v_ref[...]).astype(BF16)

    return pl.pallas_call(
        body, grid=(MEM_HEADS, S // tq),
        in_specs=[pl.BlockSpec((tq, MEM_HD), lambda h, i: (i, h)),
                  pl.BlockSpec((N_MEM, MEM_HD), lambda h, i: (0, h)),
                  pl.BlockSpec((N_MEM, MEM_HD), lambda h, i: (0, h))],
        out_specs=pl.BlockSpec((tq, MEM_HD), lambda h, i: (i, h)),
        out_shape=jax.ShapeDtypeStruct((S, D), BF16), name=name, compiler_params=_cparams(),
    )(q, k, v)


def _mem_bwd_call(q, k, v, do, *, name, tq=2048):
    S = q.shape[0]
    n = S // tq
    scale = MEM_HD ** -0.5

    def body(q_ref, k_ref, v_ref, do_ref, dq_ref, dk_ref, dv_ref, dka_ref, dva_ref):
        i = pl.program_id(1)

        @pl.when(i == 0)
        def _():
            dka_ref[...] = jnp.zeros_like(dka_ref)
            dva_ref[...] = jnp.zeros_like(dva_ref)
        qb = q_ref[...]
        kb = k_ref[...]
        dob = do_ref[...]
        s = _dot_nt(qb * scale, kb)
        p = jnp.exp(s - jnp.max(s, axis=1, keepdims=True))
        p = p / jnp.sum(p, axis=1, keepdims=True)
        dp = _dot_nt(dob, v_ref[...])
        ds = p * (dp - jnp.sum(p * dp, axis=1, keepdims=True))
        dsb = (ds * scale).astype(BF16)
        dq_ref[...] = _dot(dsb, kb).astype(BF16)
        dka_ref[...] += _dot_tn(dsb, qb)
        dva_ref[...] += _dot_tn(p.astype(BF16), dob)

        @pl.when(i == n - 1)
        def _():
            dk_ref[...] = dka_ref[...].astype(BF16)
            dv_ref[...] = dva_ref[...].astype(BF16)

    kv = pl.BlockSpec((N_MEM, MEM_HD), lambda h, i: (0, h))
    qs = pl.BlockSpec((tq, MEM_HD), lambda h, i: (i, h))
    return pl.pallas_call(
        body, grid=(MEM_HEADS, n), in_specs=[qs, kv, kv, qs], out_specs=(qs, kv, kv),
        out_shape=(jax.ShapeDtypeStruct((S, D), BF16), jax.ShapeDtypeStruct((N_MEM, D), BF16),
                   jax.ShapeDtypeStruct((N_MEM, D), BF16)),
        scratch_shapes=[pltpu.VMEM((N_MEM, MEM_HD), F32), pltpu.VMEM((N_MEM, MEM_HD), F32)],
        name=name, compiler_params=_cparams(),
    )(q, k, v, do)


def _rel_table_to_g(rel):
    return jnp.concatenate([
        jnp.broadcast_to(rel[:, N_REL - 1:N_REL], (8, 640)),
        rel[:, 1:N_REL - 1][:, ::-1],
        jnp.broadcast_to(rel[:, 0:1], (8, 129)),
    ], axis=1)


def _g_to_rel_table(dg):
    return dg[:, 639:896][:, ::-1]


def _place():
    x, y, c = lax.axis_index("x"), lax.axis_index("y"), lax.axis_index("c")
    others = [(1 - x, y), (x, 1 - y), (1 - x, 1 - y)]
    return x, y, c, others


def _half(c, rows):
    hr = rows // 2
    return pl.ds(pl.multiple_of(c * hr, 16), hr)


def _dma_sems(*shape):
    return pltpu.SemaphoreType.DMA(shape)


def _cast_slabs_call(ws, chip_arr, *, name, tm=256, task=None):
    n = len(ws)
    cols = ws[0].shape[1]
    tiles = [w.shape[0] // tm for w in ws]
    steps = max(tiles)

    def body(chip_ref, *refs):
        i = pl.program_id(0)
        for k in range(n):
            def cast(k=k):
                refs[n + k][...] = refs[k][...].astype(BF16)
            if tiles[k] == steps:
                cast()
            else:
                pl.when(i < tiles[k])(cast)

    in_specs = [pl.BlockSpec((tm, cols), lambda i, chip, t=t: (jnp.minimum(i, t - 1), 0)) for t in tiles]
    out_specs = [pl.BlockSpec((None, tm, cols), lambda i, chip, t=t: (chip[0], jnp.minimum(i, t - 1), 0)) for t in tiles]
    out_shape = [jax.ShapeDtypeStruct((N_CHIP,) + w.shape, BF16) for w in ws]
    return _pallas(body, grid=(steps,), in_specs=in_specs, out_specs=out_specs, out_shape=out_shape, name=name,
                   task=task, prefetch=1)(chip_arr, *ws)


def _cast_slab_call(w, chip_arr, *, name, tm=256, pad_rows=0):
    rows, cols = w.shape
    if pad_rows:
        tm = rows
    tm = min(tm, rows)

    def body(chip_ref, w_ref, o_ref):
        o_ref[0:tm, :] = w_ref[...].astype(BF16)
        if pad_rows:
            o_ref[tm:tm + pad_rows, :] = jnp.zeros((pad_rows, cols), BF16)

    return pl.pallas_call(
        body,
        grid_spec=pltpu.PrefetchScalarGridSpec(
            num_scalar_prefetch=1, grid=(rows // tm,),
            in_specs=[pl.BlockSpec((tm, cols), lambda i, chip: (i, 0))],
            out_specs=pl.BlockSpec((None, tm + pad_rows, cols), lambda i, chip: (chip[0], i, 0))),
        out_shape=jax.ShapeDtypeStruct((N_CHIP, rows + pad_rows, cols), BF16), name=name,
        compiler_params=_cparams(),
    )(chip_arr, w)


def _ag_ici_task(gathered):
    n = len(gathered)

    def copies(ins, outs, sems):
        send_sems, recv_sems = sems
        x, y, c, others = _place()
        me = 2 * x + y
        for k in range(n):
            mine = _half(c, gathered[k].shape[1])
            for t, (ox, oy) in enumerate(others):
                yield [pltpu.make_async_remote_copy(
                    src_ref=ins[k].at[me, mine], dst_ref=outs[k].at[slab, mine],
                    send_sem=send_sems.at[k, t], recv_sem=recv_sems.at[k, t],
                    device_id=(ox, oy, c), device_id_type=MESH) for slab in (me, 2 * ox + oy)]

    def issue(ins, outs, sems):
        for outgoing, _ in copies(ins, outs, sems):
            outgoing.start()

    def drain(ins, outs, sems):
        for outgoing, incoming in copies(ins, outs, sems):
            incoming.wait_recv()
            outgoing.wait_send()

    return _Task(gathered, [jax.ShapeDtypeStruct(g.shape, g.dtype) for g in gathered],
                 [_dma_sems(n, 3), _dma_sems(n, 3)], issue, drain, aliases={k: k for k in range(n)})


def _ag_d2d_task(gathered):
    n = len(gathered)

    def copies(ins, outs, sems):
        send_sems, recv_sems = sems
        x, y, c, others = _place()
        for k in range(n):
            rows = gathered[k].shape[1]
            mine, theirs = _half(c, rows), _half(1 - c, rows)
            for t, (ox, oy) in enumerate(others):
                slab = 2 * ox + oy
                pair = [pltpu.make_async_remote_copy(
                    src_ref=ins[k].at[slab, half], dst_ref=outs[k].at[slab, half],
                    send_sem=send_sems.at[k, t], recv_sem=recv_sems.at[k, t],
                    device_id=(x, y, 1 - c), device_id_type=MESH) for half in (mine, theirs)]
                yield pair

    def issue(ins, outs, sems):
        for outgoing, _ in copies(ins, outs, sems):
            outgoing.start()

    def drain(ins, outs, sems):
        for outgoing, incoming in copies(ins, outs, sems):
            incoming.wait_recv()
            outgoing.wait_send()

    return _Task(gathered, [jax.ShapeDtypeStruct(g.shape, g.dtype) for g in gathered],
                 [_dma_sems(n, 3), _dma_sems(n, 3)], issue, drain, aliases={k: k for k in range(n)})


def _rs_pair_task(ds):
    n = len(ds)

    def copies(ins, outs, sems):
        send_sems, recv_sems = sems
        x, y, c, _ = _place()
        for k in range(n):
            yield pltpu.make_async_remote_copy(
                src_ref=ins[k].at[:, _half(1 - c, ds[k].shape[1])], dst_ref=outs[k],
                send_sem=send_sems.at[k], recv_sem=recv_sems.at[k],
                device_id=(x, y, 1 - c), device_id_type=MESH)

    def issue(ins, outs, sems):
        for cp in copies(ins, outs, sems):
            cp.start()

    def drain(ins, outs, sems):
        for cp in copies(ins, outs, sems):
            cp.wait()

    return _Task(ds, [jax.ShapeDtypeStruct((N_CHIP, d.shape[1] // 2, d.shape[2]), d.dtype) for d in ds],
                 [_dma_sems(n), _dma_sems(n)], issue, drain)


def _pair_add_call(d, r1, c_arr, *, name, tm=512):
    _, rows, cols = d.shape
    hr = rows // 2
    tm = tm if hr % tm == 0 else hr
    nb = hr // tm

    def body(c_ref, d_ref, r_ref, o_ref):
        o_ref[...] = (d_ref[...].astype(F32) + r_ref[...].astype(F32)).astype(BF16)

    return pl.pallas_call(
        body,
        grid_spec=pltpu.PrefetchScalarGridSpec(
            num_scalar_prefetch=1, grid=(N_CHIP, nb),
            in_specs=[pl.BlockSpec((None, tm, cols), lambda j, i, c: (j, c[0] * nb + i, 0)),
                      pl.BlockSpec((None, tm, cols), lambda j, i, c: (j, i, 0))],
            out_specs=pl.BlockSpec((None, tm, cols), lambda j, i, c: (j, i, 0))),
        out_shape=jax.ShapeDtypeStruct((N_CHIP, hr, cols), BF16), name=name, compiler_params=_cparams(),
    )(c_arr, d, r1)


def _rs_chip_task(ps):
    n = len(ps)

    def copies(ins, outs, sems):
        send_sems, recv_sems = sems
        x, y, c, others = _place()
        for k in range(n):
            for t, (ox, oy) in enumerate(others):
                yield pltpu.make_async_remote_copy(
                    src_ref=ins[k].at[2 * ox + oy], dst_ref=outs[k].at[t],
                    send_sem=send_sems.at[k, t], recv_sem=recv_sems.at[k, t],
                    device_id=(ox, oy, c), device_id_type=MESH)

    def issue(ins, outs, sems):
        for cp in copies(ins, outs, sems):
            cp.start()

    def drain(ins, outs, sems):
        for cp in copies(ins, outs, sems):
            cp.wait()

    return _Task(ps, [jax.ShapeDtypeStruct((3,) + p.shape[1:], p.dtype) for p in ps],
                 [_dma_sems(n, 3), _dma_sems(n, 3)], issue, drain)


def _chip_sum_call(p, r2, place_arr, *, name, tm=512):
    _, hr, cols = r2.shape
    tm = tm if hr % tm == 0 else hr
    nb = hr // tm

    def body(place_ref, p_ref, r_ref, o_ref):
        acc = p_ref[...].astype(F32)
        for j in range(3):
            acc = acc + r_ref[j].astype(F32)
        o_ref[...] = acc

    return pl.pallas_call(
        body,
        grid_spec=pltpu.PrefetchScalarGridSpec(
            num_scalar_prefetch=1, grid=(nb,),
            in_specs=[pl.BlockSpec((None, tm, cols), lambda i, pc: (pc[0], i, 0)),
                      pl.BlockSpec((3, tm, cols), lambda i, pc: (0, i, 0))],
            out_specs=pl.BlockSpec((tm, cols), lambda i, pc: (pc[1] * nb + i, 0))),
        out_shape=jax.ShapeDtypeStruct((2 * hr, cols), F32), name=name, compiler_params=_cparams(),
    )(place_arr, p, r2)


def _rs_gather_task(gs):
    n = len(gs)

    def copies(ins, outs, sems):
        send_sems, recv_sems = sems
        x, y, c, _ = _place()
        for k in range(n):
            rows = gs[k].shape[0]
            mine, theirs = _half(c, rows), _half(1 - c, rows)
            yield [pltpu.make_async_remote_copy(
                src_ref=ins[k].at[mine], dst_ref=outs[k].at[half],
                send_sem=send_sems.at[k], recv_sem=recv_sems.at[k],
                device_id=(x, y, 1 - c), device_id_type=MESH) for half in (mine, theirs)]

    def issue(ins, outs, sems):
        for outgoing, _ in copies(ins, outs, sems):
            outgoing.start()

    def drain(ins, outs, sems):
        for outgoing, incoming in copies(ins, outs, sems):
            incoming.wait_recv()
            outgoing.wait_send()

    return _Task(gs, [jax.ShapeDtypeStruct(g.shape, g.dtype) for g in gs],
                 [_dma_sems(n), _dma_sems(n)], issue, drain, aliases={k: k for k in range(n)})


def _adamw(w, g, m, v):
    m = ADAM_B1 * m + (1.0 - ADAM_B1) * g
    v = ADAM_B2 * v + (1.0 - ADAM_B2) * jnp.square(g)
    m_hat = m / (1.0 - ADAM_B1 ** ADAM_STEP)
    v_hat = v / (1.0 - ADAM_B2 ** ADAM_STEP)
    delta = -ADAM_LR * (m_hat / (jnp.sqrt(v_hat) + ADAM_EPS) + ADAM_WD * w)
    return delta, m, v


def _adamw_call(items, *, name, tm=256, task=None):
    n = len(items)
    cols = items[0][0].shape[1]
    tiles = [it[0].shape[0] // tm for it in items]
    steps = max(tiles)

    def body(*refs):
        i = pl.program_id(0)
        ins, outs = refs[:4 * n], refs[4 * n:]
        for k in range(n):
            def update(k=k):
                g = ins[4 * k + 1][...]
                res = _adamw(ins[4 * k][...], g, ins[4 * k + 2][...], ins[4 * k + 3][...])
                outs[4 * k][...] = g
                for j in range(3):
                    outs[4 * k + 1 + j][...] = res[j]
            if tiles[k] == steps:
                update()
            else:
                pl.when(i < tiles[k])(update)

    in_specs, out_specs, out_shape, args = [], [], [], []
    for it, t in zip(items, tiles):
        spec = pl.BlockSpec((tm, cols), lambda i, t=t: (jnp.minimum(i, t - 1), 0))
        in_specs += [spec] * 4
        out_specs += [spec] * 4
        out_shape += [jax.ShapeDtypeStruct(it[0].shape, F32)] * 4
        args += list(it)
    res = _pallas(body, grid=(steps,), in_specs=in_specs, out_specs=out_specs, out_shape=out_shape,
                  name=name, task=task)(*args)
    outs, extra = res if task is not None else (res, None)
    grouped = [tuple(outs[4 * k:4 * k + 4]) for k in range(n)]
    return (grouped, extra) if task is not None else grouped


def _adamw_cols_call(w, g_pad, m, v, *, name, tn=512):
    rows, cols = w.shape

    def body(w_ref, g_ref, m_ref, v_ref, go_ref, d_ref, mo_ref, vo_ref):
        g = g_ref[0:rows, :]
        d, mn, vn = _adamw(w_ref[...], g, m_ref[...], v_ref[...])
        go_ref[...] = g
        d_ref[...] = d
        mo_ref[...] = mn
        vo_ref[...] = vn

    spec = pl.BlockSpec((rows, tn), lambda j: (0, j))
    gspec = pl.BlockSpec((g_pad.shape[0], tn), lambda j: (0, j))
    return _pallas(body, grid=(cols // tn,), in_specs=[spec, gspec, spec, spec], out_specs=(spec,) * 4,
                   out_shape=(jax.ShapeDtypeStruct((rows, cols), F32),) * 4, name=name)(w, g_pad, m, v)


N_DEV = 8
SMALL_ROWS = 24
SMALL_LAYOUT = {
    "g_mix_pre": (0, 0, 1, D), "g_mix_post": (1, 0, 1, D), "g_mem_kv": (2, 0, 1, D), "g_mem_pre": (3, 0, 1, D),
    "g_mem_post": (4, 0, 1, D), "g_ff_pre": (5, 0, 1, D), "g_ff_post": (6, 0, 1, D),
    "g_fox_out": (7, 0, 1, D_GRP), "g_chk_out": (7, D_GRP, 1, D_GRP), "b_fgt": (8, 0, 1, 8),
    "rel_bias": (16, 0, 8, N_REL),
}
SMALL = list(SMALL_LAYOUT)


LOSS_ROW = 9


def _small_call(grads, ws, ms, vs, loss_blk, task, *, name):
    n = len(SMALL)
    t_in, t_out = len(task.arrays), len(task.out_shapes)

    def body(*refs):
        g_refs, w_refs, m_refs, v_refs = (refs[j * n:(j + 1) * n] for j in range(4))
        p = 4 * n
        loss_ref, tins = refs[p], refs[p + 1:p + 1 + t_in]
        p += 1 + t_in
        outs, loss_out, touts = refs[p:p + 4 * n], refs[p + 4 * n], refs[p + 4 * n + 1:p + 4 * n + 1 + t_out]
        p += 4 * n + 1 + t_out
        mine, slots, send_sems, recv_sems = refs[p:p + 4]
        tsems = refs[p + 4:]
        task.issue(tins, touts, tsems)
        x, y, c, _ = _place()
        me = 4 * x + 2 * y + c
        mine[...] = jnp.zeros_like(mine)
        for k, name_k in enumerate(SMALL):
            r, l, nr, nl = SMALL_LAYOUT[name_k]
            mine[r:r + nr, l:l + nl] = g_refs[k][0:nr, 0:nl]
        mine[LOSS_ROW:LOSS_ROW + 1, 0:128] = loss_ref[0:1, :]
        slots[me] = mine[...]
        peers = [(dx, dy, dc) for dx in (0, 1) for dy in (0, 1) for dc in (0, 1)][1:]
        cps = []
        for t, (dx, dy, dc) in enumerate(peers):
            px, py, pc = (x + dx) % 2, (y + dy) % 2, (c + dc) % 2
            cps.append(pltpu.make_async_remote_copy(
                src_ref=mine, dst_ref=slots.at[me], send_sem=send_sems.at[t], recv_sem=recv_sems.at[t],
                device_id=(px, py, pc), device_id_type=MESH))
            cps[-1].start()
        for t, (dx, dy, dc) in enumerate(peers):
            px, py, pc = (x + dx) % 2, (y + dy) % 2, (c + dc) % 2
            pltpu.make_async_remote_copy(
                src_ref=mine, dst_ref=slots.at[4 * px + 2 * py + pc], send_sem=send_sems.at[t],
                recv_sem=recv_sems.at[t], device_id=(px, py, pc), device_id_type=MESH).wait_recv()
        for cp in cps:
            cp.wait_send()
        total = slots[0]
        for j in range(1, N_DEV):
            total = total + slots[j]
        for k, name_k in enumerate(SMALL):
            r, l, nr, nl = SMALL_LAYOUT[name_k]
            g = total[r:r + nr, l:l + nl]
            d, mn, vn = _adamw(w_refs[k][...], g, m_refs[k][...], v_refs[k][...])
            for j, val in enumerate((g, d, mn, vn)):
                outs[4 * k + j][...] = val
        loss_out[...] = jnp.broadcast_to(total[LOSS_ROW:LOSS_ROW + 1, 0:128], loss_out.shape)
        task.drain(tins, touts, tsems)

    vm = pl.BlockSpec(memory_space=pltpu.VMEM)
    out_shape = [jax.ShapeDtypeStruct(ws[k].shape, F32) for k in SMALL for _ in range(4)]
    out_shape += [jax.ShapeDtypeStruct((8, 128), F32)] + list(task.out_shapes)
    res = pl.pallas_call(
        body, in_specs=[vm] * (4 * n + 1) + [ANY] * t_in, out_specs=[vm] * (4 * n + 1) + [ANY] * t_out,
        out_shape=out_shape,
        scratch_shapes=[pltpu.VMEM((SMALL_ROWS, D), F32), pltpu.VMEM((N_DEV, SMALL_ROWS, D), F32),
                        _dma_sems(N_DEV - 1), _dma_sems(N_DEV - 1)] + list(task.sems),
        input_output_aliases={4 * n + 1 + i: 4 * n + 1 + j for i, j in task.aliases.items()},
        name=name,
    )(*[d[k] for d in (grads, ws, ms, vs) for k in SMALL], loss_blk, *task.arrays)
    return ({k: tuple(res[4 * i:4 * i + 4]) for i, k in enumerate(SMALL)}, res[4 * n], list(res[4 * n + 1:]))


WEIGHTS = ["w_in", "b_fgt", "rel_bias", "g_fox_out", "g_chk_out", "w_out", "g_mix_pre", "g_mix_post", "g_mem_kv",
           "w_mq", "w_mk", "w_mv", "w_mo", "g_mem_pre", "g_mem_post", "w_ff1", "w_ff2", "g_ff_pre", "g_ff_post"]
BIG = ["w_in", "w_out", "w_mq", "w_mk", "w_mv", "w_mo", "w_ff1", "w_ff2"]


IN_SHARD = D_IN // N_CHIP
IN_PAD = 800


IN_PIECES = [(0, 0, 770), (800, 770, 766), (1566, 3072, 4), (1600, 3076, 4), (1604, 1536, 766), (2400, 2302, 770)]
PAD_ZEROS = [(800 * j + IN_SHARD, IN_PAD - IN_SHARD) for j in range(N_CHIP)]
ALL_ZEROS = [(D_IN, D_ALL - D_IN)]


def _reorder_rows_call(src, to_all, *, name, tn=512):
    rows, cols = src.shape
    zeros = ALL_ZEROS if to_all else PAD_ZEROS

    def body(s_ref, o_ref):
        for pad0, all0, cnt in IN_PIECES:
            s0, d0 = (pad0, all0) if to_all else (all0, pad0)
            o_ref[d0:d0 + cnt, :] = s_ref[s0:s0 + cnt, :]
        for z0, cnt in zeros:
            o_ref[z0:z0 + cnt, :] = jnp.zeros((cnt, tn), src.dtype)

    spec = pl.BlockSpec((rows, tn), lambda j: (0, j))
    return _pallas(body, grid=(cols // tn,), in_specs=[spec], out_specs=spec,
                   out_shape=jax.ShapeDtypeStruct((rows, cols), src.dtype), name=name)(src)


def kernel(x, mem, w_in, b_fgt, rel_bias, g_fox_out, g_chk_out, w_out, g_mix_pre, g_mix_post, g_mem_kv, w_mq, w_mk, w_mv, w_mo, g_mem_pre, g_mem_post, w_ff1, w_ff2, g_ff_pre, g_ff_post, loss_target, m_w_in, m_b_fgt, m_rel_bias, m_g_fox_out, m_g_chk_out, m_w_out, m_g_mix_pre, m_g_mix_post, m_g_mem_kv, m_w_mq, m_w_mk, m_w_mv, m_w_mo, m_g_mem_pre, m_g_mem_post, m_w_ff1, m_w_ff2, m_g_ff_pre, m_g_ff_post, v_w_in, v_b_fgt, v_rel_bias, v_g_fox_out, v_g_chk_out, v_w_out, v_g_mix_pre, v_g_mix_post, v_g_mem_kv, v_w_mq, v_w_mk, v_w_mv, v_w_mo, v_g_mem_pre, v_g_mem_post, v_w_ff1, v_w_ff2, v_g_ff_pre, v_g_ff_post):
    w = dict(w_in=w_in, b_fgt=b_fgt, rel_bias=rel_bias, g_fox_out=g_fox_out, g_chk_out=g_chk_out, w_out=w_out,
             g_mix_pre=g_mix_pre, g_mix_post=g_mix_post, g_mem_kv=g_mem_kv, w_mq=w_mq, w_mk=w_mk, w_mv=w_mv,
             w_mo=w_mo, g_mem_pre=g_mem_pre, g_mem_post=g_mem_post, w_ff1=w_ff1, w_ff2=w_ff2, g_ff_pre=g_ff_pre,
             g_ff_post=g_ff_post)
    m = dict(w_in=m_w_in, b_fgt=m_b_fgt, rel_bias=m_rel_bias, g_fox_out=m_g_fox_out, g_chk_out=m_g_chk_out,
             w_out=m_w_out, g_mix_pre=m_g_mix_pre, g_mix_post=m_g_mix_post, g_mem_kv=m_g_mem_kv, w_mq=m_w_mq,
             w_mk=m_w_mk, w_mv=m_w_mv, w_mo=m_w_mo, g_mem_pre=m_g_mem_pre, g_mem_post=m_g_mem_post,
             w_ff1=m_w_ff1, w_ff2=m_w_ff2, g_ff_pre=m_g_ff_pre, g_ff_post=m_g_ff_post)
    v = dict(w_in=v_w_in, b_fgt=v_b_fgt, rel_bias=v_rel_bias, g_fox_out=v_g_fox_out, g_chk_out=v_g_chk_out,
             w_out=v_w_out, g_mix_pre=v_g_mix_pre, g_mix_post=v_g_mix_post, g_mem_kv=v_g_mem_kv, w_mq=v_w_mq,
             w_mk=v_w_mk, w_mv=v_w_mv, w_mo=v_w_mo, g_mem_pre=v_g_mem_pre, g_mem_post=v_g_mem_post,
             w_ff1=v_w_ff1, w_ff2=v_w_ff2, g_ff_pre=v_g_ff_pre, g_ff_post=v_g_ff_post)

    def rows(d, k):
        return d[k][0] if k == "rel_bias" else d[k]

    xs, mems, target = x[0], mem[0], loss_target[0]
    S = xs.shape[0]
    sp = {k: rows(w, k) for k in SMALL}
    b_pad = jnp.pad(sp["b_fgt"], ((0, 0), (0, 120)))
    chip = 2 * lax.axis_index("x") + lax.axis_index("y")
    chip_arr = jnp.reshape(chip, (1,)).astype(jnp.int32)
    c_arr = jnp.reshape(lax.axis_index("c"), (1,)).astype(jnp.int32)
    place_arr = jnp.concatenate([chip_arr, c_arr])
    w_in_t, m_in_t, v_in_t = w["w_in"][0].T, m["w_in"][0].T, v["w_in"][0].T
    slab = {"w_in": _cast_slab_call(w_in_t, chip_arr, name="cast_w_in", pad_rows=IN_PAD - IN_SHARD)}

    def gather_ici(names):
        return _ag_ici_task([slab[k] for k in names])

    def pair_add(k, d, r1):
        return _pair_add_call(d, r1, c_arr, name="rs_pair_add_" + k)

    rest, (g_in,) = _cast_slabs_call([w[k][0] for k in BIG[1:]], chip_arr, name="cast_rest",
                                     task=gather_ici(["w_in"]))
    slab.update(zip(BIG[1:], rest))
    h1, (g_in,) = _rms_fwd_call(xs, sp["g_mix_pre"], name="rms_mix_pre", task=_ag_d2d_task([g_in]))
    w_all_t = _reorder_rows_call(g_in.reshape(N_CHIP * IN_PAD, D), True, name="w_in_rows")
    proj, (g_out, g_mq) = _mm_nt(h1, w_all_t, "plain", rows=(0, 3072), name="mm_proj",
                                 task=gather_ici(["w_out", "w_mq"]))
    fl_raw = _mm_nt(h1, w_all_t, "plain", rows=(3072, 128), name="mm_gate", out_dtype=F32, tn=128)
    c_rep, c_t = _fox_prep_call(fl_raw, b_pad, name="fox_prep")
    bias = _chk_bias_call(_rel_table_to_g(sp["rel_bias"]), name="chk_bias")
    mid = ["w_mk", "w_mv", "w_mo", "w_ff1"]
    (yf, lse), got = _fox_fwd_call(proj, c_rep, c_t, name="fox_fwd",
                                   task=_merge_tasks([gather_ici(mid), _ag_d2d_task([g_out, g_mq])]))
    g_mid, (g_out, g_mq) = got[:4], got[4:]
    yc, got = _chk_fwd_call(proj, bias, name="chk_fwd",
                            task=_merge_tasks([gather_ici(["w_ff2"]), _ag_d2d_task(g_mid)]))
    g_ff2, (g_mk, g_mv, g_mo, g_ff1) = got[0], got[1:]
    yn = _mix_norm_fwd_call(yf, yc, sp["g_fox_out"], sp["g_chk_out"], name="mix_norm_fwd")
    z, (g_ff2,) = _mm_nn(yn, g_out, "rows", name="mm_out", out_dtype=F32, task=_ag_d2d_task([g_ff2]))
    x1, h2 = _post_pre_call(xs, z, sp["g_mix_post"], sp["g_mem_pre"], name="post_mix")
    memn = _rms_fwd_call(mems, sp["g_mem_kv"], name="rms_mem_kv")
    q2 = _mm_nn(h2, g_mq, "rows", name="mm_mq")
    k2 = _mm_nn(memn, g_mk, "rows", name="mm_mk")
    v2 = _mm_nn(memn, g_mv, "rows", name="mm_mv")
    o2 = _mem_fwd_call(q2, k2, v2, name="mem_fwd")
    y2 = _mm_nn(o2, g_mo, "rows", name="mm_mo", out_dtype=F32)
    x2, h3 = _post_pre_call(x1, y2, sp["g_mem_post"], sp["g_ff_pre"], name="post_mem")
    act, relu = _mm_nn(h3, g_ff1, "cols", name="mm_ff1", epi="relu2")
    y3 = _mm_nn(act, g_ff2, "rows", name="mm_ff2", out_dtype=F32, tm=1024)
    loss_blk, dx3, dy3, dg_ff_post = _final_call(x2, y3, sp["g_ff_post"], target, name="final")

    d_ff2 = _mm_tn(act, dy3, name="mm_dff2", tk=512, tn=1024).reshape(N_CHIP, D_FF // N_CHIP, D)
    du, (r1,) = _mm_nt(dy3, g_ff2, "rows", name="mm_du", mul2r=relu, task=_rs_pair_task([d_ff2]))
    p_ff2 = pair_add("w_ff2", d_ff2, r1)
    d_ff1 = _mm_tn(h3, du, name="mm_dff1", cols4=True)
    dh3, (r1,) = _mm_nt(du, g_ff1, "cols", name="mm_dh3", out_dtype=F32, tm=1024, task=_rs_pair_task([d_ff1]))
    p_ff1 = pair_add("w_ff1", d_ff1, r1)
    dx2, dy2, dg_ff_pre, dg_mem_post = _bwd_mid_call(dx3, x2, dh3, sp["g_ff_pre"], y2, sp["g_mem_post"], name="bwd_ff")
    d_mo = _mm_tn(o2, dy2, name="mm_dmo").reshape(N_CHIP, D // N_CHIP, D)
    do2 = _mm_nt(dy2, g_mo, "rows", name="mm_do2")
    dq2, dk2, dv2 = _mem_bwd_call(q2, k2, v2, do2, name="mem_bwd")
    d_mq = _mm_tn(h2, dq2, name="mm_dmq").reshape(N_CHIP, D // N_CHIP, D)
    dh2 = _mm_nt(dq2, g_mq, "rows", name="mm_dh2", out_dtype=F32)
    d_mk = _mm_tn(memn, dk2, name="mm_dmk").reshape(N_CHIP, D // N_CHIP, D)
    d_mv = _mm_tn(memn, dv2, name="mm_dmv").reshape(N_CHIP, D // N_CHIP, D)
    dmn_k = _mm_nt(dk2, g_mk, "rows", name="mm_dmemk", out_dtype=F32)
    dmn_v = _mm_nt(dv2, g_mv, "rows", name="mm_dmemv", out_dtype=F32)
    dg_mem_kv = _gain_grad_call(mems, sp["g_mem_kv"], dmn_k, dmn_v, name="gain_mem_kv")
    dx1, dz, dg_mem_pre, dg_mix_post = _bwd_mid_call(dx2, x1, dh2, sp["g_mem_pre"], z, sp["g_mix_post"], name="bwd_mem")
    d_out = _mm_tn(yn, dz, name="mm_dout").reshape(N_CHIP, D // N_CHIP, D)
    late = ["w_mo", "w_mq", "w_mk", "w_mv", "w_out"]
    d_late = [d_mo, d_mq, d_mk, d_mv, d_out]
    dyn, r1_late = _mm_nt(dz, g_out, "rows", name="mm_dyn", out_dtype=F32, task=_rs_pair_task(d_late))
    p_late = [pair_add(k, d, r1) for k, d, r1 in zip(late, d_late, r1_late)]
    dof, doc, delta, dg_fox, dg_chk = _mix_norm_bwd_call(dyn, yf, yc, sp["g_fox_out"], sp["g_chk_out"], name="mix_norm_bwd")
    (dqf, dkf, dvf, dcq, dck), r2_ff = _fox_bwd_call(proj, dof, lse, delta, c_rep, c_t, name="fox_bwd",
                                                      task=_rs_chip_task([p_ff2, p_ff1]))
    (dqc, dkc, dvc, dgrev), r2_late = _chk_bwd_call(proj, doc, bias, name="chk_bwd", task=_rs_chip_task(p_late))
    first = ["w_ff2", "w_ff1"] + late
    f_first = [_chip_sum_call(p, r, place_arr, name="rs_chip_sum_" + k)
               for k, p, r in zip(first, [p_ff2, p_ff1] + p_late, r2_ff + r2_late)]
    dc8 = dcq[:, :, 0:2, :].transpose(0, 2, 1, 3).reshape(8, S)
    dc_rows = jnp.concatenate([dc8, jnp.zeros((120, S), F32)], axis=0)
    dfl, db_fgt = _fox_gate_bwd_call(dc_rows, dck, fl_raw, b_pad, name="fox_gate_bwd")
    dproj = jnp.concatenate([dqf, dkf, dvf, dqc, dkc, dvc, dfl], axis=1)
    d_all_t, g_first = _mm_tn(dproj, h1, name="mm_dwin", tk=640, tn=1024, task=_rs_gather_task(f_first))
    grads = dict(zip(first, g_first))
    d_in = _reorder_rows_call(d_all_t, False, name="d_in_rows").reshape(N_CHIP, IN_PAD, D)
    delta_w, new_m, new_v = {}, {}, {}

    def adamw_items(names):
        return [(w[k][0], grads[k], m[k][0], v[k][0]) for k in names]

    upd_late, (r1,) = _adamw_call(adamw_items(late), name="adamw_late", tm=64, task=_rs_pair_task([d_in]))
    p_in = pair_add("w_in", d_in, r1)
    dh1, (r2_in,) = _mm_nn(dproj, w_all_t, "plain", name="mm_dh1", out_dtype=F32, tm=1024,
                           task=_rs_chip_task([p_in]))
    f_in = _chip_sum_call(p_in, r2_in, place_arr, name="rs_chip_sum_w_in")
    upd_ff = _adamw_call(adamw_items(first[:2]), name="adamw_ff")
    for k, res in zip(late + first[:2], upd_late + upd_ff):
        grads[k], delta_w[k], new_m[k], new_v[k] = res
    grad_x, dg_mix_pre = _bwd_last_call(dx1, xs, dh1, sp["g_mix_pre"], name="bwd_mix")

    small_g = {"g_mix_pre": dg_mix_pre, "g_mix_post": dg_mix_post, "g_mem_kv": dg_mem_kv, "g_mem_pre": dg_mem_pre,
               "g_mem_post": dg_mem_post, "g_ff_pre": dg_ff_pre, "g_ff_post": dg_ff_post, "g_fox_out": dg_fox,
               "g_chk_out": dg_chk, "b_fgt": db_fgt,
               "rel_bias": _g_to_rel_table(dgrev[:, 0:2, :].reshape(8, ROLL_W))}
    small, loss_out, (g_w_in,) = _small_call(
        small_g, sp, {k: rows(m, k) for k in SMALL}, {k: rows(v, k) for k in SMALL}, loss_blk,
        _rs_gather_task([f_in]), name="small_allreduce_adamw")
    loss = loss_out[0, 0]
    res = _adamw_cols_call(w_in_t, g_w_in, m_in_t, v_in_t, name="adamw_w_in")
    grads["w_in"], delta_w["w_in"], new_m["w_in"], new_v["w_in"] = (a.T for a in res)
    for k in SMALL:
        vals = small[k]
        if k == "rel_bias":
            vals = tuple(a[None] for a in vals)
        grads[k], delta_w[k], new_m[k], new_v[k] = vals

    def out(d, k):
        return d[k][None] if k in BIG else d[k]

    return (loss, grad_x[None], *[out(grads, k) for k in WEIGHTS], *[out(delta_w, k) for k in WEIGHTS],
            *[out(new_m, k) for k in WEIGHTS], *[out(new_v, k) for k in WEIGHTS])
```

```python
import functools

import jax
import jax.numpy as jnp
from jax import lax
from jax.experimental import pallas as pl
from jax.experimental.pallas import tpu as pltpu

F32 = jnp.float32
BF16 = jnp.bfloat16

D = 1024
HEAD = 64
N_PAIR = 4
D_GRP = 512
CHUNK = 64
LEFT = 8
MAX_REL = 128
N_REL = 2 * MAX_REL + 1
N_MEM = 256
MEM_HEADS = 4
MEM_HD = 256
D_FF = 4096
D_IN = 3080
D_ALL = 3200
EPS = 1e-6
TQ = 256
WIN = (LEFT + TQ // CHUNK) * CHUNK
PADK = LEFT * CHUNK
ROLL_W = 1024
NEG = -1e30
N_CHIP = 4
VMEM_LIMIT = 48 * 1024 * 1024

ADAM_LR = 0.001
ADAM_B1 = 0.9
ADAM_B2 = 0.999
ADAM_EPS = 1e-08
ADAM_WD = 0.01
ADAM_STEP = 10

MESH = pl.DeviceIdType.MESH


def _cparams():
    return pltpu.CompilerParams(vmem_limit_bytes=VMEM_LIMIT)


ANY = pl.BlockSpec(memory_space=pl.ANY)


class _Task:
    def __init__(self, arrays, out_shapes, sems, issue, drain, aliases=None):
        self.arrays, self.out_shapes, self.sems = list(arrays), list(out_shapes), list(sems)
        self.issue, self.drain, self.aliases = issue, drain, dict(aliases or {})


def _merge_tasks(tasks):
    tasks = [t for t in tasks if t is not None]
    if len(tasks) == 1:
        return tasks[0]
    cuts, a, o, s = [], 0, 0, 0
    aliases = {}
    for t in tasks:
        cuts.append((a, o, s))
        aliases.update({a + i: o + j for i, j in t.aliases.items()})
        a, o, s = a + len(t.arrays), o + len(t.out_shapes), s + len(t.sems)

    def part(fn_name):
        def run(ins, outs, sems):
            for t, (a0, o0, s0) in zip(tasks, cuts):
                getattr(t, fn_name)(ins[a0:a0 + len(t.arrays)], outs[o0:o0 + len(t.out_shapes)],
                                    sems[s0:s0 + len(t.sems)])
        return run

    return _Task([x for t in tasks for x in t.arrays], [x for t in tasks for x in t.out_shapes],
                 [x for t in tasks for x in t.sems], part("issue"), part("drain"), aliases)


def _pallas(body, *, grid, in_specs, out_specs, out_shape, name, scratch_shapes=(), task=None, prefetch=0):
    def make(kernel, i_specs, o_specs, o_shape, scratch, aliases):
        if prefetch:
            spec = pltpu.PrefetchScalarGridSpec(num_scalar_prefetch=prefetch, grid=grid, in_specs=i_specs,
                                                out_specs=o_specs, scratch_shapes=scratch)
            return pl.pallas_call(kernel, grid_spec=spec, out_shape=o_shape, input_output_aliases=aliases,
                                  name=name, compiler_params=_cparams())
        return pl.pallas_call(kernel, grid=grid, in_specs=i_specs, out_specs=o_specs, out_shape=o_shape,
                              scratch_shapes=scratch, input_output_aliases=aliases, name=name,
                              compiler_params=_cparams())

    if task is None:
        return make(body, list(in_specs), out_specs, out_shape, list(scratch_shapes), {})
    single = not isinstance(out_shape, (tuple, list))
    o_shapes = [out_shape] if single else list(out_shape)
    o_specs = [out_specs] if single else list(out_specs)
    n_in, n_out, n_scr = len(in_specs), len(o_shapes), len(scratch_shapes)
    t_in, t_out = len(task.arrays), len(task.out_shapes)

    def carried(*refs):
        cut = [prefetch, n_in, t_in, n_out, t_out, n_scr]
        parts, p = [], 0
        for c in cut:
            parts.append(refs[p:p + c])
            p += c
        scalars, ins, tins, outs, touts, scr = parts
        tsems = refs[p:]
        ids = [pl.program_id(a) for a in range(len(grid))]
        first = functools.reduce(jnp.logical_and, [i == 0 for i in ids])
        last = functools.reduce(jnp.logical_and, [i == g - 1 for i, g in zip(ids, grid)])

        @pl.when(first)
        def _():
            task.issue(tins, touts, tsems)
        body(*scalars, *ins, *outs, *scr)

        @pl.when(last)
        def _():
            task.drain(tins, touts, tsems)

    call = make(carried, list(in_specs) + [ANY] * t_in, o_specs + [ANY] * t_out,
                o_shapes + list(task.out_shapes), list(scratch_shapes) + list(task.sems),
                {prefetch + n_in + i: n_out + j for i, j in task.aliases.items()})

    def run(*args):
        res = call(*args, *task.arrays)
        outs = res[:n_out]
        return (outs[0] if single else tuple(outs)), list(res[n_out:])

    return run


def _dot(a, b):
    return jnp.dot(a, b, preferred_element_type=F32)


def _dot_nt(a, b):
    return lax.dot_general(a, b, (((1,), (1,)), ((), ())), preferred_element_type=F32)


def _dot_tn(a, b):
    return lax.dot_general(a, b, (((0,), (0,)), ((), ())), preferred_element_type=F32)


def _split3(x):
    hi = x.astype(BF16)
    r1 = x - hi.astype(F32)
    mid = r1.astype(BF16)
    lo = (r1 - mid.astype(F32)).astype(BF16)
    return hi, mid, lo


def _dot3(x, m01):
    hi, mid, lo = _split3(x)
    return _dot(hi, m01) + _dot(mid, m01) + _dot(lo, m01)


def _dot3_l(m01, x):
    hi, mid, lo = _split3(x)
    return _dot(m01, hi) + _dot(m01, mid) + _dot(m01, lo)


def _mm_nn(a, b, kind, *, name, out_dtype=BF16, tm=2048, tn=512, epi=None, task=None):
    M, K = a.shape
    if kind == "plain":
        N = b.shape[1]
        b_spec = pl.BlockSpec((K, tn), lambda m, n: (0, n))
    elif kind == "rows":
        N = b.shape[2]
        b_spec = pl.BlockSpec((N_CHIP, K // N_CHIP, tn), lambda m, n: (0, 0, n))
    else:
        nq = b.shape[2]
        N = N_CHIP * nq
        per = nq // tn
        b_spec = pl.BlockSpec((None, K, tn), lambda m, n: (n // per, 0, n % per))
    tm = min(tm, M)
    kq = K // N_CHIP

    def body(a_ref, b_ref, *o_refs):
        if kind == "rows":
            acc = _dot(a_ref[:, 0:kq], b_ref[0])
            for j in range(1, N_CHIP):
                acc += _dot(a_ref[:, j * kq:(j + 1) * kq], b_ref[j])
        else:
            acc = _dot(a_ref[...], b_ref[...])
        if epi == "relu2":
            r = jnp.maximum(acc, 0.0)
            o_refs[0][...] = (r * r).astype(BF16)
            o_refs[1][...] = r.astype(BF16)
        else:
            o_refs[0][...] = acc.astype(out_dtype)

    o_spec = pl.BlockSpec((tm, tn), lambda m, n: (m, n))
    if epi == "relu2":
        out_shape = (jax.ShapeDtypeStruct((M, N), BF16), jax.ShapeDtypeStruct((M, N), BF16))
        out_specs = (o_spec, o_spec)
    else:
        out_shape = jax.ShapeDtypeStruct((M, N), out_dtype)
        out_specs = o_spec
    return _pallas(
        body, grid=(M // tm, N // tn),
        in_specs=[pl.BlockSpec((tm, K), lambda m, n: (m, 0)), b_spec],
        out_specs=out_specs, out_shape=out_shape, name=name, task=task,
    )(a, b)


def _mm_nt(a, b, kind, *, name, out_dtype=BF16, tm=2048, tn=512, mul2r=None, task=None, rows=None):
    M, K = a.shape
    if kind == "plain":
        first, N = rows if rows is not None else (0, b.shape[0])
        n0 = first // tn
        b_spec = pl.BlockSpec((tn, K), lambda m, n: (n0 + n, 0))
    elif kind == "rows":
        nq = b.shape[1]
        N = N_CHIP * nq
        tn = min(tn, nq)
        per = nq // tn
        b_spec = pl.BlockSpec((None, tn, K), lambda m, n: (n // per, n % per, 0))
    else:
        N = b.shape[1]
        b_spec = pl.BlockSpec((N_CHIP, tn, K // N_CHIP), lambda m, n: (0, n, 0))
    tm = min(tm, M)
    kq = K // N_CHIP

    def body(a_ref, b_ref, *rest):
        o_ref = rest[-1]
        if kind == "cols":
            acc = _dot_nt(a_ref[:, 0:kq], b_ref[0])
            for j in range(1, N_CHIP):
                acc += _dot_nt(a_ref[:, j * kq:(j + 1) * kq], b_ref[j])
        else:
            acc = _dot_nt(a_ref[...], b_ref[...])
        if mul2r is not None:
            acc = acc * (2.0 * rest[0][...].astype(F32))
        o_ref[...] = acc.astype(out_dtype)

    in_specs = [pl.BlockSpec((tm, K), lambda m, n: (m, 0)), b_spec]
    args = [a, b]
    if mul2r is not None:
        in_specs.append(pl.BlockSpec((tm, tn), lambda m, n: (m, n)))
        args.append(mul2r)
    return _pallas(
        body, grid=(M // tm, N // tn), in_specs=in_specs,
        out_specs=pl.BlockSpec((tm, tn), lambda m, n: (m, n)),
        out_shape=jax.ShapeDtypeStruct((M, N), out_dtype), name=name, task=task,
    )(*args)


def _mm_tn(a, b, *, name, out_dtype=BF16, tk=1024, tn=512, cols4=False, task=None):
    M, K1 = a.shape
    N = b.shape[1]
    tk = min(tk, K1)
    tn = min(tn, N)

    def body(a_ref, b_ref, o_ref):
        o_ref[...] = _dot_tn(a_ref[...], b_ref[...]).astype(out_dtype)

    if cols4:
        per = (N // N_CHIP) // tn
        out_shape = jax.ShapeDtypeStruct((N_CHIP, K1, N // N_CHIP), out_dtype)
        o_spec = pl.BlockSpec((None, tk, tn), lambda k, n: (n // per, k, n % per))
    else:
        out_shape = jax.ShapeDtypeStruct((K1, N), out_dtype)
        o_spec = pl.BlockSpec((tk, tn), lambda k, n: (k, n))
    return _pallas(
        body, grid=(K1 // tk, N // tn),
        in_specs=[pl.BlockSpec((M, tk), lambda k, n: (0, k)), pl.BlockSpec((M, tn), lambda k, n: (0, n))],
        out_specs=o_spec, out_shape=out_shape, name=name, task=task,
    )(a, b)


def _rms(x, g):
    r = lax.rsqrt(jnp.mean(x * x, axis=-1, keepdims=True) + EPS)
    return x * r * g


def _rms_bwd(x, g, dy):
    r = lax.rsqrt(jnp.mean(x * x, axis=-1, keepdims=True) + EPS)
    xh = x * r
    dg = jnp.sum(dy * xh, axis=0, keepdims=True)
    dxh = dy * g
    dx = r * (dxh - xh * jnp.mean(dxh * xh, axis=-1, keepdims=True))
    return dx, dg


def _row_spec(tm, n):
    return pl.BlockSpec((tm, n), lambda i: (i, 0))


def _vec_spec(n):
    return pl.BlockSpec((1, n), lambda i: (0, 0))


def _acc_spec(n):
    return pl.BlockSpec((8, n), lambda i: (0, 0))


def _acc_add(ref, row, i):
    @pl.when(i == 0)
    def _():
        ref[...] = jnp.zeros_like(ref)
    ref[0:1, :] += row


def _rms_fwd_call(x, g, *, name, tm=1024, task=None):
    M, n = x.shape
    tm = min(tm, M)

    def body(x_ref, g_ref, h_ref):
        h_ref[...] = _rms(x_ref[...], g_ref[...]).astype(BF16)

    return _pallas(
        body, grid=(M // tm,), in_specs=[_row_spec(tm, n), _vec_spec(n)], out_specs=_row_spec(tm, n),
        out_shape=jax.ShapeDtypeStruct((M, n), BF16), name=name, task=task,
    )(x, g)


def _post_pre_call(xres, z, g_post, g_pre, *, name, tm=1024):
    M, n = xres.shape

    def body(x_ref, z_ref, gp_ref, gn_ref, xo_ref, h_ref):
        xn = x_ref[...] + _rms(z_ref[...], gp_ref[...])
        xo_ref[...] = xn
        h_ref[...] = _rms(xn, gn_ref[...]).astype(BF16)

    return pl.pallas_call(
        body, grid=(M // tm,),
        in_specs=[_row_spec(tm, n), _row_spec(tm, n), _vec_spec(n), _vec_spec(n)],
        out_specs=(_row_spec(tm, n), _row_spec(tm, n)),
        out_shape=(jax.ShapeDtypeStruct((M, n), F32), jax.ShapeDtypeStruct((M, n), BF16)),
        name=name, compiler_params=_cparams(),
    )(xres, z, g_post, g_pre)


def _final_call(x2, y3, g_post, target, *, name, tm=512):
    M, n = x2.shape

    def body(x_ref, y_ref, g_ref, t_ref, loss_ref, dx_ref, dy_ref, dg_ref):
        i = pl.program_id(0)
        y = y_ref[...]
        g = g_ref[...]
        diff = x_ref[...] + _rms(y, g) - t_ref[...]
        part = 0.5 * jnp.sum(jnp.sum(diff * diff, axis=1, keepdims=True), axis=0, keepdims=True) / n

        @pl.when(i == 0)
        def _():
            loss_ref[...] = jnp.zeros_like(loss_ref)
        loss_ref[...] += jnp.broadcast_to(part, loss_ref.shape)
        dx = diff / n
        dx_ref[...] = dx
        dy, dg = _rms_bwd(y, g, dx)
        dy_ref[...] = dy.astype(BF16)
        _acc_add(dg_ref, dg, i)

    return pl.pallas_call(
        body, grid=(M // tm,),
        in_specs=[_row_spec(tm, n), _row_spec(tm, n), _vec_spec(n), _row_spec(tm, n)],
        out_specs=(pl.BlockSpec((8, 128), lambda i: (0, 0)), _row_spec(tm, n), _row_spec(tm, n), _acc_spec(n)),
        out_shape=(jax.ShapeDtypeStruct((8, 128), F32), jax.ShapeDtypeStruct((M, n), F32),
                   jax.ShapeDtypeStruct((M, n), BF16), jax.ShapeDtypeStruct((8, n), F32)),
        name=name, compiler_params=_cparams(),
    )(x2, y3, g_post, target)


def _bwd_mid_call(dx_in, x, dh, g_pre, y, g_post, *, name, tm=512):
    M, n = x.shape

    def body(dxi_ref, x_ref, dh_ref, gpre_ref, y_ref, gpost_ref, dx_ref, dy_ref, dgpre_ref, dgpost_ref):
        i = pl.program_id(0)
        d1, dg1 = _rms_bwd(x_ref[...], gpre_ref[...], dh_ref[...])
        dx = dxi_ref[...] + d1
        dx_ref[...] = dx
        dy, dg2 = _rms_bwd(y_ref[...], gpost_ref[...], dx)
        dy_ref[...] = dy.astype(BF16)
        _acc_add(dgpre_ref, dg1, i)
        _acc_add(dgpost_ref, dg2, i)

    return pl.pallas_call(
        body, grid=(M // tm,),
        in_specs=[_row_spec(tm, n), _row_spec(tm, n), _row_spec(tm, n), _vec_spec(n), _row_spec(tm, n), _vec_spec(n)],
        out_specs=(_row_spec(tm, n), _row_spec(tm, n), _acc_spec(n), _acc_spec(n)),
        out_shape=(jax.ShapeDtypeStruct((M, n), F32), jax.ShapeDtypeStruct((M, n), BF16),
                   jax.ShapeDtypeStruct((8, n), F32), jax.ShapeDtypeStruct((8, n), F32)),
        name=name, compiler_params=_cparams(),
    )(dx_in, x, dh, g_pre, y, g_post)


def _bwd_last_call(dx_in, x, dh, g_pre, *, name, tm=1024, task=None):
    M, n = x.shape

    def body(dxi_ref, x_ref, dh_ref, g_ref, dx_ref, dg_ref):
        i = pl.program_id(0)
        d1, dg1 = _rms_bwd(x_ref[...], g_ref[...], dh_ref[...])
        dx_ref[...] = dxi_ref[...] + d1
        _acc_add(dg_ref, dg1, i)

    return _pallas(
        body, grid=(M // tm,),
        in_specs=[_row_spec(tm, n), _row_spec(tm, n), _row_spec(tm, n), _vec_spec(n)],
        out_specs=(_row_spec(tm, n), _acc_spec(n)),
        out_shape=(jax.ShapeDtypeStruct((M, n), F32), jax.ShapeDtypeStruct((8, n), F32)),
        name=name, task=task,
    )(dx_in, x, dh, g_pre)


def _gain_grad_call(x, g, dy_a, dy_b, *, name):
    M, n = x.shape

    def body(x_ref, g_ref, a_ref, b_ref, dg_ref):
        _, dg = _rms_bwd(x_ref[...], g_ref[...], a_ref[...] + b_ref[...])
        dg_ref[...] = jnp.zeros_like(dg_ref)
        dg_ref[0:1, :] = dg

    return pl.pallas_call(
        body, grid=(1,),
        in_specs=[_row_spec(M, n), _vec_spec(n), _row_spec(M, n), _row_spec(M, n)],
        out_specs=_acc_spec(n), out_shape=jax.ShapeDtypeStruct((8, n), F32),
        name=name, compiler_params=_cparams(),
    )(x, g, dy_a, dy_b)


def _head_group_matrix():
    a = lax.broadcasted_iota(jnp.int32, (D_GRP, D_GRP), 0) // HEAD
    b = lax.broadcasted_iota(jnp.int32, (D_GRP, D_GRP), 1) // HEAD
    return jnp.where(a == b, 1.0, 0.0).astype(BF16)


def _mix_norm_fwd_call(yf, yc, gf, gc, *, name, tm=1024):
    M = yf.shape[0]

    def body(yf_ref, yc_ref, gf_ref, gc_ref, o_ref):
        o_ref[:, 0:D_GRP] = _rms(yf_ref[...], gf_ref[...]).astype(BF16)
        o_ref[:, D_GRP:D] = _rms(yc_ref[...], gc_ref[...]).astype(BF16)

    return pl.pallas_call(
        body, grid=(M // tm,),
        in_specs=[_row_spec(tm, D_GRP), _row_spec(tm, D_GRP), _vec_spec(D_GRP), _vec_spec(D_GRP)],
        out_specs=_row_spec(tm, D), out_shape=jax.ShapeDtypeStruct((M, D), BF16),
        name=name, compiler_params=_cparams(),
    )(yf, yc, gf, gc)


def _mix_norm_bwd_call(dyn, yf, yc, gf, gc, *, name, tm=2 * TQ):
    M = yf.shape[0]

    def body(dyn_ref, yf_ref, yc_ref, gf_ref, gc_ref, dof_ref, doc_ref, delta_ref, dgf_ref, dgc_ref):
        i = pl.program_id(0)
        yf_ = yf_ref[...]
        dof, dgf = _rms_bwd(yf_, gf_ref[...], dyn_ref[:, 0:D_GRP])
        doc, dgc = _rms_bwd(yc_ref[...], gc_ref[...], dyn_ref[:, D_GRP:D])
        dof_b = dof.astype(BF16)
        dof_ref[...] = dof_b
        doc_ref[...] = doc.astype(BF16)
        prod = dof_b.astype(F32) * yf_
        hi = prod.astype(BF16)
        lo = (prod - hi.astype(F32)).astype(BF16)
        grp = _head_group_matrix()
        delta = _dot(hi, grp) + _dot(lo, grp)
        for b in range(tm // TQ):
            delta_ref[b] = delta[b * TQ:(b + 1) * TQ, :].T
        _acc_add(dgf_ref, dgf, i)
        _acc_add(dgc_ref, dgc, i)

    return pl.pallas_call(
        body, grid=(M // tm,),
        in_specs=[_row_spec(tm, D), _row_spec(tm, D_GRP), _row_spec(tm, D_GRP), _vec_spec(D_GRP), _vec_spec(D_GRP)],
        out_specs=(_row_spec(tm, D_GRP), _row_spec(tm, D_GRP),
                   pl.BlockSpec((tm // TQ, D_GRP, TQ), lambda i: (i, 0, 0)), _acc_spec(D_GRP), _acc_spec(D_GRP)),
        out_shape=(jax.ShapeDtypeStruct((M, D_GRP), BF16), jax.ShapeDtypeStruct((M, D_GRP), BF16),
                   jax.ShapeDtypeStruct((M // TQ, D_GRP, TQ), F32), jax.ShapeDtypeStruct((8, D_GRP), F32),
                   jax.ShapeDtypeStruct((8, D_GRP), F32)),
        name=name, compiler_params=_cparams(),
    )(dyn, yf, yc, gf, gc)


def _tri(n, lower_incl):
    a = lax.broadcasted_iota(jnp.int32, (n, n), 0)
    b = lax.broadcasted_iota(jnp.int32, (n, n), 1)
    return jnp.where(a >= b, 1.0, 0.0).astype(BF16) if lower_incl else jnp.where(a <= b, 1.0, 0.0).astype(BF16)


def _fox_prep_call(fl_raw, b_pad, *, name):
    S = fl_raw.shape[0]
    nb = S // TQ

    def body(fl_ref, b_ref, crep_ref, ct_ref, carry_ref):
        i = pl.program_id(0)

        @pl.when(i == 0)
        def _():
            carry_ref[...] = jnp.zeros_like(carry_ref)
        logf = jax.nn.log_sigmoid(fl_ref[...] + b_ref[...])
        cb = _dot3_l(_tri(TQ, True), logf) + carry_ref[0:1, :]
        carry_ref[0:1, :] = cb[TQ - 1:TQ, :]
        a = lax.broadcasted_iota(jnp.int32, (128, D_GRP), 0)
        b = lax.broadcasted_iota(jnp.int32, (128, D_GRP), 1) // HEAD
        expand = jnp.where(a == b, 1.0, 0.0).astype(BF16)
        crep = _dot3(cb, expand)
        crep_ref[...] = crep
        ct_ref[...] = crep.T

    return pl.pallas_call(
        body, grid=(nb,),
        in_specs=[_row_spec(TQ, 128), _vec_spec(128)],
        out_specs=(_row_spec(TQ, D_GRP), pl.BlockSpec((None, D_GRP, TQ), lambda i: (i, 0, 0))),
        out_shape=(jax.ShapeDtypeStruct((S, D_GRP), F32), jax.ShapeDtypeStruct((nb, D_GRP, TQ), F32)),
        scratch_shapes=[pltpu.VMEM((8, 128), F32)],
        name=name, compiler_params=_cparams(),
    )(fl_raw, b_pad)


def _lane_masks():
    lane = lax.broadcasted_iota(jnp.int32, (1, 128), 1)
    return lane < HEAD, lane >= HEAD


def _fox_fwd_call(proj, c_rep, c_t, *, name, task=None):
    S = proj.shape[0]
    nq = S // TQ
    scale = HEAD ** -0.5

    def body(q_ref, k_ref, v_ref, c_ref, ct_ref, o_ref, lse_ref):
        i = pl.program_id(1)
        m_lo, m_hi = _lane_masks()
        masks = (m_lo, m_hi)
        q = q_ref[...] * scale
        qm = [jnp.where(mk, q, jnp.zeros_like(q)) for mk in masks]
        cq = c_ref[...]
        cqh = [cq[:, 0:1], cq[:, HEAD:HEAD + 1]]
        row = lax.broadcasted_iota(jnp.int32, (TQ, TQ), 0)
        col = lax.broadcasted_iota(jnp.int32, (TQ, TQ), 1)

        def scores(j):
            start = pl.multiple_of(j * TQ, TQ)
            k = k_ref[pl.ds(start, TQ), :]
            ct = ct_ref[j]
            return tuple(_dot_nt(qm[h], k) + (cqh[h] - ct[HEAD * h:HEAD * h + 1, :]) for h in range(2))

        def update(j, ss, state, masked):
            ms, ls, acc = state
            start = pl.multiple_of(j * TQ, TQ)
            v = v_ref[pl.ds(start, TQ), :]
            new_m, new_l, pv, alpha_l = [], [], [], []
            for h in range(2):
                s = ss[h]
                if masked:
                    s = jnp.where(row >= col, s, NEG)
                mn = jnp.maximum(ms[h], jnp.max(s, axis=1, keepdims=True))
                alpha = jnp.exp(ms[h] - mn)
                p = jnp.exp(s - mn)
                new_l.append(alpha * ls[h] + jnp.sum(p, axis=1, keepdims=True))
                new_m.append(mn)
                alpha_l.append(alpha)
                pv.append(_dot(p.astype(BF16), jnp.where(masks[h], v, jnp.zeros_like(v))))
            alpha_lane = jnp.where(m_lo, alpha_l[0], alpha_l[1])
            acc = acc * alpha_lane + pv[0] + pv[1]
            return (tuple(new_m), tuple(new_l), acc)

        def step(j, carry):
            ss, state = carry
            return (scores(j + 1), update(j, ss, state, False))

        init = ((jnp.full((TQ, 1), NEG, F32),) * 2, (jnp.zeros((TQ, 1), F32),) * 2, jnp.zeros((TQ, 128), F32))
        ss, state = lax.fori_loop(0, i, step, (scores(0), init))
        ms, ls, acc = update(i, ss, state, True)
        l_lane = jnp.where(m_lo, ls[0], ls[1])
        o_ref[...] = acc / l_lane
        lse_ref[...] = jnp.where(m_lo, ms[0] + jnp.log(ls[0]), ms[1] + jnp.log(ls[1])).T

    return _pallas(
        body, grid=(N_PAIR, nq),
        in_specs=[pl.BlockSpec((TQ, 128), lambda p, i: (i, p)),
                  pl.BlockSpec((S, 128), lambda p, i: (0, N_PAIR + p)),
                  pl.BlockSpec((S, 128), lambda p, i: (0, 2 * N_PAIR + p)),
                  pl.BlockSpec((TQ, 128), lambda p, i: (i, p)),
                  pl.BlockSpec((nq, 128, TQ), lambda p, i: (0, p, 0))],
        out_specs=(pl.BlockSpec((TQ, 128), lambda p, i: (i, p)), pl.BlockSpec((None, 128, TQ), lambda p, i: (i, p, 0))),
        out_shape=(jax.ShapeDtypeStruct((S, D_GRP), F32), jax.ShapeDtypeStruct((nq, D_GRP, TQ), F32)),
        name=name, task=task,
    )(proj, proj, proj, c_rep, c_t)


def _fox_bwd_call(proj, do, lse_t, delta_t, c_rep, c_t, *, name, task=None):
    S = proj.shape[0]
    nq = S // TQ
    scale = HEAD ** -0.5

    def body(q_ref, k_ref, v_ref, do_ref, lse_ref, dl_ref, ck_ref, ct_ref,
             dq_ref, dk_ref, dv_ref, dcq_ref, dck_ref, dqa_ref):
        j = pl.program_id(1)
        m_lo, m_hi = _lane_masks()
        masks = (m_lo, m_hi)

        @pl.when(j == 0)
        def _():
            dqa_ref[...] = jnp.zeros_like(dqa_ref)
            dcq_ref[...] = jnp.zeros_like(dcq_ref)
        k = k_ref[...]
        v = v_ref[...]
        km = [jnp.where(mk, k, jnp.zeros_like(k)) for mk in masks]
        ck = ck_ref[...]
        krow = lax.broadcasted_iota(jnp.int32, (TQ, TQ), 0)
        qcol = lax.broadcasted_iota(jnp.int32, (TQ, TQ), 1)

        def probs(i):
            start = pl.multiple_of(i * TQ, TQ)
            q = q_ref[pl.ds(start, TQ), :]
            do = do_ref[pl.ds(start, TQ), :]
            lse = lse_ref[i]
            cq = ct_ref[i]
            out = []
            for h in range(2):
                lo = HEAD * h
                qm = jnp.where(masks[h], q * scale, jnp.zeros_like(q))
                dom = jnp.where(masks[h], do, jnp.zeros_like(do))
                st = _dot_nt(k, qm) + (cq[lo:lo + 1, :] - ck[:, lo:lo + 1])
                out.append((jnp.exp(st - lse[lo:lo + 1, :]), _dot_nt(v, dom)))
            return tuple(out)

        def update(i, pd, carry, masked):
            dk, dv, dck = carry
            start = pl.multiple_of(i * TQ, TQ)
            q = q_ref[pl.ds(start, TQ), :]
            do = do_ref[pl.ds(start, TQ), :]
            dl = dl_ref[i]
            dq = jnp.zeros((TQ, 128), F32)
            new_dck = []
            for h in range(2):
                lo = HEAD * h
                qm = jnp.where(masks[h], q, jnp.zeros_like(q))
                dom = jnp.where(masks[h], do, jnp.zeros_like(do))
                pt, dpt = pd[h]
                if masked:
                    pt = jnp.where(qcol >= krow, pt, 0.0)
                dst = pt * (dpt - dl[lo:lo + 1, :])
                dcq_ref[i, h:h + 1, :] += jnp.sum(dst, axis=0, keepdims=True)
                new_dck.append(dck[h] + jnp.sum(dst, axis=1, keepdims=True))
                dsb = (dst * scale).astype(BF16)
                dv = dv + _dot(pt.astype(BF16), dom)
                dk = dk + _dot(dsb, qm)
                dq = dq + _dot_tn(dsb, km[h])
            dqa_ref[pl.ds(start, TQ), :] += dq
            return (dk, dv, tuple(new_dck))

        def step(i, carry):
            pd, sums = carry
            return (probs(jnp.minimum(i + 1, nq - 1)), update(i, pd, sums, False))

        init = (jnp.zeros((TQ, 128), F32), jnp.zeros((TQ, 128), F32), (jnp.zeros((TQ, 1), F32),) * 2)
        first = probs(j)
        second = probs(jnp.minimum(j + 1, nq - 1))
        _, (dk, dv, dck) = lax.fori_loop(j + 1, nq, step, (second, update(j, first, init, True)))
        dk_ref[...] = dk.astype(BF16)
        dv_ref[...] = dv.astype(BF16)
        dck_ref[...] = -jnp.where(m_lo, dck[0], dck[1])

        @pl.when(j == nq - 1)
        def _():
            dq_ref[...] = dqa_ref[...].astype(BF16)

    res = lambda p, j: (0, p)
    stat = pl.BlockSpec((nq, 128, TQ), lambda p, j: (0, p, 0))
    blk = pl.BlockSpec((TQ, 128), lambda p, j: (j, p))
    return _pallas(
        body, grid=(N_PAIR, nq), task=task,
        in_specs=[pl.BlockSpec((S, 128), res),
                  pl.BlockSpec((TQ, 128), lambda p, j: (j, N_PAIR + p)),
                  pl.BlockSpec((TQ, 128), lambda p, j: (j, 2 * N_PAIR + p)),
                  pl.BlockSpec((S, 128), res), stat, stat, blk, stat],
        out_specs=(pl.BlockSpec((S, 128), res), blk, blk,
                   pl.BlockSpec((None, nq, 8, TQ), lambda p, j: (p, 0, 0, 0)), blk),
        out_shape=(jax.ShapeDtypeStruct((S, D_GRP), BF16), jax.ShapeDtypeStruct((S, D_GRP), BF16),
                   jax.ShapeDtypeStruct((S, D_GRP), BF16), jax.ShapeDtypeStruct((N_PAIR, nq, 8, TQ), F32),
                   jax.ShapeDtypeStruct((S, D_GRP), F32)),
        scratch_shapes=[pltpu.VMEM((S, 128), F32)],
        name=name,
    )(proj, proj, proj, do, lse_t, delta_t, c_rep, c_t)


def _fox_gate_bwd_call(dc_rows, dck, fl_raw, b_pad, *, name):
    S = fl_raw.shape[0]
    nb = S // TQ

    def body(dc_ref, dck_ref, fl_ref, b_ref, dfl_ref, db_ref, carry_ref):
        i = pl.program_id(0)

        @pl.when(i == 0)
        def _():
            carry_ref[...] = jnp.zeros_like(carry_ref)
        lane = lax.broadcasted_iota(jnp.int32, (D_GRP, 128), 0)
        head = lax.broadcasted_iota(jnp.int32, (D_GRP, 128), 1)
        pick = ((lane == head * HEAD) & (head < D_GRP // HEAD)).astype(BF16)
        dc = dc_ref[...] + _dot3(dck_ref[...], pick).T
        rc = _dot3(dc, _tri(TQ, True)) + carry_ref[:, 0:1]
        carry_ref[...] = jnp.broadcast_to(rc[:, 0:1], carry_ref.shape)
        fl = fl_ref[...] + b_ref[...]
        dfl = rc.T * jax.nn.sigmoid(-fl)
        dfl_ref[...] = dfl.astype(BF16)
        _acc_add(db_ref, jnp.sum(dfl, axis=0, keepdims=True), i)

    rev = lambda i: (nb - 1 - i, 0)
    return pl.pallas_call(
        body, grid=(nb,),
        in_specs=[pl.BlockSpec((128, TQ), lambda i: (0, nb - 1 - i)), pl.BlockSpec((TQ, D_GRP), rev),
                  pl.BlockSpec((TQ, 128), rev), _vec_spec(128)],
        out_specs=(pl.BlockSpec((TQ, 128), rev), _acc_spec(128)),
        out_shape=(jax.ShapeDtypeStruct((S, 128), BF16), jax.ShapeDtypeStruct((8, 128), F32)),
        scratch_shapes=[pltpu.VMEM((128, 128), F32)],
        name=name, compiler_params=_cparams(),
    )(dc_rows, dck, fl_raw, b_pad)


def _chk_bias_call(g_rev, *, name):
    def body(g_ref, o_ref):
        x = jnp.broadcast_to(g_ref[...], (TQ, ROLL_W))
        rolled = pltpu.roll(x, ROLL_W - (TQ - 1), 1, stride=1, stride_axis=0)
        qc = lax.broadcasted_iota(jnp.int32, (TQ, WIN), 0) // CHUNK
        kc = lax.broadcasted_iota(jnp.int32, (TQ, WIN), 1) // CHUNK
        band = (kc >= qc) & (kc <= qc + LEFT)
        o_ref[...] = jnp.where(band, rolled[:, 0:WIN], NEG)

    return pl.pallas_call(
        body, grid=(8,),
        in_specs=[pl.BlockSpec((None, 1, ROLL_W), lambda h: (h, 0, 0))],
        out_specs=pl.BlockSpec((None, TQ, WIN), lambda h: (h, 0, 0)),
        out_shape=jax.ShapeDtypeStruct((8, TQ, WIN), F32), name=name, compiler_params=_cparams(),
    )(g_rev.reshape(8, 1, ROLL_W))


def _chk_scores(i, qm, kwin, bias, scale):
    s = _dot_nt(qm * scale, kwin) + bias
    kc = lax.broadcasted_iota(jnp.int32, (TQ, WIN), 1) // CHUNK
    return jnp.where(kc + i * (TQ // CHUNK) >= LEFT, s, NEG)


def _chk_fwd_call(proj, bias, *, name, task=None):
    S = proj.shape[0]
    nq = S // TQ
    scale = HEAD ** -0.5

    def body(q_ref, k_ref, v_ref, b_ref, o_ref, kp_ref, vp_ref):
        i = pl.program_id(1)

        @pl.when(i == 0)
        def _():
            kp_ref[0:PADK, :] = jnp.zeros((PADK, 128), BF16)
            vp_ref[0:PADK, :] = jnp.zeros((PADK, 128), BF16)
            kp_ref[PADK:PADK + S, :] = k_ref[...]
            vp_ref[PADK:PADK + S, :] = v_ref[...]
        masks = _lane_masks()
        q = q_ref[...]
        start = pl.multiple_of(i * TQ, TQ)
        kwin = kp_ref[pl.ds(start, WIN), :]
        vwin = vp_ref[pl.ds(start, WIN), :]
        ss = [_chk_scores(i, jnp.where(masks[h], q, jnp.zeros_like(q)), kwin, b_ref[h], scale) for h in range(2)]
        ps = []
        for s in ss:
            p = jnp.exp(s - jnp.max(s, axis=1, keepdims=True))
            ps.append((p / jnp.sum(p, axis=1, keepdims=True)).astype(BF16))
        o_ref[...] = (_dot(ps[0], jnp.where(masks[0], vwin, jnp.zeros_like(vwin)))
                      + _dot(ps[1], jnp.where(masks[1], vwin, jnp.zeros_like(vwin))))

    c0 = 3 * N_PAIR
    return _pallas(
        body, grid=(N_PAIR, nq), task=task,
        in_specs=[pl.BlockSpec((TQ, 128), lambda p, i: (i, c0 + p)),
                  pl.BlockSpec((S, 128), lambda p, i: (0, c0 + N_PAIR + p)),
                  pl.BlockSpec((S, 128), lambda p, i: (0, c0 + 2 * N_PAIR + p)),
                  pl.BlockSpec((2, TQ, WIN), lambda p, i: (p, 0, 0))],
        out_specs=pl.BlockSpec((TQ, 128), lambda p, i: (i, p)),
        out_shape=jax.ShapeDtypeStruct((S, D_GRP), F32),
        scratch_shapes=[pltpu.VMEM((S + PADK, 128), BF16), pltpu.VMEM((S + PADK, 128), BF16)],
        name=name,
    )(proj, proj, proj, bias)


def _chk_bwd_call(proj, do, bias, *, name, task=None):
    S = proj.shape[0]
    nq = S // TQ
    scale = HEAD ** -0.5

    def body(q_ref, k_ref, v_ref, do_ref, b_ref, dq_ref, dk_ref, dv_ref, dg_ref, kp_ref, vp_ref, dkp_ref, dvp_ref, db_ref):
        i = pl.program_id(1)

        @pl.when(i == 0)
        def _():
            kp_ref[0:PADK, :] = jnp.zeros((PADK, 128), BF16)
            vp_ref[0:PADK, :] = jnp.zeros((PADK, 128), BF16)
            kp_ref[PADK:PADK + S, :] = k_ref[...]
            vp_ref[PADK:PADK + S, :] = v_ref[...]
            dkp_ref[...] = jnp.zeros_like(dkp_ref)
            dvp_ref[...] = jnp.zeros_like(dvp_ref)
            db_ref[...] = jnp.zeros_like(db_ref)
        masks = _lane_masks()
        q = q_ref[...]
        dout = do_ref[...]
        start = pl.multiple_of(i * TQ, TQ)
        kwin = kp_ref[pl.ds(start, WIN), :]
        vwin = vp_ref[pl.ds(start, WIN), :]
        qm = [jnp.where(mk, q, jnp.zeros_like(q)) for mk in masks]
        dom = [jnp.where(mk, dout, jnp.zeros_like(dout)) for mk in masks]
        ss = [_chk_scores(i, qm[h], kwin, b_ref[h], scale) for h in range(2)]
        dps = [_dot_nt(dom[h], vwin) for h in range(2)]
        pbs, dsbs = [], []
        for h in range(2):
            p = jnp.exp(ss[h] - jnp.max(ss[h], axis=1, keepdims=True))
            p = p / jnp.sum(p, axis=1, keepdims=True)
            ds = p * (dps[h] - jnp.sum(p * dps[h], axis=1, keepdims=True))
            db_ref[h] += ds
            pbs.append(p.astype(BF16))
            dsbs.append((ds * scale).astype(BF16))
        dq_ref[...] = (_dot(dsbs[0], jnp.where(masks[0], kwin, jnp.zeros_like(kwin)))
                       + _dot(dsbs[1], jnp.where(masks[1], kwin, jnp.zeros_like(kwin)))).astype(BF16)
        dkp_ref[pl.ds(start, WIN), :] += _dot_tn(dsbs[0], qm[0]) + _dot_tn(dsbs[1], qm[1])
        dvp_ref[pl.ds(start, WIN), :] += _dot_tn(pbs[0], dom[0]) + _dot_tn(pbs[1], dom[1])

        @pl.when(i == nq - 1)
        def _():
            dk_ref[...] = dkp_ref[PADK:PADK + S, :].astype(BF16)
            dv_ref[...] = dvp_ref[PADK:PADK + S, :].astype(BF16)
            a = lax.broadcasted_iota(jnp.int32, (TQ, TQ), 0)
            b = lax.broadcasted_iota(jnp.int32, (TQ, TQ), 1)
            flip = jnp.where(a + b == TQ - 1, 1.0, 0.0).astype(BF16)
            e = lax.broadcasted_iota(jnp.int32, (1, ROLL_W), 1)
            dg_ref[...] = jnp.zeros_like(dg_ref)
            for h in range(2):
                rev = _dot3_l(flip, db_ref[h])
                wide = jnp.concatenate([rev, jnp.zeros((TQ, ROLL_W - WIN), F32)], axis=1)
                diag = pltpu.roll(wide, 0, 1, stride=1, stride_axis=0)
                dg = jnp.sum(diag, axis=0, keepdims=True)
                lo = jnp.sum(jnp.where(e <= 639, dg, 0.0), axis=1, keepdims=True)
                hi = jnp.sum(jnp.where(e >= 895, dg, 0.0), axis=1, keepdims=True)
                dg_ref[h:h + 1, :] = jnp.where(e == 639, lo, jnp.where(e == 895, hi, dg))

    c0 = 3 * N_PAIR
    res = lambda p, i: (0, p)
    return _pallas(
        body, grid=(N_PAIR, nq), task=task,
        in_specs=[pl.BlockSpec((TQ, 128), lambda p, i: (i, c0 + p)),
                  pl.BlockSpec((S, 128), lambda p, i: (0, c0 + N_PAIR + p)),
                  pl.BlockSpec((S, 128), lambda p, i: (0, c0 + 2 * N_PAIR + p)),
                  pl.BlockSpec((TQ, 128), lambda p, i: (i, p)),
                  pl.BlockSpec((2, TQ, WIN), lambda p, i: (p, 0, 0))],
        out_specs=(pl.BlockSpec((TQ, 128), lambda p, i: (i, p)), pl.BlockSpec((S, 128), res),
                   pl.BlockSpec((S, 128), res), pl.BlockSpec((None, 8, ROLL_W), lambda p, i: (p, 0, 0))),
        out_shape=(jax.ShapeDtypeStruct((S, D_GRP), BF16), jax.ShapeDtypeStruct((S, D_GRP), BF16),
                   jax.ShapeDtypeStruct((S, D_GRP), BF16), jax.ShapeDtypeStruct((N_PAIR, 8, ROLL_W), F32)),
        scratch_shapes=[pltpu.VMEM((S + PADK, 128), BF16), pltpu.VMEM((S + PADK, 128), BF16),
                        pltpu.VMEM((S + PADK, 128), F32), pltpu.VMEM((S + PADK, 128), F32),
                        pltpu.VMEM((2, TQ, WIN), F32)],
        name=name,
    )(proj, proj, proj, do, bias)


def _mem_fwd_call(q, k, v, *, name, tq=2048):
    S = q.shape[0]
    scale = MEM_HD ** -0.5

    def body(q_ref, k_ref, v_ref, o_ref):
        s = _dot_nt(q_ref[...] * scale, k_ref[...])
        p = jnp.exp(s - jnp.max(s, axis=1, keepdims=True))
        p = p / jnp.sum(p, axis=1, keepdims=True)
        o_ref[...] = _dot(p.astype(BF16), v_ref[...]).astype(BF16)

    return pl.pallas_call(
        body, grid=(MEM_HEADS, S // tq),
        in_specs=[pl.BlockSpec((tq, MEM_HD), lambda h, i: (i, h)),
                  pl.BlockSpec((N_MEM, MEM_HD), lambda h, i: (0, h)),
                  pl.BlockSpec((N_MEM, MEM_HD), lambda h, i: (0, h))],
        out_specs=pl.BlockSpec((tq, MEM_HD), lambda h, i: (i, h)),
        out_shape=jax.ShapeDtypeStruct((S, D), BF16), name=name, compiler_params=_cparams(),
    )(q, k, v)


def _mem_bwd_call(q, k, v, do, *, name, tq=2048):
    S = q.shape[0]
    n = S // tq
    scale = MEM_HD ** -0.5

    def body(q_ref, k_ref, v_ref, do_ref, dq_ref, dk_ref, dv_ref, dka_ref, dva_ref):
        i = pl.program_id(1)

        @pl.when(i == 0)
        def _():
            dka_ref[...] = jnp.zeros_like(dka_ref)
            dva_ref[...] = jnp.zeros_like(dva_ref)
        qb = q_ref[...]
        kb = k_ref[...]
        dob = do_ref[...]
        s = _dot_nt(qb * scale, kb)
        p = jnp.exp(s - jnp.max(s, axis=1, keepdims=True))
        p = p / jnp.sum(p, axis=1, keepdims=True)
        dp = _dot_nt(dob, v_ref[...])
        ds = p * (dp - jnp.sum(p * dp, axis=1, keepdims=True))
        dsb = (ds * scale).astype(BF16)
        dq_ref[...] = _dot(dsb, kb).astype(BF16)
        dka_ref[...] += _dot_tn(dsb, qb)
        dva_ref[...] += _dot_tn(p.astype(BF16), dob)

        @pl.when(i == n - 1)
        def _():
            dk_ref[...] = dka_ref[...].astype(BF16)
            dv_ref[...] = dva_ref[...].astype(BF16)

    kv = pl.BlockSpec((N_MEM, MEM_HD), lambda h, i: (0, h))
    qs = pl.BlockSpec((tq, MEM_HD), lambda h, i: (i, h))
    return pl.pallas_call(
        body, grid=(MEM_HEADS, n), in_specs=[qs, kv, kv, qs], out_specs=(qs, kv, kv),
        out_shape=(jax.ShapeDtypeStruct((S, D), BF16), jax.ShapeDtypeStruct((N_MEM, D), BF16),
                   jax.ShapeDtypeStruct((N_MEM, D), BF16)),
        scratch_shapes=[pltpu.VMEM((N_MEM, MEM_HD), F32), pltpu.VMEM((N_MEM, MEM_HD), F32)],
        name=name, compiler_params=_cparams(),
    )(q, k, v, do)


def _rel_table_to_g(rel):
    return jnp.concatenate([
        jnp.broadcast_to(rel[:, N_REL - 1:N_REL], (8, 640)),
        rel[:, 1:N_REL - 1][:, ::-1],
        jnp.broadcast_to(rel[:, 0:1], (8, 129)),
    ], axis=1)


def _g_to_rel_table(dg):
    return dg[:, 639:896][:, ::-1]


def _place():
    x, y, c = lax.axis_index("x"), lax.axis_index("y"), lax.axis_index("c")
    others = [(1 - x, y), (x, 1 - y), (1 - x, 1 - y)]
    return x, y, c, others


def _half(c, rows):
    hr = rows // 2
    return pl.ds(pl.multiple_of(c * hr, 16), hr)


def _dma_sems(*shape):
    return pltpu.SemaphoreType.DMA(shape)


def _cast_slabs_call(ws, chip_arr, *, name, tm=256, task=None):
    n = len(ws)
    cols = ws[0].shape[1]
    tiles = [w.shape[0] // tm for w in ws]
    steps = max(tiles)

    def body(chip_ref, *refs):
        i = pl.program_id(0)
        for k in range(n):
            def cast(k=k):
                refs[n + k][...] = refs[k][...].astype(BF16)
            if tiles[k] == steps:
                cast()
            else:
                pl.when(i < tiles[k])(cast)

    in_specs = [pl.BlockSpec((tm, cols), lambda i, chip, t=t: (jnp.minimum(i, t - 1), 0)) for t in tiles]
    out_specs = [pl.BlockSpec((None, tm, cols), lambda i, chip, t=t: (chip[0], jnp.minimum(i, t - 1), 0)) for t in tiles]
    out_shape = [jax.ShapeDtypeStruct((N_CHIP,) + w.shape, BF16) for w in ws]
    return _pallas(body, grid=(steps,), in_specs=in_specs, out_specs=out_specs, out_shape=out_shape, name=name,
                   task=task, prefetch=1)(chip_arr, *ws)


def _cast_slab_call(w, chip_arr, *, name, tm=256, pad_rows=0):
    rows, cols = w.shape
    if pad_rows:
        tm = rows
    tm = min(tm, rows)

    def body(chip_ref, w_ref, o_ref):
        o_ref[0:tm, :] = w_ref[...].astype(BF16)
        if pad_rows:
            o_ref[tm:tm + pad_rows, :] = jnp.zeros((pad_rows, cols), BF16)

    return pl.pallas_call(
        body,
        grid_spec=pltpu.PrefetchScalarGridSpec(
            num_scalar_prefetch=1, grid=(rows // tm,),
            in_specs=[pl.BlockSpec((tm, cols), lambda i, chip: (i, 0))],
            out_specs=pl.BlockSpec((None, tm + pad_rows, cols), lambda i, chip: (chip[0], i, 0))),
        out_shape=jax.ShapeDtypeStruct((N_CHIP, rows + pad_rows, cols), BF16), name=name,
        compiler_params=_cparams(),
    )(chip_arr, w)


def _ag_ici_task(gathered):
    n = len(gathered)

    def copies(ins, outs, sems):
        send_sems, recv_sems = sems
        x, y, c, others = _place()
        me = 2 * x + y
        for k in range(n):
            mine = _half(c, gathered[k].shape[1])
            for t, (ox, oy) in enumerate(others):
                yield [pltpu.make_async_remote_copy(
                    src_ref=ins[k].at[me, mine], dst_ref=outs[k].at[slab, mine],
                    send_sem=send_sems.at[k, t], recv_sem=recv_sems.at[k, t],
                    device_id=(ox, oy, c), device_id_type=MESH) for slab in (me, 2 * ox + oy)]

    def issue(ins, outs, sems):
        for outgoing, _ in copies(ins, outs, sems):
            outgoing.start()

    def drain(ins, outs, sems):
        for outgoing, incoming in copies(ins, outs, sems):
            incoming.wait_recv()
            outgoing.wait_send()

    return _Task(gathered, [jax.ShapeDtypeStruct(g.shape, g.dtype) for g in gathered],
                 [_dma_sems(n, 3), _dma_sems(n, 3)], issue, drain, aliases={k: k for k in range(n)})


def _ag_d2d_task(gathered):
    n = len(gathered)

    def copies(ins, outs, sems):
        send_sems, recv_sems = sems
        x, y, c, others = _place()
        for k in range(n):
            rows = gathered[k].shape[1]
            mine, theirs = _half(c, rows), _half(1 - c, rows)
            for t, (ox, oy) in enumerate(others):
                slab = 2 * ox + oy
                pair = [pltpu.make_async_remote_copy(
                    src_ref=ins[k].at[slab, half], dst_ref=outs[k].at[slab, half],
                    send_sem=send_sems.at[k, t], recv_sem=recv_sems.at[k, t],
                    device_id=(x, y, 1 - c), device_id_type=MESH) for half in (mine, theirs)]
                yield pair

    def issue(ins, outs, sems):
        for outgoing, _ in copies(ins, outs, sems):
            outgoing.start()

    def drain(ins, outs, sems):
        for outgoing, incoming in copies(ins, outs, sems):
            incoming.wait_recv()
            outgoing.wait_send()

    return _Task(gathered, [jax.ShapeDtypeStruct(g.shape, g.dtype) for g in gathered],
                 [_dma_sems(n, 3), _dma_sems(n, 3)], issue, drain, aliases={k: k for k in range(n)})


def _rs_pair_task(ds):
    n = len(ds)

    def copies(ins, outs, sems):
        send_sems, recv_sems = sems
        x, y, c, _ = _place()
        for k in range(n):
            yield pltpu.make_async_remote_copy(
                src_ref=ins[k].at[:, _half(1 - c, ds[k].shape[1])], dst_ref=outs[k],
                send_sem=send_sems.at[k], recv_sem=recv_sems.at[k],
                device_id=(x, y, 1 - c), device_id_type=MESH)

    def issue(ins, outs, sems):
        for cp in copies(ins, outs, sems):
            cp.start()

    def drain(ins, outs, sems):
        for cp in copies(ins, outs, sems):
            cp.wait()

    return _Task(ds, [jax.ShapeDtypeStruct((N_CHIP, d.shape[1] // 2, d.shape[2]), d.dtype) for d in ds],
                 [_dma_sems(n), _dma_sems(n)], issue, drain)


def _pair_add_call(d, r1, c_arr, *, name, tm=512):
    _, rows, cols = d.shape
    hr = rows // 2
    tm = tm if hr % tm == 0 else hr
    nb = hr // tm

    def body(c_ref, d_ref, r_ref, o_ref):
        o_ref[...] = (d_ref[...].astype(F32) + r_ref[...].astype(F32)).astype(BF16)

    return pl.pallas_call(
        body,
        grid_spec=pltpu.PrefetchScalarGridSpec(
            num_scalar_prefetch=1, grid=(N_CHIP, nb),
            in_specs=[pl.BlockSpec((None, tm, cols), lambda j, i, c: (j, c[0] * nb + i, 0)),
                      pl.BlockSpec((None, tm, cols), lambda j, i, c: (j, i, 0))],
            out_specs=pl.BlockSpec((None, tm, cols), lambda j, i, c: (j, i, 0))),
        out_shape=jax.ShapeDtypeStruct((N_CHIP, hr, cols), BF16), name=name, compiler_params=_cparams(),
    )(c_arr, d, r1)


def _rs_chip_task(ps):
    n = len(ps)

    def copies(ins, outs, sems):
        send_sems, recv_sems = sems
        x, y, c, others = _place()
        for k in range(n):
            for t, (ox, oy) in enumerate(others):
                yield pltpu.make_async_remote_copy(
                    src_ref=ins[k].at[2 * ox + oy], dst_ref=outs[k].at[t],
                    send_sem=send_sems.at[k, t], recv_sem=recv_sems.at[k, t],
                    device_id=(ox, oy, c), device_id_type=MESH)

    def issue(ins, outs, sems):
        for cp in copies(ins, outs, sems):
            cp.start()

    def drain(ins, outs, sems):
        for cp in copies(ins, outs, sems):
            cp.wait()

    return _Task(ps, [jax.ShapeDtypeStruct((3,) + p.shape[1:], p.dtype) for p in ps],
                 [_dma_sems(n, 3), _dma_sems(n, 3)], issue, drain)


def _chip_sum_call(p, r2, place_arr, *, name, tm=512):
    _, hr, cols = r2.shape
    tm = tm if hr % tm == 0 else hr
    nb = hr // tm

    def body(place_ref, p_ref, r_ref, o_ref):
        acc = p_ref[...].astype(F32)
        for j in range(3):
            acc = acc + r_ref[j].astype(F32)
        o_ref[...] = acc

    return pl.pallas_call(
        body,
        grid_spec=pltpu.PrefetchScalarGridSpec(
            num_scalar_prefetch=1, grid=(nb,),
            in_specs=[pl.BlockSpec((None, tm, cols), lambda i, pc: (pc[0], i, 0)),
                      pl.BlockSpec((3, tm, cols), lambda i, pc: (0, i, 0))],
            out_specs=pl.BlockSpec((tm, cols), lambda i, pc: (pc[1] * nb + i, 0))),
        out_shape=jax.ShapeDtypeStruct((2 * hr, cols), F32), name=name, compiler_params=_cparams(),
    )(place_arr, p, r2)


def _rs_gather_task(gs):
    n = len(gs)

    def copies(ins, outs, sems):
        send_sems, recv_sems = sems
        x, y, c, _ = _place()
        for k in range(n):
            rows = gs[k].shape[0]
            mine, theirs = _half(c, rows), _half(1 - c, rows)
            yield [pltpu.make_async_remote_copy(
                src_ref=ins[k].at[mine], dst_ref=outs[k].at[half],
                send_sem=send_sems.at[k], recv_sem=recv_sems.at[k],
                device_id=(x, y, 1 - c), device_id_type=MESH) for half in (mine, theirs)]

    def issue(ins, outs, sems):
        for outgoing, _ in copies(ins, outs, sems):
            outgoing.start()

    def drain(ins, outs, sems):
        for outgoing, incoming in copies(ins, outs, sems):
            incoming.wait_recv()
            outgoing.wait_send()

    return _Task(gs, [jax.ShapeDtypeStruct(g.shape, g.dtype) for g in gs],
                 [_dma_sems(n), _dma_sems(n)], issue, drain, aliases={k: k for k in range(n)})


def _adamw(w, g, m, v):
    m = ADAM_B1 * m + (1.0 - ADAM_B1) * g
    v = ADAM_B2 * v + (1.0 - ADAM_B2) * jnp.square(g)
    m_hat = m / (1.0 - ADAM_B1 ** ADAM_STEP)
    v_hat = v / (1.0 - ADAM_B2 ** ADAM_STEP)
    delta = -ADAM_LR * (m_hat / (jnp.sqrt(v_hat) + ADAM_EPS) + ADAM_WD * w)
    return delta, m, v


def _adamw_call(items, *, name, tm=256, task=None):
    n = len(items)
    cols = items[0][0].shape[1]
    tiles = [it[0].shape[0] // tm for it in items]
    steps = max(tiles)

    def body(*refs):
        i = pl.program_id(0)
        ins, outs = refs[:4 * n], refs[4 * n:]
        for k in range(n):
            def update(k=k):
                g = ins[4 * k + 1][...]
                res = _adamw(ins[4 * k][...], g, ins[4 * k + 2][...], ins[4 * k + 3][...])
                outs[4 * k][...] = g
                for j in range(3):
                    outs[4 * k + 1 + j][...] = res[j]
            if tiles[k] == steps:
                update()
            else:
                pl.when(i < tiles[k])(update)

    in_specs, out_specs, out_shape, args = [], [], [], []
    for it, t in zip(items, tiles):
        spec = pl.BlockSpec((tm, cols), lambda i, t=t: (jnp.minimum(i, t - 1), 0))
        in_specs += [spec] * 4
        out_specs += [spec] * 4
        out_shape += [jax.ShapeDtypeStruct(it[0].shape, F32)] * 4
        args += list(it)
    res = _pallas(body, grid=(steps,), in_specs=in_specs, out_specs=out_specs, out_shape=out_shape,
                  name=name, task=task)(*args)
    outs, extra = res if task is not None else (res, None)
    grouped = [tuple(outs[4 * k:4 * k + 4]) for k in range(n)]
    return (grouped, extra) if task is not None else grouped


def _adamw_cols_call(w, g_pad, m, v, *, name, tn=512):
    rows, cols = w.shape

    def body(w_ref, g_ref, m_ref, v_ref, go_ref, d_ref, mo_ref, vo_ref):
        g = g_ref[0:rows, :]
        d, mn, vn = _adamw(w_ref[...], g, m_ref[...], v_ref[...])
        go_ref[...] = g
        d_ref[...] = d
        mo_ref[...] = mn
        vo_ref[...] = vn

    spec = pl.BlockSpec((rows, tn), lambda j: (0, j))
    gspec = pl.BlockSpec((g_pad.shape[0], tn), lambda j: (0, j))
    return _pallas(body, grid=(cols // tn,), in_specs=[spec, gspec, spec, spec], out_specs=(spec,) * 4,
                   out_shape=(jax.ShapeDtypeStruct((rows, cols), F32),) * 4, name=name)(w, g_pad, m, v)


N_DEV = 8
SMALL_ROWS = 24
SMALL_LAYOUT = {
    "g_mix_pre": (0, 0, 1, D), "g_mix_post": (1, 0, 1, D), "g_mem_kv": (2, 0, 1, D), "g_mem_pre": (3, 0, 1, D),
    "g_mem_post": (4, 0, 1, D), "g_ff_pre": (5, 0, 1, D), "g_ff_post": (6, 0, 1, D),
    "g_fox_out": (7, 0, 1, D_GRP), "g_chk_out": (7, D_GRP, 1, D_GRP), "b_fgt": (8, 0, 1, 8),
    "rel_bias": (16, 0, 8, N_REL),
}
SMALL = list(SMALL_LAYOUT)


LOSS_ROW = 9


def _small_reduce_call(grads, loss_blk, task, *, name):
    n = len(SMALL)
    t_in, t_out = len(task.arrays), len(task.out_shapes)

    def body(*refs):
        g_refs, loss_ref, tins = refs[:n], refs[n], refs[n + 1:n + 1 + t_in]
        p = n + 1 + t_in
        total_ref, loss_out, touts = refs[p], refs[p + 1], refs[p + 2:p + 2 + t_out]
        p += 2 + t_out
        mine, slots, send_sems, recv_sems = refs[p:p + 4]
        tsems = refs[p + 4:]
        task.issue(tins, touts, tsems)
        x, y, c, _ = _place()
        me = 4 * x + 2 * y + c
        mine[...] = jnp.zeros_like(mine)
        for k, name_k in enumerate(SMALL):
            r, l, nr, nl = SMALL_LAYOUT[name_k]
            mine[r:r + nr, l:l + nl] = g_refs[k][0:nr, 0:nl]
        mine[LOSS_ROW:LOSS_ROW + 1, 0:128] = loss_ref[0:1, :]
        slots[me] = mine[...]
        peers = [(dx, dy, dc) for dx in (0, 1) for dy in (0, 1) for dc in (0, 1)][1:]
        cps = []
        for t, (dx, dy, dc) in enumerate(peers):
            px, py, pc = (x + dx) % 2, (y + dy) % 2, (c + dc) % 2
            cps.append(pltpu.make_async_remote_copy(
                src_ref=mine, dst_ref=slots.at[me], send_sem=send_sems.at[t], recv_sem=recv_sems.at[t],
                device_id=(px, py, pc), device_id_type=MESH))
            cps[-1].start()
        for t, (dx, dy, dc) in enumerate(peers):
            px, py, pc = (x + dx) % 2, (y + dy) % 2, (c + dc) % 2
            pltpu.make_async_remote_copy(
                src_ref=mine, dst_ref=slots.at[4 * px + 2 * py + pc], send_sem=send_sems.at[t],
                recv_sem=recv_sems.at[t], device_id=(px, py, pc), device_id_type=MESH).wait_recv()
        for cp in cps:
            cp.wait_send()
        total = slots[0]
        for j in range(1, N_DEV):
            total = total + slots[j]
        total_ref[...] = total
        loss_out[...] = jnp.broadcast_to(total[LOSS_ROW:LOSS_ROW + 1, 0:128], loss_out.shape)
        task.drain(tins, touts, tsems)

    vm = pl.BlockSpec(memory_space=pltpu.VMEM)
    out_shape = [jax.ShapeDtypeStruct((SMALL_ROWS, D), F32), jax.ShapeDtypeStruct((8, 128), F32)] + list(task.out_shapes)
    res = pl.pallas_call(
        body, in_specs=[vm] * (n + 1) + [ANY] * t_in, out_specs=[vm] * 2 + [ANY] * t_out,
        out_shape=out_shape,
        scratch_shapes=[pltpu.VMEM((SMALL_ROWS, D), F32), pltpu.VMEM((N_DEV, SMALL_ROWS, D), F32),
                        _dma_sems(N_DEV - 1), _dma_sems(N_DEV - 1)] + list(task.sems),
        input_output_aliases={n + 1 + i: 2 + j for i, j in task.aliases.items()},
        name=name,
    )(*[grads[k] for k in SMALL], loss_blk, *task.arrays)
    return res[0], res[1], list(res[2:])


def _small_adamw_call(total, ws, ms, vs, *, name):
    n = len(SMALL)

    def body(*refs):
        total_ref = refs[0]
        w_refs, m_refs, v_refs = (refs[1 + j * n:1 + (j + 1) * n] for j in range(3))
        outs = refs[1 + 3 * n:]
        for k, name_k in enumerate(SMALL):
            r, l, nr, nl = SMALL_LAYOUT[name_k]
            g = total_ref[r:r + nr, l:l + nl]
            d, mn, vn = _adamw(w_refs[k][...], g, m_refs[k][...], v_refs[k][...])
            for j, val in enumerate((g, d, mn, vn)):
                outs[4 * k + j][...] = val

    vm = pl.BlockSpec(memory_space=pltpu.VMEM)
    res = pl.pallas_call(
        body, in_specs=[vm] * (3 * n + 1), out_specs=[vm] * (4 * n),
        out_shape=[jax.ShapeDtypeStruct(ws[k].shape, F32) for k in SMALL for _ in range(4)], name=name,
    )(total, *[d[k] for d in (ws, ms, vs) for k in SMALL])
    return {k: tuple(res[4 * i:4 * i + 4]) for i, k in enumerate(SMALL)}


WEIGHTS = ["w_in", "b_fgt", "rel_bias", "g_fox_out", "g_chk_out", "w_out", "g_mix_pre", "g_mix_post", "g_mem_kv",
           "w_mq", "w_mk", "w_mv", "w_mo", "g_mem_pre", "g_mem_post", "w_ff1", "w_ff2", "g_ff_pre", "g_ff_post"]
BIG = ["w_in", "w_out", "w_mq", "w_mk", "w_mv", "w_mo", "w_ff1", "w_ff2"]


IN_SHARD = D_IN // N_CHIP
IN_PAD = 800


IN_PIECES = [(0, 0, 770), (800, 770, 766), (1566, 3072, 4), (1600, 3076, 4), (1604, 1536, 766), (2400, 2302, 770)]
PAD_ZEROS = [(800 * j + IN_SHARD, IN_PAD - IN_SHARD) for j in range(N_CHIP)]
ALL_ZEROS = [(D_IN, D_ALL - D_IN)]


def _reorder_rows_call(src, to_all, *, name, tn=512):
    rows, cols = src.shape
    zeros = ALL_ZEROS if to_all else PAD_ZEROS

    def body(s_ref, o_ref):
        for pad0, all0, cnt in IN_PIECES:
            s0, d0 = (pad0, all0) if to_all else (all0, pad0)
            o_ref[d0:d0 + cnt, :] = s_ref[s0:s0 + cnt, :]
        for z0, cnt in zeros:
            o_ref[z0:z0 + cnt, :] = jnp.zeros((cnt, tn), src.dtype)

    spec = pl.BlockSpec((rows, tn), lambda j: (0, j))
    return _pallas(body, grid=(cols // tn,), in_specs=[spec], out_specs=spec,
                   out_shape=jax.ShapeDtypeStruct((rows, cols), src.dtype), name=name)(src)


def kernel(x, mem, w_in, b_fgt, rel_bias, g_fox_out, g_chk_out, w_out, g_mix_pre, g_mix_post, g_mem_kv, w_mq, w_mk, w_mv, w_mo, g_mem_pre, g_mem_post, w_ff1, w_ff2, g_ff_pre, g_ff_post, loss_target, m_w_in, m_b_fgt, m_rel_bias, m_g_fox_out, m_g_chk_out, m_w_out, m_g_mix_pre, m_g_mix_post, m_g_mem_kv, m_w_mq, m_w_mk, m_w_mv, m_w_mo, m_g_mem_pre, m_g_mem_post, m_w_ff1, m_w_ff2, m_g_ff_pre, m_g_ff_post, v_w_in, v_b_fgt, v_rel_bias, v_g_fox_out, v_g_chk_out, v_w_out, v_g_mix_pre, v_g_mix_post, v_g_mem_kv, v_w_mq, v_w_mk, v_w_mv, v_w_mo, v_g_mem_pre, v_g_mem_post, v_w_ff1, v_w_ff2, v_g_ff_pre, v_g_ff_post):
    w = dict(w_in=w_in, b_fgt=b_fgt, rel_bias=rel_bias, g_fox_out=g_fox_out, g_chk_out=g_chk_out, w_out=w_out,
             g_mix_pre=g_mix_pre, g_mix_post=g_mix_post, g_mem_kv=g_mem_kv, w_mq=w_mq, w_mk=w_mk, w_mv=w_mv,
             w_mo=w_mo, g_mem_pre=g_mem_pre, g_mem_post=g_mem_post, w_ff1=w_ff1, w_ff2=w_ff2, g_ff_pre=g_ff_pre,
             g_ff_post=g_ff_post)
    m = dict(w_in=m_w_in, b_fgt=m_b_fgt, rel_bias=m_rel_bias, g_fox_out=m_g_fox_out, g_chk_out=m_g_chk_out,
             w_out=m_w_out, g_mix_pre=m_g_mix_pre, g_mix_post=m_g_mix_post, g_mem_kv=m_g_mem_kv, w_mq=m_w_mq,
             w_mk=m_w_mk, w_mv=m_w_mv, w_mo=m_w_mo, g_mem_pre=m_g_mem_pre, g_mem_post=m_g_mem_post,
             w_ff1=m_w_ff1, w_ff2=m_w_ff2, g_ff_pre=m_g_ff_pre, g_ff_post=m_g_ff_post)
    v = dict(w_in=v_w_in, b_fgt=v_b_fgt, rel_bias=v_rel_bias, g_fox_out=v_g_fox_out, g_chk_out=v_g_chk_out,
             w_out=v_w_out, g_mix_pre=v_g_mix_pre, g_mix_post=v_g_mix_post, g_mem_kv=v_g_mem_kv, w_mq=v_w_mq,
             w_mk=v_w_mk, w_mv=v_w_mv, w_mo=v_w_mo, g_mem_pre=v_g_mem_pre, g_mem_post=v_g_mem_post,
             w_ff1=v_w_ff1, w_ff2=v_w_ff2, g_ff_pre=v_g_ff_pre, g_ff_post=v_g_ff_post)

    def rows(d, k):
        return d[k][0] if k == "rel_bias" else d[k]

    xs, mems, target = x[0], mem[0], loss_target[0]
    S = xs.shape[0]
    sp = {k: rows(w, k) for k in SMALL}
    b_pad = jnp.pad(sp["b_fgt"], ((0, 0), (0, 120)))
    chip = 2 * lax.axis_index("x") + lax.axis_index("y")
    chip_arr = jnp.reshape(chip, (1,)).astype(jnp.int32)
    c_arr = jnp.reshape(lax.axis_index("c"), (1,)).astype(jnp.int32)
    place_arr = jnp.concatenate([chip_arr, c_arr])
    w_in_t, m_in_t, v_in_t = w["w_in"][0].T, m["w_in"][0].T, v["w_in"][0].T
    slab = {"w_in": _cast_slab_call(w_in_t, chip_arr, name="cast_w_in", pad_rows=IN_PAD - IN_SHARD)}

    def gather_ici(names):
        return _ag_ici_task([slab[k] for k in names])

    def pair_add(k, d, r1):
        return _pair_add_call(d, r1, c_arr, name="rs_pair_add_" + k)

    rest, (g_in,) = _cast_slabs_call([w[k][0] for k in BIG[1:]], chip_arr, name="cast_rest",
                                     task=gather_ici(["w_in"]))
    slab.update(zip(BIG[1:], rest))
    h1, (g_in,) = _rms_fwd_call(xs, sp["g_mix_pre"], name="rms_mix_pre", task=_ag_d2d_task([g_in]))
    w_all_t = _reorder_rows_call(g_in.reshape(N_CHIP * IN_PAD, D), True, name="w_in_rows")
    proj, (g_out, g_mq) = _mm_nt(h1, w_all_t, "plain", rows=(0, 3072), name="mm_proj",
                                 task=gather_ici(["w_out", "w_mq"]))
    fl_raw = _mm_nt(h1, w_all_t, "plain", rows=(3072, 128), name="mm_gate", out_dtype=F32, tn=128)
    c_rep, c_t = _fox_prep_call(fl_raw, b_pad, name="fox_prep")
    bias = _chk_bias_call(_rel_table_to_g(sp["rel_bias"]), name="chk_bias")
    mid = ["w_mk", "w_mv", "w_mo", "w_ff1"]
    (yf, lse), got = _fox_fwd_call(proj, c_rep, c_t, name="fox_fwd",
                                   task=_merge_tasks([gather_ici(mid), _ag_d2d_task([g_out, g_mq])]))
    g_mid, (g_out, g_mq) = got[:4], got[4:]
    yc, got = _chk_fwd_call(proj, bias, name="chk_fwd",
                            task=_merge_tasks([gather_ici(["w_ff2"]), _ag_d2d_task(g_mid)]))
    g_ff2, (g_mk, g_mv, g_mo, g_ff1) = got[0], got[1:]
    yn = _mix_norm_fwd_call(yf, yc, sp["g_fox_out"], sp["g_chk_out"], name="mix_norm_fwd")
    z, (g_ff2,) = _mm_nn(yn, g_out, "rows", name="mm_out", out_dtype=F32, task=_ag_d2d_task([g_ff2]))
    x1, h2 = _post_pre_call(xs, z, sp["g_mix_post"], sp["g_mem_pre"], name="post_mix")
    memn = _rms_fwd_call(mems, sp["g_mem_kv"], name="rms_mem_kv")
    q2 = _mm_nn(h2, g_mq, "rows", name="mm_mq")
    k2 = _mm_nn(memn, g_mk, "rows", name="mm_mk")
    v2 = _mm_nn(memn, g_mv, "rows", name="mm_mv")
    o2 = _mem_fwd_call(q2, k2, v2, name="mem_fwd")
    y2 = _mm_nn(o2, g_mo, "rows", name="mm_mo", out_dtype=F32)
    x2, h3 = _post_pre_call(x1, y2, sp["g_mem_post"], sp["g_ff_pre"], name="post_mem")
    act, relu = _mm_nn(h3, g_ff1, "cols", name="mm_ff1", epi="relu2")
    y3 = _mm_nn(act, g_ff2, "rows", name="mm_ff2", out_dtype=F32, tm=1024)
    loss_blk, dx3, dy3, dg_ff_post = _final_call(x2, y3, sp["g_ff_post"], target, name="final")

    d_ff2 = _mm_tn(act, dy3, name="mm_dff2", tk=512, tn=1024).reshape(N_CHIP, D_FF // N_CHIP, D)
    du, (r1,) = _mm_nt(dy3, g_ff2, "rows", name="mm_du", mul2r=relu, task=_rs_pair_task([d_ff2]))
    p_ff2 = pair_add("w_ff2", d_ff2, r1)
    d_ff1 = _mm_tn(h3, du, name="mm_dff1", cols4=True)
    dh3, (r1,) = _mm_nt(du, g_ff1, "cols", name="mm_dh3", out_dtype=F32, tm=1024, task=_rs_pair_task([d_ff1]))
    p_ff1 = pair_add("w_ff1", d_ff1, r1)
    dx2, dy2, dg_ff_pre, dg_mem_post = _bwd_mid_call(dx3, x2, dh3, sp["g_ff_pre"], y2, sp["g_mem_post"], name="bwd_ff")
    d_mo = _mm_tn(o2, dy2, name="mm_dmo").reshape(N_CHIP, D // N_CHIP, D)
    do2 = _mm_nt(dy2, g_mo, "rows", name="mm_do2")
    dq2, dk2, dv2 = _mem_bwd_call(q2, k2, v2, do2, name="mem_bwd")
    d_mq = _mm_tn(h2, dq2, name="mm_dmq").reshape(N_CHIP, D // N_CHIP, D)
    dh2 = _mm_nt(dq2, g_mq, "rows", name="mm_dh2", out_dtype=F32)
    d_mk = _mm_tn(memn, dk2, name="mm_dmk").reshape(N_CHIP, D // N_CHIP, D)
    d_mv = _mm_tn(memn, dv2, name="mm_dmv").reshape(N_CHIP, D // N_CHIP, D)
    dmn_k = _mm_nt(dk2, g_mk, "rows", name="mm_dmemk", out_dtype=F32)
    dmn_v = _mm_nt(dv2, g_mv, "rows", name="mm_dmemv", out_dtype=F32)
    dg_mem_kv = _gain_grad_call(mems, sp["g_mem_kv"], dmn_k, dmn_v, name="gain_mem_kv")
    dx1, dz, dg_mem_pre, dg_mix_post = _bwd_mid_call(dx2, x1, dh2, sp["g_mem_pre"], z, sp["g_mix_post"], name="bwd_mem")
    d_out = _mm_tn(yn, dz, name="mm_dout").reshape(N_CHIP, D // N_CHIP, D)
    late = ["w_mo", "w_mq", "w_mk", "w_mv", "w_out"]
    d_late = [d_mo, d_mq, d_mk, d_mv, d_out]
    dyn, r1_late = _mm_nt(dz, g_out, "rows", name="mm_dyn", out_dtype=F32, task=_rs_pair_task(d_late))
    p_late = [pair_add(k, d, r1) for k, d, r1 in zip(late, d_late, r1_late)]
    dof, doc, delta, dg_fox, dg_chk = _mix_norm_bwd_call(dyn, yf, yc, sp["g_fox_out"], sp["g_chk_out"], name="mix_norm_bwd")
    (dqf, dkf, dvf, dcq, dck), r2_ff = _fox_bwd_call(proj, dof, lse, delta, c_rep, c_t, name="fox_bwd",
                                                      task=_rs_chip_task([p_ff2, p_ff1]))
    (dqc, dkc, dvc, dgrev), r2_late = _chk_bwd_call(proj, doc, bias, name="chk_bwd", task=_rs_chip_task(p_late))
    first = ["w_ff2", "w_ff1"] + late
    f_first = [_chip_sum_call(p, r, place_arr, name="rs_chip_sum_" + k)
               for k, p, r in zip(first, [p_ff2, p_ff1] + p_late, r2_ff + r2_late)]
    dc8 = dcq[:, :, 0:2, :].transpose(0, 2, 1, 3).reshape(8, S)
    dc_rows = jnp.concatenate([dc8, jnp.zeros((120, S), F32)], axis=0)
    dfl, db_fgt = _fox_gate_bwd_call(dc_rows, dck, fl_raw, b_pad, name="fox_gate_bwd")
    dproj = jnp.concatenate([dqf, dkf, dvf, dqc, dkc, dvc, dfl], axis=1)
    d_all_t, g_first = _mm_tn(dproj, h1, name="mm_dwin", tk=640, tn=1024, task=_rs_gather_task(f_first))
    grads = dict(zip(first, g_first))
    d_in = _reorder_rows_call(d_all_t, False, name="d_in_rows").reshape(N_CHIP, IN_PAD, D)
    delta_w, new_m, new_v = {}, {}, {}

    def adamw_items(names):
        return [(w[k][0], grads[k], m[k][0], v[k][0]) for k in names]

    upd_late, (r1,) = _adamw_call(adamw_items(late[:1]), name="adamw_late_first", task=_rs_pair_task([d_in]))
    upd_late += _adamw_call(adamw_items(late[1:]), name="adamw_late", tm=128)
    p_in = pair_add("w_in", d_in, r1)
    dh1, (r2_in,) = _mm_nn(dproj, w_all_t, "plain", name="mm_dh1", out_dtype=F32, tm=1024,
                           task=_rs_chip_task([p_in]))
    f_in = _chip_sum_call(p_in, r2_in, place_arr, name="rs_chip_sum_w_in")
    upd_ff = _adamw_call(adamw_items(first[:2]), name="adamw_ff")
    for k, res in zip(late + first[:2], upd_late + upd_ff):
        grads[k], delta_w[k], new_m[k], new_v[k] = res
    grad_x, dg_mix_pre = _bwd_last_call(dx1, xs, dh1, sp["g_mix_pre"], name="bwd_mix")

    small_g = {"g_mix_pre": dg_mix_pre, "g_mix_post": dg_mix_post, "g_mem_kv": dg_mem_kv, "g_mem_pre": dg_mem_pre,
               "g_mem_post": dg_mem_post, "g_ff_pre": dg_ff_pre, "g_ff_post": dg_ff_post, "g_fox_out": dg_fox,
               "g_chk_out": dg_chk, "b_fgt": db_fgt,
               "rel_bias": _g_to_rel_table(dgrev[:, 0:2, :].reshape(8, ROLL_W))}
    small_sum, loss_out, (g_w_in,) = _small_reduce_call(small_g, loss_blk, _rs_gather_task([f_in]),
                                                       name="small_allreduce")
    small = _small_adamw_call(small_sum, sp, {k: rows(m, k) for k in SMALL}, {k: rows(v, k) for k in SMALL},
                              name="small_adamw")
    loss = loss_out[0, 0]
    res = _adamw_cols_call(w_in_t, g_w_in, m_in_t, v_in_t, name="adamw_w_in")
    grads["w_in"], delta_w["w_in"], new_m["w_in"], new_v["w_in"] = (a.T for a in res)
    for k in SMALL:
        vals = small[k]
        if k == "rel_bias":
            vals = tuple(a[None] for a in vals)
        grads[k], delta_w[k], new_m[k], new_v[k] = vals

    def out(d, k):
        return d[k][None] if k in BIG else d[k]

    return (loss, grad_x[None], *[out(grads, k) for k in WEIGHTS], *[out(delta_w, k) for k in WEIGHTS],
            *[out(new_m, k) for k in WEIGHTS], *[out(new_v, k) for k in WEIGHTS])
```

```python
import functools

import jax
import jax.numpy as jnp
from jax import lax
from jax.experimental import pallas as pl
from jax.experimental.pallas import tpu as pltpu

F32 = jnp.float32
BF16 = jnp.bfloat16

D = 1024
HEAD = 64
N_PAIR = 4
D_GRP = 512
CHUNK = 64
LEFT = 8
MAX_REL = 128
N_REL = 2 * MAX_REL + 1
N_MEM = 256
MEM_HEADS = 4
MEM_HD = 256
D_FF = 4096
D_IN = 3080
D_ALL = 3200
EPS = 1e-6
TQ = 256
WIN = (LEFT + TQ // CHUNK) * CHUNK
PADK = LEFT * CHUNK
ROLL_W = 1024
NEG = -1e30
N_CHIP = 4
VMEM_LIMIT = 48 * 1024 * 1024

ADAM_LR = 0.001
ADAM_B1 = 0.9
ADAM_B2 = 0.999
ADAM_EPS = 1e-08
ADAM_WD = 0.01
ADAM_STEP = 10

MESH = pl.DeviceIdType.MESH


def _cparams():
    return pltpu.CompilerParams(vmem_limit_bytes=VMEM_LIMIT)


ANY = pl.BlockSpec(memory_space=pl.ANY)


class _Task:
    def __init__(self, arrays, out_shapes, sems, issue, drain, aliases=None):
        self.arrays, self.out_shapes, self.sems = list(arrays), list(out_shapes), list(sems)
        self.issue, self.drain, self.aliases = issue, drain, dict(aliases or {})


def _merge_tasks(tasks):
    tasks = [t for t in tasks if t is not None]
    if len(tasks) == 1:
        return tasks[0]
    cuts, a, o, s = [], 0, 0, 0
    aliases = {}
    for t in tasks:
        cuts.append((a, o, s))
        aliases.update({a + i: o + j for i, j in t.aliases.items()})
        a, o, s = a + len(t.arrays), o + len(t.out_shapes), s + len(t.sems)

    def part(fn_name):
        def run(ins, outs, sems):
            for t, (a0, o0, s0) in zip(tasks, cuts):
                getattr(t, fn_name)(ins[a0:a0 + len(t.arrays)], outs[o0:o0 + len(t.out_shapes)],
                                    sems[s0:s0 + len(t.sems)])
        return run

    return _Task([x for t in tasks for x in t.arrays], [x for t in tasks for x in t.out_shapes],
                 [x for t in tasks for x in t.sems], part("issue"), part("drain"), aliases)


def _pallas(body, *, grid, in_specs, out_specs, out_shape, name, scratch_shapes=(), task=None, prefetch=0):
    def make(kernel, i_specs, o_specs, o_shape, scratch, aliases):
        if prefetch:
            spec = pltpu.PrefetchScalarGridSpec(num_scalar_prefetch=prefetch, grid=grid, in_specs=i_specs,
                                                out_specs=o_specs, scratch_shapes=scratch)
            return pl.pallas_call(kernel, grid_spec=spec, out_shape=o_shape, input_output_aliases=aliases,
                                  name=name, compiler_params=_cparams())
        return pl.pallas_call(kernel, grid=grid, in_specs=i_specs, out_specs=o_specs, out_shape=o_shape,
                              scratch_shapes=scratch, input_output_aliases=aliases, name=name,
                              compiler_params=_cparams())

    if task is None:
        return make(body, list(in_specs), out_specs, out_shape, list(scratch_shapes), {})
    single = not isinstance(out_shape, (tuple, list))
    o_shapes = [out_shape] if single else list(out_shape)
    o_specs = [out_specs] if single else list(out_specs)
    n_in, n_out, n_scr = len(in_specs), len(o_shapes), len(scratch_shapes)
    t_in, t_out = len(task.arrays), len(task.out_shapes)

    def carried(*refs):
        cut = [prefetch, n_in, t_in, n_out, t_out, n_scr]
        parts, p = [], 0
        for c in cut:
            parts.append(refs[p:p + c])
            p += c
        scalars, ins, tins, outs, touts, scr = parts
        tsems = refs[p:]
        ids = [pl.program_id(a) for a in range(len(grid))]
        first = functools.reduce(jnp.logical_and, [i == 0 for i in ids])
        last = functools.reduce(jnp.logical_and, [i == g - 1 for i, g in zip(ids, grid)])

        @pl.when(first)
        def _():
            task.issue(tins, touts, tsems)
        body(*scalars, *ins, *outs, *scr)

        @pl.when(last)
        def _():
            task.drain(tins, touts, tsems)

    call = make(carried, list(in_specs) + [ANY] * t_in, o_specs + [ANY] * t_out,
                o_shapes + list(task.out_shapes), list(scratch_shapes) + list(task.sems),
                {prefetch + n_in + i: n_out + j for i, j in task.aliases.items()})

    def run(*args):
        res = call(*args, *task.arrays)
        outs = res[:n_out]
        return (outs[0] if single else tuple(outs)), list(res[n_out:])

    return run


def _dot(a, b):
    return jnp.dot(a, b, preferred_element_type=F32)


def _dot_nt(a, b):
    return lax.dot_general(a, b, (((1,), (1,)), ((), ())), preferred_element_type=F32)


def _dot_tn(a, b):
    return lax.dot_general(a, b, (((0,), (0,)), ((), ())), preferred_element_type=F32)


def _split3(x):
    hi = x.astype(BF16)
    r1 = x - hi.astype(F32)
    mid = r1.astype(BF16)
    lo = (r1 - mid.astype(F32)).astype(BF16)
    return hi, mid, lo


def _dot3(x, m01):
    hi, mid, lo = _split3(x)
    return _dot(hi, m01) + _dot(mid, m01) + _dot(lo, m01)


def _dot3_l(m01, x):
    hi, mid, lo = _split3(x)
    return _dot(m01, hi) + _dot(m01, mid) + _dot(m01, lo)


def _mm_nn(a, b, kind, *, name, out_dtype=BF16, tm=2048, tn=512, epi=None, task=None):
    M, K = a.shape
    if kind == "plain":
        N = b.shape[1]
        b_spec = pl.BlockSpec((K, tn), lambda m, n: (0, n))
    elif kind == "rows":
        N = b.shape[2]
        b_spec = pl.BlockSpec((N_CHIP, K // N_CHIP, tn), lambda m, n: (0, 0, n))
    else:
        nq = b.shape[2]
        N = N_CHIP * nq
        per = nq // tn
        b_spec = pl.BlockSpec((None, K, tn), lambda m, n: (n // per, 0, n % per))
    tm = min(tm, M)
    kq = K // N_CHIP

    def body(a_ref, b_ref, *o_refs):
        if kind == "rows":
            acc = _dot(a_ref[:, 0:kq], b_ref[0])
            for j in range(1, N_CHIP):
                acc += _dot(a_ref[:, j * kq:(j + 1) * kq], b_ref[j])
        else:
            acc = _dot(a_ref[...], b_ref[...])
        if epi == "relu2":
            r = jnp.maximum(acc, 0.0)
            o_refs[0][...] = (r * r).astype(BF16)
            o_refs[1][...] = r.astype(BF16)
        else:
            o_refs[0][...] = acc.astype(out_dtype)

    o_spec = pl.BlockSpec((tm, tn), lambda m, n: (m, n))
    if epi == "relu2":
        out_shape = (jax.ShapeDtypeStruct((M, N), BF16), jax.ShapeDtypeStruct((M, N), BF16))
        out_specs = (o_spec, o_spec)
    else:
        out_shape = jax.ShapeDtypeStruct((M, N), out_dtype)
        out_specs = o_spec
    return _pallas(
        body, grid=(M // tm, N // tn),
        in_specs=[pl.BlockSpec((tm, K), lambda m, n: (m, 0)), b_spec],
        out_specs=out_specs, out_shape=out_shape, name=name, task=task,
    )(a, b)


def _mm_nt(a, b, kind, *, name, out_dtype=BF16, tm=2048, tn=512, mul2r=None, task=None, rows=None):
    M, K = a.shape
    if kind == "plain":
        first, N = rows if rows is not None else (0, b.shape[0])
        n0 = first // tn
        b_spec = pl.BlockSpec((tn, K), lambda m, n: (n0 + n, 0))
    elif kind == "rows":
        nq = b.shape[1]
        N = N_CHIP * nq
        tn = min(tn, nq)
        per = nq // tn
        b_spec = pl.BlockSpec((None, tn, K), lambda m, n: (n // per, n % per, 0))
    else:
        N = b.shape[1]
        b_spec = pl.BlockSpec((N_CHIP, tn, K // N_CHIP), lambda m, n: (0, n, 0))
    tm = min(tm, M)
    kq = K // N_CHIP

    def body(a_ref, b_ref, *rest):
        o_ref = rest[-1]
        if kind == "cols":
            acc = _dot_nt(a_ref[:, 0:kq], b_ref[0])
            for j in range(1, N_CHIP):
                acc += _dot_nt(a_ref[:, j * kq:(j + 1) * kq], b_ref[j])
        else:
            acc = _dot_nt(a_ref[...], b_ref[...])
        if mul2r is not None:
            acc = acc * (2.0 * rest[0][...].astype(F32))
        o_ref[...] = acc.astype(out_dtype)

    in_specs = [pl.BlockSpec((tm, K), lambda m, n: (m, 0)), b_spec]
    args = [a, b]
    if mul2r is not None:
        in_specs.append(pl.BlockSpec((tm, tn), lambda m, n: (m, n)))
        args.append(mul2r)
    return _pallas(
        body, grid=(M // tm, N // tn), in_specs=in_specs,
        out_specs=pl.BlockSpec((tm, tn), lambda m, n: (m, n)),
        out_shape=jax.ShapeDtypeStruct((M, N), out_dtype), name=name, task=task,
    )(*args)


def _mm_tn(a, b, *, name, out_dtype=BF16, tk=1024, tn=512, cols4=False, task=None):
    M, K1 = a.shape
    N = b.shape[1]
    tk = min(tk, K1)
    tn = min(tn, N)

    def body(a_ref, b_ref, o_ref):
        o_ref[...] = _dot_tn(a_ref[...], b_ref[...]).astype(out_dtype)

    if cols4:
        per = (N // N_CHIP) // tn
        out_shape = jax.ShapeDtypeStruct((N_CHIP, K1, N // N_CHIP), out_dtype)
        o_spec = pl.BlockSpec((None, tk, tn), lambda k, n: (n // per, k, n % per))
    else:
        out_shape = jax.ShapeDtypeStruct((K1, N), out_dtype)
        o_spec = pl.BlockSpec((tk, tn), lambda k, n: (k, n))
    return _pallas(
        body, grid=(K1 // tk, N // tn),
        in_specs=[pl.BlockSpec((M, tk), lambda k, n: (0, k)), pl.BlockSpec((M, tn), lambda k, n: (0, n))],
        out_specs=o_spec, out_shape=out_shape, name=name, task=task,
    )(a, b)


def _rms(x, g):
    r = lax.rsqrt(jnp.mean(x * x, axis=-1, keepdims=True) + EPS)
    return x * r * g


def _rms_bwd(x, g, dy):
    r = lax.rsqrt(jnp.mean(x * x, axis=-1, keepdims=True) + EPS)
    xh = x * r
    dg = jnp.sum(dy * xh, axis=0, keepdims=True)
    dxh = dy * g
    dx = r * (dxh - xh * jnp.mean(dxh * xh, axis=-1, keepdims=True))
    return dx, dg


def _row_spec(tm, n):
    return pl.BlockSpec((tm, n), lambda i: (i, 0))


def _vec_spec(n):
    return pl.BlockSpec((1, n), lambda i: (0, 0))


def _acc_spec(n):
    return pl.BlockSpec((8, n), lambda i: (0, 0))


def _acc_add(ref, row, i):
    @pl.when(i == 0)
    def _():
        ref[...] = jnp.zeros_like(ref)
    ref[0:1, :] += row


def _rms_fwd_call(x, g, *, name, tm=1024, task=None):
    M, n = x.shape
    tm = min(tm, M)

    def body(x_ref, g_ref, h_ref):
        h_ref[...] = _rms(x_ref[...], g_ref[...]).astype(BF16)

    return _pallas(
        body, grid=(M // tm,), in_specs=[_row_spec(tm, n), _vec_spec(n)], out_specs=_row_spec(tm, n),
        out_shape=jax.ShapeDtypeStruct((M, n), BF16), name=name, task=task,
    )(x, g)


def _post_pre_call(xres, z, g_post, g_pre, *, name, tm=1024):
    M, n = xres.shape

    def body(x_ref, z_ref, gp_ref, gn_ref, xo_ref, h_ref):
        xn = x_ref[...] + _rms(z_ref[...], gp_ref[...])
        xo_ref[...] = xn
        h_ref[...] = _rms(xn, gn_ref[...]).astype(BF16)

    return pl.pallas_call(
        body, grid=(M // tm,),
        in_specs=[_row_spec(tm, n), _row_spec(tm, n), _vec_spec(n), _vec_spec(n)],
        out_specs=(_row_spec(tm, n), _row_spec(tm, n)),
        out_shape=(jax.ShapeDtypeStruct((M, n), F32), jax.ShapeDtypeStruct((M, n), BF16)),
        name=name, compiler_params=_cparams(),
    )(xres, z, g_post, g_pre)


def _final_call(x2, y3, g_post, target, *, name, tm=512):
    M, n = x2.shape

    def body(x_ref, y_ref, g_ref, t_ref, loss_ref, dx_ref, dy_ref, dg_ref):
        i = pl.program_id(0)
        y = y_ref[...]
        g = g_ref[...]
        diff = x_ref[...] + _rms(y, g) - t_ref[...]
        part = 0.5 * jnp.sum(jnp.sum(diff * diff, axis=1, keepdims=True), axis=0, keepdims=True) / n

        @pl.when(i == 0)
        def _():
            loss_ref[...] = jnp.zeros_like(loss_ref)
        loss_ref[...] += jnp.broadcast_to(part, loss_ref.shape)
        dx = diff / n
        dx_ref[...] = dx
        dy, dg = _rms_bwd(y, g, dx)
        dy_ref[...] = dy.astype(BF16)
        _acc_add(dg_ref, dg, i)

    return pl.pallas_call(
        body, grid=(M // tm,),
        in_specs=[_row_spec(tm, n), _row_spec(tm, n), _vec_spec(n), _row_spec(tm, n)],
        out_specs=(pl.BlockSpec((8, 128), lambda i: (0, 0)), _row_spec(tm, n), _row_spec(tm, n), _acc_spec(n)),
        out_shape=(jax.ShapeDtypeStruct((8, 128), F32), jax.ShapeDtypeStruct((M, n), F32),
                   jax.ShapeDtypeStruct((M, n), BF16), jax.ShapeDtypeStruct((8, n), F32)),
        name=name, compiler_params=_cparams(),
    )(x2, y3, g_post, target)


def _bwd_mid_call(dx_in, x, dh, g_pre, y, g_post, *, name, tm=512):
    M, n = x.shape

    def body(dxi_ref, x_ref, dh_ref, gpre_ref, y_ref, gpost_ref, dx_ref, dy_ref, dgpre_ref, dgpost_ref):
        i = pl.program_id(0)
        d1, dg1 = _rms_bwd(x_ref[...], gpre_ref[...], dh_ref[...])
        dx = dxi_ref[...] + d1
        dx_ref[...] = dx
        dy, dg2 = _rms_bwd(y_ref[...], gpost_ref[...], dx)
        dy_ref[...] = dy.astype(BF16)
        _acc_add(dgpre_ref, dg1, i)
        _acc_add(dgpost_ref, dg2, i)

    return pl.pallas_call(
        body, grid=(M // tm,),
        in_specs=[_row_spec(tm, n), _row_spec(tm, n), _row_spec(tm, n), _vec_spec(n), _row_spec(tm, n), _vec_spec(n)],
        out_specs=(_row_spec(tm, n), _row_spec(tm, n), _acc_spec(n), _acc_spec(n)),
        out_shape=(jax.ShapeDtypeStruct((M, n), F32), jax.ShapeDtypeStruct((M, n), BF16),
                   jax.ShapeDtypeStruct((8, n), F32), jax.ShapeDtypeStruct((8, n), F32)),
        name=name, compiler_params=_cparams(),
    )(dx_in, x, dh, g_pre, y, g_post)


def _bwd_last_call(dx_in, x, dh, g_pre, *, name, tm=1024, task=None):
    M, n = x.shape

    def body(dxi_ref, x_ref, dh_ref, g_ref, dx_ref, dg_ref):
        i = pl.program_id(0)
        d1, dg1 = _rms_bwd(x_ref[...], g_ref[...], dh_ref[...])
        dx_ref[...] = dxi_ref[...] + d1
        _acc_add(dg_ref, dg1, i)

    return _pallas(
        body, grid=(M // tm,),
        in_specs=[_row_spec(tm, n), _row_spec(tm, n), _row_spec(tm, n), _vec_spec(n)],
        out_specs=(_row_spec(tm, n), _acc_spec(n)),
        out_shape=(jax.ShapeDtypeStruct((M, n), F32), jax.ShapeDtypeStruct((8, n), F32)),
        name=name, task=task,
    )(dx_in, x, dh, g_pre)


def _gain_grad_call(x, g, dy_a, dy_b, *, name):
    M, n = x.shape

    def body(x_ref, g_ref, a_ref, b_ref, dg_ref):
        _, dg = _rms_bwd(x_ref[...], g_ref[...], a_ref[...] + b_ref[...])
        dg_ref[...] = jnp.zeros_like(dg_ref)
        dg_ref[0:1, :] = dg

    return pl.pallas_call(
        body, grid=(1,),
        in_specs=[_row_spec(M, n), _vec_spec(n), _row_spec(M, n), _row_spec(M, n)],
        out_specs=_acc_spec(n), out_shape=jax.ShapeDtypeStruct((8, n), F32),
        name=name, compiler_params=_cparams(),
    )(x, g, dy_a, dy_b)


def _head_group_matrix():
    a = lax.broadcasted_iota(jnp.int32, (D_GRP, D_GRP), 0) // HEAD
    b = lax.broadcasted_iota(jnp.int32, (D_GRP, D_GRP), 1) // HEAD
    return jnp.where(a == b, 1.0, 0.0).astype(BF16)


def _mix_norm_fwd_call(yf, yc, gf, gc, *, name, tm=1024):
    M = yf.shape[0]

    def body(yf_ref, yc_ref, gf_ref, gc_ref, o_ref):
        o_ref[:, 0:D_GRP] = _rms(yf_ref[...], gf_ref[...]).astype(BF16)
        o_ref[:, D_GRP:D] = _rms(yc_ref[...], gc_ref[...]).astype(BF16)

    return pl.pallas_call(
        body, grid=(M // tm,),
        in_specs=[_row_spec(tm, D_GRP), _row_spec(tm, D_GRP), _vec_spec(D_GRP), _vec_spec(D_GRP)],
        out_specs=_row_spec(tm, D), out_shape=jax.ShapeDtypeStruct((M, D), BF16),
        name=name, compiler_params=_cparams(),
    )(yf, yc, gf, gc)


def _mix_norm_bwd_call(dyn, yf, yc, gf, gc, *, name, tm=2 * TQ):
    M = yf.shape[0]

    def body(dyn_ref, yf_ref, yc_ref, gf_ref, gc_ref, dof_ref, doc_ref, delta_ref, dgf_ref, dgc_ref):
        i = pl.program_id(0)
        yf_ = yf_ref[...]
        dof, dgf = _rms_bwd(yf_, gf_ref[...], dyn_ref[:, 0:D_GRP])
        doc, dgc = _rms_bwd(yc_ref[...], gc_ref[...], dyn_ref[:, D_GRP:D])
        dof_b = dof.astype(BF16)
        dof_ref[...] = dof_b
        doc_ref[...] = doc.astype(BF16)
        prod = dof_b.astype(F32) * yf_
        hi = prod.astype(BF16)
        lo = (prod - hi.astype(F32)).astype(BF16)
        grp = _head_group_matrix()
        delta = _dot(hi, grp) + _dot(lo, grp)
        for b in range(tm // TQ):
            delta_ref[b] = delta[b * TQ:(b + 1) * TQ, :].T
        _acc_add(dgf_ref, dgf, i)
        _acc_add(dgc_ref, dgc, i)

    return pl.pallas_call(
        body, grid=(M // tm,),
        in_specs=[_row_spec(tm, D), _row_spec(tm, D_GRP), _row_spec(tm, D_GRP), _vec_spec(D_GRP), _vec_spec(D_GRP)],
        out_specs=(_row_spec(tm, D_GRP), _row_spec(tm, D_GRP),
                   pl.BlockSpec((tm // TQ, D_GRP, TQ), lambda i: (i, 0, 0)), _acc_spec(D_GRP), _acc_spec(D_GRP)),
        out_shape=(jax.ShapeDtypeStruct((M, D_GRP), BF16), jax.ShapeDtypeStruct((M, D_GRP), BF16),
                   jax.ShapeDtypeStruct((M // TQ, D_GRP, TQ), F32), jax.ShapeDtypeStruct((8, D_GRP), F32),
                   jax.ShapeDtypeStruct((8, D_GRP), F32)),
        name=name, compiler_params=_cparams(),
    )(dyn, yf, yc, gf, gc)


def _tri(n, lower_incl):
    a = lax.broadcasted_iota(jnp.int32, (n, n), 0)
    b = lax.broadcasted_iota(jnp.int32, (n, n), 1)
    return jnp.where(a >= b, 1.0, 0.0).astype(BF16) if lower_incl else jnp.where(a <= b, 1.0, 0.0).astype(BF16)


def _fox_prep_call(fl_raw, b_pad, *, name):
    S = fl_raw.shape[0]
    nb = S // TQ

    def body(fl_ref, b_ref, crep_ref, ct_ref, carry_ref):
        i = pl.program_id(0)

        @pl.when(i == 0)
        def _():
            carry_ref[...] = jnp.zeros_like(carry_ref)
        logf = jax.nn.log_sigmoid(fl_ref[...] + b_ref[...])
        cb = _dot3_l(_tri(TQ, True), logf) + carry_ref[0:1, :]
        carry_ref[0:1, :] = cb[TQ - 1:TQ, :]
        a = lax.broadcasted_iota(jnp.int32, (128, D_GRP), 0)
        b = lax.broadcasted_iota(jnp.int32, (128, D_GRP), 1) // HEAD
        expand = jnp.where(a == b, 1.0, 0.0).astype(BF16)
        crep = _dot3(cb, expand)
        crep_ref[...] = crep
        ct_ref[...] = crep.T

    return pl.pallas_call(
        body, grid=(nb,),
        in_specs=[_row_spec(TQ, 128), _vec_spec(128)],
        out_specs=(_row_spec(TQ, D_GRP), pl.BlockSpec((None, D_GRP, TQ), lambda i: (i, 0, 0))),
        out_shape=(jax.ShapeDtypeStruct((S, D_GRP), F32), jax.ShapeDtypeStruct((nb, D_GRP, TQ), F32)),
        scratch_shapes=[pltpu.VMEM((8, 128), F32)],
        name=name, compiler_params=_cparams(),
    )(fl_raw, b_pad)


def _lane_masks():
    lane = lax.broadcasted_iota(jnp.int32, (1, 128), 1)
    return lane < HEAD, lane >= HEAD


def _fox_fwd_call(proj, c_rep, c_t, *, name, task=None):
    S = proj.shape[0]
    nq = S // TQ
    scale = HEAD ** -0.5

    def body(q_ref, k_ref, v_ref, c_ref, ct_ref, o_ref, lse_ref):
        i = pl.program_id(1)
        m_lo, m_hi = _lane_masks()
        masks = (m_lo, m_hi)
        q = q_ref[...] * scale
        qm = [jnp.where(mk, q, jnp.zeros_like(q)) for mk in masks]
        cq = c_ref[...]
        cqh = [cq[:, 0:1], cq[:, HEAD:HEAD + 1]]
        row = lax.broadcasted_iota(jnp.int32, (TQ, TQ), 0)
        col = lax.broadcasted_iota(jnp.int32, (TQ, TQ), 1)

        def scores(j):
            start = pl.multiple_of(j * TQ, TQ)
            k = k_ref[pl.ds(start, TQ), :]
            ct = ct_ref[j]
            return tuple(_dot_nt(qm[h], k) + (cqh[h] - ct[HEAD * h:HEAD * h + 1, :]) for h in range(2))

        def update(j, ss, state, masked):
            ms, ls, acc = state
            start = pl.multiple_of(j * TQ, TQ)
            v = v_ref[pl.ds(start, TQ), :]
            new_m, new_l, pv, alpha_l = [], [], [], []
            for h in range(2):
                s = ss[h]
                if masked:
                    s = jnp.where(row >= col, s, NEG)
                mn = jnp.maximum(ms[h], jnp.max(s, axis=1, keepdims=True))
                alpha = jnp.exp(ms[h] - mn)
                p = jnp.exp(s - mn)
                new_l.append(alpha * ls[h] + jnp.sum(p, axis=1, keepdims=True))
                new_m.append(mn)
                alpha_l.append(alpha)
                pv.append(_dot(p.astype(BF16), jnp.where(masks[h], v, jnp.zeros_like(v))))
            alpha_lane = jnp.where(m_lo, alpha_l[0], alpha_l[1])
            acc = acc * alpha_lane + pv[0] + pv[1]
            return (tuple(new_m), tuple(new_l), acc)

        def step(j, carry):
            ss, state = carry
            return (scores(j + 1), update(j, ss, state, False))

        init = ((jnp.full((TQ, 1), NEG, F32),) * 2, (jnp.zeros((TQ, 1), F32),) * 2, jnp.zeros((TQ, 128), F32))
        ss, state = lax.fori_loop(0, i, step, (scores(0), init))
        ms, ls, acc = update(i, ss, state, True)
        l_lane = jnp.where(m_lo, ls[0], ls[1])
        o_ref[...] = acc / l_lane
        lse_ref[...] = jnp.where(m_lo, ms[0] + jnp.log(ls[0]), ms[1] + jnp.log(ls[1])).T

    return _pallas(
        body, grid=(N_PAIR, nq),
        in_specs=[pl.BlockSpec((TQ, 128), lambda p, i: (i, p)),
                  pl.BlockSpec((S, 128), lambda p, i: (0, N_PAIR + p)),
                  pl.BlockSpec((S, 128), lambda p, i: (0, 2 * N_PAIR + p)),
                  pl.BlockSpec((TQ, 128), lambda p, i: (i, p)),
                  pl.BlockSpec((nq, 128, TQ), lambda p, i: (0, p, 0))],
        out_specs=(pl.BlockSpec((TQ, 128), lambda p, i: (i, p)), pl.BlockSpec((None, 128, TQ), lambda p, i: (i, p, 0))),
        out_shape=(jax.ShapeDtypeStruct((S, D_GRP), F32), jax.ShapeDtypeStruct((nq, D_GRP, TQ), F32)),
        name=name, task=task,
    )(proj, proj, proj, c_rep, c_t)


def _fox_bwd_call(proj, do, lse_t, delta_t, c_rep, c_t, *, name, task=None):
    S = proj.shape[0]
    nq = S // TQ
    scale = HEAD ** -0.5

    def body(q_ref, k_ref, v_ref, do_ref, lse_ref, dl_ref, ck_ref, ct_ref,
             dq_ref, dk_ref, dv_ref, dcq_ref, dck_ref, dqa_ref):
        j = pl.program_id(1)
        m_lo, m_hi = _lane_masks()
        masks = (m_lo, m_hi)

        @pl.when(j == 0)
        def _():
            dqa_ref[...] = jnp.zeros_like(dqa_ref)
            dcq_ref[...] = jnp.zeros_like(dcq_ref)
        k = k_ref[...]
        v = v_ref[...]
        km = [jnp.where(mk, k, jnp.zeros_like(k)) for mk in masks]
        ck = ck_ref[...]
        krow = lax.broadcasted_iota(jnp.int32, (TQ, TQ), 0)
        qcol = lax.broadcasted_iota(jnp.int32, (TQ, TQ), 1)

        def probs(i):
            start = pl.multiple_of(i * TQ, TQ)
            q = q_ref[pl.ds(start, TQ), :]
            do = do_ref[pl.ds(start, TQ), :]
            lse = lse_ref[i]
            cq = ct_ref[i]
            out = []
            for h in range(2):
                lo = HEAD * h
                qm = jnp.where(masks[h], q * scale, jnp.zeros_like(q))
                dom = jnp.where(masks[h], do, jnp.zeros_like(do))
                st = _dot_nt(k, qm) + (cq[lo:lo + 1, :] - ck[:, lo:lo + 1])
                out.append((jnp.exp(st - lse[lo:lo + 1, :]), _dot_nt(v, dom)))
            return tuple(out)

        def update(i, pd, carry, masked):
            dk, dv, dck = carry
            start = pl.multiple_of(i * TQ, TQ)
            q = q_ref[pl.ds(start, TQ), :]
            do = do_ref[pl.ds(start, TQ), :]
            dl = dl_ref[i]
            dq = jnp.zeros((TQ, 128), F32)
            new_dck = []
            for h in range(2):
                lo = HEAD * h
                qm = jnp.where(masks[h], q, jnp.zeros_like(q))
                dom = jnp.where(masks[h], do, jnp.zeros_like(do))
                pt, dpt = pd[h]
                if masked:
                    pt = jnp.where(qcol >= krow, pt, 0.0)
                dst = pt * (dpt - dl[lo:lo + 1, :])
                dcq_ref[i, h:h + 1, :] += jnp.sum(dst, axis=0, keepdims=True)
                new_dck.append(dck[h] + jnp.sum(dst, axis=1, keepdims=True))
                dsb = (dst * scale).astype(BF16)
                dv = dv + _dot(pt.astype(BF16), dom)
                dk = dk + _dot(dsb, qm)
                dq = dq + _dot_tn(dsb, km[h])
            dqa_ref[pl.ds(start, TQ), :] += dq
            return (dk, dv, tuple(new_dck))

        def step(i, carry):
            pd, sums = carry
            return (probs(jnp.minimum(i + 1, nq - 1)), update(i, pd, sums, False))

        init = (jnp.zeros((TQ, 128), F32), jnp.zeros((TQ, 128), F32), (jnp.zeros((TQ, 1), F32),) * 2)
        first = probs(j)
        second = probs(jnp.minimum(j + 1, nq - 1))
        _, (dk, dv, dck) = lax.fori_loop(j + 1, nq, step, (second, update(j, first, init, True)))
        dk_ref[...] = dk.astype(BF16)
        dv_ref[...] = dv.astype(BF16)
        dck_ref[...] = -jnp.where(m_lo, dck[0], dck[1])

        @pl.when(j == nq - 1)
        def _():
            dq_ref[...] = dqa_ref[...].astype(BF16)

    res = lambda p, j: (0, p)
    stat = pl.BlockSpec((nq, 128, TQ), lambda p, j: (0, p, 0))
    blk = pl.BlockSpec((TQ, 128), lambda p, j: (j, p))
    return _pallas(
        body, grid=(N_PAIR, nq), task=task,
        in_specs=[pl.BlockSpec((S, 128), res),
                  pl.BlockSpec((TQ, 128), lambda p, j: (j, N_PAIR + p)),
                  pl.BlockSpec((TQ, 128), lambda p, j: (j, 2 * N_PAIR + p)),
                  pl.BlockSpec((S, 128), res), stat, stat, blk, stat],
        out_specs=(pl.BlockSpec((S, 128), res), blk, blk,
                   pl.BlockSpec((None, nq, 8, TQ), lambda p, j: (p, 0, 0, 0)), blk),
        out_shape=(jax.ShapeDtypeStruct((S, D_GRP), BF16), jax.ShapeDtypeStruct((S, D_GRP), BF16),
                   jax.ShapeDtypeStruct((S, D_GRP), BF16), jax.ShapeDtypeStruct((N_PAIR, nq, 8, TQ), F32),
                   jax.ShapeDtypeStruct((S, D_GRP), F32)),
        scratch_shapes=[pltpu.VMEM((S, 128), F32)],
        name=name,
    )(proj, proj, proj, do, lse_t, delta_t, c_rep, c_t)


def _fox_gate_bwd_call(dc_rows, dck, fl_raw, b_pad, *, name):
    S = fl_raw.shape[0]
    nb = S // TQ

    def body(dc_ref, dck_ref, fl_ref, b_ref, dfl_ref, db_ref, carry_ref):
        i = pl.program_id(0)

        @pl.when(i == 0)
        def _():
            carry_ref[...] = jnp.zeros_like(carry_ref)
        lane = lax.broadcasted_iota(jnp.int32, (D_GRP, 128), 0)
        head = lax.broadcasted_iota(jnp.int32, (D_GRP, 128), 1)
        pick = ((lane == head * HEAD) & (head < D_GRP // HEAD)).astype(BF16)
        dc = dc_ref[...] + _dot3(dck_ref[...], pick).T
        rc = _dot3(dc, _tri(TQ, True)) + carry_ref[:, 0:1]
        carry_ref[...] = jnp.broadcast_to(rc[:, 0:1], carry_ref.shape)
        fl = fl_ref[...] + b_ref[...]
        dfl = rc.T * jax.nn.sigmoid(-fl)
        dfl_ref[...] = dfl.astype(BF16)
        _acc_add(db_ref, jnp.sum(dfl, axis=0, keepdims=True), i)

    rev = lambda i: (nb - 1 - i, 0)
    return pl.pallas_call(
        body, grid=(nb,),
        in_specs=[pl.BlockSpec((128, TQ), lambda i: (0, nb - 1 - i)), pl.BlockSpec((TQ, D_GRP), rev),
                  pl.BlockSpec((TQ, 128), rev), _vec_spec(128)],
        out_specs=(pl.BlockSpec((TQ, 128), rev), _acc_spec(128)),
        out_shape=(jax.ShapeDtypeStruct((S, 128), BF16), jax.ShapeDtypeStruct((8, 128), F32)),
        scratch_shapes=[pltpu.VMEM((128, 128), F32)],
        name=name, compiler_params=_cparams(),
    )(dc_rows, dck, fl_raw, b_pad)


def _chk_bias_call(g_rev, *, name):
    def body(g_ref, o_ref):
        x = jnp.broadcast_to(g_ref[...], (TQ, ROLL_W))
        rolled = pltpu.roll(x, ROLL_W - (TQ - 1), 1, stride=1, stride_axis=0)
        qc = lax.broadcasted_iota(jnp.int32, (TQ, WIN), 0) // CHUNK
        kc = lax.broadcasted_iota(jnp.int32, (TQ, WIN), 1) // CHUNK
        band = (kc >= qc) & (kc <= qc + LEFT)
        o_ref[...] = jnp.where(band, rolled[:, 0:WIN], NEG)

    return pl.pallas_call(
        body, grid=(8,),
        in_specs=[pl.BlockSpec((None, 1, ROLL_W), lambda h: (h, 0, 0))],
        out_specs=pl.BlockSpec((None, TQ, WIN), lambda h: (h, 0, 0)),
        out_shape=jax.ShapeDtypeStruct((8, TQ, WIN), F32), name=name, compiler_params=_cparams(),
    )(g_rev.reshape(8, 1, ROLL_W))


def _chk_scores(i, qm, kwin, bias, scale):
    s = _dot_nt(qm * scale, kwin) + bias
    kc = lax.broadcasted_iota(jnp.int32, (TQ, WIN), 1) // CHUNK
    return jnp.where(kc + i * (TQ // CHUNK) >= LEFT, s, NEG)


def _chk_fwd_call(proj, bias, *, name, task=None):
    S = proj.shape[0]
    nq = S // TQ
    scale = HEAD ** -0.5

    def body(q_ref, k_ref, v_ref, b_ref, o_ref, kp_ref, vp_ref):
        i = pl.program_id(1)

        @pl.when(i == 0)
        def _():
            kp_ref[0:PADK, :] = jnp.zeros((PADK, 128), BF16)
            vp_ref[0:PADK, :] = jnp.zeros((PADK, 128), BF16)
            kp_ref[PADK:PADK + S, :] = k_ref[...]
            vp_ref[PADK:PADK + S, :] = v_ref[...]
        masks = _lane_masks()
        q = q_ref[...]
        start = pl.multiple_of(i * TQ, TQ)
        kwin = kp_ref[pl.ds(start, WIN), :]
        vwin = vp_ref[pl.ds(start, WIN), :]
        ss = [_chk_scores(i, jnp.where(masks[h], q, jnp.zeros_like(q)), kwin, b_ref[h], scale) for h in range(2)]
        ps = []
        for s in ss:
            p = jnp.exp(s - jnp.max(s, axis=1, keepdims=True))
            ps.append((p / jnp.sum(p, axis=1, keepdims=True)).astype(BF16))
        o_ref[...] = (_dot(ps[0], jnp.where(masks[0], vwin, jnp.zeros_like(vwin)))
                      + _dot(ps[1], jnp.where(masks[1], vwin, jnp.zeros_like(vwin))))

    c0 = 3 * N_PAIR
    return _pallas(
        body, grid=(N_PAIR, nq), task=task,
        in_specs=[pl.BlockSpec((TQ, 128), lambda p, i: (i, c0 + p)),
                  pl.BlockSpec((S, 128), lambda p, i: (0, c0 + N_PAIR + p)),
                  pl.BlockSpec((S, 128), lambda p, i: (0, c0 + 2 * N_PAIR + p)),
                  pl.BlockSpec((2, TQ, WIN), lambda p, i: (p, 0, 0))],
        out_specs=pl.BlockSpec((TQ, 128), lambda p, i: (i, p)),
        out_shape=jax.ShapeDtypeStruct((S, D_GRP), F32),
        scratch_shapes=[pltpu.VMEM((S + PADK, 128), BF16), pltpu.VMEM((S + PADK, 128), BF16)],
        name=name,
    )(proj, proj, proj, bias)


def _chk_bwd_call(proj, do, bias, *, name, task=None):
    S = proj.shape[0]
    nq = S // TQ
    scale = HEAD ** -0.5

    def body(q_ref, k_ref, v_ref, do_ref, b_ref, dq_ref, dk_ref, dv_ref, dg_ref, kp_ref, vp_ref, dkp_ref, dvp_ref, db_ref):
        i = pl.program_id(1)

        @pl.when(i == 0)
        def _():
            kp_ref[0:PADK, :] = jnp.zeros((PADK, 128), BF16)
            vp_ref[0:PADK, :] = jnp.zeros((PADK, 128), BF16)
            kp_ref[PADK:PADK + S, :] = k_ref[...]
            vp_ref[PADK:PADK + S, :] = v_ref[...]
            dkp_ref[...] = jnp.zeros_like(dkp_ref)
            dvp_ref[...] = jnp.zeros_like(dvp_ref)
            db_ref[...] = jnp.zeros_like(db_ref)
        masks = _lane_masks()
        q = q_ref[...]
        dout = do_ref[...]
        start = pl.multiple_of(i * TQ, TQ)
        kwin = kp_ref[pl.ds(start, WIN), :]
        vwin = vp_ref[pl.ds(start, WIN), :]
        qm = [jnp.where(mk, q, jnp.zeros_like(q)) for mk in masks]
        dom = [jnp.where(mk, dout, jnp.zeros_like(dout)) for mk in masks]
        ss = [_chk_scores(i, qm[h], kwin, b_ref[h], scale) for h in range(2)]
        dps = [_dot_nt(dom[h], vwin) for h in range(2)]
        pbs, dsbs = [], []
        for h in range(2):
            p = jnp.exp(ss[h] - jnp.max(ss[h], axis=1, keepdims=True))
            p = p / jnp.sum(p, axis=1, keepdims=True)
            ds = p * (dps[h] - jnp.sum(p * dps[h], axis=1, keepdims=True))
            db_ref[h] += ds
            pbs.append(p.astype(BF16))
            dsbs.append((ds * scale).astype(BF16))
        dq_ref[...] = (_dot(dsbs[0], jnp.where(masks[0], kwin, jnp.zeros_like(kwin)))
                       + _dot(dsbs[1], jnp.where(masks[1], kwin, jnp.zeros_like(kwin)))).astype(BF16)
        dkp_ref[pl.ds(start, WIN), :] += _dot_tn(dsbs[0], qm[0]) + _dot_tn(dsbs[1], qm[1])
        dvp_ref[pl.ds(start, WIN), :] += _dot_tn(pbs[0], dom[0]) + _dot_tn(pbs[1], dom[1])

        @pl.when(i == nq - 1)
        def _():
            dk_ref[...] = dkp_ref[PADK:PADK + S, :].astype(BF16)
            dv_ref[...] = dvp_ref[PADK:PADK + S, :].astype(BF16)
            a = lax.broadcasted_iota(jnp.int32, (TQ, TQ), 0)
            b = lax.broadcasted_iota(jnp.int32, (TQ, TQ), 1)
            flip = jnp.where(a + b == TQ - 1, 1.0, 0.0).astype(BF16)
            e = lax.broadcasted_iota(jnp.int32, (1, ROLL_W), 1)
            dg_ref[...] = jnp.zeros_like(dg_ref)
            for h in range(2):
                rev = _dot3_l(flip, db_ref[h])
                wide = jnp.concatenate([rev, jnp.zeros((TQ, ROLL_W - WIN), F32)], axis=1)
                diag = pltpu.roll(wide, 0, 1, stride=1, stride_axis=0)
                dg = jnp.sum(diag, axis=0, keepdims=True)
                lo = jnp.sum(jnp.where(e <= 639, dg, 0.0), axis=1, keepdims=True)
                hi = jnp.sum(jnp.where(e >= 895, dg, 0.0), axis=1, keepdims=True)
                dg_ref[h:h + 1, :] = jnp.where(e == 639, lo, jnp.where(e == 895, hi, dg))

    c0 = 3 * N_PAIR
    res = lambda p, i: (0, p)
    return _pallas(
        body, grid=(N_PAIR, nq), task=task,
        in_specs=[pl.BlockSpec((TQ, 128), lambda p, i: (i, c0 + p)),
                  pl.BlockSpec((S, 128), lambda p, i: (0, c0 + N_PAIR + p)),
                  pl.BlockSpec((S, 128), lambda p, i: (0, c0 + 2 * N_PAIR + p)),
                  pl.BlockSpec((TQ, 128), lambda p, i: (i, p)),
                  pl.BlockSpec((2, TQ, WIN), lambda p, i: (p, 0, 0))],
        out_specs=(pl.BlockSpec((TQ, 128), lambda p, i: (i, p)), pl.BlockSpec((S, 128), res),
                   pl.BlockSpec((S, 128), res), pl.BlockSpec((None, 8, ROLL_W), lambda p, i: (p, 0, 0))),
        out_shape=(jax.ShapeDtypeStruct((S, D_GRP), BF16), jax.ShapeDtypeStruct((S, D_GRP), BF16),
                   jax.ShapeDtypeStruct((S, D_GRP), BF16), jax.ShapeDtypeStruct((N_PAIR, 8, ROLL_W), F32)),
        scratch_shapes=[pltpu.VMEM((S + PADK, 128), BF16), pltpu.VMEM((S + PADK, 128), BF16),
                        pltpu.VMEM((S + PADK, 128), F32), pltpu.VMEM((S + PADK, 128), F32),
                        pltpu.VMEM((2, TQ, WIN), F32)],
        name=name,
    )(proj, proj, proj, do, bias)


def _mem_fwd_call(q, k, v, *, name, tq=2048):
    S = q.shape[0]
    scale = MEM_HD ** -0.5

    def body(q_ref, k_ref, v_ref, o_ref):
        s = _dot_nt(q_ref[...] * scale, k_ref[...])
        p = jnp.exp(s - jnp.max(s, axis=1, keepdims=True))
        p = p / jnp.sum(p, axis=1, keepdims=True)
        o_ref[...] = _dot(p.astype(BF16), v_ref[...]).astype(BF16)

    return pl.pallas_call(
        body, grid=(MEM_HEADS, S // tq),
        in_specs=[pl.BlockSpec((tq, MEM_HD), lambda h, i: (i, h)),
                  pl.BlockSpec((N_MEM, MEM_HD), lambda h, i: (0, h)),
                  pl.BlockSpec((N_MEM, MEM_HD), lambda h, i: (0, h))],
        out_specs=pl.BlockSpec((tq, MEM_HD), lambda h, i: (i, h)),
        out_shape=jax.ShapeDtypeStruct((S, D), BF16), name=name, compiler_params=_cparams(),
    )(q, k, v)


def _mem_bwd_call(q, k, v, do, *, name, tq=2048):
    S = q.shape[0]
    n = S // tq
    scale = MEM_HD ** -0.5

    def body(q_ref, k_ref, v_ref, do_ref, dq_ref, dk_ref, dv_ref, dka_ref, dva_ref):
        i = pl.program_id(1)

        @pl.when(i == 0)
        def _():
            dka_ref[...] = jnp.zeros_like(dka_ref)
            dva_ref[...] = jnp.zeros_like(dva_ref)
        qb = q_ref[...]
        kb = k_ref[...]
        dob = do_ref[...]
        s = _dot_nt(qb * scale, kb)
        p = jnp.exp(s - jnp.max(s, axis=1, keepdims=True))
        p = p / jnp.sum(p, axis=1, keepdims=True)
        dp = _dot_nt(dob, v_ref[...])
        ds = p * (dp - jnp.sum(p * dp, axis=1, keepdims=True))
        dsb = (ds * scale).astype(BF16)
        dq_ref[...] = _dot(dsb, kb).astype(BF16)
        dka_ref[...] += _dot_tn(dsb, qb)
        dva_ref[...] += _dot_tn(p.astype(BF16), dob)

        @pl.when(i == n - 1)
        def _():
            dk_ref[...] = dka_ref[...].astype(BF16)
            dv_ref[...] = dva_ref[...].astype(BF16)

    kv = pl.BlockSpec((N_MEM, MEM_HD), lambda h, i: (0, h))
    qs = pl.BlockSpec((tq, MEM_HD), lambda h, i: (i, h))
    return pl.pallas_call(
        body, grid=(MEM_HEADS, n), in_specs=[qs, kv, kv, qs], out_specs=(qs, kv, kv),
        out_shape=(jax.ShapeDtypeStruct((S, D), BF16), jax.ShapeDtypeStruct((N_MEM, D), BF16),
                   jax.ShapeDtypeStruct((N_MEM, D), BF16)),
        scratch_shapes=[pltpu.VMEM((N_MEM, MEM_HD), F32), pltpu.VMEM((N_MEM, MEM_HD), F32)],
        name=name, compiler_params=_cparams(),
    )(q, k, v, do)


def _rel_table_to_g(rel):
    return jnp.concatenate([
        jnp.broadcast_to(rel[:, N_REL - 1:N_REL], (8, 640)),
        rel[:, 1:N_REL - 1][:, ::-1],
        jnp.broadcast_to(rel[:, 0:1], (8, 129)),
    ], axis=1)


def _g_to_rel_table(dg):
    return dg[:, 639:896][:, ::-1]


def _place():
    x, y, c = lax.axis_index("x"), lax.axis_index("y"), lax.axis_index("c")
    others = [(1 - x, y), (x, 1 - y), (1 - x, 1 - y)]
    return x, y, c, others


def _half(c, rows):
    hr = rows // 2
    return pl.ds(pl.multiple_of(c * hr, 16), hr)


def _dma_sems(*shape):
    return pltpu.SemaphoreType.DMA(shape)


def _cast_slabs_call(ws, chip_arr, *, name, tm=256, task=None):
    n = len(ws)
    cols = ws[0].shape[1]
    tiles = [w.shape[0] // tm for w in ws]
    steps = max(tiles)

    def body(chip_ref, *refs):
        i = pl.program_id(0)
        for k in range(n):
            def cast(k=k):
                refs[n + k][...] = refs[k][...].astype(BF16)
            if tiles[k] == steps:
                cast()
            else:
                pl.when(i < tiles[k])(cast)

    in_specs = [pl.BlockSpec((tm, cols), lambda i, chip, t=t: (jnp.minimum(i, t - 1), 0)) for t in tiles]
    out_specs = [pl.BlockSpec((None, tm, cols), lambda i, chip, t=t: (chip[0], jnp.minimum(i, t - 1), 0)) for t in tiles]
    out_shape = [jax.ShapeDtypeStruct((N_CHIP,) + w.shape, BF16) for w in ws]
    return _pallas(body, grid=(steps,), in_specs=in_specs, out_specs=out_specs, out_shape=out_shape, name=name,
                   task=task, prefetch=1)(chip_arr, *ws)


def _cast_slab_call(w, chip_arr, *, name, tm=256, pad_rows=0):
    rows, cols = w.shape
    if pad_rows:
        tm = rows
    tm = min(tm, rows)

    def body(chip_ref, w_ref, o_ref):
        o_ref[0:tm, :] = w_ref[...].astype(BF16)
        if pad_rows:
            o_ref[tm:tm + pad_rows, :] = jnp.zeros((pad_rows, cols), BF16)

    return pl.pallas_call(
        body,
        grid_spec=pltpu.PrefetchScalarGridSpec(
            num_scalar_prefetch=1, grid=(rows // tm,),
            in_specs=[pl.BlockSpec((tm, cols), lambda i, chip: (i, 0))],
            out_specs=pl.BlockSpec((None, tm + pad_rows, cols), lambda i, chip: (chip[0], i, 0))),
        out_shape=jax.ShapeDtypeStruct((N_CHIP, rows + pad_rows, cols), BF16), name=name,
        compiler_params=_cparams(),
    )(chip_arr, w)


def _ag_ici_task(gathered):
    n = len(gathered)

    def copies(ins, outs, sems):
        send_sems, recv_sems = sems
        x, y, c, others = _place()
        me = 2 * x + y
        for k in range(n):
            mine = _half(c, gathered[k].shape[1])
            for t, (ox, oy) in enumerate(others):
                yield [pltpu.make_async_remote_copy(
                    src_ref=ins[k].at[me, mine], dst_ref=outs[k].at[slab, mine],
                    send_sem=send_sems.at[k, t], recv_sem=recv_sems.at[k, t],
                    device_id=(ox, oy, c), device_id_type=MESH) for slab in (me, 2 * ox + oy)]

    def issue(ins, outs, sems):
        for outgoing, _ in copies(ins, outs, sems):
            outgoing.start()

    def drain(ins, outs, sems):
        for outgoing, incoming in copies(ins, outs, sems):
            incoming.wait_recv()
            outgoing.wait_send()

    return _Task(gathered, [jax.ShapeDtypeStruct(g.shape, g.dtype) for g in gathered],
                 [_dma_sems(n, 3), _dma_sems(n, 3)], issue, drain, aliases={k: k for k in range(n)})


def _ag_d2d_task(gathered):
    n = len(gathered)

    def copies(ins, outs, sems):
        send_sems, recv_sems = sems
        x, y, c, others = _place()
        for k in range(n):
            rows = gathered[k].shape[1]
            mine, theirs = _half(c, rows), _half(1 - c, rows)
            for t, (ox, oy) in enumerate(others):
                slab = 2 * ox + oy
                pair = [pltpu.make_async_remote_copy(
                    src_ref=ins[k].at[slab, half], dst_ref=outs[k].at[slab, half],
                    send_sem=send_sems.at[k, t], recv_sem=recv_sems.at[k, t],
                    device_id=(x, y, 1 - c), device_id_type=MESH) for half in (mine, theirs)]
                yield pair

    def issue(ins, outs, sems):
        for outgoing, _ in copies(ins, outs, sems):
            outgoing.start()

    def drain(ins, outs, sems):
        for outgoing, incoming in copies(ins, outs, sems):
            incoming.wait_recv()
            outgoing.wait_send()

    return _Task(gathered, [jax.ShapeDtypeStruct(g.shape, g.dtype) for g in gathered],
                 [_dma_sems(n, 3), _dma_sems(n, 3)], issue, drain, aliases={k: k for k in range(n)})


def _rs_pair_task(ds):
    n = len(ds)

    def copies(ins, outs, sems):
        send_sems, recv_sems = sems
        x, y, c, _ = _place()
        for k in range(n):
            yield pltpu.make_async_remote_copy(
                src_ref=ins[k].at[:, _half(1 - c, ds[k].shape[1])], dst_ref=outs[k],
                send_sem=send_sems.at[k], recv_sem=recv_sems.at[k],
                device_id=(x, y, 1 - c), device_id_type=MESH)

    def issue(ins, outs, sems):
        for cp in copies(ins, outs, sems):
            cp.start()

    def drain(ins, outs, sems):
        for cp in copies(ins, outs, sems):
            cp.wait()

    return _Task(ds, [jax.ShapeDtypeStruct((N_CHIP, d.shape[1] // 2, d.shape[2]), d.dtype) for d in ds],
                 [_dma_sems(n), _dma_sems(n)], issue, drain)


def _pair_add_call(d, r1, c_arr, *, name, tm=512):
    _, rows, cols = d.shape
    hr = rows // 2
    tm = tm if hr % tm == 0 else hr
    nb = hr // tm

    def body(c_ref, d_ref, r_ref, o_ref):
        o_ref[...] = (d_ref[...].astype(F32) + r_ref[...].astype(F32)).astype(BF16)

    return pl.pallas_call(
        body,
        grid_spec=pltpu.PrefetchScalarGridSpec(
            num_scalar_prefetch=1, grid=(N_CHIP, nb),
            in_specs=[pl.BlockSpec((None, tm, cols), lambda j, i, c: (j, c[0] * nb + i, 0)),
                      pl.BlockSpec((None, tm, cols), lambda j, i, c: (j, i, 0))],
            out_specs=pl.BlockSpec((None, tm, cols), lambda j, i, c: (j, i, 0))),
        out_shape=jax.ShapeDtypeStruct((N_CHIP, hr, cols), BF16), name=name, compiler_params=_cparams(),
    )(c_arr, d, r1)


def _rs_chip_task(ps):
    n = len(ps)

    def copies(ins, outs, sems):
        send_sems, recv_sems = sems
        x, y, c, others = _place()
        for k in range(n):
            for t, (ox, oy) in enumerate(others):
                yield pltpu.make_async_remote_copy(
                    src_ref=ins[k].at[2 * ox + oy], dst_ref=outs[k].at[t],
                    send_sem=send_sems.at[k, t], recv_sem=recv_sems.at[k, t],
                    device_id=(ox, oy, c), device_id_type=MESH)

    def issue(ins, outs, sems):
        for cp in copies(ins, outs, sems):
            cp.start()

    def drain(ins, outs, sems):
        for cp in copies(ins, outs, sems):
            cp.wait()

    return _Task(ps, [jax.ShapeDtypeStruct((3,) + p.shape[1:], p.dtype) for p in ps],
                 [_dma_sems(n, 3), _dma_sems(n, 3)], issue, drain)


def _chip_sum_call(p, r2, place_arr, *, name, tm=512, task=None):
    _, hr, cols = r2.shape
    tm = tm if hr % tm == 0 else hr
    nb = hr // tm

    def body(place_ref, p_ref, r_ref, o_ref):
        acc = p_ref[...].astype(F32)
        for j in range(3):
            acc = acc + r_ref[j].astype(F32)
        o_ref[...] = acc

    return _pallas(
        body, grid=(nb,), prefetch=1,
        in_specs=[pl.BlockSpec((None, tm, cols), lambda i, pc: (pc[0], i, 0)),
                  pl.BlockSpec((3, tm, cols), lambda i, pc: (0, i, 0))],
        out_specs=pl.BlockSpec((tm, cols), lambda i, pc: (pc[1] * nb + i, 0)),
        out_shape=jax.ShapeDtypeStruct((2 * hr, cols), F32), name=name, task=task,
    )(place_arr, p, r2)


def _rs_gather_task(gs):
    n = len(gs)

    def copies(ins, outs, sems):
        send_sems, recv_sems = sems
        x, y, c, _ = _place()
        for k in range(n):
            rows = gs[k].shape[0]
            mine, theirs = _half(c, rows), _half(1 - c, rows)
            yield [pltpu.make_async_remote_copy(
                src_ref=ins[k].at[mine], dst_ref=outs[k].at[half],
                send_sem=send_sems.at[k], recv_sem=recv_sems.at[k],
                device_id=(x, y, 1 - c), device_id_type=MESH) for half in (mine, theirs)]

    def issue(ins, outs, sems):
        for outgoing, _ in copies(ins, outs, sems):
            outgoing.start()

    def drain(ins, outs, sems):
        for outgoing, incoming in copies(ins, outs, sems):
            incoming.wait_recv()
            outgoing.wait_send()

    return _Task(gs, [jax.ShapeDtypeStruct(g.shape, g.dtype) for g in gs],
                 [_dma_sems(n), _dma_sems(n)], issue, drain, aliases={k: k for k in range(n)})


def _adamw(w, g, m, v):
    m = ADAM_B1 * m + (1.0 - ADAM_B1) * g
    v = ADAM_B2 * v + (1.0 - ADAM_B2) * jnp.square(g)
    m_hat = m / (1.0 - ADAM_B1 ** ADAM_STEP)
    v_hat = v / (1.0 - ADAM_B2 ** ADAM_STEP)
    delta = -ADAM_LR * (m_hat / (jnp.sqrt(v_hat) + ADAM_EPS) + ADAM_WD * w)
    return delta, m, v


def _adamw_call(items, *, name, tm=256, task=None):
    n = len(items)
    cols = items[0][0].shape[1]
    tiles = [it[0].shape[0] // tm for it in items]
    steps = max(tiles)

    def body(*refs):
        i = pl.program_id(0)
        ins, outs = refs[:4 * n], refs[4 * n:]
        for k in range(n):
            def update(k=k):
                g = ins[4 * k + 1][...]
                res = _adamw(ins[4 * k][...], g, ins[4 * k + 2][...], ins[4 * k + 3][...])
                outs[4 * k][...] = g
                for j in range(3):
                    outs[4 * k + 1 + j][...] = res[j]
            if tiles[k] == steps:
                update()
            else:
                pl.when(i < tiles[k])(update)

    in_specs, out_specs, out_shape, args = [], [], [], []
    for it, t in zip(items, tiles):
        spec = pl.BlockSpec((tm, cols), lambda i, t=t: (jnp.minimum(i, t - 1), 0))
        in_specs += [spec] * 4
        out_specs += [spec] * 4
        out_shape += [jax.ShapeDtypeStruct(it[0].shape, F32)] * 4
        args += list(it)
    res = _pallas(body, grid=(steps,), in_specs=in_specs, out_specs=out_specs, out_shape=out_shape,
                  name=name, task=task)(*args)
    outs, extra = res if task is not None else (res, None)
    grouped = [tuple(outs[4 * k:4 * k + 4]) for k in range(n)]
    return (grouped, extra) if task is not None else grouped


def _adamw_cols_call(w, g_pad, m, v, *, name, tn=512):
    rows, cols = w.shape

    def body(w_ref, g_ref, m_ref, v_ref, go_ref, d_ref, mo_ref, vo_ref):
        g = g_ref[0:rows, :]
        d, mn, vn = _adamw(w_ref[...], g, m_ref[...], v_ref[...])
        go_ref[...] = g
        d_ref[...] = d
        mo_ref[...] = mn
        vo_ref[...] = vn

    spec = pl.BlockSpec((rows, tn), lambda j: (0, j))
    gspec = pl.BlockSpec((g_pad.shape[0], tn), lambda j: (0, j))
    return _pallas(body, grid=(cols // tn,), in_specs=[spec, gspec, spec, spec], out_specs=(spec,) * 4,
                   out_shape=(jax.ShapeDtypeStruct((rows, cols), F32),) * 4, name=name)(w, g_pad, m, v)


N_DEV = 8
SMALL_ROWS = 24
SMALL_LAYOUT = {
    "g_mix_pre": (0, 0, 1, D), "g_mix_post": (1, 0, 1, D), "g_mem_kv": (2, 0, 1, D), "g_mem_pre": (3, 0, 1, D),
    "g_mem_post": (4, 0, 1, D), "g_ff_pre": (5, 0, 1, D), "g_ff_post": (6, 0, 1, D),
    "g_fox_out": (7, 0, 1, D_GRP), "g_chk_out": (7, D_GRP, 1, D_GRP), "b_fgt": (8, 0, 1, 8),
    "rel_bias": (16, 0, 8, N_REL),
}
SMALL = list(SMALL_LAYOUT)


LOSS_ROW = 9


def _small_reduce_call(grads, loss_blk, task, *, name):
    n = len(SMALL)
    t_in, t_out = len(task.arrays), len(task.out_shapes)

    def body(*refs):
        g_refs, loss_ref, tins = refs[:n], refs[n], refs[n + 1:n + 1 + t_in]
        p = n + 1 + t_in
        total_ref, loss_out, touts = refs[p], refs[p + 1], refs[p + 2:p + 2 + t_out]
        p += 2 + t_out
        mine, slots, send_sems, recv_sems = refs[p:p + 4]
        tsems = refs[p + 4:]
        task.issue(tins, touts, tsems)
        x, y, c, _ = _place()
        me = 4 * x + 2 * y + c
        mine[...] = jnp.zeros_like(mine)
        for k, name_k in enumerate(SMALL):
            r, l, nr, nl = SMALL_LAYOUT[name_k]
            mine[r:r + nr, l:l + nl] = g_refs[k][0:nr, 0:nl]
        mine[LOSS_ROW:LOSS_ROW + 1, 0:128] = loss_ref[0:1, :]
        slots[me] = mine[...]
        peers = [(dx, dy, dc) for dx in (0, 1) for dy in (0, 1) for dc in (0, 1)][1:]
        cps = []
        for t, (dx, dy, dc) in enumerate(peers):
            px, py, pc = (x + dx) % 2, (y + dy) % 2, (c + dc) % 2
            cps.append(pltpu.make_async_remote_copy(
                src_ref=mine, dst_ref=slots.at[me], send_sem=send_sems.at[t], recv_sem=recv_sems.at[t],
                device_id=(px, py, pc), device_id_type=MESH))
            cps[-1].start()
        for t, (dx, dy, dc) in enumerate(peers):
            px, py, pc = (x + dx) % 2, (y + dy) % 2, (c + dc) % 2
            pltpu.make_async_remote_copy(
                src_ref=mine, dst_ref=slots.at[4 * px + 2 * py + pc], send_sem=send_sems.at[t],
                recv_sem=recv_sems.at[t], device_id=(px, py, pc), device_id_type=MESH).wait_recv()
        for cp in cps:
            cp.wait_send()
        total = slots[0]
        for j in range(1, N_DEV):
            total = total + slots[j]
        total_ref[...] = total
        loss_out[...] = jnp.broadcast_to(total[LOSS_ROW:LOSS_ROW + 1, 0:128], loss_out.shape)
        task.drain(tins, touts, tsems)

    vm = pl.BlockSpec(memory_space=pltpu.VMEM)
    out_shape = [jax.ShapeDtypeStruct((SMALL_ROWS, D), F32), jax.ShapeDtypeStruct((8, 128), F32)] + list(task.out_shapes)
    res = pl.pallas_call(
        body, in_specs=[vm] * (n + 1) + [ANY] * t_in, out_specs=[vm] * 2 + [ANY] * t_out,
        out_shape=out_shape,
        scratch_shapes=[pltpu.VMEM((SMALL_ROWS, D), F32), pltpu.VMEM((N_DEV, SMALL_ROWS, D), F32),
                        _dma_sems(N_DEV - 1), _dma_sems(N_DEV - 1)] + list(task.sems),
        input_output_aliases={n + 1 + i: 2 + j for i, j in task.aliases.items()},
        name=name,
    )(*[grads[k] for k in SMALL], loss_blk, *task.arrays)
    return res[0], res[1], list(res[2:])


def _small_adamw_call(total, ws, ms, vs, *, name):
    n = len(SMALL)

    def body(*refs):
        total_ref = refs[0]
        w_refs, m_refs, v_refs = (refs[1 + j * n:1 + (j + 1) * n] for j in range(3))
        outs = refs[1 + 3 * n:]
        for k, name_k in enumerate(SMALL):
            r, l, nr, nl = SMALL_LAYOUT[name_k]
            g = total_ref[r:r + nr, l:l + nl]
            d, mn, vn = _adamw(w_refs[k][...], g, m_refs[k][...], v_refs[k][...])
            for j, val in enumerate((g, d, mn, vn)):
                outs[4 * k + j][...] = val

    vm = pl.BlockSpec(memory_space=pltpu.VMEM)
    res = pl.pallas_call(
        body, in_specs=[vm] * (3 * n + 1), out_specs=[vm] * (4 * n),
        out_shape=[jax.ShapeDtypeStruct(ws[k].shape, F32) for k in SMALL for _ in range(4)], name=name,
    )(total, *[d[k] for d in (ws, ms, vs) for k in SMALL])
    return {k: tuple(res[4 * i:4 * i + 4]) for i, k in enumerate(SMALL)}


WEIGHTS = ["w_in", "b_fgt", "rel_bias", "g_fox_out", "g_chk_out", "w_out", "g_mix_pre", "g_mix_post", "g_mem_kv",
           "w_mq", "w_mk", "w_mv", "w_mo", "g_mem_pre", "g_mem_post", "w_ff1", "w_ff2", "g_ff_pre", "g_ff_post"]
BIG = ["w_in", "w_out", "w_mq", "w_mk", "w_mv", "w_mo", "w_ff1", "w_ff2"]


IN_SHARD = D_IN // N_CHIP
IN_PAD = 800


IN_PIECES = [(0, 0, 770), (800, 770, 766), (1566, 3072, 4), (1600, 3076, 4), (1604, 1536, 766), (2400, 2302, 770)]
PAD_ZEROS = [(800 * j + IN_SHARD, IN_PAD - IN_SHARD) for j in range(N_CHIP)]
ALL_ZEROS = [(D_IN, D_ALL - D_IN)]


def _reorder_rows_call(src, to_all, *, name, tn=512):
    rows, cols = src.shape
    zeros = ALL_ZEROS if to_all else PAD_ZEROS

    def body(s_ref, o_ref):
        for pad0, all0, cnt in IN_PIECES:
            s0, d0 = (pad0, all0) if to_all else (all0, pad0)
            o_ref[d0:d0 + cnt, :] = s_ref[s0:s0 + cnt, :]
        for z0, cnt in zeros:
            o_ref[z0:z0 + cnt, :] = jnp.zeros((cnt, tn), src.dtype)

    spec = pl.BlockSpec((rows, tn), lambda j: (0, j))
    return _pallas(body, grid=(cols // tn,), in_specs=[spec], out_specs=spec,
                   out_shape=jax.ShapeDtypeStruct((rows, cols), src.dtype), name=name)(src)


def kernel(x, mem, w_in, b_fgt, rel_bias, g_fox_out, g_chk_out, w_out, g_mix_pre, g_mix_post, g_mem_kv, w_mq, w_mk, w_mv, w_mo, g_mem_pre, g_mem_post, w_ff1, w_ff2, g_ff_pre, g_ff_post, loss_target, m_w_in, m_b_fgt, m_rel_bias, m_g_fox_out, m_g_chk_out, m_w_out, m_g_mix_pre, m_g_mix_post, m_g_mem_kv, m_w_mq, m_w_mk, m_w_mv, m_w_mo, m_g_mem_pre, m_g_mem_post, m_w_ff1, m_w_ff2, m_g_ff_pre, m_g_ff_post, v_w_in, v_b_fgt, v_rel_bias, v_g_fox_out, v_g_chk_out, v_w_out, v_g_mix_pre, v_g_mix_post, v_g_mem_kv, v_w_mq, v_w_mk, v_w_mv, v_w_mo, v_g_mem_pre, v_g_mem_post, v_w_ff1, v_w_ff2, v_g_ff_pre, v_g_ff_post):
    w = dict(w_in=w_in, b_fgt=b_fgt, rel_bias=rel_bias, g_fox_out=g_fox_out, g_chk_out=g_chk_out, w_out=w_out,
             g_mix_pre=g_mix_pre, g_mix_post=g_mix_post, g_mem_kv=g_mem_kv, w_mq=w_mq, w_mk=w_mk, w_mv=w_mv,
             w_mo=w_mo, g_mem_pre=g_mem_pre, g_mem_post=g_mem_post, w_ff1=w_ff1, w_ff2=w_ff2, g_ff_pre=g_ff_pre,
             g_ff_post=g_ff_post)
    m = dict(w_in=m_w_in, b_fgt=m_b_fgt, rel_bias=m_rel_bias, g_fox_out=m_g_fox_out, g_chk_out=m_g_chk_out,
             w_out=m_w_out, g_mix_pre=m_g_mix_pre, g_mix_post=m_g_mix_post, g_mem_kv=m_g_mem_kv, w_mq=m_w_mq,
             w_mk=m_w_mk, w_mv=m_w_mv, w_mo=m_w_mo, g_mem_pre=m_g_mem_pre, g_mem_post=m_g_mem_post,
             w_ff1=m_w_ff1, w_ff2=m_w_ff2, g_ff_pre=m_g_ff_pre, g_ff_post=m_g_ff_post)
    v = dict(w_in=v_w_in, b_fgt=v_b_fgt, rel_bias=v_rel_bias, g_fox_out=v_g_fox_out, g_chk_out=v_g_chk_out,
             w_out=v_w_out, g_mix_pre=v_g_mix_pre, g_mix_post=v_g_mix_post, g_mem_kv=v_g_mem_kv, w_mq=v_w_mq,
             w_mk=v_w_mk, w_mv=v_w_mv, w_mo=v_w_mo, g_mem_pre=v_g_mem_pre, g_mem_post=v_g_mem_post,
             w_ff1=v_w_ff1, w_ff2=v_w_ff2, g_ff_pre=v_g_ff_pre, g_ff_post=v_g_ff_post)

    def rows(d, k):
        return d[k][0] if k == "rel_bias" else d[k]

    xs, mems, target = x[0], mem[0], loss_target[0]
    S = xs.shape[0]
    sp = {k: rows(w, k) for k in SMALL}
    b_pad = jnp.pad(sp["b_fgt"], ((0, 0), (0, 120)))
    chip = 2 * lax.axis_index("x") + lax.axis_index("y")
    chip_arr = jnp.reshape(chip, (1,)).astype(jnp.int32)
    c_arr = jnp.reshape(lax.axis_index("c"), (1,)).astype(jnp.int32)
    place_arr = jnp.concatenate([chip_arr, c_arr])
    w_in_t, m_in_t, v_in_t = w["w_in"][0].T, m["w_in"][0].T, v["w_in"][0].T
    slab = {"w_in": _cast_slab_call(w_in_t, chip_arr, name="cast_w_in", pad_rows=IN_PAD - IN_SHARD)}

    def gather_ici(names):
        return _ag_ici_task([slab[k] for k in names])

    def pair_add(k, d, r1):
        return _pair_add_call(d, r1, c_arr, name="rs_pair_add_" + k)

    rest, (g_in,) = _cast_slabs_call([w[k][0] for k in BIG[1:]], chip_arr, name="cast_rest",
                                     task=gather_ici(["w_in"]))
    slab.update(zip(BIG[1:], rest))
    h1, (g_in,) = _rms_fwd_call(xs, sp["g_mix_pre"], name="rms_mix_pre", task=_ag_d2d_task([g_in]))
    w_all_t = _reorder_rows_call(g_in.reshape(N_CHIP * IN_PAD, D), True, name="w_in_rows")
    proj, (g_out, g_mq) = _mm_nt(h1, w_all_t, "plain", rows=(0, 3072), name="mm_proj",
                                 task=gather_ici(["w_out", "w_mq"]))
    fl_raw = _mm_nt(h1, w_all_t, "plain", rows=(3072, 128), name="mm_gate", out_dtype=F32, tn=128)
    c_rep, c_t = _fox_prep_call(fl_raw, b_pad, name="fox_prep")
    bias = _chk_bias_call(_rel_table_to_g(sp["rel_bias"]), name="chk_bias")
    mid = ["w_mk", "w_mv", "w_mo", "w_ff1"]
    (yf, lse), got = _fox_fwd_call(proj, c_rep, c_t, name="fox_fwd",
                                   task=_merge_tasks([gather_ici(mid), _ag_d2d_task([g_out, g_mq])]))
    g_mid, (g_out, g_mq) = got[:4], got[4:]
    yc, got = _chk_fwd_call(proj, bias, name="chk_fwd",
                            task=_merge_tasks([gather_ici(["w_ff2"]), _ag_d2d_task(g_mid)]))
    g_ff2, (g_mk, g_mv, g_mo, g_ff1) = got[0], got[1:]
    yn = _mix_norm_fwd_call(yf, yc, sp["g_fox_out"], sp["g_chk_out"], name="mix_norm_fwd")
    z, (g_ff2,) = _mm_nn(yn, g_out, "rows", name="mm_out", out_dtype=F32, task=_ag_d2d_task([g_ff2]))
    x1, h2 = _post_pre_call(xs, z, sp["g_mix_post"], sp["g_mem_pre"], name="post_mix")
    memn = _rms_fwd_call(mems, sp["g_mem_kv"], name="rms_mem_kv")
    q2 = _mm_nn(h2, g_mq, "rows", name="mm_mq")
    k2 = _mm_nn(memn, g_mk, "rows", name="mm_mk")
    v2 = _mm_nn(memn, g_mv, "rows", name="mm_mv")
    o2 = _mem_fwd_call(q2, k2, v2, name="mem_fwd")
    y2 = _mm_nn(o2, g_mo, "rows", name="mm_mo", out_dtype=F32)
    x2, h3 = _post_pre_call(x1, y2, sp["g_mem_post"], sp["g_ff_pre"], name="post_mem")
    act, relu = _mm_nn(h3, g_ff1, "cols", name="mm_ff1", epi="relu2")
    y3 = _mm_nn(act, g_ff2, "rows", name="mm_ff2", out_dtype=F32, tm=1024)
    loss_blk, dx3, dy3, dg_ff_post = _final_call(x2, y3, sp["g_ff_post"], target, name="final")

    d_ff2 = _mm_tn(act, dy3, name="mm_dff2", tk=512, tn=1024).reshape(N_CHIP, D_FF // N_CHIP, D)
    du, (r1,) = _mm_nt(dy3, g_ff2, "rows", name="mm_du", mul2r=relu, task=_rs_pair_task([d_ff2]))
    p_ff2 = pair_add("w_ff2", d_ff2, r1)
    d_ff1 = _mm_tn(h3, du, name="mm_dff1", cols4=True)
    dh3, (r1,) = _mm_nt(du, g_ff1, "cols", name="mm_dh3", out_dtype=F32, tm=1024, task=_rs_pair_task([d_ff1]))
    p_ff1 = pair_add("w_ff1", d_ff1, r1)
    dx2, dy2, dg_ff_pre, dg_mem_post = _bwd_mid_call(dx3, x2, dh3, sp["g_ff_pre"], y2, sp["g_mem_post"], name="bwd_ff")
    d_mo = _mm_tn(o2, dy2, name="mm_dmo").reshape(N_CHIP, D // N_CHIP, D)
    do2 = _mm_nt(dy2, g_mo, "rows", name="mm_do2")
    dq2, dk2, dv2 = _mem_bwd_call(q2, k2, v2, do2, name="mem_bwd")
    d_mq = _mm_tn(h2, dq2, name="mm_dmq").reshape(N_CHIP, D // N_CHIP, D)
    dh2 = _mm_nt(dq2, g_mq, "rows", name="mm_dh2", out_dtype=F32)
    d_mk = _mm_tn(memn, dk2, name="mm_dmk").reshape(N_CHIP, D // N_CHIP, D)
    d_mv = _mm_tn(memn, dv2, name="mm_dmv").reshape(N_CHIP, D // N_CHIP, D)
    dmn_k = _mm_nt(dk2, g_mk, "rows", name="mm_dmemk", out_dtype=F32)
    dmn_v = _mm_nt(dv2, g_mv, "rows", name="mm_dmemv", out_dtype=F32)
    dg_mem_kv = _gain_grad_call(mems, sp["g_mem_kv"], dmn_k, dmn_v, name="gain_mem_kv")
    dx1, dz, dg_mem_pre, dg_mix_post = _bwd_mid_call(dx2, x1, dh2, sp["g_mem_pre"], z, sp["g_mix_post"], name="bwd_mem")
    d_out = _mm_tn(yn, dz, name="mm_dout").reshape(N_CHIP, D // N_CHIP, D)
    late = ["w_mo", "w_mq", "w_mk", "w_mv", "w_out"]
    d_late = [d_mo, d_mq, d_mk, d_mv, d_out]
    dyn, r1_late = _mm_nt(dz, g_out, "rows", name="mm_dyn", out_dtype=F32, task=_rs_pair_task(d_late))
    p_late = [pair_add(k, d, r1) for k, d, r1 in zip(late, d_late, r1_late)]
    dof, doc, delta, dg_fox, dg_chk = _mix_norm_bwd_call(dyn, yf, yc, sp["g_fox_out"], sp["g_chk_out"], name="mix_norm_bwd")
    (dqf, dkf, dvf, dcq, dck), r2_ff = _fox_bwd_call(proj, dof, lse, delta, c_rep, c_t, name="fox_bwd",
                                                      task=_rs_chip_task([p_ff2, p_ff1]))
    (dqc, dkc, dvc, dgrev), r2_late = _chk_bwd_call(proj, doc, bias, name="chk_bwd", task=_rs_chip_task(p_late))
    first = ["w_ff2", "w_ff1"] + late
    dc8 = dcq[:, :, 0:2, :].transpose(0, 2, 1, 3).reshape(8, S)
    dc_rows = jnp.concatenate([dc8, jnp.zeros((120, S), F32)], axis=0)
    dfl, db_fgt = _fox_gate_bwd_call(dc_rows, dck, fl_raw, b_pad, name="fox_gate_bwd")
    dproj = jnp.concatenate([dqf, dkf, dvf, dqc, dkc, dvc, dfl], axis=1)
    d_all_t = _mm_tn(dproj, h1, name="mm_dwin", tk=640, tn=1024)
    d_in = _reorder_rows_call(d_all_t, False, name="d_in_rows").reshape(N_CHIP, IN_PAD, D)
    f_ff2, (r1,) = _chip_sum_call(p_ff2, r2_ff[0], place_arr, name="rs_chip_sum_w_ff2", task=_rs_pair_task([d_in]))
    f_first = [f_ff2] + [_chip_sum_call(p, r, place_arr, name="rs_chip_sum_" + k)
                         for k, p, r in zip(first[1:], [p_ff1] + p_late, r2_ff[1:] + r2_late)]
    p_in = pair_add("w_in", d_in, r1)
    dh1, got = _mm_nn(dproj, w_all_t, "plain", name="mm_dh1", out_dtype=F32, tm=1024,
                      task=_merge_tasks([_rs_chip_task([p_in]), _rs_gather_task(f_first)]))
    r2_in, grads = got[0], dict(zip(first, got[1:]))
    f_in = _chip_sum_call(p_in, r2_in, place_arr, name="rs_chip_sum_w_in")
    delta_w, new_m, new_v = {}, {}, {}

    def adamw_items(names):
        return [(w[k][0], grads[k], m[k][0], v[k][0]) for k in names]

    upd_late = _adamw_call(adamw_items(late), name="adamw_late", tm=64)
    upd_ff = _adamw_call(adamw_items(first[:2]), name="adamw_ff")
    for k, res in zip(late + first[:2], upd_late + upd_ff):
        grads[k], delta_w[k], new_m[k], new_v[k] = res
    grad_x, dg_mix_pre = _bwd_last_call(dx1, xs, dh1, sp["g_mix_pre"], name="bwd_mix")

    small_g = {"g_mix_pre": dg_mix_pre, "g_mix_post": dg_mix_post, "g_mem_kv": dg_mem_kv, "g_mem_pre": dg_mem_pre,
               "g_mem_post": dg_mem_post, "g_ff_pre": dg_ff_pre, "g_ff_post": dg_ff_post, "g_fox_out": dg_fox,
               "g_chk_out": dg_chk, "b_fgt": db_fgt,
               "rel_bias": _g_to_rel_table(dgrev[:, 0:2, :].reshape(8, ROLL_W))}
    small_sum, loss_out, (g_w_in,) = _small_reduce_call(small_g, loss_blk, _rs_gather_task([f_in]),
                                                       name="small_allreduce")
    small = _small_adamw_call(small_sum, sp, {k: rows(m, k) for k in SMALL}, {k: rows(v, k) for k in SMALL},
                              name="small_adamw")
    loss = loss_out[0, 0]
    res = _adamw_cols_call(w_in_t, g_w_in, m_in_t, v_in_t, name="adamw_w_in")
    grads["w_in"], delta_w["w_in"], new_m["w_in"], new_v["w_in"] = (a.T for a in res)
    for k in SMALL:
        vals = small[k]
        if k == "rel_bias":
            vals = tuple(a[None] for a in vals)
        grads[k], delta_w[k], new_m[k], new_v[k] = vals

    def out(d, k):
        return d[k][None] if k in BIG else d[k]

    return (loss, grad_x[None], *[out(grads, k) for k in WEIGHTS], *[out(delta_w, k) for k in WEIGHTS],
            *[out(new_m, k) for k in WEIGHTS], *[out(new_v, k) for k in WEIGHTS])
```

```python
import functools

import jax
import jax.numpy as jnp
from jax import lax
from jax.experimental import pallas as pl
from jax.experimental.pallas import tpu as pltpu

F32 = jnp.float32
BF16 = jnp.bfloat16

D = 1024
HEAD = 64
N_PAIR = 4
D_GRP = 512
CHUNK = 64
LEFT = 8
MAX_REL = 128
N_REL = 2 * MAX_REL + 1
N_MEM = 256
MEM_HEADS = 4
MEM_HD = 256
D_FF = 4096
D_IN = 3080
D_ALL = 3200
EPS = 1e-6
TQ = 256
WIN = (LEFT + TQ // CHUNK) * CHUNK
PADK = LEFT * CHUNK
ROLL_W = 1024
NEG = -1e30
N_CHIP = 4
VMEM_LIMIT = 48 * 1024 * 1024

ADAM_LR = 0.001
ADAM_B1 = 0.9
ADAM_B2 = 0.999
ADAM_EPS = 1e-08
ADAM_WD = 0.01
ADAM_STEP = 10

MESH = pl.DeviceIdType.MESH


def _cparams():
    return pltpu.CompilerParams(vmem_limit_bytes=VMEM_LIMIT)


ANY = pl.BlockSpec(memory_space=pl.ANY)


class _Task:
    def __init__(self, arrays, out_shapes, sems, issue, drain, aliases=None):
        self.arrays, self.out_shapes, self.sems = list(arrays), list(out_shapes), list(sems)
        self.issue, self.drain, self.aliases = issue, drain, dict(aliases or {})


def _merge_tasks(tasks):
    tasks = [t for t in tasks if t is not None]
    if len(tasks) == 1:
        return tasks[0]
    cuts, a, o, s = [], 0, 0, 0
    aliases = {}
    for t in tasks:
        cuts.append((a, o, s))
        aliases.update({a + i: o + j for i, j in t.aliases.items()})
        a, o, s = a + len(t.arrays), o + len(t.out_shapes), s + len(t.sems)

    def part(fn_name):
        def run(ins, outs, sems):
            for t, (a0, o0, s0) in zip(tasks, cuts):
                getattr(t, fn_name)(ins[a0:a0 + len(t.arrays)], outs[o0:o0 + len(t.out_shapes)],
                                    sems[s0:s0 + len(t.sems)])
        return run

    return _Task([x for t in tasks for x in t.arrays], [x for t in tasks for x in t.out_shapes],
                 [x for t in tasks for x in t.sems], part("issue"), part("drain"), aliases)


def _pallas(body, *, grid, in_specs, out_specs, out_shape, name, scratch_shapes=(), task=None, prefetch=0):
    def make(kernel, i_specs, o_specs, o_shape, scratch, aliases):
        if prefetch:
            spec = pltpu.PrefetchScalarGridSpec(num_scalar_prefetch=prefetch, grid=grid, in_specs=i_specs,
                                                out_specs=o_specs, scratch_shapes=scratch)
            return pl.pallas_call(kernel, grid_spec=spec, out_shape=o_shape, input_output_aliases=aliases,
                                  name=name, compiler_params=_cparams())
        return pl.pallas_call(kernel, grid=grid, in_specs=i_specs, out_specs=o_specs, out_shape=o_shape,
                              scratch_shapes=scratch, input_output_aliases=aliases, name=name,
                              compiler_params=_cparams())

    if task is None:
        return make(body, list(in_specs), out_specs, out_shape, list(scratch_shapes), {})
    single = not isinstance(out_shape, (tuple, list))
    o_shapes = [out_shape] if single else list(out_shape)
    o_specs = [out_specs] if single else list(out_specs)
    n_in, n_out, n_scr = len(in_specs), len(o_shapes), len(scratch_shapes)
    t_in, t_out = len(task.arrays), len(task.out_shapes)

    def carried(*refs):
        cut = [prefetch, n_in, t_in, n_out, t_out, n_scr]
        parts, p = [], 0
        for c in cut:
            parts.append(refs[p:p + c])
            p += c
        scalars, ins, tins, outs, touts, scr = parts
        tsems = refs[p:]
        ids = [pl.program_id(a) for a in range(len(grid))]
        first = functools.reduce(jnp.logical_and, [i == 0 for i in ids])
        last = functools.reduce(jnp.logical_and, [i == g - 1 for i, g in zip(ids, grid)])

        @pl.when(first)
        def _():
            task.issue(tins, touts, tsems)
        body(*scalars, *ins, *outs, *scr)

        @pl.when(last)
        def _():
            task.drain(tins, touts, tsems)

    call = make(carried, list(in_specs) + [ANY] * t_in, o_specs + [ANY] * t_out,
                o_shapes + list(task.out_shapes), list(scratch_shapes) + list(task.sems),
                {prefetch + n_in + i: n_out + j for i, j in task.aliases.items()})

    def run(*args):
        res = call(*args, *task.arrays)
        outs = res[:n_out]
        return (outs[0] if single else tuple(outs)), list(res[n_out:])

    return run


def _dot(a, b):
    return jnp.dot(a, b, preferred_element_type=F32)


def _dot_nt(a, b):
    return lax.dot_general(a, b, (((1,), (1,)), ((), ())), preferred_element_type=F32)


def _dot_tn(a, b):
    return lax.dot_general(a, b, (((0,), (0,)), ((), ())), preferred_element_type=F32)


def _split3(x):
    hi = x.astype(BF16)
    r1 = x - hi.astype(F32)
    mid = r1.astype(BF16)
    lo = (r1 - mid.astype(F32)).astype(BF16)
    return hi, mid, lo


def _dot3(x, m01):
    hi, mid, lo = _split3(x)
    return _dot(hi, m01) + _dot(mid, m01) + _dot(lo, m01)


def _dot3_l(m01, x):
    hi, mid, lo = _split3(x)
    return _dot(m01, hi) + _dot(m01, mid) + _dot(m01, lo)


def _mm_nn(a, b, kind, *, name, out_dtype=BF16, tm=2048, tn=512, epi=None, task=None):
    M, K = a.shape
    if kind == "plain":
        N = b.shape[1]
        b_spec = pl.BlockSpec((K, tn), lambda m, n: (0, n))
    elif kind == "rows":
        N = b.shape[2]
        b_spec = pl.BlockSpec((N_CHIP, K // N_CHIP, tn), lambda m, n: (0, 0, n))
    else:
        nq = b.shape[2]
        N = N_CHIP * nq
        per = nq // tn
        b_spec = pl.BlockSpec((None, K, tn), lambda m, n: (n // per, 0, n % per))
    tm = min(tm, M)
    kq = K // N_CHIP

    def body(a_ref, b_ref, *o_refs):
        if kind == "rows":
            acc = _dot(a_ref[:, 0:kq], b_ref[0])
            for j in range(1, N_CHIP):
                acc += _dot(a_ref[:, j * kq:(j + 1) * kq], b_ref[j])
        else:
            acc = _dot(a_ref[...], b_ref[...])
        if epi == "relu2":
            r = jnp.maximum(acc, 0.0)
            o_refs[0][...] = (r * r).astype(BF16)
            o_refs[1][...] = r.astype(BF16)
        else:
            o_refs[0][...] = acc.astype(out_dtype)

    o_spec = pl.BlockSpec((tm, tn), lambda m, n: (m, n))
    if epi == "relu2":
        out_shape = (jax.ShapeDtypeStruct((M, N), BF16), jax.ShapeDtypeStruct((M, N), BF16))
        out_specs = (o_spec, o_spec)
    else:
        out_shape = jax.ShapeDtypeStruct((M, N), out_dtype)
        out_specs = o_spec
    return _pallas(
        body, grid=(M // tm, N // tn),
        in_specs=[pl.BlockSpec((tm, K), lambda m, n: (m, 0)), b_spec],
        out_specs=out_specs, out_shape=out_shape, name=name, task=task,
    )(a, b)


def _mm_nt(a, b, kind, *, name, out_dtype=BF16, tm=2048, tn=512, mul2r=None, task=None, rows=None):
    M, K = a.shape
    if kind == "plain":
        first, N = rows if rows is not None else (0, b.shape[0])
        n0 = first // tn
        b_spec = pl.BlockSpec((tn, K), lambda m, n: (n0 + n, 0))
    elif kind == "rows":
        nq = b.shape[1]
        N = N_CHIP * nq
        tn = min(tn, nq)
        per = nq // tn
        b_spec = pl.BlockSpec((None, tn, K), lambda m, n: (n // per, n % per, 0))
    else:
        N = b.shape[1]
        b_spec = pl.BlockSpec((N_CHIP, tn, K // N_CHIP), lambda m, n: (0, n, 0))
    tm = min(tm, M)
    kq = K // N_CHIP

    def body(a_ref, b_ref, *rest):
        o_ref = rest[-1]
        if kind == "cols":
            acc = _dot_nt(a_ref[:, 0:kq], b_ref[0])
            for j in range(1, N_CHIP):
                acc += _dot_nt(a_ref[:, j * kq:(j + 1) * kq], b_ref[j])
        else:
            acc = _dot_nt(a_ref[...], b_ref[...])
        if mul2r is not None:
            acc = acc * (2.0 * rest[0][...].astype(F32))
        o_ref[...] = acc.astype(out_dtype)

    in_specs = [pl.BlockSpec((tm, K), lambda m, n: (m, 0)), b_spec]
    args = [a, b]
    if mul2r is not None:
        in_specs.append(pl.BlockSpec((tm, tn), lambda m, n: (m, n)))
        args.append(mul2r)
    return _pallas(
        body, grid=(M // tm, N // tn), in_specs=in_specs,
        out_specs=pl.BlockSpec((tm, tn), lambda m, n: (m, n)),
        out_shape=jax.ShapeDtypeStruct((M, N), out_dtype), name=name, task=task,
    )(*args)


def _mm_tn(a, b, *, name, out_dtype=BF16, tk=1024, tn=512, cols4=False, task=None):
    M, K1 = a.shape
    N = b.shape[1]
    tk = min(tk, K1)
    tn = min(tn, N)

    def body(a_ref, b_ref, o_ref):
        o_ref[...] = _dot_tn(a_ref[...], b_ref[...]).astype(out_dtype)

    if cols4:
        per = (N // N_CHIP) // tn
        out_shape = jax.ShapeDtypeStruct((N_CHIP, K1, N // N_CHIP), out_dtype)
        o_spec = pl.BlockSpec((None, tk, tn), lambda k, n: (n // per, k, n % per))
    else:
        out_shape = jax.ShapeDtypeStruct((K1, N), out_dtype)
        o_spec = pl.BlockSpec((tk, tn), lambda k, n: (k, n))
    return _pallas(
        body, grid=(K1 // tk, N // tn),
        in_specs=[pl.BlockSpec((M, tk), lambda k, n: (0, k)), pl.BlockSpec((M, tn), lambda k, n: (0, n))],
        out_specs=o_spec, out_shape=out_shape, name=name, task=task,
    )(a, b)


def _rms(x, g):
    r = lax.rsqrt(jnp.mean(x * x, axis=-1, keepdims=True) + EPS)
    return x * r * g


def _rms_bwd(x, g, dy):
    r = lax.rsqrt(jnp.mean(x * x, axis=-1, keepdims=True) + EPS)
    xh = x * r
    dg = jnp.sum(dy * xh, axis=0, keepdims=True)
    dxh = dy * g
    dx = r * (dxh - xh * jnp.mean(dxh * xh, axis=-1, keepdims=True))
    return dx, dg


def _row_spec(tm, n):
    return pl.BlockSpec((tm, n), lambda i: (i, 0))


def _vec_spec(n):
    return pl.BlockSpec((1, n), lambda i: (0, 0))


def _acc_spec(n):
    return pl.BlockSpec((8, n), lambda i: (0, 0))


def _acc_add(ref, row, i):
    @pl.when(i == 0)
    def _():
        ref[...] = jnp.zeros_like(ref)
    ref[0:1, :] += row


def _rms_fwd_call(x, g, *, name, tm=1024, task=None):
    M, n = x.shape
    tm = min(tm, M)

    def body(x_ref, g_ref, h_ref):
        h_ref[...] = _rms(x_ref[...], g_ref[...]).astype(BF16)

    return _pallas(
        body, grid=(M // tm,), in_specs=[_row_spec(tm, n), _vec_spec(n)], out_specs=_row_spec(tm, n),
        out_shape=jax.ShapeDtypeStruct((M, n), BF16), name=name, task=task,
    )(x, g)


def _post_pre_call(xres, z, g_post, g_pre, *, name, tm=1024):
    M, n = xres.shape

    def body(x_ref, z_ref, gp_ref, gn_ref, xo_ref, h_ref):
        xn = x_ref[...] + _rms(z_ref[...], gp_ref[...])
        xo_ref[...] = xn
        h_ref[...] = _rms(xn, gn_ref[...]).astype(BF16)

    return pl.pallas_call(
        body, grid=(M // tm,),
        in_specs=[_row_spec(tm, n), _row_spec(tm, n), _vec_spec(n), _vec_spec(n)],
        out_specs=(_row_spec(tm, n), _row_spec(tm, n)),
        out_shape=(jax.ShapeDtypeStruct((M, n), F32), jax.ShapeDtypeStruct((M, n), BF16)),
        name=name, compiler_params=_cparams(),
    )(xres, z, g_post, g_pre)


def _final_call(x2, y3, g_post, target, *, name, tm=512):
    M, n = x2.shape

    def body(x_ref, y_ref, g_ref, t_ref, loss_ref, dx_ref, dy_ref, dg_ref):
        i = pl.program_id(0)
        y = y_ref[...]
        g = g_ref[...]
        diff = x_ref[...] + _rms(y, g) - t_ref[...]
        part = 0.5 * jnp.sum(jnp.sum(diff * diff, axis=1, keepdims=True), axis=0, keepdims=True) / n

        @pl.when(i == 0)
        def _():
            loss_ref[...] = jnp.zeros_like(loss_ref)
        loss_ref[...] += jnp.broadcast_to(part, loss_ref.shape)
        dx = diff / n
        dx_ref[...] = dx
        dy, dg = _rms_bwd(y, g, dx)
        dy_ref[...] = dy.astype(BF16)
        _acc_add(dg_ref, dg, i)

    return pl.pallas_call(
        body, grid=(M // tm,),
        in_specs=[_row_spec(tm, n), _row_spec(tm, n), _vec_spec(n), _row_spec(tm, n)],
        out_specs=(pl.BlockSpec((8, 128), lambda i: (0, 0)), _row_spec(tm, n), _row_spec(tm, n), _acc_spec(n)),
        out_shape=(jax.ShapeDtypeStruct((8, 128), F32), jax.ShapeDtypeStruct((M, n), F32),
                   jax.ShapeDtypeStruct((M, n), BF16), jax.ShapeDtypeStruct((8, n), F32)),
        name=name, compiler_params=_cparams(),
    )(x2, y3, g_post, target)


def _bwd_mid_call(dx_in, x, dh, g_pre, y, g_post, *, name, tm=512):
    M, n = x.shape

    def body(dxi_ref, x_ref, dh_ref, gpre_ref, y_ref, gpost_ref, dx_ref, dy_ref, dgpre_ref, dgpost_ref):
        i = pl.program_id(0)
        d1, dg1 = _rms_bwd(x_ref[...], gpre_ref[...], dh_ref[...])
        dx = dxi_ref[...] + d1
        dx_ref[...] = dx
        dy, dg2 = _rms_bwd(y_ref[...], gpost_ref[...], dx)
        dy_ref[...] = dy.astype(BF16)
        _acc_add(dgpre_ref, dg1, i)
        _acc_add(dgpost_ref, dg2, i)

    return pl.pallas_call(
        body, grid=(M // tm,),
        in_specs=[_row_spec(tm, n), _row_spec(tm, n), _row_spec(tm, n), _vec_spec(n), _row_spec(tm, n), _vec_spec(n)],
        out_specs=(_row_spec(tm, n), _row_spec(tm, n), _acc_spec(n), _acc_spec(n)),
        out_shape=(jax.ShapeDtypeStruct((M, n), F32), jax.ShapeDtypeStruct((M, n), BF16),
                   jax.ShapeDtypeStruct((8, n), F32), jax.ShapeDtypeStruct((8, n), F32)),
        name=name, compiler_params=_cparams(),
    )(dx_in, x, dh, g_pre, y, g_post)


def _bwd_last_call(dx_in, x, dh, g_pre, *, name, tm=1024, task=None):
    M, n = x.shape

    def body(dxi_ref, x_ref, dh_ref, g_ref, dx_ref, dg_ref):
        i = pl.program_id(0)
        d1, dg1 = _rms_bwd(x_ref[...], g_ref[...], dh_ref[...])
        dx_ref[...] = dxi_ref[...] + d1
        _acc_add(dg_ref, dg1, i)

    return _pallas(
        body, grid=(M // tm,),
        in_specs=[_row_spec(tm, n), _row_spec(tm, n), _row_spec(tm, n), _vec_spec(n)],
        out_specs=(_row_spec(tm, n), _acc_spec(n)),
        out_shape=(jax.ShapeDtypeStruct((M, n), F32), jax.ShapeDtypeStruct((8, n), F32)),
        name=name, task=task,
    )(dx_in, x, dh, g_pre)


def _gain_grad_call(x, g, dy_a, dy_b, *, name):
    M, n = x.shape

    def body(x_ref, g_ref, a_ref, b_ref, dg_ref):
        _, dg = _rms_bwd(x_ref[...], g_ref[...], a_ref[...] + b_ref[...])
        dg_ref[...] = jnp.zeros_like(dg_ref)
        dg_ref[0:1, :] = dg

    return pl.pallas_call(
        body, grid=(1,),
        in_specs=[_row_spec(M, n), _vec_spec(n), _row_spec(M, n), _row_spec(M, n)],
        out_specs=_acc_spec(n), out_shape=jax.ShapeDtypeStruct((8, n), F32),
        name=name, compiler_params=_cparams(),
    )(x, g, dy_a, dy_b)


def _head_group_matrix():
    a = lax.broadcasted_iota(jnp.int32, (D_GRP, D_GRP), 0) // HEAD
    b = lax.broadcasted_iota(jnp.int32, (D_GRP, D_GRP), 1) // HEAD
    return jnp.where(a == b, 1.0, 0.0).astype(BF16)


def _mix_norm_fwd_call(yf, yc, gf, gc, *, name, tm=1024):
    M = yf.shape[0]

    def body(yf_ref, yc_ref, gf_ref, gc_ref, o_ref):
        o_ref[:, 0:D_GRP] = _rms(yf_ref[...], gf_ref[...]).astype(BF16)
        o_ref[:, D_GRP:D] = _rms(yc_ref[...], gc_ref[...]).astype(BF16)

    return pl.pallas_call(
        body, grid=(M // tm,),
        in_specs=[_row_spec(tm, D_GRP), _row_spec(tm, D_GRP), _vec_spec(D_GRP), _vec_spec(D_GRP)],
        out_specs=_row_spec(tm, D), out_shape=jax.ShapeDtypeStruct((M, D), BF16),
        name=name, compiler_params=_cparams(),
    )(yf, yc, gf, gc)


def _mix_norm_bwd_call(dyn, yf, yc, gf, gc, *, name, tm=2 * TQ):
    M = yf.shape[0]

    def body(dyn_ref, yf_ref, yc_ref, gf_ref, gc_ref, dof_ref, doc_ref, delta_ref, dgf_ref, dgc_ref):
        i = pl.program_id(0)
        yf_ = yf_ref[...]
        dof, dgf = _rms_bwd(yf_, gf_ref[...], dyn_ref[:, 0:D_GRP])
        doc, dgc = _rms_bwd(yc_ref[...], gc_ref[...], dyn_ref[:, D_GRP:D])
        dof_b = dof.astype(BF16)
        dof_ref[...] = dof_b
        doc_ref[...] = doc.astype(BF16)
        prod = dof_b.astype(F32) * yf_
        hi = prod.astype(BF16)
        lo = (prod - hi.astype(F32)).astype(BF16)
        grp = _head_group_matrix()
        delta = _dot(hi, grp) + _dot(lo, grp)
        for b in range(tm // TQ):
            delta_ref[b] = delta[b * TQ:(b + 1) * TQ, :].T
        _acc_add(dgf_ref, dgf, i)
        _acc_add(dgc_ref, dgc, i)

    return pl.pallas_call(
        body, grid=(M // tm,),
        in_specs=[_row_spec(tm, D), _row_spec(tm, D_GRP), _row_spec(tm, D_GRP), _vec_spec(D_GRP), _vec_spec(D_GRP)],
        out_specs=(_row_spec(tm, D_GRP), _row_spec(tm, D_GRP),
                   pl.BlockSpec((tm // TQ, D_GRP, TQ), lambda i: (i, 0, 0)), _acc_spec(D_GRP), _acc_spec(D_GRP)),
        out_shape=(jax.ShapeDtypeStruct((M, D_GRP), BF16), jax.ShapeDtypeStruct((M, D_GRP), BF16),
                   jax.ShapeDtypeStruct((M // TQ, D_GRP, TQ), F32), jax.ShapeDtypeStruct((8, D_GRP), F32),
                   jax.ShapeDtypeStruct((8, D_GRP), F32)),
        name=name, compiler_params=_cparams(),
    )(dyn, yf, yc, gf, gc)


def _tri(n, lower_incl):
    a = lax.broadcasted_iota(jnp.int32, (n, n), 0)
    b = lax.broadcasted_iota(jnp.int32, (n, n), 1)
    return jnp.where(a >= b, 1.0, 0.0).astype(BF16) if lower_incl else jnp.where(a <= b, 1.0, 0.0).astype(BF16)


def _fox_prep_call(fl_raw, b_pad, *, name):
    S = fl_raw.shape[0]
    nb = S // TQ

    def body(fl_ref, b_ref, crep_ref, ct_ref, carry_ref):
        i = pl.program_id(0)

        @pl.when(i == 0)
        def _():
            carry_ref[...] = jnp.zeros_like(carry_ref)
        logf = jax.nn.log_sigmoid(fl_ref[...] + b_ref[...])
        cb = _dot3_l(_tri(TQ, True), logf) + carry_ref[0:1, :]
        carry_ref[0:1, :] = cb[TQ - 1:TQ, :]
        a = lax.broadcasted_iota(jnp.int32, (128, D_GRP), 0)
        b = lax.broadcasted_iota(jnp.int32, (128, D_GRP), 1) // HEAD
        expand = jnp.where(a == b, 1.0, 0.0).astype(BF16)
        crep = _dot3(cb, expand)
        crep_ref[...] = crep
        ct_ref[...] = crep.T

    return pl.pallas_call(
        body, grid=(nb,),
        in_specs=[_row_spec(TQ, 128), _vec_spec(128)],
        out_specs=(_row_spec(TQ, D_GRP), pl.BlockSpec((None, D_GRP, TQ), lambda i: (i, 0, 0))),
        out_shape=(jax.ShapeDtypeStruct((S, D_GRP), F32), jax.ShapeDtypeStruct((nb, D_GRP, TQ), F32)),
        scratch_shapes=[pltpu.VMEM((8, 128), F32)],
        name=name, compiler_params=_cparams(),
    )(fl_raw, b_pad)


def _lane_masks():
    lane = lax.broadcasted_iota(jnp.int32, (1, 128), 1)
    return lane < HEAD, lane >= HEAD


def _fox_fwd_call(proj, c_rep, c_t, *, name, task=None):
    S = proj.shape[0]
    nq = S // TQ
    scale = HEAD ** -0.5

    def body(q_ref, k_ref, v_ref, c_ref, ct_ref, o_ref, lse_ref):
        i = pl.program_id(1)
        m_lo, m_hi = _lane_masks()
        masks = (m_lo, m_hi)
        q = q_ref[...] * scale
        qm = [jnp.where(mk, q, jnp.zeros_like(q)) for mk in masks]
        cq = c_ref[...]
        cqh = [cq[:, 0:1], cq[:, HEAD:HEAD + 1]]
        row = lax.broadcasted_iota(jnp.int32, (TQ, TQ), 0)
        col = lax.broadcasted_iota(jnp.int32, (TQ, TQ), 1)

        def scores(j):
            start = pl.multiple_of(j * TQ, TQ)
            k = k_ref[pl.ds(start, TQ), :]
            ct = ct_ref[j]
            return tuple(_dot_nt(qm[h], k) + (cqh[h] - ct[HEAD * h:HEAD * h + 1, :]) for h in range(2))

        def update(j, ss, state, masked):
            ms, ls, acc = state
            start = pl.multiple_of(j * TQ, TQ)
            v = v_ref[pl.ds(start, TQ), :]
            new_m, new_l, pv, alpha_l = [], [], [], []
            for h in range(2):
                s = ss[h]
                if masked:
                    s = jnp.where(row >= col, s, NEG)
                mn = jnp.maximum(ms[h], jnp.max(s, axis=1, keepdims=True))
                alpha = jnp.exp(ms[h] - mn)
                p = jnp.exp(s - mn)
                new_l.append(alpha * ls[h] + jnp.sum(p, axis=1, keepdims=True))
                new_m.append(mn)
                alpha_l.append(alpha)
                pv.append(_dot(p.astype(BF16), jnp.where(masks[h], v, jnp.zeros_like(v))))
            alpha_lane = jnp.where(m_lo, alpha_l[0], alpha_l[1])
            acc = acc * alpha_lane + pv[0] + pv[1]
            return (tuple(new_m), tuple(new_l), acc)

        def step(j, carry):
            ss, state = carry
            return (scores(j + 1), update(j, ss, state, False))

        init = ((jnp.full((TQ, 1), NEG, F32),) * 2, (jnp.zeros((TQ, 1), F32),) * 2, jnp.zeros((TQ, 128), F32))
        ss, state = lax.fori_loop(0, i, step, (scores(0), init))
        ms, ls, acc = update(i, ss, state, True)
        l_lane = jnp.where(m_lo, ls[0], ls[1])
        o_ref[...] = acc / l_lane
        lse_ref[...] = jnp.where(m_lo, ms[0] + jnp.log(ls[0]), ms[1] + jnp.log(ls[1])).T

    return _pallas(
        body, grid=(N_PAIR, nq),
        in_specs=[pl.BlockSpec((TQ, 128), lambda p, i: (i, p)),
                  pl.BlockSpec((S, 128), lambda p, i: (0, N_PAIR + p)),
                  pl.BlockSpec((S, 128), lambda p, i: (0, 2 * N_PAIR + p)),
                  pl.BlockSpec((TQ, 128), lambda p, i: (i, p)),
                  pl.BlockSpec((nq, 128, TQ), lambda p, i: (0, p, 0))],
        out_specs=(pl.BlockSpec((TQ, 128), lambda p, i: (i, p)), pl.BlockSpec((None, 128, TQ), lambda p, i: (i, p, 0))),
        out_shape=(jax.ShapeDtypeStruct((S, D_GRP), F32), jax.ShapeDtypeStruct((nq, D_GRP, TQ), F32)),
        name=name, task=task,
    )(proj, proj, proj, c_rep, c_t)


def _fox_bwd_call(proj, do, lse_t, delta_t, c_rep, c_t, *, name, task=None):
    S = proj.shape[0]
    nq = S // TQ
    scale = HEAD ** -0.5

    def body(q_ref, k_ref, v_ref, do_ref, lse_ref, dl_ref, ck_ref, ct_ref,
             dq_ref, dk_ref, dv_ref, dcq_ref, dck_ref, dqa_ref):
        j = pl.program_id(1)
        m_lo, m_hi = _lane_masks()
        masks = (m_lo, m_hi)

        @pl.when(j == 0)
        def _():
            dqa_ref[...] = jnp.zeros_like(dqa_ref)
            dcq_ref[...] = jnp.zeros_like(dcq_ref)
        k = k_ref[...]
        v = v_ref[...]
        km = [jnp.where(mk, k, jnp.zeros_like(k)) for mk in masks]
        ck = ck_ref[...]
        krow = lax.broadcasted_iota(jnp.int32, (TQ, TQ), 0)
        qcol = lax.broadcasted_iota(jnp.int32, (TQ, TQ), 1)

        def probs(i):
            start = pl.multiple_of(i * TQ, TQ)
            q = q_ref[pl.ds(start, TQ), :]
            do = do_ref[pl.ds(start, TQ), :]
            lse = lse_ref[i]
            cq = ct_ref[i]
            out = []
            for h in range(2):
                lo = HEAD * h
                qm = jnp.where(masks[h], q * scale, jnp.zeros_like(q))
                dom = jnp.where(masks[h], do, jnp.zeros_like(do))
                st = _dot_nt(k, qm) + (cq[lo:lo + 1, :] - ck[:, lo:lo + 1])
                out.append((jnp.exp(st - lse[lo:lo + 1, :]), _dot_nt(v, dom)))
            return tuple(out)

        def update(i, pd, carry, masked):
            dk, dv, dck = carry
            start = pl.multiple_of(i * TQ, TQ)
            q = q_ref[pl.ds(start, TQ), :]
            do = do_ref[pl.ds(start, TQ), :]
            dl = dl_ref[i]
            dq = jnp.zeros((TQ, 128), F32)
            new_dck = []
            for h in range(2):
                lo = HEAD * h
                qm = jnp.where(masks[h], q, jnp.zeros_like(q))
                dom = jnp.where(masks[h], do, jnp.zeros_like(do))
                pt, dpt = pd[h]
                if masked:
                    pt = jnp.where(qcol >= krow, pt, 0.0)
                dst = pt * (dpt - dl[lo:lo + 1, :])
                dcq_ref[i, h:h + 1, :] += jnp.sum(dst, axis=0, keepdims=True)
                new_dck.append(dck[h] + jnp.sum(dst, axis=1, keepdims=True))
                dsb = (dst * scale).astype(BF16)
                dv = dv + _dot(pt.astype(BF16), dom)
                dk = dk + _dot(dsb, qm)
                dq = dq + _dot_tn(dsb, km[h])
            dqa_ref[pl.ds(start, TQ), :] += dq
            return (dk, dv, tuple(new_dck))

        def step(i, carry):
            pd, sums = carry
            return (probs(jnp.minimum(i + 1, nq - 1)), update(i, pd, sums, False))

        init = (jnp.zeros((TQ, 128), F32), jnp.zeros((TQ, 128), F32), (jnp.zeros((TQ, 1), F32),) * 2)
        first = probs(j)
        second = probs(jnp.minimum(j + 1, nq - 1))
        _, (dk, dv, dck) = lax.fori_loop(j + 1, nq, step, (second, update(j, first, init, True)))
        dk_ref[...] = dk.astype(BF16)
        dv_ref[...] = dv.astype(BF16)
        dck_ref[...] = -jnp.where(m_lo, dck[0], dck[1])

        @pl.when(j == nq - 1)
        def _():
            dq_ref[...] = dqa_ref[...].astype(BF16)

    res = lambda p, j: (0, p)
    stat = pl.BlockSpec((nq, 128, TQ), lambda p, j: (0, p, 0))
    blk = pl.BlockSpec((TQ, 128), lambda p, j: (j, p))
    return _pallas(
        body, grid=(N_PAIR, nq), task=task,
        in_specs=[pl.BlockSpec((S, 128), res),
                  pl.BlockSpec((TQ, 128), lambda p, j: (j, N_PAIR + p)),
                  pl.BlockSpec((TQ, 128), lambda p, j: (j, 2 * N_PAIR + p)),
                  pl.BlockSpec((S, 128), res), stat, stat, blk, stat],
        out_specs=(pl.BlockSpec((S, 128), res), blk, blk,
                   pl.BlockSpec((None, nq, 8, TQ), lambda p, j: (p, 0, 0, 0)), blk),
        out_shape=(jax.ShapeDtypeStruct((S, D_GRP), BF16), jax.ShapeDtypeStruct((S, D_GRP), BF16),
                   jax.ShapeDtypeStruct((S, D_GRP), BF16), jax.ShapeDtypeStruct((N_PAIR, nq, 8, TQ), F32),
                   jax.ShapeDtypeStruct((S, D_GRP), F32)),
        scratch_shapes=[pltpu.VMEM((S, 128), F32)],
        name=name,
    )(proj, proj, proj, do, lse_t, delta_t, c_rep, c_t)


def _fox_gate_bwd_call(dc_rows, dck, fl_raw, b_pad, *, name):
    S = fl_raw.shape[0]
    nb = S // TQ

    def body(dc_ref, dck_ref, fl_ref, b_ref, dfl_ref, db_ref, carry_ref):
        i = pl.program_id(0)

        @pl.when(i == 0)
        def _():
            carry_ref[...] = jnp.zeros_like(carry_ref)
        lane = lax.broadcasted_iota(jnp.int32, (D_GRP, 128), 0)
        head = lax.broadcasted_iota(jnp.int32, (D_GRP, 128), 1)
        pick = ((lane == head * HEAD) & (head < D_GRP // HEAD)).astype(BF16)
        dc = dc_ref[...] + _dot3(dck_ref[...], pick).T
        rc = _dot3(dc, _tri(TQ, True)) + carry_ref[:, 0:1]
        carry_ref[...] = jnp.broadcast_to(rc[:, 0:1], carry_ref.shape)
        fl = fl_ref[...] + b_ref[...]
        dfl = rc.T * jax.nn.sigmoid(-fl)
        dfl_ref[...] = dfl.astype(BF16)
        _acc_add(db_ref, jnp.sum(dfl, axis=0, keepdims=True), i)

    rev = lambda i: (nb - 1 - i, 0)
    return pl.pallas_call(
        body, grid=(nb,),
        in_specs=[pl.BlockSpec((128, TQ), lambda i: (0, nb - 1 - i)), pl.BlockSpec((TQ, D_GRP), rev),
                  pl.BlockSpec((TQ, 128), rev), _vec_spec(128)],
        out_specs=(pl.BlockSpec((TQ, 128), rev), _acc_spec(128)),
        out_shape=(jax.ShapeDtypeStruct((S, 128), BF16), jax.ShapeDtypeStruct((8, 128), F32)),
        scratch_shapes=[pltpu.VMEM((128, 128), F32)],
        name=name, compiler_params=_cparams(),
    )(dc_rows, dck, fl_raw, b_pad)


def _chk_bias_call(g_rev, *, name):
    def body(g_ref, o_ref):
        x = jnp.broadcast_to(g_ref[...], (TQ, ROLL_W))
        rolled = pltpu.roll(x, ROLL_W - (TQ - 1), 1, stride=1, stride_axis=0)
        qc = lax.broadcasted_iota(jnp.int32, (TQ, WIN), 0) // CHUNK
        kc = lax.broadcasted_iota(jnp.int32, (TQ, WIN), 1) // CHUNK
        band = (kc >= qc) & (kc <= qc + LEFT)
        o_ref[...] = jnp.where(band, rolled[:, 0:WIN], NEG)

    return pl.pallas_call(
        body, grid=(8,),
        in_specs=[pl.BlockSpec((None, 1, ROLL_W), lambda h: (h, 0, 0))],
        out_specs=pl.BlockSpec((None, TQ, WIN), lambda h: (h, 0, 0)),
        out_shape=jax.ShapeDtypeStruct((8, TQ, WIN), F32), name=name, compiler_params=_cparams(),
    )(g_rev.reshape(8, 1, ROLL_W))


def _chk_scores(i, qm, kwin, bias, scale):
    s = _dot_nt(qm * scale, kwin) + bias
    kc = lax.broadcasted_iota(jnp.int32, (TQ, WIN), 1) // CHUNK
    return jnp.where(kc + i * (TQ // CHUNK) >= LEFT, s, NEG)


def _chk_fwd_call(proj, bias, *, name, task=None):
    S = proj.shape[0]
    nq = S // TQ
    scale = HEAD ** -0.5

    def body(q_ref, k_ref, v_ref, b_ref, o_ref, kp_ref, vp_ref):
        i = pl.program_id(1)

        @pl.when(i == 0)
        def _():
            kp_ref[0:PADK, :] = jnp.zeros((PADK, 128), BF16)
            vp_ref[0:PADK, :] = jnp.zeros((PADK, 128), BF16)
            kp_ref[PADK:PADK + S, :] = k_ref[...]
            vp_ref[PADK:PADK + S, :] = v_ref[...]
        masks = _lane_masks()
        q = q_ref[...]
        start = pl.multiple_of(i * TQ, TQ)
        kwin = kp_ref[pl.ds(start, WIN), :]
        vwin = vp_ref[pl.ds(start, WIN), :]
        ss = [_chk_scores(i, jnp.where(masks[h], q, jnp.zeros_like(q)), kwin, b_ref[h], scale) for h in range(2)]
        ps = []
        for s in ss:
            p = jnp.exp(s - jnp.max(s, axis=1, keepdims=True))
            ps.append((p / jnp.sum(p, axis=1, keepdims=True)).astype(BF16))
        o_ref[...] = (_dot(ps[0], jnp.where(masks[0], vwin, jnp.zeros_like(vwin)))
                      + _dot(ps[1], jnp.where(masks[1], vwin, jnp.zeros_like(vwin))))

    c0 = 3 * N_PAIR
    return _pallas(
        body, grid=(N_PAIR, nq), task=task,
        in_specs=[pl.BlockSpec((TQ, 128), lambda p, i: (i, c0 + p)),
                  pl.BlockSpec((S, 128), lambda p, i: (0, c0 + N_PAIR + p)),
                  pl.BlockSpec((S, 128), lambda p, i: (0, c0 + 2 * N_PAIR + p)),
                  pl.BlockSpec((2, TQ, WIN), lambda p, i: (p, 0, 0))],
        out_specs=pl.BlockSpec((TQ, 128), lambda p, i: (i, p)),
        out_shape=jax.ShapeDtypeStruct((S, D_GRP), F32),
        scratch_shapes=[pltpu.VMEM((S + PADK, 128), BF16), pltpu.VMEM((S + PADK, 128), BF16)],
        name=name,
    )(proj, proj, proj, bias)


def _chk_bwd_call(proj, do, bias, *, name, task=None):
    S = proj.shape[0]
    nq = S // TQ
    scale = HEAD ** -0.5

    def body(q_ref, k_ref, v_ref, do_ref, b_ref, dq_ref, dk_ref, dv_ref, dg_ref, kp_ref, vp_ref, dkp_ref, dvp_ref, db_ref):
        i = pl.program_id(1)

        @pl.when(i == 0)
        def _():
            kp_ref[0:PADK, :] = jnp.zeros((PADK, 128), BF16)
            vp_ref[0:PADK, :] = jnp.zeros((PADK, 128), BF16)
            kp_ref[PADK:PADK + S, :] = k_ref[...]
            vp_ref[PADK:PADK + S, :] = v_ref[...]
            dkp_ref[...] = jnp.zeros_like(dkp_ref)
            dvp_ref[...] = jnp.zeros_like(dvp_ref)
            db_ref[...] = jnp.zeros_like(db_ref)
        masks = _lane_masks()
        q = q_ref[...]
        dout = do_ref[...]
        start = pl.multiple_of(i * TQ, TQ)
        kwin = kp_ref[pl.ds(start, WIN), :]
        vwin = vp_ref[pl.ds(start, WIN), :]
        qm = [jnp.where(mk, q, jnp.zeros_like(q)) for mk in masks]
        dom = [jnp.where(mk, dout, jnp.zeros_like(dout)) for mk in masks]
        ss = [_chk_scores(i, qm[h], kwin, b_ref[h], scale) for h in range(2)]
        dps = [_dot_nt(dom[h], vwin) for h in range(2)]
        pbs, dsbs = [], []
        for h in range(2):
            p = jnp.exp(ss[h] - jnp.max(ss[h], axis=1, keepdims=True))
            p = p / jnp.sum(p, axis=1, keepdims=True)
            ds = p * (dps[h] - jnp.sum(p * dps[h], axis=1, keepdims=True))
            db_ref[h] += ds
            pbs.append(p.astype(BF16))
            dsbs.append((ds * scale).astype(BF16))
        dq_ref[...] = (_dot(dsbs[0], jnp.where(masks[0], kwin, jnp.zeros_like(kwin)))
                       + _dot(dsbs[1], jnp.where(masks[1], kwin, jnp.zeros_like(kwin)))).astype(BF16)
        dkp_ref[pl.ds(start, WIN), :] += _dot_tn(dsbs[0], qm[0]) + _dot_tn(dsbs[1], qm[1])
        dvp_ref[pl.ds(start, WIN), :] += _dot_tn(pbs[0], dom[0]) + _dot_tn(pbs[1], dom[1])

        @pl.when(i == nq - 1)
        def _():
            dk_ref[...] = dkp_ref[PADK:PADK + S, :].astype(BF16)
            dv_ref[...] = dvp_ref[PADK:PADK + S, :].astype(BF16)
            a = lax.broadcasted_iota(jnp.int32, (TQ, TQ), 0)
            b = lax.broadcasted_iota(jnp.int32, (TQ, TQ), 1)
            flip = jnp.where(a + b == TQ - 1, 1.0, 0.0).astype(BF16)
            e = lax.broadcasted_iota(jnp.int32, (1, ROLL_W), 1)
            dg_ref[...] = jnp.zeros_like(dg_ref)
            for h in range(2):
                rev = _dot3_l(flip, db_ref[h])
                wide = jnp.concatenate([rev, jnp.zeros((TQ, ROLL_W - WIN), F32)], axis=1)
                diag = pltpu.roll(wide, 0, 1, stride=1, stride_axis=0)
                dg = jnp.sum(diag, axis=0, keepdims=True)
                lo = jnp.sum(jnp.where(e <= 639, dg, 0.0), axis=1, keepdims=True)
                hi = jnp.sum(jnp.where(e >= 895, dg, 0.0), axis=1, keepdims=True)
                dg_ref[h:h + 1, :] = jnp.where(e == 639, lo, jnp.where(e == 895, hi, dg))

    c0 = 3 * N_PAIR
    res = lambda p, i: (0, p)
    return _pallas(
        body, grid=(N_PAIR, nq), task=task,
        in_specs=[pl.BlockSpec((TQ, 128), lambda p, i: (i, c0 + p)),
                  pl.BlockSpec((S, 128), lambda p, i: (0, c0 + N_PAIR + p)),
                  pl.BlockSpec((S, 128), lambda p, i: (0, c0 + 2 * N_PAIR + p)),
                  pl.BlockSpec((TQ, 128), lambda p, i: (i, p)),
                  pl.BlockSpec((2, TQ, WIN), lambda p, i: (p, 0, 0))],
        out_specs=(pl.BlockSpec((TQ, 128), lambda p, i: (i, p)), pl.BlockSpec((S, 128), res),
                   pl.BlockSpec((S, 128), res), pl.BlockSpec((None, 8, ROLL_W), lambda p, i: (p, 0, 0))),
        out_shape=(jax.ShapeDtypeStruct((S, D_GRP), BF16), jax.ShapeDtypeStruct((S, D_GRP), BF16),
                   jax.ShapeDtypeStruct((S, D_GRP), BF16), jax.ShapeDtypeStruct((N_PAIR, 8, ROLL_W), F32)),
        scratch_shapes=[pltpu.VMEM((S + PADK, 128), BF16), pltpu.VMEM((S + PADK, 128), BF16),
                        pltpu.VMEM((S + PADK, 128), F32), pltpu.VMEM((S + PADK, 128), F32),
                        pltpu.VMEM((2, TQ, WIN), F32)],
        name=name,
    )(proj, proj, proj, do, bias)


def _mem_fwd_call(q, k, v, *, name, tq=2048):
    S = q.shape[0]
    scale = MEM_HD ** -0.5

    def body(q_ref, k_ref, v_ref, o_ref):
        s = _dot_nt(q_ref[...] * scale, k_ref[...])
        p = jnp.exp(s - jnp.max(s, axis=1, keepdims=True))
        p = p / jnp.sum(p, axis=1, keepdims=True)
        o_ref[...] = _dot(p.astype(BF16), v_ref[...]).astype(BF16)

    return pl.pallas_call(
        body, grid=(MEM_HEADS, S // tq),
        in_specs=[pl.BlockSpec((tq, MEM_HD), lambda h, i: (i, h)),
                  pl.BlockSpec((N_MEM, MEM_HD), lambda h, i: (0, h)),
                  pl.BlockSpec((N_MEM, MEM_HD), lambda h, i: (0, h))],
        out_specs=pl.BlockSpec((tq, MEM_HD), lambda h, i: (i, h)),
        out_shape=jax.ShapeDtypeStruct((S, D), BF16), name=name, compiler_params=_cparams(),
    )(q, k, v)


def _mem_bwd_call(q, k, v, do, *, name, tq=2048):
    S = q.shape[0]
    n = S // tq
    scale = MEM_HD ** -0.5

    def body(q_ref, k_ref, v_ref, do_ref, dq_ref, dk_ref, dv_ref, dka_ref, dva_ref):
        i = pl.program_id(1)

        @pl.when(i == 0)
        def _():
            dka_ref[...] = jnp.zeros_like(dka_ref)
            dva_ref[...] = jnp.zeros_like(dva_ref)
        qb = q_ref[...]
        kb = k_ref[...]
        dob = do_ref[...]
        s = _dot_nt(qb * scale, kb)
        p = jnp.exp(s - jnp.max(s, axis=1, keepdims=True))
        p = p / jnp.sum(p, axis=1, keepdims=True)
        dp = _dot_nt(dob, v_ref[...])
        ds = p * (dp - jnp.sum(p * dp, axis=1, keepdims=True))
        dsb = (ds * scale).astype(BF16)
        dq_ref[...] = _dot(dsb, kb).astype(BF16)
        dka_ref[...] += _dot_tn(dsb, qb)
        dva_ref[...] += _dot_tn(p.astype(BF16), dob)

        @pl.when(i == n - 1)
        def _():
            dk_ref[...] = dka_ref[...].astype(BF16)
            dv_ref[...] = dva_ref[...].astype(BF16)

    kv = pl.BlockSpec((N_MEM, MEM_HD), lambda h, i: (0, h))
    qs = pl.BlockSpec((tq, MEM_HD), lambda h, i: (i, h))
    return pl.pallas_call(
        body, grid=(MEM_HEADS, n), in_specs=[qs, kv, kv, qs], out_specs=(qs, kv, kv),
        out_shape=(jax.ShapeDtypeStruct((S, D), BF16), jax.ShapeDtypeStruct((N_MEM, D), BF16),
                   jax.ShapeDtypeStruct((N_MEM, D), BF16)),
        scratch_shapes=[pltpu.VMEM((N_MEM, MEM_HD), F32), pltpu.VMEM((N_MEM, MEM_HD), F32)],
        name=name, compiler_params=_cparams(),
    )(q, k, v, do)


def _rel_table_to_g(rel):
    return jnp.concatenate([
        jnp.broadcast_to(rel[:, N_REL - 1:N_REL], (8, 640)),
        rel[:, 1:N_REL - 1][:, ::-1],
        jnp.broadcast_to(rel[:, 0:1], (8, 129)),
    ], axis=1)


def _g_to_rel_table(dg):
    return dg[:, 639:896][:, ::-1]


def _place():
    x, y, c = lax.axis_index("x"), lax.axis_index("y"), lax.axis_index("c")
    others = [(1 - x, y), (x, 1 - y), (1 - x, 1 - y)]
    return x, y, c, others


def _half(c, rows):
    hr = rows // 2
    return pl.ds(pl.multiple_of(c * hr, 16), hr)


def _dma_sems(*shape):
    return pltpu.SemaphoreType.DMA(shape)


def _cast_slabs_call(ws, chip_arr, *, name, tm=256, task=None):
    n = len(ws)
    cols = ws[0].shape[1]
    tiles = [w.shape[0] // tm for w in ws]
    steps = max(tiles)

    def body(chip_ref, *refs):
        i = pl.program_id(0)
        for k in range(n):
            def cast(k=k):
                refs[n + k][...] = refs[k][...].astype(BF16)
            if tiles[k] == steps:
                cast()
            else:
                pl.when(i < tiles[k])(cast)

    in_specs = [pl.BlockSpec((tm, cols), lambda i, chip, t=t: (jnp.minimum(i, t - 1), 0)) for t in tiles]
    out_specs = [pl.BlockSpec((None, tm, cols), lambda i, chip, t=t: (chip[0], jnp.minimum(i, t - 1), 0)) for t in tiles]
    out_shape = [jax.ShapeDtypeStruct((N_CHIP,) + w.shape, BF16) for w in ws]
    return _pallas(body, grid=(steps,), in_specs=in_specs, out_specs=out_specs, out_shape=out_shape, name=name,
                   task=task, prefetch=1)(chip_arr, *ws)


def _cast_slab_call(w, chip_arr, *, name, tm=256, pad_rows=0):
    rows, cols = w.shape
    if pad_rows:
        tm = rows
    tm = min(tm, rows)

    def body(chip_ref, w_ref, o_ref):
        o_ref[0:tm, :] = w_ref[...].astype(BF16)
        if pad_rows:
            o_ref[tm:tm + pad_rows, :] = jnp.zeros((pad_rows, cols), BF16)

    return pl.pallas_call(
        body,
        grid_spec=pltpu.PrefetchScalarGridSpec(
            num_scalar_prefetch=1, grid=(rows // tm,),
            in_specs=[pl.BlockSpec((tm, cols), lambda i, chip: (i, 0))],
            out_specs=pl.BlockSpec((None, tm + pad_rows, cols), lambda i, chip: (chip[0], i, 0))),
        out_shape=jax.ShapeDtypeStruct((N_CHIP, rows + pad_rows, cols), BF16), name=name,
        compiler_params=_cparams(),
    )(chip_arr, w)


def _ag_ici_task(gathered):
    n = len(gathered)

    def copies(ins, outs, sems):
        send_sems, recv_sems = sems
        x, y, c, others = _place()
        me = 2 * x + y
        for k in range(n):
            mine = _half(c, gathered[k].shape[1])
            for t, (ox, oy) in enumerate(others):
                yield [pltpu.make_async_remote_copy(
                    src_ref=ins[k].at[me, mine], dst_ref=outs[k].at[slab, mine],
                    send_sem=send_sems.at[k, t], recv_sem=recv_sems.at[k, t],
                    device_id=(ox, oy, c), device_id_type=MESH) for slab in (me, 2 * ox + oy)]

    def issue(ins, outs, sems):
        for outgoing, _ in copies(ins, outs, sems):
            outgoing.start()

    def drain(ins, outs, sems):
        for outgoing, incoming in copies(ins, outs, sems):
            incoming.wait_recv()
            outgoing.wait_send()

    return _Task(gathered, [jax.ShapeDtypeStruct(g.shape, g.dtype) for g in gathered],
                 [_dma_sems(n, 3), _dma_sems(n, 3)], issue, drain, aliases={k: k for k in range(n)})


def _ag_d2d_task(gathered):
    n = len(gathered)

    def copies(ins, outs, sems):
        send_sems, recv_sems = sems
        x, y, c, others = _place()
        for k in range(n):
            rows = gathered[k].shape[1]
            mine, theirs = _half(c, rows), _half(1 - c, rows)
            for t, (ox, oy) in enumerate(others):
                slab = 2 * ox + oy
                pair = [pltpu.make_async_remote_copy(
                    src_ref=ins[k].at[slab, half], dst_ref=outs[k].at[slab, half],
                    send_sem=send_sems.at[k, t], recv_sem=recv_sems.at[k, t],
                    device_id=(x, y, 1 - c), device_id_type=MESH) for half in (mine, theirs)]
                yield pair

    def issue(ins, outs, sems):
        for outgoing, _ in copies(ins, outs, sems):
            outgoing.start()

    def drain(ins, outs, sems):
        for outgoing, incoming in copies(ins, outs, sems):
            incoming.wait_recv()
            outgoing.wait_send()

    return _Task(gathered, [jax.ShapeDtypeStruct(g.shape, g.dtype) for g in gathered],
                 [_dma_sems(n, 3), _dma_sems(n, 3)], issue, drain, aliases={k: k for k in range(n)})


def _rs_pair_task(ds):
    n = len(ds)

    def copies(ins, outs, sems):
        send_sems, recv_sems = sems
        x, y, c, _ = _place()
        for k in range(n):
            yield pltpu.make_async_remote_copy(
                src_ref=ins[k].at[:, _half(1 - c, ds[k].shape[1])], dst_ref=outs[k],
                send_sem=send_sems.at[k], recv_sem=recv_sems.at[k],
                device_id=(x, y, 1 - c), device_id_type=MESH)

    def issue(ins, outs, sems):
        for cp in copies(ins, outs, sems):
            cp.start()

    def drain(ins, outs, sems):
        for cp in copies(ins, outs, sems):
            cp.wait()

    return _Task(ds, [jax.ShapeDtypeStruct((N_CHIP, d.shape[1] // 2, d.shape[2]), d.dtype) for d in ds],
                 [_dma_sems(n), _dma_sems(n)], issue, drain)


def _pair_add_call(d, r1, c_arr, *, name, tm=512):
    _, rows, cols = d.shape
    hr = rows // 2
    tm = tm if hr % tm == 0 else hr
    nb = hr // tm

    def body(c_ref, d_ref, r_ref, o_ref):
        o_ref[...] = (d_ref[...].astype(F32) + r_ref[...].astype(F32)).astype(BF16)

    return pl.pallas_call(
        body,
        grid_spec=pltpu.PrefetchScalarGridSpec(
            num_scalar_prefetch=1, grid=(N_CHIP, nb),
            in_specs=[pl.BlockSpec((None, tm, cols), lambda j, i, c: (j, c[0] * nb + i, 0)),
                      pl.BlockSpec((None, tm, cols), lambda j, i, c: (j, i, 0))],
            out_specs=pl.BlockSpec((None, tm, cols), lambda j, i, c: (j, i, 0))),
        out_shape=jax.ShapeDtypeStruct((N_CHIP, hr, cols), BF16), name=name, compiler_params=_cparams(),
    )(c_arr, d, r1)


def _rs_chip_task(ps):
    n = len(ps)

    def copies(ins, outs, sems):
        send_sems, recv_sems = sems
        x, y, c, others = _place()
        for k in range(n):
            for t, (ox, oy) in enumerate(others):
                yield pltpu.make_async_remote_copy(
                    src_ref=ins[k].at[2 * ox + oy], dst_ref=outs[k].at[t],
                    send_sem=send_sems.at[k, t], recv_sem=recv_sems.at[k, t],
                    device_id=(ox, oy, c), device_id_type=MESH)

    def issue(ins, outs, sems):
        for cp in copies(ins, outs, sems):
            cp.start()

    def drain(ins, outs, sems):
        for cp in copies(ins, outs, sems):
            cp.wait()

    return _Task(ps, [jax.ShapeDtypeStruct((3,) + p.shape[1:], p.dtype) for p in ps],
                 [_dma_sems(n, 3), _dma_sems(n, 3)], issue, drain)


def _chip_sum_call(p, r2, place_arr, *, name, tm=512):
    _, hr, cols = r2.shape
    tm = tm if hr % tm == 0 else hr
    nb = hr // tm

    def body(place_ref, p_ref, r_ref, o_ref):
        acc = p_ref[...].astype(F32)
        for j in range(3):
            acc = acc + r_ref[j].astype(F32)
        o_ref[...] = acc

    return _pallas(
        body, grid=(nb,), prefetch=1,
        in_specs=[pl.BlockSpec((None, tm, cols), lambda i, pc: (pc[0], i, 0)),
                  pl.BlockSpec((3, tm, cols), lambda i, pc: (0, i, 0))],
        out_specs=pl.BlockSpec((tm, cols), lambda i, pc: (pc[1] * nb + i, 0)),
        out_shape=jax.ShapeDtypeStruct((2 * hr, cols), F32), name=name,
    )(place_arr, p, r2)


def _chip_sums_call(ps, r2s, place_arr, *, name, steps=2, task=None):
    n = len(ps)

    def body(place_ref, *refs):
        ins, outs = refs[:2 * n], refs[2 * n:]
        for k in range(n):
            acc = ins[2 * k][...].astype(F32)
            for j in range(3):
                acc = acc + ins[2 * k + 1][j].astype(F32)
            outs[k][...] = acc

    in_specs, out_specs, out_shape, args = [], [], [], []
    for p, r2 in zip(ps, r2s):
        _, hr, cols = r2.shape
        tm = hr // steps
        in_specs += [pl.BlockSpec((None, tm, cols), lambda i, pc: (pc[0], i, 0)),
                     pl.BlockSpec((3, tm, cols), lambda i, pc: (0, i, 0))]
        out_specs.append(pl.BlockSpec((tm, cols), lambda i, pc: (pc[1] * steps + i, 0)))
        out_shape.append(jax.ShapeDtypeStruct((2 * hr, cols), F32))
        args += [p, r2]
    return _pallas(body, grid=(steps,), prefetch=1, in_specs=in_specs, out_specs=out_specs, out_shape=out_shape,
                   name=name, task=task)(place_arr, *args)


def _rs_gather_task(gs):
    n = len(gs)

    def copies(ins, outs, sems):
        send_sems, recv_sems = sems
        x, y, c, _ = _place()
        for k in range(n):
            rows = gs[k].shape[0]
            mine, theirs = _half(c, rows), _half(1 - c, rows)
            yield [pltpu.make_async_remote_copy(
                src_ref=ins[k].at[mine], dst_ref=outs[k].at[half],
                send_sem=send_sems.at[k], recv_sem=recv_sems.at[k],
                device_id=(x, y, 1 - c), device_id_type=MESH) for half in (mine, theirs)]

    def issue(ins, outs, sems):
        for outgoing, _ in copies(ins, outs, sems):
            outgoing.start()

    def drain(ins, outs, sems):
        for outgoing, incoming in copies(ins, outs, sems):
            incoming.wait_recv()
            outgoing.wait_send()

    return _Task(gs, [jax.ShapeDtypeStruct(g.shape, g.dtype) for g in gs],
                 [_dma_sems(n), _dma_sems(n)], issue, drain, aliases={k: k for k in range(n)})


def _adamw(w, g, m, v):
    m = ADAM_B1 * m + (1.0 - ADAM_B1) * g
    v = ADAM_B2 * v + (1.0 - ADAM_B2) * jnp.square(g)
    m_hat = m / (1.0 - ADAM_B1 ** ADAM_STEP)
    v_hat = v / (1.0 - ADAM_B2 ** ADAM_STEP)
    delta = -ADAM_LR * (m_hat / (jnp.sqrt(v_hat) + ADAM_EPS) + ADAM_WD * w)
    return delta, m, v


def _adamw_call(items, *, name, tm=256, task=None):
    n = len(items)
    cols = items[0][0].shape[1]
    tiles = [it[0].shape[0] // tm for it in items]
    steps = max(tiles)

    def body(*refs):
        i = pl.program_id(0)
        ins, outs = refs[:4 * n], refs[4 * n:]
        for k in range(n):
            def update(k=k):
                g = ins[4 * k + 1][...]
                res = _adamw(ins[4 * k][...], g, ins[4 * k + 2][...], ins[4 * k + 3][...])
                outs[4 * k][...] = g
                for j in range(3):
                    outs[4 * k + 1 + j][...] = res[j]
            if tiles[k] == steps:
                update()
            else:
                pl.when(i < tiles[k])(update)

    in_specs, out_specs, out_shape, args = [], [], [], []
    for it, t in zip(items, tiles):
        spec = pl.BlockSpec((tm, cols), lambda i, t=t: (jnp.minimum(i, t - 1), 0))
        in_specs += [spec] * 4
        out_specs += [spec] * 4
        out_shape += [jax.ShapeDtypeStruct(it[0].shape, F32)] * 4
        args += list(it)
    res = _pallas(body, grid=(steps,), in_specs=in_specs, out_specs=out_specs, out_shape=out_shape,
                  name=name, task=task)(*args)
    outs, extra = res if task is not None else (res, None)
    grouped = [tuple(outs[4 * k:4 * k + 4]) for k in range(n)]
    return (grouped, extra) if task is not None else grouped


def _adamw_cols_call(w, g_pad, m, v, *, name, tn=512):
    rows, cols = w.shape

    def body(w_ref, g_ref, m_ref, v_ref, go_ref, d_ref, mo_ref, vo_ref):
        g = g_ref[0:rows, :]
        d, mn, vn = _adamw(w_ref[...], g, m_ref[...], v_ref[...])
        go_ref[...] = g
        d_ref[...] = d
        mo_ref[...] = mn
        vo_ref[...] = vn

    spec = pl.BlockSpec((rows, tn), lambda j: (0, j))
    gspec = pl.BlockSpec((g_pad.shape[0], tn), lambda j: (0, j))
    return _pallas(body, grid=(cols // tn,), in_specs=[spec, gspec, spec, spec], out_specs=(spec,) * 4,
                   out_shape=(jax.ShapeDtypeStruct((rows, cols), F32),) * 4, name=name)(w, g_pad, m, v)


N_DEV = 8
SMALL_ROWS = 24
SMALL_LAYOUT = {
    "g_mix_pre": (0, 0, 1, D), "g_mix_post": (1, 0, 1, D), "g_mem_kv": (2, 0, 1, D), "g_mem_pre": (3, 0, 1, D),
    "g_mem_post": (4, 0, 1, D), "g_ff_pre": (5, 0, 1, D), "g_ff_post": (6, 0, 1, D),
    "g_fox_out": (7, 0, 1, D_GRP), "g_chk_out": (7, D_GRP, 1, D_GRP), "b_fgt": (8, 0, 1, 8),
    "rel_bias": (16, 0, 8, N_REL),
}
SMALL = list(SMALL_LAYOUT)


LOSS_ROW = 9


def _small_reduce_call(grads, loss_blk, task, *, name):
    n = len(SMALL)
    t_in, t_out = len(task.arrays), len(task.out_shapes)

    def body(*refs):
        g_refs, loss_ref, tins = refs[:n], refs[n], refs[n + 1:n + 1 + t_in]
        p = n + 1 + t_in
        total_ref, loss_out, touts = refs[p], refs[p + 1], refs[p + 2:p + 2 + t_out]
        p += 2 + t_out
        mine, slots, send_sems, recv_sems = refs[p:p + 4]
        tsems = refs[p + 4:]
        task.issue(tins, touts, tsems)
        x, y, c, _ = _place()
        me = 4 * x + 2 * y + c
        mine[...] = jnp.zeros_like(mine)
        for k, name_k in enumerate(SMALL):
            r, l, nr, nl = SMALL_LAYOUT[name_k]
            mine[r:r + nr, l:l + nl] = g_refs[k][0:nr, 0:nl]
        mine[LOSS_ROW:LOSS_ROW + 1, 0:128] = loss_ref[0:1, :]
        slots[me] = mine[...]
        peers = [(dx, dy, dc) for dx in (0, 1) for dy in (0, 1) for dc in (0, 1)][1:]
        cps = []
        for t, (dx, dy, dc) in enumerate(peers):
            px, py, pc = (x + dx) % 2, (y + dy) % 2, (c + dc) % 2
            cps.append(pltpu.make_async_remote_copy(
                src_ref=mine, dst_ref=slots.at[me], send_sem=send_sems.at[t], recv_sem=recv_sems.at[t],
                device_id=(px, py, pc), device_id_type=MESH))
            cps[-1].start()
        for t, (dx, dy, dc) in enumerate(peers):
            px, py, pc = (x + dx) % 2, (y + dy) % 2, (c + dc) % 2
            pltpu.make_async_remote_copy(
                src_ref=mine, dst_ref=slots.at[4 * px + 2 * py + pc], send_sem=send_sems.at[t],
                recv_sem=recv_sems.at[t], device_id=(px, py, pc), device_id_type=MESH).wait_recv()
        for cp in cps:
            cp.wait_send()
        total = slots[0]
        for j in range(1, N_DEV):
            total = total + slots[j]
        total_ref[...] = total
        loss_out[...] = jnp.broadcast_to(total[LOSS_ROW:LOSS_ROW + 1, 0:128], loss_out.shape)
        task.drain(tins, touts, tsems)

    vm = pl.BlockSpec(memory_space=pltpu.VMEM)
    out_shape = [jax.ShapeDtypeStruct((SMALL_ROWS, D), F32), jax.ShapeDtypeStruct((8, 128), F32)] + list(task.out_shapes)
    res = pl.pallas_call(
        body, in_specs=[vm] * (n + 1) + [ANY] * t_in, out_specs=[vm] * 2 + [ANY] * t_out,
        out_shape=out_shape,
        scratch_shapes=[pltpu.VMEM((SMALL_ROWS, D), F32), pltpu.VMEM((N_DEV, SMALL_ROWS, D), F32),
                        _dma_sems(N_DEV - 1), _dma_sems(N_DEV - 1)] + list(task.sems),
        input_output_aliases={n + 1 + i: 2 + j for i, j in task.aliases.items()},
        name=name,
    )(*[grads[k] for k in SMALL], loss_blk, *task.arrays)
    return res[0], res[1], list(res[2:])


def _small_adamw_call(total, ws, ms, vs, *, name):
    n = len(SMALL)

    def body(*refs):
        total_ref = refs[0]
        w_refs, m_refs, v_refs = (refs[1 + j * n:1 + (j + 1) * n] for j in range(3))
        outs = refs[1 + 3 * n:]
        for k, name_k in enumerate(SMALL):
            r, l, nr, nl = SMALL_LAYOUT[name_k]
            g = total_ref[r:r + nr, l:l + nl]
            d, mn, vn = _adamw(w_refs[k][...], g, m_refs[k][...], v_refs[k][...])
            for j, val in enumerate((g, d, mn, vn)):
                outs[4 * k + j][...] = val

    vm = pl.BlockSpec(memory_space=pltpu.VMEM)
    res = pl.pallas_call(
        body, in_specs=[vm] * (3 * n + 1), out_specs=[vm] * (4 * n),
        out_shape=[jax.ShapeDtypeStruct(ws[k].shape, F32) for k in SMALL for _ in range(4)], name=name,
    )(total, *[d[k] for d in (ws, ms, vs) for k in SMALL])
    return {k: tuple(res[4 * i:4 * i + 4]) for i, k in enumerate(SMALL)}


WEIGHTS = ["w_in", "b_fgt", "rel_bias", "g_fox_out", "g_chk_out", "w_out", "g_mix_pre", "g_mix_post", "g_mem_kv",
           "w_mq", "w_mk", "w_mv", "w_mo", "g_mem_pre", "g_mem_post", "w_ff1", "w_ff2", "g_ff_pre", "g_ff_post"]
BIG = ["w_in", "w_out", "w_mq", "w_mk", "w_mv", "w_mo", "w_ff1", "w_ff2"]


IN_SHARD = D_IN // N_CHIP
IN_PAD = 800


IN_PIECES = [(0, 0, 770), (800, 770, 766), (1566, 3072, 4), (1600, 3076, 4), (1604, 1536, 766), (2400, 2302, 770)]
PAD_ZEROS = [(800 * j + IN_SHARD, IN_PAD - IN_SHARD) for j in range(N_CHIP)]
ALL_ZEROS = [(D_IN, D_ALL - D_IN)]


def _reorder_rows_call(src, to_all, *, name, tn=512):
    rows, cols = src.shape
    zeros = ALL_ZEROS if to_all else PAD_ZEROS

    def body(s_ref, o_ref):
        for pad0, all0, cnt in IN_PIECES:
            s0, d0 = (pad0, all0) if to_all else (all0, pad0)
            o_ref[d0:d0 + cnt, :] = s_ref[s0:s0 + cnt, :]
        for z0, cnt in zeros:
            o_ref[z0:z0 + cnt, :] = jnp.zeros((cnt, tn), src.dtype)

    spec = pl.BlockSpec((rows, tn), lambda j: (0, j))
    return _pallas(body, grid=(cols // tn,), in_specs=[spec], out_specs=spec,
                   out_shape=jax.ShapeDtypeStruct((rows, cols), src.dtype), name=name)(src)


def kernel(x, mem, w_in, b_fgt, rel_bias, g_fox_out, g_chk_out, w_out, g_mix_pre, g_mix_post, g_mem_kv, w_mq, w_mk, w_mv, w_mo, g_mem_pre, g_mem_post, w_ff1, w_ff2, g_ff_pre, g_ff_post, loss_target, m_w_in, m_b_fgt, m_rel_bias, m_g_fox_out, m_g_chk_out, m_w_out, m_g_mix_pre, m_g_mix_post, m_g_mem_kv, m_w_mq, m_w_mk, m_w_mv, m_w_mo, m_g_mem_pre, m_g_mem_post, m_w_ff1, m_w_ff2, m_g_ff_pre, m_g_ff_post, v_w_in, v_b_fgt, v_rel_bias, v_g_fox_out, v_g_chk_out, v_w_out, v_g_mix_pre, v_g_mix_post, v_g_mem_kv, v_w_mq, v_w_mk, v_w_mv, v_w_mo, v_g_mem_pre, v_g_mem_post, v_w_ff1, v_w_ff2, v_g_ff_pre, v_g_ff_post):
    w = dict(w_in=w_in, b_fgt=b_fgt, rel_bias=rel_bias, g_fox_out=g_fox_out, g_chk_out=g_chk_out, w_out=w_out,
             g_mix_pre=g_mix_pre, g_mix_post=g_mix_post, g_mem_kv=g_mem_kv, w_mq=w_mq, w_mk=w_mk, w_mv=w_mv,
             w_mo=w_mo, g_mem_pre=g_mem_pre, g_mem_post=g_mem_post, w_ff1=w_ff1, w_ff2=w_ff2, g_ff_pre=g_ff_pre,
             g_ff_post=g_ff_post)
    m = dict(w_in=m_w_in, b_fgt=m_b_fgt, rel_bias=m_rel_bias, g_fox_out=m_g_fox_out, g_chk_out=m_g_chk_out,
             w_out=m_w_out, g_mix_pre=m_g_mix_pre, g_mix_post=m_g_mix_post, g_mem_kv=m_g_mem_kv, w_mq=m_w_mq,
             w_mk=m_w_mk, w_mv=m_w_mv, w_mo=m_w_mo, g_mem_pre=m_g_mem_pre, g_mem_post=m_g_mem_post,
             w_ff1=m_w_ff1, w_ff2=m_w_ff2, g_ff_pre=m_g_ff_pre, g_ff_post=m_g_ff_post)
    v = dict(w_in=v_w_in, b_fgt=v_b_fgt, rel_bias=v_rel_bias, g_fox_out=v_g_fox_out, g_chk_out=v_g_chk_out,
             w_out=v_w_out, g_mix_pre=v_g_mix_pre, g_mix_post=v_g_mix_post, g_mem_kv=v_g_mem_kv, w_mq=v_w_mq,
             w_mk=v_w_mk, w_mv=v_w_mv, w_mo=v_w_mo, g_mem_pre=v_g_mem_pre, g_mem_post=v_g_mem_post,
             w_ff1=v_w_ff1, w_ff2=v_w_ff2, g_ff_pre=v_g_ff_pre, g_ff_post=v_g_ff_post)

    def rows(d, k):
        return d[k][0] if k == "rel_bias" else d[k]

    xs, mems, target = x[0], mem[0], loss_target[0]
    S = xs.shape[0]
    sp = {k: rows(w, k) for k in SMALL}
    b_pad = jnp.pad(sp["b_fgt"], ((0, 0), (0, 120)))
    chip = 2 * lax.axis_index("x") + lax.axis_index("y")
    chip_arr = jnp.reshape(chip, (1,)).astype(jnp.int32)
    c_arr = jnp.reshape(lax.axis_index("c"), (1,)).astype(jnp.int32)
    place_arr = jnp.concatenate([chip_arr, c_arr])
    w_in_t, m_in_t, v_in_t = w["w_in"][0].T, m["w_in"][0].T, v["w_in"][0].T
    slab = {"w_in": _cast_slab_call(w_in_t, chip_arr, name="cast_w_in", pad_rows=IN_PAD - IN_SHARD)}

    def gather_ici(names):
        return _ag_ici_task([slab[k] for k in names])

    def pair_add(k, d, r1):
        return _pair_add_call(d, r1, c_arr, name="rs_pair_add_" + k)

    rest, (g_in,) = _cast_slabs_call([w[k][0] for k in BIG[1:]], chip_arr, name="cast_rest",
                                     task=gather_ici(["w_in"]))
    slab.update(zip(BIG[1:], rest))
    h1, (g_in,) = _rms_fwd_call(xs, sp["g_mix_pre"], name="rms_mix_pre", task=_ag_d2d_task([g_in]))
    w_all_t = _reorder_rows_call(g_in.reshape(N_CHIP * IN_PAD, D), True, name="w_in_rows")
    proj, (g_out, g_mq) = _mm_nt(h1, w_all_t, "plain", rows=(0, 3072), name="mm_proj",
                                 task=gather_ici(["w_out", "w_mq"]))
    fl_raw = _mm_nt(h1, w_all_t, "plain", rows=(3072, 128), name="mm_gate", out_dtype=F32, tn=128)
    c_rep, c_t = _fox_prep_call(fl_raw, b_pad, name="fox_prep")
    bias = _chk_bias_call(_rel_table_to_g(sp["rel_bias"]), name="chk_bias")
    mid = ["w_mk", "w_mv", "w_mo", "w_ff1"]
    (yf, lse), got = _fox_fwd_call(proj, c_rep, c_t, name="fox_fwd",
                                   task=_merge_tasks([gather_ici(mid), _ag_d2d_task([g_out, g_mq])]))
    g_mid, (g_out, g_mq) = got[:4], got[4:]
    yc, got = _chk_fwd_call(proj, bias, name="chk_fwd",
                            task=_merge_tasks([gather_ici(["w_ff2"]), _ag_d2d_task(g_mid)]))
    g_ff2, (g_mk, g_mv, g_mo, g_ff1) = got[0], got[1:]
    yn = _mix_norm_fwd_call(yf, yc, sp["g_fox_out"], sp["g_chk_out"], name="mix_norm_fwd")
    z, (g_ff2,) = _mm_nn(yn, g_out, "rows", name="mm_out", out_dtype=F32, task=_ag_d2d_task([g_ff2]))
    x1, h2 = _post_pre_call(xs, z, sp["g_mix_post"], sp["g_mem_pre"], name="post_mix")
    memn = _rms_fwd_call(mems, sp["g_mem_kv"], name="rms_mem_kv")
    q2 = _mm_nn(h2, g_mq, "rows", name="mm_mq")
    k2 = _mm_nn(memn, g_mk, "rows", name="mm_mk")
    v2 = _mm_nn(memn, g_mv, "rows", name="mm_mv")
    o2 = _mem_fwd_call(q2, k2, v2, name="mem_fwd")
    y2 = _mm_nn(o2, g_mo, "rows", name="mm_mo", out_dtype=F32)
    x2, h3 = _post_pre_call(x1, y2, sp["g_mem_post"], sp["g_ff_pre"], name="post_mem")
    act, relu = _mm_nn(h3, g_ff1, "cols", name="mm_ff1", epi="relu2")
    y3 = _mm_nn(act, g_ff2, "rows", name="mm_ff2", out_dtype=F32, tm=1024)
    loss_blk, dx3, dy3, dg_ff_post = _final_call(x2, y3, sp["g_ff_post"], target, name="final")

    d_ff2 = _mm_tn(act, dy3, name="mm_dff2", tk=512, tn=1024).reshape(N_CHIP, D_FF // N_CHIP, D)
    du, (r1,) = _mm_nt(dy3, g_ff2, "rows", name="mm_du", mul2r=relu, task=_rs_pair_task([d_ff2]))
    p_ff2 = pair_add("w_ff2", d_ff2, r1)
    d_ff1 = _mm_tn(h3, du, name="mm_dff1", cols4=True)
    dh3, (r1,) = _mm_nt(du, g_ff1, "cols", name="mm_dh3", out_dtype=F32, tm=1024, task=_rs_pair_task([d_ff1]))
    p_ff1 = pair_add("w_ff1", d_ff1, r1)
    dx2, dy2, dg_ff_pre, dg_mem_post = _bwd_mid_call(dx3, x2, dh3, sp["g_ff_pre"], y2, sp["g_mem_post"], name="bwd_ff")
    d_mo = _mm_tn(o2, dy2, name="mm_dmo").reshape(N_CHIP, D // N_CHIP, D)
    do2 = _mm_nt(dy2, g_mo, "rows", name="mm_do2")
    dq2, dk2, dv2 = _mem_bwd_call(q2, k2, v2, do2, name="mem_bwd")
    d_mq = _mm_tn(h2, dq2, name="mm_dmq").reshape(N_CHIP, D // N_CHIP, D)
    dh2 = _mm_nt(dq2, g_mq, "rows", name="mm_dh2", out_dtype=F32)
    d_mk = _mm_tn(memn, dk2, name="mm_dmk").reshape(N_CHIP, D // N_CHIP, D)
    d_mv = _mm_tn(memn, dv2, name="mm_dmv").reshape(N_CHIP, D // N_CHIP, D)
    dmn_k = _mm_nt(dk2, g_mk, "rows", name="mm_dmemk", out_dtype=F32)
    dmn_v = _mm_nt(dv2, g_mv, "rows", name="mm_dmemv", out_dtype=F32)
    dg_mem_kv = _gain_grad_call(mems, sp["g_mem_kv"], dmn_k, dmn_v, name="gain_mem_kv")
    dx1, dz, dg_mem_pre, dg_mix_post = _bwd_mid_call(dx2, x1, dh2, sp["g_mem_pre"], z, sp["g_mix_post"], name="bwd_mem")
    d_out = _mm_tn(yn, dz, name="mm_dout").reshape(N_CHIP, D // N_CHIP, D)
    late = ["w_mo", "w_mq", "w_mk", "w_mv", "w_out"]
    d_late = [d_mo, d_mq, d_mk, d_mv, d_out]
    dyn, r1_late = _mm_nt(dz, g_out, "rows", name="mm_dyn", out_dtype=F32, task=_rs_pair_task(d_late))
    p_late = [pair_add(k, d, r1) for k, d, r1 in zip(late, d_late, r1_late)]
    dof, doc, delta, dg_fox, dg_chk = _mix_norm_bwd_call(dyn, yf, yc, sp["g_fox_out"], sp["g_chk_out"], name="mix_norm_bwd")
    (dqf, dkf, dvf, dcq, dck), r2_ff = _fox_bwd_call(proj, dof, lse, delta, c_rep, c_t, name="fox_bwd",
                                                      task=_rs_chip_task([p_ff2, p_ff1]))
    (dqc, dkc, dvc, dgrev), r2_late = _chk_bwd_call(proj, doc, bias, name="chk_bwd", task=_rs_chip_task(p_late))
    first = ["w_ff2", "w_ff1"] + late
    dc8 = dcq[:, :, 0:2, :].transpose(0, 2, 1, 3).reshape(8, S)
    dc_rows = jnp.concatenate([dc8, jnp.zeros((120, S), F32)], axis=0)
    dfl, db_fgt = _fox_gate_bwd_call(dc_rows, dck, fl_raw, b_pad, name="fox_gate_bwd")
    dproj = jnp.concatenate([dqf, dkf, dvf, dqc, dkc, dvc, dfl], axis=1)
    d_all_t = _mm_tn(dproj, h1, name="mm_dwin", tk=640, tn=1024)
    d_in = _reorder_rows_call(d_all_t, False, name="d_in_rows").reshape(N_CHIP, IN_PAD, D)
    f_first, (r1,) = _chip_sums_call([p_ff2, p_ff1] + p_late, r2_ff + r2_late, place_arr, name="rs_chip_sums",
                                     task=_rs_pair_task([d_in]))
    p_in = pair_add("w_in", d_in, r1)
    dh1, got = _mm_nn(dproj, w_all_t, "plain", name="mm_dh1", out_dtype=F32, tm=1024,
                      task=_merge_tasks([_rs_chip_task([p_in]), _rs_gather_task(f_first)]))
    r2_in, grads = got[0], dict(zip(first, got[1:]))
    f_in = _chip_sum_call(p_in, r2_in, place_arr, name="rs_chip_sum_w_in")
    delta_w, new_m, new_v = {}, {}, {}

    def adamw_items(names):
        return [(w[k][0], grads[k], m[k][0], v[k][0]) for k in names]

    upd_late = _adamw_call(adamw_items(late), name="adamw_late", tm=64)
    upd_ff = _adamw_call(adamw_items(first[:2]), name="adamw_ff")
    for k, res in zip(late + first[:2], upd_late + upd_ff):
        grads[k], delta_w[k], new_m[k], new_v[k] = res
    grad_x, dg_mix_pre = _bwd_last_call(dx1, xs, dh1, sp["g_mix_pre"], name="bwd_mix")

    small_g = {"g_mix_pre": dg_mix_pre, "g_mix_post": dg_mix_post, "g_mem_kv": dg_mem_kv, "g_mem_pre": dg_mem_pre,
               "g_mem_post": dg_mem_post, "g_ff_pre": dg_ff_pre, "g_ff_post": dg_ff_post, "g_fox_out": dg_fox,
               "g_chk_out": dg_chk, "b_fgt": db_fgt,
               "rel_bias": _g_to_rel_table(dgrev[:, 0:2, :].reshape(8, ROLL_W))}
    small_sum, loss_out, (g_w_in,) = _small_reduce_call(small_g, loss_blk, _rs_gather_task([f_in]),
                                                       name="small_allreduce")
    small = _small_adamw_call(small_sum, sp, {k: rows(m, k) for k in SMALL}, {k: rows(v, k) for k in SMALL},
                              name="small_adamw")
    loss = loss_out[0, 0]
    res = _adamw_cols_call(w_in_t, g_w_in, m_in_t, v_in_t, name="adamw_w_in")
    grads["w_in"], delta_w["w_in"], new_m["w_in"], new_v["w_in"] = (a.T for a in res)
    for k in SMALL:
        vals = small[k]
        if k == "rel_bias":
            vals = tuple(a[None] for a in vals)
        grads[k], delta_w[k], new_m[k], new_v[k] = vals

    def out(d, k):
        return d[k][None] if k in BIG else d[k]

    return (loss, grad_x[None], *[out(grads, k) for k in WEIGHTS], *[out(delta_w, k) for k in WEIGHTS],
            *[out(new_m, k) for k in WEIGHTS], *[out(new_v, k) for k in WEIGHTS])
```

```python
import functools

import jax
import jax.numpy as jnp
from jax import lax
from jax.experimental import pallas as pl
from jax.experimental.pallas import tpu as pltpu

F32 = jnp.float32
BF16 = jnp.bfloat16

D = 1024
HEAD = 64
N_PAIR = 4
D_GRP = 512
CHUNK = 64
LEFT = 8
MAX_REL = 128
N_REL = 2 * MAX_REL + 1
N_MEM = 256
MEM_HEADS = 4
MEM_HD = 256
D_FF = 4096
D_IN = 3080
D_ALL = 3200
EPS = 1e-6
TQ = 256
WIN = (LEFT + TQ // CHUNK) * CHUNK
PADK = LEFT * CHUNK
ROLL_W = 1024
NEG = -1e30
N_CHIP = 4
VMEM_LIMIT = 48 * 1024 * 1024

ADAM_LR = 0.001
ADAM_B1 = 0.9
ADAM_B2 = 0.999
ADAM_EPS = 1e-08
ADAM_WD = 0.01
ADAM_STEP = 10

MESH = pl.DeviceIdType.MESH


def _cparams():
    return pltpu.CompilerParams(vmem_limit_bytes=VMEM_LIMIT)


ANY = pl.BlockSpec(memory_space=pl.ANY)


class _Task:
    def __init__(self, arrays, out_shapes, sems, issue, drain, aliases=None):
        self.arrays, self.out_shapes, self.sems = list(arrays), list(out_shapes), list(sems)
        self.issue, self.drain, self.aliases = issue, drain, dict(aliases or {})


def _merge_tasks(tasks):
    tasks = [t for t in tasks if t is not None]
    if len(tasks) == 1:
        return tasks[0]
    cuts, a, o, s = [], 0, 0, 0
    aliases = {}
    for t in tasks:
        cuts.append((a, o, s))
        aliases.update({a + i: o + j for i, j in t.aliases.items()})
        a, o, s = a + len(t.arrays), o + len(t.out_shapes), s + len(t.sems)

    def part(fn_name):
        def run(ins, outs, sems):
            for t, (a0, o0, s0) in zip(tasks, cuts):
                getattr(t, fn_name)(ins[a0:a0 + len(t.arrays)], outs[o0:o0 + len(t.out_shapes)],
                                    sems[s0:s0 + len(t.sems)])
        return run

    return _Task([x for t in tasks for x in t.arrays], [x for t in tasks for x in t.out_shapes],
                 [x for t in tasks for x in t.sems], part("issue"), part("drain"), aliases)


def _pallas(body, *, grid, in_specs, out_specs, out_shape, name, scratch_shapes=(), task=None, prefetch=0):
    def make(kernel, i_specs, o_specs, o_shape, scratch, aliases):
        if prefetch:
            spec = pltpu.PrefetchScalarGridSpec(num_scalar_prefetch=prefetch, grid=grid, in_specs=i_specs,
                                                out_specs=o_specs, scratch_shapes=scratch)
            return pl.pallas_call(kernel, grid_spec=spec, out_shape=o_shape, input_output_aliases=aliases,
                                  name=name, compiler_params=_cparams())
        return pl.pallas_call(kernel, grid=grid, in_specs=i_specs, out_specs=o_specs, out_shape=o_shape,
                              scratch_shapes=scratch, input_output_aliases=aliases, name=name,
                              compiler_params=_cparams())

    if task is None:
        return make(body, list(in_specs), out_specs, out_shape, list(scratch_shapes), {})
    single = not isinstance(out_shape, (tuple, list))
    o_shapes = [out_shape] if single else list(out_shape)
    o_specs = [out_specs] if single else list(out_specs)
    n_in, n_out, n_scr = len(in_specs), len(o_shapes), len(scratch_shapes)
    t_in, t_out = len(task.arrays), len(task.out_shapes)

    def carried(*refs):
        cut = [prefetch, n_in, t_in, n_out, t_out, n_scr]
        parts, p = [], 0
        for c in cut:
            parts.append(refs[p:p + c])
            p += c
        scalars, ins, tins, outs, touts, scr = parts
        tsems = refs[p:]
        ids = [pl.program_id(a) for a in range(len(grid))]
        first = functools.reduce(jnp.logical_and, [i == 0 for i in ids])
        last = functools.reduce(jnp.logical_and, [i == g - 1 for i, g in zip(ids, grid)])

        @pl.when(first)
        def _():
            task.issue(tins, touts, tsems)
        body(*scalars, *ins, *outs, *scr)

        @pl.when(last)
        def _():
            task.drain(tins, touts, tsems)

    call = make(carried, list(in_specs) + [ANY] * t_in, o_specs + [ANY] * t_out,
                o_shapes + list(task.out_shapes), list(scratch_shapes) + list(task.sems),
                {prefetch + n_in + i: n_out + j for i, j in task.aliases.items()})

    def run(*args):
        res = call(*args, *task.arrays)
        outs = res[:n_out]
        return (outs[0] if single else tuple(outs)), list(res[n_out:])

    return run


def _dot(a, b):
    return jnp.dot(a, b, preferred_element_type=F32)


def _dot_nt(a, b):
    return lax.dot_general(a, b, (((1,), (1,)), ((), ())), preferred_element_type=F32)


def _dot_tn(a, b):
    return lax.dot_general(a, b, (((0,), (0,)), ((), ())), preferred_element_type=F32)


def _split3(x):
    hi = x.astype(BF16)
    r1 = x - hi.astype(F32)
    mid = r1.astype(BF16)
    lo = (r1 - mid.astype(F32)).astype(BF16)
    return hi, mid, lo


def _dot3(x, m01):
    hi, mid, lo = _split3(x)
    return _dot(hi, m01) + _dot(mid, m01) + _dot(lo, m01)


def _dot3_l(m01, x):
    hi, mid, lo = _split3(x)
    return _dot(m01, hi) + _dot(m01, mid) + _dot(m01, lo)


def _mm_nn(a, b, kind, *, name, out_dtype=BF16, tm=2048, tn=512, epi=None, task=None):
    M, K = a.shape
    if kind == "plain":
        N = b.shape[1]
        b_spec = pl.BlockSpec((K, tn), lambda m, n: (0, n))
    elif kind == "rows":
        N = b.shape[2]
        b_spec = pl.BlockSpec((N_CHIP, K // N_CHIP, tn), lambda m, n: (0, 0, n))
    else:
        nq = b.shape[2]
        N = N_CHIP * nq
        per = nq // tn
        b_spec = pl.BlockSpec((None, K, tn), lambda m, n: (n // per, 0, n % per))
    tm = min(tm, M)
    kq = K // N_CHIP

    def body(a_ref, b_ref, *o_refs):
        if kind == "rows":
            acc = _dot(a_ref[:, 0:kq], b_ref[0])
            for j in range(1, N_CHIP):
                acc += _dot(a_ref[:, j * kq:(j + 1) * kq], b_ref[j])
        else:
            acc = _dot(a_ref[...], b_ref[...])
        if epi == "relu2":
            r = jnp.maximum(acc, 0.0)
            o_refs[0][...] = (r * r).astype(BF16)
            o_refs[1][...] = r.astype(BF16)
        else:
            o_refs[0][...] = acc.astype(out_dtype)

    o_spec = pl.BlockSpec((tm, tn), lambda m, n: (m, n))
    if epi == "relu2":
        out_shape = (jax.ShapeDtypeStruct((M, N), BF16), jax.ShapeDtypeStruct((M, N), BF16))
        out_specs = (o_spec, o_spec)
    else:
        out_shape = jax.ShapeDtypeStruct((M, N), out_dtype)
        out_specs = o_spec
    return _pallas(
        body, grid=(M // tm, N // tn),
        in_specs=[pl.BlockSpec((tm, K), lambda m, n: (m, 0)), b_spec],
        out_specs=out_specs, out_shape=out_shape, name=name, task=task,
    )(a, b)


def _mm_nt(a, b, kind, *, name, out_dtype=BF16, tm=2048, tn=512, mul2r=None, task=None, rows=None):
    M, K = a.shape
    if kind == "plain":
        first, N = rows if rows is not None else (0, b.shape[0])
        n0 = first // tn
        b_spec = pl.BlockSpec((tn, K), lambda m, n: (n0 + n, 0))
    elif kind == "rows":
        nq = b.shape[1]
        N = N_CHIP * nq
        tn = min(tn, nq)
        per = nq // tn
        b_spec = pl.BlockSpec((None, tn, K), lambda m, n: (n // per, n % per, 0))
    else:
        N = b.shape[1]
        b_spec = pl.BlockSpec((N_CHIP, tn, K // N_CHIP), lambda m, n: (0, n, 0))
    tm = min(tm, M)
    kq = K // N_CHIP

    def body(a_ref, b_ref, *rest):
        o_ref = rest[-1]
        if kind == "cols":
            acc = _dot_nt(a_ref[:, 0:kq], b_ref[0])
            for j in range(1, N_CHIP):
                acc += _dot_nt(a_ref[:, j * kq:(j + 1) * kq], b_ref[j])
        else:
            acc = _dot_nt(a_ref[...], b_ref[...])
        if mul2r is not None:
            acc = acc * (2.0 * rest[0][...].astype(F32))
        o_ref[...] = acc.astype(out_dtype)

    in_specs = [pl.BlockSpec((tm, K), lambda m, n: (m, 0)), b_spec]
    args = [a, b]
    if mul2r is not None:
        in_specs.append(pl.BlockSpec((tm, tn), lambda m, n: (m, n)))
        args.append(mul2r)
    return _pallas(
        body, grid=(M // tm, N // tn), in_specs=in_specs,
        out_specs=pl.BlockSpec((tm, tn), lambda m, n: (m, n)),
        out_shape=jax.ShapeDtypeStruct((M, N), out_dtype), name=name, task=task,
    )(*args)


def _mm_tn(a, b, *, name, out_dtype=BF16, tk=1024, tn=512, cols4=False, task=None):
    M, K1 = a.shape
    N = b.shape[1]
    tk = min(tk, K1)
    tn = min(tn, N)

    def body(a_ref, b_ref, o_ref):
        o_ref[...] = _dot_tn(a_ref[...], b_ref[...]).astype(out_dtype)

    if cols4:
        per = (N // N_CHIP) // tn
        out_shape = jax.ShapeDtypeStruct((N_CHIP, K1, N // N_CHIP), out_dtype)
        o_spec = pl.BlockSpec((None, tk, tn), lambda k, n: (n // per, k, n % per))
    else:
        out_shape = jax.ShapeDtypeStruct((K1, N), out_dtype)
        o_spec = pl.BlockSpec((tk, tn), lambda k, n: (k, n))
    return _pallas(
        body, grid=(K1 // tk, N // tn),
        in_specs=[pl.BlockSpec((M, tk), lambda k, n: (0, k)), pl.BlockSpec((M, tn), lambda k, n: (0, n))],
        out_specs=o_spec, out_shape=out_shape, name=name, task=task,
    )(a, b)


def _rms(x, g):
    r = lax.rsqrt(jnp.mean(x * x, axis=-1, keepdims=True) + EPS)
    return x * r * g


def _rms_bwd(x, g, dy):
    r = lax.rsqrt(jnp.mean(x * x, axis=-1, keepdims=True) + EPS)
    xh = x * r
    dg = jnp.sum(dy * xh, axis=0, keepdims=True)
    dxh = dy * g
    dx = r * (dxh - xh * jnp.mean(dxh * xh, axis=-1, keepdims=True))
    return dx, dg


def _row_spec(tm, n):
    return pl.BlockSpec((tm, n), lambda i: (i, 0))


def _vec_spec(n):
    return pl.BlockSpec((1, n), lambda i: (0, 0))


def _acc_spec(n):
    return pl.BlockSpec((8, n), lambda i: (0, 0))


def _acc_add(ref, row, i):
    @pl.when(i == 0)
    def _():
        ref[...] = jnp.zeros_like(ref)
    ref[0:1, :] += row


def _rms_fwd_call(x, g, *, name, tm=1024, task=None):
    M, n = x.shape
    tm = min(tm, M)

    def body(x_ref, g_ref, h_ref):
        h_ref[...] = _rms(x_ref[...], g_ref[...]).astype(BF16)

    return _pallas(
        body, grid=(M // tm,), in_specs=[_row_spec(tm, n), _vec_spec(n)], out_specs=_row_spec(tm, n),
        out_shape=jax.ShapeDtypeStruct((M, n), BF16), name=name, task=task,
    )(x, g)


def _post_pre_call(xres, z, g_post, g_pre, *, name, tm=1024):
    M, n = xres.shape

    def body(x_ref, z_ref, gp_ref, gn_ref, xo_ref, h_ref):
        xn = x_ref[...] + _rms(z_ref[...], gp_ref[...])
        xo_ref[...] = xn
        h_ref[...] = _rms(xn, gn_ref[...]).astype(BF16)

    return pl.pallas_call(
        body, grid=(M // tm,),
        in_specs=[_row_spec(tm, n), _row_spec(tm, n), _vec_spec(n), _vec_spec(n)],
        out_specs=(_row_spec(tm, n), _row_spec(tm, n)),
        out_shape=(jax.ShapeDtypeStruct((M, n), F32), jax.ShapeDtypeStruct((M, n), BF16)),
        name=name, compiler_params=_cparams(),
    )(xres, z, g_post, g_pre)


def _final_call(x2, y3, g_post, target, *, name, tm=512):
    M, n = x2.shape

    def body(x_ref, y_ref, g_ref, t_ref, loss_ref, dx_ref, dy_ref, dg_ref):
        i = pl.program_id(0)
        y = y_ref[...]
        g = g_ref[...]
        diff = x_ref[...] + _rms(y, g) - t_ref[...]
        part = 0.5 * jnp.sum(jnp.sum(diff * diff, axis=1, keepdims=True), axis=0, keepdims=True) / n

        @pl.when(i == 0)
        def _():
            loss_ref[...] = jnp.zeros_like(loss_ref)
        loss_ref[...] += jnp.broadcast_to(part, loss_ref.shape)
        dx = diff / n
        dx_ref[...] = dx
        dy, dg = _rms_bwd(y, g, dx)
        dy_ref[...] = dy.astype(BF16)
        _acc_add(dg_ref, dg, i)

    return pl.pallas_call(
        body, grid=(M // tm,),
        in_specs=[_row_spec(tm, n), _row_spec(tm, n), _vec_spec(n), _row_spec(tm, n)],
        out_specs=(pl.BlockSpec((8, 128), lambda i: (0, 0)), _row_spec(tm, n), _row_spec(tm, n), _acc_spec(n)),
        out_shape=(jax.ShapeDtypeStruct((8, 128), F32), jax.ShapeDtypeStruct((M, n), F32),
                   jax.ShapeDtypeStruct((M, n), BF16), jax.ShapeDtypeStruct((8, n), F32)),
        name=name, compiler_params=_cparams(),
    )(x2, y3, g_post, target)


def _bwd_mid_call(dx_in, x, dh, g_pre, y, g_post, *, name, tm=512):
    M, n = x.shape

    def body(dxi_ref, x_ref, dh_ref, gpre_ref, y_ref, gpost_ref, dx_ref, dy_ref, dgpre_ref, dgpost_ref):
        i = pl.program_id(0)
        d1, dg1 = _rms_bwd(x_ref[...], gpre_ref[...], dh_ref[...])
        dx = dxi_ref[...] + d1
        dx_ref[...] = dx
        dy, dg2 = _rms_bwd(y_ref[...], gpost_ref[...], dx)
        dy_ref[...] = dy.astype(BF16)
        _acc_add(dgpre_ref, dg1, i)
        _acc_add(dgpost_ref, dg2, i)

    return pl.pallas_call(
        body, grid=(M // tm,),
        in_specs=[_row_spec(tm, n), _row_spec(tm, n), _row_spec(tm, n), _vec_spec(n), _row_spec(tm, n), _vec_spec(n)],
        out_specs=(_row_spec(tm, n), _row_spec(tm, n), _acc_spec(n), _acc_spec(n)),
        out_shape=(jax.ShapeDtypeStruct((M, n), F32), jax.ShapeDtypeStruct((M, n), BF16),
                   jax.ShapeDtypeStruct((8, n), F32), jax.ShapeDtypeStruct((8, n), F32)),
        name=name, compiler_params=_cparams(),
    )(dx_in, x, dh, g_pre, y, g_post)


def _bwd_last_call(dx_in, x, dh, g_pre, *, name, tm=1024, task=None):
    M, n = x.shape

    def body(dxi_ref, x_ref, dh_ref, g_ref, dx_ref, dg_ref):
        i = pl.program_id(0)
        d1, dg1 = _rms_bwd(x_ref[...], g_ref[...], dh_ref[...])
        dx_ref[...] = dxi_ref[...] + d1
        _acc_add(dg_ref, dg1, i)

    return _pallas(
        body, grid=(M // tm,),
        in_specs=[_row_spec(tm, n), _row_spec(tm, n), _row_spec(tm, n), _vec_spec(n)],
        out_specs=(_row_spec(tm, n), _acc_spec(n)),
        out_shape=(jax.ShapeDtypeStruct((M, n), F32), jax.ShapeDtypeStruct((8, n), F32)),
        name=name, task=task,
    )(dx_in, x, dh, g_pre)


def _gain_grad_call(x, g, dy_a, dy_b, *, name):
    M, n = x.shape

    def body(x_ref, g_ref, a_ref, b_ref, dg_ref):
        _, dg = _rms_bwd(x_ref[...], g_ref[...], a_ref[...] + b_ref[...])
        dg_ref[...] = jnp.zeros_like(dg_ref)
        dg_ref[0:1, :] = dg

    return pl.pallas_call(
        body, grid=(1,),
        in_specs=[_row_spec(M, n), _vec_spec(n), _row_spec(M, n), _row_spec(M, n)],
        out_specs=_acc_spec(n), out_shape=jax.ShapeDtypeStruct((8, n), F32),
        name=name, compiler_params=_cparams(),
    )(x, g, dy_a, dy_b)


def _head_group_matrix():
    a = lax.broadcasted_iota(jnp.int32, (D_GRP, D_GRP), 0) // HEAD
    b = lax.broadcasted_iota(jnp.int32, (D_GRP, D_GRP), 1) // HEAD
    return jnp.where(a == b, 1.0, 0.0).astype(BF16)


def _mix_norm_fwd_call(yf, yc, gf, gc, *, name, tm=1024):
    M = yf.shape[0]

    def body(yf_ref, yc_ref, gf_ref, gc_ref, o_ref):
        o_ref[:, 0:D_GRP] = _rms(yf_ref[...], gf_ref[...]).astype(BF16)
        o_ref[:, D_GRP:D] = _rms(yc_ref[...], gc_ref[...]).astype(BF16)

    return pl.pallas_call(
        body, grid=(M // tm,),
        in_specs=[_row_spec(tm, D_GRP), _row_spec(tm, D_GRP), _vec_spec(D_GRP), _vec_spec(D_GRP)],
        out_specs=_row_spec(tm, D), out_shape=jax.ShapeDtypeStruct((M, D), BF16),
        name=name, compiler_params=_cparams(),
    )(yf, yc, gf, gc)


def _mix_norm_bwd_call(dyn, yf, yc, gf, gc, *, name, tm=2 * TQ):
    M = yf.shape[0]

    def body(dyn_ref, yf_ref, yc_ref, gf_ref, gc_ref, dof_ref, doc_ref, delta_ref, dgf_ref, dgc_ref):
        i = pl.program_id(0)
        yf_ = yf_ref[...]
        dof, dgf = _rms_bwd(yf_, gf_ref[...], dyn_ref[:, 0:D_GRP])
        doc, dgc = _rms_bwd(yc_ref[...], gc_ref[...], dyn_ref[:, D_GRP:D])
        dof_b = dof.astype(BF16)
        dof_ref[...] = dof_b
        doc_ref[...] = doc.astype(BF16)
        prod = dof_b.astype(F32) * yf_
        hi = prod.astype(BF16)
        lo = (prod - hi.astype(F32)).astype(BF16)
        grp = _head_group_matrix()
        delta = _dot(hi, grp) + _dot(lo, grp)
        for b in range(tm // TQ):
            delta_ref[b] = delta[b * TQ:(b + 1) * TQ, :].T
        _acc_add(dgf_ref, dgf, i)
        _acc_add(dgc_ref, dgc, i)

    return pl.pallas_call(
        body, grid=(M // tm,),
        in_specs=[_row_spec(tm, D), _row_spec(tm, D_GRP), _row_spec(tm, D_GRP), _vec_spec(D_GRP), _vec_spec(D_GRP)],
        out_specs=(_row_spec(tm, D_GRP), _row_spec(tm, D_GRP),
                   pl.BlockSpec((tm // TQ, D_GRP, TQ), lambda i: (i, 0, 0)), _acc_spec(D_GRP), _acc_spec(D_GRP)),
        out_shape=(jax.ShapeDtypeStruct((M, D_GRP), BF16), jax.ShapeDtypeStruct((M, D_GRP), BF16),
                   jax.ShapeDtypeStruct((M // TQ, D_GRP, TQ), F32), jax.ShapeDtypeStruct((8, D_GRP), F32),
                   jax.ShapeDtypeStruct((8, D_GRP), F32)),
        name=name, compiler_params=_cparams(),
    )(dyn, yf, yc, gf, gc)


def _tri(n, lower_incl):
    a = lax.broadcasted_iota(jnp.int32, (n, n), 0)
    b = lax.broadcasted_iota(jnp.int32, (n, n), 1)
    return jnp.where(a >= b, 1.0, 0.0).astype(BF16) if lower_incl else jnp.where(a <= b, 1.0, 0.0).astype(BF16)


def _fox_prep_call(fl_raw, b_pad, *, name):
    S = fl_raw.shape[0]
    nb = S // TQ

    def body(fl_ref, b_ref, crep_ref, ct_ref, carry_ref):
        i = pl.program_id(0)

        @pl.when(i == 0)
        def _():
            carry_ref[...] = jnp.zeros_like(carry_ref)
        logf = jax.nn.log_sigmoid(fl_ref[...] + b_ref[...])
        cb = _dot3_l(_tri(TQ, True), logf) + carry_ref[0:1, :]
        carry_ref[0:1, :] = cb[TQ - 1:TQ, :]
        a = lax.broadcasted_iota(jnp.int32, (128, D_GRP), 0)
        b = lax.broadcasted_iota(jnp.int32, (128, D_GRP), 1) // HEAD
        expand = jnp.where(a == b, 1.0, 0.0).astype(BF16)
        crep = _dot3(cb, expand)
        crep_ref[...] = crep
        ct_ref[...] = crep.T

    return pl.pallas_call(
        body, grid=(nb,),
        in_specs=[_row_spec(TQ, 128), _vec_spec(128)],
        out_specs=(_row_spec(TQ, D_GRP), pl.BlockSpec((None, D_GRP, TQ), lambda i: (i, 0, 0))),
        out_shape=(jax.ShapeDtypeStruct((S, D_GRP), F32), jax.ShapeDtypeStruct((nb, D_GRP, TQ), F32)),
        scratch_shapes=[pltpu.VMEM((8, 128), F32)],
        name=name, compiler_params=_cparams(),
    )(fl_raw, b_pad)


def _lane_masks():
    lane = lax.broadcasted_iota(jnp.int32, (1, 128), 1)
    return lane < HEAD, lane >= HEAD


def _fox_fwd_call(proj, c_rep, c_t, *, name, task=None):
    S = proj.shape[0]
    nq = S // TQ
    scale = HEAD ** -0.5

    def body(q_ref, k_ref, v_ref, c_ref, ct_ref, o_ref, lse_ref):
        i = pl.program_id(1)
        m_lo, m_hi = _lane_masks()
        masks = (m_lo, m_hi)
        q = q_ref[...] * scale
        qm = [jnp.where(mk, q, jnp.zeros_like(q)) for mk in masks]
        cq = c_ref[...]
        cqh = [cq[:, 0:1], cq[:, HEAD:HEAD + 1]]
        row = lax.broadcasted_iota(jnp.int32, (TQ, TQ), 0)
        col = lax.broadcasted_iota(jnp.int32, (TQ, TQ), 1)

        def scores(j):
            start = pl.multiple_of(j * TQ, TQ)
            k = k_ref[pl.ds(start, TQ), :]
            ct = ct_ref[j]
            return tuple(_dot_nt(qm[h], k) + (cqh[h] - ct[HEAD * h:HEAD * h + 1, :]) for h in range(2))

        def update(j, ss, state, masked):
            ms, ls, acc = state
            start = pl.multiple_of(j * TQ, TQ)
            v = v_ref[pl.ds(start, TQ), :]
            new_m, new_l, pv, alpha_l = [], [], [], []
            for h in range(2):
                s = ss[h]
                if masked:
                    s = jnp.where(row >= col, s, NEG)
                mn = jnp.maximum(ms[h], jnp.max(s, axis=1, keepdims=True))
                alpha = jnp.exp(ms[h] - mn)
                p = jnp.exp(s - mn)
                new_l.append(alpha * ls[h] + jnp.sum(p, axis=1, keepdims=True))
                new_m.append(mn)
                alpha_l.append(alpha)
                pv.append(_dot(p.astype(BF16), jnp.where(masks[h], v, jnp.zeros_like(v))))
            alpha_lane = jnp.where(m_lo, alpha_l[0], alpha_l[1])
            acc = acc * alpha_lane + pv[0] + pv[1]
            return (tuple(new_m), tuple(new_l), acc)

        def step(j, carry):
            ss, state = carry
            return (scores(j + 1), update(j, ss, state, False))

        init = ((jnp.full((TQ, 1), NEG, F32),) * 2, (jnp.zeros((TQ, 1), F32),) * 2, jnp.zeros((TQ, 128), F32))
        ss, state = lax.fori_loop(0, i, step, (scores(0), init))
        ms, ls, acc = update(i, ss, state, True)
        l_lane = jnp.where(m_lo, ls[0], ls[1])
        o_ref[...] = acc / l_lane
        lse_ref[...] = jnp.where(m_lo, ms[0] + jnp.log(ls[0]), ms[1] + jnp.log(ls[1])).T

    return _pallas(
        body, grid=(N_PAIR, nq),
        in_specs=[pl.BlockSpec((TQ, 128), lambda p, i: (i, p)),
                  pl.BlockSpec((S, 128), lambda p, i: (0, N_PAIR + p)),
                  pl.BlockSpec((S, 128), lambda p, i: (0, 2 * N_PAIR + p)),
                  pl.BlockSpec((TQ, 128), lambda p, i: (i, p)),
                  pl.BlockSpec((nq, 128, TQ), lambda p, i: (0, p, 0))],
        out_specs=(pl.BlockSpec((TQ, 128), lambda p, i: (i, p)), pl.BlockSpec((None, 128, TQ), lambda p, i: (i, p, 0))),
        out_shape=(jax.ShapeDtypeStruct((S, D_GRP), F32), jax.ShapeDtypeStruct((nq, D_GRP, TQ), F32)),
        name=name, task=task,
    )(proj, proj, proj, c_rep, c_t)


def _fox_bwd_call(proj, do, lse_t, delta_t, c_rep, c_t, *, name, task=None):
    S = proj.shape[0]
    nq = S // TQ
    scale = HEAD ** -0.5

    def body(q_ref, k_ref, v_ref, do_ref, lse_ref, dl_ref, ck_ref, ct_ref,
             dq_ref, dk_ref, dv_ref, dcq_ref, dck_ref, dqa_ref):
        j = pl.program_id(1)
        m_lo, m_hi = _lane_masks()
        masks = (m_lo, m_hi)

        @pl.when(j == 0)
        def _():
            dqa_ref[...] = jnp.zeros_like(dqa_ref)
            dcq_ref[...] = jnp.zeros_like(dcq_ref)
        k = k_ref[...]
        v = v_ref[...]
        km = [jnp.where(mk, k, jnp.zeros_like(k)) for mk in masks]
        ck = ck_ref[...]
        krow = lax.broadcasted_iota(jnp.int32, (TQ, TQ), 0)
        qcol = lax.broadcasted_iota(jnp.int32, (TQ, TQ), 1)

        def probs(i):
            start = pl.multiple_of(i * TQ, TQ)
            q = q_ref[pl.ds(start, TQ), :]
            do = do_ref[pl.ds(start, TQ), :]
            lse = lse_ref[i]
            cq = ct_ref[i]
            out = []
            for h in range(2):
                lo = HEAD * h
                qm = jnp.where(masks[h], q * scale, jnp.zeros_like(q))
                dom = jnp.where(masks[h], do, jnp.zeros_like(do))
                st = _dot_nt(k, qm) + (cq[lo:lo + 1, :] - ck[:, lo:lo + 1])
                out.append((jnp.exp(st - lse[lo:lo + 1, :]), _dot_nt(v, dom)))
            return tuple(out)

        def update(i, pd, carry, masked):
            dk, dv, dck = carry
            start = pl.multiple_of(i * TQ, TQ)
            q = q_ref[pl.ds(start, TQ), :]
            do = do_ref[pl.ds(start, TQ), :]
            dl = dl_ref[i]
            dq = jnp.zeros((TQ, 128), F32)
            new_dck = []
            for h in range(2):
                lo = HEAD * h
                qm = jnp.where(masks[h], q, jnp.zeros_like(q))
                dom = jnp.where(masks[h], do, jnp.zeros_like(do))
                pt, dpt = pd[h]
                if masked:
                    pt = jnp.where(qcol >= krow, pt, 0.0)
                dst = pt * (dpt - dl[lo:lo + 1, :])
                dcq_ref[i, h:h + 1, :] += jnp.sum(dst, axis=0, keepdims=True)
                new_dck.append(dck[h] + jnp.sum(dst, axis=1, keepdims=True))
                dsb = (dst * scale).astype(BF16)
                dv = dv + _dot(pt.astype(BF16), dom)
                dk = dk + _dot(dsb, qm)
                dq = dq + _dot_tn(dsb, km[h])
            dqa_ref[pl.ds(start, TQ), :] += dq
            return (dk, dv, tuple(new_dck))

        def step(i, carry):
            pd, sums = carry
            return (probs(jnp.minimum(i + 1, nq - 1)), update(i, pd, sums, False))

        init = (jnp.zeros((TQ, 128), F32), jnp.zeros((TQ, 128), F32), (jnp.zeros((TQ, 1), F32),) * 2)
        first = probs(j)
        second = probs(jnp.minimum(j + 1, nq - 1))
        _, (dk, dv, dck) = lax.fori_loop(j + 1, nq, step, (second, update(j, first, init, True)))
        dk_ref[...] = dk.astype(BF16)
        dv_ref[...] = dv.astype(BF16)
        dck_t = jnp.where(m_lo, dck[0], dck[1]).T
        row = lax.broadcasted_iota(jnp.int32, (8, TQ), 0)
        dck_ref[...] = -jnp.where(row == 0, dck_t[0:1, :], jnp.where(row == 1, dck_t[HEAD:HEAD + 1, :], 0.0))

        @pl.when(j == nq - 1)
        def _():
            dq_ref[...] = dqa_ref[...].astype(BF16)

    res = lambda p, j: (0, p)
    stat = pl.BlockSpec((nq, 128, TQ), lambda p, j: (0, p, 0))
    blk = pl.BlockSpec((TQ, 128), lambda p, j: (j, p))
    return _pallas(
        body, grid=(N_PAIR, nq), task=task,
        in_specs=[pl.BlockSpec((S, 128), res),
                  pl.BlockSpec((TQ, 128), lambda p, j: (j, N_PAIR + p)),
                  pl.BlockSpec((TQ, 128), lambda p, j: (j, 2 * N_PAIR + p)),
                  pl.BlockSpec((S, 128), res), stat, stat, blk, stat],
        out_specs=(pl.BlockSpec((S, 128), res), blk, blk,
                   pl.BlockSpec((None, nq, 8, TQ), lambda p, j: (p, 0, 0, 0)),
                   pl.BlockSpec((None, None, 8, TQ), lambda p, j: (p, j, 0, 0))),
        out_shape=(jax.ShapeDtypeStruct((S, D_GRP), BF16), jax.ShapeDtypeStruct((S, D_GRP), BF16),
                   jax.ShapeDtypeStruct((S, D_GRP), BF16), jax.ShapeDtypeStruct((N_PAIR, nq, 8, TQ), F32),
                   jax.ShapeDtypeStruct((N_PAIR, nq, 8, TQ), F32)),
        scratch_shapes=[pltpu.VMEM((S, 128), F32)],
        name=name,
    )(proj, proj, proj, do, lse_t, delta_t, c_rep, c_t)


def _fox_gate_bwd_call(dc_rows, fl_raw, b_pad, *, name):
    S = fl_raw.shape[0]
    nb = S // TQ

    def body(dc_ref, fl_ref, b_ref, dfl_ref, db_ref, carry_ref):
        i = pl.program_id(0)

        @pl.when(i == 0)
        def _():
            carry_ref[...] = jnp.zeros_like(carry_ref)
        rc = _dot3(dc_ref[...], _tri(TQ, True)) + carry_ref[:, 0:1]
        carry_ref[...] = jnp.broadcast_to(rc[:, 0:1], carry_ref.shape)
        fl = fl_ref[...] + b_ref[...]
        dfl = rc.T * jax.nn.sigmoid(-fl)
        dfl_ref[...] = dfl.astype(BF16)
        _acc_add(db_ref, jnp.sum(dfl, axis=0, keepdims=True), i)

    rev = lambda i: (nb - 1 - i, 0)
    return pl.pallas_call(
        body, grid=(nb,),
        in_specs=[pl.BlockSpec((128, TQ), lambda i: (0, nb - 1 - i)), pl.BlockSpec((TQ, 128), rev), _vec_spec(128)],
        out_specs=(pl.BlockSpec((TQ, 128), rev), _acc_spec(128)),
        out_shape=(jax.ShapeDtypeStruct((S, 128), BF16), jax.ShapeDtypeStruct((8, 128), F32)),
        scratch_shapes=[pltpu.VMEM((128, 128), F32)],
        name=name, compiler_params=_cparams(),
    )(dc_rows, fl_raw, b_pad)


def _chk_bias_call(g_rev, *, name):
    def body(g_ref, o_ref):
        x = jnp.broadcast_to(g_ref[...], (TQ, ROLL_W))
        rolled = pltpu.roll(x, ROLL_W - (TQ - 1), 1, stride=1, stride_axis=0)
        qc = lax.broadcasted_iota(jnp.int32, (TQ, WIN), 0) // CHUNK
        kc = lax.broadcasted_iota(jnp.int32, (TQ, WIN), 1) // CHUNK
        band = (kc >= qc) & (kc <= qc + LEFT)
        o_ref[...] = jnp.where(band, rolled[:, 0:WIN], NEG)

    return pl.pallas_call(
        body, grid=(8,),
        in_specs=[pl.BlockSpec((None, 1, ROLL_W), lambda h: (h, 0, 0))],
        out_specs=pl.BlockSpec((None, TQ, WIN), lambda h: (h, 0, 0)),
        out_shape=jax.ShapeDtypeStruct((8, TQ, WIN), F32), name=name, compiler_params=_cparams(),
    )(g_rev.reshape(8, 1, ROLL_W))


def _chk_scores(i, qm, kwin, bias, scale):
    s = _dot_nt(qm * scale, kwin) + bias
    kc = lax.broadcasted_iota(jnp.int32, (TQ, WIN), 1) // CHUNK
    return jnp.where(kc + i * (TQ // CHUNK) >= LEFT, s, NEG)


def _chk_fwd_call(proj, bias, *, name, task=None):
    S = proj.shape[0]
    nq = S // TQ
    scale = HEAD ** -0.5

    def body(q_ref, k_ref, v_ref, b_ref, o_ref, kp_ref, vp_ref):
        i = pl.program_id(1)

        @pl.when(i == 0)
        def _():
            kp_ref[0:PADK, :] = jnp.zeros((PADK, 128), BF16)
            vp_ref[0:PADK, :] = jnp.zeros((PADK, 128), BF16)
            kp_ref[PADK:PADK + S, :] = k_ref[...]
            vp_ref[PADK:PADK + S, :] = v_ref[...]
        masks = _lane_masks()
        q = q_ref[...]
        start = pl.multiple_of(i * TQ, TQ)
        kwin = kp_ref[pl.ds(start, WIN), :]
        vwin = vp_ref[pl.ds(start, WIN), :]
        ss = [_chk_scores(i, jnp.where(masks[h], q, jnp.zeros_like(q)), kwin, b_ref[h], scale) for h in range(2)]
        ps = []
        for s in ss:
            p = jnp.exp(s - jnp.max(s, axis=1, keepdims=True))
            ps.append((p / jnp.sum(p, axis=1, keepdims=True)).astype(BF16))
        o_ref[...] = (_dot(ps[0], jnp.where(masks[0], vwin, jnp.zeros_like(vwin)))
                      + _dot(ps[1], jnp.where(masks[1], vwin, jnp.zeros_like(vwin))))

    c0 = 3 * N_PAIR
    return _pallas(
        body, grid=(N_PAIR, nq), task=task,
        in_specs=[pl.BlockSpec((TQ, 128), lambda p, i: (i, c0 + p)),
                  pl.BlockSpec((S, 128), lambda p, i: (0, c0 + N_PAIR + p)),
                  pl.BlockSpec((S, 128), lambda p, i: (0, c0 + 2 * N_PAIR + p)),
                  pl.BlockSpec((2, TQ, WIN), lambda p, i: (p, 0, 0))],
        out_specs=pl.BlockSpec((TQ, 128), lambda p, i: (i, p)),
        out_shape=jax.ShapeDtypeStruct((S, D_GRP), F32),
        scratch_shapes=[pltpu.VMEM((S + PADK, 128), BF16), pltpu.VMEM((S + PADK, 128), BF16)],
        name=name,
    )(proj, proj, proj, bias)


def _chk_bwd_call(proj, do, bias, *, name, task=None):
    S = proj.shape[0]
    nq = S // TQ
    scale = HEAD ** -0.5

    def body(q_ref, k_ref, v_ref, do_ref, b_ref, dq_ref, dk_ref, dv_ref, dg_ref, kp_ref, vp_ref, dkp_ref, dvp_ref, db_ref):
        i = pl.program_id(1)

        @pl.when(i == 0)
        def _():
            kp_ref[0:PADK, :] = jnp.zeros((PADK, 128), BF16)
            vp_ref[0:PADK, :] = jnp.zeros((PADK, 128), BF16)
            kp_ref[PADK:PADK + S, :] = k_ref[...]
            vp_ref[PADK:PADK + S, :] = v_ref[...]
            dkp_ref[...] = jnp.zeros_like(dkp_ref)
            dvp_ref[...] = jnp.zeros_like(dvp_ref)
            db_ref[...] = jnp.zeros_like(db_ref)
        masks = _lane_masks()
        q = q_ref[...]
        dout = do_ref[...]
        start = pl.multiple_of(i * TQ, TQ)
        kwin = kp_ref[pl.ds(start, WIN), :]
        vwin = vp_ref[pl.ds(start, WIN), :]
        qm = [jnp.where(mk, q, jnp.zeros_like(q)) for mk in masks]
        dom = [jnp.where(mk, dout, jnp.zeros_like(dout)) for mk in masks]
        ss = [_chk_scores(i, qm[h], kwin, b_ref[h], scale) for h in range(2)]
        dps = [_dot_nt(dom[h], vwin) for h in range(2)]
        pbs, dsbs = [], []
        for h in range(2):
            p = jnp.exp(ss[h] - jnp.max(ss[h], axis=1, keepdims=True))
            p = p / jnp.sum(p, axis=1, keepdims=True)
            ds = p * (dps[h] - jnp.sum(p * dps[h], axis=1, keepdims=True))
            db_ref[h] += ds
            pbs.append(p.astype(BF16))
            dsbs.append((ds * scale).astype(BF16))
        dq_ref[...] = (_dot(dsbs[0], jnp.where(masks[0], kwin, jnp.zeros_like(kwin)))
                       + _dot(dsbs[1], jnp.where(masks[1], kwin, jnp.zeros_like(kwin)))).astype(BF16)
        dkp_ref[pl.ds(start, WIN), :] += _dot_tn(dsbs[0], qm[0]) + _dot_tn(dsbs[1], qm[1])
        dvp_ref[pl.ds(start, WIN), :] += _dot_tn(pbs[0], dom[0]) + _dot_tn(pbs[1], dom[1])

        @pl.when(i == nq - 1)
        def _():
            dk_ref[...] = dkp_ref[PADK:PADK + S, :].astype(BF16)
            dv_ref[...] = dvp_ref[PADK:PADK + S, :].astype(BF16)
            a = lax.broadcasted_iota(jnp.int32, (TQ, TQ), 0)
            b = lax.broadcasted_iota(jnp.int32, (TQ, TQ), 1)
            flip = jnp.where(a + b == TQ - 1, 1.0, 0.0).astype(BF16)
            e = lax.broadcasted_iota(jnp.int32, (1, ROLL_W), 1)
            dg_ref[...] = jnp.zeros_like(dg_ref)
            for h in range(2):
                rev = _dot3_l(flip, db_ref[h])
                wide = jnp.concatenate([rev, jnp.zeros((TQ, ROLL_W - WIN), F32)], axis=1)
                diag = pltpu.roll(wide, 0, 1, stride=1, stride_axis=0)
                dg = jnp.sum(diag, axis=0, keepdims=True)
                lo = jnp.sum(jnp.where(e <= 639, dg, 0.0), axis=1, keepdims=True)
                hi = jnp.sum(jnp.where(e >= 895, dg, 0.0), axis=1, keepdims=True)
                dg_ref[h:h + 1, :] = jnp.where(e == 639, lo, jnp.where(e == 895, hi, dg))

    c0 = 3 * N_PAIR
    res = lambda p, i: (0, p)
    return _pallas(
        body, grid=(N_PAIR, nq), task=task,
        in_specs=[pl.BlockSpec((TQ, 128), lambda p, i: (i, c0 + p)),
                  pl.BlockSpec((S, 128), lambda p, i: (0, c0 + N_PAIR + p)),
                  pl.BlockSpec((S, 128), lambda p, i: (0, c0 + 2 * N_PAIR + p)),
                  pl.BlockSpec((TQ, 128), lambda p, i: (i, p)),
                  pl.BlockSpec((2, TQ, WIN), lambda p, i: (p, 0, 0))],
        out_specs=(pl.BlockSpec((TQ, 128), lambda p, i: (i, p)), pl.BlockSpec((S, 128), res),
                   pl.BlockSpec((S, 128), res), pl.BlockSpec((None, 8, ROLL_W), lambda p, i: (p, 0, 0))),
        out_shape=(jax.ShapeDtypeStruct((S, D_GRP), BF16), jax.ShapeDtypeStruct((S, D_GRP), BF16),
                   jax.ShapeDtypeStruct((S, D_GRP), BF16), jax.ShapeDtypeStruct((N_PAIR, 8, ROLL_W), F32)),
        scratch_shapes=[pltpu.VMEM((S + PADK, 128), BF16), pltpu.VMEM((S + PADK, 128), BF16),
                        pltpu.VMEM((S + PADK, 128), F32), pltpu.VMEM((S + PADK, 128), F32),
                        pltpu.VMEM((2, TQ, WIN), F32)],
        name=name,
    )(proj, proj, proj, do, bias)


def _mem_fwd_call(q, k, v, *, name, tq=2048):
    S = q.shape[0]
    scale = MEM_HD ** -0.5

    def body(q_ref, k_ref, v_ref, o_ref):
        s = _dot_nt(q_ref[...] * scale, k_ref[...])
        p = jnp.exp(s - jnp.max(s, axis=1, keepdims=True))
        p = p / jnp.sum(p, axis=1, keepdims=True)
        o_ref[...] = _dot(p.astype(BF16), v_ref[...]).astype(BF16)

    return pl.pallas_call(
        body, grid=(MEM_HEADS, S // tq),
        in_specs=[pl.BlockSpec((tq, MEM_HD), lambda h, i: (i, h)),
                  pl.BlockSpec((N_MEM, MEM_HD), lambda h, i: (0, h)),
                  pl.BlockSpec((N_MEM, MEM_HD), lambda h, i: (0, h))],
        out_specs=pl.BlockSpec((tq, MEM_HD), lambda h, i: (i, h)),
        out_shape=jax.ShapeDtypeStruct((S, D), BF16), name=name, compiler_params=_cparams(),
    )(q, k, v)


def _mem_bwd_call(q, k, v, do, *, name, tq=2048):
    S = q.shape[0]
    n = S // tq
    scale = MEM_HD ** -0.5

    def body(q_ref, k_ref, v_ref, do_ref, dq_ref, dk_ref, dv_ref, dka_ref, dva_ref):
        i = pl.program_id(1)

        @pl.when(i == 0)
        def _():
            dka_ref[...] = jnp.zeros_like(dka_ref)
            dva_ref[...] = jnp.zeros_like(dva_ref)
        qb = q_ref[...]
        kb = k_ref[...]
        dob = do_ref[...]
        s = _dot_nt(qb * scale, kb)
        p = jnp.exp(s - jnp.max(s, axis=1, keepdims=True))
        p = p / jnp.sum(p, axis=1, keepdims=True)
        dp = _dot_nt(dob, v_ref[...])
        ds = p * (dp - jnp.sum(p * dp, axis=1, keepdims=True))
        dsb = (ds * scale).astype(BF16)
        dq_ref[...] = _dot(dsb, kb).astype(BF16)
        dka_ref[...] += _dot_tn(dsb, qb)
        dva_ref[...] += _dot_tn(p.astype(BF16), dob)

        @pl.when(i == n - 1)
        def _():
            dk_ref[...] = dka_ref[...].astype(BF16)
            dv_ref[...] = dva_ref[...].astype(BF16)

    kv = pl.BlockSpec((N_MEM, MEM_HD), lambda h, i: (0, h))
    qs = pl.BlockSpec((tq, MEM_HD), lambda h, i: (i, h))
    return pl.pallas_call(
        body, grid=(MEM_HEADS, n), in_specs=[qs, kv, kv, qs], out_specs=(qs, kv, kv),
        out_shape=(jax.ShapeDtypeStruct((S, D), BF16), jax.ShapeDtypeStruct((N_MEM, D), BF16),
                   jax.ShapeDtypeStruct((N_MEM, D), BF16)),
        scratch_shapes=[pltpu.VMEM((N_MEM, MEM_HD), F32), pltpu.VMEM((N_MEM, MEM_HD), F32)],
        name=name, compiler_params=_cparams(),
    )(q, k, v, do)


def _rel_table_to_g(rel):
    return jnp.concatenate([
        jnp.broadcast_to(rel[:, N_REL - 1:N_REL], (8, 640)),
        rel[:, 1:N_REL - 1][:, ::-1],
        jnp.broadcast_to(rel[:, 0:1], (8, 129)),
    ], axis=1)


def _g_to_rel_table(dg):
    return dg[:, 639:896][:, ::-1]


def _place():
    x, y, c = lax.axis_index("x"), lax.axis_index("y"), lax.axis_index("c")
    others = [(1 - x, y), (x, 1 - y), (1 - x, 1 - y)]
    return x, y, c, others


def _half(c, rows):
    hr = rows // 2
    return pl.ds(pl.multiple_of(c * hr, 16), hr)


def _dma_sems(*shape):
    return pltpu.SemaphoreType.DMA(shape)


def _cast_slabs_call(ws, chip_arr, *, name, tm=256, task=None):
    n = len(ws)
    cols = ws[0].shape[1]
    tiles = [w.shape[0] // tm for w in ws]
    steps = max(tiles)

    def body(chip_ref, *refs):
        i = pl.program_id(0)
        for k in range(n):
            def cast(k=k):
                refs[n + k][...] = refs[k][...].astype(BF16)
            if tiles[k] == steps:
                cast()
            else:
                pl.when(i < tiles[k])(cast)

    in_specs = [pl.BlockSpec((tm, cols), lambda i, chip, t=t: (jnp.minimum(i, t - 1), 0)) for t in tiles]
    out_specs = [pl.BlockSpec((None, tm, cols), lambda i, chip, t=t: (chip[0], jnp.minimum(i, t - 1), 0)) for t in tiles]
    out_shape = [jax.ShapeDtypeStruct((N_CHIP,) + w.shape, BF16) for w in ws]
    return _pallas(body, grid=(steps,), in_specs=in_specs, out_specs=out_specs, out_shape=out_shape, name=name,
                   task=task, prefetch=1)(chip_arr, *ws)


def _cast_slab_call(w, chip_arr, *, name, tm=256, pad_rows=0):
    rows, cols = w.shape
    if pad_rows:
        tm = rows
    tm = min(tm, rows)

    def body(chip_ref, w_ref, o_ref):
        o_ref[0:tm, :] = w_ref[...].astype(BF16)
        if pad_rows:
            o_ref[tm:tm + pad_rows, :] = jnp.zeros((pad_rows, cols), BF16)

    return pl.pallas_call(
        body,
        grid_spec=pltpu.PrefetchScalarGridSpec(
            num_scalar_prefetch=1, grid=(rows // tm,),
            in_specs=[pl.BlockSpec((tm, cols), lambda i, chip: (i, 0))],
            out_specs=pl.BlockSpec((None, tm + pad_rows, cols), lambda i, chip: (chip[0], i, 0))),
        out_shape=jax.ShapeDtypeStruct((N_CHIP, rows + pad_rows, cols), BF16), name=name,
        compiler_params=_cparams(),
    )(chip_arr, w)


def _ag_ici_task(gathered):
    n = len(gathered)

    def copies(ins, outs, sems):
        send_sems, recv_sems = sems
        x, y, c, others = _place()
        me = 2 * x + y
        for k in range(n):
            mine = _half(c, gathered[k].shape[1])
            for t, (ox, oy) in enumerate(others):
                yield [pltpu.make_async_remote_copy(
                    src_ref=ins[k].at[me, mine], dst_ref=outs[k].at[slab, mine],
                    send_sem=send_sems.at[k, t], recv_sem=recv_sems.at[k, t],
                    device_id=(ox, oy, c), device_id_type=MESH) for slab in (me, 2 * ox + oy)]

    def issue(ins, outs, sems):
        for outgoing, _ in copies(ins, outs, sems):
            outgoing.start()

    def drain(ins, outs, sems):
        for outgoing, incoming in copies(ins, outs, sems):
            incoming.wait_recv()
            outgoing.wait_send()

    return _Task(gathered, [jax.ShapeDtypeStruct(g.shape, g.dtype) for g in gathered],
                 [_dma_sems(n, 3), _dma_sems(n, 3)], issue, drain, aliases={k: k for k in range(n)})


def _ag_d2d_task(gathered):
    n = len(gathered)

    def copies(ins, outs, sems):
        send_sems, recv_sems = sems
        x, y, c, others = _place()
        for k in range(n):
            rows = gathered[k].shape[1]
            mine, theirs = _half(c, rows), _half(1 - c, rows)
            for t, (ox, oy) in enumerate(others):
                slab = 2 * ox + oy
                pair = [pltpu.make_async_remote_copy(
                    src_ref=ins[k].at[slab, half], dst_ref=outs[k].at[slab, half],
                    send_sem=send_sems.at[k, t], recv_sem=recv_sems.at[k, t],
                    device_id=(x, y, 1 - c), device_id_type=MESH) for half in (mine, theirs)]
                yield pair

    def issue(ins, outs, sems):
        for outgoing, _ in copies(ins, outs, sems):
            outgoing.start()

    def drain(ins, outs, sems):
        for outgoing, incoming in copies(ins, outs, sems):
            incoming.wait_recv()
            outgoing.wait_send()

    return _Task(gathered, [jax.ShapeDtypeStruct(g.shape, g.dtype) for g in gathered],
                 [_dma_sems(n, 3), _dma_sems(n, 3)], issue, drain, aliases={k: k for k in range(n)})


def _rs_pair_task(ds):
    n = len(ds)

    def copies(ins, outs, sems):
        send_sems, recv_sems = sems
        x, y, c, _ = _place()
        for k in range(n):
            yield pltpu.make_async_remote_copy(
                src_ref=ins[k].at[:, _half(1 - c, ds[k].shape[1])], dst_ref=outs[k],
                send_sem=send_sems.at[k], recv_sem=recv_sems.at[k],
                device_id=(x, y, 1 - c), device_id_type=MESH)

    def issue(ins, outs, sems):
        for cp in copies(ins, outs, sems):
            cp.start()

    def drain(ins, outs, sems):
        for cp in copies(ins, outs, sems):
            cp.wait()

    return _Task(ds, [jax.ShapeDtypeStruct((N_CHIP, d.shape[1] // 2, d.shape[2]), d.dtype) for d in ds],
                 [_dma_sems(n), _dma_sems(n)], issue, drain)


def _pair_add_call(d, r1, c_arr, *, name, tm=512):
    _, rows, cols = d.shape
    hr = rows // 2
    tm = tm if hr % tm == 0 else hr
    nb = hr // tm

    def body(c_ref, d_ref, r_ref, o_ref):
        o_ref[...] = (d_ref[...].astype(F32) + r_ref[...].astype(F32)).astype(BF16)

    return pl.pallas_call(
        body,
        grid_spec=pltpu.PrefetchScalarGridSpec(
            num_scalar_prefetch=1, grid=(N_CHIP, nb),
            in_specs=[pl.BlockSpec((None, tm, cols), lambda j, i, c: (j, c[0] * nb + i, 0)),
                      pl.BlockSpec((None, tm, cols), lambda j, i, c: (j, i, 0))],
            out_specs=pl.BlockSpec((None, tm, cols), lambda j, i, c: (j, i, 0))),
        out_shape=jax.ShapeDtypeStruct((N_CHIP, hr, cols), BF16), name=name, compiler_params=_cparams(),
    )(c_arr, d, r1)


def _rs_chip_task(ps):
    n = len(ps)

    def copies(ins, outs, sems):
        send_sems, recv_sems = sems
        x, y, c, others = _place()
        for k in range(n):
            for t, (ox, oy) in enumerate(others):
                yield pltpu.make_async_remote_copy(
                    src_ref=ins[k].at[2 * ox + oy], dst_ref=outs[k].at[t],
                    send_sem=send_sems.at[k, t], recv_sem=recv_sems.at[k, t],
                    device_id=(ox, oy, c), device_id_type=MESH)

    def issue(ins, outs, sems):
        for cp in copies(ins, outs, sems):
            cp.start()

    def drain(ins, outs, sems):
        for cp in copies(ins, outs, sems):
            cp.wait()

    return _Task(ps, [jax.ShapeDtypeStruct((3,) + p.shape[1:], p.dtype) for p in ps],
                 [_dma_sems(n, 3), _dma_sems(n, 3)], issue, drain)


def _chip_sum_call(p, r2, place_arr, *, name, tm=512):
    _, hr, cols = r2.shape
    tm = tm if hr % tm == 0 else hr
    nb = hr // tm

    def body(place_ref, p_ref, r_ref, o_ref):
        acc = p_ref[...].astype(F32)
        for j in range(3):
            acc = acc + r_ref[j].astype(F32)
        o_ref[...] = acc

    return _pallas(
        body, grid=(nb,), prefetch=1,
        in_specs=[pl.BlockSpec((None, tm, cols), lambda i, pc: (pc[0], i, 0)),
                  pl.BlockSpec((3, tm, cols), lambda i, pc: (0, i, 0))],
        out_specs=pl.BlockSpec((tm, cols), lambda i, pc: (pc[1] * nb + i, 0)),
        out_shape=jax.ShapeDtypeStruct((2 * hr, cols), F32), name=name,
    )(place_arr, p, r2)


def _chip_sums_call(ps, r2s, place_arr, *, name, steps=2, task=None):
    n = len(ps)

    def body(place_ref, *refs):
        ins, outs = refs[:2 * n], refs[2 * n:]
        for k in range(n):
            acc = ins[2 * k][...].astype(F32)
            for j in range(3):
                acc = acc + ins[2 * k + 1][j].astype(F32)
            outs[k][...] = acc

    in_specs, out_specs, out_shape, args = [], [], [], []
    for p, r2 in zip(ps, r2s):
        _, hr, cols = r2.shape
        tm = hr // steps
        in_specs += [pl.BlockSpec((None, tm, cols), lambda i, pc: (pc[0], i, 0)),
                     pl.BlockSpec((3, tm, cols), lambda i, pc: (0, i, 0))]
        out_specs.append(pl.BlockSpec((tm, cols), lambda i, pc: (pc[1] * steps + i, 0)))
        out_shape.append(jax.ShapeDtypeStruct((2 * hr, cols), F32))
        args += [p, r2]
    return _pallas(body, grid=(steps,), prefetch=1, in_specs=in_specs, out_specs=out_specs, out_shape=out_shape,
                   name=name, task=task)(place_arr, *args)


def _rs_gather_task(gs):
    n = len(gs)

    def copies(ins, outs, sems):
        send_sems, recv_sems = sems
        x, y, c, _ = _place()
        for k in range(n):
            rows = gs[k].shape[0]
            mine, theirs = _half(c, rows), _half(1 - c, rows)
            yield [pltpu.make_async_remote_copy(
                src_ref=ins[k].at[mine], dst_ref=outs[k].at[half],
                send_sem=send_sems.at[k], recv_sem=recv_sems.at[k],
                device_id=(x, y, 1 - c), device_id_type=MESH) for half in (mine, theirs)]

    def issue(ins, outs, sems):
        for outgoing, _ in copies(ins, outs, sems):
            outgoing.start()

    def drain(ins, outs, sems):
        for outgoing, incoming in copies(ins, outs, sems):
            incoming.wait_recv()
            outgoing.wait_send()

    return _Task(gs, [jax.ShapeDtypeStruct(g.shape, g.dtype) for g in gs],
                 [_dma_sems(n), _dma_sems(n)], issue, drain, aliases={k: k for k in range(n)})


def _adamw(w, g, m, v):
    m = ADAM_B1 * m + (1.0 - ADAM_B1) * g
    v = ADAM_B2 * v + (1.0 - ADAM_B2) * jnp.square(g)
    m_hat = m / (1.0 - ADAM_B1 ** ADAM_STEP)
    v_hat = v / (1.0 - ADAM_B2 ** ADAM_STEP)
    delta = -ADAM_LR * (m_hat / (jnp.sqrt(v_hat) + ADAM_EPS) + ADAM_WD * w)
    return delta, m, v


def _adamw_call(items, *, name, tm=256, task=None):
    n = len(items)
    cols = items[0][0].shape[1]
    tiles = [it[0].shape[0] // tm for it in items]
    steps = max(tiles)

    def body(*refs):
        i = pl.program_id(0)
        ins, outs = refs[:4 * n], refs[4 * n:]
        for k in range(n):
            def update(k=k):
                g = ins[4 * k + 1][...]
                res = _adamw(ins[4 * k][...], g, ins[4 * k + 2][...], ins[4 * k + 3][...])
                outs[4 * k][...] = g
                for j in range(3):
                    outs[4 * k + 1 + j][...] = res[j]
            if tiles[k] == steps:
                update()
            else:
                pl.when(i < tiles[k])(update)

    in_specs, out_specs, out_shape, args = [], [], [], []
    for it, t in zip(items, tiles):
        spec = pl.BlockSpec((tm, cols), lambda i, t=t: (jnp.minimum(i, t - 1), 0))
        in_specs += [spec] * 4
        out_specs += [spec] * 4
        out_shape += [jax.ShapeDtypeStruct(it[0].shape, F32)] * 4
        args += list(it)
    res = _pallas(body, grid=(steps,), in_specs=in_specs, out_specs=out_specs, out_shape=out_shape,
                  name=name, task=task)(*args)
    outs, extra = res if task is not None else (res, None)
    grouped = [tuple(outs[4 * k:4 * k + 4]) for k in range(n)]
    return (grouped, extra) if task is not None else grouped


def _adamw_cols_call(w, g_pad, m, v, *, name, tn=512):
    rows, cols = w.shape

    def body(w_ref, g_ref, m_ref, v_ref, go_ref, d_ref, mo_ref, vo_ref):
        g = g_ref[0:rows, :]
        d, mn, vn = _adamw(w_ref[...], g, m_ref[...], v_ref[...])
        go_ref[...] = g
        d_ref[...] = d
        mo_ref[...] = mn
        vo_ref[...] = vn

    spec = pl.BlockSpec((rows, tn), lambda j: (0, j))
    gspec = pl.BlockSpec((g_pad.shape[0], tn), lambda j: (0, j))
    return _pallas(body, grid=(cols // tn,), in_specs=[spec, gspec, spec, spec], out_specs=(spec,) * 4,
                   out_shape=(jax.ShapeDtypeStruct((rows, cols), F32),) * 4, name=name)(w, g_pad, m, v)


N_DEV = 8
SMALL_ROWS = 24
SMALL_LAYOUT = {
    "g_mix_pre": (0, 0, 1, D), "g_mix_post": (1, 0, 1, D), "g_mem_kv": (2, 0, 1, D), "g_mem_pre": (3, 0, 1, D),
    "g_mem_post": (4, 0, 1, D), "g_ff_pre": (5, 0, 1, D), "g_ff_post": (6, 0, 1, D),
    "g_fox_out": (7, 0, 1, D_GRP), "g_chk_out": (7, D_GRP, 1, D_GRP), "b_fgt": (8, 0, 1, 8),
    "rel_bias": (16, 0, 8, N_REL),
}
SMALL = list(SMALL_LAYOUT)


LOSS_ROW = 9


def _small_reduce_call(grads, loss_blk, task, *, name):
    n = len(SMALL)
    t_in, t_out = len(task.arrays), len(task.out_shapes)

    def body(*refs):
        g_refs, loss_ref, tins = refs[:n], refs[n], refs[n + 1:n + 1 + t_in]
        p = n + 1 + t_in
        total_ref, loss_out, touts = refs[p], refs[p + 1], refs[p + 2:p + 2 + t_out]
        p += 2 + t_out
        mine, slots, send_sems, recv_sems = refs[p:p + 4]
        tsems = refs[p + 4:]
        task.issue(tins, touts, tsems)
        x, y, c, _ = _place()
        me = 4 * x + 2 * y + c
        mine[...] = jnp.zeros_like(mine)
        for k, name_k in enumerate(SMALL):
            r, l, nr, nl = SMALL_LAYOUT[name_k]
            mine[r:r + nr, l:l + nl] = g_refs[k][0:nr, 0:nl]
        mine[LOSS_ROW:LOSS_ROW + 1, 0:128] = loss_ref[0:1, :]
        slots[me] = mine[...]
        peers = [(dx, dy, dc) for dx in (0, 1) for dy in (0, 1) for dc in (0, 1)][1:]
        cps = []
        for t, (dx, dy, dc) in enumerate(peers):
            px, py, pc = (x + dx) % 2, (y + dy) % 2, (c + dc) % 2
            cps.append(pltpu.make_async_remote_copy(
                src_ref=mine, dst_ref=slots.at[me], send_sem=send_sems.at[t], recv_sem=recv_sems.at[t],
                device_id=(px, py, pc), device_id_type=MESH))
            cps[-1].start()
        for t, (dx, dy, dc) in enumerate(peers):
            px, py, pc = (x + dx) % 2, (y + dy) % 2, (c + dc) % 2
            pltpu.make_async_remote_copy(
                src_ref=mine, dst_ref=slots.at[4 * px + 2 * py + pc], send_sem=send_sems.at[t],
                recv_sem=recv_sems.at[t], device_id=(px, py, pc), device_id_type=MESH).wait_recv()
        for cp in cps:
            cp.wait_send()
        total = slots[0]
        for j in range(1, N_DEV):
            total = total + slots[j]
        total_ref[...] = total
        loss_out[...] = jnp.broadcast_to(total[LOSS_ROW:LOSS_ROW + 1, 0:128], loss_out.shape)
        task.drain(tins, touts, tsems)

    vm = pl.BlockSpec(memory_space=pltpu.VMEM)
    out_shape = [jax.ShapeDtypeStruct((SMALL_ROWS, D), F32), jax.ShapeDtypeStruct((8, 128), F32)] + list(task.out_shapes)
    res = pl.pallas_call(
        body, in_specs=[vm] * (n + 1) + [ANY] * t_in, out_specs=[vm] * 2 + [ANY] * t_out,
        out_shape=out_shape,
        scratch_shapes=[pltpu.VMEM((SMALL_ROWS, D), F32), pltpu.VMEM((N_DEV, SMALL_ROWS, D), F32),
                        _dma_sems(N_DEV - 1), _dma_sems(N_DEV - 1)] + list(task.sems),
        input_output_aliases={n + 1 + i: 2 + j for i, j in task.aliases.items()},
        name=name,
    )(*[grads[k] for k in SMALL], loss_blk, *task.arrays)
    return res[0], res[1], list(res[2:])


def _small_adamw_call(total, ws, ms, vs, *, name):
    n = len(SMALL)

    def body(*refs):
        total_ref = refs[0]
        w_refs, m_refs, v_refs = (refs[1 + j * n:1 + (j + 1) * n] for j in range(3))
        outs = refs[1 + 3 * n:]
        for k, name_k in enumerate(SMALL):
            r, l, nr, nl = SMALL_LAYOUT[name_k]
            g = total_ref[r:r + nr, l:l + nl]
            d, mn, vn = _adamw(w_refs[k][...], g, m_refs[k][...], v_refs[k][...])
            for j, val in enumerate((g, d, mn, vn)):
                outs[4 * k + j][...] = val

    vm = pl.BlockSpec(memory_space=pltpu.VMEM)
    res = pl.pallas_call(
        body, in_specs=[vm] * (3 * n + 1), out_specs=[vm] * (4 * n),
        out_shape=[jax.ShapeDtypeStruct(ws[k].shape, F32) for k in SMALL for _ in range(4)], name=name,
    )(total, *[d[k] for d in (ws, ms, vs) for k in SMALL])
    return {k: tuple(res[4 * i:4 * i + 4]) for i, k in enumerate(SMALL)}


WEIGHTS = ["w_in", "b_fgt", "rel_bias", "g_fox_out", "g_chk_out", "w_out", "g_mix_pre", "g_mix_post", "g_mem_kv",
           "w_mq", "w_mk", "w_mv", "w_mo", "g_mem_pre", "g_mem_post", "w_ff1", "w_ff2", "g_ff_pre", "g_ff_post"]
BIG = ["w_in", "w_out", "w_mq", "w_mk", "w_mv", "w_mo", "w_ff1", "w_ff2"]


IN_SHARD = D_IN // N_CHIP
IN_PAD = 800


IN_PIECES = [(0, 0, 770), (800, 770, 766), (1566, 3072, 4), (1600, 3076, 4), (1604, 1536, 766), (2400, 2302, 770)]
PAD_ZEROS = [(800 * j + IN_SHARD, IN_PAD - IN_SHARD) for j in range(N_CHIP)]
ALL_ZEROS = [(D_IN, D_ALL - D_IN)]


def _reorder_rows_call(src, to_all, *, name, tn=512):
    rows, cols = src.shape
    zeros = ALL_ZEROS if to_all else PAD_ZEROS

    def body(s_ref, o_ref):
        for pad0, all0, cnt in IN_PIECES:
            s0, d0 = (pad0, all0) if to_all else (all0, pad0)
            o_ref[d0:d0 + cnt, :] = s_ref[s0:s0 + cnt, :]
        for z0, cnt in zeros:
            o_ref[z0:z0 + cnt, :] = jnp.zeros((cnt, tn), src.dtype)

    spec = pl.BlockSpec((rows, tn), lambda j: (0, j))
    return _pallas(body, grid=(cols // tn,), in_specs=[spec], out_specs=spec,
                   out_shape=jax.ShapeDtypeStruct((rows, cols), src.dtype), name=name)(src)


def kernel(x, mem, w_in, b_fgt, rel_bias, g_fox_out, g_chk_out, w_out, g_mix_pre, g_mix_post, g_mem_kv, w_mq, w_mk, w_mv, w_mo, g_mem_pre, g_mem_post, w_ff1, w_ff2, g_ff_pre, g_ff_post, loss_target, m_w_in, m_b_fgt, m_rel_bias, m_g_fox_out, m_g_chk_out, m_w_out, m_g_mix_pre, m_g_mix_post, m_g_mem_kv, m_w_mq, m_w_mk, m_w_mv, m_w_mo, m_g_mem_pre, m_g_mem_post, m_w_ff1, m_w_ff2, m_g_ff_pre, m_g_ff_post, v_w_in, v_b_fgt, v_rel_bias, v_g_fox_out, v_g_chk_out, v_w_out, v_g_mix_pre, v_g_mix_post, v_g_mem_kv, v_w_mq, v_w_mk, v_w_mv, v_w_mo, v_g_mem_pre, v_g_mem_post, v_w_ff1, v_w_ff2, v_g_ff_pre, v_g_ff_post):
    w = dict(w_in=w_in, b_fgt=b_fgt, rel_bias=rel_bias, g_fox_out=g_fox_out, g_chk_out=g_chk_out, w_out=w_out,
             g_mix_pre=g_mix_pre, g_mix_post=g_mix_post, g_mem_kv=g_mem_kv, w_mq=w_mq, w_mk=w_mk, w_mv=w_mv,
             w_mo=w_mo, g_mem_pre=g_mem_pre, g_mem_post=g_mem_post, w_ff1=w_ff1, w_ff2=w_ff2, g_ff_pre=g_ff_pre,
             g_ff_post=g_ff_post)
    m = dict(w_in=m_w_in, b_fgt=m_b_fgt, rel_bias=m_rel_bias, g_fox_out=m_g_fox_out, g_chk_out=m_g_chk_out,
             w_out=m_w_out, g_mix_pre=m_g_mix_pre, g_mix_post=m_g_mix_post, g_mem_kv=m_g_mem_kv, w_mq=m_w_mq,
             w_mk=m_w_mk, w_mv=m_w_mv, w_mo=m_w_mo, g_mem_pre=m_g_mem_pre, g_mem_post=m_g_mem_post,
             w_ff1=m_w_ff1, w_ff2=m_w_ff2, g_ff_pre=m_g_ff_pre, g_ff_post=m_g_ff_post)
    v = dict(w_in=v_w_in, b_fgt=v_b_fgt, rel_bias=v_rel_bias, g_fox_out=v_g_fox_out, g_chk_out=v_g_chk_out,
             w_out=v_w_out, g_mix_pre=v_g_mix_pre, g_mix_post=v_g_mix_post, g_mem_kv=v_g_mem_kv, w_mq=v_w_mq,
             w_mk=v_w_mk, w_mv=v_w_mv, w_mo=v_w_mo, g_mem_pre=v_g_mem_pre, g_mem_post=v_g_mem_post,
             w_ff1=v_w_ff1, w_ff2=v_w_ff2, g_ff_pre=v_g_ff_pre, g_ff_post=v_g_ff_post)

    def rows(d, k):
        return d[k][0] if k == "rel_bias" else d[k]

    xs, mems, target = x[0], mem[0], loss_target[0]
    S = xs.shape[0]
    sp = {k: rows(w, k) for k in SMALL}
    b_pad = jnp.pad(sp["b_fgt"], ((0, 0), (0, 120)))
    chip = 2 * lax.axis_index("x") + lax.axis_index("y")
    chip_arr = jnp.reshape(chip, (1,)).astype(jnp.int32)
    c_arr = jnp.reshape(lax.axis_index("c"), (1,)).astype(jnp.int32)
    place_arr = jnp.concatenate([chip_arr, c_arr])
    w_in_t, m_in_t, v_in_t = w["w_in"][0].T, m["w_in"][0].T, v["w_in"][0].T
    slab = {"w_in": _cast_slab_call(w_in_t, chip_arr, name="cast_w_in", pad_rows=IN_PAD - IN_SHARD)}

    def gather_ici(names):
        return _ag_ici_task([slab[k] for k in names])

    def pair_add(k, d, r1):
        return _pair_add_call(d, r1, c_arr, name="rs_pair_add_" + k)

    rest, (g_in,) = _cast_slabs_call([w[k][0] for k in BIG[1:]], chip_arr, name="cast_rest",
                                     task=gather_ici(["w_in"]))
    slab.update(zip(BIG[1:], rest))
    h1, (g_in,) = _rms_fwd_call(xs, sp["g_mix_pre"], name="rms_mix_pre", task=_ag_d2d_task([g_in]))
    w_all_t = _reorder_rows_call(g_in.reshape(N_CHIP * IN_PAD, D), True, name="w_in_rows")
    proj, (g_out, g_mq) = _mm_nt(h1, w_all_t, "plain", rows=(0, 3072), name="mm_proj",
                                 task=gather_ici(["w_out", "w_mq"]))
    fl_raw = _mm_nt(h1, w_all_t, "plain", rows=(3072, 128), name="mm_gate", out_dtype=F32, tn=128)
    c_rep, c_t = _fox_prep_call(fl_raw, b_pad, name="fox_prep")
    bias = _chk_bias_call(_rel_table_to_g(sp["rel_bias"]), name="chk_bias")
    mid = ["w_mk", "w_mv", "w_mo", "w_ff1"]
    (yf, lse), got = _fox_fwd_call(proj, c_rep, c_t, name="fox_fwd",
                                   task=_merge_tasks([gather_ici(mid), _ag_d2d_task([g_out, g_mq])]))
    g_mid, (g_out, g_mq) = got[:4], got[4:]
    yc, got = _chk_fwd_call(proj, bias, name="chk_fwd",
                            task=_merge_tasks([gather_ici(["w_ff2"]), _ag_d2d_task(g_mid)]))
    g_ff2, (g_mk, g_mv, g_mo, g_ff1) = got[0], got[1:]
    yn = _mix_norm_fwd_call(yf, yc, sp["g_fox_out"], sp["g_chk_out"], name="mix_norm_fwd")
    z, (g_ff2,) = _mm_nn(yn, g_out, "rows", name="mm_out", out_dtype=F32, task=_ag_d2d_task([g_ff2]))
    x1, h2 = _post_pre_call(xs, z, sp["g_mix_post"], sp["g_mem_pre"], name="post_mix")
    memn = _rms_fwd_call(mems, sp["g_mem_kv"], name="rms_mem_kv")
    q2 = _mm_nn(h2, g_mq, "rows", name="mm_mq")
    k2 = _mm_nn(memn, g_mk, "rows", name="mm_mk")
    v2 = _mm_nn(memn, g_mv, "rows", name="mm_mv")
    o2 = _mem_fwd_call(q2, k2, v2, name="mem_fwd")
    y2 = _mm_nn(o2, g_mo, "rows", name="mm_mo", out_dtype=F32)
    x2, h3 = _post_pre_call(x1, y2, sp["g_mem_post"], sp["g_ff_pre"], name="post_mem")
    act, relu = _mm_nn(h3, g_ff1, "cols", name="mm_ff1", epi="relu2")
    y3 = _mm_nn(act, g_ff2, "rows", name="mm_ff2", out_dtype=F32, tm=1024)
    loss_blk, dx3, dy3, dg_ff_post = _final_call(x2, y3, sp["g_ff_post"], target, name="final")

    d_ff2 = _mm_tn(act, dy3, name="mm_dff2", tk=512, tn=1024).reshape(N_CHIP, D_FF // N_CHIP, D)
    du, (r1,) = _mm_nt(dy3, g_ff2, "rows", name="mm_du", mul2r=relu, task=_rs_pair_task([d_ff2]))
    p_ff2 = pair_add("w_ff2", d_ff2, r1)
    d_ff1 = _mm_tn(h3, du, name="mm_dff1", cols4=True)
    dh3, (r1,) = _mm_nt(du, g_ff1, "cols", name="mm_dh3", out_dtype=F32, tm=1024, task=_rs_pair_task([d_ff1]))
    p_ff1 = pair_add("w_ff1", d_ff1, r1)
    dx2, dy2, dg_ff_pre, dg_mem_post = _bwd_mid_call(dx3, x2, dh3, sp["g_ff_pre"], y2, sp["g_mem_post"], name="bwd_ff")
    d_mo = _mm_tn(o2, dy2, name="mm_dmo").reshape(N_CHIP, D // N_CHIP, D)
    do2 = _mm_nt(dy2, g_mo, "rows", name="mm_do2")
    dq2, dk2, dv2 = _mem_bwd_call(q2, k2, v2, do2, name="mem_bwd")
    d_mq = _mm_tn(h2, dq2, name="mm_dmq").reshape(N_CHIP, D // N_CHIP, D)
    dh2 = _mm_nt(dq2, g_mq, "rows", name="mm_dh2", out_dtype=F32)
    d_mk = _mm_tn(memn, dk2, name="mm_dmk").reshape(N_CHIP, D // N_CHIP, D)
    d_mv = _mm_tn(memn, dv2, name="mm_dmv").reshape(N_CHIP, D // N_CHIP, D)
    dmn_k = _mm_nt(dk2, g_mk, "rows", name="mm_dmemk", out_dtype=F32)
    dmn_v = _mm_nt(dv2, g_mv, "rows", name="mm_dmemv", out_dtype=F32)
    dg_mem_kv = _gain_grad_call(mems, sp["g_mem_kv"], dmn_k, dmn_v, name="gain_mem_kv")
    dx1, dz, dg_mem_pre, dg_mix_post = _bwd_mid_call(dx2, x1, dh2, sp["g_mem_pre"], z, sp["g_mix_post"], name="bwd_mem")
    d_out = _mm_tn(yn, dz, name="mm_dout").reshape(N_CHIP, D // N_CHIP, D)
    late = ["w_mo", "w_mq", "w_mk", "w_mv", "w_out"]
    d_late = [d_mo, d_mq, d_mk, d_mv, d_out]
    dyn, r1_late = _mm_nt(dz, g_out, "rows", name="mm_dyn", out_dtype=F32, task=_rs_pair_task(d_late))
    p_late = [pair_add(k, d, r1) for k, d, r1 in zip(late, d_late, r1_late)]
    dof, doc, delta, dg_fox, dg_chk = _mix_norm_bwd_call(dyn, yf, yc, sp["g_fox_out"], sp["g_chk_out"], name="mix_norm_bwd")
    (dqf, dkf, dvf, dcq, dck), r2_ff = _fox_bwd_call(proj, dof, lse, delta, c_rep, c_t, name="fox_bwd",
                                                      task=_rs_chip_task([p_ff2, p_ff1]))
    (dqc, dkc, dvc, dgrev), r2_late = _chk_bwd_call(proj, doc, bias, name="chk_bwd", task=_rs_chip_task(p_late))
    first = ["w_ff2", "w_ff1"] + late
    dc8 = (dcq[:, :, 0:2, :] + dck[:, :, 0:2, :]).transpose(0, 2, 1, 3).reshape(8, S)
    dc_rows = jnp.concatenate([dc8, jnp.zeros((120, S), F32)], axis=0)
    dfl, db_fgt = _fox_gate_bwd_call(dc_rows, fl_raw, b_pad, name="fox_gate_bwd")
    dproj = jnp.concatenate([dqf, dkf, dvf, dqc, dkc, dvc, dfl], axis=1)
    d_all_t = _mm_tn(dproj, h1, name="mm_dwin", tk=640, tn=1024)
    d_in = _reorder_rows_call(d_all_t, False, name="d_in_rows").reshape(N_CHIP, IN_PAD, D)
    f_first, (r1,) = _chip_sums_call([p_ff2, p_ff1] + p_late, r2_ff + r2_late, place_arr, name="rs_chip_sums",
                                     task=_rs_pair_task([d_in]))
    p_in = pair_add("w_in", d_in, r1)
    dh1, got = _mm_nn(dproj, w_all_t, "plain", name="mm_dh1", out_dtype=F32, tm=1024,
                      task=_merge_tasks([_rs_chip_task([p_in]), _rs_gather_task(f_first)]))
    r2_in, grads = got[0], dict(zip(first, got[1:]))
    f_in = _chip_sum_call(p_in, r2_in, place_arr, name="rs_chip_sum_w_in")
    delta_w, new_m, new_v = {}, {}, {}

    def adamw_items(names):
        return [(w[k][0], grads[k], m[k][0], v[k][0]) for k in names]

    upd_late = _adamw_call(adamw_items(late), name="adamw_late", tm=64)
    upd_ff = _adamw_call(adamw_items(first[:2]), name="adamw_ff")
    for k, res in zip(late + first[:2], upd_late + upd_ff):
        grads[k], delta_w[k], new_m[k], new_v[k] = res
    grad_x, dg_mix_pre = _bwd_last_call(dx1, xs, dh1, sp["g_mix_pre"], name="bwd_mix")

    small_g = {"g_mix_pre": dg_mix_pre, "g_mix_post": dg_mix_post, "g_mem_kv": dg_mem_kv, "g_mem_pre": dg_mem_pre,
               "g_mem_post": dg_mem_post, "g_ff_pre": dg_ff_pre, "g_ff_post": dg_ff_post, "g_fox_out": dg_fox,
               "g_chk_out": dg_chk, "b_fgt": db_fgt,
               "rel_bias": _g_to_rel_table(dgrev[:, 0:2, :].reshape(8, ROLL_W))}
    small_sum, loss_out, (g_w_in,) = _small_reduce_call(small_g, loss_blk, _rs_gather_task([f_in]),
                                                       name="small_allreduce")
    small = _small_adamw_call(small_sum, sp, {k: rows(m, k) for k in SMALL}, {k: rows(v, k) for k in SMALL},
                              name="small_adamw")
    loss = loss_out[0, 0]
    res = _adamw_cols_call(w_in_t, g_w_in, m_in_t, v_in_t, name="adamw_w_in")
    grads["w_in"], delta_w["w_in"], new_m["w_in"], new_v["w_in"] = (a.T for a in res)
    for k in SMALL:
        vals = small[k]
        if k == "rel_bias":
            vals = tuple(a[None] for a in vals)
        grads[k], delta_w[k], new_m[k], new_v[k] = vals

    def out(d, k):
        return d[k][None] if k in BIG else d[k]

    return (loss, grad_x[None], *[out(grads, k) for k in WEIGHTS], *[out(delta_w, k) for k in WEIGHTS],
            *[out(new_m, k) for k in WEIGHTS], *[out(new_v, k) for k in WEIGHTS])
```

```python
import functools

import jax
import jax.numpy as jnp
from jax import lax
from jax.experimental import pallas as pl
from jax.experimental.pallas import tpu as pltpu

F32 = jnp.float32
BF16 = jnp.bfloat16

D = 1024
HEAD = 64
N_PAIR = 4
D_GRP = 512
CHUNK = 64
LEFT = 8
MAX_REL = 128
N_REL = 2 * MAX_REL + 1
N_MEM = 256
MEM_HEADS = 4
MEM_HD = 256
D_FF = 4096
D_IN = 3080
D_ALL = 3200
EPS = 1e-6
TQ = 256
WIN = (LEFT + TQ // CHUNK) * CHUNK
PADK = LEFT * CHUNK
ROLL_W = 1024
NEG = -1e30
N_CHIP = 4
VMEM_LIMIT = 48 * 1024 * 1024

ADAM_LR = 0.001
ADAM_B1 = 0.9
ADAM_B2 = 0.999
ADAM_EPS = 1e-08
ADAM_WD = 0.01
ADAM_STEP = 10

MESH = pl.DeviceIdType.MESH


def _cparams():
    return pltpu.CompilerParams(vmem_limit_bytes=VMEM_LIMIT)


ANY = pl.BlockSpec(memory_space=pl.ANY)


class _Task:
    def __init__(self, arrays, out_shapes, sems, issue, drain, aliases=None):
        self.arrays, self.out_shapes, self.sems = list(arrays), list(out_shapes), list(sems)
        self.issue, self.drain, self.aliases = issue, drain, dict(aliases or {})


def _merge_tasks(tasks):
    tasks = [t for t in tasks if t is not None]
    if len(tasks) == 1:
        return tasks[0]
    cuts, a, o, s = [], 0, 0, 0
    aliases = {}
    for t in tasks:
        cuts.append((a, o, s))
        aliases.update({a + i: o + j for i, j in t.aliases.items()})
        a, o, s = a + len(t.arrays), o + len(t.out_shapes), s + len(t.sems)

    def part(fn_name):
        def run(ins, outs, sems):
            for t, (a0, o0, s0) in zip(tasks, cuts):
                getattr(t, fn_name)(ins[a0:a0 + len(t.arrays)], outs[o0:o0 + len(t.out_shapes)],
                                    sems[s0:s0 + len(t.sems)])
        return run

    return _Task([x for t in tasks for x in t.arrays], [x for t in tasks for x in t.out_shapes],
                 [x for t in tasks for x in t.sems], part("issue"), part("drain"), aliases)


def _pallas(body, *, grid, in_specs, out_specs, out_shape, name, scratch_shapes=(), task=None, prefetch=0):
    def make(kernel, i_specs, o_specs, o_shape, scratch, aliases):
        if prefetch:
            spec = pltpu.PrefetchScalarGridSpec(num_scalar_prefetch=prefetch, grid=grid, in_specs=i_specs,
                                                out_specs=o_specs, scratch_shapes=scratch)
            return pl.pallas_call(kernel, grid_spec=spec, out_shape=o_shape, input_output_aliases=aliases,
                                  name=name, compiler_params=_cparams())
        return pl.pallas_call(kernel, grid=grid, in_specs=i_specs, out_specs=o_specs, out_shape=o_shape,
                              scratch_shapes=scratch, input_output_aliases=aliases, name=name,
                              compiler_params=_cparams())

    if task is None:
        return make(body, list(in_specs), out_specs, out_shape, list(scratch_shapes), {})
    single = not isinstance(out_shape, (tuple, list))
    o_shapes = [out_shape] if single else list(out_shape)
    o_specs = [out_specs] if single else list(out_specs)
    n_in, n_out, n_scr = len(in_specs), len(o_shapes), len(scratch_shapes)
    t_in, t_out = len(task.arrays), len(task.out_shapes)

    def carried(*refs):
        cut = [prefetch, n_in, t_in, n_out, t_out, n_scr]
        parts, p = [], 0
        for c in cut:
            parts.append(refs[p:p + c])
            p += c
        scalars, ins, tins, outs, touts, scr = parts
        tsems = refs[p:]
        ids = [pl.program_id(a) for a in range(len(grid))]
        first = functools.reduce(jnp.logical_and, [i == 0 for i in ids])
        last = functools.reduce(jnp.logical_and, [i == g - 1 for i, g in zip(ids, grid)])

        @pl.when(first)
        def _():
            task.issue(tins, touts, tsems)
        body(*scalars, *ins, *outs, *scr)

        @pl.when(last)
        def _():
            task.drain(tins, touts, tsems)

    call = make(carried, list(in_specs) + [ANY] * t_in, o_specs + [ANY] * t_out,
                o_shapes + list(task.out_shapes), list(scratch_shapes) + list(task.sems),
                {prefetch + n_in + i: n_out + j for i, j in task.aliases.items()})

    def run(*args):
        res = call(*args, *task.arrays)
        outs = res[:n_out]
        return (outs[0] if single else tuple(outs)), list(res[n_out:])

    return run


def _dot(a, b):
    return jnp.dot(a, b, preferred_element_type=F32)


def _dot_nt(a, b):
    return lax.dot_general(a, b, (((1,), (1,)), ((), ())), preferred_element_type=F32)


def _dot_tn(a, b):
    return lax.dot_general(a, b, (((0,), (0,)), ((), ())), preferred_element_type=F32)


def _split3(x):
    hi = x.astype(BF16)
    r1 = x - hi.astype(F32)
    mid = r1.astype(BF16)
    lo = (r1 - mid.astype(F32)).astype(BF16)
    return hi, mid, lo


def _dot3(x, m01):
    hi, mid, lo = _split3(x)
    return _dot(hi, m01) + _dot(mid, m01) + _dot(lo, m01)


def _dot3_l(m01, x):
    hi, mid, lo = _split3(x)
    return _dot(m01, hi) + _dot(m01, mid) + _dot(m01, lo)


def _mm_nn(a, b, kind, *, name, out_dtype=BF16, tm=2048, tn=512, epi=None, task=None):
    M, K = a.shape
    if kind == "plain":
        N = b.shape[1]
        b_spec = pl.BlockSpec((K, tn), lambda m, n: (0, n))
    elif kind == "rows":
        N = b.shape[2]
        b_spec = pl.BlockSpec((N_CHIP, K // N_CHIP, tn), lambda m, n: (0, 0, n))
    else:
        nq = b.shape[2]
        N = N_CHIP * nq
        per = nq // tn
        b_spec = pl.BlockSpec((None, K, tn), lambda m, n: (n // per, 0, n % per))
    tm = min(tm, M)
    kq = K // N_CHIP

    def body(a_ref, b_ref, *o_refs):
        if kind == "rows":
            acc = _dot(a_ref[:, 0:kq], b_ref[0])
            for j in range(1, N_CHIP):
                acc += _dot(a_ref[:, j * kq:(j + 1) * kq], b_ref[j])
        else:
            acc = _dot(a_ref[...], b_ref[...])
        if epi == "relu2":
            r = jnp.maximum(acc, 0.0)
            o_refs[0][...] = (r * r).astype(BF16)
            o_refs[1][...] = r.astype(BF16)
        else:
            o_refs[0][...] = acc.astype(out_dtype)

    o_spec = pl.BlockSpec((tm, tn), lambda m, n: (m, n))
    if epi == "relu2":
        out_shape = (jax.ShapeDtypeStruct((M, N), BF16), jax.ShapeDtypeStruct((M, N), BF16))
        out_specs = (o_spec, o_spec)
    else:
        out_shape = jax.ShapeDtypeStruct((M, N), out_dtype)
        out_specs = o_spec
    return _pallas(
        body, grid=(M // tm, N // tn),
        in_specs=[pl.BlockSpec((tm, K), lambda m, n: (m, 0)), b_spec],
        out_specs=out_specs, out_shape=out_shape, name=name, task=task,
    )(a, b)


def _mm_nt(a, b, kind, *, name, out_dtype=BF16, tm=2048, tn=512, mul2r=None, task=None, rows=None):
    M, K = a.shape
    if kind == "plain":
        first, N = rows if rows is not None else (0, b.shape[0])
        n0 = first // tn
        b_spec = pl.BlockSpec((tn, K), lambda m, n: (n0 + n, 0))
    elif kind == "rows":
        nq = b.shape[1]
        N = N_CHIP * nq
        tn = min(tn, nq)
        per = nq // tn
        b_spec = pl.BlockSpec((None, tn, K), lambda m, n: (n // per, n % per, 0))
    else:
        N = b.shape[1]
        b_spec = pl.BlockSpec((N_CHIP, tn, K // N_CHIP), lambda m, n: (0, n, 0))
    tm = min(tm, M)
    kq = K // N_CHIP

    def body(a_ref, b_ref, *rest):
        o_ref = rest[-1]
        if kind == "cols":
            acc = _dot_nt(a_ref[:, 0:kq], b_ref[0])
            for j in range(1, N_CHIP):
                acc += _dot_nt(a_ref[:, j * kq:(j + 1) * kq], b_ref[j])
        else:
            acc = _dot_nt(a_ref[...], b_ref[...])
        if mul2r is not None:
            acc = acc * (2.0 * rest[0][...].astype(F32))
        o_ref[...] = acc.astype(out_dtype)

    in_specs = [pl.BlockSpec((tm, K), lambda m, n: (m, 0)), b_spec]
    args = [a, b]
    if mul2r is not None:
        in_specs.append(pl.BlockSpec((tm, tn), lambda m, n: (m, n)))
        args.append(mul2r)
    return _pallas(
        body, grid=(M // tm, N // tn), in_specs=in_specs,
        out_specs=pl.BlockSpec((tm, tn), lambda m, n: (m, n)),
        out_shape=jax.ShapeDtypeStruct((M, N), out_dtype), name=name, task=task,
    )(*args)


def _mm_tn(a, b, *, name, out_dtype=BF16, tk=1024, tn=512, cols4=False, task=None):
    M, K1 = a.shape
    N = b.shape[1]
    tk = min(tk, K1)
    tn = min(tn, N)

    def body(a_ref, b_ref, o_ref):
        o_ref[...] = _dot_tn(a_ref[...], b_ref[...]).astype(out_dtype)

    if cols4:
        per = (N // N_CHIP) // tn
        out_shape = jax.ShapeDtypeStruct((N_CHIP, K1, N // N_CHIP), out_dtype)
        o_spec = pl.BlockSpec((None, tk, tn), lambda k, n: (n // per, k, n % per))
    else:
        out_shape = jax.ShapeDtypeStruct((K1, N), out_dtype)
        o_spec = pl.BlockSpec((tk, tn), lambda k, n: (k, n))
    return _pallas(
        body, grid=(K1 // tk, N // tn),
        in_specs=[pl.BlockSpec((M, tk), lambda k, n: (0, k)), pl.BlockSpec((M, tn), lambda k, n: (0, n))],
        out_specs=o_spec, out_shape=out_shape, name=name, task=task,
    )(a, b)


def _rms(x, g):
    r = lax.rsqrt(jnp.mean(x * x, axis=-1, keepdims=True) + EPS)
    return x * r * g


def _rms_bwd(x, g, dy):
    r = lax.rsqrt(jnp.mean(x * x, axis=-1, keepdims=True) + EPS)
    xh = x * r
    dg = jnp.sum(dy * xh, axis=0, keepdims=True)
    dxh = dy * g
    dx = r * (dxh - xh * jnp.mean(dxh * xh, axis=-1, keepdims=True))
    return dx, dg


def _row_spec(tm, n):
    return pl.BlockSpec((tm, n), lambda i: (i, 0))


def _vec_spec(n):
    return pl.BlockSpec((1, n), lambda i: (0, 0))


def _acc_spec(n):
    return pl.BlockSpec((8, n), lambda i: (0, 0))


def _acc_add(ref, row, i):
    @pl.when(i == 0)
    def _():
        ref[...] = jnp.zeros_like(ref)
    ref[0:1, :] += row


def _rms_fwd_call(x, g, *, name, tm=1024, task=None):
    M, n = x.shape
    tm = min(tm, M)

    def body(x_ref, g_ref, h_ref):
        h_ref[...] = _rms(x_ref[...], g_ref[...]).astype(BF16)

    return _pallas(
        body, grid=(M // tm,), in_specs=[_row_spec(tm, n), _vec_spec(n)], out_specs=_row_spec(tm, n),
        out_shape=jax.ShapeDtypeStruct((M, n), BF16), name=name, task=task,
    )(x, g)


def _post_pre_call(xres, z, g_post, g_pre, *, name, tm=1024):
    M, n = xres.shape

    def body(x_ref, z_ref, gp_ref, gn_ref, xo_ref, h_ref):
        xn = x_ref[...] + _rms(z_ref[...], gp_ref[...])
        xo_ref[...] = xn
        h_ref[...] = _rms(xn, gn_ref[...]).astype(BF16)

    return pl.pallas_call(
        body, grid=(M // tm,),
        in_specs=[_row_spec(tm, n), _row_spec(tm, n), _vec_spec(n), _vec_spec(n)],
        out_specs=(_row_spec(tm, n), _row_spec(tm, n)),
        out_shape=(jax.ShapeDtypeStruct((M, n), F32), jax.ShapeDtypeStruct((M, n), BF16)),
        name=name, compiler_params=_cparams(),
    )(xres, z, g_post, g_pre)


def _final_call(x2, y3, g_post, target, *, name, tm=512):
    M, n = x2.shape

    def body(x_ref, y_ref, g_ref, t_ref, loss_ref, dx_ref, dy_ref, dg_ref):
        i = pl.program_id(0)
        y = y_ref[...]
        g = g_ref[...]
        diff = x_ref[...] + _rms(y, g) - t_ref[...]
        part = 0.5 * jnp.sum(jnp.sum(diff * diff, axis=1, keepdims=True), axis=0, keepdims=True) / n

        @pl.when(i == 0)
        def _():
            loss_ref[...] = jnp.zeros_like(loss_ref)
        loss_ref[...] += jnp.broadcast_to(part, loss_ref.shape)
        dx = diff / n
        dx_ref[...] = dx
        dy, dg = _rms_bwd(y, g, dx)
        dy_ref[...] = dy.astype(BF16)
        _acc_add(dg_ref, dg, i)

    return pl.pallas_call(
        body, grid=(M // tm,),
        in_specs=[_row_spec(tm, n), _row_spec(tm, n), _vec_spec(n), _row_spec(tm, n)],
        out_specs=(pl.BlockSpec((8, 128), lambda i: (0, 0)), _row_spec(tm, n), _row_spec(tm, n), _acc_spec(n)),
        out_shape=(jax.ShapeDtypeStruct((8, 128), F32), jax.ShapeDtypeStruct((M, n), F32),
                   jax.ShapeDtypeStruct((M, n), BF16), jax.ShapeDtypeStruct((8, n), F32)),
        name=name, compiler_params=_cparams(),
    )(x2, y3, g_post, target)


def _bwd_mid_call(dx_in, x, dh, g_pre, y, g_post, *, name, tm=512):
    M, n = x.shape

    def body(dxi_ref, x_ref, dh_ref, gpre_ref, y_ref, gpost_ref, dx_ref, dy_ref, dgpre_ref, dgpost_ref):
        i = pl.program_id(0)
        d1, dg1 = _rms_bwd(x_ref[...], gpre_ref[...], dh_ref[...])
        dx = dxi_ref[...] + d1
        dx_ref[...] = dx
        dy, dg2 = _rms_bwd(y_ref[...], gpost_ref[...], dx)
        dy_ref[...] = dy.astype(BF16)
        _acc_add(dgpre_ref, dg1, i)
        _acc_add(dgpost_ref, dg2, i)

    return pl.pallas_call(
        body, grid=(M // tm,),
        in_specs=[_row_spec(tm, n), _row_spec(tm, n), _row_spec(tm, n), _vec_spec(n), _row_spec(tm, n), _vec_spec(n)],
        out_specs=(_row_spec(tm, n), _row_spec(tm, n), _acc_spec(n), _acc_spec(n)),
        out_shape=(jax.ShapeDtypeStruct((M, n), F32), jax.ShapeDtypeStruct((M, n), BF16),
                   jax.ShapeDtypeStruct((8, n), F32), jax.ShapeDtypeStruct((8, n), F32)),
        name=name, compiler_params=_cparams(),
    )(dx_in, x, dh, g_pre, y, g_post)


def _bwd_last_call(dx_in, x, dh, g_pre, *, name, tm=1024, task=None):
    M, n = x.shape

    def body(dxi_ref, x_ref, dh_ref, g_ref, dx_ref, dg_ref):
        i = pl.program_id(0)
        d1, dg1 = _rms_bwd(x_ref[...], g_ref[...], dh_ref[...])
        dx_ref[...] = dxi_ref[...] + d1
        _acc_add(dg_ref, dg1, i)

    return _pallas(
        body, grid=(M // tm,),
        in_specs=[_row_spec(tm, n), _row_spec(tm, n), _row_spec(tm, n), _vec_spec(n)],
        out_specs=(_row_spec(tm, n), _acc_spec(n)),
        out_shape=(jax.ShapeDtypeStruct((M, n), F32), jax.ShapeDtypeStruct((8, n), F32)),
        name=name, task=task,
    )(dx_in, x, dh, g_pre)


def _gain_grad_call(x, g, dy_a, dy_b, *, name):
    M, n = x.shape

    def body(x_ref, g_ref, a_ref, b_ref, dg_ref):
        _, dg = _rms_bwd(x_ref[...], g_ref[...], a_ref[...] + b_ref[...])
        dg_ref[...] = jnp.zeros_like(dg_ref)
        dg_ref[0:1, :] = dg

    return pl.pallas_call(
        body, grid=(1,),
        in_specs=[_row_spec(M, n), _vec_spec(n), _row_spec(M, n), _row_spec(M, n)],
        out_specs=_acc_spec(n), out_shape=jax.ShapeDtypeStruct((8, n), F32),
        name=name, compiler_params=_cparams(),
    )(x, g, dy_a, dy_b)


def _head_group_matrix():
    a = lax.broadcasted_iota(jnp.int32, (D_GRP, D_GRP), 0) // HEAD
    b = lax.broadcasted_iota(jnp.int32, (D_GRP, D_GRP), 1) // HEAD
    return jnp.where(a == b, 1.0, 0.0).astype(BF16)


def _mix_norm_fwd_call(yf, yc, gf, gc, *, name, tm=1024):
    M = yf.shape[0]

    def body(yf_ref, yc_ref, gf_ref, gc_ref, o_ref):
        o_ref[:, 0:D_GRP] = _rms(yf_ref[...], gf_ref[...]).astype(BF16)
        o_ref[:, D_GRP:D] = _rms(yc_ref[...], gc_ref[...]).astype(BF16)

    return pl.pallas_call(
        body, grid=(M // tm,),
        in_specs=[_row_spec(tm, D_GRP), _row_spec(tm, D_GRP), _vec_spec(D_GRP), _vec_spec(D_GRP)],
        out_specs=_row_spec(tm, D), out_shape=jax.ShapeDtypeStruct((M, D), BF16),
        name=name, compiler_params=_cparams(),
    )(yf, yc, gf, gc)


def _mix_norm_bwd_call(dyn, yf, yc, gf, gc, *, name, tm=2 * TQ):
    M = yf.shape[0]

    def body(dyn_ref, yf_ref, yc_ref, gf_ref, gc_ref, dof_ref, doc_ref, delta_ref, dgf_ref, dgc_ref):
        i = pl.program_id(0)
        yf_ = yf_ref[...]
        dof, dgf = _rms_bwd(yf_, gf_ref[...], dyn_ref[:, 0:D_GRP])
        doc, dgc = _rms_bwd(yc_ref[...], gc_ref[...], dyn_ref[:, D_GRP:D])
        dof_b = dof.astype(BF16)
        dof_ref[...] = dof_b
        doc_ref[...] = doc.astype(BF16)
        prod = dof_b.astype(F32) * yf_
        hi = prod.astype(BF16)
        lo = (prod - hi.astype(F32)).astype(BF16)
        grp = _head_group_matrix()
        delta = _dot(hi, grp) + _dot(lo, grp)
        for b in range(tm // TQ):
            delta_ref[b] = delta[b * TQ:(b + 1) * TQ, :].T
        _acc_add(dgf_ref, dgf, i)
        _acc_add(dgc_ref, dgc, i)

    return pl.pallas_call(
        body, grid=(M // tm,),
        in_specs=[_row_spec(tm, D), _row_spec(tm, D_GRP), _row_spec(tm, D_GRP), _vec_spec(D_GRP), _vec_spec(D_GRP)],
        out_specs=(_row_spec(tm, D_GRP), _row_spec(tm, D_GRP),
                   pl.BlockSpec((tm // TQ, D_GRP, TQ), lambda i: (i, 0, 0)), _acc_spec(D_GRP), _acc_spec(D_GRP)),
        out_shape=(jax.ShapeDtypeStruct((M, D_GRP), BF16), jax.ShapeDtypeStruct((M, D_GRP), BF16),
                   jax.ShapeDtypeStruct((M // TQ, D_GRP, TQ), F32), jax.ShapeDtypeStruct((8, D_GRP), F32),
                   jax.ShapeDtypeStruct((8, D_GRP), F32)),
        name=name, compiler_params=_cparams(),
    )(dyn, yf, yc, gf, gc)


def _tri(n, lower_incl):
    a = lax.broadcasted_iota(jnp.int32, (n, n), 0)
    b = lax.broadcasted_iota(jnp.int32, (n, n), 1)
    return jnp.where(a >= b, 1.0, 0.0).astype(BF16) if lower_incl else jnp.where(a <= b, 1.0, 0.0).astype(BF16)


def _fox_prep_call(fl_raw, b_pad, *, name):
    S = fl_raw.shape[0]
    nb = S // TQ

    def body(fl_ref, b_ref, crep_ref, ct_ref, carry_ref):
        i = pl.program_id(0)

        @pl.when(i == 0)
        def _():
            carry_ref[...] = jnp.zeros_like(carry_ref)
        logf = jax.nn.log_sigmoid(fl_ref[...] + b_ref[...])
        cb = _dot3_l(_tri(TQ, True), logf) + carry_ref[0:1, :]
        carry_ref[0:1, :] = cb[TQ - 1:TQ, :]
        a = lax.broadcasted_iota(jnp.int32, (128, D_GRP), 0)
        b = lax.broadcasted_iota(jnp.int32, (128, D_GRP), 1) // HEAD
        expand = jnp.where(a == b, 1.0, 0.0).astype(BF16)
        crep = _dot3(cb, expand)
        crep_ref[...] = crep
        ct_ref[...] = crep.T

    return pl.pallas_call(
        body, grid=(nb,),
        in_specs=[_row_spec(TQ, 128), _vec_spec(128)],
        out_specs=(_row_spec(TQ, D_GRP), pl.BlockSpec((None, D_GRP, TQ), lambda i: (i, 0, 0))),
        out_shape=(jax.ShapeDtypeStruct((S, D_GRP), F32), jax.ShapeDtypeStruct((nb, D_GRP, TQ), F32)),
        scratch_shapes=[pltpu.VMEM((8, 128), F32)],
        name=name, compiler_params=_cparams(),
    )(fl_raw, b_pad)


def _lane_masks():
    lane = lax.broadcasted_iota(jnp.int32, (1, 128), 1)
    return lane < HEAD, lane >= HEAD


def _fox_fwd_call(proj, c_rep, c_t, *, name, task=None):
    S = proj.shape[0]
    nq = S // TQ
    scale = HEAD ** -0.5

    def body(q_ref, k_ref, v_ref, c_ref, ct_ref, o_ref, lse_ref):
        i = pl.program_id(1)
        m_lo, m_hi = _lane_masks()
        masks = (m_lo, m_hi)
        q = q_ref[...] * scale
        qm = [jnp.where(mk, q, jnp.zeros_like(q)) for mk in masks]
        cq = c_ref[...]
        cqh = [cq[:, 0:1], cq[:, HEAD:HEAD + 1]]
        row = lax.broadcasted_iota(jnp.int32, (TQ, TQ), 0)
        col = lax.broadcasted_iota(jnp.int32, (TQ, TQ), 1)

        def scores(j):
            start = pl.multiple_of(j * TQ, TQ)
            k = k_ref[pl.ds(start, TQ), :]
            ct = ct_ref[j]
            return tuple(_dot_nt(qm[h], k) + (cqh[h] - ct[HEAD * h:HEAD * h + 1, :]) for h in range(2))

        def update(j, ss, state, masked):
            ms, ls, acc = state
            start = pl.multiple_of(j * TQ, TQ)
            v = v_ref[pl.ds(start, TQ), :]
            new_m, new_l, pv, alpha_l = [], [], [], []
            for h in range(2):
                s = ss[h]
                if masked:
                    s = jnp.where(row >= col, s, NEG)
                mn = jnp.maximum(ms[h], jnp.max(s, axis=1, keepdims=True))
                alpha = jnp.exp(ms[h] - mn)
                p = jnp.exp(s - mn)
                new_l.append(alpha * ls[h] + jnp.sum(p, axis=1, keepdims=True))
                new_m.append(mn)
                alpha_l.append(alpha)
                pv.append(_dot(p.astype(BF16), jnp.where(masks[h], v, jnp.zeros_like(v))))
            alpha_lane = jnp.where(m_lo, alpha_l[0], alpha_l[1])
            acc = acc * alpha_lane + pv[0] + pv[1]
            return (tuple(new_m), tuple(new_l), acc)

        def step(j, carry):
            ss, state = carry
            return (scores(j + 1), update(j, ss, state, False))

        init = ((jnp.full((TQ, 1), NEG, F32),) * 2, (jnp.zeros((TQ, 1), F32),) * 2, jnp.zeros((TQ, 128), F32))
        ss, state = lax.fori_loop(0, i, step, (scores(0), init))
        ms, ls, acc = update(i, ss, state, True)
        l_lane = jnp.where(m_lo, ls[0], ls[1])
        o_ref[...] = acc / l_lane
        lse_ref[...] = jnp.where(m_lo, ms[0] + jnp.log(ls[0]), ms[1] + jnp.log(ls[1])).T

    return _pallas(
        body, grid=(N_PAIR, nq),
        in_specs=[pl.BlockSpec((TQ, 128), lambda p, i: (i, p)),
                  pl.BlockSpec((S, 128), lambda p, i: (0, N_PAIR + p)),
                  pl.BlockSpec((S, 128), lambda p, i: (0, 2 * N_PAIR + p)),
                  pl.BlockSpec((TQ, 128), lambda p, i: (i, p)),
                  pl.BlockSpec((nq, 128, TQ), lambda p, i: (0, p, 0))],
        out_specs=(pl.BlockSpec((TQ, 128), lambda p, i: (i, p)), pl.BlockSpec((None, 128, TQ), lambda p, i: (i, p, 0))),
        out_shape=(jax.ShapeDtypeStruct((S, D_GRP), F32), jax.ShapeDtypeStruct((nq, D_GRP, TQ), F32)),
        name=name, task=task,
    )(proj, proj, proj, c_rep, c_t)


def _fox_bwd_call(proj, do, lse_t, delta_t, c_rep, c_t, *, name, task=None):
    S = proj.shape[0]
    nq = S // TQ
    scale = HEAD ** -0.5

    def body(q_ref, k_ref, v_ref, do_ref, lse_ref, dl_ref, ck_ref, ct_ref,
             dq_ref, dk_ref, dv_ref, dcq_ref, dck_ref, dqa_ref):
        j = pl.program_id(1)
        m_lo, m_hi = _lane_masks()
        masks = (m_lo, m_hi)

        @pl.when(j == 0)
        def _():
            dqa_ref[...] = jnp.zeros_like(dqa_ref)
            dcq_ref[...] = jnp.zeros_like(dcq_ref)
        k = k_ref[...]
        v = v_ref[...]
        km = [jnp.where(mk, k, jnp.zeros_like(k)) for mk in masks]
        ck = ck_ref[...]
        krow = lax.broadcasted_iota(jnp.int32, (TQ, TQ), 0)
        qcol = lax.broadcasted_iota(jnp.int32, (TQ, TQ), 1)

        def probs(i):
            start = pl.multiple_of(i * TQ, TQ)
            q = q_ref[pl.ds(start, TQ), :]
            do = do_ref[pl.ds(start, TQ), :]
            lse = lse_ref[i]
            cq = ct_ref[i]
            out = []
            for h in range(2):
                lo = HEAD * h
                qm = jnp.where(masks[h], q * scale, jnp.zeros_like(q))
                dom = jnp.where(masks[h], do, jnp.zeros_like(do))
                st = _dot_nt(k, qm) + (cq[lo:lo + 1, :] - ck[:, lo:lo + 1])
                out.append((jnp.exp(st - lse[lo:lo + 1, :]), _dot_nt(v, dom)))
            return tuple(out)

        def update(i, pd, carry, masked):
            dk, dv, dck = carry
            start = pl.multiple_of(i * TQ, TQ)
            q = q_ref[pl.ds(start, TQ), :]
            do = do_ref[pl.ds(start, TQ), :]
            dl = dl_ref[i]
            dq = jnp.zeros((TQ, 128), F32)
            new_dck = []
            for h in range(2):
                lo = HEAD * h
                qm = jnp.where(masks[h], q, jnp.zeros_like(q))
                dom = jnp.where(masks[h], do, jnp.zeros_like(do))
                pt, dpt = pd[h]
                if masked:
                    pt = jnp.where(qcol >= krow, pt, 0.0)
                dst = pt * (dpt - dl[lo:lo + 1, :])
                dcq_ref[i, h:h + 1, :] += jnp.sum(dst, axis=0, keepdims=True)
                new_dck.append(dck[h] + jnp.sum(dst, axis=1, keepdims=True))
                dsb = (dst * scale).astype(BF16)
                dv = dv + _dot(pt.astype(BF16), dom)
                dk = dk + _dot(dsb, qm)
                dq = dq + _dot_tn(dsb, km[h])
            dqa_ref[pl.ds(start, TQ), :] += dq
            return (dk, dv, tuple(new_dck))

        def step(i, carry):
            pd, sums = carry
            return (probs(jnp.minimum(i + 1, nq - 1)), update(i, pd, sums, False))

        init = (jnp.zeros((TQ, 128), F32), jnp.zeros((TQ, 128), F32), (jnp.zeros((TQ, 1), F32),) * 2)
        first = probs(j)
        second = probs(jnp.minimum(j + 1, nq - 1))
        _, (dk, dv, dck) = lax.fori_loop(j + 1, nq, step, (second, update(j, first, init, True)))
        dk_ref[...] = dk.astype(BF16)
        dv_ref[...] = dv.astype(BF16)
        dck_t = jnp.where(m_lo, dck[0], dck[1]).T
        row = lax.broadcasted_iota(jnp.int32, (8, TQ), 0)
        dck_ref[...] = -jnp.where(row == 0, dck_t[0:1, :], jnp.where(row == 1, dck_t[HEAD:HEAD + 1, :], 0.0))

        @pl.when(j == nq - 1)
        def _():
            dq_ref[...] = dqa_ref[...].astype(BF16)

    res = lambda p, j: (0, p)
    stat = pl.BlockSpec((nq, 128, TQ), lambda p, j: (0, p, 0))
    blk = pl.BlockSpec((TQ, 128), lambda p, j: (j, p))
    return _pallas(
        body, grid=(N_PAIR, nq), task=task,
        in_specs=[pl.BlockSpec((S, 128), res),
                  pl.BlockSpec((TQ, 128), lambda p, j: (j, N_PAIR + p)),
                  pl.BlockSpec((TQ, 128), lambda p, j: (j, 2 * N_PAIR + p)),
                  pl.BlockSpec((S, 128), res), stat, stat, blk, stat],
        out_specs=(pl.BlockSpec((S, 128), res), blk, blk,
                   pl.BlockSpec((None, nq, 8, TQ), lambda p, j: (p, 0, 0, 0)),
                   pl.BlockSpec((None, None, 8, TQ), lambda p, j: (p, j, 0, 0))),
        out_shape=(jax.ShapeDtypeStruct((S, D_GRP), BF16), jax.ShapeDtypeStruct((S, D_GRP), BF16),
                   jax.ShapeDtypeStruct((S, D_GRP), BF16), jax.ShapeDtypeStruct((N_PAIR, nq, 8, TQ), F32),
                   jax.ShapeDtypeStruct((N_PAIR, nq, 8, TQ), F32)),
        scratch_shapes=[pltpu.VMEM((S, 128), F32)],
        name=name,
    )(proj, proj, proj, do, lse_t, delta_t, c_rep, c_t)


def _fox_gate_bwd_call(dc_rows, fl_raw, b_pad, *, name):
    S = fl_raw.shape[0]
    nb = S // TQ

    def body(dc_ref, fl_ref, b_ref, dfl_ref, db_ref, carry_ref):
        i = pl.program_id(0)

        @pl.when(i == 0)
        def _():
            carry_ref[...] = jnp.zeros_like(carry_ref)
        rc = _dot3(dc_ref[...], _tri(TQ, True)) + carry_ref[:, 0:1]
        carry_ref[...] = jnp.broadcast_to(rc[:, 0:1], carry_ref.shape)
        fl = fl_ref[...] + b_ref[...]
        dfl = rc.T * jax.nn.sigmoid(-fl)
        dfl_ref[...] = dfl.astype(BF16)
        _acc_add(db_ref, jnp.sum(dfl, axis=0, keepdims=True), i)

    rev = lambda i: (nb - 1 - i, 0)
    return pl.pallas_call(
        body, grid=(nb,),
        in_specs=[pl.BlockSpec((128, TQ), lambda i: (0, nb - 1 - i)), pl.BlockSpec((TQ, 128), rev), _vec_spec(128)],
        out_specs=(pl.BlockSpec((TQ, 128), rev), _acc_spec(128)),
        out_shape=(jax.ShapeDtypeStruct((S, 128), BF16), jax.ShapeDtypeStruct((8, 128), F32)),
        scratch_shapes=[pltpu.VMEM((128, 128), F32)],
        name=name, compiler_params=_cparams(),
    )(dc_rows, fl_raw, b_pad)


def _chk_bias_call(g_rev, *, name):
    def body(g_ref, o_ref):
        x = jnp.broadcast_to(g_ref[...], (TQ, ROLL_W))
        rolled = pltpu.roll(x, ROLL_W - (TQ - 1), 1, stride=1, stride_axis=0)
        qc = lax.broadcasted_iota(jnp.int32, (TQ, WIN), 0) // CHUNK
        kc = lax.broadcasted_iota(jnp.int32, (TQ, WIN), 1) // CHUNK
        band = (kc >= qc) & (kc <= qc + LEFT)
        o_ref[...] = jnp.where(band, rolled[:, 0:WIN], NEG)

    return pl.pallas_call(
        body, grid=(8,),
        in_specs=[pl.BlockSpec((None, 1, ROLL_W), lambda h: (h, 0, 0))],
        out_specs=pl.BlockSpec((None, TQ, WIN), lambda h: (h, 0, 0)),
        out_shape=jax.ShapeDtypeStruct((8, TQ, WIN), F32), name=name, compiler_params=_cparams(),
    )(g_rev.reshape(8, 1, ROLL_W))


def _chk_scores(i, qm, kwin, bias, scale):
    s = _dot_nt(qm * scale, kwin) + bias
    kc = lax.broadcasted_iota(jnp.int32, (TQ, WIN), 1) // CHUNK
    return jnp.where(kc + i * (TQ // CHUNK) >= LEFT, s, NEG)


def _chk_fwd_call(proj, bias, *, name, task=None):
    S = proj.shape[0]
    nq = S // TQ
    scale = HEAD ** -0.5

    def body(q_ref, k_ref, v_ref, b_ref, o_ref, kp_ref, vp_ref):
        i = pl.program_id(1)

        @pl.when(i == 0)
        def _():
            kp_ref[0:PADK, :] = jnp.zeros((PADK, 128), BF16)
            vp_ref[0:PADK, :] = jnp.zeros((PADK, 128), BF16)
            kp_ref[PADK:PADK + S, :] = k_ref[...]
            vp_ref[PADK:PADK + S, :] = v_ref[...]
        masks = _lane_masks()
        q = q_ref[...]
        start = pl.multiple_of(i * TQ, TQ)
        kwin = kp_ref[pl.ds(start, WIN), :]
        vwin = vp_ref[pl.ds(start, WIN), :]
        ss = [_chk_scores(i, jnp.where(masks[h], q, jnp.zeros_like(q)), kwin, b_ref[h], scale) for h in range(2)]
        ps = []
        for s in ss:
            p = jnp.exp(s - jnp.max(s, axis=1, keepdims=True))
            ps.append((p / jnp.sum(p, axis=1, keepdims=True)).astype(BF16))
        o_ref[...] = (_dot(ps[0], jnp.where(masks[0], vwin, jnp.zeros_like(vwin)))
                      + _dot(ps[1], jnp.where(masks[1], vwin, jnp.zeros_like(vwin))))

    c0 = 3 * N_PAIR
    return _pallas(
        body, grid=(N_PAIR, nq), task=task,
        in_specs=[pl.BlockSpec((TQ, 128), lambda p, i: (i, c0 + p)),
                  pl.BlockSpec((S, 128), lambda p, i: (0, c0 + N_PAIR + p)),
                  pl.BlockSpec((S, 128), lambda p, i: (0, c0 + 2 * N_PAIR + p)),
                  pl.BlockSpec((2, TQ, WIN), lambda p, i: (p, 0, 0))],
        out_specs=pl.BlockSpec((TQ, 128), lambda p, i: (i, p)),
        out_shape=jax.ShapeDtypeStruct((S, D_GRP), F32),
        scratch_shapes=[pltpu.VMEM((S + PADK, 128), BF16), pltpu.VMEM((S + PADK, 128), BF16)],
        name=name,
    )(proj, proj, proj, bias)


def _chk_bwd_call(proj, do, bias, *, name, task=None):
    S = proj.shape[0]
    nq = S // TQ
    scale = HEAD ** -0.5

    def body(q_ref, k_ref, v_ref, do_ref, b_ref, dq_ref, dk_ref, dv_ref, dg_ref, kp_ref, vp_ref, dkp_ref, dvp_ref, db_ref):
        i = pl.program_id(1)

        @pl.when(i == 0)
        def _():
            kp_ref[0:PADK, :] = jnp.zeros((PADK, 128), BF16)
            vp_ref[0:PADK, :] = jnp.zeros((PADK, 128), BF16)
            kp_ref[PADK:PADK + S, :] = k_ref[...]
            vp_ref[PADK:PADK + S, :] = v_ref[...]
            dkp_ref[...] = jnp.zeros_like(dkp_ref)
            dvp_ref[...] = jnp.zeros_like(dvp_ref)
            db_ref[...] = jnp.zeros_like(db_ref)
        masks = _lane_masks()
        q = q_ref[...]
        dout = do_ref[...]
        start = pl.multiple_of(i * TQ, TQ)
        kwin = kp_ref[pl.ds(start, WIN), :]
        vwin = vp_ref[pl.ds(start, WIN), :]
        qm = [jnp.where(mk, q, jnp.zeros_like(q)) for mk in masks]
        dom = [jnp.where(mk, dout, jnp.zeros_like(dout)) for mk in masks]
        ss = [_chk_scores(i, qm[h], kwin, b_ref[h], scale) for h in range(2)]
        dps = [_dot_nt(dom[h], vwin) for h in range(2)]
        pbs, dsbs = [], []
        for h in range(2):
            p = jnp.exp(ss[h] - jnp.max(ss[h], axis=1, keepdims=True))
            p = p / jnp.sum(p, axis=1, keepdims=True)
            ds = p * (dps[h] - jnp.sum(p * dps[h], axis=1, keepdims=True))
            db_ref[h] += ds
            pbs.append(p.astype(BF16))
            dsbs.append((ds * scale).astype(BF16))
        dq_ref[...] = (_dot(dsbs[0], jnp.where(masks[0], kwin, jnp.zeros_like(kwin)))
                       + _dot(dsbs[1], jnp.where(masks[1], kwin, jnp.zeros_like(kwin)))).astype(BF16)
        dkp_ref[pl.ds(start, WIN), :] += _dot_tn(dsbs[0], qm[0]) + _dot_tn(dsbs[1], qm[1])
        dvp_ref[pl.ds(start, WIN), :] += _dot_tn(pbs[0], dom[0]) + _dot_tn(pbs[1], dom[1])

        @pl.when(i == nq - 1)
        def _():
            dk_ref[...] = dkp_ref[PADK:PADK + S, :].astype(BF16)
            dv_ref[...] = dvp_ref[PADK:PADK + S, :].astype(BF16)
            a = lax.broadcasted_iota(jnp.int32, (TQ, TQ), 0)
            b = lax.broadcasted_iota(jnp.int32, (TQ, TQ), 1)
            flip = jnp.where(a + b == TQ - 1, 1.0, 0.0).astype(BF16)
            e = lax.broadcasted_iota(jnp.int32, (1, ROLL_W), 1)
            dg_ref[...] = jnp.zeros_like(dg_ref)
            for h in range(2):
                rev = _dot3_l(flip, db_ref[h])
                wide = jnp.concatenate([rev, jnp.zeros((TQ, ROLL_W - WIN), F32)], axis=1)
                diag = pltpu.roll(wide, 0, 1, stride=1, stride_axis=0)
                dg = jnp.sum(diag, axis=0, keepdims=True)
                lo = jnp.sum(jnp.where(e <= 639, dg, 0.0), axis=1, keepdims=True)
                hi = jnp.sum(jnp.where(e >= 895, dg, 0.0), axis=1, keepdims=True)
                dg_ref[h:h + 1, :] = jnp.where(e == 639, lo, jnp.where(e == 895, hi, dg))

    c0 = 3 * N_PAIR
    res = lambda p, i: (0, p)
    return _pallas(
        body, grid=(N_PAIR, nq), task=task,
        in_specs=[pl.BlockSpec((TQ, 128), lambda p, i: (i, c0 + p)),
                  pl.BlockSpec((S, 128), lambda p, i: (0, c0 + N_PAIR + p)),
                  pl.BlockSpec((S, 128), lambda p, i: (0, c0 + 2 * N_PAIR + p)),
                  pl.BlockSpec((TQ, 128), lambda p, i: (i, p)),
                  pl.BlockSpec((2, TQ, WIN), lambda p, i: (p, 0, 0))],
        out_specs=(pl.BlockSpec((TQ, 128), lambda p, i: (i, p)), pl.BlockSpec((S, 128), res),
                   pl.BlockSpec((S, 128), res), pl.BlockSpec((None, 8, ROLL_W), lambda p, i: (p, 0, 0))),
        out_shape=(jax.ShapeDtypeStruct((S, D_GRP), BF16), jax.ShapeDtypeStruct((S, D_GRP), BF16),
                   jax.ShapeDtypeStruct((S, D_GRP), BF16), jax.ShapeDtypeStruct((N_PAIR, 8, ROLL_W), F32)),
        scratch_shapes=[pltpu.VMEM((S + PADK, 128), BF16), pltpu.VMEM((S + PADK, 128), BF16),
                        pltpu.VMEM((S + PADK, 128), F32), pltpu.VMEM((S + PADK, 128), F32),
                        pltpu.VMEM((2, TQ, WIN), F32)],
        name=name,
    )(proj, proj, proj, do, bias)


def _mem_fwd_call(q, k, v, *, name, tq=2048):
    S = q.shape[0]
    scale = MEM_HD ** -0.5

    def body(q_ref, k_ref, v_ref, o_ref):
        s = _dot_nt(q_ref[...] * scale, k_ref[...])
        p = jnp.exp(s - jnp.max(s, axis=1, keepdims=True))
        p = p / jnp.sum(p, axis=1, keepdims=True)
        o_ref[...] = _dot(p.astype(BF16), v_ref[...]).astype(BF16)

    return pl.pallas_call(
        body, grid=(MEM_HEADS, S // tq),
        in_specs=[pl.BlockSpec((tq, MEM_HD), lambda h, i: (i, h)),
                  pl.BlockSpec((N_MEM, MEM_HD), lambda h, i: (0, h)),
                  pl.BlockSpec((N_MEM, MEM_HD), lambda h, i: (0, h))],
        out_specs=pl.BlockSpec((tq, MEM_HD), lambda h, i: (i, h)),
        out_shape=jax.ShapeDtypeStruct((S, D), BF16), name=name, compiler_params=_cparams(),
    )(q, k, v)


def _mem_bwd_call(q, k, v, do, *, name, tq=2048):
    S = q.shape[0]
    n = S // tq
    scale = MEM_HD ** -0.5

    def body(q_ref, k_ref, v_ref, do_ref, dq_ref, dk_ref, dv_ref, dka_ref, dva_ref):
        i = pl.program_id(1)

        @pl.when(i == 0)
        def _():
            dka_ref[...] = jnp.zeros_like(dka_ref)
            dva_ref[...] = jnp.zeros_like(dva_ref)
        qb = q_ref[...]
        kb = k_ref[...]
        dob = do_ref[...]
        s = _dot_nt(qb * scale, kb)
        p = jnp.exp(s - jnp.max(s, axis=1, keepdims=True))
        p = p / jnp.sum(p, axis=1, keepdims=True)
        dp = _dot_nt(dob, v_ref[...])
        ds = p * (dp - jnp.sum(p * dp, axis=1, keepdims=True))
        dsb = (ds * scale).astype(BF16)
        dq_ref[...] = _dot(dsb, kb).astype(BF16)
        dka_ref[...] += _dot_tn(dsb, qb)
        dva_ref[...] += _dot_tn(p.astype(BF16), dob)

        @pl.when(i == n - 1)
        def _():
            dk_ref[...] = dka_ref[...].astype(BF16)
            dv_ref[...] = dva_ref[...].astype(BF16)

    kv = pl.BlockSpec((N_MEM, MEM_HD), lambda h, i: (0, h))
    qs = pl.BlockSpec((tq, MEM_HD), lambda h, i: (i, h))
    return pl.pallas_call(
        body, grid=(MEM_HEADS, n), in_specs=[qs, kv, kv, qs], out_specs=(qs, kv, kv),
        out_shape=(jax.ShapeDtypeStruct((S, D), BF16), jax.ShapeDtypeStruct((N_MEM, D), BF16),
                   jax.ShapeDtypeStruct((N_MEM, D), BF16)),
        scratch_shapes=[pltpu.VMEM((N_MEM, MEM_HD), F32), pltpu.VMEM((N_MEM, MEM_HD), F32)],
        name=name, compiler_params=_cparams(),
    )(q, k, v, do)


def _rel_table_to_g(rel):
    return jnp.concatenate([
        jnp.broadcast_to(rel[:, N_REL - 1:N_REL], (8, 640)),
        rel[:, 1:N_REL - 1][:, ::-1],
        jnp.broadcast_to(rel[:, 0:1], (8, 129)),
    ], axis=1)


def _g_to_rel_table(dg):
    return dg[:, 639:896][:, ::-1]


def _place():
    x, y, c = lax.axis_index("x"), lax.axis_index("y"), lax.axis_index("c")
    others = [(1 - x, y), (x, 1 - y), (1 - x, 1 - y)]
    return x, y, c, others


def _half(c, rows):
    hr = rows // 2
    return pl.ds(pl.multiple_of(c * hr, 16), hr)


def _dma_sems(*shape):
    return pltpu.SemaphoreType.DMA(shape)


def _cast_slabs_call(ws, chip_arr, *, name, tm=256, task=None):
    n = len(ws)
    cols = ws[0].shape[1]
    tiles = [w.shape[0] // tm for w in ws]
    steps = max(tiles)

    def body(chip_ref, *refs):
        i = pl.program_id(0)
        for k in range(n):
            def cast(k=k):
                refs[n + k][...] = refs[k][...].astype(BF16)
            if tiles[k] == steps:
                cast()
            else:
                pl.when(i < tiles[k])(cast)

    in_specs = [pl.BlockSpec((tm, cols), lambda i, chip, t=t: (jnp.minimum(i, t - 1), 0)) for t in tiles]
    out_specs = [pl.BlockSpec((None, tm, cols), lambda i, chip, t=t: (chip[0], jnp.minimum(i, t - 1), 0)) for t in tiles]
    out_shape = [jax.ShapeDtypeStruct((N_CHIP,) + w.shape, BF16) for w in ws]
    return _pallas(body, grid=(steps,), in_specs=in_specs, out_specs=out_specs, out_shape=out_shape, name=name,
                   task=task, prefetch=1)(chip_arr, *ws)


def _cast_slab_call(w, chip_arr, *, name, tm=256, pad_rows=0):
    rows, cols = w.shape
    if pad_rows:
        tm = rows
    tm = min(tm, rows)

    def body(chip_ref, w_ref, o_ref):
        o_ref[0:tm, :] = w_ref[...].astype(BF16)
        if pad_rows:
            o_ref[tm:tm + pad_rows, :] = jnp.zeros((pad_rows, cols), BF16)

    return pl.pallas_call(
        body,
        grid_spec=pltpu.PrefetchScalarGridSpec(
            num_scalar_prefetch=1, grid=(rows // tm,),
            in_specs=[pl.BlockSpec((tm, cols), lambda i, chip: (i, 0))],
            out_specs=pl.BlockSpec((None, tm + pad_rows, cols), lambda i, chip: (chip[0], i, 0))),
        out_shape=jax.ShapeDtypeStruct((N_CHIP, rows + pad_rows, cols), BF16), name=name,
        compiler_params=_cparams(),
    )(chip_arr, w)


def _ag_ici_task(gathered):
    n = len(gathered)

    def copies(ins, outs, sems):
        send_sems, recv_sems = sems
        x, y, c, others = _place()
        me = 2 * x + y
        for k in range(n):
            mine = _half(c, gathered[k].shape[1])
            for t, (ox, oy) in enumerate(others):
                yield [pltpu.make_async_remote_copy(
                    src_ref=ins[k].at[me, mine], dst_ref=outs[k].at[slab, mine],
                    send_sem=send_sems.at[k, t], recv_sem=recv_sems.at[k, t],
                    device_id=(ox, oy, c), device_id_type=MESH) for slab in (me, 2 * ox + oy)]

    def issue(ins, outs, sems):
        for outgoing, _ in copies(ins, outs, sems):
            outgoing.start()

    def drain(ins, outs, sems):
        for outgoing, incoming in copies(ins, outs, sems):
            incoming.wait_recv()
            outgoing.wait_send()

    return _Task(gathered, [jax.ShapeDtypeStruct(g.shape, g.dtype) for g in gathered],
                 [_dma_sems(n, 3), _dma_sems(n, 3)], issue, drain, aliases={k: k for k in range(n)})


def _ag_d2d_task(gathered):
    n = len(gathered)

    def copies(ins, outs, sems):
        send_sems, recv_sems = sems
        x, y, c, others = _place()
        for k in range(n):
            rows = gathered[k].shape[1]
            mine, theirs = _half(c, rows), _half(1 - c, rows)
            for t, (ox, oy) in enumerate(others):
                slab = 2 * ox + oy
                pair = [pltpu.make_async_remote_copy(
                    src_ref=ins[k].at[slab, half], dst_ref=outs[k].at[slab, half],
                    send_sem=send_sems.at[k, t], recv_sem=recv_sems.at[k, t],
                    device_id=(x, y, 1 - c), device_id_type=MESH) for half in (mine, theirs)]
                yield pair

    def issue(ins, outs, sems):
        for outgoing, _ in copies(ins, outs, sems):
            outgoing.start()

    def drain(ins, outs, sems):
        for outgoing, incoming in copies(ins, outs, sems):
            incoming.wait_recv()
            outgoing.wait_send()

    return _Task(gathered, [jax.ShapeDtypeStruct(g.shape, g.dtype) for g in gathered],
                 [_dma_sems(n, 3), _dma_sems(n, 3)], issue, drain, aliases={k: k for k in range(n)})


def _rs_pair_task(ds):
    n = len(ds)

    def copies(ins, outs, sems):
        send_sems, recv_sems = sems
        x, y, c, _ = _place()
        for k in range(n):
            yield pltpu.make_async_remote_copy(
                src_ref=ins[k].at[:, _half(1 - c, ds[k].shape[1])], dst_ref=outs[k],
                send_sem=send_sems.at[k], recv_sem=recv_sems.at[k],
                device_id=(x, y, 1 - c), device_id_type=MESH)

    def issue(ins, outs, sems):
        for cp in copies(ins, outs, sems):
            cp.start()

    def drain(ins, outs, sems):
        for cp in copies(ins, outs, sems):
            cp.wait()

    return _Task(ds, [jax.ShapeDtypeStruct((N_CHIP, d.shape[1] // 2, d.shape[2]), d.dtype) for d in ds],
                 [_dma_sems(n), _dma_sems(n)], issue, drain)


def _pair_add_call(d, r1, c_arr, *, name, tm=512):
    _, rows, cols = d.shape
    hr = rows // 2
    tm = tm if hr % tm == 0 else hr
    nb = hr // tm

    def body(c_ref, d_ref, r_ref, o_ref):
        o_ref[...] = (d_ref[...].astype(F32) + r_ref[...].astype(F32)).astype(BF16)

    return pl.pallas_call(
        body,
        grid_spec=pltpu.PrefetchScalarGridSpec(
            num_scalar_prefetch=1, grid=(N_CHIP, nb),
            in_specs=[pl.BlockSpec((None, tm, cols), lambda j, i, c: (j, c[0] * nb + i, 0)),
                      pl.BlockSpec((None, tm, cols), lambda j, i, c: (j, i, 0))],
            out_specs=pl.BlockSpec((None, tm, cols), lambda j, i, c: (j, i, 0))),
        out_shape=jax.ShapeDtypeStruct((N_CHIP, hr, cols), BF16), name=name, compiler_params=_cparams(),
    )(c_arr, d, r1)


def _rs_chip_task(ps):
    n = len(ps)

    def copies(ins, outs, sems):
        send_sems, recv_sems = sems
        x, y, c, others = _place()
        for k in range(n):
            for t, (ox, oy) in enumerate(others):
                yield pltpu.make_async_remote_copy(
                    src_ref=ins[k].at[2 * ox + oy], dst_ref=outs[k].at[t],
                    send_sem=send_sems.at[k, t], recv_sem=recv_sems.at[k, t],
                    device_id=(ox, oy, c), device_id_type=MESH)

    def issue(ins, outs, sems):
        for cp in copies(ins, outs, sems):
            cp.start()

    def drain(ins, outs, sems):
        for cp in copies(ins, outs, sems):
            cp.wait()

    return _Task(ps, [jax.ShapeDtypeStruct((3,) + p.shape[1:], p.dtype) for p in ps],
                 [_dma_sems(n, 3), _dma_sems(n, 3)], issue, drain)


def _chip_sum_call(p, r2, place_arr, *, name, tm=512):
    _, hr, cols = r2.shape
    tm = tm if hr % tm == 0 else hr
    nb = hr // tm

    def body(place_ref, p_ref, r_ref, o_ref):
        acc = p_ref[...].astype(F32)
        for j in range(3):
            acc = acc + r_ref[j].astype(F32)
        o_ref[...] = acc

    return _pallas(
        body, grid=(nb,), prefetch=1,
        in_specs=[pl.BlockSpec((None, tm, cols), lambda i, pc: (pc[0], i, 0)),
                  pl.BlockSpec((3, tm, cols), lambda i, pc: (0, i, 0))],
        out_specs=pl.BlockSpec((tm, cols), lambda i, pc: (pc[1] * nb + i, 0)),
        out_shape=jax.ShapeDtypeStruct((2 * hr, cols), F32), name=name,
    )(place_arr, p, r2)


def _chip_sums_call(ps, r2s, place_arr, *, name, steps=2, task=None):
    n = len(ps)

    def body(place_ref, *refs):
        ins, outs = refs[:2 * n], refs[2 * n:]
        for k in range(n):
            acc = ins[2 * k][...].astype(F32)
            for j in range(3):
                acc = acc + ins[2 * k + 1][j].astype(F32)
            outs[k][...] = acc

    in_specs, out_specs, out_shape, args = [], [], [], []
    for p, r2 in zip(ps, r2s):
        _, hr, cols = r2.shape
        tm = hr // steps
        in_specs += [pl.BlockSpec((None, tm, cols), lambda i, pc: (pc[0], i, 0)),
                     pl.BlockSpec((3, tm, cols), lambda i, pc: (0, i, 0))]
        out_specs.append(pl.BlockSpec((tm, cols), lambda i, pc: (pc[1] * steps + i, 0)))
        out_shape.append(jax.ShapeDtypeStruct((2 * hr, cols), F32))
        args += [p, r2]
    return _pallas(body, grid=(steps,), prefetch=1, in_specs=in_specs, out_specs=out_specs, out_shape=out_shape,
                   name=name, task=task)(place_arr, *args)


def _rs_gather_task(gs):
    n = len(gs)

    def copies(ins, outs, sems):
        send_sems, recv_sems = sems
        x, y, c, _ = _place()
        for k in range(n):
            rows = gs[k].shape[0]
            mine, theirs = _half(c, rows), _half(1 - c, rows)
            yield [pltpu.make_async_remote_copy(
                src_ref=ins[k].at[mine], dst_ref=outs[k].at[half],
                send_sem=send_sems.at[k], recv_sem=recv_sems.at[k],
                device_id=(x, y, 1 - c), device_id_type=MESH) for half in (mine, theirs)]

    def issue(ins, outs, sems):
        for outgoing, _ in copies(ins, outs, sems):
            outgoing.start()

    def drain(ins, outs, sems):
        for outgoing, incoming in copies(ins, outs, sems):
            incoming.wait_recv()
            outgoing.wait_send()

    return _Task(gs, [jax.ShapeDtypeStruct(g.shape, g.dtype) for g in gs],
                 [_dma_sems(n), _dma_sems(n)], issue, drain, aliases={k: k for k in range(n)})


def _adamw(w, g, m, v):
    m = ADAM_B1 * m + (1.0 - ADAM_B1) * g
    v = ADAM_B2 * v + (1.0 - ADAM_B2) * jnp.square(g)
    m_hat = m / (1.0 - ADAM_B1 ** ADAM_STEP)
    v_hat = v / (1.0 - ADAM_B2 ** ADAM_STEP)
    delta = -ADAM_LR * (m_hat / (jnp.sqrt(v_hat) + ADAM_EPS) + ADAM_WD * w)
    return delta, m, v


def _adamw_call(items, *, name, tm=256, task=None):
    n = len(items)
    cols = items[0][0].shape[1]
    tiles = [it[0].shape[0] // tm for it in items]
    steps = max(tiles)

    def body(*refs):
        i = pl.program_id(0)
        ins, outs = refs[:4 * n], refs[4 * n:]
        for k in range(n):
            def update(k=k):
                g = ins[4 * k + 1][...]
                res = _adamw(ins[4 * k][...], g, ins[4 * k + 2][...], ins[4 * k + 3][...])
                outs[4 * k][...] = g
                for j in range(3):
                    outs[4 * k + 1 + j][...] = res[j]
            if tiles[k] == steps:
                update()
            else:
                pl.when(i < tiles[k])(update)

    in_specs, out_specs, out_shape, args = [], [], [], []
    for it, t in zip(items, tiles):
        spec = pl.BlockSpec((tm, cols), lambda i, t=t: (jnp.minimum(i, t - 1), 0))
        in_specs += [spec] * 4
        out_specs += [spec] * 4
        out_shape += [jax.ShapeDtypeStruct(it[0].shape, F32)] * 4
        args += list(it)
    res = _pallas(body, grid=(steps,), in_specs=in_specs, out_specs=out_specs, out_shape=out_shape,
                  name=name, task=task)(*args)
    outs, extra = res if task is not None else (res, None)
    grouped = [tuple(outs[4 * k:4 * k + 4]) for k in range(n)]
    return (grouped, extra) if task is not None else grouped


def _adamw_cols_call(w, g_pad, m, v, *, name, tn=512):
    rows, cols = w.shape

    def body(w_ref, g_ref, m_ref, v_ref, go_ref, d_ref, mo_ref, vo_ref):
        g = g_ref[0:rows, :]
        d, mn, vn = _adamw(w_ref[...], g, m_ref[...], v_ref[...])
        go_ref[...] = g
        d_ref[...] = d
        mo_ref[...] = mn
        vo_ref[...] = vn

    spec = pl.BlockSpec((rows, tn), lambda j: (0, j))
    gspec = pl.BlockSpec((g_pad.shape[0], tn), lambda j: (0, j))
    return _pallas(body, grid=(cols // tn,), in_specs=[spec, gspec, spec, spec], out_specs=(spec,) * 4,
                   out_shape=(jax.ShapeDtypeStruct((rows, cols), F32),) * 4, name=name)(w, g_pad, m, v)


N_DEV = 8
SMALL_ROWS = 24
SMALL_LAYOUT = {
    "g_mix_pre": (0, 0, 1, D), "g_mix_post": (1, 0, 1, D), "g_mem_kv": (2, 0, 1, D), "g_mem_pre": (3, 0, 1, D),
    "g_mem_post": (4, 0, 1, D), "g_ff_pre": (5, 0, 1, D), "g_ff_post": (6, 0, 1, D),
    "g_fox_out": (7, 0, 1, D_GRP), "g_chk_out": (7, D_GRP, 1, D_GRP), "b_fgt": (8, 0, 1, 8),
    "rel_bias": (16, 0, 8, N_REL),
}
SMALL = list(SMALL_LAYOUT)


LOSS_ROW = 9


def _small_reduce_call(grads, loss_blk, task, *, name):
    n = len(SMALL)
    t_in, t_out = len(task.arrays), len(task.out_shapes)

    def body(*refs):
        g_refs, loss_ref, tins = refs[:n], refs[n], refs[n + 1:n + 1 + t_in]
        p = n + 1 + t_in
        total_ref, loss_out, touts = refs[p], refs[p + 1], refs[p + 2:p + 2 + t_out]
        p += 2 + t_out
        mine, slots, send_sems, recv_sems = refs[p:p + 4]
        tsems = refs[p + 4:]
        task.issue(tins, touts, tsems)
        x, y, c, _ = _place()
        me = 4 * x + 2 * y + c
        mine[...] = jnp.zeros_like(mine)
        for k, name_k in enumerate(SMALL):
            r, l, nr, nl = SMALL_LAYOUT[name_k]
            mine[r:r + nr, l:l + nl] = g_refs[k][0:nr, 0:nl]
        mine[LOSS_ROW:LOSS_ROW + 1, 0:128] = loss_ref[0:1, :]
        slots[me] = mine[...]
        peers = [(dx, dy, dc) for dx in (0, 1) for dy in (0, 1) for dc in (0, 1)][1:]
        cps = []
        for t, (dx, dy, dc) in enumerate(peers):
            px, py, pc = (x + dx) % 2, (y + dy) % 2, (c + dc) % 2
            cps.append(pltpu.make_async_remote_copy(
                src_ref=mine, dst_ref=slots.at[me], send_sem=send_sems.at[t], recv_sem=recv_sems.at[t],
                device_id=(px, py, pc), device_id_type=MESH))
            cps[-1].start()
        for t, (dx, dy, dc) in enumerate(peers):
            px, py, pc = (x + dx) % 2, (y + dy) % 2, (c + dc) % 2
            pltpu.make_async_remote_copy(
                src_ref=mine, dst_ref=slots.at[4 * px + 2 * py + pc], send_sem=send_sems.at[t],
                recv_sem=recv_sems.at[t], device_id=(px, py, pc), device_id_type=MESH).wait_recv()
        for cp in cps:
            cp.wait_send()
        total = slots[0]
        for j in range(1, N_DEV):
            total = total + slots[j]
        total_ref[...] = total
        loss_out[...] = jnp.broadcast_to(total[LOSS_ROW:LOSS_ROW + 1, 0:128], loss_out.shape)
        task.drain(tins, touts, tsems)

    vm = pl.BlockSpec(memory_space=pltpu.VMEM)
    out_shape = [jax.ShapeDtypeStruct((SMALL_ROWS, D), F32), jax.ShapeDtypeStruct((8, 128), F32)] + list(task.out_shapes)
    res = pl.pallas_call(
        body, in_specs=[vm] * (n + 1) + [ANY] * t_in, out_specs=[vm] * 2 + [ANY] * t_out,
        out_shape=out_shape,
        scratch_shapes=[pltpu.VMEM((SMALL_ROWS, D), F32), pltpu.VMEM((N_DEV, SMALL_ROWS, D), F32),
                        _dma_sems(N_DEV - 1), _dma_sems(N_DEV - 1)] + list(task.sems),
        input_output_aliases={n + 1 + i: 2 + j for i, j in task.aliases.items()},
        name=name,
    )(*[grads[k] for k in SMALL], loss_blk, *task.arrays)
    return res[0], res[1], list(res[2:])


def _small_adamw_call(total, ws, ms, vs, *, name):
    n = len(SMALL)

    def body(*refs):
        total_ref = refs[0]
        w_refs, m_refs, v_refs = (refs[1 + j * n:1 + (j + 1) * n] for j in range(3))
        outs = refs[1 + 3 * n:]
        for k, name_k in enumerate(SMALL):
            r, l, nr, nl = SMALL_LAYOUT[name_k]
            g = total_ref[r:r + nr, l:l + nl]
            d, mn, vn = _adamw(w_refs[k][...], g, m_refs[k][...], v_refs[k][...])
            for j, val in enumerate((g, d, mn, vn)):
                outs[4 * k + j][...] = val

    vm = pl.BlockSpec(memory_space=pltpu.VMEM)
    res = pl.pallas_call(
        body, in_specs=[vm] * (3 * n + 1), out_specs=[vm] * (4 * n),
        out_shape=[jax.ShapeDtypeStruct(ws[k].shape, F32) for k in SMALL for _ in range(4)], name=name,
    )(total, *[d[k] for d in (ws, ms, vs) for k in SMALL])
    return {k: tuple(res[4 * i:4 * i + 4]) for i, k in enumerate(SMALL)}


WEIGHTS = ["w_in", "b_fgt", "rel_bias", "g_fox_out", "g_chk_out", "w_out", "g_mix_pre", "g_mix_post", "g_mem_kv",
           "w_mq", "w_mk", "w_mv", "w_mo", "g_mem_pre", "g_mem_post", "w_ff1", "w_ff2", "g_ff_pre", "g_ff_post"]
BIG = ["w_in", "w_out", "w_mq", "w_mk", "w_mv", "w_mo", "w_ff1", "w_ff2"]


IN_SHARD = D_IN // N_CHIP
IN_PAD = 800


IN_PIECES = [(0, 0, 770), (800, 770, 766), (1566, 3072, 4), (1600, 3076, 4), (1604, 1536, 766), (2400, 2302, 770)]
PAD_ZEROS = [(800 * j + IN_SHARD, IN_PAD - IN_SHARD) for j in range(N_CHIP)]
ALL_ZEROS = [(D_IN, D_ALL - D_IN)]


def _reorder_rows_call(src, to_all, *, name, tn=512):
    rows, cols = src.shape
    zeros = ALL_ZEROS if to_all else PAD_ZEROS

    def body(s_ref, o_ref):
        for pad0, all0, cnt in IN_PIECES:
            s0, d0 = (pad0, all0) if to_all else (all0, pad0)
            o_ref[d0:d0 + cnt, :] = s_ref[s0:s0 + cnt, :]
        for z0, cnt in zeros:
            o_ref[z0:z0 + cnt, :] = jnp.zeros((cnt, tn), src.dtype)

    spec = pl.BlockSpec((rows, tn), lambda j: (0, j))
    return _pallas(body, grid=(cols // tn,), in_specs=[spec], out_specs=spec,
                   out_shape=jax.ShapeDtypeStruct((rows, cols), src.dtype), name=name)(src)


def kernel(x, mem, w_in, b_fgt, rel_bias, g_fox_out, g_chk_out, w_out, g_mix_pre, g_mix_post, g_mem_kv, w_mq, w_mk, w_mv, w_mo, g_mem_pre, g_mem_post, w_ff1, w_ff2, g_ff_pre, g_ff_post, loss_target, m_w_in, m_b_fgt, m_rel_bias, m_g_fox_out, m_g_chk_out, m_w_out, m_g_mix_pre, m_g_mix_post, m_g_mem_kv, m_w_mq, m_w_mk, m_w_mv, m_w_mo, m_g_mem_pre, m_g_mem_post, m_w_ff1, m_w_ff2, m_g_ff_pre, m_g_ff_post, v_w_in, v_b_fgt, v_rel_bias, v_g_fox_out, v_g_chk_out, v_w_out, v_g_mix_pre, v_g_mix_post, v_g_mem_kv, v_w_mq, v_w_mk, v_w_mv, v_w_mo, v_g_mem_pre, v_g_mem_post, v_w_ff1, v_w_ff2, v_g_ff_pre, v_g_ff_post):
    w = dict(w_in=w_in, b_fgt=b_fgt, rel_bias=rel_bias, g_fox_out=g_fox_out, g_chk_out=g_chk_out, w_out=w_out,
             g_mix_pre=g_mix_pre, g_mix_post=g_mix_post, g_mem_kv=g_mem_kv, w_mq=w_mq, w_mk=w_mk, w_mv=w_mv,
             w_mo=w_mo, g_mem_pre=g_mem_pre, g_mem_post=g_mem_post, w_ff1=w_ff1, w_ff2=w_ff2, g_ff_pre=g_ff_pre,
             g_ff_post=g_ff_post)
    m = dict(w_in=m_w_in, b_fgt=m_b_fgt, rel_bias=m_rel_bias, g_fox_out=m_g_fox_out, g_chk_out=m_g_chk_out,
             w_out=m_w_out, g_mix_pre=m_g_mix_pre, g_mix_post=m_g_mix_post, g_mem_kv=m_g_mem_kv, w_mq=m_w_mq,
             w_mk=m_w_mk, w_mv=m_w_mv, w_mo=m_w_mo, g_mem_pre=m_g_mem_pre, g_mem_post=m_g_mem_post,
             w_ff1=m_w_ff1, w_ff2=m_w_ff2, g_ff_pre=m_g_ff_pre, g_ff_post=m_g_ff_post)
    v = dict(w_in=v_w_in, b_fgt=v_b_fgt, rel_bias=v_rel_bias, g_fox_out=v_g_fox_out, g_chk_out=v_g_chk_out,
             w_out=v_w_out, g_mix_pre=v_g_mix_pre, g_mix_post=v_g_mix_post, g_mem_kv=v_g_mem_kv, w_mq=v_w_mq,
             w_mk=v_w_mk, w_mv=v_w_mv, w_mo=v_w_mo, g_mem_pre=v_g_mem_pre, g_mem_post=v_g_mem_post,
             w_ff1=v_w_ff1, w_ff2=v_w_ff2, g_ff_pre=v_g_ff_pre, g_ff_post=v_g_ff_post)

    def rows(d, k):
        return d[k][0] if k == "rel_bias" else d[k]

    xs, mems, target = x[0], mem[0], loss_target[0]
    S = xs.shape[0]
    sp = {k: rows(w, k) for k in SMALL}
    b_pad = jnp.pad(sp["b_fgt"], ((0, 0), (0, 120)))
    chip = 2 * lax.axis_index("x") + lax.axis_index("y")
    chip_arr = jnp.reshape(chip, (1,)).astype(jnp.int32)
    c_arr = jnp.reshape(lax.axis_index("c"), (1,)).astype(jnp.int32)
    place_arr = jnp.concatenate([chip_arr, c_arr])
    w_in_t, m_in_t, v_in_t = w["w_in"][0].T, m["w_in"][0].T, v["w_in"][0].T
    slab = {"w_in": _cast_slab_call(w_in_t, chip_arr, name="cast_w_in", pad_rows=IN_PAD - IN_SHARD)}

    def gather_ici(names):
        return _ag_ici_task([slab[k] for k in names])

    def pair_add(k, d, r1):
        return _pair_add_call(d, r1, c_arr, name="rs_pair_add_" + k)

    rest, (g_in,) = _cast_slabs_call([w[k][0] for k in BIG[1:]], chip_arr, name="cast_rest",
                                     task=gather_ici(["w_in"]))
    slab.update(zip(BIG[1:], rest))
    h1, (g_in,) = _rms_fwd_call(xs, sp["g_mix_pre"], name="rms_mix_pre", task=_ag_d2d_task([g_in]))
    w_all_t = _reorder_rows_call(g_in.reshape(N_CHIP * IN_PAD, D), True, name="w_in_rows")
    proj, (g_out, g_mq) = _mm_nt(h1, w_all_t, "plain", rows=(0, 3072), name="mm_proj",
                                 task=gather_ici(["w_out", "w_mq"]))
    fl_raw = _mm_nt(h1, w_all_t, "plain", rows=(3072, 128), name="mm_gate", out_dtype=F32, tn=128)
    c_rep, c_t = _fox_prep_call(fl_raw, b_pad, name="fox_prep")
    bias = _chk_bias_call(_rel_table_to_g(sp["rel_bias"]), name="chk_bias")
    mid = ["w_mk", "w_mv", "w_mo", "w_ff1"]
    (yf, lse), got = _fox_fwd_call(proj, c_rep, c_t, name="fox_fwd",
                                   task=_merge_tasks([gather_ici(mid), _ag_d2d_task([g_out, g_mq])]))
    g_mid, (g_out, g_mq) = got[:4], got[4:]
    yc, got = _chk_fwd_call(proj, bias, name="chk_fwd",
                            task=_merge_tasks([gather_ici(["w_ff2"]), _ag_d2d_task(g_mid)]))
    g_ff2, (g_mk, g_mv, g_mo, g_ff1) = got[0], got[1:]
    yn = _mix_norm_fwd_call(yf, yc, sp["g_fox_out"], sp["g_chk_out"], name="mix_norm_fwd")
    z, (g_ff2,) = _mm_nn(yn, g_out, "rows", name="mm_out", out_dtype=F32, task=_ag_d2d_task([g_ff2]))
    x1, h2 = _post_pre_call(xs, z, sp["g_mix_post"], sp["g_mem_pre"], name="post_mix")
    memn = _rms_fwd_call(mems, sp["g_mem_kv"], name="rms_mem_kv")
    q2 = _mm_nn(h2, g_mq, "rows", name="mm_mq")
    k2 = _mm_nn(memn, g_mk, "rows", name="mm_mk")
    v2 = _mm_nn(memn, g_mv, "rows", name="mm_mv")
    o2 = _mem_fwd_call(q2, k2, v2, name="mem_fwd")
    y2 = _mm_nn(o2, g_mo, "rows", name="mm_mo", out_dtype=F32)
    x2, h3 = _post_pre_call(x1, y2, sp["g_mem_post"], sp["g_ff_pre"], name="post_mem")
    act, relu = _mm_nn(h3, g_ff1, "cols", name="mm_ff1", epi="relu2", tn=1024)
    y3 = _mm_nn(act, g_ff2, "rows", name="mm_ff2", out_dtype=F32, tm=1024)
    loss_blk, dx3, dy3, dg_ff_post = _final_call(x2, y3, sp["g_ff_post"], target, name="final")

    d_ff2 = _mm_tn(act, dy3, name="mm_dff2", tk=512, tn=1024).reshape(N_CHIP, D_FF // N_CHIP, D)
    du, (r1,) = _mm_nt(dy3, g_ff2, "rows", name="mm_du", mul2r=relu, tn=1024, task=_rs_pair_task([d_ff2]))
    p_ff2 = pair_add("w_ff2", d_ff2, r1)
    d_ff1 = _mm_tn(h3, du, name="mm_dff1", cols4=True)
    dh3, (r1,) = _mm_nt(du, g_ff1, "cols", name="mm_dh3", out_dtype=F32, tm=1024, task=_rs_pair_task([d_ff1]))
    p_ff1 = pair_add("w_ff1", d_ff1, r1)
    dx2, dy2, dg_ff_pre, dg_mem_post = _bwd_mid_call(dx3, x2, dh3, sp["g_ff_pre"], y2, sp["g_mem_post"], name="bwd_ff")
    d_mo = _mm_tn(o2, dy2, name="mm_dmo").reshape(N_CHIP, D // N_CHIP, D)
    do2 = _mm_nt(dy2, g_mo, "rows", name="mm_do2")
    dq2, dk2, dv2 = _mem_bwd_call(q2, k2, v2, do2, name="mem_bwd")
    d_mq = _mm_tn(h2, dq2, name="mm_dmq").reshape(N_CHIP, D // N_CHIP, D)
    dh2 = _mm_nt(dq2, g_mq, "rows", name="mm_dh2", out_dtype=F32)
    d_mk = _mm_tn(memn, dk2, name="mm_dmk").reshape(N_CHIP, D // N_CHIP, D)
    d_mv = _mm_tn(memn, dv2, name="mm_dmv").reshape(N_CHIP, D // N_CHIP, D)
    dmn_k = _mm_nt(dk2, g_mk, "rows", name="mm_dmemk", out_dtype=F32)
    dmn_v = _mm_nt(dv2, g_mv, "rows", name="mm_dmemv", out_dtype=F32)
    dg_mem_kv = _gain_grad_call(mems, sp["g_mem_kv"], dmn_k, dmn_v, name="gain_mem_kv")
    dx1, dz, dg_mem_pre, dg_mix_post = _bwd_mid_call(dx2, x1, dh2, sp["g_mem_pre"], z, sp["g_mix_post"], name="bwd_mem")
    d_out = _mm_tn(yn, dz, name="mm_dout").reshape(N_CHIP, D // N_CHIP, D)
    late = ["w_mo", "w_mq", "w_mk", "w_mv", "w_out"]
    d_late = [d_mo, d_mq, d_mk, d_mv, d_out]
    dyn, r1_late = _mm_nt(dz, g_out, "rows", name="mm_dyn", out_dtype=F32, task=_rs_pair_task(d_late))
    p_late = [pair_add(k, d, r1) for k, d, r1 in zip(late, d_late, r1_late)]
    dof, doc, delta, dg_fox, dg_chk = _mix_norm_bwd_call(dyn, yf, yc, sp["g_fox_out"], sp["g_chk_out"], name="mix_norm_bwd")
    (dqf, dkf, dvf, dcq, dck), r2_ff = _fox_bwd_call(proj, dof, lse, delta, c_rep, c_t, name="fox_bwd",
                                                      task=_rs_chip_task([p_ff2, p_ff1]))
    (dqc, dkc, dvc, dgrev), r2_late = _chk_bwd_call(proj, doc, bias, name="chk_bwd", task=_rs_chip_task(p_late))
    first = ["w_ff2", "w_ff1"] + late
    dc8 = (dcq[:, :, 0:2, :] + dck[:, :, 0:2, :]).transpose(0, 2, 1, 3).reshape(8, S)
    dc_rows = jnp.concatenate([dc8, jnp.zeros((120, S), F32)], axis=0)
    dfl, db_fgt = _fox_gate_bwd_call(dc_rows, fl_raw, b_pad, name="fox_gate_bwd")
    dproj = jnp.concatenate([dqf, dkf, dvf, dqc, dkc, dvc, dfl], axis=1)
    d_all_t = _mm_tn(dproj, h1, name="mm_dwin", tk=640, tn=1024)
    d_in = _reorder_rows_call(d_all_t, False, name="d_in_rows").reshape(N_CHIP, IN_PAD, D)
    f_first, (r1,) = _chip_sums_call([p_ff2, p_ff1] + p_late, r2_ff + r2_late, place_arr, name="rs_chip_sums",
                                     task=_rs_pair_task([d_in]))
    p_in = pair_add("w_in", d_in, r1)
    dh1, got = _mm_nn(dproj, w_all_t, "plain", name="mm_dh1", out_dtype=F32, tm=1024,
                      task=_merge_tasks([_rs_chip_task([p_in]), _rs_gather_task(f_first)]))
    r2_in, grads = got[0], dict(zip(first, got[1:]))
    f_in = _chip_sum_call(p_in, r2_in, place_arr, name="rs_chip_sum_w_in")
    delta_w, new_m, new_v = {}, {}, {}

    def adamw_items(names):
        return [(w[k][0], grads[k], m[k][0], v[k][0]) for k in names]

    upd_late = _adamw_call(adamw_items(late), name="adamw_late", tm=64)
    upd_ff = _adamw_call(adamw_items(first[:2]), name="adamw_ff")
    for k, res in zip(late + first[:2], upd_late + upd_ff):
        grads[k], delta_w[k], new_m[k], new_v[k] = res
    grad_x, dg_mix_pre = _bwd_last_call(dx1, xs, dh1, sp["g_mix_pre"], name="bwd_mix")

    small_g = {"g_mix_pre": dg_mix_pre, "g_mix_post": dg_mix_post, "g_mem_kv": dg_mem_kv, "g_mem_pre": dg_mem_pre,
               "g_mem_post": dg_mem_post, "g_ff_pre": dg_ff_pre, "g_ff_post": dg_ff_post, "g_fox_out": dg_fox,
               "g_chk_out": dg_chk, "b_fgt": db_fgt,
               "rel_bias": _g_to_rel_table(dgrev[:, 0:2, :].reshape(8, ROLL_W))}
    small_sum, loss_out, (g_w_in,) = _small_reduce_call(small_g, loss_blk, _rs_gather_task([f_in]),
                                                       name="small_allreduce")
    small = _small_adamw_call(small_sum, sp, {k: rows(m, k) for k in SMALL}, {k: rows(v, k) for k in SMALL},
                              name="small_adamw")
    loss = loss_out[0, 0]
    res = _adamw_cols_call(w_in_t, g_w_in, m_in_t, v_in_t, name="adamw_w_in")
    grads["w_in"], delta_w["w_in"], new_m["w_in"], new_v["w_in"] = (a.T for a in res)
    for k in SMALL:
        vals = small[k]
        if k == "rel_bias":
            vals = tuple(a[None] for a in vals)
        grads[k], delta_w[k], new_m[k], new_v[k] = vals

    def out(d, k):
        return d[k][None] if k in BIG else d[k]

    return (loss, grad_x[None], *[out(grads, k) for k in WEIGHTS], *[out(delta_w, k) for k in WEIGHTS],
            *[out(new_m, k) for k in WEIGHTS], *[out(new_v, k) for k in WEIGHTS])
```

```python
import functools

import jax
import jax.numpy as jnp
from jax import lax
from jax.experimental import pallas as pl
from jax.experimental.pallas import tpu as pltpu

F32 = jnp.float32
BF16 = jnp.bfloat16

D = 1024
HEAD = 64
N_PAIR = 4
D_GRP = 512
CHUNK = 64
LEFT = 8
MAX_REL = 128
N_REL = 2 * MAX_REL + 1
N_MEM = 256
MEM_HEADS = 4
MEM_HD = 256
D_FF = 4096
D_IN = 3080
D_ALL = 3200
EPS = 1e-6
TQ = 256
WIN = (LEFT + TQ // CHUNK) * CHUNK
PADK = LEFT * CHUNK
ROLL_W = 1024
NEG = -1e30
N_CHIP = 4
VMEM_LIMIT = 48 * 1024 * 1024

ADAM_LR = 0.001
ADAM_B1 = 0.9
ADAM_B2 = 0.999
ADAM_EPS = 1e-08
ADAM_WD = 0.01
ADAM_STEP = 10

MESH = pl.DeviceIdType.MESH


def _cparams():
    return pltpu.CompilerParams(vmem_limit_bytes=VMEM_LIMIT)


ANY = pl.BlockSpec(memory_space=pl.ANY)


class _Task:
    def __init__(self, arrays, out_shapes, sems, issue, drain, aliases=None):
        self.arrays, self.out_shapes, self.sems = list(arrays), list(out_shapes), list(sems)
        self.issue, self.drain, self.aliases = issue, drain, dict(aliases or {})


def _merge_tasks(tasks):
    tasks = [t for t in tasks if t is not None]
    if len(tasks) == 1:
        return tasks[0]
    cuts, a, o, s = [], 0, 0, 0
    aliases = {}
    for t in tasks:
        cuts.append((a, o, s))
        aliases.update({a + i: o + j for i, j in t.aliases.items()})
        a, o, s = a + len(t.arrays), o + len(t.out_shapes), s + len(t.sems)

    def part(fn_name):
        def run(ins, outs, sems):
            for t, (a0, o0, s0) in zip(tasks, cuts):
                getattr(t, fn_name)(ins[a0:a0 + len(t.arrays)], outs[o0:o0 + len(t.out_shapes)],
                                    sems[s0:s0 + len(t.sems)])
        return run

    return _Task([x for t in tasks for x in t.arrays], [x for t in tasks for x in t.out_shapes],
                 [x for t in tasks for x in t.sems], part("issue"), part("drain"), aliases)


def _pallas(body, *, grid, in_specs, out_specs, out_shape, name, scratch_shapes=(), task=None, prefetch=0):
    def make(kernel, i_specs, o_specs, o_shape, scratch, aliases):
        if prefetch:
            spec = pltpu.PrefetchScalarGridSpec(num_scalar_prefetch=prefetch, grid=grid, in_specs=i_specs,
                                                out_specs=o_specs, scratch_shapes=scratch)
            return pl.pallas_call(kernel, grid_spec=spec, out_shape=o_shape, input_output_aliases=aliases,
                                  name=name, compiler_params=_cparams())
        return pl.pallas_call(kernel, grid=grid, in_specs=i_specs, out_specs=o_specs, out_shape=o_shape,
                              scratch_shapes=scratch, input_output_aliases=aliases, name=name,
                              compiler_params=_cparams())

    if task is None:
        return make(body, list(in_specs), out_specs, out_shape, list(scratch_shapes), {})
    single = not isinstance(out_shape, (tuple, list))
    o_shapes = [out_shape] if single else list(out_shape)
    o_specs = [out_specs] if single else list(out_specs)
    n_in, n_out, n_scr = len(in_specs), len(o_shapes), len(scratch_shapes)
    t_in, t_out = len(task.arrays), len(task.out_shapes)

    def carried(*refs):
        cut = [prefetch, n_in, t_in, n_out, t_out, n_scr]
        parts, p = [], 0
        for c in cut:
            parts.append(refs[p:p + c])
            p += c
        scalars, ins, tins, outs, touts, scr = parts
        tsems = refs[p:]
        ids = [pl.program_id(a) for a in range(len(grid))]
        first = functools.reduce(jnp.logical_and, [i == 0 for i in ids])
        last = functools.reduce(jnp.logical_and, [i == g - 1 for i, g in zip(ids, grid)])

        @pl.when(first)
        def _():
            task.issue(tins, touts, tsems)
        body(*scalars, *ins, *outs, *scr)

        @pl.when(last)
        def _():
            task.drain(tins, touts, tsems)

    call = make(carried, list(in_specs) + [ANY] * t_in, o_specs + [ANY] * t_out,
                o_shapes + list(task.out_shapes), list(scratch_shapes) + list(task.sems),
                {prefetch + n_in + i: n_out + j for i, j in task.aliases.items()})

    def run(*args):
        res = call(*args, *task.arrays)
        outs = res[:n_out]
        return (outs[0] if single else tuple(outs)), list(res[n_out:])

    return run


def _dot(a, b):
    return jnp.dot(a, b, preferred_element_type=F32)


def _dot_nt(a, b):
    return lax.dot_general(a, b, (((1,), (1,)), ((), ())), preferred_element_type=F32)


def _dot_tn(a, b):
    return lax.dot_general(a, b, (((0,), (0,)), ((), ())), preferred_element_type=F32)


def _split3(x):
    hi = x.astype(BF16)
    r1 = x - hi.astype(F32)
    mid = r1.astype(BF16)
    lo = (r1 - mid.astype(F32)).astype(BF16)
    return hi, mid, lo


def _dot3(x, m01):
    hi, mid, lo = _split3(x)
    return _dot(hi, m01) + _dot(mid, m01) + _dot(lo, m01)


def _dot3_l(m01, x):
    hi, mid, lo = _split3(x)
    return _dot(m01, hi) + _dot(m01, mid) + _dot(m01, lo)


def _mm_nn(a, b, kind, *, name, out_dtype=BF16, tm=2048, tn=512, epi=None, task=None):
    M, K = a.shape
    if kind == "plain":
        N = b.shape[1]
        b_spec = pl.BlockSpec((K, tn), lambda m, n: (0, n))
    elif kind == "rows":
        N = b.shape[2]
        b_spec = pl.BlockSpec((N_CHIP, K // N_CHIP, tn), lambda m, n: (0, 0, n))
    else:
        nq = b.shape[2]
        N = N_CHIP * nq
        per = nq // tn
        b_spec = pl.BlockSpec((None, K, tn), lambda m, n: (n // per, 0, n % per))
    tm = min(tm, M)
    kq = K // N_CHIP

    def body(a_ref, b_ref, *o_refs):
        if kind == "rows":
            acc = _dot(a_ref[:, 0:kq], b_ref[0])
            for j in range(1, N_CHIP):
                acc += _dot(a_ref[:, j * kq:(j + 1) * kq], b_ref[j])
        else:
            acc = _dot(a_ref[...], b_ref[...])
        if epi == "relu2":
            r = jnp.maximum(acc, 0.0)
            o_refs[0][...] = (r * r).astype(BF16)
            o_refs[1][...] = r.astype(BF16)
        else:
            o_refs[0][...] = acc.astype(out_dtype)

    o_spec = pl.BlockSpec((tm, tn), lambda m, n: (m, n))
    if epi == "relu2":
        out_shape = (jax.ShapeDtypeStruct((M, N), BF16), jax.ShapeDtypeStruct((M, N), BF16))
        out_specs = (o_spec, o_spec)
    else:
        out_shape = jax.ShapeDtypeStruct((M, N), out_dtype)
        out_specs = o_spec
    a_spec = pl.BlockSpec((tm, K), lambda m, n: (m, 0))
    if task is None:
        b_deep = pl.BlockSpec(b_spec.block_shape, b_spec.index_map, pipeline_mode=pl.Buffered(3))
        o_specs = list(out_specs) if epi == "relu2" else [out_specs]

        def piped(a_hbm, b_hbm, *o_hbm):
            pltpu.emit_pipeline(body, grid=(M // tm, N // tn), in_specs=[a_spec, b_deep],
                                out_specs=o_specs)(a_hbm, b_hbm, *o_hbm)

        return pl.pallas_call(piped, in_specs=[ANY, ANY], out_specs=(ANY, ANY) if epi == "relu2" else ANY,
                              out_shape=out_shape, name=name, compiler_params=_cparams())(a, b)
    return _pallas(
        body, grid=(M // tm, N // tn), in_specs=[a_spec, b_spec],
        out_specs=out_specs, out_shape=out_shape, name=name, task=task,
    )(a, b)


def _mm_nt(a, b, kind, *, name, out_dtype=BF16, tm=2048, tn=512, mul2r=None, task=None, rows=None):
    M, K = a.shape
    if kind == "plain":
        first, N = rows if rows is not None else (0, b.shape[0])
        n0 = first // tn
        b_spec = pl.BlockSpec((tn, K), lambda m, n: (n0 + n, 0))
    elif kind == "rows":
        nq = b.shape[1]
        N = N_CHIP * nq
        tn = min(tn, nq)
        per = nq // tn
        b_spec = pl.BlockSpec((None, tn, K), lambda m, n: (n // per, n % per, 0))
    else:
        N = b.shape[1]
        b_spec = pl.BlockSpec((N_CHIP, tn, K // N_CHIP), lambda m, n: (0, n, 0))
    tm = min(tm, M)
    kq = K // N_CHIP

    def body(a_ref, b_ref, *rest):
        o_ref = rest[-1]
        if kind == "cols":
            acc = _dot_nt(a_ref[:, 0:kq], b_ref[0])
            for j in range(1, N_CHIP):
                acc += _dot_nt(a_ref[:, j * kq:(j + 1) * kq], b_ref[j])
        else:
            acc = _dot_nt(a_ref[...], b_ref[...])
        if mul2r is not None:
            acc = acc * (2.0 * rest[0][...].astype(F32))
        o_ref[...] = acc.astype(out_dtype)

    in_specs = [pl.BlockSpec((tm, K), lambda m, n: (m, 0)), b_spec]
    args = [a, b]
    if mul2r is not None:
        in_specs.append(pl.BlockSpec((tm, tn), lambda m, n: (m, n)))
        args.append(mul2r)
    return _pallas(
        body, grid=(M // tm, N // tn), in_specs=in_specs,
        out_specs=pl.BlockSpec((tm, tn), lambda m, n: (m, n)),
        out_shape=jax.ShapeDtypeStruct((M, N), out_dtype), name=name, task=task,
    )(*args)


def _mm_tn(a, b, *, name, out_dtype=BF16, tk=1024, tn=512, cols4=False, task=None):
    M, K1 = a.shape
    N = b.shape[1]
    tk = min(tk, K1)
    tn = min(tn, N)

    def body(a_ref, b_ref, o_ref):
        o_ref[...] = _dot_tn(a_ref[...], b_ref[...]).astype(out_dtype)

    if cols4:
        per = (N // N_CHIP) // tn
        out_shape = jax.ShapeDtypeStruct((N_CHIP, K1, N // N_CHIP), out_dtype)
        o_spec = pl.BlockSpec((None, tk, tn), lambda k, n: (n // per, k, n % per))
    else:
        out_shape = jax.ShapeDtypeStruct((K1, N), out_dtype)
        o_spec = pl.BlockSpec((tk, tn), lambda k, n: (k, n))
    return _pallas(
        body, grid=(K1 // tk, N // tn),
        in_specs=[pl.BlockSpec((M, tk), lambda k, n: (0, k)), pl.BlockSpec((M, tn), lambda k, n: (0, n))],
        out_specs=o_spec, out_shape=out_shape, name=name, task=task,
    )(a, b)


def _rms(x, g):
    r = lax.rsqrt(jnp.mean(x * x, axis=-1, keepdims=True) + EPS)
    return x * r * g


def _rms_bwd(x, g, dy):
    r = lax.rsqrt(jnp.mean(x * x, axis=-1, keepdims=True) + EPS)
    xh = x * r
    dg = jnp.sum(dy * xh, axis=0, keepdims=True)
    dxh = dy * g
    dx = r * (dxh - xh * jnp.mean(dxh * xh, axis=-1, keepdims=True))
    return dx, dg


def _row_spec(tm, n):
    return pl.BlockSpec((tm, n), lambda i: (i, 0))


def _vec_spec(n):
    return pl.BlockSpec((1, n), lambda i: (0, 0))


def _acc_spec(n):
    return pl.BlockSpec((8, n), lambda i: (0, 0))


def _acc_add(ref, row, i):
    @pl.when(i == 0)
    def _():
        ref[...] = jnp.zeros_like(ref)
    ref[0:1, :] += row


def _rms_fwd_call(x, g, *, name, tm=1024, task=None):
    M, n = x.shape
    tm = min(tm, M)

    def body(x_ref, g_ref, h_ref):
        h_ref[...] = _rms(x_ref[...], g_ref[...]).astype(BF16)

    return _pallas(
        body, grid=(M // tm,), in_specs=[_row_spec(tm, n), _vec_spec(n)], out_specs=_row_spec(tm, n),
        out_shape=jax.ShapeDtypeStruct((M, n), BF16), name=name, task=task,
    )(x, g)


def _post_pre_call(xres, z, g_post, g_pre, *, name, tm=1024):
    M, n = xres.shape

    def body(x_ref, z_ref, gp_ref, gn_ref, xo_ref, h_ref):
        xn = x_ref[...] + _rms(z_ref[...], gp_ref[...])
        xo_ref[...] = xn
        h_ref[...] = _rms(xn, gn_ref[...]).astype(BF16)

    return pl.pallas_call(
        body, grid=(M // tm,),
        in_specs=[_row_spec(tm, n), _row_spec(tm, n), _vec_spec(n), _vec_spec(n)],
        out_specs=(_row_spec(tm, n), _row_spec(tm, n)),
        out_shape=(jax.ShapeDtypeStruct((M, n), F32), jax.ShapeDtypeStruct((M, n), BF16)),
        name=name, compiler_params=_cparams(),
    )(xres, z, g_post, g_pre)


def _final_call(x2, y3, g_post, target, *, name, tm=512):
    M, n = x2.shape

    def body(x_ref, y_ref, g_ref, t_ref, loss_ref, dx_ref, dy_ref, dg_ref):
        i = pl.program_id(0)
        y = y_ref[...]
        g = g_ref[...]
        diff = x_ref[...] + _rms(y, g) - t_ref[...]
        part = 0.5 * jnp.sum(jnp.sum(diff * diff, axis=1, keepdims=True), axis=0, keepdims=True) / n

        @pl.when(i == 0)
        def _():
            loss_ref[...] = jnp.zeros_like(loss_ref)
        loss_ref[...] += jnp.broadcast_to(part, loss_ref.shape)
        dx = diff / n
        dx_ref[...] = dx
        dy, dg = _rms_bwd(y, g, dx)
        dy_ref[...] = dy.astype(BF16)
        _acc_add(dg_ref, dg, i)

    return pl.pallas_call(
        body, grid=(M // tm,),
        in_specs=[_row_spec(tm, n), _row_spec(tm, n), _vec_spec(n), _row_spec(tm, n)],
        out_specs=(pl.BlockSpec((8, 128), lambda i: (0, 0)), _row_spec(tm, n), _row_spec(tm, n), _acc_spec(n)),
        out_shape=(jax.ShapeDtypeStruct((8, 128), F32), jax.ShapeDtypeStruct((M, n), F32),
                   jax.ShapeDtypeStruct((M, n), BF16), jax.ShapeDtypeStruct((8, n), F32)),
        name=name, compiler_params=_cparams(),
    )(x2, y3, g_post, target)


def _bwd_mid_call(dx_in, x, dh, g_pre, y, g_post, *, name, tm=512):
    M, n = x.shape

    def body(dxi_ref, x_ref, dh_ref, gpre_ref, y_ref, gpost_ref, dx_ref, dy_ref, dgpre_ref, dgpost_ref):
        i = pl.program_id(0)
        d1, dg1 = _rms_bwd(x_ref[...], gpre_ref[...], dh_ref[...])
        dx = dxi_ref[...] + d1
        dx_ref[...] = dx
        dy, dg2 = _rms_bwd(y_ref[...], gpost_ref[...], dx)
        dy_ref[...] = dy.astype(BF16)
        _acc_add(dgpre_ref, dg1, i)
        _acc_add(dgpost_ref, dg2, i)

    return pl.pallas_call(
        body, grid=(M // tm,),
        in_specs=[_row_spec(tm, n), _row_spec(tm, n), _row_spec(tm, n), _vec_spec(n), _row_spec(tm, n), _vec_spec(n)],
        out_specs=(_row_spec(tm, n), _row_spec(tm, n), _acc_spec(n), _acc_spec(n)),
        out_shape=(jax.ShapeDtypeStruct((M, n), F32), jax.ShapeDtypeStruct((M, n), BF16),
                   jax.ShapeDtypeStruct((8, n), F32), jax.ShapeDtypeStruct((8, n), F32)),
        name=name, compiler_params=_cparams(),
    )(dx_in, x, dh, g_pre, y, g_post)


def _bwd_last_call(dx_in, x, dh, g_pre, *, name, tm=1024, task=None):
    M, n = x.shape

    def body(dxi_ref, x_ref, dh_ref, g_ref, dx_ref, dg_ref):
        i = pl.program_id(0)
        d1, dg1 = _rms_bwd(x_ref[...], g_ref[...], dh_ref[...])
        dx_ref[...] = dxi_ref[...] + d1
        _acc_add(dg_ref, dg1, i)

    return _pallas(
        body, grid=(M // tm,),
        in_specs=[_row_spec(tm, n), _row_spec(tm, n), _row_spec(tm, n), _vec_spec(n)],
        out_specs=(_row_spec(tm, n), _acc_spec(n)),
        out_shape=(jax.ShapeDtypeStruct((M, n), F32), jax.ShapeDtypeStruct((8, n), F32)),
        name=name, task=task,
    )(dx_in, x, dh, g_pre)


def _gain_grad_call(x, g, dy_a, dy_b, *, name):
    M, n = x.shape

    def body(x_ref, g_ref, a_ref, b_ref, dg_ref):
        _, dg = _rms_bwd(x_ref[...], g_ref[...], a_ref[...] + b_ref[...])
        dg_ref[...] = jnp.zeros_like(dg_ref)
        dg_ref[0:1, :] = dg

    return pl.pallas_call(
        body, grid=(1,),
        in_specs=[_row_spec(M, n), _vec_spec(n), _row_spec(M, n), _row_spec(M, n)],
        out_specs=_acc_spec(n), out_shape=jax.ShapeDtypeStruct((8, n), F32),
        name=name, compiler_params=_cparams(),
    )(x, g, dy_a, dy_b)


def _head_group_matrix():
    a = lax.broadcasted_iota(jnp.int32, (D_GRP, D_GRP), 0) // HEAD
    b = lax.broadcasted_iota(jnp.int32, (D_GRP, D_GRP), 1) // HEAD
    return jnp.where(a == b, 1.0, 0.0).astype(BF16)


def _mix_norm_fwd_call(yf, yc, gf, gc, *, name, tm=1024):
    M = yf.shape[0]

    def body(yf_ref, yc_ref, gf_ref, gc_ref, o_ref):
        o_ref[:, 0:D_GRP] = _rms(yf_ref[...], gf_ref[...]).astype(BF16)
        o_ref[:, D_GRP:D] = _rms(yc_ref[...], gc_ref[...]).astype(BF16)

    return pl.pallas_call(
        body, grid=(M // tm,),
        in_specs=[_row_spec(tm, D_GRP), _row_spec(tm, D_GRP), _vec_spec(D_GRP), _vec_spec(D_GRP)],
        out_specs=_row_spec(tm, D), out_shape=jax.ShapeDtypeStruct((M, D), BF16),
        name=name, compiler_params=_cparams(),
    )(yf, yc, gf, gc)


def _mix_norm_bwd_call(dyn, yf, yc, gf, gc, *, name, tm=2 * TQ):
    M = yf.shape[0]

    def body(dyn_ref, yf_ref, yc_ref, gf_ref, gc_ref, dof_ref, doc_ref, delta_ref, dgf_ref, dgc_ref):
        i = pl.program_id(0)
        yf_ = yf_ref[...]
        dof, dgf = _rms_bwd(yf_, gf_ref[...], dyn_ref[:, 0:D_GRP])
        doc, dgc = _rms_bwd(yc_ref[...], gc_ref[...], dyn_ref[:, D_GRP:D])
        dof_b = dof.astype(BF16)
        dof_ref[...] = dof_b
        doc_ref[...] = doc.astype(BF16)
        prod = dof_b.astype(F32) * yf_
        hi = prod.astype(BF16)
        lo = (prod - hi.astype(F32)).astype(BF16)
        grp = _head_group_matrix()
        delta = _dot(hi, grp) + _dot(lo, grp)
        for b in range(tm // TQ):
            delta_ref[b] = delta[b * TQ:(b + 1) * TQ, :].T
        _acc_add(dgf_ref, dgf, i)
        _acc_add(dgc_ref, dgc, i)

    return pl.pallas_call(
        body, grid=(M // tm,),
        in_specs=[_row_spec(tm, D), _row_spec(tm, D_GRP), _row_spec(tm, D_GRP), _vec_spec(D_GRP), _vec_spec(D_GRP)],
        out_specs=(_row_spec(tm, D_GRP), _row_spec(tm, D_GRP),
                   pl.BlockSpec((tm // TQ, D_GRP, TQ), lambda i: (i, 0, 0)), _acc_spec(D_GRP), _acc_spec(D_GRP)),
        out_shape=(jax.ShapeDtypeStruct((M, D_GRP), BF16), jax.ShapeDtypeStruct((M, D_GRP), BF16),
                   jax.ShapeDtypeStruct((M // TQ, D_GRP, TQ), F32), jax.ShapeDtypeStruct((8, D_GRP), F32),
                   jax.ShapeDtypeStruct((8, D_GRP), F32)),
        name=name, compiler_params=_cparams(),
    )(dyn, yf, yc, gf, gc)


def _tri(n, lower_incl):
    a = lax.broadcasted_iota(jnp.int32, (n, n), 0)
    b = lax.broadcasted_iota(jnp.int32, (n, n), 1)
    return jnp.where(a >= b, 1.0, 0.0).astype(BF16) if lower_incl else jnp.where(a <= b, 1.0, 0.0).astype(BF16)


def _fox_prep_call(fl_raw, b_pad, *, name):
    S = fl_raw.shape[0]
    nb = S // TQ

    def body(fl_ref, b_ref, crep_ref, ct_ref, carry_ref):
        i = pl.program_id(0)

        @pl.when(i == 0)
        def _():
            carry_ref[...] = jnp.zeros_like(carry_ref)
        logf = jax.nn.log_sigmoid(fl_ref[...] + b_ref[...])
        cb = _dot3_l(_tri(TQ, True), logf) + carry_ref[0:1, :]
        carry_ref[0:1, :] = cb[TQ - 1:TQ, :]
        a = lax.broadcasted_iota(jnp.int32, (128, D_GRP), 0)
        b = lax.broadcasted_iota(jnp.int32, (128, D_GRP), 1) // HEAD
        expand = jnp.where(a == b, 1.0, 0.0).astype(BF16)
        crep = _dot3(cb, expand)
        crep_ref[...] = crep
        ct_ref[...] = crep.T

    return pl.pallas_call(
        body, grid=(nb,),
        in_specs=[_row_spec(TQ, 128), _vec_spec(128)],
        out_specs=(_row_spec(TQ, D_GRP), pl.BlockSpec((None, D_GRP, TQ), lambda i: (i, 0, 0))),
        out_shape=(jax.ShapeDtypeStruct((S, D_GRP), F32), jax.ShapeDtypeStruct((nb, D_GRP, TQ), F32)),
        scratch_shapes=[pltpu.VMEM((8, 128), F32)],
        name=name, compiler_params=_cparams(),
    )(fl_raw, b_pad)


def _lane_masks():
    lane = lax.broadcasted_iota(jnp.int32, (1, 128), 1)
    return lane < HEAD, lane >= HEAD


def _fox_fwd_call(proj, c_rep, c_t, *, name, task=None):
    S = proj.shape[0]
    nq = S // TQ
    scale = HEAD ** -0.5

    def body(q_ref, k_ref, v_ref, c_ref, ct_ref, o_ref, lse_ref):
        i = pl.program_id(1)
        m_lo, m_hi = _lane_masks()
        masks = (m_lo, m_hi)
        q = q_ref[...] * scale
        qm = [jnp.where(mk, q, jnp.zeros_like(q)) for mk in masks]
        cq = c_ref[...]
        cqh = [cq[:, 0:1], cq[:, HEAD:HEAD + 1]]
        row = lax.broadcasted_iota(jnp.int32, (TQ, TQ), 0)
        col = lax.broadcasted_iota(jnp.int32, (TQ, TQ), 1)

        def scores(j):
            start = pl.multiple_of(j * TQ, TQ)
            k = k_ref[pl.ds(start, TQ), :]
            ct = ct_ref[j]
            return tuple(_dot_nt(qm[h], k) + (cqh[h] - ct[HEAD * h:HEAD * h + 1, :]) for h in range(2))

        def update(j, ss, state, masked):
            ms, ls, acc = state
            start = pl.multiple_of(j * TQ, TQ)
            v = v_ref[pl.ds(start, TQ), :]
            new_m, new_l, pv, alpha_l = [], [], [], []
            for h in range(2):
                s = ss[h]
                if masked:
                    s = jnp.where(row >= col, s, NEG)
                mn = jnp.maximum(ms[h], jnp.max(s, axis=1, keepdims=True))
                alpha = jnp.exp(ms[h] - mn)
                p = jnp.exp(s - mn)
                new_l.append(alpha * ls[h] + jnp.sum(p, axis=1, keepdims=True))
                new_m.append(mn)
                alpha_l.append(alpha)
                pv.append(_dot(p.astype(BF16), jnp.where(masks[h], v, jnp.zeros_like(v))))
            alpha_lane = jnp.where(m_lo, alpha_l[0], alpha_l[1])
            acc = acc * alpha_lane + pv[0] + pv[1]
            return (tuple(new_m), tuple(new_l), acc)

        def step(j, carry):
            ss, state = carry
            return (scores(j + 1), update(j, ss, state, False))

        init = ((jnp.full((TQ, 1), NEG, F32),) * 2, (jnp.zeros((TQ, 1), F32),) * 2, jnp.zeros((TQ, 128), F32))
        ss, state = lax.fori_loop(0, i, step, (scores(0), init))
        ms, ls, acc = update(i, ss, state, True)
        l_lane = jnp.where(m_lo, ls[0], ls[1])
        o_ref[...] = acc / l_lane
        lse_ref[...] = jnp.where(m_lo, ms[0] + jnp.log(ls[0]), ms[1] + jnp.log(ls[1])).T

    return _pallas(
        body, grid=(N_PAIR, nq),
        in_specs=[pl.BlockSpec((TQ, 128), lambda p, i: (i, p)),
                  pl.BlockSpec((S, 128), lambda p, i: (0, N_PAIR + p)),
                  pl.BlockSpec((S, 128), lambda p, i: (0, 2 * N_PAIR + p)),
                  pl.BlockSpec((TQ, 128), lambda p, i: (i, p)),
                  pl.BlockSpec((nq, 128, TQ), lambda p, i: (0, p, 0))],
        out_specs=(pl.BlockSpec((TQ, 128), lambda p, i: (i, p)), pl.BlockSpec((None, 128, TQ), lambda p, i: (i, p, 0))),
        out_shape=(jax.ShapeDtypeStruct((S, D_GRP), F32), jax.ShapeDtypeStruct((nq, D_GRP, TQ), F32)),
        name=name, task=task,
    )(proj, proj, proj, c_rep, c_t)


def _fox_bwd_call(proj, do, lse_t, delta_t, c_rep, c_t, *, name, task=None):
    S = proj.shape[0]
    nq = S // TQ
    scale = HEAD ** -0.5

    def body(q_ref, k_ref, v_ref, do_ref, lse_ref, dl_ref, ck_ref, ct_ref,
             dq_ref, dk_ref, dv_ref, dcq_ref, dck_ref, dqa_ref):
        j = pl.program_id(1)
        m_lo, m_hi = _lane_masks()
        masks = (m_lo, m_hi)

        @pl.when(j == 0)
        def _():
            dqa_ref[...] = jnp.zeros_like(dqa_ref)
            dcq_ref[...] = jnp.zeros_like(dcq_ref)
        k = k_ref[...]
        v = v_ref[...]
        km = [jnp.where(mk, k, jnp.zeros_like(k)) for mk in masks]
        ck = ck_ref[...]
        krow = lax.broadcasted_iota(jnp.int32, (TQ, TQ), 0)
        qcol = lax.broadcasted_iota(jnp.int32, (TQ, TQ), 1)

        def probs(i):
            start = pl.multiple_of(i * TQ, TQ)
            q = q_ref[pl.ds(start, TQ), :]
            do = do_ref[pl.ds(start, TQ), :]
            lse = lse_ref[i]
            cq = ct_ref[i]
            out = []
            for h in range(2):
                lo = HEAD * h
                qm = jnp.where(masks[h], q * scale, jnp.zeros_like(q))
                dom = jnp.where(masks[h], do, jnp.zeros_like(do))
                st = _dot_nt(k, qm) + (cq[lo:lo + 1, :] - ck[:, lo:lo + 1])
                out.append((jnp.exp(st - lse[lo:lo + 1, :]), _dot_nt(v, dom)))
            return tuple(out)

        def update(i, pd, carry, masked):
            dk, dv, dck = carry
            start = pl.multiple_of(i * TQ, TQ)
            q = q_ref[pl.ds(start, TQ), :]
            do = do_ref[pl.ds(start, TQ), :]
            dl = dl_ref[i]
            dq = jnp.zeros((TQ, 128), F32)
            new_dck = []
            for h in range(2):
                lo = HEAD * h
                qm = jnp.where(masks[h], q, jnp.zeros_like(q))
                dom = jnp.where(masks[h], do, jnp.zeros_like(do))
                pt, dpt = pd[h]
                if masked:
                    pt = jnp.where(qcol >= krow, pt, 0.0)
                dst = pt * (dpt - dl[lo:lo + 1, :])
                dcq_ref[i, h:h + 1, :] += jnp.sum(dst, axis=0, keepdims=True)
                new_dck.append(dck[h] + jnp.sum(dst, axis=1, keepdims=True))
                dsb = (dst * scale).astype(BF16)
                dv = dv + _dot(pt.astype(BF16), dom)
                dk = dk + _dot(dsb, qm)
                dq = dq + _dot_tn(dsb, km[h])
            dqa_ref[pl.ds(start, TQ), :] += dq
            return (dk, dv, tuple(new_dck))

        def step(i, carry):
            pd, sums = carry
            return (probs(jnp.minimum(i + 1, nq - 1)), update(i, pd, sums, False))

        init = (jnp.zeros((TQ, 128), F32), jnp.zeros((TQ, 128), F32), (jnp.zeros((TQ, 1), F32),) * 2)
        first = probs(j)
        second = probs(jnp.minimum(j + 1, nq - 1))
        _, (dk, dv, dck) = lax.fori_loop(j + 1, nq, step, (second, update(j, first, init, True)))
        dk_ref[...] = dk.astype(BF16)
        dv_ref[...] = dv.astype(BF16)
        dck_t = jnp.where(m_lo, dck[0], dck[1]).T
        row = lax.broadcasted_iota(jnp.int32, (8, TQ), 0)
        dck_ref[...] = -jnp.where(row == 0, dck_t[0:1, :], jnp.where(row == 1, dck_t[HEAD:HEAD + 1, :], 0.0))

        @pl.when(j == nq - 1)
        def _():
            dq_ref[...] = dqa_ref[...].astype(BF16)

    res = lambda p, j: (0, p)
    stat = pl.BlockSpec((nq, 128, TQ), lambda p, j: (0, p, 0))
    blk = pl.BlockSpec((TQ, 128), lambda p, j: (j, p))
    return _pallas(
        body, grid=(N_PAIR, nq), task=task,
        in_specs=[pl.BlockSpec((S, 128), res),
                  pl.BlockSpec((TQ, 128), lambda p, j: (j, N_PAIR + p)),
                  pl.BlockSpec((TQ, 128), lambda p, j: (j, 2 * N_PAIR + p)),
                  pl.BlockSpec((S, 128), res), stat, stat, blk, stat],
        out_specs=(pl.BlockSpec((S, 128), res), blk, blk,
                   pl.BlockSpec((None, nq, 8, TQ), lambda p, j: (p, 0, 0, 0)),
                   pl.BlockSpec((None, None, 8, TQ), lambda p, j: (p, j, 0, 0))),
        out_shape=(jax.ShapeDtypeStruct((S, D_GRP), BF16), jax.ShapeDtypeStruct((S, D_GRP), BF16),
                   jax.ShapeDtypeStruct((S, D_GRP), BF16), jax.ShapeDtypeStruct((N_PAIR, nq, 8, TQ), F32),
                   jax.ShapeDtypeStruct((N_PAIR, nq, 8, TQ), F32)),
        scratch_shapes=[pltpu.VMEM((S, 128), F32)],
        name=name,
    )(proj, proj, proj, do, lse_t, delta_t, c_rep, c_t)


def _fox_gate_bwd_call(dc_rows, fl_raw, b_pad, *, name):
    S = fl_raw.shape[0]
    nb = S // TQ

    def body(dc_ref, fl_ref, b_ref, dfl_ref, db_ref, carry_ref):
        i = pl.program_id(0)

        @pl.when(i == 0)
        def _():
            carry_ref[...] = jnp.zeros_like(carry_ref)
        rc = _dot3(dc_ref[...], _tri(TQ, True)) + carry_ref[:, 0:1]
        carry_ref[...] = jnp.broadcast_to(rc[:, 0:1], carry_ref.shape)
        fl = fl_ref[...] + b_ref[...]
        dfl = rc.T * jax.nn.sigmoid(-fl)
        dfl_ref[...] = dfl.astype(BF16)
        _acc_add(db_ref, jnp.sum(dfl, axis=0, keepdims=True), i)

    rev = lambda i: (nb - 1 - i, 0)
    return pl.pallas_call(
        body, grid=(nb,),
        in_specs=[pl.BlockSpec((128, TQ), lambda i: (0, nb - 1 - i)), pl.BlockSpec((TQ, 128), rev), _vec_spec(128)],
        out_specs=(pl.BlockSpec((TQ, 128), rev), _acc_spec(128)),
        out_shape=(jax.ShapeDtypeStruct((S, 128), BF16), jax.ShapeDtypeStruct((8, 128), F32)),
        scratch_shapes=[pltpu.VMEM((128, 128), F32)],
        name=name, compiler_params=_cparams(),
    )(dc_rows, fl_raw, b_pad)


def _chk_bias_call(g_rev, *, name):
    def body(g_ref, o_ref):
        x = jnp.broadcast_to(g_ref[...], (TQ, ROLL_W))
        rolled = pltpu.roll(x, ROLL_W - (TQ - 1), 1, stride=1, stride_axis=0)
        qc = lax.broadcasted_iota(jnp.int32, (TQ, WIN), 0) // CHUNK
        kc = lax.broadcasted_iota(jnp.int32, (TQ, WIN), 1) // CHUNK
        band = (kc >= qc) & (kc <= qc + LEFT)
        o_ref[...] = jnp.where(band, rolled[:, 0:WIN], NEG)

    return pl.pallas_call(
        body, grid=(8,),
        in_specs=[pl.BlockSpec((None, 1, ROLL_W), lambda h: (h, 0, 0))],
        out_specs=pl.BlockSpec((None, TQ, WIN), lambda h: (h, 0, 0)),
        out_shape=jax.ShapeDtypeStruct((8, TQ, WIN), F32), name=name, compiler_params=_cparams(),
    )(g_rev.reshape(8, 1, ROLL_W))


def _chk_scores(i, qm, kwin, bias, scale):
    s = _dot_nt(qm * scale, kwin) + bias
    kc = lax.broadcasted_iota(jnp.int32, (TQ, WIN), 1) // CHUNK
    return jnp.where(kc + i * (TQ // CHUNK) >= LEFT, s, NEG)


def _chk_fwd_call(proj, bias, *, name, task=None):
    S = proj.shape[0]
    nq = S // TQ
    scale = HEAD ** -0.5

    def body(q_ref, k_ref, v_ref, b_ref, o_ref, kp_ref, vp_ref):
        i = pl.program_id(1)

        @pl.when(i == 0)
        def _():
            kp_ref[0:PADK, :] = jnp.zeros((PADK, 128), BF16)
            vp_ref[0:PADK, :] = jnp.zeros((PADK, 128), BF16)
            kp_ref[PADK:PADK + S, :] = k_ref[...]
            vp_ref[PADK:PADK + S, :] = v_ref[...]
        masks = _lane_masks()
        q = q_ref[...]
        start = pl.multiple_of(i * TQ, TQ)
        kwin = kp_ref[pl.ds(start, WIN), :]
        vwin = vp_ref[pl.ds(start, WIN), :]
        ss = [_chk_scores(i, jnp.where(masks[h], q, jnp.zeros_like(q)), kwin, b_ref[h], scale) for h in range(2)]
        ps = []
        for s in ss:
            p = jnp.exp(s - jnp.max(s, axis=1, keepdims=True))
            ps.append((p / jnp.sum(p, axis=1, keepdims=True)).astype(BF16))
        o_ref[...] = (_dot(ps[0], jnp.where(masks[0], vwin, jnp.zeros_like(vwin)))
                      + _dot(ps[1], jnp.where(masks[1], vwin, jnp.zeros_like(vwin))))

    c0 = 3 * N_PAIR
    return _pallas(
        body, grid=(N_PAIR, nq), task=task,
        in_specs=[pl.BlockSpec((TQ, 128), lambda p, i: (i, c0 + p)),
                  pl.BlockSpec((S, 128), lambda p, i: (0, c0 + N_PAIR + p)),
                  pl.BlockSpec((S, 128), lambda p, i: (0, c0 + 2 * N_PAIR + p)),
                  pl.BlockSpec((2, TQ, WIN), lambda p, i: (p, 0, 0))],
        out_specs=pl.BlockSpec((TQ, 128), lambda p, i: (i, p)),
        out_shape=jax.ShapeDtypeStruct((S, D_GRP), F32),
        scratch_shapes=[pltpu.VMEM((S + PADK, 128), BF16), pltpu.VMEM((S + PADK, 128), BF16)],
        name=name,
    )(proj, proj, proj, bias)


def _chk_bwd_call(proj, do, bias, *, name, task=None):
    S = proj.shape[0]
    nq = S // TQ
    scale = HEAD ** -0.5

    def body(q_ref, k_ref, v_ref, do_ref, b_ref, dq_ref, dk_ref, dv_ref, dg_ref, kp_ref, vp_ref, dkp_ref, dvp_ref, db_ref):
        i = pl.program_id(1)

        @pl.when(i == 0)
        def _():
            kp_ref[0:PADK, :] = jnp.zeros((PADK, 128), BF16)
            vp_ref[0:PADK, :] = jnp.zeros((PADK, 128), BF16)
            kp_ref[PADK:PADK + S, :] = k_ref[...]
            vp_ref[PADK:PADK + S, :] = v_ref[...]
            dkp_ref[...] = jnp.zeros_like(dkp_ref)
            dvp_ref[...] = jnp.zeros_like(dvp_ref)
            db_ref[...] = jnp.zeros_like(db_ref)
        masks = _lane_masks()
        q = q_ref[...]
        dout = do_ref[...]
        start = pl.multiple_of(i * TQ, TQ)
        kwin = kp_ref[pl.ds(start, WIN), :]
        vwin = vp_ref[pl.ds(start, WIN), :]
        qm = [jnp.where(mk, q, jnp.zeros_like(q)) for mk in masks]
        dom = [jnp.where(mk, dout, jnp.zeros_like(dout)) for mk in masks]
        ss = [_chk_scores(i, qm[h], kwin, b_ref[h], scale) for h in range(2)]
        dps = [_dot_nt(dom[h], vwin) for h in range(2)]
        pbs, dsbs = [], []
        for h in range(2):
            p = jnp.exp(ss[h] - jnp.max(ss[h], axis=1, keepdims=True))
            p = p / jnp.sum(p, axis=1, keepdims=True)
            ds = p * (dps[h] - jnp.sum(p * dps[h], axis=1, keepdims=True))
            db_ref[h] += ds
            pbs.append(p.astype(BF16))
            dsbs.append((ds * scale).astype(BF16))
        dq_ref[...] = (_dot(dsbs[0], jnp.where(masks[0], kwin, jnp.zeros_like(kwin)))
                       + _dot(dsbs[1], jnp.where(masks[1], kwin, jnp.zeros_like(kwin)))).astype(BF16)
        dkp_ref[pl.ds(start, WIN), :] += _dot_tn(dsbs[0], qm[0]) + _dot_tn(dsbs[1], qm[1])
        dvp_ref[pl.ds(start, WIN), :] += _dot_tn(pbs[0], dom[0]) + _dot_tn(pbs[1], dom[1])

        @pl.when(i == nq - 1)
        def _():
            dk_ref[...] = dkp_ref[PADK:PADK + S, :].astype(BF16)
            dv_ref[...] = dvp_ref[PADK:PADK + S, :].astype(BF16)
            a = lax.broadcasted_iota(jnp.int32, (TQ, TQ), 0)
            b = lax.broadcasted_iota(jnp.int32, (TQ, TQ), 1)
            flip = jnp.where(a + b == TQ - 1, 1.0, 0.0).astype(BF16)
            e = lax.broadcasted_iota(jnp.int32, (1, ROLL_W), 1)
            dg_ref[...] = jnp.zeros_like(dg_ref)
            for h in range(2):
                rev = _dot3_l(flip, db_ref[h])
                wide = jnp.concatenate([rev, jnp.zeros((TQ, ROLL_W - WIN), F32)], axis=1)
                diag = pltpu.roll(wide, 0, 1, stride=1, stride_axis=0)
                dg = jnp.sum(diag, axis=0, keepdims=True)
                lo = jnp.sum(jnp.where(e <= 639, dg, 0.0), axis=1, keepdims=True)
                hi = jnp.sum(jnp.where(e >= 895, dg, 0.0), axis=1, keepdims=True)
                dg_ref[h:h + 1, :] = jnp.where(e == 639, lo, jnp.where(e == 895, hi, dg))

    c0 = 3 * N_PAIR
    res = lambda p, i: (0, p)
    return _pallas(
        body, grid=(N_PAIR, nq), task=task,
        in_specs=[pl.BlockSpec((TQ, 128), lambda p, i: (i, c0 + p)),
                  pl.BlockSpec((S, 128), lambda p, i: (0, c0 + N_PAIR + p)),
                  pl.BlockSpec((S, 128), lambda p, i: (0, c0 + 2 * N_PAIR + p)),
                  pl.BlockSpec((TQ, 128), lambda p, i: (i, p)),
                  pl.BlockSpec((2, TQ, WIN), lambda p, i: (p, 0, 0))],
        out_specs=(pl.BlockSpec((TQ, 128), lambda p, i: (i, p)), pl.BlockSpec((S, 128), res),
                   pl.BlockSpec((S, 128), res), pl.BlockSpec((None, 8, ROLL_W), lambda p, i: (p, 0, 0))),
        out_shape=(jax.ShapeDtypeStruct((S, D_GRP), BF16), jax.ShapeDtypeStruct((S, D_GRP), BF16),
                   jax.ShapeDtypeStruct((S, D_GRP), BF16), jax.ShapeDtypeStruct((N_PAIR, 8, ROLL_W), F32)),
        scratch_shapes=[pltpu.VMEM((S + PADK, 128), BF16), pltpu.VMEM((S + PADK, 128), BF16),
                        pltpu.VMEM((S + PADK, 128), F32), pltpu.VMEM((S + PADK, 128), F32),
                        pltpu.VMEM((2, TQ, WIN), F32)],
        name=name,
    )(proj, proj, proj, do, bias)


def _mem_fwd_call(q, k, v, *, name, tq=2048):
    S = q.shape[0]
    scale = MEM_HD ** -0.5

    def body(q_ref, k_ref, v_ref, o_ref):
        s = _dot_nt(q_ref[...] * scale, k_ref[...])
        p = jnp.exp(s - jnp.max(s, axis=1, keepdims=True))
        p = p / jnp.sum(p, axis=1, keepdims=True)
        o_ref[...] = _dot(p.astype(BF16), v_ref[...]).astype(BF16)

    return pl.pallas_call(
        body, grid=(MEM_HEADS, S // tq),
        in_specs=[pl.BlockSpec((tq, MEM_HD), lambda h, i: (i, h)),
                  pl.BlockSpec((N_MEM, MEM_HD), lambda h, i: (0, h)),
                  pl.BlockSpec((N_MEM, MEM_HD), lambda h, i: (0, h))],
        out_specs=pl.BlockSpec((tq, MEM_HD), lambda h, i: (i, h)),
        out_shape=jax.ShapeDtypeStruct((S, D), BF16), name=name, compiler_params=_cparams(),
    )(q, k, v)


def _mem_bwd_call(q, k, v, do, *, name, tq=2048):
    S = q.shape[0]
    n = S // tq
    scale = MEM_HD ** -0.5

    def body(q_ref, k_ref, v_ref, do_ref, dq_ref, dk_ref, dv_ref, dka_ref, dva_ref):
        i = pl.program_id(1)

        @pl.when(i == 0)
        def _():
            dka_ref[...] = jnp.zeros_like(dka_ref)
            dva_ref[...] = jnp.zeros_like(dva_ref)
        qb = q_ref[...]
        kb = k_ref[...]
        dob = do_ref[...]
        s = _dot_nt(qb * scale, kb)
        p = jnp.exp(s - jnp.max(s, axis=1, keepdims=True))
        p = p / jnp.sum(p, axis=1, keepdims=True)
        dp = _dot_nt(dob, v_ref[...])
        ds = p * (dp - jnp.sum(p * dp, axis=1, keepdims=True))
        dsb = (ds * scale).astype(BF16)
        dq_ref[...] = _dot(dsb, kb).astype(BF16)
        dka_ref[...] += _dot_tn(dsb, qb)
        dva_ref[...] += _dot_tn(p.astype(BF16), dob)

        @pl.when(i == n - 1)
        def _():
            dk_ref[...] = dka_ref[...].astype(BF16)
            dv_ref[...] = dva_ref[...].astype(BF16)

    kv = pl.BlockSpec((N_MEM, MEM_HD), lambda h, i: (0, h))
    qs = pl.BlockSpec((tq, MEM_HD), lambda h, i: (i, h))
    return pl.pallas_call(
        body, grid=(MEM_HEADS, n), in_specs=[qs, kv, kv, qs], out_specs=(qs, kv, kv),
        out_shape=(jax.ShapeDtypeStruct((S, D), BF16), jax.ShapeDtypeStruct((N_MEM, D), BF16),
                   jax.ShapeDtypeStruct((N_MEM, D), BF16)),
        scratch_shapes=[pltpu.VMEM((N_MEM, MEM_HD), F32), pltpu.VMEM((N_MEM, MEM_HD), F32)],
        name=name, compiler_params=_cparams(),
    )(q, k, v, do)


def _rel_table_to_g(rel):
    return jnp.concatenate([
        jnp.broadcast_to(rel[:, N_REL - 1:N_REL], (8, 640)),
        rel[:, 1:N_REL - 1][:, ::-1],
        jnp.broadcast_to(rel[:, 0:1], (8, 129)),
    ], axis=1)


def _g_to_rel_table(dg):
    return dg[:, 639:896][:, ::-1]


def _place():
    x, y, c = lax.axis_index("x"), lax.axis_index("y"), lax.axis_index("c")
    others = [(1 - x, y), (x, 1 - y), (1 - x, 1 - y)]
    return x, y, c, others


def _half(c, rows):
    hr = rows // 2
    return pl.ds(pl.multiple_of(c * hr, 16), hr)


def _dma_sems(*shape):
    return pltpu.SemaphoreType.DMA(shape)


def _cast_slabs_call(ws, chip_arr, *, name, tm=256, task=None):
    n = len(ws)
    cols = ws[0].shape[1]
    tiles = [w.shape[0] // tm for w in ws]
    steps = max(tiles)

    def body(chip_ref, *refs):
        i = pl.program_id(0)
        for k in range(n):
            def cast(k=k):
                refs[n + k][...] = refs[k][...].astype(BF16)
            if tiles[k] == steps:
                cast()
            else:
                pl.when(i < tiles[k])(cast)

    in_specs = [pl.BlockSpec((tm, cols), lambda i, chip, t=t: (jnp.minimum(i, t - 1), 0)) for t in tiles]
    out_specs = [pl.BlockSpec((None, tm, cols), lambda i, chip, t=t: (chip[0], jnp.minimum(i, t - 1), 0)) for t in tiles]
    out_shape = [jax.ShapeDtypeStruct((N_CHIP,) + w.shape, BF16) for w in ws]
    return _pallas(body, grid=(steps,), in_specs=in_specs, out_specs=out_specs, out_shape=out_shape, name=name,
                   task=task, prefetch=1)(chip_arr, *ws)


def _cast_slab_call(w, chip_arr, *, name, tm=256, pad_rows=0):
    rows, cols = w.shape
    if pad_rows:
        tm = rows
    tm = min(tm, rows)

    def body(chip_ref, w_ref, o_ref):
        o_ref[0:tm, :] = w_ref[...].astype(BF16)
        if pad_rows:
            o_ref[tm:tm + pad_rows, :] = jnp.zeros((pad_rows, cols), BF16)

    return pl.pallas_call(
        body,
        grid_spec=pltpu.PrefetchScalarGridSpec(
            num_scalar_prefetch=1, grid=(rows // tm,),
            in_specs=[pl.BlockSpec((tm, cols), lambda i, chip: (i, 0))],
            out_specs=pl.BlockSpec((None, tm + pad_rows, cols), lambda i, chip: (chip[0], i, 0))),
        out_shape=jax.ShapeDtypeStruct((N_CHIP, rows + pad_rows, cols), BF16), name=name,
        compiler_params=_cparams(),
    )(chip_arr, w)


def _ag_ici_task(gathered):
    n = len(gathered)

    def copies(ins, outs, sems):
        send_sems, recv_sems = sems
        x, y, c, others = _place()
        me = 2 * x + y
        for k in range(n):
            mine = _half(c, gathered[k].shape[1])
            for t, (ox, oy) in enumerate(others):
                yield [pltpu.make_async_remote_copy(
                    src_ref=ins[k].at[me, mine], dst_ref=outs[k].at[slab, mine],
                    send_sem=send_sems.at[k, t], recv_sem=recv_sems.at[k, t],
                    device_id=(ox, oy, c), device_id_type=MESH) for slab in (me, 2 * ox + oy)]

    def issue(ins, outs, sems):
        for outgoing, _ in copies(ins, outs, sems):
            outgoing.start()

    def drain(ins, outs, sems):
        for outgoing, incoming in copies(ins, outs, sems):
            incoming.wait_recv()
            outgoing.wait_send()

    return _Task(gathered, [jax.ShapeDtypeStruct(g.shape, g.dtype) for g in gathered],
                 [_dma_sems(n, 3), _dma_sems(n, 3)], issue, drain, aliases={k: k for k in range(n)})


def _ag_d2d_task(gathered):
    n = len(gathered)

    def copies(ins, outs, sems):
        send_sems, recv_sems = sems
        x, y, c, others = _place()
        for k in range(n):
            rows = gathered[k].shape[1]
            mine, theirs = _half(c, rows), _half(1 - c, rows)
            for t, (ox, oy) in enumerate(others):
                slab = 2 * ox + oy
                pair = [pltpu.make_async_remote_copy(
                    src_ref=ins[k].at[slab, half], dst_ref=outs[k].at[slab, half],
                    send_sem=send_sems.at[k, t], recv_sem=recv_sems.at[k, t],
                    device_id=(x, y, 1 - c), device_id_type=MESH) for half in (mine, theirs)]
                yield pair

    def issue(ins, outs, sems):
        for outgoing, _ in copies(ins, outs, sems):
            outgoing.start()

    def drain(ins, outs, sems):
        for outgoing, incoming in copies(ins, outs, sems):
            incoming.wait_recv()
            outgoing.wait_send()

    return _Task(gathered, [jax.ShapeDtypeStruct(g.shape, g.dtype) for g in gathered],
                 [_dma_sems(n, 3), _dma_sems(n, 3)], issue, drain, aliases={k: k for k in range(n)})


def _rs_pair_task(ds):
    n = len(ds)

    def copies(ins, outs, sems):
        send_sems, recv_sems = sems
        x, y, c, _ = _place()
        for k in range(n):
            yield pltpu.make_async_remote_copy(
                src_ref=ins[k].at[:, _half(1 - c, ds[k].shape[1])], dst_ref=outs[k],
                send_sem=send_sems.at[k], recv_sem=recv_sems.at[k],
                device_id=(x, y, 1 - c), device_id_type=MESH)

    def issue(ins, outs, sems):
        for cp in copies(ins, outs, sems):
            cp.start()

    def drain(ins, outs, sems):
        for cp in copies(ins, outs, sems):
            cp.wait()

    return _Task(ds, [jax.ShapeDtypeStruct((N_CHIP, d.shape[1] // 2, d.shape[2]), d.dtype) for d in ds],
                 [_dma_sems(n), _dma_sems(n)], issue, drain)


def _pair_add_call(d, r1, c_arr, *, name, tm=512):
    _, rows, cols = d.shape
    hr = rows // 2
    tm = tm if hr % tm == 0 else hr
    nb = hr // tm

    def body(c_ref, d_ref, r_ref, o_ref):
        o_ref[...] = (d_ref[...].astype(F32) + r_ref[...].astype(F32)).astype(BF16)

    return pl.pallas_call(
        body,
        grid_spec=pltpu.PrefetchScalarGridSpec(
            num_scalar_prefetch=1, grid=(N_CHIP, nb),
            in_specs=[pl.BlockSpec((None, tm, cols), lambda j, i, c: (j, c[0] * nb + i, 0)),
                      pl.BlockSpec((None, tm, cols), lambda j, i, c: (j, i, 0))],
            out_specs=pl.BlockSpec((None, tm, cols), lambda j, i, c: (j, i, 0))),
        out_shape=jax.ShapeDtypeStruct((N_CHIP, hr, cols), BF16), name=name, compiler_params=_cparams(),
    )(c_arr, d, r1)


def _rs_chip_task(ps):
    n = len(ps)

    def copies(ins, outs, sems):
        send_sems, recv_sems = sems
        x, y, c, others = _place()
        for k in range(n):
            for t, (ox, oy) in enumerate(others):
                yield pltpu.make_async_remote_copy(
                    src_ref=ins[k].at[2 * ox + oy], dst_ref=outs[k].at[t],
                    send_sem=send_sems.at[k, t], recv_sem=recv_sems.at[k, t],
                    device_id=(ox, oy, c), device_id_type=MESH)

    def issue(ins, outs, sems):
        for cp in copies(ins, outs, sems):
            cp.start()

    def drain(ins, outs, sems):
        for cp in copies(ins, outs, sems):
            cp.wait()

    return _Task(ps, [jax.ShapeDtypeStruct((3,) + p.shape[1:], p.dtype) for p in ps],
                 [_dma_sems(n, 3), _dma_sems(n, 3)], issue, drain)


def _chip_sum_call(p, r2, place_arr, *, name, tm=512):
    _, hr, cols = r2.shape
    tm = tm if hr % tm == 0 else hr
    nb = hr // tm

    def body(place_ref, p_ref, r_ref, o_ref):
        acc = p_ref[...].astype(F32)
        for j in range(3):
            acc = acc + r_ref[j].astype(F32)
        o_ref[...] = acc

    return _pallas(
        body, grid=(nb,), prefetch=1,
        in_specs=[pl.BlockSpec((None, tm, cols), lambda i, pc: (pc[0], i, 0)),
                  pl.BlockSpec((3, tm, cols), lambda i, pc: (0, i, 0))],
        out_specs=pl.BlockSpec((tm, cols), lambda i, pc: (pc[1] * nb + i, 0)),
        out_shape=jax.ShapeDtypeStruct((2 * hr, cols), F32), name=name,
    )(place_arr, p, r2)


def _chip_sums_call(ps, r2s, place_arr, *, name, steps=2, task=None):
    n = len(ps)

    def body(place_ref, *refs):
        ins, outs = refs[:2 * n], refs[2 * n:]
        for k in range(n):
            acc = ins[2 * k][...].astype(F32)
            for j in range(3):
                acc = acc + ins[2 * k + 1][j].astype(F32)
            outs[k][...] = acc

    in_specs, out_specs, out_shape, args = [], [], [], []
    for p, r2 in zip(ps, r2s):
        _, hr, cols = r2.shape
        tm = hr // steps
        in_specs += [pl.BlockSpec((None, tm, cols), lambda i, pc: (pc[0], i, 0)),
                     pl.BlockSpec((3, tm, cols), lambda i, pc: (0, i, 0))]
        out_specs.append(pl.BlockSpec((tm, cols), lambda i, pc: (pc[1] * steps + i, 0)))
        out_shape.append(jax.ShapeDtypeStruct((2 * hr, cols), F32))
        args += [p, r2]
    return _pallas(body, grid=(steps,), prefetch=1, in_specs=in_specs, out_specs=out_specs, out_shape=out_shape,
                   name=name, task=task)(place_arr, *args)


def _rs_gather_task(gs):
    n = len(gs)

    def copies(ins, outs, sems):
        send_sems, recv_sems = sems
        x, y, c, _ = _place()
        for k in range(n):
            rows = gs[k].shape[0]
            mine, theirs = _half(c, rows), _half(1 - c, rows)
            yield [pltpu.make_async_remote_copy(
                src_ref=ins[k].at[mine], dst_ref=outs[k].at[half],
                send_sem=send_sems.at[k], recv_sem=recv_sems.at[k],
                device_id=(x, y, 1 - c), device_id_type=MESH) for half in (mine, theirs)]

    def issue(ins, outs, sems):
        for outgoing, _ in copies(ins, outs, sems):
            outgoing.start()

    def drain(ins, outs, sems):
        for outgoing, incoming in copies(ins, outs, sems):
            incoming.wait_recv()
            outgoing.wait_send()

    return _Task(gs, [jax.ShapeDtypeStruct(g.shape, g.dtype) for g in gs],
                 [_dma_sems(n), _dma_sems(n)], issue, drain, aliases={k: k for k in range(n)})


def _adamw(w, g, m, v):
    m = ADAM_B1 * m + (1.0 - ADAM_B1) * g
    v = ADAM_B2 * v + (1.0 - ADAM_B2) * jnp.square(g)
    m_hat = m / (1.0 - ADAM_B1 ** ADAM_STEP)
    v_hat = v / (1.0 - ADAM_B2 ** ADAM_STEP)
    delta = -ADAM_LR * (m_hat / (jnp.sqrt(v_hat) + ADAM_EPS) + ADAM_WD * w)
    return delta, m, v


def _adamw_call(items, *, name, tm=256, task=None):
    n = len(items)
    cols = items[0][0].shape[1]
    tiles = [it[0].shape[0] // tm for it in items]
    steps = max(tiles)

    def body(*refs):
        i = pl.program_id(0)
        ins, outs = refs[:4 * n], refs[4 * n:]
        for k in range(n):
            def update(k=k):
                g = ins[4 * k + 1][...]
                res = _adamw(ins[4 * k][...], g, ins[4 * k + 2][...], ins[4 * k + 3][...])
                outs[4 * k][...] = g
                for j in range(3):
                    outs[4 * k + 1 + j][...] = res[j]
            if tiles[k] == steps:
                update()
            else:
                pl.when(i < tiles[k])(update)

    in_specs, out_specs, out_shape, args = [], [], [], []
    for it, t in zip(items, tiles):
        spec = pl.BlockSpec((tm, cols), lambda i, t=t: (jnp.minimum(i, t - 1), 0))
        in_specs += [spec] * 4
        out_specs += [spec] * 4
        out_shape += [jax.ShapeDtypeStruct(it[0].shape, F32)] * 4
        args += list(it)
    res = _pallas(body, grid=(steps,), in_specs=in_specs, out_specs=out_specs, out_shape=out_shape,
                  name=name, task=task)(*args)
    outs, extra = res if task is not None else (res, None)
    grouped = [tuple(outs[4 * k:4 * k + 4]) for k in range(n)]
    return (grouped, extra) if task is not None else grouped


def _adamw_cols_call(w, g_pad, m, v, *, name, tn=512):
    rows, cols = w.shape

    def body(w_ref, g_ref, m_ref, v_ref, go_ref, d_ref, mo_ref, vo_ref):
        g = g_ref[0:rows, :]
        d, mn, vn = _adamw(w_ref[...], g, m_ref[...], v_ref[...])
        go_ref[...] = g
        d_ref[...] = d
        mo_ref[...] = mn
        vo_ref[...] = vn

    spec = pl.BlockSpec((rows, tn), lambda j: (0, j))
    gspec = pl.BlockSpec((g_pad.shape[0], tn), lambda j: (0, j))
    return _pallas(body, grid=(cols // tn,), in_specs=[spec, gspec, spec, spec], out_specs=(spec,) * 4,
                   out_shape=(jax.ShapeDtypeStruct((rows, cols), F32),) * 4, name=name)(w, g_pad, m, v)


N_DEV = 8
SMALL_ROWS = 24
SMALL_LAYOUT = {
    "g_mix_pre": (0, 0, 1, D), "g_mix_post": (1, 0, 1, D), "g_mem_kv": (2, 0, 1, D), "g_mem_pre": (3, 0, 1, D),
    "g_mem_post": (4, 0, 1, D), "g_ff_pre": (5, 0, 1, D), "g_ff_post": (6, 0, 1, D),
    "g_fox_out": (7, 0, 1, D_GRP), "g_chk_out": (7, D_GRP, 1, D_GRP), "b_fgt": (8, 0, 1, 8),
    "rel_bias": (16, 0, 8, N_REL),
}
SMALL = list(SMALL_LAYOUT)


LOSS_ROW = 9


def _small_reduce_call(grads, loss_blk, task, *, name):
    n = len(SMALL)
    t_in, t_out = len(task.arrays), len(task.out_shapes)

    def body(*refs):
        g_refs, loss_ref, tins = refs[:n], refs[n], refs[n + 1:n + 1 + t_in]
        p = n + 1 + t_in
        total_ref, loss_out, touts = refs[p], refs[p + 1], refs[p + 2:p + 2 + t_out]
        p += 2 + t_out
        mine, slots, send_sems, recv_sems = refs[p:p + 4]
        tsems = refs[p + 4:]
        task.issue(tins, touts, tsems)
        x, y, c, _ = _place()
        me = 4 * x + 2 * y + c
        mine[...] = jnp.zeros_like(mine)
        for k, name_k in enumerate(SMALL):
            r, l, nr, nl = SMALL_LAYOUT[name_k]
            mine[r:r + nr, l:l + nl] = g_refs[k][0:nr, 0:nl]
        mine[LOSS_ROW:LOSS_ROW + 1, 0:128] = loss_ref[0:1, :]
        slots[me] = mine[...]
        peers = [(dx, dy, dc) for dx in (0, 1) for dy in (0, 1) for dc in (0, 1)][1:]
        cps = []
        for t, (dx, dy, dc) in enumerate(peers):
            px, py, pc = (x + dx) % 2, (y + dy) % 2, (c + dc) % 2
            cps.append(pltpu.make_async_remote_copy(
                src_ref=mine, dst_ref=slots.at[me], send_sem=send_sems.at[t], recv_sem=recv_sems.at[t],
                device_id=(px, py, pc), device_id_type=MESH))
            cps[-1].start()
        for t, (dx, dy, dc) in enumerate(peers):
            px, py, pc = (x + dx) % 2, (y + dy) % 2, (c + dc) % 2
            pltpu.make_async_remote_copy(
                src_ref=mine, dst_ref=slots.at[4 * px + 2 * py + pc], send_sem=send_sems.at[t],
                recv_sem=recv_sems.at[t], device_id=(px, py, pc), device_id_type=MESH).wait_recv()
        for cp in cps:
            cp.wait_send()
        total = slots[0]
        for j in range(1, N_DEV):
            total = total + slots[j]
        total_ref[...] = total
        loss_out[...] = jnp.broadcast_to(total[LOSS_ROW:LOSS_ROW + 1, 0:128], loss_out.shape)
        task.drain(tins, touts, tsems)

    vm = pl.BlockSpec(memory_space=pltpu.VMEM)
    out_shape = [jax.ShapeDtypeStruct((SMALL_ROWS, D), F32), jax.ShapeDtypeStruct((8, 128), F32)] + list(task.out_shapes)
    res = pl.pallas_call(
        body, in_specs=[vm] * (n + 1) + [ANY] * t_in, out_specs=[vm] * 2 + [ANY] * t_out,
        out_shape=out_shape,
        scratch_shapes=[pltpu.VMEM((SMALL_ROWS, D), F32), pltpu.VMEM((N_DEV, SMALL_ROWS, D), F32),
                        _dma_sems(N_DEV - 1), _dma_sems(N_DEV - 1)] + list(task.sems),
        input_output_aliases={n + 1 + i: 2 + j for i, j in task.aliases.items()},
        name=name,
    )(*[grads[k] for k in SMALL], loss_blk, *task.arrays)
    return res[0], res[1], list(res[2:])


def _small_adamw_call(total, ws, ms, vs, *, name):
    n = len(SMALL)

    def body(*refs):
        total_ref = refs[0]
        w_refs, m_refs, v_refs = (refs[1 + j * n:1 + (j + 1) * n] for j in range(3))
        outs = refs[1 + 3 * n:]
        for k, name_k in enumerate(SMALL):
            r, l, nr, nl = SMALL_LAYOUT[name_k]
            g = total_ref[r:r + nr, l:l + nl]
            d, mn, vn = _adamw(w_refs[k][...], g, m_refs[k][...], v_refs[k][...])
            for j, val in enumerate((g, d, mn, vn)):
                outs[4 * k + j][...] = val

    vm = pl.BlockSpec(memory_space=pltpu.VMEM)
    res = pl.pallas_call(
        body, in_specs=[vm] * (3 * n + 1), out_specs=[vm] * (4 * n),
        out_shape=[jax.ShapeDtypeStruct(ws[k].shape, F32) for k in SMALL for _ in range(4)], name=name,
    )(total, *[d[k] for d in (ws, ms, vs) for k in SMALL])
    return {k: tuple(res[4 * i:4 * i + 4]) for i, k in enumerate(SMALL)}


WEIGHTS = ["w_in", "b_fgt", "rel_bias", "g_fox_out", "g_chk_out", "w_out", "g_mix_pre", "g_mix_post", "g_mem_kv",
           "w_mq", "w_mk", "w_mv", "w_mo", "g_mem_pre", "g_mem_post", "w_ff1", "w_ff2", "g_ff_pre", "g_ff_post"]
BIG = ["w_in", "w_out", "w_mq", "w_mk", "w_mv", "w_mo", "w_ff1", "w_ff2"]


IN_SHARD = D_IN // N_CHIP
IN_PAD = 800


IN_PIECES = [(0, 0, 770), (800, 770, 766), (1566, 3072, 4), (1600, 3076, 4), (1604, 1536, 766), (2400, 2302, 770)]
PAD_ZEROS = [(800 * j + IN_SHARD, IN_PAD - IN_SHARD) for j in range(N_CHIP)]
ALL_ZEROS = [(D_IN, D_ALL - D_IN)]


def _reorder_rows_call(src, to_all, *, name, tn=512):
    rows, cols = src.shape
    zeros = ALL_ZEROS if to_all else PAD_ZEROS

    def body(s_ref, o_ref):
        for pad0, all0, cnt in IN_PIECES:
            s0, d0 = (pad0, all0) if to_all else (all0, pad0)
            o_ref[d0:d0 + cnt, :] = s_ref[s0:s0 + cnt, :]
        for z0, cnt in zeros:
            o_ref[z0:z0 + cnt, :] = jnp.zeros((cnt, tn), src.dtype)

    spec = pl.BlockSpec((rows, tn), lambda j: (0, j))
    return _pallas(body, grid=(cols // tn,), in_specs=[spec], out_specs=spec,
                   out_shape=jax.ShapeDtypeStruct((rows, cols), src.dtype), name=name)(src)


def kernel(x, mem, w_in, b_fgt, rel_bias, g_fox_out, g_chk_out, w_out, g_mix_pre, g_mix_post, g_mem_kv, w_mq, w_mk, w_mv, w_mo, g_mem_pre, g_mem_post, w_ff1, w_ff2, g_ff_pre, g_ff_post, loss_target, m_w_in, m_b_fgt, m_rel_bias, m_g_fox_out, m_g_chk_out, m_w_out, m_g_mix_pre, m_g_mix_post, m_g_mem_kv, m_w_mq, m_w_mk, m_w_mv, m_w_mo, m_g_mem_pre, m_g_mem_post, m_w_ff1, m_w_ff2, m_g_ff_pre, m_g_ff_post, v_w_in, v_b_fgt, v_rel_bias, v_g_fox_out, v_g_chk_out, v_w_out, v_g_mix_pre, v_g_mix_post, v_g_mem_kv, v_w_mq, v_w_mk, v_w_mv, v_w_mo, v_g_mem_pre, v_g_mem_post, v_w_ff1, v_w_ff2, v_g_ff_pre, v_g_ff_post):
    w = dict(w_in=w_in, b_fgt=b_fgt, rel_bias=rel_bias, g_fox_out=g_fox_out, g_chk_out=g_chk_out, w_out=w_out,
             g_mix_pre=g_mix_pre, g_mix_post=g_mix_post, g_mem_kv=g_mem_kv, w_mq=w_mq, w_mk=w_mk, w_mv=w_mv,
             w_mo=w_mo, g_mem_pre=g_mem_pre, g_mem_post=g_mem_post, w_ff1=w_ff1, w_ff2=w_ff2, g_ff_pre=g_ff_pre,
             g_ff_post=g_ff_post)
    m = dict(w_in=m_w_in, b_fgt=m_b_fgt, rel_bias=m_rel_bias, g_fox_out=m_g_fox_out, g_chk_out=m_g_chk_out,
             w_out=m_w_out, g_mix_pre=m_g_mix_pre, g_mix_post=m_g_mix_post, g_mem_kv=m_g_mem_kv, w_mq=m_w_mq,
             w_mk=m_w_mk, w_mv=m_w_mv, w_mo=m_w_mo, g_mem_pre=m_g_mem_pre, g_mem_post=m_g_mem_post,
             w_ff1=m_w_ff1, w_ff2=m_w_ff2, g_ff_pre=m_g_ff_pre, g_ff_post=m_g_ff_post)
    v = dict(w_in=v_w_in, b_fgt=v_b_fgt, rel_bias=v_rel_bias, g_fox_out=v_g_fox_out, g_chk_out=v_g_chk_out,
             w_out=v_w_out, g_mix_pre=v_g_mix_pre, g_mix_post=v_g_mix_post, g_mem_kv=v_g_mem_kv, w_mq=v_w_mq,
             w_mk=v_w_mk, w_mv=v_w_mv, w_mo=v_w_mo, g_mem_pre=v_g_mem_pre, g_mem_post=v_g_mem_post,
             w_ff1=v_w_ff1, w_ff2=v_w_ff2, g_ff_pre=v_g_ff_pre, g_ff_post=v_g_ff_post)

    def rows(d, k):
        return d[k][0] if k == "rel_bias" else d[k]

    xs, mems, target = x[0], mem[0], loss_target[0]
    S = xs.shape[0]
    sp = {k: rows(w, k) for k in SMALL}
    b_pad = jnp.pad(sp["b_fgt"], ((0, 0), (0, 120)))
    chip = 2 * lax.axis_index("x") + lax.axis_index("y")
    chip_arr = jnp.reshape(chip, (1,)).astype(jnp.int32)
    c_arr = jnp.reshape(lax.axis_index("c"), (1,)).astype(jnp.int32)
    place_arr = jnp.concatenate([chip_arr, c_arr])
    w_in_t, m_in_t, v_in_t = w["w_in"][0].T, m["w_in"][0].T, v["w_in"][0].T
    slab = {"w_in": _cast_slab_call(w_in_t, chip_arr, name="cast_w_in", pad_rows=IN_PAD - IN_SHARD)}

    def gather_ici(names):
        return _ag_ici_task([slab[k] for k in names])

    def pair_add(k, d, r1):
        return _pair_add_call(d, r1, c_arr, name="rs_pair_add_" + k)

    rest, (g_in,) = _cast_slabs_call([w[k][0] for k in BIG[1:]], chip_arr, name="cast_rest",
                                     task=gather_ici(["w_in"]))
    slab.update(zip(BIG[1:], rest))
    h1, (g_in,) = _rms_fwd_call(xs, sp["g_mix_pre"], name="rms_mix_pre", task=_ag_d2d_task([g_in]))
    w_all_t = _reorder_rows_call(g_in.reshape(N_CHIP * IN_PAD, D), True, name="w_in_rows")
    proj, (g_out, g_mq) = _mm_nt(h1, w_all_t, "plain", rows=(0, 3072), name="mm_proj",
                                 task=gather_ici(["w_out", "w_mq"]))
    fl_raw = _mm_nt(h1, w_all_t, "plain", rows=(3072, 128), name="mm_gate", out_dtype=F32, tn=128)
    c_rep, c_t = _fox_prep_call(fl_raw, b_pad, name="fox_prep")
    bias = _chk_bias_call(_rel_table_to_g(sp["rel_bias"]), name="chk_bias")
    mid = ["w_mk", "w_mv", "w_mo", "w_ff1"]
    (yf, lse), got = _fox_fwd_call(proj, c_rep, c_t, name="fox_fwd",
                                   task=_merge_tasks([gather_ici(mid), _ag_d2d_task([g_out, g_mq])]))
    g_mid, (g_out, g_mq) = got[:4], got[4:]
    yc, got = _chk_fwd_call(proj, bias, name="chk_fwd",
                            task=_merge_tasks([gather_ici(["w_ff2"]), _ag_d2d_task(g_mid)]))
    g_ff2, (g_mk, g_mv, g_mo, g_ff1) = got[0], got[1:]
    yn = _mix_norm_fwd_call(yf, yc, sp["g_fox_out"], sp["g_chk_out"], name="mix_norm_fwd")
    z, (g_ff2,) = _mm_nn(yn, g_out, "rows", name="mm_out", out_dtype=F32, task=_ag_d2d_task([g_ff2]))
    x1, h2 = _post_pre_call(xs, z, sp["g_mix_post"], sp["g_mem_pre"], name="post_mix")
    memn = _rms_fwd_call(mems, sp["g_mem_kv"], name="rms_mem_kv")
    q2 = _mm_nn(h2, g_mq, "rows", name="mm_mq")
    k2 = _mm_nn(memn, g_mk, "rows", name="mm_mk")
    v2 = _mm_nn(memn, g_mv, "rows", name="mm_mv")
    o2 = _mem_fwd_call(q2, k2, v2, name="mem_fwd")
    y2 = _mm_nn(o2, g_mo, "rows", name="mm_mo", out_dtype=F32)
    x2, h3 = _post_pre_call(x1, y2, sp["g_mem_post"], sp["g_ff_pre"], name="post_mem")
    act, relu = _mm_nn(h3, g_ff1, "cols", name="mm_ff1", epi="relu2")
    y3 = _mm_nn(act, g_ff2, "rows", name="mm_ff2", out_dtype=F32, tm=1024)
    loss_blk, dx3, dy3, dg_ff_post = _final_call(x2, y3, sp["g_ff_post"], target, name="final")

    d_ff2 = _mm_tn(act, dy3, name="mm_dff2", tk=512, tn=1024).reshape(N_CHIP, D_FF // N_CHIP, D)
    du, (r1,) = _mm_nt(dy3, g_ff2, "rows", name="mm_du", mul2r=relu, task=_rs_pair_task([d_ff2]))
    p_ff2 = pair_add("w_ff2", d_ff2, r1)
    d_ff1 = _mm_tn(h3, du, name="mm_dff1", cols4=True)
    dh3, (r1,) = _mm_nt(du, g_ff1, "cols", name="mm_dh3", out_dtype=F32, tm=1024, task=_rs_pair_task([d_ff1]))
    p_ff1 = pair_add("w_ff1", d_ff1, r1)
    dx2, dy2, dg_ff_pre, dg_mem_post = _bwd_mid_call(dx3, x2, dh3, sp["g_ff_pre"], y2, sp["g_mem_post"], name="bwd_ff")
    d_mo = _mm_tn(o2, dy2, name="mm_dmo").reshape(N_CHIP, D // N_CHIP, D)
    do2 = _mm_nt(dy2, g_mo, "rows", name="mm_do2")
    dq2, dk2, dv2 = _mem_bwd_call(q2, k2, v2, do2, name="mem_bwd")
    d_mq = _mm_tn(h2, dq2, name="mm_dmq").reshape(N_CHIP, D // N_CHIP, D)
    dh2 = _mm_nt(dq2, g_mq, "rows", name="mm_dh2", out_dtype=F32)
    d_mk = _mm_tn(memn, dk2, name="mm_dmk").reshape(N_CHIP, D // N_CHIP, D)
    d_mv = _mm_tn(memn, dv2, name="mm_dmv").reshape(N_CHIP, D // N_CHIP, D)
    dmn_k = _mm_nt(dk2, g_mk, "rows", name="mm_dmemk", out_dtype=F32)
    dmn_v = _mm_nt(dv2, g_mv, "rows", name="mm_dmemv", out_dtype=F32)
    dg_mem_kv = _gain_grad_call(mems, sp["g_mem_kv"], dmn_k, dmn_v, name="gain_mem_kv")
    dx1, dz, dg_mem_pre, dg_mix_post = _bwd_mid_call(dx2, x1, dh2, sp["g_mem_pre"], z, sp["g_mix_post"], name="bwd_mem")
    d_out = _mm_tn(yn, dz, name="mm_dout").reshape(N_CHIP, D // N_CHIP, D)
    late = ["w_mo", "w_mq", "w_mk", "w_mv", "w_out"]
    d_late = [d_mo, d_mq, d_mk, d_mv, d_out]
    dyn, r1_late = _mm_nt(dz, g_out, "rows", name="mm_dyn", out_dtype=F32, task=_rs_pair_task(d_late))
    p_late = [pair_add(k, d, r1) for k, d, r1 in zip(late, d_late, r1_late)]
    dof, doc, delta, dg_fox, dg_chk = _mix_norm_bwd_call(dyn, yf, yc, sp["g_fox_out"], sp["g_chk_out"], name="mix_norm_bwd")
    (dqf, dkf, dvf, dcq, dck), r2_ff = _fox_bwd_call(proj, dof, lse, delta, c_rep, c_t, name="fox_bwd",
                                                      task=_rs_chip_task([p_ff2, p_ff1]))
    (dqc, dkc, dvc, dgrev), r2_late = _chk_bwd_call(proj, doc, bias, name="chk_bwd", task=_rs_chip_task(p_late))
    first = ["w_ff2", "w_ff1"] + late
    dc8 = (dcq[:, :, 0:2, :] + dck[:, :, 0:2, :]).transpose(0, 2, 1, 3).reshape(8, S)
    dc_rows = jnp.concatenate([dc8, jnp.zeros((120, S), F32)], axis=0)
    dfl, db_fgt = _fox_gate_bwd_call(dc_rows, fl_raw, b_pad, name="fox_gate_bwd")
    dproj = jnp.concatenate([dqf, dkf, dvf, dqc, dkc, dvc, dfl], axis=1)
    d_all_t = _mm_tn(dproj, h1, name="mm_dwin", tk=640, tn=1024)
    d_in = _reorder_rows_call(d_all_t, False, name="d_in_rows").reshape(N_CHIP, IN_PAD, D)
    f_first, (r1,) = _chip_sums_call([p_ff2, p_ff1] + p_late, r2_ff + r2_late, place_arr, name="rs_chip_sums",
                                     task=_rs_pair_task([d_in]))
    p_in = pair_add("w_in", d_in, r1)
    dh1, got = _mm_nn(dproj, w_all_t, "plain", name="mm_dh1", out_dtype=F32, tm=1024,
                      task=_merge_tasks([_rs_chip_task([p_in]), _rs_gather_task(f_first)]))
    r2_in, grads = got[0], dict(zip(first, got[1:]))
    f_in = _chip_sum_call(p_in, r2_in, place_arr, name="rs_chip_sum_w_in")
    delta_w, new_m, new_v = {}, {}, {}

    def adamw_items(names):
        return [(w[k][0], grads[k], m[k][0], v[k][0]) for k in names]

    upd_late = _adamw_call(adamw_items(late), name="adamw_late", tm=64)
    upd_ff = _adamw_call(adamw_items(first[:2]), name="adamw_ff")
    for k, res in zip(late + first[:2], upd_late + upd_ff):
        grads[k], delta_w[k], new_m[k], new_v[k] = res
    grad_x, dg_mix_pre = _bwd_last_call(dx1, xs, dh1, sp["g_mix_pre"], name="bwd_mix")

    small_g = {"g_mix_pre": dg_mix_pre, "g_mix_post": dg_mix_post, "g_mem_kv": dg_mem_kv, "g_mem_pre": dg_mem_pre,
               "g_mem_post": dg_mem_post, "g_ff_pre": dg_ff_pre, "g_ff_post": dg_ff_post, "g_fox_out": dg_fox,
               "g_chk_out": dg_chk, "b_fgt": db_fgt,
               "rel_bias": _g_to_rel_table(dgrev[:, 0:2, :].reshape(8, ROLL_W))}
    small_sum, loss_out, (g_w_in,) = _small_reduce_call(small_g, loss_blk, _rs_gather_task([f_in]),
                                                       name="small_allreduce")
    small = _small_adamw_call(small_sum, sp, {k: rows(m, k) for k in SMALL}, {k: rows(v, k) for k in SMALL},
                              name="small_adamw")
    loss = loss_out[0, 0]
    res = _adamw_cols_call(w_in_t, g_w_in, m_in_t, v_in_t, name="adamw_w_in")
    grads["w_in"], delta_w["w_in"], new_m["w_in"], new_v["w_in"] = (a.T for a in res)
    for k in SMALL:
        vals = small[k]
        if k == "rel_bias":
            vals = tuple(a[None] for a in vals)
        grads[k], delta_w[k], new_m[k], new_v[k] = vals

    def out(d, k):
        return d[k][None] if k in BIG else d[k]

    return (loss, grad_x[None], *[out(grads, k) for k in WEIGHTS], *[out(delta_w, k) for k in WEIGHTS],
            *[out(new_m, k) for k in WEIGHTS], *[out(new_v, k) for k in WEIGHTS])
```

```python
import functools

import jax
import jax.numpy as jnp
from jax import lax
from jax.experimental import pallas as pl
from jax.experimental.pallas import tpu as pltpu

F32 = jnp.float32
BF16 = jnp.bfloat16

D = 1024
HEAD = 64
N_PAIR = 4
D_GRP = 512
CHUNK = 64
LEFT = 8
MAX_REL = 128
N_REL = 2 * MAX_REL + 1
N_MEM = 256
MEM_HEADS = 4
MEM_HD = 256
D_FF = 4096
D_IN = 3080
D_ALL = 3200
EPS = 1e-6
TQ = 256
WIN = (LEFT + TQ // CHUNK) * CHUNK
PADK = LEFT * CHUNK
ROLL_W = 1024
NEG = -1e30
N_CHIP = 4
VMEM_LIMIT = 48 * 1024 * 1024

ADAM_LR = 0.001
ADAM_B1 = 0.9
ADAM_B2 = 0.999
ADAM_EPS = 1e-08
ADAM_WD = 0.01
ADAM_STEP = 10

MESH = pl.DeviceIdType.MESH


def _cparams():
    return pltpu.CompilerParams(vmem_limit_bytes=VMEM_LIMIT)


ANY = pl.BlockSpec(memory_space=pl.ANY)


class _Task:
    def __init__(self, arrays, out_shapes, sems, issue, drain, aliases=None):
        self.arrays, self.out_shapes, self.sems = list(arrays), list(out_shapes), list(sems)
        self.issue, self.drain, self.aliases = issue, drain, dict(aliases or {})


def _merge_tasks(tasks):
    tasks = [t for t in tasks if t is not None]
    if len(tasks) == 1:
        return tasks[0]
    cuts, a, o, s = [], 0, 0, 0
    aliases = {}
    for t in tasks:
        cuts.append((a, o, s))
        aliases.update({a + i: o + j for i, j in t.aliases.items()})
        a, o, s = a + len(t.arrays), o + len(t.out_shapes), s + len(t.sems)

    def part(fn_name):
        def run(ins, outs, sems):
            for t, (a0, o0, s0) in zip(tasks, cuts):
                getattr(t, fn_name)(ins[a0:a0 + len(t.arrays)], outs[o0:o0 + len(t.out_shapes)],
                                    sems[s0:s0 + len(t.sems)])
        return run

    return _Task([x for t in tasks for x in t.arrays], [x for t in tasks for x in t.out_shapes],
                 [x for t in tasks for x in t.sems], part("issue"), part("drain"), aliases)


def _pallas(body, *, grid, in_specs, out_specs, out_shape, name, scratch_shapes=(), task=None, prefetch=0):
    def make(kernel, i_specs, o_specs, o_shape, scratch, aliases):
        if prefetch:
            spec = pltpu.PrefetchScalarGridSpec(num_scalar_prefetch=prefetch, grid=grid, in_specs=i_specs,
                                                out_specs=o_specs, scratch_shapes=scratch)
            return pl.pallas_call(kernel, grid_spec=spec, out_shape=o_shape, input_output_aliases=aliases,
                                  name=name, compiler_params=_cparams())
        return pl.pallas_call(kernel, grid=grid, in_specs=i_specs, out_specs=o_specs, out_shape=o_shape,
                              scratch_shapes=scratch, input_output_aliases=aliases, name=name,
                              compiler_params=_cparams())

    if task is None:
        return make(body, list(in_specs), out_specs, out_shape, list(scratch_shapes), {})
    single = not isinstance(out_shape, (tuple, list))
    o_shapes = [out_shape] if single else list(out_shape)
    o_specs = [out_specs] if single else list(out_specs)
    n_in, n_out, n_scr = len(in_specs), len(o_shapes), len(scratch_shapes)
    t_in, t_out = len(task.arrays), len(task.out_shapes)

    def carried(*refs):
        cut = [prefetch, n_in, t_in, n_out, t_out, n_scr]
        parts, p = [], 0
        for c in cut:
            parts.append(refs[p:p + c])
            p += c
        scalars, ins, tins, outs, touts, scr = parts
        tsems = refs[p:]
        ids = [pl.program_id(a) for a in range(len(grid))]
        first = functools.reduce(jnp.logical_and, [i == 0 for i in ids])
        last = functools.reduce(jnp.logical_and, [i == g - 1 for i, g in zip(ids, grid)])

        @pl.when(first)
        def _():
            task.issue(tins, touts, tsems)
        body(*scalars, *ins, *outs, *scr)

        @pl.when(last)
        def _():
            task.drain(tins, touts, tsems)

    call = make(carried, list(in_specs) + [ANY] * t_in, o_specs + [ANY] * t_out,
                o_shapes + list(task.out_shapes), list(scratch_shapes) + list(task.sems),
                {prefetch + n_in + i: n_out + j for i, j in task.aliases.items()})

    def run(*args):
        res = call(*args, *task.arrays)
        outs = res[:n_out]
        return (outs[0] if single else tuple(outs)), list(res[n_out:])

    return run


def _dot(a, b):
    return jnp.dot(a, b, preferred_element_type=F32)


def _dot_nt(a, b):
    return lax.dot_general(a, b, (((1,), (1,)), ((), ())), preferred_element_type=F32)


def _dot_tn(a, b):
    return lax.dot_general(a, b, (((0,), (0,)), ((), ())), preferred_element_type=F32)


def _split3(x):
    hi = x.astype(BF16)
    r1 = x - hi.astype(F32)
    mid = r1.astype(BF16)
    lo = (r1 - mid.astype(F32)).astype(BF16)
    return hi, mid, lo


def _dot3(x, m01):
    hi, mid, lo = _split3(x)
    return _dot(hi, m01) + _dot(mid, m01) + _dot(lo, m01)


def _dot3_l(m01, x):
    hi, mid, lo = _split3(x)
    return _dot(m01, hi) + _dot(m01, mid) + _dot(m01, lo)


def _mm_nn(a, b, kind, *, name, out_dtype=BF16, tm=2048, tn=512, epi=None, task=None):
    M, K = a.shape
    if kind == "plain":
        N = b.shape[1]
        b_spec = pl.BlockSpec((K, tn), lambda m, n: (0, n))
    elif kind == "rows":
        N = b.shape[2]
        b_spec = pl.BlockSpec((N_CHIP, K // N_CHIP, tn), lambda m, n: (0, 0, n))
    else:
        nq = b.shape[2]
        N = N_CHIP * nq
        per = nq // tn
        b_spec = pl.BlockSpec((None, K, tn), lambda m, n: (n // per, 0, n % per))
    tm = min(tm, M)
    kq = K // N_CHIP

    def body(a_ref, b_ref, *o_refs):
        if kind == "rows":
            acc = _dot(a_ref[:, 0:kq], b_ref[0])
            for j in range(1, N_CHIP):
                acc += _dot(a_ref[:, j * kq:(j + 1) * kq], b_ref[j])
        else:
            acc = _dot(a_ref[...], b_ref[...])
        if epi == "relu2":
            r = jnp.maximum(acc, 0.0)
            o_refs[0][...] = (r * r).astype(BF16)
            o_refs[1][...] = r.astype(BF16)
        else:
            o_refs[0][...] = acc.astype(out_dtype)

    o_spec = pl.BlockSpec((tm, tn), lambda m, n: (m, n))
    if epi == "relu2":
        out_shape = (jax.ShapeDtypeStruct((M, N), BF16), jax.ShapeDtypeStruct((M, N), BF16))
        out_specs = (o_spec, o_spec)
    else:
        out_shape = jax.ShapeDtypeStruct((M, N), out_dtype)
        out_specs = o_spec
    return _pallas(
        body, grid=(M // tm, N // tn),
        in_specs=[pl.BlockSpec((tm, K), lambda m, n: (m, 0)), b_spec],
        out_specs=out_specs, out_shape=out_shape, name=name, task=task,
    )(a, b)


def _mm_nt(a, b, kind, *, name, out_dtype=BF16, tm=2048, tn=512, mul2r=None, task=None, rows=None):
    M, K = a.shape
    if kind == "plain":
        first, N = rows if rows is not None else (0, b.shape[0])
        n0 = first // tn
        b_spec = pl.BlockSpec((tn, K), lambda m, n: (n0 + n, 0))
    elif kind == "rows":
        nq = b.shape[1]
        N = N_CHIP * nq
        tn = min(tn, nq)
        per = nq // tn
        b_spec = pl.BlockSpec((None, tn, K), lambda m, n: (n // per, n % per, 0))
    else:
        N = b.shape[1]
        b_spec = pl.BlockSpec((N_CHIP, tn, K // N_CHIP), lambda m, n: (0, n, 0))
    tm = min(tm, M)
    kq = K // N_CHIP

    def body(a_ref, b_ref, *rest):
        o_ref = rest[-1]
        if kind == "cols":
            acc = _dot_nt(a_ref[:, 0:kq], b_ref[0])
            for j in range(1, N_CHIP):
                acc += _dot_nt(a_ref[:, j * kq:(j + 1) * kq], b_ref[j])
        else:
            acc = _dot_nt(a_ref[...], b_ref[...])
        if mul2r is not None:
            acc = acc * (2.0 * rest[0][...].astype(F32))
        o_ref[...] = acc.astype(out_dtype)

    in_specs = [pl.BlockSpec((tm, K), lambda m, n: (m, 0)), b_spec]
    args = [a, b]
    if mul2r is not None:
        in_specs.append(pl.BlockSpec((tm, tn), lambda m, n: (m, n)))
        args.append(mul2r)
    return _pallas(
        body, grid=(M // tm, N // tn), in_specs=in_specs,
        out_specs=pl.BlockSpec((tm, tn), lambda m, n: (m, n)),
        out_shape=jax.ShapeDtypeStruct((M, N), out_dtype), name=name, task=task,
    )(*args)


def _mm_tn(a, b, *, name, out_dtype=BF16, tk=1024, tn=512, cols4=False, task=None):
    M, K1 = a.shape
    N = b.shape[1]
    tk = min(tk, K1)
    tn = min(tn, N)

    def body(a_ref, b_ref, o_ref):
        o_ref[...] = _dot_tn(a_ref[...], b_ref[...]).astype(out_dtype)

    if cols4:
        per = (N // N_CHIP) // tn
        out_shape = jax.ShapeDtypeStruct((N_CHIP, K1, N // N_CHIP), out_dtype)
        o_spec = pl.BlockSpec((None, tk, tn), lambda k, n: (n // per, k, n % per))
    else:
        out_shape = jax.ShapeDtypeStruct((K1, N), out_dtype)
        o_spec = pl.BlockSpec((tk, tn), lambda k, n: (k, n))
    return _pallas(
        body, grid=(K1 // tk, N // tn),
        in_specs=[pl.BlockSpec((M, tk), lambda k, n: (0, k)), pl.BlockSpec((M, tn), lambda k, n: (0, n))],
        out_specs=o_spec, out_shape=out_shape, name=name, task=task,
    )(a, b)


def _rms(x, g):
    r = lax.rsqrt(jnp.mean(x * x, axis=-1, keepdims=True) + EPS)
    return x * r * g


def _rms_bwd(x, g, dy):
    r = lax.rsqrt(jnp.mean(x * x, axis=-1, keepdims=True) + EPS)
    xh = x * r
    dg = jnp.sum(dy * xh, axis=0, keepdims=True)
    dxh = dy * g
    dx = r * (dxh - xh * jnp.mean(dxh * xh, axis=-1, keepdims=True))
    return dx, dg


def _row_spec(tm, n):
    return pl.BlockSpec((tm, n), lambda i: (i, 0))


def _vec_spec(n):
    return pl.BlockSpec((1, n), lambda i: (0, 0))


def _acc_spec(n):
    return pl.BlockSpec((8, n), lambda i: (0, 0))


def _acc_add(ref, row, i):
    @pl.when(i == 0)
    def _():
        ref[...] = jnp.zeros_like(ref)
    ref[0:1, :] += row


def _rms_fwd_call(x, g, *, name, tm=1024, task=None):
    M, n = x.shape
    tm = min(tm, M)

    def body(x_ref, g_ref, h_ref):
        h_ref[...] = _rms(x_ref[...], g_ref[...]).astype(BF16)

    return _pallas(
        body, grid=(M // tm,), in_specs=[_row_spec(tm, n), _vec_spec(n)], out_specs=_row_spec(tm, n),
        out_shape=jax.ShapeDtypeStruct((M, n), BF16), name=name, task=task,
    )(x, g)


def _post_pre_call(xres, z, g_post, g_pre, *, name, tm=1024):
    M, n = xres.shape

    def body(x_ref, z_ref, gp_ref, gn_ref, xo_ref, h_ref):
        xn = x_ref[...] + _rms(z_ref[...], gp_ref[...])
        xo_ref[...] = xn
        h_ref[...] = _rms(xn, gn_ref[...]).astype(BF16)

    return pl.pallas_call(
        body, grid=(M // tm,),
        in_specs=[_row_spec(tm, n), _row_spec(tm, n), _vec_spec(n), _vec_spec(n)],
        out_specs=(_row_spec(tm, n), _row_spec(tm, n)),
        out_shape=(jax.ShapeDtypeStruct((M, n), F32), jax.ShapeDtypeStruct((M, n), BF16)),
        name=name, compiler_params=_cparams(),
    )(xres, z, g_post, g_pre)


def _final_call(x2, y3, g_post, target, *, name, tm=512):
    M, n = x2.shape

    def body(x_ref, y_ref, g_ref, t_ref, loss_ref, dx_ref, dy_ref, dg_ref):
        i = pl.program_id(0)
        y = y_ref[...]
        g = g_ref[...]
        diff = x_ref[...] + _rms(y, g) - t_ref[...]
        part = 0.5 * jnp.sum(jnp.sum(diff * diff, axis=1, keepdims=True), axis=0, keepdims=True) / n

        @pl.when(i == 0)
        def _():
            loss_ref[...] = jnp.zeros_like(loss_ref)
        loss_ref[...] += jnp.broadcast_to(part, loss_ref.shape)
        dx = diff / n
        dx_ref[...] = dx
        dy, dg = _rms_bwd(y, g, dx)
        dy_ref[...] = dy.astype(BF16)
        _acc_add(dg_ref, dg, i)

    return pl.pallas_call(
        body, grid=(M // tm,),
        in_specs=[_row_spec(tm, n), _row_spec(tm, n), _vec_spec(n), _row_spec(tm, n)],
        out_specs=(pl.BlockSpec((8, 128), lambda i: (0, 0)), _row_spec(tm, n), _row_spec(tm, n), _acc_spec(n)),
        out_shape=(jax.ShapeDtypeStruct((8, 128), F32), jax.ShapeDtypeStruct((M, n), F32),
                   jax.ShapeDtypeStruct((M, n), BF16), jax.ShapeDtypeStruct((8, n), F32)),
        name=name, compiler_params=_cparams(),
    )(x2, y3, g_post, target)


def _bwd_mid_call(dx_in, x, dh, g_pre, y, g_post, *, name, tm=512):
    M, n = x.shape

    deep = pl.BlockSpec((tm, n), lambda i: (i, 0), pipeline_mode=pl.Buffered(3))

    def body(dxi_hbm, x_hbm, dh_hbm, gpre_ref, y_hbm, gpost_ref, dx_hbm, dy_hbm, dgpre_ref, dgpost_ref):
        dgpre_ref[...] = jnp.zeros_like(dgpre_ref)
        dgpost_ref[...] = jnp.zeros_like(dgpost_ref)

        def tile(dxi_ref, x_ref, dh_ref, y_ref, dx_ref, dy_ref):
            d1, dg1 = _rms_bwd(x_ref[...], gpre_ref[...], dh_ref[...])
            dx = dxi_ref[...] + d1
            dx_ref[...] = dx
            dy, dg2 = _rms_bwd(y_ref[...], gpost_ref[...], dx)
            dy_ref[...] = dy.astype(BF16)
            dgpre_ref[0:1, :] += dg1
            dgpost_ref[0:1, :] += dg2

        pltpu.emit_pipeline(tile, grid=(M // tm,), in_specs=[deep] * 4,
                            out_specs=[_row_spec(tm, n), _row_spec(tm, n)])(
            dxi_hbm, x_hbm, dh_hbm, y_hbm, dx_hbm, dy_hbm)

    vm = pl.BlockSpec(memory_space=pltpu.VMEM)
    return pl.pallas_call(
        body, in_specs=[ANY, ANY, ANY, vm, ANY, vm], out_specs=(ANY, ANY, vm, vm),
        out_shape=(jax.ShapeDtypeStruct((M, n), F32), jax.ShapeDtypeStruct((M, n), BF16),
                   jax.ShapeDtypeStruct((8, n), F32), jax.ShapeDtypeStruct((8, n), F32)),
        name=name, compiler_params=_cparams(),
    )(dx_in, x, dh, g_pre, y, g_post)


def _bwd_last_call(dx_in, x, dh, g_pre, *, name, tm=1024, task=None):
    M, n = x.shape

    def body(dxi_ref, x_ref, dh_ref, g_ref, dx_ref, dg_ref):
        i = pl.program_id(0)
        d1, dg1 = _rms_bwd(x_ref[...], g_ref[...], dh_ref[...])
        dx_ref[...] = dxi_ref[...] + d1
        _acc_add(dg_ref, dg1, i)

    return _pallas(
        body, grid=(M // tm,),
        in_specs=[_row_spec(tm, n), _row_spec(tm, n), _row_spec(tm, n), _vec_spec(n)],
        out_specs=(_row_spec(tm, n), _acc_spec(n)),
        out_shape=(jax.ShapeDtypeStruct((M, n), F32), jax.ShapeDtypeStruct((8, n), F32)),
        name=name, task=task,
    )(dx_in, x, dh, g_pre)


def _gain_grad_call(x, g, dy_a, dy_b, *, name):
    M, n = x.shape

    def body(x_ref, g_ref, a_ref, b_ref, dg_ref):
        _, dg = _rms_bwd(x_ref[...], g_ref[...], a_ref[...] + b_ref[...])
        dg_ref[...] = jnp.zeros_like(dg_ref)
        dg_ref[0:1, :] = dg

    return pl.pallas_call(
        body, grid=(1,),
        in_specs=[_row_spec(M, n), _vec_spec(n), _row_spec(M, n), _row_spec(M, n)],
        out_specs=_acc_spec(n), out_shape=jax.ShapeDtypeStruct((8, n), F32),
        name=name, compiler_params=_cparams(),
    )(x, g, dy_a, dy_b)


def _head_group_matrix():
    a = lax.broadcasted_iota(jnp.int32, (D_GRP, D_GRP), 0) // HEAD
    b = lax.broadcasted_iota(jnp.int32, (D_GRP, D_GRP), 1) // HEAD
    return jnp.where(a == b, 1.0, 0.0).astype(BF16)


def _mix_norm_fwd_call(yf, yc, gf, gc, *, name, tm=1024):
    M = yf.shape[0]

    def body(yf_ref, yc_ref, gf_ref, gc_ref, o_ref):
        o_ref[:, 0:D_GRP] = _rms(yf_ref[...], gf_ref[...]).astype(BF16)
        o_ref[:, D_GRP:D] = _rms(yc_ref[...], gc_ref[...]).astype(BF16)

    return pl.pallas_call(
        body, grid=(M // tm,),
        in_specs=[_row_spec(tm, D_GRP), _row_spec(tm, D_GRP), _vec_spec(D_GRP), _vec_spec(D_GRP)],
        out_specs=_row_spec(tm, D), out_shape=jax.ShapeDtypeStruct((M, D), BF16),
        name=name, compiler_params=_cparams(),
    )(yf, yc, gf, gc)


def _mix_norm_bwd_call(dyn, yf, yc, gf, gc, *, name, tm=2 * TQ):
    M = yf.shape[0]

    def body(dyn_ref, yf_ref, yc_ref, gf_ref, gc_ref, dof_ref, doc_ref, delta_ref, dgf_ref, dgc_ref):
        i = pl.program_id(0)
        yf_ = yf_ref[...]
        dof, dgf = _rms_bwd(yf_, gf_ref[...], dyn_ref[:, 0:D_GRP])
        doc, dgc = _rms_bwd(yc_ref[...], gc_ref[...], dyn_ref[:, D_GRP:D])
        dof_b = dof.astype(BF16)
        dof_ref[...] = dof_b
        doc_ref[...] = doc.astype(BF16)
        prod = dof_b.astype(F32) * yf_
        hi = prod.astype(BF16)
        lo = (prod - hi.astype(F32)).astype(BF16)
        grp = _head_group_matrix()
        delta = _dot(hi, grp) + _dot(lo, grp)
        for b in range(tm // TQ):
            delta_ref[b] = delta[b * TQ:(b + 1) * TQ, :].T
        _acc_add(dgf_ref, dgf, i)
        _acc_add(dgc_ref, dgc, i)

    return pl.pallas_call(
        body, grid=(M // tm,),
        in_specs=[_row_spec(tm, D), _row_spec(tm, D_GRP), _row_spec(tm, D_GRP), _vec_spec(D_GRP), _vec_spec(D_GRP)],
        out_specs=(_row_spec(tm, D_GRP), _row_spec(tm, D_GRP),
                   pl.BlockSpec((tm // TQ, D_GRP, TQ), lambda i: (i, 0, 0)), _acc_spec(D_GRP), _acc_spec(D_GRP)),
        out_shape=(jax.ShapeDtypeStruct((M, D_GRP), BF16), jax.ShapeDtypeStruct((M, D_GRP), BF16),
                   jax.ShapeDtypeStruct((M // TQ, D_GRP, TQ), F32), jax.ShapeDtypeStruct((8, D_GRP), F32),
                   jax.ShapeDtypeStruct((8, D_GRP), F32)),
        name=name, compiler_params=_cparams(),
    )(dyn, yf, yc, gf, gc)


def _tri(n, lower_incl):
    a = lax.broadcasted_iota(jnp.int32, (n, n), 0)
    b = lax.broadcasted_iota(jnp.int32, (n, n), 1)
    return jnp.where(a >= b, 1.0, 0.0).astype(BF16) if lower_incl else jnp.where(a <= b, 1.0, 0.0).astype(BF16)


def _fox_prep_call(fl_raw, b_pad, *, name):
    S = fl_raw.shape[0]
    nb = S // TQ

    def body(fl_ref, b_ref, crep_ref, ct_ref, carry_ref):
        i = pl.program_id(0)

        @pl.when(i == 0)
        def _():
            carry_ref[...] = jnp.zeros_like(carry_ref)
        logf = jax.nn.log_sigmoid(fl_ref[...] + b_ref[...])
        cb = _dot3_l(_tri(TQ, True), logf) + carry_ref[0:1, :]
        carry_ref[0:1, :] = cb[TQ - 1:TQ, :]
        a = lax.broadcasted_iota(jnp.int32, (128, D_GRP), 0)
        b = lax.broadcasted_iota(jnp.int32, (128, D_GRP), 1) // HEAD
        expand = jnp.where(a == b, 1.0, 0.0).astype(BF16)
        crep = _dot3(cb, expand)
        crep_ref[...] = crep
        ct_ref[...] = crep.T

    return pl.pallas_call(
        body, grid=(nb,),
        in_specs=[_row_spec(TQ, 128), _vec_spec(128)],
        out_specs=(_row_spec(TQ, D_GRP), pl.BlockSpec((None, D_GRP, TQ), lambda i: (i, 0, 0))),
        out_shape=(jax.ShapeDtypeStruct((S, D_GRP), F32), jax.ShapeDtypeStruct((nb, D_GRP, TQ), F32)),
        scratch_shapes=[pltpu.VMEM((8, 128), F32)],
        name=name, compiler_params=_cparams(),
    )(fl_raw, b_pad)


def _lane_masks():
    lane = lax.broadcasted_iota(jnp.int32, (1, 128), 1)
    return lane < HEAD, lane >= HEAD


def _fox_fwd_call(proj, c_rep, c_t, *, name, task=None):
    S = proj.shape[0]
    nq = S // TQ
    scale = HEAD ** -0.5

    def body(q_ref, k_ref, v_ref, c_ref, ct_ref, o_ref, lse_ref):
        i = pl.program_id(1)
        m_lo, m_hi = _lane_masks()
        masks = (m_lo, m_hi)
        q = q_ref[...] * scale
        qm = [jnp.where(mk, q, jnp.zeros_like(q)) for mk in masks]
        cq = c_ref[...]
        cqh = [cq[:, 0:1], cq[:, HEAD:HEAD + 1]]
        row = lax.broadcasted_iota(jnp.int32, (TQ, TQ), 0)
        col = lax.broadcasted_iota(jnp.int32, (TQ, TQ), 1)

        def scores(j):
            start = pl.multiple_of(j * TQ, TQ)
            k = k_ref[pl.ds(start, TQ), :]
            ct = ct_ref[j]
            return tuple(_dot_nt(qm[h], k) + (cqh[h] - ct[HEAD * h:HEAD * h + 1, :]) for h in range(2))

        def update(j, ss, state, masked):
            ms, ls, acc = state
            start = pl.multiple_of(j * TQ, TQ)
            v = v_ref[pl.ds(start, TQ), :]
            new_m, new_l, pv, alpha_l = [], [], [], []
            for h in range(2):
                s = ss[h]
                if masked:
                    s = jnp.where(row >= col, s, NEG)
                mn = jnp.maximum(ms[h], jnp.max(s, axis=1, keepdims=True))
                alpha = jnp.exp(ms[h] - mn)
                p = jnp.exp(s - mn)
                new_l.append(alpha * ls[h] + jnp.sum(p, axis=1, keepdims=True))
                new_m.append(mn)
                alpha_l.append(alpha)
                pv.append(_dot(p.astype(BF16), jnp.where(masks[h], v, jnp.zeros_like(v))))
            alpha_lane = jnp.where(m_lo, alpha_l[0], alpha_l[1])
            acc = acc * alpha_lane + pv[0] + pv[1]
            return (tuple(new_m), tuple(new_l), acc)

        def step(j, carry):
            ss, state = carry
            return (scores(j + 1), update(j, ss, state, False))

        init = ((jnp.full((TQ, 1), NEG, F32),) * 2, (jnp.zeros((TQ, 1), F32),) * 2, jnp.zeros((TQ, 128), F32))
        ss, state = lax.fori_loop(0, i, step, (scores(0), init))
        ms, ls, acc = update(i, ss, state, True)
        l_lane = jnp.where(m_lo, ls[0], ls[1])
        o_ref[...] = acc / l_lane
        lse_ref[...] = jnp.where(m_lo, ms[0] + jnp.log(ls[0]), ms[1] + jnp.log(ls[1])).T

    return _pallas(
        body, grid=(N_PAIR, nq),
        in_specs=[pl.BlockSpec((TQ, 128), lambda p, i: (i, p)),
                  pl.BlockSpec((S, 128), lambda p, i: (0, N_PAIR + p)),
                  pl.BlockSpec((S, 128), lambda p, i: (0, 2 * N_PAIR + p)),
                  pl.BlockSpec((TQ, 128), lambda p, i: (i, p)),
                  pl.BlockSpec((nq, 128, TQ), lambda p, i: (0, p, 0))],
        out_specs=(pl.BlockSpec((TQ, 128), lambda p, i: (i, p)), pl.BlockSpec((None, 128, TQ), lambda p, i: (i, p, 0))),
        out_shape=(jax.ShapeDtypeStruct((S, D_GRP), F32), jax.ShapeDtypeStruct((nq, D_GRP, TQ), F32)),
        name=name, task=task,
    )(proj, proj, proj, c_rep, c_t)


def _fox_bwd_call(proj, do, lse_t, delta_t, c_rep, c_t, *, name, task=None):
    S = proj.shape[0]
    nq = S // TQ
    scale = HEAD ** -0.5

    def body(q_ref, k_ref, v_ref, do_ref, lse_ref, dl_ref, ck_ref, ct_ref,
             dq_ref, dk_ref, dv_ref, dcq_ref, dck_ref, dqa_ref):
        j = pl.program_id(1)
        m_lo, m_hi = _lane_masks()
        masks = (m_lo, m_hi)

        @pl.when(j == 0)
        def _():
            dqa_ref[...] = jnp.zeros_like(dqa_ref)
            dcq_ref[...] = jnp.zeros_like(dcq_ref)
        k = k_ref[...]
        v = v_ref[...]
        km = [jnp.where(mk, k, jnp.zeros_like(k)) for mk in masks]
        ck = ck_ref[...]
        krow = lax.broadcasted_iota(jnp.int32, (TQ, TQ), 0)
        qcol = lax.broadcasted_iota(jnp.int32, (TQ, TQ), 1)

        def probs(i):
            start = pl.multiple_of(i * TQ, TQ)
            q = q_ref[pl.ds(start, TQ), :]
            do = do_ref[pl.ds(start, TQ), :]
            lse = lse_ref[i]
            cq = ct_ref[i]
            out = []
            for h in range(2):
                lo = HEAD * h
                qm = jnp.where(masks[h], q * scale, jnp.zeros_like(q))
                dom = jnp.where(masks[h], do, jnp.zeros_like(do))
                st = _dot_nt(k, qm) + (cq[lo:lo + 1, :] - ck[:, lo:lo + 1])
                out.append((jnp.exp(st - lse[lo:lo + 1, :]), _dot_nt(v, dom)))
            return tuple(out)

        def update(i, pd, carry, masked):
            dk, dv, dck = carry
            start = pl.multiple_of(i * TQ, TQ)
            q = q_ref[pl.ds(start, TQ), :]
            do = do_ref[pl.ds(start, TQ), :]
            dl = dl_ref[i]
            dq = jnp.zeros((TQ, 128), F32)
            new_dck = []
            for h in range(2):
                lo = HEAD * h
                qm = jnp.where(masks[h], q, jnp.zeros_like(q))
                dom = jnp.where(masks[h], do, jnp.zeros_like(do))
                pt, dpt = pd[h]
                if masked:
                    pt = jnp.where(qcol >= krow, pt, 0.0)
                dst = pt * (dpt - dl[lo:lo + 1, :])
                dcq_ref[i, h:h + 1, :] += jnp.sum(dst, axis=0, keepdims=True)
                new_dck.append(dck[h] + jnp.sum(dst, axis=1, keepdims=True))
                dsb = (dst * scale).astype(BF16)
                dv = dv + _dot(pt.astype(BF16), dom)
                dk = dk + _dot(dsb, qm)
                dq = dq + _dot_tn(dsb, km[h])
            dqa_ref[pl.ds(start, TQ), :] += dq
            return (dk, dv, tuple(new_dck))

        def step(i, carry):
            pd, sums = carry
            return (probs(jnp.minimum(i + 1, nq - 1)), update(i, pd, sums, False))

        init = (jnp.zeros((TQ, 128), F32), jnp.zeros((TQ, 128), F32), (jnp.zeros((TQ, 1), F32),) * 2)
        first = probs(j)
        second = probs(jnp.minimum(j + 1, nq - 1))
        _, (dk, dv, dck) = lax.fori_loop(j + 1, nq, step, (second, update(j, first, init, True)))
        dk_ref[...] = dk.astype(BF16)
        dv_ref[...] = dv.astype(BF16)
        dck_t = jnp.where(m_lo, dck[0], dck[1]).T
        row = lax.broadcasted_iota(jnp.int32, (8, TQ), 0)
        dck_ref[...] = -jnp.where(row == 0, dck_t[0:1, :], jnp.where(row == 1, dck_t[HEAD:HEAD + 1, :], 0.0))

        @pl.when(j == nq - 1)
        def _():
            dq_ref[...] = dqa_ref[...].astype(BF16)

    res = lambda p, j: (0, p)
    stat = pl.BlockSpec((nq, 128, TQ), lambda p, j: (0, p, 0))
    blk = pl.BlockSpec((TQ, 128), lambda p, j: (j, p))
    return _pallas(
        body, grid=(N_PAIR, nq), task=task,
        in_specs=[pl.BlockSpec((S, 128), res),
                  pl.BlockSpec((TQ, 128), lambda p, j: (j, N_PAIR + p)),
                  pl.BlockSpec((TQ, 128), lambda p, j: (j, 2 * N_PAIR + p)),
                  pl.BlockSpec((S, 128), res), stat, stat, blk, stat],
        out_specs=(pl.BlockSpec((S, 128), res), blk, blk,
                   pl.BlockSpec((None, nq, 8, TQ), lambda p, j: (p, 0, 0, 0)),
                   pl.BlockSpec((None, None, 8, TQ), lambda p, j: (p, j, 0, 0))),
        out_shape=(jax.ShapeDtypeStruct((S, D_GRP), BF16), jax.ShapeDtypeStruct((S, D_GRP), BF16),
                   jax.ShapeDtypeStruct((S, D_GRP), BF16), jax.ShapeDtypeStruct((N_PAIR, nq, 8, TQ), F32),
                   jax.ShapeDtypeStruct((N_PAIR, nq, 8, TQ), F32)),
        scratch_shapes=[pltpu.VMEM((S, 128), F32)],
        name=name,
    )(proj, proj, proj, do, lse_t, delta_t, c_rep, c_t)


def _fox_gate_bwd_call(dc_rows, fl_raw, b_pad, *, name):
    S = fl_raw.shape[0]
    nb = S // TQ

    def body(dc_ref, fl_ref, b_ref, dfl_ref, db_ref, carry_ref):
        i = pl.program_id(0)

        @pl.when(i == 0)
        def _():
            carry_ref[...] = jnp.zeros_like(carry_ref)
        rc = _dot3(dc_ref[...], _tri(TQ, True)) + carry_ref[:, 0:1]
        carry_ref[...] = jnp.broadcast_to(rc[:, 0:1], carry_ref.shape)
        fl = fl_ref[...] + b_ref[...]
        dfl = rc.T * jax.nn.sigmoid(-fl)
        dfl_ref[...] = dfl.astype(BF16)
        _acc_add(db_ref, jnp.sum(dfl, axis=0, keepdims=True), i)

    rev = lambda i: (nb - 1 - i, 0)
    return pl.pallas_call(
        body, grid=(nb,),
        in_specs=[pl.BlockSpec((128, TQ), lambda i: (0, nb - 1 - i)), pl.BlockSpec((TQ, 128), rev), _vec_spec(128)],
        out_specs=(pl.BlockSpec((TQ, 128), rev), _acc_spec(128)),
        out_shape=(jax.ShapeDtypeStruct((S, 128), BF16), jax.ShapeDtypeStruct((8, 128), F32)),
        scratch_shapes=[pltpu.VMEM((128, 128), F32)],
        name=name, compiler_params=_cparams(),
    )(dc_rows, fl_raw, b_pad)


def _chk_bias_call(g_rev, *, name):
    def body(g_ref, o_ref):
        x = jnp.broadcast_to(g_ref[...], (TQ, ROLL_W))
        rolled = pltpu.roll(x, ROLL_W - (TQ - 1), 1, stride=1, stride_axis=0)
        qc = lax.broadcasted_iota(jnp.int32, (TQ, WIN), 0) // CHUNK
        kc = lax.broadcasted_iota(jnp.int32, (TQ, WIN), 1) // CHUNK
        band = (kc >= qc) & (kc <= qc + LEFT)
        o_ref[...] = jnp.where(band, rolled[:, 0:WIN], NEG)

    return pl.pallas_call(
        body, grid=(8,),
        in_specs=[pl.BlockSpec((None, 1, ROLL_W), lambda h: (h, 0, 0))],
        out_specs=pl.BlockSpec((None, TQ, WIN), lambda h: (h, 0, 0)),
        out_shape=jax.ShapeDtypeStruct((8, TQ, WIN), F32), name=name, compiler_params=_cparams(),
    )(g_rev.reshape(8, 1, ROLL_W))


def _chk_scores(i, qm, kwin, bias, scale):
    s = _dot_nt(qm * scale, kwin) + bias
    kc = lax.broadcasted_iota(jnp.int32, (TQ, WIN), 1) // CHUNK
    return jnp.where(kc + i * (TQ // CHUNK) >= LEFT, s, NEG)


def _chk_fwd_call(proj, bias, *, name, task=None):
    S = proj.shape[0]
    nq = S // TQ
    scale = HEAD ** -0.5

    def body(q_ref, k_ref, v_ref, b_ref, o_ref, kp_ref, vp_ref):
        i = pl.program_id(1)

        @pl.when(i == 0)
        def _():
            kp_ref[0:PADK, :] = jnp.zeros((PADK, 128), BF16)
            vp_ref[0:PADK, :] = jnp.zeros((PADK, 128), BF16)
            kp_ref[PADK:PADK + S, :] = k_ref[...]
            vp_ref[PADK:PADK + S, :] = v_ref[...]
        masks = _lane_masks()
        q = q_ref[...]
        start = pl.multiple_of(i * TQ, TQ)
        kwin = kp_ref[pl.ds(start, WIN), :]
        vwin = vp_ref[pl.ds(start, WIN), :]
        ss = [_chk_scores(i, jnp.where(masks[h], q, jnp.zeros_like(q)), kwin, b_ref[h], scale) for h in range(2)]
        ps = []
        for s in ss:
            p = jnp.exp(s - jnp.max(s, axis=1, keepdims=True))
            ps.append((p / jnp.sum(p, axis=1, keepdims=True)).astype(BF16))
        o_ref[...] = (_dot(ps[0], jnp.where(masks[0], vwin, jnp.zeros_like(vwin)))
                      + _dot(ps[1], jnp.where(masks[1], vwin, jnp.zeros_like(vwin))))

    c0 = 3 * N_PAIR
    return _pallas(
        body, grid=(N_PAIR, nq), task=task,
        in_specs=[pl.BlockSpec((TQ, 128), lambda p, i: (i, c0 + p)),
                  pl.BlockSpec((S, 128), lambda p, i: (0, c0 + N_PAIR + p)),
                  pl.BlockSpec((S, 128), lambda p, i: (0, c0 + 2 * N_PAIR + p)),
                  pl.BlockSpec((2, TQ, WIN), lambda p, i: (p, 0, 0))],
        out_specs=pl.BlockSpec((TQ, 128), lambda p, i: (i, p)),
        out_shape=jax.ShapeDtypeStruct((S, D_GRP), F32),
        scratch_shapes=[pltpu.VMEM((S + PADK, 128), BF16), pltpu.VMEM((S + PADK, 128), BF16)],
        name=name,
    )(proj, proj, proj, bias)


def _chk_bwd_call(proj, do, bias, *, name, task=None):
    S = proj.shape[0]
    nq = S // TQ
    scale = HEAD ** -0.5

    def body(q_ref, k_ref, v_ref, do_ref, b_ref, dq_ref, dk_ref, dv_ref, dg_ref, kp_ref, vp_ref, dkp_ref, dvp_ref, db_ref):
        i = pl.program_id(1)

        @pl.when(i == 0)
        def _():
            kp_ref[0:PADK, :] = jnp.zeros((PADK, 128), BF16)
            vp_ref[0:PADK, :] = jnp.zeros((PADK, 128), BF16)
            kp_ref[PADK:PADK + S, :] = k_ref[...]
            vp_ref[PADK:PADK + S, :] = v_ref[...]
            dkp_ref[...] = jnp.zeros_like(dkp_ref)
            dvp_ref[...] = jnp.zeros_like(dvp_ref)
            db_ref[...] = jnp.zeros_like(db_ref)
        masks = _lane_masks()
        q = q_ref[...]
        dout = do_ref[...]
        start = pl.multiple_of(i * TQ, TQ)
        kwin = kp_ref[pl.ds(start, WIN), :]
        vwin = vp_ref[pl.ds(start, WIN), :]
        qm = [jnp.where(mk, q, jnp.zeros_like(q)) for mk in masks]
        dom = [jnp.where(mk, dout, jnp.zeros_like(dout)) for mk in masks]
        ss = [_chk_scores(i, qm[h], kwin, b_ref[h], scale) for h in range(2)]
        dps = [_dot_nt(dom[h], vwin) for h in range(2)]
        pbs, dsbs = [], []
        for h in range(2):
            p = jnp.exp(ss[h] - jnp.max(ss[h], axis=1, keepdims=True))
            p = p / jnp.sum(p, axis=1, keepdims=True)
            ds = p * (dps[h] - jnp.sum(p * dps[h], axis=1, keepdims=True))
            db_ref[h] += ds
            pbs.append(p.astype(BF16))
            dsbs.append((ds * scale).astype(BF16))
        dq_ref[...] = (_dot(dsbs[0], jnp.where(masks[0], kwin, jnp.zeros_like(kwin)))
                       + _dot(dsbs[1], jnp.where(masks[1], kwin, jnp.zeros_like(kwin)))).astype(BF16)
        dkp_ref[pl.ds(start, WIN), :] += _dot_tn(dsbs[0], qm[0]) + _dot_tn(dsbs[1], qm[1])
        dvp_ref[pl.ds(start, WIN), :] += _dot_tn(pbs[0], dom[0]) + _dot_tn(pbs[1], dom[1])

        @pl.when(i == nq - 1)
        def _():
            dk_ref[...] = dkp_ref[PADK:PADK + S, :].astype(BF16)
            dv_ref[...] = dvp_ref[PADK:PADK + S, :].astype(BF16)
            a = lax.broadcasted_iota(jnp.int32, (TQ, TQ), 0)
            b = lax.broadcasted_iota(jnp.int32, (TQ, TQ), 1)
            flip = jnp.where(a + b == TQ - 1, 1.0, 0.0).astype(BF16)
            e = lax.broadcasted_iota(jnp.int32, (1, ROLL_W), 1)
            dg_ref[...] = jnp.zeros_like(dg_ref)
            for h in range(2):
                rev = _dot3_l(flip, db_ref[h])
                wide = jnp.concatenate([rev, jnp.zeros((TQ, ROLL_W - WIN), F32)], axis=1)
                diag = pltpu.roll(wide, 0, 1, stride=1, stride_axis=0)
                dg = jnp.sum(diag, axis=0, keepdims=True)
                lo = jnp.sum(jnp.where(e <= 639, dg, 0.0), axis=1, keepdims=True)
                hi = jnp.sum(jnp.where(e >= 895, dg, 0.0), axis=1, keepdims=True)
                dg_ref[h:h + 1, :] = jnp.where(e == 639, lo, jnp.where(e == 895, hi, dg))

    c0 = 3 * N_PAIR
    res = lambda p, i: (0, p)
    return _pallas(
        body, grid=(N_PAIR, nq), task=task,
        in_specs=[pl.BlockSpec((TQ, 128), lambda p, i: (i, c0 + p)),
                  pl.BlockSpec((S, 128), lambda p, i: (0, c0 + N_PAIR + p)),
                  pl.BlockSpec((S, 128), lambda p, i: (0, c0 + 2 * N_PAIR + p)),
                  pl.BlockSpec((TQ, 128), lambda p, i: (i, p)),
                  pl.BlockSpec((2, TQ, WIN), lambda p, i: (p, 0, 0))],
        out_specs=(pl.BlockSpec((TQ, 128), lambda p, i: (i, p)), pl.BlockSpec((S, 128), res),
                   pl.BlockSpec((S, 128), res), pl.BlockSpec((None, 8, ROLL_W), lambda p, i: (p, 0, 0))),
        out_shape=(jax.ShapeDtypeStruct((S, D_GRP), BF16), jax.ShapeDtypeStruct((S, D_GRP), BF16),
                   jax.ShapeDtypeStruct((S, D_GRP), BF16), jax.ShapeDtypeStruct((N_PAIR, 8, ROLL_W), F32)),
        scratch_shapes=[pltpu.VMEM((S + PADK, 128), BF16), pltpu.VMEM((S + PADK, 128), BF16),
                        pltpu.VMEM((S + PADK, 128), F32), pltpu.VMEM((S + PADK, 128), F32),
                        pltpu.VMEM((2, TQ, WIN), F32)],
        name=name,
    )(proj, proj, proj, do, bias)


def _mem_fwd_call(q, k, v, *, name, tq=2048):
    S = q.shape[0]
    scale = MEM_HD ** -0.5

    def body(q_ref, k_ref, v_ref, o_ref):
        s = _dot_nt(q_ref[...] * scale, k_ref[...])
        p = jnp.exp(s - jnp.max(s, axis=1, keepdims=True))
        p = p / jnp.sum(p, axis=1, keepdims=True)
        o_ref[...] = _dot(p.astype(BF16), v_ref[...]).astype(BF16)

    return pl.pallas_call(
        body, grid=(MEM_HEADS, S // tq),
        in_specs=[pl.BlockSpec((tq, MEM_HD), lambda h, i: (i, h)),
                  pl.BlockSpec((N_MEM, MEM_HD), lambda h, i: (0, h)),
                  pl.BlockSpec((N_MEM, MEM_HD), lambda h, i: (0, h))],
        out_specs=pl.BlockSpec((tq, MEM_HD), lambda h, i: (i, h)),
        out_shape=jax.ShapeDtypeStruct((S, D), BF16), name=name, compiler_params=_cparams(),
    )(q, k, v)


def _mem_bwd_call(q, k, v, do, *, name, tq=2048):
    S = q.shape[0]
    n = S // tq
    scale = MEM_HD ** -0.5

    def body(q_ref, k_ref, v_ref, do_ref, dq_ref, dk_ref, dv_ref, dka_ref, dva_ref):
        i = pl.program_id(1)

        @pl.when(i == 0)
        def _():
            dka_ref[...] = jnp.zeros_like(dka_ref)
            dva_ref[...] = jnp.zeros_like(dva_ref)
        qb = q_ref[...]
        kb = k_ref[...]
        dob = do_ref[...]
        s = _dot_nt(qb * scale, kb)
        p = jnp.exp(s - jnp.max(s, axis=1, keepdims=True))
        p = p / jnp.sum(p, axis=1, keepdims=True)
        dp = _dot_nt(dob, v_ref[...])
        ds = p * (dp - jnp.sum(p * dp, axis=1, keepdims=True))
        dsb = (ds * scale).astype(BF16)
        dq_ref[...] = _dot(dsb, kb).astype(BF16)
        dka_ref[...] += _dot_tn(dsb, qb)
        dva_ref[...] += _dot_tn(p.astype(BF16), dob)

        @pl.when(i == n - 1)
        def _():
            dk_ref[...] = dka_ref[...].astype(BF16)
            dv_ref[...] = dva_ref[...].astype(BF16)

    kv = pl.BlockSpec((N_MEM, MEM_HD), lambda h, i: (0, h))
    qs = pl.BlockSpec((tq, MEM_HD), lambda h, i: (i, h))
    return pl.pallas_call(
        body, grid=(MEM_HEADS, n), in_specs=[qs, kv, kv, qs], out_specs=(qs, kv, kv),
        out_shape=(jax.ShapeDtypeStruct((S, D), BF16), jax.ShapeDtypeStruct((N_MEM, D), BF16),
                   jax.ShapeDtypeStruct((N_MEM, D), BF16)),
        scratch_shapes=[pltpu.VMEM((N_MEM, MEM_HD), F32), pltpu.VMEM((N_MEM, MEM_HD), F32)],
        name=name, compiler_params=_cparams(),
    )(q, k, v, do)


def _rel_table_to_g(rel):
    return jnp.concatenate([
        jnp.broadcast_to(rel[:, N_REL - 1:N_REL], (8, 640)),
        rel[:, 1:N_REL - 1][:, ::-1],
        jnp.broadcast_to(rel[:, 0:1], (8, 129)),
    ], axis=1)


def _g_to_rel_table(dg):
    return dg[:, 639:896][:, ::-1]


def _place():
    x, y, c = lax.axis_index("x"), lax.axis_index("y"), lax.axis_index("c")
    others = [(1 - x, y), (x, 1 - y), (1 - x, 1 - y)]
    return x, y, c, others


def _half(c, rows):
    hr = rows // 2
    return pl.ds(pl.multiple_of(c * hr, 16), hr)


def _dma_sems(*shape):
    return pltpu.SemaphoreType.DMA(shape)


def _cast_slabs_call(ws, chip_arr, *, name, tm=256, task=None):
    n = len(ws)
    cols = ws[0].shape[1]
    tiles = [w.shape[0] // tm for w in ws]
    steps = max(tiles)

    def body(chip_ref, *refs):
        i = pl.program_id(0)
        for k in range(n):
            def cast(k=k):
                refs[n + k][...] = refs[k][...].astype(BF16)
            if tiles[k] == steps:
                cast()
            else:
                pl.when(i < tiles[k])(cast)

    in_specs = [pl.BlockSpec((tm, cols), lambda i, chip, t=t: (jnp.minimum(i, t - 1), 0)) for t in tiles]
    out_specs = [pl.BlockSpec((None, tm, cols), lambda i, chip, t=t: (chip[0], jnp.minimum(i, t - 1), 0)) for t in tiles]
    out_shape = [jax.ShapeDtypeStruct((N_CHIP,) + w.shape, BF16) for w in ws]
    return _pallas(body, grid=(steps,), in_specs=in_specs, out_specs=out_specs, out_shape=out_shape, name=name,
                   task=task, prefetch=1)(chip_arr, *ws)


def _cast_slab_call(w, chip_arr, *, name, tm=256, pad_rows=0):
    rows, cols = w.shape
    if pad_rows:
        tm = rows
    tm = min(tm, rows)

    def body(chip_ref, w_ref, o_ref):
        o_ref[0:tm, :] = w_ref[...].astype(BF16)
        if pad_rows:
            o_ref[tm:tm + pad_rows, :] = jnp.zeros((pad_rows, cols), BF16)

    return pl.pallas_call(
        body,
        grid_spec=pltpu.PrefetchScalarGridSpec(
            num_scalar_prefetch=1, grid=(rows // tm,),
            in_specs=[pl.BlockSpec((tm, cols), lambda i, chip: (i, 0))],
            out_specs=pl.BlockSpec((None, tm + pad_rows, cols), lambda i, chip: (chip[0], i, 0))),
        out_shape=jax.ShapeDtypeStruct((N_CHIP, rows + pad_rows, cols), BF16), name=name,
        compiler_params=_cparams(),
    )(chip_arr, w)


def _ag_ici_task(gathered):
    n = len(gathered)

    def copies(ins, outs, sems):
        send_sems, recv_sems = sems
        x, y, c, others = _place()
        me = 2 * x + y
        for k in range(n):
            mine = _half(c, gathered[k].shape[1])
            for t, (ox, oy) in enumerate(others):
                yield [pltpu.make_async_remote_copy(
                    src_ref=ins[k].at[me, mine], dst_ref=outs[k].at[slab, mine],
                    send_sem=send_sems.at[k, t], recv_sem=recv_sems.at[k, t],
                    device_id=(ox, oy, c), device_id_type=MESH) for slab in (me, 2 * ox + oy)]

    def issue(ins, outs, sems):
        for outgoing, _ in copies(ins, outs, sems):
            outgoing.start()

    def drain(ins, outs, sems):
        for outgoing, incoming in copies(ins, outs, sems):
            incoming.wait_recv()
            outgoing.wait_send()

    return _Task(gathered, [jax.ShapeDtypeStruct(g.shape, g.dtype) for g in gathered],
                 [_dma_sems(n, 3), _dma_sems(n, 3)], issue, drain, aliases={k: k for k in range(n)})


def _ag_d2d_task(gathered):
    n = len(gathered)

    def copies(ins, outs, sems):
        send_sems, recv_sems = sems
        x, y, c, others = _place()
        for k in range(n):
            rows = gathered[k].shape[1]
            mine, theirs = _half(c, rows), _half(1 - c, rows)
            for t, (ox, oy) in enumerate(others):
                slab = 2 * ox + oy
                pair = [pltpu.make_async_remote_copy(
                    src_ref=ins[k].at[slab, half], dst_ref=outs[k].at[slab, half],
                    send_sem=send_sems.at[k, t], recv_sem=recv_sems.at[k, t],
                    device_id=(x, y, 1 - c), device_id_type=MESH) for half in (mine, theirs)]
                yield pair

    def issue(ins, outs, sems):
        for outgoing, _ in copies(ins, outs, sems):
            outgoing.start()

    def drain(ins, outs, sems):
        for outgoing, incoming in copies(ins, outs, sems):
            incoming.wait_recv()
            outgoing.wait_send()

    return _Task(gathered, [jax.ShapeDtypeStruct(g.shape, g.dtype) for g in gathered],
                 [_dma_sems(n, 3), _dma_sems(n, 3)], issue, drain, aliases={k: k for k in range(n)})


def _rs_pair_task(ds):
    n = len(ds)

    def copies(ins, outs, sems):
        send_sems, recv_sems = sems
        x, y, c, _ = _place()
        for k in range(n):
            yield pltpu.make_async_remote_copy(
                src_ref=ins[k].at[:, _half(1 - c, ds[k].shape[1])], dst_ref=outs[k],
                send_sem=send_sems.at[k], recv_sem=recv_sems.at[k],
                device_id=(x, y, 1 - c), device_id_type=MESH)

    def issue(ins, outs, sems):
        for cp in copies(ins, outs, sems):
            cp.start()

    def drain(ins, outs, sems):
        for cp in copies(ins, outs, sems):
            cp.wait()

    return _Task(ds, [jax.ShapeDtypeStruct((N_CHIP, d.shape[1] // 2, d.shape[2]), d.dtype) for d in ds],
                 [_dma_sems(n), _dma_sems(n)], issue, drain)


def _pair_add_call(d, r1, c_arr, *, name, tm=512):
    _, rows, cols = d.shape
    hr = rows // 2
    tm = tm if hr % tm == 0 else hr
    nb = hr // tm

    def body(c_ref, d_ref, r_ref, o_ref):
        o_ref[...] = (d_ref[...].astype(F32) + r_ref[...].astype(F32)).astype(BF16)

    return pl.pallas_call(
        body,
        grid_spec=pltpu.PrefetchScalarGridSpec(
            num_scalar_prefetch=1, grid=(N_CHIP, nb),
            in_specs=[pl.BlockSpec((None, tm, cols), lambda j, i, c: (j, c[0] * nb + i, 0)),
                      pl.BlockSpec((None, tm, cols), lambda j, i, c: (j, i, 0))],
            out_specs=pl.BlockSpec((None, tm, cols), lambda j, i, c: (j, i, 0))),
        out_shape=jax.ShapeDtypeStruct((N_CHIP, hr, cols), BF16), name=name, compiler_params=_cparams(),
    )(c_arr, d, r1)


def _rs_chip_task(ps):
    n = len(ps)

    def copies(ins, outs, sems):
        send_sems, recv_sems = sems
        x, y, c, others = _place()
        for k in range(n):
            for t, (ox, oy) in enumerate(others):
                yield pltpu.make_async_remote_copy(
                    src_ref=ins[k].at[2 * ox + oy], dst_ref=outs[k].at[t],
                    send_sem=send_sems.at[k, t], recv_sem=recv_sems.at[k, t],
                    device_id=(ox, oy, c), device_id_type=MESH)

    def issue(ins, outs, sems):
        for cp in copies(ins, outs, sems):
            cp.start()

    def drain(ins, outs, sems):
        for cp in copies(ins, outs, sems):
            cp.wait()

    return _Task(ps, [jax.ShapeDtypeStruct((3,) + p.shape[1:], p.dtype) for p in ps],
                 [_dma_sems(n, 3), _dma_sems(n, 3)], issue, drain)


def _chip_sum_call(p, r2, place_arr, *, name, tm=512):
    _, hr, cols = r2.shape
    tm = tm if hr % tm == 0 else hr
    nb = hr // tm

    def body(place_ref, p_ref, r_ref, o_ref):
        acc = p_ref[...].astype(F32)
        for j in range(3):
            acc = acc + r_ref[j].astype(F32)
        o_ref[...] = acc

    return _pallas(
        body, grid=(nb,), prefetch=1,
        in_specs=[pl.BlockSpec((None, tm, cols), lambda i, pc: (pc[0], i, 0)),
                  pl.BlockSpec((3, tm, cols), lambda i, pc: (0, i, 0))],
        out_specs=pl.BlockSpec((tm, cols), lambda i, pc: (pc[1] * nb + i, 0)),
        out_shape=jax.ShapeDtypeStruct((2 * hr, cols), F32), name=name,
    )(place_arr, p, r2)


def _chip_sums_call(ps, r2s, place_arr, *, name, steps=2, task=None):
    n = len(ps)

    def body(place_ref, *refs):
        ins, outs = refs[:2 * n], refs[2 * n:]
        for k in range(n):
            acc = ins[2 * k][...].astype(F32)
            for j in range(3):
                acc = acc + ins[2 * k + 1][j].astype(F32)
            outs[k][...] = acc

    in_specs, out_specs, out_shape, args = [], [], [], []
    for p, r2 in zip(ps, r2s):
        _, hr, cols = r2.shape
        tm = hr // steps
        in_specs += [pl.BlockSpec((None, tm, cols), lambda i, pc: (pc[0], i, 0)),
                     pl.BlockSpec((3, tm, cols), lambda i, pc: (0, i, 0))]
        out_specs.append(pl.BlockSpec((tm, cols), lambda i, pc: (pc[1] * steps + i, 0)))
        out_shape.append(jax.ShapeDtypeStruct((2 * hr, cols), F32))
        args += [p, r2]
    return _pallas(body, grid=(steps,), prefetch=1, in_specs=in_specs, out_specs=out_specs, out_shape=out_shape,
                   name=name, task=task)(place_arr, *args)


def _rs_gather_task(gs):
    n = len(gs)

    def copies(ins, outs, sems):
        send_sems, recv_sems = sems
        x, y, c, _ = _place()
        for k in range(n):
            rows = gs[k].shape[0]
            mine, theirs = _half(c, rows), _half(1 - c, rows)
            yield [pltpu.make_async_remote_copy(
                src_ref=ins[k].at[mine], dst_ref=outs[k].at[half],
                send_sem=send_sems.at[k], recv_sem=recv_sems.at[k],
                device_id=(x, y, 1 - c), device_id_type=MESH) for half in (mine, theirs)]

    def issue(ins, outs, sems):
        for outgoing, _ in copies(ins, outs, sems):
            outgoing.start()

    def drain(ins, outs, sems):
        for outgoing, incoming in copies(ins, outs, sems):
            incoming.wait_recv()
            outgoing.wait_send()

    return _Task(gs, [jax.ShapeDtypeStruct(g.shape, g.dtype) for g in gs],
                 [_dma_sems(n), _dma_sems(n)], issue, drain, aliases={k: k for k in range(n)})


def _adamw(w, g, m, v):
    m = ADAM_B1 * m + (1.0 - ADAM_B1) * g
    v = ADAM_B2 * v + (1.0 - ADAM_B2) * jnp.square(g)
    m_hat = m / (1.0 - ADAM_B1 ** ADAM_STEP)
    v_hat = v / (1.0 - ADAM_B2 ** ADAM_STEP)
    delta = -ADAM_LR * (m_hat / (jnp.sqrt(v_hat) + ADAM_EPS) + ADAM_WD * w)
    return delta, m, v


def _adamw_call(items, *, name, tm=256, task=None):
    n = len(items)
    cols = items[0][0].shape[1]
    tiles = [it[0].shape[0] // tm for it in items]
    steps = max(tiles)

    def body(*refs):
        i = pl.program_id(0)
        ins, outs = refs[:4 * n], refs[4 * n:]
        for k in range(n):
            def update(k=k):
                g = ins[4 * k + 1][...]
                res = _adamw(ins[4 * k][...], g, ins[4 * k + 2][...], ins[4 * k + 3][...])
                outs[4 * k][...] = g
                for j in range(3):
                    outs[4 * k + 1 + j][...] = res[j]
            if tiles[k] == steps:
                update()
            else:
                pl.when(i < tiles[k])(update)

    in_specs, out_specs, out_shape, args = [], [], [], []
    for it, t in zip(items, tiles):
        spec = pl.BlockSpec((tm, cols), lambda i, t=t: (jnp.minimum(i, t - 1), 0))
        in_specs += [spec] * 4
        out_specs += [spec] * 4
        out_shape += [jax.ShapeDtypeStruct(it[0].shape, F32)] * 4
        args += list(it)
    res = _pallas(body, grid=(steps,), in_specs=in_specs, out_specs=out_specs, out_shape=out_shape,
                  name=name, task=task)(*args)
    outs, extra = res if task is not None else (res, None)
    grouped = [tuple(outs[4 * k:4 * k + 4]) for k in range(n)]
    return (grouped, extra) if task is not None else grouped


def _adamw_cols_call(w, g_pad, m, v, *, name, tn=512):
    rows, cols = w.shape

    def body(w_ref, g_ref, m_ref, v_ref, go_ref, d_ref, mo_ref, vo_ref):
        g = g_ref[0:rows, :]
        d, mn, vn = _adamw(w_ref[...], g, m_ref[...], v_ref[...])
        go_ref[...] = g
        d_ref[...] = d
        mo_ref[...] = mn
        vo_ref[...] = vn

    spec = pl.BlockSpec((rows, tn), lambda j: (0, j))
    gspec = pl.BlockSpec((g_pad.shape[0], tn), lambda j: (0, j))
    return _pallas(body, grid=(cols // tn,), in_specs=[spec, gspec, spec, spec], out_specs=(spec,) * 4,
                   out_shape=(jax.ShapeDtypeStruct((rows, cols), F32),) * 4, name=name)(w, g_pad, m, v)


N_DEV = 8
SMALL_ROWS = 24
SMALL_LAYOUT = {
    "g_mix_pre": (0, 0, 1, D), "g_mix_post": (1, 0, 1, D), "g_mem_kv": (2, 0, 1, D), "g_mem_pre": (3, 0, 1, D),
    "g_mem_post": (4, 0, 1, D), "g_ff_pre": (5, 0, 1, D), "g_ff_post": (6, 0, 1, D),
    "g_fox_out": (7, 0, 1, D_GRP), "g_chk_out": (7, D_GRP, 1, D_GRP), "b_fgt": (8, 0, 1, 8),
    "rel_bias": (16, 0, 8, N_REL),
}
SMALL = list(SMALL_LAYOUT)


LOSS_ROW = 9


def _small_reduce_call(grads, loss_blk, task, *, name):
    n = len(SMALL)
    t_in, t_out = len(task.arrays), len(task.out_shapes)

    def body(*refs):
        g_refs, loss_ref, tins = refs[:n], refs[n], refs[n + 1:n + 1 + t_in]
        p = n + 1 + t_in
        total_ref, loss_out, touts = refs[p], refs[p + 1], refs[p + 2:p + 2 + t_out]
        p += 2 + t_out
        mine, slots, send_sems, recv_sems = refs[p:p + 4]
        tsems = refs[p + 4:]
        task.issue(tins, touts, tsems)
        x, y, c, _ = _place()
        me = 4 * x + 2 * y + c
        mine[...] = jnp.zeros_like(mine)
        for k, name_k in enumerate(SMALL):
            r, l, nr, nl = SMALL_LAYOUT[name_k]
            mine[r:r + nr, l:l + nl] = g_refs[k][0:nr, 0:nl]
        mine[LOSS_ROW:LOSS_ROW + 1, 0:128] = loss_ref[0:1, :]
        slots[me] = mine[...]
        peers = [(dx, dy, dc) for dx in (0, 1) for dy in (0, 1) for dc in (0, 1)][1:]
        cps = []
        for t, (dx, dy, dc) in enumerate(peers):
            px, py, pc = (x + dx) % 2, (y + dy) % 2, (c + dc) % 2
            cps.append(pltpu.make_async_remote_copy(
                src_ref=mine, dst_ref=slots.at[me], send_sem=send_sems.at[t], recv_sem=recv_sems.at[t],
                device_id=(px, py, pc), device_id_type=MESH))
            cps[-1].start()
        for t, (dx, dy, dc) in enumerate(peers):
            px, py, pc = (x + dx) % 2, (y + dy) % 2, (c + dc) % 2
            pltpu.make_async_remote_copy(
                src_ref=mine, dst_ref=slots.at[4 * px + 2 * py + pc], send_sem=send_sems.at[t],
                recv_sem=recv_sems.at[t], device_id=(px, py, pc), device_id_type=MESH).wait_recv()
        for cp in cps:
            cp.wait_send()
        total = slots[0]
        for j in range(1, N_DEV):
            total = total + slots[j]
        total_ref[...] = total
        loss_out[...] = jnp.broadcast_to(total[LOSS_ROW:LOSS_ROW + 1, 0:128], loss_out.shape)
        task.drain(tins, touts, tsems)

    vm = pl.BlockSpec(memory_space=pltpu.VMEM)
    out_shape = [jax.ShapeDtypeStruct((SMALL_ROWS, D), F32), jax.ShapeDtypeStruct((8, 128), F32)] + list(task.out_shapes)
    res = pl.pallas_call(
        body, in_specs=[vm] * (n + 1) + [ANY] * t_in, out_specs=[vm] * 2 + [ANY] * t_out,
        out_shape=out_shape,
        scratch_shapes=[pltpu.VMEM((SMALL_ROWS, D), F32), pltpu.VMEM((N_DEV, SMALL_ROWS, D), F32),
                        _dma_sems(N_DEV - 1), _dma_sems(N_DEV - 1)] + list(task.sems),
        input_output_aliases={n + 1 + i: 2 + j for i, j in task.aliases.items()},
        name=name,
    )(*[grads[k] for k in SMALL], loss_blk, *task.arrays)
    return res[0], res[1], list(res[2:])


def _small_adamw_call(total, ws, ms, vs, *, name):
    n = len(SMALL)

    def body(*refs):
        total_ref = refs[0]
        w_refs, m_refs, v_refs = (refs[1 + j * n:1 + (j + 1) * n] for j in range(3))
        outs = refs[1 + 3 * n:]
        for k, name_k in enumerate(SMALL):
            r, l, nr, nl = SMALL_LAYOUT[name_k]
            g = total_ref[r:r + nr, l:l + nl]
            d, mn, vn = _adamw(w_refs[k][...], g, m_refs[k][...], v_refs[k][...])
            for j, val in enumerate((g, d, mn, vn)):
                outs[4 * k + j][...] = val

    vm = pl.BlockSpec(memory_space=pltpu.VMEM)
    res = pl.pallas_call(
        body, in_specs=[vm] * (3 * n + 1), out_specs=[vm] * (4 * n),
        out_shape=[jax.ShapeDtypeStruct(ws[k].shape, F32) for k in SMALL for _ in range(4)], name=name,
    )(total, *[d[k] for d in (ws, ms, vs) for k in SMALL])
    return {k: tuple(res[4 * i:4 * i + 4]) for i, k in enumerate(SMALL)}


WEIGHTS = ["w_in", "b_fgt", "rel_bias", "g_fox_out", "g_chk_out", "w_out", "g_mix_pre", "g_mix_post", "g_mem_kv",
           "w_mq", "w_mk", "w_mv", "w_mo", "g_mem_pre", "g_mem_post", "w_ff1", "w_ff2", "g_ff_pre", "g_ff_post"]
BIG = ["w_in", "w_out", "w_mq", "w_mk", "w_mv", "w_mo", "w_ff1", "w_ff2"]


IN_SHARD = D_IN // N_CHIP
IN_PAD = 800


IN_PIECES = [(0, 0, 770), (800, 770, 766), (1566, 3072, 4), (1600, 3076, 4), (1604, 1536, 766), (2400, 2302, 770)]
PAD_ZEROS = [(800 * j + IN_SHARD, IN_PAD - IN_SHARD) for j in range(N_CHIP)]
ALL_ZEROS = [(D_IN, D_ALL - D_IN)]


def _reorder_rows_call(src, to_all, *, name, tn=512):
    rows, cols = src.shape
    zeros = ALL_ZEROS if to_all else PAD_ZEROS

    def body(s_ref, o_ref):
        for pad0, all0, cnt in IN_PIECES:
            s0, d0 = (pad0, all0) if to_all else (all0, pad0)
            o_ref[d0:d0 + cnt, :] = s_ref[s0:s0 + cnt, :]
        for z0, cnt in zeros:
            o_ref[z0:z0 + cnt, :] = jnp.zeros((cnt, tn), src.dtype)

    spec = pl.BlockSpec((rows, tn), lambda j: (0, j))
    return _pallas(body, grid=(cols // tn,), in_specs=[spec], out_specs=spec,
                   out_shape=jax.ShapeDtypeStruct((rows, cols), src.dtype), name=name)(src)


def kernel(x, mem, w_in, b_fgt, rel_bias, g_fox_out, g_chk_out, w_out, g_mix_pre, g_mix_post, g_mem_kv, w_mq, w_mk, w_mv, w_mo, g_mem_pre, g_mem_post, w_ff1, w_ff2, g_ff_pre, g_ff_post, loss_target, m_w_in, m_b_fgt, m_rel_bias, m_g_fox_out, m_g_chk_out, m_w_out, m_g_mix_pre, m_g_mix_post, m_g_mem_kv, m_w_mq, m_w_mk, m_w_mv, m_w_mo, m_g_mem_pre, m_g_mem_post, m_w_ff1, m_w_ff2, m_g_ff_pre, m_g_ff_post, v_w_in, v_b_fgt, v_rel_bias, v_g_fox_out, v_g_chk_out, v_w_out, v_g_mix_pre, v_g_mix_post, v_g_mem_kv, v_w_mq, v_w_mk, v_w_mv, v_w_mo, v_g_mem_pre, v_g_mem_post, v_w_ff1, v_w_ff2, v_g_ff_pre, v_g_ff_post):
    w = dict(w_in=w_in, b_fgt=b_fgt, rel_bias=rel_bias, g_fox_out=g_fox_out, g_chk_out=g_chk_out, w_out=w_out,
             g_mix_pre=g_mix_pre, g_mix_post=g_mix_post, g_mem_kv=g_mem_kv, w_mq=w_mq, w_mk=w_mk, w_mv=w_mv,
             w_mo=w_mo, g_mem_pre=g_mem_pre, g_mem_post=g_mem_post, w_ff1=w_ff1, w_ff2=w_ff2, g_ff_pre=g_ff_pre,
             g_ff_post=g_ff_post)
    m = dict(w_in=m_w_in, b_fgt=m_b_fgt, rel_bias=m_rel_bias, g_fox_out=m_g_fox_out, g_chk_out=m_g_chk_out,
             w_out=m_w_out, g_mix_pre=m_g_mix_pre, g_mix_post=m_g_mix_post, g_mem_kv=m_g_mem_kv, w_mq=m_w_mq,
             w_mk=m_w_mk, w_mv=m_w_mv, w_mo=m_w_mo, g_mem_pre=m_g_mem_pre, g_mem_post=m_g_mem_post,
             w_ff1=m_w_ff1, w_ff2=m_w_ff2, g_ff_pre=m_g_ff_pre, g_ff_post=m_g_ff_post)
    v = dict(w_in=v_w_in, b_fgt=v_b_fgt, rel_bias=v_rel_bias, g_fox_out=v_g_fox_out, g_chk_out=v_g_chk_out,
             w_out=v_w_out, g_mix_pre=v_g_mix_pre, g_mix_post=v_g_mix_post, g_mem_kv=v_g_mem_kv, w_mq=v_w_mq,
             w_mk=v_w_mk, w_mv=v_w_mv, w_mo=v_w_mo, g_mem_pre=v_g_mem_pre, g_mem_post=v_g_mem_post,
             w_ff1=v_w_ff1, w_ff2=v_w_ff2, g_ff_pre=v_g_ff_pre, g_ff_post=v_g_ff_post)

    def rows(d, k):
        return d[k][0] if k == "rel_bias" else d[k]

    xs, mems, target = x[0], mem[0], loss_target[0]
    S = xs.shape[0]
    sp = {k: rows(w, k) for k in SMALL}
    b_pad = jnp.pad(sp["b_fgt"], ((0, 0), (0, 120)))
    chip = 2 * lax.axis_index("x") + lax.axis_index("y")
    chip_arr = jnp.reshape(chip, (1,)).astype(jnp.int32)
    c_arr = jnp.reshape(lax.axis_index("c"), (1,)).astype(jnp.int32)
    place_arr = jnp.concatenate([chip_arr, c_arr])
    w_in_t, m_in_t, v_in_t = w["w_in"][0].T, m["w_in"][0].T, v["w_in"][0].T
    slab = {"w_in": _cast_slab_call(w_in_t, chip_arr, name="cast_w_in", pad_rows=IN_PAD - IN_SHARD)}

    def gather_ici(names):
        return _ag_ici_task([slab[k] for k in names])

    def pair_add(k, d, r1):
        return _pair_add_call(d, r1, c_arr, name="rs_pair_add_" + k)

    rest, (g_in,) = _cast_slabs_call([w[k][0] for k in BIG[1:]], chip_arr, name="cast_rest",
                                     task=gather_ici(["w_in"]))
    slab.update(zip(BIG[1:], rest))
    h1, (g_in,) = _rms_fwd_call(xs, sp["g_mix_pre"], name="rms_mix_pre", task=_ag_d2d_task([g_in]))
    w_all_t = _reorder_rows_call(g_in.reshape(N_CHIP * IN_PAD, D), True, name="w_in_rows")
    proj, (g_out, g_mq) = _mm_nt(h1, w_all_t, "plain", rows=(0, 3072), name="mm_proj",
                                 task=gather_ici(["w_out", "w_mq"]))
    fl_raw = _mm_nt(h1, w_all_t, "plain", rows=(3072, 128), name="mm_gate", out_dtype=F32, tn=128)
    c_rep, c_t = _fox_prep_call(fl_raw, b_pad, name="fox_prep")
    bias = _chk_bias_call(_rel_table_to_g(sp["rel_bias"]), name="chk_bias")
    mid = ["w_mk", "w_mv", "w_mo", "w_ff1"]
    (yf, lse), got = _fox_fwd_call(proj, c_rep, c_t, name="fox_fwd",
                                   task=_merge_tasks([gather_ici(mid), _ag_d2d_task([g_out, g_mq])]))
    g_mid, (g_out, g_mq) = got[:4], got[4:]
    yc, got = _chk_fwd_call(proj, bias, name="chk_fwd",
                            task=_merge_tasks([gather_ici(["w_ff2"]), _ag_d2d_task(g_mid)]))
    g_ff2, (g_mk, g_mv, g_mo, g_ff1) = got[0], got[1:]
    yn = _mix_norm_fwd_call(yf, yc, sp["g_fox_out"], sp["g_chk_out"], name="mix_norm_fwd")
    z, (g_ff2,) = _mm_nn(yn, g_out, "rows", name="mm_out", out_dtype=F32, task=_ag_d2d_task([g_ff2]))
    x1, h2 = _post_pre_call(xs, z, sp["g_mix_post"], sp["g_mem_pre"], name="post_mix")
    memn = _rms_fwd_call(mems, sp["g_mem_kv"], name="rms_mem_kv")
    q2 = _mm_nn(h2, g_mq, "rows", name="mm_mq")
    k2 = _mm_nn(memn, g_mk, "rows", name="mm_mk")
    v2 = _mm_nn(memn, g_mv, "rows", name="mm_mv")
    o2 = _mem_fwd_call(q2, k2, v2, name="mem_fwd")
    y2 = _mm_nn(o2, g_mo, "rows", name="mm_mo", out_dtype=F32)
    x2, h3 = _post_pre_call(x1, y2, sp["g_mem_post"], sp["g_ff_pre"], name="post_mem")
    act, relu = _mm_nn(h3, g_ff1, "cols", name="mm_ff1", epi="relu2")
    y3 = _mm_nn(act, g_ff2, "rows", name="mm_ff2", out_dtype=F32, tm=1024)
    loss_blk, dx3, dy3, dg_ff_post = _final_call(x2, y3, sp["g_ff_post"], target, name="final")

    d_ff2 = _mm_tn(act, dy3, name="mm_dff2", tk=512, tn=1024).reshape(N_CHIP, D_FF // N_CHIP, D)
    du, (r1,) = _mm_nt(dy3, g_ff2, "rows", name="mm_du", mul2r=relu, task=_rs_pair_task([d_ff2]))
    p_ff2 = pair_add("w_ff2", d_ff2, r1)
    d_ff1 = _mm_tn(h3, du, name="mm_dff1", cols4=True)
    dh3, (r1,) = _mm_nt(du, g_ff1, "cols", name="mm_dh3", out_dtype=F32, tm=1024, task=_rs_pair_task([d_ff1]))
    p_ff1 = pair_add("w_ff1", d_ff1, r1)
    dx2, dy2, dg_ff_pre, dg_mem_post = _bwd_mid_call(dx3, x2, dh3, sp["g_ff_pre"], y2, sp["g_mem_post"], name="bwd_ff")
    d_mo = _mm_tn(o2, dy2, name="mm_dmo").reshape(N_CHIP, D // N_CHIP, D)
    do2 = _mm_nt(dy2, g_mo, "rows", name="mm_do2")
    dq2, dk2, dv2 = _mem_bwd_call(q2, k2, v2, do2, name="mem_bwd")
    d_mq = _mm_tn(h2, dq2, name="mm_dmq").reshape(N_CHIP, D // N_CHIP, D)
    dh2 = _mm_nt(dq2, g_mq, "rows", name="mm_dh2", out_dtype=F32)
    d_mk = _mm_tn(memn, dk2, name="mm_dmk").reshape(N_CHIP, D // N_CHIP, D)
    d_mv = _mm_tn(memn, dv2, name="mm_dmv").reshape(N_CHIP, D // N_CHIP, D)
    dmn_k = _mm_nt(dk2, g_mk, "rows", name="mm_dmemk", out_dtype=F32)
    dmn_v = _mm_nt(dv2, g_mv, "rows", name="mm_dmemv", out_dtype=F32)
    dg_mem_kv = _gain_grad_call(mems, sp["g_mem_kv"], dmn_k, dmn_v, name="gain_mem_kv")
    dx1, dz, dg_mem_pre, dg_mix_post = _bwd_mid_call(dx2, x1, dh2, sp["g_mem_pre"], z, sp["g_mix_post"], name="bwd_mem")
    d_out = _mm_tn(yn, dz, name="mm_dout").reshape(N_CHIP, D // N_CHIP, D)
    late = ["w_mo", "w_mq", "w_mk", "w_mv", "w_out"]
    d_late = [d_mo, d_mq, d_mk, d_mv, d_out]
    dyn, r1_late = _mm_nt(dz, g_out, "rows", name="mm_dyn", out_dtype=F32, task=_rs_pair_task(d_late))
    p_late = [pair_add(k, d, r1) for k, d, r1 in zip(late, d_late, r1_late)]
    dof, doc, delta, dg_fox, dg_chk = _mix_norm_bwd_call(dyn, yf, yc, sp["g_fox_out"], sp["g_chk_out"], name="mix_norm_bwd")
    (dqf, dkf, dvf, dcq, dck), r2_ff = _fox_bwd_call(proj, dof, lse, delta, c_rep, c_t, name="fox_bwd",
                                                      task=_rs_chip_task([p_ff2, p_ff1]))
    (dqc, dkc, dvc, dgrev), r2_late = _chk_bwd_call(proj, doc, bias, name="chk_bwd", task=_rs_chip_task(p_late))
    first = ["w_ff2", "w_ff1"] + late
    dc8 = (dcq[:, :, 0:2, :] + dck[:, :, 0:2, :]).transpose(0, 2, 1, 3).reshape(8, S)
    dc_rows = jnp.concatenate([dc8, jnp.zeros((120, S), F32)], axis=0)
    dfl, db_fgt = _fox_gate_bwd_call(dc_rows, fl_raw, b_pad, name="fox_gate_bwd")
    dproj = jnp.concatenate([dqf, dkf, dvf, dqc, dkc, dvc, dfl], axis=1)
    d_all_t = _mm_tn(dproj, h1, name="mm_dwin", tk=640, tn=1024)
    d_in = _reorder_rows_call(d_all_t, False, name="d_in_rows").reshape(N_CHIP, IN_PAD, D)
    f_first, (r1,) = _chip_sums_call([p_ff2, p_ff1] + p_late, r2_ff + r2_late, place_arr, name="rs_chip_sums",
                                     task=_rs_pair_task([d_in]))
    p_in = pair_add("w_in", d_in, r1)
    dh1, got = _mm_nn(dproj, w_all_t, "plain", name="mm_dh1", out_dtype=F32, tm=1024,
                      task=_merge_tasks([_rs_chip_task([p_in]), _rs_gather_task(f_first)]))
    r2_in, grads = got[0], dict(zip(first, got[1:]))
    f_in = _chip_sum_call(p_in, r2_in, place_arr, name="rs_chip_sum_w_in")
    delta_w, new_m, new_v = {}, {}, {}

    def adamw_items(names):
        return [(w[k][0], grads[k], m[k][0], v[k][0]) for k in names]

    upd_late = _adamw_call(adamw_items(late), name="adamw_late", tm=64)
    upd_ff = _adamw_call(adamw_items(first[:2]), name="adamw_ff")
    for k, res in zip(late + first[:2], upd_late + upd_ff):
        grads[k], delta_w[k], new_m[k], new_v[k] = res
    grad_x, dg_mix_pre = _bwd_last_call(dx1, xs, dh1, sp["g_mix_pre"], name="bwd_mix")

    small_g = {"g_mix_pre": dg_mix_pre, "g_mix_post": dg_mix_post, "g_mem_kv": dg_mem_kv, "g_mem_pre": dg_mem_pre,
               "g_mem_post": dg_mem_post, "g_ff_pre": dg_ff_pre, "g_ff_post": dg_ff_post, "g_fox_out": dg_fox,
               "g_chk_out": dg_chk, "b_fgt": db_fgt,
               "rel_bias": _g_to_rel_table(dgrev[:, 0:2, :].reshape(8, ROLL_W))}
    small_sum, loss_out, (g_w_in,) = _small_reduce_call(small_g, loss_blk, _rs_gather_task([f_in]),
                                                       name="small_allreduce")
    small = _small_adamw_call(small_sum, sp, {k: rows(m, k) for k in SMALL}, {k: rows(v, k) for k in SMALL},
                              name="small_adamw")
    loss = loss_out[0, 0]
    res = _adamw_cols_call(w_in_t, g_w_in, m_in_t, v_in_t, name="adamw_w_in")
    grads["w_in"], delta_w["w_in"], new_m["w_in"], new_v["w_in"] = (a.T for a in res)
    for k in SMALL:
        vals = small[k]
        if k == "rel_bias":
            vals = tuple(a[None] for a in vals)
        grads[k], delta_w[k], new_m[k], new_v[k] = vals

    def out(d, k):
        return d[k][None] if k in BIG else d[k]

    return (loss, grad_x[None], *[out(grads, k) for k in WEIGHTS], *[out(delta_w, k) for k in WEIGHTS],
            *[out(new_m, k) for k in WEIGHTS], *[out(new_v, k) for k in WEIGHTS])
```

```python
import functools

import jax
import jax.numpy as jnp
from jax import lax
from jax.experimental import pallas as pl
from jax.experimental.pallas import tpu as pltpu

F32 = jnp.float32
BF16 = jnp.bfloat16

D = 1024
HEAD = 64
N_PAIR = 4
D_GRP = 512
CHUNK = 64
LEFT = 8
MAX_REL = 128
N_REL = 2 * MAX_REL + 1
N_MEM = 256
MEM_HEADS = 4
MEM_HD = 256
D_FF = 4096
D_IN = 3080
D_ALL = 3200
EPS = 1e-6
TQ = 256
WIN = (LEFT + TQ // CHUNK) * CHUNK
PADK = LEFT * CHUNK
ROLL_W = 1024
NEG = -1e30
N_CHIP = 4
VMEM_LIMIT = 48 * 1024 * 1024

ADAM_LR = 0.001
ADAM_B1 = 0.9
ADAM_B2 = 0.999
ADAM_EPS = 1e-08
ADAM_WD = 0.01
ADAM_STEP = 10

MESH = pl.DeviceIdType.MESH


def _cparams():
    return pltpu.CompilerParams(vmem_limit_bytes=VMEM_LIMIT)


ANY = pl.BlockSpec(memory_space=pl.ANY)


class _Task:
    def __init__(self, arrays, out_shapes, sems, issue, drain, aliases=None):
        self.arrays, self.out_shapes, self.sems = list(arrays), list(out_shapes), list(sems)
        self.issue, self.drain, self.aliases = issue, drain, dict(aliases or {})


def _merge_tasks(tasks):
    tasks = [t for t in tasks if t is not None]
    if len(tasks) == 1:
        return tasks[0]
    cuts, a, o, s = [], 0, 0, 0
    aliases = {}
    for t in tasks:
        cuts.append((a, o, s))
        aliases.update({a + i: o + j for i, j in t.aliases.items()})
        a, o, s = a + len(t.arrays), o + len(t.out_shapes), s + len(t.sems)

    def part(fn_name):
        def run(ins, outs, sems):
            for t, (a0, o0, s0) in zip(tasks, cuts):
                getattr(t, fn_name)(ins[a0:a0 + len(t.arrays)], outs[o0:o0 + len(t.out_shapes)],
                                    sems[s0:s0 + len(t.sems)])
        return run

    return _Task([x for t in tasks for x in t.arrays], [x for t in tasks for x in t.out_shapes],
                 [x for t in tasks for x in t.sems], part("issue"), part("drain"), aliases)


def _pallas(body, *, grid, in_specs, out_specs, out_shape, name, scratch_shapes=(), task=None, prefetch=0):
    def make(kernel, i_specs, o_specs, o_shape, scratch, aliases):
        if prefetch:
            spec = pltpu.PrefetchScalarGridSpec(num_scalar_prefetch=prefetch, grid=grid, in_specs=i_specs,
                                                out_specs=o_specs, scratch_shapes=scratch)
            return pl.pallas_call(kernel, grid_spec=spec, out_shape=o_shape, input_output_aliases=aliases,
                                  name=name, compiler_params=_cparams())
        return pl.pallas_call(kernel, grid=grid, in_specs=i_specs, out_specs=o_specs, out_shape=o_shape,
                              scratch_shapes=scratch, input_output_aliases=aliases, name=name,
                              compiler_params=_cparams())

    if task is None:
        return make(body, list(in_specs), out_specs, out_shape, list(scratch_shapes), {})
    single = not isinstance(out_shape, (tuple, list))
    o_shapes = [out_shape] if single else list(out_shape)
    o_specs = [out_specs] if single else list(out_specs)
    n_in, n_out, n_scr = len(in_specs), len(o_shapes), len(scratch_shapes)
    t_in, t_out = len(task.arrays), len(task.out_shapes)

    def carried(*refs):
        cut = [prefetch, n_in, t_in, n_out, t_out, n_scr]
        parts, p = [], 0
        for c in cut:
            parts.append(refs[p:p + c])
            p += c
        scalars, ins, tins, outs, touts, scr = parts
        tsems = refs[p:]
        ids = [pl.program_id(a) for a in range(len(grid))]
        first = functools.reduce(jnp.logical_and, [i == 0 for i in ids])
        last = functools.reduce(jnp.logical_and, [i == g - 1 for i, g in zip(ids, grid)])

        @pl.when(first)
        def _():
            task.issue(tins, touts, tsems)
        body(*scalars, *ins, *outs, *scr)

        @pl.when(last)
        def _():
            task.drain(tins, touts, tsems)

    call = make(carried, list(in_specs) + [ANY] * t_in, o_specs + [ANY] * t_out,
                o_shapes + list(task.out_shapes), list(scratch_shapes) + list(task.sems),
                {prefetch + n_in + i: n_out + j for i, j in task.aliases.items()})

    def run(*args):
        res = call(*args, *task.arrays)
        outs = res[:n_out]
        return (outs[0] if single else tuple(outs)), list(res[n_out:])

    return run


def _dot(a, b):
    return jnp.dot(a, b, preferred_element_type=F32)


def _dot_nt(a, b):
    return lax.dot_general(a, b, (((1,), (1,)), ((), ())), preferred_element_type=F32)


def _dot_tn(a, b):
    return lax.dot_general(a, b, (((0,), (0,)), ((), ())), preferred_element_type=F32)


def _split3(x):
    hi = x.astype(BF16)
    r1 = x - hi.astype(F32)
    mid = r1.astype(BF16)
    lo = (r1 - mid.astype(F32)).astype(BF16)
    return hi, mid, lo


def _dot3(x, m01):
    hi, mid, lo = _split3(x)
    return _dot(hi, m01) + _dot(mid, m01) + _dot(lo, m01)


def _dot3_l(m01, x):
    hi, mid, lo = _split3(x)
    return _dot(m01, hi) + _dot(m01, mid) + _dot(m01, lo)


def _mm_nn(a, b, kind, *, name, out_dtype=BF16, tm=2048, tn=512, epi=None, task=None):
    M, K = a.shape
    if kind == "plain":
        N = b.shape[1]
        b_spec = pl.BlockSpec((K, tn), lambda m, n: (0, n))
    elif kind == "rows":
        N = b.shape[2]
        b_spec = pl.BlockSpec((N_CHIP, K // N_CHIP, tn), lambda m, n: (0, 0, n))
    else:
        nq = b.shape[2]
        N = N_CHIP * nq
        per = nq // tn
        b_spec = pl.BlockSpec((None, K, tn), lambda m, n: (n // per, 0, n % per))
    tm = min(tm, M)
    kq = K // N_CHIP

    def body(a_ref, b_ref, *o_refs):
        if kind == "rows":
            acc = _dot(a_ref[:, 0:kq], b_ref[0])
            for j in range(1, N_CHIP):
                acc += _dot(a_ref[:, j * kq:(j + 1) * kq], b_ref[j])
        else:
            acc = _dot(a_ref[...], b_ref[...])
        if epi == "relu2":
            r = jnp.maximum(acc, 0.0)
            o_refs[0][...] = (r * r).astype(BF16)
            o_refs[1][...] = r.astype(BF16)
        else:
            o_refs[0][...] = acc.astype(out_dtype)

    o_spec = pl.BlockSpec((tm, tn), lambda m, n: (m, n))
    if epi == "relu2":
        out_shape = (jax.ShapeDtypeStruct((M, N), BF16), jax.ShapeDtypeStruct((M, N), BF16))
        out_specs = (o_spec, o_spec)
    else:
        out_shape = jax.ShapeDtypeStruct((M, N), out_dtype)
        out_specs = o_spec
    return _pallas(
        body, grid=(M // tm, N // tn),
        in_specs=[pl.BlockSpec((tm, K), lambda m, n: (m, 0)), b_spec],
        out_specs=out_specs, out_shape=out_shape, name=name, task=task,
    )(a, b)


def _mm_nt(a, b, kind, *, name, out_dtype=BF16, tm=2048, tn=512, mul2r=None, task=None, rows=None):
    M, K = a.shape
    if kind == "plain":
        first, N = rows if rows is not None else (0, b.shape[0])
        n0 = first // tn
        b_spec = pl.BlockSpec((tn, K), lambda m, n: (n0 + n, 0))
    elif kind == "rows":
        nq = b.shape[1]
        N = N_CHIP * nq
        tn = min(tn, nq)
        per = nq // tn
        b_spec = pl.BlockSpec((None, tn, K), lambda m, n: (n // per, n % per, 0))
    else:
        N = b.shape[1]
        b_spec = pl.BlockSpec((N_CHIP, tn, K // N_CHIP), lambda m, n: (0, n, 0))
    tm = min(tm, M)
    kq = K // N_CHIP

    def body(a_ref, b_ref, *rest):
        o_ref = rest[-1]
        if kind == "cols":
            acc = _dot_nt(a_ref[:, 0:kq], b_ref[0])
            for j in range(1, N_CHIP):
                acc += _dot_nt(a_ref[:, j * kq:(j + 1) * kq], b_ref[j])
        else:
            acc = _dot_nt(a_ref[...], b_ref[...])
        if mul2r is not None:
            acc = acc * (2.0 * rest[0][...].astype(F32))
        o_ref[...] = acc.astype(out_dtype)

    in_specs = [pl.BlockSpec((tm, K), lambda m, n: (m, 0)), b_spec]
    args = [a, b]
    if mul2r is not None:
        in_specs.append(pl.BlockSpec((tm, tn), lambda m, n: (m, n)))
        args.append(mul2r)
    return _pallas(
        body, grid=(M // tm, N // tn), in_specs=in_specs,
        out_specs=pl.BlockSpec((tm, tn), lambda m, n: (m, n)),
        out_shape=jax.ShapeDtypeStruct((M, N), out_dtype), name=name, task=task,
    )(*args)


def _mm_tn(a, b, *, name, out_dtype=BF16, tk=1024, tn=512, cols4=False, task=None):
    M, K1 = a.shape
    N = b.shape[1]
    tk = min(tk, K1)
    tn = min(tn, N)

    def body(a_ref, b_ref, o_ref):
        o_ref[...] = _dot_tn(a_ref[...], b_ref[...]).astype(out_dtype)

    if cols4:
        per = (N // N_CHIP) // tn
        out_shape = jax.ShapeDtypeStruct((N_CHIP, K1, N // N_CHIP), out_dtype)
        o_spec = pl.BlockSpec((None, tk, tn), lambda k, n: (n // per, k, n % per))
    else:
        out_shape = jax.ShapeDtypeStruct((K1, N), out_dtype)
        o_spec = pl.BlockSpec((tk, tn), lambda k, n: (k, n))
    return _pallas(
        body, grid=(K1 // tk, N // tn),
        in_specs=[pl.BlockSpec((M, tk), lambda k, n: (0, k)), pl.BlockSpec((M, tn), lambda k, n: (0, n))],
        out_specs=o_spec, out_shape=out_shape, name=name, task=task,
    )(a, b)


def _rms(x, g):
    r = lax.rsqrt(jnp.mean(x * x, axis=-1, keepdims=True) + EPS)
    return x * r * g


def _rms_bwd(x, g, dy):
    r = lax.rsqrt(jnp.mean(x * x, axis=-1, keepdims=True) + EPS)
    xh = x * r
    dg = jnp.sum(dy * xh, axis=0, keepdims=True)
    dxh = dy * g
    dx = r * (dxh - xh * jnp.mean(dxh * xh, axis=-1, keepdims=True))
    return dx, dg


def _row_spec(tm, n):
    return pl.BlockSpec((tm, n), lambda i: (i, 0))


def _vec_spec(n):
    return pl.BlockSpec((1, n), lambda i: (0, 0))


def _acc_spec(n):
    return pl.BlockSpec((8, n), lambda i: (0, 0))


def _acc_add(ref, row, i):
    @pl.when(i == 0)
    def _():
        ref[...] = jnp.zeros_like(ref)
    ref[0:1, :] += row


def _rms_fwd_call(x, g, *, name, tm=1024, task=None):
    M, n = x.shape
    tm = min(tm, M)

    def body(x_ref, g_ref, h_ref):
        h_ref[...] = _rms(x_ref[...], g_ref[...]).astype(BF16)

    return _pallas(
        body, grid=(M // tm,), in_specs=[_row_spec(tm, n), _vec_spec(n)], out_specs=_row_spec(tm, n),
        out_shape=jax.ShapeDtypeStruct((M, n), BF16), name=name, task=task,
    )(x, g)


def _post_pre_call(xres, z, g_post, g_pre, *, name, tm=1024):
    M, n = xres.shape

    def body(x_ref, z_ref, gp_ref, gn_ref, xo_ref, h_ref):
        xn = x_ref[...] + _rms(z_ref[...], gp_ref[...])
        xo_ref[...] = xn
        h_ref[...] = _rms(xn, gn_ref[...]).astype(BF16)

    return pl.pallas_call(
        body, grid=(M // tm,),
        in_specs=[_row_spec(tm, n), _row_spec(tm, n), _vec_spec(n), _vec_spec(n)],
        out_specs=(_row_spec(tm, n), _row_spec(tm, n)),
        out_shape=(jax.ShapeDtypeStruct((M, n), F32), jax.ShapeDtypeStruct((M, n), BF16)),
        name=name, compiler_params=_cparams(),
    )(xres, z, g_post, g_pre)


def _final_call(x2, y3, g_post, target, *, name, tm=512):
    M, n = x2.shape

    deep = pl.BlockSpec((tm, n), lambda i: (i, 0), pipeline_mode=pl.Buffered(3))

    def body(x_hbm, y_hbm, g_ref, t_hbm, loss_ref, dx_hbm, dy_hbm, dg_ref):
        loss_ref[...] = jnp.zeros_like(loss_ref)
        dg_ref[...] = jnp.zeros_like(dg_ref)

        def tile(x_ref, y_ref, t_ref, dx_ref, dy_ref):
            y = y_ref[...]
            g = g_ref[...]
            diff = x_ref[...] + _rms(y, g) - t_ref[...]
            part = 0.5 * jnp.sum(jnp.sum(diff * diff, axis=1, keepdims=True), axis=0, keepdims=True) / n
            loss_ref[...] += jnp.broadcast_to(part, loss_ref.shape)
            dx = diff / n
            dx_ref[...] = dx
            dy, dg = _rms_bwd(y, g, dx)
            dy_ref[...] = dy.astype(BF16)
            dg_ref[0:1, :] += dg

        pltpu.emit_pipeline(tile, grid=(M // tm,), in_specs=[deep] * 3,
                            out_specs=[_row_spec(tm, n), _row_spec(tm, n)])(x_hbm, y_hbm, t_hbm, dx_hbm, dy_hbm)

    vm = pl.BlockSpec(memory_space=pltpu.VMEM)
    return pl.pallas_call(
        body, in_specs=[ANY, ANY, vm, ANY], out_specs=(vm, ANY, ANY, vm),
        out_shape=(jax.ShapeDtypeStruct((8, 128), F32), jax.ShapeDtypeStruct((M, n), F32),
                   jax.ShapeDtypeStruct((M, n), BF16), jax.ShapeDtypeStruct((8, n), F32)),
        name=name, compiler_params=_cparams(),
    )(x2, y3, g_post, target)


def _bwd_mid_call(dx_in, x, dh, g_pre, y, g_post, *, name, tm=512):
    M, n = x.shape

    deep = pl.BlockSpec((tm, n), lambda i: (i, 0), pipeline_mode=pl.Buffered(3))

    def body(dxi_hbm, x_hbm, dh_hbm, gpre_ref, y_hbm, gpost_ref, dx_hbm, dy_hbm, dgpre_ref, dgpost_ref):
        dgpre_ref[...] = jnp.zeros_like(dgpre_ref)
        dgpost_ref[...] = jnp.zeros_like(dgpost_ref)

        def tile(dxi_ref, x_ref, dh_ref, y_ref, dx_ref, dy_ref):
            d1, dg1 = _rms_bwd(x_ref[...], gpre_ref[...], dh_ref[...])
            dx = dxi_ref[...] + d1
            dx_ref[...] = dx
            dy, dg2 = _rms_bwd(y_ref[...], gpost_ref[...], dx)
            dy_ref[...] = dy.astype(BF16)
            dgpre_ref[0:1, :] += dg1
            dgpost_ref[0:1, :] += dg2

        pltpu.emit_pipeline(tile, grid=(M // tm,), in_specs=[deep] * 4,
                            out_specs=[_row_spec(tm, n), _row_spec(tm, n)])(
            dxi_hbm, x_hbm, dh_hbm, y_hbm, dx_hbm, dy_hbm)

    vm = pl.BlockSpec(memory_space=pltpu.VMEM)
    return pl.pallas_call(
        body, in_specs=[ANY, ANY, ANY, vm, ANY, vm], out_specs=(ANY, ANY, vm, vm),
        out_shape=(jax.ShapeDtypeStruct((M, n), F32), jax.ShapeDtypeStruct((M, n), BF16),
                   jax.ShapeDtypeStruct((8, n), F32), jax.ShapeDtypeStruct((8, n), F32)),
        name=name, compiler_params=_cparams(),
    )(dx_in, x, dh, g_pre, y, g_post)


def _bwd_last_call(dx_in, x, dh, g_pre, *, name, tm=1024, task=None):
    M, n = x.shape

    def body(dxi_ref, x_ref, dh_ref, g_ref, dx_ref, dg_ref):
        i = pl.program_id(0)
        d1, dg1 = _rms_bwd(x_ref[...], g_ref[...], dh_ref[...])
        dx_ref[...] = dxi_ref[...] + d1
        _acc_add(dg_ref, dg1, i)

    return _pallas(
        body, grid=(M // tm,),
        in_specs=[_row_spec(tm, n), _row_spec(tm, n), _row_spec(tm, n), _vec_spec(n)],
        out_specs=(_row_spec(tm, n), _acc_spec(n)),
        out_shape=(jax.ShapeDtypeStruct((M, n), F32), jax.ShapeDtypeStruct((8, n), F32)),
        name=name, task=task,
    )(dx_in, x, dh, g_pre)


def _gain_grad_call(x, g, dy_a, dy_b, *, name):
    M, n = x.shape

    def body(x_ref, g_ref, a_ref, b_ref, dg_ref):
        _, dg = _rms_bwd(x_ref[...], g_ref[...], a_ref[...] + b_ref[...])
        dg_ref[...] = jnp.zeros_like(dg_ref)
        dg_ref[0:1, :] = dg

    return pl.pallas_call(
        body, grid=(1,),
        in_specs=[_row_spec(M, n), _vec_spec(n), _row_spec(M, n), _row_spec(M, n)],
        out_specs=_acc_spec(n), out_shape=jax.ShapeDtypeStruct((8, n), F32),
        name=name, compiler_params=_cparams(),
    )(x, g, dy_a, dy_b)


def _head_group_matrix():
    a = lax.broadcasted_iota(jnp.int32, (D_GRP, D_GRP), 0) // HEAD
    b = lax.broadcasted_iota(jnp.int32, (D_GRP, D_GRP), 1) // HEAD
    return jnp.where(a == b, 1.0, 0.0).astype(BF16)


def _mix_norm_fwd_call(yf, yc, gf, gc, *, name, tm=1024):
    M = yf.shape[0]

    def body(yf_ref, yc_ref, gf_ref, gc_ref, o_ref):
        o_ref[:, 0:D_GRP] = _rms(yf_ref[...], gf_ref[...]).astype(BF16)
        o_ref[:, D_GRP:D] = _rms(yc_ref[...], gc_ref[...]).astype(BF16)

    return pl.pallas_call(
        body, grid=(M // tm,),
        in_specs=[_row_spec(tm, D_GRP), _row_spec(tm, D_GRP), _vec_spec(D_GRP), _vec_spec(D_GRP)],
        out_specs=_row_spec(tm, D), out_shape=jax.ShapeDtypeStruct((M, D), BF16),
        name=name, compiler_params=_cparams(),
    )(yf, yc, gf, gc)


def _mix_norm_bwd_call(dyn, yf, yc, gf, gc, *, name, tm=2 * TQ):
    M = yf.shape[0]

    def body(dyn_ref, yf_ref, yc_ref, gf_ref, gc_ref, dof_ref, doc_ref, delta_ref, dgf_ref, dgc_ref):
        i = pl.program_id(0)
        yf_ = yf_ref[...]
        dof, dgf = _rms_bwd(yf_, gf_ref[...], dyn_ref[:, 0:D_GRP])
        doc, dgc = _rms_bwd(yc_ref[...], gc_ref[...], dyn_ref[:, D_GRP:D])
        dof_b = dof.astype(BF16)
        dof_ref[...] = dof_b
        doc_ref[...] = doc.astype(BF16)
        prod = dof_b.astype(F32) * yf_
        hi = prod.astype(BF16)
        lo = (prod - hi.astype(F32)).astype(BF16)
        grp = _head_group_matrix()
        delta = _dot(hi, grp) + _dot(lo, grp)
        for b in range(tm // TQ):
            delta_ref[b] = delta[b * TQ:(b + 1) * TQ, :].T
        _acc_add(dgf_ref, dgf, i)
        _acc_add(dgc_ref, dgc, i)

    return pl.pallas_call(
        body, grid=(M // tm,),
        in_specs=[_row_spec(tm, D), _row_spec(tm, D_GRP), _row_spec(tm, D_GRP), _vec_spec(D_GRP), _vec_spec(D_GRP)],
        out_specs=(_row_spec(tm, D_GRP), _row_spec(tm, D_GRP),
                   pl.BlockSpec((tm // TQ, D_GRP, TQ), lambda i: (i, 0, 0)), _acc_spec(D_GRP), _acc_spec(D_GRP)),
        out_shape=(jax.ShapeDtypeStruct((M, D_GRP), BF16), jax.ShapeDtypeStruct((M, D_GRP), BF16),
                   jax.ShapeDtypeStruct((M // TQ, D_GRP, TQ), F32), jax.ShapeDtypeStruct((8, D_GRP), F32),
                   jax.ShapeDtypeStruct((8, D_GRP), F32)),
        name=name, compiler_params=_cparams(),
    )(dyn, yf, yc, gf, gc)


def _tri(n, lower_incl):
    a = lax.broadcasted_iota(jnp.int32, (n, n), 0)
    b = lax.broadcasted_iota(jnp.int32, (n, n), 1)
    return jnp.where(a >= b, 1.0, 0.0).astype(BF16) if lower_incl else jnp.where(a <= b, 1.0, 0.0).astype(BF16)


def _fox_prep_call(fl_raw, b_pad, *, name):
    S = fl_raw.shape[0]
    nb = S // TQ

    def body(fl_ref, b_ref, crep_ref, ct_ref, carry_ref):
        i = pl.program_id(0)

        @pl.when(i == 0)
        def _():
            carry_ref[...] = jnp.zeros_like(carry_ref)
        logf = jax.nn.log_sigmoid(fl_ref[...] + b_ref[...])
        cb = _dot3_l(_tri(TQ, True), logf) + carry_ref[0:1, :]
        carry_ref[0:1, :] = cb[TQ - 1:TQ, :]
        a = lax.broadcasted_iota(jnp.int32, (128, D_GRP), 0)
        b = lax.broadcasted_iota(jnp.int32, (128, D_GRP), 1) // HEAD
        expand = jnp.where(a == b, 1.0, 0.0).astype(BF16)
        crep = _dot3(cb, expand)
        crep_ref[...] = crep
        ct_ref[...] = crep.T

    return pl.pallas_call(
        body, grid=(nb,),
        in_specs=[_row_spec(TQ, 128), _vec_spec(128)],
        out_specs=(_row_spec(TQ, D_GRP), pl.BlockSpec((None, D_GRP, TQ), lambda i: (i, 0, 0))),
        out_shape=(jax.ShapeDtypeStruct((S, D_GRP), F32), jax.ShapeDtypeStruct((nb, D_GRP, TQ), F32)),
        scratch_shapes=[pltpu.VMEM((8, 128), F32)],
        name=name, compiler_params=_cparams(),
    )(fl_raw, b_pad)


def _lane_masks():
    lane = lax.broadcasted_iota(jnp.int32, (1, 128), 1)
    return lane < HEAD, lane >= HEAD


def _fox_fwd_call(proj, c_rep, c_t, *, name, task=None):
    S = proj.shape[0]
    nq = S // TQ
    scale = HEAD ** -0.5

    def body(q_ref, k_ref, v_ref, c_ref, ct_ref, o_ref, lse_ref):
        i = pl.program_id(1)
        m_lo, m_hi = _lane_masks()
        masks = (m_lo, m_hi)
        q = q_ref[...] * scale
        qm = [jnp.where(mk, q, jnp.zeros_like(q)) for mk in masks]
        cq = c_ref[...]
        cqh = [cq[:, 0:1], cq[:, HEAD:HEAD + 1]]
        row = lax.broadcasted_iota(jnp.int32, (TQ, TQ), 0)
        col = lax.broadcasted_iota(jnp.int32, (TQ, TQ), 1)

        def scores(j):
            start = pl.multiple_of(j * TQ, TQ)
            k = k_ref[pl.ds(start, TQ), :]
            ct = ct_ref[j]
            return tuple(_dot_nt(qm[h], k) + (cqh[h] - ct[HEAD * h:HEAD * h + 1, :]) for h in range(2))

        def update(j, ss, state, masked):
            ms, ls, acc = state
            start = pl.multiple_of(j * TQ, TQ)
            v = v_ref[pl.ds(start, TQ), :]
            new_m, new_l, pv, alpha_l = [], [], [], []
            for h in range(2):
                s = ss[h]
                if masked:
                    s = jnp.where(row >= col, s, NEG)
                mn = jnp.maximum(ms[h], jnp.max(s, axis=1, keepdims=True))
                alpha = jnp.exp(ms[h] - mn)
                p = jnp.exp(s - mn)
                new_l.append(alpha * ls[h] + jnp.sum(p, axis=1, keepdims=True))
                new_m.append(mn)
                alpha_l.append(alpha)
                pv.append(_dot(p.astype(BF16), jnp.where(masks[h], v, jnp.zeros_like(v))))
            alpha_lane = jnp.where(m_lo, alpha_l[0], alpha_l[1])
            acc = acc * alpha_lane + pv[0] + pv[1]
            return (tuple(new_m), tuple(new_l), acc)

        def step(j, carry):
            ss, state = carry
            return (scores(j + 1), update(j, ss, state, False))

        init = ((jnp.full((TQ, 1), NEG, F32),) * 2, (jnp.zeros((TQ, 1), F32),) * 2, jnp.zeros((TQ, 128), F32))
        ss, state = lax.fori_loop(0, i, step, (scores(0), init))
        ms, ls, acc = update(i, ss, state, True)
        l_lane = jnp.where(m_lo, ls[0], ls[1])
        o_ref[...] = acc / l_lane
        lse_ref[...] = jnp.where(m_lo, ms[0] + jnp.log(ls[0]), ms[1] + jnp.log(ls[1])).T

    return _pallas(
        body, grid=(N_PAIR, nq),
        in_specs=[pl.BlockSpec((TQ, 128), lambda p, i: (i, p)),
                  pl.BlockSpec((S, 128), lambda p, i: (0, N_PAIR + p)),
                  pl.BlockSpec((S, 128), lambda p, i: (0, 2 * N_PAIR + p)),
                  pl.BlockSpec((TQ, 128), lambda p, i: (i, p)),
                  pl.BlockSpec((nq, 128, TQ), lambda p, i: (0, p, 0))],
        out_specs=(pl.BlockSpec((TQ, 128), lambda p, i: (i, p)), pl.BlockSpec((None, 128, TQ), lambda p, i: (i, p, 0))),
        out_shape=(jax.ShapeDtypeStruct((S, D_GRP), F32), jax.ShapeDtypeStruct((nq, D_GRP, TQ), F32)),
        name=name, task=task,
    )(proj, proj, proj, c_rep, c_t)


def _fox_bwd_call(proj, do, lse_t, delta_t, c_rep, c_t, *, name, task=None):
    S = proj.shape[0]
    nq = S // TQ
    scale = HEAD ** -0.5

    def body(q_ref, k_ref, v_ref, do_ref, lse_ref, dl_ref, ck_ref, ct_ref,
             dq_ref, dk_ref, dv_ref, dcq_ref, dck_ref, dqa_ref):
        j = pl.program_id(1)
        m_lo, m_hi = _lane_masks()
        masks = (m_lo, m_hi)

        @pl.when(j == 0)
        def _():
            dqa_ref[...] = jnp.zeros_like(dqa_ref)
            dcq_ref[...] = jnp.zeros_like(dcq_ref)
        k = k_ref[...]
        v = v_ref[...]
        km = [jnp.where(mk, k, jnp.zeros_like(k)) for mk in masks]
        ck = ck_ref[...]
        krow = lax.broadcasted_iota(jnp.int32, (TQ, TQ), 0)
        qcol = lax.broadcasted_iota(jnp.int32, (TQ, TQ), 1)

        def probs(i):
            start = pl.multiple_of(i * TQ, TQ)
            q = q_ref[pl.ds(start, TQ), :]
            do = do_ref[pl.ds(start, TQ), :]
            lse = lse_ref[i]
            cq = ct_ref[i]
            out = []
            for h in range(2):
                lo = HEAD * h
                qm = jnp.where(masks[h], q * scale, jnp.zeros_like(q))
                dom = jnp.where(masks[h], do, jnp.zeros_like(do))
                st = _dot_nt(k, qm) + (cq[lo:lo + 1, :] - ck[:, lo:lo + 1])
                out.append((jnp.exp(st - lse[lo:lo + 1, :]), _dot_nt(v, dom)))
            return tuple(out)

        def update(i, pd, carry, masked):
            dk, dv, dck = carry
            start = pl.multiple_of(i * TQ, TQ)
            q = q_ref[pl.ds(start, TQ), :]
            do = do_ref[pl.ds(start, TQ), :]
            dl = dl_ref[i]
            dq = jnp.zeros((TQ, 128), F32)
            new_dck = []
            for h in range(2):
                lo = HEAD * h
                qm = jnp.where(masks[h], q, jnp.zeros_like(q))
                dom = jnp.where(masks[h], do, jnp.zeros_like(do))
                pt, dpt = pd[h]
                if masked:
                    pt = jnp.where(qcol >= krow, pt, 0.0)
                dst = pt * (dpt - dl[lo:lo + 1, :])
                dcq_ref[i, h:h + 1, :] += jnp.sum(dst, axis=0, keepdims=True)
                new_dck.append(dck[h] + jnp.sum(dst, axis=1, keepdims=True))
                dsb = (dst * scale).astype(BF16)
                dv = dv + _dot(pt.astype(BF16), dom)
                dk = dk + _dot(dsb, qm)
                dq = dq + _dot_tn(dsb, km[h])
            dqa_ref[pl.ds(start, TQ), :] += dq
            return (dk, dv, tuple(new_dck))

        def step(i, carry):
            pd, sums = carry
            return (probs(jnp.minimum(i + 1, nq - 1)), update(i, pd, sums, False))

        init = (jnp.zeros((TQ, 128), F32), jnp.zeros((TQ, 128), F32), (jnp.zeros((TQ, 1), F32),) * 2)
        first = probs(j)
        second = probs(jnp.minimum(j + 1, nq - 1))
        _, (dk, dv, dck) = lax.fori_loop(j + 1, nq, step, (second, update(j, first, init, True)))
        dk_ref[...] = dk.astype(BF16)
        dv_ref[...] = dv.astype(BF16)
        dck_t = jnp.where(m_lo, dck[0], dck[1]).T
        row = lax.broadcasted_iota(jnp.int32, (8, TQ), 0)
        dck_ref[...] = -jnp.where(row == 0, dck_t[0:1, :], jnp.where(row == 1, dck_t[HEAD:HEAD + 1, :], 0.0))

        @pl.when(j == nq - 1)
        def _():
            dq_ref[...] = dqa_ref[...].astype(BF16)

    res = lambda p, j: (0, p)
    stat = pl.BlockSpec((nq, 128, TQ), lambda p, j: (0, p, 0))
    blk = pl.BlockSpec((TQ, 128), lambda p, j: (j, p))
    return _pallas(
        body, grid=(N_PAIR, nq), task=task,
        in_specs=[pl.BlockSpec((S, 128), res),
                  pl.BlockSpec((TQ, 128), lambda p, j: (j, N_PAIR + p)),
                  pl.BlockSpec((TQ, 128), lambda p, j: (j, 2 * N_PAIR + p)),
                  pl.BlockSpec((S, 128), res), stat, stat, blk, stat],
        out_specs=(pl.BlockSpec((S, 128), res), blk, blk,
                   pl.BlockSpec((None, nq, 8, TQ), lambda p, j: (p, 0, 0, 0)),
                   pl.BlockSpec((None, None, 8, TQ), lambda p, j: (p, j, 0, 0))),
        out_shape=(jax.ShapeDtypeStruct((S, D_GRP), BF16), jax.ShapeDtypeStruct((S, D_GRP), BF16),
                   jax.ShapeDtypeStruct((S, D_GRP), BF16), jax.ShapeDtypeStruct((N_PAIR, nq, 8, TQ), F32),
                   jax.ShapeDtypeStruct((N_PAIR, nq, 8, TQ), F32)),
        scratch_shapes=[pltpu.VMEM((S, 128), F32)],
        name=name,
    )(proj, proj, proj, do, lse_t, delta_t, c_rep, c_t)


def _fox_gate_bwd_call(dc_rows, fl_raw, b_pad, *, name):
    S = fl_raw.shape[0]
    nb = S // TQ

    def body(dc_ref, fl_ref, b_ref, dfl_ref, db_ref, carry_ref):
        i = pl.program_id(0)

        @pl.when(i == 0)
        def _():
            carry_ref[...] = jnp.zeros_like(carry_ref)
        rc = _dot3(dc_ref[...], _tri(TQ, True)) + carry_ref[:, 0:1]
        carry_ref[...] = jnp.broadcast_to(rc[:, 0:1], carry_ref.shape)
        fl = fl_ref[...] + b_ref[...]
        dfl = rc.T * jax.nn.sigmoid(-fl)
        dfl_ref[...] = dfl.astype(BF16)
        _acc_add(db_ref, jnp.sum(dfl, axis=0, keepdims=True), i)

    rev = lambda i: (nb - 1 - i, 0)
    return pl.pallas_call(
        body, grid=(nb,),
        in_specs=[pl.BlockSpec((128, TQ), lambda i: (0, nb - 1 - i)), pl.BlockSpec((TQ, 128), rev), _vec_spec(128)],
        out_specs=(pl.BlockSpec((TQ, 128), rev), _acc_spec(128)),
        out_shape=(jax.ShapeDtypeStruct((S, 128), BF16), jax.ShapeDtypeStruct((8, 128), F32)),
        scratch_shapes=[pltpu.VMEM((128, 128), F32)],
        name=name, compiler_params=_cparams(),
    )(dc_rows, fl_raw, b_pad)


def _chk_bias_call(g_rev, *, name):
    def body(g_ref, o_ref):
        x = jnp.broadcast_to(g_ref[...], (TQ, ROLL_W))
        rolled = pltpu.roll(x, ROLL_W - (TQ - 1), 1, stride=1, stride_axis=0)
        qc = lax.broadcasted_iota(jnp.int32, (TQ, WIN), 0) // CHUNK
        kc = lax.broadcasted_iota(jnp.int32, (TQ, WIN), 1) // CHUNK
        band = (kc >= qc) & (kc <= qc + LEFT)
        o_ref[...] = jnp.where(band, rolled[:, 0:WIN], NEG)

    return pl.pallas_call(
        body, grid=(8,),
        in_specs=[pl.BlockSpec((None, 1, ROLL_W), lambda h: (h, 0, 0))],
        out_specs=pl.BlockSpec((None, TQ, WIN), lambda h: (h, 0, 0)),
        out_shape=jax.ShapeDtypeStruct((8, TQ, WIN), F32), name=name, compiler_params=_cparams(),
    )(g_rev.reshape(8, 1, ROLL_W))


def _chk_scores(i, qm, kwin, bias, scale):
    s = _dot_nt(qm * scale, kwin) + bias
    kc = lax.broadcasted_iota(jnp.int32, (TQ, WIN), 1) // CHUNK
    return jnp.where(kc + i * (TQ // CHUNK) >= LEFT, s, NEG)


def _chk_fwd_call(proj, bias, *, name, task=None):
    S = proj.shape[0]
    nq = S // TQ
    scale = HEAD ** -0.5

    def body(q_ref, k_ref, v_ref, b_ref, o_ref, kp_ref, vp_ref):
        i = pl.program_id(1)

        @pl.when(i == 0)
        def _():
            kp_ref[0:PADK, :] = jnp.zeros((PADK, 128), BF16)
            vp_ref[0:PADK, :] = jnp.zeros((PADK, 128), BF16)
            kp_ref[PADK:PADK + S, :] = k_ref[...]
            vp_ref[PADK:PADK + S, :] = v_ref[...]
        masks = _lane_masks()
        q = q_ref[...]
        start = pl.multiple_of(i * TQ, TQ)
        kwin = kp_ref[pl.ds(start, WIN), :]
        vwin = vp_ref[pl.ds(start, WIN), :]
        ss = [_chk_scores(i, jnp.where(masks[h], q, jnp.zeros_like(q)), kwin, b_ref[h], scale) for h in range(2)]
        ps = []
        for s in ss:
            p = jnp.exp(s - jnp.max(s, axis=1, keepdims=True))
            ps.append((p / jnp.sum(p, axis=1, keepdims=True)).astype(BF16))
        o_ref[...] = (_dot(ps[0], jnp.where(masks[0], vwin, jnp.zeros_like(vwin)))
                      + _dot(ps[1], jnp.where(masks[1], vwin, jnp.zeros_like(vwin))))

    c0 = 3 * N_PAIR
    return _pallas(
        body, grid=(N_PAIR, nq), task=task,
        in_specs=[pl.BlockSpec((TQ, 128), lambda p, i: (i, c0 + p)),
                  pl.BlockSpec((S, 128), lambda p, i: (0, c0 + N_PAIR + p)),
                  pl.BlockSpec((S, 128), lambda p, i: (0, c0 + 2 * N_PAIR + p)),
                  pl.BlockSpec((2, TQ, WIN), lambda p, i: (p, 0, 0))],
        out_specs=pl.BlockSpec((TQ, 128), lambda p, i: (i, p)),
        out_shape=jax.ShapeDtypeStruct((S, D_GRP), F32),
        scratch_shapes=[pltpu.VMEM((S + PADK, 128), BF16), pltpu.VMEM((S + PADK, 128), BF16)],
        name=name,
    )(proj, proj, proj, bias)


def _chk_bwd_call(proj, do, bias, *, name, task=None):
    S = proj.shape[0]
    nq = S // TQ
    scale = HEAD ** -0.5

    def body(q_ref, k_ref, v_ref, do_ref, b_ref, dq_ref, dk_ref, dv_ref, dg_ref, kp_ref, vp_ref, dkp_ref, dvp_ref, db_ref):
        i = pl.program_id(1)

        @pl.when(i == 0)
        def _():
            kp_ref[0:PADK, :] = jnp.zeros((PADK, 128), BF16)
            vp_ref[0:PADK, :] = jnp.zeros((PADK, 128), BF16)
            kp_ref[PADK:PADK + S, :] = k_ref[...]
            vp_ref[PADK:PADK + S, :] = v_ref[...]
            dkp_ref[...] = jnp.zeros_like(dkp_ref)
            dvp_ref[...] = jnp.zeros_like(dvp_ref)
            db_ref[...] = jnp.zeros_like(db_ref)
        masks = _lane_masks()
        q = q_ref[...]
        dout = do_ref[...]
        start = pl.multiple_of(i * TQ, TQ)
        kwin = kp_ref[pl.ds(start, WIN), :]
        vwin = vp_ref[pl.ds(start, WIN), :]
        qm = [jnp.where(mk, q, jnp.zeros_like(q)) for mk in masks]
        dom = [jnp.where(mk, dout, jnp.zeros_like(dout)) for mk in masks]
        ss = [_chk_scores(i, qm[h], kwin, b_ref[h], scale) for h in range(2)]
        dps = [_dot_nt(dom[h], vwin) for h in range(2)]
        pbs, dsbs = [], []
        for h in range(2):
            p = jnp.exp(ss[h] - jnp.max(ss[h], axis=1, keepdims=True))
            p = p / jnp.sum(p, axis=1, keepdims=True)
            ds = p * (dps[h] - jnp.sum(p * dps[h], axis=1, keepdims=True))
            db_ref[h] += ds
            pbs.append(p.astype(BF16))
            dsbs.append((ds * scale).astype(BF16))
        dq_ref[...] = (_dot(dsbs[0], jnp.where(masks[0], kwin, jnp.zeros_like(kwin)))
                       + _dot(dsbs[1], jnp.where(masks[1], kwin, jnp.zeros_like(kwin)))).astype(BF16)
        dkp_ref[pl.ds(start, WIN), :] += _dot_tn(dsbs[0], qm[0]) + _dot_tn(dsbs[1], qm[1])
        dvp_ref[pl.ds(start, WIN), :] += _dot_tn(pbs[0], dom[0]) + _dot_tn(pbs[1], dom[1])

        @pl.when(i == nq - 1)
        def _():
            dk_ref[...] = dkp_ref[PADK:PADK + S, :].astype(BF16)
            dv_ref[...] = dvp_ref[PADK:PADK + S, :].astype(BF16)
            a = lax.broadcasted_iota(jnp.int32, (TQ, TQ), 0)
            b = lax.broadcasted_iota(jnp.int32, (TQ, TQ), 1)
            flip = jnp.where(a + b == TQ - 1, 1.0, 0.0).astype(BF16)
            e = lax.broadcasted_iota(jnp.int32, (1, ROLL_W), 1)
            dg_ref[...] = jnp.zeros_like(dg_ref)
            for h in range(2):
                rev = _dot3_l(flip, db_ref[h])
                wide = jnp.concatenate([rev, jnp.zeros((TQ, ROLL_W - WIN), F32)], axis=1)
                diag = pltpu.roll(wide, 0, 1, stride=1, stride_axis=0)
                dg = jnp.sum(diag, axis=0, keepdims=True)
                lo = jnp.sum(jnp.where(e <= 639, dg, 0.0), axis=1, keepdims=True)
                hi = jnp.sum(jnp.where(e >= 895, dg, 0.0), axis=1, keepdims=True)
                dg_ref[h:h + 1, :] = jnp.where(e == 639, lo, jnp.where(e == 895, hi, dg))

    c0 = 3 * N_PAIR
    res = lambda p, i: (0, p)
    return _pallas(
        body, grid=(N_PAIR, nq), task=task,
        in_specs=[pl.BlockSpec((TQ, 128), lambda p, i: (i, c0 + p)),
                  pl.BlockSpec((S, 128), lambda p, i: (0, c0 + N_PAIR + p)),
                  pl.BlockSpec((S, 128), lambda p, i: (0, c0 + 2 * N_PAIR + p)),
                  pl.BlockSpec((TQ, 128), lambda p, i: (i, p)),
                  pl.BlockSpec((2, TQ, WIN), lambda p, i: (p, 0, 0))],
        out_specs=(pl.BlockSpec((TQ, 128), lambda p, i: (i, p)), pl.BlockSpec((S, 128), res),
                   pl.BlockSpec((S, 128), res), pl.BlockSpec((None, 8, ROLL_W), lambda p, i: (p, 0, 0))),
        out_shape=(jax.ShapeDtypeStruct((S, D_GRP), BF16), jax.ShapeDtypeStruct((S, D_GRP), BF16),
                   jax.ShapeDtypeStruct((S, D_GRP), BF16), jax.ShapeDtypeStruct((N_PAIR, 8, ROLL_W), F32)),
        scratch_shapes=[pltpu.VMEM((S + PADK, 128), BF16), pltpu.VMEM((S + PADK, 128), BF16),
                        pltpu.VMEM((S + PADK, 128), F32), pltpu.VMEM((S + PADK, 128), F32),
                        pltpu.VMEM((2, TQ, WIN), F32)],
        name=name,
    )(proj, proj, proj, do, bias)


def _mem_fwd_call(q, k, v, *, name, tq=2048):
    S = q.shape[0]
    scale = MEM_HD ** -0.5

    def body(q_ref, k_ref, v_ref, o_ref):
        s = _dot_nt(q_ref[...] * scale, k_ref[...])
        p = jnp.exp(s - jnp.max(s, axis=1, keepdims=True))
        p = p / jnp.sum(p, axis=1, keepdims=True)
        o_ref[...] = _dot(p.astype(BF16), v_ref[...]).astype(BF16)

    return pl.pallas_call(
        body, grid=(MEM_HEADS, S // tq),
        in_specs=[pl.BlockSpec((tq, MEM_HD), lambda h, i: (i, h)),
                  pl.BlockSpec((N_MEM, MEM_HD), lambda h, i: (0, h)),
                  pl.BlockSpec((N_MEM, MEM_HD), lambda h, i: (0, h))],
        out_specs=pl.BlockSpec((tq, MEM_HD), lambda h, i: (i, h)),
        out_shape=jax.ShapeDtypeStruct((S, D), BF16), name=name, compiler_params=_cparams(),
    )(q, k, v)


def _mem_bwd_call(q, k, v, do, *, name, tq=2048):
    S = q.shape[0]
    n = S // tq
    scale = MEM_HD ** -0.5

    def body(q_ref, k_ref, v_ref, do_ref, dq_ref, dk_ref, dv_ref, dka_ref, dva_ref):
        i = pl.program_id(1)

        @pl.when(i == 0)
        def _():
            dka_ref[...] = jnp.zeros_like(dka_ref)
            dva_ref[...] = jnp.zeros_like(dva_ref)
        qb = q_ref[...]
        kb = k_ref[...]
        dob = do_ref[...]
        s = _dot_nt(qb * scale, kb)
        p = jnp.exp(s - jnp.max(s, axis=1, keepdims=True))
        p = p / jnp.sum(p, axis=1, keepdims=True)
        dp = _dot_nt(dob, v_ref[...])
        ds = p * (dp - jnp.sum(p * dp, axis=1, keepdims=True))
        dsb = (ds * scale).astype(BF16)
        dq_ref[...] = _dot(dsb, kb).astype(BF16)
        dka_ref[...] += _dot_tn(dsb, qb)
        dva_ref[...] += _dot_tn(p.astype(BF16), dob)

        @pl.when(i == n - 1)
        def _():
            dk_ref[...] = dka_ref[...].astype(BF16)
            dv_ref[...] = dva_ref[...].astype(BF16)

    kv = pl.BlockSpec((N_MEM, MEM_HD), lambda h, i: (0, h))
    qs = pl.BlockSpec((tq, MEM_HD), lambda h, i: (i, h))
    return pl.pallas_call(
        body, grid=(MEM_HEADS, n), in_specs=[qs, kv, kv, qs], out_specs=(qs, kv, kv),
        out_shape=(jax.ShapeDtypeStruct((S, D), BF16), jax.ShapeDtypeStruct((N_MEM, D), BF16),
                   jax.ShapeDtypeStruct((N_MEM, D), BF16)),
        scratch_shapes=[pltpu.VMEM((N_MEM, MEM_HD), F32), pltpu.VMEM((N_MEM, MEM_HD), F32)],
        name=name, compiler_params=_cparams(),
    )(q, k, v, do)


def _rel_table_to_g(rel):
    return jnp.concatenate([
        jnp.broadcast_to(rel[:, N_REL - 1:N_REL], (8, 640)),
        rel[:, 1:N_REL - 1][:, ::-1],
        jnp.broadcast_to(rel[:, 0:1], (8, 129)),
    ], axis=1)


def _g_to_rel_table(dg):
    return dg[:, 639:896][:, ::-1]


def _place():
    x, y, c = lax.axis_index("x"), lax.axis_index("y"), lax.axis_index("c")
    others = [(1 - x, y), (x, 1 - y), (1 - x, 1 - y)]
    return x, y, c, others


def _half(c, rows):
    hr = rows // 2
    return pl.ds(pl.multiple_of(c * hr, 16), hr)


def _dma_sems(*shape):
    return pltpu.SemaphoreType.DMA(shape)


def _cast_slabs_call(ws, chip_arr, *, name, tm=256, task=None):
    n = len(ws)
    cols = ws[0].shape[1]
    tiles = [w.shape[0] // tm for w in ws]
    steps = max(tiles)

    def body(chip_ref, *refs):
        i = pl.program_id(0)
        for k in range(n):
            def cast(k=k):
                refs[n + k][...] = refs[k][...].astype(BF16)
            if tiles[k] == steps:
                cast()
            else:
                pl.when(i < tiles[k])(cast)

    in_specs = [pl.BlockSpec((tm, cols), lambda i, chip, t=t: (jnp.minimum(i, t - 1), 0)) for t in tiles]
    out_specs = [pl.BlockSpec((None, tm, cols), lambda i, chip, t=t: (chip[0], jnp.minimum(i, t - 1), 0)) for t in tiles]
    out_shape = [jax.ShapeDtypeStruct((N_CHIP,) + w.shape, BF16) for w in ws]
    return _pallas(body, grid=(steps,), in_specs=in_specs, out_specs=out_specs, out_shape=out_shape, name=name,
                   task=task, prefetch=1)(chip_arr, *ws)


def _cast_slab_call(w, chip_arr, *, name, tm=256, pad_rows=0):
    rows, cols = w.shape
    if pad_rows:
        tm = rows
    tm = min(tm, rows)

    def body(chip_ref, w_ref, o_ref):
        o_ref[0:tm, :] = w_ref[...].astype(BF16)
        if pad_rows:
            o_ref[tm:tm + pad_rows, :] = jnp.zeros((pad_rows, cols), BF16)

    return pl.pallas_call(
        body,
        grid_spec=pltpu.PrefetchScalarGridSpec(
            num_scalar_prefetch=1, grid=(rows // tm,),
            in_specs=[pl.BlockSpec((tm, cols), lambda i, chip: (i, 0))],
            out_specs=pl.BlockSpec((None, tm + pad_rows, cols), lambda i, chip: (chip[0], i, 0))),
        out_shape=jax.ShapeDtypeStruct((N_CHIP, rows + pad_rows, cols), BF16), name=name,
        compiler_params=_cparams(),
    )(chip_arr, w)


def _ag_ici_task(gathered):
    n = len(gathered)

    def copies(ins, outs, sems):
        send_sems, recv_sems = sems
        x, y, c, others = _place()
        me = 2 * x + y
        for k in range(n):
            mine = _half(c, gathered[k].shape[1])
            for t, (ox, oy) in enumerate(others):
                yield [pltpu.make_async_remote_copy(
                    src_ref=ins[k].at[me, mine], dst_ref=outs[k].at[slab, mine],
                    send_sem=send_sems.at[k, t], recv_sem=recv_sems.at[k, t],
                    device_id=(ox, oy, c), device_id_type=MESH) for slab in (me, 2 * ox + oy)]

    def issue(ins, outs, sems):
        for outgoing, _ in copies(ins, outs, sems):
            outgoing.start()

    def drain(ins, outs, sems):
        for outgoing, incoming in copies(ins, outs, sems):
            incoming.wait_recv()
            outgoing.wait_send()

    return _Task(gathered, [jax.ShapeDtypeStruct(g.shape, g.dtype) for g in gathered],
                 [_dma_sems(n, 3), _dma_sems(n, 3)], issue, drain, aliases={k: k for k in range(n)})


def _ag_d2d_task(gathered):
    n = len(gathered)

    def copies(ins, outs, sems):
        send_sems, recv_sems = sems
        x, y, c, others = _place()
        for k in range(n):
            rows = gathered[k].shape[1]
            mine, theirs = _half(c, rows), _half(1 - c, rows)
            for t, (ox, oy) in enumerate(others):
                slab = 2 * ox + oy
                pair = [pltpu.make_async_remote_copy(
                    src_ref=ins[k].at[slab, half], dst_ref=outs[k].at[slab, half],
                    send_sem=send_sems.at[k, t], recv_sem=recv_sems.at[k, t],
                    device_id=(x, y, 1 - c), device_id_type=MESH) for half in (mine, theirs)]
                yield pair

    def issue(ins, outs, sems):
        for outgoing, _ in copies(ins, outs, sems):
            outgoing.start()

    def drain(ins, outs, sems):
        for outgoing, incoming in copies(ins, outs, sems):
            incoming.wait_recv()
            outgoing.wait_send()

    return _Task(gathered, [jax.ShapeDtypeStruct(g.shape, g.dtype) for g in gathered],
                 [_dma_sems(n, 3), _dma_sems(n, 3)], issue, drain, aliases={k: k for k in range(n)})


def _rs_pair_task(ds):
    n = len(ds)

    def copies(ins, outs, sems):
        send_sems, recv_sems = sems
        x, y, c, _ = _place()
        for k in range(n):
            yield pltpu.make_async_remote_copy(
                src_ref=ins[k].at[:, _half(1 - c, ds[k].shape[1])], dst_ref=outs[k],
                send_sem=send_sems.at[k], recv_sem=recv_sems.at[k],
                device_id=(x, y, 1 - c), device_id_type=MESH)

    def issue(ins, outs, sems):
        for cp in copies(ins, outs, sems):
            cp.start()

    def drain(ins, outs, sems):
        for cp in copies(ins, outs, sems):
            cp.wait()

    return _Task(ds, [jax.ShapeDtypeStruct((N_CHIP, d.shape[1] // 2, d.shape[2]), d.dtype) for d in ds],
                 [_dma_sems(n), _dma_sems(n)], issue, drain)


def _pair_add_call(d, r1, c_arr, *, name, tm=512):
    _, rows, cols = d.shape
    hr = rows // 2
    tm = tm if hr % tm == 0 else hr
    nb = hr // tm

    def body(c_ref, d_ref, r_ref, o_ref):
        o_ref[...] = (d_ref[...].astype(F32) + r_ref[...].astype(F32)).astype(BF16)

    return pl.pallas_call(
        body,
        grid_spec=pltpu.PrefetchScalarGridSpec(
            num_scalar_prefetch=1, grid=(N_CHIP, nb),
            in_specs=[pl.BlockSpec((None, tm, cols), lambda j, i, c: (j, c[0] * nb + i, 0)),
                      pl.BlockSpec((None, tm, cols), lambda j, i, c: (j, i, 0))],
            out_specs=pl.BlockSpec((None, tm, cols), lambda j, i, c: (j, i, 0))),
        out_shape=jax.ShapeDtypeStruct((N_CHIP, hr, cols), BF16), name=name, compiler_params=_cparams(),
    )(c_arr, d, r1)


def _rs_chip_task(ps):
    n = len(ps)

    def copies(ins, outs, sems):
        send_sems, recv_sems = sems
        x, y, c, others = _place()
        for k in range(n):
            for t, (ox, oy) in enumerate(others):
                yield pltpu.make_async_remote_copy(
                    src_ref=ins[k].at[2 * ox + oy], dst_ref=outs[k].at[t],
                    send_sem=send_sems.at[k, t], recv_sem=recv_sems.at[k, t],
                    device_id=(ox, oy, c), device_id_type=MESH)

    def issue(ins, outs, sems):
        for cp in copies(ins, outs, sems):
            cp.start()

    def drain(ins, outs, sems):
        for cp in copies(ins, outs, sems):
            cp.wait()

    return _Task(ps, [jax.ShapeDtypeStruct((3,) + p.shape[1:], p.dtype) for p in ps],
                 [_dma_sems(n, 3), _dma_sems(n, 3)], issue, drain)


def _chip_sum_call(p, r2, place_arr, *, name, tm=512):
    _, hr, cols = r2.shape
    tm = tm if hr % tm == 0 else hr
    nb = hr // tm

    def body(place_ref, p_ref, r_ref, o_ref):
        acc = p_ref[...].astype(F32)
        for j in range(3):
            acc = acc + r_ref[j].astype(F32)
        o_ref[...] = acc

    return _pallas(
        body, grid=(nb,), prefetch=1,
        in_specs=[pl.BlockSpec((None, tm, cols), lambda i, pc: (pc[0], i, 0)),
                  pl.BlockSpec((3, tm, cols), lambda i, pc: (0, i, 0))],
        out_specs=pl.BlockSpec((tm, cols), lambda i, pc: (pc[1] * nb + i, 0)),
        out_shape=jax.ShapeDtypeStruct((2 * hr, cols), F32), name=name,
    )(place_arr, p, r2)


def _chip_sums_call(ps, r2s, place_arr, *, name, steps=2, task=None):
    n = len(ps)

    def body(place_ref, *refs):
        ins, outs = refs[:2 * n], refs[2 * n:]
        for k in range(n):
            acc = ins[2 * k][...].astype(F32)
            for j in range(3):
                acc = acc + ins[2 * k + 1][j].astype(F32)
            outs[k][...] = acc

    in_specs, out_specs, out_shape, args = [], [], [], []
    for p, r2 in zip(ps, r2s):
        _, hr, cols = r2.shape
        tm = hr // steps
        in_specs += [pl.BlockSpec((None, tm, cols), lambda i, pc: (pc[0], i, 0)),
                     pl.BlockSpec((3, tm, cols), lambda i, pc: (0, i, 0))]
        out_specs.append(pl.BlockSpec((tm, cols), lambda i, pc: (pc[1] * steps + i, 0)))
        out_shape.append(jax.ShapeDtypeStruct((2 * hr, cols), F32))
        args += [p, r2]
    return _pallas(body, grid=(steps,), prefetch=1, in_specs=in_specs, out_specs=out_specs, out_shape=out_shape,
                   name=name, task=task)(place_arr, *args)


def _rs_gather_task(gs):
    n = len(gs)

    def copies(ins, outs, sems):
        send_sems, recv_sems = sems
        x, y, c, _ = _place()
        for k in range(n):
            rows = gs[k].shape[0]
            mine, theirs = _half(c, rows), _half(1 - c, rows)
            yield [pltpu.make_async_remote_copy(
                src_ref=ins[k].at[mine], dst_ref=outs[k].at[half],
                send_sem=send_sems.at[k], recv_sem=recv_sems.at[k],
                device_id=(x, y, 1 - c), device_id_type=MESH) for half in (mine, theirs)]

    def issue(ins, outs, sems):
        for outgoing, _ in copies(ins, outs, sems):
            outgoing.start()

    def drain(ins, outs, sems):
        for outgoing, incoming in copies(ins, outs, sems):
            incoming.wait_recv()
            outgoing.wait_send()

    return _Task(gs, [jax.ShapeDtypeStruct(g.shape, g.dtype) for g in gs],
                 [_dma_sems(n), _dma_sems(n)], issue, drain, aliases={k: k for k in range(n)})


def _adamw(w, g, m, v):
    m = ADAM_B1 * m + (1.0 - ADAM_B1) * g
    v = ADAM_B2 * v + (1.0 - ADAM_B2) * jnp.square(g)
    m_hat = m / (1.0 - ADAM_B1 ** ADAM_STEP)
    v_hat = v / (1.0 - ADAM_B2 ** ADAM_STEP)
    delta = -ADAM_LR * (m_hat / (jnp.sqrt(v_hat) + ADAM_EPS) + ADAM_WD * w)
    return delta, m, v


def _adamw_call(items, *, name, tm=256, task=None):
    n = len(items)
    cols = items[0][0].shape[1]
    tiles = [it[0].shape[0] // tm for it in items]
    steps = max(tiles)

    def body(*refs):
        i = pl.program_id(0)
        ins, outs = refs[:4 * n], refs[4 * n:]
        for k in range(n):
            def update(k=k):
                g = ins[4 * k + 1][...]
                res = _adamw(ins[4 * k][...], g, ins[4 * k + 2][...], ins[4 * k + 3][...])
                outs[4 * k][...] = g
                for j in range(3):
                    outs[4 * k + 1 + j][...] = res[j]
            if tiles[k] == steps:
                update()
            else:
                pl.when(i < tiles[k])(update)

    in_specs, out_specs, out_shape, args = [], [], [], []
    for it, t in zip(items, tiles):
        spec = pl.BlockSpec((tm, cols), lambda i, t=t: (jnp.minimum(i, t - 1), 0))
        in_specs += [spec] * 4
        out_specs += [spec] * 4
        out_shape += [jax.ShapeDtypeStruct(it[0].shape, F32)] * 4
        args += list(it)
    res = _pallas(body, grid=(steps,), in_specs=in_specs, out_specs=out_specs, out_shape=out_shape,
                  name=name, task=task)(*args)
    outs, extra = res if task is not None else (res, None)
    grouped = [tuple(outs[4 * k:4 * k + 4]) for k in range(n)]
    return (grouped, extra) if task is not None else grouped


def _adamw_cols_call(w, g_pad, m, v, *, name, tn=512):
    rows, cols = w.shape

    def body(w_ref, g_ref, m_ref, v_ref, go_ref, d_ref, mo_ref, vo_ref):
        g = g_ref[0:rows, :]
        d, mn, vn = _adamw(w_ref[...], g, m_ref[...], v_ref[...])
        go_ref[...] = g
        d_ref[...] = d
        mo_ref[...] = mn
        vo_ref[...] = vn

    spec = pl.BlockSpec((rows, tn), lambda j: (0, j))
    gspec = pl.BlockSpec((g_pad.shape[0], tn), lambda j: (0, j))
    return _pallas(body, grid=(cols // tn,), in_specs=[spec, gspec, spec, spec], out_specs=(spec,) * 4,
                   out_shape=(jax.ShapeDtypeStruct((rows, cols), F32),) * 4, name=name)(w, g_pad, m, v)


N_DEV = 8
SMALL_ROWS = 24
SMALL_LAYOUT = {
    "g_mix_pre": (0, 0, 1, D), "g_mix_post": (1, 0, 1, D), "g_mem_kv": (2, 0, 1, D), "g_mem_pre": (3, 0, 1, D),
    "g_mem_post": (4, 0, 1, D), "g_ff_pre": (5, 0, 1, D), "g_ff_post": (6, 0, 1, D),
    "g_fox_out": (7, 0, 1, D_GRP), "g_chk_out": (7, D_GRP, 1, D_GRP), "b_fgt": (8, 0, 1, 8),
    "rel_bias": (16, 0, 8, N_REL),
}
SMALL = list(SMALL_LAYOUT)


LOSS_ROW = 9


def _small_reduce_call(grads, loss_blk, task, *, name):
    n = len(SMALL)
    t_in, t_out = len(task.arrays), len(task.out_shapes)

    def body(*refs):
        g_refs, loss_ref, tins = refs[:n], refs[n], refs[n + 1:n + 1 + t_in]
        p = n + 1 + t_in
        total_ref, loss_out, touts = refs[p], refs[p + 1], refs[p + 2:p + 2 + t_out]
        p += 2 + t_out
        mine, slots, send_sems, recv_sems = refs[p:p + 4]
        tsems = refs[p + 4:]
        task.issue(tins, touts, tsems)
        x, y, c, _ = _place()
        me = 4 * x + 2 * y + c
        mine[...] = jnp.zeros_like(mine)
        for k, name_k in enumerate(SMALL):
            r, l, nr, nl = SMALL_LAYOUT[name_k]
            mine[r:r + nr, l:l + nl] = g_refs[k][0:nr, 0:nl]
        mine[LOSS_ROW:LOSS_ROW + 1, 0:128] = loss_ref[0:1, :]
        slots[me] = mine[...]
        peers = [(dx, dy, dc) for dx in (0, 1) for dy in (0, 1) for dc in (0, 1)][1:]
        cps = []
        for t, (dx, dy, dc) in enumerate(peers):
            px, py, pc = (x + dx) % 2, (y + dy) % 2, (c + dc) % 2
            cps.append(pltpu.make_async_remote_copy(
                src_ref=mine, dst_ref=slots.at[me], send_sem=send_sems.at[t], recv_sem=recv_sems.at[t],
                device_id=(px, py, pc), device_id_type=MESH))
            cps[-1].start()
        for t, (dx, dy, dc) in enumerate(peers):
            px, py, pc = (x + dx) % 2, (y + dy) % 2, (c + dc) % 2
            pltpu.make_async_remote_copy(
                src_ref=mine, dst_ref=slots.at[4 * px + 2 * py + pc], send_sem=send_sems.at[t],
                recv_sem=recv_sems.at[t], device_id=(px, py, pc), device_id_type=MESH).wait_recv()
        for cp in cps:
            cp.wait_send()
        total = slots[0]
        for j in range(1, N_DEV):
            total = total + slots[j]
        total_ref[...] = total
        loss_out[...] = jnp.broadcast_to(total[LOSS_ROW:LOSS_ROW + 1, 0:128], loss_out.shape)
        task.drain(tins, touts, tsems)

    vm = pl.BlockSpec(memory_space=pltpu.VMEM)
    out_shape = [jax.ShapeDtypeStruct((SMALL_ROWS, D), F32), jax.ShapeDtypeStruct((8, 128), F32)] + list(task.out_shapes)
    res = pl.pallas_call(
        body, in_specs=[vm] * (n + 1) + [ANY] * t_in, out_specs=[vm] * 2 + [ANY] * t_out,
        out_shape=out_shape,
        scratch_shapes=[pltpu.VMEM((SMALL_ROWS, D), F32), pltpu.VMEM((N_DEV, SMALL_ROWS, D), F32),
                        _dma_sems(N_DEV - 1), _dma_sems(N_DEV - 1)] + list(task.sems),
        input_output_aliases={n + 1 + i: 2 + j for i, j in task.aliases.items()},
        name=name,
    )(*[grads[k] for k in SMALL], loss_blk, *task.arrays)
    return res[0], res[1], list(res[2:])


def _small_adamw_call(total, ws, ms, vs, *, name):
    n = len(SMALL)

    def body(*refs):
        total_ref = refs[0]
        w_refs, m_refs, v_refs = (refs[1 + j * n:1 + (j + 1) * n] for j in range(3))
        outs = refs[1 + 3 * n:]
        for k, name_k in enumerate(SMALL):
            r, l, nr, nl = SMALL_LAYOUT[name_k]
            g = total_ref[r:r + nr, l:l + nl]
            d, mn, vn = _adamw(w_refs[k][...], g, m_refs[k][...], v_refs[k][...])
            for j, val in enumerate((g, d, mn, vn)):
                outs[4 * k + j][...] = val

    vm = pl.BlockSpec(memory_space=pltpu.VMEM)
    res = pl.pallas_call(
        body, in_specs=[vm] * (3 * n + 1), out_specs=[vm] * (4 * n),
        out_shape=[jax.ShapeDtypeStruct(ws[k].shape, F32) for k in SMALL for _ in range(4)], name=name,
    )(total, *[d[k] for d in (ws, ms, vs) for k in SMALL])
    return {k: tuple(res[4 * i:4 * i + 4]) for i, k in enumerate(SMALL)}


WEIGHTS = ["w_in", "b_fgt", "rel_bias", "g_fox_out", "g_chk_out", "w_out", "g_mix_pre", "g_mix_post", "g_mem_kv",
           "w_mq", "w_mk", "w_mv", "w_mo", "g_mem_pre", "g_mem_post", "w_ff1", "w_ff2", "g_ff_pre", "g_ff_post"]
BIG = ["w_in", "w_out", "w_mq", "w_mk", "w_mv", "w_mo", "w_ff1", "w_ff2"]


IN_SHARD = D_IN // N_CHIP
IN_PAD = 800


IN_PIECES = [(0, 0, 770), (800, 770, 766), (1566, 3072, 4), (1600, 3076, 4), (1604, 1536, 766), (2400, 2302, 770)]
PAD_ZEROS = [(800 * j + IN_SHARD, IN_PAD - IN_SHARD) for j in range(N_CHIP)]
ALL_ZEROS = [(D_IN, D_ALL - D_IN)]


def _reorder_rows_call(src, to_all, *, name, tn=512):
    rows, cols = src.shape
    zeros = ALL_ZEROS if to_all else PAD_ZEROS

    def body(s_ref, o_ref):
        for pad0, all0, cnt in IN_PIECES:
            s0, d0 = (pad0, all0) if to_all else (all0, pad0)
            o_ref[d0:d0 + cnt, :] = s_ref[s0:s0 + cnt, :]
        for z0, cnt in zeros:
            o_ref[z0:z0 + cnt, :] = jnp.zeros((cnt, tn), src.dtype)

    spec = pl.BlockSpec((rows, tn), lambda j: (0, j))
    return _pallas(body, grid=(cols // tn,), in_specs=[spec], out_specs=spec,
                   out_shape=jax.ShapeDtypeStruct((rows, cols), src.dtype), name=name)(src)


def kernel(x, mem, w_in, b_fgt, rel_bias, g_fox_out, g_chk_out, w_out, g_mix_pre, g_mix_post, g_mem_kv, w_mq, w_mk, w_mv, w_mo, g_mem_pre, g_mem_post, w_ff1, w_ff2, g_ff_pre, g_ff_post, loss_target, m_w_in, m_b_fgt, m_rel_bias, m_g_fox_out, m_g_chk_out, m_w_out, m_g_mix_pre, m_g_mix_post, m_g_mem_kv, m_w_mq, m_w_mk, m_w_mv, m_w_mo, m_g_mem_pre, m_g_mem_post, m_w_ff1, m_w_ff2, m_g_ff_pre, m_g_ff_post, v_w_in, v_b_fgt, v_rel_bias, v_g_fox_out, v_g_chk_out, v_w_out, v_g_mix_pre, v_g_mix_post, v_g_mem_kv, v_w_mq, v_w_mk, v_w_mv, v_w_mo, v_g_mem_pre, v_g_mem_post, v_w_ff1, v_w_ff2, v_g_ff_pre, v_g_ff_post):
    w = dict(w_in=w_in, b_fgt=b_fgt, rel_bias=rel_bias, g_fox_out=g_fox_out, g_chk_out=g_chk_out, w_out=w_out,
             g_mix_pre=g_mix_pre, g_mix_post=g_mix_post, g_mem_kv=g_mem_kv, w_mq=w_mq, w_mk=w_mk, w_mv=w_mv,
             w_mo=w_mo, g_mem_pre=g_mem_pre, g_mem_post=g_mem_post, w_ff1=w_ff1, w_ff2=w_ff2, g_ff_pre=g_ff_pre,
             g_ff_post=g_ff_post)
    m = dict(w_in=m_w_in, b_fgt=m_b_fgt, rel_bias=m_rel_bias, g_fox_out=m_g_fox_out, g_chk_out=m_g_chk_out,
             w_out=m_w_out, g_mix_pre=m_g_mix_pre, g_mix_post=m_g_mix_post, g_mem_kv=m_g_mem_kv, w_mq=m_w_mq,
             w_mk=m_w_mk, w_mv=m_w_mv, w_mo=m_w_mo, g_mem_pre=m_g_mem_pre, g_mem_post=m_g_mem_post,
             w_ff1=m_w_ff1, w_ff2=m_w_ff2, g_ff_pre=m_g_ff_pre, g_ff_post=m_g_ff_post)
    v = dict(w_in=v_w_in, b_fgt=v_b_fgt, rel_bias=v_rel_bias, g_fox_out=v_g_fox_out, g_chk_out=v_g_chk_out,
             w_out=v_w_out, g_mix_pre=v_g_mix_pre, g_mix_post=v_g_mix_post, g_mem_kv=v_g_mem_kv, w_mq=v_w_mq,
             w_mk=v_w_mk, w_mv=v_w_mv, w_mo=v_w_mo, g_mem_pre=v_g_mem_pre, g_mem_post=v_g_mem_post,
             w_ff1=v_w_ff1, w_ff2=v_w_ff2, g_ff_pre=v_g_ff_pre, g_ff_post=v_g_ff_post)

    def rows(d, k):
        return d[k][0] if k == "rel_bias" else d[k]

    xs, mems, target = x[0], mem[0], loss_target[0]
    S = xs.shape[0]
    sp = {k: rows(w, k) for k in SMALL}
    b_pad = jnp.pad(sp["b_fgt"], ((0, 0), (0, 120)))
    chip = 2 * lax.axis_index("x") + lax.axis_index("y")
    chip_arr = jnp.reshape(chip, (1,)).astype(jnp.int32)
    c_arr = jnp.reshape(lax.axis_index("c"), (1,)).astype(jnp.int32)
    place_arr = jnp.concatenate([chip_arr, c_arr])
    w_in_t, m_in_t, v_in_t = w["w_in"][0].T, m["w_in"][0].T, v["w_in"][0].T
    slab = {"w_in": _cast_slab_call(w_in_t, chip_arr, name="cast_w_in", pad_rows=IN_PAD - IN_SHARD)}

    def gather_ici(names):
        return _ag_ici_task([slab[k] for k in names])

    def pair_add(k, d, r1):
        return _pair_add_call(d, r1, c_arr, name="rs_pair_add_" + k)

    rest, (g_in,) = _cast_slabs_call([w[k][0] for k in BIG[1:]], chip_arr, name="cast_rest",
                                     task=gather_ici(["w_in"]))
    slab.update(zip(BIG[1:], rest))
    h1, (g_in,) = _rms_fwd_call(xs, sp["g_mix_pre"], name="rms_mix_pre", task=_ag_d2d_task([g_in]))
    w_all_t = _reorder_rows_call(g_in.reshape(N_CHIP * IN_PAD, D), True, name="w_in_rows")
    proj, (g_out, g_mq) = _mm_nt(h1, w_all_t, "plain", rows=(0, 3072), name="mm_proj",
                                 task=gather_ici(["w_out", "w_mq"]))
    fl_raw = _mm_nt(h1, w_all_t, "plain", rows=(3072, 128), name="mm_gate", out_dtype=F32, tn=128)
    c_rep, c_t = _fox_prep_call(fl_raw, b_pad, name="fox_prep")
    bias = _chk_bias_call(_rel_table_to_g(sp["rel_bias"]), name="chk_bias")
    mid = ["w_mk", "w_mv", "w_mo", "w_ff1"]
    (yf, lse), got = _fox_fwd_call(proj, c_rep, c_t, name="fox_fwd",
                                   task=_merge_tasks([gather_ici(mid), _ag_d2d_task([g_out, g_mq])]))
    g_mid, (g_out, g_mq) = got[:4], got[4:]
    yc, got = _chk_fwd_call(proj, bias, name="chk_fwd",
                            task=_merge_tasks([gather_ici(["w_ff2"]), _ag_d2d_task(g_mid)]))
    g_ff2, (g_mk, g_mv, g_mo, g_ff1) = got[0], got[1:]
    yn = _mix_norm_fwd_call(yf, yc, sp["g_fox_out"], sp["g_chk_out"], name="mix_norm_fwd")
    z, (g_ff2,) = _mm_nn(yn, g_out, "rows", name="mm_out", out_dtype=F32, task=_ag_d2d_task([g_ff2]))
    x1, h2 = _post_pre_call(xs, z, sp["g_mix_post"], sp["g_mem_pre"], name="post_mix")
    memn = _rms_fwd_call(mems, sp["g_mem_kv"], name="rms_mem_kv")
    q2 = _mm_nn(h2, g_mq, "rows", name="mm_mq")
    k2 = _mm_nn(memn, g_mk, "rows", name="mm_mk")
    v2 = _mm_nn(memn, g_mv, "rows", name="mm_mv")
    o2 = _mem_fwd_call(q2, k2, v2, name="mem_fwd")
    y2 = _mm_nn(o2, g_mo, "rows", name="mm_mo", out_dtype=F32)
    x2, h3 = _post_pre_call(x1, y2, sp["g_mem_post"], sp["g_ff_pre"], name="post_mem")
    act, relu = _mm_nn(h3, g_ff1, "cols", name="mm_ff1", epi="relu2")
    y3 = _mm_nn(act, g_ff2, "rows", name="mm_ff2", out_dtype=F32, tm=1024)
    loss_blk, dx3, dy3, dg_ff_post = _final_call(x2, y3, sp["g_ff_post"], target, name="final")

    d_ff2 = _mm_tn(act, dy3, name="mm_dff2", tk=512, tn=1024).reshape(N_CHIP, D_FF // N_CHIP, D)
    du, (r1,) = _mm_nt(dy3, g_ff2, "rows", name="mm_du", mul2r=relu, task=_rs_pair_task([d_ff2]))
    p_ff2 = pair_add("w_ff2", d_ff2, r1)
    d_ff1 = _mm_tn(h3, du, name="mm_dff1", cols4=True)
    dh3, (r1,) = _mm_nt(du, g_ff1, "cols", name="mm_dh3", out_dtype=F32, tm=1024, task=_rs_pair_task([d_ff1]))
    p_ff1 = pair_add("w_ff1", d_ff1, r1)
    dx2, dy2, dg_ff_pre, dg_mem_post = _bwd_mid_call(dx3, x2, dh3, sp["g_ff_pre"], y2, sp["g_mem_post"], name="bwd_ff")
    d_mo = _mm_tn(o2, dy2, name="mm_dmo").reshape(N_CHIP, D // N_CHIP, D)
    do2 = _mm_nt(dy2, g_mo, "rows", name="mm_do2")
    dq2, dk2, dv2 = _mem_bwd_call(q2, k2, v2, do2, name="mem_bwd")
    d_mq = _mm_tn(h2, dq2, name="mm_dmq").reshape(N_CHIP, D // N_CHIP, D)
    dh2 = _mm_nt(dq2, g_mq, "rows", name="mm_dh2", out_dtype=F32)
    d_mk = _mm_tn(memn, dk2, name="mm_dmk").reshape(N_CHIP, D // N_CHIP, D)
    d_mv = _mm_tn(memn, dv2, name="mm_dmv").reshape(N_CHIP, D // N_CHIP, D)
    dmn_k = _mm_nt(dk2, g_mk, "rows", name="mm_dmemk", out_dtype=F32)
    dmn_v = _mm_nt(dv2, g_mv, "rows", name="mm_dmemv", out_dtype=F32)
    dg_mem_kv = _gain_grad_call(mems, sp["g_mem_kv"], dmn_k, dmn_v, name="gain_mem_kv")
    dx1, dz, dg_mem_pre, dg_mix_post = _bwd_mid_call(dx2, x1, dh2, sp["g_mem_pre"], z, sp["g_mix_post"], name="bwd_mem")
    d_out = _mm_tn(yn, dz, name="mm_dout").reshape(N_CHIP, D // N_CHIP, D)
    late = ["w_mo", "w_mq", "w_mk", "w_mv", "w_out"]
    d_late = [d_mo, d_mq, d_mk, d_mv, d_out]
    dyn, r1_late = _mm_nt(dz, g_out, "rows", name="mm_dyn", out_dtype=F32, task=_rs_pair_task(d_late))
    p_late = [pair_add(k, d, r1) for k, d, r1 in zip(late, d_late, r1_late)]
    dof, doc, delta, dg_fox, dg_chk = _mix_norm_bwd_call(dyn, yf, yc, sp["g_fox_out"], sp["g_chk_out"], name="mix_norm_bwd")
    (dqf, dkf, dvf, dcq, dck), r2_ff = _fox_bwd_call(proj, dof, lse, delta, c_rep, c_t, name="fox_bwd",
                                                      task=_rs_chip_task([p_ff2, p_ff1]))
    (dqc, dkc, dvc, dgrev), r2_late = _chk_bwd_call(proj, doc, bias, name="chk_bwd", task=_rs_chip_task(p_late))
    first = ["w_ff2", "w_ff1"] + late
    dc8 = (dcq[:, :, 0:2, :] + dck[:, :, 0:2, :]).transpose(0, 2, 1, 3).reshape(8, S)
    dc_rows = jnp.concatenate([dc8, jnp.zeros((120, S), F32)], axis=0)
    dfl, db_fgt = _fox_gate_bwd_call(dc_rows, fl_raw, b_pad, name="fox_gate_bwd")
    dproj = jnp.concatenate([dqf, dkf, dvf, dqc, dkc, dvc, dfl], axis=1)
    d_all_t = _mm_tn(dproj, h1, name="mm_dwin", tk=640, tn=1024)
    d_in = _reorder_rows_call(d_all_t, False, name="d_in_rows").reshape(N_CHIP, IN_PAD, D)
    f_first, (r1,) = _chip_sums_call([p_ff2, p_ff1] + p_late, r2_ff + r2_late, place_arr, name="rs_chip_sums",
                                     task=_rs_pair_task([d_in]))
    p_in = pair_add("w_in", d_in, r1)
    dh1, got = _mm_nn(dproj, w_all_t, "plain", name="mm_dh1", out_dtype=F32, tm=1024,
                      task=_merge_tasks([_rs_chip_task([p_in]), _rs_gather_task(f_first)]))
    r2_in, grads = got[0], dict(zip(first, got[1:]))
    f_in = _chip_sum_call(p_in, r2_in, place_arr, name="rs_chip_sum_w_in")
    delta_w, new_m, new_v = {}, {}, {}

    def adamw_items(names):
        return [(w[k][0], grads[k], m[k][0], v[k][0]) for k in names]

    upd_late = _adamw_call(adamw_items(late), name="adamw_late", tm=64)
    upd_ff = _adamw_call(adamw_items(first[:2]), name="adamw_ff")
    for k, res in zip(late + first[:2], upd_late + upd_ff):
        grads[k], delta_w[k], new_m[k], new_v[k] = res
    grad_x, dg_mix_pre = _bwd_last_call(dx1, xs, dh1, sp["g_mix_pre"], name="bwd_mix")

    small_g = {"g_mix_pre": dg_mix_pre, "g_mix_post": dg_mix_post, "g_mem_kv": dg_mem_kv, "g_mem_pre": dg_mem_pre,
               "g_mem_post": dg_mem_post, "g_ff_pre": dg_ff_pre, "g_ff_post": dg_ff_post, "g_fox_out": dg_fox,
               "g_chk_out": dg_chk, "b_fgt": db_fgt,
               "rel_bias": _g_to_rel_table(dgrev[:, 0:2, :].reshape(8, ROLL_W))}
    small_sum, loss_out, (g_w_in,) = _small_reduce_call(small_g, loss_blk, _rs_gather_task([f_in]),
                                                       name="small_allreduce")
    small = _small_adamw_call(small_sum, sp, {k: rows(m, k) for k in SMALL}, {k: rows(v, k) for k in SMALL},
                              name="small_adamw")
    loss = loss_out[0, 0]
    res = _adamw_cols_call(w_in_t, g_w_in, m_in_t, v_in_t, name="adamw_w_in")
    grads["w_in"], delta_w["w_in"], new_m["w_in"], new_v["w_in"] = (a.T for a in res)
    for k in SMALL:
        vals = small[k]
        if k == "rel_bias":
            vals = tuple(a[None] for a in vals)
        grads[k], delta_w[k], new_m[k], new_v[k] = vals

    def out(d, k):
        return d[k][None] if k in BIG else d[k]

    return (loss, grad_x[None], *[out(grads, k) for k in WEIGHTS], *[out(delta_w, k) for k in WEIGHTS],
            *[out(new_m, k) for k in WEIGHTS], *[out(new_v, k) for k in WEIGHTS])
```

```python
import functools

import jax
import jax.numpy as jnp
from jax import lax
from jax.experimental import pallas as pl
from jax.experimental.pallas import tpu as pltpu

F32 = jnp.float32
BF16 = jnp.bfloat16

D = 1024
HEAD = 64
N_PAIR = 4
D_GRP = 512
CHUNK = 64
LEFT = 8
MAX_REL = 128
N_REL = 2 * MAX_REL + 1
N_MEM = 256
MEM_HEADS = 4
MEM_HD = 256
D_FF = 4096
D_IN = 3080
D_ALL = 3200
EPS = 1e-6
TQ = 256
WIN = (LEFT + TQ // CHUNK) * CHUNK
PADK = LEFT * CHUNK
ROLL_W = 1024
NEG = -1e30
N_CHIP = 4
VMEM_LIMIT = 48 * 1024 * 1024

ADAM_LR = 0.001
ADAM_B1 = 0.9
ADAM_B2 = 0.999
ADAM_EPS = 1e-08
ADAM_WD = 0.01
ADAM_STEP = 10

MESH = pl.DeviceIdType.MESH


def _cparams():
    return pltpu.CompilerParams(vmem_limit_bytes=VMEM_LIMIT)


ANY = pl.BlockSpec(memory_space=pl.ANY)


class _Task:
    def __init__(self, arrays, out_shapes, sems, issue, drain, aliases=None):
        self.arrays, self.out_shapes, self.sems = list(arrays), list(out_shapes), list(sems)
        self.issue, self.drain, self.aliases = issue, drain, dict(aliases or {})


def _merge_tasks(tasks):
    tasks = [t for t in tasks if t is not None]
    if len(tasks) == 1:
        return tasks[0]
    cuts, a, o, s = [], 0, 0, 0
    aliases = {}
    for t in tasks:
        cuts.append((a, o, s))
        aliases.update({a + i: o + j for i, j in t.aliases.items()})
        a, o, s = a + len(t.arrays), o + len(t.out_shapes), s + len(t.sems)

    def part(fn_name):
        def run(ins, outs, sems):
            for t, (a0, o0, s0) in zip(tasks, cuts):
                getattr(t, fn_name)(ins[a0:a0 + len(t.arrays)], outs[o0:o0 + len(t.out_shapes)],
                                    sems[s0:s0 + len(t.sems)])
        return run

    return _Task([x for t in tasks for x in t.arrays], [x for t in tasks for x in t.out_shapes],
                 [x for t in tasks for x in t.sems], part("issue"), part("drain"), aliases)


def _pallas(body, *, grid, in_specs, out_specs, out_shape, name, scratch_shapes=(), task=None, prefetch=0):
    def make(kernel, i_specs, o_specs, o_shape, scratch, aliases):
        if prefetch:
            spec = pltpu.PrefetchScalarGridSpec(num_scalar_prefetch=prefetch, grid=grid, in_specs=i_specs,
                                                out_specs=o_specs, scratch_shapes=scratch)
            return pl.pallas_call(kernel, grid_spec=spec, out_shape=o_shape, input_output_aliases=aliases,
                                  name=name, compiler_params=_cparams())
        return pl.pallas_call(kernel, grid=grid, in_specs=i_specs, out_specs=o_specs, out_shape=o_shape,
                              scratch_shapes=scratch, input_output_aliases=aliases, name=name,
                              compiler_params=_cparams())

    if task is None:
        return make(body, list(in_specs), out_specs, out_shape, list(scratch_shapes), {})
    single = not isinstance(out_shape, (tuple, list))
    o_shapes = [out_shape] if single else list(out_shape)
    o_specs = [out_specs] if single else list(out_specs)
    n_in, n_out, n_scr = len(in_specs), len(o_shapes), len(scratch_shapes)
    t_in, t_out = len(task.arrays), len(task.out_shapes)

    def carried(*refs):
        cut = [prefetch, n_in, t_in, n_out, t_out, n_scr]
        parts, p = [], 0
        for c in cut:
            parts.append(refs[p:p + c])
            p += c
        scalars, ins, tins, outs, touts, scr = parts
        tsems = refs[p:]
        ids = [pl.program_id(a) for a in range(len(grid))]
        first = functools.reduce(jnp.logical_and, [i == 0 for i in ids])
        last = functools.reduce(jnp.logical_and, [i == g - 1 for i, g in zip(ids, grid)])

        @pl.when(first)
        def _():
            task.issue(tins, touts, tsems)
        body(*scalars, *ins, *outs, *scr)

        @pl.when(last)
        def _():
            task.drain(tins, touts, tsems)

    call = make(carried, list(in_specs) + [ANY] * t_in, o_specs + [ANY] * t_out,
                o_shapes + list(task.out_shapes), list(scratch_shapes) + list(task.sems),
                {prefetch + n_in + i: n_out + j for i, j in task.aliases.items()})

    def run(*args):
        res = call(*args, *task.arrays)
        outs = res[:n_out]
        return (outs[0] if single else tuple(outs)), list(res[n_out:])

    return run


def _dot(a, b):
    return jnp.dot(a, b, preferred_element_type=F32)


def _dot_nt(a, b):
    return lax.dot_general(a, b, (((1,), (1,)), ((), ())), preferred_element_type=F32)


def _dot_tn(a, b):
    return lax.dot_general(a, b, (((0,), (0,)), ((), ())), preferred_element_type=F32)


def _split3(x):
    hi = x.astype(BF16)
    r1 = x - hi.astype(F32)
    mid = r1.astype(BF16)
    lo = (r1 - mid.astype(F32)).astype(BF16)
    return hi, mid, lo


def _dot3(x, m01):
    hi, mid, lo = _split3(x)
    return _dot(hi, m01) + _dot(mid, m01) + _dot(lo, m01)


def _dot3_l(m01, x):
    hi, mid, lo = _split3(x)
    return _dot(m01, hi) + _dot(m01, mid) + _dot(m01, lo)


def _mm_nn(a, b, kind, *, name, out_dtype=BF16, tm=2048, tn=512, epi=None, task=None):
    M, K = a.shape
    if kind == "plain":
        N = b.shape[1]
        b_spec = pl.BlockSpec((K, tn), lambda m, n: (0, n))
    elif kind == "rows":
        N = b.shape[2]
        b_spec = pl.BlockSpec((N_CHIP, K // N_CHIP, tn), lambda m, n: (0, 0, n))
    else:
        nq = b.shape[2]
        N = N_CHIP * nq
        per = nq // tn
        b_spec = pl.BlockSpec((None, K, tn), lambda m, n: (n // per, 0, n % per))
    tm = min(tm, M)
    kq = K // N_CHIP

    def body(a_ref, b_ref, *o_refs):
        if kind == "rows":
            acc = _dot(a_ref[:, 0:kq], b_ref[0])
            for j in range(1, N_CHIP):
                acc += _dot(a_ref[:, j * kq:(j + 1) * kq], b_ref[j])
        else:
            acc = _dot(a_ref[...], b_ref[...])
        if epi == "relu2":
            r = jnp.maximum(acc, 0.0)
            o_refs[0][...] = (r * r).astype(BF16)
            o_refs[1][...] = r.astype(BF16)
        else:
            o_refs[0][...] = acc.astype(out_dtype)

    o_spec = pl.BlockSpec((tm, tn), lambda m, n: (m, n))
    if epi == "relu2":
        out_shape = (jax.ShapeDtypeStruct((M, N), BF16), jax.ShapeDtypeStruct((M, N), BF16))
        out_specs = (o_spec, o_spec)
    else:
        out_shape = jax.ShapeDtypeStruct((M, N), out_dtype)
        out_specs = o_spec
    return _pallas(
        body, grid=(M // tm, N // tn),
        in_specs=[pl.BlockSpec((tm, K), lambda m, n: (m, 0)), b_spec],
        out_specs=out_specs, out_shape=out_shape, name=name, task=task,
    )(a, b)


def _mm_nt(a, b, kind, *, name, out_dtype=BF16, tm=2048, tn=512, mul2r=None, task=None, rows=None):
    M, K = a.shape
    if kind == "plain":
        first, N = rows if rows is not None else (0, b.shape[0])
        n0 = first // tn
        b_spec = pl.BlockSpec((tn, K), lambda m, n: (n0 + n, 0))
    elif kind == "rows":
        nq = b.shape[1]
        N = N_CHIP * nq
        tn = min(tn, nq)
        per = nq // tn
        b_spec = pl.BlockSpec((None, tn, K), lambda m, n: (n // per, n % per, 0))
    else:
        N = b.shape[1]
        b_spec = pl.BlockSpec((N_CHIP, tn, K // N_CHIP), lambda m, n: (0, n, 0))
    tm = min(tm, M)
    kq = K // N_CHIP

    def body(a_ref, b_ref, *rest):
        o_ref = rest[-1]
        if kind == "cols":
            acc = _dot_nt(a_ref[:, 0:kq], b_ref[0])
            for j in range(1, N_CHIP):
                acc += _dot_nt(a_ref[:, j * kq:(j + 1) * kq], b_ref[j])
        else:
            acc = _dot_nt(a_ref[...], b_ref[...])
        if mul2r is not None:
            acc = acc * (2.0 * rest[0][...].astype(F32))
        o_ref[...] = acc.astype(out_dtype)

    in_specs = [pl.BlockSpec((tm, K), lambda m, n: (m, 0)), b_spec]
    args = [a, b]
    if mul2r is not None:
        in_specs.append(pl.BlockSpec((tm, tn), lambda m, n: (m, n)))
        args.append(mul2r)
    return _pallas(
        body, grid=(M // tm, N // tn), in_specs=in_specs,
        out_specs=pl.BlockSpec((tm, tn), lambda m, n: (m, n)),
        out_shape=jax.ShapeDtypeStruct((M, N), out_dtype), name=name, task=task,
    )(*args)


def _mm_tn(a, b, *, name, out_dtype=BF16, tk=1024, tn=512, cols4=False, task=None):
    M, K1 = a.shape
    N = b.shape[1]
    tk = min(tk, K1)
    tn = min(tn, N)

    def body(a_ref, b_ref, o_ref):
        o_ref[...] = _dot_tn(a_ref[...], b_ref[...]).astype(out_dtype)

    if cols4:
        per = (N // N_CHIP) // tn
        out_shape = jax.ShapeDtypeStruct((N_CHIP, K1, N // N_CHIP), out_dtype)
        o_spec = pl.BlockSpec((None, tk, tn), lambda k, n: (n // per, k, n % per))
    else:
        out_shape = jax.ShapeDtypeStruct((K1, N), out_dtype)
        o_spec = pl.BlockSpec((tk, tn), lambda k, n: (k, n))
    return _pallas(
        body, grid=(K1 // tk, N // tn),
        in_specs=[pl.BlockSpec((M, tk), lambda k, n: (0, k)), pl.BlockSpec((M, tn), lambda k, n: (0, n))],
        out_specs=o_spec, out_shape=out_shape, name=name, task=task,
    )(a, b)


def _rms(x, g):
    r = lax.rsqrt(jnp.mean(x * x, axis=-1, keepdims=True) + EPS)
    return x * r * g


def _rms_bwd(x, g, dy):
    r = lax.rsqrt(jnp.mean(x * x, axis=-1, keepdims=True) + EPS)
    xh = x * r
    dg = jnp.sum(dy * xh, axis=0, keepdims=True)
    dxh = dy * g
    dx = r * (dxh - xh * jnp.mean(dxh * xh, axis=-1, keepdims=True))
    return dx, dg


def _row_spec(tm, n):
    return pl.BlockSpec((tm, n), lambda i: (i, 0))


def _vec_spec(n):
    return pl.BlockSpec((1, n), lambda i: (0, 0))


def _acc_spec(n):
    return pl.BlockSpec((8, n), lambda i: (0, 0))


def _acc_add(ref, row, i):
    @pl.when(i == 0)
    def _():
        ref[...] = jnp.zeros_like(ref)
    ref[0:1, :] += row


def _rms_fwd_call(x, g, *, name, tm=1024, task=None):
    M, n = x.shape
    tm = min(tm, M)

    def body(x_ref, g_ref, h_ref):
        h_ref[...] = _rms(x_ref[...], g_ref[...]).astype(BF16)

    return _pallas(
        body, grid=(M // tm,), in_specs=[_row_spec(tm, n), _vec_spec(n)], out_specs=_row_spec(tm, n),
        out_shape=jax.ShapeDtypeStruct((M, n), BF16), name=name, task=task,
    )(x, g)


def _post_pre_call(xres, z, g_post, g_pre, *, name, tm=512):
    M, n = xres.shape
    deep = pl.BlockSpec((tm, n), lambda i: (i, 0), pipeline_mode=pl.Buffered(3))

    def body(x_hbm, z_hbm, gp_ref, gn_ref, xo_hbm, h_hbm):
        def tile(x_ref, z_ref, xo_ref, h_ref):
            xn = x_ref[...] + _rms(z_ref[...], gp_ref[...])
            xo_ref[...] = xn
            h_ref[...] = _rms(xn, gn_ref[...]).astype(BF16)

        pltpu.emit_pipeline(tile, grid=(M // tm,), in_specs=[deep] * 2,
                            out_specs=[_row_spec(tm, n), _row_spec(tm, n)])(x_hbm, z_hbm, xo_hbm, h_hbm)

    vm = pl.BlockSpec(memory_space=pltpu.VMEM)
    return pl.pallas_call(
        body, in_specs=[ANY, ANY, vm, vm], out_specs=(ANY, ANY),
        out_shape=(jax.ShapeDtypeStruct((M, n), F32), jax.ShapeDtypeStruct((M, n), BF16)),
        name=name, compiler_params=_cparams(),
    )(xres, z, g_post, g_pre)


def _final_call(x2, y3, g_post, target, *, name, tm=512):
    M, n = x2.shape

    deep = pl.BlockSpec((tm, n), lambda i: (i, 0), pipeline_mode=pl.Buffered(3))

    def body(x_hbm, y_hbm, g_ref, t_hbm, loss_ref, dx_hbm, dy_hbm, dg_ref):
        loss_ref[...] = jnp.zeros_like(loss_ref)
        dg_ref[...] = jnp.zeros_like(dg_ref)

        def tile(x_ref, y_ref, t_ref, dx_ref, dy_ref):
            y = y_ref[...]
            g = g_ref[...]
            diff = x_ref[...] + _rms(y, g) - t_ref[...]
            part = 0.5 * jnp.sum(jnp.sum(diff * diff, axis=1, keepdims=True), axis=0, keepdims=True) / n
            loss_ref[...] += jnp.broadcast_to(part, loss_ref.shape)
            dx = diff / n
            dx_ref[...] = dx
            dy, dg = _rms_bwd(y, g, dx)
            dy_ref[...] = dy.astype(BF16)
            dg_ref[0:1, :] += dg

        pltpu.emit_pipeline(tile, grid=(M // tm,), in_specs=[deep] * 3,
                            out_specs=[_row_spec(tm, n), _row_spec(tm, n)])(x_hbm, y_hbm, t_hbm, dx_hbm, dy_hbm)

    vm = pl.BlockSpec(memory_space=pltpu.VMEM)
    return pl.pallas_call(
        body, in_specs=[ANY, ANY, vm, ANY], out_specs=(vm, ANY, ANY, vm),
        out_shape=(jax.ShapeDtypeStruct((8, 128), F32), jax.ShapeDtypeStruct((M, n), F32),
                   jax.ShapeDtypeStruct((M, n), BF16), jax.ShapeDtypeStruct((8, n), F32)),
        name=name, compiler_params=_cparams(),
    )(x2, y3, g_post, target)


def _bwd_mid_call(dx_in, x, dh, g_pre, y, g_post, *, name, tm=512):
    M, n = x.shape

    deep = pl.BlockSpec((tm, n), lambda i: (i, 0), pipeline_mode=pl.Buffered(3))

    def body(dxi_hbm, x_hbm, dh_hbm, gpre_ref, y_hbm, gpost_ref, dx_hbm, dy_hbm, dgpre_ref, dgpost_ref):
        dgpre_ref[...] = jnp.zeros_like(dgpre_ref)
        dgpost_ref[...] = jnp.zeros_like(dgpost_ref)

        def tile(dxi_ref, x_ref, dh_ref, y_ref, dx_ref, dy_ref):
            d1, dg1 = _rms_bwd(x_ref[...], gpre_ref[...], dh_ref[...])
            dx = dxi_ref[...] + d1
            dx_ref[...] = dx
            dy, dg2 = _rms_bwd(y_ref[...], gpost_ref[...], dx)
            dy_ref[...] = dy.astype(BF16)
            dgpre_ref[0:1, :] += dg1
            dgpost_ref[0:1, :] += dg2

        pltpu.emit_pipeline(tile, grid=(M // tm,), in_specs=[deep] * 4,
                            out_specs=[_row_spec(tm, n), _row_spec(tm, n)])(
            dxi_hbm, x_hbm, dh_hbm, y_hbm, dx_hbm, dy_hbm)

    vm = pl.BlockSpec(memory_space=pltpu.VMEM)
    return pl.pallas_call(
        body, in_specs=[ANY, ANY, ANY, vm, ANY, vm], out_specs=(ANY, ANY, vm, vm),
        out_shape=(jax.ShapeDtypeStruct((M, n), F32), jax.ShapeDtypeStruct((M, n), BF16),
                   jax.ShapeDtypeStruct((8, n), F32), jax.ShapeDtypeStruct((8, n), F32)),
        name=name, compiler_params=_cparams(),
    )(dx_in, x, dh, g_pre, y, g_post)


def _bwd_last_call(dx_in, x, dh, g_pre, *, name, tm=1024, task=None):
    M, n = x.shape

    def body(dxi_ref, x_ref, dh_ref, g_ref, dx_ref, dg_ref):
        i = pl.program_id(0)
        d1, dg1 = _rms_bwd(x_ref[...], g_ref[...], dh_ref[...])
        dx_ref[...] = dxi_ref[...] + d1
        _acc_add(dg_ref, dg1, i)

    return _pallas(
        body, grid=(M // tm,),
        in_specs=[_row_spec(tm, n), _row_spec(tm, n), _row_spec(tm, n), _vec_spec(n)],
        out_specs=(_row_spec(tm, n), _acc_spec(n)),
        out_shape=(jax.ShapeDtypeStruct((M, n), F32), jax.ShapeDtypeStruct((8, n), F32)),
        name=name, task=task,
    )(dx_in, x, dh, g_pre)


def _gain_grad_call(x, g, dy_a, dy_b, *, name):
    M, n = x.shape

    def body(x_ref, g_ref, a_ref, b_ref, dg_ref):
        _, dg = _rms_bwd(x_ref[...], g_ref[...], a_ref[...] + b_ref[...])
        dg_ref[...] = jnp.zeros_like(dg_ref)
        dg_ref[0:1, :] = dg

    return pl.pallas_call(
        body, grid=(1,),
        in_specs=[_row_spec(M, n), _vec_spec(n), _row_spec(M, n), _row_spec(M, n)],
        out_specs=_acc_spec(n), out_shape=jax.ShapeDtypeStruct((8, n), F32),
        name=name, compiler_params=_cparams(),
    )(x, g, dy_a, dy_b)


def _head_group_matrix():
    a = lax.broadcasted_iota(jnp.int32, (D_GRP, D_GRP), 0) // HEAD
    b = lax.broadcasted_iota(jnp.int32, (D_GRP, D_GRP), 1) // HEAD
    return jnp.where(a == b, 1.0, 0.0).astype(BF16)


def _mix_norm_fwd_call(yf, yc, gf, gc, *, name, tm=1024):
    M = yf.shape[0]

    def body(yf_ref, yc_ref, gf_ref, gc_ref, o_ref):
        o_ref[:, 0:D_GRP] = _rms(yf_ref[...], gf_ref[...]).astype(BF16)
        o_ref[:, D_GRP:D] = _rms(yc_ref[...], gc_ref[...]).astype(BF16)

    return pl.pallas_call(
        body, grid=(M // tm,),
        in_specs=[_row_spec(tm, D_GRP), _row_spec(tm, D_GRP), _vec_spec(D_GRP), _vec_spec(D_GRP)],
        out_specs=_row_spec(tm, D), out_shape=jax.ShapeDtypeStruct((M, D), BF16),
        name=name, compiler_params=_cparams(),
    )(yf, yc, gf, gc)


def _mix_norm_bwd_call(dyn, yf, yc, gf, gc, *, name, tm=2 * TQ):
    M = yf.shape[0]

    def body(dyn_ref, yf_ref, yc_ref, gf_ref, gc_ref, dof_ref, doc_ref, delta_ref, dgf_ref, dgc_ref):
        i = pl.program_id(0)
        yf_ = yf_ref[...]
        dof, dgf = _rms_bwd(yf_, gf_ref[...], dyn_ref[:, 0:D_GRP])
        doc, dgc = _rms_bwd(yc_ref[...], gc_ref[...], dyn_ref[:, D_GRP:D])
        dof_b = dof.astype(BF16)
        dof_ref[...] = dof_b
        doc_ref[...] = doc.astype(BF16)
        prod = dof_b.astype(F32) * yf_
        hi = prod.astype(BF16)
        lo = (prod - hi.astype(F32)).astype(BF16)
        grp = _head_group_matrix()
        delta = _dot(hi, grp) + _dot(lo, grp)
        for b in range(tm // TQ):
            delta_ref[b] = delta[b * TQ:(b + 1) * TQ, :].T
        _acc_add(dgf_ref, dgf, i)
        _acc_add(dgc_ref, dgc, i)

    return pl.pallas_call(
        body, grid=(M // tm,),
        in_specs=[_row_spec(tm, D), _row_spec(tm, D_GRP), _row_spec(tm, D_GRP), _vec_spec(D_GRP), _vec_spec(D_GRP)],
        out_specs=(_row_spec(tm, D_GRP), _row_spec(tm, D_GRP),
                   pl.BlockSpec((tm // TQ, D_GRP, TQ), lambda i: (i, 0, 0)), _acc_spec(D_GRP), _acc_spec(D_GRP)),
        out_shape=(jax.ShapeDtypeStruct((M, D_GRP), BF16), jax.ShapeDtypeStruct((M, D_GRP), BF16),
                   jax.ShapeDtypeStruct((M // TQ, D_GRP, TQ), F32), jax.ShapeDtypeStruct((8, D_GRP), F32),
                   jax.ShapeDtypeStruct((8, D_GRP), F32)),
        name=name, compiler_params=_cparams(),
    )(dyn, yf, yc, gf, gc)


def _tri(n, lower_incl):
    a = lax.broadcasted_iota(jnp.int32, (n, n), 0)
    b = lax.broadcasted_iota(jnp.int32, (n, n), 1)
    return jnp.where(a >= b, 1.0, 0.0).astype(BF16) if lower_incl else jnp.where(a <= b, 1.0, 0.0).astype(BF16)


def _fox_prep_call(fl_raw, b_pad, *, name):
    S = fl_raw.shape[0]
    nb = S // TQ

    def body(fl_ref, b_ref, crep_ref, ct_ref, carry_ref):
        i = pl.program_id(0)

        @pl.when(i == 0)
        def _():
            carry_ref[...] = jnp.zeros_like(carry_ref)
        logf = jax.nn.log_sigmoid(fl_ref[...] + b_ref[...])
        cb = _dot3_l(_tri(TQ, True), logf) + carry_ref[0:1, :]
        carry_ref[0:1, :] = cb[TQ - 1:TQ, :]
        a = lax.broadcasted_iota(jnp.int32, (128, D_GRP), 0)
        b = lax.broadcasted_iota(jnp.int32, (128, D_GRP), 1) // HEAD
        expand = jnp.where(a == b, 1.0, 0.0).astype(BF16)
        crep = _dot3(cb, expand)
        crep_ref[...] = crep
        ct_ref[...] = crep.T

    return pl.pallas_call(
        body, grid=(nb,),
        in_specs=[_row_spec(TQ, 128), _vec_spec(128)],
        out_specs=(_row_spec(TQ, D_GRP), pl.BlockSpec((None, D_GRP, TQ), lambda i: (i, 0, 0))),
        out_shape=(jax.ShapeDtypeStruct((S, D_GRP), F32), jax.ShapeDtypeStruct((nb, D_GRP, TQ), F32)),
        scratch_shapes=[pltpu.VMEM((8, 128), F32)],
        name=name, compiler_params=_cparams(),
    )(fl_raw, b_pad)


def _lane_masks():
    lane = lax.broadcasted_iota(jnp.int32, (1, 128), 1)
    return lane < HEAD, lane >= HEAD


def _fox_fwd_call(proj, c_rep, c_t, *, name, task=None):
    S = proj.shape[0]
    nq = S // TQ
    scale = HEAD ** -0.5

    def body(q_ref, k_ref, v_ref, c_ref, ct_ref, o_ref, lse_ref):
        i = pl.program_id(1)
        m_lo, m_hi = _lane_masks()
        masks = (m_lo, m_hi)
        q = q_ref[...] * scale
        qm = [jnp.where(mk, q, jnp.zeros_like(q)) for mk in masks]
        cq = c_ref[...]
        cqh = [cq[:, 0:1], cq[:, HEAD:HEAD + 1]]
        row = lax.broadcasted_iota(jnp.int32, (TQ, TQ), 0)
        col = lax.broadcasted_iota(jnp.int32, (TQ, TQ), 1)

        def scores(j):
            start = pl.multiple_of(j * TQ, TQ)
            k = k_ref[pl.ds(start, TQ), :]
            ct = ct_ref[j]
            return tuple(_dot_nt(qm[h], k) + (cqh[h] - ct[HEAD * h:HEAD * h + 1, :]) for h in range(2))

        def update(j, ss, state, masked):
            ms, ls, acc = state
            start = pl.multiple_of(j * TQ, TQ)
            v = v_ref[pl.ds(start, TQ), :]
            new_m, new_l, pv, alpha_l = [], [], [], []
            for h in range(2):
                s = ss[h]
                if masked:
                    s = jnp.where(row >= col, s, NEG)
                mn = jnp.maximum(ms[h], jnp.max(s, axis=1, keepdims=True))
                alpha = jnp.exp(ms[h] - mn)
                p = jnp.exp(s - mn)
                new_l.append(alpha * ls[h] + jnp.sum(p, axis=1, keepdims=True))
                new_m.append(mn)
                alpha_l.append(alpha)
                pv.append(_dot(p.astype(BF16), jnp.where(masks[h], v, jnp.zeros_like(v))))
            alpha_lane = jnp.where(m_lo, alpha_l[0], alpha_l[1])
            acc = acc * alpha_lane + pv[0] + pv[1]
            return (tuple(new_m), tuple(new_l), acc)

        def step(j, carry):
            ss, state = carry
            return (scores(j + 1), update(j, ss, state, False))

        init = ((jnp.full((TQ, 1), NEG, F32),) * 2, (jnp.zeros((TQ, 1), F32),) * 2, jnp.zeros((TQ, 128), F32))
        ss, state = lax.fori_loop(0, i, step, (scores(0), init))
        ms, ls, acc = update(i, ss, state, True)
        l_lane = jnp.where(m_lo, ls[0], ls[1])
        o_ref[...] = acc / l_lane
        lse_ref[...] = jnp.where(m_lo, ms[0] + jnp.log(ls[0]), ms[1] + jnp.log(ls[1])).T

    return _pallas(
        body, grid=(N_PAIR, nq),
        in_specs=[pl.BlockSpec((TQ, 128), lambda p, i: (i, p)),
                  pl.BlockSpec((S, 128), lambda p, i: (0, N_PAIR + p)),
                  pl.BlockSpec((S, 128), lambda p, i: (0, 2 * N_PAIR + p)),
                  pl.BlockSpec((TQ, 128), lambda p, i: (i, p)),
                  pl.BlockSpec((nq, 128, TQ), lambda p, i: (0, p, 0))],
        out_specs=(pl.BlockSpec((TQ, 128), lambda p, i: (i, p)), pl.BlockSpec((None, 128, TQ), lambda p, i: (i, p, 0))),
        out_shape=(jax.ShapeDtypeStruct((S, D_GRP), F32), jax.ShapeDtypeStruct((nq, D_GRP, TQ), F32)),
        name=name, task=task,
    )(proj, proj, proj, c_rep, c_t)


def _fox_bwd_call(proj, do, lse_t, delta_t, c_rep, c_t, *, name, task=None):
    S = proj.shape[0]
    nq = S // TQ
    scale = HEAD ** -0.5

    def body(q_ref, k_ref, v_ref, do_ref, lse_ref, dl_ref, ck_ref, ct_ref,
             dq_ref, dk_ref, dv_ref, dcq_ref, dck_ref, dqa_ref):
        j = pl.program_id(1)
        m_lo, m_hi = _lane_masks()
        masks = (m_lo, m_hi)

        @pl.when(j == 0)
        def _():
            dqa_ref[...] = jnp.zeros_like(dqa_ref)
            dcq_ref[...] = jnp.zeros_like(dcq_ref)
        k = k_ref[...]
        v = v_ref[...]
        km = [jnp.where(mk, k, jnp.zeros_like(k)) for mk in masks]
        ck = ck_ref[...]
        krow = lax.broadcasted_iota(jnp.int32, (TQ, TQ), 0)
        qcol = lax.broadcasted_iota(jnp.int32, (TQ, TQ), 1)

        def probs(i):
            start = pl.multiple_of(i * TQ, TQ)
            q = q_ref[pl.ds(start, TQ), :]
            do = do_ref[pl.ds(start, TQ), :]
            lse = lse_ref[i]
            cq = ct_ref[i]
            out = []
            for h in range(2):
                lo = HEAD * h
                qm = jnp.where(masks[h], q * scale, jnp.zeros_like(q))
                dom = jnp.where(masks[h], do, jnp.zeros_like(do))
                st = _dot_nt(k, qm) + (cq[lo:lo + 1, :] - ck[:, lo:lo + 1])
                out.append((jnp.exp(st - lse[lo:lo + 1, :]), _dot_nt(v, dom)))
            return tuple(out)

        def update(i, pd, carry, masked):
            dk, dv, dck = carry
            start = pl.multiple_of(i * TQ, TQ)
            q = q_ref[pl.ds(start, TQ), :]
            do = do_ref[pl.ds(start, TQ), :]
            dl = dl_ref[i]
            dq = jnp.zeros((TQ, 128), F32)
            new_dck = []
            for h in range(2):
                lo = HEAD * h
                qm = jnp.where(masks[h], q, jnp.zeros_like(q))
                dom = jnp.where(masks[h], do, jnp.zeros_like(do))
                pt, dpt = pd[h]
                if masked:
                    pt = jnp.where(qcol >= krow, pt, 0.0)
                dst = pt * (dpt - dl[lo:lo + 1, :])
                dcq_ref[i, h:h + 1, :] += jnp.sum(dst, axis=0, keepdims=True)
                new_dck.append(dck[h] + jnp.sum(dst, axis=1, keepdims=True))
                dsb = (dst * scale).astype(BF16)
                dv = dv + _dot(pt.astype(BF16), dom)
                dk = dk + _dot(dsb, qm)
                dq = dq + _dot_tn(dsb, km[h])
            dqa_ref[pl.ds(start, TQ), :] += dq
            return (dk, dv, tuple(new_dck))

        def step(i, carry):
            pd, sums = carry
            return (probs(jnp.minimum(i + 1, nq - 1)), update(i, pd, sums, False))

        init = (jnp.zeros((TQ, 128), F32), jnp.zeros((TQ, 128), F32), (jnp.zeros((TQ, 1), F32),) * 2)
        first = probs(j)
        second = probs(jnp.minimum(j + 1, nq - 1))
        _, (dk, dv, dck) = lax.fori_loop(j + 1, nq, step, (second, update(j, first, init, True)))
        dk_ref[...] = dk.astype(BF16)
        dv_ref[...] = dv.astype(BF16)
        dck_t = jnp.where(m_lo, dck[0], dck[1]).T
        row = lax.broadcasted_iota(jnp.int32, (8, TQ), 0)
        dck_ref[...] = -jnp.where(row == 0, dck_t[0:1, :], jnp.where(row == 1, dck_t[HEAD:HEAD + 1, :], 0.0))

        @pl.when(j == nq - 1)
        def _():
            dq_ref[...] = dqa_ref[...].astype(BF16)

    res = lambda p, j: (0, p)
    stat = pl.BlockSpec((nq, 128, TQ), lambda p, j: (0, p, 0))
    blk = pl.BlockSpec((TQ, 128), lambda p, j: (j, p))
    return _pallas(
        body, grid=(N_PAIR, nq), task=task,
        in_specs=[pl.BlockSpec((S, 128), res),
                  pl.BlockSpec((TQ, 128), lambda p, j: (j, N_PAIR + p)),
                  pl.BlockSpec((TQ, 128), lambda p, j: (j, 2 * N_PAIR + p)),
                  pl.BlockSpec((S, 128), res), stat, stat, blk, stat],
        out_specs=(pl.BlockSpec((S, 128), res), blk, blk,
                   pl.BlockSpec((None, nq, 8, TQ), lambda p, j: (p, 0, 0, 0)),
                   pl.BlockSpec((None, None, 8, TQ), lambda p, j: (p, j, 0, 0))),
        out_shape=(jax.ShapeDtypeStruct((S, D_GRP), BF16), jax.ShapeDtypeStruct((S, D_GRP), BF16),
                   jax.ShapeDtypeStruct((S, D_GRP), BF16), jax.ShapeDtypeStruct((N_PAIR, nq, 8, TQ), F32),
                   jax.ShapeDtypeStruct((N_PAIR, nq, 8, TQ), F32)),
        scratch_shapes=[pltpu.VMEM((S, 128), F32)],
        name=name,
    )(proj, proj, proj, do, lse_t, delta_t, c_rep, c_t)


def _fox_gate_bwd_call(dc_rows, fl_raw, b_pad, *, name):
    S = fl_raw.shape[0]
    nb = S // TQ

    def body(dc_ref, fl_ref, b_ref, dfl_ref, db_ref, carry_ref):
        i = pl.program_id(0)

        @pl.when(i == 0)
        def _():
            carry_ref[...] = jnp.zeros_like(carry_ref)
        rc = _dot3(dc_ref[...], _tri(TQ, True)) + carry_ref[:, 0:1]
        carry_ref[...] = jnp.broadcast_to(rc[:, 0:1], carry_ref.shape)
        fl = fl_ref[...] + b_ref[...]
        dfl = rc.T * jax.nn.sigmoid(-fl)
        dfl_ref[...] = dfl.astype(BF16)
        _acc_add(db_ref, jnp.sum(dfl, axis=0, keepdims=True), i)

    rev = lambda i: (nb - 1 - i, 0)
    return pl.pallas_call(
        body, grid=(nb,),
        in_specs=[pl.BlockSpec((128, TQ), lambda i: (0, nb - 1 - i)), pl.BlockSpec((TQ, 128), rev), _vec_spec(128)],
        out_specs=(pl.BlockSpec((TQ, 128), rev), _acc_spec(128)),
        out_shape=(jax.ShapeDtypeStruct((S, 128), BF16), jax.ShapeDtypeStruct((8, 128), F32)),
        scratch_shapes=[pltpu.VMEM((128, 128), F32)],
        name=name, compiler_params=_cparams(),
    )(dc_rows, fl_raw, b_pad)


def _chk_bias_call(g_rev, *, name):
    def body(g_ref, o_ref):
        x = jnp.broadcast_to(g_ref[...], (TQ, ROLL_W))
        rolled = pltpu.roll(x, ROLL_W - (TQ - 1), 1, stride=1, stride_axis=0)
        qc = lax.broadcasted_iota(jnp.int32, (TQ, WIN), 0) // CHUNK
        kc = lax.broadcasted_iota(jnp.int32, (TQ, WIN), 1) // CHUNK
        band = (kc >= qc) & (kc <= qc + LEFT)
        o_ref[...] = jnp.where(band, rolled[:, 0:WIN], NEG)

    return pl.pallas_call(
        body, grid=(8,),
        in_specs=[pl.BlockSpec((None, 1, ROLL_W), lambda h: (h, 0, 0))],
        out_specs=pl.BlockSpec((None, TQ, WIN), lambda h: (h, 0, 0)),
        out_shape=jax.ShapeDtypeStruct((8, TQ, WIN), F32), name=name, compiler_params=_cparams(),
    )(g_rev.reshape(8, 1, ROLL_W))


def _chk_scores(i, qm, kwin, bias, scale):
    s = _dot_nt(qm * scale, kwin) + bias
    kc = lax.broadcasted_iota(jnp.int32, (TQ, WIN), 1) // CHUNK
    return jnp.where(kc + i * (TQ // CHUNK) >= LEFT, s, NEG)


def _chk_fwd_call(proj, bias, *, name, task=None):
    S = proj.shape[0]
    nq = S // TQ
    scale = HEAD ** -0.5

    def body(q_ref, k_ref, v_ref, b_ref, o_ref, kp_ref, vp_ref):
        i = pl.program_id(1)

        @pl.when(i == 0)
        def _():
            kp_ref[0:PADK, :] = jnp.zeros((PADK, 128), BF16)
            vp_ref[0:PADK, :] = jnp.zeros((PADK, 128), BF16)
            kp_ref[PADK:PADK + S, :] = k_ref[...]
            vp_ref[PADK:PADK + S, :] = v_ref[...]
        masks = _lane_masks()
        q = q_ref[...]
        start = pl.multiple_of(i * TQ, TQ)
        kwin = kp_ref[pl.ds(start, WIN), :]
        vwin = vp_ref[pl.ds(start, WIN), :]
        ss = [_chk_scores(i, jnp.where(masks[h], q, jnp.zeros_like(q)), kwin, b_ref[h], scale) for h in range(2)]
        ps = []
        for s in ss:
            p = jnp.exp(s - jnp.max(s, axis=1, keepdims=True))
            ps.append((p / jnp.sum(p, axis=1, keepdims=True)).astype(BF16))
        o_ref[...] = (_dot(ps[0], jnp.where(masks[0], vwin, jnp.zeros_like(vwin)))
                      + _dot(ps[1], jnp.where(masks[1], vwin, jnp.zeros_like(vwin))))

    c0 = 3 * N_PAIR
    return _pallas(
        body, grid=(N_PAIR, nq), task=task,
        in_specs=[pl.BlockSpec((TQ, 128), lambda p, i: (i, c0 + p)),
                  pl.BlockSpec((S, 128), lambda p, i: (0, c0 + N_PAIR + p)),
                  pl.BlockSpec((S, 128), lambda p, i: (0, c0 + 2 * N_PAIR + p)),
                  pl.BlockSpec((2, TQ, WIN), lambda p, i: (p, 0, 0))],
        out_specs=pl.BlockSpec((TQ, 128), lambda p, i: (i, p)),
        out_shape=jax.ShapeDtypeStruct((S, D_GRP), F32),
        scratch_shapes=[pltpu.VMEM((S + PADK, 128), BF16), pltpu.VMEM((S + PADK, 128), BF16)],
        name=name,
    )(proj, proj, proj, bias)


def _chk_bwd_call(proj, do, bias, *, name, task=None):
    S = proj.shape[0]
    nq = S // TQ
    scale = HEAD ** -0.5

    def body(q_ref, k_ref, v_ref, do_ref, b_ref, dq_ref, dk_ref, dv_ref, dg_ref, kp_ref, vp_ref, dkp_ref, dvp_ref, db_ref):
        i = pl.program_id(1)

        @pl.when(i == 0)
        def _():
            kp_ref[0:PADK, :] = jnp.zeros((PADK, 128), BF16)
            vp_ref[0:PADK, :] = jnp.zeros((PADK, 128), BF16)
            kp_ref[PADK:PADK + S, :] = k_ref[...]
            vp_ref[PADK:PADK + S, :] = v_ref[...]
            dkp_ref[...] = jnp.zeros_like(dkp_ref)
            dvp_ref[...] = jnp.zeros_like(dvp_ref)
            db_ref[...] = jnp.zeros_like(db_ref)
        masks = _lane_masks()
        q = q_ref[...]
        dout = do_ref[...]
        start = pl.multiple_of(i * TQ, TQ)
        kwin = kp_ref[pl.ds(start, WIN), :]
        vwin = vp_ref[pl.ds(start, WIN), :]
        qm = [jnp.where(mk, q, jnp.zeros_like(q)) for mk in masks]
        dom = [jnp.where(mk, dout, jnp.zeros_like(dout)) for mk in masks]
        ss = [_chk_scores(i, qm[h], kwin, b_ref[h], scale) for h in range(2)]
        dps = [_dot_nt(dom[h], vwin) for h in range(2)]
        pbs, dsbs = [], []
        for h in range(2):
            p = jnp.exp(ss[h] - jnp.max(ss[h], axis=1, keepdims=True))
            p = p / jnp.sum(p, axis=1, keepdims=True)
            ds = p * (dps[h] - jnp.sum(p * dps[h], axis=1, keepdims=True))
            db_ref[h] += ds
            pbs.append(p.astype(BF16))
            dsbs.append((ds * scale).astype(BF16))
        dq_ref[...] = (_dot(dsbs[0], jnp.where(masks[0], kwin, jnp.zeros_like(kwin)))
                       + _dot(dsbs[1], jnp.where(masks[1], kwin, jnp.zeros_like(kwin)))).astype(BF16)
        dkp_ref[pl.ds(start, WIN), :] += _dot_tn(dsbs[0], qm[0]) + _dot_tn(dsbs[1], qm[1])
        dvp_ref[pl.ds(start, WIN), :] += _dot_tn(pbs[0], dom[0]) + _dot_tn(pbs[1], dom[1])

        @pl.when(i == nq - 1)
        def _():
            dk_ref[...] = dkp_ref[PADK:PADK + S, :].astype(BF16)
            dv_ref[...] = dvp_ref[PADK:PADK + S, :].astype(BF16)
            a = lax.broadcasted_iota(jnp.int32, (TQ, TQ), 0)
            b = lax.broadcasted_iota(jnp.int32, (TQ, TQ), 1)
            flip = jnp.where(a + b == TQ - 1, 1.0, 0.0).astype(BF16)
            e = lax.broadcasted_iota(jnp.int32, (1, ROLL_W), 1)
            dg_ref[...] = jnp.zeros_like(dg_ref)
            for h in range(2):
                rev = _dot3_l(flip, db_ref[h])
                wide = jnp.concatenate([rev, jnp.zeros((TQ, ROLL_W - WIN), F32)], axis=1)
                diag = pltpu.roll(wide, 0, 1, stride=1, stride_axis=0)
                dg = jnp.sum(diag, axis=0, keepdims=True)
                lo = jnp.sum(jnp.where(e <= 639, dg, 0.0), axis=1, keepdims=True)
                hi = jnp.sum(jnp.where(e >= 895, dg, 0.0), axis=1, keepdims=True)
                dg_ref[h:h + 1, :] = jnp.where(e == 639, lo, jnp.where(e == 895, hi, dg))

    c0 = 3 * N_PAIR
    res = lambda p, i: (0, p)
    return _pallas(
        body, grid=(N_PAIR, nq), task=task,
        in_specs=[pl.BlockSpec((TQ, 128), lambda p, i: (i, c0 + p)),
                  pl.BlockSpec((S, 128), lambda p, i: (0, c0 + N_PAIR + p)),
                  pl.BlockSpec((S, 128), lambda p, i: (0, c0 + 2 * N_PAIR + p)),
                  pl.BlockSpec((TQ, 128), lambda p, i: (i, p)),
                  pl.BlockSpec((2, TQ, WIN), lambda p, i: (p, 0, 0))],
        out_specs=(pl.BlockSpec((TQ, 128), lambda p, i: (i, p)), pl.BlockSpec((S, 128), res),
                   pl.BlockSpec((S, 128), res), pl.BlockSpec((None, 8, ROLL_W), lambda p, i: (p, 0, 0))),
        out_shape=(jax.ShapeDtypeStruct((S, D_GRP), BF16), jax.ShapeDtypeStruct((S, D_GRP), BF16),
                   jax.ShapeDtypeStruct((S, D_GRP), BF16), jax.ShapeDtypeStruct((N_PAIR, 8, ROLL_W), F32)),
        scratch_shapes=[pltpu.VMEM((S + PADK, 128), BF16), pltpu.VMEM((S + PADK, 128), BF16),
                        pltpu.VMEM((S + PADK, 128), F32), pltpu.VMEM((S + PADK, 128), F32),
                        pltpu.VMEM((2, TQ, WIN), F32)],
        name=name,
    )(proj, proj, proj, do, bias)


def _mem_fwd_call(q, k, v, *, name, tq=2048):
    S = q.shape[0]
    scale = MEM_HD ** -0.5

    def body(q_ref, k_ref, v_ref, o_ref):
        s = _dot_nt(q_ref[...] * scale, k_ref[...])
        p = jnp.exp(s - jnp.max(s, axis=1, keepdims=True))
        p = p / jnp.sum(p, axis=1, keepdims=True)
        o_ref[...] = _dot(p.astype(BF16), v_ref[...]).astype(BF16)

    return pl.pallas_call(
        body, grid=(MEM_HEADS, S // tq),
        in_specs=[pl.BlockSpec((tq, MEM_HD), lambda h, i: (i, h)),
                  pl.BlockSpec((N_MEM, MEM_HD), lambda h, i: (0, h)),
                  pl.BlockSpec((N_MEM, MEM_HD), lambda h, i: (0, h))],
        out_specs=pl.BlockSpec((tq, MEM_HD), lambda h, i: (i, h)),
        out_shape=jax.ShapeDtypeStruct((S, D), BF16), name=name, compiler_params=_cparams(),
    )(q, k, v)


def _mem_bwd_call(q, k, v, do, *, name, tq=2048):
    S = q.shape[0]
    n = S // tq
    scale = MEM_HD ** -0.5

    def body(q_ref, k_ref, v_ref, do_ref, dq_ref, dk_ref, dv_ref, dka_ref, dva_ref):
        i = pl.program_id(1)

        @pl.when(i == 0)
        def _():
            dka_ref[...] = jnp.zeros_like(dka_ref)
            dva_ref[...] = jnp.zeros_like(dva_ref)
        qb = q_ref[...]
        kb = k_ref[...]
        dob = do_ref[...]
        s = _dot_nt(qb * scale, kb)
        p = jnp.exp(s - jnp.max(s, axis=1, keepdims=True))
        p = p / jnp.sum(p, axis=1, keepdims=True)
        dp = _dot_nt(dob, v_ref[...])
        ds = p * (dp - jnp.sum(p * dp, axis=1, keepdims=True))
        dsb = (ds * scale).astype(BF16)
        dq_ref[...] = _dot(dsb, kb).astype(BF16)
        dka_ref[...] += _dot_tn(dsb, qb)
        dva_ref[...] += _dot_tn(p.astype(BF16), dob)

        @pl.when(i == n - 1)
        def _():
            dk_ref[...] = dka_ref[...].astype(BF16)
            dv_ref[...] = dva_ref[...].astype(BF16)

    kv = pl.BlockSpec((N_MEM, MEM_HD), lambda h, i: (0, h))
    qs = pl.BlockSpec((tq, MEM_HD), lambda h, i: (i, h))
    return pl.pallas_call(
        body, grid=(MEM_HEADS, n), in_specs=[qs, kv, kv, qs], out_specs=(qs, kv, kv),
        out_shape=(jax.ShapeDtypeStruct((S, D), BF16), jax.ShapeDtypeStruct((N_MEM, D), BF16),
                   jax.ShapeDtypeStruct((N_MEM, D), BF16)),
        scratch_shapes=[pltpu.VMEM((N_MEM, MEM_HD), F32), pltpu.VMEM((N_MEM, MEM_HD), F32)],
        name=name, compiler_params=_cparams(),
    )(q, k, v, do)


def _rel_table_to_g(rel):
    return jnp.concatenate([
        jnp.broadcast_to(rel[:, N_REL - 1:N_REL], (8, 640)),
        rel[:, 1:N_REL - 1][:, ::-1],
        jnp.broadcast_to(rel[:, 0:1], (8, 129)),
    ], axis=1)


def _g_to_rel_table(dg):
    return dg[:, 639:896][:, ::-1]


def _place():
    x, y, c = lax.axis_index("x"), lax.axis_index("y"), lax.axis_index("c")
    others = [(1 - x, y), (x, 1 - y), (1 - x, 1 - y)]
    return x, y, c, others


def _half(c, rows):
    hr = rows // 2
    return pl.ds(pl.multiple_of(c * hr, 16), hr)


def _dma_sems(*shape):
    return pltpu.SemaphoreType.DMA(shape)


def _cast_slabs_call(ws, chip_arr, *, name, tm=256, task=None):
    n = len(ws)
    cols = ws[0].shape[1]
    tiles = [w.shape[0] // tm for w in ws]
    steps = max(tiles)

    def body(chip_ref, *refs):
        i = pl.program_id(0)
        for k in range(n):
            def cast(k=k):
                refs[n + k][...] = refs[k][...].astype(BF16)
            if tiles[k] == steps:
                cast()
            else:
                pl.when(i < tiles[k])(cast)

    in_specs = [pl.BlockSpec((tm, cols), lambda i, chip, t=t: (jnp.minimum(i, t - 1), 0)) for t in tiles]
    out_specs = [pl.BlockSpec((None, tm, cols), lambda i, chip, t=t: (chip[0], jnp.minimum(i, t - 1), 0)) for t in tiles]
    out_shape = [jax.ShapeDtypeStruct((N_CHIP,) + w.shape, BF16) for w in ws]
    return _pallas(body, grid=(steps,), in_specs=in_specs, out_specs=out_specs, out_shape=out_shape, name=name,
                   task=task, prefetch=1)(chip_arr, *ws)


def _cast_slab_call(w, chip_arr, *, name, tm=256, pad_rows=0):
    rows, cols = w.shape
    if pad_rows:
        tm = rows
    tm = min(tm, rows)

    def body(chip_ref, w_ref, o_ref):
        o_ref[0:tm, :] = w_ref[...].astype(BF16)
        if pad_rows:
            o_ref[tm:tm + pad_rows, :] = jnp.zeros((pad_rows, cols), BF16)

    return pl.pallas_call(
        body,
        grid_spec=pltpu.PrefetchScalarGridSpec(
            num_scalar_prefetch=1, grid=(rows // tm,),
            in_specs=[pl.BlockSpec((tm, cols), lambda i, chip: (i, 0))],
            out_specs=pl.BlockSpec((None, tm + pad_rows, cols), lambda i, chip: (chip[0], i, 0))),
        out_shape=jax.ShapeDtypeStruct((N_CHIP, rows + pad_rows, cols), BF16), name=name,
        compiler_params=_cparams(),
    )(chip_arr, w)


def _ag_ici_task(gathered):
    n = len(gathered)

    def copies(ins, outs, sems):
        send_sems, recv_sems = sems
        x, y, c, others = _place()
        me = 2 * x + y
        for k in range(n):
            mine = _half(c, gathered[k].shape[1])
            for t, (ox, oy) in enumerate(others):
                yield [pltpu.make_async_remote_copy(
                    src_ref=ins[k].at[me, mine], dst_ref=outs[k].at[slab, mine],
                    send_sem=send_sems.at[k, t], recv_sem=recv_sems.at[k, t],
                    device_id=(ox, oy, c), device_id_type=MESH) for slab in (me, 2 * ox + oy)]

    def issue(ins, outs, sems):
        for outgoing, _ in copies(ins, outs, sems):
            outgoing.start()

    def drain(ins, outs, sems):
        for outgoing, incoming in copies(ins, outs, sems):
            incoming.wait_recv()
            outgoing.wait_send()

    return _Task(gathered, [jax.ShapeDtypeStruct(g.shape, g.dtype) for g in gathered],
                 [_dma_sems(n, 3), _dma_sems(n, 3)], issue, drain, aliases={k: k for k in range(n)})


def _ag_d2d_task(gathered):
    n = len(gathered)

    def copies(ins, outs, sems):
        send_sems, recv_sems = sems
        x, y, c, others = _place()
        for k in range(n):
            rows = gathered[k].shape[1]
            mine, theirs = _half(c, rows), _half(1 - c, rows)
            for t, (ox, oy) in enumerate(others):
                slab = 2 * ox + oy
                pair = [pltpu.make_async_remote_copy(
                    src_ref=ins[k].at[slab, half], dst_ref=outs[k].at[slab, half],
                    send_sem=send_sems.at[k, t], recv_sem=recv_sems.at[k, t],
                    device_id=(x, y, 1 - c), device_id_type=MESH) for half in (mine, theirs)]
                yield pair

    def issue(ins, outs, sems):
        for outgoing, _ in copies(ins, outs, sems):
            outgoing.start()

    def drain(ins, outs, sems):
        for outgoing, incoming in copies(ins, outs, sems):
            incoming.wait_recv()
            outgoing.wait_send()

    return _Task(gathered, [jax.ShapeDtypeStruct(g.shape, g.dtype) for g in gathered],
                 [_dma_sems(n, 3), _dma_sems(n, 3)], issue, drain, aliases={k: k for k in range(n)})


def _rs_pair_task(ds):
    n = len(ds)

    def copies(ins, outs, sems):
        send_sems, recv_sems = sems
        x, y, c, _ = _place()
        for k in range(n):
            yield pltpu.make_async_remote_copy(
                src_ref=ins[k].at[:, _half(1 - c, ds[k].shape[1])], dst_ref=outs[k],
                send_sem=send_sems.at[k], recv_sem=recv_sems.at[k],
                device_id=(x, y, 1 - c), device_id_type=MESH)

    def issue(ins, outs, sems):
        for cp in copies(ins, outs, sems):
            cp.start()

    def drain(ins, outs, sems):
        for cp in copies(ins, outs, sems):
            cp.wait()

    return _Task(ds, [jax.ShapeDtypeStruct((N_CHIP, d.shape[1] // 2, d.shape[2]), d.dtype) for d in ds],
                 [_dma_sems(n), _dma_sems(n)], issue, drain)


def _pair_add_call(d, r1, c_arr, *, name, tm=512):
    _, rows, cols = d.shape
    hr = rows // 2
    tm = tm if hr % tm == 0 else hr
    nb = hr // tm

    def body(c_ref, d_ref, r_ref, o_ref):
        o_ref[...] = (d_ref[...].astype(F32) + r_ref[...].astype(F32)).astype(BF16)

    return pl.pallas_call(
        body,
        grid_spec=pltpu.PrefetchScalarGridSpec(
            num_scalar_prefetch=1, grid=(N_CHIP, nb),
            in_specs=[pl.BlockSpec((None, tm, cols), lambda j, i, c: (j, c[0] * nb + i, 0)),
                      pl.BlockSpec((None, tm, cols), lambda j, i, c: (j, i, 0))],
            out_specs=pl.BlockSpec((None, tm, cols), lambda j, i, c: (j, i, 0))),
        out_shape=jax.ShapeDtypeStruct((N_CHIP, hr, cols), BF16), name=name, compiler_params=_cparams(),
    )(c_arr, d, r1)


def _rs_chip_task(ps):
    n = len(ps)

    def copies(ins, outs, sems):
        send_sems, recv_sems = sems
        x, y, c, others = _place()
        for k in range(n):
            for t, (ox, oy) in enumerate(others):
                yield pltpu.make_async_remote_copy(
                    src_ref=ins[k].at[2 * ox + oy], dst_ref=outs[k].at[t],
                    send_sem=send_sems.at[k, t], recv_sem=recv_sems.at[k, t],
                    device_id=(ox, oy, c), device_id_type=MESH)

    def issue(ins, outs, sems):
        for cp in copies(ins, outs, sems):
            cp.start()

    def drain(ins, outs, sems):
        for cp in copies(ins, outs, sems):
            cp.wait()

    return _Task(ps, [jax.ShapeDtypeStruct((3,) + p.shape[1:], p.dtype) for p in ps],
                 [_dma_sems(n, 3), _dma_sems(n, 3)], issue, drain)


def _chip_sum_call(p, r2, place_arr, *, name, tm=512):
    _, hr, cols = r2.shape
    tm = tm if hr % tm == 0 else hr
    nb = hr // tm

    def body(place_ref, p_ref, r_ref, o_ref):
        acc = p_ref[...].astype(F32)
        for j in range(3):
            acc = acc + r_ref[j].astype(F32)
        o_ref[...] = acc

    return _pallas(
        body, grid=(nb,), prefetch=1,
        in_specs=[pl.BlockSpec((None, tm, cols), lambda i, pc: (pc[0], i, 0)),
                  pl.BlockSpec((3, tm, cols), lambda i, pc: (0, i, 0))],
        out_specs=pl.BlockSpec((tm, cols), lambda i, pc: (pc[1] * nb + i, 0)),
        out_shape=jax.ShapeDtypeStruct((2 * hr, cols), F32), name=name,
    )(place_arr, p, r2)


def _chip_sums_call(ps, r2s, place_arr, *, name, steps=2, task=None):
    n = len(ps)

    def body(place_ref, *refs):
        ins, outs = refs[:2 * n], refs[2 * n:]
        for k in range(n):
            acc = ins[2 * k][...].astype(F32)
            for j in range(3):
                acc = acc + ins[2 * k + 1][j].astype(F32)
            outs[k][...] = acc

    in_specs, out_specs, out_shape, args = [], [], [], []
    for p, r2 in zip(ps, r2s):
        _, hr, cols = r2.shape
        tm = hr // steps
        in_specs += [pl.BlockSpec((None, tm, cols), lambda i, pc: (pc[0], i, 0)),
                     pl.BlockSpec((3, tm, cols), lambda i, pc: (0, i, 0))]
        out_specs.append(pl.BlockSpec((tm, cols), lambda i, pc: (pc[1] * steps + i, 0)))
        out_shape.append(jax.ShapeDtypeStruct((2 * hr, cols), F32))
        args += [p, r2]
    return _pallas(body, grid=(steps,), prefetch=1, in_specs=in_specs, out_specs=out_specs, out_shape=out_shape,
                   name=name, task=task)(place_arr, *args)


def _rs_gather_task(gs):
    n = len(gs)

    def copies(ins, outs, sems):
        send_sems, recv_sems = sems
        x, y, c, _ = _place()
        for k in range(n):
            rows = gs[k].shape[0]
            mine, theirs = _half(c, rows), _half(1 - c, rows)
            yield [pltpu.make_async_remote_copy(
                src_ref=ins[k].at[mine], dst_ref=outs[k].at[half],
                send_sem=send_sems.at[k], recv_sem=recv_sems.at[k],
                device_id=(x, y, 1 - c), device_id_type=MESH) for half in (mine, theirs)]

    def issue(ins, outs, sems):
        for outgoing, _ in copies(ins, outs, sems):
            outgoing.start()

    def drain(ins, outs, sems):
        for outgoing, incoming in copies(ins, outs, sems):
            incoming.wait_recv()
            outgoing.wait_send()

    return _Task(gs, [jax.ShapeDtypeStruct(g.shape, g.dtype) for g in gs],
                 [_dma_sems(n), _dma_sems(n)], issue, drain, aliases={k: k for k in range(n)})


def _adamw(w, g, m, v):
    m = ADAM_B1 * m + (1.0 - ADAM_B1) * g
    v = ADAM_B2 * v + (1.0 - ADAM_B2) * jnp.square(g)
    m_hat = m / (1.0 - ADAM_B1 ** ADAM_STEP)
    v_hat = v / (1.0 - ADAM_B2 ** ADAM_STEP)
    delta = -ADAM_LR * (m_hat / (jnp.sqrt(v_hat) + ADAM_EPS) + ADAM_WD * w)
    return delta, m, v


def _adamw_call(items, *, name, tm=256, task=None):
    n = len(items)
    cols = items[0][0].shape[1]
    tiles = [it[0].shape[0] // tm for it in items]
    steps = max(tiles)

    def body(*refs):
        i = pl.program_id(0)
        ins, outs = refs[:4 * n], refs[4 * n:]
        for k in range(n):
            def update(k=k):
                g = ins[4 * k + 1][...]
                res = _adamw(ins[4 * k][...], g, ins[4 * k + 2][...], ins[4 * k + 3][...])
                outs[4 * k][...] = g
                for j in range(3):
                    outs[4 * k + 1 + j][...] = res[j]
            if tiles[k] == steps:
                update()
            else:
                pl.when(i < tiles[k])(update)

    in_specs, out_specs, out_shape, args = [], [], [], []
    for it, t in zip(items, tiles):
        spec = pl.BlockSpec((tm, cols), lambda i, t=t: (jnp.minimum(i, t - 1), 0))
        in_specs += [spec] * 4
        out_specs += [spec] * 4
        out_shape += [jax.ShapeDtypeStruct(it[0].shape, F32)] * 4
        args += list(it)
    res = _pallas(body, grid=(steps,), in_specs=in_specs, out_specs=out_specs, out_shape=out_shape,
                  name=name, task=task)(*args)
    outs, extra = res if task is not None else (res, None)
    grouped = [tuple(outs[4 * k:4 * k + 4]) for k in range(n)]
    return (grouped, extra) if task is not None else grouped


def _adamw_cols_call(w, g_pad, m, v, *, name, tn=512):
    rows, cols = w.shape

    def body(w_ref, g_ref, m_ref, v_ref, go_ref, d_ref, mo_ref, vo_ref):
        g = g_ref[0:rows, :]
        d, mn, vn = _adamw(w_ref[...], g, m_ref[...], v_ref[...])
        go_ref[...] = g
        d_ref[...] = d
        mo_ref[...] = mn
        vo_ref[...] = vn

    spec = pl.BlockSpec((rows, tn), lambda j: (0, j))
    gspec = pl.BlockSpec((g_pad.shape[0], tn), lambda j: (0, j))
    return _pallas(body, grid=(cols // tn,), in_specs=[spec, gspec, spec, spec], out_specs=(spec,) * 4,
                   out_shape=(jax.ShapeDtypeStruct((rows, cols), F32),) * 4, name=name)(w, g_pad, m, v)


N_DEV = 8
SMALL_ROWS = 24
SMALL_LAYOUT = {
    "g_mix_pre": (0, 0, 1, D), "g_mix_post": (1, 0, 1, D), "g_mem_kv": (2, 0, 1, D), "g_mem_pre": (3, 0, 1, D),
    "g_mem_post": (4, 0, 1, D), "g_ff_pre": (5, 0, 1, D), "g_ff_post": (6, 0, 1, D),
    "g_fox_out": (7, 0, 1, D_GRP), "g_chk_out": (7, D_GRP, 1, D_GRP), "b_fgt": (8, 0, 1, 8),
    "rel_bias": (16, 0, 8, N_REL),
}
SMALL = list(SMALL_LAYOUT)


LOSS_ROW = 9


def _small_reduce_call(grads, loss_blk, task, *, name):
    n = len(SMALL)
    t_in, t_out = len(task.arrays), len(task.out_shapes)

    def body(*refs):
        g_refs, loss_ref, tins = refs[:n], refs[n], refs[n + 1:n + 1 + t_in]
        p = n + 1 + t_in
        total_ref, loss_out, touts = refs[p], refs[p + 1], refs[p + 2:p + 2 + t_out]
        p += 2 + t_out
        mine, slots, send_sems, recv_sems = refs[p:p + 4]
        tsems = refs[p + 4:]
        task.issue(tins, touts, tsems)
        x, y, c, _ = _place()
        me = 4 * x + 2 * y + c
        mine[...] = jnp.zeros_like(mine)
        for k, name_k in enumerate(SMALL):
            r, l, nr, nl = SMALL_LAYOUT[name_k]
            mine[r:r + nr, l:l + nl] = g_refs[k][0:nr, 0:nl]
        mine[LOSS_ROW:LOSS_ROW + 1, 0:128] = loss_ref[0:1, :]
        slots[me] = mine[...]
        peers = [(dx, dy, dc) for dx in (0, 1) for dy in (0, 1) for dc in (0, 1)][1:]
        cps = []
        for t, (dx, dy, dc) in enumerate(peers):
            px, py, pc = (x + dx) % 2, (y + dy) % 2, (c + dc) % 2
            cps.append(pltpu.make_async_remote_copy(
                src_ref=mine, dst_ref=slots.at[me], send_sem=send_sems.at[t], recv_sem=recv_sems.at[t],
                device_id=(px, py, pc), device_id_type=MESH))
            cps[-1].start()
        for t, (dx, dy, dc) in enumerate(peers):
            px, py, pc = (x + dx) % 2, (y + dy) % 2, (c + dc) % 2
            pltpu.make_async_remote_copy(
                src_ref=mine, dst_ref=slots.at[4 * px + 2 * py + pc], send_sem=send_sems.at[t],
                recv_sem=recv_sems.at[t], device_id=(px, py, pc), device_id_type=MESH).wait_recv()
        for cp in cps:
            cp.wait_send()
        total = slots[0]
        for j in range(1, N_DEV):
            total = total + slots[j]
        total_ref[...] = total
        loss_out[...] = jnp.broadcast_to(total[LOSS_ROW:LOSS_ROW + 1, 0:128], loss_out.shape)
        task.drain(tins, touts, tsems)

    vm = pl.BlockSpec(memory_space=pltpu.VMEM)
    out_shape = [jax.ShapeDtypeStruct((SMALL_ROWS, D), F32), jax.ShapeDtypeStruct((8, 128), F32)] + list(task.out_shapes)
    res = pl.pallas_call(
        body, in_specs=[vm] * (n + 1) + [ANY] * t_in, out_specs=[vm] * 2 + [ANY] * t_out,
        out_shape=out_shape,
        scratch_shapes=[pltpu.VMEM((SMALL_ROWS, D), F32), pltpu.VMEM((N_DEV, SMALL_ROWS, D), F32),
                        _dma_sems(N_DEV - 1), _dma_sems(N_DEV - 1)] + list(task.sems),
        input_output_aliases={n + 1 + i: 2 + j for i, j in task.aliases.items()},
        name=name,
    )(*[grads[k] for k in SMALL], loss_blk, *task.arrays)
    return res[0], res[1], list(res[2:])


def _small_adamw_call(total, ws, ms, vs, *, name):
    n = len(SMALL)

    def body(*refs):
        total_ref = refs[0]
        w_refs, m_refs, v_refs = (refs[1 + j * n:1 + (j + 1) * n] for j in range(3))
        outs = refs[1 + 3 * n:]
        for k, name_k in enumerate(SMALL):
            r, l, nr, nl = SMALL_LAYOUT[name_k]
            g = total_ref[r:r + nr, l:l + nl]
            d, mn, vn = _adamw(w_refs[k][...], g, m_refs[k][...], v_refs[k][...])
            for j, val in enumerate((g, d, mn, vn)):
                outs[4 * k + j][...] = val

    vm = pl.BlockSpec(memory_space=pltpu.VMEM)
    res = pl.pallas_call(
        body, in_specs=[vm] * (3 * n + 1), out_specs=[vm] * (4 * n),
        out_shape=[jax.ShapeDtypeStruct(ws[k].shape, F32) for k in SMALL for _ in range(4)], name=name,
    )(total, *[d[k] for d in (ws, ms, vs) for k in SMALL])
    return {k: tuple(res[4 * i:4 * i + 4]) for i, k in enumerate(SMALL)}


WEIGHTS = ["w_in", "b_fgt", "rel_bias", "g_fox_out", "g_chk_out", "w_out", "g_mix_pre", "g_mix_post", "g_mem_kv",
           "w_mq", "w_mk", "w_mv", "w_mo", "g_mem_pre", "g_mem_post", "w_ff1", "w_ff2", "g_ff_pre", "g_ff_post"]
BIG = ["w_in", "w_out", "w_mq", "w_mk", "w_mv", "w_mo", "w_ff1", "w_ff2"]


IN_SHARD = D_IN // N_CHIP
IN_PAD = 800


IN_PIECES = [(0, 0, 770), (800, 770, 766), (1566, 3072, 4), (1600, 3076, 4), (1604, 1536, 766), (2400, 2302, 770)]
PAD_ZEROS = [(800 * j + IN_SHARD, IN_PAD - IN_SHARD) for j in range(N_CHIP)]
ALL_ZEROS = [(D_IN, D_ALL - D_IN)]


def _reorder_rows_call(src, to_all, *, name, tn=512):
    rows, cols = src.shape
    zeros = ALL_ZEROS if to_all else PAD_ZEROS

    def body(s_ref, o_ref):
        for pad0, all0, cnt in IN_PIECES:
            s0, d0 = (pad0, all0) if to_all else (all0, pad0)
            o_ref[d0:d0 + cnt, :] = s_ref[s0:s0 + cnt, :]
        for z0, cnt in zeros:
            o_ref[z0:z0 + cnt, :] = jnp.zeros((cnt, tn), src.dtype)

    spec = pl.BlockSpec((rows, tn), lambda j: (0, j))
    return _pallas(body, grid=(cols // tn,), in_specs=[spec], out_specs=spec,
                   out_shape=jax.ShapeDtypeStruct((rows, cols), src.dtype), name=name)(src)


def kernel(x, mem, w_in, b_fgt, rel_bias, g_fox_out, g_chk_out, w_out, g_mix_pre, g_mix_post, g_mem_kv, w_mq, w_mk, w_mv, w_mo, g_mem_pre, g_mem_post, w_ff1, w_ff2, g_ff_pre, g_ff_post, loss_target, m_w_in, m_b_fgt, m_rel_bias, m_g_fox_out, m_g_chk_out, m_w_out, m_g_mix_pre, m_g_mix_post, m_g_mem_kv, m_w_mq, m_w_mk, m_w_mv, m_w_mo, m_g_mem_pre, m_g_mem_post, m_w_ff1, m_w_ff2, m_g_ff_pre, m_g_ff_post, v_w_in, v_b_fgt, v_rel_bias, v_g_fox_out, v_g_chk_out, v_w_out, v_g_mix_pre, v_g_mix_post, v_g_mem_kv, v_w_mq, v_w_mk, v_w_mv, v_w_mo, v_g_mem_pre, v_g_mem_post, v_w_ff1, v_w_ff2, v_g_ff_pre, v_g_ff_post):
    w = dict(w_in=w_in, b_fgt=b_fgt, rel_bias=rel_bias, g_fox_out=g_fox_out, g_chk_out=g_chk_out, w_out=w_out,
             g_mix_pre=g_mix_pre, g_mix_post=g_mix_post, g_mem_kv=g_mem_kv, w_mq=w_mq, w_mk=w_mk, w_mv=w_mv,
             w_mo=w_mo, g_mem_pre=g_mem_pre, g_mem_post=g_mem_post, w_ff1=w_ff1, w_ff2=w_ff2, g_ff_pre=g_ff_pre,
             g_ff_post=g_ff_post)
    m = dict(w_in=m_w_in, b_fgt=m_b_fgt, rel_bias=m_rel_bias, g_fox_out=m_g_fox_out, g_chk_out=m_g_chk_out,
             w_out=m_w_out, g_mix_pre=m_g_mix_pre, g_mix_post=m_g_mix_post, g_mem_kv=m_g_mem_kv, w_mq=m_w_mq,
             w_mk=m_w_mk, w_mv=m_w_mv, w_mo=m_w_mo, g_mem_pre=m_g_mem_pre, g_mem_post=m_g_mem_post,
             w_ff1=m_w_ff1, w_ff2=m_w_ff2, g_ff_pre=m_g_ff_pre, g_ff_post=m_g_ff_post)
    v = dict(w_in=v_w_in, b_fgt=v_b_fgt, rel_bias=v_rel_bias, g_fox_out=v_g_fox_out, g_chk_out=v_g_chk_out,
             w_out=v_w_out, g_mix_pre=v_g_mix_pre, g_mix_post=v_g_mix_post, g_mem_kv=v_g_mem_kv, w_mq=v_w_mq,
             w_mk=v_w_mk, w_mv=v_w_mv, w_mo=v_w_mo, g_mem_pre=v_g_mem_pre, g_mem_post=v_g_mem_post,
             w_ff1=v_w_ff1, w_ff2=v_w_ff2, g_ff_pre=v_g_ff_pre, g_ff_post=v_g_ff_post)

    def rows(d, k):
        return d[k][0] if k == "rel_bias" else d[k]

    xs, mems, target = x[0], mem[0], loss_target[0]
    S = xs.shape[0]
    sp = {k: rows(w, k) for k in SMALL}
    b_pad = jnp.pad(sp["b_fgt"], ((0, 0), (0, 120)))
    chip = 2 * lax.axis_index("x") + lax.axis_index("y")
    chip_arr = jnp.reshape(chip, (1,)).astype(jnp.int32)
    c_arr = jnp.reshape(lax.axis_index("c"), (1,)).astype(jnp.int32)
    place_arr = jnp.concatenate([chip_arr, c_arr])
    w_in_t, m_in_t, v_in_t = w["w_in"][0].T, m["w_in"][0].T, v["w_in"][0].T
    slab = {"w_in": _cast_slab_call(w_in_t, chip_arr, name="cast_w_in", pad_rows=IN_PAD - IN_SHARD)}

    def gather_ici(names):
        return _ag_ici_task([slab[k] for k in names])

    def pair_add(k, d, r1):
        return _pair_add_call(d, r1, c_arr, name="rs_pair_add_" + k)

    rest, (g_in,) = _cast_slabs_call([w[k][0] for k in BIG[1:]], chip_arr, name="cast_rest",
                                     task=gather_ici(["w_in"]))
    slab.update(zip(BIG[1:], rest))
    h1, (g_in,) = _rms_fwd_call(xs, sp["g_mix_pre"], name="rms_mix_pre", task=_ag_d2d_task([g_in]))
    w_all_t = _reorder_rows_call(g_in.reshape(N_CHIP * IN_PAD, D), True, name="w_in_rows")
    proj, (g_out, g_mq) = _mm_nt(h1, w_all_t, "plain", rows=(0, 3072), name="mm_proj",
                                 task=gather_ici(["w_out", "w_mq"]))
    fl_raw = _mm_nt(h1, w_all_t, "plain", rows=(3072, 128), name="mm_gate", out_dtype=F32, tn=128)
    c_rep, c_t = _fox_prep_call(fl_raw, b_pad, name="fox_prep")
    bias = _chk_bias_call(_rel_table_to_g(sp["rel_bias"]), name="chk_bias")
    mid = ["w_mk", "w_mv", "w_mo", "w_ff1"]
    (yf, lse), got = _fox_fwd_call(proj, c_rep, c_t, name="fox_fwd",
                                   task=_merge_tasks([gather_ici(mid), _ag_d2d_task([g_out, g_mq])]))
    g_mid, (g_out, g_mq) = got[:4], got[4:]
    yc, got = _chk_fwd_call(proj, bias, name="chk_fwd",
                            task=_merge_tasks([gather_ici(["w_ff2"]), _ag_d2d_task(g_mid)]))
    g_ff2, (g_mk, g_mv, g_mo, g_ff1) = got[0], got[1:]
    yn = _mix_norm_fwd_call(yf, yc, sp["g_fox_out"], sp["g_chk_out"], name="mix_norm_fwd")
    z, (g_ff2,) = _mm_nn(yn, g_out, "rows", name="mm_out", out_dtype=F32, task=_ag_d2d_task([g_ff2]))
    x1, h2 = _post_pre_call(xs, z, sp["g_mix_post"], sp["g_mem_pre"], name="post_mix")
    memn = _rms_fwd_call(mems, sp["g_mem_kv"], name="rms_mem_kv")
    q2 = _mm_nn(h2, g_mq, "rows", name="mm_mq")
    k2 = _mm_nn(memn, g_mk, "rows", name="mm_mk")
    v2 = _mm_nn(memn, g_mv, "rows", name="mm_mv")
    o2 = _mem_fwd_call(q2, k2, v2, name="mem_fwd")
    y2 = _mm_nn(o2, g_mo, "rows", name="mm_mo", out_dtype=F32)
    x2, h3 = _post_pre_call(x1, y2, sp["g_mem_post"], sp["g_ff_pre"], name="post_mem")
    act, relu = _mm_nn(h3, g_ff1, "cols", name="mm_ff1", epi="relu2")
    y3 = _mm_nn(act, g_ff2, "rows", name="mm_ff2", out_dtype=F32, tm=1024)
    loss_blk, dx3, dy3, dg_ff_post = _final_call(x2, y3, sp["g_ff_post"], target, name="final")

    d_ff2 = _mm_tn(act, dy3, name="mm_dff2", tk=512, tn=1024).reshape(N_CHIP, D_FF // N_CHIP, D)
    du, (r1,) = _mm_nt(dy3, g_ff2, "rows", name="mm_du", mul2r=relu, task=_rs_pair_task([d_ff2]))
    p_ff2 = pair_add("w_ff2", d_ff2, r1)
    d_ff1 = _mm_tn(h3, du, name="mm_dff1", cols4=True)
    dh3, (r1,) = _mm_nt(du, g_ff1, "cols", name="mm_dh3", out_dtype=F32, tm=1024, task=_rs_pair_task([d_ff1]))
    p_ff1 = pair_add("w_ff1", d_ff1, r1)
    dx2, dy2, dg_ff_pre, dg_mem_post = _bwd_mid_call(dx3, x2, dh3, sp["g_ff_pre"], y2, sp["g_mem_post"], name="bwd_ff")
    d_mo = _mm_tn(o2, dy2, name="mm_dmo").reshape(N_CHIP, D // N_CHIP, D)
    do2 = _mm_nt(dy2, g_mo, "rows", name="mm_do2")
    dq2, dk2, dv2 = _mem_bwd_call(q2, k2, v2, do2, name="mem_bwd")
    d_mq = _mm_tn(h2, dq2, name="mm_dmq").reshape(N_CHIP, D // N_CHIP, D)
    dh2 = _mm_nt(dq2, g_mq, "rows", name="mm_dh2", out_dtype=F32)
    d_mk = _mm_tn(memn, dk2, name="mm_dmk").reshape(N_CHIP, D // N_CHIP, D)
    d_mv = _mm_tn(memn, dv2, name="mm_dmv").reshape(N_CHIP, D // N_CHIP, D)
    dmn_k = _mm_nt(dk2, g_mk, "rows", name="mm_dmemk", out_dtype=F32)
    dmn_v = _mm_nt(dv2, g_mv, "rows", name="mm_dmemv", out_dtype=F32)
    dg_mem_kv = _gain_grad_call(mems, sp["g_mem_kv"], dmn_k, dmn_v, name="gain_mem_kv")
    dx1, dz, dg_mem_pre, dg_mix_post = _bwd_mid_call(dx2, x1, dh2, sp["g_mem_pre"], z, sp["g_mix_post"], name="bwd_mem")
    d_out = _mm_tn(yn, dz, name="mm_dout").reshape(N_CHIP, D // N_CHIP, D)
    late = ["w_mo", "w_mq", "w_mk", "w_mv", "w_out"]
    d_late = [d_mo, d_mq, d_mk, d_mv, d_out]
    dyn, r1_late = _mm_nt(dz, g_out, "rows", name="mm_dyn", out_dtype=F32, task=_rs_pair_task(d_late))
    p_late = [pair_add(k, d, r1) for k, d, r1 in zip(late, d_late, r1_late)]
    dof, doc, delta, dg_fox, dg_chk = _mix_norm_bwd_call(dyn, yf, yc, sp["g_fox_out"], sp["g_chk_out"], name="mix_norm_bwd")
    (dqf, dkf, dvf, dcq, dck), r2_ff = _fox_bwd_call(proj, dof, lse, delta, c_rep, c_t, name="fox_bwd",
                                                      task=_rs_chip_task([p_ff2, p_ff1]))
    (dqc, dkc, dvc, dgrev), r2_late = _chk_bwd_call(proj, doc, bias, name="chk_bwd", task=_rs_chip_task(p_late))
    first = ["w_ff2", "w_ff1"] + late
    dc8 = (dcq[:, :, 0:2, :] + dck[:, :, 0:2, :]).transpose(0, 2, 1, 3).reshape(8, S)
    dc_rows = jnp.concatenate([dc8, jnp.zeros((120, S), F32)], axis=0)
    dfl, db_fgt = _fox_gate_bwd_call(dc_rows, fl_raw, b_pad, name="fox_gate_bwd")
    dproj = jnp.concatenate([dqf, dkf, dvf, dqc, dkc, dvc, dfl], axis=1)
    d_all_t = _mm_tn(dproj, h1, name="mm_dwin", tk=640, tn=1024)
    d_in = _reorder_rows_call(d_all_t, False, name="d_in_rows").reshape(N_CHIP, IN_PAD, D)
    f_first, (r1,) = _chip_sums_call([p_ff2, p_ff1] + p_late, r2_ff + r2_late, place_arr, name="rs_chip_sums",
                                     task=_rs_pair_task([d_in]))
    p_in = pair_add("w_in", d_in, r1)
    dh1, got = _mm_nn(dproj, w_all_t, "plain", name="mm_dh1", out_dtype=F32, tm=1024,
                      task=_merge_tasks([_rs_chip_task([p_in]), _rs_gather_task(f_first)]))
    r2_in, grads = got[0], dict(zip(first, got[1:]))
    f_in = _chip_sum_call(p_in, r2_in, place_arr, name="rs_chip_sum_w_in")
    delta_w, new_m, new_v = {}, {}, {}

    def adamw_items(names):
        return [(w[k][0], grads[k], m[k][0], v[k][0]) for k in names]

    upd_late = _adamw_call(adamw_items(late), name="adamw_late", tm=64)
    upd_ff = _adamw_call(adamw_items(first[:2]), name="adamw_ff")
    for k, res in zip(late + first[:2], upd_late + upd_ff):
        grads[k], delta_w[k], new_m[k], new_v[k] = res
    grad_x, dg_mix_pre = _bwd_last_call(dx1, xs, dh1, sp["g_mix_pre"], name="bwd_mix")

    small_g = {"g_mix_pre": dg_mix_pre, "g_mix_post": dg_mix_post, "g_mem_kv": dg_mem_kv, "g_mem_pre": dg_mem_pre,
               "g_mem_post": dg_mem_post, "g_ff_pre": dg_ff_pre, "g_ff_post": dg_ff_post, "g_fox_out": dg_fox,
               "g_chk_out": dg_chk, "b_fgt": db_fgt,
               "rel_bias": _g_to_rel_table(dgrev[:, 0:2, :].reshape(8, ROLL_W))}
    small_sum, loss_out, (g_w_in,) = _small_reduce_call(small_g, loss_blk, _rs_gather_task([f_in]),
                                                       name="small_allreduce")
    small = _small_adamw_call(small_sum, sp, {k: rows(m, k) for k in SMALL}, {k: rows(v, k) for k in SMALL},
                              name="small_adamw")
    loss = loss_out[0, 0]
    res = _adamw_cols_call(w_in_t, g_w_in, m_in_t, v_in_t, name="adamw_w_in")
    grads["w_in"], delta_w["w_in"], new_m["w_in"], new_v["w_in"] = (a.T for a in res)
    for k in SMALL:
        vals = small[k]
        if k == "rel_bias":
            vals = tuple(a[None] for a in vals)
        grads[k], delta_w[k], new_m[k], new_v[k] = vals

    def out(d, k):
        return d[k][None] if k in BIG else d[k]

    return (loss, grad_x[None], *[out(grads, k) for k in WEIGHTS], *[out(delta_w, k) for k in WEIGHTS],
            *[out(new_m, k) for k in WEIGHTS], *[out(new_v, k) for k in WEIGHTS])
```
